```python
import math
import jax, jax.numpy as jnp
from jax import lax
import numpy as np

D_MODEL = 1024
BATCH = 32
SEQ = 2048
DEPTH = 2

HEAD_DIM = 64
N_HEADS = D_MODEL // HEAD_DIM
N_KV_HEADS = N_HEADS // 4
QKV_WIDTH = (N_HEADS + 2 * N_KV_HEADS) * HEAD_DIM
ATTN_HALF_WINDOW = 128
DILATED_GROUPS = ((128, 1), (512, 4), (2048, 16))
N_DGROUPS = len(DILATED_GROUPS)
N_MIXERS = 2
N_LAYERS_A = (DEPTH + 1) // 2
N_LAYERS_B = DEPTH // 2
D_FF = -(-8 * D_MODEL // (3 * 256)) * 256
ROPE_THETA = 10000.0
RMS_EPS = 1e-6
NEG_INF = -1e30

kernel_name = "hybrid_window_sink_dilated_encoder"


def rmsnorm(x, g):
    x32 = x.astype(jnp.float32)
    y = x32 * lax.rsqrt(jnp.mean(x32 * x32, axis=-1, keepdims=True) + RMS_EPS)
    return (y * g.astype(jnp.float32)).astype(x.dtype)


def rope_tables(seq):
    inv_freq = 1.0 / (ROPE_THETA ** (jnp.arange(0, HEAD_DIM, 2, dtype=jnp.float32) / HEAD_DIM))
    ang = jnp.arange(seq, dtype=jnp.float32)[:, None] * inv_freq[None, :]
    return jnp.cos(ang)[:, None, :], jnp.sin(ang)[:, None, :]


def apply_rope(t, cos, sin):
    t32 = t.astype(jnp.float32)
    t1, t2 = jnp.split(t32, 2, axis=-1)
    out = jnp.concatenate([t1 * cos - t2 * sin, t2 * cos + t1 * sin], axis=-1)
    return out.astype(t.dtype)


def split_qkv(proj, cos, sin):
    b, s, _ = proj.shape
    qw = N_HEADS * HEAD_DIM
    kw = N_KV_HEADS * HEAD_DIM
    q = proj[..., :qw].reshape(b, s, N_HEADS, HEAD_DIM)
    k = proj[..., qw:qw + kw].reshape(b, s, N_KV_HEADS, HEAD_DIM)
    v = proj[..., qw + kw:].reshape(b, s, N_KV_HEADS, HEAD_DIM)
    return apply_rope(q, cos, sin), apply_rope(k, cos, sin), v


def banded_attention(q, k, v, half_window, sink=None):
    n, length, n_q, dh = q.shape
    n_kv = k.shape[2]
    grp = n_q // n_kv
    w = half_window
    nb = -(-length // w)
    lp = nb * w
    qb = jnp.pad(q, ((0, 0), (0, lp - length), (0, 0), (0, 0))).reshape(n, nb, w, n_kv, grp, dh)
    pad_kv = ((0, 0), (w, w + lp - length), (0, 0), (0, 0))
    kp = jnp.pad(k, pad_kv)
    vp = jnp.pad(v, pad_kv)
    scale = 1.0 / math.sqrt(dh)
    offs_q = jnp.arange(w)
    offs_k = jnp.arange(3 * w) - w
    sink_l = None if sink is None else sink.astype(jnp.float32).reshape(n_kv, grp)[None, :, :, None]

    def one_block(i):
        start = i * w
        q_i = lax.dynamic_index_in_dim(qb, i, axis=1, keepdims=False)
        k_i = lax.dynamic_slice_in_dim(kp, start, 3 * w, axis=1)
        v_i = lax.dynamic_slice_in_dim(vp, start, 3 * w, axis=1)
        s = jnp.einsum('nqkgd,nskd->nkgqs', q_i, k_i).astype(jnp.float32) * scale
        qpos = start + offs_q
        kpos = start + offs_k
        valid = ((jnp.abs(qpos[:, None] - kpos[None, :]) <= w)
                 & (kpos[None, :] >= 0) & (kpos[None, :] < length))
        s = jnp.where(valid, s, NEG_INF)
        m = jnp.max(s, axis=-1)
        if sink_l is not None:
            m = jnp.maximum(m, sink_l)
        p = jnp.exp(s - m[..., None])
        denom = jnp.sum(p, axis=-1)
        if sink_l is not None:
            denom = denom + jnp.exp(sink_l - m)
        o = jnp.einsum('nkgqs,nskd->nqkgd', p, v_i.astype(jnp.float32))
        o = o / jnp.transpose(denom, (0, 3, 1, 2))[..., None]
        lse = jnp.transpose(m + jnp.log(denom), (0, 3, 1, 2))
        return o.astype(q.dtype), lse

    o, lse = lax.map(one_block, jnp.arange(nb))
    o = jnp.moveaxis(o, 0, 1).reshape(n, lp, n_q, dh)[:, :length]
    lse = jnp.moveaxis(lse, 0, 1).reshape(n, lp, n_q)[:, :length]
    return o, lse


def dilated_attention(q, k, v, dilation, half_window):
    b, s, n_q, dh = q.shape
    d = dilation

    def to_residue(t):
        return t.reshape(b, s // d, d, t.shape[2], dh).transpose(0, 2, 1, 3, 4).reshape(b * d, s // d, t.shape[2], dh)

    o, lse = banded_attention(to_residue(q), to_residue(k), to_residue(v), half_window // d)
    o = o.reshape(b, d, s // d, n_q, dh).transpose(0, 2, 1, 3, 4).reshape(b, s, n_q, dh)
    lse = lse.reshape(b, d, s // d, n_q).transpose(0, 2, 1, 3).reshape(b, s, n_q)
    return o, lse


def mixer_window_sink(h, w_in, sink, w_out, cos, sin):
    b, s, _ = h.shape
    q, k, v = split_qkv(h @ w_in, cos, sin)
    o, _ = banded_attention(q, k, v, ATTN_HALF_WINDOW, sink)
    return o.reshape(b, s, N_HEADS * HEAD_DIM) @ w_out


def mixer_dilated(h, w_in, w_out, cos, sin):
    b, s, _ = h.shape
    proj = (h @ w_in).reshape(b, s, N_DGROUPS, QKV_WIDTH)
    outs, lses = [], []
    for g, (window, dilation) in enumerate(DILATED_GROUPS):
        q, k, v = split_qkv(proj[:, :, g], cos, sin)
        o, lse = dilated_attention(q, k, v, dilation, window // 2)
        outs.append(o)
        lses.append(lse)
    wts = jax.nn.softmax(jnp.stack(lses, axis=0), axis=0)
    o = (wts[0][..., None] * outs[0].astype(jnp.float32)
         + wts[1][..., None] * outs[1].astype(jnp.float32)
         + wts[2][..., None] * outs[2].astype(jnp.float32))
    return o.astype(h.dtype).reshape(b, s, N_HEADS * HEAD_DIM) @ w_out


def swiglu(h, w_gate, w_up, w_down):
    return (jax.nn.silu(h @ w_gate) * (h @ w_up)) @ w_down


def _fwd_setup_inputs(seed: int = 0) -> dict:
    key = jax.random.key(seed)
    ks = jax.random.split(key, 14)
    f32 = jnp.float32
    d = D_MODEL
    hd = N_HEADS * HEAD_DIM
    x = jax.random.normal(ks[0], (BATCH, SEQ, d), f32)
    a_w_in = jax.random.normal(ks[1], (N_LAYERS_A, d, QKV_WIDTH), f32) * d ** -0.5
    a_sink = jax.random.normal(ks[2], (N_LAYERS_A, N_HEADS), f32) * 0.5
    a_w_out = jax.random.normal(ks[3], (N_LAYERS_A, hd, d), f32) * hd ** -0.5
    b_w_in = jax.random.normal(ks[4], (N_LAYERS_B, d, N_DGROUPS * QKV_WIDTH), f32) * d ** -0.5
    b_w_out = jax.random.normal(ks[5], (N_LAYERS_B, hd, d), f32) * hd ** -0.5
    norm_mix = 1.0 + 0.02 * jax.random.normal(ks[6], (DEPTH, d), f32)
    norm_ffn = 1.0 + 0.02 * jax.random.normal(ks[7], (DEPTH, d), f32)
    w_gate = jax.random.normal(ks[8], (DEPTH, d, D_FF), f32) * d ** -0.5
    w_up = jax.random.normal(ks[9], (DEPTH, d, D_FF), f32) * d ** -0.5
    w_down = jax.random.normal(ks[10], (DEPTH, D_FF, d), f32) * D_FF ** -0.5
    final_norm = 1.0 + 0.02 * jax.random.normal(ks[11], (d,), f32)
    return {"x": x, "a_w_in": a_w_in, "a_sink": a_sink, "a_w_out": a_w_out,
            "b_w_in": b_w_in, "b_w_out": b_w_out, "norm_mix": norm_mix, "norm_ffn": norm_ffn,
            "w_gate": w_gate, "w_up": w_up, "w_down": w_down, "final_norm": final_norm}


def _fwd_reference(x, a_w_in, a_sink, a_w_out, b_w_in, b_w_out, norm_mix, norm_ffn,
              w_gate, w_up, w_down, final_norm):
    cos, sin = rope_tables(x.shape[1])
    for i in range(DEPTH):
        h = rmsnorm(x, norm_mix[i])
        j = i // N_MIXERS
        if i % N_MIXERS == 0:
            mix = mixer_window_sink(h, a_w_in[j], a_sink[j], a_w_out[j], cos, sin)
        else:
            mix = mixer_dilated(h, b_w_in[j], b_w_out[j], cos, sin)
        x = x + mix
        h = rmsnorm(x, norm_ffn[i])
        x = x + swiglu(h, w_gate[i], w_up[i], w_down[i])
    return rmsnorm(x, final_norm)


import jax as _jax
import jax.numpy as _jnp

TWIN_FORMAT = 'train_step'
FWD_PARAMS = ['x', 'a_w_in', 'a_sink', 'a_w_out', 'b_w_in', 'b_w_out', 'norm_mix', 'norm_ffn', 'w_gate', 'w_up', 'w_down', 'final_norm']
TWIN_WEIGHTS = ['a_w_in', 'a_sink', 'a_w_out', 'b_w_in', 'b_w_out', 'norm_mix', 'norm_ffn', 'w_gate', 'w_up', 'w_down', 'final_norm']
TWIN_DIFF_INPUT = 'x'
TWIN_INPUTS = ['x', 'a_w_in', 'a_sink', 'a_w_out', 'b_w_in', 'b_w_out', 'norm_mix', 'norm_ffn', 'w_gate', 'w_up', 'w_down', 'final_norm', 'loss_target', 'm_a_w_in', 'm_a_sink', 'm_a_w_out', 'm_b_w_in', 'm_b_w_out', 'm_norm_mix', 'm_norm_ffn', 'm_w_gate', 'm_w_up', 'm_w_down', 'm_final_norm', 'v_a_w_in', 'v_a_sink', 'v_a_w_out', 'v_b_w_in', 'v_b_w_out', 'v_norm_mix', 'v_norm_ffn', 'v_w_gate', 'v_w_up', 'v_w_down', 'v_final_norm']
TWIN_OUTPUTS = ['loss', 'grad_x', 'grad_a_w_in', 'grad_a_sink', 'grad_a_w_out', 'grad_b_w_in', 'grad_b_w_out', 'grad_norm_mix', 'grad_norm_ffn', 'grad_w_gate', 'grad_w_up', 'grad_w_down', 'grad_final_norm', 'delta_a_w_in', 'delta_a_sink', 'delta_a_w_out', 'delta_b_w_in', 'delta_b_w_out', 'delta_norm_mix', 'delta_norm_ffn', 'delta_w_gate', 'delta_w_up', 'delta_w_down', 'delta_final_norm', 'new_m_a_w_in', 'new_m_a_sink', 'new_m_a_w_out', 'new_m_b_w_in', 'new_m_b_w_out', 'new_m_norm_mix', 'new_m_norm_ffn', 'new_m_w_gate', 'new_m_w_up', 'new_m_w_down', 'new_m_final_norm', 'new_v_a_w_in', 'new_v_a_sink', 'new_v_a_w_out', 'new_v_b_w_in', 'new_v_b_w_out', 'new_v_norm_mix', 'new_v_norm_ffn', 'new_v_w_gate', 'new_v_w_up', 'new_v_w_down', 'new_v_final_norm']
TWIN_LEAF_KINDS = {'loss': 'loss', 'grad_x': 'grad_x', 'grad_a_w_in': 'grad_w', 'grad_a_sink': 'grad_w', 'grad_a_w_out': 'grad_w', 'grad_b_w_in': 'grad_w', 'grad_b_w_out': 'grad_w', 'grad_norm_mix': 'grad_w', 'grad_norm_ffn': 'grad_w', 'grad_w_gate': 'grad_w', 'grad_w_up': 'grad_w', 'grad_w_down': 'grad_w', 'grad_final_norm': 'grad_w', 'delta_a_w_in': 'delta_w', 'delta_a_sink': 'delta_w', 'delta_a_w_out': 'delta_w', 'delta_b_w_in': 'delta_w', 'delta_b_w_out': 'delta_w', 'delta_norm_mix': 'delta_w', 'delta_norm_ffn': 'delta_w', 'delta_w_gate': 'delta_w', 'delta_w_up': 'delta_w', 'delta_w_down': 'delta_w', 'delta_final_norm': 'delta_w', 'new_m_a_w_in': 'new_m', 'new_m_a_sink': 'new_m', 'new_m_a_w_out': 'new_m', 'new_m_b_w_in': 'new_m', 'new_m_b_w_out': 'new_m', 'new_m_norm_mix': 'new_m', 'new_m_norm_ffn': 'new_m', 'new_m_w_gate': 'new_m', 'new_m_w_up': 'new_m', 'new_m_w_down': 'new_m', 'new_m_final_norm': 'new_m', 'new_v_a_w_in': 'new_v', 'new_v_a_sink': 'new_v', 'new_v_a_w_out': 'new_v', 'new_v_b_w_in': 'new_v', 'new_v_b_w_out': 'new_v', 'new_v_norm_mix': 'new_v', 'new_v_norm_ffn': 'new_v', 'new_v_w_gate': 'new_v', 'new_v_w_up': 'new_v', 'new_v_w_down': 'new_v', 'new_v_final_norm': 'new_v'}


def _forward(args):
    return _fwd_reference(*[args[k] for k in FWD_PARAMS])


def _output_shape():
    out = _jax.eval_shape(lambda: _forward(_fwd_setup_inputs(0)))
    return out.shape, out.dtype

N_MICROBATCH = 1
ADAM_LR = 0.001
ADAM_B1 = 0.9
ADAM_B2 = 0.999
ADAM_EPS = 1e-08
ADAM_WD = 0.01
ADAM_STEP = 10
PER_EXAMPLE_BATCH_AXIS = {'x': 0, 'loss_target': 0}
SHARED_INPUTS = []
_WEIGHT_DTYPES = {'a_w_in': _jnp.float32, 'a_sink': _jnp.float32, 'a_w_out': _jnp.float32, 'b_w_in': _jnp.float32, 'b_w_out': _jnp.float32, 'norm_mix': _jnp.float32, 'norm_ffn': _jnp.float32, 'w_gate': _jnp.float32, 'w_up': _jnp.float32, 'w_down': _jnp.float32, 'final_norm': _jnp.float32}
MOMENT_SCALE = {'a_w_in': 5.380602e-02, 'a_sink': 1.840506e-03, 'a_w_out': 3.951796e-02, 'b_w_in': 2.323221e-02, 'b_w_out': 3.009439e-02, 'norm_mix': 5.900153e-02, 'norm_ffn': 1.905381e-01, 'w_gate': 8.019778e-02, 'w_up': 7.752556e-02, 'w_down': 1.286697e-01, 'final_norm': 6.383983e+01}


def _to_microbatches(a, axis):
    t = _jnp.moveaxis(a, axis, 0)
    t = t.reshape((N_MICROBATCH, t.shape[0] // N_MICROBATCH) + t.shape[1:])
    return _jnp.moveaxis(t, 1, axis + 1)


def setup_inputs(seed: int = 0) -> dict:
    inp = _fwd_setup_inputs(seed)
    key = _jax.random.fold_in(_jax.random.key(seed), 7919)
    shape, _ = _output_shape()
    out = dict(inp)
    out["loss_target"] = _jax.random.normal(_jax.random.fold_in(key, 0), shape, _jnp.float32)
    for i, name in enumerate(TWIN_WEIGHTS):
        w = inp[name].astype(_jnp.float32)
        if MOMENT_SCALE is None:
            s = _jnp.sqrt(_jnp.mean(_jnp.square(w)) + 1e-30)
        else:
            s = MOMENT_SCALE[name]
        km, kv = _jax.random.split(_jax.random.fold_in(key, i + 1))
        out[name] = w
        out["m_" + name] = s * _jax.random.normal(km, w.shape, _jnp.float32)
        out["v_" + name] = (s * s) * _jax.random.uniform(kv, w.shape, _jnp.float32, 0.5, 1.5)
    if N_MICROBATCH > 1:
        for name, axis in PER_EXAMPLE_BATCH_AXIS.items():
            out[name] = _to_microbatches(out[name], axis)
    return {'x': out['x'], 'a_w_in': out['a_w_in'], 'a_sink': out['a_sink'], 'a_w_out': out['a_w_out'], 'b_w_in': out['b_w_in'], 'b_w_out': out['b_w_out'], 'norm_mix': out['norm_mix'], 'norm_ffn': out['norm_ffn'], 'w_gate': out['w_gate'], 'w_up': out['w_up'], 'w_down': out['w_down'], 'final_norm': out['final_norm'], 'loss_target': out['loss_target'], 'm_a_w_in': out['m_a_w_in'], 'm_a_sink': out['m_a_sink'], 'm_a_w_out': out['m_a_w_out'], 'm_b_w_in': out['m_b_w_in'], 'm_b_w_out': out['m_b_w_out'], 'm_norm_mix': out['m_norm_mix'], 'm_norm_ffn': out['m_norm_ffn'], 'm_w_gate': out['m_w_gate'], 'm_w_up': out['m_w_up'], 'm_w_down': out['m_w_down'], 'm_final_norm': out['m_final_norm'], 'v_a_w_in': out['v_a_w_in'], 'v_a_sink': out['v_a_sink'], 'v_a_w_out': out['v_a_w_out'], 'v_b_w_in': out['v_b_w_in'], 'v_b_w_out': out['v_b_w_out'], 'v_norm_mix': out['v_norm_mix'], 'v_norm_ffn': out['v_norm_ffn'], 'v_w_gate': out['v_w_gate'], 'v_w_up': out['v_w_up'], 'v_w_down': out['v_w_down'], 'v_final_norm': out['v_final_norm']}


def _loss(weights, diff, rest, loss_target):
    with _jax.named_scope("forward"):
        args = {**rest, TWIN_DIFF_INPUT: diff, **{k: w.astype(_WEIGHT_DTYPES[k]) for k, w in weights.items()}}
        y = _forward(args)
    with _jax.named_scope("loss_head"):
        err = _jnp.square(y.astype(_jnp.float32) - loss_target)
        return 0.5 * _jnp.sum(_jnp.mean(err, axis=-1)) if err.ndim else 0.5 * err


def _adamw(w, g, m, v):
    m = ADAM_B1 * m + (1.0 - ADAM_B1) * g
    v = ADAM_B2 * v + (1.0 - ADAM_B2) * _jnp.square(g)
    m_hat = m / (1.0 - ADAM_B1 ** ADAM_STEP)
    v_hat = v / (1.0 - ADAM_B2 ** ADAM_STEP)
    delta = -ADAM_LR * (m_hat / (_jnp.sqrt(v_hat) + ADAM_EPS) + ADAM_WD * w)
    return delta, m, v


def reference(x, a_w_in, a_sink, a_w_out, b_w_in, b_w_out, norm_mix, norm_ffn, w_gate, w_up, w_down, final_norm, loss_target, m_a_w_in, m_a_sink, m_a_w_out, m_b_w_in, m_b_w_out, m_norm_mix, m_norm_ffn, m_w_gate, m_w_up, m_w_down, m_final_norm, v_a_w_in, v_a_sink, v_a_w_out, v_b_w_in, v_b_w_out, v_norm_mix, v_norm_ffn, v_w_gate, v_w_up, v_w_down, v_final_norm):
    given = dict(x=x, a_w_in=a_w_in, a_sink=a_sink, a_w_out=a_w_out, b_w_in=b_w_in, b_w_out=b_w_out, norm_mix=norm_mix, norm_ffn=norm_ffn, w_gate=w_gate, w_up=w_up, w_down=w_down, final_norm=final_norm, loss_target=loss_target, m_a_w_in=m_a_w_in, m_a_sink=m_a_sink, m_a_w_out=m_a_w_out, m_b_w_in=m_b_w_in, m_b_w_out=m_b_w_out, m_norm_mix=m_norm_mix, m_norm_ffn=m_norm_ffn, m_w_gate=m_w_gate, m_w_up=m_w_up, m_w_down=m_w_down, m_final_norm=m_final_norm, v_a_w_in=v_a_w_in, v_a_sink=v_a_sink, v_a_w_out=v_a_w_out, v_b_w_in=v_b_w_in, v_b_w_out=v_b_w_out, v_norm_mix=v_norm_mix, v_norm_ffn=v_norm_ffn, v_w_gate=v_w_gate, v_w_up=v_w_up, v_w_down=v_w_down, v_final_norm=v_final_norm)
    weights = {n: given[n] for n in TWIN_WEIGHTS}
    shared = {n: given[n] for n in SHARED_INPUTS}
    per_example = {n: given[n] for n in ['x']}
    grad_fn = _jax.value_and_grad(_loss, argnums=(0, 1))

    def one_microbatch(ex, loss_target):
        ex = dict(ex)
        diff = ex.pop(TWIN_DIFF_INPUT)
        return grad_fn(weights, diff, {**shared, **ex}, loss_target)

    if N_MICROBATCH == 1:
        loss, (grad_w, grad_x) = one_microbatch(per_example, given["loss_target"])
    else:
        def body(carry, xs):
            loss_sum, grad_sum = carry
            l_k, (gw_k, gx_k) = one_microbatch(xs[0], xs[1])
            with _jax.named_scope("update"):
                return (loss_sum + l_k, _jax.tree.map(_jnp.add, grad_sum, gw_k)), gx_k

        init = (_jnp.zeros((), _jnp.float32), _jax.tree.map(_jnp.zeros_like, weights))
        (loss, grad_w), grad_x = _jax.lax.scan(body, init, (per_example, given["loss_target"]))
    with _jax.named_scope("update"):
        delta_w, new_m, new_v = {}, {}, {}
        for n in TWIN_WEIGHTS:
            delta_w[n], new_m[n], new_v[n] = _adamw(weights[n], grad_w[n], given["m_" + n], given["v_" + n])
    return (loss, grad_x, *[grad_w[n] for n in TWIN_WEIGHTS], *[delta_w[n] for n in TWIN_WEIGHTS],
            *[new_m[n] for n in TWIN_WEIGHTS], *[new_v[n] for n in TWIN_WEIGHTS])
```

```python
import functools
import math

import jax
import jax.numpy as jnp
from jax import lax
from jax.experimental import pallas as pl
from jax.experimental.pallas import tpu as pltpu

F32 = jnp.float32
BF16 = jnp.bfloat16

D_MODEL = 1024
HEAD_DIM = 64
N_HEADS = 16
N_KV = 4
QKV_W = 1536
D_FF = 2816
N_CHIPS = 4
FF_SH = D_FF // N_CHIPS
HALF_WINDOW_A = 128
DILATED = ((128, 1), (512, 4), (2048, 16))
ROPE_THETA = 10000.0
RMS_EPS = 1e-6
NEG_INF = -1e30
LANES = 128
ADAM_LR, ADAM_B1, ADAM_B2, ADAM_EPS, ADAM_WD, ADAM_STEP = 0.001, 0.9, 0.999, 1e-08, 0.01, 10
VMEM_LIMIT = 56 * 1024 * 1024
MESH = pl.DeviceIdType.MESH


def _cp(**kw):
    return pltpu.CompilerParams(vmem_limit_bytes=VMEM_LIMIT, **kw)


def _row_tile(t, cap):
    tm = min(cap, t)
    assert t % tm == 0
    return tm


def _rope_tables(seq, dil):
    inv = 1.0 / (ROPE_THETA ** (jnp.arange(0, HEAD_DIM, 2, dtype=F32) / HEAD_DIM))
    ang = jnp.arange(seq, dtype=F32)[:, None] * inv[None, :]
    cos, sin = jnp.cos(ang), jnp.sin(ang)
    cos = jnp.tile(cos, (1, 4))
    sin = jnp.concatenate([-sin, sin, -sin, sin], axis=1)

    def perm(t):
        return t.reshape(seq // dil, dil, LANES).transpose(1, 0, 2).reshape(seq, LANES)

    return perm(cos), perm(sin)


def _swap_halves(t):
    lane = lax.broadcasted_iota(jnp.int32, t.shape, 1)
    return jnp.where((lane % HEAD_DIM) < HEAD_DIM // 2, pltpu.roll(t, LANES - 32, 1), pltpu.roll(t, 32, 1))


def _rope(t, cos, sin):
    return t * cos + _swap_halves(t) * sin


def _rope_t(t, cos, sin):
    return t * cos - _swap_halves(t) * sin


def _to_residue(t, batch, dil):
    if dil == 1:
        return t
    s = t.shape[0] // batch
    return t.reshape(batch, s // dil, dil, t.shape[1]).transpose(0, 2, 1, 3).reshape(t.shape)


def _from_residue(t, batch, dil):
    if dil == 1:
        return t
    s = t.shape[0] // batch
    return t.reshape(batch, dil, s // dil, t.shape[1]).transpose(0, 2, 1, 3).reshape(t.shape)


def _rms_fwd(x, w, name):
    t = x.shape[0]
    tm = _row_tile(t, 512)

    def body(x_ref, w_ref, o_ref):
        xv = x_ref[...]
        r = lax.rsqrt(jnp.mean(xv * xv, axis=-1, keepdims=True) + RMS_EPS)
        o_ref[...] = ((xv * r) * w_ref[...]).astype(BF16)

    return pl.pallas_call(
        body, name=name, grid=(t // tm,),
        in_specs=[pl.BlockSpec((tm, D_MODEL), lambda i: (i, 0)), pl.BlockSpec((1, D_MODEL), lambda i: (0, 0))],
        out_specs=pl.BlockSpec((tm, D_MODEL), lambda i: (i, 0)),
        out_shape=jax.ShapeDtypeStruct((t, D_MODEL), BF16), compiler_params=_cp(),
    )(x, w)


def _rms_bwd(x, w, dhs, dres, name):
    t = x.shape[0]
    tm = _row_tile(t, 512)
    n = len(dhs)

    def body(*refs):
        x_ref, w_ref = refs[0], refs[1]
        dh_refs = refs[2:2 + n]
        dres_ref = refs[2 + n]
        dx_ref, dxb_ref, dw_ref = refs[3 + n:]
        xv = x_ref[...]
        r = lax.rsqrt(jnp.mean(xv * xv, axis=-1, keepdims=True) + RMS_EPS)
        xh = xv * r
        dy = dh_refs[0][...]
        for k in range(1, n):
            dy = dy + dh_refs[k][...]
        dxh = dy * w_ref[...]
        dx = dres_ref[...] + r * (dxh - xh * jnp.mean(dxh * xh, axis=-1, keepdims=True))
        dx_ref[...] = dx
        dxb_ref[...] = dx.astype(BF16)

        @pl.when(pl.program_id(0) == 0)
        def _():
            dw_ref[...] = jnp.zeros_like(dw_ref)

        dw_ref[...] += jnp.sum(dy * xh, axis=0, keepdims=True)

    row = pl.BlockSpec((tm, D_MODEL), lambda i: (i, 0))
    vec = pl.BlockSpec((1, D_MODEL), lambda i: (0, 0))
    return pl.pallas_call(
        body, name=name, grid=(t // tm,),
        in_specs=[row, vec] + [row] * n + [row],
        out_specs=[row, row, vec],
        out_shape=[jax.ShapeDtypeStruct((t, D_MODEL), F32), jax.ShapeDtypeStruct((t, D_MODEL), BF16),
                   jax.ShapeDtypeStruct((1, D_MODEL), F32)],
        compiler_params=_cp(),
    )(x, w, *dhs, dres)


def _final_loss(x, w, target, name):
    t = x.shape[0]
    tm = _row_tile(t, 512)

    def body(x_ref, w_ref, t_ref, dx_ref, dxb_ref, l_ref, dw_ref):
        xv = x_ref[...]
        r = lax.rsqrt(jnp.mean(xv * xv, axis=-1, keepdims=True) + RMS_EPS)
        xh = xv * r
        err = xh * w_ref[...] - t_ref[...]
        dy = err * (1.0 / D_MODEL)
        dxh = dy * w_ref[...]
        dx = r * (dxh - xh * jnp.mean(dxh * xh, axis=-1, keepdims=True))
        dx_ref[...] = dx
        dxb_ref[...] = dx.astype(BF16)

        @pl.when(pl.program_id(0) == 0)
        def _():
            l_ref[...] = jnp.zeros_like(l_ref)
            dw_ref[...] = jnp.zeros_like(dw_ref)

        l_ref[...] += jnp.sum(err * err, axis=0, keepdims=True)
        dw_ref[...] += jnp.sum(dy * xh, axis=0, keepdims=True)

    row = pl.BlockSpec((tm, D_MODEL), lambda i: (i, 0))
    vec = pl.BlockSpec((1, D_MODEL), lambda i: (0, 0))
    return pl.pallas_call(
        body, name=name, grid=(t // tm,),
        in_specs=[row, vec, row], out_specs=[row, row, vec, vec],
        out_shape=[jax.ShapeDtypeStruct((t, D_MODEL), F32), jax.ShapeDtypeStruct((t, D_MODEL), BF16),
                   jax.ShapeDtypeStruct((1, D_MODEL), F32), jax.ShapeDtypeStruct((1, D_MODEL), F32)],
        compiler_params=_cp(),
    )(x, w, target)


def _qkv_proj(h, w, cos, sin, group, name):
    t = h.shape[0]
    seq = cos.shape[0]
    tm = _row_tile(seq, 1024)
    n_rope = (N_HEADS + N_KV) * HEAD_DIM // LANES

    def body(h_ref, w_ref, cos_ref, sin_ref, o_ref):
        acc = jnp.dot(h_ref[...], w_ref[...], preferred_element_type=F32)
        cs, sn = cos_ref[...], sin_ref[...]
        for c in range(QKV_W // LANES):
            blk = acc[:, c * LANES:(c + 1) * LANES]
            if c < n_rope:
                blk = _rope(blk, cs, sn)
            o_ref[:, c * LANES:(c + 1) * LANES] = blk.astype(BF16)

    tab = pl.BlockSpec((tm, LANES), lambda i: (i % (seq // tm), 0))
    return pl.pallas_call(
        body, name=name, grid=(t // tm,),
        in_specs=[pl.BlockSpec((tm, D_MODEL), lambda i: (i, 0)),
                  pl.BlockSpec((D_MODEL, QKV_W), lambda i: (0, group)), tab, tab],
        out_specs=pl.BlockSpec((tm, QKV_W), lambda i: (i, 0)),
        out_shape=jax.ShapeDtypeStruct((t, QKV_W), BF16), compiler_params=_cp(),
    )(h, w, cos, sin)


def _mm_res(a, w, res, name):
    t, k = a.shape
    tm = _row_tile(t, 1024)

    def body(a_ref, w_ref, r_ref, o_ref):
        o_ref[...] = r_ref[...] + jnp.dot(a_ref[...], w_ref[...], preferred_element_type=F32)

    return pl.pallas_call(
        body, name=name, grid=(t // tm,),
        in_specs=[pl.BlockSpec((tm, k), lambda i: (i, 0)), pl.BlockSpec((k, D_MODEL), lambda i: (0, 0)),
                  pl.BlockSpec((tm, D_MODEL), lambda i: (i, 0))],
        out_specs=pl.BlockSpec((tm, D_MODEL), lambda i: (i, 0)),
        out_shape=jax.ShapeDtypeStruct((t, D_MODEL), F32), compiler_params=_cp(),
    )(a, w, res)


def _mm_nt(dy, w, group, out_dtype, name):
    t, n = dy.shape
    k = w.shape[0]
    tm = _row_tile(t, 1024)

    def body(dy_ref, w_ref, o_ref):
        o_ref[...] = lax.dot_general(dy_ref[...], w_ref[...], (((1,), (1,)), ((), ())),
                                     preferred_element_type=F32).astype(out_dtype)

    return pl.pallas_call(
        body, name=name, grid=(t // tm,),
        in_specs=[pl.BlockSpec((tm, n), lambda i: (i, 0)), pl.BlockSpec((k, n), lambda i: (0, group))],
        out_specs=pl.BlockSpec((tm, k), lambda i: (i, 0)),
        out_shape=jax.ShapeDtypeStruct((t, k), out_dtype), compiler_params=_cp(),
    )(dy, w)


def _out_bwd(dx, w, o, name):
    t = dx.shape[0]
    tm = _row_tile(t, 512)

    def body(dx_ref, w_ref, o_ref, et_ref, do_ref, adj_ref):
        do = lax.dot_general(dx_ref[...], w_ref[...], (((1,), (1,)), ((), ())), preferred_element_type=F32)
        do_ref[...] = do.astype(BF16)
        adj_ref[...] = -jnp.dot(do * o_ref[...].astype(F32), et_ref[...], precision=lax.Precision.HIGHEST,
                                preferred_element_type=F32)

    row = pl.BlockSpec((tm, D_MODEL), lambda i: (i, 0))
    return pl.pallas_call(
        body, name=name, grid=(t // tm,),
        in_specs=[row, pl.BlockSpec((D_MODEL, D_MODEL), lambda i: (0, 0)), row,
                  pl.BlockSpec((D_MODEL, LANES), lambda i: (0, 0))],
        out_specs=[row, pl.BlockSpec((tm, LANES), lambda i: (i, 0))],
        out_shape=[jax.ShapeDtypeStruct((t, D_MODEL), BF16), jax.ShapeDtypeStruct((t, LANES), F32)],
        compiler_params=_cp(),
    )(dx, w, o, _head_expander().T)


def _mm_tn(a, bs, name):
    aq = a.ndim == 3
    bq = bs[0].ndim == 3
    t, ka = a.shape[-2:]
    n = bs[0].shape[-1]
    nq = N_CHIPS if (aq or bq) else 1
    tt = _row_tile(t, 512)
    tn = n if n <= 1024 else 768
    assert n % tn == 0
    nb = len(bs)
    steps = t // tt

    def body(*refs):
        a_ref = refs[0]
        b_refs = refs[1:1 + nb]
        o_refs = refs[1 + nb:1 + 2 * nb]
        acc_refs = refs[1 + 2 * nb:]
        s = pl.program_id(2)
        av = a_ref[...]
        for b_ref, o_ref, acc_ref in zip(b_refs, o_refs, acc_refs):
            part = lax.dot_general(av, b_ref[...], (((0,), (0,)), ((), ())), preferred_element_type=F32)

            @pl.when(s == 0)
            def _():
                acc_ref[...] = part

            @pl.when(s > 0)
            def _():
                acc_ref[...] += part

            @pl.when(s == steps - 1)
            def _():
                o_ref[...] = acc_ref[...].astype(BF16)

    a_spec = (pl.BlockSpec((None, tt, ka), lambda q, j, s: (q, s, 0)) if aq
              else pl.BlockSpec((tt, ka), lambda q, j, s: (s, 0)))
    b_spec = (pl.BlockSpec((None, tt, tn), lambda q, j, s: (q, s, j)) if bq
              else pl.BlockSpec((tt, tn), lambda q, j, s: (s, j)))
    if nq > 1:
        o_spec = pl.BlockSpec((None, ka, tn), lambda q, j, s: (q, 0, j))
        o_shape = jax.ShapeDtypeStruct((nq, ka, n), BF16)
    else:
        o_spec = pl.BlockSpec((ka, tn), lambda q, j, s: (0, j))
        o_shape = jax.ShapeDtypeStruct((ka, n), BF16)
    outs = pl.pallas_call(
        body, name=name, grid=(nq, n // tn, steps),
        in_specs=[a_spec] + [b_spec] * nb, out_specs=[o_spec] * nb, out_shape=[o_shape] * nb,
        scratch_shapes=[pltpu.VMEM((ka, tn), F32)] * nb, compiler_params=_cp(),
    )(a, *bs)
    return outs


def _sigmoid(x):
    return 1.0 / (1.0 + jnp.exp(-x))


def _ffn_up(h, wg, wu, layer, name):
    t = h.shape[0]
    tm = _row_tile(t, 1024)

    def body(h_ref, wg_ref, wu_ref, g_ref, u_ref, a_ref):
        hv = h_ref[...]
        g = jnp.dot(hv, wg_ref[...], preferred_element_type=F32)
        u = jnp.dot(hv, wu_ref[...], preferred_element_type=F32)
        g_ref[...] = g.astype(BF16)
        u_ref[...] = u.astype(BF16)
        a_ref[...] = (g * _sigmoid(g) * u).astype(BF16)

    wspec = pl.BlockSpec((None, None, D_MODEL, FF_SH), lambda q, i: (q, layer, 0, 0))
    ospec = pl.BlockSpec((None, tm, FF_SH), lambda q, i: (q, i, 0))
    oshape = jax.ShapeDtypeStruct((N_CHIPS, t, FF_SH), BF16)
    return pl.pallas_call(
        body, name=name, grid=(N_CHIPS, t // tm),
        in_specs=[pl.BlockSpec((tm, D_MODEL), lambda q, i: (i, 0)), wspec, wspec],
        out_specs=[ospec] * 3, out_shape=[oshape] * 3, compiler_params=_cp(),
    )(h, wg, wu)


def _ffn_down(a, wd, res, layer, name):
    t = a.shape[1]
    tm = _row_tile(t, 512)

    def body(a_ref, w_ref, r_ref, o_ref):
        acc = r_ref[...]
        for q in range(N_CHIPS):
            acc = acc + jnp.dot(a_ref[q], w_ref[q], preferred_element_type=F32)
        o_ref[...] = acc

    return pl.pallas_call(
        body, name=name, grid=(t // tm,),
        in_specs=[pl.BlockSpec((N_CHIPS, tm, FF_SH), lambda i: (0, i, 0)),
                  pl.BlockSpec((N_CHIPS, None, FF_SH, D_MODEL), lambda i: (0, layer, 0, 0)),
                  pl.BlockSpec((tm, D_MODEL), lambda i: (i, 0))],
        out_specs=pl.BlockSpec((tm, D_MODEL), lambda i: (i, 0)),
        out_shape=jax.ShapeDtypeStruct((t, D_MODEL), F32), compiler_params=_cp(),
    )(a, wd, res)


def _ffn_down_bwd(dx, wd, g, u, layer, name):
    t = dx.shape[0]
    tm = _row_tile(t, 1024)

    def body(dx_ref, w_ref, g_ref, u_ref, dg_ref, du_ref):
        da = lax.dot_general(dx_ref[...], w_ref[...], (((1,), (1,)), ((), ())), preferred_element_type=F32)
        gv = g_ref[...].astype(F32)
        uv = u_ref[...].astype(F32)
        sg = _sigmoid(gv)
        du_ref[...] = (da * (gv * sg)).astype(BF16)
        dg_ref[...] = (da * uv * (sg * (1.0 + gv * (1.0 - sg)))).astype(BF16)

    aspec = pl.BlockSpec((None, tm, FF_SH), lambda q, i: (q, i, 0))
    oshape = jax.ShapeDtypeStruct((N_CHIPS, t, FF_SH), BF16)
    return pl.pallas_call(
        body, name=name, grid=(N_CHIPS, t // tm),
        in_specs=[pl.BlockSpec((tm, D_MODEL), lambda q, i: (i, 0)),
                  pl.BlockSpec((None, None, FF_SH, D_MODEL), lambda q, i: (q, layer, 0, 0)), aspec, aspec],
        out_specs=[aspec] * 2, out_shape=[oshape] * 2, compiler_params=_cp(),
    )(dx, wd, g, u)


def _ffn_up_bwd(dg, du, wg, wu, layer, name):
    t = dg.shape[1]
    tm = _row_tile(t, 512)
    nt = (((1,), (1,)), ((), ()))

    def body(dg_ref, du_ref, wg_ref, wu_ref, o_ref):
        acc = jnp.zeros((tm, D_MODEL), F32)
        for q in range(N_CHIPS):
            acc = acc + lax.dot_general(dg_ref[q], wg_ref[q], nt, preferred_element_type=F32)
            acc = acc + lax.dot_general(du_ref[q], wu_ref[q], nt, preferred_element_type=F32)
        o_ref[...] = acc

    aspec = pl.BlockSpec((N_CHIPS, tm, FF_SH), lambda i: (0, i, 0))
    wspec = pl.BlockSpec((N_CHIPS, None, D_MODEL, FF_SH), lambda i: (0, layer, 0, 0))
    return pl.pallas_call(
        body, name=name, grid=(t // tm,),
        in_specs=[aspec, aspec, wspec, wspec],
        out_specs=pl.BlockSpec((tm, D_MODEL), lambda i: (i, 0)),
        out_shape=jax.ShapeDtypeStruct((t, D_MODEL), F32), compiler_params=_cp(),
    )(dg, du, wg, wu)


def _attn_geometry(length, half_window):
    qb = min(LANES, length)
    kw = min(qb + 2 * half_window, length)
    return qb, kw, length // qb


def _dup_kv(src_ref, dst_ref, s, length):
    ch = min(length, 256)
    lo = lax.broadcasted_iota(jnp.int32, (ch, LANES), 1) < HEAD_DIM

    def chunk(c, carry):
        r0 = pl.multiple_of(c * ch, ch)
        for j in range(N_KV // 2):
            tile = src_ref[s, pl.ds(r0, ch), j * LANES:(j + 1) * LANES].astype(F32)
            rolled = pltpu.roll(tile, HEAD_DIM, 1)
            dst_ref[2 * j, pl.ds(r0, ch), :] = jnp.where(lo, tile, rolled).astype(BF16)
            dst_ref[2 * j + 1, pl.ds(r0, ch), :] = jnp.where(lo, rolled, tile).astype(BF16)
        return carry

    lax.fori_loop(0, length // ch, chunk, 0)


def _stack_heads(ref, s, q0, qb, g):
    lo = lax.broadcasted_iota(jnp.int32, (qb, LANES), 1) < HEAD_DIM
    parts = []
    for a in range(4):
        col = (2 * g + a // 2) * LANES
        tile = ref[s, pl.ds(q0, qb), col:col + LANES]
        keep = lo if a % 2 == 0 else jnp.logical_not(lo)
        parts.append(jnp.where(keep, tile, jnp.zeros_like(tile)))
    return jnp.concatenate(parts, axis=0)


def _unstack_pair(stacked, qb, pair):
    lo = lax.broadcasted_iota(jnp.int32, (qb, LANES), 1) < HEAD_DIM
    return jnp.where(lo, stacked[(2 * pair) * qb:(2 * pair + 1) * qb], stacked[(2 * pair + 1) * qb:(2 * pair + 2) * qb])


def _band_mask(q0, k0, qb, kw, half_window):
    row = lax.broadcasted_iota(jnp.int32, (4 * qb, kw), 0) & (qb - 1)
    col = lax.broadcasted_iota(jnp.int32, (4 * qb, kw), 1)
    return jnp.abs((q0 + row) - (k0 + col)) <= half_window


def _head_column(vals, qb):
    return jnp.concatenate([jnp.full((qb, 1), v, F32) for v in vals], axis=0)


def _attn_fwd(qkv, sink, n_seq, length, half_window, seq_blk, out_dtype, with_lse, name):
    qb, kw, nblk = _attn_geometry(length, half_window)
    scale = 1.0 / math.sqrt(HEAD_DIM)
    with_sink = sink is not None
    nt = (((1,), (1,)), ((), ()))
    qkv3 = qkv.reshape(n_seq, length, QKV_W)

    def body(*refs):
        refs = list(refs)
        sink_ref = refs.pop(0) if with_sink else None
        q_ref, k_ref, v_ref, o_ref = refs[:4]
        lse_ref = refs[4] if with_lse else None
        kx_ref, vx_ref = refs[-2:]
        lane = lax.broadcasted_iota(jnp.int32, (qb, LANES), 1)
        for s in range(seq_blk):
            _dup_kv(k_ref, kx_ref, s, length)
            _dup_kv(v_ref, vx_ref, s, length)

            def block(i, carry):
                q0 = pl.multiple_of(i * qb, qb)
                k0 = pl.multiple_of(jnp.clip(i * qb - half_window, 0, length - kw), HEAD_DIM)
                valid = _band_mask(q0, k0, qb, kw, half_window)
                lse_tile = jnp.zeros((qb, LANES), F32)
                for g in range(N_KV):
                    qs = _stack_heads(q_ref, s, q0, qb, g)
                    kx = kx_ref[g, pl.ds(k0, kw), :]
                    vx = vx_ref[g, pl.ds(k0, kw), :]
                    sc = lax.dot_general(qs, kx, nt, preferred_element_type=F32) * scale
                    sc = jnp.where(valid, sc, NEG_INF)
                    m = jnp.max(sc, axis=1, keepdims=True)
                    if with_sink:
                        sk = _head_column([sink_ref[4 * g + a] for a in range(4)], qb)
                        m = jnp.maximum(m, sk)
                    p = jnp.exp(sc - m)
                    den = jnp.sum(p, axis=1, keepdims=True)
                    if with_sink:
                        den = den + jnp.exp(sk - m)
                    o = jnp.dot(p.astype(BF16), vx, preferred_element_type=F32) / den
                    for pair in range(2):
                        col = (2 * g + pair) * LANES
                        o_ref[s, pl.ds(q0, qb), col:col + LANES] = _unstack_pair(o, qb, pair).astype(out_dtype)
                    if with_lse:
                        lse = m + jnp.log(den)
                        for a in range(4):
                            lse_tile = lse_tile + jnp.where(lane == 4 * g + a, lse[a * qb:(a + 1) * qb], 0.0)
                if with_lse:
                    lse_ref[s, pl.ds(q0, qb), :] = lse_tile
                return carry

            lax.fori_loop(0, nblk, block, 0)

    in_specs = [pl.BlockSpec((seq_blk, length, N_HEADS * HEAD_DIM), lambda n: (n, 0, 0)),
                pl.BlockSpec((seq_blk, length, N_KV * HEAD_DIM), lambda n: (n, 0, 4)),
                pl.BlockSpec((seq_blk, length, N_KV * HEAD_DIM), lambda n: (n, 0, 5))]
    args = [qkv3, qkv3, qkv3]
    if with_sink:
        in_specs.insert(0, pl.BlockSpec(memory_space=pltpu.SMEM))
        args.insert(0, sink)
    out_specs = [pl.BlockSpec((seq_blk, length, D_MODEL), lambda n: (n, 0, 0))]
    out_shape = [jax.ShapeDtypeStruct((n_seq, length, D_MODEL), out_dtype)]
    if with_lse:
        out_specs.append(pl.BlockSpec((seq_blk, length, LANES), lambda n: (n, 0, 0)))
        out_shape.append(jax.ShapeDtypeStruct((n_seq, length, LANES), F32))
    outs = pl.pallas_call(
        body, name=name, grid=(n_seq // seq_blk,), in_specs=in_specs, out_specs=out_specs, out_shape=out_shape,
        scratch_shapes=[pltpu.VMEM((N_KV, length, LANES), BF16), pltpu.VMEM((N_KV, length, LANES), BF16)],
        compiler_params=_cp(),
    )(*args)
    return [o.reshape(n_seq * length, o.shape[-1]) for o in outs]


def _attn_bwd(qkv, do, adj, sink, cos, sin, n_seq, length, half_window, seq_blk, dil, name):
    qb, kw, nblk = _attn_geometry(length, half_window)
    scale = 1.0 / math.sqrt(HEAD_DIM)
    with_sink = sink is not None
    nt = (((1,), (1,)), ((), ()))
    tn = (((0,), (0,)), ((), ()))
    qkv3 = qkv.reshape(n_seq, length, QKV_W)
    do3 = do.reshape(n_seq, length, D_MODEL)
    aux3 = adj.reshape(n_seq, length, LANES)
    tabs = [t.reshape(dil, length, LANES) for t in (cos, sin)]
    tab_blocks = dil // seq_blk if dil >= seq_blk else 1

    def body(*refs):
        refs = list(refs)
        sink_ref = refs.pop(0) if with_sink else None
        q_ref, k_ref, v_ref, do_ref, aux_ref, cos_ref, sin_ref, dqkv_ref = refs[:8]
        ds_ref = refs[8] if with_sink else None
        kx_ref, vx_ref, dkx_ref, dvx_ref = refs[-4:]
        lane = lax.broadcasted_iota(jnp.int32, (qb, LANES), 1)
        lo = lane < HEAD_DIM
        if with_sink:
            @pl.when(pl.program_id(0) == 0)
            def _():
                ds_ref[...] = jnp.zeros_like(ds_ref)

        for s in range(seq_blk):
            ts = s % dil
            _dup_kv(k_ref, kx_ref, s, length)
            _dup_kv(v_ref, vx_ref, s, length)
            dkx_ref[...] = jnp.zeros_like(dkx_ref)
            dvx_ref[...] = jnp.zeros_like(dvx_ref)

            def block(i, dsink):
                q0 = pl.multiple_of(i * qb, qb)
                k0 = pl.multiple_of(jnp.clip(i * qb - half_window, 0, length - kw), HEAD_DIM)
                valid = _band_mask(q0, k0, qb, kw, half_window)
                cs = cos_ref[ts, pl.ds(q0, qb), :]
                sn = sin_ref[ts, pl.ds(q0, qb), :]
                adj_tile = aux_ref[s, pl.ds(q0, qb), :]
                for g in range(N_KV):
                    qs = _stack_heads(q_ref, s, q0, qb, g)
                    dos = _stack_heads(do_ref, s, q0, qb, g)
                    kx = kx_ref[g, pl.ds(k0, kw), :]
                    vx = vx_ref[g, pl.ds(k0, kw), :]
                    sc = lax.dot_general(qs, kx, nt, preferred_element_type=F32) * scale
                    sc = jnp.where(valid, sc, NEG_INF)
                    m = jnp.max(sc, axis=1, keepdims=True)
                    if with_sink:
                        sk = _head_column([sink_ref[4 * g + a] for a in range(4)], qb)
                        m = jnp.maximum(m, sk)
                    e = jnp.exp(sc - m)
                    den = jnp.sum(e, axis=1, keepdims=True)
                    if with_sink:
                        esk = jnp.exp(sk - m)
                        den = den + esk
                    rden = 1.0 / den
                    p = e * rden
                    shift = jnp.concatenate(
                        [jnp.sum(jnp.where(lane == 4 * g + a, adj_tile, 0.0), axis=1, keepdims=True)
                         for a in range(4)], axis=0)
                    dp = lax.dot_general(dos, vx, nt, preferred_element_type=F32)
                    dsc = p * (dp + shift)
                    if with_sink:
                        dsk = esk * rden * shift
                        for a in range(4):
                            tot = jnp.sum(dsk[a * qb:(a + 1) * qb], axis=0, keepdims=True)
                            dsink = dsink + jnp.where(lane[:1] == 4 * g + a, tot, 0.0)
                    dsb = dsc.astype(BF16)
                    pb = p.astype(BF16)
                    dq = jnp.dot(dsb, kx, preferred_element_type=F32) * scale
                    for pair in range(2):
                        col = (2 * g + pair) * LANES
                        tile = _rope_t(_unstack_pair(dq, qb, pair), cs, sn)
                        dqkv_ref[s, pl.ds(q0, qb), col:col + LANES] = tile.astype(BF16)
                    dkx_ref[g, pl.ds(k0, kw), :] += lax.dot_general(dsb, qs, tn, preferred_element_type=F32) * scale
                    dvx_ref[g, pl.ds(k0, kw), :] += lax.dot_general(pb, dos, tn, preferred_element_type=F32)
                return dsink

            dsink = lax.fori_loop(0, nblk, block, jnp.zeros((1, LANES), F32))
            if with_sink:
                ds_ref[0:1, :] += dsink

            ch = min(length, 256)
            lo_c = lax.broadcasted_iota(jnp.int32, (ch, LANES), 1) < HEAD_DIM

            def fin(c, carry):
                r0 = pl.multiple_of(c * ch, ch)
                cs = cos_ref[ts, pl.ds(r0, ch), :]
                sn = sin_ref[ts, pl.ds(r0, ch), :]
                for j in range(N_KV // 2):
                    both = []
                    for acc_ref in (dkx_ref, dvx_ref):
                        t0 = acc_ref[2 * j, pl.ds(r0, ch), :]
                        t1 = acc_ref[2 * j + 1, pl.ds(r0, ch), :]
                        t0 = t0 + pltpu.roll(t0, HEAD_DIM, 1)
                        t1 = t1 + pltpu.roll(t1, HEAD_DIM, 1)
                        both.append(jnp.where(lo_c, t0, t1))
                    kcol = N_HEADS * HEAD_DIM + j * LANES
                    vcol = (N_HEADS + N_KV) * HEAD_DIM + j * LANES
                    dqkv_ref[s, pl.ds(r0, ch), kcol:kcol + LANES] = _rope_t(both[0], cs, sn).astype(BF16)
                    dqkv_ref[s, pl.ds(r0, ch), vcol:vcol + LANES] = both[1].astype(BF16)
                return carry

            lax.fori_loop(0, length // ch, fin, 0)

    seq_map = lambda n: (n, 0, 0)
    tab_map = (lambda n: (n % tab_blocks, 0, 0)) if dil >= seq_blk else (lambda n: (0, 0, 0))
    tab_rows = min(seq_blk, dil)
    in_specs = [pl.BlockSpec((seq_blk, length, N_HEADS * HEAD_DIM), seq_map),
                pl.BlockSpec((seq_blk, length, N_KV * HEAD_DIM), lambda n: (n, 0, 4)),
                pl.BlockSpec((seq_blk, length, N_KV * HEAD_DIM), lambda n: (n, 0, 5)),
                pl.BlockSpec((seq_blk, length, D_MODEL), seq_map),
                pl.BlockSpec((seq_blk, length, LANES), seq_map),
                pl.BlockSpec((tab_rows, length, LANES), tab_map),
                pl.BlockSpec((tab_rows, length, LANES), tab_map)]
    args = [qkv3, qkv3, qkv3, do3, aux3] + tabs
    if with_sink:
        in_specs.insert(0, pl.BlockSpec(memory_space=pltpu.SMEM))
        args.insert(0, sink)
    out_specs = [pl.BlockSpec((seq_blk, length, QKV_W), seq_map)]
    out_shape = [jax.ShapeDtypeStruct((n_seq, length, QKV_W), BF16)]
    if with_sink:
        out_specs.append(pl.BlockSpec((8, LANES), lambda n: (0, 0)))
        out_shape.append(jax.ShapeDtypeStruct((8, LANES), F32))
    outs = pl.pallas_call(
        body, name=name, grid=(n_seq // seq_blk,), in_specs=in_specs, out_specs=out_specs, out_shape=out_shape,
        scratch_shapes=[pltpu.VMEM((N_KV, length, LANES), BF16), pltpu.VMEM((N_KV, length, LANES), BF16),
                        pltpu.VMEM((N_KV, length, LANES), F32), pltpu.VMEM((N_KV, length, LANES), F32)],
        compiler_params=_cp(),
    )(*args)
    dqkv = outs[0].reshape(n_seq * length, QKV_W)
    return (dqkv, outs[1]) if with_sink else (dqkv, None)


def _head_expander():
    h = jnp.arange(LANES)[:, None]
    l = jnp.arange(D_MODEL)[None, :]
    return (l // HEAD_DIM == h).astype(F32)


def _mix_weights(lses):
    m = jnp.maximum(jnp.maximum(lses[0], lses[1]), lses[2])
    es = [jnp.exp(v - m) for v in lses]
    tot = es[0] + es[1] + es[2]
    return [e / tot for e in es]


def _mix_fwd(os_, lses, name):
    t = os_[0].shape[0]
    tm = _row_tile(t, 512)
    hi = lax.Precision.HIGHEST

    def body(o0, o1, o2, l0, l1, l2, e_ref, out_ref):
        wts = _mix_weights([l0[...], l1[...], l2[...]])
        acc = jnp.zeros((tm, D_MODEL), F32)
        for w, o_ref in zip(wts, (o0, o1, o2)):
            acc = acc + jnp.dot(w, e_ref[...], precision=hi, preferred_element_type=F32) * o_ref[...]
        out_ref[...] = acc.astype(BF16)

    row = pl.BlockSpec((tm, D_MODEL), lambda i: (i, 0))
    lrow = pl.BlockSpec((tm, LANES), lambda i: (i, 0))
    return pl.pallas_call(
        body, name=name, grid=(t // tm,),
        in_specs=[row] * 3 + [lrow] * 3 + [pl.BlockSpec((LANES, D_MODEL), lambda i: (0, 0))],
        out_specs=row, out_shape=jax.ShapeDtypeStruct((t, D_MODEL), BF16), compiler_params=_cp(),
    )(*os_, *lses, _head_expander())


def _mix_bwd(dmix, os_, lses, name):
    t = dmix.shape[0]
    tm = _row_tile(t, 512)
    hi = lax.Precision.HIGHEST

    def body(d_ref, o0, o1, o2, l0, l1, l2, e_ref, et_ref, do0, do1, do2, a0, a1, a2):
        wts = _mix_weights([l0[...], l1[...], l2[...]])
        dv = d_ref[...].astype(F32)
        cs = [jnp.dot(dv * o_ref[...], et_ref[...], precision=hi, preferred_element_type=F32) for o_ref in (o0, o1, o2)]
        mean_c = wts[0] * cs[0] + wts[1] * cs[1] + wts[2] * cs[2]
        for w, c, do_ref, a_ref in zip(wts, cs, (do0, do1, do2), (a0, a1, a2)):
            do_ref[...] = (jnp.dot(w, e_ref[...], precision=hi, preferred_element_type=F32) * dv).astype(BF16)
            a_ref[...] = w * (c - mean_c) - w * c

    row = pl.BlockSpec((tm, D_MODEL), lambda i: (i, 0))
    lrow = pl.BlockSpec((tm, LANES), lambda i: (i, 0))
    e = _head_expander()
    return pl.pallas_call(
        body, name=name, grid=(t // tm,),
        in_specs=[row] * 4 + [lrow] * 3 + [pl.BlockSpec((LANES, D_MODEL), lambda i: (0, 0)),
                                            pl.BlockSpec((D_MODEL, LANES), lambda i: (0, 0))],
        out_specs=[row] * 3 + [lrow] * 3,
        out_shape=[jax.ShapeDtypeStruct((t, D_MODEL), BF16)] * 3 + [jax.ShapeDtypeStruct((t, LANES), F32)] * 3,
        compiler_params=_cp(),
    )(dmix, *os_, *lses, e, e.T)


def _group_geometry(batch, seq, dil, window):
    length = seq // dil
    n_seq = batch * dil
    seq_blk = max(1, min(dil, 1024 // length))
    return n_seq, length, (window // 2) // dil, seq_blk


def _local_step(x, target, a_in, a_sink, a_out, b_in, b_out, norm_mix, norm_ffn, wg, wu, wd, final_norm):
    batch, seq, _ = x.shape
    t = batch * seq
    x0 = x.reshape(t, D_MODEL)
    tgt = target.reshape(t, D_MODEL)
    tabs = {d: _rope_tables(seq, d) for _, d in DILATED}
    nm = [norm_mix[i:i + 1] for i in range(2)]
    nf = [norm_ffn[i:i + 1] for i in range(2)]

    h0 = _rms_fwd(x0, nm[0], "rms_mix0")
    qkv0 = _qkv_proj(h0, a_in, *tabs[1], 0, "qkv0")
    (o0,) = _attn_fwd(qkv0, a_sink, batch, seq, HALF_WINDOW_A, 1, BF16, False, "attn0")
    x1 = _mm_res(o0, a_out, x0, "out0")
    hf0 = _rms_fwd(x1, nf[0], "rms_ffn0")
    g0, u0, act0 = _ffn_up(hf0, wg, wu, 0, "ffn_up0")
    x2 = _ffn_down(act0, wd, x1, 0, "ffn_down0")

    h1 = _rms_fwd(x2, nm[1], "rms_mix1")
    geo = [_group_geometry(batch, seq, d, w) for w, d in DILATED]
    h1g, qkv1, o1, lse1 = [], [], [], []
    for gi, (_, d) in enumerate(DILATED):
        n_seq, length, hw, sb = geo[gi]
        hp = _to_residue(h1, batch, d)
        pj = _qkv_proj(hp, b_in, *tabs[d], gi, f"qkv1_{gi}")
        o, lse = _attn_fwd(pj, None, n_seq, length, hw, sb, F32, True, f"attn1_{gi}")
        h1g.append(hp)
        qkv1.append(pj)
        o1.append(_from_residue(o, batch, d))
        lse1.append(_from_residue(lse, batch, d))
    omix = _mix_fwd(o1, lse1, "mix")
    x3 = _mm_res(omix, b_out, x2, "out1")
    hf1 = _rms_fwd(x3, nf[1], "rms_ffn1")
    g1, u1, act1 = _ffn_up(hf1, wg, wu, 1, "ffn_up1")
    x4 = _ffn_down(act1, wd, x3, 1, "ffn_down1")

    dx4, dx4b, loss_cols, d_final = _final_loss(x4, final_norm.reshape(1, D_MODEL), tgt, "final_loss")

    def ffn_bwd(dxo, dxob, x_mid, hf, g, u, act, layer):
        dg, du = _ffn_down_bwd(dxob, wd, g, u, layer, f"ffn_down_bwd{layer}")
        (d_wd,) = _mm_tn(act, [dxob], f"grad_wd{layer}")
        dh = _ffn_up_bwd(dg, du, wg, wu, layer, f"ffn_up_bwd{layer}")
        d_wg, d_wu = _mm_tn(hf, [dg, du], f"grad_wgu{layer}")
        dxm, dxmb, d_nf = _rms_bwd(x_mid, nf[layer], [dh], dxo, f"rms_ffn_bwd{layer}")
        return dxm, dxmb, d_nf, d_wg, d_wu, d_wd

    dx3, dx3b, d_nf1, d_wg1, d_wu1, d_wd1 = ffn_bwd(dx4, dx4b, x3, hf1, g1, u1, act1, 1)

    dmix = _mm_nt(dx3b, b_out, 0, BF16, "out1_bwd")
    (d_b_out,) = _mm_tn(omix, [dx3b], "grad_b_out")
    mb = _mix_bwd(dmix, o1, lse1, "mix_bwd")
    dh1, d_b_in = [], []
    for gi, (_, d) in enumerate(DILATED):
        n_seq, length, hw, sb = geo[gi]
        dog = _to_residue(mb[gi], batch, d)
        adj = _to_residue(mb[3 + gi], batch, d)
        dpj, _ = _attn_bwd(qkv1[gi], dog, adj, None, *tabs[d], n_seq, length, hw, sb, d, f"attn1_bwd{gi}")
        (dw,) = _mm_tn(h1g[gi], [dpj], f"grad_b_in{gi}")
        d_b_in.append(dw)
        dh1.append(_from_residue(_mm_nt(dpj, b_in, gi, F32, f"qkv1_bwd{gi}"), batch, d))
    dx2, dx2b, d_nm1 = _rms_bwd(x2, nm[1], dh1, dx3, "rms_mix_bwd1")

    dx1, dx1b, d_nf0, d_wg0, d_wu0, d_wd0 = ffn_bwd(dx2, dx2b, x1, hf0, g0, u0, act0, 0)

    do0, adj0 = _out_bwd(dx1b, a_out, o0, "out0_bwd")
    (d_a_out,) = _mm_tn(o0, [dx1b], "grad_a_out")
    dqkv0, d_sink = _attn_bwd(qkv0, do0, adj0, a_sink, *tabs[1], batch, seq, HALF_WINDOW_A, 1, 1, "attn0_bwd")
    (d_a_in,) = _mm_tn(h0, [dqkv0], "grad_a_in")
    dh0 = _mm_nt(dqkv0, a_in, 0, F32, "qkv0_bwd")
    gx, _, d_nm0 = _rms_bwd(x0, nm[0], [dh0], dx1, "rms_mix_bwd0")

    grads = dict(a_in=d_a_in, a_out=d_a_out, b_in=jnp.concatenate(d_b_in, axis=1), b_out=d_b_out,
                 wg=(d_wg0, d_wg1), wu=(d_wu0, d_wu1), wd=(d_wd0, d_wd1))
    vecs = dict(norm_mix=(d_nm0, d_nm1), norm_ffn=(d_nf0, d_nf1), final=d_final, loss_cols=loss_cols, sink=d_sink)
    return gx.reshape(x.shape), grads, vecs


ANY = pl.BlockSpec(memory_space=pl.ANY)


def _me():
    return lax.axis_index("x"), lax.axis_index("y"), lax.axis_index("c")


def _chip_peer(x, y, j):
    px = 1 - x if j & 2 else x
    py = 1 - y if j & 1 else y
    return px, py, 2 * px + py


def _remote(src, dst, sems, k, dev):
    return pltpu.make_async_remote_copy(src_ref=src, dst_ref=dst, send_sem=sems[0].at[k], recv_sem=sems[1].at[k],
                                        device_id=dev, device_id_type=MESH)


def _col_window(ref, q, width):
    return ref.at[:, pl.ds(pl.multiple_of(q * width, LANES), width)]


def _half0(ref, h):
    n = ref.shape[0] // 2
    return ref.at[pl.ds(h * n, n)]


def _half1(ref, h):
    n = ref.shape[1] // 2
    return ref.at[:, pl.ds(h * n, n)]


def _gather_weights(shards):
    col_fam = (True, False, True, False, False, False, False)
    n_w = len(shards)

    def body(*refs):
        ins, outs = refs[:n_w], refs[n_w:2 * n_w]
        sems = refs[2 * n_w:2 * n_w + 2]
        lsem = refs[2 * n_w + 2]
        x, y, c = _me()
        myq = 2 * x + y
        sib = (x, y, 1 - c)

        def slot(w, q):
            if col_fam[w]:
                return _col_window(outs[w], q, ins[w].shape[1])
            return outs[w].at[q]

        local = [pltpu.make_async_copy(ins[w], slot(w, myq), lsem.at[w]) for w in range(n_w)]
        for cp in local:
            cp.start()
        first = []
        for w in range(n_w):
            for j in (1, 2, 3):
                px, py, _ = _chip_peer(x, y, j)
                cp = _remote(_half0(ins[w], c), _half0(slot(w, myq), c), sems, w * 6 + j - 1, (px, py, c))
                cp.start()
                first.append(cp)
        passed = []
        for w in range(n_w):
            for j in (1, 2, 3):
                _, _, pq = _chip_peer(x, y, j)
                land = _half0(slot(w, pq), c)
                _remote(land, land, sems, w * 6 + j - 1, sib).wait_recv()
                cp = _remote(land, land, sems, w * 6 + 2 + j, sib)
                cp.start()
                passed.append(cp)
        for w in range(n_w):
            for j in (1, 2, 3):
                _, _, pq = _chip_peer(x, y, j)
                land = _half0(slot(w, pq), 1 - c)
                _remote(land, land, sems, w * 6 + 2 + j, sib).wait_recv()
        for cp in first + passed:
            cp.wait_send()
        for cp in local:
            cp.wait()

    out_shape = []
    for w, s in enumerate(shards):
        if col_fam[w]:
            out_shape.append(jax.ShapeDtypeStruct((s.shape[0], N_CHIPS * s.shape[1]), s.dtype))
        else:
            out_shape.append(jax.ShapeDtypeStruct((N_CHIPS,) + s.shape, s.dtype))
    return pl.pallas_call(
        body, name="gather_weights", in_specs=[ANY] * n_w, out_specs=[ANY] * n_w, out_shape=out_shape,
        scratch_shapes=[pltpu.SemaphoreType.DMA((6 * n_w,)), pltpu.SemaphoreType.DMA((6 * n_w,)),
                        pltpu.SemaphoreType.DMA((n_w,))],
    )(*shards)


def _grad_half(ref, col, h):
    return _half0(ref, h) if col else _half1(ref, h)


def _swap_halves_with_sibling(grads, col_fam):
    n_w = len(grads)

    def body(*refs):
        ins, outs = refs[:n_w], refs[n_w:2 * n_w]
        sems = refs[2 * n_w:]
        x, y, c = _me()
        sib = (x, y, 1 - c)
        cps = [_remote(_grad_half(ins[w], col_fam[w], 1 - c), outs[w], sems, w, sib) for w in range(n_w)]
        for cp in cps:
            cp.start()
        for cp in cps:
            cp.wait_recv()
        for cp in cps:
            cp.wait_send()

    out_shape = []
    for w, g in enumerate(grads):
        shp = (g.shape[0] // 2, g.shape[1]) if col_fam[w] else (g.shape[0], g.shape[1] // 2, g.shape[2])
        out_shape.append(jax.ShapeDtypeStruct(shp, g.dtype))
    return pl.pallas_call(
        body, name="grad_swap_sibling", in_specs=[ANY] * n_w, out_specs=[ANY] * n_w, out_shape=out_shape,
        scratch_shapes=[pltpu.SemaphoreType.DMA((n_w,)), pltpu.SemaphoreType.DMA((n_w,))],
    )(*grads)


def _half_add(mine, recv, c_arr, col, name):
    if col:
        rows, n = recv.shape
        tr = rows // 2
        grid = (2,)
        in_specs = [pl.BlockSpec((tr, n), lambda i, c: (2 * c[0] + i, 0)), pl.BlockSpec((tr, n), lambda i, c: (i, 0))]
        out_spec = pl.BlockSpec((tr, n), lambda i, c: (i, 0))
    else:
        _, rows, n = recv.shape
        grid = (N_CHIPS,)
        in_specs = [pl.BlockSpec((None, rows, n), lambda q, c: (q, c[0], 0)),
                    pl.BlockSpec((None, rows, n), lambda q, c: (q, 0, 0))]
        out_spec = pl.BlockSpec((None, rows, n), lambda q, c: (q, 0, 0))

    def body(c_ref, a_ref, b_ref, o_ref):
        o_ref[...] = (a_ref[...].astype(F32) + b_ref[...].astype(F32)).astype(BF16)

    return pl.pallas_call(
        body, name=name,
        grid_spec=pltpu.PrefetchScalarGridSpec(num_scalar_prefetch=1, grid=grid, in_specs=in_specs, out_specs=out_spec),
        out_shape=jax.ShapeDtypeStruct(recv.shape, BF16), compiler_params=_cp(),
    )(c_arr, mine, recv)


def _scatter_chip_sums(sums, col_fam):
    n_w = len(sums)

    def body(*refs):
        ins, outs = refs[:n_w], refs[n_w:2 * n_w]
        sems = refs[2 * n_w:2 * n_w + 2]
        lsem = refs[2 * n_w + 2]
        x, y, c = _me()
        myq = 2 * x + y

        def slab(w, q):
            if col_fam[w]:
                return _col_window(ins[w], q, ins[w].shape[1] // N_CHIPS)
            return ins[w].at[q]

        local = [pltpu.make_async_copy(slab(w, myq), outs[w].at[myq], lsem.at[w]) for w in range(n_w)]
        for cp in local:
            cp.start()
        cps = []
        for w in range(n_w):
            for j in (1, 2, 3):
                px, py, pq = _chip_peer(x, y, j)
                cp = _remote(slab(w, pq), outs[w].at[myq], sems, w * 3 + j - 1, (px, py, c))
                cp.start()
                cps.append(cp)
        for w in range(n_w):
            for j in (1, 2, 3):
                _, _, pq = _chip_peer(x, y, j)
                land = outs[w].at[pq]
                _remote(land, land, sems, w * 3 + j - 1, (x, y, c)).wait_recv()
        for cp in cps:
            cp.wait_send()
        for cp in local:
            cp.wait()

    out_shape = []
    for w, s in enumerate(sums):
        shp = (s.shape[0], s.shape[1] // N_CHIPS) if col_fam[w] else s.shape[1:]
        out_shape.append(jax.ShapeDtypeStruct((N_CHIPS,) + shp, s.dtype))
    return pl.pallas_call(
        body, name="grad_scatter_chips", in_specs=[ANY] * n_w, out_specs=[ANY] * n_w, out_shape=out_shape,
        scratch_shapes=[pltpu.SemaphoreType.DMA((3 * n_w,)), pltpu.SemaphoreType.DMA((3 * n_w,)),
                        pltpu.SemaphoreType.DMA((n_w,))],
    )(*sums)


def _sum_chips(parts, name):
    _, rows, n = parts.shape
    tr = rows // 2 if rows % 32 == 0 else rows

    def body(p_ref, o_ref):
        acc = p_ref[0].astype(F32)
        for q in range(1, N_CHIPS):
            acc = acc + p_ref[q].astype(F32)
        o_ref[...] = acc

    return pl.pallas_call(
        body, name=name, grid=(rows // tr,),
        in_specs=[pl.BlockSpec((N_CHIPS, tr, n), lambda i: (0, i, 0))],
        out_specs=pl.BlockSpec((tr, n), lambda i: (i, 0)),
        out_shape=jax.ShapeDtypeStruct((rows, n), F32), compiler_params=_cp(),
    )(parts)


def _join_halves(halves, out_shapes, place):
    n_h = len(halves)
    n_o = len(out_shapes)

    def body(*refs):
        ins, outs = refs[:n_h], refs[n_h:n_h + n_o]
        sems = refs[n_h + n_o:n_h + n_o + 2]
        lsem = refs[n_h + n_o + 2]
        x, y, c = _me()
        sib = (x, y, 1 - c)

        def dst(k, h):
            o, lead = place[k]
            return _half0(outs[o].at[lead], h)

        local = [pltpu.make_async_copy(ins[k], dst(k, c), lsem.at[k]) for k in range(n_h)]
        cps = [_remote(ins[k], dst(k, c), sems, k, sib) for k in range(n_h)]
        for cp in local + cps:
            cp.start()
        for k in range(n_h):
            land = dst(k, 1 - c)
            _remote(land, land, sems, k, sib).wait_recv()
        for cp in cps:
            cp.wait_send()
        for cp in local:
            cp.wait()

    return pl.pallas_call(
        body, name="grad_join_sibling", in_specs=[ANY] * n_h, out_specs=[ANY] * n_o,
        out_shape=[jax.ShapeDtypeStruct(s, F32) for s in out_shapes],
        scratch_shapes=[pltpu.SemaphoreType.DMA((n_h,)), pltpu.SemaphoreType.DMA((n_h,)),
                        pltpu.SemaphoreType.DMA((n_h,))],
    )(*halves)


def _allreduce_rows(rows):
    n_dev = 8
    n_r = len(rows)
    assert n_r <= 8

    def body(*refs):
        r_refs = refs[:n_r]
        o_ref, slots, send_sems, recv_sems = refs[n_r:]
        x, y, c = _me()
        me = 4 * x + 2 * y + c
        slots[me] = jnp.concatenate([r[...] for r in r_refs] + [jnp.zeros((8 - n_r, D_MODEL), F32)], axis=0)

        def peer(k):
            return (1 - x if k & 4 else x, 1 - y if k & 2 else y, 1 - c if k & 1 else c)

        cps = []
        for k in range(1, n_dev):
            cp = pltpu.make_async_remote_copy(src_ref=slots.at[me], dst_ref=slots.at[me], send_sem=send_sems.at[k - 1],
                                              recv_sem=recv_sems.at[k - 1], device_id=peer(k), device_id_type=MESH)
            cp.start()
            cps.append(cp)
        for k in range(1, n_dev):
            px, py, pc = peer(k)
            land = slots.at[4 * px + 2 * py + pc]
            pltpu.make_async_remote_copy(src_ref=land, dst_ref=land, send_sem=send_sems.at[k - 1],
                                         recv_sem=recv_sems.at[k - 1], device_id=peer(k),
                                         device_id_type=MESH).wait_recv()
        for cp in cps:
            cp.wait_send()
        acc = slots[0]
        for d in range(1, n_dev):
            acc = acc + slots[d]
        o_ref[...] = acc

    vm = pl.BlockSpec(memory_space=pltpu.VMEM)
    return pl.pallas_call(
        body, name="allreduce_rows", in_specs=[vm] * n_r, out_specs=vm,
        out_shape=jax.ShapeDtypeStruct((8, D_MODEL), F32),
        scratch_shapes=[pltpu.VMEM((n_dev, 8, D_MODEL), F32), pltpu.SemaphoreType.DMA((n_dev - 1,)),
                        pltpu.SemaphoreType.DMA((n_dev - 1,))],
    )(*rows)


def _adamw(w, g, m, v, name):
    shape = w.shape
    if len(shape) == 1:
        lead, rows, cols = 1, 1, shape[0]
    else:
        rows, cols = shape[-2:]
        lead = math.prod(shape[:-2])
    args = [a.reshape(lead, rows, cols) for a in (w, g, m, v)]
    tr = rows // 2 if rows % 16 == 0 else rows

    def body(w_ref, g_ref, m_ref, v_ref, d_ref, nm_ref, nv_ref):
        gv = g_ref[...]
        nm = ADAM_B1 * m_ref[...] + (1.0 - ADAM_B1) * gv
        nv = ADAM_B2 * v_ref[...] + (1.0 - ADAM_B2) * jnp.square(gv)
        m_hat = nm / (1.0 - ADAM_B1 ** ADAM_STEP)
        v_hat = nv / (1.0 - ADAM_B2 ** ADAM_STEP)
        d_ref[...] = -ADAM_LR * (m_hat / (jnp.sqrt(v_hat) + ADAM_EPS) + ADAM_WD * w_ref[...])
        nm_ref[...] = nm
        nv_ref[...] = nv

    spec = pl.BlockSpec((None, tr, cols), lambda l, i: (l, i, 0))
    outs = pl.pallas_call(
        body, name=name, grid=(lead, rows // tr), in_specs=[spec] * 4, out_specs=[spec] * 3,
        out_shape=[jax.ShapeDtypeStruct((lead, rows, cols), F32)] * 3, compiler_params=_cp(),
    )(*args)
    return [o.reshape(shape) for o in outs]


def kernel(x, a_w_in, a_sink, a_w_out, b_w_in, b_w_out, norm_mix, norm_ffn, w_gate, w_up, w_down, final_norm, loss_target, m_a_w_in, m_a_sink, m_a_w_out, m_b_w_in, m_b_w_out, m_norm_mix, m_norm_ffn, m_w_gate, m_w_up, m_w_down, m_final_norm, v_a_w_in, v_a_sink, v_a_w_out, v_b_w_in, v_b_w_out, v_norm_mix, v_norm_ffn, v_w_gate, v_w_up, v_w_down, v_final_norm):
    weights = dict(a_w_in=a_w_in, a_sink=a_sink, a_w_out=a_w_out, b_w_in=b_w_in, b_w_out=b_w_out, norm_mix=norm_mix,
                   norm_ffn=norm_ffn, w_gate=w_gate, w_up=w_up, w_down=w_down, final_norm=final_norm)
    mom = dict(a_w_in=m_a_w_in, a_sink=m_a_sink, a_w_out=m_a_w_out, b_w_in=m_b_w_in, b_w_out=m_b_w_out,
               norm_mix=m_norm_mix, norm_ffn=m_norm_ffn, w_gate=m_w_gate, w_up=m_w_up, w_down=m_w_down,
               final_norm=m_final_norm)
    var = dict(a_w_in=v_a_w_in, a_sink=v_a_sink, a_w_out=v_a_w_out, b_w_in=v_b_w_in, b_w_out=v_b_w_out,
               norm_mix=v_norm_mix, norm_ffn=v_norm_ffn, w_gate=v_w_gate, w_up=v_w_up, w_down=v_w_down,
               final_norm=v_final_norm)
    order = ["a_w_in", "a_sink", "a_w_out", "b_w_in", "b_w_out", "norm_mix", "norm_ffn", "w_gate", "w_up", "w_down",
             "final_norm"]

    shards = [a_w_in[0], a_w_out[0], b_w_in[0], b_w_out[0], w_gate, w_up, w_down]
    a_in, a_out, b_in, b_out, wg, wu, wd = _gather_weights([s.astype(BF16) for s in shards])
    a_out = a_out.reshape(D_MODEL, D_MODEL)
    b_out = b_out.reshape(D_MODEL, D_MODEL)

    gx, grads, vecs = _local_step(x, loss_target, a_in, a_sink[0], a_out, b_in, b_out, norm_mix, norm_ffn, wg, wu, wd,
                                  final_norm)

    rows_out = D_MODEL // N_CHIPS
    partials = [grads["a_in"], grads["b_in"],
                grads["a_out"].reshape(N_CHIPS, rows_out, D_MODEL), grads["b_out"].reshape(N_CHIPS, rows_out, D_MODEL),
                grads["wg"][0], grads["wg"][1], grads["wu"][0], grads["wu"][1], grads["wd"][0], grads["wd"][1]]
    col_fam = (True, True) + (False,) * 8
    names = ("a_in", "b_in", "a_out", "b_out", "wg0", "wg1", "wu0", "wu1", "wd0", "wd1")
    theirs = _swap_halves_with_sibling(partials, col_fam)
    c_arr = lax.axis_index("c").astype(jnp.int32).reshape(1)
    sums = [_half_add(p, r, c_arr, cf, f"chip_sum_{nm}") for p, r, cf, nm in zip(partials, theirs, col_fam, names)]
    contrib = _scatter_chip_sums(sums, col_fam)
    halves = [_sum_chips(p, f"sum_chips_{nm}") for p, nm in zip(contrib, names)]
    g_a_in, g_b_in, g_a_out, g_b_out, g_wg, g_wu, g_wd = _join_halves(
        halves, [a_w_in.shape, b_w_in.shape, a_w_out.shape, b_w_out.shape, w_gate.shape, w_up.shape, w_down.shape],
        [(0, 0), (1, 0), (2, 0), (3, 0), (4, 0), (4, 1), (5, 0), (5, 1), (6, 0), (6, 1)])

    sink_row = jnp.pad(vecs["sink"][0:1], ((0, 0), (0, D_MODEL - LANES)))
    tot = _allreduce_rows([vecs["norm_mix"][0], vecs["norm_mix"][1], vecs["norm_ffn"][0], vecs["norm_ffn"][1],
                           vecs["final"], vecs["loss_cols"], sink_row])
    loss = (0.5 / D_MODEL) * jnp.sum(tot[5])
    gw = dict(a_w_in=g_a_in, a_sink=tot[6:7, :N_HEADS], a_w_out=g_a_out, b_w_in=g_b_in, b_w_out=g_b_out,
              norm_mix=tot[0:2], norm_ffn=tot[2:4], w_gate=g_wg, w_up=g_wu, w_down=g_wd, final_norm=tot[4])

    delta, new_m, new_v = {}, {}, {}
    for n in order:
        delta[n], new_m[n], new_v[n] = _adamw(weights[n], gw[n], mom[n], var[n], f"adamw_{n}")
    return (loss, gx, *[gw[n] for n in order], *[delta[n] for n in order], *[new_m[n] for n in order],
            *[new_v[n] for n in order])
```

```python
import functools
import math

import jax
import jax.numpy as jnp
from jax import lax
from jax.experimental import pallas as pl
from jax.experimental.pallas import tpu as pltpu

F32 = jnp.float32
BF16 = jnp.bfloat16

D_MODEL = 1024
HEAD_DIM = 64
N_HEADS = 16
N_KV = 4
QKV_W = 1536
D_FF = 2816
N_CHIPS = 4
FF_SH = D_FF // N_CHIPS
HALF_WINDOW_A = 128
DILATED = ((128, 1), (512, 4), (2048, 16))
ROPE_THETA = 10000.0
RMS_EPS = 1e-6
NEG_INF = -1e30
LANES = 128
ADAM_LR, ADAM_B1, ADAM_B2, ADAM_EPS, ADAM_WD, ADAM_STEP = 0.001, 0.9, 0.999, 1e-08, 0.01, 10
VMEM_LIMIT = 56 * 1024 * 1024
MESH = pl.DeviceIdType.MESH


def _cp(**kw):
    return pltpu.CompilerParams(vmem_limit_bytes=VMEM_LIMIT, **kw)


def _row_tile(t, cap):
    tm = min(cap, t)
    assert t % tm == 0
    return tm


def _rope_tables(seq, dil):
    inv = 1.0 / (ROPE_THETA ** (jnp.arange(0, HEAD_DIM, 2, dtype=F32) / HEAD_DIM))
    ang = jnp.arange(seq, dtype=F32)[:, None] * inv[None, :]
    cos, sin = jnp.cos(ang), jnp.sin(ang)
    cos = jnp.tile(cos, (1, 4))
    sin = jnp.concatenate([-sin, sin, -sin, sin], axis=1)

    def perm(t):
        return t.reshape(seq // dil, dil, LANES).transpose(1, 0, 2).reshape(seq, LANES)

    return perm(cos), perm(sin)


def _swap_halves(t):
    lane = lax.broadcasted_iota(jnp.int32, t.shape, 1)
    return jnp.where((lane % HEAD_DIM) < HEAD_DIM // 2, pltpu.roll(t, LANES - 32, 1), pltpu.roll(t, 32, 1))


def _rope(t, cos, sin):
    return t * cos + _swap_halves(t) * sin


def _rope_t(t, cos, sin):
    return t * cos - _swap_halves(t) * sin


def _to_residue(t, batch, dil):
    if dil == 1:
        return t
    s = t.shape[0] // batch
    return t.reshape(batch, s // dil, dil, t.shape[1]).transpose(0, 2, 1, 3).reshape(t.shape)


def _from_residue(t, batch, dil):
    if dil == 1:
        return t
    s = t.shape[0] // batch
    return t.reshape(batch, dil, s // dil, t.shape[1]).transpose(0, 2, 1, 3).reshape(t.shape)


def _rms_fwd(x, w, name):
    t = x.shape[0]
    tm = _row_tile(t, 512)

    def body(x_ref, w_ref, o_ref):
        xv = x_ref[...]
        r = lax.rsqrt(jnp.mean(xv * xv, axis=-1, keepdims=True) + RMS_EPS)
        o_ref[...] = ((xv * r) * w_ref[...]).astype(BF16)

    return pl.pallas_call(
        body, name=name, grid=(t // tm,),
        in_specs=[pl.BlockSpec((tm, D_MODEL), lambda i: (i, 0)), pl.BlockSpec((1, D_MODEL), lambda i: (0, 0))],
        out_specs=pl.BlockSpec((tm, D_MODEL), lambda i: (i, 0)),
        out_shape=jax.ShapeDtypeStruct((t, D_MODEL), BF16), compiler_params=_cp(),
    )(x, w)


def _rms_bwd(x, w, dhs, dres, name):
    t = x.shape[0]
    tm = _row_tile(t, 512)
    n = len(dhs)

    def body(*refs):
        x_ref, w_ref = refs[0], refs[1]
        dh_refs = refs[2:2 + n]
        dres_ref = refs[2 + n]
        dx_ref, dxb_ref, dw_ref = refs[3 + n:]
        xv = x_ref[...]
        r = lax.rsqrt(jnp.mean(xv * xv, axis=-1, keepdims=True) + RMS_EPS)
        xh = xv * r
        dy = dh_refs[0][...]
        for k in range(1, n):
            dy = dy + dh_refs[k][...]
        dxh = dy * w_ref[...]
        dx = dres_ref[...] + r * (dxh - xh * jnp.mean(dxh * xh, axis=-1, keepdims=True))
        dx_ref[...] = dx
        dxb_ref[...] = dx.astype(BF16)

        @pl.when(pl.program_id(0) == 0)
        def _():
            dw_ref[...] = jnp.zeros_like(dw_ref)

        dw_ref[...] += jnp.sum(dy * xh, axis=0, keepdims=True)

    row = pl.BlockSpec((tm, D_MODEL), lambda i: (i, 0))
    vec = pl.BlockSpec((1, D_MODEL), lambda i: (0, 0))
    return pl.pallas_call(
        body, name=name, grid=(t // tm,),
        in_specs=[row, vec] + [row] * n + [row],
        out_specs=[row, row, vec],
        out_shape=[jax.ShapeDtypeStruct((t, D_MODEL), F32), jax.ShapeDtypeStruct((t, D_MODEL), BF16),
                   jax.ShapeDtypeStruct((1, D_MODEL), F32)],
        compiler_params=_cp(),
    )(x, w, *dhs, dres)


def _final_loss(x, w, target, name):
    t = x.shape[0]
    tm = _row_tile(t, 512)

    def body(x_ref, w_ref, t_ref, dx_ref, dxb_ref, l_ref, dw_ref):
        xv = x_ref[...]
        r = lax.rsqrt(jnp.mean(xv * xv, axis=-1, keepdims=True) + RMS_EPS)
        xh = xv * r
        err = xh * w_ref[...] - t_ref[...]
        dy = err * (1.0 / D_MODEL)
        dxh = dy * w_ref[...]
        dx = r * (dxh - xh * jnp.mean(dxh * xh, axis=-1, keepdims=True))
        dx_ref[...] = dx
        dxb_ref[...] = dx.astype(BF16)

        @pl.when(pl.program_id(0) == 0)
        def _():
            l_ref[...] = jnp.zeros_like(l_ref)
            dw_ref[...] = jnp.zeros_like(dw_ref)

        l_ref[...] += jnp.sum(err * err, axis=0, keepdims=True)
        dw_ref[...] += jnp.sum(dy * xh, axis=0, keepdims=True)

    row = pl.BlockSpec((tm, D_MODEL), lambda i: (i, 0))
    vec = pl.BlockSpec((1, D_MODEL), lambda i: (0, 0))
    return pl.pallas_call(
        body, name=name, grid=(t // tm,),
        in_specs=[row, vec, row], out_specs=[row, row, vec, vec],
        out_shape=[jax.ShapeDtypeStruct((t, D_MODEL), F32), jax.ShapeDtypeStruct((t, D_MODEL), BF16),
                   jax.ShapeDtypeStruct((1, D_MODEL), F32), jax.ShapeDtypeStruct((1, D_MODEL), F32)],
        compiler_params=_cp(),
    )(x, w, target)


def _qkv_proj(h, w, cos, sin, group, name):
    t = h.shape[0]
    seq = cos.shape[0]
    tm = _row_tile(seq, 1024)
    n_rope = (N_HEADS + N_KV) * HEAD_DIM // LANES

    def body(h_ref, w_ref, cos_ref, sin_ref, o_ref):
        acc = jnp.dot(h_ref[...], w_ref[...], preferred_element_type=F32)
        cs, sn = cos_ref[...], sin_ref[...]
        for c in range(QKV_W // LANES):
            blk = acc[:, c * LANES:(c + 1) * LANES]
            if c < n_rope:
                blk = _rope(blk, cs, sn)
            o_ref[:, c * LANES:(c + 1) * LANES] = blk.astype(BF16)

    tab = pl.BlockSpec((tm, LANES), lambda i: (i % (seq // tm), 0))
    return pl.pallas_call(
        body, name=name, grid=(t // tm,),
        in_specs=[pl.BlockSpec((tm, D_MODEL), lambda i: (i, 0)),
                  pl.BlockSpec((D_MODEL, QKV_W), lambda i: (0, group)), tab, tab],
        out_specs=pl.BlockSpec((tm, QKV_W), lambda i: (i, 0)),
        out_shape=jax.ShapeDtypeStruct((t, QKV_W), BF16), compiler_params=_cp(),
    )(h, w, cos, sin)


def _mm_res(a, w, res, name):
    t, k = a.shape
    tm = _row_tile(t, 1024)

    def body(a_ref, w_ref, r_ref, o_ref):
        o_ref[...] = r_ref[...] + jnp.dot(a_ref[...], w_ref[...], preferred_element_type=F32)

    return pl.pallas_call(
        body, name=name, grid=(t // tm,),
        in_specs=[pl.BlockSpec((tm, k), lambda i: (i, 0)), pl.BlockSpec((k, D_MODEL), lambda i: (0, 0)),
                  pl.BlockSpec((tm, D_MODEL), lambda i: (i, 0))],
        out_specs=pl.BlockSpec((tm, D_MODEL), lambda i: (i, 0)),
        out_shape=jax.ShapeDtypeStruct((t, D_MODEL), F32), compiler_params=_cp(),
    )(a, w, res)


def _mm_nt(dy, w, group, out_dtype, name):
    t, n = dy.shape
    k = w.shape[0]
    tm = _row_tile(t, 1024)

    def body(dy_ref, w_ref, o_ref):
        o_ref[...] = lax.dot_general(dy_ref[...], w_ref[...], (((1,), (1,)), ((), ())),
                                     preferred_element_type=F32).astype(out_dtype)

    return pl.pallas_call(
        body, name=name, grid=(t // tm,),
        in_specs=[pl.BlockSpec((tm, n), lambda i: (i, 0)), pl.BlockSpec((k, n), lambda i: (0, group))],
        out_specs=pl.BlockSpec((tm, k), lambda i: (i, 0)),
        out_shape=jax.ShapeDtypeStruct((t, k), out_dtype), compiler_params=_cp(),
    )(dy, w)


def _out_bwd(dx, w, o, name):
    t = dx.shape[0]
    tm = _row_tile(t, 512)

    def body(dx_ref, w_ref, o_ref, et_ref, do_ref, adj_ref):
        do = lax.dot_general(dx_ref[...], w_ref[...], (((1,), (1,)), ((), ())), preferred_element_type=F32)
        do_ref[...] = do.astype(BF16)
        adj_ref[...] = -_dot_split(do * o_ref[...].astype(F32), et_ref[...])

    row = pl.BlockSpec((tm, D_MODEL), lambda i: (i, 0))
    return pl.pallas_call(
        body, name=name, grid=(t // tm,),
        in_specs=[row, pl.BlockSpec((D_MODEL, D_MODEL), lambda i: (0, 0)), row,
                  pl.BlockSpec((D_MODEL, LANES), lambda i: (0, 0))],
        out_specs=[row, pl.BlockSpec((tm, LANES), lambda i: (i, 0))],
        out_shape=[jax.ShapeDtypeStruct((t, D_MODEL), BF16), jax.ShapeDtypeStruct((t, LANES), F32)],
        compiler_params=_cp(),
    )(dx, w, o, _head_expander().T)


def _mm_tn(a, bs, name):
    aq = a.ndim == 3
    bq = bs[0].ndim == 3
    t, ka = a.shape[-2:]
    n = bs[0].shape[-1]
    nq = N_CHIPS if (aq or bq) else 1
    tt = _row_tile(t, 512)
    tn = n if n <= 1024 else 768
    assert n % tn == 0
    nb = len(bs)
    steps = t // tt

    def body(*refs):
        a_ref = refs[0]
        b_refs = refs[1:1 + nb]
        o_refs = refs[1 + nb:1 + 2 * nb]
        acc_refs = refs[1 + 2 * nb:]
        s = pl.program_id(2)
        av = a_ref[...]
        for b_ref, o_ref, acc_ref in zip(b_refs, o_refs, acc_refs):
            part = lax.dot_general(av, b_ref[...], (((0,), (0,)), ((), ())), preferred_element_type=F32)

            @pl.when(s == 0)
            def _():
                acc_ref[...] = part

            @pl.when(s > 0)
            def _():
                acc_ref[...] += part

            @pl.when(s == steps - 1)
            def _():
                o_ref[...] = acc_ref[...].astype(BF16)

    a_spec = (pl.BlockSpec((None, tt, ka), lambda q, j, s: (q, s, 0)) if aq
              else pl.BlockSpec((tt, ka), lambda q, j, s: (s, 0)))
    b_spec = (pl.BlockSpec((None, tt, tn), lambda q, j, s: (q, s, j)) if bq
              else pl.BlockSpec((tt, tn), lambda q, j, s: (s, j)))
    if nq > 1:
        o_spec = pl.BlockSpec((None, ka, tn), lambda q, j, s: (q, 0, j))
        o_shape = jax.ShapeDtypeStruct((nq, ka, n), BF16)
    else:
        o_spec = pl.BlockSpec((ka, tn), lambda q, j, s: (0, j))
        o_shape = jax.ShapeDtypeStruct((ka, n), BF16)
    outs = pl.pallas_call(
        body, name=name, grid=(nq, n // tn, steps),
        in_specs=[a_spec] + [b_spec] * nb, out_specs=[o_spec] * nb, out_shape=[o_shape] * nb,
        scratch_shapes=[pltpu.VMEM((ka, tn), F32)] * nb, compiler_params=_cp(),
    )(a, *bs)
    return outs


def _sigmoid(x):
    return 1.0 / (1.0 + jnp.exp(-x))


def _ffn_up(h, wg, wu, layer, name):
    t = h.shape[0]
    tm = _row_tile(t, 1024)

    def body(h_ref, wg_ref, wu_ref, g_ref, u_ref, a_ref):
        hv = h_ref[...]
        g = jnp.dot(hv, wg_ref[...], preferred_element_type=F32)
        u = jnp.dot(hv, wu_ref[...], preferred_element_type=F32)
        g_ref[...] = g.astype(BF16)
        u_ref[...] = u.astype(BF16)
        a_ref[...] = (g * _sigmoid(g) * u).astype(BF16)

    wspec = pl.BlockSpec((None, None, D_MODEL, FF_SH), lambda q, i: (q, layer, 0, 0))
    ospec = pl.BlockSpec((None, tm, FF_SH), lambda q, i: (q, i, 0))
    oshape = jax.ShapeDtypeStruct((N_CHIPS, t, FF_SH), BF16)
    return pl.pallas_call(
        body, name=name, grid=(N_CHIPS, t // tm),
        in_specs=[pl.BlockSpec((tm, D_MODEL), lambda q, i: (i, 0)), wspec, wspec],
        out_specs=[ospec] * 3, out_shape=[oshape] * 3, compiler_params=_cp(),
    )(h, wg, wu)


def _ffn_down(a, wd, res, layer, name):
    t = a.shape[1]
    tm = _row_tile(t, 512)

    def body(a_ref, w_ref, r_ref, o_ref):
        acc = r_ref[...]
        for q in range(N_CHIPS):
            acc = acc + jnp.dot(a_ref[q], w_ref[q], preferred_element_type=F32)
        o_ref[...] = acc

    return pl.pallas_call(
        body, name=name, grid=(t // tm,),
        in_specs=[pl.BlockSpec((N_CHIPS, tm, FF_SH), lambda i: (0, i, 0)),
                  pl.BlockSpec((N_CHIPS, None, FF_SH, D_MODEL), lambda i: (0, layer, 0, 0)),
                  pl.BlockSpec((tm, D_MODEL), lambda i: (i, 0))],
        out_specs=pl.BlockSpec((tm, D_MODEL), lambda i: (i, 0)),
        out_shape=jax.ShapeDtypeStruct((t, D_MODEL), F32), compiler_params=_cp(),
    )(a, wd, res)


def _ffn_down_bwd(dx, wd, g, u, layer, name):
    t = dx.shape[0]
    tm = _row_tile(t, 1024)

    def body(dx_ref, w_ref, g_ref, u_ref, dg_ref, du_ref):
        da = lax.dot_general(dx_ref[...], w_ref[...], (((1,), (1,)), ((), ())), preferred_element_type=F32)
        gv = g_ref[...].astype(F32)
        uv = u_ref[...].astype(F32)
        sg = _sigmoid(gv)
        du_ref[...] = (da * (gv * sg)).astype(BF16)
        dg_ref[...] = (da * uv * (sg * (1.0 + gv * (1.0 - sg)))).astype(BF16)

    aspec = pl.BlockSpec((None, tm, FF_SH), lambda q, i: (q, i, 0))
    oshape = jax.ShapeDtypeStruct((N_CHIPS, t, FF_SH), BF16)
    return pl.pallas_call(
        body, name=name, grid=(N_CHIPS, t // tm),
        in_specs=[pl.BlockSpec((tm, D_MODEL), lambda q, i: (i, 0)),
                  pl.BlockSpec((None, None, FF_SH, D_MODEL), lambda q, i: (q, layer, 0, 0)), aspec, aspec],
        out_specs=[aspec] * 2, out_shape=[oshape] * 2, compiler_params=_cp(),
    )(dx, wd, g, u)


def _ffn_up_bwd(dg, du, wg, wu, layer, name):
    t = dg.shape[1]
    tm = _row_tile(t, 512)
    nt = (((1,), (1,)), ((), ()))

    def body(dg_ref, du_ref, wg_ref, wu_ref, o_ref):
        acc = jnp.zeros((tm, D_MODEL), F32)
        for q in range(N_CHIPS):
            acc = acc + lax.dot_general(dg_ref[q], wg_ref[q], nt, preferred_element_type=F32)
            acc = acc + lax.dot_general(du_ref[q], wu_ref[q], nt, preferred_element_type=F32)
        o_ref[...] = acc

    aspec = pl.BlockSpec((N_CHIPS, tm, FF_SH), lambda i: (0, i, 0))
    wspec = pl.BlockSpec((N_CHIPS, None, D_MODEL, FF_SH), lambda i: (0, layer, 0, 0))
    return pl.pallas_call(
        body, name=name, grid=(t // tm,),
        in_specs=[aspec, aspec, wspec, wspec],
        out_specs=pl.BlockSpec((tm, D_MODEL), lambda i: (i, 0)),
        out_shape=jax.ShapeDtypeStruct((t, D_MODEL), F32), compiler_params=_cp(),
    )(dg, du, wg, wu)


def _attn_geometry(length, half_window):
    qb = min(LANES, length)
    kw = min(qb + 2 * half_window, length)
    return qb, kw, length // qb


def _dup_kv(src_ref, dst_ref, s, length):
    ch = min(length, 256)
    lo = lax.broadcasted_iota(jnp.int32, (ch, LANES), 1) < HEAD_DIM

    def chunk(c, carry):
        r0 = pl.multiple_of(c * ch, ch)
        for j in range(N_KV // 2):
            tile = src_ref[s, pl.ds(r0, ch), j * LANES:(j + 1) * LANES].astype(F32)
            rolled = pltpu.roll(tile, HEAD_DIM, 1)
            dst_ref[2 * j, pl.ds(r0, ch), :] = jnp.where(lo, tile, rolled).astype(BF16)
            dst_ref[2 * j + 1, pl.ds(r0, ch), :] = jnp.where(lo, rolled, tile).astype(BF16)
        return carry

    lax.fori_loop(0, length // ch, chunk, 0)


def _stack_heads(ref, s, q0, qb, g):
    lo = lax.broadcasted_iota(jnp.int32, (qb, LANES), 1) < HEAD_DIM
    parts = []
    for a in range(4):
        col = (2 * g + a // 2) * LANES
        tile = ref[s, pl.ds(q0, qb), col:col + LANES]
        keep = lo if a % 2 == 0 else jnp.logical_not(lo)
        parts.append(jnp.where(keep, tile, jnp.zeros_like(tile)))
    return jnp.concatenate(parts, axis=0)


def _unstack_pair(stacked, qb, pair):
    lo = lax.broadcasted_iota(jnp.int32, (qb, LANES), 1) < HEAD_DIM
    return jnp.where(lo, stacked[(2 * pair) * qb:(2 * pair + 1) * qb], stacked[(2 * pair + 1) * qb:(2 * pair + 2) * qb])


def _band_mask(q0, k0, qb, kw, half_window):
    row = lax.broadcasted_iota(jnp.int32, (4 * qb, kw), 0) & (qb - 1)
    col = lax.broadcasted_iota(jnp.int32, (4 * qb, kw), 1)
    return jnp.abs((q0 + row) - (k0 + col)) <= half_window


def _head_column(vals, qb):
    return jnp.concatenate([jnp.full((qb, 1), v, F32) for v in vals], axis=0)


def _attn_fwd(qkv, sink, n_seq, length, half_window, seq_blk, out_dtype, with_lse, name):
    qb, kw, nblk = _attn_geometry(length, half_window)
    scale = 1.0 / math.sqrt(HEAD_DIM)
    with_sink = sink is not None
    nt = (((1,), (1,)), ((), ()))
    qkv3 = qkv.reshape(n_seq, length, QKV_W)

    def body(*refs):
        refs = list(refs)
        sink_ref = refs.pop(0) if with_sink else None
        q_ref, k_ref, v_ref, o_ref = refs[:4]
        lse_ref = refs[4] if with_lse else None
        kx_ref, vx_ref = refs[-2:]
        lane = lax.broadcasted_iota(jnp.int32, (qb, LANES), 1)
        for s in range(seq_blk):
            _dup_kv(k_ref, kx_ref, s, length)
            _dup_kv(v_ref, vx_ref, s, length)

            def block(i, carry):
                q0 = pl.multiple_of(i * qb, qb)
                k0 = pl.multiple_of(jnp.clip(i * qb - half_window, 0, length - kw), HEAD_DIM)
                valid = _band_mask(q0, k0, qb, kw, half_window)
                lse_tile = jnp.zeros((qb, LANES), F32)
                for g in range(N_KV):
                    qs = _stack_heads(q_ref, s, q0, qb, g)
                    kx = kx_ref[g, pl.ds(k0, kw), :]
                    vx = vx_ref[g, pl.ds(k0, kw), :]
                    sc = lax.dot_general(qs, kx, nt, preferred_element_type=F32) * scale
                    sc = jnp.where(valid, sc, NEG_INF)
                    m = jnp.max(sc, axis=1, keepdims=True)
                    if with_sink:
                        sk = _head_column([sink_ref[4 * g + a] for a in range(4)], qb)
                        m = jnp.maximum(m, sk)
                    p = jnp.exp(sc - m)
                    den = jnp.sum(p, axis=1, keepdims=True)
                    if with_sink:
                        den = den + jnp.exp(sk - m)
                    o = jnp.dot(p.astype(BF16), vx, preferred_element_type=F32) / den
                    for pair in range(2):
                        col = (2 * g + pair) * LANES
                        o_ref[s, pl.ds(q0, qb), col:col + LANES] = _unstack_pair(o, qb, pair).astype(out_dtype)
                    if with_lse:
                        lse = m + jnp.log(den)
                        for a in range(4):
                            lse_tile = lse_tile + jnp.where(lane == 4 * g + a, lse[a * qb:(a + 1) * qb], 0.0)
                if with_lse:
                    lse_ref[s, pl.ds(q0, qb), :] = lse_tile
                return carry

            lax.fori_loop(0, nblk, block, 0)

    in_specs = [pl.BlockSpec((seq_blk, length, N_HEADS * HEAD_DIM), lambda n: (n, 0, 0)),
                pl.BlockSpec((seq_blk, length, N_KV * HEAD_DIM), lambda n: (n, 0, 4)),
                pl.BlockSpec((seq_blk, length, N_KV * HEAD_DIM), lambda n: (n, 0, 5))]
    args = [qkv3, qkv3, qkv3]
    if with_sink:
        in_specs.insert(0, pl.BlockSpec(memory_space=pltpu.SMEM))
        args.insert(0, sink)
    out_specs = [pl.BlockSpec((seq_blk, length, D_MODEL), lambda n: (n, 0, 0))]
    out_shape = [jax.ShapeDtypeStruct((n_seq, length, D_MODEL), out_dtype)]
    if with_lse:
        out_specs.append(pl.BlockSpec((seq_blk, length, LANES), lambda n: (n, 0, 0)))
        out_shape.append(jax.ShapeDtypeStruct((n_seq, length, LANES), F32))
    outs = pl.pallas_call(
        body, name=name, grid=(n_seq // seq_blk,), in_specs=in_specs, out_specs=out_specs, out_shape=out_shape,
        scratch_shapes=[pltpu.VMEM((N_KV, length, LANES), BF16), pltpu.VMEM((N_KV, length, LANES), BF16)],
        compiler_params=_cp(),
    )(*args)
    return [o.reshape(n_seq * length, o.shape[-1]) for o in outs]


def _attn_bwd(qkv, do, adj, sink, cos, sin, n_seq, length, half_window, seq_blk, dil, name):
    qb, kw, nblk = _attn_geometry(length, half_window)
    scale = 1.0 / math.sqrt(HEAD_DIM)
    with_sink = sink is not None
    nt = (((1,), (1,)), ((), ()))
    tn = (((0,), (0,)), ((), ()))
    qkv3 = qkv.reshape(n_seq, length, QKV_W)
    do3 = do.reshape(n_seq, length, D_MODEL)
    aux3 = adj.reshape(n_seq, length, LANES)
    tabs = [t.reshape(dil, length, LANES) for t in (cos, sin)]
    tab_blocks = dil // seq_blk if dil >= seq_blk else 1

    def body(*refs):
        refs = list(refs)
        sink_ref = refs.pop(0) if with_sink else None
        q_ref, k_ref, v_ref, do_ref, aux_ref, cos_ref, sin_ref, dqkv_ref = refs[:8]
        ds_ref = refs[8] if with_sink else None
        kx_ref, vx_ref, dkx_ref, dvx_ref = refs[-4:]
        lane = lax.broadcasted_iota(jnp.int32, (qb, LANES), 1)
        lo = lane < HEAD_DIM
        if with_sink:
            @pl.when(pl.program_id(0) == 0)
            def _():
                ds_ref[...] = jnp.zeros_like(ds_ref)

        for s in range(seq_blk):
            ts = s % dil
            _dup_kv(k_ref, kx_ref, s, length)
            _dup_kv(v_ref, vx_ref, s, length)
            dkx_ref[...] = jnp.zeros_like(dkx_ref)
            dvx_ref[...] = jnp.zeros_like(dvx_ref)

            def block(i, dsink):
                q0 = pl.multiple_of(i * qb, qb)
                k0 = pl.multiple_of(jnp.clip(i * qb - half_window, 0, length - kw), HEAD_DIM)
                valid = _band_mask(q0, k0, qb, kw, half_window)
                cs = cos_ref[ts, pl.ds(q0, qb), :]
                sn = sin_ref[ts, pl.ds(q0, qb), :]
                adj_tile = aux_ref[s, pl.ds(q0, qb), :]
                for g in range(N_KV):
                    qs = _stack_heads(q_ref, s, q0, qb, g)
                    dos = _stack_heads(do_ref, s, q0, qb, g)
                    kx = kx_ref[g, pl.ds(k0, kw), :]
                    vx = vx_ref[g, pl.ds(k0, kw), :]
                    sc = lax.dot_general(qs, kx, nt, preferred_element_type=F32) * scale
                    sc = jnp.where(valid, sc, NEG_INF)
                    m = jnp.max(sc, axis=1, keepdims=True)
                    if with_sink:
                        sk = _head_column([sink_ref[4 * g + a] for a in range(4)], qb)
                        m = jnp.maximum(m, sk)
                    e = jnp.exp(sc - m)
                    den = jnp.sum(e, axis=1, keepdims=True)
                    if with_sink:
                        esk = jnp.exp(sk - m)
                        den = den + esk
                    rden = 1.0 / den
                    p = e * rden
                    shift = jnp.concatenate(
                        [jnp.sum(jnp.where(lane == 4 * g + a, adj_tile, 0.0), axis=1, keepdims=True)
                         for a in range(4)], axis=0)
                    dp = lax.dot_general(dos, vx, nt, preferred_element_type=F32)
                    dsc = p * (dp + shift)
                    if with_sink:
                        dsk = esk * rden * shift
                        for a in range(4):
                            tot = jnp.sum(dsk[a * qb:(a + 1) * qb], axis=0, keepdims=True)
                            dsink = dsink + jnp.where(lane[:1] == 4 * g + a, tot, 0.0)
                    dsb = dsc.astype(BF16)
                    pb = p.astype(BF16)
                    dq = jnp.dot(dsb, kx, preferred_element_type=F32) * scale
                    for pair in range(2):
                        col = (2 * g + pair) * LANES
                        tile = _rope_t(_unstack_pair(dq, qb, pair), cs, sn)
                        dqkv_ref[s, pl.ds(q0, qb), col:col + LANES] = tile.astype(BF16)
                    dkx_ref[g, pl.ds(k0, kw), :] += lax.dot_general(dsb, qs, tn, preferred_element_type=F32) * scale
                    dvx_ref[g, pl.ds(k0, kw), :] += lax.dot_general(pb, dos, tn, preferred_element_type=F32)
                return dsink

            dsink = lax.fori_loop(0, nblk, block, jnp.zeros((1, LANES), F32))
            if with_sink:
                ds_ref[0:1, :] += dsink

            ch = min(length, 256)
            lo_c = lax.broadcasted_iota(jnp.int32, (ch, LANES), 1) < HEAD_DIM

            def fin(c, carry):
                r0 = pl.multiple_of(c * ch, ch)
                cs = cos_ref[ts, pl.ds(r0, ch), :]
                sn = sin_ref[ts, pl.ds(r0, ch), :]
                for j in range(N_KV // 2):
                    both = []
                    for acc_ref in (dkx_ref, dvx_ref):
                        t0 = acc_ref[2 * j, pl.ds(r0, ch), :]
                        t1 = acc_ref[2 * j + 1, pl.ds(r0, ch), :]
                        t0 = t0 + pltpu.roll(t0, HEAD_DIM, 1)
                        t1 = t1 + pltpu.roll(t1, HEAD_DIM, 1)
                        both.append(jnp.where(lo_c, t0, t1))
                    kcol = N_HEADS * HEAD_DIM + j * LANES
                    vcol = (N_HEADS + N_KV) * HEAD_DIM + j * LANES
                    dqkv_ref[s, pl.ds(r0, ch), kcol:kcol + LANES] = _rope_t(both[0], cs, sn).astype(BF16)
                    dqkv_ref[s, pl.ds(r0, ch), vcol:vcol + LANES] = both[1].astype(BF16)
                return carry

            lax.fori_loop(0, length // ch, fin, 0)

    seq_map = lambda n: (n, 0, 0)
    tab_map = (lambda n: (n % tab_blocks, 0, 0)) if dil >= seq_blk else (lambda n: (0, 0, 0))
    tab_rows = min(seq_blk, dil)
    in_specs = [pl.BlockSpec((seq_blk, length, N_HEADS * HEAD_DIM), seq_map),
                pl.BlockSpec((seq_blk, length, N_KV * HEAD_DIM), lambda n: (n, 0, 4)),
                pl.BlockSpec((seq_blk, length, N_KV * HEAD_DIM), lambda n: (n, 0, 5)),
                pl.BlockSpec((seq_blk, length, D_MODEL), seq_map),
                pl.BlockSpec((seq_blk, length, LANES), seq_map),
                pl.BlockSpec((tab_rows, length, LANES), tab_map),
                pl.BlockSpec((tab_rows, length, LANES), tab_map)]
    args = [qkv3, qkv3, qkv3, do3, aux3] + tabs
    if with_sink:
        in_specs.insert(0, pl.BlockSpec(memory_space=pltpu.SMEM))
        args.insert(0, sink)
    out_specs = [pl.BlockSpec((seq_blk, length, QKV_W), seq_map)]
    out_shape = [jax.ShapeDtypeStruct((n_seq, length, QKV_W), BF16)]
    if with_sink:
        out_specs.append(pl.BlockSpec((8, LANES), lambda n: (0, 0)))
        out_shape.append(jax.ShapeDtypeStruct((8, LANES), F32))
    outs = pl.pallas_call(
        body, name=name, grid=(n_seq // seq_blk,), in_specs=in_specs, out_specs=out_specs, out_shape=out_shape,
        scratch_shapes=[pltpu.VMEM((N_KV, length, LANES), BF16), pltpu.VMEM((N_KV, length, LANES), BF16),
                        pltpu.VMEM((N_KV, length, LANES), F32), pltpu.VMEM((N_KV, length, LANES), F32)],
        compiler_params=_cp(),
    )(*args)
    dqkv = outs[0].reshape(n_seq * length, QKV_W)
    return (dqkv, outs[1]) if with_sink else (dqkv, None)


def _head_expander():
    h = jnp.arange(LANES)[:, None]
    l = jnp.arange(D_MODEL)[None, :]
    return (l // HEAD_DIM == h).astype(BF16)


def _dot_split(a, e):
    hi = a.astype(BF16)
    lo = (a - hi.astype(F32)).astype(BF16)
    return jnp.dot(hi, e, preferred_element_type=F32) + jnp.dot(lo, e, preferred_element_type=F32)


def _mix_weights(lses):
    m = jnp.maximum(jnp.maximum(lses[0], lses[1]), lses[2])
    es = [jnp.exp(v - m) for v in lses]
    tot = es[0] + es[1] + es[2]
    return [e / tot for e in es]


def _mix_fwd(os_, lses, name):
    t = os_[0].shape[0]
    tm = _row_tile(t, 512)

    def body(o0, o1, o2, l0, l1, l2, e_ref, out_ref):
        wts = _mix_weights([l0[...], l1[...], l2[...]])
        acc = jnp.zeros((tm, D_MODEL), F32)
        for w, o_ref in zip(wts, (o0, o1, o2)):
            acc = acc + _dot_split(w, e_ref[...]) * o_ref[...]
        out_ref[...] = acc.astype(BF16)

    row = pl.BlockSpec((tm, D_MODEL), lambda i: (i, 0))
    lrow = pl.BlockSpec((tm, LANES), lambda i: (i, 0))
    return pl.pallas_call(
        body, name=name, grid=(t // tm,),
        in_specs=[row] * 3 + [lrow] * 3 + [pl.BlockSpec((LANES, D_MODEL), lambda i: (0, 0))],
        out_specs=row, out_shape=jax.ShapeDtypeStruct((t, D_MODEL), BF16), compiler_params=_cp(),
    )(*os_, *lses, _head_expander())


def _mix_bwd(dmix, os_, lses, name):
    t = dmix.shape[0]
    tm = _row_tile(t, 512)

    def body(d_ref, o0, o1, o2, l0, l1, l2, e_ref, et_ref, do0, do1, do2, a0, a1, a2):
        wts = _mix_weights([l0[...], l1[...], l2[...]])
        dv = d_ref[...].astype(F32)
        cs = [_dot_split(dv * o_ref[...], et_ref[...]) for o_ref in (o0, o1, o2)]
        mean_c = wts[0] * cs[0] + wts[1] * cs[1] + wts[2] * cs[2]
        for w, c, do_ref, a_ref in zip(wts, cs, (do0, do1, do2), (a0, a1, a2)):
            do_ref[...] = (_dot_split(w, e_ref[...]) * dv).astype(BF16)
            a_ref[...] = w * (c - mean_c) - w * c

    row = pl.BlockSpec((tm, D_MODEL), lambda i: (i, 0))
    lrow = pl.BlockSpec((tm, LANES), lambda i: (i, 0))
    e = _head_expander()
    return pl.pallas_call(
        body, name=name, grid=(t // tm,),
        in_specs=[row] * 4 + [lrow] * 3 + [pl.BlockSpec((LANES, D_MODEL), lambda i: (0, 0)),
                                            pl.BlockSpec((D_MODEL, LANES), lambda i: (0, 0))],
        out_specs=[row] * 3 + [lrow] * 3,
        out_shape=[jax.ShapeDtypeStruct((t, D_MODEL), BF16)] * 3 + [jax.ShapeDtypeStruct((t, LANES), F32)] * 3,
        compiler_params=_cp(),
    )(dmix, *os_, *lses, e, e.T)


def _group_geometry(batch, seq, dil, window):
    length = seq // dil
    n_seq = batch * dil
    seq_blk = max(1, min(dil, 1024 // length))
    return n_seq, length, (window // 2) // dil, seq_blk


def _local_step(x, target, a_in, a_sink, a_out, b_in, b_out, norm_mix, norm_ffn, wg, wu, wd, final_norm):
    batch, seq, _ = x.shape
    t = batch * seq
    x0 = x.reshape(t, D_MODEL)
    tgt = target.reshape(t, D_MODEL)
    tabs = {d: _rope_tables(seq, d) for _, d in DILATED}
    nm = [norm_mix[i:i + 1] for i in range(2)]
    nf = [norm_ffn[i:i + 1] for i in range(2)]

    h0 = _rms_fwd(x0, nm[0], "rms_mix0")
    qkv0 = _qkv_proj(h0, a_in, *tabs[1], 0, "qkv0")
    (o0,) = _attn_fwd(qkv0, a_sink, batch, seq, HALF_WINDOW_A, 1, BF16, False, "attn0")
    x1 = _mm_res(o0, a_out, x0, "out0")
    hf0 = _rms_fwd(x1, nf[0], "rms_ffn0")
    g0, u0, act0 = _ffn_up(hf0, wg, wu, 0, "ffn_up0")
    x2 = _ffn_down(act0, wd, x1, 0, "ffn_down0")

    h1 = _rms_fwd(x2, nm[1], "rms_mix1")
    geo = [_group_geometry(batch, seq, d, w) for w, d in DILATED]
    h1g, qkv1, o1, lse1 = [], [], [], []
    for gi, (_, d) in enumerate(DILATED):
        n_seq, length, hw, sb = geo[gi]
        hp = _to_residue(h1, batch, d)
        pj = _qkv_proj(hp, b_in, *tabs[d], gi, f"qkv1_{gi}")
        o, lse = _attn_fwd(pj, None, n_seq, length, hw, sb, F32, True, f"attn1_{gi}")
        h1g.append(hp)
        qkv1.append(pj)
        o1.append(_from_residue(o, batch, d))
        lse1.append(_from_residue(lse, batch, d))
    omix = _mix_fwd(o1, lse1, "mix")
    x3 = _mm_res(omix, b_out, x2, "out1")
    hf1 = _rms_fwd(x3, nf[1], "rms_ffn1")
    g1, u1, act1 = _ffn_up(hf1, wg, wu, 1, "ffn_up1")
    x4 = _ffn_down(act1, wd, x3, 1, "ffn_down1")

    dx4, dx4b, loss_cols, d_final = _final_loss(x4, final_norm.reshape(1, D_MODEL), tgt, "final_loss")

    def ffn_bwd(dxo, dxob, x_mid, hf, g, u, act, layer):
        dg, du = _ffn_down_bwd(dxob, wd, g, u, layer, f"ffn_down_bwd{layer}")
        (d_wd,) = _mm_tn(act, [dxob], f"grad_wd{layer}")
        dh = _ffn_up_bwd(dg, du, wg, wu, layer, f"ffn_up_bwd{layer}")
        d_wg, d_wu = _mm_tn(hf, [dg, du], f"grad_wgu{layer}")
        dxm, dxmb, d_nf = _rms_bwd(x_mid, nf[layer], [dh], dxo, f"rms_ffn_bwd{layer}")
        return dxm, dxmb, d_nf, d_wg, d_wu, d_wd

    dx3, dx3b, d_nf1, d_wg1, d_wu1, d_wd1 = ffn_bwd(dx4, dx4b, x3, hf1, g1, u1, act1, 1)

    dmix = _mm_nt(dx3b, b_out, 0, BF16, "out1_bwd")
    (d_b_out,) = _mm_tn(omix, [dx3b], "grad_b_out")
    mb = _mix_bwd(dmix, o1, lse1, "mix_bwd")
    dh1, d_b_in = [], []
    for gi, (_, d) in enumerate(DILATED):
        n_seq, length, hw, sb = geo[gi]
        dog = _to_residue(mb[gi], batch, d)
        adj = _to_residue(mb[3 + gi], batch, d)
        dpj, _ = _attn_bwd(qkv1[gi], dog, adj, None, *tabs[d], n_seq, length, hw, sb, d, f"attn1_bwd{gi}")
        (dw,) = _mm_tn(h1g[gi], [dpj], f"grad_b_in{gi}")
        d_b_in.append(dw)
        dh1.append(_from_residue(_mm_nt(dpj, b_in, gi, F32, f"qkv1_bwd{gi}"), batch, d))
    dx2, dx2b, d_nm1 = _rms_bwd(x2, nm[1], dh1, dx3, "rms_mix_bwd1")

    dx1, dx1b, d_nf0, d_wg0, d_wu0, d_wd0 = ffn_bwd(dx2, dx2b, x1, hf0, g0, u0, act0, 0)

    do0, adj0 = _out_bwd(dx1b, a_out, o0, "out0_bwd")
    (d_a_out,) = _mm_tn(o0, [dx1b], "grad_a_out")
    dqkv0, d_sink = _attn_bwd(qkv0, do0, adj0, a_sink, *tabs[1], batch, seq, HALF_WINDOW_A, 1, 1, "attn0_bwd")
    (d_a_in,) = _mm_tn(h0, [dqkv0], "grad_a_in")
    dh0 = _mm_nt(dqkv0, a_in, 0, F32, "qkv0_bwd")
    gx, _, d_nm0 = _rms_bwd(x0, nm[0], [dh0], dx1, "rms_mix_bwd0")

    grads = dict(a_in=d_a_in, a_out=d_a_out, b_in=jnp.concatenate(d_b_in, axis=1), b_out=d_b_out,
                 wg=(d_wg0, d_wg1), wu=(d_wu0, d_wu1), wd=(d_wd0, d_wd1))
    vecs = dict(norm_mix=(d_nm0, d_nm1), norm_ffn=(d_nf0, d_nf1), final=d_final, loss_cols=loss_cols, sink=d_sink)
    return gx.reshape(x.shape), grads, vecs


ANY = pl.BlockSpec(memory_space=pl.ANY)


def _me():
    return lax.axis_index("x"), lax.axis_index("y"), lax.axis_index("c")


def _chip_peer(x, y, j):
    px = 1 - x if j & 2 else x
    py = 1 - y if j & 1 else y
    return px, py, 2 * px + py


def _remote(src, dst, sems, k, dev):
    return pltpu.make_async_remote_copy(src_ref=src, dst_ref=dst, send_sem=sems[0].at[k], recv_sem=sems[1].at[k],
                                        device_id=dev, device_id_type=MESH)


def _col_window(ref, q, width):
    return ref.at[:, pl.ds(pl.multiple_of(q * width, LANES), width)]


def _half0(ref, h):
    n = ref.shape[0] // 2
    return ref.at[pl.ds(h * n, n)]


def _half1(ref, h):
    n = ref.shape[1] // 2
    return ref.at[:, pl.ds(h * n, n)]


def _half_rows(ref, h):
    n = ref.shape[-2] // 2
    if len(ref.shape) == 2:
        return ref.at[pl.ds(h * n, n)]
    return ref.at[:, pl.ds(h * n, n)]


def _place_shard(w, q_arr, col, name):
    lead, rows, cols = w.shape

    def body(q_ref, w_ref, o_ref):
        o_ref[...] = w_ref[...].astype(BF16)

    if col:
        assert lead == 1
        out_spec = pl.BlockSpec((rows, cols), lambda l, q: (0, q[0]))
        out_shape = jax.ShapeDtypeStruct((rows, N_CHIPS * cols), BF16)
    else:
        out_spec = pl.BlockSpec((None, None, rows, cols), lambda l, q: (q[0], l, 0, 0))
        out_shape = jax.ShapeDtypeStruct((N_CHIPS, lead, rows, cols), BF16)
    return pl.pallas_call(
        body, name=name,
        grid_spec=pltpu.PrefetchScalarGridSpec(
            num_scalar_prefetch=1, grid=(lead,),
            in_specs=[pl.BlockSpec((None, rows, cols), lambda l, q: (l, 0, 0))], out_specs=out_spec),
        out_shape=out_shape, compiler_params=_cp(),
    )(q_arr, w)


def _gather_weights(bufs):
    col_fam = (True, False, True, False, False, False, False)
    n_w = len(bufs)

    def body(*refs):
        outs = refs[n_w:2 * n_w]
        sems = refs[2 * n_w:2 * n_w + 2]
        x, y, c = _me()
        myq = 2 * x + y
        sib = (x, y, 1 - c)

        def slot(w, q):
            if col_fam[w]:
                return _col_window(outs[w], q, outs[w].shape[1] // N_CHIPS)
            return outs[w].at[q]

        first = []
        for w in range(n_w):
            for j in (1, 2, 3):
                px, py, _ = _chip_peer(x, y, j)
                mine = _half_rows(slot(w, myq), c)
                cp = _remote(mine, mine, sems, w * 6 + j - 1, (px, py, c))
                cp.start()
                first.append(cp)
        passed = []
        for w in range(n_w):
            for j in (1, 2, 3):
                _, _, pq = _chip_peer(x, y, j)
                land = _half_rows(slot(w, pq), c)
                _remote(land, land, sems, w * 6 + j - 1, sib).wait_recv()
                cp = _remote(land, land, sems, w * 6 + 2 + j, sib)
                cp.start()
                passed.append(cp)
        for w in range(n_w):
            for j in (1, 2, 3):
                _, _, pq = _chip_peer(x, y, j)
                land = _half_rows(slot(w, pq), 1 - c)
                _remote(land, land, sems, w * 6 + 2 + j, sib).wait_recv()
        for cp in first + passed:
            cp.wait_send()

    return pl.pallas_call(
        body, name="gather_weights", in_specs=[ANY] * n_w, out_specs=[ANY] * n_w,
        out_shape=[jax.ShapeDtypeStruct(b.shape, b.dtype) for b in bufs],
        input_output_aliases={w: w for w in range(n_w)},
        scratch_shapes=[pltpu.SemaphoreType.DMA((6 * n_w,)), pltpu.SemaphoreType.DMA((6 * n_w,))],
    )(*bufs)


def _grad_half(ref, col, h):
    return _half0(ref, h) if col else _half1(ref, h)


def _swap_halves_with_sibling(grads, col_fam):
    n_w = len(grads)

    def body(*refs):
        ins, outs = refs[:n_w], refs[n_w:2 * n_w]
        sems = refs[2 * n_w:]
        x, y, c = _me()
        sib = (x, y, 1 - c)
        cps = [_remote(_grad_half(ins[w], col_fam[w], 1 - c), outs[w], sems, w, sib) for w in range(n_w)]
        for cp in cps:
            cp.start()
        for cp in cps:
            cp.wait_recv()
        for cp in cps:
            cp.wait_send()

    out_shape = []
    for w, g in enumerate(grads):
        shp = (g.shape[0] // 2, g.shape[1]) if col_fam[w] else (g.shape[0], g.shape[1] // 2, g.shape[2])
        out_shape.append(jax.ShapeDtypeStruct(shp, g.dtype))
    return pl.pallas_call(
        body, name="grad_swap_sibling", in_specs=[ANY] * n_w, out_specs=[ANY] * n_w, out_shape=out_shape,
        scratch_shapes=[pltpu.SemaphoreType.DMA((n_w,)), pltpu.SemaphoreType.DMA((n_w,))],
    )(*grads)


def _half_add(mine, recv, c_arr, col, name):
    if col:
        rows, n = recv.shape
        tr = rows // 2
        grid = (2,)
        in_specs = [pl.BlockSpec((tr, n), lambda i, c: (2 * c[0] + i, 0)), pl.BlockSpec((tr, n), lambda i, c: (i, 0))]
        out_spec = pl.BlockSpec((tr, n), lambda i, c: (i, 0))
    else:
        _, rows, n = recv.shape
        grid = (N_CHIPS,)
        in_specs = [pl.BlockSpec((None, rows, n), lambda q, c: (q, c[0], 0)),
                    pl.BlockSpec((None, rows, n), lambda q, c: (q, 0, 0))]
        out_spec = pl.BlockSpec((None, rows, n), lambda q, c: (q, 0, 0))

    def body(c_ref, a_ref, b_ref, o_ref):
        o_ref[...] = (a_ref[...].astype(F32) + b_ref[...].astype(F32)).astype(BF16)

    return pl.pallas_call(
        body, name=name,
        grid_spec=pltpu.PrefetchScalarGridSpec(num_scalar_prefetch=1, grid=grid, in_specs=in_specs, out_specs=out_spec),
        out_shape=jax.ShapeDtypeStruct(recv.shape, BF16), compiler_params=_cp(),
    )(c_arr, mine, recv)


def _scatter_chip_sums(sums, col_fam):
    n_w = len(sums)

    def body(*refs):
        ins, outs = refs[:n_w], refs[n_w:2 * n_w]
        sems = refs[2 * n_w:2 * n_w + 2]
        lsem = refs[2 * n_w + 2]
        x, y, c = _me()
        myq = 2 * x + y

        def slab(w, q):
            if col_fam[w]:
                return _col_window(ins[w], q, ins[w].shape[1] // N_CHIPS)
            return ins[w].at[q]

        local = [pltpu.make_async_copy(slab(w, myq), outs[w].at[myq], lsem.at[w]) for w in range(n_w)]
        for cp in local:
            cp.start()
        cps = []
        for w in range(n_w):
            for j in (1, 2, 3):
                px, py, pq = _chip_peer(x, y, j)
                cp = _remote(slab(w, pq), outs[w].at[myq], sems, w * 3 + j - 1, (px, py, c))
                cp.start()
                cps.append(cp)
        for w in range(n_w):
            for j in (1, 2, 3):
                _, _, pq = _chip_peer(x, y, j)
                land = outs[w].at[pq]
                _remote(land, land, sems, w * 3 + j - 1, (x, y, c)).wait_recv()
        for cp in cps:
            cp.wait_send()
        for cp in local:
            cp.wait()

    out_shape = []
    for w, s in enumerate(sums):
        shp = (s.shape[0], s.shape[1] // N_CHIPS) if col_fam[w] else s.shape[1:]
        out_shape.append(jax.ShapeDtypeStruct((N_CHIPS,) + shp, s.dtype))
    return pl.pallas_call(
        body, name="grad_scatter_chips", in_specs=[ANY] * n_w, out_specs=[ANY] * n_w, out_shape=out_shape,
        scratch_shapes=[pltpu.SemaphoreType.DMA((3 * n_w,)), pltpu.SemaphoreType.DMA((3 * n_w,)),
                        pltpu.SemaphoreType.DMA((n_w,))],
    )(*sums)


def _sum_chips(parts, c_arr, prev, lead, shape, name):
    _, rows, n = parts.shape
    tr = rows // 2 if rows % 32 == 0 else rows
    nblk = rows // tr

    def body(c_ref, p_ref, *rest):
        o_ref = rest[-1]
        acc = p_ref[0].astype(F32)
        for q in range(1, N_CHIPS):
            acc = acc + p_ref[q].astype(F32)
        o_ref[...] = acc

    in_specs = [pl.BlockSpec((N_CHIPS, tr, n), lambda i, c: (0, i, 0))]
    args = [c_arr, parts]
    aliases = {}
    if prev is not None:
        in_specs.append(ANY)
        args.append(prev)
        aliases = {2: 0}
    return pl.pallas_call(
        body, name=name,
        grid_spec=pltpu.PrefetchScalarGridSpec(
            num_scalar_prefetch=1, grid=(nblk,), in_specs=in_specs,
            out_specs=pl.BlockSpec((None, tr, n), lambda i, c: (lead, c[0] * nblk + i, 0))),
        out_shape=jax.ShapeDtypeStruct(shape, F32), input_output_aliases=aliases, compiler_params=_cp(),
    )(*args)


def _join_halves(bufs, place):
    n_o = len(bufs)
    n_h = len(place)

    def body(*refs):
        outs = refs[n_o:2 * n_o]
        sems = refs[2 * n_o:2 * n_o + 2]
        x, y, c = _me()
        sib = (x, y, 1 - c)

        def half(k, h):
            o, lead = place[k]
            return _half_rows(outs[o].at[lead], h)

        cps = [_remote(half(k, c), half(k, c), sems, k, sib) for k in range(n_h)]
        for cp in cps:
            cp.start()
        for k in range(n_h):
            land = half(k, 1 - c)
            _remote(land, land, sems, k, sib).wait_recv()
        for cp in cps:
            cp.wait_send()

    return pl.pallas_call(
        body, name="grad_join_sibling", in_specs=[ANY] * n_o, out_specs=[ANY] * n_o,
        out_shape=[jax.ShapeDtypeStruct(b.shape, b.dtype) for b in bufs],
        input_output_aliases={k: k for k in range(n_o)},
        scratch_shapes=[pltpu.SemaphoreType.DMA((n_h,)), pltpu.SemaphoreType.DMA((n_h,))],
    )(*bufs)


def _allreduce_rows(rows):
    n_dev = 8
    n_r = len(rows)
    assert n_r <= 8

    def body(*refs):
        r_refs = refs[:n_r]
        o_ref, slots, send_sems, recv_sems = refs[n_r:]
        x, y, c = _me()
        me = 4 * x + 2 * y + c
        slots[me] = jnp.concatenate([r[...] for r in r_refs] + [jnp.zeros((8 - n_r, D_MODEL), F32)], axis=0)

        def peer(k):
            return (1 - x if k & 4 else x, 1 - y if k & 2 else y, 1 - c if k & 1 else c)

        cps = []
        for k in range(1, n_dev):
            cp = pltpu.make_async_remote_copy(src_ref=slots.at[me], dst_ref=slots.at[me], send_sem=send_sems.at[k - 1],
                                              recv_sem=recv_sems.at[k - 1], device_id=peer(k), device_id_type=MESH)
            cp.start()
            cps.append(cp)
        for k in range(1, n_dev):
            px, py, pc = peer(k)
            land = slots.at[4 * px + 2 * py + pc]
            pltpu.make_async_remote_copy(src_ref=land, dst_ref=land, send_sem=send_sems.at[k - 1],
                                         recv_sem=recv_sems.at[k - 1], device_id=peer(k),
                                         device_id_type=MESH).wait_recv()
        for cp in cps:
            cp.wait_send()
        acc = slots[0]
        for d in range(1, n_dev):
            acc = acc + slots[d]
        o_ref[...] = acc

    vm = pl.BlockSpec(memory_space=pltpu.VMEM)
    return pl.pallas_call(
        body, name="allreduce_rows", in_specs=[vm] * n_r, out_specs=vm,
        out_shape=jax.ShapeDtypeStruct((8, D_MODEL), F32),
        scratch_shapes=[pltpu.VMEM((n_dev, 8, D_MODEL), F32), pltpu.SemaphoreType.DMA((n_dev - 1,)),
                        pltpu.SemaphoreType.DMA((n_dev - 1,))],
    )(*rows)


def _adamw(w, g, m, v, name):
    shape = w.shape
    if len(shape) == 1:
        lead, rows, cols = 1, 1, shape[0]
    else:
        rows, cols = shape[-2:]
        lead = math.prod(shape[:-2])
    args = [a.reshape(lead, rows, cols) for a in (w, g, m, v)]
    tr = rows // 2 if rows % 16 == 0 else rows

    def body(w_ref, g_ref, m_ref, v_ref, d_ref, nm_ref, nv_ref):
        gv = g_ref[...]
        nm = ADAM_B1 * m_ref[...] + (1.0 - ADAM_B1) * gv
        nv = ADAM_B2 * v_ref[...] + (1.0 - ADAM_B2) * jnp.square(gv)
        m_hat = nm / (1.0 - ADAM_B1 ** ADAM_STEP)
        v_hat = nv / (1.0 - ADAM_B2 ** ADAM_STEP)
        d_ref[...] = -ADAM_LR * (m_hat / (jnp.sqrt(v_hat) + ADAM_EPS) + ADAM_WD * w_ref[...])
        nm_ref[...] = nm
        nv_ref[...] = nv

    spec = pl.BlockSpec((None, tr, cols), lambda l, i: (l, i, 0))
    outs = pl.pallas_call(
        body, name=name, grid=(lead, rows // tr), in_specs=[spec] * 4, out_specs=[spec] * 3,
        out_shape=[jax.ShapeDtypeStruct((lead, rows, cols), F32)] * 3, compiler_params=_cp(),
    )(*args)
    return [o.reshape(shape) for o in outs]


def kernel(x, a_w_in, a_sink, a_w_out, b_w_in, b_w_out, norm_mix, norm_ffn, w_gate, w_up, w_down, final_norm, loss_target, m_a_w_in, m_a_sink, m_a_w_out, m_b_w_in, m_b_w_out, m_norm_mix, m_norm_ffn, m_w_gate, m_w_up, m_w_down, m_final_norm, v_a_w_in, v_a_sink, v_a_w_out, v_b_w_in, v_b_w_out, v_norm_mix, v_norm_ffn, v_w_gate, v_w_up, v_w_down, v_final_norm):
    weights = dict(a_w_in=a_w_in, a_sink=a_sink, a_w_out=a_w_out, b_w_in=b_w_in, b_w_out=b_w_out, norm_mix=norm_mix,
                   norm_ffn=norm_ffn, w_gate=w_gate, w_up=w_up, w_down=w_down, final_norm=final_norm)
    mom = dict(a_w_in=m_a_w_in, a_sink=m_a_sink, a_w_out=m_a_w_out, b_w_in=m_b_w_in, b_w_out=m_b_w_out,
               norm_mix=m_norm_mix, norm_ffn=m_norm_ffn, w_gate=m_w_gate, w_up=m_w_up, w_down=m_w_down,
               final_norm=m_final_norm)
    var = dict(a_w_in=v_a_w_in, a_sink=v_a_sink, a_w_out=v_a_w_out, b_w_in=v_b_w_in, b_w_out=v_b_w_out,
               norm_mix=v_norm_mix, norm_ffn=v_norm_ffn, w_gate=v_w_gate, w_up=v_w_up, w_down=v_w_down,
               final_norm=v_final_norm)
    order = ["a_w_in", "a_sink", "a_w_out", "b_w_in", "b_w_out", "norm_mix", "norm_ffn", "w_gate", "w_up", "w_down",
             "final_norm"]

    c_arr = lax.axis_index("c").astype(jnp.int32).reshape(1)
    q_arr = (2 * lax.axis_index("x") + lax.axis_index("y")).astype(jnp.int32).reshape(1)
    shards = [a_w_in, a_w_out, b_w_in, b_w_out, w_gate, w_up, w_down]
    shard_names = ("a_in", "a_out", "b_in", "b_out", "wg", "wu", "wd")
    placed = [_place_shard(s, q_arr, col, f"place_{nm}")
              for s, col, nm in zip(shards, (True, False, True, False, False, False, False), shard_names)]
    a_in, a_out, b_in, b_out, wg, wu, wd = _gather_weights(placed)
    a_out = a_out.reshape(D_MODEL, D_MODEL)
    b_out = b_out.reshape(D_MODEL, D_MODEL)

    gx, grads, vecs = _local_step(x, loss_target, a_in, a_sink[0], a_out, b_in, b_out, norm_mix, norm_ffn, wg, wu, wd,
                                  final_norm)

    rows_out = D_MODEL // N_CHIPS
    partials = [grads["a_in"], grads["b_in"],
                grads["a_out"].reshape(N_CHIPS, rows_out, D_MODEL), grads["b_out"].reshape(N_CHIPS, rows_out, D_MODEL),
                grads["wg"][0], grads["wg"][1], grads["wu"][0], grads["wu"][1], grads["wd"][0], grads["wd"][1]]
    col_fam = (True, True) + (False,) * 8
    names = ("a_in", "b_in", "a_out", "b_out", "wg0", "wg1", "wu0", "wu1", "wd0", "wd1")
    theirs = _swap_halves_with_sibling(partials, col_fam)
    sums = [_half_add(p, r, c_arr, cf, f"chip_sum_{nm}") for p, r, cf, nm in zip(partials, theirs, col_fam, names)]
    contrib = _scatter_chip_sums(sums, col_fam)
    shapes = [a_w_in.shape, b_w_in.shape, a_w_out.shape, b_w_out.shape, w_gate.shape, w_up.shape, w_down.shape]
    place = [(0, 0), (1, 0), (2, 0), (3, 0), (4, 0), (4, 1), (5, 0), (5, 1), (6, 0), (6, 1)]
    bufs = [None] * len(shapes)
    for p, nm, (o, lead) in zip(contrib, names, place):
        bufs[o] = _sum_chips(p, c_arr, bufs[o], lead, shapes[o], f"sum_chips_{nm}")
    g_a_in, g_b_in, g_a_out, g_b_out, g_wg, g_wu, g_wd = _join_halves(bufs, place)

    sink_row = jnp.pad(vecs["sink"][0:1], ((0, 0), (0, D_MODEL - LANES)))
    tot = _allreduce_rows([vecs["norm_mix"][0], vecs["norm_mix"][1], vecs["norm_ffn"][0], vecs["norm_ffn"][1],
                           vecs["final"], vecs["loss_cols"], sink_row])
    loss = (0.5 / D_MODEL) * jnp.sum(tot[5])
    gw = dict(a_w_in=g_a_in, a_sink=tot[6:7, :N_HEADS], a_w_out=g_a_out, b_w_in=g_b_in, b_w_out=g_b_out,
              norm_mix=tot[0:2], norm_ffn=tot[2:4], w_gate=g_wg, w_up=g_wu, w_down=g_wd, final_norm=tot[4])

    delta, new_m, new_v = {}, {}, {}
    for n in order:
        delta[n], new_m[n], new_v[n] = _adamw(weights[n], gw[n], mom[n], var[n], f"adamw_{n}")
    return (loss, gx, *[gw[n] for n in order], *[delta[n] for n in order], *[new_m[n] for n in order],
            *[new_v[n] for n in order])
```

```python
import functools
import math

import jax
import jax.numpy as jnp
from jax import lax
from jax.experimental import pallas as pl
from jax.experimental.pallas import tpu as pltpu

F32 = jnp.float32
BF16 = jnp.bfloat16

D_MODEL = 1024
HEAD_DIM = 64
N_HEADS = 16
N_KV = 4
QKV_W = 1536
D_FF = 2816
N_CHIPS = 4
FF_SH = D_FF // N_CHIPS
HALF_WINDOW_A = 128
DILATED = ((128, 1), (512, 4), (2048, 16))
ROPE_THETA = 10000.0
RMS_EPS = 1e-6
NEG_INF = -1e30
LANES = 128
ADAM_LR, ADAM_B1, ADAM_B2, ADAM_EPS, ADAM_WD, ADAM_STEP = 0.001, 0.9, 0.999, 1e-08, 0.01, 10
VMEM_LIMIT = 56 * 1024 * 1024
MESH = pl.DeviceIdType.MESH


def _cp(**kw):
    return pltpu.CompilerParams(vmem_limit_bytes=VMEM_LIMIT, **kw)


def _row_tile(t, cap):
    tm = min(cap, t)
    assert t % tm == 0
    return tm


def _rope_tables(seq, dil):
    inv = 1.0 / (ROPE_THETA ** (jnp.arange(0, HEAD_DIM, 2, dtype=F32) / HEAD_DIM))
    ang = jnp.arange(seq, dtype=F32)[:, None] * inv[None, :]
    cos, sin = jnp.cos(ang), jnp.sin(ang)
    cos = jnp.tile(cos, (1, 4))
    sin = jnp.concatenate([-sin, sin, -sin, sin], axis=1)

    def perm(t):
        return t.reshape(seq // dil, dil, LANES).transpose(1, 0, 2).reshape(seq, LANES)

    return perm(cos), perm(sin)


def _swap_halves(t):
    lane = lax.broadcasted_iota(jnp.int32, t.shape, 1)
    return jnp.where((lane % HEAD_DIM) < HEAD_DIM // 2, pltpu.roll(t, LANES - 32, 1), pltpu.roll(t, 32, 1))


def _rope(t, cos, sin):
    return t * cos + _swap_halves(t) * sin


def _rope_t(t, cos, sin):
    return t * cos - _swap_halves(t) * sin


def _to_residue(t, batch, dil):
    if dil == 1:
        return t
    s = t.shape[0] // batch
    return t.reshape(batch, s // dil, dil, t.shape[1]).transpose(0, 2, 1, 3).reshape(t.shape)


def _from_residue(t, batch, dil):
    if dil == 1:
        return t
    s = t.shape[0] // batch
    return t.reshape(batch, dil, s // dil, t.shape[1]).transpose(0, 2, 1, 3).reshape(t.shape)


def _rms_fwd(x, w, name):
    t = x.shape[0]
    tm = _row_tile(t, 512)

    def body(x_ref, w_ref, o_ref):
        xv = x_ref[...]
        r = lax.rsqrt(jnp.mean(xv * xv, axis=-1, keepdims=True) + RMS_EPS)
        o_ref[...] = ((xv * r) * w_ref[...]).astype(BF16)

    return pl.pallas_call(
        body, name=name, grid=(t // tm,),
        in_specs=[pl.BlockSpec((tm, D_MODEL), lambda i: (i, 0)), pl.BlockSpec((1, D_MODEL), lambda i: (0, 0))],
        out_specs=pl.BlockSpec((tm, D_MODEL), lambda i: (i, 0)),
        out_shape=jax.ShapeDtypeStruct((t, D_MODEL), BF16), compiler_params=_cp(),
    )(x, w)


def _rms_bwd(x, w, dhs, dres, name):
    t = x.shape[0]
    tm = _row_tile(t, 512)
    n = len(dhs)

    def body(*refs):
        x_ref, w_ref = refs[0], refs[1]
        dh_refs = refs[2:2 + n]
        dres_ref = refs[2 + n]
        dx_ref, dxb_ref, dw_ref = refs[3 + n:]
        xv = x_ref[...]
        r = lax.rsqrt(jnp.mean(xv * xv, axis=-1, keepdims=True) + RMS_EPS)
        xh = xv * r
        dy = dh_refs[0][...]
        for k in range(1, n):
            dy = dy + dh_refs[k][...]
        dxh = dy * w_ref[...]
        dx = dres_ref[...] + r * (dxh - xh * jnp.mean(dxh * xh, axis=-1, keepdims=True))
        dx_ref[...] = dx
        dxb_ref[...] = dx.astype(BF16)

        @pl.when(pl.program_id(0) == 0)
        def _():
            dw_ref[...] = jnp.zeros_like(dw_ref)

        dw_ref[...] += jnp.sum(dy * xh, axis=0, keepdims=True)

    row = pl.BlockSpec((tm, D_MODEL), lambda i: (i, 0))
    vec = pl.BlockSpec((1, D_MODEL), lambda i: (0, 0))
    return pl.pallas_call(
        body, name=name, grid=(t // tm,),
        in_specs=[row, vec] + [row] * n + [row],
        out_specs=[row, row, vec],
        out_shape=[jax.ShapeDtypeStruct((t, D_MODEL), F32), jax.ShapeDtypeStruct((t, D_MODEL), BF16),
                   jax.ShapeDtypeStruct((1, D_MODEL), F32)],
        compiler_params=_cp(),
    )(x, w, *dhs, dres)


def _final_loss(x, w, target, name):
    t = x.shape[0]
    tm = _row_tile(t, 512)

    def body(x_ref, w_ref, t_ref, dx_ref, dxb_ref, l_ref, dw_ref):
        xv = x_ref[...]
        r = lax.rsqrt(jnp.mean(xv * xv, axis=-1, keepdims=True) + RMS_EPS)
        xh = xv * r
        err = xh * w_ref[...] - t_ref[...]
        dy = err * (1.0 / D_MODEL)
        dxh = dy * w_ref[...]
        dx = r * (dxh - xh * jnp.mean(dxh * xh, axis=-1, keepdims=True))
        dx_ref[...] = dx
        dxb_ref[...] = dx.astype(BF16)

        @pl.when(pl.program_id(0) == 0)
        def _():
            l_ref[...] = jnp.zeros_like(l_ref)
            dw_ref[...] = jnp.zeros_like(dw_ref)

        l_ref[...] += jnp.sum(err * err, axis=0, keepdims=True)
        dw_ref[...] += jnp.sum(dy * xh, axis=0, keepdims=True)

    row = pl.BlockSpec((tm, D_MODEL), lambda i: (i, 0))
    vec = pl.BlockSpec((1, D_MODEL), lambda i: (0, 0))
    return pl.pallas_call(
        body, name=name, grid=(t // tm,),
        in_specs=[row, vec, row], out_specs=[row, row, vec, vec],
        out_shape=[jax.ShapeDtypeStruct((t, D_MODEL), F32), jax.ShapeDtypeStruct((t, D_MODEL), BF16),
                   jax.ShapeDtypeStruct((1, D_MODEL), F32), jax.ShapeDtypeStruct((1, D_MODEL), F32)],
        compiler_params=_cp(),
    )(x, w, target)


def _qkv_proj(h, w, cos, sin, group, name):
    t = h.shape[0]
    seq = cos.shape[0]
    tm = _row_tile(seq, 1024)
    n_q = N_HEADS * HEAD_DIM // LANES
    n_rope = (N_HEADS + N_KV) * HEAD_DIM // LANES
    scale = 1.0 / math.sqrt(HEAD_DIM)

    def body(h_ref, w_ref, cos_ref, sin_ref, o_ref):
        acc = jnp.dot(h_ref[...], w_ref[...], preferred_element_type=F32)
        cs, sn = cos_ref[...], sin_ref[...]
        csq, snq = cs * scale, sn * scale
        for c in range(QKV_W // LANES):
            blk = acc[:, c * LANES:(c + 1) * LANES]
            if c < n_q:
                blk = _rope(blk, csq, snq)
            elif c < n_rope:
                blk = _rope(blk, cs, sn)
            o_ref[:, c * LANES:(c + 1) * LANES] = blk.astype(BF16)

    tab = pl.BlockSpec((tm, LANES), lambda i: (i % (seq // tm), 0))
    return pl.pallas_call(
        body, name=name, grid=(t // tm,),
        in_specs=[pl.BlockSpec((tm, D_MODEL), lambda i: (i, 0)),
                  pl.BlockSpec((D_MODEL, QKV_W), lambda i: (0, group)), tab, tab],
        out_specs=pl.BlockSpec((tm, QKV_W), lambda i: (i, 0)),
        out_shape=jax.ShapeDtypeStruct((t, QKV_W), BF16), compiler_params=_cp(),
    )(h, w, cos, sin)


def _mm_res(a, w, res, name):
    t, k = a.shape
    tm = _row_tile(t, 1024)

    def body(a_ref, w_ref, r_ref, o_ref):
        o_ref[...] = r_ref[...] + jnp.dot(a_ref[...], w_ref[...], preferred_element_type=F32)

    return pl.pallas_call(
        body, name=name, grid=(t // tm,),
        in_specs=[pl.BlockSpec((tm, k), lambda i: (i, 0)), pl.BlockSpec((k, D_MODEL), lambda i: (0, 0)),
                  pl.BlockSpec((tm, D_MODEL), lambda i: (i, 0))],
        out_specs=pl.BlockSpec((tm, D_MODEL), lambda i: (i, 0)),
        out_shape=jax.ShapeDtypeStruct((t, D_MODEL), F32), compiler_params=_cp(),
    )(a, w, res)


def _mm_nt(dy, w, group, out_dtype, name):
    t, n = dy.shape
    k = w.shape[0]
    tm = _row_tile(t, 1024)

    def body(dy_ref, w_ref, o_ref):
        o_ref[...] = lax.dot_general(dy_ref[...], w_ref[...], (((1,), (1,)), ((), ())),
                                     preferred_element_type=F32).astype(out_dtype)

    return pl.pallas_call(
        body, name=name, grid=(t // tm,),
        in_specs=[pl.BlockSpec((tm, n), lambda i: (i, 0)), pl.BlockSpec((k, n), lambda i: (0, group))],
        out_specs=pl.BlockSpec((tm, k), lambda i: (i, 0)),
        out_shape=jax.ShapeDtypeStruct((t, k), out_dtype), compiler_params=_cp(),
    )(dy, w)


def _out_bwd(dx, w, o, name):
    t = dx.shape[0]
    tm = _row_tile(t, 512)

    def body(dx_ref, w_ref, o_ref, et_ref, do_ref, adj_ref):
        do = lax.dot_general(dx_ref[...], w_ref[...], (((1,), (1,)), ((), ())), preferred_element_type=F32)
        do_ref[...] = do.astype(BF16)
        adj_ref[...] = -_dot_split(do * o_ref[...].astype(F32), et_ref[...])

    row = pl.BlockSpec((tm, D_MODEL), lambda i: (i, 0))
    return pl.pallas_call(
        body, name=name, grid=(t // tm,),
        in_specs=[row, pl.BlockSpec((D_MODEL, D_MODEL), lambda i: (0, 0)), row,
                  pl.BlockSpec((D_MODEL, LANES), lambda i: (0, 0))],
        out_specs=[row, pl.BlockSpec((tm, LANES), lambda i: (i, 0))],
        out_shape=[jax.ShapeDtypeStruct((t, D_MODEL), BF16), jax.ShapeDtypeStruct((t, LANES), F32)],
        compiler_params=_cp(),
    )(dx, w, o, _head_expander().T)


def _mm_tn(a, bs, name):
    aq = a.ndim == 3
    bq = bs[0].ndim == 3
    t, ka = a.shape[-2:]
    n = bs[0].shape[-1]
    nq = N_CHIPS if (aq or bq) else 1
    tt = _row_tile(t, 512)
    tn = n if n <= 1024 else 768
    assert n % tn == 0
    nb = len(bs)
    steps = t // tt

    def body(*refs):
        a_ref = refs[0]
        b_refs = refs[1:1 + nb]
        o_refs = refs[1 + nb:1 + 2 * nb]
        acc_refs = refs[1 + 2 * nb:]
        s = pl.program_id(2)
        av = a_ref[...]
        for b_ref, o_ref, acc_ref in zip(b_refs, o_refs, acc_refs):
            part = lax.dot_general(av, b_ref[...], (((0,), (0,)), ((), ())), preferred_element_type=F32)

            @pl.when(s == 0)
            def _():
                acc_ref[...] = part

            @pl.when(s > 0)
            def _():
                acc_ref[...] += part

            @pl.when(s == steps - 1)
            def _():
                o_ref[...] = acc_ref[...].astype(BF16)

    a_spec = (pl.BlockSpec((None, tt, ka), lambda q, j, s: (q, s, 0)) if aq
              else pl.BlockSpec((tt, ka), lambda q, j, s: (s, 0)))
    b_spec = (pl.BlockSpec((None, tt, tn), lambda q, j, s: (q, s, j)) if bq
              else pl.BlockSpec((tt, tn), lambda q, j, s: (s, j)))
    if nq > 1:
        o_spec = pl.BlockSpec((None, ka, tn), lambda q, j, s: (q, 0, j))
        o_shape = jax.ShapeDtypeStruct((nq, ka, n), BF16)
    else:
        o_spec = pl.BlockSpec((ka, tn), lambda q, j, s: (0, j))
        o_shape = jax.ShapeDtypeStruct((ka, n), BF16)
    outs = pl.pallas_call(
        body, name=name, grid=(nq, n // tn, steps),
        in_specs=[a_spec] + [b_spec] * nb, out_specs=[o_spec] * nb, out_shape=[o_shape] * nb,
        scratch_shapes=[pltpu.VMEM((ka, tn), F32)] * nb, compiler_params=_cp(),
    )(a, *bs)
    return outs


def _sigmoid(x):
    return 1.0 / (1.0 + jnp.exp(-x))


def _ffn_up(h, wg, wu, layer, name):
    t = h.shape[0]
    tm = _row_tile(t, 1024)

    def body(h_ref, wg_ref, wu_ref, g_ref, u_ref, a_ref):
        hv = h_ref[...]
        g = jnp.dot(hv, wg_ref[...], preferred_element_type=F32)
        u = jnp.dot(hv, wu_ref[...], preferred_element_type=F32)
        g_ref[...] = g.astype(BF16)
        u_ref[...] = u.astype(BF16)
        a_ref[...] = (g * _sigmoid(g) * u).astype(BF16)

    wspec = pl.BlockSpec((None, None, D_MODEL, FF_SH), lambda q, i: (q, layer, 0, 0))
    ospec = pl.BlockSpec((None, tm, FF_SH), lambda q, i: (q, i, 0))
    oshape = jax.ShapeDtypeStruct((N_CHIPS, t, FF_SH), BF16)
    return pl.pallas_call(
        body, name=name, grid=(N_CHIPS, t // tm),
        in_specs=[pl.BlockSpec((tm, D_MODEL), lambda q, i: (i, 0)), wspec, wspec],
        out_specs=[ospec] * 3, out_shape=[oshape] * 3, compiler_params=_cp(),
    )(h, wg, wu)


def _ffn_down(a, wd, res, layer, name):
    t = a.shape[1]
    tm = _row_tile(t, 512)

    def body(a_ref, w_ref, r_ref, o_ref):
        acc = r_ref[...]
        for q in range(N_CHIPS):
            acc = acc + jnp.dot(a_ref[q], w_ref[q], preferred_element_type=F32)
        o_ref[...] = acc

    return pl.pallas_call(
        body, name=name, grid=(t // tm,),
        in_specs=[pl.BlockSpec((N_CHIPS, tm, FF_SH), lambda i: (0, i, 0)),
                  pl.BlockSpec((N_CHIPS, None, FF_SH, D_MODEL), lambda i: (0, layer, 0, 0)),
                  pl.BlockSpec((tm, D_MODEL), lambda i: (i, 0))],
        out_specs=pl.BlockSpec((tm, D_MODEL), lambda i: (i, 0)),
        out_shape=jax.ShapeDtypeStruct((t, D_MODEL), F32), compiler_params=_cp(),
    )(a, wd, res)


def _ffn_down_bwd(dx, wd, g, u, layer, name):
    t = dx.shape[0]
    tm = _row_tile(t, 1024)

    def body(dx_ref, w_ref, g_ref, u_ref, dg_ref, du_ref):
        da = lax.dot_general(dx_ref[...], w_ref[...], (((1,), (1,)), ((), ())), preferred_element_type=F32)
        gv = g_ref[...].astype(F32)
        uv = u_ref[...].astype(F32)
        sg = _sigmoid(gv)
        du_ref[...] = (da * (gv * sg)).astype(BF16)
        dg_ref[...] = (da * uv * (sg * (1.0 + gv * (1.0 - sg)))).astype(BF16)

    aspec = pl.BlockSpec((None, tm, FF_SH), lambda q, i: (q, i, 0))
    oshape = jax.ShapeDtypeStruct((N_CHIPS, t, FF_SH), BF16)
    return pl.pallas_call(
        body, name=name, grid=(N_CHIPS, t // tm),
        in_specs=[pl.BlockSpec((tm, D_MODEL), lambda q, i: (i, 0)),
                  pl.BlockSpec((None, None, FF_SH, D_MODEL), lambda q, i: (q, layer, 0, 0)), aspec, aspec],
        out_specs=[aspec] * 2, out_shape=[oshape] * 2, compiler_params=_cp(),
    )(dx, wd, g, u)


def _ffn_up_bwd(dg, du, wg, wu, layer, name):
    t = dg.shape[1]
    tm = _row_tile(t, 512)
    nt = (((1,), (1,)), ((), ()))

    def body(dg_ref, du_ref, wg_ref, wu_ref, o_ref):
        acc = jnp.zeros((tm, D_MODEL), F32)
        for q in range(N_CHIPS):
            acc = acc + lax.dot_general(dg_ref[q], wg_ref[q], nt, preferred_element_type=F32)
            acc = acc + lax.dot_general(du_ref[q], wu_ref[q], nt, preferred_element_type=F32)
        o_ref[...] = acc

    aspec = pl.BlockSpec((N_CHIPS, tm, FF_SH), lambda i: (0, i, 0))
    wspec = pl.BlockSpec((N_CHIPS, None, D_MODEL, FF_SH), lambda i: (0, layer, 0, 0))
    return pl.pallas_call(
        body, name=name, grid=(t // tm,),
        in_specs=[aspec, aspec, wspec, wspec],
        out_specs=pl.BlockSpec((tm, D_MODEL), lambda i: (i, 0)),
        out_shape=jax.ShapeDtypeStruct((t, D_MODEL), F32), compiler_params=_cp(),
    )(dg, du, wg, wu)


def _attn_geometry(length, half_window):
    qb = min(LANES, length)
    kw = min(qb + 2 * half_window, length)
    return qb, kw, length // qb


def _dup_kv(src_ref, dst_ref, s, length):
    ch = min(length, 256)
    lo = lax.broadcasted_iota(jnp.int32, (ch, LANES), 1) < HEAD_DIM

    def chunk(c, carry):
        r0 = pl.multiple_of(c * ch, ch)
        for j in range(N_KV // 2):
            tile = src_ref[s, pl.ds(r0, ch), j * LANES:(j + 1) * LANES].astype(F32)
            rolled = pltpu.roll(tile, HEAD_DIM, 1)
            dst_ref[2 * j, pl.ds(r0, ch), :] = jnp.where(lo, tile, rolled).astype(BF16)
            dst_ref[2 * j + 1, pl.ds(r0, ch), :] = jnp.where(lo, rolled, tile).astype(BF16)
        return carry

    lax.fori_loop(0, length // ch, chunk, 0)


def _stack_heads(ref, s, q0, qb, g):
    lo = lax.broadcasted_iota(jnp.int32, (qb, LANES), 1) < HEAD_DIM
    parts = []
    for a in range(4):
        col = (2 * g + a // 2) * LANES
        tile = ref[s, pl.ds(q0, qb), col:col + LANES]
        keep = lo if a % 2 == 0 else jnp.logical_not(lo)
        parts.append(jnp.where(keep, tile, jnp.zeros_like(tile)))
    return jnp.concatenate(parts, axis=0)


def _unstack_pair_t(stacked_t, qb, pair):
    lo = lax.broadcasted_iota(jnp.int32, (LANES, qb), 0) < HEAD_DIM
    both = jnp.where(lo, stacked_t[:, (2 * pair) * qb:(2 * pair + 1) * qb],
                     stacked_t[:, (2 * pair + 1) * qb:(2 * pair + 2) * qb])
    return both.T


def _band_mask_t(q0, k0, qb, kw, half_window):
    key = lax.broadcasted_iota(jnp.int32, (kw, 4 * qb), 0)
    qry = lax.broadcasted_iota(jnp.int32, (kw, 4 * qb), 1) & (qb - 1)
    return jnp.abs((q0 + qry) - (k0 + key)) <= half_window


def _block_origin(i, qb, kw, half_window, length):
    if isinstance(i, int):
        return i * qb, min(max(i * qb - half_window, 0), length - kw)
    return (pl.multiple_of(i * qb, qb),
            pl.multiple_of(jnp.clip(i * qb - half_window, 0, length - kw), HEAD_DIM))


def _head_row(vals, qb):
    return jnp.concatenate([jnp.broadcast_to(v, (1, qb)).astype(F32) for v in vals], axis=1)


def _attn_fwd(qkv, sink, n_seq, length, half_window, seq_blk, out_dtype, with_lse, name):
    qb, kw, nblk = _attn_geometry(length, half_window)
    with_sink = sink is not None
    nt = (((1,), (1,)), ((), ()))
    tn = (((0,), (0,)), ((), ()))
    qkv3 = qkv.reshape(n_seq, length, QKV_W)

    def body(*refs):
        refs = list(refs)
        sink_ref = refs.pop(0) if with_sink else None
        q_ref, k_ref, v_ref, o_ref = refs[:4]
        lse_ref = refs[4] if with_lse else None
        kx_ref, vx_ref = refs[-2:]
        head_row = lax.broadcasted_iota(jnp.int32, (N_HEADS, qb), 0)
        for s in range(seq_blk):
            _dup_kv(k_ref, kx_ref, s, length)
            _dup_kv(v_ref, vx_ref, s, length)

            def block(i, carry):
                q0, k0 = _block_origin(i, qb, kw, half_window, length)
                valid = _band_mask_t(q0, k0, qb, kw, half_window)
                lse_tile = jnp.zeros((N_HEADS, qb), F32)
                for g in range(N_KV):
                    qs = _stack_heads(q_ref, s, q0, qb, g)
                    kx = kx_ref[g, pl.ds(k0, kw), :]
                    vx = vx_ref[g, pl.ds(k0, kw), :]
                    st = lax.dot_general(kx, qs, nt, preferred_element_type=F32)
                    st = jnp.where(valid, st, NEG_INF)
                    m = jnp.max(st, axis=0, keepdims=True)
                    if with_sink:
                        sk = _head_row([sink_ref[4 * g + a] for a in range(4)], qb)
                        m = jnp.maximum(m, sk)
                    e = jnp.exp(st - m)
                    den = jnp.sum(e, axis=0, keepdims=True)
                    if with_sink:
                        den = den + jnp.exp(sk - m)
                    ot = lax.dot_general(vx, e.astype(BF16), tn, preferred_element_type=F32) / den
                    for pair in range(2):
                        col = (2 * g + pair) * LANES
                        o_ref[s, pl.ds(q0, qb), col:col + LANES] = _unstack_pair_t(ot, qb, pair).astype(out_dtype)
                    if with_lse:
                        lse = m + jnp.log(den)
                        for a in range(4):
                            lse_tile = jnp.where(head_row == 4 * g + a, lse[:, a * qb:(a + 1) * qb], lse_tile)
                if with_lse:
                    lse_ref[s, :, pl.ds(q0, qb)] = lse_tile
                return carry

            if nblk == 1:
                block(0, 0)
            else:
                lax.fori_loop(0, nblk, block, 0)

    in_specs = [pl.BlockSpec((seq_blk, length, N_HEADS * HEAD_DIM), lambda n: (n, 0, 0)),
                pl.BlockSpec((seq_blk, length, N_KV * HEAD_DIM), lambda n: (n, 0, 4)),
                pl.BlockSpec((seq_blk, length, N_KV * HEAD_DIM), lambda n: (n, 0, 5))]
    args = [qkv3, qkv3, qkv3]
    if with_sink:
        in_specs.insert(0, pl.BlockSpec(memory_space=pltpu.SMEM))
        args.insert(0, sink)
    out_specs = [pl.BlockSpec((seq_blk, length, D_MODEL), lambda n: (n, 0, 0))]
    out_shape = [jax.ShapeDtypeStruct((n_seq, length, D_MODEL), out_dtype)]
    if with_lse:
        out_specs.append(pl.BlockSpec((seq_blk, N_HEADS, length), lambda n: (n, 0, 0)))
        out_shape.append(jax.ShapeDtypeStruct((n_seq, N_HEADS, length), F32))
    outs = pl.pallas_call(
        body, name=name, grid=(n_seq // seq_blk,), in_specs=in_specs, out_specs=out_specs, out_shape=out_shape,
        scratch_shapes=[pltpu.VMEM((N_KV, length, LANES), BF16), pltpu.VMEM((N_KV, length, LANES), BF16)],
        compiler_params=_cp(),
    )(*args)
    o = outs[0].reshape(n_seq * length, D_MODEL)
    return (o, outs[1]) if with_lse else (o,)


def _attn_bwd(qkv, do, adj, sink, cos, sin, n_seq, length, half_window, seq_blk, dil, name):
    qb, kw, nblk = _attn_geometry(length, half_window)
    scale = 1.0 / math.sqrt(HEAD_DIM)
    with_sink = sink is not None
    nt = (((1,), (1,)), ((), ()))
    tn = (((0,), (0,)), ((), ()))
    qkv3 = qkv.reshape(n_seq, length, QKV_W)
    do3 = do.reshape(n_seq, length, D_MODEL)
    tabs = [t.reshape(dil, length, LANES) for t in (cos, sin)]
    tab_blocks = dil // seq_blk if dil >= seq_blk else 1

    def body(*refs):
        refs = list(refs)
        sink_ref = refs.pop(0) if with_sink else None
        q_ref, k_ref, v_ref, do_ref, aux_ref, cos_ref, sin_ref, dqkv_ref = refs[:8]
        ds_ref = refs[8] if with_sink else None
        kx_ref, vx_ref, dkx_ref, dvx_ref = refs[-4:]
        lane = lax.broadcasted_iota(jnp.int32, (1, LANES), 1)
        if with_sink:
            @pl.when(pl.program_id(0) == 0)
            def _():
                ds_ref[...] = jnp.zeros_like(ds_ref)

        for s in range(seq_blk):
            ts = s % dil
            _dup_kv(k_ref, kx_ref, s, length)
            _dup_kv(v_ref, vx_ref, s, length)
            dkx_ref[...] = jnp.zeros_like(dkx_ref)
            dvx_ref[...] = jnp.zeros_like(dvx_ref)

            def block(i, dsink):
                q0, k0 = _block_origin(i, qb, kw, half_window, length)
                valid = _band_mask_t(q0, k0, qb, kw, half_window)
                cs = cos_ref[ts, pl.ds(q0, qb), :] * scale
                sn = sin_ref[ts, pl.ds(q0, qb), :] * scale
                adj_tile = aux_ref[s, :, pl.ds(q0, qb)]
                for g in range(N_KV):
                    qs = _stack_heads(q_ref, s, q0, qb, g)
                    dos = _stack_heads(do_ref, s, q0, qb, g)
                    kx = kx_ref[g, pl.ds(k0, kw), :]
                    vx = vx_ref[g, pl.ds(k0, kw), :]
                    st = lax.dot_general(kx, qs, nt, preferred_element_type=F32)
                    st = jnp.where(valid, st, NEG_INF)
                    m = jnp.max(st, axis=0, keepdims=True)
                    if with_sink:
                        sk = _head_row([sink_ref[4 * g + a] for a in range(4)], qb)
                        m = jnp.maximum(m, sk)
                    e = jnp.exp(st - m)
                    den = jnp.sum(e, axis=0, keepdims=True)
                    if with_sink:
                        esk = jnp.exp(sk - m)
                        den = den + esk
                    rden = 1.0 / den
                    pt = e * rden
                    shift = _head_row([adj_tile[4 * g + a:4 * g + a + 1, :] for a in range(4)], qb)
                    dpt = lax.dot_general(vx, dos, nt, preferred_element_type=F32)
                    dst = pt * (dpt + shift)
                    if with_sink:
                        dsk = esk * rden * shift
                        for a in range(4):
                            tot = jnp.sum(dsk[:, a * qb:(a + 1) * qb], axis=1, keepdims=True)
                            dsink = dsink + jnp.where(lane == 4 * g + a, tot, 0.0)
                    dsb = dst.astype(BF16)
                    pb = pt.astype(BF16)
                    dqt = lax.dot_general(kx, dsb, tn, preferred_element_type=F32)
                    for pair in range(2):
                        col = (2 * g + pair) * LANES
                        tile = _rope_t(_unstack_pair_t(dqt, qb, pair), cs, sn)
                        dqkv_ref[s, pl.ds(q0, qb), col:col + LANES] = tile.astype(BF16)
                    dkx_ref[g, pl.ds(k0, kw), :] += jnp.dot(dsb, qs, preferred_element_type=F32)
                    dvx_ref[g, pl.ds(k0, kw), :] += jnp.dot(pb, dos, preferred_element_type=F32)
                return dsink

            if nblk == 1:
                dsink = block(0, jnp.zeros((1, LANES), F32))
            else:
                dsink = lax.fori_loop(0, nblk, block, jnp.zeros((1, LANES), F32))
            if with_sink:
                ds_ref[0:1, :] += dsink

            ch = min(length, 256)
            lo_c = lax.broadcasted_iota(jnp.int32, (ch, LANES), 1) < HEAD_DIM

            def fin(c, carry):
                r0 = pl.multiple_of(c * ch, ch)
                cs = cos_ref[ts, pl.ds(r0, ch), :]
                sn = sin_ref[ts, pl.ds(r0, ch), :]
                for j in range(N_KV // 2):
                    both = []
                    for acc_ref in (dkx_ref, dvx_ref):
                        t0 = acc_ref[2 * j, pl.ds(r0, ch), :]
                        t1 = acc_ref[2 * j + 1, pl.ds(r0, ch), :]
                        t0 = t0 + pltpu.roll(t0, HEAD_DIM, 1)
                        t1 = t1 + pltpu.roll(t1, HEAD_DIM, 1)
                        both.append(jnp.where(lo_c, t0, t1))
                    kcol = N_HEADS * HEAD_DIM + j * LANES
                    vcol = (N_HEADS + N_KV) * HEAD_DIM + j * LANES
                    dqkv_ref[s, pl.ds(r0, ch), kcol:kcol + LANES] = _rope_t(both[0], cs, sn).astype(BF16)
                    dqkv_ref[s, pl.ds(r0, ch), vcol:vcol + LANES] = both[1].astype(BF16)
                return carry

            lax.fori_loop(0, length // ch, fin, 0)

    seq_map = lambda n: (n, 0, 0)
    tab_map = (lambda n: (n % tab_blocks, 0, 0)) if dil >= seq_blk else (lambda n: (0, 0, 0))
    tab_rows = min(seq_blk, dil)
    in_specs = [pl.BlockSpec((seq_blk, length, N_HEADS * HEAD_DIM), seq_map),
                pl.BlockSpec((seq_blk, length, N_KV * HEAD_DIM), lambda n: (n, 0, 4)),
                pl.BlockSpec((seq_blk, length, N_KV * HEAD_DIM), lambda n: (n, 0, 5)),
                pl.BlockSpec((seq_blk, length, D_MODEL), seq_map),
                pl.BlockSpec((seq_blk, N_HEADS, length), seq_map),
                pl.BlockSpec((tab_rows, length, LANES), tab_map),
                pl.BlockSpec((tab_rows, length, LANES), tab_map)]
    args = [qkv3, qkv3, qkv3, do3, adj] + tabs
    if with_sink:
        in_specs.insert(0, pl.BlockSpec(memory_space=pltpu.SMEM))
        args.insert(0, sink)
    out_specs = [pl.BlockSpec((seq_blk, length, QKV_W), seq_map)]
    out_shape = [jax.ShapeDtypeStruct((n_seq, length, QKV_W), BF16)]
    if with_sink:
        out_specs.append(pl.BlockSpec((8, LANES), lambda n: (0, 0)))
        out_shape.append(jax.ShapeDtypeStruct((8, LANES), F32))
    outs = pl.pallas_call(
        body, name=name, grid=(n_seq // seq_blk,), in_specs=in_specs, out_specs=out_specs, out_shape=out_shape,
        scratch_shapes=[pltpu.VMEM((N_KV, length, LANES), BF16), pltpu.VMEM((N_KV, length, LANES), BF16),
                        pltpu.VMEM((N_KV, length, LANES), F32), pltpu.VMEM((N_KV, length, LANES), F32)],
        compiler_params=_cp(),
    )(*args)
    dqkv = outs[0].reshape(n_seq * length, QKV_W)
    return (dqkv, outs[1]) if with_sink else (dqkv, None)


def _head_expander():
    h = jnp.arange(LANES)[:, None]
    l = jnp.arange(D_MODEL)[None, :]
    return (l // HEAD_DIM == h).astype(BF16)


def _dot_split(a, e):
    hi = a.astype(BF16)
    lo = (a - hi.astype(F32)).astype(BF16)
    return jnp.dot(hi, e, preferred_element_type=F32) + jnp.dot(lo, e, preferred_element_type=F32)


def _mix_weights(lses):
    m = jnp.maximum(jnp.maximum(lses[0], lses[1]), lses[2])
    es = [jnp.exp(v - m) for v in lses]
    tot = es[0] + es[1] + es[2]
    return [e / tot for e in es]


def _mix_fwd(os_, lses, name):
    t = os_[0].shape[0]
    tm = _row_tile(t, 512)

    def body(o0, o1, o2, l0, l1, l2, e_ref, out_ref):
        wts = _mix_weights([l0[...], l1[...], l2[...]])
        acc = jnp.zeros((tm, D_MODEL), F32)
        for w, o_ref in zip(wts, (o0, o1, o2)):
            acc = acc + _dot_split(w, e_ref[...]) * o_ref[...]
        out_ref[...] = acc.astype(BF16)

    row = pl.BlockSpec((tm, D_MODEL), lambda i: (i, 0))
    lrow = pl.BlockSpec((tm, LANES), lambda i: (i, 0))
    return pl.pallas_call(
        body, name=name, grid=(t // tm,),
        in_specs=[row] * 3 + [lrow] * 3 + [pl.BlockSpec((LANES, D_MODEL), lambda i: (0, 0))],
        out_specs=row, out_shape=jax.ShapeDtypeStruct((t, D_MODEL), BF16), compiler_params=_cp(),
    )(*os_, *lses, _head_expander())


def _mix_bwd(dmix, os_, lses, name):
    t = dmix.shape[0]
    tm = _row_tile(t, 512)

    def body(d_ref, o0, o1, o2, l0, l1, l2, e_ref, et_ref, do0, do1, do2, a0, a1, a2):
        wts = _mix_weights([l0[...], l1[...], l2[...]])
        dv = d_ref[...].astype(F32)
        cs = [_dot_split(dv * o_ref[...], et_ref[...]) for o_ref in (o0, o1, o2)]
        mean_c = wts[0] * cs[0] + wts[1] * cs[1] + wts[2] * cs[2]
        for w, c, do_ref, a_ref in zip(wts, cs, (do0, do1, do2), (a0, a1, a2)):
            do_ref[...] = (_dot_split(w, e_ref[...]) * dv).astype(BF16)
            a_ref[...] = w * (c - mean_c) - w * c

    row = pl.BlockSpec((tm, D_MODEL), lambda i: (i, 0))
    lrow = pl.BlockSpec((tm, LANES), lambda i: (i, 0))
    e = _head_expander()
    return pl.pallas_call(
        body, name=name, grid=(t // tm,),
        in_specs=[row] * 4 + [lrow] * 3 + [pl.BlockSpec((LANES, D_MODEL), lambda i: (0, 0)),
                                            pl.BlockSpec((D_MODEL, LANES), lambda i: (0, 0))],
        out_specs=[row] * 3 + [lrow] * 3,
        out_shape=[jax.ShapeDtypeStruct((t, D_MODEL), BF16)] * 3 + [jax.ShapeDtypeStruct((t, LANES), F32)] * 3,
        compiler_params=_cp(),
    )(dmix, *os_, *lses, e, e.T)


def _stats_to_tokens(stat, batch, dil):
    n_seq, _, length = stat.shape
    t = stat.transpose(0, 2, 1).reshape(n_seq * length, N_HEADS)
    return _from_residue(jnp.pad(t, ((0, 0), (0, LANES - N_HEADS))), batch, dil)


def _stats_from_tokens(stat, batch, dil, n_seq, length):
    t = _to_residue(stat[:, :N_HEADS], batch, dil)
    return t.reshape(n_seq, length, N_HEADS).transpose(0, 2, 1)


def _group_geometry(batch, seq, dil, window):
    length = seq // dil
    n_seq = batch * dil
    seq_blk = max(1, min(dil, 1024 // length))
    return n_seq, length, (window // 2) // dil, seq_blk


def _local_step(x, target, a_in, a_sink, a_out, b_in, b_out, norm_mix, norm_ffn, wg, wu, wd, final_norm):
    batch, seq, _ = x.shape
    t = batch * seq
    x0 = x.reshape(t, D_MODEL)
    tgt = target.reshape(t, D_MODEL)
    tabs = {d: _rope_tables(seq, d) for _, d in DILATED}
    nm = [norm_mix[i:i + 1] for i in range(2)]
    nf = [norm_ffn[i:i + 1] for i in range(2)]

    h0 = _rms_fwd(x0, nm[0], "rms_mix0")
    qkv0 = _qkv_proj(h0, a_in, *tabs[1], 0, "qkv0")
    (o0,) = _attn_fwd(qkv0, a_sink, batch, seq, HALF_WINDOW_A, 1, BF16, False, "attn0")
    x1 = _mm_res(o0, a_out, x0, "out0")
    hf0 = _rms_fwd(x1, nf[0], "rms_ffn0")
    g0, u0, act0 = _ffn_up(hf0, wg, wu, 0, "ffn_up0")
    x2 = _ffn_down(act0, wd, x1, 0, "ffn_down0")

    h1 = _rms_fwd(x2, nm[1], "rms_mix1")
    geo = [_group_geometry(batch, seq, d, w) for w, d in DILATED]
    h1g, qkv1, o1, lse1 = [], [], [], []
    for gi, (_, d) in enumerate(DILATED):
        n_seq, length, hw, sb = geo[gi]
        hp = _to_residue(h1, batch, d)
        pj = _qkv_proj(hp, b_in, *tabs[d], gi, f"qkv1_{gi}")
        o, lse = _attn_fwd(pj, None, n_seq, length, hw, sb, F32, True, f"attn1_{gi}")
        h1g.append(hp)
        qkv1.append(pj)
        o1.append(_from_residue(o, batch, d))
        lse1.append(_stats_to_tokens(lse, batch, d))
    omix = _mix_fwd(o1, lse1, "mix")
    x3 = _mm_res(omix, b_out, x2, "out1")
    hf1 = _rms_fwd(x3, nf[1], "rms_ffn1")
    g1, u1, act1 = _ffn_up(hf1, wg, wu, 1, "ffn_up1")
    x4 = _ffn_down(act1, wd, x3, 1, "ffn_down1")

    dx4, dx4b, loss_cols, d_final = _final_loss(x4, final_norm.reshape(1, D_MODEL), tgt, "final_loss")

    def ffn_bwd(dxo, dxob, x_mid, hf, g, u, act, layer):
        dg, du = _ffn_down_bwd(dxob, wd, g, u, layer, f"ffn_down_bwd{layer}")
        (d_wd,) = _mm_tn(act, [dxob], f"grad_wd{layer}")
        dh = _ffn_up_bwd(dg, du, wg, wu, layer, f"ffn_up_bwd{layer}")
        d_wg, d_wu = _mm_tn(hf, [dg, du], f"grad_wgu{layer}")
        dxm, dxmb, d_nf = _rms_bwd(x_mid, nf[layer], [dh], dxo, f"rms_ffn_bwd{layer}")
        return dxm, dxmb, d_nf, d_wg, d_wu, d_wd

    dx3, dx3b, d_nf1, d_wg1, d_wu1, d_wd1 = ffn_bwd(dx4, dx4b, x3, hf1, g1, u1, act1, 1)

    dmix = _mm_nt(dx3b, b_out, 0, BF16, "out1_bwd")
    (d_b_out,) = _mm_tn(omix, [dx3b], "grad_b_out")
    mb = _mix_bwd(dmix, o1, lse1, "mix_bwd")
    dh1, d_b_in = [], []
    for gi, (_, d) in enumerate(DILATED):
        n_seq, length, hw, sb = geo[gi]
        dog = _to_residue(mb[gi], batch, d)
        adj = _stats_from_tokens(mb[3 + gi], batch, d, n_seq, length)
        dpj, _ = _attn_bwd(qkv1[gi], dog, adj, None, *tabs[d], n_seq, length, hw, sb, d, f"attn1_bwd{gi}")
        (dw,) = _mm_tn(h1g[gi], [dpj], f"grad_b_in{gi}")
        d_b_in.append(dw)
        dh1.append(_from_residue(_mm_nt(dpj, b_in, gi, F32, f"qkv1_bwd{gi}"), batch, d))
    dx2, dx2b, d_nm1 = _rms_bwd(x2, nm[1], dh1, dx3, "rms_mix_bwd1")

    dx1, dx1b, d_nf0, d_wg0, d_wu0, d_wd0 = ffn_bwd(dx2, dx2b, x1, hf0, g0, u0, act0, 0)

    do0, adj0 = _out_bwd(dx1b, a_out, o0, "out0_bwd")
    (d_a_out,) = _mm_tn(o0, [dx1b], "grad_a_out")
    adj0 = _stats_from_tokens(adj0, batch, 1, batch, seq)
    dqkv0, d_sink = _attn_bwd(qkv0, do0, adj0, a_sink, *tabs[1], batch, seq, HALF_WINDOW_A, 1, 1, "attn0_bwd")
    (d_a_in,) = _mm_tn(h0, [dqkv0], "grad_a_in")
    dh0 = _mm_nt(dqkv0, a_in, 0, F32, "qkv0_bwd")
    gx, _, d_nm0 = _rms_bwd(x0, nm[0], [dh0], dx1, "rms_mix_bwd0")

    grads = dict(a_in=d_a_in, a_out=d_a_out, b_in=jnp.concatenate(d_b_in, axis=1), b_out=d_b_out,
                 wg=(d_wg0, d_wg1), wu=(d_wu0, d_wu1), wd=(d_wd0, d_wd1))
    vecs = dict(norm_mix=(d_nm0, d_nm1), norm_ffn=(d_nf0, d_nf1), final=d_final, loss_cols=loss_cols, sink=d_sink)
    return gx.reshape(x.shape), grads, vecs


ANY = pl.BlockSpec(memory_space=pl.ANY)


def _me():
    return lax.axis_index("x"), lax.axis_index("y"), lax.axis_index("c")


def _chip_peer(x, y, j):
    px = 1 - x if j & 2 else x
    py = 1 - y if j & 1 else y
    return px, py, 2 * px + py


def _remote(src, dst, sems, k, dev):
    return pltpu.make_async_remote_copy(src_ref=src, dst_ref=dst, send_sem=sems[0].at[k], recv_sem=sems[1].at[k],
                                        device_id=dev, device_id_type=MESH)


def _col_window(ref, q, width):
    return ref.at[:, pl.ds(pl.multiple_of(q * width, LANES), width)]


def _half0(ref, h):
    n = ref.shape[0] // 2
    return ref.at[pl.ds(h * n, n)]


def _half1(ref, h):
    n = ref.shape[1] // 2
    return ref.at[:, pl.ds(h * n, n)]


def _half_rows(ref, h):
    n = ref.shape[-2] // 2
    if len(ref.shape) == 2:
        return ref.at[pl.ds(h * n, n)]
    return ref.at[:, pl.ds(h * n, n)]


def _place_shard(w, q_arr, col, name):
    lead, rows, cols = w.shape

    def body(q_ref, w_ref, o_ref):
        o_ref[...] = w_ref[...].astype(BF16)

    if col:
        assert lead == 1
        out_spec = pl.BlockSpec((rows, cols), lambda l, q: (0, q[0]))
        out_shape = jax.ShapeDtypeStruct((rows, N_CHIPS * cols), BF16)
    else:
        out_spec = pl.BlockSpec((None, None, rows, cols), lambda l, q: (q[0], l, 0, 0))
        out_shape = jax.ShapeDtypeStruct((N_CHIPS, lead, rows, cols), BF16)
    return pl.pallas_call(
        body, name=name,
        grid_spec=pltpu.PrefetchScalarGridSpec(
            num_scalar_prefetch=1, grid=(lead,),
            in_specs=[pl.BlockSpec((None, rows, cols), lambda l, q: (l, 0, 0))], out_specs=out_spec),
        out_shape=out_shape, compiler_params=_cp(),
    )(q_arr, w)


def _gather_weights(bufs):
    col_fam = (True, False, True, False, False, False, False)
    n_w = len(bufs)

    def body(*refs):
        outs = refs[n_w:2 * n_w]
        sems = refs[2 * n_w:2 * n_w + 2]
        x, y, c = _me()
        myq = 2 * x + y
        sib = (x, y, 1 - c)

        def slot(w, q):
            if col_fam[w]:
                return _col_window(outs[w], q, outs[w].shape[1] // N_CHIPS)
            return outs[w].at[q]

        first = []
        for w in range(n_w):
            for j in (1, 2, 3):
                px, py, _ = _chip_peer(x, y, j)
                mine = _half_rows(slot(w, myq), c)
                cp = _remote(mine, mine, sems, w * 6 + j - 1, (px, py, c))
                cp.start()
                first.append(cp)
        passed = []
        for w in range(n_w):
            for j in (1, 2, 3):
                _, _, pq = _chip_peer(x, y, j)
                land = _half_rows(slot(w, pq), c)
                _remote(land, land, sems, w * 6 + j - 1, sib).wait_recv()
                cp = _remote(land, land, sems, w * 6 + 2 + j, sib)
                cp.start()
                passed.append(cp)
        for w in range(n_w):
            for j in (1, 2, 3):
                _, _, pq = _chip_peer(x, y, j)
                land = _half_rows(slot(w, pq), 1 - c)
                _remote(land, land, sems, w * 6 + 2 + j, sib).wait_recv()
        for cp in first + passed:
            cp.wait_send()

    return pl.pallas_call(
        body, name="gather_weights", in_specs=[ANY] * n_w, out_specs=[ANY] * n_w,
        out_shape=[jax.ShapeDtypeStruct(b.shape, b.dtype) for b in bufs],
        input_output_aliases={w: w for w in range(n_w)},
        scratch_shapes=[pltpu.SemaphoreType.DMA((6 * n_w,)), pltpu.SemaphoreType.DMA((6 * n_w,))],
    )(*bufs)


def _grad_half(ref, col, h):
    return _half0(ref, h) if col else _half1(ref, h)


def _swap_halves_with_sibling(grads, col_fam):
    n_w = len(grads)

    def body(*refs):
        ins, outs = refs[:n_w], refs[n_w:2 * n_w]
        sems = refs[2 * n_w:]
        x, y, c = _me()
        sib = (x, y, 1 - c)
        cps = [_remote(_grad_half(ins[w], col_fam[w], 1 - c), outs[w], sems, w, sib) for w in range(n_w)]
        for cp in cps:
            cp.start()
        for cp in cps:
            cp.wait_recv()
        for cp in cps:
            cp.wait_send()

    out_shape = []
    for w, g in enumerate(grads):
        shp = (g.shape[0] // 2, g.shape[1]) if col_fam[w] else (g.shape[0], g.shape[1] // 2, g.shape[2])
        out_shape.append(jax.ShapeDtypeStruct(shp, g.dtype))
    return pl.pallas_call(
        body, name="grad_swap_sibling", in_specs=[ANY] * n_w, out_specs=[ANY] * n_w, out_shape=out_shape,
        scratch_shapes=[pltpu.SemaphoreType.DMA((n_w,)), pltpu.SemaphoreType.DMA((n_w,))],
    )(*grads)


def _half_add(mine, recv, c_arr, col, name):
    if col:
        rows, n = recv.shape
        tr = rows // 2
        grid = (2,)
        in_specs = [pl.BlockSpec((tr, n), lambda i, c: (2 * c[0] + i, 0)), pl.BlockSpec((tr, n), lambda i, c: (i, 0))]
        out_spec = pl.BlockSpec((tr, n), lambda i, c: (i, 0))
    else:
        _, rows, n = recv.shape
        grid = (N_CHIPS,)
        in_specs = [pl.BlockSpec((None, rows, n), lambda q, c: (q, c[0], 0)),
                    pl.BlockSpec((None, rows, n), lambda q, c: (q, 0, 0))]
        out_spec = pl.BlockSpec((None, rows, n), lambda q, c: (q, 0, 0))

    def body(c_ref, a_ref, b_ref, o_ref):
        o_ref[...] = (a_ref[...].astype(F32) + b_ref[...].astype(F32)).astype(BF16)

    return pl.pallas_call(
        body, name=name,
        grid_spec=pltpu.PrefetchScalarGridSpec(num_scalar_prefetch=1, grid=grid, in_specs=in_specs, out_specs=out_spec),
        out_shape=jax.ShapeDtypeStruct(recv.shape, BF16), compiler_params=_cp(),
    )(c_arr, mine, recv)


def _scatter_chip_sums(sums, col_fam):
    n_w = len(sums)

    def body(*refs):
        ins, outs = refs[:n_w], refs[n_w:2 * n_w]
        sems = refs[2 * n_w:2 * n_w + 2]
        lsem = refs[2 * n_w + 2]
        x, y, c = _me()
        myq = 2 * x + y

        def slab(w, q):
            if col_fam[w]:
                return _col_window(ins[w], q, ins[w].shape[1] // N_CHIPS)
            return ins[w].at[q]

        local = [pltpu.make_async_copy(slab(w, myq), outs[w].at[myq], lsem.at[w]) for w in range(n_w)]
        for cp in local:
            cp.start()
        cps = []
        for w in range(n_w):
            for j in (1, 2, 3):
                px, py, pq = _chip_peer(x, y, j)
                cp = _remote(slab(w, pq), outs[w].at[myq], sems, w * 3 + j - 1, (px, py, c))
                cp.start()
                cps.append(cp)
        for w in range(n_w):
            for j in (1, 2, 3):
                _, _, pq = _chip_peer(x, y, j)
                land = outs[w].at[pq]
                _remote(land, land, sems, w * 3 + j - 1, (x, y, c)).wait_recv()
        for cp in cps:
            cp.wait_send()
        for cp in local:
            cp.wait()

    out_shape = []
    for w, s in enumerate(sums):
        shp = (s.shape[0], s.shape[1] // N_CHIPS) if col_fam[w] else s.shape[1:]
        out_shape.append(jax.ShapeDtypeStruct((N_CHIPS,) + shp, s.dtype))
    return pl.pallas_call(
        body, name="grad_scatter_chips", in_specs=[ANY] * n_w, out_specs=[ANY] * n_w, out_shape=out_shape,
        scratch_shapes=[pltpu.SemaphoreType.DMA((3 * n_w,)), pltpu.SemaphoreType.DMA((3 * n_w,)),
                        pltpu.SemaphoreType.DMA((n_w,))],
    )(*sums)


def _sum_chips(parts, c_arr, prev, lead, shape, name):
    _, rows, n = parts.shape
    tr = rows // 2 if rows % 32 == 0 else rows
    nblk = rows // tr

    def body(c_ref, p_ref, *rest):
        o_ref = rest[-1]
        acc = p_ref[0].astype(F32)
        for q in range(1, N_CHIPS):
            acc = acc + p_ref[q].astype(F32)
        o_ref[...] = acc

    in_specs = [pl.BlockSpec((N_CHIPS, tr, n), lambda i, c: (0, i, 0))]
    args = [c_arr, parts]
    aliases = {}
    if prev is not None:
        in_specs.append(ANY)
        args.append(prev)
        aliases = {2: 0}
    return pl.pallas_call(
        body, name=name,
        grid_spec=pltpu.PrefetchScalarGridSpec(
            num_scalar_prefetch=1, grid=(nblk,), in_specs=in_specs,
            out_specs=pl.BlockSpec((None, tr, n), lambda i, c: (lead, c[0] * nblk + i, 0))),
        out_shape=jax.ShapeDtypeStruct(shape, F32), input_output_aliases=aliases, compiler_params=_cp(),
    )(*args)


def _join_halves(bufs, place):
    n_o = len(bufs)
    n_h = len(place)

    def body(*refs):
        outs = refs[n_o:2 * n_o]
        sems = refs[2 * n_o:2 * n_o + 2]
        x, y, c = _me()
        sib = (x, y, 1 - c)

        def half(k, h):
            o, lead = place[k]
            return _half_rows(outs[o].at[lead], h)

        cps = [_remote(half(k, c), half(k, c), sems, k, sib) for k in range(n_h)]
        for cp in cps:
            cp.start()
        for k in range(n_h):
            land = half(k, 1 - c)
            _remote(land, land, sems, k, sib).wait_recv()
        for cp in cps:
            cp.wait_send()

    return pl.pallas_call(
        body, name="grad_join_sibling", in_specs=[ANY] * n_o, out_specs=[ANY] * n_o,
        out_shape=[jax.ShapeDtypeStruct(b.shape, b.dtype) for b in bufs],
        input_output_aliases={k: k for k in range(n_o)},
        scratch_shapes=[pltpu.SemaphoreType.DMA((n_h,)), pltpu.SemaphoreType.DMA((n_h,))],
    )(*bufs)


def _allreduce_rows(rows):
    n_dev = 8
    n_r = len(rows)
    assert n_r <= 8

    def body(*refs):
        r_refs = refs[:n_r]
        o_ref, slots, send_sems, recv_sems = refs[n_r:]
        x, y, c = _me()
        me = 4 * x + 2 * y + c
        slots[me] = jnp.concatenate([r[...] for r in r_refs] + [jnp.zeros((8 - n_r, D_MODEL), F32)], axis=0)

        def peer(k):
            return (1 - x if k & 4 else x, 1 - y if k & 2 else y, 1 - c if k & 1 else c)

        cps = []
        for k in range(1, n_dev):
            cp = pltpu.make_async_remote_copy(src_ref=slots.at[me], dst_ref=slots.at[me], send_sem=send_sems.at[k - 1],
                                              recv_sem=recv_sems.at[k - 1], device_id=peer(k), device_id_type=MESH)
            cp.start()
            cps.append(cp)
        for k in range(1, n_dev):
            px, py, pc = peer(k)
            land = slots.at[4 * px + 2 * py + pc]
            pltpu.make_async_remote_copy(src_ref=land, dst_ref=land, send_sem=send_sems.at[k - 1],
                                         recv_sem=recv_sems.at[k - 1], device_id=peer(k),
                                         device_id_type=MESH).wait_recv()
        for cp in cps:
            cp.wait_send()
        acc = slots[0]
        for d in range(1, n_dev):
            acc = acc + slots[d]
        o_ref[...] = acc

    vm = pl.BlockSpec(memory_space=pltpu.VMEM)
    return pl.pallas_call(
        body, name="allreduce_rows", in_specs=[vm] * n_r, out_specs=vm,
        out_shape=jax.ShapeDtypeStruct((8, D_MODEL), F32),
        scratch_shapes=[pltpu.VMEM((n_dev, 8, D_MODEL), F32), pltpu.SemaphoreType.DMA((n_dev - 1,)),
                        pltpu.SemaphoreType.DMA((n_dev - 1,))],
    )(*rows)


def _adamw(w, g, m, v, name):
    shape = w.shape
    if len(shape) == 1:
        lead, rows, cols = 1, 1, shape[0]
    else:
        rows, cols = shape[-2:]
        lead = math.prod(shape[:-2])
    args = [a.reshape(lead, rows, cols) for a in (w, g, m, v)]
    tr = rows // 2 if rows % 16 == 0 else rows

    def body(w_ref, g_ref, m_ref, v_ref, d_ref, nm_ref, nv_ref):
        gv = g_ref[...]
        nm = ADAM_B1 * m_ref[...] + (1.0 - ADAM_B1) * gv
        nv = ADAM_B2 * v_ref[...] + (1.0 - ADAM_B2) * jnp.square(gv)
        m_hat = nm / (1.0 - ADAM_B1 ** ADAM_STEP)
        v_hat = nv / (1.0 - ADAM_B2 ** ADAM_STEP)
        d_ref[...] = -ADAM_LR * (m_hat / (jnp.sqrt(v_hat) + ADAM_EPS) + ADAM_WD * w_ref[...])
        nm_ref[...] = nm
        nv_ref[...] = nv

    spec = pl.BlockSpec((None, tr, cols), lambda l, i: (l, i, 0))
    outs = pl.pallas_call(
        body, name=name, grid=(lead, rows // tr), in_specs=[spec] * 4, out_specs=[spec] * 3,
        out_shape=[jax.ShapeDtypeStruct((lead, rows, cols), F32)] * 3, compiler_params=_cp(),
    )(*args)
    return [o.reshape(shape) for o in outs]


def kernel(x, a_w_in, a_sink, a_w_out, b_w_in, b_w_out, norm_mix, norm_ffn, w_gate, w_up, w_down, final_norm, loss_target, m_a_w_in, m_a_sink, m_a_w_out, m_b_w_in, m_b_w_out, m_norm_mix, m_norm_ffn, m_w_gate, m_w_up, m_w_down, m_final_norm, v_a_w_in, v_a_sink, v_a_w_out, v_b_w_in, v_b_w_out, v_norm_mix, v_norm_ffn, v_w_gate, v_w_up, v_w_down, v_final_norm):
    weights = dict(a_w_in=a_w_in, a_sink=a_sink, a_w_out=a_w_out, b_w_in=b_w_in, b_w_out=b_w_out, norm_mix=norm_mix,
                   norm_ffn=norm_ffn, w_gate=w_gate, w_up=w_up, w_down=w_down, final_norm=final_norm)
    mom = dict(a_w_in=m_a_w_in, a_sink=m_a_sink, a_w_out=m_a_w_out, b_w_in=m_b_w_in, b_w_out=m_b_w_out,
               norm_mix=m_norm_mix, norm_ffn=m_norm_ffn, w_gate=m_w_gate, w_up=m_w_up, w_down=m_w_down,
               final_norm=m_final_norm)
    var = dict(a_w_in=v_a_w_in, a_sink=v_a_sink, a_w_out=v_a_w_out, b_w_in=v_b_w_in, b_w_out=v_b_w_out,
               norm_mix=v_norm_mix, norm_ffn=v_norm_ffn, w_gate=v_w_gate, w_up=v_w_up, w_down=v_w_down,
               final_norm=v_final_norm)
    order = ["a_w_in", "a_sink", "a_w_out", "b_w_in", "b_w_out", "norm_mix", "norm_ffn", "w_gate", "w_up", "w_down",
             "final_norm"]

    c_arr = lax.axis_index("c").astype(jnp.int32).reshape(1)
    q_arr = (2 * lax.axis_index("x") + lax.axis_index("y")).astype(jnp.int32).reshape(1)
    shards = [a_w_in, a_w_out, b_w_in, b_w_out, w_gate, w_up, w_down]
    shard_names = ("a_in", "a_out", "b_in", "b_out", "wg", "wu", "wd")
    placed = [_place_shard(s, q_arr, col, f"place_{nm}")
              for s, col, nm in zip(shards, (True, False, True, False, False, False, False), shard_names)]
    a_in, a_out, b_in, b_out, wg, wu, wd = _gather_weights(placed)
    a_out = a_out.reshape(D_MODEL, D_MODEL)
    b_out = b_out.reshape(D_MODEL, D_MODEL)

    gx, grads, vecs = _local_step(x, loss_target, a_in, a_sink[0], a_out, b_in, b_out, norm_mix, norm_ffn, wg, wu, wd,
                                  final_norm)

    rows_out = D_MODEL // N_CHIPS
    partials = [grads["a_in"], grads["b_in"],
                grads["a_out"].reshape(N_CHIPS, rows_out, D_MODEL), grads["b_out"].reshape(N_CHIPS, rows_out, D_MODEL),
                grads["wg"][0], grads["wg"][1], grads["wu"][0], grads["wu"][1], grads["wd"][0], grads["wd"][1]]
    col_fam = (True, True) + (False,) * 8
    names = ("a_in", "b_in", "a_out", "b_out", "wg0", "wg1", "wu0", "wu1", "wd0", "wd1")
    theirs = _swap_halves_with_sibling(partials, col_fam)
    sums = [_half_add(p, r, c_arr, cf, f"chip_sum_{nm}") for p, r, cf, nm in zip(partials, theirs, col_fam, names)]
    contrib = _scatter_chip_sums(sums, col_fam)
    shapes = [a_w_in.shape, b_w_in.shape, a_w_out.shape, b_w_out.shape, w_gate.shape, w_up.shape, w_down.shape]
    place = [(0, 0), (1, 0), (2, 0), (3, 0), (4, 0), (4, 1), (5, 0), (5, 1), (6, 0), (6, 1)]
    bufs = [None] * len(shapes)
    for p, nm, (o, lead) in zip(contrib, names, place):
        bufs[o] = _sum_chips(p, c_arr, bufs[o], lead, shapes[o], f"sum_chips_{nm}")
    g_a_in, g_b_in, g_a_out, g_b_out, g_wg, g_wu, g_wd = _join_halves(bufs, place)

    sink_row = jnp.pad(vecs["sink"][0:1], ((0, 0), (0, D_MODEL - LANES)))
    tot = _allreduce_rows([vecs["norm_mix"][0], vecs["norm_mix"][1], vecs["norm_ffn"][0], vecs["norm_ffn"][1],
                           vecs["final"], vecs["loss_cols"], sink_row])
    loss = (0.5 / D_MODEL) * jnp.sum(tot[5])
    gw = dict(a_w_in=g_a_in, a_sink=tot[6:7, :N_HEADS], a_w_out=g_a_out, b_w_in=g_b_in, b_w_out=g_b_out,
              norm_mix=tot[0:2], norm_ffn=tot[2:4], w_gate=g_wg, w_up=g_wu, w_down=g_wd, final_norm=tot[4])

    delta, new_m, new_v = {}, {}, {}
    for n in order:
        delta[n], new_m[n], new_v[n] = _adamw(weights[n], gw[n], mom[n], var[n], f"adamw_{n}")
    return (loss, gx, *[gw[n] for n in order], *[delta[n] for n in order], *[new_m[n] for n in order],
            *[new_v[n] for n in order])
```

```python
import functools
import math

import jax
import jax.numpy as jnp
from jax import lax
from jax.experimental import pallas as pl
from jax.experimental.pallas import tpu as pltpu

F32 = jnp.float32
BF16 = jnp.bfloat16

D_MODEL = 1024
HEAD_DIM = 64
N_HEADS = 16
N_KV = 4
QKV_W = 1536
D_FF = 2816
N_CHIPS = 4
FF_SH = D_FF // N_CHIPS
HALF_WINDOW_A = 128
DILATED = ((128, 1), (512, 4), (2048, 16))
ROPE_THETA = 10000.0
RMS_EPS = 1e-6
NEG_INF = -1e30
LANES = 128
ADAM_LR, ADAM_B1, ADAM_B2, ADAM_EPS, ADAM_WD, ADAM_STEP = 0.001, 0.9, 0.999, 1e-08, 0.01, 10
VMEM_LIMIT = 56 * 1024 * 1024
MESH = pl.DeviceIdType.MESH


def _cp(**kw):
    return pltpu.CompilerParams(vmem_limit_bytes=VMEM_LIMIT, **kw)


def _row_tile(t, cap):
    tm = min(cap, t)
    assert t % tm == 0
    return tm


def _rope_tables(seq, dil):
    inv = 1.0 / (ROPE_THETA ** (jnp.arange(0, HEAD_DIM, 2, dtype=F32) / HEAD_DIM))
    ang = jnp.arange(seq, dtype=F32)[:, None] * inv[None, :]
    cos, sin = jnp.cos(ang), jnp.sin(ang)
    cos = jnp.tile(cos, (1, 4))
    sin = jnp.concatenate([-sin, sin, -sin, sin], axis=1)

    def perm(t):
        return t.reshape(seq // dil, dil, LANES).transpose(1, 0, 2).reshape(seq, LANES)

    return perm(cos), perm(sin)


def _swap_halves(t):
    lane = lax.broadcasted_iota(jnp.int32, t.shape, 1)
    return jnp.where((lane % HEAD_DIM) < HEAD_DIM // 2, pltpu.roll(t, LANES - 32, 1), pltpu.roll(t, 32, 1))


def _rope(t, cos, sin):
    return t * cos + _swap_halves(t) * sin


def _rope_t(t, cos, sin):
    return t * cos - _swap_halves(t) * sin


def _to_residue(t, batch, dil):
    if dil == 1:
        return t
    s = t.shape[0] // batch
    return t.reshape(batch, s // dil, dil, t.shape[1]).transpose(0, 2, 1, 3).reshape(t.shape)


def _from_residue(t, batch, dil):
    if dil == 1:
        return t
    s = t.shape[0] // batch
    return t.reshape(batch, dil, s // dil, t.shape[1]).transpose(0, 2, 1, 3).reshape(t.shape)


def _rms_fwd(x, w, name, with_t=False):
    t = x.shape[0]
    tm = _row_tile(t, 512)

    def body(x_ref, w_ref, o_ref, *ot_ref):
        xv = x_ref[...]
        r = lax.rsqrt(jnp.mean(xv * xv, axis=-1, keepdims=True) + RMS_EPS)
        y = (xv * r) * w_ref[...]
        o_ref[...] = y.astype(BF16)
        if with_t:
            ot_ref[0][...] = y.T.astype(BF16)

    out_specs = [pl.BlockSpec((tm, D_MODEL), lambda i: (i, 0))]
    out_shape = [jax.ShapeDtypeStruct((t, D_MODEL), BF16)]
    if with_t:
        out_specs.append(pl.BlockSpec((D_MODEL, tm), lambda i: (0, i)))
        out_shape.append(jax.ShapeDtypeStruct((D_MODEL, t), BF16))
    outs = pl.pallas_call(
        body, name=name, grid=(t // tm,),
        in_specs=[pl.BlockSpec((tm, D_MODEL), lambda i: (i, 0)), pl.BlockSpec((1, D_MODEL), lambda i: (0, 0))],
        out_specs=out_specs, out_shape=out_shape, compiler_params=_cp(),
    )(x, w)
    return outs if with_t else outs[0]


def _rms_bwd(x, w, dhs, dres, name):
    t = x.shape[0]
    tm = _row_tile(t, 512)
    n = len(dhs)

    def body(*refs):
        x_ref, w_ref = refs[0], refs[1]
        dh_refs = refs[2:2 + n]
        dres_ref = refs[2 + n]
        dx_ref, dxb_ref, dxt_ref, dw_ref = refs[3 + n:]
        xv = x_ref[...]
        r = lax.rsqrt(jnp.mean(xv * xv, axis=-1, keepdims=True) + RMS_EPS)
        xh = xv * r
        dy = dh_refs[0][...]
        for k in range(1, n):
            dy = dy + dh_refs[k][...]
        dxh = dy * w_ref[...]
        dx = dres_ref[...] + r * (dxh - xh * jnp.mean(dxh * xh, axis=-1, keepdims=True))
        dx_ref[...] = dx
        dxb_ref[...] = dx.astype(BF16)
        dxt_ref[...] = dx.T.astype(BF16)

        @pl.when(pl.program_id(0) == 0)
        def _():
            dw_ref[...] = jnp.zeros_like(dw_ref)

        dw_ref[...] += jnp.sum(dy * xh, axis=0, keepdims=True)

    row = pl.BlockSpec((tm, D_MODEL), lambda i: (i, 0))
    vec = pl.BlockSpec((1, D_MODEL), lambda i: (0, 0))
    return pl.pallas_call(
        body, name=name, grid=(t // tm,),
        in_specs=[row, vec] + [row] * n + [row],
        out_specs=[row, row, pl.BlockSpec((D_MODEL, tm), lambda i: (0, i)), vec],
        out_shape=[jax.ShapeDtypeStruct((t, D_MODEL), F32), jax.ShapeDtypeStruct((t, D_MODEL), BF16),
                   jax.ShapeDtypeStruct((D_MODEL, t), BF16), jax.ShapeDtypeStruct((1, D_MODEL), F32)],
        compiler_params=_cp(),
    )(x, w, *dhs, dres)


def _final_loss(x, w, target, name):
    t = x.shape[0]
    tm = _row_tile(t, 512)

    def body(x_ref, w_ref, t_ref, dx_ref, dxb_ref, dxt_ref, l_ref, dw_ref):
        xv = x_ref[...]
        r = lax.rsqrt(jnp.mean(xv * xv, axis=-1, keepdims=True) + RMS_EPS)
        xh = xv * r
        err = xh * w_ref[...] - t_ref[...]
        dy = err * (1.0 / D_MODEL)
        dxh = dy * w_ref[...]
        dx = r * (dxh - xh * jnp.mean(dxh * xh, axis=-1, keepdims=True))
        dx_ref[...] = dx
        dxb_ref[...] = dx.astype(BF16)
        dxt_ref[...] = dx.T.astype(BF16)

        @pl.when(pl.program_id(0) == 0)
        def _():
            l_ref[...] = jnp.zeros_like(l_ref)
            dw_ref[...] = jnp.zeros_like(dw_ref)

        l_ref[...] += jnp.sum(err * err, axis=0, keepdims=True)
        dw_ref[...] += jnp.sum(dy * xh, axis=0, keepdims=True)

    row = pl.BlockSpec((tm, D_MODEL), lambda i: (i, 0))
    vec = pl.BlockSpec((1, D_MODEL), lambda i: (0, 0))
    return pl.pallas_call(
        body, name=name, grid=(t // tm,),
        in_specs=[row, vec, row], out_specs=[row, row, pl.BlockSpec((D_MODEL, tm), lambda i: (0, i)), vec, vec],
        out_shape=[jax.ShapeDtypeStruct((t, D_MODEL), F32), jax.ShapeDtypeStruct((t, D_MODEL), BF16),
                   jax.ShapeDtypeStruct((D_MODEL, t), BF16),
                   jax.ShapeDtypeStruct((1, D_MODEL), F32), jax.ShapeDtypeStruct((1, D_MODEL), F32)],
        compiler_params=_cp(),
    )(x, w, target)


def _qkv_proj(h, w, cos, sin, group, name):
    t = h.shape[0]
    seq = cos.shape[0]
    tm = _row_tile(seq, 1024)
    n_q = N_HEADS * HEAD_DIM // LANES
    n_rope = (N_HEADS + N_KV) * HEAD_DIM // LANES
    scale = 1.0 / math.sqrt(HEAD_DIM)

    def body(h_ref, w_ref, cos_ref, sin_ref, o_ref):
        acc = jnp.dot(h_ref[...], w_ref[...], preferred_element_type=F32)
        cs, sn = cos_ref[...], sin_ref[...]
        csq, snq = cs * scale, sn * scale
        for c in range(QKV_W // LANES):
            blk = acc[:, c * LANES:(c + 1) * LANES]
            if c < n_q:
                blk = _rope(blk, csq, snq)
            elif c < n_rope:
                blk = _rope(blk, cs, sn)
            o_ref[:, c * LANES:(c + 1) * LANES] = blk.astype(BF16)

    tab = pl.BlockSpec((tm, LANES), lambda i: (i % (seq // tm), 0))
    return pl.pallas_call(
        body, name=name, grid=(t // tm,),
        in_specs=[pl.BlockSpec((tm, D_MODEL), lambda i: (i, 0)),
                  pl.BlockSpec((D_MODEL, QKV_W), lambda i: (0, group)), tab, tab],
        out_specs=pl.BlockSpec((tm, QKV_W), lambda i: (i, 0)),
        out_shape=jax.ShapeDtypeStruct((t, QKV_W), BF16), compiler_params=_cp(),
    )(h, w, cos, sin)


def _mm_res(a, w, res, name):
    t, k = a.shape
    tm = _row_tile(t, 1024)

    def body(a_ref, w_ref, r_ref, o_ref):
        o_ref[...] = r_ref[...] + jnp.dot(a_ref[...], w_ref[...], preferred_element_type=F32)

    return pl.pallas_call(
        body, name=name, grid=(t // tm,),
        in_specs=[pl.BlockSpec((tm, k), lambda i: (i, 0)), pl.BlockSpec((k, D_MODEL), lambda i: (0, 0)),
                  pl.BlockSpec((tm, D_MODEL), lambda i: (i, 0))],
        out_specs=pl.BlockSpec((tm, D_MODEL), lambda i: (i, 0)),
        out_shape=jax.ShapeDtypeStruct((t, D_MODEL), F32), compiler_params=_cp(),
    )(a, w, res)


def _mm_nt(dy, w, group, out_dtype, name):
    t, n = dy.shape
    k = w.shape[0]
    tm = _row_tile(t, 1024)

    def body(dy_ref, w_ref, o_ref):
        o_ref[...] = lax.dot_general(dy_ref[...], w_ref[...], (((1,), (1,)), ((), ())),
                                     preferred_element_type=F32).astype(out_dtype)

    return pl.pallas_call(
        body, name=name, grid=(t // tm,),
        in_specs=[pl.BlockSpec((tm, n), lambda i: (i, 0)), pl.BlockSpec((k, n), lambda i: (0, group))],
        out_specs=pl.BlockSpec((tm, k), lambda i: (i, 0)),
        out_shape=jax.ShapeDtypeStruct((t, k), out_dtype), compiler_params=_cp(),
    )(dy, w)


def _out_bwd(dx, w, o, name):
    t = dx.shape[0]
    tm = _row_tile(t, 512)

    def body(dx_ref, w_ref, o_ref, et_ref, do_ref, adj_ref):
        do = lax.dot_general(dx_ref[...], w_ref[...], (((1,), (1,)), ((), ())), preferred_element_type=F32)
        do_ref[...] = do.astype(BF16)
        adj_ref[...] = -_dot_split(do * o_ref[...].astype(F32), et_ref[...])

    row = pl.BlockSpec((tm, D_MODEL), lambda i: (i, 0))
    return pl.pallas_call(
        body, name=name, grid=(t // tm,),
        in_specs=[row, pl.BlockSpec((D_MODEL, D_MODEL), lambda i: (0, 0)), row,
                  pl.BlockSpec((D_MODEL, LANES), lambda i: (0, 0))],
        out_specs=[row, pl.BlockSpec((tm, LANES), lambda i: (i, 0))],
        out_shape=[jax.ShapeDtypeStruct((t, D_MODEL), BF16), jax.ShapeDtypeStruct((t, LANES), F32)],
        compiler_params=_cp(),
    )(dx, w, o, _head_expander().T)


def _mm_tn(a, bs, name):
    aq = a.ndim == 3
    bq = bs[0].ndim == 3
    t, ka = a.shape[-2:]
    n = bs[0].shape[-1]
    nq = N_CHIPS if (aq or bq) else 1
    tt = _row_tile(t, 512)
    tn = n if n <= 1024 else 768
    assert n % tn == 0
    nb = len(bs)
    steps = t // tt

    def body(*refs):
        a_ref = refs[0]
        b_refs = refs[1:1 + nb]
        o_refs = refs[1 + nb:1 + 2 * nb]
        acc_refs = refs[1 + 2 * nb:]
        s = pl.program_id(2)
        av = a_ref[...]
        for b_ref, o_ref, acc_ref in zip(b_refs, o_refs, acc_refs):
            part = lax.dot_general(av, b_ref[...], (((0,), (0,)), ((), ())), preferred_element_type=F32)

            @pl.when(s == 0)
            def _():
                acc_ref[...] = part

            @pl.when(s > 0)
            def _():
                acc_ref[...] += part

            @pl.when(s == steps - 1)
            def _():
                o_ref[...] = acc_ref[...].astype(BF16)

    a_spec = (pl.BlockSpec((None, tt, ka), lambda q, j, s: (q, s, 0)) if aq
              else pl.BlockSpec((tt, ka), lambda q, j, s: (s, 0)))
    b_spec = (pl.BlockSpec((None, tt, tn), lambda q, j, s: (q, s, j)) if bq
              else pl.BlockSpec((tt, tn), lambda q, j, s: (s, j)))
    if nq > 1:
        o_spec = pl.BlockSpec((None, ka, tn), lambda q, j, s: (q, 0, j))
        o_shape = jax.ShapeDtypeStruct((nq, ka, n), BF16)
    else:
        o_spec = pl.BlockSpec((ka, tn), lambda q, j, s: (0, j))
        o_shape = jax.ShapeDtypeStruct((ka, n), BF16)
    outs = pl.pallas_call(
        body, name=name, grid=(nq, n // tn, steps),
        in_specs=[a_spec] + [b_spec] * nb, out_specs=[o_spec] * nb, out_shape=[o_shape] * nb,
        scratch_shapes=[pltpu.VMEM((ka, tn), F32)] * nb, compiler_params=_cp(),
    )(a, *bs)
    return outs


def _mm_grad(at, bs, name):
    ka, t = at.shape
    bq = bs[0].ndim == 3
    n = bs[0].shape[-1]
    nq = N_CHIPS if bq else 1
    tt = _row_tile(t, 512)
    tn = n if n <= 1024 else 768
    assert n % tn == 0
    nb = len(bs)
    steps = t // tt

    def body(*refs):
        a_ref = refs[0]
        b_refs = refs[1:1 + nb]
        o_refs = refs[1 + nb:1 + 2 * nb]
        acc_refs = refs[1 + 2 * nb:]
        s = pl.program_id(2)
        av = a_ref[...]
        for b_ref, o_ref, acc_ref in zip(b_refs, o_refs, acc_refs):
            part = jnp.dot(av, b_ref[...], preferred_element_type=F32)

            @pl.when(s == 0)
            def _():
                acc_ref[...] = part

            @pl.when(s > 0)
            def _():
                acc_ref[...] += part

            @pl.when(s == steps - 1)
            def _():
                o_ref[...] = acc_ref[...].astype(BF16)

    a_spec = pl.BlockSpec((ka, tt), lambda q, j, s: (0, s))
    if bq:
        b_spec = pl.BlockSpec((None, tt, tn), lambda q, j, s: (q, s, j))
        o_spec = pl.BlockSpec((None, ka, tn), lambda q, j, s: (q, 0, j))
        o_shape = jax.ShapeDtypeStruct((nq, ka, n), BF16)
    else:
        b_spec = pl.BlockSpec((tt, tn), lambda q, j, s: (s, j))
        o_spec = pl.BlockSpec((ka, tn), lambda q, j, s: (0, j))
        o_shape = jax.ShapeDtypeStruct((ka, n), BF16)
    return pl.pallas_call(
        body, name=name, grid=(nq, n // tn, steps),
        in_specs=[a_spec] + [b_spec] * nb, out_specs=[o_spec] * nb, out_shape=[o_shape] * nb,
        scratch_shapes=[pltpu.VMEM((ka, tn), F32)] * nb, compiler_params=_cp(),
    )(at, *bs)


def _sigmoid(x):
    return 1.0 / (1.0 + jnp.exp(-x))


def _ffn_up(h, wg, wu, layer, name):
    t = h.shape[0]
    tm = _row_tile(t, 1024)

    def body(h_ref, wg_ref, wu_ref, g_ref, u_ref, a_ref):
        hv = h_ref[...]
        g = jnp.dot(hv, wg_ref[...], preferred_element_type=F32)
        u = jnp.dot(hv, wu_ref[...], preferred_element_type=F32)
        g_ref[...] = g.astype(BF16)
        u_ref[...] = u.astype(BF16)
        a_ref[...] = (g * _sigmoid(g) * u).astype(BF16)

    wspec = pl.BlockSpec((None, None, D_MODEL, FF_SH), lambda q, i: (q, layer, 0, 0))
    ospec = pl.BlockSpec((None, tm, FF_SH), lambda q, i: (q, i, 0))
    oshape = jax.ShapeDtypeStruct((N_CHIPS, t, FF_SH), BF16)
    return pl.pallas_call(
        body, name=name, grid=(N_CHIPS, t // tm),
        in_specs=[pl.BlockSpec((tm, D_MODEL), lambda q, i: (i, 0)), wspec, wspec],
        out_specs=[ospec] * 3, out_shape=[oshape] * 3, compiler_params=_cp(),
    )(h, wg, wu)


def _ffn_down(a, wd, res, layer, name):
    t = a.shape[1]
    tm = _row_tile(t, 512)

    def body(a_ref, w_ref, r_ref, o_ref):
        acc = r_ref[...]
        for q in range(N_CHIPS):
            acc = acc + jnp.dot(a_ref[q], w_ref[q], preferred_element_type=F32)
        o_ref[...] = acc

    return pl.pallas_call(
        body, name=name, grid=(t // tm,),
        in_specs=[pl.BlockSpec((N_CHIPS, tm, FF_SH), lambda i: (0, i, 0)),
                  pl.BlockSpec((N_CHIPS, None, FF_SH, D_MODEL), lambda i: (0, layer, 0, 0)),
                  pl.BlockSpec((tm, D_MODEL), lambda i: (i, 0))],
        out_specs=pl.BlockSpec((tm, D_MODEL), lambda i: (i, 0)),
        out_shape=jax.ShapeDtypeStruct((t, D_MODEL), F32), compiler_params=_cp(),
    )(a, wd, res)


def _ffn_down_bwd(dx, wd, g, u, layer, name):
    t = dx.shape[0]
    tm = _row_tile(t, 1024)

    def body(dx_ref, w_ref, g_ref, u_ref, dg_ref, du_ref):
        da = lax.dot_general(dx_ref[...], w_ref[...], (((1,), (1,)), ((), ())), preferred_element_type=F32)
        gv = g_ref[...].astype(F32)
        uv = u_ref[...].astype(F32)
        sg = _sigmoid(gv)
        du_ref[...] = (da * (gv * sg)).astype(BF16)
        dg_ref[...] = (da * uv * (sg * (1.0 + gv * (1.0 - sg)))).astype(BF16)

    aspec = pl.BlockSpec((None, tm, FF_SH), lambda q, i: (q, i, 0))
    oshape = jax.ShapeDtypeStruct((N_CHIPS, t, FF_SH), BF16)
    return pl.pallas_call(
        body, name=name, grid=(N_CHIPS, t // tm),
        in_specs=[pl.BlockSpec((tm, D_MODEL), lambda q, i: (i, 0)),
                  pl.BlockSpec((None, None, FF_SH, D_MODEL), lambda q, i: (q, layer, 0, 0)), aspec, aspec],
        out_specs=[aspec] * 2, out_shape=[oshape] * 2, compiler_params=_cp(),
    )(dx, wd, g, u)


def _ffn_up_bwd(dg, du, wg, wu, layer, name):
    t = dg.shape[1]
    tm = _row_tile(t, 512)
    nt = (((1,), (1,)), ((), ()))

    def body(dg_ref, du_ref, wg_ref, wu_ref, o_ref):
        acc = jnp.zeros((tm, D_MODEL), F32)
        for q in range(N_CHIPS):
            acc = acc + lax.dot_general(dg_ref[q], wg_ref[q], nt, preferred_element_type=F32)
            acc = acc + lax.dot_general(du_ref[q], wu_ref[q], nt, preferred_element_type=F32)
        o_ref[...] = acc

    aspec = pl.BlockSpec((N_CHIPS, tm, FF_SH), lambda i: (0, i, 0))
    wspec = pl.BlockSpec((N_CHIPS, None, D_MODEL, FF_SH), lambda i: (0, layer, 0, 0))
    return pl.pallas_call(
        body, name=name, grid=(t // tm,),
        in_specs=[aspec, aspec, wspec, wspec],
        out_specs=pl.BlockSpec((tm, D_MODEL), lambda i: (i, 0)),
        out_shape=jax.ShapeDtypeStruct((t, D_MODEL), F32), compiler_params=_cp(),
    )(dg, du, wg, wu)


def _attn_geometry(length, half_window):
    qb = min(LANES, length)
    kw = min(qb + 2 * half_window, length)
    return qb, kw, length // qb


def _dup_kv(src_ref, dst_ref, s, length):
    ch = min(length, 256)
    lo = lax.broadcasted_iota(jnp.int32, (ch, LANES), 1) < HEAD_DIM

    def chunk(c, carry):
        r0 = pl.multiple_of(c * ch, ch)
        for j in range(N_KV // 2):
            tile = src_ref[s, pl.ds(r0, ch), j * LANES:(j + 1) * LANES].astype(F32)
            rolled = pltpu.roll(tile, HEAD_DIM, 1)
            dst_ref[2 * j, pl.ds(r0, ch), :] = jnp.where(lo, tile, rolled).astype(BF16)
            dst_ref[2 * j + 1, pl.ds(r0, ch), :] = jnp.where(lo, rolled, tile).astype(BF16)
        return carry

    lax.fori_loop(0, length // ch, chunk, 0)


def _stack_heads(ref, s, q0, qb, g):
    lo = lax.broadcasted_iota(jnp.int32, (qb, LANES), 1) < HEAD_DIM
    parts = []
    for a in range(4):
        col = (2 * g + a // 2) * LANES
        tile = ref[s, pl.ds(q0, qb), col:col + LANES]
        keep = lo if a % 2 == 0 else jnp.logical_not(lo)
        parts.append(jnp.where(keep, tile, jnp.zeros_like(tile)))
    return jnp.concatenate(parts, axis=0)


def _unstack_pair_t(stacked_t, qb, pair):
    lo = lax.broadcasted_iota(jnp.int32, (LANES, qb), 0) < HEAD_DIM
    both = jnp.where(lo, stacked_t[:, (2 * pair) * qb:(2 * pair + 1) * qb],
                     stacked_t[:, (2 * pair + 1) * qb:(2 * pair + 2) * qb])
    return both.T


def _band_mask_t(q0, k0, qb, kw, half_window):
    key = lax.broadcasted_iota(jnp.int32, (kw, 4 * qb), 0)
    qry = lax.broadcasted_iota(jnp.int32, (kw, 4 * qb), 1) & (qb - 1)
    return jnp.abs((q0 + qry) - (k0 + key)) <= half_window


def _block_origin(i, qb, kw, half_window, length):
    if isinstance(i, int):
        return i * qb, min(max(i * qb - half_window, 0), length - kw)
    return (pl.multiple_of(i * qb, qb),
            pl.multiple_of(jnp.clip(i * qb - half_window, 0, length - kw), HEAD_DIM))


def _head_row(vals, qb):
    return jnp.concatenate([jnp.broadcast_to(v, (1, qb)).astype(F32) for v in vals], axis=1)


def _attn_fwd(qkv, sink, n_seq, length, half_window, seq_blk, out_dtype, with_lse, name):
    qb, kw, nblk = _attn_geometry(length, half_window)
    with_sink = sink is not None
    nt = (((1,), (1,)), ((), ()))
    tn = (((0,), (0,)), ((), ()))
    qkv3 = qkv.reshape(n_seq, length, QKV_W)

    def body(*refs):
        refs = list(refs)
        sink_ref = refs.pop(0) if with_sink else None
        q_ref, k_ref, v_ref, o_ref = refs[:4]
        lse_ref = refs[4] if with_lse else None
        kx_ref, vx_ref = refs[-2:]
        head_row = lax.broadcasted_iota(jnp.int32, (N_HEADS, qb), 0)
        for s in range(seq_blk):
            _dup_kv(k_ref, kx_ref, s, length)
            _dup_kv(v_ref, vx_ref, s, length)

            def block(i, carry):
                q0, k0 = _block_origin(i, qb, kw, half_window, length)
                valid = _band_mask_t(q0, k0, qb, kw, half_window)
                lse_tile = jnp.zeros((N_HEADS, qb), F32)
                for g in range(N_KV):
                    qs = _stack_heads(q_ref, s, q0, qb, g)
                    kx = kx_ref[g, pl.ds(k0, kw), :]
                    vx = vx_ref[g, pl.ds(k0, kw), :]
                    st = lax.dot_general(kx, qs, nt, preferred_element_type=F32)
                    st = jnp.where(valid, st, NEG_INF)
                    m = jnp.max(st, axis=0, keepdims=True)
                    if with_sink:
                        sk = _head_row([sink_ref[4 * g + a] for a in range(4)], qb)
                        m = jnp.maximum(m, sk)
                    e = jnp.exp(st - m)
                    den = jnp.sum(e, axis=0, keepdims=True)
                    if with_sink:
                        den = den + jnp.exp(sk - m)
                    ot = lax.dot_general(vx, e.astype(BF16), tn, preferred_element_type=F32) / den
                    for pair in range(2):
                        col = (2 * g + pair) * LANES
                        o_ref[s, pl.ds(q0, qb), col:col + LANES] = _unstack_pair_t(ot, qb, pair).astype(out_dtype)
                    if with_lse:
                        lse = m + jnp.log(den)
                        for a in range(4):
                            lse_tile = jnp.where(head_row == 4 * g + a, lse[:, a * qb:(a + 1) * qb], lse_tile)
                if with_lse:
                    lse_ref[s, :, pl.ds(q0, qb)] = lse_tile
                return carry

            if nblk == 1:
                block(0, 0)
            else:
                lax.fori_loop(0, nblk, block, 0)

    in_specs = [pl.BlockSpec((seq_blk, length, N_HEADS * HEAD_DIM), lambda n: (n, 0, 0)),
                pl.BlockSpec((seq_blk, length, N_KV * HEAD_DIM), lambda n: (n, 0, 4)),
                pl.BlockSpec((seq_blk, length, N_KV * HEAD_DIM), lambda n: (n, 0, 5))]
    args = [qkv3, qkv3, qkv3]
    if with_sink:
        in_specs.insert(0, pl.BlockSpec(memory_space=pltpu.SMEM))
        args.insert(0, sink)
    out_specs = [pl.BlockSpec((seq_blk, length, D_MODEL), lambda n: (n, 0, 0))]
    out_shape = [jax.ShapeDtypeStruct((n_seq, length, D_MODEL), out_dtype)]
    if with_lse:
        out_specs.append(pl.BlockSpec((seq_blk, N_HEADS, length), lambda n: (n, 0, 0)))
        out_shape.append(jax.ShapeDtypeStruct((n_seq, N_HEADS, length), F32))
    outs = pl.pallas_call(
        body, name=name, grid=(n_seq // seq_blk,), in_specs=in_specs, out_specs=out_specs, out_shape=out_shape,
        scratch_shapes=[pltpu.VMEM((N_KV, length, LANES), BF16), pltpu.VMEM((N_KV, length, LANES), BF16)],
        compiler_params=_cp(),
    )(*args)
    o = outs[0].reshape(n_seq * length, D_MODEL)
    return (o, outs[1]) if with_lse else (o,)


def _attn_bwd(qkv, do, adj, sink, cos, sin, n_seq, length, half_window, seq_blk, dil, name):
    qb, kw, nblk = _attn_geometry(length, half_window)
    scale = 1.0 / math.sqrt(HEAD_DIM)
    with_sink = sink is not None
    nt = (((1,), (1,)), ((), ()))
    tn = (((0,), (0,)), ((), ()))
    qkv3 = qkv.reshape(n_seq, length, QKV_W)
    do3 = do.reshape(n_seq, length, D_MODEL)
    tabs = [t.reshape(dil, length, LANES) for t in (cos, sin)]
    tab_blocks = dil // seq_blk if dil >= seq_blk else 1

    def body(*refs):
        refs = list(refs)
        sink_ref = refs.pop(0) if with_sink else None
        q_ref, k_ref, v_ref, do_ref, aux_ref, cos_ref, sin_ref, dqkv_ref = refs[:8]
        ds_ref = refs[8] if with_sink else None
        kx_ref, vx_ref, dkx_ref, dvx_ref = refs[-4:]
        lane = lax.broadcasted_iota(jnp.int32, (1, LANES), 1)
        if with_sink:
            @pl.when(pl.program_id(0) == 0)
            def _():
                ds_ref[...] = jnp.zeros_like(ds_ref)

        for s in range(seq_blk):
            ts = s % dil
            _dup_kv(k_ref, kx_ref, s, length)
            _dup_kv(v_ref, vx_ref, s, length)
            dkx_ref[...] = jnp.zeros_like(dkx_ref)
            dvx_ref[...] = jnp.zeros_like(dvx_ref)

            def block(i, dsink):
                q0, k0 = _block_origin(i, qb, kw, half_window, length)
                valid = _band_mask_t(q0, k0, qb, kw, half_window)
                cs = cos_ref[ts, pl.ds(q0, qb), :] * scale
                sn = sin_ref[ts, pl.ds(q0, qb), :] * scale
                adj_tile = aux_ref[s, :, pl.ds(q0, qb)]
                for g in range(N_KV):
                    qs = _stack_heads(q_ref, s, q0, qb, g)
                    dos = _stack_heads(do_ref, s, q0, qb, g)
                    kx = kx_ref[g, pl.ds(k0, kw), :]
                    vx = vx_ref[g, pl.ds(k0, kw), :]
                    st = lax.dot_general(kx, qs, nt, preferred_element_type=F32)
                    st = jnp.where(valid, st, NEG_INF)
                    m = jnp.max(st, axis=0, keepdims=True)
                    if with_sink:
                        sk = _head_row([sink_ref[4 * g + a] for a in range(4)], qb)
                        m = jnp.maximum(m, sk)
                    e = jnp.exp(st - m)
                    den = jnp.sum(e, axis=0, keepdims=True)
                    if with_sink:
                        esk = jnp.exp(sk - m)
                        den = den + esk
                    rden = 1.0 / den
                    pt = e * rden
                    shift = _head_row([adj_tile[4 * g + a:4 * g + a + 1, :] for a in range(4)], qb)
                    dpt = lax.dot_general(vx, dos, nt, preferred_element_type=F32)
                    dst = pt * (dpt + shift)
                    if with_sink:
                        dsk = esk * rden * shift
                        for a in range(4):
                            tot = jnp.sum(dsk[:, a * qb:(a + 1) * qb], axis=1, keepdims=True)
                            dsink = dsink + jnp.where(lane == 4 * g + a, tot, 0.0)
                    dsb = dst.astype(BF16)
                    pb = pt.astype(BF16)
                    dqt = lax.dot_general(kx, dsb, tn, preferred_element_type=F32)
                    for pair in range(2):
                        col = (2 * g + pair) * LANES
                        tile = _rope_t(_unstack_pair_t(dqt, qb, pair), cs, sn)
                        dqkv_ref[s, pl.ds(q0, qb), col:col + LANES] = tile.astype(BF16)
                    dkx_ref[g, pl.ds(k0, kw), :] += jnp.dot(dsb, qs, preferred_element_type=F32)
                    dvx_ref[g, pl.ds(k0, kw), :] += jnp.dot(pb, dos, preferred_element_type=F32)
                return dsink

            if nblk == 1:
                dsink = block(0, jnp.zeros((1, LANES), F32))
            else:
                dsink = lax.fori_loop(0, nblk, block, jnp.zeros((1, LANES), F32))
            if with_sink:
                ds_ref[0:1, :] += dsink

            ch = min(length, 256)
            lo_c = lax.broadcasted_iota(jnp.int32, (ch, LANES), 1) < HEAD_DIM

            def fin(c, carry):
                r0 = pl.multiple_of(c * ch, ch)
                cs = cos_ref[ts, pl.ds(r0, ch), :]
                sn = sin_ref[ts, pl.ds(r0, ch), :]
                for j in range(N_KV // 2):
                    both = []
                    for acc_ref in (dkx_ref, dvx_ref):
                        t0 = acc_ref[2 * j, pl.ds(r0, ch), :]
                        t1 = acc_ref[2 * j + 1, pl.ds(r0, ch), :]
                        t0 = t0 + pltpu.roll(t0, HEAD_DIM, 1)
                        t1 = t1 + pltpu.roll(t1, HEAD_DIM, 1)
                        both.append(jnp.where(lo_c, t0, t1))
                    kcol = N_HEADS * HEAD_DIM + j * LANES
                    vcol = (N_HEADS + N_KV) * HEAD_DIM + j * LANES
                    dqkv_ref[s, pl.ds(r0, ch), kcol:kcol + LANES] = _rope_t(both[0], cs, sn).astype(BF16)
                    dqkv_ref[s, pl.ds(r0, ch), vcol:vcol + LANES] = both[1].astype(BF16)
                return carry

            lax.fori_loop(0, length // ch, fin, 0)

    seq_map = lambda n: (n, 0, 0)
    tab_map = (lambda n: (n % tab_blocks, 0, 0)) if dil >= seq_blk else (lambda n: (0, 0, 0))
    tab_rows = min(seq_blk, dil)
    in_specs = [pl.BlockSpec((seq_blk, length, N_HEADS * HEAD_DIM), seq_map),
                pl.BlockSpec((seq_blk, length, N_KV * HEAD_DIM), lambda n: (n, 0, 4)),
                pl.BlockSpec((seq_blk, length, N_KV * HEAD_DIM), lambda n: (n, 0, 5)),
                pl.BlockSpec((seq_blk, length, D_MODEL), seq_map),
                pl.BlockSpec((seq_blk, N_HEADS, length), seq_map),
                pl.BlockSpec((tab_rows, length, LANES), tab_map),
                pl.BlockSpec((tab_rows, length, LANES), tab_map)]
    args = [qkv3, qkv3, qkv3, do3, adj] + tabs
    if with_sink:
        in_specs.insert(0, pl.BlockSpec(memory_space=pltpu.SMEM))
        args.insert(0, sink)
    out_specs = [pl.BlockSpec((seq_blk, length, QKV_W), seq_map)]
    out_shape = [jax.ShapeDtypeStruct((n_seq, length, QKV_W), BF16)]
    if with_sink:
        out_specs.append(pl.BlockSpec((8, LANES), lambda n: (0, 0)))
        out_shape.append(jax.ShapeDtypeStruct((8, LANES), F32))
    outs = pl.pallas_call(
        body, name=name, grid=(n_seq // seq_blk,), in_specs=in_specs, out_specs=out_specs, out_shape=out_shape,
        scratch_shapes=[pltpu.VMEM((N_KV, length, LANES), BF16), pltpu.VMEM((N_KV, length, LANES), BF16),
                        pltpu.VMEM((N_KV, length, LANES), F32), pltpu.VMEM((N_KV, length, LANES), F32)],
        compiler_params=_cp(),
    )(*args)
    dqkv = outs[0].reshape(n_seq * length, QKV_W)
    return (dqkv, outs[1]) if with_sink else (dqkv, None)


def _head_expander():
    h = jnp.arange(LANES)[:, None]
    l = jnp.arange(D_MODEL)[None, :]
    return (l // HEAD_DIM == h).astype(BF16)


def _dot_split(a, e):
    hi = a.astype(BF16)
    lo = (a - hi.astype(F32)).astype(BF16)
    return jnp.dot(hi, e, preferred_element_type=F32) + jnp.dot(lo, e, preferred_element_type=F32)


def _mix_weights(lses):
    m = jnp.maximum(jnp.maximum(lses[0], lses[1]), lses[2])
    es = [jnp.exp(v - m) for v in lses]
    tot = es[0] + es[1] + es[2]
    return [e / tot for e in es]


def _mix_fwd(os_, lses, name):
    t = os_[0].shape[0]
    tm = _row_tile(t, 512)

    def body(o0, o1, o2, l0, l1, l2, e_ref, out_ref):
        wts = _mix_weights([l0[...], l1[...], l2[...]])
        acc = jnp.zeros((tm, D_MODEL), F32)
        for w, o_ref in zip(wts, (o0, o1, o2)):
            acc = acc + _dot_split(w, e_ref[...]) * o_ref[...]
        out_ref[...] = acc.astype(BF16)

    row = pl.BlockSpec((tm, D_MODEL), lambda i: (i, 0))
    lrow = pl.BlockSpec((tm, LANES), lambda i: (i, 0))
    return pl.pallas_call(
        body, name=name, grid=(t // tm,),
        in_specs=[row] * 3 + [lrow] * 3 + [pl.BlockSpec((LANES, D_MODEL), lambda i: (0, 0))],
        out_specs=row, out_shape=jax.ShapeDtypeStruct((t, D_MODEL), BF16), compiler_params=_cp(),
    )(*os_, *lses, _head_expander())


def _mix_bwd(dmix, os_, lses, name):
    t = dmix.shape[0]
    tm = _row_tile(t, 512)

    def body(d_ref, o0, o1, o2, l0, l1, l2, e_ref, et_ref, do0, do1, do2, a0, a1, a2):
        wts = _mix_weights([l0[...], l1[...], l2[...]])
        dv = d_ref[...].astype(F32)
        cs = [_dot_split(dv * o_ref[...], et_ref[...]) for o_ref in (o0, o1, o2)]
        mean_c = wts[0] * cs[0] + wts[1] * cs[1] + wts[2] * cs[2]
        for w, c, do_ref, a_ref in zip(wts, cs, (do0, do1, do2), (a0, a1, a2)):
            do_ref[...] = (_dot_split(w, e_ref[...]) * dv).astype(BF16)
            a_ref[...] = w * (c - mean_c) - w * c

    row = pl.BlockSpec((tm, D_MODEL), lambda i: (i, 0))
    lrow = pl.BlockSpec((tm, LANES), lambda i: (i, 0))
    e = _head_expander()
    return pl.pallas_call(
        body, name=name, grid=(t // tm,),
        in_specs=[row] * 4 + [lrow] * 3 + [pl.BlockSpec((LANES, D_MODEL), lambda i: (0, 0)),
                                            pl.BlockSpec((D_MODEL, LANES), lambda i: (0, 0))],
        out_specs=[row] * 3 + [lrow] * 3,
        out_shape=[jax.ShapeDtypeStruct((t, D_MODEL), BF16)] * 3 + [jax.ShapeDtypeStruct((t, LANES), F32)] * 3,
        compiler_params=_cp(),
    )(dmix, *os_, *lses, e, e.T)


def _stats_to_tokens(stat, batch, dil):
    n_seq, _, length = stat.shape
    t = stat.transpose(0, 2, 1).reshape(n_seq * length, N_HEADS)
    return _from_residue(jnp.pad(t, ((0, 0), (0, LANES - N_HEADS))), batch, dil)


def _stats_from_tokens(stat, batch, dil, n_seq, length):
    t = _to_residue(stat[:, :N_HEADS], batch, dil)
    return t.reshape(n_seq, length, N_HEADS).transpose(0, 2, 1)


def _group_geometry(batch, seq, dil, window):
    length = seq // dil
    n_seq = batch * dil
    seq_blk = max(1, min(dil, 1024 // length))
    return n_seq, length, (window // 2) // dil, seq_blk


def _local_step(x, target, a_in, a_sink, a_out, b_in, b_out, norm_mix, norm_ffn, wg, wu, wd, final_norm):
    batch, seq, _ = x.shape
    t = batch * seq
    x0 = x.reshape(t, D_MODEL)
    tgt = target.reshape(t, D_MODEL)
    tabs = {d: _rope_tables(seq, d) for _, d in DILATED}
    nm = [norm_mix[i:i + 1] for i in range(2)]
    nf = [norm_ffn[i:i + 1] for i in range(2)]

    h0, h0t = _rms_fwd(x0, nm[0], "rms_mix0", True)
    qkv0 = _qkv_proj(h0, a_in, *tabs[1], 0, "qkv0")
    (o0,) = _attn_fwd(qkv0, a_sink, batch, seq, HALF_WINDOW_A, 1, BF16, False, "attn0")
    x1 = _mm_res(o0, a_out, x0, "out0")
    hf0, hf0t = _rms_fwd(x1, nf[0], "rms_ffn0", True)
    g0, u0, act0 = _ffn_up(hf0, wg, wu, 0, "ffn_up0")
    x2 = _ffn_down(act0, wd, x1, 0, "ffn_down0")

    h1 = _rms_fwd(x2, nm[1], "rms_mix1")
    geo = [_group_geometry(batch, seq, d, w) for w, d in DILATED]
    h1g, qkv1, o1, lse1 = [], [], [], []
    for gi, (_, d) in enumerate(DILATED):
        n_seq, length, hw, sb = geo[gi]
        hp = _to_residue(h1, batch, d)
        pj = _qkv_proj(hp, b_in, *tabs[d], gi, f"qkv1_{gi}")
        o, lse = _attn_fwd(pj, None, n_seq, length, hw, sb, F32, True, f"attn1_{gi}")
        h1g.append(hp)
        qkv1.append(pj)
        o1.append(_from_residue(o, batch, d))
        lse1.append(_stats_to_tokens(lse, batch, d))
    omix = _mix_fwd(o1, lse1, "mix")
    x3 = _mm_res(omix, b_out, x2, "out1")
    hf1, hf1t = _rms_fwd(x3, nf[1], "rms_ffn1", True)
    g1, u1, act1 = _ffn_up(hf1, wg, wu, 1, "ffn_up1")
    x4 = _ffn_down(act1, wd, x3, 1, "ffn_down1")

    dx4, dx4b, dx4t, loss_cols, d_final = _final_loss(x4, final_norm.reshape(1, D_MODEL), tgt, "final_loss")

    def ffn_bwd(dxo, dxob, dxot, x_mid, hft, g, u, act, layer):
        dg, du = _ffn_down_bwd(dxob, wd, g, u, layer, f"ffn_down_bwd{layer}")
        (d_wdt,) = _mm_grad(dxot, [act], f"grad_wd{layer}")
        dh = _ffn_up_bwd(dg, du, wg, wu, layer, f"ffn_up_bwd{layer}")
        d_wg, d_wu = _mm_grad(hft, [dg, du], f"grad_wgu{layer}")
        dxm, dxmb, dxmt, d_nf = _rms_bwd(x_mid, nf[layer], [dh], dxo, f"rms_ffn_bwd{layer}")
        return dxm, dxmb, dxmt, d_nf, d_wg, d_wu, d_wdt

    dx3, dx3b, _, d_nf1, d_wg1, d_wu1, d_wd1 = ffn_bwd(dx4, dx4b, dx4t, x3, hf1t, g1, u1, act1, 1)

    dmix = _mm_nt(dx3b, b_out, 0, BF16, "out1_bwd")
    (d_b_out,) = _mm_tn(omix, [dx3b], "grad_b_out")
    mb = _mix_bwd(dmix, o1, lse1, "mix_bwd")
    dh1, d_b_in = [], []
    for gi, (_, d) in enumerate(DILATED):
        n_seq, length, hw, sb = geo[gi]
        dog = _to_residue(mb[gi], batch, d)
        adj = _stats_from_tokens(mb[3 + gi], batch, d, n_seq, length)
        dpj, _ = _attn_bwd(qkv1[gi], dog, adj, None, *tabs[d], n_seq, length, hw, sb, d, f"attn1_bwd{gi}")
        (dw,) = _mm_tn(h1g[gi], [dpj], f"grad_b_in{gi}")
        d_b_in.append(dw)
        dh1.append(_from_residue(_mm_nt(dpj, b_in, gi, F32, f"qkv1_bwd{gi}"), batch, d))
    dx2, dx2b, dx2t, d_nm1 = _rms_bwd(x2, nm[1], dh1, dx3, "rms_mix_bwd1")

    dx1, dx1b, _, d_nf0, d_wg0, d_wu0, d_wd0 = ffn_bwd(dx2, dx2b, dx2t, x1, hf0t, g0, u0, act0, 0)

    do0, adj0 = _out_bwd(dx1b, a_out, o0, "out0_bwd")
    (d_a_out,) = _mm_tn(o0, [dx1b], "grad_a_out")
    adj0 = _stats_from_tokens(adj0, batch, 1, batch, seq)
    dqkv0, d_sink = _attn_bwd(qkv0, do0, adj0, a_sink, *tabs[1], batch, seq, HALF_WINDOW_A, 1, 1, "attn0_bwd")
    (d_a_in,) = _mm_grad(h0t, [dqkv0], "grad_a_in")
    dh0 = _mm_nt(dqkv0, a_in, 0, F32, "qkv0_bwd")
    gx, _, _, d_nm0 = _rms_bwd(x0, nm[0], [dh0], dx1, "rms_mix_bwd0")

    grads = dict(a_in=d_a_in, a_out=d_a_out, b_in=jnp.concatenate(d_b_in, axis=1), b_out=d_b_out,
                 wg=(d_wg0, d_wg1), wu=(d_wu0, d_wu1), wd=(d_wd0, d_wd1))
    vecs = dict(norm_mix=(d_nm0, d_nm1), norm_ffn=(d_nf0, d_nf1), final=d_final, loss_cols=loss_cols, sink=d_sink)
    return gx.reshape(x.shape), grads, vecs


ANY = pl.BlockSpec(memory_space=pl.ANY)


def _me():
    return lax.axis_index("x"), lax.axis_index("y"), lax.axis_index("c")


def _chip_peer(x, y, j):
    px = 1 - x if j & 2 else x
    py = 1 - y if j & 1 else y
    return px, py, 2 * px + py


def _remote(src, dst, sems, k, dev):
    return pltpu.make_async_remote_copy(src_ref=src, dst_ref=dst, send_sem=sems[0].at[k], recv_sem=sems[1].at[k],
                                        device_id=dev, device_id_type=MESH)


def _col_window(ref, q, width):
    return ref.at[:, pl.ds(pl.multiple_of(q * width, LANES), width)]


def _half0(ref, h):
    n = ref.shape[0] // 2
    return ref.at[pl.ds(h * n, n)]


def _half1(ref, h):
    n = ref.shape[1] // 2
    return ref.at[:, pl.ds(h * n, n)]


def _half_rows(ref, h):
    n = ref.shape[-2] // 2
    if len(ref.shape) == 2:
        return ref.at[pl.ds(h * n, n)]
    return ref.at[:, pl.ds(h * n, n)]


def _place_shard(w, q_arr, col, name):
    lead, rows, cols = w.shape

    def body(q_ref, w_ref, o_ref):
        o_ref[...] = w_ref[...].astype(BF16)

    if col:
        assert lead == 1
        out_spec = pl.BlockSpec((rows, cols), lambda l, q: (0, q[0]))
        out_shape = jax.ShapeDtypeStruct((rows, N_CHIPS * cols), BF16)
    else:
        out_spec = pl.BlockSpec((None, None, rows, cols), lambda l, q: (q[0], l, 0, 0))
        out_shape = jax.ShapeDtypeStruct((N_CHIPS, lead, rows, cols), BF16)
    return pl.pallas_call(
        body, name=name,
        grid_spec=pltpu.PrefetchScalarGridSpec(
            num_scalar_prefetch=1, grid=(lead,),
            in_specs=[pl.BlockSpec((None, rows, cols), lambda l, q: (l, 0, 0))], out_specs=out_spec),
        out_shape=out_shape, compiler_params=_cp(),
    )(q_arr, w)


def _gather_weights(bufs):
    col_fam = (True, False, True, False, False, False, False)
    n_w = len(bufs)

    def body(*refs):
        outs = refs[n_w:2 * n_w]
        sems = refs[2 * n_w:2 * n_w + 2]
        x, y, c = _me()
        myq = 2 * x + y
        sib = (x, y, 1 - c)

        def slot(w, q):
            if col_fam[w]:
                return _col_window(outs[w], q, outs[w].shape[1] // N_CHIPS)
            return outs[w].at[q]

        first = []
        for w in range(n_w):
            for j in (1, 2, 3):
                px, py, _ = _chip_peer(x, y, j)
                mine = _half_rows(slot(w, myq), c)
                cp = _remote(mine, mine, sems, w * 6 + j - 1, (px, py, c))
                cp.start()
                first.append(cp)
        passed = []
        for w in range(n_w):
            for j in (1, 2, 3):
                _, _, pq = _chip_peer(x, y, j)
                land = _half_rows(slot(w, pq), c)
                _remote(land, land, sems, w * 6 + j - 1, sib).wait_recv()
                cp = _remote(land, land, sems, w * 6 + 2 + j, sib)
                cp.start()
                passed.append(cp)
        for w in range(n_w):
            for j in (1, 2, 3):
                _, _, pq = _chip_peer(x, y, j)
                land = _half_rows(slot(w, pq), 1 - c)
                _remote(land, land, sems, w * 6 + 2 + j, sib).wait_recv()
        for cp in first + passed:
            cp.wait_send()

    return pl.pallas_call(
        body, name="gather_weights", in_specs=[ANY] * n_w, out_specs=[ANY] * n_w,
        out_shape=[jax.ShapeDtypeStruct(b.shape, b.dtype) for b in bufs],
        input_output_aliases={w: w for w in range(n_w)},
        scratch_shapes=[pltpu.SemaphoreType.DMA((6 * n_w,)), pltpu.SemaphoreType.DMA((6 * n_w,))],
    )(*bufs)


def _grad_half(ref, col, h):
    return _half0(ref, h) if col else _half1(ref, h)


def _swap_halves_with_sibling(grads, col_fam):
    n_w = len(grads)

    def body(*refs):
        ins, outs = refs[:n_w], refs[n_w:2 * n_w]
        sems = refs[2 * n_w:]
        x, y, c = _me()
        sib = (x, y, 1 - c)
        cps = [_remote(_grad_half(ins[w], col_fam[w], 1 - c), outs[w], sems, w, sib) for w in range(n_w)]
        for cp in cps:
            cp.start()
        for cp in cps:
            cp.wait_recv()
        for cp in cps:
            cp.wait_send()

    out_shape = []
    for w, g in enumerate(grads):
        shp = (g.shape[0] // 2, g.shape[1]) if col_fam[w] else (g.shape[0], g.shape[1] // 2, g.shape[2])
        out_shape.append(jax.ShapeDtypeStruct(shp, g.dtype))
    return pl.pallas_call(
        body, name="grad_swap_sibling", in_specs=[ANY] * n_w, out_specs=[ANY] * n_w, out_shape=out_shape,
        scratch_shapes=[pltpu.SemaphoreType.DMA((n_w,)), pltpu.SemaphoreType.DMA((n_w,))],
    )(*grads)


def _half_add(mine, recv, c_arr, col, name):
    if col:
        rows, n = recv.shape
        tr = rows // 2
        grid = (2,)
        in_specs = [pl.BlockSpec((tr, n), lambda i, c: (2 * c[0] + i, 0)), pl.BlockSpec((tr, n), lambda i, c: (i, 0))]
        out_spec = pl.BlockSpec((tr, n), lambda i, c: (i, 0))
    else:
        _, rows, n = recv.shape
        grid = (N_CHIPS,)
        in_specs = [pl.BlockSpec((None, rows, n), lambda q, c: (q, c[0], 0)),
                    pl.BlockSpec((None, rows, n), lambda q, c: (q, 0, 0))]
        out_spec = pl.BlockSpec((None, rows, n), lambda q, c: (q, 0, 0))

    def body(c_ref, a_ref, b_ref, o_ref):
        o_ref[...] = (a_ref[...].astype(F32) + b_ref[...].astype(F32)).astype(BF16)

    return pl.pallas_call(
        body, name=name,
        grid_spec=pltpu.PrefetchScalarGridSpec(num_scalar_prefetch=1, grid=grid, in_specs=in_specs, out_specs=out_spec),
        out_shape=jax.ShapeDtypeStruct(recv.shape, BF16), compiler_params=_cp(),
    )(c_arr, mine, recv)


def _scatter_chip_sums(sums, col_fam):
    n_w = len(sums)

    def body(*refs):
        ins, outs = refs[:n_w], refs[n_w:2 * n_w]
        sems = refs[2 * n_w:2 * n_w + 2]
        lsem = refs[2 * n_w + 2]
        x, y, c = _me()
        myq = 2 * x + y

        def slab(w, q):
            if col_fam[w]:
                return _col_window(ins[w], q, ins[w].shape[1] // N_CHIPS)
            return ins[w].at[q]

        local = [pltpu.make_async_copy(slab(w, myq), outs[w].at[myq], lsem.at[w]) for w in range(n_w)]
        for cp in local:
            cp.start()
        cps = []
        for w in range(n_w):
            for j in (1, 2, 3):
                px, py, pq = _chip_peer(x, y, j)
                cp = _remote(slab(w, pq), outs[w].at[myq], sems, w * 3 + j - 1, (px, py, c))
                cp.start()
                cps.append(cp)
        for w in range(n_w):
            for j in (1, 2, 3):
                _, _, pq = _chip_peer(x, y, j)
                land = outs[w].at[pq]
                _remote(land, land, sems, w * 3 + j - 1, (x, y, c)).wait_recv()
        for cp in cps:
            cp.wait_send()
        for cp in local:
            cp.wait()

    out_shape = []
    for w, s in enumerate(sums):
        shp = (s.shape[0], s.shape[1] // N_CHIPS) if col_fam[w] else s.shape[1:]
        out_shape.append(jax.ShapeDtypeStruct((N_CHIPS,) + shp, s.dtype))
    return pl.pallas_call(
        body, name="grad_scatter_chips", in_specs=[ANY] * n_w, out_specs=[ANY] * n_w, out_shape=out_shape,
        scratch_shapes=[pltpu.SemaphoreType.DMA((3 * n_w,)), pltpu.SemaphoreType.DMA((3 * n_w,)),
                        pltpu.SemaphoreType.DMA((n_w,))],
    )(*sums)


def _sum_chips(parts, c_arr, prev, lead, shape, name):
    _, rows, n = parts.shape
    tr = rows // 2 if rows % 32 == 0 else rows
    nblk = rows // tr

    def body(c_ref, p_ref, *rest):
        o_ref = rest[-1]
        acc = p_ref[0].astype(F32)
        for q in range(1, N_CHIPS):
            acc = acc + p_ref[q].astype(F32)
        o_ref[...] = acc

    in_specs = [pl.BlockSpec((N_CHIPS, tr, n), lambda i, c: (0, i, 0))]
    args = [c_arr, parts]
    aliases = {}
    if prev is not None:
        in_specs.append(ANY)
        args.append(prev)
        aliases = {2: 0}
    return pl.pallas_call(
        body, name=name,
        grid_spec=pltpu.PrefetchScalarGridSpec(
            num_scalar_prefetch=1, grid=(nblk,), in_specs=in_specs,
            out_specs=pl.BlockSpec((None, tr, n), lambda i, c: (lead, c[0] * nblk + i, 0))),
        out_shape=jax.ShapeDtypeStruct(shape, F32), input_output_aliases=aliases, compiler_params=_cp(),
    )(*args)


def _join_halves(bufs, place):
    n_o = len(bufs)
    n_h = len(place)

    def body(*refs):
        outs = refs[n_o:2 * n_o]
        sems = refs[2 * n_o:2 * n_o + 2]
        x, y, c = _me()
        sib = (x, y, 1 - c)

        def half(k, h):
            o, lead = place[k]
            return _half_rows(outs[o].at[lead], h)

        cps = [_remote(half(k, c), half(k, c), sems, k, sib) for k in range(n_h)]
        for cp in cps:
            cp.start()
        for k in range(n_h):
            land = half(k, 1 - c)
            _remote(land, land, sems, k, sib).wait_recv()
        for cp in cps:
            cp.wait_send()

    return pl.pallas_call(
        body, name="grad_join_sibling", in_specs=[ANY] * n_o, out_specs=[ANY] * n_o,
        out_shape=[jax.ShapeDtypeStruct(b.shape, b.dtype) for b in bufs],
        input_output_aliases={k: k for k in range(n_o)},
        scratch_shapes=[pltpu.SemaphoreType.DMA((n_h,)), pltpu.SemaphoreType.DMA((n_h,))],
    )(*bufs)


def _allreduce_rows(rows):
    n_dev = 8
    n_r = len(rows)
    assert n_r <= 8

    def body(*refs):
        r_refs = refs[:n_r]
        o_ref, slots, send_sems, recv_sems = refs[n_r:]
        x, y, c = _me()
        me = 4 * x + 2 * y + c
        slots[me] = jnp.concatenate([r[...] for r in r_refs] + [jnp.zeros((8 - n_r, D_MODEL), F32)], axis=0)

        def peer(k):
            return (1 - x if k & 4 else x, 1 - y if k & 2 else y, 1 - c if k & 1 else c)

        cps = []
        for k in range(1, n_dev):
            cp = pltpu.make_async_remote_copy(src_ref=slots.at[me], dst_ref=slots.at[me], send_sem=send_sems.at[k - 1],
                                              recv_sem=recv_sems.at[k - 1], device_id=peer(k), device_id_type=MESH)
            cp.start()
            cps.append(cp)
        for k in range(1, n_dev):
            px, py, pc = peer(k)
            land = slots.at[4 * px + 2 * py + pc]
            pltpu.make_async_remote_copy(src_ref=land, dst_ref=land, send_sem=send_sems.at[k - 1],
                                         recv_sem=recv_sems.at[k - 1], device_id=peer(k),
                                         device_id_type=MESH).wait_recv()
        for cp in cps:
            cp.wait_send()
        acc = slots[0]
        for d in range(1, n_dev):
            acc = acc + slots[d]
        o_ref[...] = acc

    vm = pl.BlockSpec(memory_space=pltpu.VMEM)
    return pl.pallas_call(
        body, name="allreduce_rows", in_specs=[vm] * n_r, out_specs=vm,
        out_shape=jax.ShapeDtypeStruct((8, D_MODEL), F32),
        scratch_shapes=[pltpu.VMEM((n_dev, 8, D_MODEL), F32), pltpu.SemaphoreType.DMA((n_dev - 1,)),
                        pltpu.SemaphoreType.DMA((n_dev - 1,))],
    )(*rows)


def _adamw(w, g, m, v, name):
    shape = w.shape
    if len(shape) == 1:
        lead, rows, cols = 1, 1, shape[0]
    else:
        rows, cols = shape[-2:]
        lead = math.prod(shape[:-2])
    args = [a.reshape(lead, rows, cols) for a in (w, g, m, v)]
    tr = rows // 2 if rows % 16 == 0 else rows

    def body(w_ref, g_ref, m_ref, v_ref, d_ref, nm_ref, nv_ref):
        gv = g_ref[...]
        nm = ADAM_B1 * m_ref[...] + (1.0 - ADAM_B1) * gv
        nv = ADAM_B2 * v_ref[...] + (1.0 - ADAM_B2) * jnp.square(gv)
        m_hat = nm / (1.0 - ADAM_B1 ** ADAM_STEP)
        v_hat = nv / (1.0 - ADAM_B2 ** ADAM_STEP)
        d_ref[...] = -ADAM_LR * (m_hat / (jnp.sqrt(v_hat) + ADAM_EPS) + ADAM_WD * w_ref[...])
        nm_ref[...] = nm
        nv_ref[...] = nv

    spec = pl.BlockSpec((None, tr, cols), lambda l, i: (l, i, 0))
    outs = pl.pallas_call(
        body, name=name, grid=(lead, rows // tr), in_specs=[spec] * 4, out_specs=[spec] * 3,
        out_shape=[jax.ShapeDtypeStruct((lead, rows, cols), F32)] * 3, compiler_params=_cp(),
    )(*args)
    return [o.reshape(shape) for o in outs]


def kernel(x, a_w_in, a_sink, a_w_out, b_w_in, b_w_out, norm_mix, norm_ffn, w_gate, w_up, w_down, final_norm, loss_target, m_a_w_in, m_a_sink, m_a_w_out, m_b_w_in, m_b_w_out, m_norm_mix, m_norm_ffn, m_w_gate, m_w_up, m_w_down, m_final_norm, v_a_w_in, v_a_sink, v_a_w_out, v_b_w_in, v_b_w_out, v_norm_mix, v_norm_ffn, v_w_gate, v_w_up, v_w_down, v_final_norm):
    weights = dict(a_w_in=a_w_in, a_sink=a_sink, a_w_out=a_w_out, b_w_in=b_w_in, b_w_out=b_w_out, norm_mix=norm_mix,
                   norm_ffn=norm_ffn, w_gate=w_gate, w_up=w_up, w_down=w_down, final_norm=final_norm)
    mom = dict(a_w_in=m_a_w_in, a_sink=m_a_sink, a_w_out=m_a_w_out, b_w_in=m_b_w_in, b_w_out=m_b_w_out,
               norm_mix=m_norm_mix, norm_ffn=m_norm_ffn, w_gate=m_w_gate, w_up=m_w_up, w_down=m_w_down,
               final_norm=m_final_norm)
    var = dict(a_w_in=v_a_w_in, a_sink=v_a_sink, a_w_out=v_a_w_out, b_w_in=v_b_w_in, b_w_out=v_b_w_out,
               norm_mix=v_norm_mix, norm_ffn=v_norm_ffn, w_gate=v_w_gate, w_up=v_w_up, w_down=v_w_down,
               final_norm=v_final_norm)
    order = ["a_w_in", "a_sink", "a_w_out", "b_w_in", "b_w_out", "norm_mix", "norm_ffn", "w_gate", "w_up", "w_down",
             "final_norm"]

    c_arr = lax.axis_index("c").astype(jnp.int32).reshape(1)
    q_arr = (2 * lax.axis_index("x") + lax.axis_index("y")).astype(jnp.int32).reshape(1)
    shards = [a_w_in, a_w_out, b_w_in, b_w_out, w_gate, w_up, w_down]
    shard_names = ("a_in", "a_out", "b_in", "b_out", "wg", "wu", "wd")
    placed = [_place_shard(s, q_arr, col, f"place_{nm}")
              for s, col, nm in zip(shards, (True, False, True, False, False, False, False), shard_names)]
    a_in, a_out, b_in, b_out, wg, wu, wd = _gather_weights(placed)
    a_out = a_out.reshape(D_MODEL, D_MODEL)
    b_out = b_out.reshape(D_MODEL, D_MODEL)

    gx, grads, vecs = _local_step(x, loss_target, a_in, a_sink[0], a_out, b_in, b_out, norm_mix, norm_ffn, wg, wu, wd,
                                  final_norm)

    rows_out = D_MODEL // N_CHIPS
    partials = [grads["a_in"], grads["b_in"],
                grads["a_out"].reshape(N_CHIPS, rows_out, D_MODEL), grads["b_out"].reshape(N_CHIPS, rows_out, D_MODEL),
                grads["wg"][0], grads["wg"][1], grads["wu"][0], grads["wu"][1], grads["wd"][0], grads["wd"][1]]
    col_fam = (True, True) + (False,) * 8
    names = ("a_in", "b_in", "a_out", "b_out", "wg0", "wg1", "wu0", "wu1", "wd0", "wd1")
    theirs = _swap_halves_with_sibling(partials, col_fam)
    sums = [_half_add(p, r, c_arr, cf, f"chip_sum_{nm}") for p, r, cf, nm in zip(partials, theirs, col_fam, names)]
    contrib = _scatter_chip_sums(sums, col_fam)
    shapes = [a_w_in.shape, b_w_in.shape, a_w_out.shape, b_w_out.shape, w_gate.shape, w_up.shape, w_gate.shape]
    place = [(0, 0), (1, 0), (2, 0), (3, 0), (4, 0), (4, 1), (5, 0), (5, 1), (6, 0), (6, 1)]
    bufs = [None] * len(shapes)
    for p, nm, (o, lead) in zip(contrib, names, place):
        bufs[o] = _sum_chips(p, c_arr, bufs[o], lead, shapes[o], f"sum_chips_{nm}")
    g_a_in, g_b_in, g_a_out, g_b_out, g_wg, g_wu, g_wdt = _join_halves(bufs, place)
    g_wd = g_wdt.transpose(0, 2, 1)

    sink_row = jnp.pad(vecs["sink"][0:1], ((0, 0), (0, D_MODEL - LANES)))
    tot = _allreduce_rows([vecs["norm_mix"][0], vecs["norm_mix"][1], vecs["norm_ffn"][0], vecs["norm_ffn"][1],
                           vecs["final"], vecs["loss_cols"], sink_row])
    loss = (0.5 / D_MODEL) * jnp.sum(tot[5])
    gw = dict(a_w_in=g_a_in, a_sink=tot[6:7, :N_HEADS], a_w_out=g_a_out, b_w_in=g_b_in, b_w_out=g_b_out,
              norm_mix=tot[0:2], norm_ffn=tot[2:4], w_gate=g_wg, w_up=g_wu, w_down=g_wd, final_norm=tot[4])

    delta, new_m, new_v = {}, {}, {}
    for n in order:
        delta[n], new_m[n], new_v[n] = _adamw(weights[n], gw[n], mom[n], var[n], f"adamw_{n}")
    return (loss, gx, *[gw[n] for n in order], *[delta[n] for n in order], *[new_m[n] for n in order],
            *[new_v[n] for n in order])
```

```python
import functools
import math

import jax
import jax.numpy as jnp
from jax import lax
from jax.experimental import pallas as pl
from jax.experimental.pallas import tpu as pltpu

F32 = jnp.float32
BF16 = jnp.bfloat16

D_MODEL = 1024
HEAD_DIM = 64
N_HEADS = 16
N_KV = 4
QKV_W = 1536
D_FF = 2816
N_CHIPS = 4
FF_SH = D_FF // N_CHIPS
HALF_WINDOW_A = 128
DILATED = ((128, 1), (512, 4), (2048, 16))
ROPE_THETA = 10000.0
RMS_EPS = 1e-6
NEG_INF = -1e30
LANES = 128
ADAM_LR, ADAM_B1, ADAM_B2, ADAM_EPS, ADAM_WD, ADAM_STEP = 0.001, 0.9, 0.999, 1e-08, 0.01, 10
VMEM_LIMIT = 56 * 1024 * 1024
GRAD_TOKENS = 2048
MESH = pl.DeviceIdType.MESH


def _cp(**kw):
    return pltpu.CompilerParams(vmem_limit_bytes=VMEM_LIMIT, **kw)


def _row_tile(t, cap):
    tm = min(cap, t)
    assert t % tm == 0
    return tm


def _rope_tables(seq, dil):
    inv = 1.0 / (ROPE_THETA ** (jnp.arange(0, HEAD_DIM, 2, dtype=F32) / HEAD_DIM))
    ang = jnp.arange(seq, dtype=F32)[:, None] * inv[None, :]
    cos, sin = jnp.cos(ang), jnp.sin(ang)
    cos = jnp.tile(cos, (1, 4))
    sin = jnp.concatenate([-sin, sin, -sin, sin], axis=1)

    def perm(t):
        return t.reshape(seq // dil, dil, LANES).transpose(1, 0, 2).reshape(seq, LANES)

    return perm(cos), perm(sin)


def _swap_halves(t):
    lane = lax.broadcasted_iota(jnp.int32, t.shape, 1)
    return jnp.where((lane % HEAD_DIM) < HEAD_DIM // 2, pltpu.roll(t, LANES - 32, 1), pltpu.roll(t, 32, 1))


def _rope(t, cos, sin):
    return t * cos + _swap_halves(t) * sin


def _rope_t(t, cos, sin):
    return t * cos - _swap_halves(t) * sin


def _to_residue(t, batch, dil):
    if dil == 1:
        return t
    s = t.shape[0] // batch
    return t.reshape(batch, s // dil, dil, t.shape[1]).transpose(0, 2, 1, 3).reshape(t.shape)


def _from_residue(t, batch, dil):
    if dil == 1:
        return t
    s = t.shape[0] // batch
    return t.reshape(batch, dil, s // dil, t.shape[1]).transpose(0, 2, 1, 3).reshape(t.shape)


def _rms_fwd(x, w, name, with_t=False):
    t = x.shape[0]
    tm = _row_tile(t, 512)

    def body(x_ref, w_ref, o_ref, *ot_ref):
        xv = x_ref[...]
        r = lax.rsqrt(jnp.mean(xv * xv, axis=-1, keepdims=True) + RMS_EPS)
        y = (xv * r) * w_ref[...]
        o_ref[...] = y.astype(BF16)
        if with_t:
            ot_ref[0][...] = y.T.astype(BF16)

    out_specs = [pl.BlockSpec((tm, D_MODEL), lambda i: (i, 0))]
    out_shape = [jax.ShapeDtypeStruct((t, D_MODEL), BF16)]
    if with_t:
        out_specs.append(pl.BlockSpec((D_MODEL, tm), lambda i: (0, i)))
        out_shape.append(jax.ShapeDtypeStruct((D_MODEL, t), BF16))
    outs = pl.pallas_call(
        body, name=name, grid=(t // tm,),
        in_specs=[pl.BlockSpec((tm, D_MODEL), lambda i: (i, 0)), pl.BlockSpec((1, D_MODEL), lambda i: (0, 0))],
        out_specs=out_specs, out_shape=out_shape, compiler_params=_cp(),
    )(x, w)
    return outs if with_t else outs[0]


def _rms_bwd(x, w, dhs, dres, name, with_t=False):
    t = x.shape[0]
    tm = _row_tile(t, 512)
    n = len(dhs)

    def body(*refs):
        x_ref, w_ref = refs[0], refs[1]
        dh_refs = refs[2:2 + n]
        dres_ref = refs[2 + n]
        dx_ref, dxb_ref = refs[3 + n:5 + n]
        dw_ref = refs[-1]
        xv = x_ref[...]
        r = lax.rsqrt(jnp.mean(xv * xv, axis=-1, keepdims=True) + RMS_EPS)
        xh = xv * r
        dy = dh_refs[0][...]
        for k in range(1, n):
            dy = dy + dh_refs[k][...]
        dxh = dy * w_ref[...]
        dx = dres_ref[...] + r * (dxh - xh * jnp.mean(dxh * xh, axis=-1, keepdims=True))
        dx_ref[...] = dx
        dxb_ref[...] = dx.astype(BF16)
        if with_t:
            refs[5 + n][...] = dx.T.astype(BF16)

        @pl.when(pl.program_id(0) == 0)
        def _():
            dw_ref[...] = jnp.zeros_like(dw_ref)

        dw_ref[...] += jnp.sum(dy * xh, axis=0, keepdims=True)

    row = pl.BlockSpec((tm, D_MODEL), lambda i: (i, 0))
    vec = pl.BlockSpec((1, D_MODEL), lambda i: (0, 0))
    out_specs = [row, row]
    out_shape = [jax.ShapeDtypeStruct((t, D_MODEL), F32), jax.ShapeDtypeStruct((t, D_MODEL), BF16)]
    if with_t:
        out_specs.append(pl.BlockSpec((D_MODEL, tm), lambda i: (0, i)))
        out_shape.append(jax.ShapeDtypeStruct((D_MODEL, t), BF16))
    return pl.pallas_call(
        body, name=name, grid=(t // tm,),
        in_specs=[row, vec] + [row] * n + [row],
        out_specs=out_specs + [vec], out_shape=out_shape + [jax.ShapeDtypeStruct((1, D_MODEL), F32)],
        compiler_params=_cp(),
    )(x, w, *dhs, dres)


def _final_loss(x, w, target, name):
    t = x.shape[0]
    tm = _row_tile(t, 512)

    def body(x_ref, w_ref, t_ref, dx_ref, dxb_ref, dxt_ref, l_ref, dw_ref):
        xv = x_ref[...]
        r = lax.rsqrt(jnp.mean(xv * xv, axis=-1, keepdims=True) + RMS_EPS)
        xh = xv * r
        err = xh * w_ref[...] - t_ref[...]
        dy = err * (1.0 / D_MODEL)
        dxh = dy * w_ref[...]
        dx = r * (dxh - xh * jnp.mean(dxh * xh, axis=-1, keepdims=True))
        dx_ref[...] = dx
        dxb_ref[...] = dx.astype(BF16)
        dxt_ref[...] = dx.T.astype(BF16)

        @pl.when(pl.program_id(0) == 0)
        def _():
            l_ref[...] = jnp.zeros_like(l_ref)
            dw_ref[...] = jnp.zeros_like(dw_ref)

        l_ref[...] += jnp.sum(err * err, axis=0, keepdims=True)
        dw_ref[...] += jnp.sum(dy * xh, axis=0, keepdims=True)

    row = pl.BlockSpec((tm, D_MODEL), lambda i: (i, 0))
    vec = pl.BlockSpec((1, D_MODEL), lambda i: (0, 0))
    return pl.pallas_call(
        body, name=name, grid=(t // tm,),
        in_specs=[row, vec, row], out_specs=[row, row, pl.BlockSpec((D_MODEL, tm), lambda i: (0, i)), vec, vec],
        out_shape=[jax.ShapeDtypeStruct((t, D_MODEL), F32), jax.ShapeDtypeStruct((t, D_MODEL), BF16),
                   jax.ShapeDtypeStruct((D_MODEL, t), BF16),
                   jax.ShapeDtypeStruct((1, D_MODEL), F32), jax.ShapeDtypeStruct((1, D_MODEL), F32)],
        compiler_params=_cp(),
    )(x, w, target)


def _qkv_proj(h, w, cos, sin, group, name):
    t = h.shape[0]
    seq = cos.shape[0]
    tm = _row_tile(seq, 1024)
    n_q = N_HEADS * HEAD_DIM // LANES
    n_rope = (N_HEADS + N_KV) * HEAD_DIM // LANES
    scale = 1.0 / math.sqrt(HEAD_DIM)

    def body(h_ref, w_ref, cos_ref, sin_ref, o_ref):
        acc = jnp.dot(h_ref[...], w_ref[...], preferred_element_type=F32)
        cs, sn = cos_ref[...], sin_ref[...]
        csq, snq = cs * scale, sn * scale
        for c in range(QKV_W // LANES):
            blk = acc[:, c * LANES:(c + 1) * LANES]
            if c < n_q:
                blk = _rope(blk, csq, snq)
            elif c < n_rope:
                blk = _rope(blk, cs, sn)
            o_ref[:, c * LANES:(c + 1) * LANES] = blk.astype(BF16)

    tab = pl.BlockSpec((tm, LANES), lambda i: (i % (seq // tm), 0))
    return pl.pallas_call(
        body, name=name, grid=(t // tm,),
        in_specs=[pl.BlockSpec((tm, D_MODEL), lambda i: (i, 0)),
                  pl.BlockSpec((D_MODEL, QKV_W), lambda i: (0, group)), tab, tab],
        out_specs=pl.BlockSpec((tm, QKV_W), lambda i: (i, 0)),
        out_shape=jax.ShapeDtypeStruct((t, QKV_W), BF16), compiler_params=_cp(),
    )(h, w, cos, sin)


def _mm_res(a, w, res, name):
    t, k = a.shape
    tm = _row_tile(t, 1024)

    def body(a_ref, w_ref, r_ref, o_ref):
        o_ref[...] = r_ref[...] + jnp.dot(a_ref[...], w_ref[...], preferred_element_type=F32)

    return pl.pallas_call(
        body, name=name, grid=(t // tm,),
        in_specs=[pl.BlockSpec((tm, k), lambda i: (i, 0)), pl.BlockSpec((k, D_MODEL), lambda i: (0, 0)),
                  pl.BlockSpec((tm, D_MODEL), lambda i: (i, 0))],
        out_specs=pl.BlockSpec((tm, D_MODEL), lambda i: (i, 0)),
        out_shape=jax.ShapeDtypeStruct((t, D_MODEL), F32), compiler_params=_cp(),
    )(a, w, res)


def _mm_nt(dy, w, group, out_dtype, name):
    t, n = dy.shape
    k = w.shape[0]
    tm = _row_tile(t, 1024)

    def body(dy_ref, w_ref, o_ref):
        o_ref[...] = lax.dot_general(dy_ref[...], w_ref[...], (((1,), (1,)), ((), ())),
                                     preferred_element_type=F32).astype(out_dtype)

    return pl.pallas_call(
        body, name=name, grid=(t // tm,),
        in_specs=[pl.BlockSpec((tm, n), lambda i: (i, 0)), pl.BlockSpec((k, n), lambda i: (0, group))],
        out_specs=pl.BlockSpec((tm, k), lambda i: (i, 0)),
        out_shape=jax.ShapeDtypeStruct((t, k), out_dtype), compiler_params=_cp(),
    )(dy, w)


def _out_bwd(dx, w, o, name):
    t = dx.shape[0]
    tm = _row_tile(t, 512)

    def body(dx_ref, w_ref, o_ref, et_ref, do_ref, adj_ref):
        do = lax.dot_general(dx_ref[...], w_ref[...], (((1,), (1,)), ((), ())), preferred_element_type=F32)
        do_ref[...] = do.astype(BF16)
        adj_ref[...] = -_dot_split(do * o_ref[...].astype(F32), et_ref[...])

    row = pl.BlockSpec((tm, D_MODEL), lambda i: (i, 0))
    return pl.pallas_call(
        body, name=name, grid=(t // tm,),
        in_specs=[row, pl.BlockSpec((D_MODEL, D_MODEL), lambda i: (0, 0)), row,
                  pl.BlockSpec((D_MODEL, LANES), lambda i: (0, 0))],
        out_specs=[row, pl.BlockSpec((tm, LANES), lambda i: (i, 0))],
        out_shape=[jax.ShapeDtypeStruct((t, D_MODEL), BF16), jax.ShapeDtypeStruct((t, LANES), F32)],
        compiler_params=_cp(),
    )(dx, w, o, _head_expander().T)


def _mm_tn(a, bs, name):
    aq = a.ndim == 3
    bq = bs[0].ndim == 3
    t, ka = a.shape[-2:]
    n = bs[0].shape[-1]
    nq = N_CHIPS if (aq or bq) else 1
    tt = _row_tile(t, GRAD_TOKENS)
    tn = n if n <= 1024 else 768
    assert n % tn == 0
    nb = len(bs)
    steps = t // tt

    def body(*refs):
        a_ref = refs[0]
        b_refs = refs[1:1 + nb]
        o_refs = refs[1 + nb:1 + 2 * nb]
        acc_refs = refs[1 + 2 * nb:]
        s = pl.program_id(2)
        av = a_ref[...]
        for b_ref, o_ref, acc_ref in zip(b_refs, o_refs, acc_refs):
            @pl.when(s == 0)
            def _():
                acc_ref[...] = jnp.zeros_like(acc_ref)

            acc_ref[...] += lax.dot_general(av, b_ref[...], (((0,), (0,)), ((), ())), preferred_element_type=F32)

            @pl.when(s == steps - 1)
            def _():
                o_ref[...] = acc_ref[...].astype(BF16)

    a_spec = (pl.BlockSpec((None, tt, ka), lambda q, j, s: (q, s, 0)) if aq
              else pl.BlockSpec((tt, ka), lambda q, j, s: (s, 0)))
    b_spec = (pl.BlockSpec((None, tt, tn), lambda q, j, s: (q, s, j)) if bq
              else pl.BlockSpec((tt, tn), lambda q, j, s: (s, j)))
    if nq > 1:
        o_spec = pl.BlockSpec((None, ka, tn), lambda q, j, s: (q, 0, j))
        o_shape = jax.ShapeDtypeStruct((nq, ka, n), BF16)
    else:
        o_spec = pl.BlockSpec((ka, tn), lambda q, j, s: (0, j))
        o_shape = jax.ShapeDtypeStruct((ka, n), BF16)
    outs = pl.pallas_call(
        body, name=name, grid=(nq, n // tn, steps),
        in_specs=[a_spec] + [b_spec] * nb, out_specs=[o_spec] * nb, out_shape=[o_shape] * nb,
        scratch_shapes=[pltpu.VMEM((ka, tn), F32)] * nb, compiler_params=_cp(),
    )(a, *bs)
    return outs


def _mm_grad(at, bs, name):
    ka, t = at.shape
    bq = bs[0].ndim == 3
    n = bs[0].shape[-1]
    nq = N_CHIPS if bq else 1
    tt = _row_tile(t, GRAD_TOKENS)
    tn = n if n <= 1024 else 768
    assert n % tn == 0
    nb = len(bs)
    steps = t // tt

    def body(*refs):
        a_ref = refs[0]
        b_refs = refs[1:1 + nb]
        o_refs = refs[1 + nb:1 + 2 * nb]
        acc_refs = refs[1 + 2 * nb:]
        s = pl.program_id(2)
        av = a_ref[...]
        for b_ref, o_ref, acc_ref in zip(b_refs, o_refs, acc_refs):
            @pl.when(s == 0)
            def _():
                acc_ref[...] = jnp.zeros_like(acc_ref)

            acc_ref[...] += jnp.dot(av, b_ref[...], preferred_element_type=F32)

            @pl.when(s == steps - 1)
            def _():
                o_ref[...] = acc_ref[...].astype(BF16)

    a_spec = pl.BlockSpec((ka, tt), lambda q, j, s: (0, s))
    if bq:
        b_spec = pl.BlockSpec((None, tt, tn), lambda q, j, s: (q, s, j))
        o_spec = pl.BlockSpec((None, ka, tn), lambda q, j, s: (q, 0, j))
        o_shape = jax.ShapeDtypeStruct((nq, ka, n), BF16)
    else:
        b_spec = pl.BlockSpec((tt, tn), lambda q, j, s: (s, j))
        o_spec = pl.BlockSpec((ka, tn), lambda q, j, s: (0, j))
        o_shape = jax.ShapeDtypeStruct((ka, n), BF16)
    return pl.pallas_call(
        body, name=name, grid=(nq, n // tn, steps),
        in_specs=[a_spec] + [b_spec] * nb, out_specs=[o_spec] * nb, out_shape=[o_shape] * nb,
        scratch_shapes=[pltpu.VMEM((ka, tn), F32)] * nb, compiler_params=_cp(),
    )(at, *bs)


def _sigmoid(x):
    return 1.0 / (1.0 + jnp.exp(-x))


def _ffn_up(h, wg, wu, layer, name):
    t = h.shape[0]
    tm = _row_tile(t, 1024)

    def body(h_ref, wg_ref, wu_ref, a_ref, dg_ref, du_ref):
        hv = h_ref[...]
        g = jnp.dot(hv, wg_ref[...], preferred_element_type=F32)
        u = jnp.dot(hv, wu_ref[...], preferred_element_type=F32)
        sg = _sigmoid(g)
        silu = g * sg
        a_ref[...] = (silu * u).astype(BF16)
        dg_ref[...] = (sg * (1.0 + g * (1.0 - sg)) * u).astype(BF16)
        du_ref[...] = silu.astype(BF16)

    wspec = pl.BlockSpec((None, None, D_MODEL, FF_SH), lambda q, i: (q, layer, 0, 0))
    ospec = pl.BlockSpec((None, tm, FF_SH), lambda q, i: (q, i, 0))
    oshape = jax.ShapeDtypeStruct((N_CHIPS, t, FF_SH), BF16)
    return pl.pallas_call(
        body, name=name, grid=(N_CHIPS, t // tm),
        in_specs=[pl.BlockSpec((tm, D_MODEL), lambda q, i: (i, 0)), wspec, wspec],
        out_specs=[ospec] * 3, out_shape=[oshape] * 3, compiler_params=_cp(),
    )(h, wg, wu)


def _ffn_down(a, wd, res, layer, name):
    t = a.shape[1]
    tm = _row_tile(t, 512)

    def body(a_ref, w_ref, r_ref, o_ref):
        acc = r_ref[...]
        for q in range(N_CHIPS):
            acc = acc + jnp.dot(a_ref[q], w_ref[q], preferred_element_type=F32)
        o_ref[...] = acc

    return pl.pallas_call(
        body, name=name, grid=(t // tm,),
        in_specs=[pl.BlockSpec((N_CHIPS, tm, FF_SH), lambda i: (0, i, 0)),
                  pl.BlockSpec((N_CHIPS, None, FF_SH, D_MODEL), lambda i: (0, layer, 0, 0)),
                  pl.BlockSpec((tm, D_MODEL), lambda i: (i, 0))],
        out_specs=pl.BlockSpec((tm, D_MODEL), lambda i: (i, 0)),
        out_shape=jax.ShapeDtypeStruct((t, D_MODEL), F32), compiler_params=_cp(),
    )(a, wd, res)


def _ffn_down_bwd(dx, wd, fg, fu, layer, name):
    t = dx.shape[0]
    tm = _row_tile(t, 1024)

    def body(dx_ref, w_ref, fg_ref, fu_ref, dg_ref, du_ref):
        da = lax.dot_general(dx_ref[...], w_ref[...], (((1,), (1,)), ((), ())), preferred_element_type=F32)
        dg_ref[...] = (da * fg_ref[...].astype(F32)).astype(BF16)
        du_ref[...] = (da * fu_ref[...].astype(F32)).astype(BF16)

    aspec = pl.BlockSpec((None, tm, FF_SH), lambda q, i: (q, i, 0))
    oshape = jax.ShapeDtypeStruct((N_CHIPS, t, FF_SH), BF16)
    return pl.pallas_call(
        body, name=name, grid=(N_CHIPS, t // tm),
        in_specs=[pl.BlockSpec((tm, D_MODEL), lambda q, i: (i, 0)),
                  pl.BlockSpec((None, None, FF_SH, D_MODEL), lambda q, i: (q, layer, 0, 0)), aspec, aspec],
        out_specs=[aspec] * 2, out_shape=[oshape] * 2, compiler_params=_cp(),
    )(dx, wd, fg, fu)


def _ffn_up_bwd(dg, du, wg, wu, layer, name):
    t = dg.shape[1]
    tm = _row_tile(t, 512)
    nt = (((1,), (1,)), ((), ()))

    def body(dg_ref, du_ref, wg_ref, wu_ref, o_ref):
        acc = jnp.zeros((tm, D_MODEL), F32)
        for q in range(N_CHIPS):
            acc = acc + lax.dot_general(dg_ref[q], wg_ref[q], nt, preferred_element_type=F32)
            acc = acc + lax.dot_general(du_ref[q], wu_ref[q], nt, preferred_element_type=F32)
        o_ref[...] = acc

    aspec = pl.BlockSpec((N_CHIPS, tm, FF_SH), lambda i: (0, i, 0))
    wspec = pl.BlockSpec((N_CHIPS, None, D_MODEL, FF_SH), lambda i: (0, layer, 0, 0))
    return pl.pallas_call(
        body, name=name, grid=(t // tm,),
        in_specs=[aspec, aspec, wspec, wspec],
        out_specs=pl.BlockSpec((tm, D_MODEL), lambda i: (i, 0)),
        out_shape=jax.ShapeDtypeStruct((t, D_MODEL), F32), compiler_params=_cp(),
    )(dg, du, wg, wu)


def _attn_geometry(length, half_window):
    qb = min(LANES, length)
    kw = min(qb + 2 * half_window, length)
    return qb, kw, length // qb


def _dup_kv(src_ref, dst_ref, s, length):
    ch = min(length, 256)
    lo = lax.broadcasted_iota(jnp.int32, (ch, LANES), 1) < HEAD_DIM

    def chunk(c, carry):
        r0 = pl.multiple_of(c * ch, ch)
        for j in range(N_KV // 2):
            tile = src_ref[s, pl.ds(r0, ch), j * LANES:(j + 1) * LANES].astype(F32)
            rolled = pltpu.roll(tile, HEAD_DIM, 1)
            dst_ref[2 * j, pl.ds(r0, ch), :] = jnp.where(lo, tile, rolled).astype(BF16)
            dst_ref[2 * j + 1, pl.ds(r0, ch), :] = jnp.where(lo, rolled, tile).astype(BF16)
        return carry

    lax.fori_loop(0, length // ch, chunk, 0)


def _stack_heads(ref, s, q0, qb, g):
    lo = lax.broadcasted_iota(jnp.int32, (qb, LANES), 1) < HEAD_DIM
    parts = []
    for a in range(4):
        col = (2 * g + a // 2) * LANES
        tile = ref[s, pl.ds(q0, qb), col:col + LANES]
        keep = lo if a % 2 == 0 else jnp.logical_not(lo)
        parts.append(jnp.where(keep, tile, jnp.zeros_like(tile)))
    return jnp.concatenate(parts, axis=0)


def _unstack_pair_t(stacked_t, qb, pair):
    lo = lax.broadcasted_iota(jnp.int32, (LANES, qb), 0) < HEAD_DIM
    both = jnp.where(lo, stacked_t[:, (2 * pair) * qb:(2 * pair + 1) * qb],
                     stacked_t[:, (2 * pair + 1) * qb:(2 * pair + 2) * qb])
    return both.T


def _band_mask_t(q0, k0, qb, kw, half_window):
    key = lax.broadcasted_iota(jnp.int32, (kw, 4 * qb), 0)
    qry = lax.broadcasted_iota(jnp.int32, (kw, 4 * qb), 1) & (qb - 1)
    return jnp.abs((q0 + qry) - (k0 + key)) <= half_window


def _block_origin(i, qb, kw, half_window, length):
    if isinstance(i, int):
        return i * qb, min(max(i * qb - half_window, 0), length - kw)
    return (pl.multiple_of(i * qb, qb),
            pl.multiple_of(jnp.clip(i * qb - half_window, 0, length - kw), HEAD_DIM))


def _head_row(vals, qb):
    return jnp.concatenate([jnp.broadcast_to(v, (1, qb)).astype(F32) for v in vals], axis=1)


def _attn_fwd(qkv, sink, n_seq, length, half_window, seq_blk, out_dtype, with_lse, name):
    qb, kw, nblk = _attn_geometry(length, half_window)
    with_sink = sink is not None
    nt = (((1,), (1,)), ((), ()))
    tn = (((0,), (0,)), ((), ()))
    qkv3 = qkv.reshape(n_seq, length, QKV_W)

    def body(*refs):
        refs = list(refs)
        sink_ref = refs.pop(0) if with_sink else None
        q_ref, k_ref, v_ref, o_ref = refs[:4]
        lse_ref = refs[4] if with_lse else None
        kx_ref, vx_ref = refs[-2:]
        head_row = lax.broadcasted_iota(jnp.int32, (N_HEADS, qb), 0)
        for s in range(seq_blk):
            _dup_kv(k_ref, kx_ref, s, length)
            _dup_kv(v_ref, vx_ref, s, length)

            def block(i, carry):
                q0, k0 = _block_origin(i, qb, kw, half_window, length)
                valid = _band_mask_t(q0, k0, qb, kw, half_window)
                lse_tile = jnp.zeros((N_HEADS, qb), F32)
                for g in range(N_KV):
                    qs = _stack_heads(q_ref, s, q0, qb, g)
                    kx = kx_ref[g, pl.ds(k0, kw), :]
                    vx = vx_ref[g, pl.ds(k0, kw), :]
                    st = lax.dot_general(kx, qs, nt, preferred_element_type=F32)
                    st = jnp.where(valid, st, NEG_INF)
                    m = jnp.max(st, axis=0, keepdims=True)
                    if with_sink:
                        sk = _head_row([sink_ref[4 * g + a] for a in range(4)], qb)
                        m = jnp.maximum(m, sk)
                    e = jnp.exp(st - m)
                    den = jnp.sum(e, axis=0, keepdims=True)
                    if with_sink:
                        den = den + jnp.exp(sk - m)
                    ot = lax.dot_general(vx, e.astype(BF16), tn, preferred_element_type=F32) / den
                    for pair in range(2):
                        col = (2 * g + pair) * LANES
                        o_ref[s, pl.ds(q0, qb), col:col + LANES] = _unstack_pair_t(ot, qb, pair).astype(out_dtype)
                    if with_lse:
                        lse = m + jnp.log(den)
                        for a in range(4):
                            lse_tile = jnp.where(head_row == 4 * g + a, lse[:, a * qb:(a + 1) * qb], lse_tile)
                if with_lse:
                    lse_ref[s, :, pl.ds(q0, qb)] = lse_tile
                return carry

            if nblk == 1:
                block(0, 0)
            else:
                lax.fori_loop(0, nblk, block, 0)

    in_specs = [pl.BlockSpec((seq_blk, length, N_HEADS * HEAD_DIM), lambda n: (n, 0, 0)),
                pl.BlockSpec((seq_blk, length, N_KV * HEAD_DIM), lambda n: (n, 0, 4)),
                pl.BlockSpec((seq_blk, length, N_KV * HEAD_DIM), lambda n: (n, 0, 5))]
    args = [qkv3, qkv3, qkv3]
    if with_sink:
        in_specs.insert(0, pl.BlockSpec(memory_space=pltpu.SMEM))
        args.insert(0, sink)
    out_specs = [pl.BlockSpec((seq_blk, length, D_MODEL), lambda n: (n, 0, 0))]
    out_shape = [jax.ShapeDtypeStruct((n_seq, length, D_MODEL), out_dtype)]
    if with_lse:
        out_specs.append(pl.BlockSpec((seq_blk, N_HEADS, length), lambda n: (n, 0, 0)))
        out_shape.append(jax.ShapeDtypeStruct((n_seq, N_HEADS, length), F32))
    outs = pl.pallas_call(
        body, name=name, grid=(n_seq // seq_blk,), in_specs=in_specs, out_specs=out_specs, out_shape=out_shape,
        scratch_shapes=[pltpu.VMEM((N_KV, length, LANES), BF16), pltpu.VMEM((N_KV, length, LANES), BF16)],
        compiler_params=_cp(),
    )(*args)
    o = outs[0].reshape(n_seq * length, D_MODEL)
    return (o, outs[1]) if with_lse else (o,)


def _attn_bwd(qkv, do, adj, sink, cos, sin, n_seq, length, half_window, seq_blk, dil, name):
    qb, kw, nblk = _attn_geometry(length, half_window)
    scale = 1.0 / math.sqrt(HEAD_DIM)
    with_sink = sink is not None
    nt = (((1,), (1,)), ((), ()))
    tn = (((0,), (0,)), ((), ()))
    qkv3 = qkv.reshape(n_seq, length, QKV_W)
    do3 = do.reshape(n_seq, length, D_MODEL)
    tabs = [t.reshape(dil, length, LANES) for t in (cos, sin)]
    tab_blocks = dil // seq_blk if dil >= seq_blk else 1

    def body(*refs):
        refs = list(refs)
        sink_ref = refs.pop(0) if with_sink else None
        q_ref, k_ref, v_ref, do_ref, aux_ref, cos_ref, sin_ref, dqkv_ref = refs[:8]
        ds_ref = refs[8] if with_sink else None
        kx_ref, vx_ref, dkx_ref, dvx_ref = refs[-4:]
        lane = lax.broadcasted_iota(jnp.int32, (1, LANES), 1)
        if with_sink:
            @pl.when(pl.program_id(0) == 0)
            def _():
                ds_ref[...] = jnp.zeros_like(ds_ref)

        for s in range(seq_blk):
            ts = s % dil
            _dup_kv(k_ref, kx_ref, s, length)
            _dup_kv(v_ref, vx_ref, s, length)
            dkx_ref[...] = jnp.zeros_like(dkx_ref)
            dvx_ref[...] = jnp.zeros_like(dvx_ref)

            def block(i, dsink):
                q0, k0 = _block_origin(i, qb, kw, half_window, length)
                valid = _band_mask_t(q0, k0, qb, kw, half_window)
                cs = cos_ref[ts, pl.ds(q0, qb), :] * scale
                sn = sin_ref[ts, pl.ds(q0, qb), :] * scale
                adj_tile = aux_ref[s, :, pl.ds(q0, qb)]
                for g in range(N_KV):
                    qs = _stack_heads(q_ref, s, q0, qb, g)
                    dos = _stack_heads(do_ref, s, q0, qb, g)
                    kx = kx_ref[g, pl.ds(k0, kw), :]
                    vx = vx_ref[g, pl.ds(k0, kw), :]
                    st = lax.dot_general(kx, qs, nt, preferred_element_type=F32)
                    st = jnp.where(valid, st, NEG_INF)
                    m = jnp.max(st, axis=0, keepdims=True)
                    if with_sink:
                        sk = _head_row([sink_ref[4 * g + a] for a in range(4)], qb)
                        m = jnp.maximum(m, sk)
                    e = jnp.exp(st - m)
                    den = jnp.sum(e, axis=0, keepdims=True)
                    if with_sink:
                        esk = jnp.exp(sk - m)
                        den = den + esk
                    rden = 1.0 / den
                    pt = e * rden
                    shift = _head_row([adj_tile[4 * g + a:4 * g + a + 1, :] for a in range(4)], qb)
                    dpt = lax.dot_general(vx, dos, nt, preferred_element_type=F32)
                    dst = pt * (dpt + shift)
                    if with_sink:
                        dsk = esk * rden * shift
                        for a in range(4):
                            tot = jnp.sum(dsk[:, a * qb:(a + 1) * qb], axis=1, keepdims=True)
                            dsink = dsink + jnp.where(lane == 4 * g + a, tot, 0.0)
                    dsb = dst.astype(BF16)
                    pb = pt.astype(BF16)
                    dqt = lax.dot_general(kx, dsb, tn, preferred_element_type=F32)
                    for pair in range(2):
                        col = (2 * g + pair) * LANES
                        tile = _rope_t(_unstack_pair_t(dqt, qb, pair), cs, sn)
                        dqkv_ref[s, pl.ds(q0, qb), col:col + LANES] = tile.astype(BF16)
                    dkx_ref[g, pl.ds(k0, kw), :] += jnp.dot(dsb, qs, preferred_element_type=F32)
                    dvx_ref[g, pl.ds(k0, kw), :] += jnp.dot(pb, dos, preferred_element_type=F32)
                return dsink

            if nblk == 1:
                dsink = block(0, jnp.zeros((1, LANES), F32))
            else:
                dsink = lax.fori_loop(0, nblk, block, jnp.zeros((1, LANES), F32))
            if with_sink:
                ds_ref[0:1, :] += dsink

            ch = min(length, 256)
            lo_c = lax.broadcasted_iota(jnp.int32, (ch, LANES), 1) < HEAD_DIM

            def fin(c, carry):
                r0 = pl.multiple_of(c * ch, ch)
                cs = cos_ref[ts, pl.ds(r0, ch), :]
                sn = sin_ref[ts, pl.ds(r0, ch), :]
                for j in range(N_KV // 2):
                    both = []
                    for acc_ref in (dkx_ref, dvx_ref):
                        t0 = acc_ref[2 * j, pl.ds(r0, ch), :]
                        t1 = acc_ref[2 * j + 1, pl.ds(r0, ch), :]
                        t0 = t0 + pltpu.roll(t0, HEAD_DIM, 1)
                        t1 = t1 + pltpu.roll(t1, HEAD_DIM, 1)
                        both.append(jnp.where(lo_c, t0, t1))
                    kcol = N_HEADS * HEAD_DIM + j * LANES
                    vcol = (N_HEADS + N_KV) * HEAD_DIM + j * LANES
                    dqkv_ref[s, pl.ds(r0, ch), kcol:kcol + LANES] = _rope_t(both[0], cs, sn).astype(BF16)
                    dqkv_ref[s, pl.ds(r0, ch), vcol:vcol + LANES] = both[1].astype(BF16)
                return carry

            lax.fori_loop(0, length // ch, fin, 0)

    seq_map = lambda n: (n, 0, 0)
    tab_map = (lambda n: (n % tab_blocks, 0, 0)) if dil >= seq_blk else (lambda n: (0, 0, 0))
    tab_rows = min(seq_blk, dil)
    in_specs = [pl.BlockSpec((seq_blk, length, N_HEADS * HEAD_DIM), seq_map),
                pl.BlockSpec((seq_blk, length, N_KV * HEAD_DIM), lambda n: (n, 0, 4)),
                pl.BlockSpec((seq_blk, length, N_KV * HEAD_DIM), lambda n: (n, 0, 5)),
                pl.BlockSpec((seq_blk, length, D_MODEL), seq_map),
                pl.BlockSpec((seq_blk, N_HEADS, length), seq_map),
                pl.BlockSpec((tab_rows, length, LANES), tab_map),
                pl.BlockSpec((tab_rows, length, LANES), tab_map)]
    args = [qkv3, qkv3, qkv3, do3, adj] + tabs
    if with_sink:
        in_specs.insert(0, pl.BlockSpec(memory_space=pltpu.SMEM))
        args.insert(0, sink)
    out_specs = [pl.BlockSpec((seq_blk, length, QKV_W), seq_map)]
    out_shape = [jax.ShapeDtypeStruct((n_seq, length, QKV_W), BF16)]
    if with_sink:
        out_specs.append(pl.BlockSpec((8, LANES), lambda n: (0, 0)))
        out_shape.append(jax.ShapeDtypeStruct((8, LANES), F32))
    outs = pl.pallas_call(
        body, name=name, grid=(n_seq // seq_blk,), in_specs=in_specs, out_specs=out_specs, out_shape=out_shape,
        scratch_shapes=[pltpu.VMEM((N_KV, length, LANES), BF16), pltpu.VMEM((N_KV, length, LANES), BF16),
                        pltpu.VMEM((N_KV, length, LANES), F32), pltpu.VMEM((N_KV, length, LANES), F32)],
        compiler_params=_cp(),
    )(*args)
    dqkv = outs[0].reshape(n_seq * length, QKV_W)
    return (dqkv, outs[1]) if with_sink else (dqkv, None)


def _head_expander():
    h = jnp.arange(LANES)[:, None]
    l = jnp.arange(D_MODEL)[None, :]
    return (l // HEAD_DIM == h).astype(BF16)


def _dot_split(a, e):
    hi = a.astype(BF16)
    lo = (a - hi.astype(F32)).astype(BF16)
    return jnp.dot(hi, e, preferred_element_type=F32) + jnp.dot(lo, e, preferred_element_type=F32)


def _mix_weights(lses):
    m = jnp.maximum(jnp.maximum(lses[0], lses[1]), lses[2])
    es = [jnp.exp(v - m) for v in lses]
    tot = es[0] + es[1] + es[2]
    return [e / tot for e in es]


def _mix_fwd(os_, lses, name):
    t = os_[0].shape[0]
    tm = _row_tile(t, 512)

    def body(o0, o1, o2, l0, l1, l2, e_ref, out_ref):
        wts = _mix_weights([l0[...], l1[...], l2[...]])
        acc = jnp.zeros((tm, D_MODEL), F32)
        for w, o_ref in zip(wts, (o0, o1, o2)):
            acc = acc + _dot_split(w, e_ref[...]) * o_ref[...]
        out_ref[...] = acc.astype(BF16)

    row = pl.BlockSpec((tm, D_MODEL), lambda i: (i, 0))
    lrow = pl.BlockSpec((tm, LANES), lambda i: (i, 0))
    return pl.pallas_call(
        body, name=name, grid=(t // tm,),
        in_specs=[row] * 3 + [lrow] * 3 + [pl.BlockSpec((LANES, D_MODEL), lambda i: (0, 0))],
        out_specs=row, out_shape=jax.ShapeDtypeStruct((t, D_MODEL), BF16), compiler_params=_cp(),
    )(*os_, *lses, _head_expander())


def _mix_bwd(dmix, os_, lses, name):
    t = dmix.shape[0]
    tm = _row_tile(t, 512)

    def body(d_ref, o0, o1, o2, l0, l1, l2, e_ref, et_ref, do0, do1, do2, a0, a1, a2):
        wts = _mix_weights([l0[...], l1[...], l2[...]])
        dv = d_ref[...].astype(F32)
        cs = [_dot_split(dv * o_ref[...], et_ref[...]) for o_ref in (o0, o1, o2)]
        mean_c = wts[0] * cs[0] + wts[1] * cs[1] + wts[2] * cs[2]
        for w, c, do_ref, a_ref in zip(wts, cs, (do0, do1, do2), (a0, a1, a2)):
            do_ref[...] = (_dot_split(w, e_ref[...]) * dv).astype(BF16)
            a_ref[...] = w * (c - mean_c) - w * c

    row = pl.BlockSpec((tm, D_MODEL), lambda i: (i, 0))
    lrow = pl.BlockSpec((tm, LANES), lambda i: (i, 0))
    e = _head_expander()
    return pl.pallas_call(
        body, name=name, grid=(t // tm,),
        in_specs=[row] * 4 + [lrow] * 3 + [pl.BlockSpec((LANES, D_MODEL), lambda i: (0, 0)),
                                            pl.BlockSpec((D_MODEL, LANES), lambda i: (0, 0))],
        out_specs=[row] * 3 + [lrow] * 3,
        out_shape=[jax.ShapeDtypeStruct((t, D_MODEL), BF16)] * 3 + [jax.ShapeDtypeStruct((t, LANES), F32)] * 3,
        compiler_params=_cp(),
    )(dmix, *os_, *lses, e, e.T)


def _stats_to_tokens(stat, batch, dil):
    n_seq, _, length = stat.shape
    t = stat.transpose(0, 2, 1).reshape(n_seq * length, N_HEADS)
    return _from_residue(jnp.pad(t, ((0, 0), (0, LANES - N_HEADS))), batch, dil)


def _stats_from_tokens(stat, batch, dil, n_seq, length):
    t = _to_residue(stat[:, :N_HEADS], batch, dil)
    return t.reshape(n_seq, length, N_HEADS).transpose(0, 2, 1)


def _group_geometry(batch, seq, dil, window):
    length = seq // dil
    n_seq = batch * dil
    seq_blk = max(1, min(dil, 1024 // length))
    return n_seq, length, (window // 2) // dil, seq_blk


def _local_step(x, target, a_in, a_sink, a_out, b_in, b_out, norm_mix, norm_ffn, wg, wu, wd, final_norm):
    batch, seq, _ = x.shape
    t = batch * seq
    x0 = x.reshape(t, D_MODEL)
    tgt = target.reshape(t, D_MODEL)
    tabs = {d: _rope_tables(seq, d) for _, d in DILATED}
    nm = [norm_mix[i:i + 1] for i in range(2)]
    nf = [norm_ffn[i:i + 1] for i in range(2)]

    h0, h0t = _rms_fwd(x0, nm[0], "rms_mix0", True)
    qkv0 = _qkv_proj(h0, a_in, *tabs[1], 0, "qkv0")
    (o0,) = _attn_fwd(qkv0, a_sink, batch, seq, HALF_WINDOW_A, 1, BF16, False, "attn0")
    x1 = _mm_res(o0, a_out, x0, "out0")
    hf0, hf0t = _rms_fwd(x1, nf[0], "rms_ffn0", True)
    act0, g0, u0 = _ffn_up(hf0, wg, wu, 0, "ffn_up0")
    x2 = _ffn_down(act0, wd, x1, 0, "ffn_down0")

    h1 = _rms_fwd(x2, nm[1], "rms_mix1")
    geo = [_group_geometry(batch, seq, d, w) for w, d in DILATED]
    h1g, qkv1, o1, lse1 = [], [], [], []
    for gi, (_, d) in enumerate(DILATED):
        n_seq, length, hw, sb = geo[gi]
        hp = _to_residue(h1, batch, d)
        pj = _qkv_proj(hp, b_in, *tabs[d], gi, f"qkv1_{gi}")
        o, lse = _attn_fwd(pj, None, n_seq, length, hw, sb, F32, True, f"attn1_{gi}")
        h1g.append(hp)
        qkv1.append(pj)
        o1.append(_from_residue(o, batch, d))
        lse1.append(_stats_to_tokens(lse, batch, d))
    omix = _mix_fwd(o1, lse1, "mix")
    x3 = _mm_res(omix, b_out, x2, "out1")
    hf1, hf1t = _rms_fwd(x3, nf[1], "rms_ffn1", True)
    act1, g1, u1 = _ffn_up(hf1, wg, wu, 1, "ffn_up1")
    x4 = _ffn_down(act1, wd, x3, 1, "ffn_down1")

    dx4, dx4b, dx4t, loss_cols, d_final = _final_loss(x4, final_norm.reshape(1, D_MODEL), tgt, "final_loss")

    def ffn_bwd(dxo, dxob, dxot, x_mid, hft, g, u, act, layer):
        dg, du = _ffn_down_bwd(dxob, wd, g, u, layer, f"ffn_down_bwd{layer}")
        (d_wdt,) = _mm_grad(dxot, [act], f"grad_wd{layer}")
        dh = _ffn_up_bwd(dg, du, wg, wu, layer, f"ffn_up_bwd{layer}")
        d_wg, d_wu = _mm_grad(hft, [dg, du], f"grad_wgu{layer}")
        dxm, dxmb, d_nf = _rms_bwd(x_mid, nf[layer], [dh], dxo, f"rms_ffn_bwd{layer}")
        return dxm, dxmb, d_nf, d_wg, d_wu, d_wdt

    dx3, dx3b, d_nf1, d_wg1, d_wu1, d_wd1 = ffn_bwd(dx4, dx4b, dx4t, x3, hf1t, g1, u1, act1, 1)

    dmix = _mm_nt(dx3b, b_out, 0, BF16, "out1_bwd")
    (d_b_out,) = _mm_tn(omix, [dx3b], "grad_b_out")
    mb = _mix_bwd(dmix, o1, lse1, "mix_bwd")
    dh1, d_b_in = [], []
    for gi, (_, d) in enumerate(DILATED):
        n_seq, length, hw, sb = geo[gi]
        dog = _to_residue(mb[gi], batch, d)
        adj = _stats_from_tokens(mb[3 + gi], batch, d, n_seq, length)
        dpj, _ = _attn_bwd(qkv1[gi], dog, adj, None, *tabs[d], n_seq, length, hw, sb, d, f"attn1_bwd{gi}")
        (dw,) = _mm_tn(h1g[gi], [dpj], f"grad_b_in{gi}")
        d_b_in.append(dw)
        dh1.append(_from_residue(_mm_nt(dpj, b_in, gi, F32, f"qkv1_bwd{gi}"), batch, d))
    dx2, dx2b, dx2t, d_nm1 = _rms_bwd(x2, nm[1], dh1, dx3, "rms_mix_bwd1", True)

    dx1, dx1b, d_nf0, d_wg0, d_wu0, d_wd0 = ffn_bwd(dx2, dx2b, dx2t, x1, hf0t, g0, u0, act0, 0)

    do0, adj0 = _out_bwd(dx1b, a_out, o0, "out0_bwd")
    (d_a_out,) = _mm_tn(o0, [dx1b], "grad_a_out")
    adj0 = _stats_from_tokens(adj0, batch, 1, batch, seq)
    dqkv0, d_sink = _attn_bwd(qkv0, do0, adj0, a_sink, *tabs[1], batch, seq, HALF_WINDOW_A, 1, 1, "attn0_bwd")
    (d_a_in,) = _mm_grad(h0t, [dqkv0], "grad_a_in")
    dh0 = _mm_nt(dqkv0, a_in, 0, F32, "qkv0_bwd")
    gx, _, d_nm0 = _rms_bwd(x0, nm[0], [dh0], dx1, "rms_mix_bwd0")

    grads = dict(a_in=d_a_in, a_out=d_a_out, b_in=jnp.concatenate(d_b_in, axis=1), b_out=d_b_out,
                 wg=(d_wg0, d_wg1), wu=(d_wu0, d_wu1), wd=(d_wd0, d_wd1))
    vecs = dict(norm_mix=(d_nm0, d_nm1), norm_ffn=(d_nf0, d_nf1), final=d_final, loss_cols=loss_cols, sink=d_sink)
    return gx.reshape(x.shape), grads, vecs


ANY = pl.BlockSpec(memory_space=pl.ANY)


def _me():
    return lax.axis_index("x"), lax.axis_index("y"), lax.axis_index("c")


def _chip_peer(x, y, j):
    px = 1 - x if j & 2 else x
    py = 1 - y if j & 1 else y
    return px, py, 2 * px + py


def _remote(src, dst, sems, k, dev):
    return pltpu.make_async_remote_copy(src_ref=src, dst_ref=dst, send_sem=sems[0].at[k], recv_sem=sems[1].at[k],
                                        device_id=dev, device_id_type=MESH)


def _col_window(ref, q, width):
    return ref.at[:, pl.ds(pl.multiple_of(q * width, LANES), width)]


def _half0(ref, h):
    n = ref.shape[0] // 2
    return ref.at[pl.ds(h * n, n)]


def _half1(ref, h):
    n = ref.shape[1] // 2
    return ref.at[:, pl.ds(h * n, n)]


def _half_rows(ref, h):
    n = ref.shape[-2] // 2
    if len(ref.shape) == 2:
        return ref.at[pl.ds(h * n, n)]
    return ref.at[:, pl.ds(h * n, n)]


def _place_shard(w, q_arr, col, name):
    lead, rows, cols = w.shape

    def body(q_ref, w_ref, o_ref):
        o_ref[...] = w_ref[...].astype(BF16)

    if col:
        assert lead == 1
        out_spec = pl.BlockSpec((rows, cols), lambda l, q: (0, q[0]))
        out_shape = jax.ShapeDtypeStruct((rows, N_CHIPS * cols), BF16)
    else:
        out_spec = pl.BlockSpec((None, None, rows, cols), lambda l, q: (q[0], l, 0, 0))
        out_shape = jax.ShapeDtypeStruct((N_CHIPS, lead, rows, cols), BF16)
    return pl.pallas_call(
        body, name=name,
        grid_spec=pltpu.PrefetchScalarGridSpec(
            num_scalar_prefetch=1, grid=(lead,),
            in_specs=[pl.BlockSpec((None, rows, cols), lambda l, q: (l, 0, 0))], out_specs=out_spec),
        out_shape=out_shape, compiler_params=_cp(),
    )(q_arr, w)


def _gather_weights(bufs):
    col_fam = (True, False, True, False, False, False, False)
    n_w = len(bufs)

    def body(*refs):
        outs = refs[n_w:2 * n_w]
        sems = refs[2 * n_w:2 * n_w + 2]
        x, y, c = _me()
        myq = 2 * x + y
        sib = (x, y, 1 - c)

        def slot(w, q):
            if col_fam[w]:
                return _col_window(outs[w], q, outs[w].shape[1] // N_CHIPS)
            return outs[w].at[q]

        first = []
        for w in range(n_w):
            for j in (1, 2, 3):
                px, py, _ = _chip_peer(x, y, j)
                mine = _half_rows(slot(w, myq), c)
                cp = _remote(mine, mine, sems, w * 6 + j - 1, (px, py, c))
                cp.start()
                first.append(cp)
        passed = []
        for w in range(n_w):
            for j in (1, 2, 3):
                _, _, pq = _chip_peer(x, y, j)
                land = _half_rows(slot(w, pq), c)
                _remote(land, land, sems, w * 6 + j - 1, sib).wait_recv()
                cp = _remote(land, land, sems, w * 6 + 2 + j, sib)
                cp.start()
                passed.append(cp)
        for w in range(n_w):
            for j in (1, 2, 3):
                _, _, pq = _chip_peer(x, y, j)
                land = _half_rows(slot(w, pq), 1 - c)
                _remote(land, land, sems, w * 6 + 2 + j, sib).wait_recv()
        for cp in first + passed:
            cp.wait_send()

    return pl.pallas_call(
        body, name="gather_weights", in_specs=[ANY] * n_w, out_specs=[ANY] * n_w,
        out_shape=[jax.ShapeDtypeStruct(b.shape, b.dtype) for b in bufs],
        input_output_aliases={w: w for w in range(n_w)},
        scratch_shapes=[pltpu.SemaphoreType.DMA((6 * n_w,)), pltpu.SemaphoreType.DMA((6 * n_w,))],
    )(*bufs)


def _grad_half(ref, col, h):
    return _half0(ref, h) if col else _half1(ref, h)


def _swap_halves_with_sibling(grads, col_fam):
    n_w = len(grads)

    def body(*refs):
        ins, outs = refs[:n_w], refs[n_w:2 * n_w]
        sems = refs[2 * n_w:]
        x, y, c = _me()
        sib = (x, y, 1 - c)
        cps = [_remote(_grad_half(ins[w], col_fam[w], 1 - c), outs[w], sems, w, sib) for w in range(n_w)]
        for cp in cps:
            cp.start()
        for cp in cps:
            cp.wait_recv()
        for cp in cps:
            cp.wait_send()

    out_shape = []
    for w, g in enumerate(grads):
        shp = (g.shape[0] // 2, g.shape[1]) if col_fam[w] else (g.shape[0], g.shape[1] // 2, g.shape[2])
        out_shape.append(jax.ShapeDtypeStruct(shp, g.dtype))
    return pl.pallas_call(
        body, name="grad_swap_sibling", in_specs=[ANY] * n_w, out_specs=[ANY] * n_w, out_shape=out_shape,
        scratch_shapes=[pltpu.SemaphoreType.DMA((n_w,)), pltpu.SemaphoreType.DMA((n_w,))],
    )(*grads)


def _half_add(mine, recv, c_arr, col, name):
    if col:
        rows, n = recv.shape
        tr = rows // 2
        grid = (2,)
        in_specs = [pl.BlockSpec((tr, n), lambda i, c: (2 * c[0] + i, 0)), pl.BlockSpec((tr, n), lambda i, c: (i, 0))]
        out_spec = pl.BlockSpec((tr, n), lambda i, c: (i, 0))
    else:
        _, rows, n = recv.shape
        grid = (N_CHIPS,)
        in_specs = [pl.BlockSpec((None, rows, n), lambda q, c: (q, c[0], 0)),
                    pl.BlockSpec((None, rows, n), lambda q, c: (q, 0, 0))]
        out_spec = pl.BlockSpec((None, rows, n), lambda q, c: (q, 0, 0))

    def body(c_ref, a_ref, b_ref, o_ref):
        o_ref[...] = (a_ref[...].astype(F32) + b_ref[...].astype(F32)).astype(BF16)

    return pl.pallas_call(
        body, name=name,
        grid_spec=pltpu.PrefetchScalarGridSpec(num_scalar_prefetch=1, grid=grid, in_specs=in_specs, out_specs=out_spec),
        out_shape=jax.ShapeDtypeStruct(recv.shape, BF16), compiler_params=_cp(),
    )(c_arr, mine, recv)


def _scatter_chip_sums(sums, col_fam):
    n_w = len(sums)

    def body(*refs):
        ins, outs = refs[:n_w], refs[n_w:2 * n_w]
        sems = refs[2 * n_w:2 * n_w + 2]
        lsem = refs[2 * n_w + 2]
        x, y, c = _me()
        myq = 2 * x + y

        def slab(w, q):
            if col_fam[w]:
                return _col_window(ins[w], q, ins[w].shape[1] // N_CHIPS)
            return ins[w].at[q]

        local = [pltpu.make_async_copy(slab(w, myq), outs[w].at[myq], lsem.at[w]) for w in range(n_w)]
        for cp in local:
            cp.start()
        cps = []
        for w in range(n_w):
            for j in (1, 2, 3):
                px, py, pq = _chip_peer(x, y, j)
                cp = _remote(slab(w, pq), outs[w].at[myq], sems, w * 3 + j - 1, (px, py, c))
                cp.start()
                cps.append(cp)
        for w in range(n_w):
            for j in (1, 2, 3):
                _, _, pq = _chip_peer(x, y, j)
                land = outs[w].at[pq]
                _remote(land, land, sems, w * 3 + j - 1, (x, y, c)).wait_recv()
        for cp in cps:
            cp.wait_send()
        for cp in local:
            cp.wait()

    out_shape = []
    for w, s in enumerate(sums):
        shp = (s.shape[0], s.shape[1] // N_CHIPS) if col_fam[w] else s.shape[1:]
        out_shape.append(jax.ShapeDtypeStruct((N_CHIPS,) + shp, s.dtype))
    return pl.pallas_call(
        body, name="grad_scatter_chips", in_specs=[ANY] * n_w, out_specs=[ANY] * n_w, out_shape=out_shape,
        scratch_shapes=[pltpu.SemaphoreType.DMA((3 * n_w,)), pltpu.SemaphoreType.DMA((3 * n_w,)),
                        pltpu.SemaphoreType.DMA((n_w,))],
    )(*sums)


def _sum_chips(parts, c_arr, prev, lead, shape, name):
    _, rows, n = parts.shape
    tr = rows // 2 if rows % 32 == 0 else rows
    nblk = rows // tr

    def body(c_ref, p_ref, *rest):
        o_ref = rest[-1]
        acc = p_ref[0].astype(F32)
        for q in range(1, N_CHIPS):
            acc = acc + p_ref[q].astype(F32)
        o_ref[...] = acc

    in_specs = [pl.BlockSpec((N_CHIPS, tr, n), lambda i, c: (0, i, 0))]
    args = [c_arr, parts]
    aliases = {}
    if prev is not None:
        in_specs.append(ANY)
        args.append(prev)
        aliases = {2: 0}
    return pl.pallas_call(
        body, name=name,
        grid_spec=pltpu.PrefetchScalarGridSpec(
            num_scalar_prefetch=1, grid=(nblk,), in_specs=in_specs,
            out_specs=pl.BlockSpec((None, tr, n), lambda i, c: (lead, c[0] * nblk + i, 0))),
        out_shape=jax.ShapeDtypeStruct(shape, F32), input_output_aliases=aliases, compiler_params=_cp(),
    )(*args)


def _join_halves(bufs, place):
    n_o = len(bufs)
    n_h = len(place)

    def body(*refs):
        outs = refs[n_o:2 * n_o]
        sems = refs[2 * n_o:2 * n_o + 2]
        x, y, c = _me()
        sib = (x, y, 1 - c)

        def half(k, h):
            o, lead = place[k]
            return _half_rows(outs[o].at[lead], h)

        cps = [_remote(half(k, c), half(k, c), sems, k, sib) for k in range(n_h)]
        for cp in cps:
            cp.start()
        for k in range(n_h):
            land = half(k, 1 - c)
            _remote(land, land, sems, k, sib).wait_recv()
        for cp in cps:
            cp.wait_send()

    return pl.pallas_call(
        body, name="grad_join_sibling", in_specs=[ANY] * n_o, out_specs=[ANY] * n_o,
        out_shape=[jax.ShapeDtypeStruct(b.shape, b.dtype) for b in bufs],
        input_output_aliases={k: k for k in range(n_o)},
        scratch_shapes=[pltpu.SemaphoreType.DMA((n_h,)), pltpu.SemaphoreType.DMA((n_h,))],
    )(*bufs)


def _allreduce_rows(rows):
    n_dev = 8
    n_r = len(rows)
    assert n_r <= 8

    def body(*refs):
        r_refs = refs[:n_r]
        o_ref, slots, send_sems, recv_sems = refs[n_r:]
        x, y, c = _me()
        me = 4 * x + 2 * y + c
        slots[me] = jnp.concatenate([r[...] for r in r_refs] + [jnp.zeros((8 - n_r, D_MODEL), F32)], axis=0)

        def peer(k):
            return (1 - x if k & 4 else x, 1 - y if k & 2 else y, 1 - c if k & 1 else c)

        cps = []
        for k in range(1, n_dev):
            cp = pltpu.make_async_remote_copy(src_ref=slots.at[me], dst_ref=slots.at[me], send_sem=send_sems.at[k - 1],
                                              recv_sem=recv_sems.at[k - 1], device_id=peer(k), device_id_type=MESH)
            cp.start()
            cps.append(cp)
        for k in range(1, n_dev):
            px, py, pc = peer(k)
            land = slots.at[4 * px + 2 * py + pc]
            pltpu.make_async_remote_copy(src_ref=land, dst_ref=land, send_sem=send_sems.at[k - 1],
                                         recv_sem=recv_sems.at[k - 1], device_id=peer(k),
                                         device_id_type=MESH).wait_recv()
        for cp in cps:
            cp.wait_send()
        acc = slots[0]
        for d in range(1, n_dev):
            acc = acc + slots[d]
        o_ref[...] = acc

    vm = pl.BlockSpec(memory_space=pltpu.VMEM)
    return pl.pallas_call(
        body, name="allreduce_rows", in_specs=[vm] * n_r, out_specs=vm,
        out_shape=jax.ShapeDtypeStruct((8, D_MODEL), F32),
        scratch_shapes=[pltpu.VMEM((n_dev, 8, D_MODEL), F32), pltpu.SemaphoreType.DMA((n_dev - 1,)),
                        pltpu.SemaphoreType.DMA((n_dev - 1,))],
    )(*rows)


def _adamw(w, g, m, v, name):
    shape = w.shape
    if len(shape) == 1:
        lead, rows, cols = 1, 1, shape[0]
    else:
        rows, cols = shape[-2:]
        lead = math.prod(shape[:-2])
    args = [a.reshape(lead, rows, cols) for a in (w, g, m, v)]
    tr = rows // 2 if rows % 16 == 0 else rows

    def body(w_ref, g_ref, m_ref, v_ref, d_ref, nm_ref, nv_ref):
        gv = g_ref[...]
        nm = ADAM_B1 * m_ref[...] + (1.0 - ADAM_B1) * gv
        nv = ADAM_B2 * v_ref[...] + (1.0 - ADAM_B2) * jnp.square(gv)
        m_hat = nm / (1.0 - ADAM_B1 ** ADAM_STEP)
        v_hat = nv / (1.0 - ADAM_B2 ** ADAM_STEP)
        d_ref[...] = -ADAM_LR * (m_hat / (jnp.sqrt(v_hat) + ADAM_EPS) + ADAM_WD * w_ref[...])
        nm_ref[...] = nm
        nv_ref[...] = nv

    spec = pl.BlockSpec((None, tr, cols), lambda l, i: (l, i, 0))
    outs = pl.pallas_call(
        body, name=name, grid=(lead, rows // tr), in_specs=[spec] * 4, out_specs=[spec] * 3,
        out_shape=[jax.ShapeDtypeStruct((lead, rows, cols), F32)] * 3, compiler_params=_cp(),
    )(*args)
    return [o.reshape(shape) for o in outs]


def kernel(x, a_w_in, a_sink, a_w_out, b_w_in, b_w_out, norm_mix, norm_ffn, w_gate, w_up, w_down, final_norm, loss_target, m_a_w_in, m_a_sink, m_a_w_out, m_b_w_in, m_b_w_out, m_norm_mix, m_norm_ffn, m_w_gate, m_w_up, m_w_down, m_final_norm, v_a_w_in, v_a_sink, v_a_w_out, v_b_w_in, v_b_w_out, v_norm_mix, v_norm_ffn, v_w_gate, v_w_up, v_w_down, v_final_norm):
    weights = dict(a_w_in=a_w_in, a_sink=a_sink, a_w_out=a_w_out, b_w_in=b_w_in, b_w_out=b_w_out, norm_mix=norm_mix,
                   norm_ffn=norm_ffn, w_gate=w_gate, w_up=w_up, w_down=w_down, final_norm=final_norm)
    mom = dict(a_w_in=m_a_w_in, a_sink=m_a_sink, a_w_out=m_a_w_out, b_w_in=m_b_w_in, b_w_out=m_b_w_out,
               norm_mix=m_norm_mix, norm_ffn=m_norm_ffn, w_gate=m_w_gate, w_up=m_w_up, w_down=m_w_down,
               final_norm=m_final_norm)
    var = dict(a_w_in=v_a_w_in, a_sink=v_a_sink, a_w_out=v_a_w_out, b_w_in=v_b_w_in, b_w_out=v_b_w_out,
               norm_mix=v_norm_mix, norm_ffn=v_norm_ffn, w_gate=v_w_gate, w_up=v_w_up, w_down=v_w_down,
               final_norm=v_final_norm)
    order = ["a_w_in", "a_sink", "a_w_out", "b_w_in", "b_w_out", "norm_mix", "norm_ffn", "w_gate", "w_up", "w_down",
             "final_norm"]

    c_arr = lax.axis_index("c").astype(jnp.int32).reshape(1)
    q_arr = (2 * lax.axis_index("x") + lax.axis_index("y")).astype(jnp.int32).reshape(1)
    shards = [a_w_in, a_w_out, b_w_in, b_w_out, w_gate, w_up, w_down]
    shard_names = ("a_in", "a_out", "b_in", "b_out", "wg", "wu", "wd")
    placed = [_place_shard(s, q_arr, col, f"place_{nm}")
              for s, col, nm in zip(shards, (True, False, True, False, False, False, False), shard_names)]
    a_in, a_out, b_in, b_out, wg, wu, wd = _gather_weights(placed)
    a_out = a_out.reshape(D_MODEL, D_MODEL)
    b_out = b_out.reshape(D_MODEL, D_MODEL)

    gx, grads, vecs = _local_step(x, loss_target, a_in, a_sink[0], a_out, b_in, b_out, norm_mix, norm_ffn, wg, wu, wd,
                                  final_norm)

    rows_out = D_MODEL // N_CHIPS
    partials = [grads["a_in"], grads["b_in"],
                grads["a_out"].reshape(N_CHIPS, rows_out, D_MODEL), grads["b_out"].reshape(N_CHIPS, rows_out, D_MODEL),
                grads["wg"][0], grads["wg"][1], grads["wu"][0], grads["wu"][1], grads["wd"][0], grads["wd"][1]]
    col_fam = (True, True) + (False,) * 8
    names = ("a_in", "b_in", "a_out", "b_out", "wg0", "wg1", "wu0", "wu1", "wd0", "wd1")
    theirs = _swap_halves_with_sibling(partials, col_fam)
    sums = [_half_add(p, r, c_arr, cf, f"chip_sum_{nm}") for p, r, cf, nm in zip(partials, theirs, col_fam, names)]
    contrib = _scatter_chip_sums(sums, col_fam)
    shapes = [a_w_in.shape, b_w_in.shape, a_w_out.shape, b_w_out.shape, w_gate.shape, w_up.shape, w_gate.shape]
    place = [(0, 0), (1, 0), (2, 0), (3, 0), (4, 0), (4, 1), (5, 0), (5, 1), (6, 0), (6, 1)]
    bufs = [None] * len(shapes)
    for p, nm, (o, lead) in zip(contrib, names, place):
        bufs[o] = _sum_chips(p, c_arr, bufs[o], lead, shapes[o], f"sum_chips_{nm}")
    g_a_in, g_b_in, g_a_out, g_b_out, g_wg, g_wu, g_wdt = _join_halves(bufs, place)
    g_wd = g_wdt.transpose(0, 2, 1)

    sink_row = jnp.pad(vecs["sink"][0:1], ((0, 0), (0, D_MODEL - LANES)))
    tot = _allreduce_rows([vecs["norm_mix"][0], vecs["norm_mix"][1], vecs["norm_ffn"][0], vecs["norm_ffn"][1],
                           vecs["final"], vecs["loss_cols"], sink_row])
    loss = (0.5 / D_MODEL) * jnp.sum(tot[5])
    gw = dict(a_w_in=g_a_in, a_sink=tot[6:7, :N_HEADS], a_w_out=g_a_out, b_w_in=g_b_in, b_w_out=g_b_out,
              norm_mix=tot[0:2], norm_ffn=tot[2:4], w_gate=g_wg, w_up=g_wu, w_down=g_wd, final_norm=tot[4])

    delta, new_m, new_v = {}, {}, {}
    for n in order:
        delta[n], new_m[n], new_v[n] = _adamw(weights[n], gw[n], mom[n], var[n], f"adamw_{n}")
    return (loss, gx, *[gw[n] for n in order], *[delta[n] for n in order], *[new_m[n] for n in order],
            *[new_v[n] for n in order])
```

```python
import functools
import math

import jax
import jax.numpy as jnp
from jax import lax
from jax.experimental import pallas as pl
from jax.experimental.pallas import tpu as pltpu
from jax.experimental.pallas import tpu_sc as plsc

F32 = jnp.float32
BF16 = jnp.bfloat16

D_MODEL = 1024
HEAD_DIM = 64
N_HEADS = 16
N_KV = 4
QKV_W = 1536
D_FF = 2816
N_CHIPS = 4
FF_SH = D_FF // N_CHIPS
HALF_WINDOW_A = 128
DILATED = ((128, 1), (512, 4), (2048, 16))
ROPE_THETA = 10000.0
RMS_EPS = 1e-6
NEG_INF = -1e30
LANES = 128
ADAM_LR, ADAM_B1, ADAM_B2, ADAM_EPS, ADAM_WD, ADAM_STEP = 0.001, 0.9, 0.999, 1e-08, 0.01, 10
VMEM_LIMIT = 56 * 1024 * 1024
GRAD_TOKENS = 2048
MESH = pl.DeviceIdType.MESH


def _cp(**kw):
    return pltpu.CompilerParams(vmem_limit_bytes=VMEM_LIMIT, **kw)


def _row_tile(t, cap):
    tm = min(cap, t)
    assert t % tm == 0
    return tm


def _rope_tables(seq, dil):
    inv = 1.0 / (ROPE_THETA ** (jnp.arange(0, HEAD_DIM, 2, dtype=F32) / HEAD_DIM))
    ang = jnp.arange(seq, dtype=F32)[:, None] * inv[None, :]
    cos, sin = jnp.cos(ang), jnp.sin(ang)
    cos = jnp.tile(cos, (1, 4))
    sin = jnp.concatenate([-sin, sin, -sin, sin], axis=1)

    def perm(t):
        return t.reshape(seq // dil, dil, LANES).transpose(1, 0, 2).reshape(seq, LANES)

    return perm(cos), perm(sin)


def _swap_halves(t):
    lane = lax.broadcasted_iota(jnp.int32, t.shape, 1)
    return jnp.where((lane % HEAD_DIM) < HEAD_DIM // 2, pltpu.roll(t, LANES - 32, 1), pltpu.roll(t, 32, 1))


def _rope(t, cos, sin):
    return t * cos + _swap_halves(t) * sin


def _rope_t(t, cos, sin):
    return t * cos - _swap_halves(t) * sin


def _to_residue(t, batch, dil):
    if dil == 1:
        return t
    s = t.shape[0] // batch
    return t.reshape(batch, s // dil, dil, t.shape[1]).transpose(0, 2, 1, 3).reshape(t.shape)


def _from_residue(t, batch, dil):
    if dil == 1:
        return t
    s = t.shape[0] // batch
    return t.reshape(batch, dil, s // dil, t.shape[1]).transpose(0, 2, 1, 3).reshape(t.shape)


def _rms_fwd(x, w, name, with_t=False):
    t = x.shape[0]
    tm = _row_tile(t, 512)

    def body(x_ref, w_ref, o_ref, *ot_ref):
        xv = x_ref[...]
        r = lax.rsqrt(jnp.mean(xv * xv, axis=-1, keepdims=True) + RMS_EPS)
        y = (xv * r) * w_ref[...]
        o_ref[...] = y.astype(BF16)
        if with_t:
            ot_ref[0][...] = y.T.astype(BF16)

    out_specs = [pl.BlockSpec((tm, D_MODEL), lambda i: (i, 0))]
    out_shape = [jax.ShapeDtypeStruct((t, D_MODEL), BF16)]
    if with_t:
        out_specs.append(pl.BlockSpec((D_MODEL, tm), lambda i: (0, i)))
        out_shape.append(jax.ShapeDtypeStruct((D_MODEL, t), BF16))
    outs = pl.pallas_call(
        body, name=name, grid=(t // tm,),
        in_specs=[pl.BlockSpec((tm, D_MODEL), lambda i: (i, 0)), pl.BlockSpec((1, D_MODEL), lambda i: (0, 0))],
        out_specs=out_specs, out_shape=out_shape, compiler_params=_cp(),
    )(x, w)
    return outs if with_t else outs[0]


def _rms_bwd(x, w, dhs, dres, name, with_t=False):
    t = x.shape[0]
    tm = _row_tile(t, 512)
    n = len(dhs)

    def body(*refs):
        x_ref, w_ref = refs[0], refs[1]
        dh_refs = refs[2:2 + n]
        dres_ref = refs[2 + n]
        dx_ref, dxb_ref = refs[3 + n:5 + n]
        dw_ref = refs[-1]
        xv = x_ref[...]
        r = lax.rsqrt(jnp.mean(xv * xv, axis=-1, keepdims=True) + RMS_EPS)
        xh = xv * r
        dy = dh_refs[0][...]
        for k in range(1, n):
            dy = dy + dh_refs[k][...]
        dxh = dy * w_ref[...]
        dx = dres_ref[...] + r * (dxh - xh * jnp.mean(dxh * xh, axis=-1, keepdims=True))
        dx_ref[...] = dx
        dxb_ref[...] = dx.astype(BF16)
        if with_t:
            refs[5 + n][...] = dx.T.astype(BF16)

        @pl.when(pl.program_id(0) == 0)
        def _():
            dw_ref[...] = jnp.zeros_like(dw_ref)

        dw_ref[...] += jnp.sum(dy * xh, axis=0, keepdims=True)

    row = pl.BlockSpec((tm, D_MODEL), lambda i: (i, 0))
    vec = pl.BlockSpec((1, D_MODEL), lambda i: (0, 0))
    out_specs = [row, row]
    out_shape = [jax.ShapeDtypeStruct((t, D_MODEL), F32), jax.ShapeDtypeStruct((t, D_MODEL), BF16)]
    if with_t:
        out_specs.append(pl.BlockSpec((D_MODEL, tm), lambda i: (0, i)))
        out_shape.append(jax.ShapeDtypeStruct((D_MODEL, t), BF16))
    return pl.pallas_call(
        body, name=name, grid=(t // tm,),
        in_specs=[row, vec] + [row] * n + [row],
        out_specs=out_specs + [vec], out_shape=out_shape + [jax.ShapeDtypeStruct((1, D_MODEL), F32)],
        compiler_params=_cp(),
    )(x, w, *dhs, dres)


def _final_loss(x, w, target, name):
    t = x.shape[0]
    tm = _row_tile(t, 512)

    def body(x_ref, w_ref, t_ref, dx_ref, dxb_ref, dxt_ref, l_ref, dw_ref):
        xv = x_ref[...]
        r = lax.rsqrt(jnp.mean(xv * xv, axis=-1, keepdims=True) + RMS_EPS)
        xh = xv * r
        err = xh * w_ref[...] - t_ref[...]
        dy = err * (1.0 / D_MODEL)
        dxh = dy * w_ref[...]
        dx = r * (dxh - xh * jnp.mean(dxh * xh, axis=-1, keepdims=True))
        dx_ref[...] = dx
        dxb_ref[...] = dx.astype(BF16)
        dxt_ref[...] = dx.T.astype(BF16)

        @pl.when(pl.program_id(0) == 0)
        def _():
            l_ref[...] = jnp.zeros_like(l_ref)
            dw_ref[...] = jnp.zeros_like(dw_ref)

        l_ref[...] += jnp.sum(err * err, axis=0, keepdims=True)
        dw_ref[...] += jnp.sum(dy * xh, axis=0, keepdims=True)

    row = pl.BlockSpec((tm, D_MODEL), lambda i: (i, 0))
    vec = pl.BlockSpec((1, D_MODEL), lambda i: (0, 0))
    return pl.pallas_call(
        body, name=name, grid=(t // tm,),
        in_specs=[row, vec, row], out_specs=[row, row, pl.BlockSpec((D_MODEL, tm), lambda i: (0, i)), vec, vec],
        out_shape=[jax.ShapeDtypeStruct((t, D_MODEL), F32), jax.ShapeDtypeStruct((t, D_MODEL), BF16),
                   jax.ShapeDtypeStruct((D_MODEL, t), BF16),
                   jax.ShapeDtypeStruct((1, D_MODEL), F32), jax.ShapeDtypeStruct((1, D_MODEL), F32)],
        compiler_params=_cp(),
    )(x, w, target)


def _qkv_proj(h, w, cos, sin, group, name):
    t = h.shape[0]
    seq = cos.shape[0]
    tm = _row_tile(seq, 1024)
    n_q = N_HEADS * HEAD_DIM // LANES
    n_rope = (N_HEADS + N_KV) * HEAD_DIM // LANES
    scale = 1.0 / math.sqrt(HEAD_DIM)

    def body(h_ref, w_ref, cos_ref, sin_ref, o_ref):
        acc = jnp.dot(h_ref[...], w_ref[...], preferred_element_type=F32)
        cs, sn = cos_ref[...], sin_ref[...]
        csq, snq = cs * scale, sn * scale
        for c in range(QKV_W // LANES):
            blk = acc[:, c * LANES:(c + 1) * LANES]
            if c < n_q:
                blk = _rope(blk, csq, snq)
            elif c < n_rope:
                blk = _rope(blk, cs, sn)
            o_ref[:, c * LANES:(c + 1) * LANES] = blk.astype(BF16)

    tab = pl.BlockSpec((tm, LANES), lambda i: (i % (seq // tm), 0))
    return pl.pallas_call(
        body, name=name, grid=(t // tm,),
        in_specs=[pl.BlockSpec((tm, D_MODEL), lambda i: (i, 0)),
                  pl.BlockSpec((D_MODEL, QKV_W), lambda i: (0, group)), tab, tab],
        out_specs=pl.BlockSpec((tm, QKV_W), lambda i: (i, 0)),
        out_shape=jax.ShapeDtypeStruct((t, QKV_W), BF16), compiler_params=_cp(),
    )(h, w, cos, sin)


def _mm_res(a, w, res, name):
    t, k = a.shape
    tm = _row_tile(t, 1024)

    def body(a_ref, w_ref, r_ref, o_ref):
        o_ref[...] = r_ref[...] + jnp.dot(a_ref[...], w_ref[...], preferred_element_type=F32)

    return pl.pallas_call(
        body, name=name, grid=(t // tm,),
        in_specs=[pl.BlockSpec((tm, k), lambda i: (i, 0)), pl.BlockSpec((k, D_MODEL), lambda i: (0, 0)),
                  pl.BlockSpec((tm, D_MODEL), lambda i: (i, 0))],
        out_specs=pl.BlockSpec((tm, D_MODEL), lambda i: (i, 0)),
        out_shape=jax.ShapeDtypeStruct((t, D_MODEL), F32), compiler_params=_cp(),
    )(a, w, res)


def _mm_nt(dy, w, group, out_dtype, name):
    t, n = dy.shape
    k = w.shape[0]
    tm = _row_tile(t, 1024)

    def body(dy_ref, w_ref, o_ref):
        o_ref[...] = lax.dot_general(dy_ref[...], w_ref[...], (((1,), (1,)), ((), ())),
                                     preferred_element_type=F32).astype(out_dtype)

    return pl.pallas_call(
        body, name=name, grid=(t // tm,),
        in_specs=[pl.BlockSpec((tm, n), lambda i: (i, 0)), pl.BlockSpec((k, n), lambda i: (0, group))],
        out_specs=pl.BlockSpec((tm, k), lambda i: (i, 0)),
        out_shape=jax.ShapeDtypeStruct((t, k), out_dtype), compiler_params=_cp(),
    )(dy, w)


def _out_bwd(dx, w, o, name):
    t = dx.shape[0]
    tm = _row_tile(t, 512)

    def body(dx_ref, w_ref, o_ref, et_ref, do_ref, adj_ref):
        do = lax.dot_general(dx_ref[...], w_ref[...], (((1,), (1,)), ((), ())), preferred_element_type=F32)
        do_ref[...] = do.astype(BF16)
        adj_ref[...] = -_dot_split(do * o_ref[...].astype(F32), et_ref[...])

    row = pl.BlockSpec((tm, D_MODEL), lambda i: (i, 0))
    return pl.pallas_call(
        body, name=name, grid=(t // tm,),
        in_specs=[row, pl.BlockSpec((D_MODEL, D_MODEL), lambda i: (0, 0)), row,
                  pl.BlockSpec((D_MODEL, LANES), lambda i: (0, 0))],
        out_specs=[row, pl.BlockSpec((tm, LANES), lambda i: (i, 0))],
        out_shape=[jax.ShapeDtypeStruct((t, D_MODEL), BF16), jax.ShapeDtypeStruct((t, LANES), F32)],
        compiler_params=_cp(),
    )(dx, w, o, _head_expander().T)


def _mm_tn(a, bs, name):
    aq = a.ndim == 3
    bq = bs[0].ndim == 3
    t, ka = a.shape[-2:]
    n = bs[0].shape[-1]
    nq = N_CHIPS if (aq or bq) else 1
    tt = _row_tile(t, GRAD_TOKENS)
    tn = n if n <= 1024 else 768
    assert n % tn == 0
    nb = len(bs)
    steps = t // tt

    def body(*refs):
        a_ref = refs[0]
        b_refs = refs[1:1 + nb]
        o_refs = refs[1 + nb:1 + 2 * nb]
        acc_refs = refs[1 + 2 * nb:]
        s = pl.program_id(2)
        av = a_ref[...]
        for b_ref, o_ref, acc_ref in zip(b_refs, o_refs, acc_refs):
            @pl.when(s == 0)
            def _():
                acc_ref[...] = jnp.zeros_like(acc_ref)

            acc_ref[...] += lax.dot_general(av, b_ref[...], (((0,), (0,)), ((), ())), preferred_element_type=F32)

            @pl.when(s == steps - 1)
            def _():
                o_ref[...] = acc_ref[...].astype(BF16)

    a_spec = (pl.BlockSpec((None, tt, ka), lambda q, j, s: (q, s, 0)) if aq
              else pl.BlockSpec((tt, ka), lambda q, j, s: (s, 0)))
    b_spec = (pl.BlockSpec((None, tt, tn), lambda q, j, s: (q, s, j)) if bq
              else pl.BlockSpec((tt, tn), lambda q, j, s: (s, j)))
    if nq > 1:
        o_spec = pl.BlockSpec((None, ka, tn), lambda q, j, s: (q, 0, j))
        o_shape = jax.ShapeDtypeStruct((nq, ka, n), BF16)
    else:
        o_spec = pl.BlockSpec((ka, tn), lambda q, j, s: (0, j))
        o_shape = jax.ShapeDtypeStruct((ka, n), BF16)
    outs = pl.pallas_call(
        body, name=name, grid=(nq, n // tn, steps),
        in_specs=[a_spec] + [b_spec] * nb, out_specs=[o_spec] * nb, out_shape=[o_shape] * nb,
        scratch_shapes=[pltpu.VMEM((ka, tn), F32)] * nb, compiler_params=_cp(),
    )(a, *bs)
    return outs


def _mm_grad(at, bs, name):
    ka, t = at.shape
    bq = bs[0].ndim == 3
    n = bs[0].shape[-1]
    nq = N_CHIPS if bq else 1
    tt = _row_tile(t, GRAD_TOKENS)
    tn = n if n <= 1024 else 768
    assert n % tn == 0
    nb = len(bs)
    steps = t // tt

    def body(*refs):
        a_ref = refs[0]
        b_refs = refs[1:1 + nb]
        o_refs = refs[1 + nb:1 + 2 * nb]
        acc_refs = refs[1 + 2 * nb:]
        s = pl.program_id(2)
        av = a_ref[...]
        for b_ref, o_ref, acc_ref in zip(b_refs, o_refs, acc_refs):
            @pl.when(s == 0)
            def _():
                acc_ref[...] = jnp.zeros_like(acc_ref)

            acc_ref[...] += jnp.dot(av, b_ref[...], preferred_element_type=F32)

            @pl.when(s == steps - 1)
            def _():
                o_ref[...] = acc_ref[...].astype(BF16)

    a_spec = pl.BlockSpec((ka, tt), lambda q, j, s: (0, s))
    if bq:
        b_spec = pl.BlockSpec((None, tt, tn), lambda q, j, s: (q, s, j))
        o_spec = pl.BlockSpec((None, ka, tn), lambda q, j, s: (q, 0, j))
        o_shape = jax.ShapeDtypeStruct((nq, ka, n), BF16)
    else:
        b_spec = pl.BlockSpec((tt, tn), lambda q, j, s: (s, j))
        o_spec = pl.BlockSpec((ka, tn), lambda q, j, s: (0, j))
        o_shape = jax.ShapeDtypeStruct((ka, n), BF16)
    return pl.pallas_call(
        body, name=name, grid=(nq, n // tn, steps),
        in_specs=[a_spec] + [b_spec] * nb, out_specs=[o_spec] * nb, out_shape=[o_shape] * nb,
        scratch_shapes=[pltpu.VMEM((ka, tn), F32)] * nb, compiler_params=_cp(),
    )(at, *bs)


def _sigmoid(x):
    return 1.0 / (1.0 + jnp.exp(-x))


def _ffn_up(h, wg, wu, layer, name):
    t = h.shape[0]
    tm = _row_tile(t, 1024)

    def body(h_ref, wg_ref, wu_ref, a_ref, dg_ref, du_ref):
        hv = h_ref[...]
        g = jnp.dot(hv, wg_ref[...], preferred_element_type=F32)
        u = jnp.dot(hv, wu_ref[...], preferred_element_type=F32)
        sg = _sigmoid(g)
        silu = g * sg
        a_ref[...] = (silu * u).astype(BF16)
        dg_ref[...] = (sg * (1.0 + g * (1.0 - sg)) * u).astype(BF16)
        du_ref[...] = silu.astype(BF16)

    wspec = pl.BlockSpec((None, None, D_MODEL, FF_SH), lambda q, i: (q, layer, 0, 0))
    ospec = pl.BlockSpec((None, tm, FF_SH), lambda q, i: (q, i, 0))
    oshape = jax.ShapeDtypeStruct((N_CHIPS, t, FF_SH), BF16)
    return pl.pallas_call(
        body, name=name, grid=(N_CHIPS, t // tm),
        in_specs=[pl.BlockSpec((tm, D_MODEL), lambda q, i: (i, 0)), wspec, wspec],
        out_specs=[ospec] * 3, out_shape=[oshape] * 3, compiler_params=_cp(),
    )(h, wg, wu)


def _ffn_down(a, wd, res, layer, name):
    t = a.shape[1]
    tm = _row_tile(t, 512)

    def body(a_ref, w_ref, r_ref, o_ref):
        acc = r_ref[...]
        for q in range(N_CHIPS):
            acc = acc + jnp.dot(a_ref[q], w_ref[q], preferred_element_type=F32)
        o_ref[...] = acc

    return pl.pallas_call(
        body, name=name, grid=(t // tm,),
        in_specs=[pl.BlockSpec((N_CHIPS, tm, FF_SH), lambda i: (0, i, 0)),
                  pl.BlockSpec((N_CHIPS, None, FF_SH, D_MODEL), lambda i: (0, layer, 0, 0)),
                  pl.BlockSpec((tm, D_MODEL), lambda i: (i, 0))],
        out_specs=pl.BlockSpec((tm, D_MODEL), lambda i: (i, 0)),
        out_shape=jax.ShapeDtypeStruct((t, D_MODEL), F32), compiler_params=_cp(),
    )(a, wd, res)


def _ffn_down_bwd(dx, wd, fg, fu, layer, name):
    t = dx.shape[0]
    tm = _row_tile(t, 1024)

    def body(dx_ref, w_ref, fg_ref, fu_ref, dg_ref, du_ref):
        da = lax.dot_general(dx_ref[...], w_ref[...], (((1,), (1,)), ((), ())), preferred_element_type=F32)
        dg_ref[...] = (da * fg_ref[...].astype(F32)).astype(BF16)
        du_ref[...] = (da * fu_ref[...].astype(F32)).astype(BF16)

    aspec = pl.BlockSpec((None, tm, FF_SH), lambda q, i: (q, i, 0))
    oshape = jax.ShapeDtypeStruct((N_CHIPS, t, FF_SH), BF16)
    return pl.pallas_call(
        body, name=name, grid=(N_CHIPS, t // tm),
        in_specs=[pl.BlockSpec((tm, D_MODEL), lambda q, i: (i, 0)),
                  pl.BlockSpec((None, None, FF_SH, D_MODEL), lambda q, i: (q, layer, 0, 0)), aspec, aspec],
        out_specs=[aspec] * 2, out_shape=[oshape] * 2, compiler_params=_cp(),
    )(dx, wd, fg, fu)


def _ffn_up_bwd(dg, du, wg, wu, layer, name):
    t = dg.shape[1]
    tm = _row_tile(t, 512)
    nt = (((1,), (1,)), ((), ()))

    def body(dg_ref, du_ref, wg_ref, wu_ref, o_ref):
        acc = jnp.zeros((tm, D_MODEL), F32)
        for q in range(N_CHIPS):
            acc = acc + lax.dot_general(dg_ref[q], wg_ref[q], nt, preferred_element_type=F32)
            acc = acc + lax.dot_general(du_ref[q], wu_ref[q], nt, preferred_element_type=F32)
        o_ref[...] = acc

    aspec = pl.BlockSpec((N_CHIPS, tm, FF_SH), lambda i: (0, i, 0))
    wspec = pl.BlockSpec((N_CHIPS, None, D_MODEL, FF_SH), lambda i: (0, layer, 0, 0))
    return pl.pallas_call(
        body, name=name, grid=(t // tm,),
        in_specs=[aspec, aspec, wspec, wspec],
        out_specs=pl.BlockSpec((tm, D_MODEL), lambda i: (i, 0)),
        out_shape=jax.ShapeDtypeStruct((t, D_MODEL), F32), compiler_params=_cp(),
    )(dg, du, wg, wu)


def _attn_geometry(length, half_window):
    qb = min(LANES, length)
    kw = min(qb + 2 * half_window, length)
    return qb, kw, length // qb


def _dup_kv(src_ref, dst_ref, s, length):
    ch = min(length, 256)
    lo = lax.broadcasted_iota(jnp.int32, (ch, LANES), 1) < HEAD_DIM

    def chunk(c, carry):
        r0 = pl.multiple_of(c * ch, ch)
        for j in range(N_KV // 2):
            tile = src_ref[s, pl.ds(r0, ch), j * LANES:(j + 1) * LANES].astype(F32)
            rolled = pltpu.roll(tile, HEAD_DIM, 1)
            dst_ref[2 * j, pl.ds(r0, ch), :] = jnp.where(lo, tile, rolled).astype(BF16)
            dst_ref[2 * j + 1, pl.ds(r0, ch), :] = jnp.where(lo, rolled, tile).astype(BF16)
        return carry

    lax.fori_loop(0, length // ch, chunk, 0)


def _stack_heads(ref, s, q0, qb, g):
    lo = lax.broadcasted_iota(jnp.int32, (qb, LANES), 1) < HEAD_DIM
    parts = []
    for a in range(4):
        col = (2 * g + a // 2) * LANES
        tile = ref[s, pl.ds(q0, qb), col:col + LANES]
        keep = lo if a % 2 == 0 else jnp.logical_not(lo)
        parts.append(jnp.where(keep, tile, jnp.zeros_like(tile)))
    return jnp.concatenate(parts, axis=0)


def _unstack_pair_t(stacked_t, qb, pair):
    lo = lax.broadcasted_iota(jnp.int32, (LANES, qb), 0) < HEAD_DIM
    both = jnp.where(lo, stacked_t[:, (2 * pair) * qb:(2 * pair + 1) * qb],
                     stacked_t[:, (2 * pair + 1) * qb:(2 * pair + 2) * qb])
    return both.T


def _band_mask_t(q0, k0, qb, kw, half_window):
    key = lax.broadcasted_iota(jnp.int32, (kw, 4 * qb), 0)
    qry = lax.broadcasted_iota(jnp.int32, (kw, 4 * qb), 1) & (qb - 1)
    return jnp.abs((q0 + qry) - (k0 + key)) <= half_window


def _block_origin(i, qb, kw, half_window, length):
    if isinstance(i, int):
        return i * qb, min(max(i * qb - half_window, 0), length - kw)
    return (pl.multiple_of(i * qb, qb),
            pl.multiple_of(jnp.clip(i * qb - half_window, 0, length - kw), HEAD_DIM))


def _head_row(vals, qb):
    return jnp.concatenate([jnp.broadcast_to(v, (1, qb)).astype(F32) for v in vals], axis=1)


def _attn_fwd(qkv, sink, n_seq, length, half_window, seq_blk, out_dtype, with_lse, name):
    qb, kw, nblk = _attn_geometry(length, half_window)
    with_sink = sink is not None
    nt = (((1,), (1,)), ((), ()))
    tn = (((0,), (0,)), ((), ()))
    qkv3 = qkv.reshape(n_seq, length, QKV_W)

    def body(*refs):
        refs = list(refs)
        sink_ref = refs.pop(0) if with_sink else None
        q_ref, k_ref, v_ref, o_ref = refs[:4]
        lse_ref = refs[4] if with_lse else None
        kx_ref, vx_ref = refs[-2:]
        head_row = lax.broadcasted_iota(jnp.int32, (N_HEADS, qb), 0)
        for s in range(seq_blk):
            _dup_kv(k_ref, kx_ref, s, length)
            _dup_kv(v_ref, vx_ref, s, length)

            def block(i, carry):
                q0, k0 = _block_origin(i, qb, kw, half_window, length)
                valid = _band_mask_t(q0, k0, qb, kw, half_window)
                lse_tile = jnp.zeros((N_HEADS, qb), F32)
                for g in range(N_KV):
                    qs = _stack_heads(q_ref, s, q0, qb, g)
                    kx = kx_ref[g, pl.ds(k0, kw), :]
                    vx = vx_ref[g, pl.ds(k0, kw), :]
                    st = lax.dot_general(kx, qs, nt, preferred_element_type=F32)
                    st = jnp.where(valid, st, NEG_INF)
                    m = jnp.max(st, axis=0, keepdims=True)
                    if with_sink:
                        sk = _head_row([sink_ref[4 * g + a] for a in range(4)], qb)
                        m = jnp.maximum(m, sk)
                    e = jnp.exp(st - m)
                    den = jnp.sum(e, axis=0, keepdims=True)
                    if with_sink:
                        den = den + jnp.exp(sk - m)
                    ot = lax.dot_general(vx, e.astype(BF16), tn, preferred_element_type=F32) / den
                    for pair in range(2):
                        col = (2 * g + pair) * LANES
                        o_ref[s, pl.ds(q0, qb), col:col + LANES] = _unstack_pair_t(ot, qb, pair).astype(out_dtype)
                    if with_lse:
                        lse = m + jnp.log(den)
                        for a in range(4):
                            lse_tile = jnp.where(head_row == 4 * g + a, lse[:, a * qb:(a + 1) * qb], lse_tile)
                if with_lse:
                    lse_ref[s, :, pl.ds(q0, qb)] = lse_tile
                return carry

            if nblk == 1:
                block(0, 0)
            else:
                lax.fori_loop(0, nblk, block, 0)

    in_specs = [pl.BlockSpec((seq_blk, length, N_HEADS * HEAD_DIM), lambda n: (n, 0, 0)),
                pl.BlockSpec((seq_blk, length, N_KV * HEAD_DIM), lambda n: (n, 0, 4)),
                pl.BlockSpec((seq_blk, length, N_KV * HEAD_DIM), lambda n: (n, 0, 5))]
    args = [qkv3, qkv3, qkv3]
    if with_sink:
        in_specs.insert(0, pl.BlockSpec(memory_space=pltpu.SMEM))
        args.insert(0, sink)
    out_specs = [pl.BlockSpec((seq_blk, length, D_MODEL), lambda n: (n, 0, 0))]
    out_shape = [jax.ShapeDtypeStruct((n_seq, length, D_MODEL), out_dtype)]
    if with_lse:
        out_specs.append(pl.BlockSpec((seq_blk, N_HEADS, length), lambda n: (n, 0, 0)))
        out_shape.append(jax.ShapeDtypeStruct((n_seq, N_HEADS, length), F32))
    outs = pl.pallas_call(
        body, name=name, grid=(n_seq // seq_blk,), in_specs=in_specs, out_specs=out_specs, out_shape=out_shape,
        scratch_shapes=[pltpu.VMEM((N_KV, length, LANES), BF16), pltpu.VMEM((N_KV, length, LANES), BF16)],
        compiler_params=_cp(),
    )(*args)
    o = outs[0].reshape(n_seq * length, D_MODEL)
    return (o, outs[1]) if with_lse else (o,)


def _attn_bwd(qkv, do, adj, sink, cos, sin, n_seq, length, half_window, seq_blk, dil, name):
    qb, kw, nblk = _attn_geometry(length, half_window)
    scale = 1.0 / math.sqrt(HEAD_DIM)
    with_sink = sink is not None
    nt = (((1,), (1,)), ((), ()))
    tn = (((0,), (0,)), ((), ()))
    qkv3 = qkv.reshape(n_seq, length, QKV_W)
    do3 = do.reshape(n_seq, length, D_MODEL)
    tabs = [t.reshape(dil, length, LANES) for t in (cos, sin)]
    tab_blocks = dil // seq_blk if dil >= seq_blk else 1

    def body(*refs):
        refs = list(refs)
        sink_ref = refs.pop(0) if with_sink else None
        q_ref, k_ref, v_ref, do_ref, aux_ref, cos_ref, sin_ref, dqkv_ref = refs[:8]
        ds_ref = refs[8] if with_sink else None
        kx_ref, vx_ref, dkx_ref, dvx_ref = refs[-4:]
        lane = lax.broadcasted_iota(jnp.int32, (1, LANES), 1)
        if with_sink:
            @pl.when(pl.program_id(0) == 0)
            def _():
                ds_ref[...] = jnp.zeros_like(ds_ref)

        for s in range(seq_blk):
            ts = s % dil
            _dup_kv(k_ref, kx_ref, s, length)
            _dup_kv(v_ref, vx_ref, s, length)
            dkx_ref[...] = jnp.zeros_like(dkx_ref)
            dvx_ref[...] = jnp.zeros_like(dvx_ref)

            def block(i, dsink):
                q0, k0 = _block_origin(i, qb, kw, half_window, length)
                valid = _band_mask_t(q0, k0, qb, kw, half_window)
                cs = cos_ref[ts, pl.ds(q0, qb), :] * scale
                sn = sin_ref[ts, pl.ds(q0, qb), :] * scale
                adj_tile = aux_ref[s, :, pl.ds(q0, qb)]
                for g in range(N_KV):
                    qs = _stack_heads(q_ref, s, q0, qb, g)
                    dos = _stack_heads(do_ref, s, q0, qb, g)
                    kx = kx_ref[g, pl.ds(k0, kw), :]
                    vx = vx_ref[g, pl.ds(k0, kw), :]
                    st = lax.dot_general(kx, qs, nt, preferred_element_type=F32)
                    st = jnp.where(valid, st, NEG_INF)
                    m = jnp.max(st, axis=0, keepdims=True)
                    if with_sink:
                        sk = _head_row([sink_ref[4 * g + a] for a in range(4)], qb)
                        m = jnp.maximum(m, sk)
                    e = jnp.exp(st - m)
                    den = jnp.sum(e, axis=0, keepdims=True)
                    if with_sink:
                        esk = jnp.exp(sk - m)
                        den = den + esk
                    rden = 1.0 / den
                    pt = e * rden
                    shift = _head_row([adj_tile[4 * g + a:4 * g + a + 1, :] for a in range(4)], qb)
                    dpt = lax.dot_general(vx, dos, nt, preferred_element_type=F32)
                    dst = pt * (dpt + shift)
                    if with_sink:
                        dsk = esk * rden * shift
                        for a in range(4):
                            tot = jnp.sum(dsk[:, a * qb:(a + 1) * qb], axis=1, keepdims=True)
                            dsink = dsink + jnp.where(lane == 4 * g + a, tot, 0.0)
                    dsb = dst.astype(BF16)
                    pb = pt.astype(BF16)
                    dqt = lax.dot_general(kx, dsb, tn, preferred_element_type=F32)
                    for pair in range(2):
                        col = (2 * g + pair) * LANES
                        tile = _rope_t(_unstack_pair_t(dqt, qb, pair), cs, sn)
                        dqkv_ref[s, pl.ds(q0, qb), col:col + LANES] = tile.astype(BF16)
                    dkx_ref[g, pl.ds(k0, kw), :] += jnp.dot(dsb, qs, preferred_element_type=F32)
                    dvx_ref[g, pl.ds(k0, kw), :] += jnp.dot(pb, dos, preferred_element_type=F32)
                return dsink

            if nblk == 1:
                dsink = block(0, jnp.zeros((1, LANES), F32))
            else:
                dsink = lax.fori_loop(0, nblk, block, jnp.zeros((1, LANES), F32))
            if with_sink:
                ds_ref[0:1, :] += dsink

            ch = min(length, 256)
            lo_c = lax.broadcasted_iota(jnp.int32, (ch, LANES), 1) < HEAD_DIM

            def fin(c, carry):
                r0 = pl.multiple_of(c * ch, ch)
                cs = cos_ref[ts, pl.ds(r0, ch), :]
                sn = sin_ref[ts, pl.ds(r0, ch), :]
                for j in range(N_KV // 2):
                    both = []
                    for acc_ref in (dkx_ref, dvx_ref):
                        t0 = acc_ref[2 * j, pl.ds(r0, ch), :]
                        t1 = acc_ref[2 * j + 1, pl.ds(r0, ch), :]
                        t0 = t0 + pltpu.roll(t0, HEAD_DIM, 1)
                        t1 = t1 + pltpu.roll(t1, HEAD_DIM, 1)
                        both.append(jnp.where(lo_c, t0, t1))
                    kcol = N_HEADS * HEAD_DIM + j * LANES
                    vcol = (N_HEADS + N_KV) * HEAD_DIM + j * LANES
                    dqkv_ref[s, pl.ds(r0, ch), kcol:kcol + LANES] = _rope_t(both[0], cs, sn).astype(BF16)
                    dqkv_ref[s, pl.ds(r0, ch), vcol:vcol + LANES] = both[1].astype(BF16)
                return carry

            lax.fori_loop(0, length // ch, fin, 0)

    seq_map = lambda n: (n, 0, 0)
    tab_map = (lambda n: (n % tab_blocks, 0, 0)) if dil >= seq_blk else (lambda n: (0, 0, 0))
    tab_rows = min(seq_blk, dil)
    in_specs = [pl.BlockSpec((seq_blk, length, N_HEADS * HEAD_DIM), seq_map),
                pl.BlockSpec((seq_blk, length, N_KV * HEAD_DIM), lambda n: (n, 0, 4)),
                pl.BlockSpec((seq_blk, length, N_KV * HEAD_DIM), lambda n: (n, 0, 5)),
                pl.BlockSpec((seq_blk, length, D_MODEL), seq_map),
                pl.BlockSpec((seq_blk, N_HEADS, length), seq_map),
                pl.BlockSpec((tab_rows, length, LANES), tab_map),
                pl.BlockSpec((tab_rows, length, LANES), tab_map)]
    args = [qkv3, qkv3, qkv3, do3, adj] + tabs
    if with_sink:
        in_specs.insert(0, pl.BlockSpec(memory_space=pltpu.SMEM))
        args.insert(0, sink)
    out_specs = [pl.BlockSpec((seq_blk, length, QKV_W), seq_map)]
    out_shape = [jax.ShapeDtypeStruct((n_seq, length, QKV_W), BF16)]
    if with_sink:
        out_specs.append(pl.BlockSpec((8, LANES), lambda n: (0, 0)))
        out_shape.append(jax.ShapeDtypeStruct((8, LANES), F32))
    outs = pl.pallas_call(
        body, name=name, grid=(n_seq // seq_blk,), in_specs=in_specs, out_specs=out_specs, out_shape=out_shape,
        scratch_shapes=[pltpu.VMEM((N_KV, length, LANES), BF16), pltpu.VMEM((N_KV, length, LANES), BF16),
                        pltpu.VMEM((N_KV, length, LANES), F32), pltpu.VMEM((N_KV, length, LANES), F32)],
        compiler_params=_cp(),
    )(*args)
    dqkv = outs[0].reshape(n_seq * length, QKV_W)
    return (dqkv, outs[1]) if with_sink else (dqkv, None)


def _head_expander():
    h = jnp.arange(LANES)[:, None]
    l = jnp.arange(D_MODEL)[None, :]
    return (l // HEAD_DIM == h).astype(BF16)


def _dot_split(a, e):
    hi = a.astype(BF16)
    lo = (a - hi.astype(F32)).astype(BF16)
    return jnp.dot(hi, e, preferred_element_type=F32) + jnp.dot(lo, e, preferred_element_type=F32)


def _mix_weights(lses):
    m = jnp.maximum(jnp.maximum(lses[0], lses[1]), lses[2])
    es = [jnp.exp(v - m) for v in lses]
    tot = es[0] + es[1] + es[2]
    return [e / tot for e in es]


def _mix_fwd(os_, lses, name):
    t = os_[0].shape[0]
    tm = _row_tile(t, 512)

    def body(o0, o1, o2, l0, l1, l2, e_ref, out_ref):
        wts = _mix_weights([l0[...], l1[...], l2[...]])
        acc = jnp.zeros((tm, D_MODEL), F32)
        for w, o_ref in zip(wts, (o0, o1, o2)):
            acc = acc + _dot_split(w, e_ref[...]) * o_ref[...]
        out_ref[...] = acc.astype(BF16)

    row = pl.BlockSpec((tm, D_MODEL), lambda i: (i, 0))
    lrow = pl.BlockSpec((tm, LANES), lambda i: (i, 0))
    return pl.pallas_call(
        body, name=name, grid=(t // tm,),
        in_specs=[row] * 3 + [lrow] * 3 + [pl.BlockSpec((LANES, D_MODEL), lambda i: (0, 0))],
        out_specs=row, out_shape=jax.ShapeDtypeStruct((t, D_MODEL), BF16), compiler_params=_cp(),
    )(*os_, *lses, _head_expander())


def _mix_bwd(dmix, os_, lses, name):
    t = dmix.shape[0]
    tm = _row_tile(t, 512)

    def body(d_ref, o0, o1, o2, l0, l1, l2, e_ref, et_ref, do0, do1, do2, a0, a1, a2):
        wts = _mix_weights([l0[...], l1[...], l2[...]])
        dv = d_ref[...].astype(F32)
        cs = [_dot_split(dv * o_ref[...], et_ref[...]) for o_ref in (o0, o1, o2)]
        mean_c = wts[0] * cs[0] + wts[1] * cs[1] + wts[2] * cs[2]
        for w, c, do_ref, a_ref in zip(wts, cs, (do0, do1, do2), (a0, a1, a2)):
            do_ref[...] = (_dot_split(w, e_ref[...]) * dv).astype(BF16)
            a_ref[...] = w * (c - mean_c) - w * c

    row = pl.BlockSpec((tm, D_MODEL), lambda i: (i, 0))
    lrow = pl.BlockSpec((tm, LANES), lambda i: (i, 0))
    e = _head_expander()
    return pl.pallas_call(
        body, name=name, grid=(t // tm,),
        in_specs=[row] * 4 + [lrow] * 3 + [pl.BlockSpec((LANES, D_MODEL), lambda i: (0, 0)),
                                            pl.BlockSpec((D_MODEL, LANES), lambda i: (0, 0))],
        out_specs=[row] * 3 + [lrow] * 3,
        out_shape=[jax.ShapeDtypeStruct((t, D_MODEL), BF16)] * 3 + [jax.ShapeDtypeStruct((t, LANES), F32)] * 3,
        compiler_params=_cp(),
    )(dmix, *os_, *lses, e, e.T)


def _stats_to_tokens(stat, batch, dil):
    n_seq, _, length = stat.shape
    t = stat.transpose(0, 2, 1).reshape(n_seq * length, N_HEADS)
    return _from_residue(jnp.pad(t, ((0, 0), (0, LANES - N_HEADS))), batch, dil)


def _stats_from_tokens(stat, batch, dil, n_seq, length):
    t = _to_residue(stat[:, :N_HEADS], batch, dil)
    return t.reshape(n_seq, length, N_HEADS).transpose(0, 2, 1)


def _group_geometry(batch, seq, dil, window):
    length = seq // dil
    n_seq = batch * dil
    seq_blk = max(1, min(dil, 1024 // length))
    return n_seq, length, (window // 2) // dil, seq_blk


def _local_step(x, target, a_in, a_sink, a_out, b_in, b_out, norm_mix, norm_ffn, wg, wu, wd, final_norm):
    batch, seq, _ = x.shape
    t = batch * seq
    x0 = x.reshape(t, D_MODEL)
    tgt = target.reshape(t, D_MODEL)
    tabs = {d: _rope_tables(seq, d) for _, d in DILATED}
    nm = [norm_mix[i:i + 1] for i in range(2)]
    nf = [norm_ffn[i:i + 1] for i in range(2)]

    h0, h0t = _rms_fwd(x0, nm[0], "rms_mix0", True)
    qkv0 = _qkv_proj(h0, a_in, *tabs[1], 0, "qkv0")
    (o0,) = _attn_fwd(qkv0, a_sink, batch, seq, HALF_WINDOW_A, 1, BF16, False, "attn0")
    x1 = _mm_res(o0, a_out, x0, "out0")
    hf0, hf0t = _rms_fwd(x1, nf[0], "rms_ffn0", True)
    act0, g0, u0 = _ffn_up(hf0, wg, wu, 0, "ffn_up0")
    x2 = _ffn_down(act0, wd, x1, 0, "ffn_down0")

    h1 = _rms_fwd(x2, nm[1], "rms_mix1")
    geo = [_group_geometry(batch, seq, d, w) for w, d in DILATED]
    h1g, qkv1, o1, lse1 = [], [], [], []
    for gi, (_, d) in enumerate(DILATED):
        n_seq, length, hw, sb = geo[gi]
        hp = _to_residue(h1, batch, d)
        pj = _qkv_proj(hp, b_in, *tabs[d], gi, f"qkv1_{gi}")
        o, lse = _attn_fwd(pj, None, n_seq, length, hw, sb, F32, True, f"attn1_{gi}")
        h1g.append(hp)
        qkv1.append(pj)
        o1.append(_from_residue(o, batch, d))
        lse1.append(_stats_to_tokens(lse, batch, d))
    omix = _mix_fwd(o1, lse1, "mix")
    x3 = _mm_res(omix, b_out, x2, "out1")
    hf1, hf1t = _rms_fwd(x3, nf[1], "rms_ffn1", True)
    act1, g1, u1 = _ffn_up(hf1, wg, wu, 1, "ffn_up1")
    x4 = _ffn_down(act1, wd, x3, 1, "ffn_down1")

    dx4, dx4b, dx4t, loss_cols, d_final = _final_loss(x4, final_norm.reshape(1, D_MODEL), tgt, "final_loss")

    def ffn_bwd(dxo, dxob, dxot, x_mid, hft, g, u, act, layer):
        dg, du = _ffn_down_bwd(dxob, wd, g, u, layer, f"ffn_down_bwd{layer}")
        (d_wdt,) = _mm_grad(dxot, [act], f"grad_wd{layer}")
        dh = _ffn_up_bwd(dg, du, wg, wu, layer, f"ffn_up_bwd{layer}")
        d_wg, d_wu = _mm_grad(hft, [dg, du], f"grad_wgu{layer}")
        dxm, dxmb, d_nf = _rms_bwd(x_mid, nf[layer], [dh], dxo, f"rms_ffn_bwd{layer}")
        return dxm, dxmb, d_nf, d_wg, d_wu, d_wdt

    dx3, dx3b, d_nf1, d_wg1, d_wu1, d_wd1 = ffn_bwd(dx4, dx4b, dx4t, x3, hf1t, g1, u1, act1, 1)

    dmix = _mm_nt(dx3b, b_out, 0, BF16, "out1_bwd")
    (d_b_out,) = _mm_tn(omix, [dx3b], "grad_b_out")
    mb = _mix_bwd(dmix, o1, lse1, "mix_bwd")
    dh1, d_b_in = [], []
    for gi, (_, d) in enumerate(DILATED):
        n_seq, length, hw, sb = geo[gi]
        dog = _to_residue(mb[gi], batch, d)
        adj = _stats_from_tokens(mb[3 + gi], batch, d, n_seq, length)
        dpj, _ = _attn_bwd(qkv1[gi], dog, adj, None, *tabs[d], n_seq, length, hw, sb, d, f"attn1_bwd{gi}")
        (dw,) = _mm_tn(h1g[gi], [dpj], f"grad_b_in{gi}")
        d_b_in.append(dw)
        dh1.append(_from_residue(_mm_nt(dpj, b_in, gi, F32, f"qkv1_bwd{gi}"), batch, d))
    dx2, dx2b, dx2t, d_nm1 = _rms_bwd(x2, nm[1], dh1, dx3, "rms_mix_bwd1", True)

    dx1, dx1b, d_nf0, d_wg0, d_wu0, d_wd0 = ffn_bwd(dx2, dx2b, dx2t, x1, hf0t, g0, u0, act0, 0)

    do0, adj0 = _out_bwd(dx1b, a_out, o0, "out0_bwd")
    (d_a_out,) = _mm_tn(o0, [dx1b], "grad_a_out")
    adj0 = _stats_from_tokens(adj0, batch, 1, batch, seq)
    dqkv0, d_sink = _attn_bwd(qkv0, do0, adj0, a_sink, *tabs[1], batch, seq, HALF_WINDOW_A, 1, 1, "attn0_bwd")
    (d_a_in,) = _mm_grad(h0t, [dqkv0], "grad_a_in")
    dh0 = _mm_nt(dqkv0, a_in, 0, F32, "qkv0_bwd")
    gx, _, d_nm0 = _rms_bwd(x0, nm[0], [dh0], dx1, "rms_mix_bwd0")

    grads = dict(a_in=d_a_in, a_out=d_a_out, b_in=jnp.concatenate(d_b_in, axis=1), b_out=d_b_out,
                 wg=(d_wg0, d_wg1), wu=(d_wu0, d_wu1), wd=(d_wd0, d_wd1))
    vecs = dict(norm_mix=(d_nm0, d_nm1), norm_ffn=(d_nf0, d_nf1), final=d_final, loss_cols=loss_cols, sink=d_sink)
    return gx.reshape(x.shape), grads, vecs


ANY = pl.BlockSpec(memory_space=pl.ANY)
HBM = pltpu.MemorySpace.HBM


def _me():
    return lax.axis_index("x"), lax.axis_index("y"), lax.axis_index("c")


def _chip_peer(x, y, j):
    px = 1 - x if j & 2 else x
    py = 1 - y if j & 1 else y
    return px, py, 2 * px + py


def _remote(src, dst, sems, k, dev):
    return pltpu.make_async_remote_copy(src_ref=src, dst_ref=dst, send_sem=sems[0].at[k], recv_sem=sems[1].at[k],
                                        device_id=dev, device_id_type=MESH)


def _col_window(ref, q, width):
    return ref.at[:, pl.ds(pl.multiple_of(q * width, LANES), width)]


def _half0(ref, h):
    n = ref.shape[0] // 2
    return ref.at[pl.ds(h * n, n)]


def _half1(ref, h):
    n = ref.shape[1] // 2
    return ref.at[:, pl.ds(h * n, n)]


def _half_rows(ref, h):
    n = ref.shape[-2] // 2
    if len(ref.shape) == 2:
        return ref.at[pl.ds(h * n, n)]
    return ref.at[:, pl.ds(h * n, n)]


def _place_shard(w, q_arr, col, name):
    lead, rows, cols = w.shape

    def body(q_ref, w_ref, o_ref):
        o_ref[...] = w_ref[...].astype(BF16)

    if col:
        assert lead == 1
        out_spec = pl.BlockSpec((rows, cols), lambda l, q: (0, q[0]))
        out_shape = jax.ShapeDtypeStruct((rows, N_CHIPS * cols), BF16)
    else:
        out_spec = pl.BlockSpec((None, None, rows, cols), lambda l, q: (q[0], l, 0, 0))
        out_shape = jax.ShapeDtypeStruct((N_CHIPS, lead, rows, cols), BF16)
    return pl.pallas_call(
        body, name=name,
        grid_spec=pltpu.PrefetchScalarGridSpec(
            num_scalar_prefetch=1, grid=(lead,),
            in_specs=[pl.BlockSpec((None, rows, cols), lambda l, q: (l, 0, 0))], out_specs=out_spec),
        out_shape=out_shape, compiler_params=_cp(),
    )(q_arr, w)


def _handshake(peers):
    barrier = pltpu.get_barrier_semaphore()
    for p in peers:
        pl.semaphore_signal(barrier, inc=1, device_id=p, device_id_type=MESH)
    pl.semaphore_wait(barrier, len(peers))


def _on_sequencer(name, collective_id, n_sem, n_local, body):
    @pl.kernel(mesh=plsc.ScalarSubcoreMesh(axis_name="seq", num_cores=1), name=name,
               scratch_types=(pltpu.SemaphoreType.DMA((n_sem,)), pltpu.SemaphoreType.DMA((n_sem,)),
                              pltpu.SemaphoreType.DMA((max(n_local, 1),))),
               compiler_params=pltpu.CompilerParams(collective_id=collective_id))
    def launch(send_sems, recv_sems, local_sems):
        body((send_sems, recv_sems), local_sems)

    launch()


def _gather_plan(outs, col_fam, sems, handshake):
    n_w = len(outs)
    x, y, c = _me()
    myq = 2 * x + y
    sib = (x, y, 1 - c)
    if handshake:
        _handshake([sib] + [_chip_peer(x, y, j)[:2] + (c,) for j in (1, 2, 3)])

    def slot(w, q):
        if col_fam[w]:
            return _col_window(outs[w], q, outs[w].shape[1] // N_CHIPS)
        return outs[w].at[q]

    first = []
    for w in range(n_w):
        for j in (1, 2, 3):
            px, py, _ = _chip_peer(x, y, j)
            mine = _half_rows(slot(w, myq), c)
            cp = _remote(mine, mine, sems, w * 6 + j - 1, (px, py, c))
            cp.start()
            first.append(cp)
    passed = []
    for w in range(n_w):
        for j in (1, 2, 3):
            _, _, pq = _chip_peer(x, y, j)
            land = _half_rows(slot(w, pq), c)
            _remote(land, land, sems, w * 6 + j - 1, sib).wait_recv()
            cp = _remote(land, land, sems, w * 6 + 2 + j, sib)
            cp.start()
            passed.append(cp)
    for w in range(n_w):
        for j in (1, 2, 3):
            _, _, pq = _chip_peer(x, y, j)
            land = _half_rows(slot(w, pq), 1 - c)
            _remote(land, land, sems, w * 6 + 2 + j, sib).wait_recv()
    for cp in first + passed:
        cp.wait_send()


def _gather_weights(bufs, col_fam):
    n_w = len(bufs)

    def body(*refs):
        _gather_plan(refs[n_w:2 * n_w], col_fam, refs[2 * n_w:2 * n_w + 2], False)

    return pl.pallas_call(
        body, name="gather_weights", in_specs=[ANY] * n_w, out_specs=[ANY] * n_w,
        out_shape=[jax.ShapeDtypeStruct(b.shape, b.dtype) for b in bufs],
        input_output_aliases={w: w for w in range(n_w)},
        scratch_shapes=[pltpu.SemaphoreType.DMA((6 * n_w,)), pltpu.SemaphoreType.DMA((6 * n_w,))],
    )(*bufs)


def _gather_weights_async(bufs, col_fam, name, collective_id):
    refs = [jax.new_ref(b, memory_space=HBM) for b in bufs]
    _on_sequencer(name, collective_id, 6 * len(bufs), 0,
                  lambda sems, _: _gather_plan(refs, col_fam, sems, True))
    return [r[...] for r in refs]


def _grad_half(ref, col, h):
    return _half0(ref, h) if col else _half1(ref, h)


def _swap_halves_with_sibling(grads, col_fam):
    n_w = len(grads)

    def body(*refs):
        _swap_plan(refs[:n_w], refs[n_w:2 * n_w], col_fam, refs[2 * n_w:], False)

    return pl.pallas_call(
        body, name="grad_swap_sibling", in_specs=[ANY] * n_w, out_specs=[ANY] * n_w,
        out_shape=_swap_shapes(grads, col_fam),
        scratch_shapes=[pltpu.SemaphoreType.DMA((n_w,)), pltpu.SemaphoreType.DMA((n_w,))],
    )(*grads)


def _swap_shapes(grads, col_fam):
    out = []
    for w, g in enumerate(grads):
        shp = (g.shape[0] // 2, g.shape[1]) if col_fam[w] else (g.shape[0], g.shape[1] // 2, g.shape[2])
        out.append(jax.ShapeDtypeStruct(shp, g.dtype))
    return out


def _swap_plan(ins, outs, col_fam, sems, handshake):
    x, y, c = _me()
    sib = (x, y, 1 - c)
    if handshake:
        _handshake([sib])
    cps = [_remote(_grad_half(ins[w], col_fam[w], 1 - c), outs[w], sems, w, sib) for w in range(len(ins))]
    for cp in cps:
        cp.start()
    for cp in cps:
        cp.wait_recv()
    for cp in cps:
        cp.wait_send()


def _swap_halves_async(grads, col_fam, name, collective_id):
    srcs = [jax.new_ref(g, memory_space=HBM) for g in grads]
    dsts = [jax.empty_ref(s, memory_space=HBM) for s in _swap_shapes(grads, col_fam)]
    _on_sequencer(name, collective_id, len(grads), 0, lambda sems, _: _swap_plan(srcs, dsts, col_fam, sems, True))
    return [r[...] for r in srcs], [r[...] for r in dsts]


def _half_add(mine, recv, c_arr, col, name):
    if col:
        rows, n = recv.shape
        tr = rows // 2
        grid = (2,)
        in_specs = [pl.BlockSpec((tr, n), lambda i, c: (2 * c[0] + i, 0)), pl.BlockSpec((tr, n), lambda i, c: (i, 0))]
        out_spec = pl.BlockSpec((tr, n), lambda i, c: (i, 0))
    else:
        _, rows, n = recv.shape
        grid = (N_CHIPS,)
        in_specs = [pl.BlockSpec((None, rows, n), lambda q, c: (q, c[0], 0)),
                    pl.BlockSpec((None, rows, n), lambda q, c: (q, 0, 0))]
        out_spec = pl.BlockSpec((None, rows, n), lambda q, c: (q, 0, 0))

    def body(c_ref, a_ref, b_ref, o_ref):
        o_ref[...] = (a_ref[...].astype(F32) + b_ref[...].astype(F32)).astype(BF16)

    return pl.pallas_call(
        body, name=name,
        grid_spec=pltpu.PrefetchScalarGridSpec(num_scalar_prefetch=1, grid=grid, in_specs=in_specs, out_specs=out_spec),
        out_shape=jax.ShapeDtypeStruct(recv.shape, BF16), compiler_params=_cp(),
    )(c_arr, mine, recv)


def _scatter_chip_sums(sums, col_fam):
    n_w = len(sums)

    def body(*refs):
        _scatter_plan(refs[:n_w], refs[n_w:2 * n_w], col_fam, refs[2 * n_w:2 * n_w + 2], refs[2 * n_w + 2], False)

    return pl.pallas_call(
        body, name="grad_scatter_chips", in_specs=[ANY] * n_w, out_specs=[ANY] * n_w,
        out_shape=_scatter_shapes(sums, col_fam),
        scratch_shapes=[pltpu.SemaphoreType.DMA((3 * n_w,)), pltpu.SemaphoreType.DMA((3 * n_w,)),
                        pltpu.SemaphoreType.DMA((n_w,))],
    )(*sums)


def _scatter_shapes(sums, col_fam):
    out = []
    for w, s in enumerate(sums):
        shp = (s.shape[0], s.shape[1] // N_CHIPS) if col_fam[w] else s.shape[1:]
        out.append(jax.ShapeDtypeStruct((N_CHIPS,) + shp, s.dtype))
    return out


def _scatter_plan(ins, outs, col_fam, sems, lsem, handshake):
    n_w = len(ins)
    x, y, c = _me()
    myq = 2 * x + y
    if handshake:
        _handshake([_chip_peer(x, y, j)[:2] + (c,) for j in (1, 2, 3)])

    def slab(w, q):
        if col_fam[w]:
            return _col_window(ins[w], q, ins[w].shape[1] // N_CHIPS)
        return ins[w].at[q]

    local = [pltpu.make_async_copy(slab(w, myq), outs[w].at[myq], lsem.at[w]) for w in range(n_w)]
    for cp in local:
        cp.start()
    cps = []
    for w in range(n_w):
        for j in (1, 2, 3):
            px, py, pq = _chip_peer(x, y, j)
            cp = _remote(slab(w, pq), outs[w].at[myq], sems, w * 3 + j - 1, (px, py, c))
            cp.start()
            cps.append(cp)
    for w in range(n_w):
        for j in (1, 2, 3):
            _, _, pq = _chip_peer(x, y, j)
            land = outs[w].at[pq]
            _remote(land, land, sems, w * 3 + j - 1, (x, y, c)).wait_recv()
    for cp in cps:
        cp.wait_send()
    for cp in local:
        cp.wait()


def _scatter_chip_sums_async(sums, col_fam, name, collective_id):
    srcs = [jax.new_ref(s, memory_space=HBM) for s in sums]
    dsts = [jax.empty_ref(s, memory_space=HBM) for s in _scatter_shapes(sums, col_fam)]
    _on_sequencer(name, collective_id, 3 * len(sums), len(sums),
                  lambda sems, lsem: _scatter_plan(srcs, dsts, col_fam, sems, lsem, True))
    return [r[...] for r in dsts]


def _sum_chips(parts, c_arr, prev, lead, shape, name):
    _, rows, n = parts.shape
    tr = rows // 2 if rows % 32 == 0 else rows
    nblk = rows // tr

    def body(c_ref, p_ref, *rest):
        o_ref = rest[-1]
        acc = p_ref[0].astype(F32)
        for q in range(1, N_CHIPS):
            acc = acc + p_ref[q].astype(F32)
        o_ref[...] = acc

    in_specs = [pl.BlockSpec((N_CHIPS, tr, n), lambda i, c: (0, i, 0))]
    args = [c_arr, parts]
    aliases = {}
    if prev is not None:
        in_specs.append(ANY)
        args.append(prev)
        aliases = {2: 0}
    return pl.pallas_call(
        body, name=name,
        grid_spec=pltpu.PrefetchScalarGridSpec(
            num_scalar_prefetch=1, grid=(nblk,), in_specs=in_specs,
            out_specs=pl.BlockSpec((None, tr, n), lambda i, c: (lead, c[0] * nblk + i, 0))),
        out_shape=jax.ShapeDtypeStruct(shape, F32), input_output_aliases=aliases, compiler_params=_cp(),
    )(*args)


def _join_halves(bufs, place):
    n_o = len(bufs)
    n_h = len(place)

    def body(*refs):
        outs = refs[n_o:2 * n_o]
        sems = refs[2 * n_o:2 * n_o + 2]
        x, y, c = _me()
        sib = (x, y, 1 - c)

        def half(k, h):
            o, lead = place[k]
            return _half_rows(outs[o].at[lead], h)

        cps = [_remote(half(k, c), half(k, c), sems, k, sib) for k in range(n_h)]
        for cp in cps:
            cp.start()
        for k in range(n_h):
            land = half(k, 1 - c)
            _remote(land, land, sems, k, sib).wait_recv()
        for cp in cps:
            cp.wait_send()

    return pl.pallas_call(
        body, name="grad_join_sibling", in_specs=[ANY] * n_o, out_specs=[ANY] * n_o,
        out_shape=[jax.ShapeDtypeStruct(b.shape, b.dtype) for b in bufs],
        input_output_aliases={k: k for k in range(n_o)},
        scratch_shapes=[pltpu.SemaphoreType.DMA((n_h,)), pltpu.SemaphoreType.DMA((n_h,))],
    )(*bufs)


def _allreduce_rows(rows):
    n_dev = 8
    n_r = len(rows)
    assert n_r <= 8

    def body(*refs):
        r_refs = refs[:n_r]
        o_ref, slots, send_sems, recv_sems = refs[n_r:]
        x, y, c = _me()
        me = 4 * x + 2 * y + c
        slots[me] = jnp.concatenate([r[...] for r in r_refs] + [jnp.zeros((8 - n_r, D_MODEL), F32)], axis=0)

        def peer(k):
            return (1 - x if k & 4 else x, 1 - y if k & 2 else y, 1 - c if k & 1 else c)

        cps = []
        for k in range(1, n_dev):
            cp = pltpu.make_async_remote_copy(src_ref=slots.at[me], dst_ref=slots.at[me], send_sem=send_sems.at[k - 1],
                                              recv_sem=recv_sems.at[k - 1], device_id=peer(k), device_id_type=MESH)
            cp.start()
            cps.append(cp)
        for k in range(1, n_dev):
            px, py, pc = peer(k)
            land = slots.at[4 * px + 2 * py + pc]
            pltpu.make_async_remote_copy(src_ref=land, dst_ref=land, send_sem=send_sems.at[k - 1],
                                         recv_sem=recv_sems.at[k - 1], device_id=peer(k),
                                         device_id_type=MESH).wait_recv()
        for cp in cps:
            cp.wait_send()
        acc = slots[0]
        for d in range(1, n_dev):
            acc = acc + slots[d]
        o_ref[...] = acc

    vm = pl.BlockSpec(memory_space=pltpu.VMEM)
    return pl.pallas_call(
        body, name="allreduce_rows", in_specs=[vm] * n_r, out_specs=vm,
        out_shape=jax.ShapeDtypeStruct((8, D_MODEL), F32),
        scratch_shapes=[pltpu.VMEM((n_dev, 8, D_MODEL), F32), pltpu.SemaphoreType.DMA((n_dev - 1,)),
                        pltpu.SemaphoreType.DMA((n_dev - 1,))],
    )(*rows)


def _adamw(w, g, m, v, name):
    shape = w.shape
    if len(shape) == 1:
        lead, rows, cols = 1, 1, shape[0]
    else:
        rows, cols = shape[-2:]
        lead = math.prod(shape[:-2])
    args = [a.reshape(lead, rows, cols) for a in (w, g, m, v)]
    tr = rows // 2 if rows % 16 == 0 else rows

    def body(w_ref, g_ref, m_ref, v_ref, d_ref, nm_ref, nv_ref):
        gv = g_ref[...]
        nm = ADAM_B1 * m_ref[...] + (1.0 - ADAM_B1) * gv
        nv = ADAM_B2 * v_ref[...] + (1.0 - ADAM_B2) * jnp.square(gv)
        m_hat = nm / (1.0 - ADAM_B1 ** ADAM_STEP)
        v_hat = nv / (1.0 - ADAM_B2 ** ADAM_STEP)
        d_ref[...] = -ADAM_LR * (m_hat / (jnp.sqrt(v_hat) + ADAM_EPS) + ADAM_WD * w_ref[...])
        nm_ref[...] = nm
        nv_ref[...] = nv

    spec = pl.BlockSpec((None, tr, cols), lambda l, i: (l, i, 0))
    outs = pl.pallas_call(
        body, name=name, grid=(lead, rows // tr), in_specs=[spec] * 4, out_specs=[spec] * 3,
        out_shape=[jax.ShapeDtypeStruct((lead, rows, cols), F32)] * 3, compiler_params=_cp(),
    )(*args)
    return [o.reshape(shape) for o in outs]


def kernel(x, a_w_in, a_sink, a_w_out, b_w_in, b_w_out, norm_mix, norm_ffn, w_gate, w_up, w_down, final_norm, loss_target, m_a_w_in, m_a_sink, m_a_w_out, m_b_w_in, m_b_w_out, m_norm_mix, m_norm_ffn, m_w_gate, m_w_up, m_w_down, m_final_norm, v_a_w_in, v_a_sink, v_a_w_out, v_b_w_in, v_b_w_out, v_norm_mix, v_norm_ffn, v_w_gate, v_w_up, v_w_down, v_final_norm):
    weights = dict(a_w_in=a_w_in, a_sink=a_sink, a_w_out=a_w_out, b_w_in=b_w_in, b_w_out=b_w_out, norm_mix=norm_mix,
                   norm_ffn=norm_ffn, w_gate=w_gate, w_up=w_up, w_down=w_down, final_norm=final_norm)
    mom = dict(a_w_in=m_a_w_in, a_sink=m_a_sink, a_w_out=m_a_w_out, b_w_in=m_b_w_in, b_w_out=m_b_w_out,
               norm_mix=m_norm_mix, norm_ffn=m_norm_ffn, w_gate=m_w_gate, w_up=m_w_up, w_down=m_w_down,
               final_norm=m_final_norm)
    var = dict(a_w_in=v_a_w_in, a_sink=v_a_sink, a_w_out=v_a_w_out, b_w_in=v_b_w_in, b_w_out=v_b_w_out,
               norm_mix=v_norm_mix, norm_ffn=v_norm_ffn, w_gate=v_w_gate, w_up=v_w_up, w_down=v_w_down,
               final_norm=v_final_norm)
    order = ["a_w_in", "a_sink", "a_w_out", "b_w_in", "b_w_out", "norm_mix", "norm_ffn", "w_gate", "w_up", "w_down",
             "final_norm"]

    c_arr = lax.axis_index("c").astype(jnp.int32).reshape(1)
    q_arr = (2 * lax.axis_index("x") + lax.axis_index("y")).astype(jnp.int32).reshape(1)
    shards = [a_w_in, a_w_out, b_w_in, b_w_out, w_gate, w_up, w_down]
    shard_names = ("a_in", "a_out", "b_in", "b_out", "wg", "wu", "wd")
    placed = [_place_shard(s, q_arr, col, f"place_{nm}")
              for s, col, nm in zip(shards, (True, False, True, False, False, False, False), shard_names)]
    a_in, a_out = _gather_weights(placed[:2], (True, False))
    b_in, b_out, wg, wu, wd = _gather_weights_async(placed[2:], (True, False, False, False, False),
                                                    "gather_weights_late", 1)
    a_out = a_out.reshape(D_MODEL, D_MODEL)
    b_out = b_out.reshape(D_MODEL, D_MODEL)

    gx, grads, vecs = _local_step(x, loss_target, a_in, a_sink[0], a_out, b_in, b_out, norm_mix, norm_ffn, wg, wu, wd,
                                  final_norm)

    rows_out = D_MODEL // N_CHIPS
    partials = [grads["a_in"], grads["b_in"],
                grads["a_out"].reshape(N_CHIPS, rows_out, D_MODEL), grads["b_out"].reshape(N_CHIPS, rows_out, D_MODEL),
                grads["wg"][0], grads["wg"][1], grads["wu"][0], grads["wu"][1], grads["wd"][0], grads["wd"][1]]
    col_fam = (True, True) + (False,) * 8
    names = ("a_in", "b_in", "a_out", "b_out", "wg0", "wg1", "wu0", "wu1", "wd0", "wd1")
    contrib = [None] * len(partials)

    def reduce_group(idx, tag, ids):
        parts = [partials[k] for k in idx]
        cols = tuple(col_fam[k] for k in idx)
        if ids is None:
            theirs = _swap_halves_with_sibling(parts, cols)
        else:
            parts, theirs = _swap_halves_async(parts, cols, f"grad_swap_{tag}", ids[0])
        sums = [_half_add(p, r, c_arr, cf, f"chip_sum_{names[k]}") for p, r, cf, k in zip(parts, theirs, cols, idx)]
        if ids is None:
            out = _scatter_chip_sums(sums, cols)
        else:
            out = _scatter_chip_sums_async(sums, cols, f"grad_scatter_{tag}", ids[1])
        for k, o in zip(idx, out):
            contrib[k] = o

    reduce_group([1, 3, 5, 7, 9], "layer1", (2, 3))
    reduce_group([4, 6, 8], "ffn0", (4, 5))
    reduce_group([0, 2], "mixer0", None)
    shapes = [a_w_in.shape, b_w_in.shape, a_w_out.shape, b_w_out.shape, w_gate.shape, w_up.shape, w_gate.shape]
    place = [(0, 0), (1, 0), (2, 0), (3, 0), (4, 0), (4, 1), (5, 0), (5, 1), (6, 0), (6, 1)]
    bufs = [None] * len(shapes)
    for p, nm, (o, lead) in zip(contrib, names, place):
        bufs[o] = _sum_chips(p, c_arr, bufs[o], lead, shapes[o], f"sum_chips_{nm}")
    g_a_in, g_b_in, g_a_out, g_b_out, g_wg, g_wu, g_wdt = _join_halves(bufs, place)
    g_wd = g_wdt.transpose(0, 2, 1)

    sink_row = jnp.pad(vecs["sink"][0:1], ((0, 0), (0, D_MODEL - LANES)))
    tot = _allreduce_rows([vecs["norm_mix"][0], vecs["norm_mix"][1], vecs["norm_ffn"][0], vecs["norm_ffn"][1],
                           vecs["final"], vecs["loss_cols"], sink_row])
    loss = (0.5 / D_MODEL) * jnp.sum(tot[5])
    gw = dict(a_w_in=g_a_in, a_sink=tot[6:7, :N_HEADS], a_w_out=g_a_out, b_w_in=g_b_in, b_w_out=g_b_out,
              norm_mix=tot[0:2], norm_ffn=tot[2:4], w_gate=g_wg, w_up=g_wu, w_down=g_wd, final_norm=tot[4])

    delta, new_m, new_v = {}, {}, {}
    for n in order:
        delta[n], new_m[n], new_v[n] = _adamw(weights[n], gw[n], mom[n], var[n], f"adamw_{n}")
    return (loss, gx, *[gw[n] for n in order], *[delta[n] for n in order], *[new_m[n] for n in order],
            *[new_v[n] for n in order])
```

```python
import functools
import math

import jax
import jax.numpy as jnp
from jax import lax
from jax.experimental import pallas as pl
from jax.experimental.pallas import tpu as pltpu
from jax.experimental.pallas import tpu_sc as plsc

F32 = jnp.float32
BF16 = jnp.bfloat16

D_MODEL = 1024
HEAD_DIM = 64
N_HEADS = 16
N_KV = 4
QKV_W = 1536
D_FF = 2816
N_CHIPS = 4
FF_SH = D_FF // N_CHIPS
HALF_WINDOW_A = 128
DILATED = ((128, 1), (512, 4), (2048, 16))
ROPE_THETA = 10000.0
RMS_EPS = 1e-6
NEG_INF = -1e30
LANES = 128
ADAM_LR, ADAM_B1, ADAM_B2, ADAM_EPS, ADAM_WD, ADAM_STEP = 0.001, 0.9, 0.999, 1e-08, 0.01, 10
VMEM_LIMIT = 56 * 1024 * 1024
GRAD_TOKENS = 2048
MESH = pl.DeviceIdType.MESH


def _cp(**kw):
    return pltpu.CompilerParams(vmem_limit_bytes=VMEM_LIMIT, **kw)


def _row_tile(t, cap):
    tm = min(cap, t)
    assert t % tm == 0
    return tm


def _rope_tables(seq, dil):
    inv = 1.0 / (ROPE_THETA ** (jnp.arange(0, HEAD_DIM, 2, dtype=F32) / HEAD_DIM))
    ang = jnp.arange(seq, dtype=F32)[:, None] * inv[None, :]
    cos, sin = jnp.cos(ang), jnp.sin(ang)
    cos = jnp.tile(cos, (1, 4))
    sin = jnp.concatenate([-sin, sin, -sin, sin], axis=1)

    def perm(t):
        return t.reshape(seq // dil, dil, LANES).transpose(1, 0, 2).reshape(seq, LANES)

    return perm(cos), perm(sin)


def _swap_halves(t):
    lane = lax.broadcasted_iota(jnp.int32, t.shape, 1)
    return jnp.where((lane % HEAD_DIM) < HEAD_DIM // 2, pltpu.roll(t, LANES - 32, 1), pltpu.roll(t, 32, 1))


def _rope(t, cos, sin):
    return t * cos + _swap_halves(t) * sin


def _rope_t(t, cos, sin):
    return t * cos - _swap_halves(t) * sin


def _to_residue(t, batch, dil):
    if dil == 1:
        return t
    s = t.shape[0] // batch
    return t.reshape(batch, s // dil, dil, t.shape[1]).transpose(0, 2, 1, 3).reshape(t.shape)


def _from_residue(t, batch, dil):
    if dil == 1:
        return t
    s = t.shape[0] // batch
    return t.reshape(batch, dil, s // dil, t.shape[1]).transpose(0, 2, 1, 3).reshape(t.shape)


def _rms_fwd(x, w, name, with_t=False):
    t = x.shape[0]
    tm = _row_tile(t, 512)

    def body(x_ref, w_ref, o_ref, *ot_ref):
        xv = x_ref[...]
        r = lax.rsqrt(jnp.mean(xv * xv, axis=-1, keepdims=True) + RMS_EPS)
        y = (xv * r) * w_ref[...]
        o_ref[...] = y.astype(BF16)
        if with_t:
            ot_ref[0][...] = y.T.astype(BF16)

    out_specs = [pl.BlockSpec((tm, D_MODEL), lambda i: (i, 0))]
    out_shape = [jax.ShapeDtypeStruct((t, D_MODEL), BF16)]
    if with_t:
        out_specs.append(pl.BlockSpec((D_MODEL, tm), lambda i: (0, i)))
        out_shape.append(jax.ShapeDtypeStruct((D_MODEL, t), BF16))
    outs = pl.pallas_call(
        body, name=name, grid=(t // tm,),
        in_specs=[pl.BlockSpec((tm, D_MODEL), lambda i: (i, 0)), pl.BlockSpec((1, D_MODEL), lambda i: (0, 0))],
        out_specs=out_specs, out_shape=out_shape, compiler_params=_cp(),
    )(x, w)
    return outs if with_t else outs[0]


def _rms_bwd(x, w, dhs, dres, name, with_t=False):
    t = x.shape[0]
    tm = _row_tile(t, 512)
    n = len(dhs)

    def body(*refs):
        x_ref, w_ref = refs[0], refs[1]
        dh_refs = refs[2:2 + n]
        dres_ref = refs[2 + n]
        dx_ref, dxb_ref = refs[3 + n:5 + n]
        dw_ref = refs[-1]
        xv = x_ref[...]
        r = lax.rsqrt(jnp.mean(xv * xv, axis=-1, keepdims=True) + RMS_EPS)
        xh = xv * r
        dy = dh_refs[0][...]
        for k in range(1, n):
            dy = dy + dh_refs[k][...]
        dxh = dy * w_ref[...]
        dx = dres_ref[...] + r * (dxh - xh * jnp.mean(dxh * xh, axis=-1, keepdims=True))
        dx_ref[...] = dx
        dxb_ref[...] = dx.astype(BF16)
        if with_t:
            refs[5 + n][...] = dx.T.astype(BF16)

        @pl.when(pl.program_id(0) == 0)
        def _():
            dw_ref[...] = jnp.zeros_like(dw_ref)

        dw_ref[...] += jnp.sum(dy * xh, axis=0, keepdims=True)

    row = pl.BlockSpec((tm, D_MODEL), lambda i: (i, 0))
    vec = pl.BlockSpec((1, D_MODEL), lambda i: (0, 0))
    out_specs = [row, row]
    out_shape = [jax.ShapeDtypeStruct((t, D_MODEL), F32), jax.ShapeDtypeStruct((t, D_MODEL), BF16)]
    if with_t:
        out_specs.append(pl.BlockSpec((D_MODEL, tm), lambda i: (0, i)))
        out_shape.append(jax.ShapeDtypeStruct((D_MODEL, t), BF16))
    return pl.pallas_call(
        body, name=name, grid=(t // tm,),
        in_specs=[row, vec] + [row] * n + [row],
        out_specs=out_specs + [vec], out_shape=out_shape + [jax.ShapeDtypeStruct((1, D_MODEL), F32)],
        compiler_params=_cp(),
    )(x, w, *dhs, dres)


def _final_loss(x, w, target, name):
    t = x.shape[0]
    tm = _row_tile(t, 512)

    def body(x_ref, w_ref, t_ref, dx_ref, dxb_ref, dxt_ref, l_ref, dw_ref):
        xv = x_ref[...]
        r = lax.rsqrt(jnp.mean(xv * xv, axis=-1, keepdims=True) + RMS_EPS)
        xh = xv * r
        err = xh * w_ref[...] - t_ref[...]
        dy = err * (1.0 / D_MODEL)
        dxh = dy * w_ref[...]
        dx = r * (dxh - xh * jnp.mean(dxh * xh, axis=-1, keepdims=True))
        dx_ref[...] = dx
        dxb_ref[...] = dx.astype(BF16)
        dxt_ref[...] = dx.T.astype(BF16)

        @pl.when(pl.program_id(0) == 0)
        def _():
            l_ref[...] = jnp.zeros_like(l_ref)
            dw_ref[...] = jnp.zeros_like(dw_ref)

        l_ref[...] += jnp.sum(err * err, axis=0, keepdims=True)
        dw_ref[...] += jnp.sum(dy * xh, axis=0, keepdims=True)

    row = pl.BlockSpec((tm, D_MODEL), lambda i: (i, 0))
    vec = pl.BlockSpec((1, D_MODEL), lambda i: (0, 0))
    return pl.pallas_call(
        body, name=name, grid=(t // tm,),
        in_specs=[row, vec, row], out_specs=[row, row, pl.BlockSpec((D_MODEL, tm), lambda i: (0, i)), vec, vec],
        out_shape=[jax.ShapeDtypeStruct((t, D_MODEL), F32), jax.ShapeDtypeStruct((t, D_MODEL), BF16),
                   jax.ShapeDtypeStruct((D_MODEL, t), BF16),
                   jax.ShapeDtypeStruct((1, D_MODEL), F32), jax.ShapeDtypeStruct((1, D_MODEL), F32)],
        compiler_params=_cp(),
    )(x, w, target)


def _qkv_proj(h, w, cos, sin, group, name):
    t = h.shape[0]
    seq = cos.shape[0]
    tm = _row_tile(seq, 1024)
    n_q = N_HEADS * HEAD_DIM // LANES
    n_rope = (N_HEADS + N_KV) * HEAD_DIM // LANES
    scale = 1.0 / math.sqrt(HEAD_DIM)

    def body(h_ref, w_ref, cos_ref, sin_ref, o_ref):
        acc = jnp.dot(h_ref[...], w_ref[...], preferred_element_type=F32)
        cs, sn = cos_ref[...], sin_ref[...]
        csq, snq = cs * scale, sn * scale
        for c in range(QKV_W // LANES):
            blk = acc[:, c * LANES:(c + 1) * LANES]
            if c < n_q:
                blk = _rope(blk, csq, snq)
            elif c < n_rope:
                blk = _rope(blk, cs, sn)
            o_ref[:, c * LANES:(c + 1) * LANES] = blk.astype(BF16)

    tab = pl.BlockSpec((tm, LANES), lambda i: (i % (seq // tm), 0))
    return pl.pallas_call(
        body, name=name, grid=(t // tm,),
        in_specs=[pl.BlockSpec((tm, D_MODEL), lambda i: (i, 0)),
                  pl.BlockSpec((D_MODEL, QKV_W), lambda i: (0, group)), tab, tab],
        out_specs=pl.BlockSpec((tm, QKV_W), lambda i: (i, 0)),
        out_shape=jax.ShapeDtypeStruct((t, QKV_W), BF16), compiler_params=_cp(),
    )(h, w, cos, sin)


def _mm_res(a, w, res, name):
    t, k = a.shape
    tm = _row_tile(t, 1024)

    def body(a_ref, w_ref, r_ref, o_ref):
        o_ref[...] = r_ref[...] + jnp.dot(a_ref[...], w_ref[...], preferred_element_type=F32)

    return pl.pallas_call(
        body, name=name, grid=(t // tm,),
        in_specs=[pl.BlockSpec((tm, k), lambda i: (i, 0)), pl.BlockSpec((k, D_MODEL), lambda i: (0, 0)),
                  pl.BlockSpec((tm, D_MODEL), lambda i: (i, 0))],
        out_specs=pl.BlockSpec((tm, D_MODEL), lambda i: (i, 0)),
        out_shape=jax.ShapeDtypeStruct((t, D_MODEL), F32), compiler_params=_cp(),
    )(a, w, res)


def _mm_nt(dy, w, group, out_dtype, name):
    t, n = dy.shape
    k = w.shape[0]
    tm = _row_tile(t, 1024)

    def body(dy_ref, w_ref, o_ref):
        o_ref[...] = lax.dot_general(dy_ref[...], w_ref[...], (((1,), (1,)), ((), ())),
                                     preferred_element_type=F32).astype(out_dtype)

    return pl.pallas_call(
        body, name=name, grid=(t // tm,),
        in_specs=[pl.BlockSpec((tm, n), lambda i: (i, 0)), pl.BlockSpec((k, n), lambda i: (0, group))],
        out_specs=pl.BlockSpec((tm, k), lambda i: (i, 0)),
        out_shape=jax.ShapeDtypeStruct((t, k), out_dtype), compiler_params=_cp(),
    )(dy, w)


def _out_bwd(dx, w, o, name):
    t = dx.shape[0]
    tm = _row_tile(t, 512)

    def body(dx_ref, w_ref, o_ref, et_ref, do_ref, adj_ref):
        do = lax.dot_general(dx_ref[...], w_ref[...], (((1,), (1,)), ((), ())), preferred_element_type=F32)
        do_ref[...] = do.astype(BF16)
        adj_ref[...] = -_dot_split(do * o_ref[...].astype(F32), et_ref[...])

    row = pl.BlockSpec((tm, D_MODEL), lambda i: (i, 0))
    return pl.pallas_call(
        body, name=name, grid=(t // tm,),
        in_specs=[row, pl.BlockSpec((D_MODEL, D_MODEL), lambda i: (0, 0)), row,
                  pl.BlockSpec((D_MODEL, LANES), lambda i: (0, 0))],
        out_specs=[row, pl.BlockSpec((tm, LANES), lambda i: (i, 0))],
        out_shape=[jax.ShapeDtypeStruct((t, D_MODEL), BF16), jax.ShapeDtypeStruct((t, LANES), F32)],
        compiler_params=_cp(),
    )(dx, w, o, _head_expander().T)


def _mm_tn(a, bs, name):
    aq = a.ndim == 3
    bq = bs[0].ndim == 3
    t, ka = a.shape[-2:]
    n = bs[0].shape[-1]
    nq = N_CHIPS if (aq or bq) else 1
    tt = _row_tile(t, GRAD_TOKENS)
    tn = n if n <= 1024 else 768
    assert n % tn == 0
    nb = len(bs)
    steps = t // tt

    def body(*refs):
        a_ref = refs[0]
        b_refs = refs[1:1 + nb]
        o_refs = refs[1 + nb:1 + 2 * nb]
        acc_refs = refs[1 + 2 * nb:]
        s = pl.program_id(2)
        av = a_ref[...]
        for b_ref, o_ref, acc_ref in zip(b_refs, o_refs, acc_refs):
            @pl.when(s == 0)
            def _():
                acc_ref[...] = jnp.zeros_like(acc_ref)

            acc_ref[...] += lax.dot_general(av, b_ref[...], (((0,), (0,)), ((), ())), preferred_element_type=F32)

            @pl.when(s == steps - 1)
            def _():
                o_ref[...] = acc_ref[...].astype(BF16)

    a_spec = (pl.BlockSpec((None, tt, ka), lambda q, j, s: (q, s, 0)) if aq
              else pl.BlockSpec((tt, ka), lambda q, j, s: (s, 0)))
    b_spec = (pl.BlockSpec((None, tt, tn), lambda q, j, s: (q, s, j)) if bq
              else pl.BlockSpec((tt, tn), lambda q, j, s: (s, j)))
    if nq > 1:
        o_spec = pl.BlockSpec((None, ka, tn), lambda q, j, s: (q, 0, j))
        o_shape = jax.ShapeDtypeStruct((nq, ka, n), BF16)
    else:
        o_spec = pl.BlockSpec((ka, tn), lambda q, j, s: (0, j))
        o_shape = jax.ShapeDtypeStruct((ka, n), BF16)
    outs = pl.pallas_call(
        body, name=name, grid=(nq, n // tn, steps),
        in_specs=[a_spec] + [b_spec] * nb, out_specs=[o_spec] * nb, out_shape=[o_shape] * nb,
        scratch_shapes=[pltpu.VMEM((ka, tn), F32)] * nb, compiler_params=_cp(),
    )(a, *bs)
    return outs


def _mm_grad(at, bs, name):
    ka, t = at.shape
    bq = bs[0].ndim == 3
    n = bs[0].shape[-1]
    nq = N_CHIPS if bq else 1
    tt = _row_tile(t, GRAD_TOKENS)
    tn = n if n <= 1024 else 768
    assert n % tn == 0
    nb = len(bs)
    steps = t // tt

    def body(*refs):
        a_ref = refs[0]
        b_refs = refs[1:1 + nb]
        o_refs = refs[1 + nb:1 + 2 * nb]
        acc_refs = refs[1 + 2 * nb:]
        s = pl.program_id(2)
        av = a_ref[...]
        for b_ref, o_ref, acc_ref in zip(b_refs, o_refs, acc_refs):
            @pl.when(s == 0)
            def _():
                acc_ref[...] = jnp.zeros_like(acc_ref)

            acc_ref[...] += jnp.dot(av, b_ref[...], preferred_element_type=F32)

            @pl.when(s == steps - 1)
            def _():
                o_ref[...] = acc_ref[...].astype(BF16)

    a_spec = pl.BlockSpec((ka, tt), lambda q, j, s: (0, s))
    if bq:
        b_spec = pl.BlockSpec((None, tt, tn), lambda q, j, s: (q, s, j))
        o_spec = pl.BlockSpec((None, ka, tn), lambda q, j, s: (q, 0, j))
        o_shape = jax.ShapeDtypeStruct((nq, ka, n), BF16)
    else:
        b_spec = pl.BlockSpec((tt, tn), lambda q, j, s: (s, j))
        o_spec = pl.BlockSpec((ka, tn), lambda q, j, s: (0, j))
        o_shape = jax.ShapeDtypeStruct((ka, n), BF16)
    return pl.pallas_call(
        body, name=name, grid=(nq, n // tn, steps),
        in_specs=[a_spec] + [b_spec] * nb, out_specs=[o_spec] * nb, out_shape=[o_shape] * nb,
        scratch_shapes=[pltpu.VMEM((ka, tn), F32)] * nb, compiler_params=_cp(),
    )(at, *bs)


def _sigmoid(x):
    return 1.0 / (1.0 + jnp.exp(-x))


def _ffn_up(h, wg, wu, layer, name):
    t = h.shape[0]
    tm = _row_tile(t, 1024)

    def body(h_ref, wg_ref, wu_ref, a_ref, dg_ref, du_ref):
        hv = h_ref[...]
        g = jnp.dot(hv, wg_ref[...], preferred_element_type=F32)
        u = jnp.dot(hv, wu_ref[...], preferred_element_type=F32)
        sg = _sigmoid(g)
        silu = g * sg
        a_ref[...] = (silu * u).astype(BF16)
        dg_ref[...] = (sg * (1.0 + g * (1.0 - sg)) * u).astype(BF16)
        du_ref[...] = silu.astype(BF16)

    wspec = pl.BlockSpec((None, None, D_MODEL, FF_SH), lambda q, i: (q, layer, 0, 0))
    ospec = pl.BlockSpec((None, tm, FF_SH), lambda q, i: (q, i, 0))
    oshape = jax.ShapeDtypeStruct((N_CHIPS, t, FF_SH), BF16)
    return pl.pallas_call(
        body, name=name, grid=(N_CHIPS, t // tm),
        in_specs=[pl.BlockSpec((tm, D_MODEL), lambda q, i: (i, 0)), wspec, wspec],
        out_specs=[ospec] * 3, out_shape=[oshape] * 3, compiler_params=_cp(),
    )(h, wg, wu)


def _ffn_down(a, wd, res, layer, name):
    t = a.shape[1]
    tm = _row_tile(t, 512)

    def body(a_ref, w_ref, r_ref, o_ref):
        acc = r_ref[...]
        for q in range(N_CHIPS):
            acc = acc + jnp.dot(a_ref[q], w_ref[q], preferred_element_type=F32)
        o_ref[...] = acc

    return pl.pallas_call(
        body, name=name, grid=(t // tm,),
        in_specs=[pl.BlockSpec((N_CHIPS, tm, FF_SH), lambda i: (0, i, 0)),
                  pl.BlockSpec((N_CHIPS, None, FF_SH, D_MODEL), lambda i: (0, layer, 0, 0)),
                  pl.BlockSpec((tm, D_MODEL), lambda i: (i, 0))],
        out_specs=pl.BlockSpec((tm, D_MODEL), lambda i: (i, 0)),
        out_shape=jax.ShapeDtypeStruct((t, D_MODEL), F32), compiler_params=_cp(),
    )(a, wd, res)


def _ffn_down_bwd(dx, wd, fg, fu, layer, name):
    t = dx.shape[0]
    tm = _row_tile(t, 1024)

    def body(dx_ref, w_ref, fg_ref, fu_ref, dg_ref, du_ref):
        da = lax.dot_general(dx_ref[...], w_ref[...], (((1,), (1,)), ((), ())), preferred_element_type=F32)
        dg_ref[...] = (da * fg_ref[...].astype(F32)).astype(BF16)
        du_ref[...] = (da * fu_ref[...].astype(F32)).astype(BF16)

    aspec = pl.BlockSpec((None, tm, FF_SH), lambda q, i: (q, i, 0))
    oshape = jax.ShapeDtypeStruct((N_CHIPS, t, FF_SH), BF16)
    return pl.pallas_call(
        body, name=name, grid=(N_CHIPS, t // tm),
        in_specs=[pl.BlockSpec((tm, D_MODEL), lambda q, i: (i, 0)),
                  pl.BlockSpec((None, None, FF_SH, D_MODEL), lambda q, i: (q, layer, 0, 0)), aspec, aspec],
        out_specs=[aspec] * 2, out_shape=[oshape] * 2, compiler_params=_cp(),
    )(dx, wd, fg, fu)


def _ffn_up_bwd(dg, du, wg, wu, layer, name):
    t = dg.shape[1]
    tm = _row_tile(t, 512)
    nt = (((1,), (1,)), ((), ()))

    def body(dg_ref, du_ref, wg_ref, wu_ref, o_ref):
        acc = jnp.zeros((tm, D_MODEL), F32)
        for q in range(N_CHIPS):
            acc = acc + lax.dot_general(dg_ref[q], wg_ref[q], nt, preferred_element_type=F32)
            acc = acc + lax.dot_general(du_ref[q], wu_ref[q], nt, preferred_element_type=F32)
        o_ref[...] = acc

    aspec = pl.BlockSpec((N_CHIPS, tm, FF_SH), lambda i: (0, i, 0))
    wspec = pl.BlockSpec((N_CHIPS, None, D_MODEL, FF_SH), lambda i: (0, layer, 0, 0))
    return pl.pallas_call(
        body, name=name, grid=(t // tm,),
        in_specs=[aspec, aspec, wspec, wspec],
        out_specs=pl.BlockSpec((tm, D_MODEL), lambda i: (i, 0)),
        out_shape=jax.ShapeDtypeStruct((t, D_MODEL), F32), compiler_params=_cp(),
    )(dg, du, wg, wu)


def _attn_geometry(length, half_window):
    qb = min(LANES, length)
    kw = min(qb + 2 * half_window, length)
    return qb, kw, length // qb


def _dup_kv(src_ref, dst_ref, s, length):
    ch = min(length, 256)
    lo = lax.broadcasted_iota(jnp.int32, (ch, LANES), 1) < HEAD_DIM

    def chunk(c, carry):
        r0 = pl.multiple_of(c * ch, ch)
        for j in range(N_KV // 2):
            tile = src_ref[s, pl.ds(r0, ch), j * LANES:(j + 1) * LANES].astype(F32)
            rolled = pltpu.roll(tile, HEAD_DIM, 1)
            dst_ref[2 * j, pl.ds(r0, ch), :] = jnp.where(lo, tile, rolled).astype(BF16)
            dst_ref[2 * j + 1, pl.ds(r0, ch), :] = jnp.where(lo, rolled, tile).astype(BF16)
        return carry

    lax.fori_loop(0, length // ch, chunk, 0)


def _stack_heads(ref, s, q0, qb, g):
    lo = lax.broadcasted_iota(jnp.int32, (qb, LANES), 1) < HEAD_DIM
    parts = []
    for a in range(4):
        col = (2 * g + a // 2) * LANES
        tile = ref[s, pl.ds(q0, qb), col:col + LANES]
        keep = lo if a % 2 == 0 else jnp.logical_not(lo)
        parts.append(jnp.where(keep, tile, jnp.zeros_like(tile)))
    return jnp.concatenate(parts, axis=0)


def _unstack_pair_t(stacked_t, qb, pair):
    lo = lax.broadcasted_iota(jnp.int32, (LANES, qb), 0) < HEAD_DIM
    both = jnp.where(lo, stacked_t[:, (2 * pair) * qb:(2 * pair + 1) * qb],
                     stacked_t[:, (2 * pair + 1) * qb:(2 * pair + 2) * qb])
    return both.T


def _band_mask_t(q0, k0, qb, kw, half_window):
    key = lax.broadcasted_iota(jnp.int32, (kw, 4 * qb), 0)
    qry = lax.broadcasted_iota(jnp.int32, (kw, 4 * qb), 1) & (qb - 1)
    return jnp.abs((q0 + qry) - (k0 + key)) <= half_window


def _block_origin(i, qb, kw, half_window, length):
    if isinstance(i, int):
        return i * qb, min(max(i * qb - half_window, 0), length - kw)
    return (pl.multiple_of(i * qb, qb),
            pl.multiple_of(jnp.clip(i * qb - half_window, 0, length - kw), HEAD_DIM))


def _head_row(vals, qb):
    return jnp.concatenate([jnp.broadcast_to(v, (1, qb)).astype(F32) for v in vals], axis=1)


def _attn_fwd(qkv, sink, n_seq, length, half_window, seq_blk, out_dtype, with_lse, name):
    qb, kw, nblk = _attn_geometry(length, half_window)
    with_sink = sink is not None
    nt = (((1,), (1,)), ((), ()))
    tn = (((0,), (0,)), ((), ()))
    qkv3 = qkv.reshape(n_seq, length, QKV_W)

    def body(*refs):
        refs = list(refs)
        sink_ref = refs.pop(0) if with_sink else None
        q_ref, k_ref, v_ref, o_ref = refs[:4]
        lse_ref = refs[4] if with_lse else None
        kx_ref, vx_ref = refs[-2:]
        head_row = lax.broadcasted_iota(jnp.int32, (N_HEADS, qb), 0)
        for s in range(seq_blk):
            _dup_kv(k_ref, kx_ref, s, length)
            _dup_kv(v_ref, vx_ref, s, length)

            def block(i, carry):
                q0, k0 = _block_origin(i, qb, kw, half_window, length)
                valid = _band_mask_t(q0, k0, qb, kw, half_window)
                lse_tile = jnp.zeros((N_HEADS, qb), F32)
                for g in range(N_KV):
                    qs = _stack_heads(q_ref, s, q0, qb, g)
                    kx = kx_ref[g, pl.ds(k0, kw), :]
                    vx = vx_ref[g, pl.ds(k0, kw), :]
                    st = lax.dot_general(kx, qs, nt, preferred_element_type=F32)
                    st = jnp.where(valid, st, NEG_INF)
                    m = jnp.max(st, axis=0, keepdims=True)
                    if with_sink:
                        sk = _head_row([sink_ref[4 * g + a] for a in range(4)], qb)
                        m = jnp.maximum(m, sk)
                    e = jnp.exp(st - m)
                    den = jnp.sum(e, axis=0, keepdims=True)
                    if with_sink:
                        den = den + jnp.exp(sk - m)
                    ot = lax.dot_general(vx, e.astype(BF16), tn, preferred_element_type=F32) / den
                    for pair in range(2):
                        col = (2 * g + pair) * LANES
                        o_ref[s, pl.ds(q0, qb), col:col + LANES] = _unstack_pair_t(ot, qb, pair).astype(out_dtype)
                    if with_lse:
                        lse = m + jnp.log(den)
                        for a in range(4):
                            lse_tile = jnp.where(head_row == 4 * g + a, lse[:, a * qb:(a + 1) * qb], lse_tile)
                if with_lse:
                    lse_ref[s, :, pl.ds(q0, qb)] = lse_tile
                return carry

            if nblk == 1:
                block(0, 0)
            else:
                lax.fori_loop(0, nblk, block, 0)

    in_specs = [pl.BlockSpec((seq_blk, length, N_HEADS * HEAD_DIM), lambda n: (n, 0, 0)),
                pl.BlockSpec((seq_blk, length, N_KV * HEAD_DIM), lambda n: (n, 0, 4)),
                pl.BlockSpec((seq_blk, length, N_KV * HEAD_DIM), lambda n: (n, 0, 5))]
    args = [qkv3, qkv3, qkv3]
    if with_sink:
        in_specs.insert(0, pl.BlockSpec(memory_space=pltpu.SMEM))
        args.insert(0, sink)
    out_specs = [pl.BlockSpec((seq_blk, length, D_MODEL), lambda n: (n, 0, 0))]
    out_shape = [jax.ShapeDtypeStruct((n_seq, length, D_MODEL), out_dtype)]
    if with_lse:
        out_specs.append(pl.BlockSpec((seq_blk, N_HEADS, length), lambda n: (n, 0, 0)))
        out_shape.append(jax.ShapeDtypeStruct((n_seq, N_HEADS, length), F32))
    outs = pl.pallas_call(
        body, name=name, grid=(n_seq // seq_blk,), in_specs=in_specs, out_specs=out_specs, out_shape=out_shape,
        scratch_shapes=[pltpu.VMEM((N_KV, length, LANES), BF16), pltpu.VMEM((N_KV, length, LANES), BF16)],
        compiler_params=_cp(),
    )(*args)
    o = outs[0].reshape(n_seq * length, D_MODEL)
    return (o, outs[1]) if with_lse else (o,)


def _attn_bwd(qkv, do, adj, sink, cos, sin, n_seq, length, half_window, seq_blk, dil, name):
    qb, kw, nblk = _attn_geometry(length, half_window)
    scale = 1.0 / math.sqrt(HEAD_DIM)
    with_sink = sink is not None
    nt = (((1,), (1,)), ((), ()))
    tn = (((0,), (0,)), ((), ()))
    qkv3 = qkv.reshape(n_seq, length, QKV_W)
    do3 = do.reshape(n_seq, length, D_MODEL)
    tabs = [t.reshape(dil, length, LANES) for t in (cos, sin)]
    tab_blocks = dil // seq_blk if dil >= seq_blk else 1

    def body(*refs):
        refs = list(refs)
        sink_ref = refs.pop(0) if with_sink else None
        q_ref, k_ref, v_ref, do_ref, aux_ref, cos_ref, sin_ref, dqkv_ref = refs[:8]
        ds_ref = refs[8] if with_sink else None
        kx_ref, vx_ref, dkx_ref, dvx_ref = refs[-4:]
        lane = lax.broadcasted_iota(jnp.int32, (1, LANES), 1)
        if with_sink:
            @pl.when(pl.program_id(0) == 0)
            def _():
                ds_ref[...] = jnp.zeros_like(ds_ref)

        for s in range(seq_blk):
            ts = s % dil
            _dup_kv(k_ref, kx_ref, s, length)
            _dup_kv(v_ref, vx_ref, s, length)
            dkx_ref[...] = jnp.zeros_like(dkx_ref)
            dvx_ref[...] = jnp.zeros_like(dvx_ref)

            def block(i, dsink):
                q0, k0 = _block_origin(i, qb, kw, half_window, length)
                valid = _band_mask_t(q0, k0, qb, kw, half_window)
                cs = cos_ref[ts, pl.ds(q0, qb), :] * scale
                sn = sin_ref[ts, pl.ds(q0, qb), :] * scale
                adj_tile = aux_ref[s, :, pl.ds(q0, qb)]
                for g in range(N_KV):
                    qs = _stack_heads(q_ref, s, q0, qb, g)
                    dos = _stack_heads(do_ref, s, q0, qb, g)
                    kx = kx_ref[g, pl.ds(k0, kw), :]
                    vx = vx_ref[g, pl.ds(k0, kw), :]
                    st = lax.dot_general(kx, qs, nt, preferred_element_type=F32)
                    st = jnp.where(valid, st, NEG_INF)
                    m = jnp.max(st, axis=0, keepdims=True)
                    if with_sink:
                        sk = _head_row([sink_ref[4 * g + a] for a in range(4)], qb)
                        m = jnp.maximum(m, sk)
                    e = jnp.exp(st - m)
                    den = jnp.sum(e, axis=0, keepdims=True)
                    if with_sink:
                        esk = jnp.exp(sk - m)
                        den = den + esk
                    rden = 1.0 / den
                    pt = e * rden
                    shift = _head_row([adj_tile[4 * g + a:4 * g + a + 1, :] for a in range(4)], qb)
                    dpt = lax.dot_general(vx, dos, nt, preferred_element_type=F32)
                    dst = pt * (dpt + shift)
                    if with_sink:
                        dsk = esk * rden * shift
                        for a in range(4):
                            tot = jnp.sum(dsk[:, a * qb:(a + 1) * qb], axis=1, keepdims=True)
                            dsink = dsink + jnp.where(lane == 4 * g + a, tot, 0.0)
                    dsb = dst.astype(BF16)
                    pb = pt.astype(BF16)
                    dqt = lax.dot_general(kx, dsb, tn, preferred_element_type=F32)
                    for pair in range(2):
                        col = (2 * g + pair) * LANES
                        tile = _rope_t(_unstack_pair_t(dqt, qb, pair), cs, sn)
                        dqkv_ref[s, pl.ds(q0, qb), col:col + LANES] = tile.astype(BF16)
                    dkx_ref[g, pl.ds(k0, kw), :] += jnp.dot(dsb, qs, preferred_element_type=F32)
                    dvx_ref[g, pl.ds(k0, kw), :] += jnp.dot(pb, dos, preferred_element_type=F32)
                return dsink

            if nblk == 1:
                dsink = block(0, jnp.zeros((1, LANES), F32))
            else:
                dsink = lax.fori_loop(0, nblk, block, jnp.zeros((1, LANES), F32))
            if with_sink:
                ds_ref[0:1, :] += dsink

            ch = min(length, 256)
            lo_c = lax.broadcasted_iota(jnp.int32, (ch, LANES), 1) < HEAD_DIM

            def fin(c, carry):
                r0 = pl.multiple_of(c * ch, ch)
                cs = cos_ref[ts, pl.ds(r0, ch), :]
                sn = sin_ref[ts, pl.ds(r0, ch), :]
                for j in range(N_KV // 2):
                    both = []
                    for acc_ref in (dkx_ref, dvx_ref):
                        t0 = acc_ref[2 * j, pl.ds(r0, ch), :]
                        t1 = acc_ref[2 * j + 1, pl.ds(r0, ch), :]
                        t0 = t0 + pltpu.roll(t0, HEAD_DIM, 1)
                        t1 = t1 + pltpu.roll(t1, HEAD_DIM, 1)
                        both.append(jnp.where(lo_c, t0, t1))
                    kcol = N_HEADS * HEAD_DIM + j * LANES
                    vcol = (N_HEADS + N_KV) * HEAD_DIM + j * LANES
                    dqkv_ref[s, pl.ds(r0, ch), kcol:kcol + LANES] = _rope_t(both[0], cs, sn).astype(BF16)
                    dqkv_ref[s, pl.ds(r0, ch), vcol:vcol + LANES] = both[1].astype(BF16)
                return carry

            lax.fori_loop(0, length // ch, fin, 0)

    seq_map = lambda n: (n, 0, 0)
    tab_map = (lambda n: (n % tab_blocks, 0, 0)) if dil >= seq_blk else (lambda n: (0, 0, 0))
    tab_rows = min(seq_blk, dil)
    in_specs = [pl.BlockSpec((seq_blk, length, N_HEADS * HEAD_DIM), seq_map),
                pl.BlockSpec((seq_blk, length, N_KV * HEAD_DIM), lambda n: (n, 0, 4)),
                pl.BlockSpec((seq_blk, length, N_KV * HEAD_DIM), lambda n: (n, 0, 5)),
                pl.BlockSpec((seq_blk, length, D_MODEL), seq_map),
                pl.BlockSpec((seq_blk, N_HEADS, length), seq_map),
                pl.BlockSpec((tab_rows, length, LANES), tab_map),
                pl.BlockSpec((tab_rows, length, LANES), tab_map)]
    args = [qkv3, qkv3, qkv3, do3, adj] + tabs
    if with_sink:
        in_specs.insert(0, pl.BlockSpec(memory_space=pltpu.SMEM))
        args.insert(0, sink)
    out_specs = [pl.BlockSpec((seq_blk, length, QKV_W), seq_map)]
    out_shape = [jax.ShapeDtypeStruct((n_seq, length, QKV_W), BF16)]
    if with_sink:
        out_specs.append(pl.BlockSpec((8, LANES), lambda n: (0, 0)))
        out_shape.append(jax.ShapeDtypeStruct((8, LANES), F32))
    outs = pl.pallas_call(
        body, name=name, grid=(n_seq // seq_blk,), in_specs=in_specs, out_specs=out_specs, out_shape=out_shape,
        scratch_shapes=[pltpu.VMEM((N_KV, length, LANES), BF16), pltpu.VMEM((N_KV, length, LANES), BF16),
                        pltpu.VMEM((N_KV, length, LANES), F32), pltpu.VMEM((N_KV, length, LANES), F32)],
        compiler_params=_cp(),
    )(*args)
    dqkv = outs[0].reshape(n_seq * length, QKV_W)
    return (dqkv, outs[1]) if with_sink else (dqkv, None)


def _head_expander():
    h = jnp.arange(LANES)[:, None]
    l = jnp.arange(D_MODEL)[None, :]
    return (l // HEAD_DIM == h).astype(BF16)


def _dot_split(a, e):
    hi = a.astype(BF16)
    lo = (a - hi.astype(F32)).astype(BF16)
    return jnp.dot(hi, e, preferred_element_type=F32) + jnp.dot(lo, e, preferred_element_type=F32)


def _mix_weights(lses):
    m = jnp.maximum(jnp.maximum(lses[0], lses[1]), lses[2])
    es = [jnp.exp(v - m) for v in lses]
    tot = es[0] + es[1] + es[2]
    return [e / tot for e in es]


def _mix_fwd(os_, lses, name):
    t = os_[0].shape[0]
    tm = _row_tile(t, 512)

    def body(o0, o1, o2, l0, l1, l2, e_ref, out_ref):
        wts = _mix_weights([l0[...], l1[...], l2[...]])
        acc = jnp.zeros((tm, D_MODEL), F32)
        for w, o_ref in zip(wts, (o0, o1, o2)):
            acc = acc + _dot_split(w, e_ref[...]) * o_ref[...]
        out_ref[...] = acc.astype(BF16)

    row = pl.BlockSpec((tm, D_MODEL), lambda i: (i, 0))
    lrow = pl.BlockSpec((tm, LANES), lambda i: (i, 0))
    return pl.pallas_call(
        body, name=name, grid=(t // tm,),
        in_specs=[row] * 3 + [lrow] * 3 + [pl.BlockSpec((LANES, D_MODEL), lambda i: (0, 0))],
        out_specs=row, out_shape=jax.ShapeDtypeStruct((t, D_MODEL), BF16), compiler_params=_cp(),
    )(*os_, *lses, _head_expander())


def _mix_bwd(dmix, os_, lses, name):
    t = dmix.shape[0]
    tm = _row_tile(t, 512)

    def body(d_ref, o0, o1, o2, l0, l1, l2, e_ref, et_ref, do0, do1, do2, a0, a1, a2):
        wts = _mix_weights([l0[...], l1[...], l2[...]])
        dv = d_ref[...].astype(F32)
        cs = [_dot_split(dv * o_ref[...], et_ref[...]) for o_ref in (o0, o1, o2)]
        mean_c = wts[0] * cs[0] + wts[1] * cs[1] + wts[2] * cs[2]
        for w, c, do_ref, a_ref in zip(wts, cs, (do0, do1, do2), (a0, a1, a2)):
            do_ref[...] = (_dot_split(w, e_ref[...]) * dv).astype(BF16)
            a_ref[...] = w * (c - mean_c) - w * c

    row = pl.BlockSpec((tm, D_MODEL), lambda i: (i, 0))
    lrow = pl.BlockSpec((tm, LANES), lambda i: (i, 0))
    e = _head_expander()
    return pl.pallas_call(
        body, name=name, grid=(t // tm,),
        in_specs=[row] * 4 + [lrow] * 3 + [pl.BlockSpec((LANES, D_MODEL), lambda i: (0, 0)),
                                            pl.BlockSpec((D_MODEL, LANES), lambda i: (0, 0))],
        out_specs=[row] * 3 + [lrow] * 3,
        out_shape=[jax.ShapeDtypeStruct((t, D_MODEL), BF16)] * 3 + [jax.ShapeDtypeStruct((t, LANES), F32)] * 3,
        compiler_params=_cp(),
    )(dmix, *os_, *lses, e, e.T)


def _stats_to_tokens(stat, batch, dil):
    n_seq, _, length = stat.shape
    t = stat.transpose(0, 2, 1).reshape(n_seq * length, N_HEADS)
    return _from_residue(jnp.pad(t, ((0, 0), (0, LANES - N_HEADS))), batch, dil)


def _stats_from_tokens(stat, batch, dil, n_seq, length):
    t = _to_residue(stat[:, :N_HEADS], batch, dil)
    return t.reshape(n_seq, length, N_HEADS).transpose(0, 2, 1)


def _group_geometry(batch, seq, dil, window):
    length = seq // dil
    n_seq = batch * dil
    seq_blk = max(1, min(dil, 1024 // length))
    return n_seq, length, (window // 2) // dil, seq_blk


def _local_step(x, target, a_in, a_sink, a_out, b_in, b_out, norm_mix, norm_ffn, wg, wu, wd, final_norm):
    batch, seq, _ = x.shape
    t = batch * seq
    x0 = x.reshape(t, D_MODEL)
    tgt = target.reshape(t, D_MODEL)
    tabs = {d: _rope_tables(seq, d) for _, d in DILATED}
    nm = [norm_mix[i:i + 1] for i in range(2)]
    nf = [norm_ffn[i:i + 1] for i in range(2)]

    h0, h0t = _rms_fwd(x0, nm[0], "rms_mix0", True)
    qkv0 = _qkv_proj(h0, a_in, *tabs[1], 0, "qkv0")
    (o0,) = _attn_fwd(qkv0, a_sink, batch, seq, HALF_WINDOW_A, 1, BF16, False, "attn0")
    x1 = _mm_res(o0, a_out, x0, "out0")
    hf0, hf0t = _rms_fwd(x1, nf[0], "rms_ffn0", True)
    act0, g0, u0 = _ffn_up(hf0, wg, wu, 0, "ffn_up0")
    x2 = _ffn_down(act0, wd, x1, 0, "ffn_down0")

    h1 = _rms_fwd(x2, nm[1], "rms_mix1")
    geo = [_group_geometry(batch, seq, d, w) for w, d in DILATED]
    h1g, qkv1, o1, lse1 = [], [], [], []
    for gi, (_, d) in enumerate(DILATED):
        n_seq, length, hw, sb = geo[gi]
        hp = _to_residue(h1, batch, d)
        pj = _qkv_proj(hp, b_in, *tabs[d], gi, f"qkv1_{gi}")
        o, lse = _attn_fwd(pj, None, n_seq, length, hw, sb, F32, True, f"attn1_{gi}")
        h1g.append(hp)
        qkv1.append(pj)
        o1.append(_from_residue(o, batch, d))
        lse1.append(_stats_to_tokens(lse, batch, d))
    omix = _mix_fwd(o1, lse1, "mix")
    x3 = _mm_res(omix, b_out, x2, "out1")
    hf1, hf1t = _rms_fwd(x3, nf[1], "rms_ffn1", True)
    act1, g1, u1 = _ffn_up(hf1, wg, wu, 1, "ffn_up1")
    x4 = _ffn_down(act1, wd, x3, 1, "ffn_down1")

    dx4, dx4b, dx4t, loss_cols, d_final = _final_loss(x4, final_norm.reshape(1, D_MODEL), tgt, "final_loss")

    def ffn_bwd(dxo, dxob, dxot, x_mid, hft, g, u, act, layer):
        dg, du = _ffn_down_bwd(dxob, wd, g, u, layer, f"ffn_down_bwd{layer}")
        (d_wdt,) = _mm_grad(dxot, [act], f"grad_wd{layer}")
        dh = _ffn_up_bwd(dg, du, wg, wu, layer, f"ffn_up_bwd{layer}")
        d_wg, d_wu = _mm_grad(hft, [dg, du], f"grad_wgu{layer}")
        dxm, dxmb, d_nf = _rms_bwd(x_mid, nf[layer], [dh], dxo, f"rms_ffn_bwd{layer}")
        return dxm, dxmb, d_nf, d_wg, d_wu, d_wdt

    dx3, dx3b, d_nf1, d_wg1, d_wu1, d_wd1 = ffn_bwd(dx4, dx4b, dx4t, x3, hf1t, g1, u1, act1, 1)

    dmix = _mm_nt(dx3b, b_out, 0, BF16, "out1_bwd")
    (d_b_out,) = _mm_tn(omix, [dx3b], "grad_b_out")
    mb = _mix_bwd(dmix, o1, lse1, "mix_bwd")
    dh1, d_b_in = [], []
    for gi, (_, d) in enumerate(DILATED):
        n_seq, length, hw, sb = geo[gi]
        dog = _to_residue(mb[gi], batch, d)
        adj = _stats_from_tokens(mb[3 + gi], batch, d, n_seq, length)
        dpj, _ = _attn_bwd(qkv1[gi], dog, adj, None, *tabs[d], n_seq, length, hw, sb, d, f"attn1_bwd{gi}")
        (dw,) = _mm_tn(h1g[gi], [dpj], f"grad_b_in{gi}")
        d_b_in.append(dw)
        dh1.append(_from_residue(_mm_nt(dpj, b_in, gi, F32, f"qkv1_bwd{gi}"), batch, d))
    dx2, dx2b, dx2t, d_nm1 = _rms_bwd(x2, nm[1], dh1, dx3, "rms_mix_bwd1", True)

    dx1, dx1b, d_nf0, d_wg0, d_wu0, d_wd0 = ffn_bwd(dx2, dx2b, dx2t, x1, hf0t, g0, u0, act0, 0)

    do0, adj0 = _out_bwd(dx1b, a_out, o0, "out0_bwd")
    (d_a_out,) = _mm_tn(o0, [dx1b], "grad_a_out")
    adj0 = _stats_from_tokens(adj0, batch, 1, batch, seq)
    dqkv0, d_sink = _attn_bwd(qkv0, do0, adj0, a_sink, *tabs[1], batch, seq, HALF_WINDOW_A, 1, 1, "attn0_bwd")
    (d_a_in,) = _mm_grad(h0t, [dqkv0], "grad_a_in")
    dh0 = _mm_nt(dqkv0, a_in, 0, F32, "qkv0_bwd")
    gx, _, d_nm0 = _rms_bwd(x0, nm[0], [dh0], dx1, "rms_mix_bwd0")

    grads = dict(a_in=d_a_in, a_out=d_a_out, b_in=jnp.concatenate(d_b_in, axis=1), b_out=d_b_out,
                 wg=(d_wg0, d_wg1), wu=(d_wu0, d_wu1), wd=(d_wd0, d_wd1))
    vecs = dict(norm_mix=(d_nm0, d_nm1), norm_ffn=(d_nf0, d_nf1), final=d_final, loss_cols=loss_cols, sink=d_sink)
    return gx.reshape(x.shape), grads, vecs


ANY = pl.BlockSpec(memory_space=pl.ANY)
HBM = pltpu.MemorySpace.HBM


def _me():
    return lax.axis_index("x"), lax.axis_index("y"), lax.axis_index("c")


def _chip_peer(x, y, j):
    px = 1 - x if j & 2 else x
    py = 1 - y if j & 1 else y
    return px, py, 2 * px + py


def _remote(src, dst, sems, k, dev):
    return pltpu.make_async_remote_copy(src_ref=src, dst_ref=dst, send_sem=sems[0].at[k], recv_sem=sems[1].at[k],
                                        device_id=dev, device_id_type=MESH)


def _col_window(ref, q, width):
    return ref.at[:, pl.ds(pl.multiple_of(q * width, LANES), width)]


def _half0(ref, h):
    n = ref.shape[0] // 2
    return ref.at[pl.ds(h * n, n)]


def _half1(ref, h):
    n = ref.shape[1] // 2
    return ref.at[:, pl.ds(h * n, n)]


def _half_rows(ref, h):
    n = ref.shape[-2] // 2
    if len(ref.shape) == 2:
        return ref.at[pl.ds(h * n, n)]
    return ref.at[:, pl.ds(h * n, n)]


def _place_shard(w, q_arr, col, name):
    lead, rows, cols = w.shape

    def body(q_ref, w_ref, o_ref):
        o_ref[...] = w_ref[...].astype(BF16)

    if col:
        assert lead == 1
        out_spec = pl.BlockSpec((rows, cols), lambda l, q: (0, q[0]))
        out_shape = jax.ShapeDtypeStruct((rows, N_CHIPS * cols), BF16)
    else:
        out_spec = pl.BlockSpec((None, None, rows, cols), lambda l, q: (q[0], l, 0, 0))
        out_shape = jax.ShapeDtypeStruct((N_CHIPS, lead, rows, cols), BF16)
    return pl.pallas_call(
        body, name=name,
        grid_spec=pltpu.PrefetchScalarGridSpec(
            num_scalar_prefetch=1, grid=(lead,),
            in_specs=[pl.BlockSpec((None, rows, cols), lambda l, q: (l, 0, 0))], out_specs=out_spec),
        out_shape=out_shape, compiler_params=_cp(),
    )(q_arr, w)


def _handshake(peers):
    barrier = pltpu.get_barrier_semaphore()
    for p in peers:
        pl.semaphore_signal(barrier, inc=1, device_id=p, device_id_type=MESH)
    pl.semaphore_wait(barrier, len(peers))


def _on_sequencer(name, collective_id, n_sem, n_local, body):
    @pl.kernel(mesh=plsc.ScalarSubcoreMesh(axis_name="seq", num_cores=1), name=name,
               scratch_types=(pltpu.SemaphoreType.DMA((n_sem,)), pltpu.SemaphoreType.DMA((n_sem,)),
                              pltpu.SemaphoreType.DMA((max(n_local, 1),))),
               compiler_params=pltpu.CompilerParams(collective_id=collective_id))
    def launch(send_sems, recv_sems, local_sems):
        body((send_sems, recv_sems), local_sems)

    launch()


def _gather_plan(outs, col_fam, sems, handshake):
    n_w = len(outs)
    x, y, c = _me()
    myq = 2 * x + y
    sib = (x, y, 1 - c)
    if handshake:
        _handshake([sib] + [_chip_peer(x, y, j)[:2] + (c,) for j in (1, 2, 3)])

    def slot(w, q):
        if col_fam[w]:
            return _col_window(outs[w], q, outs[w].shape[1] // N_CHIPS)
        return outs[w].at[q]

    first = []
    for w in range(n_w):
        for j in (1, 2, 3):
            px, py, _ = _chip_peer(x, y, j)
            mine = _half_rows(slot(w, myq), c)
            cp = _remote(mine, mine, sems, w * 6 + j - 1, (px, py, c))
            cp.start()
            first.append(cp)
    passed = []
    for w in range(n_w):
        for j in (1, 2, 3):
            _, _, pq = _chip_peer(x, y, j)
            land = _half_rows(slot(w, pq), c)
            _remote(land, land, sems, w * 6 + j - 1, sib).wait_recv()
            cp = _remote(land, land, sems, w * 6 + 2 + j, sib)
            cp.start()
            passed.append(cp)
    for w in range(n_w):
        for j in (1, 2, 3):
            _, _, pq = _chip_peer(x, y, j)
            land = _half_rows(slot(w, pq), 1 - c)
            _remote(land, land, sems, w * 6 + 2 + j, sib).wait_recv()
    for cp in first + passed:
        cp.wait_send()


def _gather_weights(bufs, col_fam):
    n_w = len(bufs)

    def body(*refs):
        _gather_plan(refs[n_w:2 * n_w], col_fam, refs[2 * n_w:2 * n_w + 2], False)

    return pl.pallas_call(
        body, name="gather_weights", in_specs=[ANY] * n_w, out_specs=[ANY] * n_w,
        out_shape=[jax.ShapeDtypeStruct(b.shape, b.dtype) for b in bufs],
        input_output_aliases={w: w for w in range(n_w)},
        scratch_shapes=[pltpu.SemaphoreType.DMA((6 * n_w,)), pltpu.SemaphoreType.DMA((6 * n_w,))],
    )(*bufs)


def _gather_weights_async(bufs, col_fam, name, collective_id):
    refs = [jax.new_ref(b, memory_space=HBM) for b in bufs]
    _on_sequencer(name, collective_id, 6 * len(bufs), 0,
                  lambda sems, _: _gather_plan(refs, col_fam, sems, True))
    return [r[...] for r in refs]


def _grad_half(ref, col, h):
    return _half0(ref, h) if col else _half1(ref, h)


def _swap_halves_with_sibling(grads, col_fam):
    n_w = len(grads)

    def body(*refs):
        _swap_plan(refs[:n_w], refs[n_w:2 * n_w], col_fam, refs[2 * n_w:], False)

    return pl.pallas_call(
        body, name="grad_swap_sibling", in_specs=[ANY] * n_w, out_specs=[ANY] * n_w,
        out_shape=_swap_shapes(grads, col_fam),
        scratch_shapes=[pltpu.SemaphoreType.DMA((n_w,)), pltpu.SemaphoreType.DMA((n_w,))],
    )(*grads)


def _swap_shapes(grads, col_fam):
    out = []
    for w, g in enumerate(grads):
        shp = (g.shape[0] // 2, g.shape[1]) if col_fam[w] else (g.shape[0], g.shape[1] // 2, g.shape[2])
        out.append(jax.ShapeDtypeStruct(shp, g.dtype))
    return out


def _swap_plan(ins, outs, col_fam, sems, handshake):
    x, y, c = _me()
    sib = (x, y, 1 - c)
    if handshake:
        _handshake([sib])
    cps = [_remote(_grad_half(ins[w], col_fam[w], 1 - c), outs[w], sems, w, sib) for w in range(len(ins))]
    for cp in cps:
        cp.start()
    for cp in cps:
        cp.wait_recv()
    for cp in cps:
        cp.wait_send()


def _swap_halves_async(grads, col_fam, name, collective_id):
    srcs = [jax.new_ref(g, memory_space=HBM) for g in grads]
    dsts = [jax.empty_ref(s, memory_space=HBM) for s in _swap_shapes(grads, col_fam)]
    _on_sequencer(name, collective_id, len(grads), 0, lambda sems, _: _swap_plan(srcs, dsts, col_fam, sems, True))
    return [r[...] for r in srcs], [r[...] for r in dsts]


def _half_add(mine, recv, c_arr, col, name):
    if col:
        rows, n = recv.shape
        tr = rows // 2
        grid = (2,)
        in_specs = [pl.BlockSpec((tr, n), lambda i, c: (2 * c[0] + i, 0)), pl.BlockSpec((tr, n), lambda i, c: (i, 0))]
        out_spec = pl.BlockSpec((tr, n), lambda i, c: (i, 0))
    else:
        _, rows, n = recv.shape
        grid = (N_CHIPS,)
        in_specs = [pl.BlockSpec((None, rows, n), lambda q, c: (q, c[0], 0)),
                    pl.BlockSpec((None, rows, n), lambda q, c: (q, 0, 0))]
        out_spec = pl.BlockSpec((None, rows, n), lambda q, c: (q, 0, 0))

    def body(c_ref, a_ref, b_ref, o_ref):
        o_ref[...] = (a_ref[...].astype(F32) + b_ref[...].astype(F32)).astype(BF16)

    return pl.pallas_call(
        body, name=name,
        grid_spec=pltpu.PrefetchScalarGridSpec(num_scalar_prefetch=1, grid=grid, in_specs=in_specs, out_specs=out_spec),
        out_shape=jax.ShapeDtypeStruct(recv.shape, BF16), compiler_params=_cp(),
    )(c_arr, mine, recv)


def _scatter_chip_sums(sums, col_fam):
    n_w = len(sums)

    def body(*refs):
        _scatter_plan(refs[:n_w], refs[n_w:2 * n_w], col_fam, refs[2 * n_w:2 * n_w + 2], refs[2 * n_w + 2], False)

    return pl.pallas_call(
        body, name="grad_scatter_chips", in_specs=[ANY] * n_w, out_specs=[ANY] * n_w,
        out_shape=_scatter_shapes(sums, col_fam),
        scratch_shapes=[pltpu.SemaphoreType.DMA((3 * n_w,)), pltpu.SemaphoreType.DMA((3 * n_w,)),
                        pltpu.SemaphoreType.DMA((n_w,))],
    )(*sums)


def _scatter_shapes(sums, col_fam):
    out = []
    for w, s in enumerate(sums):
        shp = (s.shape[0], s.shape[1] // N_CHIPS) if col_fam[w] else s.shape[1:]
        out.append(jax.ShapeDtypeStruct((N_CHIPS,) + shp, s.dtype))
    return out


def _scatter_plan(ins, outs, col_fam, sems, lsem, handshake):
    n_w = len(ins)
    x, y, c = _me()
    myq = 2 * x + y
    if handshake:
        _handshake([_chip_peer(x, y, j)[:2] + (c,) for j in (1, 2, 3)])

    def slab(w, q):
        if col_fam[w]:
            return _col_window(ins[w], q, ins[w].shape[1] // N_CHIPS)
        return ins[w].at[q]

    local = [pltpu.make_async_copy(slab(w, myq), outs[w].at[myq], lsem.at[w]) for w in range(n_w)]
    for cp in local:
        cp.start()
    cps = []
    for w in range(n_w):
        for j in (1, 2, 3):
            px, py, pq = _chip_peer(x, y, j)
            cp = _remote(slab(w, pq), outs[w].at[myq], sems, w * 3 + j - 1, (px, py, c))
            cp.start()
            cps.append(cp)
    for w in range(n_w):
        for j in (1, 2, 3):
            _, _, pq = _chip_peer(x, y, j)
            land = outs[w].at[pq]
            _remote(land, land, sems, w * 3 + j - 1, (x, y, c)).wait_recv()
    for cp in cps:
        cp.wait_send()
    for cp in local:
        cp.wait()


def _scatter_chip_sums_async(sums, col_fam, name, collective_id):
    srcs = [jax.new_ref(s, memory_space=HBM) for s in sums]
    dsts = [jax.empty_ref(s, memory_space=HBM) for s in _scatter_shapes(sums, col_fam)]
    _on_sequencer(name, collective_id, 3 * len(sums), len(sums),
                  lambda sems, lsem: _scatter_plan(srcs, dsts, col_fam, sems, lsem, True))
    return [r[...] for r in dsts]


def _sum_chips(parts, c_arr, prev, lead, shape, name):
    _, rows, n = parts.shape
    tr = rows // 2 if rows % 32 == 0 else rows
    nblk = rows // tr

    def body(c_ref, p_ref, *rest):
        o_ref = rest[-1]
        acc = p_ref[0].astype(F32)
        for q in range(1, N_CHIPS):
            acc = acc + p_ref[q].astype(F32)
        o_ref[...] = acc

    in_specs = [pl.BlockSpec((N_CHIPS, tr, n), lambda i, c: (0, i, 0))]
    args = [c_arr, parts]
    aliases = {}
    if prev is not None:
        in_specs.append(ANY)
        args.append(prev)
        aliases = {2: 0}
    return pl.pallas_call(
        body, name=name,
        grid_spec=pltpu.PrefetchScalarGridSpec(
            num_scalar_prefetch=1, grid=(nblk,), in_specs=in_specs,
            out_specs=pl.BlockSpec((None, tr, n), lambda i, c: (lead, c[0] * nblk + i, 0))),
        out_shape=jax.ShapeDtypeStruct(shape, F32), input_output_aliases=aliases, compiler_params=_cp(),
    )(*args)


def _join_plan(outs, place, sems, handshake):
    x, y, c = _me()
    sib = (x, y, 1 - c)
    if handshake:
        _handshake([sib])

    def half(k, h):
        o, lead = place[k]
        return _half_rows(outs[o].at[lead], h)

    cps = [_remote(half(k, c), half(k, c), sems, k, sib) for k in range(len(place))]
    for cp in cps:
        cp.start()
    for k in range(len(place)):
        land = half(k, 1 - c)
        _remote(land, land, sems, k, sib).wait_recv()
    for cp in cps:
        cp.wait_send()


def _join_halves(bufs, place, name):
    n_o = len(bufs)
    n_h = len(place)

    def body(*refs):
        _join_plan(refs[n_o:2 * n_o], place, refs[2 * n_o:2 * n_o + 2], False)

    return pl.pallas_call(
        body, name=name, in_specs=[ANY] * n_o, out_specs=[ANY] * n_o,
        out_shape=[jax.ShapeDtypeStruct(b.shape, b.dtype) for b in bufs],
        input_output_aliases={k: k for k in range(n_o)},
        scratch_shapes=[pltpu.SemaphoreType.DMA((n_h,)), pltpu.SemaphoreType.DMA((n_h,))],
    )(*bufs)


def _join_halves_async(bufs, place, name, collective_id):
    refs = [jax.new_ref(b, memory_space=HBM) for b in bufs]
    _on_sequencer(name, collective_id, len(place), 0, lambda sems, _: _join_plan(refs, place, sems, True))
    return [r[...] for r in refs]


def _allreduce_rows(rows):
    n_dev = 8
    n_r = len(rows)
    assert n_r <= 8

    def body(*refs):
        r_refs = refs[:n_r]
        o_ref, slots, send_sems, recv_sems = refs[n_r:]
        x, y, c = _me()
        me = 4 * x + 2 * y + c
        slots[me] = jnp.concatenate([r[...] for r in r_refs] + [jnp.zeros((8 - n_r, D_MODEL), F32)], axis=0)

        def peer(k):
            return (1 - x if k & 4 else x, 1 - y if k & 2 else y, 1 - c if k & 1 else c)

        cps = []
        for k in range(1, n_dev):
            cp = pltpu.make_async_remote_copy(src_ref=slots.at[me], dst_ref=slots.at[me], send_sem=send_sems.at[k - 1],
                                              recv_sem=recv_sems.at[k - 1], device_id=peer(k), device_id_type=MESH)
            cp.start()
            cps.append(cp)
        for k in range(1, n_dev):
            px, py, pc = peer(k)
            land = slots.at[4 * px + 2 * py + pc]
            pltpu.make_async_remote_copy(src_ref=land, dst_ref=land, send_sem=send_sems.at[k - 1],
                                         recv_sem=recv_sems.at[k - 1], device_id=peer(k),
                                         device_id_type=MESH).wait_recv()
        for cp in cps:
            cp.wait_send()
        acc = slots[0]
        for d in range(1, n_dev):
            acc = acc + slots[d]
        o_ref[...] = acc

    vm = pl.BlockSpec(memory_space=pltpu.VMEM)
    return pl.pallas_call(
        body, name="allreduce_rows", in_specs=[vm] * n_r, out_specs=vm,
        out_shape=jax.ShapeDtypeStruct((8, D_MODEL), F32),
        scratch_shapes=[pltpu.VMEM((n_dev, 8, D_MODEL), F32), pltpu.SemaphoreType.DMA((n_dev - 1,)),
                        pltpu.SemaphoreType.DMA((n_dev - 1,))],
    )(*rows)


def _adamw(w, g, m, v, name):
    shape = w.shape
    if len(shape) == 1:
        lead, rows, cols = 1, 1, shape[0]
    else:
        rows, cols = shape[-2:]
        lead = math.prod(shape[:-2])
    args = [a.reshape(lead, rows, cols) for a in (w, g, m, v)]
    tr = rows // 2 if rows % 16 == 0 else rows

    def body(w_ref, g_ref, m_ref, v_ref, d_ref, nm_ref, nv_ref):
        gv = g_ref[...]
        nm = ADAM_B1 * m_ref[...] + (1.0 - ADAM_B1) * gv
        nv = ADAM_B2 * v_ref[...] + (1.0 - ADAM_B2) * jnp.square(gv)
        m_hat = nm / (1.0 - ADAM_B1 ** ADAM_STEP)
        v_hat = nv / (1.0 - ADAM_B2 ** ADAM_STEP)
        d_ref[...] = -ADAM_LR * (m_hat / (jnp.sqrt(v_hat) + ADAM_EPS) + ADAM_WD * w_ref[...])
        nm_ref[...] = nm
        nv_ref[...] = nv

    spec = pl.BlockSpec((None, tr, cols), lambda l, i: (l, i, 0))
    outs = pl.pallas_call(
        body, name=name, grid=(lead, rows // tr), in_specs=[spec] * 4, out_specs=[spec] * 3,
        out_shape=[jax.ShapeDtypeStruct((lead, rows, cols), F32)] * 3, compiler_params=_cp(),
    )(*args)
    return [o.reshape(shape) for o in outs]


def kernel(x, a_w_in, a_sink, a_w_out, b_w_in, b_w_out, norm_mix, norm_ffn, w_gate, w_up, w_down, final_norm, loss_target, m_a_w_in, m_a_sink, m_a_w_out, m_b_w_in, m_b_w_out, m_norm_mix, m_norm_ffn, m_w_gate, m_w_up, m_w_down, m_final_norm, v_a_w_in, v_a_sink, v_a_w_out, v_b_w_in, v_b_w_out, v_norm_mix, v_norm_ffn, v_w_gate, v_w_up, v_w_down, v_final_norm):
    weights = dict(a_w_in=a_w_in, a_sink=a_sink, a_w_out=a_w_out, b_w_in=b_w_in, b_w_out=b_w_out, norm_mix=norm_mix,
                   norm_ffn=norm_ffn, w_gate=w_gate, w_up=w_up, w_down=w_down, final_norm=final_norm)
    mom = dict(a_w_in=m_a_w_in, a_sink=m_a_sink, a_w_out=m_a_w_out, b_w_in=m_b_w_in, b_w_out=m_b_w_out,
               norm_mix=m_norm_mix, norm_ffn=m_norm_ffn, w_gate=m_w_gate, w_up=m_w_up, w_down=m_w_down,
               final_norm=m_final_norm)
    var = dict(a_w_in=v_a_w_in, a_sink=v_a_sink, a_w_out=v_a_w_out, b_w_in=v_b_w_in, b_w_out=v_b_w_out,
               norm_mix=v_norm_mix, norm_ffn=v_norm_ffn, w_gate=v_w_gate, w_up=v_w_up, w_down=v_w_down,
               final_norm=v_final_norm)
    order = ["a_w_in", "a_sink", "a_w_out", "b_w_in", "b_w_out", "norm_mix", "norm_ffn", "w_gate", "w_up", "w_down",
             "final_norm"]

    c_arr = lax.axis_index("c").astype(jnp.int32).reshape(1)
    q_arr = (2 * lax.axis_index("x") + lax.axis_index("y")).astype(jnp.int32).reshape(1)
    shards = [a_w_in, a_w_out, b_w_in, b_w_out, w_gate, w_up, w_down]
    shard_names = ("a_in", "a_out", "b_in", "b_out", "wg", "wu", "wd")
    placed = [_place_shard(s, q_arr, col, f"place_{nm}")
              for s, col, nm in zip(shards, (True, False, True, False, False, False, False), shard_names)]
    (a_in,) = _gather_weights(placed[:1], (True,))
    a_out, b_in, b_out, wg, wu, wd = _gather_weights_async(placed[1:], (False, True, False, False, False, False),
                                                           "gather_weights_late", 1)
    a_out = a_out.reshape(D_MODEL, D_MODEL)
    b_out = b_out.reshape(D_MODEL, D_MODEL)

    gx, grads, vecs = _local_step(x, loss_target, a_in, a_sink[0], a_out, b_in, b_out, norm_mix, norm_ffn, wg, wu, wd,
                                  final_norm)

    rows_out = D_MODEL // N_CHIPS
    partials = [grads["a_in"], grads["b_in"],
                grads["a_out"].reshape(N_CHIPS, rows_out, D_MODEL), grads["b_out"].reshape(N_CHIPS, rows_out, D_MODEL),
                grads["wg"][0], grads["wg"][1], grads["wu"][0], grads["wu"][1], grads["wd"][0], grads["wd"][1]]
    col_fam = (True, True) + (False,) * 8
    names = ("a_in", "b_in", "a_out", "b_out", "wg0", "wg1", "wu0", "wu1", "wd0", "wd1")
    contrib = [None] * len(partials)

    def reduce_group(idx, tag, ids):
        parts = [partials[k] for k in idx]
        cols = tuple(col_fam[k] for k in idx)
        if ids is None:
            theirs = _swap_halves_with_sibling(parts, cols)
        else:
            parts, theirs = _swap_halves_async(parts, cols, f"grad_swap_{tag}", ids[0])
        sums = [_half_add(p, r, c_arr, cf, f"chip_sum_{names[k]}") for p, r, cf, k in zip(parts, theirs, cols, idx)]
        if ids is None:
            out = _scatter_chip_sums(sums, cols)
        else:
            out = _scatter_chip_sums_async(sums, cols, f"grad_scatter_{tag}", ids[1])
        for k, o in zip(idx, out):
            contrib[k] = o

    reduce_group([1, 3, 5, 7, 9], "layer1", (2, 3))
    reduce_group([2, 4, 6, 8], "ffn0", (4, 5))
    reduce_group([0], "a_in", (6, 7))
    shapes = [a_w_in.shape, b_w_in.shape, a_w_out.shape, b_w_out.shape, w_gate.shape, w_up.shape, w_gate.shape]
    place = [(0, 0), (1, 0), (2, 0), (3, 0), (4, 0), (4, 1), (5, 0), (5, 1), (6, 0), (6, 1)]
    bufs = [None] * len(shapes)
    for p, nm, (o, lead) in zip(contrib, names, place):
        bufs[o] = _sum_chips(p, c_arr, bufs[o], lead, shapes[o], f"sum_chips_{nm}")
    g_b_in, g_a_out, g_b_out, g_wg, g_wu, g_wdt = _join_halves_async(
        bufs[1:], [(o - 1, lead) for o, lead in place[1:]], "grad_join_late", 8)
    (g_a_in,) = _join_halves(bufs[:1], place[:1], "grad_join_a_in")
    g_wd = g_wdt.transpose(0, 2, 1)

    sink_row = jnp.pad(vecs["sink"][0:1], ((0, 0), (0, D_MODEL - LANES)))
    tot = _allreduce_rows([vecs["norm_mix"][0], vecs["norm_mix"][1], vecs["norm_ffn"][0], vecs["norm_ffn"][1],
                           vecs["final"], vecs["loss_cols"], sink_row])
    loss = (0.5 / D_MODEL) * jnp.sum(tot[5])
    gw = dict(a_w_in=g_a_in, a_sink=tot[6:7, :N_HEADS], a_w_out=g_a_out, b_w_in=g_b_in, b_w_out=g_b_out,
              norm_mix=tot[0:2], norm_ffn=tot[2:4], w_gate=g_wg, w_up=g_wu, w_down=g_wd, final_norm=tot[4])

    delta, new_m, new_v = {}, {}, {}
    for n in order:
        delta[n], new_m[n], new_v[n] = _adamw(weights[n], gw[n], mom[n], var[n], f"adamw_{n}")
    return (loss, gx, *[gw[n] for n in order], *[delta[n] for n in order], *[new_m[n] for n in order],
            *[new_v[n] for n in order])
```

```python
import functools
import math

import jax
import jax.numpy as jnp
from jax import lax
from jax.experimental import pallas as pl
from jax.experimental.pallas import tpu as pltpu
from jax.experimental.pallas import tpu_sc as plsc

F32 = jnp.float32
BF16 = jnp.bfloat16

D_MODEL = 1024
HEAD_DIM = 64
N_HEADS = 16
N_KV = 4
QKV_W = 1536
D_FF = 2816
N_CHIPS = 4
FF_SH = D_FF // N_CHIPS
HALF_WINDOW_A = 128
DILATED = ((128, 1), (512, 4), (2048, 16))
ROPE_THETA = 10000.0
RMS_EPS = 1e-6
NEG_INF = -1e30
LANES = 128
ADAM_LR, ADAM_B1, ADAM_B2, ADAM_EPS, ADAM_WD, ADAM_STEP = 0.001, 0.9, 0.999, 1e-08, 0.01, 10
VMEM_LIMIT = 56 * 1024 * 1024
GRAD_TOKENS = 2048
MESH = pl.DeviceIdType.MESH


def _cp(**kw):
    return pltpu.CompilerParams(vmem_limit_bytes=VMEM_LIMIT, **kw)


def _row_tile(t, cap):
    tm = min(cap, t)
    assert t % tm == 0
    return tm


def _rope_tables(seq, dil):
    inv = 1.0 / (ROPE_THETA ** (jnp.arange(0, HEAD_DIM, 2, dtype=F32) / HEAD_DIM))
    ang = jnp.arange(seq, dtype=F32)[:, None] * inv[None, :]
    cos, sin = jnp.cos(ang), jnp.sin(ang)
    cos = jnp.tile(cos, (1, 4))
    sin = jnp.concatenate([-sin, sin, -sin, sin], axis=1)

    def perm(t):
        return t.reshape(seq // dil, dil, LANES).transpose(1, 0, 2).reshape(seq, LANES)

    return perm(cos), perm(sin)


def _swap_halves(t):
    lane = lax.broadcasted_iota(jnp.int32, t.shape, 1)
    return jnp.where((lane % HEAD_DIM) < HEAD_DIM // 2, pltpu.roll(t, LANES - 32, 1), pltpu.roll(t, 32, 1))


def _rope(t, cos, sin):
    return t * cos + _swap_halves(t) * sin


def _rope_t(t, cos, sin):
    return t * cos - _swap_halves(t) * sin


def _to_residue(t, batch, dil):
    if dil == 1:
        return t
    s = t.shape[0] // batch
    return t.reshape(batch, s // dil, dil, t.shape[1]).transpose(0, 2, 1, 3).reshape(t.shape)


def _from_residue(t, batch, dil):
    if dil == 1:
        return t
    s = t.shape[0] // batch
    return t.reshape(batch, dil, s // dil, t.shape[1]).transpose(0, 2, 1, 3).reshape(t.shape)


def _rms_fwd(x, w, name, with_t=False):
    t = x.shape[0]
    tm = _row_tile(t, 512)

    def body(x_ref, w_ref, o_ref, *ot_ref):
        xv = x_ref[...]
        r = lax.rsqrt(jnp.mean(xv * xv, axis=-1, keepdims=True) + RMS_EPS)
        y = (xv * r) * w_ref[...]
        o_ref[...] = y.astype(BF16)
        if with_t:
            ot_ref[0][...] = y.T.astype(BF16)

    out_specs = [pl.BlockSpec((tm, D_MODEL), lambda i: (i, 0))]
    out_shape = [jax.ShapeDtypeStruct((t, D_MODEL), BF16)]
    if with_t:
        out_specs.append(pl.BlockSpec((D_MODEL, tm), lambda i: (0, i)))
        out_shape.append(jax.ShapeDtypeStruct((D_MODEL, t), BF16))
    outs = pl.pallas_call(
        body, name=name, grid=(t // tm,),
        in_specs=[pl.BlockSpec((tm, D_MODEL), lambda i: (i, 0)), pl.BlockSpec((1, D_MODEL), lambda i: (0, 0))],
        out_specs=out_specs, out_shape=out_shape, compiler_params=_cp(),
    )(x, w)
    return outs if with_t else outs[0]


def _rms_bwd(x, w, dhs, dres, name, with_t=False):
    t = x.shape[0]
    tm = _row_tile(t, 512)
    n = len(dhs)

    def body(*refs):
        x_ref, w_ref = refs[0], refs[1]
        dh_refs = refs[2:2 + n]
        dres_ref = refs[2 + n]
        dx_ref, dxb_ref = refs[3 + n:5 + n]
        dw_ref = refs[-1]
        xv = x_ref[...]
        r = lax.rsqrt(jnp.mean(xv * xv, axis=-1, keepdims=True) + RMS_EPS)
        xh = xv * r
        dy = dh_refs[0][...]
        for k in range(1, n):
            dy = dy + dh_refs[k][...]
        dxh = dy * w_ref[...]
        dx = dres_ref[...] + r * (dxh - xh * jnp.mean(dxh * xh, axis=-1, keepdims=True))
        dx_ref[...] = dx
        dxb_ref[...] = dx.astype(BF16)
        if with_t:
            refs[5 + n][...] = dx.T.astype(BF16)

        @pl.when(pl.program_id(0) == 0)
        def _():
            dw_ref[...] = jnp.zeros_like(dw_ref)

        dw_ref[...] += jnp.sum(dy * xh, axis=0, keepdims=True)

    row = pl.BlockSpec((tm, D_MODEL), lambda i: (i, 0))
    vec = pl.BlockSpec((1, D_MODEL), lambda i: (0, 0))
    out_specs = [row, row]
    out_shape = [jax.ShapeDtypeStruct((t, D_MODEL), F32), jax.ShapeDtypeStruct((t, D_MODEL), BF16)]
    if with_t:
        out_specs.append(pl.BlockSpec((D_MODEL, tm), lambda i: (0, i)))
        out_shape.append(jax.ShapeDtypeStruct((D_MODEL, t), BF16))
    return pl.pallas_call(
        body, name=name, grid=(t // tm,),
        in_specs=[row, vec] + [row] * n + [row],
        out_specs=out_specs + [vec], out_shape=out_shape + [jax.ShapeDtypeStruct((1, D_MODEL), F32)],
        compiler_params=_cp(),
    )(x, w, *dhs, dres)


def _final_loss(x, w, target, name):
    t = x.shape[0]
    tm = _row_tile(t, 512)

    def body(x_ref, w_ref, t_ref, dx_ref, dxb_ref, dxt_ref, l_ref, dw_ref):
        xv = x_ref[...]
        r = lax.rsqrt(jnp.mean(xv * xv, axis=-1, keepdims=True) + RMS_EPS)
        xh = xv * r
        err = xh * w_ref[...] - t_ref[...]
        dy = err * (1.0 / D_MODEL)
        dxh = dy * w_ref[...]
        dx = r * (dxh - xh * jnp.mean(dxh * xh, axis=-1, keepdims=True))
        dx_ref[...] = dx
        dxb_ref[...] = dx.astype(BF16)
        dxt_ref[...] = dx.T.astype(BF16)

        @pl.when(pl.program_id(0) == 0)
        def _():
            l_ref[...] = jnp.zeros_like(l_ref)
            dw_ref[...] = jnp.zeros_like(dw_ref)

        l_ref[...] += jnp.sum(err * err, axis=0, keepdims=True)
        dw_ref[...] += jnp.sum(dy * xh, axis=0, keepdims=True)

    row = pl.BlockSpec((tm, D_MODEL), lambda i: (i, 0))
    vec = pl.BlockSpec((1, D_MODEL), lambda i: (0, 0))
    return pl.pallas_call(
        body, name=name, grid=(t // tm,),
        in_specs=[row, vec, row], out_specs=[row, row, pl.BlockSpec((D_MODEL, tm), lambda i: (0, i)), vec, vec],
        out_shape=[jax.ShapeDtypeStruct((t, D_MODEL), F32), jax.ShapeDtypeStruct((t, D_MODEL), BF16),
                   jax.ShapeDtypeStruct((D_MODEL, t), BF16),
                   jax.ShapeDtypeStruct((1, D_MODEL), F32), jax.ShapeDtypeStruct((1, D_MODEL), F32)],
        compiler_params=_cp(),
    )(x, w, target)


def _qkv_proj(h, w, cos, sin, group, name):
    t = h.shape[0]
    seq = cos.shape[0]
    tm = _row_tile(seq, 1024)
    n_q = N_HEADS * HEAD_DIM // LANES
    n_rope = (N_HEADS + N_KV) * HEAD_DIM // LANES
    scale = 1.0 / math.sqrt(HEAD_DIM)

    def body(h_ref, w_ref, cos_ref, sin_ref, o_ref):
        acc = jnp.dot(h_ref[...], w_ref[...], preferred_element_type=F32)
        cs, sn = cos_ref[...], sin_ref[...]
        csq, snq = cs * scale, sn * scale
        for c in range(QKV_W // LANES):
            blk = acc[:, c * LANES:(c + 1) * LANES]
            if c < n_q:
                blk = _rope(blk, csq, snq)
            elif c < n_rope:
                blk = _rope(blk, cs, sn)
            o_ref[:, c * LANES:(c + 1) * LANES] = blk.astype(BF16)

    tab = pl.BlockSpec((tm, LANES), lambda i: (i % (seq // tm), 0))
    return pl.pallas_call(
        body, name=name, grid=(t // tm,),
        in_specs=[pl.BlockSpec((tm, D_MODEL), lambda i: (i, 0)),
                  pl.BlockSpec((D_MODEL, QKV_W), lambda i: (0, group)), tab, tab],
        out_specs=pl.BlockSpec((tm, QKV_W), lambda i: (i, 0)),
        out_shape=jax.ShapeDtypeStruct((t, QKV_W), BF16), compiler_params=_cp(),
    )(h, w, cos, sin)


def _mm_res(a, w, res, name):
    t, k = a.shape
    tm = _row_tile(t, 1024)

    def body(a_ref, w_ref, r_ref, o_ref):
        o_ref[...] = r_ref[...] + jnp.dot(a_ref[...], w_ref[...], preferred_element_type=F32)

    return pl.pallas_call(
        body, name=name, grid=(t // tm,),
        in_specs=[pl.BlockSpec((tm, k), lambda i: (i, 0)), pl.BlockSpec((k, D_MODEL), lambda i: (0, 0)),
                  pl.BlockSpec((tm, D_MODEL), lambda i: (i, 0))],
        out_specs=pl.BlockSpec((tm, D_MODEL), lambda i: (i, 0)),
        out_shape=jax.ShapeDtypeStruct((t, D_MODEL), F32), compiler_params=_cp(),
    )(a, w, res)


def _mm_nt(dy, w, group, out_dtype, name):
    t, n = dy.shape
    k = w.shape[0]
    tm = _row_tile(t, 1024)

    def body(dy_ref, w_ref, o_ref):
        o_ref[...] = lax.dot_general(dy_ref[...], w_ref[...], (((1,), (1,)), ((), ())),
                                     preferred_element_type=F32).astype(out_dtype)

    return pl.pallas_call(
        body, name=name, grid=(t // tm,),
        in_specs=[pl.BlockSpec((tm, n), lambda i: (i, 0)), pl.BlockSpec((k, n), lambda i: (0, group))],
        out_specs=pl.BlockSpec((tm, k), lambda i: (i, 0)),
        out_shape=jax.ShapeDtypeStruct((t, k), out_dtype), compiler_params=_cp(),
    )(dy, w)


def _out_bwd(dx, w, o, name):
    t = dx.shape[0]
    tm = _row_tile(t, 512)

    def body(dx_ref, w_ref, o_ref, et_ref, do_ref, adj_ref):
        do = lax.dot_general(dx_ref[...], w_ref[...], (((1,), (1,)), ((), ())), preferred_element_type=F32)
        do_ref[...] = do.astype(BF16)
        adj_ref[...] = -_dot_split(do * o_ref[...].astype(F32), et_ref[...])

    row = pl.BlockSpec((tm, D_MODEL), lambda i: (i, 0))
    return pl.pallas_call(
        body, name=name, grid=(t // tm,),
        in_specs=[row, pl.BlockSpec((D_MODEL, D_MODEL), lambda i: (0, 0)), row,
                  pl.BlockSpec((D_MODEL, LANES), lambda i: (0, 0))],
        out_specs=[row, pl.BlockSpec((tm, LANES), lambda i: (i, 0))],
        out_shape=[jax.ShapeDtypeStruct((t, D_MODEL), BF16), jax.ShapeDtypeStruct((t, LANES), F32)],
        compiler_params=_cp(),
    )(dx, w, o, _head_expander().T)


def _mm_tn(a, bs, name):
    aq = a.ndim == 3
    bq = bs[0].ndim == 3
    t, ka = a.shape[-2:]
    n = bs[0].shape[-1]
    nq = N_CHIPS if (aq or bq) else 1
    tt = _row_tile(t, GRAD_TOKENS)
    tn = n if n <= 1024 else 768
    assert n % tn == 0
    nb = len(bs)
    steps = t // tt

    def body(*refs):
        a_ref = refs[0]
        b_refs = refs[1:1 + nb]
        o_refs = refs[1 + nb:1 + 2 * nb]
        acc_refs = refs[1 + 2 * nb:]
        s = pl.program_id(2)
        av = a_ref[...]
        for b_ref, o_ref, acc_ref in zip(b_refs, o_refs, acc_refs):
            @pl.when(s == 0)
            def _():
                acc_ref[...] = jnp.zeros_like(acc_ref)

            acc_ref[...] += lax.dot_general(av, b_ref[...], (((0,), (0,)), ((), ())), preferred_element_type=F32)

            @pl.when(s == steps - 1)
            def _():
                o_ref[...] = acc_ref[...].astype(BF16)

    a_spec = (pl.BlockSpec((None, tt, ka), lambda q, j, s: (q, s, 0)) if aq
              else pl.BlockSpec((tt, ka), lambda q, j, s: (s, 0)))
    b_spec = (pl.BlockSpec((None, tt, tn), lambda q, j, s: (q, s, j)) if bq
              else pl.BlockSpec((tt, tn), lambda q, j, s: (s, j)))
    if nq > 1:
        o_spec = pl.BlockSpec((None, ka, tn), lambda q, j, s: (q, 0, j))
        o_shape = jax.ShapeDtypeStruct((nq, ka, n), BF16)
    else:
        o_spec = pl.BlockSpec((ka, tn), lambda q, j, s: (0, j))
        o_shape = jax.ShapeDtypeStruct((ka, n), BF16)
    outs = pl.pallas_call(
        body, name=name, grid=(nq, n // tn, steps),
        in_specs=[a_spec] + [b_spec] * nb, out_specs=[o_spec] * nb, out_shape=[o_shape] * nb,
        scratch_shapes=[pltpu.VMEM((ka, tn), F32)] * nb, compiler_params=_cp(),
    )(a, *bs)
    return outs


def _mm_grad(at, bs, name):
    ka, t = at.shape
    bq = bs[0].ndim == 3
    n = bs[0].shape[-1]
    nq = N_CHIPS if bq else 1
    tt = _row_tile(t, GRAD_TOKENS)
    tn = n if n <= 1024 else 768
    assert n % tn == 0
    nb = len(bs)
    steps = t // tt

    def body(*refs):
        a_ref = refs[0]
        b_refs = refs[1:1 + nb]
        o_refs = refs[1 + nb:1 + 2 * nb]
        acc_refs = refs[1 + 2 * nb:]
        s = pl.program_id(2)
        av = a_ref[...]
        for b_ref, o_ref, acc_ref in zip(b_refs, o_refs, acc_refs):
            @pl.when(s == 0)
            def _():
                acc_ref[...] = jnp.zeros_like(acc_ref)

            acc_ref[...] += jnp.dot(av, b_ref[...], preferred_element_type=F32)

            @pl.when(s == steps - 1)
            def _():
                o_ref[...] = acc_ref[...].astype(BF16)

    a_spec = pl.BlockSpec((ka, tt), lambda q, j, s: (0, s))
    if bq:
        b_spec = pl.BlockSpec((None, tt, tn), lambda q, j, s: (q, s, j))
        o_spec = pl.BlockSpec((None, ka, tn), lambda q, j, s: (q, 0, j))
        o_shape = jax.ShapeDtypeStruct((nq, ka, n), BF16)
    else:
        b_spec = pl.BlockSpec((tt, tn), lambda q, j, s: (s, j))
        o_spec = pl.BlockSpec((ka, tn), lambda q, j, s: (0, j))
        o_shape = jax.ShapeDtypeStruct((ka, n), BF16)
    return pl.pallas_call(
        body, name=name, grid=(nq, n // tn, steps),
        in_specs=[a_spec] + [b_spec] * nb, out_specs=[o_spec] * nb, out_shape=[o_shape] * nb,
        scratch_shapes=[pltpu.VMEM((ka, tn), F32)] * nb, compiler_params=_cp(),
    )(at, *bs)


def _sigmoid(x):
    return 1.0 / (1.0 + jnp.exp(-x))


def _ffn_up(h, wg, wu, layer, name):
    t = h.shape[0]
    tm = _row_tile(t, 1024)

    def body(h_ref, wg_ref, wu_ref, a_ref, dg_ref, du_ref):
        hv = h_ref[...]
        g = jnp.dot(hv, wg_ref[...], preferred_element_type=F32)
        u = jnp.dot(hv, wu_ref[...], preferred_element_type=F32)
        sg = _sigmoid(g)
        silu = g * sg
        a_ref[...] = (silu * u).astype(BF16)
        dg_ref[...] = (sg * (1.0 + g * (1.0 - sg)) * u).astype(BF16)
        du_ref[...] = silu.astype(BF16)

    wspec = pl.BlockSpec((None, None, D_MODEL, FF_SH), lambda q, i: (q, layer, 0, 0))
    ospec = pl.BlockSpec((None, tm, FF_SH), lambda q, i: (q, i, 0))
    oshape = jax.ShapeDtypeStruct((N_CHIPS, t, FF_SH), BF16)
    return pl.pallas_call(
        body, name=name, grid=(N_CHIPS, t // tm),
        in_specs=[pl.BlockSpec((tm, D_MODEL), lambda q, i: (i, 0)), wspec, wspec],
        out_specs=[ospec] * 3, out_shape=[oshape] * 3, compiler_params=_cp(),
    )(h, wg, wu)


def _ffn_down(a, wd, res, layer, name):
    t = a.shape[1]
    tm = _row_tile(t, 512)

    def body(a_ref, w_ref, r_ref, o_ref):
        acc = r_ref[...]
        for q in range(N_CHIPS):
            acc = acc + jnp.dot(a_ref[q], w_ref[q], preferred_element_type=F32)
        o_ref[...] = acc

    return pl.pallas_call(
        body, name=name, grid=(t // tm,),
        in_specs=[pl.BlockSpec((N_CHIPS, tm, FF_SH), lambda i: (0, i, 0)),
                  pl.BlockSpec((N_CHIPS, None, FF_SH, D_MODEL), lambda i: (0, layer, 0, 0)),
                  pl.BlockSpec((tm, D_MODEL), lambda i: (i, 0))],
        out_specs=pl.BlockSpec((tm, D_MODEL), lambda i: (i, 0)),
        out_shape=jax.ShapeDtypeStruct((t, D_MODEL), F32), compiler_params=_cp(),
    )(a, wd, res)


def _ffn_down_bwd(dx, wd, fg, fu, layer, name):
    t = dx.shape[0]
    tm = _row_tile(t, 512)

    def body(dx_ref, w_ref, fg_ref, fu_ref, dg_ref, du_ref):
        dxv = dx_ref[...]
        for q in range(N_CHIPS):
            da = lax.dot_general(dxv, w_ref[q], (((1,), (1,)), ((), ())), preferred_element_type=F32)
            dg_ref[q] = (da * fg_ref[q].astype(F32)).astype(BF16)
            du_ref[q] = (da * fu_ref[q].astype(F32)).astype(BF16)

    aspec = pl.BlockSpec((N_CHIPS, tm, FF_SH), lambda i: (0, i, 0))
    oshape = jax.ShapeDtypeStruct((N_CHIPS, t, FF_SH), BF16)
    return pl.pallas_call(
        body, name=name, grid=(t // tm,),
        in_specs=[pl.BlockSpec((tm, D_MODEL), lambda i: (i, 0)),
                  pl.BlockSpec((N_CHIPS, None, FF_SH, D_MODEL), lambda i: (0, layer, 0, 0)), aspec, aspec],
        out_specs=[aspec] * 2, out_shape=[oshape] * 2, compiler_params=_cp(),
    )(dx, wd, fg, fu)


def _ffn_up_bwd(dg, du, wg, wu, layer, name):
    t = dg.shape[1]
    tm = _row_tile(t, 512)
    nt = (((1,), (1,)), ((), ()))

    def body(dg_ref, du_ref, wg_ref, wu_ref, o_ref):
        acc = jnp.zeros((tm, D_MODEL), F32)
        for q in range(N_CHIPS):
            acc = acc + lax.dot_general(dg_ref[q], wg_ref[q], nt, preferred_element_type=F32)
            acc = acc + lax.dot_general(du_ref[q], wu_ref[q], nt, preferred_element_type=F32)
        o_ref[...] = acc

    aspec = pl.BlockSpec((N_CHIPS, tm, FF_SH), lambda i: (0, i, 0))
    wspec = pl.BlockSpec((N_CHIPS, None, D_MODEL, FF_SH), lambda i: (0, layer, 0, 0))
    return pl.pallas_call(
        body, name=name, grid=(t // tm,),
        in_specs=[aspec, aspec, wspec, wspec],
        out_specs=pl.BlockSpec((tm, D_MODEL), lambda i: (i, 0)),
        out_shape=jax.ShapeDtypeStruct((t, D_MODEL), F32), compiler_params=_cp(),
    )(dg, du, wg, wu)


def _attn_geometry(length, half_window):
    qb = min(LANES, length)
    kw = min(qb + 2 * half_window, length)
    return qb, kw, length // qb


def _dup_kv(src_ref, dst_ref, s, length):
    ch = min(length, 256)
    lo = lax.broadcasted_iota(jnp.int32, (ch, LANES), 1) < HEAD_DIM

    def chunk(c, carry):
        r0 = pl.multiple_of(c * ch, ch)
        for j in range(N_KV // 2):
            tile = src_ref[s, pl.ds(r0, ch), j * LANES:(j + 1) * LANES].astype(F32)
            rolled = pltpu.roll(tile, HEAD_DIM, 1)
            dst_ref[2 * j, pl.ds(r0, ch), :] = jnp.where(lo, tile, rolled).astype(BF16)
            dst_ref[2 * j + 1, pl.ds(r0, ch), :] = jnp.where(lo, rolled, tile).astype(BF16)
        return carry

    lax.fori_loop(0, length // ch, chunk, 0)


def _stack_heads(ref, s, q0, qb, g):
    lo = lax.broadcasted_iota(jnp.int32, (qb, LANES), 1) < HEAD_DIM
    parts = []
    for a in range(4):
        col = (2 * g + a // 2) * LANES
        tile = ref[s, pl.ds(q0, qb), col:col + LANES]
        keep = lo if a % 2 == 0 else jnp.logical_not(lo)
        parts.append(jnp.where(keep, tile, jnp.zeros_like(tile)))
    return jnp.concatenate(parts, axis=0)


def _unstack_pair_t(stacked_t, qb, pair):
    lo = lax.broadcasted_iota(jnp.int32, (LANES, qb), 0) < HEAD_DIM
    both = jnp.where(lo, stacked_t[:, (2 * pair) * qb:(2 * pair + 1) * qb],
                     stacked_t[:, (2 * pair + 1) * qb:(2 * pair + 2) * qb])
    return both.T


def _band_mask_t(q0, k0, qb, kw, half_window):
    key = lax.broadcasted_iota(jnp.int32, (kw, 4 * qb), 0)
    qry = lax.broadcasted_iota(jnp.int32, (kw, 4 * qb), 1) & (qb - 1)
    return jnp.abs((q0 + qry) - (k0 + key)) <= half_window


def _block_origin(i, qb, kw, half_window, length):
    if isinstance(i, int):
        return i * qb, min(max(i * qb - half_window, 0), length - kw)
    return (pl.multiple_of(i * qb, qb),
            pl.multiple_of(jnp.clip(i * qb - half_window, 0, length - kw), HEAD_DIM))


def _head_row(vals, qb):
    return jnp.concatenate([jnp.broadcast_to(v, (1, qb)).astype(F32) for v in vals], axis=1)


def _attn_fwd(qkv, sink, n_seq, length, half_window, seq_blk, out_dtype, with_lse, name):
    qb, kw, nblk = _attn_geometry(length, half_window)
    with_sink = sink is not None
    nt = (((1,), (1,)), ((), ()))
    tn = (((0,), (0,)), ((), ()))
    qkv3 = qkv.reshape(n_seq, length, QKV_W)

    def body(*refs):
        refs = list(refs)
        sink_ref = refs.pop(0) if with_sink else None
        q_ref, k_ref, v_ref, o_ref = refs[:4]
        lse_ref = refs[4] if with_lse else None
        kx_ref, vx_ref = refs[-2:]
        head_row = lax.broadcasted_iota(jnp.int32, (N_HEADS, qb), 0)
        for s in range(seq_blk):
            _dup_kv(k_ref, kx_ref, s, length)
            _dup_kv(v_ref, vx_ref, s, length)

            def block(i, carry):
                q0, k0 = _block_origin(i, qb, kw, half_window, length)
                valid = _band_mask_t(q0, k0, qb, kw, half_window)
                lse_tile = jnp.zeros((N_HEADS, qb), F32)
                for g in range(N_KV):
                    qs = _stack_heads(q_ref, s, q0, qb, g)
                    kx = kx_ref[g, pl.ds(k0, kw), :]
                    vx = vx_ref[g, pl.ds(k0, kw), :]
                    st = lax.dot_general(kx, qs, nt, preferred_element_type=F32)
                    st = jnp.where(valid, st, NEG_INF)
                    m = jnp.max(st, axis=0, keepdims=True)
                    if with_sink:
                        sk = _head_row([sink_ref[4 * g + a] for a in range(4)], qb)
                        m = jnp.maximum(m, sk)
                    e = jnp.exp(st - m)
                    den = jnp.sum(e, axis=0, keepdims=True)
                    if with_sink:
                        den = den + jnp.exp(sk - m)
                    ot = lax.dot_general(vx, e.astype(BF16), tn, preferred_element_type=F32) / den
                    for pair in range(2):
                        col = (2 * g + pair) * LANES
                        o_ref[s, pl.ds(q0, qb), col:col + LANES] = _unstack_pair_t(ot, qb, pair).astype(out_dtype)
                    if with_lse:
                        lse = m + jnp.log(den)
                        for a in range(4):
                            lse_tile = jnp.where(head_row == 4 * g + a, lse[:, a * qb:(a + 1) * qb], lse_tile)
                if with_lse:
                    lse_ref[s, :, pl.ds(q0, qb)] = lse_tile
                return carry

            if nblk == 1:
                block(0, 0)
            else:
                lax.fori_loop(0, nblk, block, 0)

    in_specs = [pl.BlockSpec((seq_blk, length, N_HEADS * HEAD_DIM), lambda n: (n, 0, 0)),
                pl.BlockSpec((seq_blk, length, N_KV * HEAD_DIM), lambda n: (n, 0, 4)),
                pl.BlockSpec((seq_blk, length, N_KV * HEAD_DIM), lambda n: (n, 0, 5))]
    args = [qkv3, qkv3, qkv3]
    if with_sink:
        in_specs.insert(0, pl.BlockSpec(memory_space=pltpu.SMEM))
        args.insert(0, sink)
    out_specs = [pl.BlockSpec((seq_blk, length, D_MODEL), lambda n: (n, 0, 0))]
    out_shape = [jax.ShapeDtypeStruct((n_seq, length, D_MODEL), out_dtype)]
    if with_lse:
        out_specs.append(pl.BlockSpec((seq_blk, N_HEADS, length), lambda n: (n, 0, 0)))
        out_shape.append(jax.ShapeDtypeStruct((n_seq, N_HEADS, length), F32))
    outs = pl.pallas_call(
        body, name=name, grid=(n_seq // seq_blk,), in_specs=in_specs, out_specs=out_specs, out_shape=out_shape,
        scratch_shapes=[pltpu.VMEM((N_KV, length, LANES), BF16), pltpu.VMEM((N_KV, length, LANES), BF16)],
        compiler_params=_cp(),
    )(*args)
    o = outs[0].reshape(n_seq * length, D_MODEL)
    return (o, outs[1]) if with_lse else (o,)


def _attn_bwd(qkv, do, adj, sink, cos, sin, n_seq, length, half_window, seq_blk, dil, name):
    qb, kw, nblk = _attn_geometry(length, half_window)
    scale = 1.0 / math.sqrt(HEAD_DIM)
    with_sink = sink is not None
    nt = (((1,), (1,)), ((), ()))
    tn = (((0,), (0,)), ((), ()))
    qkv3 = qkv.reshape(n_seq, length, QKV_W)
    do3 = do.reshape(n_seq, length, D_MODEL)
    tabs = [t.reshape(dil, length, LANES) for t in (cos, sin)]
    tab_blocks = dil // seq_blk if dil >= seq_blk else 1

    def body(*refs):
        refs = list(refs)
        sink_ref = refs.pop(0) if with_sink else None
        q_ref, k_ref, v_ref, do_ref, aux_ref, cos_ref, sin_ref, dqkv_ref = refs[:8]
        ds_ref = refs[8] if with_sink else None
        kx_ref, vx_ref, dkx_ref, dvx_ref = refs[-4:]
        lane = lax.broadcasted_iota(jnp.int32, (1, LANES), 1)
        if with_sink:
            @pl.when(pl.program_id(0) == 0)
            def _():
                ds_ref[...] = jnp.zeros_like(ds_ref)

        for s in range(seq_blk):
            ts = s % dil
            _dup_kv(k_ref, kx_ref, s, length)
            _dup_kv(v_ref, vx_ref, s, length)
            dkx_ref[...] = jnp.zeros_like(dkx_ref)
            dvx_ref[...] = jnp.zeros_like(dvx_ref)

            def block(i, dsink):
                q0, k0 = _block_origin(i, qb, kw, half_window, length)
                valid = _band_mask_t(q0, k0, qb, kw, half_window)
                cs = cos_ref[ts, pl.ds(q0, qb), :] * scale
                sn = sin_ref[ts, pl.ds(q0, qb), :] * scale
                adj_tile = aux_ref[s, :, pl.ds(q0, qb)]
                for g in range(N_KV):
                    qs = _stack_heads(q_ref, s, q0, qb, g)
                    dos = _stack_heads(do_ref, s, q0, qb, g)
                    kx = kx_ref[g, pl.ds(k0, kw), :]
                    vx = vx_ref[g, pl.ds(k0, kw), :]
                    st = lax.dot_general(kx, qs, nt, preferred_element_type=F32)
                    st = jnp.where(valid, st, NEG_INF)
                    m = jnp.max(st, axis=0, keepdims=True)
                    if with_sink:
                        sk = _head_row([sink_ref[4 * g + a] for a in range(4)], qb)
                        m = jnp.maximum(m, sk)
                    e = jnp.exp(st - m)
                    den = jnp.sum(e, axis=0, keepdims=True)
                    if with_sink:
                        esk = jnp.exp(sk - m)
                        den = den + esk
                    rden = 1.0 / den
                    pt = e * rden
                    shift = _head_row([adj_tile[4 * g + a:4 * g + a + 1, :] for a in range(4)], qb)
                    dpt = lax.dot_general(vx, dos, nt, preferred_element_type=F32)
                    dst = pt * (dpt + shift)
                    if with_sink:
                        dsk = esk * rden * shift
                        for a in range(4):
                            tot = jnp.sum(dsk[:, a * qb:(a + 1) * qb], axis=1, keepdims=True)
                            dsink = dsink + jnp.where(lane == 4 * g + a, tot, 0.0)
                    dsb = dst.astype(BF16)
                    pb = pt.astype(BF16)
                    dqt = lax.dot_general(kx, dsb, tn, preferred_element_type=F32)
                    for pair in range(2):
                        col = (2 * g + pair) * LANES
                        tile = _rope_t(_unstack_pair_t(dqt, qb, pair), cs, sn)
                        dqkv_ref[s, pl.ds(q0, qb), col:col + LANES] = tile.astype(BF16)
                    dkx_ref[g, pl.ds(k0, kw), :] += jnp.dot(dsb, qs, preferred_element_type=F32)
                    dvx_ref[g, pl.ds(k0, kw), :] += jnp.dot(pb, dos, preferred_element_type=F32)
                return dsink

            if nblk == 1:
                dsink = block(0, jnp.zeros((1, LANES), F32))
            else:
                dsink = lax.fori_loop(0, nblk, block, jnp.zeros((1, LANES), F32))
            if with_sink:
                ds_ref[0:1, :] += dsink

            ch = min(length, 256)
            lo_c = lax.broadcasted_iota(jnp.int32, (ch, LANES), 1) < HEAD_DIM

            def fin(c, carry):
                r0 = pl.multiple_of(c * ch, ch)
                cs = cos_ref[ts, pl.ds(r0, ch), :]
                sn = sin_ref[ts, pl.ds(r0, ch), :]
                for j in range(N_KV // 2):
                    both = []
                    for acc_ref in (dkx_ref, dvx_ref):
                        t0 = acc_ref[2 * j, pl.ds(r0, ch), :]
                        t1 = acc_ref[2 * j + 1, pl.ds(r0, ch), :]
                        t0 = t0 + pltpu.roll(t0, HEAD_DIM, 1)
                        t1 = t1 + pltpu.roll(t1, HEAD_DIM, 1)
                        both.append(jnp.where(lo_c, t0, t1))
                    kcol = N_HEADS * HEAD_DIM + j * LANES
                    vcol = (N_HEADS + N_KV) * HEAD_DIM + j * LANES
                    dqkv_ref[s, pl.ds(r0, ch), kcol:kcol + LANES] = _rope_t(both[0], cs, sn).astype(BF16)
                    dqkv_ref[s, pl.ds(r0, ch), vcol:vcol + LANES] = both[1].astype(BF16)
                return carry

            lax.fori_loop(0, length // ch, fin, 0)

    seq_map = lambda n: (n, 0, 0)
    tab_map = (lambda n: (n % tab_blocks, 0, 0)) if dil >= seq_blk else (lambda n: (0, 0, 0))
    tab_rows = min(seq_blk, dil)
    in_specs = [pl.BlockSpec((seq_blk, length, N_HEADS * HEAD_DIM), seq_map),
                pl.BlockSpec((seq_blk, length, N_KV * HEAD_DIM), lambda n: (n, 0, 4)),
                pl.BlockSpec((seq_blk, length, N_KV * HEAD_DIM), lambda n: (n, 0, 5)),
                pl.BlockSpec((seq_blk, length, D_MODEL), seq_map),
                pl.BlockSpec((seq_blk, N_HEADS, length), seq_map),
                pl.BlockSpec((tab_rows, length, LANES), tab_map),
                pl.BlockSpec((tab_rows, length, LANES), tab_map)]
    args = [qkv3, qkv3, qkv3, do3, adj] + tabs
    if with_sink:
        in_specs.insert(0, pl.BlockSpec(memory_space=pltpu.SMEM))
        args.insert(0, sink)
    out_specs = [pl.BlockSpec((seq_blk, length, QKV_W), seq_map)]
    out_shape = [jax.ShapeDtypeStruct((n_seq, length, QKV_W), BF16)]
    if with_sink:
        out_specs.append(pl.BlockSpec((8, LANES), lambda n: (0, 0)))
        out_shape.append(jax.ShapeDtypeStruct((8, LANES), F32))
    outs = pl.pallas_call(
        body, name=name, grid=(n_seq // seq_blk,), in_specs=in_specs, out_specs=out_specs, out_shape=out_shape,
        scratch_shapes=[pltpu.VMEM((N_KV, length, LANES), BF16), pltpu.VMEM((N_KV, length, LANES), BF16),
                        pltpu.VMEM((N_KV, length, LANES), F32), pltpu.VMEM((N_KV, length, LANES), F32)],
        compiler_params=_cp(),
    )(*args)
    dqkv = outs[0].reshape(n_seq * length, QKV_W)
    return (dqkv, outs[1]) if with_sink else (dqkv, None)


def _head_expander():
    h = jnp.arange(LANES)[:, None]
    l = jnp.arange(D_MODEL)[None, :]
    return (l // HEAD_DIM == h).astype(BF16)


def _dot_split(a, e):
    hi = a.astype(BF16)
    lo = (a - hi.astype(F32)).astype(BF16)
    return jnp.dot(hi, e, preferred_element_type=F32) + jnp.dot(lo, e, preferred_element_type=F32)


def _mix_weights(lses):
    m = jnp.maximum(jnp.maximum(lses[0], lses[1]), lses[2])
    es = [jnp.exp(v - m) for v in lses]
    tot = es[0] + es[1] + es[2]
    return [e / tot for e in es]


def _mix_fwd(os_, lses, name):
    t = os_[0].shape[0]
    tm = _row_tile(t, 512)

    def body(o0, o1, o2, l0, l1, l2, e_ref, out_ref):
        wts = _mix_weights([l0[...], l1[...], l2[...]])
        acc = jnp.zeros((tm, D_MODEL), F32)
        for w, o_ref in zip(wts, (o0, o1, o2)):
            acc = acc + _dot_split(w, e_ref[...]) * o_ref[...]
        out_ref[...] = acc.astype(BF16)

    row = pl.BlockSpec((tm, D_MODEL), lambda i: (i, 0))
    lrow = pl.BlockSpec((tm, LANES), lambda i: (i, 0))
    return pl.pallas_call(
        body, name=name, grid=(t // tm,),
        in_specs=[row] * 3 + [lrow] * 3 + [pl.BlockSpec((LANES, D_MODEL), lambda i: (0, 0))],
        out_specs=row, out_shape=jax.ShapeDtypeStruct((t, D_MODEL), BF16), compiler_params=_cp(),
    )(*os_, *lses, _head_expander())


def _mix_bwd(dmix, os_, lses, name):
    t = dmix.shape[0]
    tm = _row_tile(t, 512)

    def body(d_ref, o0, o1, o2, l0, l1, l2, e_ref, et_ref, do0, do1, do2, a0, a1, a2):
        wts = _mix_weights([l0[...], l1[...], l2[...]])
        dv = d_ref[...].astype(F32)
        cs = [_dot_split(dv * o_ref[...], et_ref[...]) for o_ref in (o0, o1, o2)]
        mean_c = wts[0] * cs[0] + wts[1] * cs[1] + wts[2] * cs[2]
        for w, c, do_ref, a_ref in zip(wts, cs, (do0, do1, do2), (a0, a1, a2)):
            do_ref[...] = (_dot_split(w, e_ref[...]) * dv).astype(BF16)
            a_ref[...] = w * (c - mean_c) - w * c

    row = pl.BlockSpec((tm, D_MODEL), lambda i: (i, 0))
    lrow = pl.BlockSpec((tm, LANES), lambda i: (i, 0))
    e = _head_expander()
    return pl.pallas_call(
        body, name=name, grid=(t // tm,),
        in_specs=[row] * 4 + [lrow] * 3 + [pl.BlockSpec((LANES, D_MODEL), lambda i: (0, 0)),
                                            pl.BlockSpec((D_MODEL, LANES), lambda i: (0, 0))],
        out_specs=[row] * 3 + [lrow] * 3,
        out_shape=[jax.ShapeDtypeStruct((t, D_MODEL), BF16)] * 3 + [jax.ShapeDtypeStruct((t, LANES), F32)] * 3,
        compiler_params=_cp(),
    )(dmix, *os_, *lses, e, e.T)


def _stats_to_tokens(stat, batch, dil):
    n_seq, _, length = stat.shape
    t = stat.transpose(0, 2, 1).reshape(n_seq * length, N_HEADS)
    return _from_residue(jnp.pad(t, ((0, 0), (0, LANES - N_HEADS))), batch, dil)


def _stats_from_tokens(stat, batch, dil, n_seq, length):
    t = _to_residue(stat[:, :N_HEADS], batch, dil)
    return t.reshape(n_seq, length, N_HEADS).transpose(0, 2, 1)


def _group_geometry(batch, seq, dil, window):
    length = seq // dil
    n_seq = batch * dil
    seq_blk = max(1, min(dil, 1024 // length))
    return n_seq, length, (window // 2) // dil, seq_blk


def _local_step(x, target, a_in, a_sink, a_out, b_in, b_out, norm_mix, norm_ffn, wg, wu, wd, final_norm):
    batch, seq, _ = x.shape
    t = batch * seq
    x0 = x.reshape(t, D_MODEL)
    tgt = target.reshape(t, D_MODEL)
    tabs = {d: _rope_tables(seq, d) for _, d in DILATED}
    nm = [norm_mix[i:i + 1] for i in range(2)]
    nf = [norm_ffn[i:i + 1] for i in range(2)]

    h0, h0t = _rms_fwd(x0, nm[0], "rms_mix0", True)
    qkv0 = _qkv_proj(h0, a_in, *tabs[1], 0, "qkv0")
    (o0,) = _attn_fwd(qkv0, a_sink, batch, seq, HALF_WINDOW_A, 1, BF16, False, "attn0")
    x1 = _mm_res(o0, a_out, x0, "out0")
    hf0, hf0t = _rms_fwd(x1, nf[0], "rms_ffn0", True)
    act0, g0, u0 = _ffn_up(hf0, wg, wu, 0, "ffn_up0")
    x2 = _ffn_down(act0, wd, x1, 0, "ffn_down0")

    h1 = _rms_fwd(x2, nm[1], "rms_mix1")
    geo = [_group_geometry(batch, seq, d, w) for w, d in DILATED]
    h1g, qkv1, o1, lse1 = [], [], [], []
    for gi, (_, d) in enumerate(DILATED):
        n_seq, length, hw, sb = geo[gi]
        hp = _to_residue(h1, batch, d)
        pj = _qkv_proj(hp, b_in, *tabs[d], gi, f"qkv1_{gi}")
        o, lse = _attn_fwd(pj, None, n_seq, length, hw, sb, F32, True, f"attn1_{gi}")
        h1g.append(hp)
        qkv1.append(pj)
        o1.append(_from_residue(o, batch, d))
        lse1.append(_stats_to_tokens(lse, batch, d))
    omix = _mix_fwd(o1, lse1, "mix")
    x3 = _mm_res(omix, b_out, x2, "out1")
    hf1, hf1t = _rms_fwd(x3, nf[1], "rms_ffn1", True)
    act1, g1, u1 = _ffn_up(hf1, wg, wu, 1, "ffn_up1")
    x4 = _ffn_down(act1, wd, x3, 1, "ffn_down1")

    dx4, dx4b, dx4t, loss_cols, d_final = _final_loss(x4, final_norm.reshape(1, D_MODEL), tgt, "final_loss")

    def ffn_bwd(dxo, dxob, dxot, x_mid, hft, g, u, act, layer):
        dg, du = _ffn_down_bwd(dxob, wd, g, u, layer, f"ffn_down_bwd{layer}")
        (d_wdt,) = _mm_grad(dxot, [act], f"grad_wd{layer}")
        dh = _ffn_up_bwd(dg, du, wg, wu, layer, f"ffn_up_bwd{layer}")
        d_wg, d_wu = _mm_grad(hft, [dg, du], f"grad_wgu{layer}")
        dxm, dxmb, d_nf = _rms_bwd(x_mid, nf[layer], [dh], dxo, f"rms_ffn_bwd{layer}")
        return dxm, dxmb, d_nf, d_wg, d_wu, d_wdt

    dx3, dx3b, d_nf1, d_wg1, d_wu1, d_wd1 = ffn_bwd(dx4, dx4b, dx4t, x3, hf1t, g1, u1, act1, 1)

    dmix = _mm_nt(dx3b, b_out, 0, BF16, "out1_bwd")
    (d_b_out,) = _mm_tn(omix, [dx3b], "grad_b_out")
    mb = _mix_bwd(dmix, o1, lse1, "mix_bwd")
    dh1, d_b_in = [], []
    for gi, (_, d) in enumerate(DILATED):
        n_seq, length, hw, sb = geo[gi]
        dog = _to_residue(mb[gi], batch, d)
        adj = _stats_from_tokens(mb[3 + gi], batch, d, n_seq, length)
        dpj, _ = _attn_bwd(qkv1[gi], dog, adj, None, *tabs[d], n_seq, length, hw, sb, d, f"attn1_bwd{gi}")
        (dw,) = _mm_tn(h1g[gi], [dpj], f"grad_b_in{gi}")
        d_b_in.append(dw)
        dh1.append(_from_residue(_mm_nt(dpj, b_in, gi, F32, f"qkv1_bwd{gi}"), batch, d))
    dx2, dx2b, dx2t, d_nm1 = _rms_bwd(x2, nm[1], dh1, dx3, "rms_mix_bwd1", True)

    dx1, dx1b, d_nf0, d_wg0, d_wu0, d_wd0 = ffn_bwd(dx2, dx2b, dx2t, x1, hf0t, g0, u0, act0, 0)

    do0, adj0 = _out_bwd(dx1b, a_out, o0, "out0_bwd")
    (d_a_out,) = _mm_tn(o0, [dx1b], "grad_a_out")
    adj0 = _stats_from_tokens(adj0, batch, 1, batch, seq)
    dqkv0, d_sink = _attn_bwd(qkv0, do0, adj0, a_sink, *tabs[1], batch, seq, HALF_WINDOW_A, 1, 1, "attn0_bwd")
    (d_a_in,) = _mm_grad(h0t, [dqkv0], "grad_a_in")
    dh0 = _mm_nt(dqkv0, a_in, 0, F32, "qkv0_bwd")
    gx, _, d_nm0 = _rms_bwd(x0, nm[0], [dh0], dx1, "rms_mix_bwd0")

    grads = dict(a_in=d_a_in, a_out=d_a_out, b_in=jnp.concatenate(d_b_in, axis=1), b_out=d_b_out,
                 wg=(d_wg0, d_wg1), wu=(d_wu0, d_wu1), wd=(d_wd0, d_wd1))
    vecs = dict(norm_mix=(d_nm0, d_nm1), norm_ffn=(d_nf0, d_nf1), final=d_final, loss_cols=loss_cols, sink=d_sink)
    return gx.reshape(x.shape), grads, vecs


ANY = pl.BlockSpec(memory_space=pl.ANY)
HBM = pltpu.MemorySpace.HBM


def _me():
    return lax.axis_index("x"), lax.axis_index("y"), lax.axis_index("c")


def _chip_peer(x, y, j):
    px = 1 - x if j & 2 else x
    py = 1 - y if j & 1 else y
    return px, py, 2 * px + py


def _remote(src, dst, sems, k, dev):
    return pltpu.make_async_remote_copy(src_ref=src, dst_ref=dst, send_sem=sems[0].at[k], recv_sem=sems[1].at[k],
                                        device_id=dev, device_id_type=MESH)


def _col_window(ref, q, width):
    return ref.at[:, pl.ds(pl.multiple_of(q * width, LANES), width)]


def _half0(ref, h):
    n = ref.shape[0] // 2
    return ref.at[pl.ds(h * n, n)]


def _half1(ref, h):
    n = ref.shape[1] // 2
    return ref.at[:, pl.ds(h * n, n)]


def _half_rows(ref, h):
    n = ref.shape[-2] // 2
    if len(ref.shape) == 2:
        return ref.at[pl.ds(h * n, n)]
    return ref.at[:, pl.ds(h * n, n)]


def _place_shard(w, q_arr, col, name):
    lead, rows, cols = w.shape

    def body(q_ref, w_ref, o_ref):
        o_ref[...] = w_ref[...].astype(BF16)

    if col:
        assert lead == 1
        out_spec = pl.BlockSpec((rows, cols), lambda l, q: (0, q[0]))
        out_shape = jax.ShapeDtypeStruct((rows, N_CHIPS * cols), BF16)
    else:
        out_spec = pl.BlockSpec((None, None, rows, cols), lambda l, q: (q[0], l, 0, 0))
        out_shape = jax.ShapeDtypeStruct((N_CHIPS, lead, rows, cols), BF16)
    return pl.pallas_call(
        body, name=name,
        grid_spec=pltpu.PrefetchScalarGridSpec(
            num_scalar_prefetch=1, grid=(lead,),
            in_specs=[pl.BlockSpec((None, rows, cols), lambda l, q: (l, 0, 0))], out_specs=out_spec),
        out_shape=out_shape, compiler_params=_cp(),
    )(q_arr, w)


def _handshake(peers):
    barrier = pltpu.get_barrier_semaphore()
    for p in peers:
        pl.semaphore_signal(barrier, inc=1, device_id=p, device_id_type=MESH)
    pl.semaphore_wait(barrier, len(peers))


def _on_sequencer(name, collective_id, n_sem, n_local, body):
    @pl.kernel(mesh=plsc.ScalarSubcoreMesh(axis_name="seq", num_cores=1), name=name,
               scratch_types=(pltpu.SemaphoreType.DMA((n_sem,)), pltpu.SemaphoreType.DMA((n_sem,)),
                              pltpu.SemaphoreType.DMA((max(n_local, 1),))),
               compiler_params=pltpu.CompilerParams(collective_id=collective_id))
    def launch(send_sems, recv_sems, local_sems):
        body((send_sems, recv_sems), local_sems)

    launch()


def _gather_plan(outs, col_fam, sems, handshake):
    n_w = len(outs)
    x, y, c = _me()
    myq = 2 * x + y
    sib = (x, y, 1 - c)
    if handshake:
        _handshake([sib] + [_chip_peer(x, y, j)[:2] + (c,) for j in (1, 2, 3)])

    def slot(w, q):
        if col_fam[w]:
            return _col_window(outs[w], q, outs[w].shape[1] // N_CHIPS)
        return outs[w].at[q]

    first = []
    for w in range(n_w):
        for j in (1, 2, 3):
            px, py, _ = _chip_peer(x, y, j)
            mine = _half_rows(slot(w, myq), c)
            cp = _remote(mine, mine, sems, w * 6 + j - 1, (px, py, c))
            cp.start()
            first.append(cp)
    passed = []
    for w in range(n_w):
        for j in (1, 2, 3):
            _, _, pq = _chip_peer(x, y, j)
            land = _half_rows(slot(w, pq), c)
            _remote(land, land, sems, w * 6 + j - 1, sib).wait_recv()
            cp = _remote(land, land, sems, w * 6 + 2 + j, sib)
            cp.start()
            passed.append(cp)
    for w in range(n_w):
        for j in (1, 2, 3):
            _, _, pq = _chip_peer(x, y, j)
            land = _half_rows(slot(w, pq), 1 - c)
            _remote(land, land, sems, w * 6 + 2 + j, sib).wait_recv()
    for cp in first + passed:
        cp.wait_send()


def _gather_weights(bufs, col_fam):
    n_w = len(bufs)

    def body(*refs):
        _gather_plan(refs[n_w:2 * n_w], col_fam, refs[2 * n_w:2 * n_w + 2], False)

    return pl.pallas_call(
        body, name="gather_weights", in_specs=[ANY] * n_w, out_specs=[ANY] * n_w,
        out_shape=[jax.ShapeDtypeStruct(b.shape, b.dtype) for b in bufs],
        input_output_aliases={w: w for w in range(n_w)},
        scratch_shapes=[pltpu.SemaphoreType.DMA((6 * n_w,)), pltpu.SemaphoreType.DMA((6 * n_w,))],
    )(*bufs)


def _gather_weights_async(bufs, col_fam, name, collective_id):
    refs = [jax.new_ref(b, memory_space=HBM) for b in bufs]
    _on_sequencer(name, collective_id, 6 * len(bufs), 0,
                  lambda sems, _: _gather_plan(refs, col_fam, sems, True))
    return [r[...] for r in refs]


def _grad_half(ref, col, h):
    return _half0(ref, h) if col else _half1(ref, h)


def _swap_halves_with_sibling(grads, col_fam):
    n_w = len(grads)

    def body(*refs):
        _swap_plan(refs[:n_w], refs[n_w:2 * n_w], col_fam, refs[2 * n_w:], False)

    return pl.pallas_call(
        body, name="grad_swap_sibling", in_specs=[ANY] * n_w, out_specs=[ANY] * n_w,
        out_shape=_swap_shapes(grads, col_fam),
        scratch_shapes=[pltpu.SemaphoreType.DMA((n_w,)), pltpu.SemaphoreType.DMA((n_w,))],
    )(*grads)


def _swap_shapes(grads, col_fam):
    out = []
    for w, g in enumerate(grads):
        shp = (g.shape[0] // 2, g.shape[1]) if col_fam[w] else (g.shape[0], g.shape[1] // 2, g.shape[2])
        out.append(jax.ShapeDtypeStruct(shp, g.dtype))
    return out


def _swap_plan(ins, outs, col_fam, sems, handshake):
    x, y, c = _me()
    sib = (x, y, 1 - c)
    if handshake:
        _handshake([sib])
    cps = [_remote(_grad_half(ins[w], col_fam[w], 1 - c), outs[w], sems, w, sib) for w in range(len(ins))]
    for cp in cps:
        cp.start()
    for cp in cps:
        cp.wait_recv()
    for cp in cps:
        cp.wait_send()


def _swap_halves_async(grads, col_fam, name, collective_id):
    srcs = [jax.new_ref(g, memory_space=HBM) for g in grads]
    dsts = [jax.empty_ref(s, memory_space=HBM) for s in _swap_shapes(grads, col_fam)]
    _on_sequencer(name, collective_id, len(grads), 0, lambda sems, _: _swap_plan(srcs, dsts, col_fam, sems, True))
    return [r[...] for r in srcs], [r[...] for r in dsts]


def _half_add(mine, recv, c_arr, col, name):
    if col:
        rows, n = recv.shape
        tr = rows // 2
        grid = (2,)
        in_specs = [pl.BlockSpec((tr, n), lambda i, c: (2 * c[0] + i, 0)), pl.BlockSpec((tr, n), lambda i, c: (i, 0))]
        out_spec = pl.BlockSpec((tr, n), lambda i, c: (i, 0))
    else:
        _, rows, n = recv.shape
        grid = (N_CHIPS,)
        in_specs = [pl.BlockSpec((None, rows, n), lambda q, c: (q, c[0], 0)),
                    pl.BlockSpec((None, rows, n), lambda q, c: (q, 0, 0))]
        out_spec = pl.BlockSpec((None, rows, n), lambda q, c: (q, 0, 0))

    def body(c_ref, a_ref, b_ref, o_ref):
        o_ref[...] = (a_ref[...].astype(F32) + b_ref[...].astype(F32)).astype(BF16)

    return pl.pallas_call(
        body, name=name,
        grid_spec=pltpu.PrefetchScalarGridSpec(num_scalar_prefetch=1, grid=grid, in_specs=in_specs, out_specs=out_spec),
        out_shape=jax.ShapeDtypeStruct(recv.shape, BF16), compiler_params=_cp(),
    )(c_arr, mine, recv)


def _scatter_chip_sums(sums, col_fam):
    n_w = len(sums)

    def body(*refs):
        _scatter_plan(refs[:n_w], refs[n_w:2 * n_w], col_fam, refs[2 * n_w:2 * n_w + 2], refs[2 * n_w + 2], False)

    return pl.pallas_call(
        body, name="grad_scatter_chips", in_specs=[ANY] * n_w, out_specs=[ANY] * n_w,
        out_shape=_scatter_shapes(sums, col_fam),
        scratch_shapes=[pltpu.SemaphoreType.DMA((3 * n_w,)), pltpu.SemaphoreType.DMA((3 * n_w,)),
                        pltpu.SemaphoreType.DMA((n_w,))],
    )(*sums)


def _scatter_shapes(sums, col_fam):
    out = []
    for w, s in enumerate(sums):
        shp = (s.shape[0], s.shape[1] // N_CHIPS) if col_fam[w] else s.shape[1:]
        out.append(jax.ShapeDtypeStruct((N_CHIPS,) + shp, s.dtype))
    return out


def _scatter_plan(ins, outs, col_fam, sems, lsem, handshake):
    n_w = len(ins)
    x, y, c = _me()
    myq = 2 * x + y
    if handshake:
        _handshake([_chip_peer(x, y, j)[:2] + (c,) for j in (1, 2, 3)])

    def slab(w, q):
        if col_fam[w]:
            return _col_window(ins[w], q, ins[w].shape[1] // N_CHIPS)
        return ins[w].at[q]

    local = [pltpu.make_async_copy(slab(w, myq), outs[w].at[myq], lsem.at[w]) for w in range(n_w)]
    for cp in local:
        cp.start()
    cps = []
    for w in range(n_w):
        for j in (1, 2, 3):
            px, py, pq = _chip_peer(x, y, j)
            cp = _remote(slab(w, pq), outs[w].at[myq], sems, w * 3 + j - 1, (px, py, c))
            cp.start()
            cps.append(cp)
    for w in range(n_w):
        for j in (1, 2, 3):
            _, _, pq = _chip_peer(x, y, j)
            land = outs[w].at[pq]
            _remote(land, land, sems, w * 3 + j - 1, (x, y, c)).wait_recv()
    for cp in cps:
        cp.wait_send()
    for cp in local:
        cp.wait()


def _scatter_chip_sums_async(sums, col_fam, name, collective_id):
    srcs = [jax.new_ref(s, memory_space=HBM) for s in sums]
    dsts = [jax.empty_ref(s, memory_space=HBM) for s in _scatter_shapes(sums, col_fam)]
    _on_sequencer(name, collective_id, 3 * len(sums), len(sums),
                  lambda sems, lsem: _scatter_plan(srcs, dsts, col_fam, sems, lsem, True))
    return [r[...] for r in dsts]


def _sum_chips(parts, c_arr, prev, lead, shape, name):
    _, rows, n = parts.shape
    tr = rows // 2 if rows % 32 == 0 else rows
    nblk = rows // tr

    def body(c_ref, p_ref, *rest):
        o_ref = rest[-1]
        acc = p_ref[0].astype(F32)
        for q in range(1, N_CHIPS):
            acc = acc + p_ref[q].astype(F32)
        o_ref[...] = acc

    in_specs = [pl.BlockSpec((N_CHIPS, tr, n), lambda i, c: (0, i, 0))]
    args = [c_arr, parts]
    aliases = {}
    if prev is not None:
        in_specs.append(ANY)
        args.append(prev)
        aliases = {2: 0}
    return pl.pallas_call(
        body, name=name,
        grid_spec=pltpu.PrefetchScalarGridSpec(
            num_scalar_prefetch=1, grid=(nblk,), in_specs=in_specs,
            out_specs=pl.BlockSpec((None, tr, n), lambda i, c: (lead, c[0] * nblk + i, 0))),
        out_shape=jax.ShapeDtypeStruct(shape, F32), input_output_aliases=aliases, compiler_params=_cp(),
    )(*args)


def _join_plan(outs, place, sems, handshake):
    x, y, c = _me()
    sib = (x, y, 1 - c)
    if handshake:
        _handshake([sib])

    def half(k, h):
        o, lead = place[k]
        return _half_rows(outs[o].at[lead], h)

    cps = [_remote(half(k, c), half(k, c), sems, k, sib) for k in range(len(place))]
    for cp in cps:
        cp.start()
    for k in range(len(place)):
        land = half(k, 1 - c)
        _remote(land, land, sems, k, sib).wait_recv()
    for cp in cps:
        cp.wait_send()


def _join_halves(bufs, place, name):
    n_o = len(bufs)
    n_h = len(place)

    def body(*refs):
        _join_plan(refs[n_o:2 * n_o], place, refs[2 * n_o:2 * n_o + 2], False)

    return pl.pallas_call(
        body, name=name, in_specs=[ANY] * n_o, out_specs=[ANY] * n_o,
        out_shape=[jax.ShapeDtypeStruct(b.shape, b.dtype) for b in bufs],
        input_output_aliases={k: k for k in range(n_o)},
        scratch_shapes=[pltpu.SemaphoreType.DMA((n_h,)), pltpu.SemaphoreType.DMA((n_h,))],
    )(*bufs)


def _join_halves_async(bufs, place, name, collective_id):
    refs = [jax.new_ref(b, memory_space=HBM) for b in bufs]
    _on_sequencer(name, collective_id, len(place), 0, lambda sems, _: _join_plan(refs, place, sems, True))
    return [r[...] for r in refs]


def _allreduce_rows(rows):
    n_dev = 8
    n_r = len(rows)
    assert n_r <= 8

    def body(*refs):
        r_refs = refs[:n_r]
        o_ref, slots, send_sems, recv_sems = refs[n_r:]
        x, y, c = _me()
        me = 4 * x + 2 * y + c
        slots[me] = jnp.concatenate([r[...] for r in r_refs] + [jnp.zeros((8 - n_r, D_MODEL), F32)], axis=0)

        def peer(k):
            return (1 - x if k & 4 else x, 1 - y if k & 2 else y, 1 - c if k & 1 else c)

        cps = []
        for k in range(1, n_dev):
            cp = pltpu.make_async_remote_copy(src_ref=slots.at[me], dst_ref=slots.at[me], send_sem=send_sems.at[k - 1],
                                              recv_sem=recv_sems.at[k - 1], device_id=peer(k), device_id_type=MESH)
            cp.start()
            cps.append(cp)
        for k in range(1, n_dev):
            px, py, pc = peer(k)
            land = slots.at[4 * px + 2 * py + pc]
            pltpu.make_async_remote_copy(src_ref=land, dst_ref=land, send_sem=send_sems.at[k - 1],
                                         recv_sem=recv_sems.at[k - 1], device_id=peer(k),
                                         device_id_type=MESH).wait_recv()
        for cp in cps:
            cp.wait_send()
        acc = slots[0]
        for d in range(1, n_dev):
            acc = acc + slots[d]
        o_ref[...] = acc

    vm = pl.BlockSpec(memory_space=pltpu.VMEM)
    return pl.pallas_call(
        body, name="allreduce_rows", in_specs=[vm] * n_r, out_specs=vm,
        out_shape=jax.ShapeDtypeStruct((8, D_MODEL), F32),
        scratch_shapes=[pltpu.VMEM((n_dev, 8, D_MODEL), F32), pltpu.SemaphoreType.DMA((n_dev - 1,)),
                        pltpu.SemaphoreType.DMA((n_dev - 1,))],
    )(*rows)


def _adamw(w, g, m, v, name):
    shape = w.shape
    if len(shape) == 1:
        lead, rows, cols = 1, 1, shape[0]
    else:
        rows, cols = shape[-2:]
        lead = math.prod(shape[:-2])
    args = [a.reshape(lead, rows, cols) for a in (w, g, m, v)]
    tr = rows // 2 if rows % 16 == 0 else rows

    def body(w_ref, g_ref, m_ref, v_ref, d_ref, nm_ref, nv_ref):
        gv = g_ref[...]
        nm = ADAM_B1 * m_ref[...] + (1.0 - ADAM_B1) * gv
        nv = ADAM_B2 * v_ref[...] + (1.0 - ADAM_B2) * jnp.square(gv)
        m_hat = nm / (1.0 - ADAM_B1 ** ADAM_STEP)
        v_hat = nv / (1.0 - ADAM_B2 ** ADAM_STEP)
        d_ref[...] = -ADAM_LR * (m_hat / (jnp.sqrt(v_hat) + ADAM_EPS) + ADAM_WD * w_ref[...])
        nm_ref[...] = nm
        nv_ref[...] = nv

    spec = pl.BlockSpec((None, tr, cols), lambda l, i: (l, i, 0))
    outs = pl.pallas_call(
        body, name=name, grid=(lead, rows // tr), in_specs=[spec] * 4, out_specs=[spec] * 3,
        out_shape=[jax.ShapeDtypeStruct((lead, rows, cols), F32)] * 3, compiler_params=_cp(),
    )(*args)
    return [o.reshape(shape) for o in outs]


def kernel(x, a_w_in, a_sink, a_w_out, b_w_in, b_w_out, norm_mix, norm_ffn, w_gate, w_up, w_down, final_norm, loss_target, m_a_w_in, m_a_sink, m_a_w_out, m_b_w_in, m_b_w_out, m_norm_mix, m_norm_ffn, m_w_gate, m_w_up, m_w_down, m_final_norm, v_a_w_in, v_a_sink, v_a_w_out, v_b_w_in, v_b_w_out, v_norm_mix, v_norm_ffn, v_w_gate, v_w_up, v_w_down, v_final_norm):
    weights = dict(a_w_in=a_w_in, a_sink=a_sink, a_w_out=a_w_out, b_w_in=b_w_in, b_w_out=b_w_out, norm_mix=norm_mix,
                   norm_ffn=norm_ffn, w_gate=w_gate, w_up=w_up, w_down=w_down, final_norm=final_norm)
    mom = dict(a_w_in=m_a_w_in, a_sink=m_a_sink, a_w_out=m_a_w_out, b_w_in=m_b_w_in, b_w_out=m_b_w_out,
               norm_mix=m_norm_mix, norm_ffn=m_norm_ffn, w_gate=m_w_gate, w_up=m_w_up, w_down=m_w_down,
               final_norm=m_final_norm)
    var = dict(a_w_in=v_a_w_in, a_sink=v_a_sink, a_w_out=v_a_w_out, b_w_in=v_b_w_in, b_w_out=v_b_w_out,
               norm_mix=v_norm_mix, norm_ffn=v_norm_ffn, w_gate=v_w_gate, w_up=v_w_up, w_down=v_w_down,
               final_norm=v_final_norm)
    order = ["a_w_in", "a_sink", "a_w_out", "b_w_in", "b_w_out", "norm_mix", "norm_ffn", "w_gate", "w_up", "w_down",
             "final_norm"]

    c_arr = lax.axis_index("c").astype(jnp.int32).reshape(1)
    q_arr = (2 * lax.axis_index("x") + lax.axis_index("y")).astype(jnp.int32).reshape(1)
    shards = [a_w_in, a_w_out, b_w_in, b_w_out, w_gate, w_up, w_down]
    shard_names = ("a_in", "a_out", "b_in", "b_out", "wg", "wu", "wd")
    placed = [_place_shard(s, q_arr, col, f"place_{nm}")
              for s, col, nm in zip(shards, (True, False, True, False, False, False, False), shard_names)]
    (a_in,) = _gather_weights(placed[:1], (True,))
    a_out, b_in, b_out, wg, wu, wd = _gather_weights_async(placed[1:], (False, True, False, False, False, False),
                                                           "gather_weights_late", 1)
    a_out = a_out.reshape(D_MODEL, D_MODEL)
    b_out = b_out.reshape(D_MODEL, D_MODEL)

    gx, grads, vecs = _local_step(x, loss_target, a_in, a_sink[0], a_out, b_in, b_out, norm_mix, norm_ffn, wg, wu, wd,
                                  final_norm)

    rows_out = D_MODEL // N_CHIPS
    partials = [grads["a_in"], grads["b_in"],
                grads["a_out"].reshape(N_CHIPS, rows_out, D_MODEL), grads["b_out"].reshape(N_CHIPS, rows_out, D_MODEL),
                grads["wg"][0], grads["wg"][1], grads["wu"][0], grads["wu"][1], grads["wd"][0], grads["wd"][1]]
    col_fam = (True, True) + (False,) * 8
    names = ("a_in", "b_in", "a_out", "b_out", "wg0", "wg1", "wu0", "wu1", "wd0", "wd1")
    contrib = [None] * len(partials)

    def reduce_group(idx, tag, ids):
        parts = [partials[k] for k in idx]
        cols = tuple(col_fam[k] for k in idx)
        if ids is None:
            theirs = _swap_halves_with_sibling(parts, cols)
        else:
            parts, theirs = _swap_halves_async(parts, cols, f"grad_swap_{tag}", ids[0])
        sums = [_half_add(p, r, c_arr, cf, f"chip_sum_{names[k]}") for p, r, cf, k in zip(parts, theirs, cols, idx)]
        if ids is None:
            out = _scatter_chip_sums(sums, cols)
        else:
            out = _scatter_chip_sums_async(sums, cols, f"grad_scatter_{tag}", ids[1])
        for k, o in zip(idx, out):
            contrib[k] = o

    reduce_group([1, 3, 5, 7, 9], "layer1", (2, 3))
    reduce_group([2, 4, 6, 8], "ffn0", (4, 5))
    reduce_group([0], "a_in", None)
    shapes = [a_w_in.shape, b_w_in.shape, a_w_out.shape, b_w_out.shape, w_gate.shape, w_up.shape, w_gate.shape]
    place = [(0, 0), (1, 0), (2, 0), (3, 0), (4, 0), (4, 1), (5, 0), (5, 1), (6, 0), (6, 1)]
    bufs = [None] * len(shapes)
    for p, nm, (o, lead) in zip(contrib, names, place):
        bufs[o] = _sum_chips(p, c_arr, bufs[o], lead, shapes[o], f"sum_chips_{nm}")
    g_a_in, g_b_in, g_a_out, g_b_out, g_wg, g_wu, g_wdt = _join_halves(bufs, place, "grad_join_sibling")
    g_wd = g_wdt.transpose(0, 2, 1)

    sink_row = jnp.pad(vecs["sink"][0:1], ((0, 0), (0, D_MODEL - LANES)))
    tot = _allreduce_rows([vecs["norm_mix"][0], vecs["norm_mix"][1], vecs["norm_ffn"][0], vecs["norm_ffn"][1],
                           vecs["final"], vecs["loss_cols"], sink_row])
    loss = (0.5 / D_MODEL) * jnp.sum(tot[5])
    gw = dict(a_w_in=g_a_in, a_sink=tot[6:7, :N_HEADS], a_w_out=g_a_out, b_w_in=g_b_in, b_w_out=g_b_out,
              norm_mix=tot[0:2], norm_ffn=tot[2:4], w_gate=g_wg, w_up=g_wu, w_down=g_wd, final_norm=tot[4])

    delta, new_m, new_v = {}, {}, {}
    for n in order:
        delta[n], new_m[n], new_v[n] = _adamw(weights[n], gw[n], mom[n], var[n], f"adamw_{n}")
    return (loss, gx, *[gw[n] for n in order], *[delta[n] for n in order], *[new_m[n] for n in order],
            *[new_v[n] for n in order])
```

```python
import functools
import math

import jax
import jax.numpy as jnp
from jax import lax
from jax.experimental import pallas as pl
from jax.experimental.pallas import tpu as pltpu
from jax.experimental.pallas import tpu_sc as plsc

F32 = jnp.float32
BF16 = jnp.bfloat16

D_MODEL = 1024
HEAD_DIM = 64
N_HEADS = 16
N_KV = 4
QKV_W = 1536
D_FF = 2816
N_CHIPS = 4
FF_SH = D_FF // N_CHIPS
HALF_WINDOW_A = 128
DILATED = ((128, 1), (512, 4), (2048, 16))
ROPE_THETA = 10000.0
RMS_EPS = 1e-6
NEG_INF = -1e30
LANES = 128
ADAM_LR, ADAM_B1, ADAM_B2, ADAM_EPS, ADAM_WD, ADAM_STEP = 0.001, 0.9, 0.999, 1e-08, 0.01, 10
VMEM_LIMIT = 56 * 1024 * 1024
GRAD_TOKENS = 2048
MESH = pl.DeviceIdType.MESH


def _cp(**kw):
    return pltpu.CompilerParams(vmem_limit_bytes=VMEM_LIMIT, **kw)


def _row_tile(t, cap):
    tm = min(cap, t)
    assert t % tm == 0
    return tm


def _rope_tables(seq, dil):
    inv = 1.0 / (ROPE_THETA ** (jnp.arange(0, HEAD_DIM, 2, dtype=F32) / HEAD_DIM))
    ang = jnp.arange(seq, dtype=F32)[:, None] * inv[None, :]
    cos, sin = jnp.cos(ang), jnp.sin(ang)
    cos = jnp.tile(cos, (1, 4))
    sin = jnp.concatenate([-sin, sin, -sin, sin], axis=1)

    def perm(t):
        return t.reshape(seq // dil, dil, LANES).transpose(1, 0, 2).reshape(seq, LANES)

    return perm(cos), perm(sin)


def _swap_halves(t):
    lane = lax.broadcasted_iota(jnp.int32, t.shape, 1)
    return jnp.where((lane % HEAD_DIM) < HEAD_DIM // 2, pltpu.roll(t, LANES - 32, 1), pltpu.roll(t, 32, 1))


def _rope(t, cos, sin):
    return t * cos + _swap_halves(t) * sin


def _rope_t(t, cos, sin):
    return t * cos - _swap_halves(t) * sin


def _to_residue(t, batch, dil):
    if dil == 1:
        return t
    s = t.shape[0] // batch
    return t.reshape(batch, s // dil, dil, t.shape[1]).transpose(0, 2, 1, 3).reshape(t.shape)


def _from_residue(t, batch, dil):
    if dil == 1:
        return t
    s = t.shape[0] // batch
    return t.reshape(batch, dil, s // dil, t.shape[1]).transpose(0, 2, 1, 3).reshape(t.shape)


def _rms_fwd(x, w, name, with_t=False):
    t = x.shape[0]
    tm = _row_tile(t, 512)

    def body(x_ref, w_ref, o_ref, *ot_ref):
        xv = x_ref[...]
        r = lax.rsqrt(jnp.mean(xv * xv, axis=-1, keepdims=True) + RMS_EPS)
        y = (xv * r) * w_ref[...]
        o_ref[...] = y.astype(BF16)
        if with_t:
            ot_ref[0][...] = y.T.astype(BF16)

    out_specs = [pl.BlockSpec((tm, D_MODEL), lambda i: (i, 0))]
    out_shape = [jax.ShapeDtypeStruct((t, D_MODEL), BF16)]
    if with_t:
        out_specs.append(pl.BlockSpec((D_MODEL, tm), lambda i: (0, i)))
        out_shape.append(jax.ShapeDtypeStruct((D_MODEL, t), BF16))
    outs = pl.pallas_call(
        body, name=name, grid=(t // tm,),
        in_specs=[pl.BlockSpec((tm, D_MODEL), lambda i: (i, 0)), pl.BlockSpec((1, D_MODEL), lambda i: (0, 0))],
        out_specs=out_specs, out_shape=out_shape, compiler_params=_cp(),
    )(x, w)
    return outs if with_t else outs[0]


def _rms_bwd(x, w, dhs, dres, name, with_t=False):
    t = x.shape[0]
    tm = _row_tile(t, 512)
    n = len(dhs)

    def body(*refs):
        x_ref, w_ref = refs[0], refs[1]
        dh_refs = refs[2:2 + n]
        dres_ref = refs[2 + n]
        dx_ref, dxb_ref = refs[3 + n:5 + n]
        dw_ref = refs[-1]
        xv = x_ref[...]
        r = lax.rsqrt(jnp.mean(xv * xv, axis=-1, keepdims=True) + RMS_EPS)
        xh = xv * r
        dy = dh_refs[0][...]
        for k in range(1, n):
            dy = dy + dh_refs[k][...]
        dxh = dy * w_ref[...]
        dx = dres_ref[...] + r * (dxh - xh * jnp.mean(dxh * xh, axis=-1, keepdims=True))
        dx_ref[...] = dx
        dxb_ref[...] = dx.astype(BF16)
        if with_t:
            refs[5 + n][...] = dx.T.astype(BF16)

        @pl.when(pl.program_id(0) == 0)
        def _():
            dw_ref[...] = jnp.zeros_like(dw_ref)

        dw_ref[...] += jnp.sum(dy * xh, axis=0, keepdims=True)

    row = pl.BlockSpec((tm, D_MODEL), lambda i: (i, 0))
    vec = pl.BlockSpec((1, D_MODEL), lambda i: (0, 0))
    out_specs = [row, row]
    out_shape = [jax.ShapeDtypeStruct((t, D_MODEL), F32), jax.ShapeDtypeStruct((t, D_MODEL), BF16)]
    if with_t:
        out_specs.append(pl.BlockSpec((D_MODEL, tm), lambda i: (0, i)))
        out_shape.append(jax.ShapeDtypeStruct((D_MODEL, t), BF16))
    return pl.pallas_call(
        body, name=name, grid=(t // tm,),
        in_specs=[row, vec] + [row] * n + [row],
        out_specs=out_specs + [vec], out_shape=out_shape + [jax.ShapeDtypeStruct((1, D_MODEL), F32)],
        compiler_params=_cp(),
    )(x, w, *dhs, dres)


def _final_loss(x, w, target, name):
    t = x.shape[0]
    tm = _row_tile(t, 512)

    def body(x_ref, w_ref, t_ref, dx_ref, dxb_ref, dxt_ref, l_ref, dw_ref):
        xv = x_ref[...]
        r = lax.rsqrt(jnp.mean(xv * xv, axis=-1, keepdims=True) + RMS_EPS)
        xh = xv * r
        err = xh * w_ref[...] - t_ref[...]
        dy = err * (1.0 / D_MODEL)
        dxh = dy * w_ref[...]
        dx = r * (dxh - xh * jnp.mean(dxh * xh, axis=-1, keepdims=True))
        dx_ref[...] = dx
        dxb_ref[...] = dx.astype(BF16)
        dxt_ref[...] = dx.T.astype(BF16)

        @pl.when(pl.program_id(0) == 0)
        def _():
            l_ref[...] = jnp.zeros_like(l_ref)
            dw_ref[...] = jnp.zeros_like(dw_ref)

        l_ref[...] += jnp.sum(err * err, axis=0, keepdims=True)
        dw_ref[...] += jnp.sum(dy * xh, axis=0, keepdims=True)

    row = pl.BlockSpec((tm, D_MODEL), lambda i: (i, 0))
    vec = pl.BlockSpec((1, D_MODEL), lambda i: (0, 0))
    return pl.pallas_call(
        body, name=name, grid=(t // tm,),
        in_specs=[row, vec, row], out_specs=[row, row, pl.BlockSpec((D_MODEL, tm), lambda i: (0, i)), vec, vec],
        out_shape=[jax.ShapeDtypeStruct((t, D_MODEL), F32), jax.ShapeDtypeStruct((t, D_MODEL), BF16),
                   jax.ShapeDtypeStruct((D_MODEL, t), BF16),
                   jax.ShapeDtypeStruct((1, D_MODEL), F32), jax.ShapeDtypeStruct((1, D_MODEL), F32)],
        compiler_params=_cp(),
    )(x, w, target)


def _qkv_proj(h, w, cos, sin, group, name):
    t = h.shape[0]
    seq = cos.shape[0]
    tm = _row_tile(seq, 1024)
    n_q = N_HEADS * HEAD_DIM // LANES
    n_rope = (N_HEADS + N_KV) * HEAD_DIM // LANES
    scale = 1.0 / math.sqrt(HEAD_DIM)

    def body(h_ref, w_ref, cos_ref, sin_ref, o_ref):
        acc = jnp.dot(h_ref[...], w_ref[...], preferred_element_type=F32)
        cs, sn = cos_ref[...], sin_ref[...]
        csq, snq = cs * scale, sn * scale
        for c in range(QKV_W // LANES):
            blk = acc[:, c * LANES:(c + 1) * LANES]
            if c < n_q:
                blk = _rope(blk, csq, snq)
            elif c < n_rope:
                blk = _rope(blk, cs, sn)
            o_ref[:, c * LANES:(c + 1) * LANES] = blk.astype(BF16)

    tab = pl.BlockSpec((tm, LANES), lambda i: (i % (seq // tm), 0))
    return pl.pallas_call(
        body, name=name, grid=(t // tm,),
        in_specs=[pl.BlockSpec((tm, D_MODEL), lambda i: (i, 0)),
                  pl.BlockSpec((D_MODEL, QKV_W), lambda i: (0, group)), tab, tab],
        out_specs=pl.BlockSpec((tm, QKV_W), lambda i: (i, 0)),
        out_shape=jax.ShapeDtypeStruct((t, QKV_W), BF16), compiler_params=_cp(),
    )(h, w, cos, sin)


def _mm_res(a, w, res, name):
    t, k = a.shape
    tm = _row_tile(t, 1024)

    def body(a_ref, w_ref, r_ref, o_ref):
        o_ref[...] = r_ref[...] + jnp.dot(a_ref[...], w_ref[...], preferred_element_type=F32)

    return pl.pallas_call(
        body, name=name, grid=(t // tm,),
        in_specs=[pl.BlockSpec((tm, k), lambda i: (i, 0)), pl.BlockSpec((k, D_MODEL), lambda i: (0, 0)),
                  pl.BlockSpec((tm, D_MODEL), lambda i: (i, 0))],
        out_specs=pl.BlockSpec((tm, D_MODEL), lambda i: (i, 0)),
        out_shape=jax.ShapeDtypeStruct((t, D_MODEL), F32), compiler_params=_cp(),
    )(a, w, res)


def _mm_nt(dy, w, group, out_dtype, name):
    t, n = dy.shape
    k = w.shape[0]
    tm = _row_tile(t, 1024)

    def body(dy_ref, w_ref, o_ref):
        o_ref[...] = lax.dot_general(dy_ref[...], w_ref[...], (((1,), (1,)), ((), ())),
                                     preferred_element_type=F32).astype(out_dtype)

    return pl.pallas_call(
        body, name=name, grid=(t // tm,),
        in_specs=[pl.BlockSpec((tm, n), lambda i: (i, 0)), pl.BlockSpec((k, n), lambda i: (0, group))],
        out_specs=pl.BlockSpec((tm, k), lambda i: (i, 0)),
        out_shape=jax.ShapeDtypeStruct((t, k), out_dtype), compiler_params=_cp(),
    )(dy, w)


def _out_bwd(dx, w, o, name):
    t = dx.shape[0]
    tm = _row_tile(t, 512)

    def body(dx_ref, w_ref, o_ref, et_ref, do_ref, adj_ref):
        do = lax.dot_general(dx_ref[...], w_ref[...], (((1,), (1,)), ((), ())), preferred_element_type=F32)
        do_ref[...] = do.astype(BF16)
        adj_ref[...] = -_dot_split(do * o_ref[...].astype(F32), et_ref[...])

    row = pl.BlockSpec((tm, D_MODEL), lambda i: (i, 0))
    return pl.pallas_call(
        body, name=name, grid=(t // tm,),
        in_specs=[row, pl.BlockSpec((D_MODEL, D_MODEL), lambda i: (0, 0)), row,
                  pl.BlockSpec((D_MODEL, LANES), lambda i: (0, 0))],
        out_specs=[row, pl.BlockSpec((tm, LANES), lambda i: (i, 0))],
        out_shape=[jax.ShapeDtypeStruct((t, D_MODEL), BF16), jax.ShapeDtypeStruct((t, LANES), F32)],
        compiler_params=_cp(),
    )(dx, w, o, _head_expander().T)


def _mm_tn(a, bs, name):
    aq = a.ndim == 3
    bq = bs[0].ndim == 3
    t, ka = a.shape[-2:]
    n = bs[0].shape[-1]
    nq = N_CHIPS if (aq or bq) else 1
    tt = _row_tile(t, GRAD_TOKENS)
    tn = n if n <= 1024 else 768
    assert n % tn == 0
    nb = len(bs)
    steps = t // tt

    def body(*refs):
        a_ref = refs[0]
        b_refs = refs[1:1 + nb]
        o_refs = refs[1 + nb:1 + 2 * nb]
        acc_refs = refs[1 + 2 * nb:]
        s = pl.program_id(2)
        av = a_ref[...]
        for b_ref, o_ref, acc_ref in zip(b_refs, o_refs, acc_refs):
            @pl.when(s == 0)
            def _():
                acc_ref[...] = jnp.zeros_like(acc_ref)

            acc_ref[...] += lax.dot_general(av, b_ref[...], (((0,), (0,)), ((), ())), preferred_element_type=F32)

            @pl.when(s == steps - 1)
            def _():
                o_ref[...] = acc_ref[...].astype(BF16)

    a_spec = (pl.BlockSpec((None, tt, ka), lambda q, j, s: (q, s, 0)) if aq
              else pl.BlockSpec((tt, ka), lambda q, j, s: (s, 0)))
    b_spec = (pl.BlockSpec((None, tt, tn), lambda q, j, s: (q, s, j)) if bq
              else pl.BlockSpec((tt, tn), lambda q, j, s: (s, j)))
    if nq > 1:
        o_spec = pl.BlockSpec((None, ka, tn), lambda q, j, s: (q, 0, j))
        o_shape = jax.ShapeDtypeStruct((nq, ka, n), BF16)
    else:
        o_spec = pl.BlockSpec((ka, tn), lambda q, j, s: (0, j))
        o_shape = jax.ShapeDtypeStruct((ka, n), BF16)
    outs = pl.pallas_call(
        body, name=name, grid=(nq, n // tn, steps),
        in_specs=[a_spec] + [b_spec] * nb, out_specs=[o_spec] * nb, out_shape=[o_shape] * nb,
        scratch_shapes=[pltpu.VMEM((ka, tn), F32)] * nb, compiler_params=_cp(),
    )(a, *bs)
    return outs


def _mm_grad(at, bs, name):
    ka, t = at.shape
    bq = bs[0].ndim == 3
    n = bs[0].shape[-1]
    nq = N_CHIPS if bq else 1
    tt = _row_tile(t, GRAD_TOKENS)
    tn = n if n <= 1024 else 768
    assert n % tn == 0
    nb = len(bs)
    steps = t // tt

    def body(*refs):
        a_ref = refs[0]
        b_refs = refs[1:1 + nb]
        o_refs = refs[1 + nb:1 + 2 * nb]
        acc_refs = refs[1 + 2 * nb:]
        s = pl.program_id(2)
        av = a_ref[...]
        for b_ref, o_ref, acc_ref in zip(b_refs, o_refs, acc_refs):
            @pl.when(s == 0)
            def _():
                acc_ref[...] = jnp.zeros_like(acc_ref)

            acc_ref[...] += jnp.dot(av, b_ref[...], preferred_element_type=F32)

            @pl.when(s == steps - 1)
            def _():
                o_ref[...] = acc_ref[...].astype(BF16)

    a_spec = pl.BlockSpec((ka, tt), lambda q, j, s: (0, s))
    if bq:
        b_spec = pl.BlockSpec((None, tt, tn), lambda q, j, s: (q, s, j))
        o_spec = pl.BlockSpec((None, ka, tn), lambda q, j, s: (q, 0, j))
        o_shape = jax.ShapeDtypeStruct((nq, ka, n), BF16)
    else:
        b_spec = pl.BlockSpec((tt, tn), lambda q, j, s: (s, j))
        o_spec = pl.BlockSpec((ka, tn), lambda q, j, s: (0, j))
        o_shape = jax.ShapeDtypeStruct((ka, n), BF16)
    return pl.pallas_call(
        body, name=name, grid=(nq, n // tn, steps),
        in_specs=[a_spec] + [b_spec] * nb, out_specs=[o_spec] * nb, out_shape=[o_shape] * nb,
        scratch_shapes=[pltpu.VMEM((ka, tn), F32)] * nb, compiler_params=_cp(),
    )(at, *bs)


def _sigmoid(x):
    return 1.0 / (1.0 + jnp.exp(-x))


def _ffn_up(h, wg, wu, layer, name):
    t = h.shape[0]
    tm = _row_tile(t, 1024)

    def body(h_ref, wg_ref, wu_ref, a_ref, dg_ref, du_ref):
        hv = h_ref[...]
        g = jnp.dot(hv, wg_ref[...], preferred_element_type=F32)
        u = jnp.dot(hv, wu_ref[...], preferred_element_type=F32)
        sg = _sigmoid(g)
        silu = g * sg
        a_ref[...] = (silu * u).astype(BF16)
        dg_ref[...] = (sg * (1.0 + g * (1.0 - sg)) * u).astype(BF16)
        du_ref[...] = silu.astype(BF16)

    wspec = pl.BlockSpec((None, None, D_MODEL, FF_SH), lambda q, i: (q, layer, 0, 0))
    ospec = pl.BlockSpec((None, tm, FF_SH), lambda q, i: (q, i, 0))
    oshape = jax.ShapeDtypeStruct((N_CHIPS, t, FF_SH), BF16)
    return pl.pallas_call(
        body, name=name, grid=(N_CHIPS, t // tm),
        in_specs=[pl.BlockSpec((tm, D_MODEL), lambda q, i: (i, 0)), wspec, wspec],
        out_specs=[ospec] * 3, out_shape=[oshape] * 3, compiler_params=_cp(),
    )(h, wg, wu)


def _ffn_down(a, wd, res, layer, name):
    t = a.shape[1]
    tm = _row_tile(t, 512)

    def body(a_ref, w_ref, r_ref, o_ref):
        acc = r_ref[...]
        for q in range(N_CHIPS):
            acc = acc + jnp.dot(a_ref[q], w_ref[q], preferred_element_type=F32)
        o_ref[...] = acc

    return pl.pallas_call(
        body, name=name, grid=(t // tm,),
        in_specs=[pl.BlockSpec((N_CHIPS, tm, FF_SH), lambda i: (0, i, 0)),
                  pl.BlockSpec((N_CHIPS, None, FF_SH, D_MODEL), lambda i: (0, layer, 0, 0)),
                  pl.BlockSpec((tm, D_MODEL), lambda i: (i, 0))],
        out_specs=pl.BlockSpec((tm, D_MODEL), lambda i: (i, 0)),
        out_shape=jax.ShapeDtypeStruct((t, D_MODEL), F32), compiler_params=_cp(),
    )(a, wd, res)


def _ffn_down_bwd(dx, wd, fg, fu, layer, name):
    t = dx.shape[0]
    tm = _row_tile(t, 512)

    def body(dx_ref, w_ref, fg_ref, fu_ref, dg_ref, du_ref):
        dxv = dx_ref[...]
        for q in range(N_CHIPS):
            da = lax.dot_general(dxv, w_ref[q], (((1,), (1,)), ((), ())), preferred_element_type=F32)
            dg_ref[q] = (da * fg_ref[q].astype(F32)).astype(BF16)
            du_ref[q] = (da * fu_ref[q].astype(F32)).astype(BF16)

    aspec = pl.BlockSpec((N_CHIPS, tm, FF_SH), lambda i: (0, i, 0))
    oshape = jax.ShapeDtypeStruct((N_CHIPS, t, FF_SH), BF16)
    return pl.pallas_call(
        body, name=name, grid=(t // tm,),
        in_specs=[pl.BlockSpec((tm, D_MODEL), lambda i: (i, 0)),
                  pl.BlockSpec((N_CHIPS, None, FF_SH, D_MODEL), lambda i: (0, layer, 0, 0)), aspec, aspec],
        out_specs=[aspec] * 2, out_shape=[oshape] * 2, compiler_params=_cp(),
    )(dx, wd, fg, fu)


def _ffn_up_bwd(dg, du, wg, wu, layer, name):
    t = dg.shape[1]
    tm = _row_tile(t, 512)
    nt = (((1,), (1,)), ((), ()))

    def body(dg_ref, du_ref, wg_ref, wu_ref, o_ref):
        acc = jnp.zeros((tm, D_MODEL), F32)
        for q in range(N_CHIPS):
            acc = acc + lax.dot_general(dg_ref[q], wg_ref[q], nt, preferred_element_type=F32)
            acc = acc + lax.dot_general(du_ref[q], wu_ref[q], nt, preferred_element_type=F32)
        o_ref[...] = acc

    aspec = pl.BlockSpec((N_CHIPS, tm, FF_SH), lambda i: (0, i, 0))
    wspec = pl.BlockSpec((N_CHIPS, None, D_MODEL, FF_SH), lambda i: (0, layer, 0, 0))
    return pl.pallas_call(
        body, name=name, grid=(t // tm,),
        in_specs=[aspec, aspec, wspec, wspec],
        out_specs=pl.BlockSpec((tm, D_MODEL), lambda i: (i, 0)),
        out_shape=jax.ShapeDtypeStruct((t, D_MODEL), F32), compiler_params=_cp(),
    )(dg, du, wg, wu)


def _attn_geometry(length, half_window):
    qb = min(LANES, length)
    kw = min(qb + 2 * half_window, length)
    return qb, kw, length // qb


def _dup_kv(src_ref, dst_ref, s, length):
    ch = min(length, 256)
    lo = lax.broadcasted_iota(jnp.int32, (ch, LANES), 1) < HEAD_DIM

    def chunk(c, carry):
        r0 = pl.multiple_of(c * ch, ch)
        for j in range(N_KV // 2):
            tile = src_ref[s, pl.ds(r0, ch), j * LANES:(j + 1) * LANES].astype(F32)
            rolled = pltpu.roll(tile, HEAD_DIM, 1)
            dst_ref[2 * j, pl.ds(r0, ch), :] = jnp.where(lo, tile, rolled).astype(BF16)
            dst_ref[2 * j + 1, pl.ds(r0, ch), :] = jnp.where(lo, rolled, tile).astype(BF16)
        return carry

    lax.fori_loop(0, length // ch, chunk, 0)


def _stack_heads(ref, s, q0, qb, g):
    lo = lax.broadcasted_iota(jnp.int32, (qb, LANES), 1) < HEAD_DIM
    parts = []
    for a in range(4):
        col = (2 * g + a // 2) * LANES
        tile = ref[s, pl.ds(q0, qb), col:col + LANES]
        keep = lo if a % 2 == 0 else jnp.logical_not(lo)
        parts.append(jnp.where(keep, tile, jnp.zeros_like(tile)))
    return jnp.concatenate(parts, axis=0)


def _unstack_pair_t(stacked_t, qb, pair):
    lo = lax.broadcasted_iota(jnp.int32, (LANES, qb), 0) < HEAD_DIM
    both = jnp.where(lo, stacked_t[:, (2 * pair) * qb:(2 * pair + 1) * qb],
                     stacked_t[:, (2 * pair + 1) * qb:(2 * pair + 2) * qb])
    return both.T


def _band_mask_t(q0, k0, qb, kw, half_window):
    key = lax.broadcasted_iota(jnp.int32, (kw, 4 * qb), 0)
    qry = lax.broadcasted_iota(jnp.int32, (kw, 4 * qb), 1) & (qb - 1)
    return jnp.abs((q0 + qry) - (k0 + key)) <= half_window


def _block_origin(i, qb, kw, half_window, length):
    if isinstance(i, int):
        return i * qb, min(max(i * qb - half_window, 0), length - kw)
    return (pl.multiple_of(i * qb, qb),
            pl.multiple_of(jnp.clip(i * qb - half_window, 0, length - kw), HEAD_DIM))


def _head_row(vals, qb):
    return jnp.concatenate([jnp.broadcast_to(v, (1, qb)).astype(F32) for v in vals], axis=1)


def _attn_fwd(qkv, sink, n_seq, length, half_window, seq_blk, out_dtype, name):
    qb, kw, nblk = _attn_geometry(length, half_window)
    with_sink = sink is not None
    nt = (((1,), (1,)), ((), ()))
    tn = (((0,), (0,)), ((), ()))
    qkv3 = qkv.reshape(n_seq, length, QKV_W)

    def body(*refs):
        refs = list(refs)
        sink_ref = refs.pop(0) if with_sink else None
        q_ref, k_ref, v_ref, o_ref, lse_ref = refs[:5]
        kx_ref, vx_ref = refs[-2:]
        head_row = lax.broadcasted_iota(jnp.int32, (N_HEADS, qb), 0)
        for s in range(seq_blk):
            _dup_kv(k_ref, kx_ref, s, length)
            _dup_kv(v_ref, vx_ref, s, length)

            def block(i, carry):
                q0, k0 = _block_origin(i, qb, kw, half_window, length)
                valid = _band_mask_t(q0, k0, qb, kw, half_window)
                lse_tile = jnp.zeros((N_HEADS, qb), F32)
                for g in range(N_KV):
                    qs = _stack_heads(q_ref, s, q0, qb, g)
                    kx = kx_ref[g, pl.ds(k0, kw), :]
                    vx = vx_ref[g, pl.ds(k0, kw), :]
                    st = lax.dot_general(kx, qs, nt, preferred_element_type=F32)
                    st = jnp.where(valid, st, NEG_INF)
                    m = jnp.max(st, axis=0, keepdims=True)
                    if with_sink:
                        sk = _head_row([sink_ref[4 * g + a] for a in range(4)], qb)
                        m = jnp.maximum(m, sk)
                    e = jnp.exp(st - m)
                    den = jnp.sum(e, axis=0, keepdims=True)
                    if with_sink:
                        den = den + jnp.exp(sk - m)
                    ot = lax.dot_general(vx, e.astype(BF16), tn, preferred_element_type=F32) / den
                    for pair in range(2):
                        col = (2 * g + pair) * LANES
                        o_ref[s, pl.ds(q0, qb), col:col + LANES] = _unstack_pair_t(ot, qb, pair).astype(out_dtype)
                    lse = m + jnp.log(den)
                    for a in range(4):
                        lse_tile = jnp.where(head_row == 4 * g + a, lse[:, a * qb:(a + 1) * qb], lse_tile)
                lse_ref[s, :, pl.ds(q0, qb)] = lse_tile
                return carry

            if nblk == 1:
                block(0, 0)
            else:
                lax.fori_loop(0, nblk, block, 0)

    in_specs = [pl.BlockSpec((seq_blk, length, N_HEADS * HEAD_DIM), lambda n: (n, 0, 0)),
                pl.BlockSpec((seq_blk, length, N_KV * HEAD_DIM), lambda n: (n, 0, 4)),
                pl.BlockSpec((seq_blk, length, N_KV * HEAD_DIM), lambda n: (n, 0, 5))]
    args = [qkv3, qkv3, qkv3]
    if with_sink:
        in_specs.insert(0, pl.BlockSpec(memory_space=pltpu.SMEM))
        args.insert(0, sink)
    out_specs = [pl.BlockSpec((seq_blk, length, D_MODEL), lambda n: (n, 0, 0)),
                 pl.BlockSpec((seq_blk, N_HEADS, length), lambda n: (n, 0, 0))]
    out_shape = [jax.ShapeDtypeStruct((n_seq, length, D_MODEL), out_dtype),
                 jax.ShapeDtypeStruct((n_seq, N_HEADS, length), F32)]
    o, lse = pl.pallas_call(
        body, name=name, grid=(n_seq // seq_blk,), in_specs=in_specs, out_specs=out_specs, out_shape=out_shape,
        scratch_shapes=[pltpu.VMEM((N_KV, length, LANES), BF16), pltpu.VMEM((N_KV, length, LANES), BF16)],
        compiler_params=_cp(),
    )(*args)
    return o.reshape(n_seq * length, D_MODEL), lse


def _attn_bwd(qkv, do, adj, lse, sink, cos, sin, n_seq, length, half_window, seq_blk, dil, name):
    qb, kw, nblk = _attn_geometry(length, half_window)
    scale = 1.0 / math.sqrt(HEAD_DIM)
    with_sink = sink is not None
    nt = (((1,), (1,)), ((), ()))
    tn = (((0,), (0,)), ((), ()))
    qkv3 = qkv.reshape(n_seq, length, QKV_W)
    do3 = do.reshape(n_seq, length, D_MODEL)
    tabs = [t.reshape(dil, length, LANES) for t in (cos, sin)]
    tab_blocks = dil // seq_blk if dil >= seq_blk else 1

    def body(*refs):
        refs = list(refs)
        sink_ref = refs.pop(0) if with_sink else None
        q_ref, k_ref, v_ref, do_ref, aux_ref, lse_ref, cos_ref, sin_ref, dqkv_ref = refs[:9]
        ds_ref = refs[9] if with_sink else None
        kx_ref, vx_ref, dkx_ref, dvx_ref = refs[-4:]
        lane = lax.broadcasted_iota(jnp.int32, (1, LANES), 1)
        if with_sink:
            @pl.when(pl.program_id(0) == 0)
            def _():
                ds_ref[...] = jnp.zeros_like(ds_ref)

        for s in range(seq_blk):
            ts = s % dil
            _dup_kv(k_ref, kx_ref, s, length)
            _dup_kv(v_ref, vx_ref, s, length)
            dkx_ref[...] = jnp.zeros_like(dkx_ref)
            dvx_ref[...] = jnp.zeros_like(dvx_ref)

            def block(i, dsink):
                q0, k0 = _block_origin(i, qb, kw, half_window, length)
                valid = _band_mask_t(q0, k0, qb, kw, half_window)
                cs = cos_ref[ts, pl.ds(q0, qb), :] * scale
                sn = sin_ref[ts, pl.ds(q0, qb), :] * scale
                adj_tile = aux_ref[s, :, pl.ds(q0, qb)]
                lse_tile = lse_ref[s, :, pl.ds(q0, qb)]
                for g in range(N_KV):
                    qs = _stack_heads(q_ref, s, q0, qb, g)
                    dos = _stack_heads(do_ref, s, q0, qb, g)
                    kx = kx_ref[g, pl.ds(k0, kw), :]
                    vx = vx_ref[g, pl.ds(k0, kw), :]
                    st = lax.dot_general(kx, qs, nt, preferred_element_type=F32)
                    lse = _head_row([lse_tile[4 * g + a:4 * g + a + 1, :] for a in range(4)], qb)
                    pt = jnp.exp(jnp.where(valid, st, NEG_INF) - lse)
                    shift = _head_row([adj_tile[4 * g + a:4 * g + a + 1, :] for a in range(4)], qb)
                    dpt = lax.dot_general(vx, dos, nt, preferred_element_type=F32)
                    dst = pt * (dpt + shift)
                    if with_sink:
                        sk = _head_row([sink_ref[4 * g + a] for a in range(4)], qb)
                        dsk = jnp.exp(sk - lse) * shift
                        for a in range(4):
                            tot = jnp.sum(dsk[:, a * qb:(a + 1) * qb], axis=1, keepdims=True)
                            dsink = dsink + jnp.where(lane == 4 * g + a, tot, 0.0)
                    dsb = dst.astype(BF16)
                    pb = pt.astype(BF16)
                    dqt = lax.dot_general(kx, dsb, tn, preferred_element_type=F32)
                    for pair in range(2):
                        col = (2 * g + pair) * LANES
                        tile = _rope_t(_unstack_pair_t(dqt, qb, pair), cs, sn)
                        dqkv_ref[s, pl.ds(q0, qb), col:col + LANES] = tile.astype(BF16)
                    dkx_ref[g, pl.ds(k0, kw), :] += jnp.dot(dsb, qs, preferred_element_type=F32)
                    dvx_ref[g, pl.ds(k0, kw), :] += jnp.dot(pb, dos, preferred_element_type=F32)
                return dsink

            if nblk == 1:
                dsink = block(0, jnp.zeros((1, LANES), F32))
            else:
                dsink = lax.fori_loop(0, nblk, block, jnp.zeros((1, LANES), F32))
            if with_sink:
                ds_ref[0:1, :] += dsink

            ch = min(length, 256)
            lo_c = lax.broadcasted_iota(jnp.int32, (ch, LANES), 1) < HEAD_DIM

            def fin(c, carry):
                r0 = pl.multiple_of(c * ch, ch)
                cs = cos_ref[ts, pl.ds(r0, ch), :]
                sn = sin_ref[ts, pl.ds(r0, ch), :]
                for j in range(N_KV // 2):
                    both = []
                    for acc_ref in (dkx_ref, dvx_ref):
                        t0 = acc_ref[2 * j, pl.ds(r0, ch), :]
                        t1 = acc_ref[2 * j + 1, pl.ds(r0, ch), :]
                        t0 = t0 + pltpu.roll(t0, HEAD_DIM, 1)
                        t1 = t1 + pltpu.roll(t1, HEAD_DIM, 1)
                        both.append(jnp.where(lo_c, t0, t1))
                    kcol = N_HEADS * HEAD_DIM + j * LANES
                    vcol = (N_HEADS + N_KV) * HEAD_DIM + j * LANES
                    dqkv_ref[s, pl.ds(r0, ch), kcol:kcol + LANES] = _rope_t(both[0], cs, sn).astype(BF16)
                    dqkv_ref[s, pl.ds(r0, ch), vcol:vcol + LANES] = both[1].astype(BF16)
                return carry

            lax.fori_loop(0, length // ch, fin, 0)

    seq_map = lambda n: (n, 0, 0)
    tab_map = (lambda n: (n % tab_blocks, 0, 0)) if dil >= seq_blk else (lambda n: (0, 0, 0))
    tab_rows = min(seq_blk, dil)
    in_specs = [pl.BlockSpec((seq_blk, length, N_HEADS * HEAD_DIM), seq_map),
                pl.BlockSpec((seq_blk, length, N_KV * HEAD_DIM), lambda n: (n, 0, 4)),
                pl.BlockSpec((seq_blk, length, N_KV * HEAD_DIM), lambda n: (n, 0, 5)),
                pl.BlockSpec((seq_blk, length, D_MODEL), seq_map),
                pl.BlockSpec((seq_blk, N_HEADS, length), seq_map),
                pl.BlockSpec((seq_blk, N_HEADS, length), seq_map),
                pl.BlockSpec((tab_rows, length, LANES), tab_map),
                pl.BlockSpec((tab_rows, length, LANES), tab_map)]
    args = [qkv3, qkv3, qkv3, do3, adj, lse] + tabs
    if with_sink:
        in_specs.insert(0, pl.BlockSpec(memory_space=pltpu.SMEM))
        args.insert(0, sink)
    out_specs = [pl.BlockSpec((seq_blk, length, QKV_W), seq_map)]
    out_shape = [jax.ShapeDtypeStruct((n_seq, length, QKV_W), BF16)]
    if with_sink:
        out_specs.append(pl.BlockSpec((8, LANES), lambda n: (0, 0)))
        out_shape.append(jax.ShapeDtypeStruct((8, LANES), F32))
    outs = pl.pallas_call(
        body, name=name, grid=(n_seq // seq_blk,), in_specs=in_specs, out_specs=out_specs, out_shape=out_shape,
        scratch_shapes=[pltpu.VMEM((N_KV, length, LANES), BF16), pltpu.VMEM((N_KV, length, LANES), BF16),
                        pltpu.VMEM((N_KV, length, LANES), F32), pltpu.VMEM((N_KV, length, LANES), F32)],
        compiler_params=_cp(),
    )(*args)
    dqkv = outs[0].reshape(n_seq * length, QKV_W)
    return (dqkv, outs[1]) if with_sink else (dqkv, None)


def _head_expander():
    h = jnp.arange(LANES)[:, None]
    l = jnp.arange(D_MODEL)[None, :]
    return (l // HEAD_DIM == h).astype(BF16)


def _dot_split(a, e):
    hi = a.astype(BF16)
    lo = (a - hi.astype(F32)).astype(BF16)
    return jnp.dot(hi, e, preferred_element_type=F32) + jnp.dot(lo, e, preferred_element_type=F32)


def _mix_weights(lses):
    m = jnp.maximum(jnp.maximum(lses[0], lses[1]), lses[2])
    es = [jnp.exp(v - m) for v in lses]
    tot = es[0] + es[1] + es[2]
    return [e / tot for e in es]


def _mix_fwd(os_, lses, name):
    t = os_[0].shape[0]
    tm = _row_tile(t, 512)

    def body(o0, o1, o2, l0, l1, l2, e_ref, out_ref):
        wts = _mix_weights([l0[...], l1[...], l2[...]])
        acc = jnp.zeros((tm, D_MODEL), F32)
        for w, o_ref in zip(wts, (o0, o1, o2)):
            acc = acc + _dot_split(w, e_ref[...]) * o_ref[...]
        out_ref[...] = acc.astype(BF16)

    row = pl.BlockSpec((tm, D_MODEL), lambda i: (i, 0))
    lrow = pl.BlockSpec((tm, LANES), lambda i: (i, 0))
    return pl.pallas_call(
        body, name=name, grid=(t // tm,),
        in_specs=[row] * 3 + [lrow] * 3 + [pl.BlockSpec((LANES, D_MODEL), lambda i: (0, 0))],
        out_specs=row, out_shape=jax.ShapeDtypeStruct((t, D_MODEL), BF16), compiler_params=_cp(),
    )(*os_, *lses, _head_expander())


def _mix_bwd(dmix, os_, lses, name):
    t = dmix.shape[0]
    tm = _row_tile(t, 512)

    def body(d_ref, o0, o1, o2, l0, l1, l2, e_ref, et_ref, do0, do1, do2, a0, a1, a2):
        wts = _mix_weights([l0[...], l1[...], l2[...]])
        dv = d_ref[...].astype(F32)
        cs = [_dot_split(dv * o_ref[...], et_ref[...]) for o_ref in (o0, o1, o2)]
        mean_c = wts[0] * cs[0] + wts[1] * cs[1] + wts[2] * cs[2]
        for w, c, do_ref, a_ref in zip(wts, cs, (do0, do1, do2), (a0, a1, a2)):
            do_ref[...] = (_dot_split(w, e_ref[...]) * dv).astype(BF16)
            a_ref[...] = w * (c - mean_c) - w * c

    row = pl.BlockSpec((tm, D_MODEL), lambda i: (i, 0))
    lrow = pl.BlockSpec((tm, LANES), lambda i: (i, 0))
    e = _head_expander()
    return pl.pallas_call(
        body, name=name, grid=(t // tm,),
        in_specs=[row] * 4 + [lrow] * 3 + [pl.BlockSpec((LANES, D_MODEL), lambda i: (0, 0)),
                                            pl.BlockSpec((D_MODEL, LANES), lambda i: (0, 0))],
        out_specs=[row] * 3 + [lrow] * 3,
        out_shape=[jax.ShapeDtypeStruct((t, D_MODEL), BF16)] * 3 + [jax.ShapeDtypeStruct((t, LANES), F32)] * 3,
        compiler_params=_cp(),
    )(dmix, *os_, *lses, e, e.T)


def _stats_to_tokens(stat, batch, dil):
    n_seq, _, length = stat.shape
    t = stat.transpose(0, 2, 1).reshape(n_seq * length, N_HEADS)
    return _from_residue(jnp.pad(t, ((0, 0), (0, LANES - N_HEADS))), batch, dil)


def _stats_from_tokens(stat, batch, dil, n_seq, length):
    t = _to_residue(stat[:, :N_HEADS], batch, dil)
    return t.reshape(n_seq, length, N_HEADS).transpose(0, 2, 1)


def _group_geometry(batch, seq, dil, window):
    length = seq // dil
    n_seq = batch * dil
    seq_blk = max(1, min(dil, 1024 // length))
    return n_seq, length, (window // 2) // dil, seq_blk


def _local_step(x, target, a_in, a_sink, a_out, b_in, b_out, norm_mix, norm_ffn, wg, wu, wd, final_norm):
    batch, seq, _ = x.shape
    t = batch * seq
    x0 = x.reshape(t, D_MODEL)
    tgt = target.reshape(t, D_MODEL)
    tabs = {d: _rope_tables(seq, d) for _, d in DILATED}
    nm = [norm_mix[i:i + 1] for i in range(2)]
    nf = [norm_ffn[i:i + 1] for i in range(2)]

    h0, h0t = _rms_fwd(x0, nm[0], "rms_mix0", True)
    qkv0 = _qkv_proj(h0, a_in, *tabs[1], 0, "qkv0")
    o0, lse0 = _attn_fwd(qkv0, a_sink, batch, seq, HALF_WINDOW_A, 1, BF16, "attn0")
    x1 = _mm_res(o0, a_out, x0, "out0")
    hf0, hf0t = _rms_fwd(x1, nf[0], "rms_ffn0", True)
    act0, g0, u0 = _ffn_up(hf0, wg, wu, 0, "ffn_up0")
    x2 = _ffn_down(act0, wd, x1, 0, "ffn_down0")

    h1 = _rms_fwd(x2, nm[1], "rms_mix1")
    geo = [_group_geometry(batch, seq, d, w) for w, d in DILATED]
    h1g, qkv1, o1, lse1, lse1r = [], [], [], [], []
    for gi, (_, d) in enumerate(DILATED):
        n_seq, length, hw, sb = geo[gi]
        hp = _to_residue(h1, batch, d)
        pj = _qkv_proj(hp, b_in, *tabs[d], gi, f"qkv1_{gi}")
        o, lse = _attn_fwd(pj, None, n_seq, length, hw, sb, F32, f"attn1_{gi}")
        h1g.append(hp)
        qkv1.append(pj)
        o1.append(_from_residue(o, batch, d))
        lse1r.append(lse)
        lse1.append(_stats_to_tokens(lse, batch, d))
    omix = _mix_fwd(o1, lse1, "mix")
    x3 = _mm_res(omix, b_out, x2, "out1")
    hf1, hf1t = _rms_fwd(x3, nf[1], "rms_ffn1", True)
    act1, g1, u1 = _ffn_up(hf1, wg, wu, 1, "ffn_up1")
    x4 = _ffn_down(act1, wd, x3, 1, "ffn_down1")

    dx4, dx4b, dx4t, loss_cols, d_final = _final_loss(x4, final_norm.reshape(1, D_MODEL), tgt, "final_loss")

    def ffn_bwd(dxo, dxob, dxot, x_mid, hft, g, u, act, layer):
        dg, du = _ffn_down_bwd(dxob, wd, g, u, layer, f"ffn_down_bwd{layer}")
        (d_wdt,) = _mm_grad(dxot, [act], f"grad_wd{layer}")
        dh = _ffn_up_bwd(dg, du, wg, wu, layer, f"ffn_up_bwd{layer}")
        d_wg, d_wu = _mm_grad(hft, [dg, du], f"grad_wgu{layer}")
        dxm, dxmb, d_nf = _rms_bwd(x_mid, nf[layer], [dh], dxo, f"rms_ffn_bwd{layer}")
        return dxm, dxmb, d_nf, d_wg, d_wu, d_wdt

    dx3, dx3b, d_nf1, d_wg1, d_wu1, d_wd1 = ffn_bwd(dx4, dx4b, dx4t, x3, hf1t, g1, u1, act1, 1)

    dmix = _mm_nt(dx3b, b_out, 0, BF16, "out1_bwd")
    (d_b_out,) = _mm_tn(omix, [dx3b], "grad_b_out")
    mb = _mix_bwd(dmix, o1, lse1, "mix_bwd")
    dh1, d_b_in = [], []
    for gi, (_, d) in enumerate(DILATED):
        n_seq, length, hw, sb = geo[gi]
        dog = _to_residue(mb[gi], batch, d)
        adj = _stats_from_tokens(mb[3 + gi], batch, d, n_seq, length)
        dpj, _ = _attn_bwd(qkv1[gi], dog, adj, lse1r[gi], None, *tabs[d], n_seq, length, hw, sb, d, f"attn1_bwd{gi}")
        (dw,) = _mm_tn(h1g[gi], [dpj], f"grad_b_in{gi}")
        d_b_in.append(dw)
        dh1.append(_from_residue(_mm_nt(dpj, b_in, gi, F32, f"qkv1_bwd{gi}"), batch, d))
    dx2, dx2b, dx2t, d_nm1 = _rms_bwd(x2, nm[1], dh1, dx3, "rms_mix_bwd1", True)

    dx1, dx1b, d_nf0, d_wg0, d_wu0, d_wd0 = ffn_bwd(dx2, dx2b, dx2t, x1, hf0t, g0, u0, act0, 0)

    do0, adj0 = _out_bwd(dx1b, a_out, o0, "out0_bwd")
    (d_a_out,) = _mm_tn(o0, [dx1b], "grad_a_out")
    adj0 = _stats_from_tokens(adj0, batch, 1, batch, seq)
    dqkv0, d_sink = _attn_bwd(qkv0, do0, adj0, lse0, a_sink, *tabs[1], batch, seq, HALF_WINDOW_A, 1, 1, "attn0_bwd")
    (d_a_in,) = _mm_grad(h0t, [dqkv0], "grad_a_in")
    dh0 = _mm_nt(dqkv0, a_in, 0, F32, "qkv0_bwd")
    gx, _, d_nm0 = _rms_bwd(x0, nm[0], [dh0], dx1, "rms_mix_bwd0")

    grads = dict(a_in=d_a_in, a_out=d_a_out, b_in=jnp.concatenate(d_b_in, axis=1), b_out=d_b_out,
                 wg=(d_wg0, d_wg1), wu=(d_wu0, d_wu1), wd=(d_wd0, d_wd1))
    vecs = dict(norm_mix=(d_nm0, d_nm1), norm_ffn=(d_nf0, d_nf1), final=d_final, loss_cols=loss_cols, sink=d_sink)
    return gx.reshape(x.shape), grads, vecs


ANY = pl.BlockSpec(memory_space=pl.ANY)
HBM = pltpu.MemorySpace.HBM


def _me():
    return lax.axis_index("x"), lax.axis_index("y"), lax.axis_index("c")


def _chip_peer(x, y, j):
    px = 1 - x if j & 2 else x
    py = 1 - y if j & 1 else y
    return px, py, 2 * px + py


def _remote(src, dst, sems, k, dev):
    return pltpu.make_async_remote_copy(src_ref=src, dst_ref=dst, send_sem=sems[0].at[k], recv_sem=sems[1].at[k],
                                        device_id=dev, device_id_type=MESH)


def _col_window(ref, q, width):
    return ref.at[:, pl.ds(pl.multiple_of(q * width, LANES), width)]


def _half0(ref, h):
    n = ref.shape[0] // 2
    return ref.at[pl.ds(h * n, n)]


def _half1(ref, h):
    n = ref.shape[1] // 2
    return ref.at[:, pl.ds(h * n, n)]


def _half_rows(ref, h):
    n = ref.shape[-2] // 2
    if len(ref.shape) == 2:
        return ref.at[pl.ds(h * n, n)]
    return ref.at[:, pl.ds(h * n, n)]


def _place_shard(w, q_arr, col, name):
    lead, rows, cols = w.shape

    def body(q_ref, w_ref, o_ref):
        o_ref[...] = w_ref[...].astype(BF16)

    if col:
        assert lead == 1
        out_spec = pl.BlockSpec((rows, cols), lambda l, q: (0, q[0]))
        out_shape = jax.ShapeDtypeStruct((rows, N_CHIPS * cols), BF16)
    else:
        out_spec = pl.BlockSpec((None, None, rows, cols), lambda l, q: (q[0], l, 0, 0))
        out_shape = jax.ShapeDtypeStruct((N_CHIPS, lead, rows, cols), BF16)
    return pl.pallas_call(
        body, name=name,
        grid_spec=pltpu.PrefetchScalarGridSpec(
            num_scalar_prefetch=1, grid=(lead,),
            in_specs=[pl.BlockSpec((None, rows, cols), lambda l, q: (l, 0, 0))], out_specs=out_spec),
        out_shape=out_shape, compiler_params=_cp(),
    )(q_arr, w)


def _handshake(peers):
    barrier = pltpu.get_barrier_semaphore()
    for p in peers:
        pl.semaphore_signal(barrier, inc=1, device_id=p, device_id_type=MESH)
    pl.semaphore_wait(barrier, len(peers))


def _on_sequencer(name, collective_id, n_sem, n_local, body):
    @pl.kernel(mesh=plsc.ScalarSubcoreMesh(axis_name="seq", num_cores=1), name=name,
               scratch_types=(pltpu.SemaphoreType.DMA((n_sem,)), pltpu.SemaphoreType.DMA((n_sem,)),
                              pltpu.SemaphoreType.DMA((max(n_local, 1),))),
               compiler_params=pltpu.CompilerParams(collective_id=collective_id))
    def launch(send_sems, recv_sems, local_sems):
        body((send_sems, recv_sems), local_sems)

    launch()


def _gather_plan(outs, col_fam, sems, handshake):
    n_w = len(outs)
    x, y, c = _me()
    myq = 2 * x + y
    sib = (x, y, 1 - c)
    if handshake:
        _handshake([sib] + [_chip_peer(x, y, j)[:2] + (c,) for j in (1, 2, 3)])

    def slot(w, q):
        if col_fam[w]:
            return _col_window(outs[w], q, outs[w].shape[1] // N_CHIPS)
        return outs[w].at[q]

    first = []
    for w in range(n_w):
        for j in (1, 2, 3):
            px, py, _ = _chip_peer(x, y, j)
            mine = _half_rows(slot(w, myq), c)
            cp = _remote(mine, mine, sems, w * 6 + j - 1, (px, py, c))
            cp.start()
            first.append(cp)
    passed = []
    for w in range(n_w):
        for j in (1, 2, 3):
            _, _, pq = _chip_peer(x, y, j)
            land = _half_rows(slot(w, pq), c)
            _remote(land, land, sems, w * 6 + j - 1, sib).wait_recv()
            cp = _remote(land, land, sems, w * 6 + 2 + j, sib)
            cp.start()
            passed.append(cp)
    for w in range(n_w):
        for j in (1, 2, 3):
            _, _, pq = _chip_peer(x, y, j)
            land = _half_rows(slot(w, pq), 1 - c)
            _remote(land, land, sems, w * 6 + 2 + j, sib).wait_recv()
    for cp in first + passed:
        cp.wait_send()


def _gather_weights(bufs, col_fam):
    n_w = len(bufs)

    def body(*refs):
        _gather_plan(refs[n_w:2 * n_w], col_fam, refs[2 * n_w:2 * n_w + 2], False)

    return pl.pallas_call(
        body, name="gather_weights", in_specs=[ANY] * n_w, out_specs=[ANY] * n_w,
        out_shape=[jax.ShapeDtypeStruct(b.shape, b.dtype) for b in bufs],
        input_output_aliases={w: w for w in range(n_w)},
        scratch_shapes=[pltpu.SemaphoreType.DMA((6 * n_w,)), pltpu.SemaphoreType.DMA((6 * n_w,))],
    )(*bufs)


def _gather_weights_async(bufs, col_fam, name, collective_id):
    refs = [jax.new_ref(b, memory_space=HBM) for b in bufs]
    _on_sequencer(name, collective_id, 6 * len(bufs), 0,
                  lambda sems, _: _gather_plan(refs, col_fam, sems, True))
    return [r[...] for r in refs]


def _grad_half(ref, col, h):
    return _half0(ref, h) if col else _half1(ref, h)


def _swap_halves_with_sibling(grads, col_fam):
    n_w = len(grads)

    def body(*refs):
        _swap_plan(refs[:n_w], refs[n_w:2 * n_w], col_fam, refs[2 * n_w:], False)

    return pl.pallas_call(
        body, name="grad_swap_sibling", in_specs=[ANY] * n_w, out_specs=[ANY] * n_w,
        out_shape=_swap_shapes(grads, col_fam),
        scratch_shapes=[pltpu.SemaphoreType.DMA((n_w,)), pltpu.SemaphoreType.DMA((n_w,))],
    )(*grads)


def _swap_shapes(grads, col_fam):
    out = []
    for w, g in enumerate(grads):
        shp = (g.shape[0] // 2, g.shape[1]) if col_fam[w] else (g.shape[0], g.shape[1] // 2, g.shape[2])
        out.append(jax.ShapeDtypeStruct(shp, g.dtype))
    return out


def _swap_plan(ins, outs, col_fam, sems, handshake):
    x, y, c = _me()
    sib = (x, y, 1 - c)
    if handshake:
        _handshake([sib])
    cps = [_remote(_grad_half(ins[w], col_fam[w], 1 - c), outs[w], sems, w, sib) for w in range(len(ins))]
    for cp in cps:
        cp.start()
    for cp in cps:
        cp.wait_recv()
    for cp in cps:
        cp.wait_send()


def _swap_halves_async(grads, col_fam, name, collective_id):
    srcs = [jax.new_ref(g, memory_space=HBM) for g in grads]
    dsts = [jax.empty_ref(s, memory_space=HBM) for s in _swap_shapes(grads, col_fam)]
    _on_sequencer(name, collective_id, len(grads), 0, lambda sems, _: _swap_plan(srcs, dsts, col_fam, sems, True))
    return [r[...] for r in srcs], [r[...] for r in dsts]


def _half_add(mine, recv, c_arr, col, name):
    if col:
        rows, n = recv.shape
        tr = rows // 2
        grid = (2,)
        in_specs = [pl.BlockSpec((tr, n), lambda i, c: (2 * c[0] + i, 0)), pl.BlockSpec((tr, n), lambda i, c: (i, 0))]
        out_spec = pl.BlockSpec((tr, n), lambda i, c: (i, 0))
    else:
        _, rows, n = recv.shape
        grid = (N_CHIPS,)
        in_specs = [pl.BlockSpec((None, rows, n), lambda q, c: (q, c[0], 0)),
                    pl.BlockSpec((None, rows, n), lambda q, c: (q, 0, 0))]
        out_spec = pl.BlockSpec((None, rows, n), lambda q, c: (q, 0, 0))

    def body(c_ref, a_ref, b_ref, o_ref):
        o_ref[...] = (a_ref[...].astype(F32) + b_ref[...].astype(F32)).astype(BF16)

    return pl.pallas_call(
        body, name=name,
        grid_spec=pltpu.PrefetchScalarGridSpec(num_scalar_prefetch=1, grid=grid, in_specs=in_specs, out_specs=out_spec),
        out_shape=jax.ShapeDtypeStruct(recv.shape, BF16), compiler_params=_cp(),
    )(c_arr, mine, recv)


def _scatter_chip_sums(sums, col_fam):
    n_w = len(sums)

    def body(*refs):
        _scatter_plan(refs[:n_w], refs[n_w:2 * n_w], col_fam, refs[2 * n_w:2 * n_w + 2], refs[2 * n_w + 2], False)

    return pl.pallas_call(
        body, name="grad_scatter_chips", in_specs=[ANY] * n_w, out_specs=[ANY] * n_w,
        out_shape=_scatter_shapes(sums, col_fam),
        scratch_shapes=[pltpu.SemaphoreType.DMA((3 * n_w,)), pltpu.SemaphoreType.DMA((3 * n_w,)),
                        pltpu.SemaphoreType.DMA((n_w,))],
    )(*sums)


def _scatter_shapes(sums, col_fam):
    out = []
    for w, s in enumerate(sums):
        shp = (s.shape[0], s.shape[1] // N_CHIPS) if col_fam[w] else s.shape[1:]
        out.append(jax.ShapeDtypeStruct((N_CHIPS,) + shp, s.dtype))
    return out


def _scatter_plan(ins, outs, col_fam, sems, lsem, handshake):
    n_w = len(ins)
    x, y, c = _me()
    myq = 2 * x + y
    if handshake:
        _handshake([_chip_peer(x, y, j)[:2] + (c,) for j in (1, 2, 3)])

    def slab(w, q):
        if col_fam[w]:
            return _col_window(ins[w], q, ins[w].shape[1] // N_CHIPS)
        return ins[w].at[q]

    local = [pltpu.make_async_copy(slab(w, myq), outs[w].at[myq], lsem.at[w]) for w in range(n_w)]
    for cp in local:
        cp.start()
    cps = []
    for w in range(n_w):
        for j in (1, 2, 3):
            px, py, pq = _chip_peer(x, y, j)
            cp = _remote(slab(w, pq), outs[w].at[myq], sems, w * 3 + j - 1, (px, py, c))
            cp.start()
            cps.append(cp)
    for w in range(n_w):
        for j in (1, 2, 3):
            _, _, pq = _chip_peer(x, y, j)
            land = outs[w].at[pq]
            _remote(land, land, sems, w * 3 + j - 1, (x, y, c)).wait_recv()
    for cp in cps:
        cp.wait_send()
    for cp in local:
        cp.wait()


def _scatter_chip_sums_async(sums, col_fam, name, collective_id):
    srcs = [jax.new_ref(s, memory_space=HBM) for s in sums]
    dsts = [jax.empty_ref(s, memory_space=HBM) for s in _scatter_shapes(sums, col_fam)]
    _on_sequencer(name, collective_id, 3 * len(sums), len(sums),
                  lambda sems, lsem: _scatter_plan(srcs, dsts, col_fam, sems, lsem, True))
    return [r[...] for r in dsts]


def _sum_chips(parts, c_arr, prev, lead, shape, name):
    _, rows, n = parts.shape
    tr = rows // 2 if rows % 32 == 0 else rows
    nblk = rows // tr

    def body(c_ref, p_ref, *rest):
        o_ref = rest[-1]
        acc = p_ref[0].astype(F32)
        for q in range(1, N_CHIPS):
            acc = acc + p_ref[q].astype(F32)
        o_ref[...] = acc

    in_specs = [pl.BlockSpec((N_CHIPS, tr, n), lambda i, c: (0, i, 0))]
    args = [c_arr, parts]
    aliases = {}
    if prev is not None:
        in_specs.append(ANY)
        args.append(prev)
        aliases = {2: 0}
    return pl.pallas_call(
        body, name=name,
        grid_spec=pltpu.PrefetchScalarGridSpec(
            num_scalar_prefetch=1, grid=(nblk,), in_specs=in_specs,
            out_specs=pl.BlockSpec((None, tr, n), lambda i, c: (lead, c[0] * nblk + i, 0))),
        out_shape=jax.ShapeDtypeStruct(shape, F32), input_output_aliases=aliases, compiler_params=_cp(),
    )(*args)


def _join_plan(outs, place, sems, handshake):
    x, y, c = _me()
    sib = (x, y, 1 - c)
    if handshake:
        _handshake([sib])

    def half(k, h):
        o, lead = place[k]
        return _half_rows(outs[o].at[lead], h)

    cps = [_remote(half(k, c), half(k, c), sems, k, sib) for k in range(len(place))]
    for cp in cps:
        cp.start()
    for k in range(len(place)):
        land = half(k, 1 - c)
        _remote(land, land, sems, k, sib).wait_recv()
    for cp in cps:
        cp.wait_send()


def _join_halves(bufs, place, name):
    n_o = len(bufs)
    n_h = len(place)

    def body(*refs):
        _join_plan(refs[n_o:2 * n_o], place, refs[2 * n_o:2 * n_o + 2], False)

    return pl.pallas_call(
        body, name=name, in_specs=[ANY] * n_o, out_specs=[ANY] * n_o,
        out_shape=[jax.ShapeDtypeStruct(b.shape, b.dtype) for b in bufs],
        input_output_aliases={k: k for k in range(n_o)},
        scratch_shapes=[pltpu.SemaphoreType.DMA((n_h,)), pltpu.SemaphoreType.DMA((n_h,))],
    )(*bufs)


def _join_halves_async(bufs, place, name, collective_id):
    refs = [jax.new_ref(b, memory_space=HBM) for b in bufs]
    _on_sequencer(name, collective_id, len(place), 0, lambda sems, _: _join_plan(refs, place, sems, True))
    return [r[...] for r in refs]


def _allreduce_rows(rows):
    n_dev = 8
    n_r = len(rows)
    assert n_r <= 8

    def body(*refs):
        r_refs = refs[:n_r]
        o_ref, slots, send_sems, recv_sems = refs[n_r:]
        x, y, c = _me()
        me = 4 * x + 2 * y + c
        slots[me] = jnp.concatenate([r[...] for r in r_refs] + [jnp.zeros((8 - n_r, D_MODEL), F32)], axis=0)

        def peer(k):
            return (1 - x if k & 4 else x, 1 - y if k & 2 else y, 1 - c if k & 1 else c)

        cps = []
        for k in range(1, n_dev):
            cp = pltpu.make_async_remote_copy(src_ref=slots.at[me], dst_ref=slots.at[me], send_sem=send_sems.at[k - 1],
                                              recv_sem=recv_sems.at[k - 1], device_id=peer(k), device_id_type=MESH)
            cp.start()
            cps.append(cp)
        for k in range(1, n_dev):
            px, py, pc = peer(k)
            land = slots.at[4 * px + 2 * py + pc]
            pltpu.make_async_remote_copy(src_ref=land, dst_ref=land, send_sem=send_sems.at[k - 1],
                                         recv_sem=recv_sems.at[k - 1], device_id=peer(k),
                                         device_id_type=MESH).wait_recv()
        for cp in cps:
            cp.wait_send()
        acc = slots[0]
        for d in range(1, n_dev):
            acc = acc + slots[d]
        o_ref[...] = acc

    vm = pl.BlockSpec(memory_space=pltpu.VMEM)
    return pl.pallas_call(
        body, name="allreduce_rows", in_specs=[vm] * n_r, out_specs=vm,
        out_shape=jax.ShapeDtypeStruct((8, D_MODEL), F32),
        scratch_shapes=[pltpu.VMEM((n_dev, 8, D_MODEL), F32), pltpu.SemaphoreType.DMA((n_dev - 1,)),
                        pltpu.SemaphoreType.DMA((n_dev - 1,))],
    )(*rows)


def _adamw(w, g, m, v, name):
    shape = w.shape
    if len(shape) == 1:
        lead, rows, cols = 1, 1, shape[0]
    else:
        rows, cols = shape[-2:]
        lead = math.prod(shape[:-2])
    args = [a.reshape(lead, rows, cols) for a in (w, g, m, v)]
    tr = rows // 2 if rows % 16 == 0 else rows

    def body(w_ref, g_ref, m_ref, v_ref, d_ref, nm_ref, nv_ref):
        gv = g_ref[...]
        nm = ADAM_B1 * m_ref[...] + (1.0 - ADAM_B1) * gv
        nv = ADAM_B2 * v_ref[...] + (1.0 - ADAM_B2) * jnp.square(gv)
        m_hat = nm / (1.0 - ADAM_B1 ** ADAM_STEP)
        v_hat = nv / (1.0 - ADAM_B2 ** ADAM_STEP)
        d_ref[...] = -ADAM_LR * (m_hat / (jnp.sqrt(v_hat) + ADAM_EPS) + ADAM_WD * w_ref[...])
        nm_ref[...] = nm
        nv_ref[...] = nv

    spec = pl.BlockSpec((None, tr, cols), lambda l, i: (l, i, 0))
    outs = pl.pallas_call(
        body, name=name, grid=(lead, rows // tr), in_specs=[spec] * 4, out_specs=[spec] * 3,
        out_shape=[jax.ShapeDtypeStruct((lead, rows, cols), F32)] * 3, compiler_params=_cp(),
    )(*args)
    return [o.reshape(shape) for o in outs]


def kernel(x, a_w_in, a_sink, a_w_out, b_w_in, b_w_out, norm_mix, norm_ffn, w_gate, w_up, w_down, final_norm, loss_target, m_a_w_in, m_a_sink, m_a_w_out, m_b_w_in, m_b_w_out, m_norm_mix, m_norm_ffn, m_w_gate, m_w_up, m_w_down, m_final_norm, v_a_w_in, v_a_sink, v_a_w_out, v_b_w_in, v_b_w_out, v_norm_mix, v_norm_ffn, v_w_gate, v_w_up, v_w_down, v_final_norm):
    weights = dict(a_w_in=a_w_in, a_sink=a_sink, a_w_out=a_w_out, b_w_in=b_w_in, b_w_out=b_w_out, norm_mix=norm_mix,
                   norm_ffn=norm_ffn, w_gate=w_gate, w_up=w_up, w_down=w_down, final_norm=final_norm)
    mom = dict(a_w_in=m_a_w_in, a_sink=m_a_sink, a_w_out=m_a_w_out, b_w_in=m_b_w_in, b_w_out=m_b_w_out,
               norm_mix=m_norm_mix, norm_ffn=m_norm_ffn, w_gate=m_w_gate, w_up=m_w_up, w_down=m_w_down,
               final_norm=m_final_norm)
    var = dict(a_w_in=v_a_w_in, a_sink=v_a_sink, a_w_out=v_a_w_out, b_w_in=v_b_w_in, b_w_out=v_b_w_out,
               norm_mix=v_norm_mix, norm_ffn=v_norm_ffn, w_gate=v_w_gate, w_up=v_w_up, w_down=v_w_down,
               final_norm=v_final_norm)
    order = ["a_w_in", "a_sink", "a_w_out", "b_w_in", "b_w_out", "norm_mix", "norm_ffn", "w_gate", "w_up", "w_down",
             "final_norm"]

    c_arr = lax.axis_index("c").astype(jnp.int32).reshape(1)
    q_arr = (2 * lax.axis_index("x") + lax.axis_index("y")).astype(jnp.int32).reshape(1)
    shards = [a_w_in, a_w_out, b_w_in, b_w_out, w_gate, w_up, w_down]
    shard_names = ("a_in", "a_out", "b_in", "b_out", "wg", "wu", "wd")
    placed = [_place_shard(s, q_arr, col, f"place_{nm}")
              for s, col, nm in zip(shards, (True, False, True, False, False, False, False), shard_names)]
    (a_in,) = _gather_weights(placed[:1], (True,))
    a_out, b_in, b_out, wg, wu, wd = _gather_weights_async(placed[1:], (False, True, False, False, False, False),
                                                           "gather_weights_late", 1)
    a_out = a_out.reshape(D_MODEL, D_MODEL)
    b_out = b_out.reshape(D_MODEL, D_MODEL)

    gx, grads, vecs = _local_step(x, loss_target, a_in, a_sink[0], a_out, b_in, b_out, norm_mix, norm_ffn, wg, wu, wd,
                                  final_norm)

    rows_out = D_MODEL // N_CHIPS
    partials = [grads["a_in"], grads["b_in"],
                grads["a_out"].reshape(N_CHIPS, rows_out, D_MODEL), grads["b_out"].reshape(N_CHIPS, rows_out, D_MODEL),
                grads["wg"][0], grads["wg"][1], grads["wu"][0], grads["wu"][1], grads["wd"][0], grads["wd"][1]]
    col_fam = (True, True) + (False,) * 8
    names = ("a_in", "b_in", "a_out", "b_out", "wg0", "wg1", "wu0", "wu1", "wd0", "wd1")
    contrib = [None] * len(partials)

    def reduce_group(idx, tag, ids):
        parts = [partials[k] for k in idx]
        cols = tuple(col_fam[k] for k in idx)
        if ids is None:
            theirs = _swap_halves_with_sibling(parts, cols)
        else:
            parts, theirs = _swap_halves_async(parts, cols, f"grad_swap_{tag}", ids[0])
        sums = [_half_add(p, r, c_arr, cf, f"chip_sum_{names[k]}") for p, r, cf, k in zip(parts, theirs, cols, idx)]
        if ids is None:
            out = _scatter_chip_sums(sums, cols)
        else:
            out = _scatter_chip_sums_async(sums, cols, f"grad_scatter_{tag}", ids[1])
        for k, o in zip(idx, out):
            contrib[k] = o

    reduce_group([1, 3, 5, 7, 9], "layer1", (2, 3))
    reduce_group([2, 4, 6, 8], "ffn0", (4, 5))
    reduce_group([0], "a_in", None)
    shapes = [a_w_in.shape, b_w_in.shape, a_w_out.shape, b_w_out.shape, w_gate.shape, w_up.shape, w_gate.shape]
    place = [(0, 0), (1, 0), (2, 0), (3, 0), (4, 0), (4, 1), (5, 0), (5, 1), (6, 0), (6, 1)]
    bufs = [None] * len(shapes)
    for p, nm, (o, lead) in zip(contrib, names, place):
        bufs[o] = _sum_chips(p, c_arr, bufs[o], lead, shapes[o], f"sum_chips_{nm}")
    g_a_in, g_b_in, g_a_out, g_b_out, g_wg, g_wu, g_wdt = _join_halves(bufs, place, "grad_join_sibling")
    g_wd = g_wdt.transpose(0, 2, 1)

    sink_row = jnp.pad(vecs["sink"][0:1], ((0, 0), (0, D_MODEL - LANES)))
    tot = _allreduce_rows([vecs["norm_mix"][0], vecs["norm_mix"][1], vecs["norm_ffn"][0], vecs["norm_ffn"][1],
                           vecs["final"], vecs["loss_cols"], sink_row])
    loss = (0.5 / D_MODEL) * jnp.sum(tot[5])
    gw = dict(a_w_in=g_a_in, a_sink=tot[6:7, :N_HEADS], a_w_out=g_a_out, b_w_in=g_b_in, b_w_out=g_b_out,
              norm_mix=tot[0:2], norm_ffn=tot[2:4], w_gate=g_wg, w_up=g_wu, w_down=g_wd, final_norm=tot[4])

    delta, new_m, new_v = {}, {}, {}
    for n in order:
        delta[n], new_m[n], new_v[n] = _adamw(weights[n], gw[n], mom[n], var[n], f"adamw_{n}")
    return (loss, gx, *[gw[n] for n in order], *[delta[n] for n in order], *[new_m[n] for n in order],
            *[new_v[n] for n in order])
```

```python
import functools
import math

import jax
import jax.numpy as jnp
from jax import lax
from jax.experimental import pallas as pl
from jax.experimental.pallas import tpu as pltpu
from jax.experimental.pallas import tpu_sc as plsc

F32 = jnp.float32
BF16 = jnp.bfloat16

D_MODEL = 1024
HEAD_DIM = 64
N_HEADS = 16
N_KV = 4
QKV_W = 1536
D_FF = 2816
N_CHIPS = 4
FF_SH = D_FF // N_CHIPS
HALF_WINDOW_A = 128
DILATED = ((128, 1), (512, 4), (2048, 16))
ROPE_THETA = 10000.0
RMS_EPS = 1e-6
NEG_INF = -1e30
LANES = 128
ADAM_LR, ADAM_B1, ADAM_B2, ADAM_EPS, ADAM_WD, ADAM_STEP = 0.001, 0.9, 0.999, 1e-08, 0.01, 10
VMEM_LIMIT = 56 * 1024 * 1024
GRAD_TOKENS = 2048
MESH = pl.DeviceIdType.MESH


def _cp(**kw):
    return pltpu.CompilerParams(vmem_limit_bytes=VMEM_LIMIT, **kw)


def _row_tile(t, cap):
    tm = min(cap, t)
    assert t % tm == 0
    return tm


def _rope_tables(seq, dil):
    inv = 1.0 / (ROPE_THETA ** (jnp.arange(0, HEAD_DIM, 2, dtype=F32) / HEAD_DIM))
    ang = jnp.arange(seq, dtype=F32)[:, None] * inv[None, :]
    cos, sin = jnp.cos(ang), jnp.sin(ang)
    cos = jnp.tile(cos, (1, 4))
    sin = jnp.concatenate([-sin, sin, -sin, sin], axis=1)

    def perm(t):
        return t.reshape(seq // dil, dil, LANES).transpose(1, 0, 2).reshape(seq, LANES)

    return perm(cos), perm(sin)


def _swap_halves(t):
    lane = lax.broadcasted_iota(jnp.int32, t.shape, 1)
    return jnp.where((lane % HEAD_DIM) < HEAD_DIM // 2, pltpu.roll(t, LANES - 32, 1), pltpu.roll(t, 32, 1))


def _rope(t, cos, sin):
    return t * cos + _swap_halves(t) * sin


def _rope_t(t, cos, sin):
    return t * cos - _swap_halves(t) * sin


def _to_residue(t, batch, dil):
    if dil == 1:
        return t
    s = t.shape[0] // batch
    return t.reshape(batch, s // dil, dil, t.shape[1]).transpose(0, 2, 1, 3).reshape(t.shape)


def _from_residue(t, batch, dil):
    if dil == 1:
        return t
    s = t.shape[0] // batch
    return t.reshape(batch, dil, s // dil, t.shape[1]).transpose(0, 2, 1, 3).reshape(t.shape)


def _rms_fwd(x, w, name, with_t=False):
    t = x.shape[0]
    tm = _row_tile(t, 512)

    def body(x_ref, w_ref, o_ref, *ot_ref):
        xv = x_ref[...]
        r = lax.rsqrt(jnp.mean(xv * xv, axis=-1, keepdims=True) + RMS_EPS)
        y = (xv * r) * w_ref[...]
        o_ref[...] = y.astype(BF16)
        if with_t:
            ot_ref[0][...] = y.T.astype(BF16)

    out_specs = [pl.BlockSpec((tm, D_MODEL), lambda i: (i, 0))]
    out_shape = [jax.ShapeDtypeStruct((t, D_MODEL), BF16)]
    if with_t:
        out_specs.append(pl.BlockSpec((D_MODEL, tm), lambda i: (0, i)))
        out_shape.append(jax.ShapeDtypeStruct((D_MODEL, t), BF16))
    outs = pl.pallas_call(
        body, name=name, grid=(t // tm,),
        in_specs=[pl.BlockSpec((tm, D_MODEL), lambda i: (i, 0)), pl.BlockSpec((1, D_MODEL), lambda i: (0, 0))],
        out_specs=out_specs, out_shape=out_shape, compiler_params=_cp(),
    )(x, w)
    return outs if with_t else outs[0]


def _rms_bwd(x, w, dhs, dres, name, with_t=False):
    t = x.shape[0]
    tm = _row_tile(t, 512)
    n = len(dhs)

    def body(*refs):
        x_ref, w_ref = refs[0], refs[1]
        dh_refs = refs[2:2 + n]
        dres_ref = refs[2 + n]
        dx_ref, dxb_ref = refs[3 + n:5 + n]
        dw_ref = refs[-1]
        xv = x_ref[...]
        r = lax.rsqrt(jnp.mean(xv * xv, axis=-1, keepdims=True) + RMS_EPS)
        xh = xv * r
        dy = dh_refs[0][...].astype(F32)
        for k in range(1, n):
            dy = dy + dh_refs[k][...].astype(F32)
        dxh = dy * w_ref[...]
        dx = dres_ref[...] + r * (dxh - xh * jnp.mean(dxh * xh, axis=-1, keepdims=True))
        dx_ref[...] = dx
        dxb_ref[...] = dx.astype(BF16)
        if with_t:
            refs[5 + n][...] = dx.T.astype(BF16)

        @pl.when(pl.program_id(0) == 0)
        def _():
            dw_ref[...] = jnp.zeros_like(dw_ref)

        dw_ref[...] += jnp.sum(dy * xh, axis=0, keepdims=True)

    row = pl.BlockSpec((tm, D_MODEL), lambda i: (i, 0))
    vec = pl.BlockSpec((1, D_MODEL), lambda i: (0, 0))
    out_specs = [row, row]
    out_shape = [jax.ShapeDtypeStruct((t, D_MODEL), F32), jax.ShapeDtypeStruct((t, D_MODEL), BF16)]
    if with_t:
        out_specs.append(pl.BlockSpec((D_MODEL, tm), lambda i: (0, i)))
        out_shape.append(jax.ShapeDtypeStruct((D_MODEL, t), BF16))
    return pl.pallas_call(
        body, name=name, grid=(t // tm,),
        in_specs=[row, vec] + [row] * n + [row],
        out_specs=out_specs + [vec], out_shape=out_shape + [jax.ShapeDtypeStruct((1, D_MODEL), F32)],
        compiler_params=_cp(),
    )(x, w, *dhs, dres)


def _final_loss(x, w, target, name):
    t = x.shape[0]
    tm = _row_tile(t, 512)

    def body(x_ref, w_ref, t_ref, dx_ref, dxb_ref, dxt_ref, l_ref, dw_ref):
        xv = x_ref[...]
        r = lax.rsqrt(jnp.mean(xv * xv, axis=-1, keepdims=True) + RMS_EPS)
        xh = xv * r
        err = xh * w_ref[...] - t_ref[...]
        dy = err * (1.0 / D_MODEL)
        dxh = dy * w_ref[...]
        dx = r * (dxh - xh * jnp.mean(dxh * xh, axis=-1, keepdims=True))
        dx_ref[...] = dx
        dxb_ref[...] = dx.astype(BF16)
        dxt_ref[...] = dx.T.astype(BF16)

        @pl.when(pl.program_id(0) == 0)
        def _():
            l_ref[...] = jnp.zeros_like(l_ref)
            dw_ref[...] = jnp.zeros_like(dw_ref)

        l_ref[...] += jnp.sum(err * err, axis=0, keepdims=True)
        dw_ref[...] += jnp.sum(dy * xh, axis=0, keepdims=True)

    row = pl.BlockSpec((tm, D_MODEL), lambda i: (i, 0))
    vec = pl.BlockSpec((1, D_MODEL), lambda i: (0, 0))
    return pl.pallas_call(
        body, name=name, grid=(t // tm,),
        in_specs=[row, vec, row], out_specs=[row, row, pl.BlockSpec((D_MODEL, tm), lambda i: (0, i)), vec, vec],
        out_shape=[jax.ShapeDtypeStruct((t, D_MODEL), F32), jax.ShapeDtypeStruct((t, D_MODEL), BF16),
                   jax.ShapeDtypeStruct((D_MODEL, t), BF16),
                   jax.ShapeDtypeStruct((1, D_MODEL), F32), jax.ShapeDtypeStruct((1, D_MODEL), F32)],
        compiler_params=_cp(),
    )(x, w, target)


def _qkv_proj(h, w, cos, sin, group, name):
    t = h.shape[0]
    seq = cos.shape[0]
    tm = _row_tile(seq, 1024)
    n_q = N_HEADS * HEAD_DIM // LANES
    n_rope = (N_HEADS + N_KV) * HEAD_DIM // LANES
    scale = 1.0 / math.sqrt(HEAD_DIM)

    def body(h_ref, w_ref, cos_ref, sin_ref, o_ref):
        acc = jnp.dot(h_ref[...], w_ref[...], preferred_element_type=F32)
        cs, sn = cos_ref[...], sin_ref[...]
        csq, snq = cs * scale, sn * scale
        for c in range(QKV_W // LANES):
            blk = acc[:, c * LANES:(c + 1) * LANES]
            if c < n_q:
                blk = _rope(blk, csq, snq)
            elif c < n_rope:
                blk = _rope(blk, cs, sn)
            o_ref[:, c * LANES:(c + 1) * LANES] = blk.astype(BF16)

    tab = pl.BlockSpec((tm, LANES), lambda i: (i % (seq // tm), 0))
    return pl.pallas_call(
        body, name=name, grid=(t // tm,),
        in_specs=[pl.BlockSpec((tm, D_MODEL), lambda i: (i, 0)),
                  pl.BlockSpec((D_MODEL, QKV_W), lambda i: (0, group)), tab, tab],
        out_specs=pl.BlockSpec((tm, QKV_W), lambda i: (i, 0)),
        out_shape=jax.ShapeDtypeStruct((t, QKV_W), BF16), compiler_params=_cp(),
    )(h, w, cos, sin)


def _mm_res(a, w, res, name):
    t, k = a.shape
    tm = _row_tile(t, 1024)

    def body(a_ref, w_ref, r_ref, o_ref):
        o_ref[...] = r_ref[...] + jnp.dot(a_ref[...], w_ref[...], preferred_element_type=F32)

    return pl.pallas_call(
        body, name=name, grid=(t // tm,),
        in_specs=[pl.BlockSpec((tm, k), lambda i: (i, 0)), pl.BlockSpec((k, D_MODEL), lambda i: (0, 0)),
                  pl.BlockSpec((tm, D_MODEL), lambda i: (i, 0))],
        out_specs=pl.BlockSpec((tm, D_MODEL), lambda i: (i, 0)),
        out_shape=jax.ShapeDtypeStruct((t, D_MODEL), F32), compiler_params=_cp(),
    )(a, w, res)


def _mm_nt(dy, w, group, out_dtype, name):
    t, n = dy.shape
    k = w.shape[0]
    tm = _row_tile(t, 1024)

    def body(dy_ref, w_ref, o_ref):
        o_ref[...] = lax.dot_general(dy_ref[...], w_ref[...], (((1,), (1,)), ((), ())),
                                     preferred_element_type=F32).astype(out_dtype)

    return pl.pallas_call(
        body, name=name, grid=(t // tm,),
        in_specs=[pl.BlockSpec((tm, n), lambda i: (i, 0)), pl.BlockSpec((k, n), lambda i: (0, group))],
        out_specs=pl.BlockSpec((tm, k), lambda i: (i, 0)),
        out_shape=jax.ShapeDtypeStruct((t, k), out_dtype), compiler_params=_cp(),
    )(dy, w)


def _out_bwd(dx, w, o, name):
    t = dx.shape[0]
    tm = _row_tile(t, 512)

    def body(dx_ref, w_ref, o_ref, et_ref, do_ref, adj_ref):
        do = lax.dot_general(dx_ref[...], w_ref[...], (((1,), (1,)), ((), ())), preferred_element_type=F32)
        do_ref[...] = do.astype(BF16)
        adj_ref[...] = -_dot_split(do * o_ref[...].astype(F32), et_ref[...])

    row = pl.BlockSpec((tm, D_MODEL), lambda i: (i, 0))
    return pl.pallas_call(
        body, name=name, grid=(t // tm,),
        in_specs=[row, pl.BlockSpec((D_MODEL, D_MODEL), lambda i: (0, 0)), row,
                  pl.BlockSpec((D_MODEL, LANES), lambda i: (0, 0))],
        out_specs=[row, pl.BlockSpec((tm, LANES), lambda i: (i, 0))],
        out_shape=[jax.ShapeDtypeStruct((t, D_MODEL), BF16), jax.ShapeDtypeStruct((t, LANES), F32)],
        compiler_params=_cp(),
    )(dx, w, o, _head_expander().T)


def _mm_tn(a, bs, name):
    aq = a.ndim == 3
    bq = bs[0].ndim == 3
    t, ka = a.shape[-2:]
    n = bs[0].shape[-1]
    nq = N_CHIPS if (aq or bq) else 1
    tt = _row_tile(t, GRAD_TOKENS)
    tn = n if n <= 1024 else 768
    assert n % tn == 0
    nb = len(bs)
    steps = t // tt

    def body(*refs):
        a_ref = refs[0]
        b_refs = refs[1:1 + nb]
        o_refs = refs[1 + nb:1 + 2 * nb]
        acc_refs = refs[1 + 2 * nb:]
        s = pl.program_id(2)
        av = a_ref[...]
        for b_ref, o_ref, acc_ref in zip(b_refs, o_refs, acc_refs):
            @pl.when(s == 0)
            def _():
                acc_ref[...] = jnp.zeros_like(acc_ref)

            acc_ref[...] += lax.dot_general(av, b_ref[...], (((0,), (0,)), ((), ())), preferred_element_type=F32)

            @pl.when(s == steps - 1)
            def _():
                o_ref[...] = acc_ref[...].astype(BF16)

    a_spec = (pl.BlockSpec((None, tt, ka), lambda q, j, s: (q, s, 0)) if aq
              else pl.BlockSpec((tt, ka), lambda q, j, s: (s, 0)))
    b_spec = (pl.BlockSpec((None, tt, tn), lambda q, j, s: (q, s, j)) if bq
              else pl.BlockSpec((tt, tn), lambda q, j, s: (s, j)))
    if nq > 1:
        o_spec = pl.BlockSpec((None, ka, tn), lambda q, j, s: (q, 0, j))
        o_shape = jax.ShapeDtypeStruct((nq, ka, n), BF16)
    else:
        o_spec = pl.BlockSpec((ka, tn), lambda q, j, s: (0, j))
        o_shape = jax.ShapeDtypeStruct((ka, n), BF16)
    outs = pl.pallas_call(
        body, name=name, grid=(nq, n // tn, steps),
        in_specs=[a_spec] + [b_spec] * nb, out_specs=[o_spec] * nb, out_shape=[o_shape] * nb,
        scratch_shapes=[pltpu.VMEM((ka, tn), F32)] * nb, compiler_params=_cp(),
    )(a, *bs)
    return outs


def _mm_grad(at, bs, name):
    ka, t = at.shape
    bq = bs[0].ndim == 3
    n = bs[0].shape[-1]
    nq = N_CHIPS if bq else 1
    tt = _row_tile(t, GRAD_TOKENS)
    tn = n if n <= 1024 else 768
    assert n % tn == 0
    nb = len(bs)
    steps = t // tt

    def body(*refs):
        a_ref = refs[0]
        b_refs = refs[1:1 + nb]
        o_refs = refs[1 + nb:1 + 2 * nb]
        acc_refs = refs[1 + 2 * nb:]
        s = pl.program_id(2)
        av = a_ref[...]
        for b_ref, o_ref, acc_ref in zip(b_refs, o_refs, acc_refs):
            @pl.when(s == 0)
            def _():
                acc_ref[...] = jnp.zeros_like(acc_ref)

            acc_ref[...] += jnp.dot(av, b_ref[...], preferred_element_type=F32)

            @pl.when(s == steps - 1)
            def _():
                o_ref[...] = acc_ref[...].astype(BF16)

    a_spec = pl.BlockSpec((ka, tt), lambda q, j, s: (0, s))
    if bq:
        b_spec = pl.BlockSpec((None, tt, tn), lambda q, j, s: (q, s, j))
        o_spec = pl.BlockSpec((None, ka, tn), lambda q, j, s: (q, 0, j))
        o_shape = jax.ShapeDtypeStruct((nq, ka, n), BF16)
    else:
        b_spec = pl.BlockSpec((tt, tn), lambda q, j, s: (s, j))
        o_spec = pl.BlockSpec((ka, tn), lambda q, j, s: (0, j))
        o_shape = jax.ShapeDtypeStruct((ka, n), BF16)
    return pl.pallas_call(
        body, name=name, grid=(nq, n // tn, steps),
        in_specs=[a_spec] + [b_spec] * nb, out_specs=[o_spec] * nb, out_shape=[o_shape] * nb,
        scratch_shapes=[pltpu.VMEM((ka, tn), F32)] * nb, compiler_params=_cp(),
    )(at, *bs)


def _sigmoid(x):
    return 1.0 / (1.0 + jnp.exp(-x))


def _ffn_up(h, wg, wu, layer, name):
    t = h.shape[0]
    tm = _row_tile(t, 1024)

    def body(h_ref, wg_ref, wu_ref, a_ref, dg_ref, du_ref):
        hv = h_ref[...]
        g = jnp.dot(hv, wg_ref[...], preferred_element_type=F32)
        u = jnp.dot(hv, wu_ref[...], preferred_element_type=F32)
        sg = _sigmoid(g)
        silu = g * sg
        a_ref[...] = (silu * u).astype(BF16)
        dg_ref[...] = (sg * (1.0 + g * (1.0 - sg)) * u).astype(BF16)
        du_ref[...] = silu.astype(BF16)

    wspec = pl.BlockSpec((None, None, D_MODEL, FF_SH), lambda q, i: (q, layer, 0, 0))
    ospec = pl.BlockSpec((None, tm, FF_SH), lambda q, i: (q, i, 0))
    oshape = jax.ShapeDtypeStruct((N_CHIPS, t, FF_SH), BF16)
    return pl.pallas_call(
        body, name=name, grid=(N_CHIPS, t // tm),
        in_specs=[pl.BlockSpec((tm, D_MODEL), lambda q, i: (i, 0)), wspec, wspec],
        out_specs=[ospec] * 3, out_shape=[oshape] * 3, compiler_params=_cp(),
    )(h, wg, wu)


def _ffn_down(a, wd, res, layer, name):
    t = a.shape[1]
    tm = _row_tile(t, 512)

    def body(a_ref, w_ref, r_ref, o_ref):
        acc = r_ref[...]
        for q in range(N_CHIPS):
            acc = acc + jnp.dot(a_ref[q], w_ref[q], preferred_element_type=F32)
        o_ref[...] = acc

    return pl.pallas_call(
        body, name=name, grid=(t // tm,),
        in_specs=[pl.BlockSpec((N_CHIPS, tm, FF_SH), lambda i: (0, i, 0)),
                  pl.BlockSpec((N_CHIPS, None, FF_SH, D_MODEL), lambda i: (0, layer, 0, 0)),
                  pl.BlockSpec((tm, D_MODEL), lambda i: (i, 0))],
        out_specs=pl.BlockSpec((tm, D_MODEL), lambda i: (i, 0)),
        out_shape=jax.ShapeDtypeStruct((t, D_MODEL), F32), compiler_params=_cp(),
    )(a, wd, res)


def _ffn_down_bwd(dx, wd, fg, fu, layer, name):
    t = dx.shape[0]
    tm = _row_tile(t, 512)

    def body(dx_ref, w_ref, fg_ref, fu_ref, dg_ref, du_ref):
        dxv = dx_ref[...]
        for q in range(N_CHIPS):
            da = lax.dot_general(dxv, w_ref[q], (((1,), (1,)), ((), ())), preferred_element_type=F32)
            dg_ref[q] = (da * fg_ref[q].astype(F32)).astype(BF16)
            du_ref[q] = (da * fu_ref[q].astype(F32)).astype(BF16)

    aspec = pl.BlockSpec((N_CHIPS, tm, FF_SH), lambda i: (0, i, 0))
    oshape = jax.ShapeDtypeStruct((N_CHIPS, t, FF_SH), BF16)
    return pl.pallas_call(
        body, name=name, grid=(t // tm,),
        in_specs=[pl.BlockSpec((tm, D_MODEL), lambda i: (i, 0)),
                  pl.BlockSpec((N_CHIPS, None, FF_SH, D_MODEL), lambda i: (0, layer, 0, 0)), aspec, aspec],
        out_specs=[aspec] * 2, out_shape=[oshape] * 2, compiler_params=_cp(),
    )(dx, wd, fg, fu)


def _ffn_up_bwd(dg, du, wg, wu, layer, name):
    t = dg.shape[1]
    tm = _row_tile(t, 512)
    nt = (((1,), (1,)), ((), ()))

    def body(dg_ref, du_ref, wg_ref, wu_ref, o_ref):
        acc = jnp.zeros((tm, D_MODEL), F32)
        for q in range(N_CHIPS):
            acc = acc + lax.dot_general(dg_ref[q], wg_ref[q], nt, preferred_element_type=F32)
            acc = acc + lax.dot_general(du_ref[q], wu_ref[q], nt, preferred_element_type=F32)
        o_ref[...] = acc

    aspec = pl.BlockSpec((N_CHIPS, tm, FF_SH), lambda i: (0, i, 0))
    wspec = pl.BlockSpec((N_CHIPS, None, D_MODEL, FF_SH), lambda i: (0, layer, 0, 0))
    return pl.pallas_call(
        body, name=name, grid=(t // tm,),
        in_specs=[aspec, aspec, wspec, wspec],
        out_specs=pl.BlockSpec((tm, D_MODEL), lambda i: (i, 0)),
        out_shape=jax.ShapeDtypeStruct((t, D_MODEL), F32), compiler_params=_cp(),
    )(dg, du, wg, wu)


def _attn_geometry(length, half_window):
    qb = min(LANES, length)
    kw = min(qb + 2 * half_window, length)
    return qb, kw, length // qb


def _dup_kv(src_ref, dst_ref, s, length):
    ch = min(length, 256)
    lo = lax.broadcasted_iota(jnp.int32, (ch, LANES), 1) < HEAD_DIM

    def chunk(c, carry):
        r0 = pl.multiple_of(c * ch, ch)
        for j in range(N_KV // 2):
            tile = src_ref[s, pl.ds(r0, ch), j * LANES:(j + 1) * LANES].astype(F32)
            rolled = pltpu.roll(tile, HEAD_DIM, 1)
            dst_ref[2 * j, pl.ds(r0, ch), :] = jnp.where(lo, tile, rolled).astype(BF16)
            dst_ref[2 * j + 1, pl.ds(r0, ch), :] = jnp.where(lo, rolled, tile).astype(BF16)
        return carry

    lax.fori_loop(0, length // ch, chunk, 0)


def _stack_heads(ref, s, q0, qb, g):
    lo = lax.broadcasted_iota(jnp.int32, (qb, LANES), 1) < HEAD_DIM
    parts = []
    for a in range(4):
        col = (2 * g + a // 2) * LANES
        tile = ref[s, pl.ds(q0, qb), col:col + LANES]
        keep = lo if a % 2 == 0 else jnp.logical_not(lo)
        parts.append(jnp.where(keep, tile, jnp.zeros_like(tile)))
    return jnp.concatenate(parts, axis=0)


def _unstack_pair_t(stacked_t, qb, pair):
    both = jnp.concatenate([stacked_t[:, (2 * pair) * qb:(2 * pair + 1) * qb],
                            stacked_t[:, (2 * pair + 1) * qb:(2 * pair + 2) * qb]], axis=0)
    return both.T


def _band_mask_t(q0, k0, qb, kw, half_window):
    key = lax.broadcasted_iota(jnp.int32, (kw, 4 * qb), 0)
    qry = lax.broadcasted_iota(jnp.int32, (kw, 4 * qb), 1) & (qb - 1)
    return jnp.abs((q0 + qry) - (k0 + key)) <= half_window


def _block_origin(i, qb, kw, half_window, length):
    if isinstance(i, int):
        return i * qb, min(max(i * qb - half_window, 0), length - kw)
    return (pl.multiple_of(i * qb, qb),
            pl.multiple_of(jnp.clip(i * qb - half_window, 0, length - kw), HEAD_DIM))


def _head_row(vals, qb):
    return jnp.concatenate([jnp.broadcast_to(v, (1, qb)).astype(F32) for v in vals], axis=1)


def _attn_fwd(qkv, sink, n_seq, length, half_window, seq_blk, out_dtype, name):
    qb, kw, nblk = _attn_geometry(length, half_window)
    with_sink = sink is not None
    nt = (((1,), (1,)), ((), ()))
    tn = (((0,), (0,)), ((), ()))
    qkv3 = qkv.reshape(n_seq, length, QKV_W)

    def body(*refs):
        refs = list(refs)
        sink_ref = refs.pop(0) if with_sink else None
        q_ref, k_ref, v_ref, o_ref, lse_ref = refs[:5]
        kx_ref, vx_ref = refs[-2:]
        head_row = lax.broadcasted_iota(jnp.int32, (N_HEADS, qb), 0)
        for s in range(seq_blk):
            _dup_kv(k_ref, kx_ref, s, length)
            _dup_kv(v_ref, vx_ref, s, length)

            def block(i, carry):
                q0, k0 = _block_origin(i, qb, kw, half_window, length)
                valid = _band_mask_t(q0, k0, qb, kw, half_window)
                lse_tile = jnp.zeros((N_HEADS, qb), F32)
                groups = range(N_KV)
                sts = [lax.dot_general(kx_ref[g, pl.ds(k0, kw), :], _stack_heads(q_ref, s, q0, qb, g), nt,
                                       preferred_element_type=F32) for g in groups]
                sts = [jnp.where(valid, st, NEG_INF) for st in sts]
                ms = [jnp.max(st, axis=0, keepdims=True) for st in sts]
                if with_sink:
                    sks = [_head_row([sink_ref[4 * g + a] for a in range(4)], qb) for g in groups]
                    ms = [jnp.maximum(m, sk) for m, sk in zip(ms, sks)]
                es = [jnp.exp(st - m) for st, m in zip(sts, ms)]
                dens = [jnp.sum(e, axis=0, keepdims=True) for e in es]
                if with_sink:
                    dens = [den + jnp.exp(sk - m) for den, sk, m in zip(dens, sks, ms)]
                ots = [lax.dot_general(vx_ref[g, pl.ds(k0, kw), 0:HEAD_DIM], es[g].astype(BF16), tn,
                                       preferred_element_type=F32) / dens[g] for g in groups]
                for g in groups:
                    for pair in range(2):
                        col = (2 * g + pair) * LANES
                        o_ref[s, pl.ds(q0, qb), col:col + LANES] = _unstack_pair_t(ots[g], qb, pair).astype(out_dtype)
                    lse = ms[g] + jnp.log(dens[g])
                    for a in range(4):
                        lse_tile = jnp.where(head_row == 4 * g + a, lse[:, a * qb:(a + 1) * qb], lse_tile)
                lse_ref[s, :, pl.ds(q0, qb)] = lse_tile
                return carry

            if nblk == 1:
                block(0, 0)
            else:
                lax.fori_loop(0, nblk, block, 0)

    in_specs = [pl.BlockSpec((seq_blk, length, N_HEADS * HEAD_DIM), lambda n: (n, 0, 0)),
                pl.BlockSpec((seq_blk, length, N_KV * HEAD_DIM), lambda n: (n, 0, 4)),
                pl.BlockSpec((seq_blk, length, N_KV * HEAD_DIM), lambda n: (n, 0, 5))]
    args = [qkv3, qkv3, qkv3]
    if with_sink:
        in_specs.insert(0, pl.BlockSpec(memory_space=pltpu.SMEM))
        args.insert(0, sink)
    out_specs = [pl.BlockSpec((seq_blk, length, D_MODEL), lambda n: (n, 0, 0)),
                 pl.BlockSpec((seq_blk, N_HEADS, length), lambda n: (n, 0, 0))]
    out_shape = [jax.ShapeDtypeStruct((n_seq, length, D_MODEL), out_dtype),
                 jax.ShapeDtypeStruct((n_seq, N_HEADS, length), F32)]
    o, lse = pl.pallas_call(
        body, name=name, grid=(n_seq // seq_blk,), in_specs=in_specs, out_specs=out_specs, out_shape=out_shape,
        scratch_shapes=[pltpu.VMEM((N_KV, length, LANES), BF16), pltpu.VMEM((N_KV, length, LANES), BF16)],
        compiler_params=_cp(),
    )(*args)
    return o.reshape(n_seq * length, D_MODEL), lse


def _attn_bwd(qkv, do, adj, lse, sink, cos, sin, n_seq, length, half_window, seq_blk, dil, name):
    qb, kw, nblk = _attn_geometry(length, half_window)
    scale = 1.0 / math.sqrt(HEAD_DIM)
    with_sink = sink is not None
    nt = (((1,), (1,)), ((), ()))
    tn = (((0,), (0,)), ((), ()))
    qkv3 = qkv.reshape(n_seq, length, QKV_W)
    do3 = do.reshape(n_seq, length, D_MODEL)
    tabs = [t.reshape(dil, length, LANES) for t in (cos, sin)]
    tab_blocks = dil // seq_blk if dil >= seq_blk else 1

    def body(*refs):
        refs = list(refs)
        sink_ref = refs.pop(0) if with_sink else None
        q_ref, k_ref, v_ref, do_ref, aux_ref, lse_ref, cos_ref, sin_ref, dqkv_ref = refs[:9]
        ds_ref = refs[9] if with_sink else None
        kx_ref, vx_ref, dkx_ref, dvx_ref = refs[-4:]
        lane = lax.broadcasted_iota(jnp.int32, (1, LANES), 1)
        if with_sink:
            @pl.when(pl.program_id(0) == 0)
            def _():
                ds_ref[...] = jnp.zeros_like(ds_ref)

        for s in range(seq_blk):
            ts = s % dil
            _dup_kv(k_ref, kx_ref, s, length)
            _dup_kv(v_ref, vx_ref, s, length)
            dkx_ref[...] = jnp.zeros_like(dkx_ref)
            dvx_ref[...] = jnp.zeros_like(dvx_ref)

            def block(i, dsink):
                q0, k0 = _block_origin(i, qb, kw, half_window, length)
                valid = _band_mask_t(q0, k0, qb, kw, half_window)
                cs = cos_ref[ts, pl.ds(q0, qb), :] * scale
                sn = sin_ref[ts, pl.ds(q0, qb), :] * scale
                adj_tile = aux_ref[s, :, pl.ds(q0, qb)]
                lse_tile = lse_ref[s, :, pl.ds(q0, qb)]
                groups = range(N_KV)
                qss = [_stack_heads(q_ref, s, q0, qb, g) for g in groups]
                doss = [_stack_heads(do_ref, s, q0, qb, g) for g in groups]
                kxs = [kx_ref[g, pl.ds(k0, kw), :] for g in groups]
                sts = [lax.dot_general(kxs[g], qss[g], nt, preferred_element_type=F32) for g in groups]
                dpts = [lax.dot_general(vx_ref[g, pl.ds(k0, kw), :], doss[g], nt, preferred_element_type=F32)
                        for g in groups]
                lses = [_head_row([lse_tile[4 * g + a:4 * g + a + 1, :] for a in range(4)], qb) for g in groups]
                shifts = [_head_row([adj_tile[4 * g + a:4 * g + a + 1, :] for a in range(4)], qb) for g in groups]
                pts = [jnp.exp(jnp.where(valid, sts[g], NEG_INF) - lses[g]) for g in groups]
                dsbs = [(pts[g] * (dpts[g] + shifts[g])).astype(BF16) for g in groups]
                pbs = [pt.astype(BF16) for pt in pts]
                if with_sink:
                    for g in groups:
                        sk = _head_row([sink_ref[4 * g + a] for a in range(4)], qb)
                        dsk = jnp.exp(sk - lses[g]) * shifts[g]
                        for a in range(4):
                            tot = jnp.sum(dsk[:, a * qb:(a + 1) * qb], axis=1, keepdims=True)
                            dsink = dsink + jnp.where(lane == 4 * g + a, tot, 0.0)
                dqts = [lax.dot_general(kx_ref[g, pl.ds(k0, kw), 0:HEAD_DIM], dsbs[g], tn, preferred_element_type=F32)
                        for g in groups]
                for g in groups:
                    for pair in range(2):
                        col = (2 * g + pair) * LANES
                        tile = _rope_t(_unstack_pair_t(dqts[g], qb, pair), cs, sn)
                        dqkv_ref[s, pl.ds(q0, qb), col:col + LANES] = tile.astype(BF16)
                for g in groups:
                    dkx_ref[g, pl.ds(k0, kw), :] += jnp.dot(dsbs[g], qss[g], preferred_element_type=F32)
                    dvx_ref[g, pl.ds(k0, kw), :] += jnp.dot(pbs[g], doss[g], preferred_element_type=F32)
                return dsink

            if nblk == 1:
                dsink = block(0, jnp.zeros((1, LANES), F32))
            else:
                dsink = lax.fori_loop(0, nblk, block, jnp.zeros((1, LANES), F32))
            if with_sink:
                ds_ref[0:1, :] += dsink

            ch = min(length, 256)
            lo_c = lax.broadcasted_iota(jnp.int32, (ch, LANES), 1) < HEAD_DIM

            def fin(c, carry):
                r0 = pl.multiple_of(c * ch, ch)
                cs = cos_ref[ts, pl.ds(r0, ch), :]
                sn = sin_ref[ts, pl.ds(r0, ch), :]
                for j in range(N_KV // 2):
                    both = []
                    for acc_ref in (dkx_ref, dvx_ref):
                        t0 = acc_ref[2 * j, pl.ds(r0, ch), :]
                        t1 = acc_ref[2 * j + 1, pl.ds(r0, ch), :]
                        t0 = t0 + pltpu.roll(t0, HEAD_DIM, 1)
                        t1 = t1 + pltpu.roll(t1, HEAD_DIM, 1)
                        both.append(jnp.where(lo_c, t0, t1))
                    kcol = N_HEADS * HEAD_DIM + j * LANES
                    vcol = (N_HEADS + N_KV) * HEAD_DIM + j * LANES
                    dqkv_ref[s, pl.ds(r0, ch), kcol:kcol + LANES] = _rope_t(both[0], cs, sn).astype(BF16)
                    dqkv_ref[s, pl.ds(r0, ch), vcol:vcol + LANES] = both[1].astype(BF16)
                return carry

            lax.fori_loop(0, length // ch, fin, 0)

    seq_map = lambda n: (n, 0, 0)
    tab_map = (lambda n: (n % tab_blocks, 0, 0)) if dil >= seq_blk else (lambda n: (0, 0, 0))
    tab_rows = min(seq_blk, dil)
    in_specs = [pl.BlockSpec((seq_blk, length, N_HEADS * HEAD_DIM), seq_map),
                pl.BlockSpec((seq_blk, length, N_KV * HEAD_DIM), lambda n: (n, 0, 4)),
                pl.BlockSpec((seq_blk, length, N_KV * HEAD_DIM), lambda n: (n, 0, 5)),
                pl.BlockSpec((seq_blk, length, D_MODEL), seq_map),
                pl.BlockSpec((seq_blk, N_HEADS, length), seq_map),
                pl.BlockSpec((seq_blk, N_HEADS, length), seq_map),
                pl.BlockSpec((tab_rows, length, LANES), tab_map),
                pl.BlockSpec((tab_rows, length, LANES), tab_map)]
    args = [qkv3, qkv3, qkv3, do3, adj, lse] + tabs
    if with_sink:
        in_specs.insert(0, pl.BlockSpec(memory_space=pltpu.SMEM))
        args.insert(0, sink)
    out_specs = [pl.BlockSpec((seq_blk, length, QKV_W), seq_map)]
    out_shape = [jax.ShapeDtypeStruct((n_seq, length, QKV_W), BF16)]
    if with_sink:
        out_specs.append(pl.BlockSpec((8, LANES), lambda n: (0, 0)))
        out_shape.append(jax.ShapeDtypeStruct((8, LANES), F32))
    outs = pl.pallas_call(
        body, name=name, grid=(n_seq // seq_blk,), in_specs=in_specs, out_specs=out_specs, out_shape=out_shape,
        scratch_shapes=[pltpu.VMEM((N_KV, length, LANES), BF16), pltpu.VMEM((N_KV, length, LANES), BF16),
                        pltpu.VMEM((N_KV, length, LANES), F32), pltpu.VMEM((N_KV, length, LANES), F32)],
        compiler_params=_cp(),
    )(*args)
    dqkv = outs[0].reshape(n_seq * length, QKV_W)
    return (dqkv, outs[1]) if with_sink else (dqkv, None)


def _head_expander():
    h = jnp.arange(LANES)[:, None]
    l = jnp.arange(D_MODEL)[None, :]
    return (l // HEAD_DIM == h).astype(BF16)


def _dot_split(a, e):
    hi = a.astype(BF16)
    lo = (a - hi.astype(F32)).astype(BF16)
    return jnp.dot(hi, e, preferred_element_type=F32) + jnp.dot(lo, e, preferred_element_type=F32)


def _mix_weights(lses):
    m = jnp.maximum(jnp.maximum(lses[0], lses[1]), lses[2])
    es = [jnp.exp(v - m) for v in lses]
    tot = es[0] + es[1] + es[2]
    return [e / tot for e in es]


def _mix_fwd(os_, lses, name):
    t = os_[0].shape[0]
    tm = _row_tile(t, 512)

    def body(o0, o1, o2, l0, l1, l2, e_ref, out_ref):
        wts = _mix_weights([l0[...], l1[...], l2[...]])
        acc = jnp.zeros((tm, D_MODEL), F32)
        for w, o_ref in zip(wts, (o0, o1, o2)):
            acc = acc + _dot_split(w, e_ref[...]) * o_ref[...]
        out_ref[...] = acc.astype(BF16)

    row = pl.BlockSpec((tm, D_MODEL), lambda i: (i, 0))
    lrow = pl.BlockSpec((tm, LANES), lambda i: (i, 0))
    return pl.pallas_call(
        body, name=name, grid=(t // tm,),
        in_specs=[row] * 3 + [lrow] * 3 + [pl.BlockSpec((LANES, D_MODEL), lambda i: (0, 0))],
        out_specs=row, out_shape=jax.ShapeDtypeStruct((t, D_MODEL), BF16), compiler_params=_cp(),
    )(*os_, *lses, _head_expander())


def _mix_bwd(dmix, os_, lses, name):
    t = dmix.shape[0]
    tm = _row_tile(t, 512)

    def body(d_ref, o0, o1, o2, l0, l1, l2, e_ref, et_ref, do0, do1, do2, a0, a1, a2):
        wts = _mix_weights([l0[...], l1[...], l2[...]])
        dv = d_ref[...].astype(F32)
        cs = [_dot_split(dv * o_ref[...], et_ref[...]) for o_ref in (o0, o1, o2)]
        mean_c = wts[0] * cs[0] + wts[1] * cs[1] + wts[2] * cs[2]
        for w, c, do_ref, a_ref in zip(wts, cs, (do0, do1, do2), (a0, a1, a2)):
            do_ref[...] = (_dot_split(w, e_ref[...]) * dv).astype(BF16)
            a_ref[...] = w * (c - mean_c) - w * c

    row = pl.BlockSpec((tm, D_MODEL), lambda i: (i, 0))
    lrow = pl.BlockSpec((tm, LANES), lambda i: (i, 0))
    e = _head_expander()
    return pl.pallas_call(
        body, name=name, grid=(t // tm,),
        in_specs=[row] * 4 + [lrow] * 3 + [pl.BlockSpec((LANES, D_MODEL), lambda i: (0, 0)),
                                            pl.BlockSpec((D_MODEL, LANES), lambda i: (0, 0))],
        out_specs=[row] * 3 + [lrow] * 3,
        out_shape=[jax.ShapeDtypeStruct((t, D_MODEL), BF16)] * 3 + [jax.ShapeDtypeStruct((t, LANES), F32)] * 3,
        compiler_params=_cp(),
    )(dmix, *os_, *lses, e, e.T)


def _stats_to_tokens(stat, batch, dil):
    n_seq, _, length = stat.shape
    t = stat.transpose(0, 2, 1).reshape(n_seq * length, N_HEADS)
    return _from_residue(jnp.pad(t, ((0, 0), (0, LANES - N_HEADS))), batch, dil)


def _stats_from_tokens(stat, batch, dil, n_seq, length):
    t = _to_residue(stat[:, :N_HEADS], batch, dil)
    return t.reshape(n_seq, length, N_HEADS).transpose(0, 2, 1)


def _group_geometry(batch, seq, dil, window):
    length = seq // dil
    n_seq = batch * dil
    seq_blk = max(1, min(dil, 1024 // length))
    return n_seq, length, (window // 2) // dil, seq_blk


def _local_step(x, target, a_in, a_sink, a_out, b_in, b_out, norm_mix, norm_ffn, wg, wu, wd, final_norm):
    batch, seq, _ = x.shape
    t = batch * seq
    x0 = x.reshape(t, D_MODEL)
    tgt = target.reshape(t, D_MODEL)
    tabs = {d: _rope_tables(seq, d) for _, d in DILATED}
    nm = [norm_mix[i:i + 1] for i in range(2)]
    nf = [norm_ffn[i:i + 1] for i in range(2)]

    h0, h0t = _rms_fwd(x0, nm[0], "rms_mix0", True)
    qkv0 = _qkv_proj(h0, a_in, *tabs[1], 0, "qkv0")
    o0, lse0 = _attn_fwd(qkv0, a_sink, batch, seq, HALF_WINDOW_A, 1, BF16, "attn0")
    x1 = _mm_res(o0, a_out, x0, "out0")
    hf0, hf0t = _rms_fwd(x1, nf[0], "rms_ffn0", True)
    act0, g0, u0 = _ffn_up(hf0, wg, wu, 0, "ffn_up0")
    x2 = _ffn_down(act0, wd, x1, 0, "ffn_down0")

    h1 = _rms_fwd(x2, nm[1], "rms_mix1")
    geo = [_group_geometry(batch, seq, d, w) for w, d in DILATED]
    h1g, qkv1, o1, lse1, lse1r = [], [], [], [], []
    for gi, (_, d) in enumerate(DILATED):
        n_seq, length, hw, sb = geo[gi]
        hp = _to_residue(h1, batch, d)
        pj = _qkv_proj(hp, b_in, *tabs[d], gi, f"qkv1_{gi}")
        o, lse = _attn_fwd(pj, None, n_seq, length, hw, sb, BF16, f"attn1_{gi}")
        h1g.append(hp)
        qkv1.append(pj)
        o1.append(_from_residue(o, batch, d))
        lse1r.append(lse)
        lse1.append(_stats_to_tokens(lse, batch, d))
    omix = _mix_fwd(o1, lse1, "mix")
    x3 = _mm_res(omix, b_out, x2, "out1")
    hf1, hf1t = _rms_fwd(x3, nf[1], "rms_ffn1", True)
    act1, g1, u1 = _ffn_up(hf1, wg, wu, 1, "ffn_up1")
    x4 = _ffn_down(act1, wd, x3, 1, "ffn_down1")

    dx4, dx4b, dx4t, loss_cols, d_final = _final_loss(x4, final_norm.reshape(1, D_MODEL), tgt, "final_loss")

    def ffn_bwd(dxo, dxob, dxot, x_mid, hft, g, u, act, layer):
        dg, du = _ffn_down_bwd(dxob, wd, g, u, layer, f"ffn_down_bwd{layer}")
        (d_wdt,) = _mm_grad(dxot, [act], f"grad_wd{layer}")
        dh = _ffn_up_bwd(dg, du, wg, wu, layer, f"ffn_up_bwd{layer}")
        d_wg, d_wu = _mm_grad(hft, [dg, du], f"grad_wgu{layer}")
        dxm, dxmb, d_nf = _rms_bwd(x_mid, nf[layer], [dh], dxo, f"rms_ffn_bwd{layer}")
        return dxm, dxmb, d_nf, d_wg, d_wu, d_wdt

    dx3, dx3b, d_nf1, d_wg1, d_wu1, d_wd1 = ffn_bwd(dx4, dx4b, dx4t, x3, hf1t, g1, u1, act1, 1)

    dmix = _mm_nt(dx3b, b_out, 0, BF16, "out1_bwd")
    (d_b_out,) = _mm_tn(omix, [dx3b], "grad_b_out")
    mb = _mix_bwd(dmix, o1, lse1, "mix_bwd")
    dh1, d_b_in = [], []
    for gi, (_, d) in enumerate(DILATED):
        n_seq, length, hw, sb = geo[gi]
        dog = _to_residue(mb[gi], batch, d)
        adj = _stats_from_tokens(mb[3 + gi], batch, d, n_seq, length)
        dpj, _ = _attn_bwd(qkv1[gi], dog, adj, lse1r[gi], None, *tabs[d], n_seq, length, hw, sb, d, f"attn1_bwd{gi}")
        (dw,) = _mm_tn(h1g[gi], [dpj], f"grad_b_in{gi}")
        d_b_in.append(dw)
        dh1.append(_from_residue(_mm_nt(dpj, b_in, gi, BF16, f"qkv1_bwd{gi}"), batch, d))
    dx2, dx2b, dx2t, d_nm1 = _rms_bwd(x2, nm[1], dh1, dx3, "rms_mix_bwd1", True)

    dx1, dx1b, d_nf0, d_wg0, d_wu0, d_wd0 = ffn_bwd(dx2, dx2b, dx2t, x1, hf0t, g0, u0, act0, 0)

    do0, adj0 = _out_bwd(dx1b, a_out, o0, "out0_bwd")
    (d_a_out,) = _mm_tn(o0, [dx1b], "grad_a_out")
    adj0 = _stats_from_tokens(adj0, batch, 1, batch, seq)
    dqkv0, d_sink = _attn_bwd(qkv0, do0, adj0, lse0, a_sink, *tabs[1], batch, seq, HALF_WINDOW_A, 1, 1, "attn0_bwd")
    (d_a_in,) = _mm_grad(h0t, [dqkv0], "grad_a_in")
    dh0 = _mm_nt(dqkv0, a_in, 0, F32, "qkv0_bwd")
    gx, _, d_nm0 = _rms_bwd(x0, nm[0], [dh0], dx1, "rms_mix_bwd0")

    grads = dict(a_in=d_a_in, a_out=d_a_out, b_in=jnp.concatenate(d_b_in, axis=1), b_out=d_b_out,
                 wg=(d_wg0, d_wg1), wu=(d_wu0, d_wu1), wd=(d_wd0, d_wd1))
    vecs = dict(norm_mix=(d_nm0, d_nm1), norm_ffn=(d_nf0, d_nf1), final=d_final, loss_cols=loss_cols, sink=d_sink)
    return gx.reshape(x.shape), grads, vecs


ANY = pl.BlockSpec(memory_space=pl.ANY)
HBM = pltpu.MemorySpace.HBM


def _me():
    return lax.axis_index("x"), lax.axis_index("y"), lax.axis_index("c")


def _chip_peer(x, y, j):
    px = 1 - x if j & 2 else x
    py = 1 - y if j & 1 else y
    return px, py, 2 * px + py


def _remote(src, dst, sems, k, dev):
    return pltpu.make_async_remote_copy(src_ref=src, dst_ref=dst, send_sem=sems[0].at[k], recv_sem=sems[1].at[k],
                                        device_id=dev, device_id_type=MESH)


def _col_window(ref, q, width):
    return ref.at[:, pl.ds(pl.multiple_of(q * width, LANES), width)]


def _half0(ref, h):
    n = ref.shape[0] // 2
    return ref.at[pl.ds(h * n, n)]


def _half1(ref, h):
    n = ref.shape[1] // 2
    return ref.at[:, pl.ds(h * n, n)]


def _half_rows(ref, h):
    n = ref.shape[-2] // 2
    if len(ref.shape) == 2:
        return ref.at[pl.ds(h * n, n)]
    return ref.at[:, pl.ds(h * n, n)]


def _place_shard(w, q_arr, col, name):
    lead, rows, cols = w.shape

    def body(q_ref, w_ref, o_ref):
        o_ref[...] = w_ref[...].astype(BF16)

    if col:
        assert lead == 1
        out_spec = pl.BlockSpec((rows, cols), lambda l, q: (0, q[0]))
        out_shape = jax.ShapeDtypeStruct((rows, N_CHIPS * cols), BF16)
    else:
        out_spec = pl.BlockSpec((None, None, rows, cols), lambda l, q: (q[0], l, 0, 0))
        out_shape = jax.ShapeDtypeStruct((N_CHIPS, lead, rows, cols), BF16)
    return pl.pallas_call(
        body, name=name,
        grid_spec=pltpu.PrefetchScalarGridSpec(
            num_scalar_prefetch=1, grid=(lead,),
            in_specs=[pl.BlockSpec((None, rows, cols), lambda l, q: (l, 0, 0))], out_specs=out_spec),
        out_shape=out_shape, compiler_params=_cp(),
    )(q_arr, w)


def _handshake(peers):
    barrier = pltpu.get_barrier_semaphore()
    for p in peers:
        pl.semaphore_signal(barrier, inc=1, device_id=p, device_id_type=MESH)
    pl.semaphore_wait(barrier, len(peers))


def _on_sequencer(name, collective_id, n_sem, n_local, body):
    @pl.kernel(mesh=plsc.ScalarSubcoreMesh(axis_name="seq", num_cores=1), name=name,
               scratch_types=(pltpu.SemaphoreType.DMA((n_sem,)), pltpu.SemaphoreType.DMA((n_sem,)),
                              pltpu.SemaphoreType.DMA((max(n_local, 1),))),
               compiler_params=pltpu.CompilerParams(collective_id=collective_id))
    def launch(send_sems, recv_sems, local_sems):
        body((send_sems, recv_sems), local_sems)

    launch()


def _gather_plan(outs, col_fam, sems, handshake):
    n_w = len(outs)
    x, y, c = _me()
    myq = 2 * x + y
    sib = (x, y, 1 - c)
    if handshake:
        _handshake([sib] + [_chip_peer(x, y, j)[:2] + (c,) for j in (1, 2, 3)])

    def slot(w, q):
        if col_fam[w]:
            return _col_window(outs[w], q, outs[w].shape[1] // N_CHIPS)
        return outs[w].at[q]

    first = []
    for w in range(n_w):
        for j in (1, 2, 3):
            px, py, _ = _chip_peer(x, y, j)
            mine = _half_rows(slot(w, myq), c)
            cp = _remote(mine, mine, sems, w * 6 + j - 1, (px, py, c))
            cp.start()
            first.append(cp)
    passed = []
    for w in range(n_w):
        for j in (1, 2, 3):
            _, _, pq = _chip_peer(x, y, j)
            land = _half_rows(slot(w, pq), c)
            _remote(land, land, sems, w * 6 + j - 1, sib).wait_recv()
            cp = _remote(land, land, sems, w * 6 + 2 + j, sib)
            cp.start()
            passed.append(cp)
    for w in range(n_w):
        for j in (1, 2, 3):
            _, _, pq = _chip_peer(x, y, j)
            land = _half_rows(slot(w, pq), 1 - c)
            _remote(land, land, sems, w * 6 + 2 + j, sib).wait_recv()
    for cp in first + passed:
        cp.wait_send()


def _gather_weights(bufs, col_fam):
    n_w = len(bufs)

    def body(*refs):
        _gather_plan(refs[n_w:2 * n_w], col_fam, refs[2 * n_w:2 * n_w + 2], False)

    return pl.pallas_call(
        body, name="gather_weights", in_specs=[ANY] * n_w, out_specs=[ANY] * n_w,
        out_shape=[jax.ShapeDtypeStruct(b.shape, b.dtype) for b in bufs],
        input_output_aliases={w: w for w in range(n_w)},
        scratch_shapes=[pltpu.SemaphoreType.DMA((6 * n_w,)), pltpu.SemaphoreType.DMA((6 * n_w,))],
    )(*bufs)


def _gather_weights_async(bufs, col_fam, name, collective_id):
    refs = [jax.new_ref(b, memory_space=HBM) for b in bufs]
    _on_sequencer(name, collective_id, 6 * len(bufs), 0,
                  lambda sems, _: _gather_plan(refs, col_fam, sems, True))
    return [r[...] for r in refs]


def _grad_half(ref, col, h):
    return _half0(ref, h) if col else _half1(ref, h)


def _swap_halves_with_sibling(grads, col_fam):
    n_w = len(grads)

    def body(*refs):
        _swap_plan(refs[:n_w], refs[n_w:2 * n_w], col_fam, refs[2 * n_w:], False)

    return pl.pallas_call(
        body, name="grad_swap_sibling", in_specs=[ANY] * n_w, out_specs=[ANY] * n_w,
        out_shape=_swap_shapes(grads, col_fam),
        scratch_shapes=[pltpu.SemaphoreType.DMA((n_w,)), pltpu.SemaphoreType.DMA((n_w,))],
    )(*grads)


def _swap_shapes(grads, col_fam):
    out = []
    for w, g in enumerate(grads):
        shp = (g.shape[0] // 2, g.shape[1]) if col_fam[w] else (g.shape[0], g.shape[1] // 2, g.shape[2])
        out.append(jax.ShapeDtypeStruct(shp, g.dtype))
    return out


def _swap_plan(ins, outs, col_fam, sems, handshake):
    x, y, c = _me()
    sib = (x, y, 1 - c)
    if handshake:
        _handshake([sib])
    cps = [_remote(_grad_half(ins[w], col_fam[w], 1 - c), outs[w], sems, w, sib) for w in range(len(ins))]
    for cp in cps:
        cp.start()
    for cp in cps:
        cp.wait_recv()
    for cp in cps:
        cp.wait_send()


def _swap_halves_async(grads, col_fam, name, collective_id):
    srcs = [jax.new_ref(g, memory_space=HBM) for g in grads]
    dsts = [jax.empty_ref(s, memory_space=HBM) for s in _swap_shapes(grads, col_fam)]
    _on_sequencer(name, collective_id, len(grads), 0, lambda sems, _: _swap_plan(srcs, dsts, col_fam, sems, True))
    return [r[...] for r in srcs], [r[...] for r in dsts]


def _half_add(mine, recv, c_arr, col, name):
    if col:
        rows, n = recv.shape
        tr = rows // 2
        grid = (2,)
        in_specs = [pl.BlockSpec((tr, n), lambda i, c: (2 * c[0] + i, 0)), pl.BlockSpec((tr, n), lambda i, c: (i, 0))]
        out_spec = pl.BlockSpec((tr, n), lambda i, c: (i, 0))
    else:
        _, rows, n = recv.shape
        grid = (N_CHIPS,)
        in_specs = [pl.BlockSpec((None, rows, n), lambda q, c: (q, c[0], 0)),
                    pl.BlockSpec((None, rows, n), lambda q, c: (q, 0, 0))]
        out_spec = pl.BlockSpec((None, rows, n), lambda q, c: (q, 0, 0))

    def body(c_ref, a_ref, b_ref, o_ref):
        o_ref[...] = (a_ref[...].astype(F32) + b_ref[...].astype(F32)).astype(BF16)

    return pl.pallas_call(
        body, name=name,
        grid_spec=pltpu.PrefetchScalarGridSpec(num_scalar_prefetch=1, grid=grid, in_specs=in_specs, out_specs=out_spec),
        out_shape=jax.ShapeDtypeStruct(recv.shape, BF16), compiler_params=_cp(),
    )(c_arr, mine, recv)


def _scatter_chip_sums(sums, col_fam):
    n_w = len(sums)

    def body(*refs):
        _scatter_plan(refs[:n_w], refs[n_w:2 * n_w], col_fam, refs[2 * n_w:2 * n_w + 2], refs[2 * n_w + 2], False)

    return pl.pallas_call(
        body, name="grad_scatter_chips", in_specs=[ANY] * n_w, out_specs=[ANY] * n_w,
        out_shape=_scatter_shapes(sums, col_fam),
        scratch_shapes=[pltpu.SemaphoreType.DMA((3 * n_w,)), pltpu.SemaphoreType.DMA((3 * n_w,)),
                        pltpu.SemaphoreType.DMA((n_w,))],
    )(*sums)


def _scatter_shapes(sums, col_fam):
    out = []
    for w, s in enumerate(sums):
        shp = (s.shape[0], s.shape[1] // N_CHIPS) if col_fam[w] else s.shape[1:]
        out.append(jax.ShapeDtypeStruct((N_CHIPS,) + shp, s.dtype))
    return out


def _scatter_plan(ins, outs, col_fam, sems, lsem, handshake):
    n_w = len(ins)
    x, y, c = _me()
    myq = 2 * x + y
    if handshake:
        _handshake([_chip_peer(x, y, j)[:2] + (c,) for j in (1, 2, 3)])

    def slab(w, q):
        if col_fam[w]:
            return _col_window(ins[w], q, ins[w].shape[1] // N_CHIPS)
        return ins[w].at[q]

    local = [pltpu.make_async_copy(slab(w, myq), outs[w].at[myq], lsem.at[w]) for w in range(n_w)]
    for cp in local:
        cp.start()
    cps = []
    for w in range(n_w):
        for j in (1, 2, 3):
            px, py, pq = _chip_peer(x, y, j)
            cp = _remote(slab(w, pq), outs[w].at[myq], sems, w * 3 + j - 1, (px, py, c))
            cp.start()
            cps.append(cp)
    for w in range(n_w):
        for j in (1, 2, 3):
            _, _, pq = _chip_peer(x, y, j)
            land = outs[w].at[pq]
            _remote(land, land, sems, w * 3 + j - 1, (x, y, c)).wait_recv()
    for cp in cps:
        cp.wait_send()
    for cp in local:
        cp.wait()


def _scatter_chip_sums_async(sums, col_fam, name, collective_id):
    srcs = [jax.new_ref(s, memory_space=HBM) for s in sums]
    dsts = [jax.empty_ref(s, memory_space=HBM) for s in _scatter_shapes(sums, col_fam)]
    _on_sequencer(name, collective_id, 3 * len(sums), len(sums),
                  lambda sems, lsem: _scatter_plan(srcs, dsts, col_fam, sems, lsem, True))
    return [r[...] for r in dsts]


def _sum_chips(parts, c_arr, prev, lead, shape, name):
    _, rows, n = parts.shape
    tr = rows // 2 if rows % 32 == 0 else rows
    nblk = rows // tr

    def body(c_ref, p_ref, *rest):
        o_ref = rest[-1]
        acc = p_ref[0].astype(F32)
        for q in range(1, N_CHIPS):
            acc = acc + p_ref[q].astype(F32)
        o_ref[...] = acc

    in_specs = [pl.BlockSpec((N_CHIPS, tr, n), lambda i, c: (0, i, 0))]
    args = [c_arr, parts]
    aliases = {}
    if prev is not None:
        in_specs.append(ANY)
        args.append(prev)
        aliases = {2: 0}
    return pl.pallas_call(
        body, name=name,
        grid_spec=pltpu.PrefetchScalarGridSpec(
            num_scalar_prefetch=1, grid=(nblk,), in_specs=in_specs,
            out_specs=pl.BlockSpec((None, tr, n), lambda i, c: (lead, c[0] * nblk + i, 0))),
        out_shape=jax.ShapeDtypeStruct(shape, F32), input_output_aliases=aliases, compiler_params=_cp(),
    )(*args)


def _join_plan(outs, place, sems, handshake):
    x, y, c = _me()
    sib = (x, y, 1 - c)
    if handshake:
        _handshake([sib])

    def half(k, h):
        o, lead = place[k]
        return _half_rows(outs[o].at[lead], h)

    cps = [_remote(half(k, c), half(k, c), sems, k, sib) for k in range(len(place))]
    for cp in cps:
        cp.start()
    for k in range(len(place)):
        land = half(k, 1 - c)
        _remote(land, land, sems, k, sib).wait_recv()
    for cp in cps:
        cp.wait_send()


def _join_halves(bufs, place, name):
    n_o = len(bufs)
    n_h = len(place)

    def body(*refs):
        _join_plan(refs[n_o:2 * n_o], place, refs[2 * n_o:2 * n_o + 2], False)

    return pl.pallas_call(
        body, name=name, in_specs=[ANY] * n_o, out_specs=[ANY] * n_o,
        out_shape=[jax.ShapeDtypeStruct(b.shape, b.dtype) for b in bufs],
        input_output_aliases={k: k for k in range(n_o)},
        scratch_shapes=[pltpu.SemaphoreType.DMA((n_h,)), pltpu.SemaphoreType.DMA((n_h,))],
    )(*bufs)


def _join_halves_async(bufs, place, name, collective_id):
    refs = [jax.new_ref(b, memory_space=HBM) for b in bufs]
    _on_sequencer(name, collective_id, len(place), 0, lambda sems, _: _join_plan(refs, place, sems, True))
    return [r[...] for r in refs]


def _allreduce_rows(rows):
    n_dev = 8
    n_r = len(rows)
    assert n_r <= 8

    def body(*refs):
        r_refs = refs[:n_r]
        o_ref, slots, send_sems, recv_sems = refs[n_r:]
        x, y, c = _me()
        me = 4 * x + 2 * y + c
        slots[me] = jnp.concatenate([r[...] for r in r_refs] + [jnp.zeros((8 - n_r, D_MODEL), F32)], axis=0)

        def peer(k):
            return (1 - x if k & 4 else x, 1 - y if k & 2 else y, 1 - c if k & 1 else c)

        cps = []
        for k in range(1, n_dev):
            cp = pltpu.make_async_remote_copy(src_ref=slots.at[me], dst_ref=slots.at[me], send_sem=send_sems.at[k - 1],
                                              recv_sem=recv_sems.at[k - 1], device_id=peer(k), device_id_type=MESH)
            cp.start()
            cps.append(cp)
        for k in range(1, n_dev):
            px, py, pc = peer(k)
            land = slots.at[4 * px + 2 * py + pc]
            pltpu.make_async_remote_copy(src_ref=land, dst_ref=land, send_sem=send_sems.at[k - 1],
                                         recv_sem=recv_sems.at[k - 1], device_id=peer(k),
                                         device_id_type=MESH).wait_recv()
        for cp in cps:
            cp.wait_send()
        acc = slots[0]
        for d in range(1, n_dev):
            acc = acc + slots[d]
        o_ref[...] = acc

    vm = pl.BlockSpec(memory_space=pltpu.VMEM)
    return pl.pallas_call(
        body, name="allreduce_rows", in_specs=[vm] * n_r, out_specs=vm,
        out_shape=jax.ShapeDtypeStruct((8, D_MODEL), F32),
        scratch_shapes=[pltpu.VMEM((n_dev, 8, D_MODEL), F32), pltpu.SemaphoreType.DMA((n_dev - 1,)),
                        pltpu.SemaphoreType.DMA((n_dev - 1,))],
    )(*rows)


def _adamw(w, g, m, v, name):
    shape = w.shape
    if len(shape) == 1:
        lead, rows, cols = 1, 1, shape[0]
    else:
        rows, cols = shape[-2:]
        lead = math.prod(shape[:-2])
    args = [a.reshape(lead, rows, cols) for a in (w, g, m, v)]
    tr = rows // 2 if rows % 16 == 0 else rows

    def body(w_ref, g_ref, m_ref, v_ref, d_ref, nm_ref, nv_ref):
        gv = g_ref[...]
        nm = ADAM_B1 * m_ref[...] + (1.0 - ADAM_B1) * gv
        nv = ADAM_B2 * v_ref[...] + (1.0 - ADAM_B2) * jnp.square(gv)
        m_hat = nm / (1.0 - ADAM_B1 ** ADAM_STEP)
        v_hat = nv / (1.0 - ADAM_B2 ** ADAM_STEP)
        d_ref[...] = -ADAM_LR * (m_hat / (jnp.sqrt(v_hat) + ADAM_EPS) + ADAM_WD * w_ref[...])
        nm_ref[...] = nm
        nv_ref[...] = nv

    spec = pl.BlockSpec((None, tr, cols), lambda l, i: (l, i, 0))
    outs = pl.pallas_call(
        body, name=name, grid=(lead, rows // tr), in_specs=[spec] * 4, out_specs=[spec] * 3,
        out_shape=[jax.ShapeDtypeStruct((lead, rows, cols), F32)] * 3, compiler_params=_cp(),
    )(*args)
    return [o.reshape(shape) for o in outs]


def kernel(x, a_w_in, a_sink, a_w_out, b_w_in, b_w_out, norm_mix, norm_ffn, w_gate, w_up, w_down, final_norm, loss_target, m_a_w_in, m_a_sink, m_a_w_out, m_b_w_in, m_b_w_out, m_norm_mix, m_norm_ffn, m_w_gate, m_w_up, m_w_down, m_final_norm, v_a_w_in, v_a_sink, v_a_w_out, v_b_w_in, v_b_w_out, v_norm_mix, v_norm_ffn, v_w_gate, v_w_up, v_w_down, v_final_norm):
    weights = dict(a_w_in=a_w_in, a_sink=a_sink, a_w_out=a_w_out, b_w_in=b_w_in, b_w_out=b_w_out, norm_mix=norm_mix,
                   norm_ffn=norm_ffn, w_gate=w_gate, w_up=w_up, w_down=w_down, final_norm=final_norm)
    mom = dict(a_w_in=m_a_w_in, a_sink=m_a_sink, a_w_out=m_a_w_out, b_w_in=m_b_w_in, b_w_out=m_b_w_out,
               norm_mix=m_norm_mix, norm_ffn=m_norm_ffn, w_gate=m_w_gate, w_up=m_w_up, w_down=m_w_down,
               final_norm=m_final_norm)
    var = dict(a_w_in=v_a_w_in, a_sink=v_a_sink, a_w_out=v_a_w_out, b_w_in=v_b_w_in, b_w_out=v_b_w_out,
               norm_mix=v_norm_mix, norm_ffn=v_norm_ffn, w_gate=v_w_gate, w_up=v_w_up, w_down=v_w_down,
               final_norm=v_final_norm)
    order = ["a_w_in", "a_sink", "a_w_out", "b_w_in", "b_w_out", "norm_mix", "norm_ffn", "w_gate", "w_up", "w_down",
             "final_norm"]

    c_arr = lax.axis_index("c").astype(jnp.int32).reshape(1)
    q_arr = (2 * lax.axis_index("x") + lax.axis_index("y")).astype(jnp.int32).reshape(1)
    shards = [a_w_in, a_w_out, b_w_in, b_w_out, w_gate, w_up, w_down]
    shard_names = ("a_in", "a_out", "b_in", "b_out", "wg", "wu", "wd")
    placed = [_place_shard(s, q_arr, col, f"place_{nm}")
              for s, col, nm in zip(shards, (True, False, True, False, False, False, False), shard_names)]
    (a_in,) = _gather_weights(placed[:1], (True,))
    a_out, b_in, b_out, wg, wu, wd = _gather_weights_async(placed[1:], (False, True, False, False, False, False),
                                                           "gather_weights_late", 1)
    a_out = a_out.reshape(D_MODEL, D_MODEL)
    b_out = b_out.reshape(D_MODEL, D_MODEL)

    gx, grads, vecs = _local_step(x, loss_target, a_in, a_sink[0], a_out, b_in, b_out, norm_mix, norm_ffn, wg, wu, wd,
                                  final_norm)

    rows_out = D_MODEL // N_CHIPS
    partials = [grads["a_in"], grads["b_in"],
                grads["a_out"].reshape(N_CHIPS, rows_out, D_MODEL), grads["b_out"].reshape(N_CHIPS, rows_out, D_MODEL),
                grads["wg"][0], grads["wg"][1], grads["wu"][0], grads["wu"][1], grads["wd"][0], grads["wd"][1]]
    col_fam = (True, True) + (False,) * 8
    names = ("a_in", "b_in", "a_out", "b_out", "wg0", "wg1", "wu0", "wu1", "wd0", "wd1")
    contrib = [None] * len(partials)

    def reduce_group(idx, tag, ids):
        parts = [partials[k] for k in idx]
        cols = tuple(col_fam[k] for k in idx)
        if ids is None:
            theirs = _swap_halves_with_sibling(parts, cols)
        else:
            parts, theirs = _swap_halves_async(parts, cols, f"grad_swap_{tag}", ids[0])
        sums = [_half_add(p, r, c_arr, cf, f"chip_sum_{names[k]}") for p, r, cf, k in zip(parts, theirs, cols, idx)]
        if ids is None:
            out = _scatter_chip_sums(sums, cols)
        else:
            out = _scatter_chip_sums_async(sums, cols, f"grad_scatter_{tag}", ids[1])
        for k, o in zip(idx, out):
            contrib[k] = o

    reduce_group([1, 3, 5, 7, 9], "layer1", (2, 3))
    reduce_group([2, 4, 6, 8], "ffn0", (4, 5))
    reduce_group([0], "a_in", None)
    shapes = [a_w_in.shape, b_w_in.shape, a_w_out.shape, b_w_out.shape, w_gate.shape, w_up.shape, w_gate.shape]
    place = [(0, 0), (1, 0), (2, 0), (3, 0), (4, 0), (4, 1), (5, 0), (5, 1), (6, 0), (6, 1)]
    bufs = [None] * len(shapes)
    for p, nm, (o, lead) in zip(contrib, names, place):
        bufs[o] = _sum_chips(p, c_arr, bufs[o], lead, shapes[o], f"sum_chips_{nm}")
    g_a_in, g_b_in, g_a_out, g_b_out, g_wg, g_wu, g_wdt = _join_halves(bufs, place, "grad_join_sibling")
    g_wd = g_wdt.transpose(0, 2, 1)

    sink_row = jnp.pad(vecs["sink"][0:1], ((0, 0), (0, D_MODEL - LANES)))
    tot = _allreduce_rows([vecs["norm_mix"][0], vecs["norm_mix"][1], vecs["norm_ffn"][0], vecs["norm_ffn"][1],
                           vecs["final"], vecs["loss_cols"], sink_row])
    loss = (0.5 / D_MODEL) * jnp.sum(tot[5])
    gw = dict(a_w_in=g_a_in, a_sink=tot[6:7, :N_HEADS], a_w_out=g_a_out, b_w_in=g_b_in, b_w_out=g_b_out,
              norm_mix=tot[0:2], norm_ffn=tot[2:4], w_gate=g_wg, w_up=g_wu, w_down=g_wd, final_norm=tot[4])

    delta, new_m, new_v = {}, {}, {}
    for n in order:
        delta[n], new_m[n], new_v[n] = _adamw(weights[n], gw[n], mom[n], var[n], f"adamw_{n}")
    return (loss, gx, *[gw[n] for n in order], *[delta[n] for n in order], *[new_m[n] for n in order],
            *[new_v[n] for n in order])
```

```python
import functools
import math

import jax
import jax.numpy as jnp
from jax import lax
from jax.experimental import pallas as pl
from jax.experimental.pallas import tpu as pltpu
from jax.experimental.pallas import tpu_sc as plsc

F32 = jnp.float32
BF16 = jnp.bfloat16

D_MODEL = 1024
HEAD_DIM = 64
N_HEADS = 16
N_KV = 4
QKV_W = 1536
D_FF = 2816
N_CHIPS = 4
FF_SH = D_FF // N_CHIPS
HALF_WINDOW_A = 128
DILATED = ((128, 1), (512, 4), (2048, 16))
ROPE_THETA = 10000.0
RMS_EPS = 1e-6
NEG_INF = -1e30
LANES = 128
ADAM_LR, ADAM_B1, ADAM_B2, ADAM_EPS, ADAM_WD, ADAM_STEP = 0.001, 0.9, 0.999, 1e-08, 0.01, 10
VMEM_LIMIT = 56 * 1024 * 1024
GRAD_TOKENS = 2048
MESH = pl.DeviceIdType.MESH


def _cp(**kw):
    return pltpu.CompilerParams(vmem_limit_bytes=VMEM_LIMIT, **kw)


def _row_tile(t, cap):
    tm = min(cap, t)
    assert t % tm == 0
    return tm


def _rope_tables(seq, dil):
    inv = 1.0 / (ROPE_THETA ** (jnp.arange(0, HEAD_DIM, 2, dtype=F32) / HEAD_DIM))
    ang = jnp.arange(seq, dtype=F32)[:, None] * inv[None, :]
    cos, sin = jnp.cos(ang), jnp.sin(ang)
    cos = jnp.tile(cos, (1, 4))
    sin = jnp.concatenate([-sin, sin, -sin, sin], axis=1)

    def perm(t):
        return t.reshape(seq // dil, dil, LANES).transpose(1, 0, 2).reshape(seq, LANES)

    return perm(cos), perm(sin)


def _swap_halves(t):
    lane = lax.broadcasted_iota(jnp.int32, t.shape, 1)
    return jnp.where((lane % HEAD_DIM) < HEAD_DIM // 2, pltpu.roll(t, LANES - 32, 1), pltpu.roll(t, 32, 1))


def _rope(t, cos, sin):
    return t * cos + _swap_halves(t) * sin


def _rope_t(t, cos, sin):
    return t * cos - _swap_halves(t) * sin


def _to_residue(t, batch, dil):
    if dil == 1:
        return t
    s = t.shape[0] // batch
    return t.reshape(batch, s // dil, dil, t.shape[1]).transpose(0, 2, 1, 3).reshape(t.shape)


def _from_residue(t, batch, dil):
    if dil == 1:
        return t
    s = t.shape[0] // batch
    return t.reshape(batch, dil, s // dil, t.shape[1]).transpose(0, 2, 1, 3).reshape(t.shape)


def _rms_fwd(x, w, name, with_t=False):
    t = x.shape[0]
    tm = _row_tile(t, 512)

    def body(x_ref, w_ref, o_ref, *ot_ref):
        y = _rms_tile(x_ref[...], w_ref[...])
        o_ref[...] = y.astype(BF16)
        if with_t:
            ot_ref[0][...] = y.T.astype(BF16)

    out_specs = [pl.BlockSpec((tm, D_MODEL), lambda i: (i, 0))]
    out_shape = [jax.ShapeDtypeStruct((t, D_MODEL), BF16)]
    if with_t:
        out_specs.append(pl.BlockSpec((D_MODEL, tm), lambda i: (0, i)))
        out_shape.append(jax.ShapeDtypeStruct((D_MODEL, t), BF16))
    outs = pl.pallas_call(
        body, name=name, grid=(t // tm,),
        in_specs=[pl.BlockSpec((tm, D_MODEL), lambda i: (i, 0)), pl.BlockSpec((1, D_MODEL), lambda i: (0, 0))],
        out_specs=out_specs, out_shape=out_shape, compiler_params=_cp(),
    )(x, w)
    return outs if with_t else outs[0]


def _rms_bwd_tile(xv, wv, dy, dres):
    r = lax.rsqrt(jnp.mean(xv * xv, axis=-1, keepdims=True) + RMS_EPS)
    xh = xv * r
    dxh = dy * wv
    dx = dres + r * (dxh - xh * jnp.mean(dxh * xh, axis=-1, keepdims=True))
    return dx, jnp.sum(dy * xh, axis=0, keepdims=True)


def _accumulate(ref, part):
    @pl.when(pl.program_id(0) == 0)
    def _():
        ref[...] = jnp.zeros_like(ref)

    ref[...] += part


def _rms_bwd(x, w, dhs, dres, name, with_t=False):
    t = x.shape[0]
    tm = _row_tile(t, 512)
    n = len(dhs)

    def body(*refs):
        x_ref, w_ref = refs[0], refs[1]
        dh_refs = refs[2:2 + n]
        dres_ref = refs[2 + n]
        dx_ref, dxb_ref = refs[3 + n:5 + n]
        dw_ref = refs[-1]
        dy = dh_refs[0][...].astype(F32)
        for k in range(1, n):
            dy = dy + dh_refs[k][...].astype(F32)
        dx, dw = _rms_bwd_tile(x_ref[...], w_ref[...], dy, dres_ref[...])
        dx_ref[...] = dx
        dxb_ref[...] = dx.astype(BF16)
        if with_t:
            refs[5 + n][...] = dx.T.astype(BF16)
        _accumulate(dw_ref, dw)

    row = pl.BlockSpec((tm, D_MODEL), lambda i: (i, 0))
    vec = pl.BlockSpec((1, D_MODEL), lambda i: (0, 0))
    out_specs = [row, row]
    out_shape = [jax.ShapeDtypeStruct((t, D_MODEL), F32), jax.ShapeDtypeStruct((t, D_MODEL), BF16)]
    if with_t:
        out_specs.append(pl.BlockSpec((D_MODEL, tm), lambda i: (0, i)))
        out_shape.append(jax.ShapeDtypeStruct((D_MODEL, t), BF16))
    return pl.pallas_call(
        body, name=name, grid=(t // tm,),
        in_specs=[row, vec] + [row] * n + [row],
        out_specs=out_specs + [vec], out_shape=out_shape + [jax.ShapeDtypeStruct((1, D_MODEL), F32)],
        compiler_params=_cp(),
    )(x, w, *dhs, dres)


def _final_tile(xv, wv, tv):
    r = lax.rsqrt(jnp.mean(xv * xv, axis=-1, keepdims=True) + RMS_EPS)
    xh = xv * r
    err = xh * wv - tv
    dy = err * (1.0 / D_MODEL)
    dxh = dy * wv
    dx = r * (dxh - xh * jnp.mean(dxh * xh, axis=-1, keepdims=True))
    return dx, jnp.sum(err * err, axis=0, keepdims=True), jnp.sum(dy * xh, axis=0, keepdims=True)


def _qkv_proj(h, w, cos, sin, group, name):
    t = h.shape[0]
    seq = cos.shape[0]
    tm = _row_tile(seq, 1024)
    n_q = N_HEADS * HEAD_DIM // LANES
    n_rope = (N_HEADS + N_KV) * HEAD_DIM // LANES
    scale = 1.0 / math.sqrt(HEAD_DIM)

    def body(h_ref, w_ref, cos_ref, sin_ref, o_ref):
        acc = jnp.dot(h_ref[...], w_ref[...], preferred_element_type=F32)
        cs, sn = cos_ref[...], sin_ref[...]
        csq, snq = cs * scale, sn * scale
        for c in range(QKV_W // LANES):
            blk = acc[:, c * LANES:(c + 1) * LANES]
            if c < n_q:
                blk = _rope(blk, csq, snq)
            elif c < n_rope:
                blk = _rope(blk, cs, sn)
            o_ref[:, c * LANES:(c + 1) * LANES] = blk.astype(BF16)

    tab = pl.BlockSpec((tm, LANES), lambda i: (i % (seq // tm), 0))
    return pl.pallas_call(
        body, name=name, grid=(t // tm,),
        in_specs=[pl.BlockSpec((tm, D_MODEL), lambda i: (i, 0)),
                  pl.BlockSpec((D_MODEL, QKV_W), lambda i: (0, group)), tab, tab],
        out_specs=pl.BlockSpec((tm, QKV_W), lambda i: (i, 0)),
        out_shape=jax.ShapeDtypeStruct((t, QKV_W), BF16), compiler_params=_cp(),
    )(h, w, cos, sin)


def _rms_tile(xv, wv):
    return (xv * lax.rsqrt(jnp.mean(xv * xv, axis=-1, keepdims=True) + RMS_EPS)) * wv


def _mm_res(a, w, res, nw, name):
    t, k = a.shape
    tm = _row_tile(t, 512)

    def body(a_ref, w_ref, r_ref, nw_ref, o_ref, h_ref, ht_ref):
        xv = r_ref[...] + jnp.dot(a_ref[...], w_ref[...], preferred_element_type=F32)
        o_ref[...] = xv
        h = _rms_tile(xv, nw_ref[...])
        h_ref[...] = h.astype(BF16)
        ht_ref[...] = h.T.astype(BF16)

    row = pl.BlockSpec((tm, D_MODEL), lambda i: (i, 0))
    return pl.pallas_call(
        body, name=name, grid=(t // tm,),
        in_specs=[pl.BlockSpec((tm, k), lambda i: (i, 0)),
                  pl.BlockSpec((k, D_MODEL), lambda i: (0, 0), pipeline_mode=pl.Buffered(1)), row,
                  pl.BlockSpec((1, D_MODEL), lambda i: (0, 0))],
        out_specs=[row, row, pl.BlockSpec((D_MODEL, tm), lambda i: (0, i))],
        out_shape=[jax.ShapeDtypeStruct((t, D_MODEL), F32), jax.ShapeDtypeStruct((t, D_MODEL), BF16),
                   jax.ShapeDtypeStruct((D_MODEL, t), BF16)],
        compiler_params=_cp(),
    )(a, w, res, nw)


def _mm_nt(dy, w, group, out_dtype, name):
    t, n = dy.shape
    k = w.shape[0]
    tm = _row_tile(t, 1024)

    def body(dy_ref, w_ref, o_ref):
        o_ref[...] = lax.dot_general(dy_ref[...], w_ref[...], (((1,), (1,)), ((), ())),
                                     preferred_element_type=F32).astype(out_dtype)

    return pl.pallas_call(
        body, name=name, grid=(t // tm,),
        in_specs=[pl.BlockSpec((tm, n), lambda i: (i, 0)), pl.BlockSpec((k, n), lambda i: (0, group))],
        out_specs=pl.BlockSpec((tm, k), lambda i: (i, 0)),
        out_shape=jax.ShapeDtypeStruct((t, k), out_dtype), compiler_params=_cp(),
    )(dy, w)


def _mm_nt_rms(dy, w, x, nw, dres, name):
    t, n = dy.shape
    tm = _row_tile(t, 512)

    def body(dy_ref, w_ref, x_ref, nw_ref, dres_ref, dx_ref, dw_ref):
        dh = lax.dot_general(dy_ref[...], w_ref[...], (((1,), (1,)), ((), ())), preferred_element_type=F32)
        dx, dw = _rms_bwd_tile(x_ref[...], nw_ref[...], dh, dres_ref[...])
        dx_ref[...] = dx
        _accumulate(dw_ref, dw)

    row = pl.BlockSpec((tm, D_MODEL), lambda i: (i, 0))
    vec = pl.BlockSpec((1, D_MODEL), lambda i: (0, 0))
    return pl.pallas_call(
        body, name=name, grid=(t // tm,),
        in_specs=[pl.BlockSpec((tm, n), lambda i: (i, 0)),
                  pl.BlockSpec((D_MODEL, n), lambda i: (0, 0), pipeline_mode=pl.Buffered(1)), row, vec, row],
        out_specs=[row, vec],
        out_shape=[jax.ShapeDtypeStruct((t, D_MODEL), F32), jax.ShapeDtypeStruct((1, D_MODEL), F32)],
        compiler_params=_cp(),
    )(dy, w, x, nw, dres)


def _out_bwd(dx, w, o, name):
    t = dx.shape[0]
    tm = _row_tile(t, 512)

    def body(dx_ref, w_ref, o_ref, et_ref, do_ref, adj_ref):
        do = lax.dot_general(dx_ref[...], w_ref[...], (((1,), (1,)), ((), ())), preferred_element_type=F32)
        do_ref[...] = do.astype(BF16)
        adj_ref[...] = -_dot_split(do * o_ref[...].astype(F32), et_ref[...])

    row = pl.BlockSpec((tm, D_MODEL), lambda i: (i, 0))
    return pl.pallas_call(
        body, name=name, grid=(t // tm,),
        in_specs=[row, pl.BlockSpec((D_MODEL, D_MODEL), lambda i: (0, 0)), row,
                  pl.BlockSpec((D_MODEL, LANES), lambda i: (0, 0))],
        out_specs=[row, pl.BlockSpec((tm, LANES), lambda i: (i, 0))],
        out_shape=[jax.ShapeDtypeStruct((t, D_MODEL), BF16), jax.ShapeDtypeStruct((t, LANES), F32)],
        compiler_params=_cp(),
    )(dx, w, o, _head_expander().T)


def _mm_tn(a, bs, name):
    aq = a.ndim == 3
    bq = bs[0].ndim == 3
    t, ka = a.shape[-2:]
    n = bs[0].shape[-1]
    nq = N_CHIPS if (aq or bq) else 1
    tt = _row_tile(t, GRAD_TOKENS)
    tn = n if n <= 1024 else 768
    assert n % tn == 0
    nb = len(bs)
    steps = t // tt

    def body(*refs):
        a_ref = refs[0]
        b_refs = refs[1:1 + nb]
        o_refs = refs[1 + nb:1 + 2 * nb]
        acc_refs = refs[1 + 2 * nb:]
        s = pl.program_id(2)
        av = a_ref[...]
        for b_ref, o_ref, acc_ref in zip(b_refs, o_refs, acc_refs):
            @pl.when(s == 0)
            def _():
                acc_ref[...] = jnp.zeros_like(acc_ref)

            acc_ref[...] += lax.dot_general(av, b_ref[...], (((0,), (0,)), ((), ())), preferred_element_type=F32)

            @pl.when(s == steps - 1)
            def _():
                o_ref[...] = acc_ref[...].astype(BF16)

    a_spec = (pl.BlockSpec((None, tt, ka), lambda q, j, s: (q, s, 0)) if aq
              else pl.BlockSpec((tt, ka), lambda q, j, s: (s, 0)))
    b_spec = (pl.BlockSpec((None, tt, tn), lambda q, j, s: (q, s, j)) if bq
              else pl.BlockSpec((tt, tn), lambda q, j, s: (s, j)))
    if nq > 1:
        o_spec = pl.BlockSpec((None, ka, tn), lambda q, j, s: (q, 0, j))
        o_shape = jax.ShapeDtypeStruct((nq, ka, n), BF16)
    else:
        o_spec = pl.BlockSpec((ka, tn), lambda q, j, s: (0, j))
        o_shape = jax.ShapeDtypeStruct((ka, n), BF16)
    outs = pl.pallas_call(
        body, name=name, grid=(nq, n // tn, steps),
        in_specs=[a_spec] + [b_spec] * nb, out_specs=[o_spec] * nb, out_shape=[o_shape] * nb,
        scratch_shapes=[pltpu.VMEM((ka, tn), F32)] * nb, compiler_params=_cp(),
    )(a, *bs)
    return outs


def _mm_grad(at, bs, name):
    ka, t = at.shape
    bq = bs[0].ndim == 3
    n = bs[0].shape[-1]
    nq = N_CHIPS if bq else 1
    tt = _row_tile(t, GRAD_TOKENS)
    tn = n if n <= 1024 else 768
    assert n % tn == 0
    nb = len(bs)
    steps = t // tt

    def body(*refs):
        a_ref = refs[0]
        b_refs = refs[1:1 + nb]
        o_refs = refs[1 + nb:1 + 2 * nb]
        acc_refs = refs[1 + 2 * nb:]
        s = pl.program_id(2)
        av = a_ref[...]
        for b_ref, o_ref, acc_ref in zip(b_refs, o_refs, acc_refs):
            @pl.when(s == 0)
            def _():
                acc_ref[...] = jnp.zeros_like(acc_ref)

            acc_ref[...] += jnp.dot(av, b_ref[...], preferred_element_type=F32)

            @pl.when(s == steps - 1)
            def _():
                o_ref[...] = acc_ref[...].astype(BF16)

    a_spec = pl.BlockSpec((ka, tt), lambda q, j, s: (0, s))
    if bq:
        b_spec = pl.BlockSpec((None, tt, tn), lambda q, j, s: (q, s, j))
        o_spec = pl.BlockSpec((None, ka, tn), lambda q, j, s: (q, 0, j))
        o_shape = jax.ShapeDtypeStruct((nq, ka, n), BF16)
    else:
        b_spec = pl.BlockSpec((tt, tn), lambda q, j, s: (s, j))
        o_spec = pl.BlockSpec((ka, tn), lambda q, j, s: (0, j))
        o_shape = jax.ShapeDtypeStruct((ka, n), BF16)
    return pl.pallas_call(
        body, name=name, grid=(nq, n // tn, steps),
        in_specs=[a_spec] + [b_spec] * nb, out_specs=[o_spec] * nb, out_shape=[o_shape] * nb,
        scratch_shapes=[pltpu.VMEM((ka, tn), F32)] * nb, compiler_params=_cp(),
    )(at, *bs)


def _sigmoid(x):
    return 1.0 / (1.0 + jnp.exp(-x))


def _ffn_up(h, wg, wu, layer, name):
    t = h.shape[0]
    tm = _row_tile(t, 1024)

    def body(h_ref, wg_ref, wu_ref, a_ref, dg_ref, du_ref):
        hv = h_ref[...]
        g = jnp.dot(hv, wg_ref[...], preferred_element_type=F32)
        u = jnp.dot(hv, wu_ref[...], preferred_element_type=F32)
        sg = _sigmoid(g)
        silu = g * sg
        a_ref[...] = (silu * u).astype(BF16)
        dg_ref[...] = (sg * (1.0 + g * (1.0 - sg)) * u).astype(BF16)
        du_ref[...] = silu.astype(BF16)

    wspec = pl.BlockSpec((None, None, D_MODEL, FF_SH), lambda q, i: (q, layer, 0, 0))
    ospec = pl.BlockSpec((None, tm, FF_SH), lambda q, i: (q, i, 0))
    oshape = jax.ShapeDtypeStruct((N_CHIPS, t, FF_SH), BF16)
    return pl.pallas_call(
        body, name=name, grid=(N_CHIPS, t // tm),
        in_specs=[pl.BlockSpec((tm, D_MODEL), lambda q, i: (i, 0)), wspec, wspec],
        out_specs=[ospec] * 3, out_shape=[oshape] * 3, compiler_params=_cp(),
    )(h, wg, wu)


def _ffn_down(a, wd, res, layer, name, norm_w=None, head=None):
    t = a.shape[1]
    tm = _row_tile(t, 512)
    resident = pl.BlockSpec((N_CHIPS, None, FF_SH, D_MODEL), lambda i: (0, layer, 0, 0), pipeline_mode=pl.Buffered(1))
    row = pl.BlockSpec((tm, D_MODEL), lambda i: (i, 0))
    vec = pl.BlockSpec((1, D_MODEL), lambda i: (0, 0))

    def hidden(a_ref, w_ref, r_ref):
        acc = r_ref[...]
        for q in range(N_CHIPS):
            acc = acc + jnp.dot(a_ref[q], w_ref[q], preferred_element_type=F32)
        return acc

    if head is None:
        def body(a_ref, w_ref, r_ref, nw_ref, o_ref, h_ref):
            xv = hidden(a_ref, w_ref, r_ref)
            o_ref[...] = xv
            h_ref[...] = _rms_tile(xv, nw_ref[...]).astype(BF16)

        return pl.pallas_call(
            body, name=name, grid=(t // tm,),
            in_specs=[pl.BlockSpec((N_CHIPS, tm, FF_SH), lambda i: (0, i, 0)), resident, row, vec],
            out_specs=[row, row],
            out_shape=[jax.ShapeDtypeStruct((t, D_MODEL), F32), jax.ShapeDtypeStruct((t, D_MODEL), BF16)],
            compiler_params=_cp(),
        )(a, wd, res, norm_w)

    def body(a_ref, w_ref, r_ref, nw_ref, t_ref, dx_ref, dxb_ref, dxt_ref, l_ref, dw_ref):
        dx, sq, dw = _final_tile(hidden(a_ref, w_ref, r_ref), nw_ref[...], t_ref[...])
        dx_ref[...] = dx
        dxb_ref[...] = dx.astype(BF16)
        dxt_ref[...] = dx.T.astype(BF16)
        _accumulate(l_ref, sq)
        _accumulate(dw_ref, dw)

    return pl.pallas_call(
        body, name=name, grid=(t // tm,),
        in_specs=[pl.BlockSpec((N_CHIPS, tm, FF_SH), lambda i: (0, i, 0)), resident, row, vec, row],
        out_specs=[row, row, pl.BlockSpec((D_MODEL, tm), lambda i: (0, i)), vec, vec],
        out_shape=[jax.ShapeDtypeStruct((t, D_MODEL), F32), jax.ShapeDtypeStruct((t, D_MODEL), BF16),
                   jax.ShapeDtypeStruct((D_MODEL, t), BF16),
                   jax.ShapeDtypeStruct((1, D_MODEL), F32), jax.ShapeDtypeStruct((1, D_MODEL), F32)],
        compiler_params=_cp(),
    )(a, wd, res, *head)


def _ffn_down_bwd(dx, wd, fg, fu, layer, name):
    t = dx.shape[0]
    tm = _row_tile(t, 512)

    def body(dx_ref, w_ref, fg_ref, fu_ref, dg_ref, du_ref):
        dxv = dx_ref[...]
        for q in range(N_CHIPS):
            da = lax.dot_general(dxv, w_ref[q], (((1,), (1,)), ((), ())), preferred_element_type=F32)
            dg_ref[q] = (da * fg_ref[q].astype(F32)).astype(BF16)
            du_ref[q] = (da * fu_ref[q].astype(F32)).astype(BF16)

    aspec = pl.BlockSpec((N_CHIPS, tm, FF_SH), lambda i: (0, i, 0))
    oshape = jax.ShapeDtypeStruct((N_CHIPS, t, FF_SH), BF16)
    return pl.pallas_call(
        body, name=name, grid=(t // tm,),
        in_specs=[pl.BlockSpec((tm, D_MODEL), lambda i: (i, 0)),
                  pl.BlockSpec((N_CHIPS, None, FF_SH, D_MODEL), lambda i: (0, layer, 0, 0)), aspec, aspec],
        out_specs=[aspec] * 2, out_shape=[oshape] * 2, compiler_params=_cp(),
    )(dx, wd, fg, fu)


def _ffn_up_bwd(dg, du, wg, wu, layer, x, nw, dres, name):
    t = dg.shape[1]
    tm = _row_tile(t, 512)
    nt = (((1,), (1,)), ((), ()))

    def body(dg_ref, du_ref, wg_ref, wu_ref, x_ref, nw_ref, dres_ref, dx_ref, dxb_ref, dw_ref):
        acc = jnp.zeros((tm, D_MODEL), F32)
        for q in range(N_CHIPS):
            acc = acc + lax.dot_general(dg_ref[q], wg_ref[q], nt, preferred_element_type=F32)
            acc = acc + lax.dot_general(du_ref[q], wu_ref[q], nt, preferred_element_type=F32)
        dx, dw = _rms_bwd_tile(x_ref[...], nw_ref[...], acc, dres_ref[...])
        dx_ref[...] = dx
        dxb_ref[...] = dx.astype(BF16)
        _accumulate(dw_ref, dw)

    aspec = pl.BlockSpec((N_CHIPS, tm, FF_SH), lambda i: (0, i, 0))
    wspec = pl.BlockSpec((N_CHIPS, None, D_MODEL, FF_SH), lambda i: (0, layer, 0, 0), pipeline_mode=pl.Buffered(1))
    row = pl.BlockSpec((tm, D_MODEL), lambda i: (i, 0))
    vec = pl.BlockSpec((1, D_MODEL), lambda i: (0, 0))
    return pl.pallas_call(
        body, name=name, grid=(t // tm,),
        in_specs=[aspec, aspec, wspec, wspec, row, vec, row],
        out_specs=[row, row, vec],
        out_shape=[jax.ShapeDtypeStruct((t, D_MODEL), F32), jax.ShapeDtypeStruct((t, D_MODEL), BF16),
                   jax.ShapeDtypeStruct((1, D_MODEL), F32)],
        compiler_params=_cp(),
    )(dg, du, wg, wu, x, nw, dres)


def _attn_geometry(length, half_window):
    qb = min(LANES, length)
    kw = min(qb + 2 * half_window, length)
    return qb, kw, length // qb


def _dup_kv(src_ref, dst_ref, s, length):
    ch = min(length, 256)
    lo = lax.broadcasted_iota(jnp.int32, (ch, LANES), 1) < HEAD_DIM

    def chunk(c, carry):
        r0 = pl.multiple_of(c * ch, ch)
        for j in range(N_KV // 2):
            tile = src_ref[s, pl.ds(r0, ch), j * LANES:(j + 1) * LANES].astype(F32)
            rolled = pltpu.roll(tile, HEAD_DIM, 1)
            dst_ref[2 * j, pl.ds(r0, ch), :] = jnp.where(lo, tile, rolled).astype(BF16)
            dst_ref[2 * j + 1, pl.ds(r0, ch), :] = jnp.where(lo, rolled, tile).astype(BF16)
        return carry

    lax.fori_loop(0, length // ch, chunk, 0)


def _stack_heads(ref, s, q0, qb, g):
    lo = lax.broadcasted_iota(jnp.int32, (qb, LANES), 1) < HEAD_DIM
    parts = []
    for a in range(4):
        col = (2 * g + a // 2) * LANES
        tile = ref[s, pl.ds(q0, qb), col:col + LANES]
        keep = lo if a % 2 == 0 else jnp.logical_not(lo)
        parts.append(jnp.where(keep, tile, jnp.zeros_like(tile)))
    return jnp.concatenate(parts, axis=0)


def _unstack_pair_t(stacked_t, qb, pair):
    both = jnp.concatenate([stacked_t[:, (2 * pair) * qb:(2 * pair + 1) * qb],
                            stacked_t[:, (2 * pair + 1) * qb:(2 * pair + 2) * qb]], axis=0)
    return both.T


def _band_mask_t(q0, k0, qb, kw, half_window):
    key = lax.broadcasted_iota(jnp.int32, (kw, 4 * qb), 0)
    qry = lax.broadcasted_iota(jnp.int32, (kw, 4 * qb), 1) & (qb - 1)
    return jnp.abs((q0 + qry) - (k0 + key)) <= half_window


def _block_origin(i, qb, kw, half_window, length):
    if isinstance(i, int):
        return i * qb, min(max(i * qb - half_window, 0), length - kw)
    return (pl.multiple_of(i * qb, qb),
            pl.multiple_of(jnp.clip(i * qb - half_window, 0, length - kw), HEAD_DIM))


def _head_row(vals, qb):
    return jnp.concatenate([jnp.broadcast_to(v, (1, qb)).astype(F32) for v in vals], axis=1)


def _attn_fwd(qkv, sink, n_seq, length, half_window, seq_blk, out_dtype, name):
    qb, kw, nblk = _attn_geometry(length, half_window)
    with_sink = sink is not None
    nt = (((1,), (1,)), ((), ()))
    tn = (((0,), (0,)), ((), ()))
    qkv3 = qkv.reshape(n_seq, length, QKV_W)

    def body(*refs):
        refs = list(refs)
        sink_ref = refs.pop(0) if with_sink else None
        q_ref, k_ref, v_ref, o_ref, lse_ref = refs[:5]
        kx_ref, vx_ref = refs[-2:]
        head_row = lax.broadcasted_iota(jnp.int32, (N_HEADS, qb), 0)
        for s in range(seq_blk):
            _dup_kv(k_ref, kx_ref, s, length)
            _dup_kv(v_ref, vx_ref, s, length)

            def block(i, carry):
                q0, k0 = _block_origin(i, qb, kw, half_window, length)
                valid = _band_mask_t(q0, k0, qb, kw, half_window)
                lse_tile = jnp.zeros((N_HEADS, qb), F32)
                groups = range(N_KV)
                sts = [lax.dot_general(kx_ref[g, pl.ds(k0, kw), :], _stack_heads(q_ref, s, q0, qb, g), nt,
                                       preferred_element_type=F32) for g in groups]
                sts = [jnp.where(valid, st, NEG_INF) for st in sts]
                ms = [jnp.max(st, axis=0, keepdims=True) for st in sts]
                if with_sink:
                    sks = [_head_row([sink_ref[4 * g + a] for a in range(4)], qb) for g in groups]
                    ms = [jnp.maximum(m, sk) for m, sk in zip(ms, sks)]
                es = [jnp.exp(st - m) for st, m in zip(sts, ms)]
                dens = [jnp.sum(e, axis=0, keepdims=True) for e in es]
                if with_sink:
                    dens = [den + jnp.exp(sk - m) for den, sk, m in zip(dens, sks, ms)]
                ots = [lax.dot_general(vx_ref[g, pl.ds(k0, kw), 0:HEAD_DIM], es[g].astype(BF16), tn,
                                       preferred_element_type=F32) / dens[g] for g in groups]
                for g in groups:
                    for pair in range(2):
                        col = (2 * g + pair) * LANES
                        o_ref[s, pl.ds(q0, qb), col:col + LANES] = _unstack_pair_t(ots[g], qb, pair).astype(out_dtype)
                    lse = ms[g] + jnp.log(dens[g])
                    for a in range(4):
                        lse_tile = jnp.where(head_row == 4 * g + a, lse[:, a * qb:(a + 1) * qb], lse_tile)
                lse_ref[s, :, pl.ds(q0, qb)] = lse_tile
                return carry

            if nblk == 1:
                block(0, 0)
            else:
                lax.fori_loop(0, nblk, block, 0)

    in_specs = [pl.BlockSpec((seq_blk, length, N_HEADS * HEAD_DIM), lambda n: (n, 0, 0)),
                pl.BlockSpec((seq_blk, length, N_KV * HEAD_DIM), lambda n: (n, 0, 4)),
                pl.BlockSpec((seq_blk, length, N_KV * HEAD_DIM), lambda n: (n, 0, 5))]
    args = [qkv3, qkv3, qkv3]
    if with_sink:
        in_specs.insert(0, pl.BlockSpec(memory_space=pltpu.SMEM))
        args.insert(0, sink)
    out_specs = [pl.BlockSpec((seq_blk, length, D_MODEL), lambda n: (n, 0, 0)),
                 pl.BlockSpec((seq_blk, N_HEADS, length), lambda n: (n, 0, 0))]
    out_shape = [jax.ShapeDtypeStruct((n_seq, length, D_MODEL), out_dtype),
                 jax.ShapeDtypeStruct((n_seq, N_HEADS, length), F32)]
    o, lse = pl.pallas_call(
        body, name=name, grid=(n_seq // seq_blk,), in_specs=in_specs, out_specs=out_specs, out_shape=out_shape,
        scratch_shapes=[pltpu.VMEM((N_KV, length, LANES), BF16), pltpu.VMEM((N_KV, length, LANES), BF16)],
        compiler_params=_cp(),
    )(*args)
    return o.reshape(n_seq * length, D_MODEL), lse


def _attn_bwd(qkv, do, adj, lse, sink, cos, sin, n_seq, length, half_window, seq_blk, dil, name):
    qb, kw, nblk = _attn_geometry(length, half_window)
    scale = 1.0 / math.sqrt(HEAD_DIM)
    with_sink = sink is not None
    nt = (((1,), (1,)), ((), ()))
    tn = (((0,), (0,)), ((), ()))
    qkv3 = qkv.reshape(n_seq, length, QKV_W)
    do3 = do.reshape(n_seq, length, D_MODEL)
    tabs = [t.reshape(dil, length, LANES) for t in (cos, sin)]
    tab_blocks = dil // seq_blk if dil >= seq_blk else 1

    def body(*refs):
        refs = list(refs)
        sink_ref = refs.pop(0) if with_sink else None
        q_ref, k_ref, v_ref, do_ref, aux_ref, lse_ref, cos_ref, sin_ref, dqkv_ref = refs[:9]
        ds_ref = refs[9] if with_sink else None
        kx_ref, vx_ref, dkx_ref, dvx_ref = refs[-4:]
        lane = lax.broadcasted_iota(jnp.int32, (1, LANES), 1)
        if with_sink:
            @pl.when(pl.program_id(0) == 0)
            def _():
                ds_ref[...] = jnp.zeros_like(ds_ref)

        for s in range(seq_blk):
            ts = s % dil
            _dup_kv(k_ref, kx_ref, s, length)
            _dup_kv(v_ref, vx_ref, s, length)
            dkx_ref[...] = jnp.zeros_like(dkx_ref)
            dvx_ref[...] = jnp.zeros_like(dvx_ref)

            def block(i, dsink):
                q0, k0 = _block_origin(i, qb, kw, half_window, length)
                valid = _band_mask_t(q0, k0, qb, kw, half_window)
                cs = cos_ref[ts, pl.ds(q0, qb), :] * scale
                sn = sin_ref[ts, pl.ds(q0, qb), :] * scale
                adj_tile = aux_ref[s, :, pl.ds(q0, qb)]
                lse_tile = lse_ref[s, :, pl.ds(q0, qb)]
                groups = range(N_KV)
                qss = [_stack_heads(q_ref, s, q0, qb, g) for g in groups]
                doss = [_stack_heads(do_ref, s, q0, qb, g) for g in groups]
                kxs = [kx_ref[g, pl.ds(k0, kw), :] for g in groups]
                sts = [lax.dot_general(kxs[g], qss[g], nt, preferred_element_type=F32) for g in groups]
                dpts = [lax.dot_general(vx_ref[g, pl.ds(k0, kw), :], doss[g], nt, preferred_element_type=F32)
                        for g in groups]
                lses = [_head_row([lse_tile[4 * g + a:4 * g + a + 1, :] for a in range(4)], qb) for g in groups]
                shifts = [_head_row([adj_tile[4 * g + a:4 * g + a + 1, :] for a in range(4)], qb) for g in groups]
                pts = [jnp.exp(jnp.where(valid, sts[g], NEG_INF) - lses[g]) for g in groups]
                dsbs = [(pts[g] * (dpts[g] + shifts[g])).astype(BF16) for g in groups]
                pbs = [pt.astype(BF16) for pt in pts]
                if with_sink:
                    for g in groups:
                        sk = _head_row([sink_ref[4 * g + a] for a in range(4)], qb)
                        dsk = jnp.exp(sk - lses[g]) * shifts[g]
                        for a in range(4):
                            tot = jnp.sum(dsk[:, a * qb:(a + 1) * qb], axis=1, keepdims=True)
                            dsink = dsink + jnp.where(lane == 4 * g + a, tot, 0.0)
                dqts = [lax.dot_general(kx_ref[g, pl.ds(k0, kw), 0:HEAD_DIM], dsbs[g], tn, preferred_element_type=F32)
                        for g in groups]
                for g in groups:
                    for pair in range(2):
                        col = (2 * g + pair) * LANES
                        tile = _rope_t(_unstack_pair_t(dqts[g], qb, pair), cs, sn)
                        dqkv_ref[s, pl.ds(q0, qb), col:col + LANES] = tile.astype(BF16)
                for g in groups:
                    dkx_ref[g, pl.ds(k0, kw), :] += jnp.dot(dsbs[g], qss[g], preferred_element_type=F32)
                    dvx_ref[g, pl.ds(k0, kw), :] += jnp.dot(pbs[g], doss[g], preferred_element_type=F32)
                return dsink

            if nblk == 1:
                dsink = block(0, jnp.zeros((1, LANES), F32))
            else:
                dsink = lax.fori_loop(0, nblk, block, jnp.zeros((1, LANES), F32))
            if with_sink:
                ds_ref[0:1, :] += dsink

            ch = min(length, 256)
            lo_c = lax.broadcasted_iota(jnp.int32, (ch, LANES), 1) < HEAD_DIM

            def fin(c, carry):
                r0 = pl.multiple_of(c * ch, ch)
                cs = cos_ref[ts, pl.ds(r0, ch), :]
                sn = sin_ref[ts, pl.ds(r0, ch), :]
                for j in range(N_KV // 2):
                    both = []
                    for acc_ref in (dkx_ref, dvx_ref):
                        t0 = acc_ref[2 * j, pl.ds(r0, ch), :]
                        t1 = acc_ref[2 * j + 1, pl.ds(r0, ch), :]
                        t0 = t0 + pltpu.roll(t0, HEAD_DIM, 1)
                        t1 = t1 + pltpu.roll(t1, HEAD_DIM, 1)
                        both.append(jnp.where(lo_c, t0, t1))
                    kcol = N_HEADS * HEAD_DIM + j * LANES
                    vcol = (N_HEADS + N_KV) * HEAD_DIM + j * LANES
                    dqkv_ref[s, pl.ds(r0, ch), kcol:kcol + LANES] = _rope_t(both[0], cs, sn).astype(BF16)
                    dqkv_ref[s, pl.ds(r0, ch), vcol:vcol + LANES] = both[1].astype(BF16)
                return carry

            lax.fori_loop(0, length // ch, fin, 0)

    seq_map = lambda n: (n, 0, 0)
    tab_map = (lambda n: (n % tab_blocks, 0, 0)) if dil >= seq_blk else (lambda n: (0, 0, 0))
    tab_rows = min(seq_blk, dil)
    in_specs = [pl.BlockSpec((seq_blk, length, N_HEADS * HEAD_DIM), seq_map),
                pl.BlockSpec((seq_blk, length, N_KV * HEAD_DIM), lambda n: (n, 0, 4)),
                pl.BlockSpec((seq_blk, length, N_KV * HEAD_DIM), lambda n: (n, 0, 5)),
                pl.BlockSpec((seq_blk, length, D_MODEL), seq_map),
                pl.BlockSpec((seq_blk, N_HEADS, length), seq_map),
                pl.BlockSpec((seq_blk, N_HEADS, length), seq_map),
                pl.BlockSpec((tab_rows, length, LANES), tab_map),
                pl.BlockSpec((tab_rows, length, LANES), tab_map)]
    args = [qkv3, qkv3, qkv3, do3, adj, lse] + tabs
    if with_sink:
        in_specs.insert(0, pl.BlockSpec(memory_space=pltpu.SMEM))
        args.insert(0, sink)
    out_specs = [pl.BlockSpec((seq_blk, length, QKV_W), seq_map)]
    out_shape = [jax.ShapeDtypeStruct((n_seq, length, QKV_W), BF16)]
    if with_sink:
        out_specs.append(pl.BlockSpec((8, LANES), lambda n: (0, 0)))
        out_shape.append(jax.ShapeDtypeStruct((8, LANES), F32))
    outs = pl.pallas_call(
        body, name=name, grid=(n_seq // seq_blk,), in_specs=in_specs, out_specs=out_specs, out_shape=out_shape,
        scratch_shapes=[pltpu.VMEM((N_KV, length, LANES), BF16), pltpu.VMEM((N_KV, length, LANES), BF16),
                        pltpu.VMEM((N_KV, length, LANES), F32), pltpu.VMEM((N_KV, length, LANES), F32)],
        compiler_params=_cp(),
    )(*args)
    dqkv = outs[0].reshape(n_seq * length, QKV_W)
    return (dqkv, outs[1]) if with_sink else (dqkv, None)


def _head_expander():
    h = jnp.arange(LANES)[:, None]
    l = jnp.arange(D_MODEL)[None, :]
    return (l // HEAD_DIM == h).astype(BF16)


def _dot_split(a, e):
    hi = a.astype(BF16)
    lo = (a - hi.astype(F32)).astype(BF16)
    return jnp.dot(hi, e, preferred_element_type=F32) + jnp.dot(lo, e, preferred_element_type=F32)


def _mix_weights(lses):
    m = jnp.maximum(jnp.maximum(lses[0], lses[1]), lses[2])
    es = [jnp.exp(v - m) for v in lses]
    tot = es[0] + es[1] + es[2]
    return [e / tot for e in es]


def _mix_fwd(os_, lses, name):
    t = os_[0].shape[0]
    tm = _row_tile(t, 512)

    def body(o0, o1, o2, l0, l1, l2, e_ref, out_ref):
        wts = _mix_weights([l0[...], l1[...], l2[...]])
        acc = jnp.zeros((tm, D_MODEL), F32)
        for w, o_ref in zip(wts, (o0, o1, o2)):
            acc = acc + _dot_split(w, e_ref[...]) * o_ref[...]
        out_ref[...] = acc.astype(BF16)

    row = pl.BlockSpec((tm, D_MODEL), lambda i: (i, 0))
    lrow = pl.BlockSpec((tm, LANES), lambda i: (i, 0))
    return pl.pallas_call(
        body, name=name, grid=(t // tm,),
        in_specs=[row] * 3 + [lrow] * 3 + [pl.BlockSpec((LANES, D_MODEL), lambda i: (0, 0))],
        out_specs=row, out_shape=jax.ShapeDtypeStruct((t, D_MODEL), BF16), compiler_params=_cp(),
    )(*os_, *lses, _head_expander())


def _mix_bwd(dmix, os_, lses, name):
    t = dmix.shape[0]
    tm = _row_tile(t, 512)

    def body(d_ref, o0, o1, o2, l0, l1, l2, e_ref, et_ref, do0, do1, do2, a0, a1, a2):
        wts = _mix_weights([l0[...], l1[...], l2[...]])
        dv = d_ref[...].astype(F32)
        cs = [_dot_split(dv * o_ref[...], et_ref[...]) for o_ref in (o0, o1, o2)]
        mean_c = wts[0] * cs[0] + wts[1] * cs[1] + wts[2] * cs[2]
        for w, c, do_ref, a_ref in zip(wts, cs, (do0, do1, do2), (a0, a1, a2)):
            do_ref[...] = (_dot_split(w, e_ref[...]) * dv).astype(BF16)
            a_ref[...] = w * (c - mean_c) - w * c

    row = pl.BlockSpec((tm, D_MODEL), lambda i: (i, 0))
    lrow = pl.BlockSpec((tm, LANES), lambda i: (i, 0))
    e = _head_expander()
    return pl.pallas_call(
        body, name=name, grid=(t // tm,),
        in_specs=[row] * 4 + [lrow] * 3 + [pl.BlockSpec((LANES, D_MODEL), lambda i: (0, 0)),
                                            pl.BlockSpec((D_MODEL, LANES), lambda i: (0, 0))],
        out_specs=[row] * 3 + [lrow] * 3,
        out_shape=[jax.ShapeDtypeStruct((t, D_MODEL), BF16)] * 3 + [jax.ShapeDtypeStruct((t, LANES), F32)] * 3,
        compiler_params=_cp(),
    )(dmix, *os_, *lses, e, e.T)


def _stats_to_tokens(stat, batch, dil):
    n_seq, _, length = stat.shape
    t = stat.transpose(0, 2, 1).reshape(n_seq * length, N_HEADS)
    return _from_residue(jnp.pad(t, ((0, 0), (0, LANES - N_HEADS))), batch, dil)


def _stats_from_tokens(stat, batch, dil, n_seq, length):
    t = _to_residue(stat[:, :N_HEADS], batch, dil)
    return t.reshape(n_seq, length, N_HEADS).transpose(0, 2, 1)


def _group_geometry(batch, seq, dil, window):
    length = seq // dil
    n_seq = batch * dil
    seq_blk = max(1, min(dil, 1024 // length))
    return n_seq, length, (window // 2) // dil, seq_blk


def _local_step(x, target, a_in, a_sink, a_out, b_in, b_out, norm_mix, norm_ffn, wg, wu, wd, final_norm):
    batch, seq, _ = x.shape
    t = batch * seq
    x0 = x.reshape(t, D_MODEL)
    tgt = target.reshape(t, D_MODEL)
    tabs = {d: _rope_tables(seq, d) for _, d in DILATED}
    nm = [norm_mix[i:i + 1] for i in range(2)]
    nf = [norm_ffn[i:i + 1] for i in range(2)]

    h0, h0t = _rms_fwd(x0, nm[0], "rms_mix0", True)
    qkv0 = _qkv_proj(h0, a_in, *tabs[1], 0, "qkv0")
    o0, lse0 = _attn_fwd(qkv0, a_sink, batch, seq, HALF_WINDOW_A, 1, BF16, "attn0")
    x1, hf0, hf0t = _mm_res(o0, a_out, x0, nf[0], "out0")
    act0, g0, u0 = _ffn_up(hf0, wg, wu, 0, "ffn_up0")
    x2, h1 = _ffn_down(act0, wd, x1, 0, "ffn_down0", norm_w=nm[1])

    geo = [_group_geometry(batch, seq, d, w) for w, d in DILATED]
    h1g, qkv1, o1, lse1, lse1r = [], [], [], [], []
    for gi, (_, d) in enumerate(DILATED):
        n_seq, length, hw, sb = geo[gi]
        hp = _to_residue(h1, batch, d)
        pj = _qkv_proj(hp, b_in, *tabs[d], gi, f"qkv1_{gi}")
        o, lse = _attn_fwd(pj, None, n_seq, length, hw, sb, BF16, f"attn1_{gi}")
        h1g.append(hp)
        qkv1.append(pj)
        o1.append(_from_residue(o, batch, d))
        lse1r.append(lse)
        lse1.append(_stats_to_tokens(lse, batch, d))
    omix = _mix_fwd(o1, lse1, "mix")
    x3, hf1, hf1t = _mm_res(omix, b_out, x2, nf[1], "out1")
    act1, g1, u1 = _ffn_up(hf1, wg, wu, 1, "ffn_up1")
    dx4, dx4b, dx4t, loss_cols, d_final = _ffn_down(act1, wd, x3, 1, "ffn_down1_loss",
                                                     head=(final_norm.reshape(1, D_MODEL), tgt))

    def ffn_bwd(dxo, dxob, dxot, x_mid, hft, g, u, act, layer):
        dg, du = _ffn_down_bwd(dxob, wd, g, u, layer, f"ffn_down_bwd{layer}")
        (d_wdt,) = _mm_grad(dxot, [act], f"grad_wd{layer}")
        dxm, dxmb, d_nf = _ffn_up_bwd(dg, du, wg, wu, layer, x_mid, nf[layer], dxo, f"ffn_up_bwd{layer}")
        d_wg, d_wu = _mm_grad(hft, [dg, du], f"grad_wgu{layer}")
        return dxm, dxmb, d_nf, d_wg, d_wu, d_wdt

    dx3, dx3b, d_nf1, d_wg1, d_wu1, d_wd1 = ffn_bwd(dx4, dx4b, dx4t, x3, hf1t, g1, u1, act1, 1)

    dmix = _mm_nt(dx3b, b_out, 0, BF16, "out1_bwd")
    (d_b_out,) = _mm_tn(omix, [dx3b], "grad_b_out")
    mb = _mix_bwd(dmix, o1, lse1, "mix_bwd")
    dh1, d_b_in = [], []
    for gi, (_, d) in enumerate(DILATED):
        n_seq, length, hw, sb = geo[gi]
        dog = _to_residue(mb[gi], batch, d)
        adj = _stats_from_tokens(mb[3 + gi], batch, d, n_seq, length)
        dpj, _ = _attn_bwd(qkv1[gi], dog, adj, lse1r[gi], None, *tabs[d], n_seq, length, hw, sb, d, f"attn1_bwd{gi}")
        (dw,) = _mm_tn(h1g[gi], [dpj], f"grad_b_in{gi}")
        d_b_in.append(dw)
        dh1.append(_from_residue(_mm_nt(dpj, b_in, gi, BF16, f"qkv1_bwd{gi}"), batch, d))
    dx2, dx2b, dx2t, d_nm1 = _rms_bwd(x2, nm[1], dh1, dx3, "rms_mix_bwd1", True)

    dx1, dx1b, d_nf0, d_wg0, d_wu0, d_wd0 = ffn_bwd(dx2, dx2b, dx2t, x1, hf0t, g0, u0, act0, 0)

    do0, adj0 = _out_bwd(dx1b, a_out, o0, "out0_bwd")
    (d_a_out,) = _mm_tn(o0, [dx1b], "grad_a_out")
    adj0 = _stats_from_tokens(adj0, batch, 1, batch, seq)
    dqkv0, d_sink = _attn_bwd(qkv0, do0, adj0, lse0, a_sink, *tabs[1], batch, seq, HALF_WINDOW_A, 1, 1, "attn0_bwd")
    (d_a_in,) = _mm_grad(h0t, [dqkv0], "grad_a_in")
    gx, d_nm0 = _mm_nt_rms(dqkv0, a_in, x0, nm[0], dx1, "qkv0_bwd")

    grads = dict(a_in=d_a_in, a_out=d_a_out, b_in=jnp.concatenate(d_b_in, axis=1), b_out=d_b_out,
                 wg=(d_wg0, d_wg1), wu=(d_wu0, d_wu1), wd=(d_wd0, d_wd1))
    vecs = dict(norm_mix=(d_nm0, d_nm1), norm_ffn=(d_nf0, d_nf1), final=d_final, loss_cols=loss_cols, sink=d_sink)
    return gx.reshape(x.shape), grads, vecs


ANY = pl.BlockSpec(memory_space=pl.ANY)
HBM = pltpu.MemorySpace.HBM


def _me():
    return lax.axis_index("x"), lax.axis_index("y"), lax.axis_index("c")


def _chip_peer(x, y, j):
    px = 1 - x if j & 2 else x
    py = 1 - y if j & 1 else y
    return px, py, 2 * px + py


def _remote(src, dst, sems, k, dev):
    return pltpu.make_async_remote_copy(src_ref=src, dst_ref=dst, send_sem=sems[0].at[k], recv_sem=sems[1].at[k],
                                        device_id=dev, device_id_type=MESH)


def _col_window(ref, q, width):
    return ref.at[:, pl.ds(pl.multiple_of(q * width, LANES), width)]


def _half0(ref, h):
    n = ref.shape[0] // 2
    return ref.at[pl.ds(h * n, n)]


def _half1(ref, h):
    n = ref.shape[1] // 2
    return ref.at[:, pl.ds(h * n, n)]


def _half_rows(ref, h):
    n = ref.shape[-2] // 2
    if len(ref.shape) == 2:
        return ref.at[pl.ds(h * n, n)]
    return ref.at[:, pl.ds(h * n, n)]


def _place_shard(w, q_arr, col, name):
    lead, rows, cols = w.shape

    def body(q_ref, w_ref, o_ref):
        o_ref[...] = w_ref[...].astype(BF16)

    if col:
        assert lead == 1
        out_spec = pl.BlockSpec((rows, cols), lambda l, q: (0, q[0]))
        out_shape = jax.ShapeDtypeStruct((rows, N_CHIPS * cols), BF16)
    else:
        out_spec = pl.BlockSpec((None, None, rows, cols), lambda l, q: (q[0], l, 0, 0))
        out_shape = jax.ShapeDtypeStruct((N_CHIPS, lead, rows, cols), BF16)
    return pl.pallas_call(
        body, name=name,
        grid_spec=pltpu.PrefetchScalarGridSpec(
            num_scalar_prefetch=1, grid=(lead,),
            in_specs=[pl.BlockSpec((None, rows, cols), lambda l, q: (l, 0, 0))], out_specs=out_spec),
        out_shape=out_shape, compiler_params=_cp(),
    )(q_arr, w)


def _handshake(peers):
    barrier = pltpu.get_barrier_semaphore()
    for p in peers:
        pl.semaphore_signal(barrier, inc=1, device_id=p, device_id_type=MESH)
    pl.semaphore_wait(barrier, len(peers))


def _on_sequencer(name, collective_id, n_sem, n_local, body):
    @pl.kernel(mesh=plsc.ScalarSubcoreMesh(axis_name="seq", num_cores=1), name=name,
               scratch_types=(pltpu.SemaphoreType.DMA((n_sem,)), pltpu.SemaphoreType.DMA((n_sem,)),
                              pltpu.SemaphoreType.DMA((max(n_local, 1),))),
               compiler_params=pltpu.CompilerParams(collective_id=collective_id))
    def launch(send_sems, recv_sems, local_sems):
        body((send_sems, recv_sems), local_sems)

    launch()


def _gather_plan(outs, col_fam, sems, handshake):
    n_w = len(outs)
    x, y, c = _me()
    myq = 2 * x + y
    sib = (x, y, 1 - c)
    if handshake:
        _handshake([sib] + [_chip_peer(x, y, j)[:2] + (c,) for j in (1, 2, 3)])

    def slot(w, q):
        if col_fam[w]:
            return _col_window(outs[w], q, outs[w].shape[1] // N_CHIPS)
        return outs[w].at[q]

    first = []
    for w in range(n_w):
        for j in (1, 2, 3):
            px, py, _ = _chip_peer(x, y, j)
            mine = _half_rows(slot(w, myq), c)
            cp = _remote(mine, mine, sems, w * 6 + j - 1, (px, py, c))
            cp.start()
            first.append(cp)
    passed = []
    for w in range(n_w):
        for j in (1, 2, 3):
            _, _, pq = _chip_peer(x, y, j)
            land = _half_rows(slot(w, pq), c)
            _remote(land, land, sems, w * 6 + j - 1, sib).wait_recv()
            cp = _remote(land, land, sems, w * 6 + 2 + j, sib)
            cp.start()
            passed.append(cp)
    for w in range(n_w):
        for j in (1, 2, 3):
            _, _, pq = _chip_peer(x, y, j)
            land = _half_rows(slot(w, pq), 1 - c)
            _remote(land, land, sems, w * 6 + 2 + j, sib).wait_recv()
    for cp in first + passed:
        cp.wait_send()


def _gather_weights(bufs, col_fam):
    n_w = len(bufs)

    def body(*refs):
        _gather_plan(refs[n_w:2 * n_w], col_fam, refs[2 * n_w:2 * n_w + 2], False)

    return pl.pallas_call(
        body, name="gather_weights", in_specs=[ANY] * n_w, out_specs=[ANY] * n_w,
        out_shape=[jax.ShapeDtypeStruct(b.shape, b.dtype) for b in bufs],
        input_output_aliases={w: w for w in range(n_w)},
        scratch_shapes=[pltpu.SemaphoreType.DMA((6 * n_w,)), pltpu.SemaphoreType.DMA((6 * n_w,))],
    )(*bufs)


def _gather_weights_async(bufs, col_fam, name, collective_id):
    refs = [jax.new_ref(b, memory_space=HBM) for b in bufs]
    _on_sequencer(name, collective_id, 6 * len(bufs), 0,
                  lambda sems, _: _gather_plan(refs, col_fam, sems, True))
    return [r[...] for r in refs]


def _grad_half(ref, col, h):
    return _half0(ref, h) if col else _half1(ref, h)


def _swap_halves_with_sibling(grads, col_fam):
    n_w = len(grads)

    def body(*refs):
        _swap_plan(refs[:n_w], refs[n_w:2 * n_w], col_fam, refs[2 * n_w:], False)

    return pl.pallas_call(
        body, name="grad_swap_sibling", in_specs=[ANY] * n_w, out_specs=[ANY] * n_w,
        out_shape=_swap_shapes(grads, col_fam),
        scratch_shapes=[pltpu.SemaphoreType.DMA((n_w,)), pltpu.SemaphoreType.DMA((n_w,))],
    )(*grads)


def _swap_shapes(grads, col_fam):
    out = []
    for w, g in enumerate(grads):
        shp = (g.shape[0] // 2, g.shape[1]) if col_fam[w] else (g.shape[0], g.shape[1] // 2, g.shape[2])
        out.append(jax.ShapeDtypeStruct(shp, g.dtype))
    return out


def _swap_plan(ins, outs, col_fam, sems, handshake):
    x, y, c = _me()
    sib = (x, y, 1 - c)
    if handshake:
        _handshake([sib])
    cps = [_remote(_grad_half(ins[w], col_fam[w], 1 - c), outs[w], sems, w, sib) for w in range(len(ins))]
    for cp in cps:
        cp.start()
    for cp in cps:
        cp.wait_recv()
    for cp in cps:
        cp.wait_send()


def _swap_halves_async(grads, col_fam, name, collective_id):
    srcs = [jax.new_ref(g, memory_space=HBM) for g in grads]
    dsts = [jax.empty_ref(s, memory_space=HBM) for s in _swap_shapes(grads, col_fam)]
    _on_sequencer(name, collective_id, len(grads), 0, lambda sems, _: _swap_plan(srcs, dsts, col_fam, sems, True))
    return [r[...] for r in srcs], [r[...] for r in dsts]


def _half_add(mine, recv, c_arr, col, name):
    if col:
        rows, n = recv.shape
        tr = rows // 2
        grid = (2,)
        in_specs = [pl.BlockSpec((tr, n), lambda i, c: (2 * c[0] + i, 0)), pl.BlockSpec((tr, n), lambda i, c: (i, 0))]
        out_spec = pl.BlockSpec((tr, n), lambda i, c: (i, 0))
    else:
        _, rows, n = recv.shape
        grid = (N_CHIPS,)
        in_specs = [pl.BlockSpec((None, rows, n), lambda q, c: (q, c[0], 0)),
                    pl.BlockSpec((None, rows, n), lambda q, c: (q, 0, 0))]
        out_spec = pl.BlockSpec((None, rows, n), lambda q, c: (q, 0, 0))

    def body(c_ref, a_ref, b_ref, o_ref):
        o_ref[...] = (a_ref[...].astype(F32) + b_ref[...].astype(F32)).astype(BF16)

    return pl.pallas_call(
        body, name=name,
        grid_spec=pltpu.PrefetchScalarGridSpec(num_scalar_prefetch=1, grid=grid, in_specs=in_specs, out_specs=out_spec),
        out_shape=jax.ShapeDtypeStruct(recv.shape, BF16), compiler_params=_cp(),
    )(c_arr, mine, recv)


def _scatter_chip_sums(sums, col_fam):
    n_w = len(sums)

    def body(*refs):
        _scatter_plan(refs[:n_w], refs[n_w:2 * n_w], col_fam, refs[2 * n_w:2 * n_w + 2], refs[2 * n_w + 2], False)

    return pl.pallas_call(
        body, name="grad_scatter_chips", in_specs=[ANY] * n_w, out_specs=[ANY] * n_w,
        out_shape=_scatter_shapes(sums, col_fam),
        scratch_shapes=[pltpu.SemaphoreType.DMA((3 * n_w,)), pltpu.SemaphoreType.DMA((3 * n_w,)),
                        pltpu.SemaphoreType.DMA((n_w,))],
    )(*sums)


def _scatter_shapes(sums, col_fam):
    out = []
    for w, s in enumerate(sums):
        shp = (s.shape[0], s.shape[1] // N_CHIPS) if col_fam[w] else s.shape[1:]
        out.append(jax.ShapeDtypeStruct((N_CHIPS,) + shp, s.dtype))
    return out


def _scatter_plan(ins, outs, col_fam, sems, lsem, handshake):
    n_w = len(ins)
    x, y, c = _me()
    myq = 2 * x + y
    if handshake:
        _handshake([_chip_peer(x, y, j)[:2] + (c,) for j in (1, 2, 3)])

    def slab(w, q):
        if col_fam[w]:
            return _col_window(ins[w], q, ins[w].shape[1] // N_CHIPS)
        return ins[w].at[q]

    local = [pltpu.make_async_copy(slab(w, myq), outs[w].at[myq], lsem.at[w]) for w in range(n_w)]
    for cp in local:
        cp.start()
    cps = []
    for w in range(n_w):
        for j in (1, 2, 3):
            px, py, pq = _chip_peer(x, y, j)
            cp = _remote(slab(w, pq), outs[w].at[myq], sems, w * 3 + j - 1, (px, py, c))
            cp.start()
            cps.append(cp)
    for w in range(n_w):
        for j in (1, 2, 3):
            _, _, pq = _chip_peer(x, y, j)
            land = outs[w].at[pq]
            _remote(land, land, sems, w * 3 + j - 1, (x, y, c)).wait_recv()
    for cp in cps:
        cp.wait_send()
    for cp in local:
        cp.wait()


def _scatter_chip_sums_async(sums, col_fam, name, collective_id):
    srcs = [jax.new_ref(s, memory_space=HBM) for s in sums]
    dsts = [jax.empty_ref(s, memory_space=HBM) for s in _scatter_shapes(sums, col_fam)]
    _on_sequencer(name, collective_id, 3 * len(sums), len(sums),
                  lambda sems, lsem: _scatter_plan(srcs, dsts, col_fam, sems, lsem, True))
    return [r[...] for r in dsts]


def _sum_chips(parts, c_arr, prev, lead, shape, name):
    _, rows, n = parts.shape
    tr = rows // 2 if rows % 32 == 0 else rows
    nblk = rows // tr

    def body(c_ref, p_ref, *rest):
        o_ref = rest[-1]
        acc = p_ref[0].astype(F32)
        for q in range(1, N_CHIPS):
            acc = acc + p_ref[q].astype(F32)
        o_ref[...] = acc

    in_specs = [pl.BlockSpec((N_CHIPS, tr, n), lambda i, c: (0, i, 0))]
    args = [c_arr, parts]
    aliases = {}
    if prev is not None:
        in_specs.append(ANY)
        args.append(prev)
        aliases = {2: 0}
    return pl.pallas_call(
        body, name=name,
        grid_spec=pltpu.PrefetchScalarGridSpec(
            num_scalar_prefetch=1, grid=(nblk,), in_specs=in_specs,
            out_specs=pl.BlockSpec((None, tr, n), lambda i, c: (lead, c[0] * nblk + i, 0))),
        out_shape=jax.ShapeDtypeStruct(shape, F32), input_output_aliases=aliases, compiler_params=_cp(),
    )(*args)


def _join_plan(outs, place, sems, handshake):
    x, y, c = _me()
    sib = (x, y, 1 - c)
    if handshake:
        _handshake([sib])

    def half(k, h):
        o, lead = place[k]
        return _half_rows(outs[o].at[lead], h)

    cps = [_remote(half(k, c), half(k, c), sems, k, sib) for k in range(len(place))]
    for cp in cps:
        cp.start()
    for k in range(len(place)):
        land = half(k, 1 - c)
        _remote(land, land, sems, k, sib).wait_recv()
    for cp in cps:
        cp.wait_send()


def _join_halves(bufs, place, name):
    n_o = len(bufs)
    n_h = len(place)

    def body(*refs):
        _join_plan(refs[n_o:2 * n_o], place, refs[2 * n_o:2 * n_o + 2], False)

    return pl.pallas_call(
        body, name=name, in_specs=[ANY] * n_o, out_specs=[ANY] * n_o,
        out_shape=[jax.ShapeDtypeStruct(b.shape, b.dtype) for b in bufs],
        input_output_aliases={k: k for k in range(n_o)},
        scratch_shapes=[pltpu.SemaphoreType.DMA((n_h,)), pltpu.SemaphoreType.DMA((n_h,))],
    )(*bufs)


def _join_halves_async(bufs, place, name, collective_id):
    refs = [jax.new_ref(b, memory_space=HBM) for b in bufs]
    _on_sequencer(name, collective_id, len(place), 0, lambda sems, _: _join_plan(refs, place, sems, True))
    return [r[...] for r in refs]


def _allreduce_rows(rows):
    n_dev = 8
    n_r = len(rows)
    assert n_r <= 8

    def body(*refs):
        r_refs = refs[:n_r]
        o_ref, slots, send_sems, recv_sems = refs[n_r:]
        x, y, c = _me()
        me = 4 * x + 2 * y + c
        slots[me] = jnp.concatenate([r[...] for r in r_refs] + [jnp.zeros((8 - n_r, D_MODEL), F32)], axis=0)

        def peer(k):
            return (1 - x if k & 4 else x, 1 - y if k & 2 else y, 1 - c if k & 1 else c)

        cps = []
        for k in range(1, n_dev):
            cp = pltpu.make_async_remote_copy(src_ref=slots.at[me], dst_ref=slots.at[me], send_sem=send_sems.at[k - 1],
                                              recv_sem=recv_sems.at[k - 1], device_id=peer(k), device_id_type=MESH)
            cp.start()
            cps.append(cp)
        for k in range(1, n_dev):
            px, py, pc = peer(k)
            land = slots.at[4 * px + 2 * py + pc]
            pltpu.make_async_remote_copy(src_ref=land, dst_ref=land, send_sem=send_sems.at[k - 1],
                                         recv_sem=recv_sems.at[k - 1], device_id=peer(k),
                                         device_id_type=MESH).wait_recv()
        for cp in cps:
            cp.wait_send()
        acc = slots[0]
        for d in range(1, n_dev):
            acc = acc + slots[d]
        o_ref[...] = acc

    vm = pl.BlockSpec(memory_space=pltpu.VMEM)
    return pl.pallas_call(
        body, name="allreduce_rows", in_specs=[vm] * n_r, out_specs=vm,
        out_shape=jax.ShapeDtypeStruct((8, D_MODEL), F32),
        scratch_shapes=[pltpu.VMEM((n_dev, 8, D_MODEL), F32), pltpu.SemaphoreType.DMA((n_dev - 1,)),
                        pltpu.SemaphoreType.DMA((n_dev - 1,))],
    )(*rows)


def _adamw(w, g, m, v, name):
    shape = w.shape
    if len(shape) == 1:
        lead, rows, cols = 1, 1, shape[0]
    else:
        rows, cols = shape[-2:]
        lead = math.prod(shape[:-2])
    args = [a.reshape(lead, rows, cols) for a in (w, g, m, v)]
    tr = rows // 2 if rows % 16 == 0 else rows

    def body(w_ref, g_ref, m_ref, v_ref, d_ref, nm_ref, nv_ref):
        gv = g_ref[...]
        nm = ADAM_B1 * m_ref[...] + (1.0 - ADAM_B1) * gv
        nv = ADAM_B2 * v_ref[...] + (1.0 - ADAM_B2) * jnp.square(gv)
        m_hat = nm / (1.0 - ADAM_B1 ** ADAM_STEP)
        v_hat = nv / (1.0 - ADAM_B2 ** ADAM_STEP)
        d_ref[...] = -ADAM_LR * (m_hat / (jnp.sqrt(v_hat) + ADAM_EPS) + ADAM_WD * w_ref[...])
        nm_ref[...] = nm
        nv_ref[...] = nv

    spec = pl.BlockSpec((None, tr, cols), lambda l, i: (l, i, 0))
    outs = pl.pallas_call(
        body, name=name, grid=(lead, rows // tr), in_specs=[spec] * 4, out_specs=[spec] * 3,
        out_shape=[jax.ShapeDtypeStruct((lead, rows, cols), F32)] * 3, compiler_params=_cp(),
    )(*args)
    return [o.reshape(shape) for o in outs]


def kernel(x, a_w_in, a_sink, a_w_out, b_w_in, b_w_out, norm_mix, norm_ffn, w_gate, w_up, w_down, final_norm, loss_target, m_a_w_in, m_a_sink, m_a_w_out, m_b_w_in, m_b_w_out, m_norm_mix, m_norm_ffn, m_w_gate, m_w_up, m_w_down, m_final_norm, v_a_w_in, v_a_sink, v_a_w_out, v_b_w_in, v_b_w_out, v_norm_mix, v_norm_ffn, v_w_gate, v_w_up, v_w_down, v_final_norm):
    weights = dict(a_w_in=a_w_in, a_sink=a_sink, a_w_out=a_w_out, b_w_in=b_w_in, b_w_out=b_w_out, norm_mix=norm_mix,
                   norm_ffn=norm_ffn, w_gate=w_gate, w_up=w_up, w_down=w_down, final_norm=final_norm)
    mom = dict(a_w_in=m_a_w_in, a_sink=m_a_sink, a_w_out=m_a_w_out, b_w_in=m_b_w_in, b_w_out=m_b_w_out,
               norm_mix=m_norm_mix, norm_ffn=m_norm_ffn, w_gate=m_w_gate, w_up=m_w_up, w_down=m_w_down,
               final_norm=m_final_norm)
    var = dict(a_w_in=v_a_w_in, a_sink=v_a_sink, a_w_out=v_a_w_out, b_w_in=v_b_w_in, b_w_out=v_b_w_out,
               norm_mix=v_norm_mix, norm_ffn=v_norm_ffn, w_gate=v_w_gate, w_up=v_w_up, w_down=v_w_down,
               final_norm=v_final_norm)
    order = ["a_w_in", "a_sink", "a_w_out", "b_w_in", "b_w_out", "norm_mix", "norm_ffn", "w_gate", "w_up", "w_down",
             "final_norm"]

    c_arr = lax.axis_index("c").astype(jnp.int32).reshape(1)
    q_arr = (2 * lax.axis_index("x") + lax.axis_index("y")).astype(jnp.int32).reshape(1)
    shards = [a_w_in, a_w_out, b_w_in, b_w_out, w_gate, w_up, w_down]
    shard_names = ("a_in", "a_out", "b_in", "b_out", "wg", "wu", "wd")
    placed = [_place_shard(s, q_arr, col, f"place_{nm}")
              for s, col, nm in zip(shards, (True, False, True, False, False, False, False), shard_names)]
    (a_in,) = _gather_weights(placed[:1], (True,))
    a_out, b_in, b_out, wg, wu, wd = _gather_weights_async(placed[1:], (False, True, False, False, False, False),
                                                           "gather_weights_late", 1)
    a_out = a_out.reshape(D_MODEL, D_MODEL)
    b_out = b_out.reshape(D_MODEL, D_MODEL)

    gx, grads, vecs = _local_step(x, loss_target, a_in, a_sink[0], a_out, b_in, b_out, norm_mix, norm_ffn, wg, wu, wd,
                                  final_norm)

    rows_out = D_MODEL // N_CHIPS
    partials = [grads["a_in"], grads["b_in"],
                grads["a_out"].reshape(N_CHIPS, rows_out, D_MODEL), grads["b_out"].reshape(N_CHIPS, rows_out, D_MODEL),
                grads["wg"][0], grads["wg"][1], grads["wu"][0], grads["wu"][1], grads["wd"][0], grads["wd"][1]]
    col_fam = (True, True) + (False,) * 8
    names = ("a_in", "b_in", "a_out", "b_out", "wg0", "wg1", "wu0", "wu1", "wd0", "wd1")
    contrib = [None] * len(partials)

    def reduce_group(idx, tag, ids):
        parts = [partials[k] for k in idx]
        cols = tuple(col_fam[k] for k in idx)
        if ids is None:
            theirs = _swap_halves_with_sibling(parts, cols)
        else:
            parts, theirs = _swap_halves_async(parts, cols, f"grad_swap_{tag}", ids[0])
        sums = [_half_add(p, r, c_arr, cf, f"chip_sum_{names[k]}") for p, r, cf, k in zip(parts, theirs, cols, idx)]
        if ids is None:
            out = _scatter_chip_sums(sums, cols)
        else:
            out = _scatter_chip_sums_async(sums, cols, f"grad_scatter_{tag}", ids[1])
        for k, o in zip(idx, out):
            contrib[k] = o

    reduce_group([1, 3, 5, 7, 9], "layer1", (2, 3))
    reduce_group([2, 4, 6, 8], "ffn0", (4, 5))
    reduce_group([0], "a_in", None)
    shapes = [a_w_in.shape, b_w_in.shape, a_w_out.shape, b_w_out.shape, w_gate.shape, w_up.shape, w_gate.shape]
    place = [(0, 0), (1, 0), (2, 0), (3, 0), (4, 0), (4, 1), (5, 0), (5, 1), (6, 0), (6, 1)]
    bufs = [None] * len(shapes)
    for p, nm, (o, lead) in zip(contrib, names, place):
        bufs[o] = _sum_chips(p, c_arr, bufs[o], lead, shapes[o], f"sum_chips_{nm}")
    g_a_in, g_b_in, g_a_out, g_b_out, g_wg, g_wu, g_wdt = _join_halves(bufs, place, "grad_join_sibling")
    g_wd = g_wdt.transpose(0, 2, 1)

    sink_row = jnp.pad(vecs["sink"][0:1], ((0, 0), (0, D_MODEL - LANES)))
    tot = _allreduce_rows([vecs["norm_mix"][0], vecs["norm_mix"][1], vecs["norm_ffn"][0], vecs["norm_ffn"][1],
                           vecs["final"], vecs["loss_cols"], sink_row])
    loss = (0.5 / D_MODEL) * jnp.sum(tot[5])
    gw = dict(a_w_in=g_a_in, a_sink=tot[6:7, :N_HEADS], a_w_out=g_a_out, b_w_in=g_b_in, b_w_out=g_b_out,
              norm_mix=tot[0:2], norm_ffn=tot[2:4], w_gate=g_wg, w_up=g_wu, w_down=g_wd, final_norm=tot[4])

    delta, new_m, new_v = {}, {}, {}
    for n in order:
        delta[n], new_m[n], new_v[n] = _adamw(weights[n], gw[n], mom[n], var[n], f"adamw_{n}")
    return (loss, gx, *[gw[n] for n in order], *[delta[n] for n in order], *[new_m[n] for n in order],
            *[new_v[n] for n in order])
```

```python
import functools
import math

import jax
import jax.numpy as jnp
from jax import lax
from jax.experimental import pallas as pl
from jax.experimental.pallas import tpu as pltpu
from jax.experimental.pallas import tpu_sc as plsc

F32 = jnp.float32
BF16 = jnp.bfloat16

D_MODEL = 1024
HEAD_DIM = 64
N_HEADS = 16
N_KV = 4
QKV_W = 1536
D_FF = 2816
N_CHIPS = 4
FF_SH = D_FF // N_CHIPS
HALF_WINDOW_A = 128
DILATED = ((128, 1), (512, 4), (2048, 16))
ROPE_THETA = 10000.0
RMS_EPS = 1e-6
NEG_INF = -1e30
LANES = 128
ADAM_LR, ADAM_B1, ADAM_B2, ADAM_EPS, ADAM_WD, ADAM_STEP = 0.001, 0.9, 0.999, 1e-08, 0.01, 10
VMEM_LIMIT = 56 * 1024 * 1024
GRAD_TOKENS = 2048
MESH = pl.DeviceIdType.MESH


def _cp(**kw):
    return pltpu.CompilerParams(vmem_limit_bytes=VMEM_LIMIT, **kw)


def _row_tile(t, cap):
    tm = min(cap, t)
    assert t % tm == 0
    return tm


def _rope_tables(seq, dil):
    inv = 1.0 / (ROPE_THETA ** (jnp.arange(0, HEAD_DIM, 2, dtype=F32) / HEAD_DIM))
    ang = jnp.arange(seq, dtype=F32)[:, None] * inv[None, :]
    cos, sin = jnp.cos(ang), jnp.sin(ang)
    cos = jnp.tile(cos, (1, 4))
    sin = jnp.concatenate([-sin, sin, -sin, sin], axis=1)

    def perm(t):
        return t.reshape(seq // dil, dil, LANES).transpose(1, 0, 2).reshape(seq, LANES)

    return perm(cos), perm(sin)


def _swap_halves(t):
    lane = lax.broadcasted_iota(jnp.int32, t.shape, 1)
    return jnp.where((lane % HEAD_DIM) < HEAD_DIM // 2, pltpu.roll(t, LANES - 32, 1), pltpu.roll(t, 32, 1))


def _rope(t, cos, sin):
    return t * cos + _swap_halves(t) * sin


def _rope_t(t, cos, sin):
    return t * cos - _swap_halves(t) * sin


def _to_residue(t, batch, dil):
    if dil == 1:
        return t
    s = t.shape[0] // batch
    return t.reshape(batch, s // dil, dil, t.shape[1]).transpose(0, 2, 1, 3).reshape(t.shape)


def _from_residue(t, batch, dil):
    if dil == 1:
        return t
    s = t.shape[0] // batch
    return t.reshape(batch, dil, s // dil, t.shape[1]).transpose(0, 2, 1, 3).reshape(t.shape)


def _rms_fwd(x, w, name, with_t=False):
    t = x.shape[0]
    tm = _row_tile(t, 512)

    def body(x_ref, w_ref, o_ref, *ot_ref):
        y = _rms_tile(x_ref[...], w_ref[...])
        o_ref[...] = y.astype(BF16)
        if with_t:
            ot_ref[0][...] = y.T.astype(BF16)

    out_specs = [pl.BlockSpec((tm, D_MODEL), lambda i: (i, 0))]
    out_shape = [jax.ShapeDtypeStruct((t, D_MODEL), BF16)]
    if with_t:
        out_specs.append(pl.BlockSpec((D_MODEL, tm), lambda i: (0, i)))
        out_shape.append(jax.ShapeDtypeStruct((D_MODEL, t), BF16))
    outs = pl.pallas_call(
        body, name=name, grid=(t // tm,),
        in_specs=[pl.BlockSpec((tm, D_MODEL), lambda i: (i, 0)), pl.BlockSpec((1, D_MODEL), lambda i: (0, 0))],
        out_specs=out_specs, out_shape=out_shape, compiler_params=_cp(),
    )(x, w)
    return outs if with_t else outs[0]


def _rms_bwd_tile(xv, wv, dy, dres):
    r = lax.rsqrt(jnp.mean(xv * xv, axis=-1, keepdims=True) + RMS_EPS)
    xh = xv * r
    dxh = dy * wv
    dx = dres + r * (dxh - xh * jnp.mean(dxh * xh, axis=-1, keepdims=True))
    return dx, jnp.sum(dy * xh, axis=0, keepdims=True)


def _accumulate(ref, part):
    @pl.when(pl.program_id(0) == 0)
    def _():
        ref[...] = jnp.zeros_like(ref)

    ref[...] += part


def _rms_bwd(x, w, dhs, dres, name, with_t=False):
    t = x.shape[0]
    tm = _row_tile(t, 512)
    n = len(dhs)

    def body(*refs):
        x_ref, w_ref = refs[0], refs[1]
        dh_refs = refs[2:2 + n]
        dres_ref = refs[2 + n]
        dx_ref, dxb_ref = refs[3 + n:5 + n]
        dw_ref = refs[-1]
        dy = dh_refs[0][...].astype(F32)
        for k in range(1, n):
            dy = dy + dh_refs[k][...].astype(F32)
        dx, dw = _rms_bwd_tile(x_ref[...], w_ref[...], dy, dres_ref[...])
        dx_ref[...] = dx
        dxb_ref[...] = dx.astype(BF16)
        if with_t:
            refs[5 + n][...] = dx.T.astype(BF16)
        _accumulate(dw_ref, dw)

    row = pl.BlockSpec((tm, D_MODEL), lambda i: (i, 0))
    vec = pl.BlockSpec((1, D_MODEL), lambda i: (0, 0))
    out_specs = [row, row]
    out_shape = [jax.ShapeDtypeStruct((t, D_MODEL), F32), jax.ShapeDtypeStruct((t, D_MODEL), BF16)]
    if with_t:
        out_specs.append(pl.BlockSpec((D_MODEL, tm), lambda i: (0, i)))
        out_shape.append(jax.ShapeDtypeStruct((D_MODEL, t), BF16))
    return pl.pallas_call(
        body, name=name, grid=(t // tm,),
        in_specs=[row, vec] + [row] * n + [row],
        out_specs=out_specs + [vec], out_shape=out_shape + [jax.ShapeDtypeStruct((1, D_MODEL), F32)],
        compiler_params=_cp(),
    )(x, w, *dhs, dres)


def _final_tile(xv, wv, tv):
    r = lax.rsqrt(jnp.mean(xv * xv, axis=-1, keepdims=True) + RMS_EPS)
    xh = xv * r
    err = xh * wv - tv
    dy = err * (1.0 / D_MODEL)
    dxh = dy * wv
    dx = r * (dxh - xh * jnp.mean(dxh * xh, axis=-1, keepdims=True))
    return dx, jnp.sum(err * err, axis=0, keepdims=True), jnp.sum(dy * xh, axis=0, keepdims=True)


def _qkv_proj(h, w, cos, sin, group, name):
    t = h.shape[0]
    seq = cos.shape[0]
    tm = _row_tile(seq, 1024)
    n_q = N_HEADS * HEAD_DIM // LANES
    n_rope = (N_HEADS + N_KV) * HEAD_DIM // LANES
    scale = 1.0 / math.sqrt(HEAD_DIM)

    def body(h_ref, w_ref, cos_ref, sin_ref, o_ref):
        acc = jnp.dot(h_ref[...], w_ref[...], preferred_element_type=F32)
        cs, sn = cos_ref[...], sin_ref[...]
        csq, snq = cs * scale, sn * scale
        for c in range(QKV_W // LANES):
            blk = acc[:, c * LANES:(c + 1) * LANES]
            if c < n_q:
                blk = _rope(blk, csq, snq)
            elif c < n_rope:
                blk = _rope(blk, cs, sn)
            o_ref[:, c * LANES:(c + 1) * LANES] = blk.astype(BF16)

    tab = pl.BlockSpec((tm, LANES), lambda i: (i % (seq // tm), 0))
    return pl.pallas_call(
        body, name=name, grid=(t // tm,),
        in_specs=[pl.BlockSpec((tm, D_MODEL), lambda i: (i, 0)),
                  pl.BlockSpec((D_MODEL, QKV_W), lambda i: (0, group)), tab, tab],
        out_specs=pl.BlockSpec((tm, QKV_W), lambda i: (i, 0)),
        out_shape=jax.ShapeDtypeStruct((t, QKV_W), BF16), compiler_params=_cp(),
    )(h, w, cos, sin)


def _rms_tile(xv, wv):
    return (xv * lax.rsqrt(jnp.mean(xv * xv, axis=-1, keepdims=True) + RMS_EPS)) * wv


def _mm_res(a, w, res, nw, name):
    t, k = a.shape
    tm = _row_tile(t, 512)

    def body(a_ref, w_ref, r_ref, nw_ref, o_ref, h_ref, ht_ref):
        xv = r_ref[...] + jnp.dot(a_ref[...], w_ref[...], preferred_element_type=F32)
        o_ref[...] = xv
        h = _rms_tile(xv, nw_ref[...])
        h_ref[...] = h.astype(BF16)
        ht_ref[...] = h.T.astype(BF16)

    row = pl.BlockSpec((tm, D_MODEL), lambda i: (i, 0))
    return pl.pallas_call(
        body, name=name, grid=(t // tm,),
        in_specs=[pl.BlockSpec((tm, k), lambda i: (i, 0)),
                  pl.BlockSpec((k, D_MODEL), lambda i: (0, 0), pipeline_mode=pl.Buffered(1)), row,
                  pl.BlockSpec((1, D_MODEL), lambda i: (0, 0))],
        out_specs=[row, row, pl.BlockSpec((D_MODEL, tm), lambda i: (0, i))],
        out_shape=[jax.ShapeDtypeStruct((t, D_MODEL), F32), jax.ShapeDtypeStruct((t, D_MODEL), BF16),
                   jax.ShapeDtypeStruct((D_MODEL, t), BF16)],
        compiler_params=_cp(),
    )(a, w, res, nw)


def _mm_nt(dy, w, group, out_dtype, name):
    t, n = dy.shape
    k = w.shape[0]
    tm = _row_tile(t, 1024)

    def body(dy_ref, w_ref, o_ref):
        o_ref[...] = lax.dot_general(dy_ref[...], w_ref[...], (((1,), (1,)), ((), ())),
                                     preferred_element_type=F32).astype(out_dtype)

    return pl.pallas_call(
        body, name=name, grid=(t // tm,),
        in_specs=[pl.BlockSpec((tm, n), lambda i: (i, 0)), pl.BlockSpec((k, n), lambda i: (0, group))],
        out_specs=pl.BlockSpec((tm, k), lambda i: (i, 0)),
        out_shape=jax.ShapeDtypeStruct((t, k), out_dtype), compiler_params=_cp(),
    )(dy, w)


def _mm_nt_rms(dy, w, x, nw, dres, name):
    t, n = dy.shape
    tm = _row_tile(t, 512)

    def body(dy_ref, w_ref, x_ref, nw_ref, dres_ref, dx_ref, dw_ref):
        dh = lax.dot_general(dy_ref[...], w_ref[...], (((1,), (1,)), ((), ())), preferred_element_type=F32)
        dx, dw = _rms_bwd_tile(x_ref[...], nw_ref[...], dh, dres_ref[...])
        dx_ref[...] = dx
        _accumulate(dw_ref, dw)

    row = pl.BlockSpec((tm, D_MODEL), lambda i: (i, 0))
    vec = pl.BlockSpec((1, D_MODEL), lambda i: (0, 0))
    return pl.pallas_call(
        body, name=name, grid=(t // tm,),
        in_specs=[pl.BlockSpec((tm, n), lambda i: (i, 0)),
                  pl.BlockSpec((D_MODEL, n), lambda i: (0, 0), pipeline_mode=pl.Buffered(1)), row, vec, row],
        out_specs=[row, vec],
        out_shape=[jax.ShapeDtypeStruct((t, D_MODEL), F32), jax.ShapeDtypeStruct((1, D_MODEL), F32)],
        compiler_params=_cp(),
    )(dy, w, x, nw, dres)


def _out_bwd(dx, w, o, name):
    t = dx.shape[0]
    tm = _row_tile(t, 512)

    def body(dx_ref, w_ref, o_ref, et_ref, do_ref, adj_ref):
        do = lax.dot_general(dx_ref[...], w_ref[...], (((1,), (1,)), ((), ())), preferred_element_type=F32)
        do_ref[...] = do.astype(BF16)
        adj_ref[...] = -_dot_split(do * o_ref[...].astype(F32), et_ref[...])

    row = pl.BlockSpec((tm, D_MODEL), lambda i: (i, 0))
    return pl.pallas_call(
        body, name=name, grid=(t // tm,),
        in_specs=[row, pl.BlockSpec((D_MODEL, D_MODEL), lambda i: (0, 0)), row,
                  pl.BlockSpec((D_MODEL, LANES), lambda i: (0, 0))],
        out_specs=[row, pl.BlockSpec((tm, LANES), lambda i: (i, 0))],
        out_shape=[jax.ShapeDtypeStruct((t, D_MODEL), BF16), jax.ShapeDtypeStruct((t, LANES), F32)],
        compiler_params=_cp(),
    )(dx, w, o, _head_expander().T)


def _mm_tn(a, bs, name):
    aq = a.ndim == 3
    bq = bs[0].ndim == 3
    t, ka = a.shape[-2:]
    n = bs[0].shape[-1]
    nq = N_CHIPS if (aq or bq) else 1
    tt = _row_tile(t, GRAD_TOKENS)
    tn = n if n <= 1024 else 768
    assert n % tn == 0
    nb = len(bs)
    steps = t // tt

    def body(*refs):
        a_ref = refs[0]
        b_refs = refs[1:1 + nb]
        o_refs = refs[1 + nb:1 + 2 * nb]
        acc_refs = refs[1 + 2 * nb:]
        s = pl.program_id(2)
        av = a_ref[...]
        for b_ref, o_ref, acc_ref in zip(b_refs, o_refs, acc_refs):
            @pl.when(s == 0)
            def _():
                acc_ref[...] = jnp.zeros_like(acc_ref)

            acc_ref[...] += lax.dot_general(av, b_ref[...], (((0,), (0,)), ((), ())), preferred_element_type=F32)

            @pl.when(s == steps - 1)
            def _():
                o_ref[...] = acc_ref[...].astype(BF16)

    a_spec = (pl.BlockSpec((None, tt, ka), lambda q, j, s: (q, s, 0)) if aq
              else pl.BlockSpec((tt, ka), lambda q, j, s: (s, 0)))
    b_spec = (pl.BlockSpec((None, tt, tn), lambda q, j, s: (q, s, j)) if bq
              else pl.BlockSpec((tt, tn), lambda q, j, s: (s, j)))
    if nq > 1:
        o_spec = pl.BlockSpec((None, ka, tn), lambda q, j, s: (q, 0, j))
        o_shape = jax.ShapeDtypeStruct((nq, ka, n), BF16)
    else:
        o_spec = pl.BlockSpec((ka, tn), lambda q, j, s: (0, j))
        o_shape = jax.ShapeDtypeStruct((ka, n), BF16)
    outs = pl.pallas_call(
        body, name=name, grid=(nq, n // tn, steps),
        in_specs=[a_spec] + [b_spec] * nb, out_specs=[o_spec] * nb, out_shape=[o_shape] * nb,
        scratch_shapes=[pltpu.VMEM((ka, tn), F32)] * nb, compiler_params=_cp(),
    )(a, *bs)
    return outs


def _mm_grad(at, bs, name):
    ka, t = at.shape
    bq = bs[0].ndim == 3
    n = bs[0].shape[-1]
    nq = N_CHIPS if bq else 1
    tt = _row_tile(t, GRAD_TOKENS)
    tn = n if n <= 1024 else 768
    assert n % tn == 0
    nb = len(bs)
    steps = t // tt

    def body(*refs):
        a_ref = refs[0]
        b_refs = refs[1:1 + nb]
        o_refs = refs[1 + nb:1 + 2 * nb]
        acc_refs = refs[1 + 2 * nb:]
        s = pl.program_id(2)
        av = a_ref[...]
        for b_ref, o_ref, acc_ref in zip(b_refs, o_refs, acc_refs):
            @pl.when(s == 0)
            def _():
                acc_ref[...] = jnp.zeros_like(acc_ref)

            acc_ref[...] += jnp.dot(av, b_ref[...], preferred_element_type=F32)

            @pl.when(s == steps - 1)
            def _():
                o_ref[...] = acc_ref[...].astype(BF16)

    a_spec = pl.BlockSpec((ka, tt), lambda q, j, s: (0, s))
    if bq:
        b_spec = pl.BlockSpec((None, tt, tn), lambda q, j, s: (q, s, j))
        o_spec = pl.BlockSpec((None, ka, tn), lambda q, j, s: (q, 0, j))
        o_shape = jax.ShapeDtypeStruct((nq, ka, n), BF16)
    else:
        b_spec = pl.BlockSpec((tt, tn), lambda q, j, s: (s, j))
        o_spec = pl.BlockSpec((ka, tn), lambda q, j, s: (0, j))
        o_shape = jax.ShapeDtypeStruct((ka, n), BF16)
    return pl.pallas_call(
        body, name=name, grid=(nq, n // tn, steps),
        in_specs=[a_spec] + [b_spec] * nb, out_specs=[o_spec] * nb, out_shape=[o_shape] * nb,
        scratch_shapes=[pltpu.VMEM((ka, tn), F32)] * nb, compiler_params=_cp(),
    )(at, *bs)


def _sigmoid(x):
    return 1.0 / (1.0 + jnp.exp(-x))


def _ffn_up(h, wg, wu, layer, name):
    t = h.shape[0]
    tm = _row_tile(t, 1024)

    def body(h_ref, wg_ref, wu_ref, a_ref, dg_ref, du_ref):
        hv = h_ref[...]
        g = jnp.dot(hv, wg_ref[...], preferred_element_type=F32)
        u = jnp.dot(hv, wu_ref[...], preferred_element_type=F32)
        sg = _sigmoid(g)
        silu = g * sg
        a_ref[...] = (silu * u).astype(BF16)
        dg_ref[...] = (sg * (1.0 + g * (1.0 - sg)) * u).astype(BF16)
        du_ref[...] = silu.astype(BF16)

    wspec = pl.BlockSpec((None, None, D_MODEL, FF_SH), lambda q, i: (q, layer, 0, 0))
    ospec = pl.BlockSpec((None, tm, FF_SH), lambda q, i: (q, i, 0))
    oshape = jax.ShapeDtypeStruct((N_CHIPS, t, FF_SH), BF16)
    return pl.pallas_call(
        body, name=name, grid=(N_CHIPS, t // tm),
        in_specs=[pl.BlockSpec((tm, D_MODEL), lambda q, i: (i, 0)), wspec, wspec],
        out_specs=[ospec] * 3, out_shape=[oshape] * 3, compiler_params=_cp(),
    )(h, wg, wu)


def _ffn_down(a, wd, res, layer, name, norm_w=None, head=None):
    t = a.shape[1]
    tm = _row_tile(t, 512)
    resident = pl.BlockSpec((N_CHIPS, None, FF_SH, D_MODEL), lambda i: (0, layer, 0, 0), pipeline_mode=pl.Buffered(1))
    row = pl.BlockSpec((tm, D_MODEL), lambda i: (i, 0))
    vec = pl.BlockSpec((1, D_MODEL), lambda i: (0, 0))

    def hidden(a_ref, w_ref, r_ref):
        acc = r_ref[...]
        for q in range(N_CHIPS):
            acc = acc + jnp.dot(a_ref[q], w_ref[q], preferred_element_type=F32)
        return acc

    if head is None:
        def body(a_ref, w_ref, r_ref, nw_ref, o_ref, h_ref):
            xv = hidden(a_ref, w_ref, r_ref)
            o_ref[...] = xv
            h_ref[...] = _rms_tile(xv, nw_ref[...]).astype(BF16)

        return pl.pallas_call(
            body, name=name, grid=(t // tm,),
            in_specs=[pl.BlockSpec((N_CHIPS, tm, FF_SH), lambda i: (0, i, 0)), resident, row, vec],
            out_specs=[row, row],
            out_shape=[jax.ShapeDtypeStruct((t, D_MODEL), F32), jax.ShapeDtypeStruct((t, D_MODEL), BF16)],
            compiler_params=_cp(),
        )(a, wd, res, norm_w)

    def body(a_ref, w_ref, r_ref, nw_ref, t_ref, dx_ref, dxb_ref, dxt_ref, l_ref, dw_ref):
        dx, sq, dw = _final_tile(hidden(a_ref, w_ref, r_ref), nw_ref[...], t_ref[...])
        dx_ref[...] = dx
        dxb_ref[...] = dx.astype(BF16)
        dxt_ref[...] = dx.T.astype(BF16)
        _accumulate(l_ref, sq)
        _accumulate(dw_ref, dw)

    return pl.pallas_call(
        body, name=name, grid=(t // tm,),
        in_specs=[pl.BlockSpec((N_CHIPS, tm, FF_SH), lambda i: (0, i, 0)), resident, row, vec, row],
        out_specs=[row, row, pl.BlockSpec((D_MODEL, tm), lambda i: (0, i)), vec, vec],
        out_shape=[jax.ShapeDtypeStruct((t, D_MODEL), F32), jax.ShapeDtypeStruct((t, D_MODEL), BF16),
                   jax.ShapeDtypeStruct((D_MODEL, t), BF16),
                   jax.ShapeDtypeStruct((1, D_MODEL), F32), jax.ShapeDtypeStruct((1, D_MODEL), F32)],
        compiler_params=_cp(),
    )(a, wd, res, *head)


def _ffn_down_bwd(dx, wd, fg, fu, layer, name):
    t = dx.shape[0]
    tm = _row_tile(t, 512)

    def body(dx_ref, w_ref, fg_ref, fu_ref, dg_ref, du_ref):
        dxv = dx_ref[...]
        for q in range(N_CHIPS):
            da = lax.dot_general(dxv, w_ref[q], (((1,), (1,)), ((), ())), preferred_element_type=F32)
            dg_ref[q] = (da * fg_ref[q].astype(F32)).astype(BF16)
            du_ref[q] = (da * fu_ref[q].astype(F32)).astype(BF16)

    aspec = pl.BlockSpec((N_CHIPS, tm, FF_SH), lambda i: (0, i, 0))
    oshape = jax.ShapeDtypeStruct((N_CHIPS, t, FF_SH), BF16)
    return pl.pallas_call(
        body, name=name, grid=(t // tm,),
        in_specs=[pl.BlockSpec((tm, D_MODEL), lambda i: (i, 0)),
                  pl.BlockSpec((N_CHIPS, None, FF_SH, D_MODEL), lambda i: (0, layer, 0, 0)), aspec, aspec],
        out_specs=[aspec] * 2, out_shape=[oshape] * 2, compiler_params=_cp(),
    )(dx, wd, fg, fu)


def _ffn_up_bwd(dg, du, wg, wu, layer, x, nw, dres, name):
    t = dg.shape[1]
    tm = _row_tile(t, 512)
    nt = (((1,), (1,)), ((), ()))

    def body(dg_ref, du_ref, wg_ref, wu_ref, x_ref, nw_ref, dres_ref, dx_ref, dxb_ref, dw_ref):
        acc = jnp.zeros((tm, D_MODEL), F32)
        for q in range(N_CHIPS):
            acc = acc + lax.dot_general(dg_ref[q], wg_ref[q], nt, preferred_element_type=F32)
            acc = acc + lax.dot_general(du_ref[q], wu_ref[q], nt, preferred_element_type=F32)
        dx, dw = _rms_bwd_tile(x_ref[...], nw_ref[...], acc, dres_ref[...])
        dx_ref[...] = dx
        dxb_ref[...] = dx.astype(BF16)
        _accumulate(dw_ref, dw)

    aspec = pl.BlockSpec((N_CHIPS, tm, FF_SH), lambda i: (0, i, 0))
    wspec = pl.BlockSpec((N_CHIPS, None, D_MODEL, FF_SH), lambda i: (0, layer, 0, 0), pipeline_mode=pl.Buffered(1))
    row = pl.BlockSpec((tm, D_MODEL), lambda i: (i, 0))
    vec = pl.BlockSpec((1, D_MODEL), lambda i: (0, 0))
    return pl.pallas_call(
        body, name=name, grid=(t // tm,),
        in_specs=[aspec, aspec, wspec, wspec, row, vec, row],
        out_specs=[row, row, vec],
        out_shape=[jax.ShapeDtypeStruct((t, D_MODEL), F32), jax.ShapeDtypeStruct((t, D_MODEL), BF16),
                   jax.ShapeDtypeStruct((1, D_MODEL), F32)],
        compiler_params=_cp(),
    )(dg, du, wg, wu, x, nw, dres)


def _attn_geometry(length, half_window):
    qb = min(LANES, length)
    kw = min(qb + 2 * half_window, length)
    return qb, kw, length // qb


def _dup_kv(src_ref, dst_ref, s, length):
    ch = min(length, 256)
    lo = lax.broadcasted_iota(jnp.int32, (ch, LANES), 1) < HEAD_DIM

    def chunk(c, carry):
        r0 = pl.multiple_of(c * ch, ch)
        for j in range(N_KV // 2):
            tile = src_ref[s, pl.ds(r0, ch), j * LANES:(j + 1) * LANES].astype(F32)
            rolled = pltpu.roll(tile, HEAD_DIM, 1)
            dst_ref[2 * j, pl.ds(r0, ch), :] = jnp.where(lo, tile, rolled).astype(BF16)
            dst_ref[2 * j + 1, pl.ds(r0, ch), :] = jnp.where(lo, rolled, tile).astype(BF16)
        return carry

    lax.fori_loop(0, length // ch, chunk, 0)


def _stack_heads(ref, s, q0, qb, g):
    lo = lax.broadcasted_iota(jnp.int32, (qb, LANES), 1) < HEAD_DIM
    parts = []
    for a in range(4):
        col = (2 * g + a // 2) * LANES
        tile = ref[s, pl.ds(q0, qb), col:col + LANES]
        keep = lo if a % 2 == 0 else jnp.logical_not(lo)
        parts.append(jnp.where(keep, tile, jnp.zeros_like(tile)))
    return jnp.concatenate(parts, axis=0)


def _unstack_pair_t(stacked_t, qb, pair):
    both = jnp.concatenate([stacked_t[:, (2 * pair) * qb:(2 * pair + 1) * qb],
                            stacked_t[:, (2 * pair + 1) * qb:(2 * pair + 2) * qb]], axis=0)
    return both.T


def _band_mask_t(q0, k0, qb, kw, half_window):
    key = lax.broadcasted_iota(jnp.int32, (kw, 4 * qb), 0)
    qry = lax.broadcasted_iota(jnp.int32, (kw, 4 * qb), 1) & (qb - 1)
    return jnp.abs((q0 + qry) - (k0 + key)) <= half_window


def _block_origin(i, qb, kw, half_window, length):
    if isinstance(i, int):
        return i * qb, min(max(i * qb - half_window, 0), length - kw)
    return (pl.multiple_of(i * qb, qb),
            pl.multiple_of(jnp.clip(i * qb - half_window, 0, length - kw), HEAD_DIM))


def _head_row(vals, qb):
    return jnp.concatenate([jnp.broadcast_to(v, (1, qb)).astype(F32) for v in vals], axis=1)


def _attn_fwd(qkv, sink, n_seq, length, half_window, seq_blk, out_dtype, name):
    qb, kw, nblk = _attn_geometry(length, half_window)
    with_sink = sink is not None
    nt = (((1,), (1,)), ((), ()))
    tn = (((0,), (0,)), ((), ()))
    qkv3 = qkv.reshape(n_seq, length, QKV_W)

    def body(*refs):
        refs = list(refs)
        sink_ref = refs.pop(0) if with_sink else None
        q_ref, k_ref, v_ref, o_ref, lse_ref = refs[:5]
        kx_ref, vx_ref = refs[-2:]
        head_row = lax.broadcasted_iota(jnp.int32, (N_HEADS, qb), 0)
        for s in range(seq_blk):
            _dup_kv(k_ref, kx_ref, s, length)
            _dup_kv(v_ref, vx_ref, s, length)

            def block(i, carry):
                q0, k0 = _block_origin(i, qb, kw, half_window, length)
                valid = _band_mask_t(q0, k0, qb, kw, half_window)
                lse_tile = jnp.zeros((N_HEADS, qb), F32)
                groups = range(N_KV)
                sts = [lax.dot_general(kx_ref[g, pl.ds(k0, kw), :], _stack_heads(q_ref, s, q0, qb, g), nt,
                                       preferred_element_type=F32) for g in groups]
                sts = [jnp.where(valid, st, NEG_INF) for st in sts]
                ms = [jnp.max(st, axis=0, keepdims=True) for st in sts]
                if with_sink:
                    sks = [_head_row([sink_ref[4 * g + a] for a in range(4)], qb) for g in groups]
                    ms = [jnp.maximum(m, sk) for m, sk in zip(ms, sks)]
                es = [jnp.exp(st - m) for st, m in zip(sts, ms)]
                dens = [jnp.sum(e, axis=0, keepdims=True) for e in es]
                if with_sink:
                    dens = [den + jnp.exp(sk - m) for den, sk, m in zip(dens, sks, ms)]
                ots = [lax.dot_general(vx_ref[g, pl.ds(k0, kw), 0:HEAD_DIM], es[g].astype(BF16), tn,
                                       preferred_element_type=F32) / dens[g] for g in groups]
                for g in groups:
                    for pair in range(2):
                        col = (2 * g + pair) * LANES
                        o_ref[s, pl.ds(q0, qb), col:col + LANES] = _unstack_pair_t(ots[g], qb, pair).astype(out_dtype)
                    lse = ms[g] + jnp.log(dens[g])
                    for a in range(4):
                        lse_tile = jnp.where(head_row == 4 * g + a, lse[:, a * qb:(a + 1) * qb], lse_tile)
                lse_ref[s, :, pl.ds(q0, qb)] = lse_tile
                return carry

            if nblk == 1:
                block(0, 0)
            else:
                lax.fori_loop(0, nblk, block, 0)

    in_specs = [pl.BlockSpec((seq_blk, length, N_HEADS * HEAD_DIM), lambda n: (n, 0, 0)),
                pl.BlockSpec((seq_blk, length, N_KV * HEAD_DIM), lambda n: (n, 0, 4)),
                pl.BlockSpec((seq_blk, length, N_KV * HEAD_DIM), lambda n: (n, 0, 5))]
    args = [qkv3, qkv3, qkv3]
    if with_sink:
        in_specs.insert(0, pl.BlockSpec(memory_space=pltpu.SMEM))
        args.insert(0, sink)
    out_specs = [pl.BlockSpec((seq_blk, length, D_MODEL), lambda n: (n, 0, 0)),
                 pl.BlockSpec((seq_blk, N_HEADS, length), lambda n: (n, 0, 0))]
    out_shape = [jax.ShapeDtypeStruct((n_seq, length, D_MODEL), out_dtype),
                 jax.ShapeDtypeStruct((n_seq, N_HEADS, length), F32)]
    o, lse = pl.pallas_call(
        body, name=name, grid=(n_seq // seq_blk,), in_specs=in_specs, out_specs=out_specs, out_shape=out_shape,
        scratch_shapes=[pltpu.VMEM((N_KV, length, LANES), BF16), pltpu.VMEM((N_KV, length, LANES), BF16)],
        compiler_params=_cp(),
    )(*args)
    return o.reshape(n_seq * length, D_MODEL), lse


def _attn_bwd(qkv, do, adj, lse, sink, cos, sin, n_seq, length, half_window, seq_blk, dil, name):
    qb, kw, nblk = _attn_geometry(length, half_window)
    scale = 1.0 / math.sqrt(HEAD_DIM)
    with_sink = sink is not None
    nt = (((1,), (1,)), ((), ()))
    tn = (((0,), (0,)), ((), ()))
    qkv3 = qkv.reshape(n_seq, length, QKV_W)
    do3 = do.reshape(n_seq, length, D_MODEL)
    tabs = [t.reshape(dil, length, LANES) for t in (cos, sin)]
    tab_blocks = dil // seq_blk if dil >= seq_blk else 1

    def body(*refs):
        refs = list(refs)
        sink_ref = refs.pop(0) if with_sink else None
        q_ref, k_ref, v_ref, do_ref, aux_ref, lse_ref, cos_ref, sin_ref, dqkv_ref = refs[:9]
        ds_ref = refs[9] if with_sink else None
        kx_ref, vx_ref, dkx_ref, dvx_ref = refs[-4:]
        lane = lax.broadcasted_iota(jnp.int32, (1, LANES), 1)
        if with_sink:
            @pl.when(pl.program_id(0) == 0)
            def _():
                ds_ref[...] = jnp.zeros_like(ds_ref)

        for s in range(seq_blk):
            ts = s % dil
            _dup_kv(k_ref, kx_ref, s, length)
            _dup_kv(v_ref, vx_ref, s, length)
            dkx_ref[...] = jnp.zeros_like(dkx_ref)
            dvx_ref[...] = jnp.zeros_like(dvx_ref)

            def block(i, dsink):
                q0, k0 = _block_origin(i, qb, kw, half_window, length)
                valid = _band_mask_t(q0, k0, qb, kw, half_window)
                cs = cos_ref[ts, pl.ds(q0, qb), :] * scale
                sn = sin_ref[ts, pl.ds(q0, qb), :] * scale
                adj_tile = aux_ref[s, :, pl.ds(q0, qb)]
                lse_tile = lse_ref[s, :, pl.ds(q0, qb)]
                groups = range(N_KV)
                qss = [_stack_heads(q_ref, s, q0, qb, g) for g in groups]
                doss = [_stack_heads(do_ref, s, q0, qb, g) for g in groups]
                kxs = [kx_ref[g, pl.ds(k0, kw), :] for g in groups]
                sts = [lax.dot_general(kxs[g], qss[g], nt, preferred_element_type=F32) for g in groups]
                dpts = [lax.dot_general(vx_ref[g, pl.ds(k0, kw), :], doss[g], nt, preferred_element_type=F32)
                        for g in groups]
                lses = [_head_row([lse_tile[4 * g + a:4 * g + a + 1, :] for a in range(4)], qb) for g in groups]
                shifts = [_head_row([adj_tile[4 * g + a:4 * g + a + 1, :] for a in range(4)], qb) for g in groups]
                pts = [jnp.exp(jnp.where(valid, sts[g], NEG_INF) - lses[g]) for g in groups]
                dsbs = [(pts[g] * (dpts[g] + shifts[g])).astype(BF16) for g in groups]
                pbs = [pt.astype(BF16) for pt in pts]
                if with_sink:
                    for g in groups:
                        sk = _head_row([sink_ref[4 * g + a] for a in range(4)], qb)
                        dsk = jnp.exp(sk - lses[g]) * shifts[g]
                        for a in range(4):
                            tot = jnp.sum(dsk[:, a * qb:(a + 1) * qb], axis=1, keepdims=True)
                            dsink = dsink + jnp.where(lane == 4 * g + a, tot, 0.0)
                dqts = [lax.dot_general(kx_ref[g, pl.ds(k0, kw), 0:HEAD_DIM], dsbs[g], tn, preferred_element_type=F32)
                        for g in groups]
                for g in groups:
                    for pair in range(2):
                        col = (2 * g + pair) * LANES
                        tile = _rope_t(_unstack_pair_t(dqts[g], qb, pair), cs, sn)
                        dqkv_ref[s, pl.ds(q0, qb), col:col + LANES] = tile.astype(BF16)
                for g in groups:
                    dkx_ref[g, pl.ds(k0, kw), :] += jnp.dot(dsbs[g], qss[g], preferred_element_type=F32)
                    dvx_ref[g, pl.ds(k0, kw), :] += jnp.dot(pbs[g], doss[g], preferred_element_type=F32)
                return dsink

            if nblk == 1:
                dsink = block(0, jnp.zeros((1, LANES), F32))
            else:
                dsink = lax.fori_loop(0, nblk, block, jnp.zeros((1, LANES), F32))
            if with_sink:
                ds_ref[0:1, :] += dsink

            ch = min(length, 256)
            lo_c = lax.broadcasted_iota(jnp.int32, (ch, LANES), 1) < HEAD_DIM

            def fin(c, carry):
                r0 = pl.multiple_of(c * ch, ch)
                cs = cos_ref[ts, pl.ds(r0, ch), :]
                sn = sin_ref[ts, pl.ds(r0, ch), :]
                for j in range(N_KV // 2):
                    both = []
                    for acc_ref in (dkx_ref, dvx_ref):
                        t0 = acc_ref[2 * j, pl.ds(r0, ch), :]
                        t1 = acc_ref[2 * j + 1, pl.ds(r0, ch), :]
                        t0 = t0 + pltpu.roll(t0, HEAD_DIM, 1)
                        t1 = t1 + pltpu.roll(t1, HEAD_DIM, 1)
                        both.append(jnp.where(lo_c, t0, t1))
                    kcol = N_HEADS * HEAD_DIM + j * LANES
                    vcol = (N_HEADS + N_KV) * HEAD_DIM + j * LANES
                    dqkv_ref[s, pl.ds(r0, ch), kcol:kcol + LANES] = _rope_t(both[0], cs, sn).astype(BF16)
                    dqkv_ref[s, pl.ds(r0, ch), vcol:vcol + LANES] = both[1].astype(BF16)
                return carry

            lax.fori_loop(0, length // ch, fin, 0)

    seq_map = lambda n: (n, 0, 0)
    tab_map = (lambda n: (n % tab_blocks, 0, 0)) if dil >= seq_blk else (lambda n: (0, 0, 0))
    tab_rows = min(seq_blk, dil)
    in_specs = [pl.BlockSpec((seq_blk, length, N_HEADS * HEAD_DIM), seq_map),
                pl.BlockSpec((seq_blk, length, N_KV * HEAD_DIM), lambda n: (n, 0, 4)),
                pl.BlockSpec((seq_blk, length, N_KV * HEAD_DIM), lambda n: (n, 0, 5)),
                pl.BlockSpec((seq_blk, length, D_MODEL), seq_map),
                pl.BlockSpec((seq_blk, N_HEADS, length), seq_map),
                pl.BlockSpec((seq_blk, N_HEADS, length), seq_map),
                pl.BlockSpec((tab_rows, length, LANES), tab_map),
                pl.BlockSpec((tab_rows, length, LANES), tab_map)]
    args = [qkv3, qkv3, qkv3, do3, adj, lse] + tabs
    if with_sink:
        in_specs.insert(0, pl.BlockSpec(memory_space=pltpu.SMEM))
        args.insert(0, sink)
    out_specs = [pl.BlockSpec((seq_blk, length, QKV_W), seq_map)]
    out_shape = [jax.ShapeDtypeStruct((n_seq, length, QKV_W), BF16)]
    if with_sink:
        out_specs.append(pl.BlockSpec((8, LANES), lambda n: (0, 0)))
        out_shape.append(jax.ShapeDtypeStruct((8, LANES), F32))
    outs = pl.pallas_call(
        body, name=name, grid=(n_seq // seq_blk,), in_specs=in_specs, out_specs=out_specs, out_shape=out_shape,
        scratch_shapes=[pltpu.VMEM((N_KV, length, LANES), BF16), pltpu.VMEM((N_KV, length, LANES), BF16),
                        pltpu.VMEM((N_KV, length, LANES), F32), pltpu.VMEM((N_KV, length, LANES), F32)],
        compiler_params=_cp(),
    )(*args)
    dqkv = outs[0].reshape(n_seq * length, QKV_W)
    return (dqkv, outs[1]) if with_sink else (dqkv, None)


def _head_expander():
    h = jnp.arange(LANES)[:, None]
    l = jnp.arange(D_MODEL)[None, :]
    return (l // HEAD_DIM == h).astype(BF16)


def _dot_split(a, e):
    hi = a.astype(BF16)
    lo = (a - hi.astype(F32)).astype(BF16)
    return jnp.dot(hi, e, preferred_element_type=F32) + jnp.dot(lo, e, preferred_element_type=F32)


def _mix_weights(lses):
    m = jnp.maximum(jnp.maximum(lses[0], lses[1]), lses[2])
    es = [jnp.exp(v - m) for v in lses]
    tot = es[0] + es[1] + es[2]
    return [e / tot for e in es]


def _mix_fwd(os_, lses, name):
    t = os_[0].shape[0]
    tm = _row_tile(t, 512)

    def body(o0, o1, o2, l0, l1, l2, e_ref, out_ref):
        wts = _mix_weights([l0[...], l1[...], l2[...]])
        acc = jnp.zeros((tm, D_MODEL), F32)
        for w, o_ref in zip(wts, (o0, o1, o2)):
            acc = acc + _dot_split(w, e_ref[...]) * o_ref[...]
        out_ref[...] = acc.astype(BF16)

    row = pl.BlockSpec((tm, D_MODEL), lambda i: (i, 0))
    lrow = pl.BlockSpec((tm, LANES), lambda i: (i, 0))
    return pl.pallas_call(
        body, name=name, grid=(t // tm,),
        in_specs=[row] * 3 + [lrow] * 3 + [pl.BlockSpec((LANES, D_MODEL), lambda i: (0, 0))],
        out_specs=row, out_shape=jax.ShapeDtypeStruct((t, D_MODEL), BF16), compiler_params=_cp(),
    )(*os_, *lses, _head_expander())


def _mix_bwd(dmix, os_, lses, name):
    t = dmix.shape[0]
    tm = _row_tile(t, 512)

    def body(d_ref, o0, o1, o2, l0, l1, l2, e_ref, et_ref, do0, do1, do2, a0, a1, a2):
        wts = _mix_weights([l0[...], l1[...], l2[...]])
        dv = d_ref[...].astype(F32)
        cs = [_dot_split(dv * o_ref[...], et_ref[...]) for o_ref in (o0, o1, o2)]
        mean_c = wts[0] * cs[0] + wts[1] * cs[1] + wts[2] * cs[2]
        for w, c, do_ref, a_ref in zip(wts, cs, (do0, do1, do2), (a0, a1, a2)):
            do_ref[...] = (_dot_split(w, e_ref[...]) * dv).astype(BF16)
            a_ref[...] = w * (c - mean_c) - w * c

    row = pl.BlockSpec((tm, D_MODEL), lambda i: (i, 0))
    lrow = pl.BlockSpec((tm, LANES), lambda i: (i, 0))
    e = _head_expander()
    return pl.pallas_call(
        body, name=name, grid=(t // tm,),
        in_specs=[row] * 4 + [lrow] * 3 + [pl.BlockSpec((LANES, D_MODEL), lambda i: (0, 0)),
                                            pl.BlockSpec((D_MODEL, LANES), lambda i: (0, 0))],
        out_specs=[row] * 3 + [lrow] * 3,
        out_shape=[jax.ShapeDtypeStruct((t, D_MODEL), BF16)] * 3 + [jax.ShapeDtypeStruct((t, LANES), F32)] * 3,
        compiler_params=_cp(),
    )(dmix, *os_, *lses, e, e.T)


def _stats_to_tokens(stat, batch, dil):
    n_seq, _, length = stat.shape
    t = stat.transpose(0, 2, 1).reshape(n_seq * length, N_HEADS)
    return _from_residue(jnp.pad(t, ((0, 0), (0, LANES - N_HEADS))), batch, dil)


def _stats_from_tokens(stat, batch, dil, n_seq, length):
    t = _to_residue(stat[:, :N_HEADS], batch, dil)
    return t.reshape(n_seq, length, N_HEADS).transpose(0, 2, 1)


def _group_geometry(batch, seq, dil, window):
    length = seq // dil
    n_seq = batch * dil
    seq_blk = max(1, min(dil, 1024 // length))
    return n_seq, length, (window // 2) // dil, seq_blk


def _local_step(x, target, a_in, a_sink, a_out, b_in, b_out, norm_mix, norm_ffn, wg, wu, wd, final_norm):
    batch, seq, _ = x.shape
    t = batch * seq
    x0 = x.reshape(t, D_MODEL)
    tgt = target.reshape(t, D_MODEL)
    tabs = {d: _rope_tables(seq, d) for _, d in DILATED}
    nm = [norm_mix[i:i + 1] for i in range(2)]
    nf = [norm_ffn[i:i + 1] for i in range(2)]

    h0, h0t = _rms_fwd(x0, nm[0], "rms_mix0", True)
    qkv0 = _qkv_proj(h0, a_in, *tabs[1], 0, "qkv0")
    o0, lse0 = _attn_fwd(qkv0, a_sink, batch, seq, HALF_WINDOW_A, 1, BF16, "attn0")
    x1, hf0, hf0t = _mm_res(o0, a_out, x0, nf[0], "out0")
    act0, g0, u0 = _ffn_up(hf0, wg[0], wu[0], 0, "ffn_up0")
    x2, h1 = _ffn_down(act0, wd[0], x1, 0, "ffn_down0", norm_w=nm[1])

    geo = [_group_geometry(batch, seq, d, w) for w, d in DILATED]
    h1g, qkv1, o1, lse1, lse1r = [], [], [], [], []
    for gi, (_, d) in enumerate(DILATED):
        n_seq, length, hw, sb = geo[gi]
        hp = _to_residue(h1, batch, d)
        pj = _qkv_proj(hp, b_in, *tabs[d], gi, f"qkv1_{gi}")
        o, lse = _attn_fwd(pj, None, n_seq, length, hw, sb, BF16, f"attn1_{gi}")
        h1g.append(hp)
        qkv1.append(pj)
        o1.append(_from_residue(o, batch, d))
        lse1r.append(lse)
        lse1.append(_stats_to_tokens(lse, batch, d))
    omix = _mix_fwd(o1, lse1, "mix")
    x3, hf1, hf1t = _mm_res(omix, b_out, x2, nf[1], "out1")
    act1, g1, u1 = _ffn_up(hf1, wg[1], wu[1], 0, "ffn_up1")
    dx4, dx4b, dx4t, loss_cols, d_final = _ffn_down(act1, wd[1], x3, 0, "ffn_down1_loss",
                                                     head=(final_norm.reshape(1, D_MODEL), tgt))

    def ffn_bwd(dxo, dxob, dxot, x_mid, hft, g, u, act, layer):
        dg, du = _ffn_down_bwd(dxob, wd[layer], g, u, 0, f"ffn_down_bwd{layer}")
        (d_wdt,) = _mm_grad(dxot, [act], f"grad_wd{layer}")
        dxm, dxmb, d_nf = _ffn_up_bwd(dg, du, wg[layer], wu[layer], 0, x_mid, nf[layer], dxo, f"ffn_up_bwd{layer}")
        d_wg, d_wu = _mm_grad(hft, [dg, du], f"grad_wgu{layer}")
        return dxm, dxmb, d_nf, d_wg, d_wu, d_wdt

    dx3, dx3b, d_nf1, d_wg1, d_wu1, d_wd1 = ffn_bwd(dx4, dx4b, dx4t, x3, hf1t, g1, u1, act1, 1)

    dmix = _mm_nt(dx3b, b_out, 0, BF16, "out1_bwd")
    (d_b_out,) = _mm_tn(omix, [dx3b], "grad_b_out")
    mb = _mix_bwd(dmix, o1, lse1, "mix_bwd")
    dh1, d_b_in = [], []
    for gi, (_, d) in enumerate(DILATED):
        n_seq, length, hw, sb = geo[gi]
        dog = _to_residue(mb[gi], batch, d)
        adj = _stats_from_tokens(mb[3 + gi], batch, d, n_seq, length)
        dpj, _ = _attn_bwd(qkv1[gi], dog, adj, lse1r[gi], None, *tabs[d], n_seq, length, hw, sb, d, f"attn1_bwd{gi}")
        (dw,) = _mm_tn(h1g[gi], [dpj], f"grad_b_in{gi}")
        d_b_in.append(dw)
        dh1.append(_from_residue(_mm_nt(dpj, b_in, gi, BF16, f"qkv1_bwd{gi}"), batch, d))
    dx2, dx2b, dx2t, d_nm1 = _rms_bwd(x2, nm[1], dh1, dx3, "rms_mix_bwd1", True)

    dx1, dx1b, d_nf0, d_wg0, d_wu0, d_wd0 = ffn_bwd(dx2, dx2b, dx2t, x1, hf0t, g0, u0, act0, 0)

    do0, adj0 = _out_bwd(dx1b, a_out, o0, "out0_bwd")
    (d_a_out,) = _mm_tn(o0, [dx1b], "grad_a_out")
    adj0 = _stats_from_tokens(adj0, batch, 1, batch, seq)
    dqkv0, d_sink = _attn_bwd(qkv0, do0, adj0, lse0, a_sink, *tabs[1], batch, seq, HALF_WINDOW_A, 1, 1, "attn0_bwd")
    (d_a_in,) = _mm_grad(h0t, [dqkv0], "grad_a_in")
    gx, d_nm0 = _mm_nt_rms(dqkv0, a_in, x0, nm[0], dx1, "qkv0_bwd")

    grads = dict(a_in=d_a_in, a_out=d_a_out, b_in=jnp.concatenate(d_b_in, axis=1), b_out=d_b_out,
                 wg=(d_wg0, d_wg1), wu=(d_wu0, d_wu1), wd=(d_wd0, d_wd1))
    vecs = dict(norm_mix=(d_nm0, d_nm1), norm_ffn=(d_nf0, d_nf1), final=d_final, loss_cols=loss_cols, sink=d_sink)
    return gx.reshape(x.shape), grads, vecs


ANY = pl.BlockSpec(memory_space=pl.ANY)
HBM = pltpu.MemorySpace.HBM


def _me():
    return lax.axis_index("x"), lax.axis_index("y"), lax.axis_index("c")


def _chip_peer(x, y, j):
    px = 1 - x if j & 2 else x
    py = 1 - y if j & 1 else y
    return px, py, 2 * px + py


def _remote(src, dst, sems, k, dev):
    return pltpu.make_async_remote_copy(src_ref=src, dst_ref=dst, send_sem=sems[0].at[k], recv_sem=sems[1].at[k],
                                        device_id=dev, device_id_type=MESH)


def _col_window(ref, q, width):
    return ref.at[:, pl.ds(pl.multiple_of(q * width, LANES), width)]


def _half0(ref, h):
    n = ref.shape[0] // 2
    return ref.at[pl.ds(h * n, n)]


def _half1(ref, h):
    n = ref.shape[1] // 2
    return ref.at[:, pl.ds(h * n, n)]


def _half_rows(ref, h):
    n = ref.shape[-2] // 2
    if len(ref.shape) == 2:
        return ref.at[pl.ds(h * n, n)]
    return ref.at[:, pl.ds(h * n, n)]


def _place_shard(w, layer, q_arr, col, name):
    _, rows, cols = w.shape

    def body(q_ref, w_ref, o_ref):
        o_ref[...] = w_ref[...].astype(BF16)

    if col:
        out_spec = pl.BlockSpec((rows, cols), lambda l, q: (0, q[0]))
        out_shape = jax.ShapeDtypeStruct((rows, N_CHIPS * cols), BF16)
    else:
        out_spec = pl.BlockSpec((None, None, rows, cols), lambda l, q: (q[0], 0, 0, 0))
        out_shape = jax.ShapeDtypeStruct((N_CHIPS, 1, rows, cols), BF16)
    return pl.pallas_call(
        body, name=name,
        grid_spec=pltpu.PrefetchScalarGridSpec(
            num_scalar_prefetch=1, grid=(1,),
            in_specs=[pl.BlockSpec((None, rows, cols), lambda l, q: (layer, 0, 0))], out_specs=out_spec),
        out_shape=out_shape, compiler_params=_cp(),
    )(q_arr, w)


def _handshake(peers):
    barrier = pltpu.get_barrier_semaphore()
    for p in peers:
        pl.semaphore_signal(barrier, inc=1, device_id=p, device_id_type=MESH)
    pl.semaphore_wait(barrier, len(peers))


def _on_sequencer(name, collective_id, n_sem, n_local, body):
    @pl.kernel(mesh=plsc.ScalarSubcoreMesh(axis_name="seq", num_cores=1), name=name,
               scratch_types=(pltpu.SemaphoreType.DMA((n_sem,)), pltpu.SemaphoreType.DMA((n_sem,)),
                              pltpu.SemaphoreType.DMA((max(n_local, 1),))),
               compiler_params=pltpu.CompilerParams(collective_id=collective_id))
    def launch(send_sems, recv_sems, local_sems):
        body((send_sems, recv_sems), local_sems)

    launch()


def _gather_plan(outs, col_fam, sems, handshake):
    n_w = len(outs)
    x, y, c = _me()
    myq = 2 * x + y
    sib = (x, y, 1 - c)
    if handshake:
        _handshake([sib] + [_chip_peer(x, y, j)[:2] + (c,) for j in (1, 2, 3)])

    def slot(w, q):
        if col_fam[w]:
            return _col_window(outs[w], q, outs[w].shape[1] // N_CHIPS)
        return outs[w].at[q]

    first = []
    for w in range(n_w):
        for j in (1, 2, 3):
            px, py, _ = _chip_peer(x, y, j)
            mine = _half_rows(slot(w, myq), c)
            cp = _remote(mine, mine, sems, w * 6 + j - 1, (px, py, c))
            cp.start()
            first.append(cp)
    passed = []
    for w in range(n_w):
        for j in (1, 2, 3):
            _, _, pq = _chip_peer(x, y, j)
            land = _half_rows(slot(w, pq), c)
            _remote(land, land, sems, w * 6 + j - 1, sib).wait_recv()
            cp = _remote(land, land, sems, w * 6 + 2 + j, sib)
            cp.start()
            passed.append(cp)
    for w in range(n_w):
        for j in (1, 2, 3):
            _, _, pq = _chip_peer(x, y, j)
            land = _half_rows(slot(w, pq), 1 - c)
            _remote(land, land, sems, w * 6 + 2 + j, sib).wait_recv()
    for cp in first + passed:
        cp.wait_send()


def _gather_weights(bufs, col_fam):
    n_w = len(bufs)

    def body(*refs):
        _gather_plan(refs[n_w:2 * n_w], col_fam, refs[2 * n_w:2 * n_w + 2], False)

    return pl.pallas_call(
        body, name="gather_weights", in_specs=[ANY] * n_w, out_specs=[ANY] * n_w,
        out_shape=[jax.ShapeDtypeStruct(b.shape, b.dtype) for b in bufs],
        input_output_aliases={w: w for w in range(n_w)},
        scratch_shapes=[pltpu.SemaphoreType.DMA((6 * n_w,)), pltpu.SemaphoreType.DMA((6 * n_w,))],
    )(*bufs)


def _gather_weights_async(bufs, col_fam, name, collective_id):
    refs = [jax.new_ref(b, memory_space=HBM) for b in bufs]
    _on_sequencer(name, collective_id, 6 * len(bufs), 0,
                  lambda sems, _: _gather_plan(refs, col_fam, sems, True))
    return [r[...] for r in refs]


def _grad_half(ref, col, h):
    return _half0(ref, h) if col else _half1(ref, h)


def _swap_halves_with_sibling(grads, col_fam):
    n_w = len(grads)

    def body(*refs):
        _swap_plan(refs[:n_w], refs[n_w:2 * n_w], col_fam, refs[2 * n_w:], False)

    return pl.pallas_call(
        body, name="grad_swap_sibling", in_specs=[ANY] * n_w, out_specs=[ANY] * n_w,
        out_shape=_swap_shapes(grads, col_fam),
        scratch_shapes=[pltpu.SemaphoreType.DMA((n_w,)), pltpu.SemaphoreType.DMA((n_w,))],
    )(*grads)


def _swap_shapes(grads, col_fam):
    out = []
    for w, g in enumerate(grads):
        shp = (g.shape[0] // 2, g.shape[1]) if col_fam[w] else (g.shape[0], g.shape[1] // 2, g.shape[2])
        out.append(jax.ShapeDtypeStruct(shp, g.dtype))
    return out


def _swap_plan(ins, outs, col_fam, sems, handshake):
    x, y, c = _me()
    sib = (x, y, 1 - c)
    if handshake:
        _handshake([sib])
    cps = [_remote(_grad_half(ins[w], col_fam[w], 1 - c), outs[w], sems, w, sib) for w in range(len(ins))]
    for cp in cps:
        cp.start()
    for cp in cps:
        cp.wait_recv()
    for cp in cps:
        cp.wait_send()


def _swap_halves_async(grads, col_fam, name, collective_id):
    srcs = [jax.new_ref(g, memory_space=HBM) for g in grads]
    dsts = [jax.empty_ref(s, memory_space=HBM) for s in _swap_shapes(grads, col_fam)]
    _on_sequencer(name, collective_id, len(grads), 0, lambda sems, _: _swap_plan(srcs, dsts, col_fam, sems, True))
    return [r[...] for r in srcs], [r[...] for r in dsts]


def _half_add(mine, recv, c_arr, col, name):
    if col:
        rows, n = recv.shape
        tr = rows // 2
        grid = (2,)
        in_specs = [pl.BlockSpec((tr, n), lambda i, c: (2 * c[0] + i, 0)), pl.BlockSpec((tr, n), lambda i, c: (i, 0))]
        out_spec = pl.BlockSpec((tr, n), lambda i, c: (i, 0))
    else:
        _, rows, n = recv.shape
        grid = (N_CHIPS,)
        in_specs = [pl.BlockSpec((None, rows, n), lambda q, c: (q, c[0], 0)),
                    pl.BlockSpec((None, rows, n), lambda q, c: (q, 0, 0))]
        out_spec = pl.BlockSpec((None, rows, n), lambda q, c: (q, 0, 0))

    def body(c_ref, a_ref, b_ref, o_ref):
        o_ref[...] = (a_ref[...].astype(F32) + b_ref[...].astype(F32)).astype(BF16)

    return pl.pallas_call(
        body, name=name,
        grid_spec=pltpu.PrefetchScalarGridSpec(num_scalar_prefetch=1, grid=grid, in_specs=in_specs, out_specs=out_spec),
        out_shape=jax.ShapeDtypeStruct(recv.shape, BF16), compiler_params=_cp(),
    )(c_arr, mine, recv)


def _scatter_chip_sums(sums, col_fam):
    n_w = len(sums)

    def body(*refs):
        _scatter_plan(refs[:n_w], refs[n_w:2 * n_w], col_fam, refs[2 * n_w:2 * n_w + 2], refs[2 * n_w + 2], False)

    return pl.pallas_call(
        body, name="grad_scatter_chips", in_specs=[ANY] * n_w, out_specs=[ANY] * n_w,
        out_shape=_scatter_shapes(sums, col_fam),
        scratch_shapes=[pltpu.SemaphoreType.DMA((3 * n_w,)), pltpu.SemaphoreType.DMA((3 * n_w,)),
                        pltpu.SemaphoreType.DMA((n_w,))],
    )(*sums)


def _scatter_shapes(sums, col_fam):
    out = []
    for w, s in enumerate(sums):
        shp = (s.shape[0], s.shape[1] // N_CHIPS) if col_fam[w] else s.shape[1:]
        out.append(jax.ShapeDtypeStruct((N_CHIPS,) + shp, s.dtype))
    return out


def _scatter_plan(ins, outs, col_fam, sems, lsem, handshake):
    n_w = len(ins)
    x, y, c = _me()
    myq = 2 * x + y
    if handshake:
        _handshake([_chip_peer(x, y, j)[:2] + (c,) for j in (1, 2, 3)])

    def slab(w, q):
        if col_fam[w]:
            return _col_window(ins[w], q, ins[w].shape[1] // N_CHIPS)
        return ins[w].at[q]

    local = [pltpu.make_async_copy(slab(w, myq), outs[w].at[myq], lsem.at[w]) for w in range(n_w)]
    for cp in local:
        cp.start()
    cps = []
    for w in range(n_w):
        for j in (1, 2, 3):
            px, py, pq = _chip_peer(x, y, j)
            cp = _remote(slab(w, pq), outs[w].at[myq], sems, w * 3 + j - 1, (px, py, c))
            cp.start()
            cps.append(cp)
    for w in range(n_w):
        for j in (1, 2, 3):
            _, _, pq = _chip_peer(x, y, j)
            land = outs[w].at[pq]
            _remote(land, land, sems, w * 3 + j - 1, (x, y, c)).wait_recv()
    for cp in cps:
        cp.wait_send()
    for cp in local:
        cp.wait()


def _scatter_chip_sums_async(sums, col_fam, name, collective_id):
    srcs = [jax.new_ref(s, memory_space=HBM) for s in sums]
    dsts = [jax.empty_ref(s, memory_space=HBM) for s in _scatter_shapes(sums, col_fam)]
    _on_sequencer(name, collective_id, 3 * len(sums), len(sums),
                  lambda sems, lsem: _scatter_plan(srcs, dsts, col_fam, sems, lsem, True))
    return [r[...] for r in dsts]


def _sum_chips(parts, c_arr, prev, lead, shape, name):
    _, rows, n = parts.shape
    tr = rows // 2 if rows % 32 == 0 else rows
    nblk = rows // tr

    def body(c_ref, p_ref, *rest):
        o_ref = rest[-1]
        acc = p_ref[0].astype(F32)
        for q in range(1, N_CHIPS):
            acc = acc + p_ref[q].astype(F32)
        o_ref[...] = acc

    in_specs = [pl.BlockSpec((N_CHIPS, tr, n), lambda i, c: (0, i, 0))]
    args = [c_arr, parts]
    aliases = {}
    if prev is not None:
        in_specs.append(ANY)
        args.append(prev)
        aliases = {2: 0}
    return pl.pallas_call(
        body, name=name,
        grid_spec=pltpu.PrefetchScalarGridSpec(
            num_scalar_prefetch=1, grid=(nblk,), in_specs=in_specs,
            out_specs=pl.BlockSpec((None, tr, n), lambda i, c: (lead, c[0] * nblk + i, 0))),
        out_shape=jax.ShapeDtypeStruct(shape, F32), input_output_aliases=aliases, compiler_params=_cp(),
    )(*args)


def _join_plan(outs, place, sems, handshake):
    x, y, c = _me()
    sib = (x, y, 1 - c)
    if handshake:
        _handshake([sib])

    def half(k, h):
        o, lead = place[k]
        return _half_rows(outs[o].at[lead], h)

    cps = [_remote(half(k, c), half(k, c), sems, k, sib) for k in range(len(place))]
    for cp in cps:
        cp.start()
    for k in range(len(place)):
        land = half(k, 1 - c)
        _remote(land, land, sems, k, sib).wait_recv()
    for cp in cps:
        cp.wait_send()


def _join_halves(bufs, place, name):
    n_o = len(bufs)
    n_h = len(place)

    def body(*refs):
        _join_plan(refs[n_o:2 * n_o], place, refs[2 * n_o:2 * n_o + 2], False)

    return pl.pallas_call(
        body, name=name, in_specs=[ANY] * n_o, out_specs=[ANY] * n_o,
        out_shape=[jax.ShapeDtypeStruct(b.shape, b.dtype) for b in bufs],
        input_output_aliases={k: k for k in range(n_o)},
        scratch_shapes=[pltpu.SemaphoreType.DMA((n_h,)), pltpu.SemaphoreType.DMA((n_h,))],
    )(*bufs)


def _join_halves_async(bufs, place, name, collective_id):
    refs = [jax.new_ref(b, memory_space=HBM) for b in bufs]
    _on_sequencer(name, collective_id, len(place), 0, lambda sems, _: _join_plan(refs, place, sems, True))
    return [r[...] for r in refs]


def _allreduce_rows(rows):
    n_dev = 8
    n_r = len(rows)
    assert n_r <= 8

    def body(*refs):
        r_refs = refs[:n_r]
        o_ref, slots, send_sems, recv_sems = refs[n_r:]
        x, y, c = _me()
        me = 4 * x + 2 * y + c
        slots[me] = jnp.concatenate([r[...] for r in r_refs] + [jnp.zeros((8 - n_r, D_MODEL), F32)], axis=0)

        def peer(k):
            return (1 - x if k & 4 else x, 1 - y if k & 2 else y, 1 - c if k & 1 else c)

        cps = []
        for k in range(1, n_dev):
            cp = pltpu.make_async_remote_copy(src_ref=slots.at[me], dst_ref=slots.at[me], send_sem=send_sems.at[k - 1],
                                              recv_sem=recv_sems.at[k - 1], device_id=peer(k), device_id_type=MESH)
            cp.start()
            cps.append(cp)
        for k in range(1, n_dev):
            px, py, pc = peer(k)
            land = slots.at[4 * px + 2 * py + pc]
            pltpu.make_async_remote_copy(src_ref=land, dst_ref=land, send_sem=send_sems.at[k - 1],
                                         recv_sem=recv_sems.at[k - 1], device_id=peer(k),
                                         device_id_type=MESH).wait_recv()
        for cp in cps:
            cp.wait_send()
        acc = slots[0]
        for d in range(1, n_dev):
            acc = acc + slots[d]
        o_ref[...] = acc

    vm = pl.BlockSpec(memory_space=pltpu.VMEM)
    return pl.pallas_call(
        body, name="allreduce_rows", in_specs=[vm] * n_r, out_specs=vm,
        out_shape=jax.ShapeDtypeStruct((8, D_MODEL), F32),
        scratch_shapes=[pltpu.VMEM((n_dev, 8, D_MODEL), F32), pltpu.SemaphoreType.DMA((n_dev - 1,)),
                        pltpu.SemaphoreType.DMA((n_dev - 1,))],
    )(*rows)


def _adamw(w, g, m, v, name):
    shape = w.shape
    if len(shape) == 1:
        lead, rows, cols = 1, 1, shape[0]
    else:
        rows, cols = shape[-2:]
        lead = math.prod(shape[:-2])
    args = [a.reshape(lead, rows, cols) for a in (w, g, m, v)]
    tr = rows // 2 if rows % 16 == 0 else rows

    def body(w_ref, g_ref, m_ref, v_ref, d_ref, nm_ref, nv_ref):
        gv = g_ref[...]
        nm = ADAM_B1 * m_ref[...] + (1.0 - ADAM_B1) * gv
        nv = ADAM_B2 * v_ref[...] + (1.0 - ADAM_B2) * jnp.square(gv)
        m_hat = nm / (1.0 - ADAM_B1 ** ADAM_STEP)
        v_hat = nv / (1.0 - ADAM_B2 ** ADAM_STEP)
        d_ref[...] = -ADAM_LR * (m_hat / (jnp.sqrt(v_hat) + ADAM_EPS) + ADAM_WD * w_ref[...])
        nm_ref[...] = nm
        nv_ref[...] = nv

    spec = pl.BlockSpec((None, tr, cols), lambda l, i: (l, i, 0))
    outs = pl.pallas_call(
        body, name=name, grid=(lead, rows // tr), in_specs=[spec] * 4, out_specs=[spec] * 3,
        out_shape=[jax.ShapeDtypeStruct((lead, rows, cols), F32)] * 3, compiler_params=_cp(),
    )(*args)
    return [o.reshape(shape) for o in outs]


def kernel(x, a_w_in, a_sink, a_w_out, b_w_in, b_w_out, norm_mix, norm_ffn, w_gate, w_up, w_down, final_norm, loss_target, m_a_w_in, m_a_sink, m_a_w_out, m_b_w_in, m_b_w_out, m_norm_mix, m_norm_ffn, m_w_gate, m_w_up, m_w_down, m_final_norm, v_a_w_in, v_a_sink, v_a_w_out, v_b_w_in, v_b_w_out, v_norm_mix, v_norm_ffn, v_w_gate, v_w_up, v_w_down, v_final_norm):
    weights = dict(a_w_in=a_w_in, a_sink=a_sink, a_w_out=a_w_out, b_w_in=b_w_in, b_w_out=b_w_out, norm_mix=norm_mix,
                   norm_ffn=norm_ffn, w_gate=w_gate, w_up=w_up, w_down=w_down, final_norm=final_norm)
    mom = dict(a_w_in=m_a_w_in, a_sink=m_a_sink, a_w_out=m_a_w_out, b_w_in=m_b_w_in, b_w_out=m_b_w_out,
               norm_mix=m_norm_mix, norm_ffn=m_norm_ffn, w_gate=m_w_gate, w_up=m_w_up, w_down=m_w_down,
               final_norm=m_final_norm)
    var = dict(a_w_in=v_a_w_in, a_sink=v_a_sink, a_w_out=v_a_w_out, b_w_in=v_b_w_in, b_w_out=v_b_w_out,
               norm_mix=v_norm_mix, norm_ffn=v_norm_ffn, w_gate=v_w_gate, w_up=v_w_up, w_down=v_w_down,
               final_norm=v_final_norm)
    order = ["a_w_in", "a_sink", "a_w_out", "b_w_in", "b_w_out", "norm_mix", "norm_ffn", "w_gate", "w_up", "w_down",
             "final_norm"]

    c_arr = lax.axis_index("c").astype(jnp.int32).reshape(1)
    q_arr = (2 * lax.axis_index("x") + lax.axis_index("y")).astype(jnp.int32).reshape(1)
    def placed(w, layer, col, nm):
        return _place_shard(w, layer, q_arr, col, f"place_{nm}")

    (a_in,) = _gather_weights_async([placed(a_w_in, 0, True, "a_in")], (True,), "gather_weights_first", 6)
    a_out, wg0, wu0, wd0 = _gather_weights_async(
        [placed(a_w_out, 0, False, "a_out"), placed(w_gate, 0, False, "wg0"), placed(w_up, 0, False, "wu0"),
         placed(w_down, 0, False, "wd0")], (False,) * 4, "gather_weights_layer0", 1)
    b_in, b_out, wg1, wu1, wd1 = _gather_weights_async(
        [placed(b_w_in, 0, True, "b_in"), placed(b_w_out, 0, False, "b_out"), placed(w_gate, 1, False, "wg1"),
         placed(w_up, 1, False, "wu1"), placed(w_down, 1, False, "wd1")], (True,) + (False,) * 4,
        "gather_weights_layer1", 7)
    a_out = a_out.reshape(D_MODEL, D_MODEL)
    b_out = b_out.reshape(D_MODEL, D_MODEL)
    wg, wu, wd = (wg0, wg1), (wu0, wu1), (wd0, wd1)

    gx, grads, vecs = _local_step(x, loss_target, a_in, a_sink[0], a_out, b_in, b_out, norm_mix, norm_ffn, wg, wu, wd,
                                  final_norm)

    rows_out = D_MODEL // N_CHIPS
    partials = [grads["a_in"], grads["b_in"],
                grads["a_out"].reshape(N_CHIPS, rows_out, D_MODEL), grads["b_out"].reshape(N_CHIPS, rows_out, D_MODEL),
                grads["wg"][0], grads["wg"][1], grads["wu"][0], grads["wu"][1], grads["wd"][0], grads["wd"][1]]
    col_fam = (True, True) + (False,) * 8
    names = ("a_in", "b_in", "a_out", "b_out", "wg0", "wg1", "wu0", "wu1", "wd0", "wd1")
    contrib = [None] * len(partials)

    def reduce_group(idx, tag, ids):
        parts = [partials[k] for k in idx]
        cols = tuple(col_fam[k] for k in idx)
        if ids is None:
            theirs = _swap_halves_with_sibling(parts, cols)
        else:
            parts, theirs = _swap_halves_async(parts, cols, f"grad_swap_{tag}", ids[0])
        sums = [_half_add(p, r, c_arr, cf, f"chip_sum_{names[k]}") for p, r, cf, k in zip(parts, theirs, cols, idx)]
        if ids is None:
            out = _scatter_chip_sums(sums, cols)
        else:
            out = _scatter_chip_sums_async(sums, cols, f"grad_scatter_{tag}", ids[1])
        for k, o in zip(idx, out):
            contrib[k] = o

    reduce_group([1, 3, 5, 7, 9], "layer1", (2, 3))
    reduce_group([2, 4, 6, 8], "ffn0", (4, 5))
    reduce_group([0], "a_in", None)
    shapes = [a_w_in.shape, b_w_in.shape, a_w_out.shape, b_w_out.shape, w_gate.shape, w_up.shape, w_gate.shape]
    place = [(0, 0), (1, 0), (2, 0), (3, 0), (4, 0), (4, 1), (5, 0), (5, 1), (6, 0), (6, 1)]
    bufs = [None] * len(shapes)
    for p, nm, (o, lead) in zip(contrib, names, place):
        bufs[o] = _sum_chips(p, c_arr, bufs[o], lead, shapes[o], f"sum_chips_{nm}")
    g_a_in, g_b_in, g_a_out, g_b_out, g_wg, g_wu, g_wdt = _join_halves(bufs, place, "grad_join_sibling")
    g_wd = g_wdt.transpose(0, 2, 1)

    sink_row = jnp.pad(vecs["sink"][0:1], ((0, 0), (0, D_MODEL - LANES)))
    tot = _allreduce_rows([vecs["norm_mix"][0], vecs["norm_mix"][1], vecs["norm_ffn"][0], vecs["norm_ffn"][1],
                           vecs["final"], vecs["loss_cols"], sink_row])
    loss = (0.5 / D_MODEL) * jnp.sum(tot[5])
    gw = dict(a_w_in=g_a_in, a_sink=tot[6:7, :N_HEADS], a_w_out=g_a_out, b_w_in=g_b_in, b_w_out=g_b_out,
              norm_mix=tot[0:2], norm_ffn=tot[2:4], w_gate=g_wg, w_up=g_wu, w_down=g_wd, final_norm=tot[4])

    delta, new_m, new_v = {}, {}, {}
    for n in order:
        delta[n], new_m[n], new_v[n] = _adamw(weights[n], gw[n], mom[n], var[n], f"adamw_{n}")
    return (loss, gx, *[gw[n] for n in order], *[delta[n] for n in order], *[new_m[n] for n in order],
            *[new_v[n] for n in order])
```

```python
import functools
import math

import jax
import jax.numpy as jnp
from jax import lax
from jax.experimental import pallas as pl
from jax.experimental.pallas import tpu as pltpu
from jax.experimental.pallas import tpu_sc as plsc

F32 = jnp.float32
BF16 = jnp.bfloat16

D_MODEL = 1024
HEAD_DIM = 64
N_HEADS = 16
N_KV = 4
QKV_W = 1536
D_FF = 2816
N_CHIPS = 4
FF_SH = D_FF // N_CHIPS
HALF_WINDOW_A = 128
DILATED = ((128, 1), (512, 4), (2048, 16))
ROPE_THETA = 10000.0
RMS_EPS = 1e-6
NEG_INF = -1e30
LANES = 128
ADAM_LR, ADAM_B1, ADAM_B2, ADAM_EPS, ADAM_WD, ADAM_STEP = 0.001, 0.9, 0.999, 1e-08, 0.01, 10
VMEM_LIMIT = 56 * 1024 * 1024
GRAD_TOKENS = 2048
MESH = pl.DeviceIdType.MESH


def _cp(**kw):
    return pltpu.CompilerParams(vmem_limit_bytes=VMEM_LIMIT, **kw)


def _row_tile(t, cap):
    tm = min(cap, t)
    assert t % tm == 0
    return tm


def _rope_tables(seq, dil):
    inv = 1.0 / (ROPE_THETA ** (jnp.arange(0, HEAD_DIM, 2, dtype=F32) / HEAD_DIM))
    ang = jnp.arange(seq, dtype=F32)[:, None] * inv[None, :]
    cos, sin = jnp.cos(ang), jnp.sin(ang)
    cos = jnp.tile(cos, (1, 4))
    sin = jnp.concatenate([-sin, sin, -sin, sin], axis=1)

    def perm(t):
        return t.reshape(seq // dil, dil, LANES).transpose(1, 0, 2).reshape(seq, LANES)

    return perm(cos), perm(sin)


def _swap_halves(t):
    lane = lax.broadcasted_iota(jnp.int32, t.shape, 1)
    return jnp.where((lane % HEAD_DIM) < HEAD_DIM // 2, pltpu.roll(t, LANES - 32, 1), pltpu.roll(t, 32, 1))


def _rope(t, cos, sin):
    return t * cos + _swap_halves(t) * sin


def _rope_t(t, cos, sin):
    return t * cos - _swap_halves(t) * sin


def _to_residue(t, batch, dil):
    if dil == 1:
        return t
    s = t.shape[0] // batch
    return t.reshape(batch, s // dil, dil, t.shape[1]).transpose(0, 2, 1, 3).reshape(t.shape)


def _from_residue(t, batch, dil):
    if dil == 1:
        return t
    s = t.shape[0] // batch
    return t.reshape(batch, dil, s // dil, t.shape[1]).transpose(0, 2, 1, 3).reshape(t.shape)


def _rms_fwd(x, w, name, with_t=False):
    t = x.shape[0]
    tm = _row_tile(t, 512)

    def body(x_ref, w_ref, o_ref, *ot_ref):
        y = _rms_tile(x_ref[...], w_ref[...])
        o_ref[...] = y.astype(BF16)
        if with_t:
            ot_ref[0][...] = y.T.astype(BF16)

    out_specs = [pl.BlockSpec((tm, D_MODEL), lambda i: (i, 0))]
    out_shape = [jax.ShapeDtypeStruct((t, D_MODEL), BF16)]
    if with_t:
        out_specs.append(pl.BlockSpec((D_MODEL, tm), lambda i: (0, i)))
        out_shape.append(jax.ShapeDtypeStruct((D_MODEL, t), BF16))
    outs = pl.pallas_call(
        body, name=name, grid=(t // tm,),
        in_specs=[pl.BlockSpec((tm, D_MODEL), lambda i: (i, 0)), pl.BlockSpec((1, D_MODEL), lambda i: (0, 0))],
        out_specs=out_specs, out_shape=out_shape, compiler_params=_cp(),
    )(x, w)
    return outs if with_t else outs[0]


def _rms_bwd_tile(xv, wv, dy, dres):
    r = lax.rsqrt(jnp.mean(xv * xv, axis=-1, keepdims=True) + RMS_EPS)
    xh = xv * r
    dxh = dy * wv
    dx = dres + r * (dxh - xh * jnp.mean(dxh * xh, axis=-1, keepdims=True))
    return dx, jnp.sum(dy * xh, axis=0, keepdims=True)


def _accumulate(ref, part):
    @pl.when(pl.program_id(0) == 0)
    def _():
        ref[...] = jnp.zeros_like(ref)

    ref[...] += part


def _rms_bwd(x, w, dhs, dres, name, with_t=False):
    t = x.shape[0]
    tm = _row_tile(t, 512)
    n = len(dhs)

    def body(*refs):
        x_ref, w_ref = refs[0], refs[1]
        dh_refs = refs[2:2 + n]
        dres_ref = refs[2 + n]
        dx_ref, dxb_ref = refs[3 + n:5 + n]
        dw_ref = refs[-1]
        dy = dh_refs[0][...].astype(F32)
        for k in range(1, n):
            dy = dy + dh_refs[k][...].astype(F32)
        dx, dw = _rms_bwd_tile(x_ref[...], w_ref[...], dy, dres_ref[...])
        dx_ref[...] = dx
        dxb_ref[...] = dx.astype(BF16)
        if with_t:
            refs[5 + n][...] = dx.T.astype(BF16)
        _accumulate(dw_ref, dw)

    row = pl.BlockSpec((tm, D_MODEL), lambda i: (i, 0))
    vec = pl.BlockSpec((1, D_MODEL), lambda i: (0, 0))
    out_specs = [row, row]
    out_shape = [jax.ShapeDtypeStruct((t, D_MODEL), F32), jax.ShapeDtypeStruct((t, D_MODEL), BF16)]
    if with_t:
        out_specs.append(pl.BlockSpec((D_MODEL, tm), lambda i: (0, i)))
        out_shape.append(jax.ShapeDtypeStruct((D_MODEL, t), BF16))
    return pl.pallas_call(
        body, name=name, grid=(t // tm,),
        in_specs=[row, vec] + [row] * n + [row],
        out_specs=out_specs + [vec], out_shape=out_shape + [jax.ShapeDtypeStruct((1, D_MODEL), F32)],
        compiler_params=_cp(),
    )(x, w, *dhs, dres)


def _final_tile(xv, wv, tv):
    r = lax.rsqrt(jnp.mean(xv * xv, axis=-1, keepdims=True) + RMS_EPS)
    xh = xv * r
    err = xh * wv - tv
    dy = err * (1.0 / D_MODEL)
    dxh = dy * wv
    dx = r * (dxh - xh * jnp.mean(dxh * xh, axis=-1, keepdims=True))
    return dx, jnp.sum(err * err, axis=0, keepdims=True), jnp.sum(dy * xh, axis=0, keepdims=True)


def _qkv_proj(h, w, cos, sin, group, name):
    t = h.shape[0]
    seq = cos.shape[0]
    tm = _row_tile(seq, 1024)
    n_q = N_HEADS * HEAD_DIM // LANES
    n_rope = (N_HEADS + N_KV) * HEAD_DIM // LANES
    scale = 1.0 / math.sqrt(HEAD_DIM)

    def body(h_ref, w_ref, cos_ref, sin_ref, o_ref):
        acc = jnp.dot(h_ref[...], w_ref[...], preferred_element_type=F32)
        cs, sn = cos_ref[...], sin_ref[...]
        csq, snq = cs * scale, sn * scale
        for c in range(QKV_W // LANES):
            blk = acc[:, c * LANES:(c + 1) * LANES]
            if c < n_q:
                blk = _rope(blk, csq, snq)
            elif c < n_rope:
                blk = _rope(blk, cs, sn)
            o_ref[:, c * LANES:(c + 1) * LANES] = blk.astype(BF16)

    tab = pl.BlockSpec((tm, LANES), lambda i: (i % (seq // tm), 0))
    return pl.pallas_call(
        body, name=name, grid=(t // tm,),
        in_specs=[pl.BlockSpec((tm, D_MODEL), lambda i: (i, 0)),
                  pl.BlockSpec((D_MODEL, QKV_W), lambda i: (0, group)), tab, tab],
        out_specs=pl.BlockSpec((tm, QKV_W), lambda i: (i, 0)),
        out_shape=jax.ShapeDtypeStruct((t, QKV_W), BF16), compiler_params=_cp(),
    )(h, w, cos, sin)


def _rms_tile(xv, wv):
    return (xv * lax.rsqrt(jnp.mean(xv * xv, axis=-1, keepdims=True) + RMS_EPS)) * wv


def _mm_res(a, w, res, nw, name):
    t, k = a.shape
    tm = _row_tile(t, 512)

    def body(a_ref, w_ref, r_ref, nw_ref, o_ref, h_ref):
        xv = r_ref[...] + jnp.dot(a_ref[...], w_ref[...], preferred_element_type=F32)
        o_ref[...] = xv
        h_ref[...] = _rms_tile(xv, nw_ref[...]).astype(BF16)

    row = pl.BlockSpec((tm, D_MODEL), lambda i: (i, 0))
    return pl.pallas_call(
        body, name=name, grid=(t // tm,),
        in_specs=[pl.BlockSpec((tm, k), lambda i: (i, 0)),
                  pl.BlockSpec((k, D_MODEL), lambda i: (0, 0), pipeline_mode=pl.Buffered(1)), row,
                  pl.BlockSpec((1, D_MODEL), lambda i: (0, 0))],
        out_specs=[row, row],
        out_shape=[jax.ShapeDtypeStruct((t, D_MODEL), F32), jax.ShapeDtypeStruct((t, D_MODEL), BF16)],
        compiler_params=_cp(),
    )(a, w, res, nw)


def _mm_nt(dy, w, group, out_dtype, name):
    t, n = dy.shape
    k = w.shape[0]
    tm = _row_tile(t, 1024)

    def body(dy_ref, w_ref, o_ref):
        o_ref[...] = lax.dot_general(dy_ref[...], w_ref[...], (((1,), (1,)), ((), ())),
                                     preferred_element_type=F32).astype(out_dtype)

    return pl.pallas_call(
        body, name=name, grid=(t // tm,),
        in_specs=[pl.BlockSpec((tm, n), lambda i: (i, 0)), pl.BlockSpec((k, n), lambda i: (0, group))],
        out_specs=pl.BlockSpec((tm, k), lambda i: (i, 0)),
        out_shape=jax.ShapeDtypeStruct((t, k), out_dtype), compiler_params=_cp(),
    )(dy, w)


def _mm_nt_rms(dy, w, x, nw, dres, name):
    t, n = dy.shape
    tm = _row_tile(t, 512)

    def body(dy_ref, w_ref, x_ref, nw_ref, dres_ref, dx_ref, dw_ref):
        dh = lax.dot_general(dy_ref[...], w_ref[...], (((1,), (1,)), ((), ())), preferred_element_type=F32)
        dx, dw = _rms_bwd_tile(x_ref[...], nw_ref[...], dh, dres_ref[...])
        dx_ref[...] = dx
        _accumulate(dw_ref, dw)

    row = pl.BlockSpec((tm, D_MODEL), lambda i: (i, 0))
    vec = pl.BlockSpec((1, D_MODEL), lambda i: (0, 0))
    return pl.pallas_call(
        body, name=name, grid=(t // tm,),
        in_specs=[pl.BlockSpec((tm, n), lambda i: (i, 0)),
                  pl.BlockSpec((D_MODEL, n), lambda i: (0, 0), pipeline_mode=pl.Buffered(1)), row, vec, row],
        out_specs=[row, vec],
        out_shape=[jax.ShapeDtypeStruct((t, D_MODEL), F32), jax.ShapeDtypeStruct((1, D_MODEL), F32)],
        compiler_params=_cp(),
    )(dy, w, x, nw, dres)


def _out_bwd(dx, w, o, name):
    t = dx.shape[0]
    tm = _row_tile(t, 512)

    def body(dx_ref, w_ref, o_ref, et_ref, do_ref, adj_ref):
        do = lax.dot_general(dx_ref[...], w_ref[...], (((1,), (1,)), ((), ())), preferred_element_type=F32)
        do_ref[...] = do.astype(BF16)
        adj_ref[...] = -_dot_split(do * o_ref[...].astype(F32), et_ref[...])

    row = pl.BlockSpec((tm, D_MODEL), lambda i: (i, 0))
    return pl.pallas_call(
        body, name=name, grid=(t // tm,),
        in_specs=[row, pl.BlockSpec((D_MODEL, D_MODEL), lambda i: (0, 0)), row,
                  pl.BlockSpec((D_MODEL, LANES), lambda i: (0, 0))],
        out_specs=[row, pl.BlockSpec((tm, LANES), lambda i: (i, 0))],
        out_shape=[jax.ShapeDtypeStruct((t, D_MODEL), BF16), jax.ShapeDtypeStruct((t, LANES), F32)],
        compiler_params=_cp(),
    )(dx, w, o, _head_expander().T)


def _mm_tn(a, bs, name):
    aq = a.ndim == 3
    bq = bs[0].ndim == 3
    t, ka = a.shape[-2:]
    n = bs[0].shape[-1]
    nq = N_CHIPS if (aq or bq) else 1
    tt = _row_tile(t, GRAD_TOKENS)
    tn = n if n <= 1024 else 768
    assert n % tn == 0
    nb = len(bs)
    steps = t // tt

    def body(*refs):
        a_ref = refs[0]
        b_refs = refs[1:1 + nb]
        o_refs = refs[1 + nb:1 + 2 * nb]
        acc_refs = refs[1 + 2 * nb:]
        s = pl.program_id(2)
        av = a_ref[...]
        for b_ref, o_ref, acc_ref in zip(b_refs, o_refs, acc_refs):
            @pl.when(s == 0)
            def _():
                acc_ref[...] = jnp.zeros_like(acc_ref)

            acc_ref[...] += lax.dot_general(av, b_ref[...], (((0,), (0,)), ((), ())), preferred_element_type=F32)

            @pl.when(s == steps - 1)
            def _():
                o_ref[...] = acc_ref[...].astype(BF16)

    a_spec = (pl.BlockSpec((None, tt, ka), lambda q, j, s: (q, s, 0)) if aq
              else pl.BlockSpec((tt, ka), lambda q, j, s: (s, 0)))
    b_spec = (pl.BlockSpec((None, tt, tn), lambda q, j, s: (q, s, j)) if bq
              else pl.BlockSpec((tt, tn), lambda q, j, s: (s, j)))
    if nq > 1:
        o_spec = pl.BlockSpec((None, ka, tn), lambda q, j, s: (q, 0, j))
        o_shape = jax.ShapeDtypeStruct((nq, ka, n), BF16)
    else:
        o_spec = pl.BlockSpec((ka, tn), lambda q, j, s: (0, j))
        o_shape = jax.ShapeDtypeStruct((ka, n), BF16)
    outs = pl.pallas_call(
        body, name=name, grid=(nq, n // tn, steps),
        in_specs=[a_spec] + [b_spec] * nb, out_specs=[o_spec] * nb, out_shape=[o_shape] * nb,
        scratch_shapes=[pltpu.VMEM((ka, tn), F32)] * nb, compiler_params=_cp(),
    )(a, *bs)
    return outs


def _mm_grad(at, bs, name):
    ka, t = at.shape
    bq = bs[0].ndim == 3
    n = bs[0].shape[-1]
    nq = N_CHIPS if bq else 1
    tt = _row_tile(t, GRAD_TOKENS)
    tn = n if n <= 1024 else 768
    assert n % tn == 0
    nb = len(bs)
    steps = t // tt

    def body(*refs):
        a_ref = refs[0]
        b_refs = refs[1:1 + nb]
        o_refs = refs[1 + nb:1 + 2 * nb]
        acc_refs = refs[1 + 2 * nb:]
        s = pl.program_id(2)
        av = a_ref[...]
        for b_ref, o_ref, acc_ref in zip(b_refs, o_refs, acc_refs):
            @pl.when(s == 0)
            def _():
                acc_ref[...] = jnp.zeros_like(acc_ref)

            acc_ref[...] += jnp.dot(av, b_ref[...], preferred_element_type=F32)

            @pl.when(s == steps - 1)
            def _():
                o_ref[...] = acc_ref[...].astype(BF16)

    a_spec = pl.BlockSpec((ka, tt), lambda q, j, s: (0, s))
    if bq:
        b_spec = pl.BlockSpec((None, tt, tn), lambda q, j, s: (q, s, j))
        o_spec = pl.BlockSpec((None, ka, tn), lambda q, j, s: (q, 0, j))
        o_shape = jax.ShapeDtypeStruct((nq, ka, n), BF16)
    else:
        b_spec = pl.BlockSpec((tt, tn), lambda q, j, s: (s, j))
        o_spec = pl.BlockSpec((ka, tn), lambda q, j, s: (0, j))
        o_shape = jax.ShapeDtypeStruct((ka, n), BF16)
    return pl.pallas_call(
        body, name=name, grid=(nq, n // tn, steps),
        in_specs=[a_spec] + [b_spec] * nb, out_specs=[o_spec] * nb, out_shape=[o_shape] * nb,
        scratch_shapes=[pltpu.VMEM((ka, tn), F32)] * nb, compiler_params=_cp(),
    )(at, *bs)


def _sigmoid(x):
    return 1.0 / (1.0 + jnp.exp(-x))


def _ffn_up(h, wg, wu, layer, name):
    t = h.shape[0]
    tm = _row_tile(t, 1024)
    nt = (((1,), (1,)), ((), ()))

    def body(h_ref, wg_ref, wu_ref, a_ref, dg_ref, du_ref):
        hv = h_ref[...]
        g = lax.dot_general(hv, wg_ref[...], nt, preferred_element_type=F32)
        u = lax.dot_general(hv, wu_ref[...], nt, preferred_element_type=F32)
        sg = _sigmoid(g)
        silu = g * sg
        a_ref[...] = (silu * u).astype(BF16)
        dg_ref[...] = (sg * (1.0 + g * (1.0 - sg)) * u).astype(BF16)
        du_ref[...] = silu.astype(BF16)

    wspec = pl.BlockSpec((None, None, FF_SH, D_MODEL), lambda q, i: (q, layer, 0, 0))
    ospec = pl.BlockSpec((None, tm, FF_SH), lambda q, i: (q, i, 0))
    oshape = jax.ShapeDtypeStruct((N_CHIPS, t, FF_SH), BF16)
    return pl.pallas_call(
        body, name=name, grid=(N_CHIPS, t // tm),
        in_specs=[pl.BlockSpec((tm, D_MODEL), lambda q, i: (i, 0)), wspec, wspec],
        out_specs=[ospec] * 3, out_shape=[oshape] * 3, compiler_params=_cp(),
    )(h, wg, wu)


def _ffn_down(a, wd, res, layer, name, norm_w=None, head=None):
    t = a.shape[1]
    tm = _row_tile(t, 512)
    resident = pl.BlockSpec((N_CHIPS, None, FF_SH, D_MODEL), lambda i: (0, layer, 0, 0), pipeline_mode=pl.Buffered(1))
    row = pl.BlockSpec((tm, D_MODEL), lambda i: (i, 0))
    vec = pl.BlockSpec((1, D_MODEL), lambda i: (0, 0))

    def hidden(a_ref, w_ref, r_ref):
        acc = r_ref[...]
        for q in range(N_CHIPS):
            acc = acc + jnp.dot(a_ref[q], w_ref[q], preferred_element_type=F32)
        return acc

    if head is None:
        def body(a_ref, w_ref, r_ref, nw_ref, o_ref, h_ref):
            xv = hidden(a_ref, w_ref, r_ref)
            o_ref[...] = xv
            h_ref[...] = _rms_tile(xv, nw_ref[...]).astype(BF16)

        return pl.pallas_call(
            body, name=name, grid=(t // tm,),
            in_specs=[pl.BlockSpec((N_CHIPS, tm, FF_SH), lambda i: (0, i, 0)), resident, row, vec],
            out_specs=[row, row],
            out_shape=[jax.ShapeDtypeStruct((t, D_MODEL), F32), jax.ShapeDtypeStruct((t, D_MODEL), BF16)],
            compiler_params=_cp(),
        )(a, wd, res, norm_w)

    def body(a_ref, w_ref, r_ref, nw_ref, t_ref, dx_ref, dxb_ref, l_ref, dw_ref):
        dx, sq, dw = _final_tile(hidden(a_ref, w_ref, r_ref), nw_ref[...], t_ref[...])
        dx_ref[...] = dx
        dxb_ref[...] = dx.astype(BF16)
        _accumulate(l_ref, sq)
        _accumulate(dw_ref, dw)

    return pl.pallas_call(
        body, name=name, grid=(t // tm,),
        in_specs=[pl.BlockSpec((N_CHIPS, tm, FF_SH), lambda i: (0, i, 0)), resident, row, vec, row],
        out_specs=[row, row, vec, vec],
        out_shape=[jax.ShapeDtypeStruct((t, D_MODEL), F32), jax.ShapeDtypeStruct((t, D_MODEL), BF16),
                   jax.ShapeDtypeStruct((1, D_MODEL), F32), jax.ShapeDtypeStruct((1, D_MODEL), F32)],
        compiler_params=_cp(),
    )(a, wd, res, *head)


def _ffn_down_bwd(dx, wd, fg, fu, layer, name):
    t = dx.shape[0]
    tm = _row_tile(t, 512)

    def body(dx_ref, w_ref, fg_ref, fu_ref, dg_ref, du_ref):
        dxv = dx_ref[...]
        for q in range(N_CHIPS):
            da = lax.dot_general(dxv, w_ref[q], (((1,), (1,)), ((), ())), preferred_element_type=F32)
            dg_ref[q] = (da * fg_ref[q].astype(F32)).astype(BF16)
            du_ref[q] = (da * fu_ref[q].astype(F32)).astype(BF16)

    aspec = pl.BlockSpec((N_CHIPS, tm, FF_SH), lambda i: (0, i, 0))
    oshape = jax.ShapeDtypeStruct((N_CHIPS, t, FF_SH), BF16)
    return pl.pallas_call(
        body, name=name, grid=(t // tm,),
        in_specs=[pl.BlockSpec((tm, D_MODEL), lambda i: (i, 0)),
                  pl.BlockSpec((N_CHIPS, None, FF_SH, D_MODEL), lambda i: (0, layer, 0, 0)), aspec, aspec],
        out_specs=[aspec] * 2, out_shape=[oshape] * 2, compiler_params=_cp(),
    )(dx, wd, fg, fu)


def _ffn_up_bwd(dg, du, wg, wu, layer, x, nw, dres, name):
    t = dg.shape[1]
    tm = _row_tile(t, 512)

    def body(dg_ref, du_ref, wg_ref, wu_ref, x_ref, nw_ref, dres_ref, dx_ref, dxb_ref, dw_ref):
        acc = jnp.zeros((tm, D_MODEL), F32)
        for q in range(N_CHIPS):
            acc = acc + jnp.dot(dg_ref[q], wg_ref[q], preferred_element_type=F32)
            acc = acc + jnp.dot(du_ref[q], wu_ref[q], preferred_element_type=F32)
        dx, dw = _rms_bwd_tile(x_ref[...], nw_ref[...], acc, dres_ref[...])
        dx_ref[...] = dx
        dxb_ref[...] = dx.astype(BF16)
        _accumulate(dw_ref, dw)

    aspec = pl.BlockSpec((N_CHIPS, tm, FF_SH), lambda i: (0, i, 0))
    wspec = pl.BlockSpec((N_CHIPS, None, FF_SH, D_MODEL), lambda i: (0, layer, 0, 0), pipeline_mode=pl.Buffered(1))
    row = pl.BlockSpec((tm, D_MODEL), lambda i: (i, 0))
    vec = pl.BlockSpec((1, D_MODEL), lambda i: (0, 0))
    return pl.pallas_call(
        body, name=name, grid=(t // tm,),
        in_specs=[aspec, aspec, wspec, wspec, row, vec, row],
        out_specs=[row, row, vec],
        out_shape=[jax.ShapeDtypeStruct((t, D_MODEL), F32), jax.ShapeDtypeStruct((t, D_MODEL), BF16),
                   jax.ShapeDtypeStruct((1, D_MODEL), F32)],
        compiler_params=_cp(),
    )(dg, du, wg, wu, x, nw, dres)


def _attn_geometry(length, half_window):
    qb = min(LANES, length)
    kw = min(qb + 2 * half_window, length)
    return qb, kw, length // qb


def _dup_kv(src_ref, dst_ref, s, length):
    ch = min(length, 256)
    lo = lax.broadcasted_iota(jnp.int32, (ch, LANES), 1) < HEAD_DIM

    def chunk(c, carry):
        r0 = pl.multiple_of(c * ch, ch)
        for j in range(N_KV // 2):
            tile = src_ref[s, pl.ds(r0, ch), j * LANES:(j + 1) * LANES].astype(F32)
            rolled = pltpu.roll(tile, HEAD_DIM, 1)
            dst_ref[2 * j, pl.ds(r0, ch), :] = jnp.where(lo, tile, rolled).astype(BF16)
            dst_ref[2 * j + 1, pl.ds(r0, ch), :] = jnp.where(lo, rolled, tile).astype(BF16)
        return carry

    lax.fori_loop(0, length // ch, chunk, 0)


def _stack_heads(ref, s, q0, qb, g):
    lo = lax.broadcasted_iota(jnp.int32, (qb, LANES), 1) < HEAD_DIM
    parts = []
    for a in range(4):
        col = (2 * g + a // 2) * LANES
        tile = ref[s, pl.ds(q0, qb), col:col + LANES]
        keep = lo if a % 2 == 0 else jnp.logical_not(lo)
        parts.append(jnp.where(keep, tile, jnp.zeros_like(tile)))
    return jnp.concatenate(parts, axis=0)


def _unstack_pair_t(stacked_t, qb, pair):
    both = jnp.concatenate([stacked_t[:, (2 * pair) * qb:(2 * pair + 1) * qb],
                            stacked_t[:, (2 * pair + 1) * qb:(2 * pair + 2) * qb]], axis=0)
    return both.T


def _band_mask_t(q0, k0, qb, kw, half_window):
    key = lax.broadcasted_iota(jnp.int32, (kw, 4 * qb), 0)
    qry = lax.broadcasted_iota(jnp.int32, (kw, 4 * qb), 1) & (qb - 1)
    return jnp.abs((q0 + qry) - (k0 + key)) <= half_window


def _block_origin(i, qb, kw, half_window, length):
    if isinstance(i, int):
        return i * qb, min(max(i * qb - half_window, 0), length - kw)
    return (pl.multiple_of(i * qb, qb),
            pl.multiple_of(jnp.clip(i * qb - half_window, 0, length - kw), HEAD_DIM))


def _head_row(vals, qb):
    return jnp.concatenate([jnp.broadcast_to(v, (1, qb)).astype(F32) for v in vals], axis=1)


def _attn_fwd(qkv, sink, n_seq, length, half_window, seq_blk, out_dtype, name):
    qb, kw, nblk = _attn_geometry(length, half_window)
    with_sink = sink is not None
    nt = (((1,), (1,)), ((), ()))
    tn = (((0,), (0,)), ((), ()))
    qkv3 = qkv.reshape(n_seq, length, QKV_W)

    def body(*refs):
        refs = list(refs)
        sink_ref = refs.pop(0) if with_sink else None
        q_ref, k_ref, v_ref, o_ref, lse_ref = refs[:5]
        kx_ref, vx_ref = refs[-2:]
        head_row = lax.broadcasted_iota(jnp.int32, (N_HEADS, qb), 0)
        for s in range(seq_blk):
            _dup_kv(k_ref, kx_ref, s, length)
            _dup_kv(v_ref, vx_ref, s, length)

            def block(i, carry):
                q0, k0 = _block_origin(i, qb, kw, half_window, length)
                valid = _band_mask_t(q0, k0, qb, kw, half_window)
                lse_tile = jnp.zeros((N_HEADS, qb), F32)
                groups = range(N_KV)
                sts = [lax.dot_general(kx_ref[g, pl.ds(k0, kw), :], _stack_heads(q_ref, s, q0, qb, g), nt,
                                       preferred_element_type=F32) for g in groups]
                sts = [jnp.where(valid, st, NEG_INF) for st in sts]
                ms = [jnp.max(st, axis=0, keepdims=True) for st in sts]
                if with_sink:
                    sks = [_head_row([sink_ref[4 * g + a] for a in range(4)], qb) for g in groups]
                    ms = [jnp.maximum(m, sk) for m, sk in zip(ms, sks)]
                es = [jnp.exp(st - m) for st, m in zip(sts, ms)]
                dens = [jnp.sum(e, axis=0, keepdims=True) for e in es]
                if with_sink:
                    dens = [den + jnp.exp(sk - m) for den, sk, m in zip(dens, sks, ms)]
                ots = [lax.dot_general(vx_ref[g, pl.ds(k0, kw), 0:HEAD_DIM], es[g].astype(BF16), tn,
                                       preferred_element_type=F32) / dens[g] for g in groups]
                for g in groups:
                    for pair in range(2):
                        col = (2 * g + pair) * LANES
                        o_ref[s, pl.ds(q0, qb), col:col + LANES] = _unstack_pair_t(ots[g], qb, pair).astype(out_dtype)
                    lse = ms[g] + jnp.log(dens[g])
                    for a in range(4):
                        lse_tile = jnp.where(head_row == 4 * g + a, lse[:, a * qb:(a + 1) * qb], lse_tile)
                lse_ref[s, :, pl.ds(q0, qb)] = lse_tile
                return carry

            if nblk == 1:
                block(0, 0)
            else:
                lax.fori_loop(0, nblk, block, 0)

    in_specs = [pl.BlockSpec((seq_blk, length, N_HEADS * HEAD_DIM), lambda n: (n, 0, 0)),
                pl.BlockSpec((seq_blk, length, N_KV * HEAD_DIM), lambda n: (n, 0, 4)),
                pl.BlockSpec((seq_blk, length, N_KV * HEAD_DIM), lambda n: (n, 0, 5))]
    args = [qkv3, qkv3, qkv3]
    if with_sink:
        in_specs.insert(0, pl.BlockSpec(memory_space=pltpu.SMEM))
        args.insert(0, sink)
    out_specs = [pl.BlockSpec((seq_blk, length, D_MODEL), lambda n: (n, 0, 0)),
                 pl.BlockSpec((seq_blk, N_HEADS, length), lambda n: (n, 0, 0))]
    out_shape = [jax.ShapeDtypeStruct((n_seq, length, D_MODEL), out_dtype),
                 jax.ShapeDtypeStruct((n_seq, N_HEADS, length), F32)]
    o, lse = pl.pallas_call(
        body, name=name, grid=(n_seq // seq_blk,), in_specs=in_specs, out_specs=out_specs, out_shape=out_shape,
        scratch_shapes=[pltpu.VMEM((N_KV, length, LANES), BF16), pltpu.VMEM((N_KV, length, LANES), BF16)],
        compiler_params=_cp(),
    )(*args)
    return o.reshape(n_seq * length, D_MODEL), lse


def _attn_bwd(qkv, do, adj, lse, sink, cos, sin, n_seq, length, half_window, seq_blk, dil, name):
    qb, kw, nblk = _attn_geometry(length, half_window)
    scale = 1.0 / math.sqrt(HEAD_DIM)
    with_sink = sink is not None
    nt = (((1,), (1,)), ((), ()))
    tn = (((0,), (0,)), ((), ()))
    qkv3 = qkv.reshape(n_seq, length, QKV_W)
    do3 = do.reshape(n_seq, length, D_MODEL)
    tabs = [t.reshape(dil, length, LANES) for t in (cos, sin)]
    tab_blocks = dil // seq_blk if dil >= seq_blk else 1

    def body(*refs):
        refs = list(refs)
        sink_ref = refs.pop(0) if with_sink else None
        q_ref, k_ref, v_ref, do_ref, aux_ref, lse_ref, cos_ref, sin_ref, dqkv_ref = refs[:9]
        ds_ref = refs[9] if with_sink else None
        kx_ref, vx_ref, dkx_ref, dvx_ref = refs[-4:]
        lane = lax.broadcasted_iota(jnp.int32, (1, LANES), 1)
        if with_sink:
            @pl.when(pl.program_id(0) == 0)
            def _():
                ds_ref[...] = jnp.zeros_like(ds_ref)

        for s in range(seq_blk):
            ts = s % dil
            _dup_kv(k_ref, kx_ref, s, length)
            _dup_kv(v_ref, vx_ref, s, length)
            dkx_ref[...] = jnp.zeros_like(dkx_ref)
            dvx_ref[...] = jnp.zeros_like(dvx_ref)

            def block(i, dsink):
                q0, k0 = _block_origin(i, qb, kw, half_window, length)
                valid = _band_mask_t(q0, k0, qb, kw, half_window)
                cs = cos_ref[ts, pl.ds(q0, qb), :] * scale
                sn = sin_ref[ts, pl.ds(q0, qb), :] * scale
                adj_tile = aux_ref[s, :, pl.ds(q0, qb)]
                lse_tile = lse_ref[s, :, pl.ds(q0, qb)]
                groups = range(N_KV)
                qss = [_stack_heads(q_ref, s, q0, qb, g) for g in groups]
                doss = [_stack_heads(do_ref, s, q0, qb, g) for g in groups]
                kxs = [kx_ref[g, pl.ds(k0, kw), :] for g in groups]
                sts = [lax.dot_general(kxs[g], qss[g], nt, preferred_element_type=F32) for g in groups]
                dpts = [lax.dot_general(vx_ref[g, pl.ds(k0, kw), :], doss[g], nt, preferred_element_type=F32)
                        for g in groups]
                lses = [_head_row([lse_tile[4 * g + a:4 * g + a + 1, :] for a in range(4)], qb) for g in groups]
                shifts = [_head_row([adj_tile[4 * g + a:4 * g + a + 1, :] for a in range(4)], qb) for g in groups]
                pts = [jnp.exp(jnp.where(valid, sts[g], NEG_INF) - lses[g]) for g in groups]
                dsbs = [(pts[g] * (dpts[g] + shifts[g])).astype(BF16) for g in groups]
                pbs = [pt.astype(BF16) for pt in pts]
                if with_sink:
                    for g in groups:
                        sk = _head_row([sink_ref[4 * g + a] for a in range(4)], qb)
                        dsk = jnp.exp(sk - lses[g]) * shifts[g]
                        for a in range(4):
                            tot = jnp.sum(dsk[:, a * qb:(a + 1) * qb], axis=1, keepdims=True)
                            dsink = dsink + jnp.where(lane == 4 * g + a, tot, 0.0)
                dqts = [lax.dot_general(kx_ref[g, pl.ds(k0, kw), 0:HEAD_DIM], dsbs[g], tn, preferred_element_type=F32)
                        for g in groups]
                for g in groups:
                    for pair in range(2):
                        col = (2 * g + pair) * LANES
                        tile = _rope_t(_unstack_pair_t(dqts[g], qb, pair), cs, sn)
                        dqkv_ref[s, pl.ds(q0, qb), col:col + LANES] = tile.astype(BF16)
                for g in groups:
                    dkx_ref[g, pl.ds(k0, kw), :] += jnp.dot(dsbs[g], qss[g], preferred_element_type=F32)
                    dvx_ref[g, pl.ds(k0, kw), :] += jnp.dot(pbs[g], doss[g], preferred_element_type=F32)
                return dsink

            if nblk == 1:
                dsink = block(0, jnp.zeros((1, LANES), F32))
            else:
                dsink = lax.fori_loop(0, nblk, block, jnp.zeros((1, LANES), F32))
            if with_sink:
                ds_ref[0:1, :] += dsink

            ch = min(length, 256)
            lo_c = lax.broadcasted_iota(jnp.int32, (ch, LANES), 1) < HEAD_DIM

            def fin(c, carry):
                r0 = pl.multiple_of(c * ch, ch)
                cs = cos_ref[ts, pl.ds(r0, ch), :]
                sn = sin_ref[ts, pl.ds(r0, ch), :]
                for j in range(N_KV // 2):
                    both = []
                    for acc_ref in (dkx_ref, dvx_ref):
                        t0 = acc_ref[2 * j, pl.ds(r0, ch), :]
                        t1 = acc_ref[2 * j + 1, pl.ds(r0, ch), :]
                        t0 = t0 + pltpu.roll(t0, HEAD_DIM, 1)
                        t1 = t1 + pltpu.roll(t1, HEAD_DIM, 1)
                        both.append(jnp.where(lo_c, t0, t1))
                    kcol = N_HEADS * HEAD_DIM + j * LANES
                    vcol = (N_HEADS + N_KV) * HEAD_DIM + j * LANES
                    dqkv_ref[s, pl.ds(r0, ch), kcol:kcol + LANES] = _rope_t(both[0], cs, sn).astype(BF16)
                    dqkv_ref[s, pl.ds(r0, ch), vcol:vcol + LANES] = both[1].astype(BF16)
                return carry

            lax.fori_loop(0, length // ch, fin, 0)

    seq_map = lambda n: (n, 0, 0)
    tab_map = (lambda n: (n % tab_blocks, 0, 0)) if dil >= seq_blk else (lambda n: (0, 0, 0))
    tab_rows = min(seq_blk, dil)
    in_specs = [pl.BlockSpec((seq_blk, length, N_HEADS * HEAD_DIM), seq_map),
                pl.BlockSpec((seq_blk, length, N_KV * HEAD_DIM), lambda n: (n, 0, 4)),
                pl.BlockSpec((seq_blk, length, N_KV * HEAD_DIM), lambda n: (n, 0, 5)),
                pl.BlockSpec((seq_blk, length, D_MODEL), seq_map),
                pl.BlockSpec((seq_blk, N_HEADS, length), seq_map),
                pl.BlockSpec((seq_blk, N_HEADS, length), seq_map),
                pl.BlockSpec((tab_rows, length, LANES), tab_map),
                pl.BlockSpec((tab_rows, length, LANES), tab_map)]
    args = [qkv3, qkv3, qkv3, do3, adj, lse] + tabs
    if with_sink:
        in_specs.insert(0, pl.BlockSpec(memory_space=pltpu.SMEM))
        args.insert(0, sink)
    out_specs = [pl.BlockSpec((seq_blk, length, QKV_W), seq_map)]
    out_shape = [jax.ShapeDtypeStruct((n_seq, length, QKV_W), BF16)]
    if with_sink:
        out_specs.append(pl.BlockSpec((8, LANES), lambda n: (0, 0)))
        out_shape.append(jax.ShapeDtypeStruct((8, LANES), F32))
    outs = pl.pallas_call(
        body, name=name, grid=(n_seq // seq_blk,), in_specs=in_specs, out_specs=out_specs, out_shape=out_shape,
        scratch_shapes=[pltpu.VMEM((N_KV, length, LANES), BF16), pltpu.VMEM((N_KV, length, LANES), BF16),
                        pltpu.VMEM((N_KV, length, LANES), F32), pltpu.VMEM((N_KV, length, LANES), F32)],
        compiler_params=_cp(),
    )(*args)
    dqkv = outs[0].reshape(n_seq * length, QKV_W)
    return (dqkv, outs[1]) if with_sink else (dqkv, None)


def _head_expander():
    h = jnp.arange(LANES)[:, None]
    l = jnp.arange(D_MODEL)[None, :]
    return (l // HEAD_DIM == h).astype(BF16)


def _dot_split(a, e):
    hi = a.astype(BF16)
    lo = (a - hi.astype(F32)).astype(BF16)
    return jnp.dot(hi, e, preferred_element_type=F32) + jnp.dot(lo, e, preferred_element_type=F32)


def _mix_weights(lses):
    m = jnp.maximum(jnp.maximum(lses[0], lses[1]), lses[2])
    es = [jnp.exp(v - m) for v in lses]
    tot = es[0] + es[1] + es[2]
    return [e / tot for e in es]


def _mix_fwd(os_, lses, name):
    t = os_[0].shape[0]
    tm = _row_tile(t, 512)

    def body(o0, o1, o2, l0, l1, l2, e_ref, out_ref):
        wts = _mix_weights([l0[...], l1[...], l2[...]])
        acc = jnp.zeros((tm, D_MODEL), F32)
        for w, o_ref in zip(wts, (o0, o1, o2)):
            acc = acc + _dot_split(w, e_ref[...]) * o_ref[...]
        out_ref[...] = acc.astype(BF16)

    row = pl.BlockSpec((tm, D_MODEL), lambda i: (i, 0))
    lrow = pl.BlockSpec((tm, LANES), lambda i: (i, 0))
    return pl.pallas_call(
        body, name=name, grid=(t // tm,),
        in_specs=[row] * 3 + [lrow] * 3 + [pl.BlockSpec((LANES, D_MODEL), lambda i: (0, 0))],
        out_specs=row, out_shape=jax.ShapeDtypeStruct((t, D_MODEL), BF16), compiler_params=_cp(),
    )(*os_, *lses, _head_expander())


def _mix_bwd(dmix, os_, lses, name):
    t = dmix.shape[0]
    tm = _row_tile(t, 512)

    def body(d_ref, o0, o1, o2, l0, l1, l2, e_ref, et_ref, do0, do1, do2, a0, a1, a2):
        wts = _mix_weights([l0[...], l1[...], l2[...]])
        dv = d_ref[...].astype(F32)
        cs = [_dot_split(dv * o_ref[...], et_ref[...]) for o_ref in (o0, o1, o2)]
        mean_c = wts[0] * cs[0] + wts[1] * cs[1] + wts[2] * cs[2]
        for w, c, do_ref, a_ref in zip(wts, cs, (do0, do1, do2), (a0, a1, a2)):
            do_ref[...] = (_dot_split(w, e_ref[...]) * dv).astype(BF16)
            a_ref[...] = w * (c - mean_c) - w * c

    row = pl.BlockSpec((tm, D_MODEL), lambda i: (i, 0))
    lrow = pl.BlockSpec((tm, LANES), lambda i: (i, 0))
    e = _head_expander()
    return pl.pallas_call(
        body, name=name, grid=(t // tm,),
        in_specs=[row] * 4 + [lrow] * 3 + [pl.BlockSpec((LANES, D_MODEL), lambda i: (0, 0)),
                                            pl.BlockSpec((D_MODEL, LANES), lambda i: (0, 0))],
        out_specs=[row] * 3 + [lrow] * 3,
        out_shape=[jax.ShapeDtypeStruct((t, D_MODEL), BF16)] * 3 + [jax.ShapeDtypeStruct((t, LANES), F32)] * 3,
        compiler_params=_cp(),
    )(dmix, *os_, *lses, e, e.T)


def _stats_to_tokens(stat, batch, dil):
    n_seq, _, length = stat.shape
    t = stat.transpose(0, 2, 1).reshape(n_seq * length, N_HEADS)
    return _from_residue(jnp.pad(t, ((0, 0), (0, LANES - N_HEADS))), batch, dil)


def _stats_from_tokens(stat, batch, dil, n_seq, length):
    t = _to_residue(stat[:, :N_HEADS], batch, dil)
    return t.reshape(n_seq, length, N_HEADS).transpose(0, 2, 1)


def _group_geometry(batch, seq, dil, window):
    length = seq // dil
    n_seq = batch * dil
    seq_blk = max(1, min(dil, 1024 // length))
    return n_seq, length, (window // 2) // dil, seq_blk


def _local_step(x, target, a_in, a_sink, a_out, b_in, b_out, norm_mix, norm_ffn, wg, wu, wd, final_norm):
    batch, seq, _ = x.shape
    t = batch * seq
    x0 = x.reshape(t, D_MODEL)
    tgt = target.reshape(t, D_MODEL)
    tabs = {d: _rope_tables(seq, d) for _, d in DILATED}
    nm = [norm_mix[i:i + 1] for i in range(2)]
    nf = [norm_ffn[i:i + 1] for i in range(2)]

    h0, h0t = _rms_fwd(x0, nm[0], "rms_mix0", True)
    qkv0 = _qkv_proj(h0, a_in, *tabs[1], 0, "qkv0")
    o0, lse0 = _attn_fwd(qkv0, a_sink, batch, seq, HALF_WINDOW_A, 1, BF16, "attn0")
    x1, hf0 = _mm_res(o0, a_out, x0, nf[0], "out0")
    act0, g0, u0 = _ffn_up(hf0, wg[0], wu[0], 0, "ffn_up0")
    x2, h1 = _ffn_down(act0, wd[0], x1, 0, "ffn_down0", norm_w=nm[1])

    geo = [_group_geometry(batch, seq, d, w) for w, d in DILATED]
    h1g, qkv1, o1, lse1, lse1r = [], [], [], [], []
    for gi, (_, d) in enumerate(DILATED):
        n_seq, length, hw, sb = geo[gi]
        hp = _to_residue(h1, batch, d)
        pj = _qkv_proj(hp, b_in, *tabs[d], gi, f"qkv1_{gi}")
        o, lse = _attn_fwd(pj, None, n_seq, length, hw, sb, BF16, f"attn1_{gi}")
        h1g.append(hp)
        qkv1.append(pj)
        o1.append(_from_residue(o, batch, d))
        lse1r.append(lse)
        lse1.append(_stats_to_tokens(lse, batch, d))
    omix = _mix_fwd(o1, lse1, "mix")
    x3, hf1 = _mm_res(omix, b_out, x2, nf[1], "out1")
    act1, g1, u1 = _ffn_up(hf1, wg[1], wu[1], 0, "ffn_up1")
    dx4, dx4b, loss_cols, d_final = _ffn_down(act1, wd[1], x3, 0, "ffn_down1_loss",
                                                     head=(final_norm.reshape(1, D_MODEL), tgt))

    def ffn_bwd(dxo, dxob, x_mid, hf, g, u, act, layer):
        dg, du = _ffn_down_bwd(dxob, wd[layer], g, u, 0, f"ffn_down_bwd{layer}")
        (d_wd,) = _mm_tn(act, [dxob], f"grad_wd{layer}")
        dxm, dxmb, d_nf = _ffn_up_bwd(dg, du, wg[layer], wu[layer], 0, x_mid, nf[layer], dxo, f"ffn_up_bwd{layer}")
        (d_wgt,) = _mm_tn(dg, [hf], f"grad_wg{layer}")
        (d_wut,) = _mm_tn(du, [hf], f"grad_wu{layer}")
        return dxm, dxmb, d_nf, d_wgt, d_wut, d_wd

    dx3, dx3b, d_nf1, d_wg1, d_wu1, d_wd1 = ffn_bwd(dx4, dx4b, x3, hf1, g1, u1, act1, 1)

    dmix = _mm_nt(dx3b, b_out, 0, BF16, "out1_bwd")
    (d_b_out,) = _mm_tn(omix, [dx3b], "grad_b_out")
    mb = _mix_bwd(dmix, o1, lse1, "mix_bwd")
    dh1, d_b_in = [], []
    for gi, (_, d) in enumerate(DILATED):
        n_seq, length, hw, sb = geo[gi]
        dog = _to_residue(mb[gi], batch, d)
        adj = _stats_from_tokens(mb[3 + gi], batch, d, n_seq, length)
        dpj, _ = _attn_bwd(qkv1[gi], dog, adj, lse1r[gi], None, *tabs[d], n_seq, length, hw, sb, d, f"attn1_bwd{gi}")
        (dw,) = _mm_tn(h1g[gi], [dpj], f"grad_b_in{gi}")
        d_b_in.append(dw)
        dh1.append(_from_residue(_mm_nt(dpj, b_in, gi, BF16, f"qkv1_bwd{gi}"), batch, d))
    dx2, dx2b, d_nm1 = _rms_bwd(x2, nm[1], dh1, dx3, "rms_mix_bwd1")

    dx1, dx1b, d_nf0, d_wg0, d_wu0, d_wd0 = ffn_bwd(dx2, dx2b, x1, hf0, g0, u0, act0, 0)

    do0, adj0 = _out_bwd(dx1b, a_out, o0, "out0_bwd")
    (d_a_out,) = _mm_tn(o0, [dx1b], "grad_a_out")
    adj0 = _stats_from_tokens(adj0, batch, 1, batch, seq)
    dqkv0, d_sink = _attn_bwd(qkv0, do0, adj0, lse0, a_sink, *tabs[1], batch, seq, HALF_WINDOW_A, 1, 1, "attn0_bwd")
    (d_a_in,) = _mm_grad(h0t, [dqkv0], "grad_a_in")
    gx, d_nm0 = _mm_nt_rms(dqkv0, a_in, x0, nm[0], dx1, "qkv0_bwd")

    grads = dict(a_in=d_a_in, a_out=d_a_out, b_in=jnp.concatenate(d_b_in, axis=1), b_out=d_b_out,
                 wg=(d_wg0, d_wg1), wu=(d_wu0, d_wu1), wd=(d_wd0, d_wd1))
    vecs = dict(norm_mix=(d_nm0, d_nm1), norm_ffn=(d_nf0, d_nf1), final=d_final, loss_cols=loss_cols, sink=d_sink)
    return gx.reshape(x.shape), grads, vecs


ANY = pl.BlockSpec(memory_space=pl.ANY)
HBM = pltpu.MemorySpace.HBM


def _me():
    return lax.axis_index("x"), lax.axis_index("y"), lax.axis_index("c")


def _chip_peer(x, y, j):
    px = 1 - x if j & 2 else x
    py = 1 - y if j & 1 else y
    return px, py, 2 * px + py


def _remote(src, dst, sems, k, dev):
    return pltpu.make_async_remote_copy(src_ref=src, dst_ref=dst, send_sem=sems[0].at[k], recv_sem=sems[1].at[k],
                                        device_id=dev, device_id_type=MESH)


def _col_window(ref, q, width):
    return ref.at[:, pl.ds(pl.multiple_of(q * width, LANES), width)]


def _half0(ref, h):
    n = ref.shape[0] // 2
    return ref.at[pl.ds(h * n, n)]


def _half1(ref, h):
    n = ref.shape[1] // 2
    return ref.at[:, pl.ds(h * n, n)]


def _half_rows(ref, h):
    n = ref.shape[-2] // 2
    if len(ref.shape) == 2:
        return ref.at[pl.ds(h * n, n)]
    return ref.at[:, pl.ds(h * n, n)]


def _place_shard(w, layer, q_arr, col, name):
    _, rows, cols = w.shape

    def body(q_ref, w_ref, o_ref):
        o_ref[...] = w_ref[...].astype(BF16)

    if col:
        out_spec = pl.BlockSpec((rows, cols), lambda l, q: (0, q[0]))
        out_shape = jax.ShapeDtypeStruct((rows, N_CHIPS * cols), BF16)
    else:
        out_spec = pl.BlockSpec((None, None, rows, cols), lambda l, q: (q[0], 0, 0, 0))
        out_shape = jax.ShapeDtypeStruct((N_CHIPS, 1, rows, cols), BF16)
    return pl.pallas_call(
        body, name=name,
        grid_spec=pltpu.PrefetchScalarGridSpec(
            num_scalar_prefetch=1, grid=(1,),
            in_specs=[pl.BlockSpec((None, rows, cols), lambda l, q: (layer, 0, 0))], out_specs=out_spec),
        out_shape=out_shape, compiler_params=_cp(),
    )(q_arr, w)


def _handshake(peers):
    barrier = pltpu.get_barrier_semaphore()
    for p in peers:
        pl.semaphore_signal(barrier, inc=1, device_id=p, device_id_type=MESH)
    pl.semaphore_wait(barrier, len(peers))


def _on_sequencer(name, collective_id, n_sem, n_local, body):
    @pl.kernel(mesh=plsc.ScalarSubcoreMesh(axis_name="seq", num_cores=1), name=name,
               scratch_types=(pltpu.SemaphoreType.DMA((n_sem,)), pltpu.SemaphoreType.DMA((n_sem,)),
                              pltpu.SemaphoreType.DMA((max(n_local, 1),))),
               compiler_params=pltpu.CompilerParams(collective_id=collective_id))
    def launch(send_sems, recv_sems, local_sems):
        body((send_sems, recv_sems), local_sems)

    launch()


def _gather_plan(outs, col_fam, sems, handshake):
    n_w = len(outs)
    x, y, c = _me()
    myq = 2 * x + y
    sib = (x, y, 1 - c)
    if handshake:
        _handshake([sib] + [_chip_peer(x, y, j)[:2] + (c,) for j in (1, 2, 3)])

    def slot(w, q):
        if col_fam[w]:
            return _col_window(outs[w], q, outs[w].shape[1] // N_CHIPS)
        return outs[w].at[q]

    first = []
    for w in range(n_w):
        for j in (1, 2, 3):
            px, py, _ = _chip_peer(x, y, j)
            mine = _half_rows(slot(w, myq), c)
            cp = _remote(mine, mine, sems, w * 6 + j - 1, (px, py, c))
            cp.start()
            first.append(cp)
    passed = []
    for w in range(n_w):
        for j in (1, 2, 3):
            _, _, pq = _chip_peer(x, y, j)
            land = _half_rows(slot(w, pq), c)
            _remote(land, land, sems, w * 6 + j - 1, sib).wait_recv()
            cp = _remote(land, land, sems, w * 6 + 2 + j, sib)
            cp.start()
            passed.append(cp)
    for w in range(n_w):
        for j in (1, 2, 3):
            _, _, pq = _chip_peer(x, y, j)
            land = _half_rows(slot(w, pq), 1 - c)
            _remote(land, land, sems, w * 6 + 2 + j, sib).wait_recv()
    for cp in first + passed:
        cp.wait_send()


def _gather_weights(bufs, col_fam):
    n_w = len(bufs)

    def body(*refs):
        _gather_plan(refs[n_w:2 * n_w], col_fam, refs[2 * n_w:2 * n_w + 2], False)

    return pl.pallas_call(
        body, name="gather_weights", in_specs=[ANY] * n_w, out_specs=[ANY] * n_w,
        out_shape=[jax.ShapeDtypeStruct(b.shape, b.dtype) for b in bufs],
        input_output_aliases={w: w for w in range(n_w)},
        scratch_shapes=[pltpu.SemaphoreType.DMA((6 * n_w,)), pltpu.SemaphoreType.DMA((6 * n_w,))],
    )(*bufs)


def _gather_weights_async(bufs, col_fam, name, collective_id):
    refs = [jax.new_ref(b, memory_space=HBM) for b in bufs]
    _on_sequencer(name, collective_id, 6 * len(bufs), 0,
                  lambda sems, _: _gather_plan(refs, col_fam, sems, True))
    return [r[...] for r in refs]


def _grad_half(ref, col, h):
    return _half0(ref, h) if col else _half1(ref, h)


def _swap_halves_with_sibling(grads, col_fam):
    n_w = len(grads)

    def body(*refs):
        _swap_plan(refs[:n_w], refs[n_w:2 * n_w], col_fam, refs[2 * n_w:], False)

    return pl.pallas_call(
        body, name="grad_swap_sibling", in_specs=[ANY] * n_w, out_specs=[ANY] * n_w,
        out_shape=_swap_shapes(grads, col_fam),
        scratch_shapes=[pltpu.SemaphoreType.DMA((n_w,)), pltpu.SemaphoreType.DMA((n_w,))],
    )(*grads)


def _swap_shapes(grads, col_fam):
    out = []
    for w, g in enumerate(grads):
        shp = (g.shape[0] // 2, g.shape[1]) if col_fam[w] else (g.shape[0], g.shape[1] // 2, g.shape[2])
        out.append(jax.ShapeDtypeStruct(shp, g.dtype))
    return out


def _swap_plan(ins, outs, col_fam, sems, handshake):
    x, y, c = _me()
    sib = (x, y, 1 - c)
    if handshake:
        _handshake([sib])
    cps = [_remote(_grad_half(ins[w], col_fam[w], 1 - c), outs[w], sems, w, sib) for w in range(len(ins))]
    for cp in cps:
        cp.start()
    for cp in cps:
        cp.wait_recv()
    for cp in cps:
        cp.wait_send()


def _swap_halves_async(grads, col_fam, name, collective_id):
    srcs = [jax.new_ref(g, memory_space=HBM) for g in grads]
    dsts = [jax.empty_ref(s, memory_space=HBM) for s in _swap_shapes(grads, col_fam)]
    _on_sequencer(name, collective_id, len(grads), 0, lambda sems, _: _swap_plan(srcs, dsts, col_fam, sems, True))
    return [r[...] for r in srcs], [r[...] for r in dsts]


def _half_add(mine, recv, c_arr, col, name):
    if col:
        rows, n = recv.shape
        tr = rows // 2
        grid = (2,)
        in_specs = [pl.BlockSpec((tr, n), lambda i, c: (2 * c[0] + i, 0)), pl.BlockSpec((tr, n), lambda i, c: (i, 0))]
        out_spec = pl.BlockSpec((tr, n), lambda i, c: (i, 0))
    else:
        _, rows, n = recv.shape
        grid = (N_CHIPS,)
        in_specs = [pl.BlockSpec((None, rows, n), lambda q, c: (q, c[0], 0)),
                    pl.BlockSpec((None, rows, n), lambda q, c: (q, 0, 0))]
        out_spec = pl.BlockSpec((None, rows, n), lambda q, c: (q, 0, 0))

    def body(c_ref, a_ref, b_ref, o_ref):
        o_ref[...] = (a_ref[...].astype(F32) + b_ref[...].astype(F32)).astype(BF16)

    return pl.pallas_call(
        body, name=name,
        grid_spec=pltpu.PrefetchScalarGridSpec(num_scalar_prefetch=1, grid=grid, in_specs=in_specs, out_specs=out_spec),
        out_shape=jax.ShapeDtypeStruct(recv.shape, BF16), compiler_params=_cp(),
    )(c_arr, mine, recv)


def _scatter_chip_sums(sums, col_fam):
    n_w = len(sums)

    def body(*refs):
        _scatter_plan(refs[:n_w], refs[n_w:2 * n_w], col_fam, refs[2 * n_w:2 * n_w + 2], refs[2 * n_w + 2], False)

    return pl.pallas_call(
        body, name="grad_scatter_chips", in_specs=[ANY] * n_w, out_specs=[ANY] * n_w,
        out_shape=_scatter_shapes(sums, col_fam),
        scratch_shapes=[pltpu.SemaphoreType.DMA((3 * n_w,)), pltpu.SemaphoreType.DMA((3 * n_w,)),
                        pltpu.SemaphoreType.DMA((n_w,))],
    )(*sums)


def _scatter_shapes(sums, col_fam):
    out = []
    for w, s in enumerate(sums):
        shp = (s.shape[0], s.shape[1] // N_CHIPS) if col_fam[w] else s.shape[1:]
        out.append(jax.ShapeDtypeStruct((N_CHIPS,) + shp, s.dtype))
    return out


def _scatter_plan(ins, outs, col_fam, sems, lsem, handshake):
    n_w = len(ins)
    x, y, c = _me()
    myq = 2 * x + y
    if handshake:
        _handshake([_chip_peer(x, y, j)[:2] + (c,) for j in (1, 2, 3)])

    def slab(w, q):
        if col_fam[w]:
            return _col_window(ins[w], q, ins[w].shape[1] // N_CHIPS)
        return ins[w].at[q]

    local = [pltpu.make_async_copy(slab(w, myq), outs[w].at[myq], lsem.at[w]) for w in range(n_w)]
    for cp in local:
        cp.start()
    cps = []
    for w in range(n_w):
        for j in (1, 2, 3):
            px, py, pq = _chip_peer(x, y, j)
            cp = _remote(slab(w, pq), outs[w].at[myq], sems, w * 3 + j - 1, (px, py, c))
            cp.start()
            cps.append(cp)
    for w in range(n_w):
        for j in (1, 2, 3):
            _, _, pq = _chip_peer(x, y, j)
            land = outs[w].at[pq]
            _remote(land, land, sems, w * 3 + j - 1, (x, y, c)).wait_recv()
    for cp in cps:
        cp.wait_send()
    for cp in local:
        cp.wait()


def _scatter_chip_sums_async(sums, col_fam, name, collective_id):
    srcs = [jax.new_ref(s, memory_space=HBM) for s in sums]
    dsts = [jax.empty_ref(s, memory_space=HBM) for s in _scatter_shapes(sums, col_fam)]
    _on_sequencer(name, collective_id, 3 * len(sums), len(sums),
                  lambda sems, lsem: _scatter_plan(srcs, dsts, col_fam, sems, lsem, True))
    return [r[...] for r in dsts]


def _sum_chips(parts, c_arr, prev, lead, shape, name):
    _, rows, n = parts.shape
    tr = rows // 2 if rows % 32 == 0 else rows
    nblk = rows // tr

    def body(c_ref, p_ref, *rest):
        o_ref = rest[-1]
        acc = p_ref[0].astype(F32)
        for q in range(1, N_CHIPS):
            acc = acc + p_ref[q].astype(F32)
        o_ref[...] = acc

    in_specs = [pl.BlockSpec((N_CHIPS, tr, n), lambda i, c: (0, i, 0))]
    args = [c_arr, parts]
    aliases = {}
    if prev is not None:
        in_specs.append(ANY)
        args.append(prev)
        aliases = {2: 0}
    return pl.pallas_call(
        body, name=name,
        grid_spec=pltpu.PrefetchScalarGridSpec(
            num_scalar_prefetch=1, grid=(nblk,), in_specs=in_specs,
            out_specs=pl.BlockSpec((None, tr, n), lambda i, c: (lead, c[0] * nblk + i, 0))),
        out_shape=jax.ShapeDtypeStruct(shape, F32), input_output_aliases=aliases, compiler_params=_cp(),
    )(*args)


def _join_plan(outs, place, sems, handshake):
    x, y, c = _me()
    sib = (x, y, 1 - c)
    if handshake:
        _handshake([sib])

    def half(k, h):
        o, lead = place[k]
        return _half_rows(outs[o].at[lead], h)

    cps = [_remote(half(k, c), half(k, c), sems, k, sib) for k in range(len(place))]
    for cp in cps:
        cp.start()
    for k in range(len(place)):
        land = half(k, 1 - c)
        _remote(land, land, sems, k, sib).wait_recv()
    for cp in cps:
        cp.wait_send()


def _join_halves(bufs, place, name):
    n_o = len(bufs)
    n_h = len(place)

    def body(*refs):
        _join_plan(refs[n_o:2 * n_o], place, refs[2 * n_o:2 * n_o + 2], False)

    return pl.pallas_call(
        body, name=name, in_specs=[ANY] * n_o, out_specs=[ANY] * n_o,
        out_shape=[jax.ShapeDtypeStruct(b.shape, b.dtype) for b in bufs],
        input_output_aliases={k: k for k in range(n_o)},
        scratch_shapes=[pltpu.SemaphoreType.DMA((n_h,)), pltpu.SemaphoreType.DMA((n_h,))],
    )(*bufs)


def _join_halves_async(bufs, place, name, collective_id):
    refs = [jax.new_ref(b, memory_space=HBM) for b in bufs]
    _on_sequencer(name, collective_id, len(place), 0, lambda sems, _: _join_plan(refs, place, sems, True))
    return [r[...] for r in refs]


def _allreduce_rows(rows):
    n_dev = 8
    n_r = len(rows)
    assert n_r <= 8

    def body(*refs):
        r_refs = refs[:n_r]
        o_ref, slots, send_sems, recv_sems = refs[n_r:]
        x, y, c = _me()
        me = 4 * x + 2 * y + c
        slots[me] = jnp.concatenate([r[...] for r in r_refs] + [jnp.zeros((8 - n_r, D_MODEL), F32)], axis=0)

        def peer(k):
            return (1 - x if k & 4 else x, 1 - y if k & 2 else y, 1 - c if k & 1 else c)

        cps = []
        for k in range(1, n_dev):
            cp = pltpu.make_async_remote_copy(src_ref=slots.at[me], dst_ref=slots.at[me], send_sem=send_sems.at[k - 1],
                                              recv_sem=recv_sems.at[k - 1], device_id=peer(k), device_id_type=MESH)
            cp.start()
            cps.append(cp)
        for k in range(1, n_dev):
            px, py, pc = peer(k)
            land = slots.at[4 * px + 2 * py + pc]
            pltpu.make_async_remote_copy(src_ref=land, dst_ref=land, send_sem=send_sems.at[k - 1],
                                         recv_sem=recv_sems.at[k - 1], device_id=peer(k),
                                         device_id_type=MESH).wait_recv()
        for cp in cps:
            cp.wait_send()
        acc = slots[0]
        for d in range(1, n_dev):
            acc = acc + slots[d]
        o_ref[...] = acc

    vm = pl.BlockSpec(memory_space=pltpu.VMEM)
    return pl.pallas_call(
        body, name="allreduce_rows", in_specs=[vm] * n_r, out_specs=vm,
        out_shape=jax.ShapeDtypeStruct((8, D_MODEL), F32),
        scratch_shapes=[pltpu.VMEM((n_dev, 8, D_MODEL), F32), pltpu.SemaphoreType.DMA((n_dev - 1,)),
                        pltpu.SemaphoreType.DMA((n_dev - 1,))],
    )(*rows)


def _adamw(w, g, m, v, name):
    shape = w.shape
    if len(shape) == 1:
        lead, rows, cols = 1, 1, shape[0]
    else:
        rows, cols = shape[-2:]
        lead = math.prod(shape[:-2])
    args = [a.reshape(lead, rows, cols) for a in (w, g, m, v)]
    tr = rows // 2 if rows % 16 == 0 else rows

    def body(w_ref, g_ref, m_ref, v_ref, d_ref, nm_ref, nv_ref):
        gv = g_ref[...]
        nm = ADAM_B1 * m_ref[...] + (1.0 - ADAM_B1) * gv
        nv = ADAM_B2 * v_ref[...] + (1.0 - ADAM_B2) * jnp.square(gv)
        m_hat = nm / (1.0 - ADAM_B1 ** ADAM_STEP)
        v_hat = nv / (1.0 - ADAM_B2 ** ADAM_STEP)
        d_ref[...] = -ADAM_LR * (m_hat / (jnp.sqrt(v_hat) + ADAM_EPS) + ADAM_WD * w_ref[...])
        nm_ref[...] = nm
        nv_ref[...] = nv

    spec = pl.BlockSpec((None, tr, cols), lambda l, i: (l, i, 0))
    outs = pl.pallas_call(
        body, name=name, grid=(lead, rows // tr), in_specs=[spec] * 4, out_specs=[spec] * 3,
        out_shape=[jax.ShapeDtypeStruct((lead, rows, cols), F32)] * 3, compiler_params=_cp(),
    )(*args)
    return [o.reshape(shape) for o in outs]


def kernel(x, a_w_in, a_sink, a_w_out, b_w_in, b_w_out, norm_mix, norm_ffn, w_gate, w_up, w_down, final_norm, loss_target, m_a_w_in, m_a_sink, m_a_w_out, m_b_w_in, m_b_w_out, m_norm_mix, m_norm_ffn, m_w_gate, m_w_up, m_w_down, m_final_norm, v_a_w_in, v_a_sink, v_a_w_out, v_b_w_in, v_b_w_out, v_norm_mix, v_norm_ffn, v_w_gate, v_w_up, v_w_down, v_final_norm):
    weights = dict(a_w_in=a_w_in, a_sink=a_sink, a_w_out=a_w_out, b_w_in=b_w_in, b_w_out=b_w_out, norm_mix=norm_mix,
                   norm_ffn=norm_ffn, w_gate=w_gate, w_up=w_up, w_down=w_down, final_norm=final_norm)
    mom = dict(a_w_in=m_a_w_in, a_sink=m_a_sink, a_w_out=m_a_w_out, b_w_in=m_b_w_in, b_w_out=m_b_w_out,
               norm_mix=m_norm_mix, norm_ffn=m_norm_ffn, w_gate=m_w_gate, w_up=m_w_up, w_down=m_w_down,
               final_norm=m_final_norm)
    var = dict(a_w_in=v_a_w_in, a_sink=v_a_sink, a_w_out=v_a_w_out, b_w_in=v_b_w_in, b_w_out=v_b_w_out,
               norm_mix=v_norm_mix, norm_ffn=v_norm_ffn, w_gate=v_w_gate, w_up=v_w_up, w_down=v_w_down,
               final_norm=v_final_norm)
    order = ["a_w_in", "a_sink", "a_w_out", "b_w_in", "b_w_out", "norm_mix", "norm_ffn", "w_gate", "w_up", "w_down",
             "final_norm"]
    swapped = ("w_gate", "w_up")
    for n in swapped:
        weights[n], mom[n], var[n] = (a.transpose(0, 2, 1) for a in (weights[n], mom[n], var[n]))
    w_gate_t, w_up_t = weights["w_gate"], weights["w_up"]

    c_arr = lax.axis_index("c").astype(jnp.int32).reshape(1)
    q_arr = (2 * lax.axis_index("x") + lax.axis_index("y")).astype(jnp.int32).reshape(1)

    def placed(w, layer, col, nm):
        return _place_shard(w, layer, q_arr, col, f"place_{nm}")

    (a_in,) = _gather_weights_async([placed(a_w_in, 0, True, "a_in")], (True,), "gather_weights_first", 6)
    a_out, wg0, wu0, wd0 = _gather_weights_async(
        [placed(a_w_out, 0, False, "a_out"), placed(w_gate_t, 0, False, "wg0"), placed(w_up_t, 0, False, "wu0"),
         placed(w_down, 0, False, "wd0")], (False,) * 4, "gather_weights_layer0", 1)
    b_in, b_out, wg1, wu1, wd1 = _gather_weights_async(
        [placed(b_w_in, 0, True, "b_in"), placed(b_w_out, 0, False, "b_out"), placed(w_gate_t, 1, False, "wg1"),
         placed(w_up_t, 1, False, "wu1"), placed(w_down, 1, False, "wd1")], (True,) + (False,) * 4,
        "gather_weights_layer1", 7)
    a_out = a_out.reshape(D_MODEL, D_MODEL)
    b_out = b_out.reshape(D_MODEL, D_MODEL)
    wg, wu, wd = (wg0, wg1), (wu0, wu1), (wd0, wd1)

    gx, grads, vecs = _local_step(x, loss_target, a_in, a_sink[0], a_out, b_in, b_out, norm_mix, norm_ffn, wg, wu, wd,
                                  final_norm)

    rows_out = D_MODEL // N_CHIPS
    partials = [grads["a_in"], grads["b_in"],
                grads["a_out"].reshape(N_CHIPS, rows_out, D_MODEL), grads["b_out"].reshape(N_CHIPS, rows_out, D_MODEL),
                grads["wg"][0], grads["wg"][1], grads["wu"][0], grads["wu"][1], grads["wd"][0], grads["wd"][1]]
    col_fam = (True, True) + (False,) * 8
    names = ("a_in", "b_in", "a_out", "b_out", "wg0", "wg1", "wu0", "wu1", "wd0", "wd1")
    contrib = [None] * len(partials)

    def reduce_group(idx, tag, ids):
        parts = [partials[k] for k in idx]
        cols = tuple(col_fam[k] for k in idx)
        if ids is None:
            theirs = _swap_halves_with_sibling(parts, cols)
        else:
            parts, theirs = _swap_halves_async(parts, cols, f"grad_swap_{tag}", ids[0])
        sums = [_half_add(p, r, c_arr, cf, f"chip_sum_{names[k]}") for p, r, cf, k in zip(parts, theirs, cols, idx)]
        if ids is None:
            out = _scatter_chip_sums(sums, cols)
        else:
            out = _scatter_chip_sums_async(sums, cols, f"grad_scatter_{tag}", ids[1])
        for k, o in zip(idx, out):
            contrib[k] = o

    reduce_group([1, 3, 5, 7, 9], "layer1", (2, 3))
    reduce_group([2, 4, 6, 8], "ffn0", (4, 5))
    reduce_group([0], "a_in", None)
    shapes = [a_w_in.shape, b_w_in.shape, a_w_out.shape, b_w_out.shape, w_down.shape, w_down.shape, w_down.shape]
    place = [(0, 0), (1, 0), (2, 0), (3, 0), (4, 0), (4, 1), (5, 0), (5, 1), (6, 0), (6, 1)]
    bufs = [None] * len(shapes)
    for p, nm, (o, lead) in zip(contrib, names, place):
        bufs[o] = _sum_chips(p, c_arr, bufs[o], lead, shapes[o], f"sum_chips_{nm}")
    g_a_in, g_b_in, g_a_out, g_b_out, g_wg, g_wu, g_wd = _join_halves(bufs, place, "grad_join_sibling")

    sink_row = jnp.pad(vecs["sink"][0:1], ((0, 0), (0, D_MODEL - LANES)))
    tot = _allreduce_rows([vecs["norm_mix"][0], vecs["norm_mix"][1], vecs["norm_ffn"][0], vecs["norm_ffn"][1],
                           vecs["final"], vecs["loss_cols"], sink_row])
    loss = (0.5 / D_MODEL) * jnp.sum(tot[5])
    gw = dict(a_w_in=g_a_in, a_sink=tot[6:7, :N_HEADS], a_w_out=g_a_out, b_w_in=g_b_in, b_w_out=g_b_out,
              norm_mix=tot[0:2], norm_ffn=tot[2:4], w_gate=g_wg, w_up=g_wu, w_down=g_wd, final_norm=tot[4])

    delta, new_m, new_v = {}, {}, {}
    for n in order:
        delta[n], new_m[n], new_v[n] = _adamw(weights[n], gw[n], mom[n], var[n], f"adamw_{n}")
    for n in swapped:
        gw[n], delta[n], new_m[n], new_v[n] = (a.transpose(0, 2, 1) for a in (gw[n], delta[n], new_m[n], new_v[n]))
    return (loss, gx, *[gw[n] for n in order], *[delta[n] for n in order], *[new_m[n] for n in order],
            *[new_v[n] for n in order])
```

```python
import math

import jax
import jax.numpy as jnp
from jax import lax
from jax.experimental import pallas as pl
from jax.experimental.pallas import tpu as pltpu
from jax.experimental.pallas import tpu_sc as plsc

F32 = jnp.float32
BF16 = jnp.bfloat16

D_MODEL = 1024
HEAD_DIM = 64
N_HEADS = 16
N_KV = 4
QKV_W = 1536
D_FF = 2816
N_CHIPS = 4
FF_SH = D_FF // N_CHIPS
HALF_WINDOW_A = 128
DILATED = ((128, 1), (512, 4), (2048, 16))
ROPE_THETA = 10000.0
RMS_EPS = 1e-6
NEG_INF = -1e30
LANES = 128
ADAM_LR, ADAM_B1, ADAM_B2, ADAM_EPS, ADAM_WD, ADAM_STEP = 0.001, 0.9, 0.999, 1e-08, 0.01, 10
VMEM_LIMIT = 56 * 1024 * 1024
GRAD_TOKENS = 2048
MESH = pl.DeviceIdType.MESH


def _cp(**kw):
    return pltpu.CompilerParams(vmem_limit_bytes=VMEM_LIMIT, **kw)


def _row_tile(t, cap):
    tm = min(cap, t)
    assert t % tm == 0
    return tm


def _rope_tables(seq, dil):
    inv = 1.0 / (ROPE_THETA ** (jnp.arange(0, HEAD_DIM, 2, dtype=F32) / HEAD_DIM))
    ang = jnp.arange(seq, dtype=F32)[:, None] * inv[None, :]
    cos, sin = jnp.cos(ang), jnp.sin(ang)
    cos = jnp.tile(cos, (1, 4))
    sin = jnp.concatenate([-sin, sin, -sin, sin], axis=1)

    def perm(t):
        return t.reshape(seq // dil, dil, LANES).transpose(1, 0, 2).reshape(seq, LANES)

    return perm(cos), perm(sin)


def _swap_halves(t):
    lane = lax.broadcasted_iota(jnp.int32, t.shape, 1)
    return jnp.where((lane % HEAD_DIM) < HEAD_DIM // 2, pltpu.roll(t, LANES - 32, 1), pltpu.roll(t, 32, 1))


def _rope(t, cos, sin):
    return t * cos + _swap_halves(t) * sin


def _rope_t(t, cos, sin):
    return t * cos - _swap_halves(t) * sin


def _to_residue(t, batch, dil):
    if dil == 1:
        return t
    s = t.shape[0] // batch
    return t.reshape(batch, s // dil, dil, t.shape[1]).transpose(0, 2, 1, 3).reshape(t.shape)


def _from_residue(t, batch, dil):
    if dil == 1:
        return t
    s = t.shape[0] // batch
    return t.reshape(batch, dil, s // dil, t.shape[1]).transpose(0, 2, 1, 3).reshape(t.shape)


def _rms_fwd(x, w, name):
    t = x.shape[0]
    tm = _row_tile(t, 512)

    def body(x_ref, w_ref, o_ref):
        o_ref[...] = _rms_tile(x_ref[...], w_ref[...]).astype(BF16)

    return pl.pallas_call(
        body, name=name, grid=(t // tm,),
        in_specs=[pl.BlockSpec((tm, D_MODEL), lambda i: (i, 0)), pl.BlockSpec((1, D_MODEL), lambda i: (0, 0))],
        out_specs=pl.BlockSpec((tm, D_MODEL), lambda i: (i, 0)),
        out_shape=jax.ShapeDtypeStruct((t, D_MODEL), BF16), compiler_params=_cp(),
    )(x, w)


def _rms_bwd_tile(xv, wv, dy, dres):
    r = lax.rsqrt(jnp.mean(xv * xv, axis=-1, keepdims=True) + RMS_EPS)
    xh = xv * r
    dxh = dy * wv
    dx = dres + r * (dxh - xh * jnp.mean(dxh * xh, axis=-1, keepdims=True))
    return dx, jnp.sum(dy * xh, axis=0, keepdims=True)


def _accumulate(ref, part):
    @pl.when(pl.program_id(0) == 0)
    def _():
        ref[...] = jnp.zeros_like(ref)

    ref[...] += part


def _rms_bwd(x, w, dhs, dres, name):
    t = x.shape[0]
    tm = _row_tile(t, 512)
    n = len(dhs)

    def body(*refs):
        x_ref, w_ref = refs[0], refs[1]
        dh_refs = refs[2:2 + n]
        dres_ref = refs[2 + n]
        dx_ref, dxb_ref, dw_ref = refs[3 + n:]
        dy = dh_refs[0][...].astype(F32)
        for k in range(1, n):
            dy = dy + dh_refs[k][...].astype(F32)
        dx, dw = _rms_bwd_tile(x_ref[...], w_ref[...], dy, dres_ref[...])
        dx_ref[...] = dx
        dxb_ref[...] = dx.astype(BF16)
        _accumulate(dw_ref, dw)

    row = pl.BlockSpec((tm, D_MODEL), lambda i: (i, 0))
    vec = pl.BlockSpec((1, D_MODEL), lambda i: (0, 0))
    return pl.pallas_call(
        body, name=name, grid=(t // tm,),
        in_specs=[row, vec] + [row] * n + [row],
        out_specs=[row, row, vec],
        out_shape=[jax.ShapeDtypeStruct((t, D_MODEL), F32), jax.ShapeDtypeStruct((t, D_MODEL), BF16),
                   jax.ShapeDtypeStruct((1, D_MODEL), F32)],
        compiler_params=_cp(),
    )(x, w, *dhs, dres)


def _final_tile(xv, wv, tv):
    r = lax.rsqrt(jnp.mean(xv * xv, axis=-1, keepdims=True) + RMS_EPS)
    xh = xv * r
    err = xh * wv - tv
    dy = err * (1.0 / D_MODEL)
    dxh = dy * wv
    dx = r * (dxh - xh * jnp.mean(dxh * xh, axis=-1, keepdims=True))
    return dx, jnp.sum(err * err, axis=0, keepdims=True), jnp.sum(dy * xh, axis=0, keepdims=True)


def _qkv_proj(h, w, cos, sin, group, name):
    t = h.shape[0]
    seq = cos.shape[0]
    tm = _row_tile(seq, 1024)
    n_q = N_HEADS * HEAD_DIM // LANES
    n_rope = (N_HEADS + N_KV) * HEAD_DIM // LANES
    scale = 1.0 / math.sqrt(HEAD_DIM)

    def body(h_ref, w_ref, cos_ref, sin_ref, o_ref):
        acc = jnp.dot(h_ref[...], w_ref[...], preferred_element_type=F32)
        cs, sn = cos_ref[...], sin_ref[...]
        csq, snq = cs * scale, sn * scale
        for c in range(QKV_W // LANES):
            blk = acc[:, c * LANES:(c + 1) * LANES]
            if c < n_q:
                blk = _rope(blk, csq, snq)
            elif c < n_rope:
                blk = _rope(blk, cs, sn)
            o_ref[:, c * LANES:(c + 1) * LANES] = blk.astype(BF16)

    tab = pl.BlockSpec((tm, LANES), lambda i: (i % (seq // tm), 0))
    return pl.pallas_call(
        body, name=name, grid=(t // tm,),
        in_specs=[pl.BlockSpec((tm, D_MODEL), lambda i: (i, 0)),
                  pl.BlockSpec((D_MODEL, QKV_W), lambda i: (0, group)), tab, tab],
        out_specs=pl.BlockSpec((tm, QKV_W), lambda i: (i, 0)),
        out_shape=jax.ShapeDtypeStruct((t, QKV_W), BF16), compiler_params=_cp(),
    )(h, w, cos, sin)


def _rms_tile(xv, wv):
    return (xv * lax.rsqrt(jnp.mean(xv * xv, axis=-1, keepdims=True) + RMS_EPS)) * wv


def _mm_res(a, w, res, nw, name):
    t, k = a.shape
    tm = _row_tile(t, 512)

    def body(a_ref, w_ref, r_ref, nw_ref, o_ref, h_ref):
        xv = r_ref[...] + jnp.dot(a_ref[...], w_ref[...], preferred_element_type=F32)
        o_ref[...] = xv
        h_ref[...] = _rms_tile(xv, nw_ref[...]).astype(BF16)

    row = pl.BlockSpec((tm, D_MODEL), lambda i: (i, 0))
    return pl.pallas_call(
        body, name=name, grid=(t // tm,),
        in_specs=[pl.BlockSpec((tm, k), lambda i: (i, 0)),
                  pl.BlockSpec((k, D_MODEL), lambda i: (0, 0), pipeline_mode=pl.Buffered(1)), row,
                  pl.BlockSpec((1, D_MODEL), lambda i: (0, 0))],
        out_specs=[row, row],
        out_shape=[jax.ShapeDtypeStruct((t, D_MODEL), F32), jax.ShapeDtypeStruct((t, D_MODEL), BF16)],
        compiler_params=_cp(),
    )(a, w, res, nw)


def _mm_nt(dy, w, group, out_dtype, name):
    t, n = dy.shape
    k = w.shape[0]
    tm = _row_tile(t, 1024)

    def body(dy_ref, w_ref, o_ref):
        o_ref[...] = lax.dot_general(dy_ref[...], w_ref[...], (((1,), (1,)), ((), ())),
                                     preferred_element_type=F32).astype(out_dtype)

    return pl.pallas_call(
        body, name=name, grid=(t // tm,),
        in_specs=[pl.BlockSpec((tm, n), lambda i: (i, 0)), pl.BlockSpec((k, n), lambda i: (0, group))],
        out_specs=pl.BlockSpec((tm, k), lambda i: (i, 0)),
        out_shape=jax.ShapeDtypeStruct((t, k), out_dtype), compiler_params=_cp(),
    )(dy, w)


def _mm_nt_rms(dy, w, x, nw, dres, name):
    t, n = dy.shape
    tm = _row_tile(t, 512)

    def body(dy_ref, w_ref, x_ref, nw_ref, dres_ref, dx_ref, dw_ref):
        dh = lax.dot_general(dy_ref[...], w_ref[...], (((1,), (1,)), ((), ())), preferred_element_type=F32)
        dx, dw = _rms_bwd_tile(x_ref[...], nw_ref[...], dh, dres_ref[...])
        dx_ref[...] = dx
        _accumulate(dw_ref, dw)

    row = pl.BlockSpec((tm, D_MODEL), lambda i: (i, 0))
    vec = pl.BlockSpec((1, D_MODEL), lambda i: (0, 0))
    return pl.pallas_call(
        body, name=name, grid=(t // tm,),
        in_specs=[pl.BlockSpec((tm, n), lambda i: (i, 0)),
                  pl.BlockSpec((D_MODEL, n), lambda i: (0, 0), pipeline_mode=pl.Buffered(1)), row, vec, row],
        out_specs=[row, vec],
        out_shape=[jax.ShapeDtypeStruct((t, D_MODEL), F32), jax.ShapeDtypeStruct((1, D_MODEL), F32)],
        compiler_params=_cp(),
    )(dy, w, x, nw, dres)


def _out_bwd(dx, w, o, name):
    t = dx.shape[0]
    tm = _row_tile(t, 512)

    def body(dx_ref, w_ref, o_ref, et_ref, do_ref, adj_ref):
        do = lax.dot_general(dx_ref[...], w_ref[...], (((1,), (1,)), ((), ())), preferred_element_type=F32)
        do_ref[...] = do.astype(BF16)
        adj_ref[...] = -_dot_split(do * o_ref[...].astype(F32), et_ref[...])

    row = pl.BlockSpec((tm, D_MODEL), lambda i: (i, 0))
    return pl.pallas_call(
        body, name=name, grid=(t // tm,),
        in_specs=[row, pl.BlockSpec((D_MODEL, D_MODEL), lambda i: (0, 0)), row,
                  pl.BlockSpec((D_MODEL, LANES), lambda i: (0, 0))],
        out_specs=[row, pl.BlockSpec((tm, LANES), lambda i: (i, 0))],
        out_shape=[jax.ShapeDtypeStruct((t, D_MODEL), BF16), jax.ShapeDtypeStruct((t, LANES), F32)],
        compiler_params=_cp(),
    )(dx, w, o, _head_expander().T)


def _mm_tn(a, bs, name):
    aq = a.ndim == 3
    bq = bs[0].ndim == 3
    t, ka = a.shape[-2:]
    n = bs[0].shape[-1]
    nq = N_CHIPS if (aq or bq) else 1
    tt = _row_tile(t, GRAD_TOKENS)
    tn = n if n <= 1024 else 768
    assert n % tn == 0
    nb = len(bs)
    steps = t // tt

    def body(*refs):
        a_ref = refs[0]
        b_refs = refs[1:1 + nb]
        o_refs = refs[1 + nb:1 + 2 * nb]
        acc_refs = refs[1 + 2 * nb:]
        s = pl.program_id(2)
        av = a_ref[...]
        for b_ref, o_ref, acc_ref in zip(b_refs, o_refs, acc_refs):
            @pl.when(s == 0)
            def _():
                acc_ref[...] = jnp.zeros_like(acc_ref)

            acc_ref[...] += lax.dot_general(av, b_ref[...], (((0,), (0,)), ((), ())), preferred_element_type=F32)

            @pl.when(s == steps - 1)
            def _():
                o_ref[...] = acc_ref[...].astype(BF16)

    a_spec = (pl.BlockSpec((None, tt, ka), lambda q, j, s: (q, s, 0)) if aq
              else pl.BlockSpec((tt, ka), lambda q, j, s: (s, 0)))
    b_spec = (pl.BlockSpec((None, tt, tn), lambda q, j, s: (q, s, j)) if bq
              else pl.BlockSpec((tt, tn), lambda q, j, s: (s, j)))
    if nq > 1:
        o_spec = pl.BlockSpec((None, ka, tn), lambda q, j, s: (q, 0, j))
        o_shape = jax.ShapeDtypeStruct((nq, ka, n), BF16)
    else:
        o_spec = pl.BlockSpec((ka, tn), lambda q, j, s: (0, j))
        o_shape = jax.ShapeDtypeStruct((ka, n), BF16)
    outs = pl.pallas_call(
        body, name=name, grid=(nq, n // tn, steps),
        in_specs=[a_spec] + [b_spec] * nb, out_specs=[o_spec] * nb, out_shape=[o_shape] * nb,
        scratch_shapes=[pltpu.VMEM((ka, tn), F32)] * nb, compiler_params=_cp(),
    )(a, *bs)
    return outs


def _sigmoid(x):
    return 1.0 / (1.0 + jnp.exp(-x))


def _ffn_up(h, wg, wu, layer, name):
    t = h.shape[0]
    tm = _row_tile(t, 1024)
    nt = (((1,), (1,)), ((), ()))

    def body(h_ref, wg_ref, wu_ref, a_ref, dg_ref, du_ref):
        hv = h_ref[...]
        g = lax.dot_general(hv, wg_ref[...], nt, preferred_element_type=F32)
        u = lax.dot_general(hv, wu_ref[...], nt, preferred_element_type=F32)
        sg = _sigmoid(g)
        silu = g * sg
        a_ref[...] = (silu * u).astype(BF16)
        dg_ref[...] = (sg * (1.0 + g * (1.0 - sg)) * u).astype(BF16)
        du_ref[...] = silu.astype(BF16)

    wspec = pl.BlockSpec((None, None, FF_SH, D_MODEL), lambda q, i: (q, layer, 0, 0))
    ospec = pl.BlockSpec((None, tm, FF_SH), lambda q, i: (q, i, 0))
    oshape = jax.ShapeDtypeStruct((N_CHIPS, t, FF_SH), BF16)
    return pl.pallas_call(
        body, name=name, grid=(N_CHIPS, t // tm),
        in_specs=[pl.BlockSpec((tm, D_MODEL), lambda q, i: (i, 0)), wspec, wspec],
        out_specs=[ospec] * 3, out_shape=[oshape] * 3, compiler_params=_cp(),
    )(h, wg, wu)


def _ffn_down(a, wd, res, layer, name, norm_w=None, head=None):
    t = a.shape[1]
    tm = _row_tile(t, 512)
    resident = pl.BlockSpec((N_CHIPS, None, FF_SH, D_MODEL), lambda i: (0, layer, 0, 0), pipeline_mode=pl.Buffered(1))
    row = pl.BlockSpec((tm, D_MODEL), lambda i: (i, 0))
    vec = pl.BlockSpec((1, D_MODEL), lambda i: (0, 0))

    def hidden(a_ref, w_ref, r_ref):
        acc = r_ref[...]
        for q in range(N_CHIPS):
            acc = acc + jnp.dot(a_ref[q], w_ref[q], preferred_element_type=F32)
        return acc

    if head is None:
        def body(a_ref, w_ref, r_ref, nw_ref, o_ref, h_ref):
            xv = hidden(a_ref, w_ref, r_ref)
            o_ref[...] = xv
            h_ref[...] = _rms_tile(xv, nw_ref[...]).astype(BF16)

        return pl.pallas_call(
            body, name=name, grid=(t // tm,),
            in_specs=[pl.BlockSpec((N_CHIPS, tm, FF_SH), lambda i: (0, i, 0)), resident, row, vec],
            out_specs=[row, row],
            out_shape=[jax.ShapeDtypeStruct((t, D_MODEL), F32), jax.ShapeDtypeStruct((t, D_MODEL), BF16)],
            compiler_params=_cp(),
        )(a, wd, res, norm_w)

    def body(a_ref, w_ref, r_ref, nw_ref, t_ref, dx_ref, dxb_ref, l_ref, dw_ref):
        dx, sq, dw = _final_tile(hidden(a_ref, w_ref, r_ref), nw_ref[...], t_ref[...])
        dx_ref[...] = dx
        dxb_ref[...] = dx.astype(BF16)
        _accumulate(l_ref, sq)
        _accumulate(dw_ref, dw)

    return pl.pallas_call(
        body, name=name, grid=(t // tm,),
        in_specs=[pl.BlockSpec((N_CHIPS, tm, FF_SH), lambda i: (0, i, 0)), resident, row, vec, row],
        out_specs=[row, row, vec, vec],
        out_shape=[jax.ShapeDtypeStruct((t, D_MODEL), F32), jax.ShapeDtypeStruct((t, D_MODEL), BF16),
                   jax.ShapeDtypeStruct((1, D_MODEL), F32), jax.ShapeDtypeStruct((1, D_MODEL), F32)],
        compiler_params=_cp(),
    )(a, wd, res, *head)


def _ffn_down_bwd(dx, wd, fg, fu, layer, name):
    t = dx.shape[0]
    tm = _row_tile(t, 512)

    def body(dx_ref, w_ref, fg_ref, fu_ref, dg_ref, du_ref):
        dxv = dx_ref[...]
        for q in range(N_CHIPS):
            da = lax.dot_general(dxv, w_ref[q], (((1,), (1,)), ((), ())), preferred_element_type=F32)
            dg_ref[q] = (da * fg_ref[q].astype(F32)).astype(BF16)
            du_ref[q] = (da * fu_ref[q].astype(F32)).astype(BF16)

    aspec = pl.BlockSpec((N_CHIPS, tm, FF_SH), lambda i: (0, i, 0))
    oshape = jax.ShapeDtypeStruct((N_CHIPS, t, FF_SH), BF16)
    return pl.pallas_call(
        body, name=name, grid=(t // tm,),
        in_specs=[pl.BlockSpec((tm, D_MODEL), lambda i: (i, 0)),
                  pl.BlockSpec((N_CHIPS, None, FF_SH, D_MODEL), lambda i: (0, layer, 0, 0)), aspec, aspec],
        out_specs=[aspec] * 2, out_shape=[oshape] * 2, compiler_params=_cp(),
    )(dx, wd, fg, fu)


def _ffn_up_bwd(dg, du, wg, wu, layer, x, nw, dres, name):
    t = dg.shape[1]
    tm = _row_tile(t, 512)

    def body(dg_ref, du_ref, wg_ref, wu_ref, x_ref, nw_ref, dres_ref, dx_ref, dxb_ref, dw_ref):
        acc = jnp.zeros((tm, D_MODEL), F32)
        for q in range(N_CHIPS):
            acc = acc + jnp.dot(dg_ref[q], wg_ref[q], preferred_element_type=F32)
            acc = acc + jnp.dot(du_ref[q], wu_ref[q], preferred_element_type=F32)
        dx, dw = _rms_bwd_tile(x_ref[...], nw_ref[...], acc, dres_ref[...])
        dx_ref[...] = dx
        dxb_ref[...] = dx.astype(BF16)
        _accumulate(dw_ref, dw)

    aspec = pl.BlockSpec((N_CHIPS, tm, FF_SH), lambda i: (0, i, 0))
    wspec = pl.BlockSpec((N_CHIPS, None, FF_SH, D_MODEL), lambda i: (0, layer, 0, 0), pipeline_mode=pl.Buffered(1))
    row = pl.BlockSpec((tm, D_MODEL), lambda i: (i, 0))
    vec = pl.BlockSpec((1, D_MODEL), lambda i: (0, 0))
    return pl.pallas_call(
        body, name=name, grid=(t // tm,),
        in_specs=[aspec, aspec, wspec, wspec, row, vec, row],
        out_specs=[row, row, vec],
        out_shape=[jax.ShapeDtypeStruct((t, D_MODEL), F32), jax.ShapeDtypeStruct((t, D_MODEL), BF16),
                   jax.ShapeDtypeStruct((1, D_MODEL), F32)],
        compiler_params=_cp(),
    )(dg, du, wg, wu, x, nw, dres)


def _attn_geometry(length, half_window):
    qb = min(LANES, length)
    kw = min(qb + 2 * half_window, length)
    return qb, kw, length // qb


def _dup_kv(src_ref, dst_ref, s, length):
    ch = min(length, 256)
    lo = lax.broadcasted_iota(jnp.int32, (ch, LANES), 1) < HEAD_DIM

    def chunk(c, carry):
        r0 = pl.multiple_of(c * ch, ch)
        for j in range(N_KV // 2):
            tile = src_ref[s, pl.ds(r0, ch), j * LANES:(j + 1) * LANES].astype(F32)
            rolled = pltpu.roll(tile, HEAD_DIM, 1)
            dst_ref[2 * j, pl.ds(r0, ch), :] = jnp.where(lo, tile, rolled).astype(BF16)
            dst_ref[2 * j + 1, pl.ds(r0, ch), :] = jnp.where(lo, rolled, tile).astype(BF16)
        return carry

    lax.fori_loop(0, length // ch, chunk, 0)


def _stack_heads(ref, s, q0, qb, g):
    lo = lax.broadcasted_iota(jnp.int32, (qb, LANES), 1) < HEAD_DIM
    parts = []
    for a in range(4):
        col = (2 * g + a // 2) * LANES
        tile = ref[s, pl.ds(q0, qb), col:col + LANES]
        keep = lo if a % 2 == 0 else jnp.logical_not(lo)
        parts.append(jnp.where(keep, tile, jnp.zeros_like(tile)))
    return jnp.concatenate(parts, axis=0)


def _unstack_pair_t(stacked_t, qb, pair):
    both = jnp.concatenate([stacked_t[:, (2 * pair) * qb:(2 * pair + 1) * qb],
                            stacked_t[:, (2 * pair + 1) * qb:(2 * pair + 2) * qb]], axis=0)
    return both.T


def _band_mask_t(q0, k0, qb, kw, half_window):
    key = lax.broadcasted_iota(jnp.int32, (kw, 4 * qb), 0)
    qry = lax.broadcasted_iota(jnp.int32, (kw, 4 * qb), 1) & (qb - 1)
    return jnp.abs((q0 + qry) - (k0 + key)) <= half_window


def _block_origin(i, qb, kw, half_window, length):
    if isinstance(i, int):
        return i * qb, min(max(i * qb - half_window, 0), length - kw)
    return (pl.multiple_of(i * qb, qb),
            pl.multiple_of(jnp.clip(i * qb - half_window, 0, length - kw), HEAD_DIM))


def _head_row(vals, qb):
    return jnp.concatenate([jnp.broadcast_to(v, (1, qb)).astype(F32) for v in vals], axis=1)


def _attn_fwd(qkv, sink, n_seq, length, half_window, seq_blk, out_dtype, name):
    qb, kw, nblk = _attn_geometry(length, half_window)
    with_sink = sink is not None
    nt = (((1,), (1,)), ((), ()))
    tn = (((0,), (0,)), ((), ()))
    qkv3 = qkv.reshape(n_seq, length, QKV_W)

    def body(*refs):
        refs = list(refs)
        sink_ref = refs.pop(0) if with_sink else None
        q_ref, k_ref, v_ref, o_ref, lse_ref = refs[:5]
        kx_ref, vx_ref = refs[-2:]
        head_row = lax.broadcasted_iota(jnp.int32, (N_HEADS, qb), 0)
        for s in range(seq_blk):
            _dup_kv(k_ref, kx_ref, s, length)
            _dup_kv(v_ref, vx_ref, s, length)

            def block(i, carry):
                q0, k0 = _block_origin(i, qb, kw, half_window, length)
                valid = _band_mask_t(q0, k0, qb, kw, half_window)
                lse_tile = jnp.zeros((N_HEADS, qb), F32)
                groups = range(N_KV)
                sts = [lax.dot_general(kx_ref[g, pl.ds(k0, kw), :], _stack_heads(q_ref, s, q0, qb, g), nt,
                                       preferred_element_type=F32) for g in groups]
                sts = [jnp.where(valid, st, NEG_INF) for st in sts]
                ms = [jnp.max(st, axis=0, keepdims=True) for st in sts]
                if with_sink:
                    sks = [_head_row([sink_ref[4 * g + a] for a in range(4)], qb) for g in groups]
                    ms = [jnp.maximum(m, sk) for m, sk in zip(ms, sks)]
                es = [jnp.exp(st - m) for st, m in zip(sts, ms)]
                dens = [jnp.sum(e, axis=0, keepdims=True) for e in es]
                if with_sink:
                    dens = [den + jnp.exp(sk - m) for den, sk, m in zip(dens, sks, ms)]
                ots = [lax.dot_general(vx_ref[g, pl.ds(k0, kw), 0:HEAD_DIM], es[g].astype(BF16), tn,
                                       preferred_element_type=F32) / dens[g] for g in groups]
                for g in groups:
                    for pair in range(2):
                        col = (2 * g + pair) * LANES
                        o_ref[s, pl.ds(q0, qb), col:col + LANES] = _unstack_pair_t(ots[g], qb, pair).astype(out_dtype)
                    lse = ms[g] + jnp.log(dens[g])
                    for a in range(4):
                        lse_tile = jnp.where(head_row == 4 * g + a, lse[:, a * qb:(a + 1) * qb], lse_tile)
                lse_ref[s, :, pl.ds(q0, qb)] = lse_tile
                return carry

            if nblk == 1:
                block(0, 0)
            else:
                lax.fori_loop(0, nblk, block, 0)

    in_specs = [pl.BlockSpec((seq_blk, length, N_HEADS * HEAD_DIM), lambda n: (n, 0, 0)),
                pl.BlockSpec((seq_blk, length, N_KV * HEAD_DIM), lambda n: (n, 0, 4)),
                pl.BlockSpec((seq_blk, length, N_KV * HEAD_DIM), lambda n: (n, 0, 5))]
    args = [qkv3, qkv3, qkv3]
    if with_sink:
        in_specs.insert(0, pl.BlockSpec(memory_space=pltpu.SMEM))
        args.insert(0, sink)
    out_specs = [pl.BlockSpec((seq_blk, length, D_MODEL), lambda n: (n, 0, 0)),
                 pl.BlockSpec((seq_blk, N_HEADS, length), lambda n: (n, 0, 0))]
    out_shape = [jax.ShapeDtypeStruct((n_seq, length, D_MODEL), out_dtype),
                 jax.ShapeDtypeStruct((n_seq, N_HEADS, length), F32)]
    o, lse = pl.pallas_call(
        body, name=name, grid=(n_seq // seq_blk,), in_specs=in_specs, out_specs=out_specs, out_shape=out_shape,
        scratch_shapes=[pltpu.VMEM((N_KV, length, LANES), BF16), pltpu.VMEM((N_KV, length, LANES), BF16)],
        compiler_params=_cp(),
    )(*args)
    return o.reshape(n_seq * length, D_MODEL), lse


def _attn_bwd(qkv, do, adj, lse, sink, cos, sin, n_seq, length, half_window, seq_blk, dil, name):
    qb, kw, nblk = _attn_geometry(length, half_window)
    scale = 1.0 / math.sqrt(HEAD_DIM)
    with_sink = sink is not None
    nt = (((1,), (1,)), ((), ()))
    tn = (((0,), (0,)), ((), ()))
    qkv3 = qkv.reshape(n_seq, length, QKV_W)
    do3 = do.reshape(n_seq, length, D_MODEL)
    tabs = [t.reshape(dil, length, LANES) for t in (cos, sin)]
    tab_blocks = dil // seq_blk if dil >= seq_blk else 1

    def body(*refs):
        refs = list(refs)
        sink_ref = refs.pop(0) if with_sink else None
        q_ref, k_ref, v_ref, do_ref, aux_ref, lse_ref, cos_ref, sin_ref, dqkv_ref = refs[:9]
        ds_ref = refs[9] if with_sink else None
        kx_ref, vx_ref, dkx_ref, dvx_ref = refs[-4:]
        lane = lax.broadcasted_iota(jnp.int32, (1, LANES), 1)
        if with_sink:
            @pl.when(pl.program_id(0) == 0)
            def _():
                ds_ref[...] = jnp.zeros_like(ds_ref)

        for s in range(seq_blk):
            ts = s % dil
            _dup_kv(k_ref, kx_ref, s, length)
            _dup_kv(v_ref, vx_ref, s, length)
            dkx_ref[...] = jnp.zeros_like(dkx_ref)
            dvx_ref[...] = jnp.zeros_like(dvx_ref)

            def block(i, dsink):
                q0, k0 = _block_origin(i, qb, kw, half_window, length)
                valid = _band_mask_t(q0, k0, qb, kw, half_window)
                cs = cos_ref[ts, pl.ds(q0, qb), :] * scale
                sn = sin_ref[ts, pl.ds(q0, qb), :] * scale
                adj_tile = aux_ref[s, :, pl.ds(q0, qb)]
                lse_tile = lse_ref[s, :, pl.ds(q0, qb)]
                groups = range(N_KV)
                qss = [_stack_heads(q_ref, s, q0, qb, g) for g in groups]
                doss = [_stack_heads(do_ref, s, q0, qb, g) for g in groups]
                kxs = [kx_ref[g, pl.ds(k0, kw), :] for g in groups]
                sts = [lax.dot_general(kxs[g], qss[g], nt, preferred_element_type=F32) for g in groups]
                dpts = [lax.dot_general(vx_ref[g, pl.ds(k0, kw), :], doss[g], nt, preferred_element_type=F32)
                        for g in groups]
                lses = [_head_row([lse_tile[4 * g + a:4 * g + a + 1, :] for a in range(4)], qb) for g in groups]
                shifts = [_head_row([adj_tile[4 * g + a:4 * g + a + 1, :] for a in range(4)], qb) for g in groups]
                pts = [jnp.exp(jnp.where(valid, sts[g], NEG_INF) - lses[g]) for g in groups]
                dsbs = [(pts[g] * (dpts[g] + shifts[g])).astype(BF16) for g in groups]
                pbs = [pt.astype(BF16) for pt in pts]
                if with_sink:
                    for g in groups:
                        sk = _head_row([sink_ref[4 * g + a] for a in range(4)], qb)
                        dsk = jnp.exp(sk - lses[g]) * shifts[g]
                        for a in range(4):
                            tot = jnp.sum(dsk[:, a * qb:(a + 1) * qb], axis=1, keepdims=True)
                            dsink = dsink + jnp.where(lane == 4 * g + a, tot, 0.0)
                dqts = [lax.dot_general(kx_ref[g, pl.ds(k0, kw), 0:HEAD_DIM], dsbs[g], tn, preferred_element_type=F32)
                        for g in groups]
                for g in groups:
                    for pair in range(2):
                        col = (2 * g + pair) * LANES
                        tile = _rope_t(_unstack_pair_t(dqts[g], qb, pair), cs, sn)
                        dqkv_ref[s, pl.ds(q0, qb), col:col + LANES] = tile.astype(BF16)
                for g in groups:
                    dkx_ref[g, pl.ds(k0, kw), :] += jnp.dot(dsbs[g], qss[g], preferred_element_type=F32)
                    dvx_ref[g, pl.ds(k0, kw), :] += jnp.dot(pbs[g], doss[g], preferred_element_type=F32)
                return dsink

            if nblk == 1:
                dsink = block(0, jnp.zeros((1, LANES), F32))
            else:
                dsink = lax.fori_loop(0, nblk, block, jnp.zeros((1, LANES), F32))
            if with_sink:
                ds_ref[0:1, :] += dsink

            ch = min(length, 256)
            lo_c = lax.broadcasted_iota(jnp.int32, (ch, LANES), 1) < HEAD_DIM

            def fin(c, carry):
                r0 = pl.multiple_of(c * ch, ch)
                cs = cos_ref[ts, pl.ds(r0, ch), :]
                sn = sin_ref[ts, pl.ds(r0, ch), :]
                for j in range(N_KV // 2):
                    both = []
                    for acc_ref in (dkx_ref, dvx_ref):
                        t0 = acc_ref[2 * j, pl.ds(r0, ch), :]
                        t1 = acc_ref[2 * j + 1, pl.ds(r0, ch), :]
                        t0 = t0 + pltpu.roll(t0, HEAD_DIM, 1)
                        t1 = t1 + pltpu.roll(t1, HEAD_DIM, 1)
                        both.append(jnp.where(lo_c, t0, t1))
                    kcol = N_HEADS * HEAD_DIM + j * LANES
                    vcol = (N_HEADS + N_KV) * HEAD_DIM + j * LANES
                    dqkv_ref[s, pl.ds(r0, ch), kcol:kcol + LANES] = _rope_t(both[0], cs, sn).astype(BF16)
                    dqkv_ref[s, pl.ds(r0, ch), vcol:vcol + LANES] = both[1].astype(BF16)
                return carry

            lax.fori_loop(0, length // ch, fin, 0)

    seq_map = lambda n: (n, 0, 0)
    tab_map = (lambda n: (n % tab_blocks, 0, 0)) if dil >= seq_blk else (lambda n: (0, 0, 0))
    tab_rows = min(seq_blk, dil)
    in_specs = [pl.BlockSpec((seq_blk, length, N_HEADS * HEAD_DIM), seq_map),
                pl.BlockSpec((seq_blk, length, N_KV * HEAD_DIM), lambda n: (n, 0, 4)),
                pl.BlockSpec((seq_blk, length, N_KV * HEAD_DIM), lambda n: (n, 0, 5)),
                pl.BlockSpec((seq_blk, length, D_MODEL), seq_map),
                pl.BlockSpec((seq_blk, N_HEADS, length), seq_map),
                pl.BlockSpec((seq_blk, N_HEADS, length), seq_map),
                pl.BlockSpec((tab_rows, length, LANES), tab_map),
                pl.BlockSpec((tab_rows, length, LANES), tab_map)]
    args = [qkv3, qkv3, qkv3, do3, adj, lse] + tabs
    if with_sink:
        in_specs.insert(0, pl.BlockSpec(memory_space=pltpu.SMEM))
        args.insert(0, sink)
    out_specs = [pl.BlockSpec((seq_blk, length, QKV_W), seq_map)]
    out_shape = [jax.ShapeDtypeStruct((n_seq, length, QKV_W), BF16)]
    if with_sink:
        out_specs.append(pl.BlockSpec((8, LANES), lambda n: (0, 0)))
        out_shape.append(jax.ShapeDtypeStruct((8, LANES), F32))
    outs = pl.pallas_call(
        body, name=name, grid=(n_seq // seq_blk,), in_specs=in_specs, out_specs=out_specs, out_shape=out_shape,
        scratch_shapes=[pltpu.VMEM((N_KV, length, LANES), BF16), pltpu.VMEM((N_KV, length, LANES), BF16),
                        pltpu.VMEM((N_KV, length, LANES), F32), pltpu.VMEM((N_KV, length, LANES), F32)],
        compiler_params=_cp(),
    )(*args)
    dqkv = outs[0].reshape(n_seq * length, QKV_W)
    return (dqkv, outs[1]) if with_sink else (dqkv, None)


def _head_expander():
    h = jnp.arange(LANES)[:, None]
    l = jnp.arange(D_MODEL)[None, :]
    return (l // HEAD_DIM == h).astype(BF16)


def _dot_split(a, e):
    hi = a.astype(BF16)
    lo = (a - hi.astype(F32)).astype(BF16)
    return jnp.dot(hi, e, preferred_element_type=F32) + jnp.dot(lo, e, preferred_element_type=F32)


def _mix_weights(lses):
    m = jnp.maximum(jnp.maximum(lses[0], lses[1]), lses[2])
    es = [jnp.exp(v - m) for v in lses]
    tot = es[0] + es[1] + es[2]
    return [e / tot for e in es]


def _mix_fwd(os_, lses, name):
    t = os_[0].shape[0]
    tm = _row_tile(t, 512)

    def body(o0, o1, o2, l0, l1, l2, e_ref, out_ref):
        wts = _mix_weights([l0[...], l1[...], l2[...]])
        acc = jnp.zeros((tm, D_MODEL), F32)
        for w, o_ref in zip(wts, (o0, o1, o2)):
            acc = acc + _dot_split(w, e_ref[...]) * o_ref[...]
        out_ref[...] = acc.astype(BF16)

    row = pl.BlockSpec((tm, D_MODEL), lambda i: (i, 0))
    lrow = pl.BlockSpec((tm, LANES), lambda i: (i, 0))
    return pl.pallas_call(
        body, name=name, grid=(t // tm,),
        in_specs=[row] * 3 + [lrow] * 3 + [pl.BlockSpec((LANES, D_MODEL), lambda i: (0, 0))],
        out_specs=row, out_shape=jax.ShapeDtypeStruct((t, D_MODEL), BF16), compiler_params=_cp(),
    )(*os_, *lses, _head_expander())


def _mix_bwd(dmix, os_, lses, name):
    t = dmix.shape[0]
    tm = _row_tile(t, 512)

    def body(d_ref, o0, o1, o2, l0, l1, l2, e_ref, et_ref, do0, do1, do2, a0, a1, a2):
        wts = _mix_weights([l0[...], l1[...], l2[...]])
        dv = d_ref[...].astype(F32)
        cs = [_dot_split(dv * o_ref[...], et_ref[...]) for o_ref in (o0, o1, o2)]
        mean_c = wts[0] * cs[0] + wts[1] * cs[1] + wts[2] * cs[2]
        for w, c, do_ref, a_ref in zip(wts, cs, (do0, do1, do2), (a0, a1, a2)):
            do_ref[...] = (_dot_split(w, e_ref[...]) * dv).astype(BF16)
            a_ref[...] = w * (c - mean_c) - w * c

    row = pl.BlockSpec((tm, D_MODEL), lambda i: (i, 0))
    lrow = pl.BlockSpec((tm, LANES), lambda i: (i, 0))
    e = _head_expander()
    return pl.pallas_call(
        body, name=name, grid=(t // tm,),
        in_specs=[row] * 4 + [lrow] * 3 + [pl.BlockSpec((LANES, D_MODEL), lambda i: (0, 0)),
                                            pl.BlockSpec((D_MODEL, LANES), lambda i: (0, 0))],
        out_specs=[row] * 3 + [lrow] * 3,
        out_shape=[jax.ShapeDtypeStruct((t, D_MODEL), BF16)] * 3 + [jax.ShapeDtypeStruct((t, LANES), F32)] * 3,
        compiler_params=_cp(),
    )(dmix, *os_, *lses, e, e.T)


def _stats_to_tokens(stat, batch, dil):
    n_seq, _, length = stat.shape
    t = stat.transpose(0, 2, 1).reshape(n_seq * length, N_HEADS)
    return _from_residue(jnp.pad(t, ((0, 0), (0, LANES - N_HEADS))), batch, dil)


def _stats_from_tokens(stat, batch, dil, n_seq, length):
    t = _to_residue(stat[:, :N_HEADS], batch, dil)
    return t.reshape(n_seq, length, N_HEADS).transpose(0, 2, 1)


def _group_geometry(batch, seq, dil, window):
    length = seq // dil
    n_seq = batch * dil
    seq_blk = max(1, min(dil, 1024 // length))
    return n_seq, length, (window // 2) // dil, seq_blk


def _local_step(x, target, a_in, a_sink, a_out, b_in, b_out, norm_mix, norm_ffn, wg, wu, wd, final_norm):
    batch, seq, _ = x.shape
    t = batch * seq
    x0 = x.reshape(t, D_MODEL)
    tgt = target.reshape(t, D_MODEL)
    tabs = {d: _rope_tables(seq, d) for _, d in DILATED}
    nm = [norm_mix[i:i + 1] for i in range(2)]
    nf = [norm_ffn[i:i + 1] for i in range(2)]

    h0 = _rms_fwd(x0, nm[0], "rms_mix0")
    qkv0 = _qkv_proj(h0, a_in, *tabs[1], 0, "qkv0")
    o0, lse0 = _attn_fwd(qkv0, a_sink, batch, seq, HALF_WINDOW_A, 1, BF16, "attn0")
    x1, hf0 = _mm_res(o0, a_out, x0, nf[0], "out0")
    act0, g0, u0 = _ffn_up(hf0, wg[0], wu[0], 0, "ffn_up0")
    x2, h1 = _ffn_down(act0, wd[0], x1, 0, "ffn_down0", norm_w=nm[1])

    geo = [_group_geometry(batch, seq, d, w) for w, d in DILATED]
    h1g, qkv1, o1, lse1, lse1r = [], [], [], [], []
    for gi, (_, d) in enumerate(DILATED):
        n_seq, length, hw, sb = geo[gi]
        hp = _to_residue(h1, batch, d)
        pj = _qkv_proj(hp, b_in, *tabs[d], gi, f"qkv1_{gi}")
        o, lse = _attn_fwd(pj, None, n_seq, length, hw, sb, BF16, f"attn1_{gi}")
        h1g.append(hp)
        qkv1.append(pj)
        o1.append(_from_residue(o, batch, d))
        lse1r.append(lse)
        lse1.append(_stats_to_tokens(lse, batch, d))
    omix = _mix_fwd(o1, lse1, "mix")
    x3, hf1 = _mm_res(omix, b_out, x2, nf[1], "out1")
    act1, g1, u1 = _ffn_up(hf1, wg[1], wu[1], 0, "ffn_up1")
    dx4, dx4b, loss_cols, d_final = _ffn_down(act1, wd[1], x3, 0, "ffn_down1_loss",
                                                     head=(final_norm.reshape(1, D_MODEL), tgt))

    def ffn_bwd(dxo, dxob, x_mid, hf, g, u, act, layer):
        dg, du = _ffn_down_bwd(dxob, wd[layer], g, u, 0, f"ffn_down_bwd{layer}")
        (d_wd,) = _mm_tn(act, [dxob], f"grad_wd{layer}")
        dxm, dxmb, d_nf = _ffn_up_bwd(dg, du, wg[layer], wu[layer], 0, x_mid, nf[layer], dxo, f"ffn_up_bwd{layer}")
        (d_wgt,) = _mm_tn(dg, [hf], f"grad_wg{layer}")
        (d_wut,) = _mm_tn(du, [hf], f"grad_wu{layer}")
        return dxm, dxmb, d_nf, d_wgt, d_wut, d_wd

    dx3, dx3b, d_nf1, d_wg1, d_wu1, d_wd1 = ffn_bwd(dx4, dx4b, x3, hf1, g1, u1, act1, 1)

    dmix = _mm_nt(dx3b, b_out, 0, BF16, "out1_bwd")
    (d_b_out,) = _mm_tn(omix, [dx3b], "grad_b_out")
    mb = _mix_bwd(dmix, o1, lse1, "mix_bwd")
    dh1, d_b_in = [], []
    for gi, (_, d) in enumerate(DILATED):
        n_seq, length, hw, sb = geo[gi]
        dog = _to_residue(mb[gi], batch, d)
        adj = _stats_from_tokens(mb[3 + gi], batch, d, n_seq, length)
        dpj, _ = _attn_bwd(qkv1[gi], dog, adj, lse1r[gi], None, *tabs[d], n_seq, length, hw, sb, d, f"attn1_bwd{gi}")
        (dw,) = _mm_tn(h1g[gi], [dpj], f"grad_b_in{gi}")
        d_b_in.append(dw)
        dh1.append(_from_residue(_mm_nt(dpj, b_in, gi, BF16, f"qkv1_bwd{gi}"), batch, d))
    dx2, dx2b, d_nm1 = _rms_bwd(x2, nm[1], dh1, dx3, "rms_mix_bwd1")

    dx1, dx1b, d_nf0, d_wg0, d_wu0, d_wd0 = ffn_bwd(dx2, dx2b, x1, hf0, g0, u0, act0, 0)

    do0, adj0 = _out_bwd(dx1b, a_out, o0, "out0_bwd")
    (d_a_out,) = _mm_tn(o0, [dx1b], "grad_a_out")
    adj0 = _stats_from_tokens(adj0, batch, 1, batch, seq)
    dqkv0, d_sink = _attn_bwd(qkv0, do0, adj0, lse0, a_sink, *tabs[1], batch, seq, HALF_WINDOW_A, 1, 1, "attn0_bwd")
    (d_a_in,) = _mm_tn(h0, [dqkv0], "grad_a_in")
    gx, d_nm0 = _mm_nt_rms(dqkv0, a_in, x0, nm[0], dx1, "qkv0_bwd")

    grads = dict(a_in=d_a_in, a_out=d_a_out, b_in=jnp.concatenate(d_b_in, axis=1), b_out=d_b_out,
                 wg=(d_wg0, d_wg1), wu=(d_wu0, d_wu1), wd=(d_wd0, d_wd1))
    vecs = dict(norm_mix=(d_nm0, d_nm1), norm_ffn=(d_nf0, d_nf1), final=d_final, loss_cols=loss_cols, sink=d_sink)
    return gx.reshape(x.shape), grads, vecs


ANY = pl.BlockSpec(memory_space=pl.ANY)
HBM = pltpu.MemorySpace.HBM


def _me():
    return lax.axis_index("x"), lax.axis_index("y"), lax.axis_index("c")


def _chip_peer(x, y, j):
    px = 1 - x if j & 2 else x
    py = 1 - y if j & 1 else y
    return px, py, 2 * px + py


def _remote(src, dst, sems, k, dev):
    return pltpu.make_async_remote_copy(src_ref=src, dst_ref=dst, send_sem=sems[0].at[k], recv_sem=sems[1].at[k],
                                        device_id=dev, device_id_type=MESH)


def _col_window(ref, q, width):
    return ref.at[:, pl.ds(pl.multiple_of(q * width, LANES), width)]


def _half0(ref, h):
    n = ref.shape[0] // 2
    return ref.at[pl.ds(h * n, n)]


def _half1(ref, h):
    n = ref.shape[1] // 2
    return ref.at[:, pl.ds(h * n, n)]


def _half_rows(ref, h):
    n = ref.shape[-2] // 2
    if len(ref.shape) == 2:
        return ref.at[pl.ds(h * n, n)]
    return ref.at[:, pl.ds(h * n, n)]


def _place_shard(w, layer, q_arr, col, name):
    _, rows, cols = w.shape

    def body(q_ref, w_ref, o_ref):
        o_ref[...] = w_ref[...].astype(BF16)

    if col:
        out_spec = pl.BlockSpec((rows, cols), lambda l, q: (0, q[0]))
        out_shape = jax.ShapeDtypeStruct((rows, N_CHIPS * cols), BF16)
    else:
        out_spec = pl.BlockSpec((None, None, rows, cols), lambda l, q: (q[0], 0, 0, 0))
        out_shape = jax.ShapeDtypeStruct((N_CHIPS, 1, rows, cols), BF16)
    return pl.pallas_call(
        body, name=name,
        grid_spec=pltpu.PrefetchScalarGridSpec(
            num_scalar_prefetch=1, grid=(1,),
            in_specs=[pl.BlockSpec((None, rows, cols), lambda l, q: (layer, 0, 0))], out_specs=out_spec),
        out_shape=out_shape, compiler_params=_cp(),
    )(q_arr, w)


def _handshake(peers):
    barrier = pltpu.get_barrier_semaphore()
    for p in peers:
        pl.semaphore_signal(barrier, inc=1, device_id=p, device_id_type=MESH)
    pl.semaphore_wait(barrier, len(peers))


def _on_sequencer(name, collective_id, n_sem, n_local, body):
    @pl.kernel(mesh=plsc.ScalarSubcoreMesh(axis_name="seq", num_cores=1), name=name,
               scratch_types=(pltpu.SemaphoreType.DMA((n_sem,)), pltpu.SemaphoreType.DMA((n_sem,)),
                              pltpu.SemaphoreType.DMA((max(n_local, 1),))),
               compiler_params=pltpu.CompilerParams(collective_id=collective_id))
    def launch(send_sems, recv_sems, local_sems):
        body((send_sems, recv_sems), local_sems)

    launch()


def _gather_plan(outs, col_fam, sems, handshake):
    n_w = len(outs)
    x, y, c = _me()
    myq = 2 * x + y
    sib = (x, y, 1 - c)
    if handshake:
        _handshake([sib] + [_chip_peer(x, y, j)[:2] + (c,) for j in (1, 2, 3)])

    def slot(w, q):
        if col_fam[w]:
            return _col_window(outs[w], q, outs[w].shape[1] // N_CHIPS)
        return outs[w].at[q]

    first = []
    for w in range(n_w):
        for j in (1, 2, 3):
            px, py, _ = _chip_peer(x, y, j)
            mine = _half_rows(slot(w, myq), c)
            cp = _remote(mine, mine, sems, w * 6 + j - 1, (px, py, c))
            cp.start()
            first.append(cp)
    passed = []
    for w in range(n_w):
        for j in (1, 2, 3):
            _, _, pq = _chip_peer(x, y, j)
            land = _half_rows(slot(w, pq), c)
            _remote(land, land, sems, w * 6 + j - 1, sib).wait_recv()
            cp = _remote(land, land, sems, w * 6 + 2 + j, sib)
            cp.start()
            passed.append(cp)
    for w in range(n_w):
        for j in (1, 2, 3):
            _, _, pq = _chip_peer(x, y, j)
            land = _half_rows(slot(w, pq), 1 - c)
            _remote(land, land, sems, w * 6 + 2 + j, sib).wait_recv()
    for cp in first + passed:
        cp.wait_send()


def _gather_weights(bufs, col_fam):
    n_w = len(bufs)

    def body(*refs):
        _gather_plan(refs[n_w:2 * n_w], col_fam, refs[2 * n_w:2 * n_w + 2], False)

    return pl.pallas_call(
        body, name="gather_weights", in_specs=[ANY] * n_w, out_specs=[ANY] * n_w,
        out_shape=[jax.ShapeDtypeStruct(b.shape, b.dtype) for b in bufs],
        input_output_aliases={w: w for w in range(n_w)},
        scratch_shapes=[pltpu.SemaphoreType.DMA((6 * n_w,)), pltpu.SemaphoreType.DMA((6 * n_w,))],
    )(*bufs)


def _gather_weights_async(bufs, col_fam, name, collective_id):
    refs = [jax.new_ref(b, memory_space=HBM) for b in bufs]
    _on_sequencer(name, collective_id, 6 * len(bufs), 0,
                  lambda sems, _: _gather_plan(refs, col_fam, sems, True))
    return [r[...] for r in refs]


def _grad_half(ref, col, h):
    return _half0(ref, h) if col else _half1(ref, h)


def _swap_halves_with_sibling(grads, col_fam):
    n_w = len(grads)

    def body(*refs):
        _swap_plan(refs[:n_w], refs[n_w:2 * n_w], col_fam, refs[2 * n_w:], False)

    return pl.pallas_call(
        body, name="grad_swap_sibling", in_specs=[ANY] * n_w, out_specs=[ANY] * n_w,
        out_shape=_swap_shapes(grads, col_fam),
        scratch_shapes=[pltpu.SemaphoreType.DMA((n_w,)), pltpu.SemaphoreType.DMA((n_w,))],
    )(*grads)


def _swap_shapes(grads, col_fam):
    out = []
    for w, g in enumerate(grads):
        shp = (g.shape[0] // 2, g.shape[1]) if col_fam[w] else (g.shape[0], g.shape[1] // 2, g.shape[2])
        out.append(jax.ShapeDtypeStruct(shp, g.dtype))
    return out


def _swap_plan(ins, outs, col_fam, sems, handshake):
    x, y, c = _me()
    sib = (x, y, 1 - c)
    if handshake:
        _handshake([sib])
    cps = [_remote(_grad_half(ins[w], col_fam[w], 1 - c), outs[w], sems, w, sib) for w in range(len(ins))]
    for cp in cps:
        cp.start()
    for cp in cps:
        cp.wait_recv()
    for cp in cps:
        cp.wait_send()


def _swap_halves_async(grads, col_fam, name, collective_id):
    srcs = [jax.new_ref(g, memory_space=HBM) for g in grads]
    dsts = [jax.empty_ref(s, memory_space=HBM) for s in _swap_shapes(grads, col_fam)]
    _on_sequencer(name, collective_id, len(grads), 0, lambda sems, _: _swap_plan(srcs, dsts, col_fam, sems, True))
    return [r[...] for r in srcs], [r[...] for r in dsts]


def _half_add(mine, recv, c_arr, col, name):
    if col:
        rows, n = recv.shape
        tr = rows // 2
        grid = (2,)
        in_specs = [pl.BlockSpec((tr, n), lambda i, c: (2 * c[0] + i, 0)), pl.BlockSpec((tr, n), lambda i, c: (i, 0))]
        out_spec = pl.BlockSpec((tr, n), lambda i, c: (i, 0))
    else:
        _, rows, n = recv.shape
        grid = (N_CHIPS,)
        in_specs = [pl.BlockSpec((None, rows, n), lambda q, c: (q, c[0], 0)),
                    pl.BlockSpec((None, rows, n), lambda q, c: (q, 0, 0))]
        out_spec = pl.BlockSpec((None, rows, n), lambda q, c: (q, 0, 0))

    def body(c_ref, a_ref, b_ref, o_ref):
        o_ref[...] = (a_ref[...].astype(F32) + b_ref[...].astype(F32)).astype(BF16)

    return pl.pallas_call(
        body, name=name,
        grid_spec=pltpu.PrefetchScalarGridSpec(num_scalar_prefetch=1, grid=grid, in_specs=in_specs, out_specs=out_spec),
        out_shape=jax.ShapeDtypeStruct(recv.shape, BF16), compiler_params=_cp(),
    )(c_arr, mine, recv)


def _scatter_chip_sums(sums, col_fam):
    n_w = len(sums)

    def body(*refs):
        _scatter_plan(refs[:n_w], refs[n_w:2 * n_w], col_fam, refs[2 * n_w:2 * n_w + 2], refs[2 * n_w + 2], False)

    return pl.pallas_call(
        body, name="grad_scatter_chips", in_specs=[ANY] * n_w, out_specs=[ANY] * n_w,
        out_shape=_scatter_shapes(sums, col_fam),
        scratch_shapes=[pltpu.SemaphoreType.DMA((3 * n_w,)), pltpu.SemaphoreType.DMA((3 * n_w,)),
                        pltpu.SemaphoreType.DMA((n_w,))],
    )(*sums)


def _scatter_shapes(sums, col_fam):
    out = []
    for w, s in enumerate(sums):
        shp = (s.shape[0], s.shape[1] // N_CHIPS) if col_fam[w] else s.shape[1:]
        out.append(jax.ShapeDtypeStruct((N_CHIPS,) + shp, s.dtype))
    return out


def _scatter_plan(ins, outs, col_fam, sems, lsem, handshake):
    n_w = len(ins)
    x, y, c = _me()
    myq = 2 * x + y
    if handshake:
        _handshake([_chip_peer(x, y, j)[:2] + (c,) for j in (1, 2, 3)])

    def slab(w, q):
        if col_fam[w]:
            return _col_window(ins[w], q, ins[w].shape[1] // N_CHIPS)
        return ins[w].at[q]

    local = [pltpu.make_async_copy(slab(w, myq), outs[w].at[myq], lsem.at[w]) for w in range(n_w)]
    for cp in local:
        cp.start()
    cps = []
    for w in range(n_w):
        for j in (1, 2, 3):
            px, py, pq = _chip_peer(x, y, j)
            cp = _remote(slab(w, pq), outs[w].at[myq], sems, w * 3 + j - 1, (px, py, c))
            cp.start()
            cps.append(cp)
    for w in range(n_w):
        for j in (1, 2, 3):
            _, _, pq = _chip_peer(x, y, j)
            land = outs[w].at[pq]
            _remote(land, land, sems, w * 3 + j - 1, (x, y, c)).wait_recv()
    for cp in cps:
        cp.wait_send()
    for cp in local:
        cp.wait()


def _scatter_chip_sums_async(sums, col_fam, name, collective_id):
    srcs = [jax.new_ref(s, memory_space=HBM) for s in sums]
    dsts = [jax.empty_ref(s, memory_space=HBM) for s in _scatter_shapes(sums, col_fam)]
    _on_sequencer(name, collective_id, 3 * len(sums), len(sums),
                  lambda sems, lsem: _scatter_plan(srcs, dsts, col_fam, sems, lsem, True))
    return [r[...] for r in dsts]


def _sum_chips(parts, c_arr, prev, lead, shape, name):
    _, rows, n = parts.shape
    tr = rows // 2 if rows % 32 == 0 else rows
    nblk = rows // tr

    def body(c_ref, p_ref, *rest):
        o_ref = rest[-1]
        acc = p_ref[0].astype(F32)
        for q in range(1, N_CHIPS):
            acc = acc + p_ref[q].astype(F32)
        o_ref[...] = acc

    in_specs = [pl.BlockSpec((N_CHIPS, tr, n), lambda i, c: (0, i, 0))]
    args = [c_arr, parts]
    aliases = {}
    if prev is not None:
        in_specs.append(ANY)
        args.append(prev)
        aliases = {2: 0}
    return pl.pallas_call(
        body, name=name,
        grid_spec=pltpu.PrefetchScalarGridSpec(
            num_scalar_prefetch=1, grid=(nblk,), in_specs=in_specs,
            out_specs=pl.BlockSpec((None, tr, n), lambda i, c: (lead, c[0] * nblk + i, 0))),
        out_shape=jax.ShapeDtypeStruct(shape, F32), input_output_aliases=aliases, compiler_params=_cp(),
    )(*args)


def _join_plan(outs, place, sems, handshake):
    x, y, c = _me()
    sib = (x, y, 1 - c)
    if handshake:
        _handshake([sib])

    def half(k, h):
        o, lead = place[k]
        return _half_rows(outs[o].at[lead], h)

    cps = [_remote(half(k, c), half(k, c), sems, k, sib) for k in range(len(place))]
    for cp in cps:
        cp.start()
    for k in range(len(place)):
        land = half(k, 1 - c)
        _remote(land, land, sems, k, sib).wait_recv()
    for cp in cps:
        cp.wait_send()


def _join_halves(bufs, place, name):
    n_o = len(bufs)
    n_h = len(place)

    def body(*refs):
        _join_plan(refs[n_o:2 * n_o], place, refs[2 * n_o:2 * n_o + 2], False)

    return pl.pallas_call(
        body, name=name, in_specs=[ANY] * n_o, out_specs=[ANY] * n_o,
        out_shape=[jax.ShapeDtypeStruct(b.shape, b.dtype) for b in bufs],
        input_output_aliases={k: k for k in range(n_o)},
        scratch_shapes=[pltpu.SemaphoreType.DMA((n_h,)), pltpu.SemaphoreType.DMA((n_h,))],
    )(*bufs)


def _allreduce_rows(rows):
    n_dev = 8
    n_r = len(rows)
    assert n_r <= 8

    def body(*refs):
        r_refs = refs[:n_r]
        o_ref, slots, send_sems, recv_sems = refs[n_r:]
        x, y, c = _me()
        me = 4 * x + 2 * y + c
        slots[me] = jnp.concatenate([r[...] for r in r_refs] + [jnp.zeros((8 - n_r, D_MODEL), F32)], axis=0)

        def peer(k):
            return (1 - x if k & 4 else x, 1 - y if k & 2 else y, 1 - c if k & 1 else c)

        cps = []
        for k in range(1, n_dev):
            cp = pltpu.make_async_remote_copy(src_ref=slots.at[me], dst_ref=slots.at[me], send_sem=send_sems.at[k - 1],
                                              recv_sem=recv_sems.at[k - 1], device_id=peer(k), device_id_type=MESH)
            cp.start()
            cps.append(cp)
        for k in range(1, n_dev):
            px, py, pc = peer(k)
            land = slots.at[4 * px + 2 * py + pc]
            pltpu.make_async_remote_copy(src_ref=land, dst_ref=land, send_sem=send_sems.at[k - 1],
                                         recv_sem=recv_sems.at[k - 1], device_id=peer(k),
                                         device_id_type=MESH).wait_recv()
        for cp in cps:
            cp.wait_send()
        acc = slots[0]
        for d in range(1, n_dev):
            acc = acc + slots[d]
        o_ref[...] = acc

    vm = pl.BlockSpec(memory_space=pltpu.VMEM)
    return pl.pallas_call(
        body, name="allreduce_rows", in_specs=[vm] * n_r, out_specs=vm,
        out_shape=jax.ShapeDtypeStruct((8, D_MODEL), F32),
        scratch_shapes=[pltpu.VMEM((n_dev, 8, D_MODEL), F32), pltpu.SemaphoreType.DMA((n_dev - 1,)),
                        pltpu.SemaphoreType.DMA((n_dev - 1,))],
    )(*rows)


def _adamw(w, g, m, v, name):
    shape = w.shape
    if len(shape) == 1:
        lead, rows, cols = 1, 1, shape[0]
    else:
        rows, cols = shape[-2:]
        lead = math.prod(shape[:-2])
    args = [a.reshape(lead, rows, cols) for a in (w, g, m, v)]
    tr = rows // 2 if rows % 16 == 0 else rows

    def body(w_ref, g_ref, m_ref, v_ref, d_ref, nm_ref, nv_ref):
        gv = g_ref[...]
        nm = ADAM_B1 * m_ref[...] + (1.0 - ADAM_B1) * gv
        nv = ADAM_B2 * v_ref[...] + (1.0 - ADAM_B2) * jnp.square(gv)
        m_hat = nm / (1.0 - ADAM_B1 ** ADAM_STEP)
        v_hat = nv / (1.0 - ADAM_B2 ** ADAM_STEP)
        d_ref[...] = -ADAM_LR * (m_hat / (jnp.sqrt(v_hat) + ADAM_EPS) + ADAM_WD * w_ref[...])
        nm_ref[...] = nm
        nv_ref[...] = nv

    spec = pl.BlockSpec((None, tr, cols), lambda l, i: (l, i, 0))
    outs = pl.pallas_call(
        body, name=name, grid=(lead, rows // tr), in_specs=[spec] * 4, out_specs=[spec] * 3,
        out_shape=[jax.ShapeDtypeStruct((lead, rows, cols), F32)] * 3, compiler_params=_cp(),
    )(*args)
    return [o.reshape(shape) for o in outs]


def kernel(x, a_w_in, a_sink, a_w_out, b_w_in, b_w_out, norm_mix, norm_ffn, w_gate, w_up, w_down, final_norm, loss_target, m_a_w_in, m_a_sink, m_a_w_out, m_b_w_in, m_b_w_out, m_norm_mix, m_norm_ffn, m_w_gate, m_w_up, m_w_down, m_final_norm, v_a_w_in, v_a_sink, v_a_w_out, v_b_w_in, v_b_w_out, v_norm_mix, v_norm_ffn, v_w_gate, v_w_up, v_w_down, v_final_norm):
    weights = dict(a_w_in=a_w_in, a_sink=a_sink, a_w_out=a_w_out, b_w_in=b_w_in, b_w_out=b_w_out, norm_mix=norm_mix,
                   norm_ffn=norm_ffn, w_gate=w_gate, w_up=w_up, w_down=w_down, final_norm=final_norm)
    mom = dict(a_w_in=m_a_w_in, a_sink=m_a_sink, a_w_out=m_a_w_out, b_w_in=m_b_w_in, b_w_out=m_b_w_out,
               norm_mix=m_norm_mix, norm_ffn=m_norm_ffn, w_gate=m_w_gate, w_up=m_w_up, w_down=m_w_down,
               final_norm=m_final_norm)
    var = dict(a_w_in=v_a_w_in, a_sink=v_a_sink, a_w_out=v_a_w_out, b_w_in=v_b_w_in, b_w_out=v_b_w_out,
               norm_mix=v_norm_mix, norm_ffn=v_norm_ffn, w_gate=v_w_gate, w_up=v_w_up, w_down=v_w_down,
               final_norm=v_final_norm)
    order = ["a_w_in", "a_sink", "a_w_out", "b_w_in", "b_w_out", "norm_mix", "norm_ffn", "w_gate", "w_up", "w_down",
             "final_norm"]
    swapped = ("w_gate", "w_up")
    for n in swapped:
        weights[n], mom[n], var[n] = (a.transpose(0, 2, 1) for a in (weights[n], mom[n], var[n]))
    w_gate_t, w_up_t = weights["w_gate"], weights["w_up"]

    c_arr = lax.axis_index("c").astype(jnp.int32).reshape(1)
    q_arr = (2 * lax.axis_index("x") + lax.axis_index("y")).astype(jnp.int32).reshape(1)

    def placed(w, layer, col, nm):
        return _place_shard(w, layer, q_arr, col, f"place_{nm}")

    (a_in,) = _gather_weights_async([placed(a_w_in, 0, True, "a_in")], (True,), "gather_weights_first", 6)
    a_out, wg0, wu0, wd0 = _gather_weights_async(
        [placed(a_w_out, 0, False, "a_out"), placed(w_gate_t, 0, False, "wg0"), placed(w_up_t, 0, False, "wu0"),
         placed(w_down, 0, False, "wd0")], (False,) * 4, "gather_weights_layer0", 1)
    b_in, b_out, wg1, wu1, wd1 = _gather_weights_async(
        [placed(b_w_in, 0, True, "b_in"), placed(b_w_out, 0, False, "b_out"), placed(w_gate_t, 1, False, "wg1"),
         placed(w_up_t, 1, False, "wu1"), placed(w_down, 1, False, "wd1")], (True,) + (False,) * 4,
        "gather_weights_layer1", 7)
    a_out = a_out.reshape(D_MODEL, D_MODEL)
    b_out = b_out.reshape(D_MODEL, D_MODEL)
    wg, wu, wd = (wg0, wg1), (wu0, wu1), (wd0, wd1)

    gx, grads, vecs = _local_step(x, loss_target, a_in, a_sink[0], a_out, b_in, b_out, norm_mix, norm_ffn, wg, wu, wd,
                                  final_norm)

    rows_out = D_MODEL // N_CHIPS
    partials = [grads["a_in"], grads["b_in"],
                grads["a_out"].reshape(N_CHIPS, rows_out, D_MODEL), grads["b_out"].reshape(N_CHIPS, rows_out, D_MODEL),
                grads["wg"][0], grads["wg"][1], grads["wu"][0], grads["wu"][1], grads["wd"][0], grads["wd"][1]]
    col_fam = (True, True) + (False,) * 8
    names = ("a_in", "b_in", "a_out", "b_out", "wg0", "wg1", "wu0", "wu1", "wd0", "wd1")
    contrib = [None] * len(partials)

    def reduce_group(idx, tag, ids):
        parts = [partials[k] for k in idx]
        cols = tuple(col_fam[k] for k in idx)
        if ids is None:
            theirs = _swap_halves_with_sibling(parts, cols)
        else:
            parts, theirs = _swap_halves_async(parts, cols, f"grad_swap_{tag}", ids[0])
        sums = [_half_add(p, r, c_arr, cf, f"chip_sum_{names[k]}") for p, r, cf, k in zip(parts, theirs, cols, idx)]
        if ids is None:
            out = _scatter_chip_sums(sums, cols)
        else:
            out = _scatter_chip_sums_async(sums, cols, f"grad_scatter_{tag}", ids[1])
        for k, o in zip(idx, out):
            contrib[k] = o

    reduce_group([1, 3, 5, 7, 9], "layer1", (2, 3))
    reduce_group([2, 4, 6, 8], "ffn0", (4, 5))
    reduce_group([0], "a_in", None)
    shapes = [a_w_in.shape, b_w_in.shape, a_w_out.shape, b_w_out.shape, w_down.shape, w_down.shape, w_down.shape]
    place = [(0, 0), (1, 0), (2, 0), (3, 0), (4, 0), (4, 1), (5, 0), (5, 1), (6, 0), (6, 1)]
    bufs = [None] * len(shapes)
    for p, nm, (o, lead) in zip(contrib, names, place):
        bufs[o] = _sum_chips(p, c_arr, bufs[o], lead, shapes[o], f"sum_chips_{nm}")
    g_a_in, g_b_in, g_a_out, g_b_out, g_wg, g_wu, g_wd = _join_halves(bufs, place, "grad_join_sibling")

    sink_row = jnp.pad(vecs["sink"][0:1], ((0, 0), (0, D_MODEL - LANES)))
    tot = _allreduce_rows([vecs["norm_mix"][0], vecs["norm_mix"][1], vecs["norm_ffn"][0], vecs["norm_ffn"][1],
                           vecs["final"], vecs["loss_cols"], sink_row])
    loss = (0.5 / D_MODEL) * jnp.sum(tot[5])
    gw = dict(a_w_in=g_a_in, a_sink=tot[6:7, :N_HEADS], a_w_out=g_a_out, b_w_in=g_b_in, b_w_out=g_b_out,
              norm_mix=tot[0:2], norm_ffn=tot[2:4], w_gate=g_wg, w_up=g_wu, w_down=g_wd, final_norm=tot[4])

    delta, new_m, new_v = {}, {}, {}
    for n in order:
        delta[n], new_m[n], new_v[n] = _adamw(weights[n], gw[n], mom[n], var[n], f"adamw_{n}")
    for n in swapped:
        gw[n], delta[n], new_m[n], new_v[n] = (a.transpose(0, 2, 1) for a in (gw[n], delta[n], new_m[n], new_v[n]))
    return (loss, gx, *[gw[n] for n in order], *[delta[n] for n in order], *[new_m[n] for n in order],
            *[new_v[n] for n in order])
```

```python
import math

import jax
import jax.numpy as jnp
from jax import lax
from jax.experimental import pallas as pl
from jax.experimental.pallas import tpu as pltpu
from jax.experimental.pallas import tpu_sc as plsc

F32 = jnp.float32
BF16 = jnp.bfloat16

D_MODEL = 1024
HEAD_DIM = 64
N_HEADS = 16
N_KV = 4
QKV_W = 1536
D_FF = 2816
N_CHIPS = 4
FF_SH = D_FF // N_CHIPS
HALF_WINDOW_A = 128
DILATED = ((128, 1), (512, 4), (2048, 16))
ROPE_THETA = 10000.0
RMS_EPS = 1e-6
NEG_INF = -1e30
LANES = 128
ADAM_LR, ADAM_B1, ADAM_B2, ADAM_EPS, ADAM_WD, ADAM_STEP = 0.001, 0.9, 0.999, 1e-08, 0.01, 10
VMEM_LIMIT = 56 * 1024 * 1024
GRAD_TOKENS = 2048
MESH = pl.DeviceIdType.MESH


def _cp(**kw):
    return pltpu.CompilerParams(vmem_limit_bytes=VMEM_LIMIT, **kw)


def _row_tile(t, cap):
    tm = min(cap, t)
    assert t % tm == 0
    return tm


def _rope_tables(seq, dil):
    inv = 1.0 / (ROPE_THETA ** (jnp.arange(0, HEAD_DIM, 2, dtype=F32) / HEAD_DIM))
    ang = jnp.arange(seq, dtype=F32)[:, None] * inv[None, :]
    cos, sin = jnp.cos(ang), jnp.sin(ang)
    cos = jnp.tile(cos, (1, 4))
    sin = jnp.concatenate([-sin, sin, -sin, sin], axis=1)

    def perm(t):
        return t.reshape(seq // dil, dil, LANES).transpose(1, 0, 2).reshape(seq, LANES)

    return perm(cos), perm(sin)


def _swap_halves(t):
    lane = lax.broadcasted_iota(jnp.int32, t.shape, 1)
    return jnp.where((lane % HEAD_DIM) < HEAD_DIM // 2, pltpu.roll(t, LANES - 32, 1), pltpu.roll(t, 32, 1))


def _rope(t, cos, sin):
    return t * cos + _swap_halves(t) * sin


def _rope_t(t, cos, sin):
    return t * cos - _swap_halves(t) * sin


def _to_residue(t, batch, dil):
    if dil == 1:
        return t
    s = t.shape[0] // batch
    return t.reshape(batch, s // dil, dil, t.shape[1]).transpose(0, 2, 1, 3).reshape(t.shape)


def _from_residue(t, batch, dil):
    if dil == 1:
        return t
    s = t.shape[0] // batch
    return t.reshape(batch, dil, s // dil, t.shape[1]).transpose(0, 2, 1, 3).reshape(t.shape)


def _rms_fwd(x, w, name):
    t = x.shape[0]
    tm = _row_tile(t, 512)

    def body(x_ref, w_ref, o_ref):
        o_ref[...] = _rms_tile(x_ref[...], w_ref[...]).astype(BF16)

    return pl.pallas_call(
        body, name=name, grid=(t // tm,),
        in_specs=[pl.BlockSpec((tm, D_MODEL), lambda i: (i, 0)), pl.BlockSpec((1, D_MODEL), lambda i: (0, 0))],
        out_specs=pl.BlockSpec((tm, D_MODEL), lambda i: (i, 0)),
        out_shape=jax.ShapeDtypeStruct((t, D_MODEL), BF16), compiler_params=_cp(),
    )(x, w)


def _rms_bwd_tile(xv, wv, dy, dres):
    r = lax.rsqrt(jnp.mean(xv * xv, axis=-1, keepdims=True) + RMS_EPS)
    xh = xv * r
    dxh = dy * wv
    dx = dres + r * (dxh - xh * jnp.mean(dxh * xh, axis=-1, keepdims=True))
    return dx, jnp.sum(dy * xh, axis=0, keepdims=True)


def _accumulate(ref, part):
    @pl.when(pl.program_id(0) == 0)
    def _():
        ref[...] = jnp.zeros_like(ref)

    ref[...] += part


def _rms_bwd(x, w, dhs, dres, name):
    t = x.shape[0]
    tm = _row_tile(t, 512)
    n = len(dhs)

    def body(*refs):
        x_ref, w_ref = refs[0], refs[1]
        dh_refs = refs[2:2 + n]
        dres_ref = refs[2 + n]
        dx_ref, dxb_ref, dw_ref = refs[3 + n:]
        dy = dh_refs[0][...].astype(F32)
        for k in range(1, n):
            dy = dy + dh_refs[k][...].astype(F32)
        dx, dw = _rms_bwd_tile(x_ref[...], w_ref[...], dy, dres_ref[...])
        dx_ref[...] = dx
        dxb_ref[...] = dx.astype(BF16)
        _accumulate(dw_ref, dw)

    row = pl.BlockSpec((tm, D_MODEL), lambda i: (i, 0))
    vec = pl.BlockSpec((1, D_MODEL), lambda i: (0, 0))
    return pl.pallas_call(
        body, name=name, grid=(t // tm,),
        in_specs=[row, vec] + [row] * n + [row],
        out_specs=[row, row, vec],
        out_shape=[jax.ShapeDtypeStruct((t, D_MODEL), F32), jax.ShapeDtypeStruct((t, D_MODEL), BF16),
                   jax.ShapeDtypeStruct((1, D_MODEL), F32)],
        compiler_params=_cp(),
    )(x, w, *dhs, dres)


def _final_tile(xv, wv, tv):
    r = lax.rsqrt(jnp.mean(xv * xv, axis=-1, keepdims=True) + RMS_EPS)
    xh = xv * r
    err = xh * wv - tv
    dy = err * (1.0 / D_MODEL)
    dxh = dy * wv
    dx = r * (dxh - xh * jnp.mean(dxh * xh, axis=-1, keepdims=True))
    return dx, jnp.sum(err * err, axis=0, keepdims=True), jnp.sum(dy * xh, axis=0, keepdims=True)


def _qkv_proj(h, w, cos, sin, group, name):
    t = h.shape[0]
    seq = cos.shape[0]
    tm = _row_tile(seq, 512)
    n_q = N_HEADS * HEAD_DIM // LANES
    n_rope = (N_HEADS + N_KV) * HEAD_DIM // LANES
    scale = 1.0 / math.sqrt(HEAD_DIM)

    def body(h_ref, w_ref, cos_ref, sin_ref, o_ref):
        acc = jnp.dot(h_ref[...], w_ref[...], preferred_element_type=F32)
        cs, sn = cos_ref[...], sin_ref[...]
        csq, snq = cs * scale, sn * scale
        for c in range(QKV_W // LANES):
            blk = acc[:, c * LANES:(c + 1) * LANES]
            if c < n_q:
                blk = _rope(blk, csq, snq)
            elif c < n_rope:
                blk = _rope(blk, cs, sn)
            o_ref[:, c * LANES:(c + 1) * LANES] = blk.astype(BF16)

    tab = pl.BlockSpec((tm, LANES), lambda i: (i % (seq // tm), 0))
    return pl.pallas_call(
        body, name=name, grid=(t // tm,),
        in_specs=[pl.BlockSpec((tm, D_MODEL), lambda i: (i, 0)),
                  pl.BlockSpec((D_MODEL, QKV_W), lambda i: (0, group)), tab, tab],
        out_specs=pl.BlockSpec((tm, QKV_W), lambda i: (i, 0)),
        out_shape=jax.ShapeDtypeStruct((t, QKV_W), BF16), compiler_params=_cp(),
    )(h, w, cos, sin)


def _rms_tile(xv, wv):
    return (xv * lax.rsqrt(jnp.mean(xv * xv, axis=-1, keepdims=True) + RMS_EPS)) * wv


def _mm_res(a, w, res, nw, name):
    t, k = a.shape
    tm = _row_tile(t, 512)

    def body(a_ref, w_ref, r_ref, nw_ref, o_ref, h_ref):
        xv = r_ref[...] + jnp.dot(a_ref[...], w_ref[...], preferred_element_type=F32)
        o_ref[...] = xv
        h_ref[...] = _rms_tile(xv, nw_ref[...]).astype(BF16)

    row = pl.BlockSpec((tm, D_MODEL), lambda i: (i, 0))
    return pl.pallas_call(
        body, name=name, grid=(t // tm,),
        in_specs=[pl.BlockSpec((tm, k), lambda i: (i, 0)),
                  pl.BlockSpec((k, D_MODEL), lambda i: (0, 0), pipeline_mode=pl.Buffered(1)), row,
                  pl.BlockSpec((1, D_MODEL), lambda i: (0, 0))],
        out_specs=[row, row],
        out_shape=[jax.ShapeDtypeStruct((t, D_MODEL), F32), jax.ShapeDtypeStruct((t, D_MODEL), BF16)],
        compiler_params=_cp(),
    )(a, w, res, nw)


def _mm_nt(dy, w, group, out_dtype, name):
    t, n = dy.shape
    k = w.shape[0]
    tm = _row_tile(t, 512)

    def body(dy_ref, w_ref, o_ref):
        o_ref[...] = lax.dot_general(dy_ref[...], w_ref[...], (((1,), (1,)), ((), ())),
                                     preferred_element_type=F32).astype(out_dtype)

    return pl.pallas_call(
        body, name=name, grid=(t // tm,),
        in_specs=[pl.BlockSpec((tm, n), lambda i: (i, 0)), pl.BlockSpec((k, n), lambda i: (0, group))],
        out_specs=pl.BlockSpec((tm, k), lambda i: (i, 0)),
        out_shape=jax.ShapeDtypeStruct((t, k), out_dtype), compiler_params=_cp(),
    )(dy, w)


def _mm_nt_rms(dy, w, x, nw, dres, name):
    t, n = dy.shape
    tm = _row_tile(t, 512)

    def body(dy_ref, w_ref, x_ref, nw_ref, dres_ref, dx_ref, dw_ref):
        dh = lax.dot_general(dy_ref[...], w_ref[...], (((1,), (1,)), ((), ())), preferred_element_type=F32)
        dx, dw = _rms_bwd_tile(x_ref[...], nw_ref[...], dh, dres_ref[...])
        dx_ref[...] = dx
        _accumulate(dw_ref, dw)

    row = pl.BlockSpec((tm, D_MODEL), lambda i: (i, 0))
    vec = pl.BlockSpec((1, D_MODEL), lambda i: (0, 0))
    return pl.pallas_call(
        body, name=name, grid=(t // tm,),
        in_specs=[pl.BlockSpec((tm, n), lambda i: (i, 0)),
                  pl.BlockSpec((D_MODEL, n), lambda i: (0, 0), pipeline_mode=pl.Buffered(1)), row, vec, row],
        out_specs=[row, vec],
        out_shape=[jax.ShapeDtypeStruct((t, D_MODEL), F32), jax.ShapeDtypeStruct((1, D_MODEL), F32)],
        compiler_params=_cp(),
    )(dy, w, x, nw, dres)


def _out_bwd(dx, w, o, name):
    t = dx.shape[0]
    tm = _row_tile(t, 512)

    def body(dx_ref, w_ref, o_ref, et_ref, do_ref, adj_ref):
        do = lax.dot_general(dx_ref[...], w_ref[...], (((1,), (1,)), ((), ())), preferred_element_type=F32)
        do_ref[...] = do.astype(BF16)
        adj_ref[...] = -_dot_split(do * o_ref[...].astype(F32), et_ref[...])

    row = pl.BlockSpec((tm, D_MODEL), lambda i: (i, 0))
    return pl.pallas_call(
        body, name=name, grid=(t // tm,),
        in_specs=[row, pl.BlockSpec((D_MODEL, D_MODEL), lambda i: (0, 0)), row,
                  pl.BlockSpec((D_MODEL, LANES), lambda i: (0, 0))],
        out_specs=[row, pl.BlockSpec((tm, LANES), lambda i: (i, 0))],
        out_shape=[jax.ShapeDtypeStruct((t, D_MODEL), BF16), jax.ShapeDtypeStruct((t, LANES), F32)],
        compiler_params=_cp(),
    )(dx, w, o, _head_expander().T)


def _mm_tn(a, bs, name):
    aq = a.ndim == 3
    bq = bs[0].ndim == 3
    t, ka = a.shape[-2:]
    n = bs[0].shape[-1]
    nq = N_CHIPS if (aq or bq) else 1
    tt = _row_tile(t, GRAD_TOKENS)
    tn = n if n <= 1024 else 768
    assert n % tn == 0
    nb = len(bs)
    steps = t // tt

    def body(*refs):
        a_ref = refs[0]
        b_refs = refs[1:1 + nb]
        o_refs = refs[1 + nb:1 + 2 * nb]
        acc_refs = refs[1 + 2 * nb:]
        s = pl.program_id(2)
        av = a_ref[...]
        for b_ref, o_ref, acc_ref in zip(b_refs, o_refs, acc_refs):
            @pl.when(s == 0)
            def _():
                acc_ref[...] = jnp.zeros_like(acc_ref)

            acc_ref[...] += lax.dot_general(av, b_ref[...], (((0,), (0,)), ((), ())), preferred_element_type=F32)

            @pl.when(s == steps - 1)
            def _():
                o_ref[...] = acc_ref[...].astype(BF16)

    a_spec = (pl.BlockSpec((None, tt, ka), lambda q, j, s: (q, s, 0)) if aq
              else pl.BlockSpec((tt, ka), lambda q, j, s: (s, 0)))
    b_spec = (pl.BlockSpec((None, tt, tn), lambda q, j, s: (q, s, j)) if bq
              else pl.BlockSpec((tt, tn), lambda q, j, s: (s, j)))
    if nq > 1:
        o_spec = pl.BlockSpec((None, ka, tn), lambda q, j, s: (q, 0, j))
        o_shape = jax.ShapeDtypeStruct((nq, ka, n), BF16)
    else:
        o_spec = pl.BlockSpec((ka, tn), lambda q, j, s: (0, j))
        o_shape = jax.ShapeDtypeStruct((ka, n), BF16)
    outs = pl.pallas_call(
        body, name=name, grid=(nq, n // tn, steps),
        in_specs=[a_spec] + [b_spec] * nb, out_specs=[o_spec] * nb, out_shape=[o_shape] * nb,
        scratch_shapes=[pltpu.VMEM((ka, tn), F32)] * nb, compiler_params=_cp(),
    )(a, *bs)
    return outs


def _sigmoid(x):
    return 1.0 / (1.0 + jnp.exp(-x))


def _ffn_up(h, wg, wu, layer, name):
    t = h.shape[0]
    tm = _row_tile(t, 1024)
    nt = (((1,), (1,)), ((), ()))

    def body(h_ref, wg_ref, wu_ref, a_ref, dg_ref, du_ref):
        hv = h_ref[...]
        g = lax.dot_general(hv, wg_ref[...], nt, preferred_element_type=F32)
        u = lax.dot_general(hv, wu_ref[...], nt, preferred_element_type=F32)
        sg = _sigmoid(g)
        silu = g * sg
        a_ref[...] = (silu * u).astype(BF16)
        dg_ref[...] = (sg * (1.0 + g * (1.0 - sg)) * u).astype(BF16)
        du_ref[...] = silu.astype(BF16)

    wspec = pl.BlockSpec((None, None, FF_SH, D_MODEL), lambda q, i: (q, layer, 0, 0))
    ospec = pl.BlockSpec((None, tm, FF_SH), lambda q, i: (q, i, 0))
    oshape = jax.ShapeDtypeStruct((N_CHIPS, t, FF_SH), BF16)
    return pl.pallas_call(
        body, name=name, grid=(N_CHIPS, t // tm),
        in_specs=[pl.BlockSpec((tm, D_MODEL), lambda q, i: (i, 0)), wspec, wspec],
        out_specs=[ospec] * 3, out_shape=[oshape] * 3, compiler_params=_cp(),
    )(h, wg, wu)


def _ffn_down(a, wd, res, layer, name, norm_w=None, head=None):
    t = a.shape[1]
    tm = _row_tile(t, 512)
    resident = pl.BlockSpec((N_CHIPS, None, FF_SH, D_MODEL), lambda i: (0, layer, 0, 0), pipeline_mode=pl.Buffered(1))
    row = pl.BlockSpec((tm, D_MODEL), lambda i: (i, 0))
    vec = pl.BlockSpec((1, D_MODEL), lambda i: (0, 0))

    def hidden(a_ref, w_ref, r_ref):
        acc = r_ref[...]
        for q in range(N_CHIPS):
            acc = acc + jnp.dot(a_ref[q], w_ref[q], preferred_element_type=F32)
        return acc

    if head is None:
        def body(a_ref, w_ref, r_ref, nw_ref, o_ref, h_ref):
            xv = hidden(a_ref, w_ref, r_ref)
            o_ref[...] = xv
            h_ref[...] = _rms_tile(xv, nw_ref[...]).astype(BF16)

        return pl.pallas_call(
            body, name=name, grid=(t // tm,),
            in_specs=[pl.BlockSpec((N_CHIPS, tm, FF_SH), lambda i: (0, i, 0)), resident, row, vec],
            out_specs=[row, row],
            out_shape=[jax.ShapeDtypeStruct((t, D_MODEL), F32), jax.ShapeDtypeStruct((t, D_MODEL), BF16)],
            compiler_params=_cp(),
        )(a, wd, res, norm_w)

    def body(a_ref, w_ref, r_ref, nw_ref, t_ref, dx_ref, dxb_ref, l_ref, dw_ref):
        dx, sq, dw = _final_tile(hidden(a_ref, w_ref, r_ref), nw_ref[...], t_ref[...])
        dx_ref[...] = dx
        dxb_ref[...] = dx.astype(BF16)
        _accumulate(l_ref, sq)
        _accumulate(dw_ref, dw)

    return pl.pallas_call(
        body, name=name, grid=(t // tm,),
        in_specs=[pl.BlockSpec((N_CHIPS, tm, FF_SH), lambda i: (0, i, 0)), resident, row, vec, row],
        out_specs=[row, row, vec, vec],
        out_shape=[jax.ShapeDtypeStruct((t, D_MODEL), F32), jax.ShapeDtypeStruct((t, D_MODEL), BF16),
                   jax.ShapeDtypeStruct((1, D_MODEL), F32), jax.ShapeDtypeStruct((1, D_MODEL), F32)],
        compiler_params=_cp(),
    )(a, wd, res, *head)


def _ffn_bwd(dy, wd, wg, wu, fg, fu, x, nw, dres, name):
    t = dy.shape[0]
    tm = _row_tile(t, 256)
    nt = (((1,), (1,)), ((), ()))

    def body(dy_ref, wd_ref, wg_ref, wu_ref, fg_ref, fu_ref, x_ref, nw_ref, dres_ref,
             dg_ref, du_ref, dx_ref, dxb_ref, dw_ref):
        dyv = dy_ref[...]
        acc = jnp.zeros((tm, D_MODEL), F32)
        for q in range(N_CHIPS):
            da = lax.dot_general(dyv, wd_ref[q], nt, preferred_element_type=F32)
            dg = (da * fg_ref[q].astype(F32)).astype(BF16)
            du = (da * fu_ref[q].astype(F32)).astype(BF16)
            dg_ref[q] = dg
            du_ref[q] = du
            acc = acc + jnp.dot(dg, wg_ref[q], preferred_element_type=F32)
            acc = acc + jnp.dot(du, wu_ref[q], preferred_element_type=F32)
        dx, dw = _rms_bwd_tile(x_ref[...], nw_ref[...], acc, dres_ref[...])
        dx_ref[...] = dx
        dxb_ref[...] = dx.astype(BF16)
        _accumulate(dw_ref, dw)

    aspec = pl.BlockSpec((N_CHIPS, tm, FF_SH), lambda i: (0, i, 0))
    wspec = pl.BlockSpec((N_CHIPS, None, FF_SH, D_MODEL), lambda i: (0, 0, 0, 0), pipeline_mode=pl.Buffered(1))
    row = pl.BlockSpec((tm, D_MODEL), lambda i: (i, 0))
    vec = pl.BlockSpec((1, D_MODEL), lambda i: (0, 0))
    ashape = jax.ShapeDtypeStruct((N_CHIPS, t, FF_SH), BF16)
    return pl.pallas_call(
        body, name=name, grid=(t // tm,),
        in_specs=[row, wspec, wspec, wspec, aspec, aspec, row, vec, row],
        out_specs=[aspec, aspec, row, row, vec],
        out_shape=[ashape, ashape, jax.ShapeDtypeStruct((t, D_MODEL), F32), jax.ShapeDtypeStruct((t, D_MODEL), BF16),
                   jax.ShapeDtypeStruct((1, D_MODEL), F32)],
        compiler_params=_cp(),
    )(dy, wd, wg, wu, fg, fu, x, nw, dres)


def _attn_geometry(length, half_window):
    qb = min(LANES, length)
    kw = min(qb + 2 * half_window, length)
    return qb, kw, length // qb


def _dup_kv(src_ref, dst_ref, s, length):
    ch = min(length, 256)
    lo = lax.broadcasted_iota(jnp.int32, (ch, LANES), 1) < HEAD_DIM

    def chunk(c, carry):
        r0 = pl.multiple_of(c * ch, ch)
        for j in range(N_KV // 2):
            tile = src_ref[s, pl.ds(r0, ch), j * LANES:(j + 1) * LANES].astype(F32)
            rolled = pltpu.roll(tile, HEAD_DIM, 1)
            dst_ref[2 * j, pl.ds(r0, ch), :] = jnp.where(lo, tile, rolled).astype(BF16)
            dst_ref[2 * j + 1, pl.ds(r0, ch), :] = jnp.where(lo, rolled, tile).astype(BF16)
        return carry

    lax.fori_loop(0, length // ch, chunk, 0)


def _stack_heads(ref, s, q0, qb, g):
    lo = lax.broadcasted_iota(jnp.int32, (qb, LANES), 1) < HEAD_DIM
    parts = []
    for a in range(4):
        col = (2 * g + a // 2) * LANES
        tile = ref[s, pl.ds(q0, qb), col:col + LANES]
        keep = lo if a % 2 == 0 else jnp.logical_not(lo)
        parts.append(jnp.where(keep, tile, jnp.zeros_like(tile)))
    return jnp.concatenate(parts, axis=0)


def _unstack_pair_t(stacked_t, qb, pair):
    both = jnp.concatenate([stacked_t[:, (2 * pair) * qb:(2 * pair + 1) * qb],
                            stacked_t[:, (2 * pair + 1) * qb:(2 * pair + 2) * qb]], axis=0)
    return both.T


def _band_mask_t(q0, k0, qb, kw, half_window):
    key = lax.broadcasted_iota(jnp.int32, (kw, 4 * qb), 0)
    qry = lax.broadcasted_iota(jnp.int32, (kw, 4 * qb), 1) & (qb - 1)
    return jnp.abs((q0 + qry) - (k0 + key)) <= half_window


def _block_origin(i, qb, kw, half_window, length):
    if isinstance(i, int):
        return i * qb, min(max(i * qb - half_window, 0), length - kw)
    return (pl.multiple_of(i * qb, qb),
            pl.multiple_of(jnp.clip(i * qb - half_window, 0, length - kw), HEAD_DIM))


def _head_row(vals, qb):
    return jnp.concatenate([jnp.broadcast_to(v, (1, qb)).astype(F32) for v in vals], axis=1)


def _attn_fwd(qkv, sink, n_seq, length, half_window, seq_blk, out_dtype, name):
    qb, kw, nblk = _attn_geometry(length, half_window)
    with_sink = sink is not None
    nt = (((1,), (1,)), ((), ()))
    tn = (((0,), (0,)), ((), ()))
    qkv3 = qkv.reshape(n_seq, length, QKV_W)

    def body(*refs):
        refs = list(refs)
        sink_ref = refs.pop(0) if with_sink else None
        q_ref, k_ref, v_ref, o_ref, lse_ref = refs[:5]
        kx_ref, vx_ref = refs[-2:]
        head_row = lax.broadcasted_iota(jnp.int32, (N_HEADS, qb), 0)
        for s in range(seq_blk):
            _dup_kv(k_ref, kx_ref, s, length)
            _dup_kv(v_ref, vx_ref, s, length)

            def block(i, carry):
                q0, k0 = _block_origin(i, qb, kw, half_window, length)
                valid = _band_mask_t(q0, k0, qb, kw, half_window)
                lse_tile = jnp.zeros((N_HEADS, qb), F32)
                groups = range(N_KV)
                sts = [lax.dot_general(kx_ref[g, pl.ds(k0, kw), :], _stack_heads(q_ref, s, q0, qb, g), nt,
                                       preferred_element_type=F32) for g in groups]
                sts = [jnp.where(valid, st, NEG_INF) for st in sts]
                ms = [jnp.max(st, axis=0, keepdims=True) for st in sts]
                if with_sink:
                    sks = [_head_row([sink_ref[4 * g + a] for a in range(4)], qb) for g in groups]
                    ms = [jnp.maximum(m, sk) for m, sk in zip(ms, sks)]
                es = [jnp.exp(st - m) for st, m in zip(sts, ms)]
                dens = [jnp.sum(e, axis=0, keepdims=True) for e in es]
                if with_sink:
                    dens = [den + jnp.exp(sk - m) for den, sk, m in zip(dens, sks, ms)]
                ots = [lax.dot_general(vx_ref[g, pl.ds(k0, kw), 0:HEAD_DIM], es[g].astype(BF16), tn,
                                       preferred_element_type=F32) / dens[g] for g in groups]
                for g in groups:
                    for pair in range(2):
                        col = (2 * g + pair) * LANES
                        o_ref[s, pl.ds(q0, qb), col:col + LANES] = _unstack_pair_t(ots[g], qb, pair).astype(out_dtype)
                    lse = ms[g] + jnp.log(dens[g])
                    for a in range(4):
                        lse_tile = jnp.where(head_row == 4 * g + a, lse[:, a * qb:(a + 1) * qb], lse_tile)
                lse_ref[s, :, pl.ds(q0, qb)] = lse_tile
                return carry

            if nblk == 1:
                block(0, 0)
            else:
                lax.fori_loop(0, nblk, block, 0)

    in_specs = [pl.BlockSpec((seq_blk, length, N_HEADS * HEAD_DIM), lambda n: (n, 0, 0)),
                pl.BlockSpec((seq_blk, length, N_KV * HEAD_DIM), lambda n: (n, 0, 4)),
                pl.BlockSpec((seq_blk, length, N_KV * HEAD_DIM), lambda n: (n, 0, 5))]
    args = [qkv3, qkv3, qkv3]
    if with_sink:
        in_specs.insert(0, pl.BlockSpec(memory_space=pltpu.SMEM))
        args.insert(0, sink)
    out_specs = [pl.BlockSpec((seq_blk, length, D_MODEL), lambda n: (n, 0, 0)),
                 pl.BlockSpec((seq_blk, N_HEADS, length), lambda n: (n, 0, 0))]
    out_shape = [jax.ShapeDtypeStruct((n_seq, length, D_MODEL), out_dtype),
                 jax.ShapeDtypeStruct((n_seq, N_HEADS, length), F32)]
    o, lse = pl.pallas_call(
        body, name=name, grid=(n_seq // seq_blk,), in_specs=in_specs, out_specs=out_specs, out_shape=out_shape,
        scratch_shapes=[pltpu.VMEM((N_KV, length, LANES), BF16), pltpu.VMEM((N_KV, length, LANES), BF16)],
        compiler_params=_cp(),
    )(*args)
    return o.reshape(n_seq * length, D_MODEL), lse


def _attn_bwd(qkv, do, adj, lse, sink, cos, sin, n_seq, length, half_window, seq_blk, dil, name):
    qb, kw, nblk = _attn_geometry(length, half_window)
    scale = 1.0 / math.sqrt(HEAD_DIM)
    with_sink = sink is not None
    nt = (((1,), (1,)), ((), ()))
    tn = (((0,), (0,)), ((), ()))
    qkv3 = qkv.reshape(n_seq, length, QKV_W)
    do3 = do.reshape(n_seq, length, D_MODEL)
    tabs = [t.reshape(dil, length, LANES) for t in (cos, sin)]
    tab_blocks = dil // seq_blk if dil >= seq_blk else 1

    def body(*refs):
        refs = list(refs)
        sink_ref = refs.pop(0) if with_sink else None
        q_ref, k_ref, v_ref, do_ref, aux_ref, lse_ref, cos_ref, sin_ref, dqkv_ref = refs[:9]
        ds_ref = refs[9] if with_sink else None
        kx_ref, vx_ref, dkx_ref, dvx_ref = refs[-4:]
        lane = lax.broadcasted_iota(jnp.int32, (1, LANES), 1)
        if with_sink:
            @pl.when(pl.program_id(0) == 0)
            def _():
                ds_ref[...] = jnp.zeros_like(ds_ref)

        for s in range(seq_blk):
            ts = s % dil
            _dup_kv(k_ref, kx_ref, s, length)
            _dup_kv(v_ref, vx_ref, s, length)
            dkx_ref[...] = jnp.zeros_like(dkx_ref)
            dvx_ref[...] = jnp.zeros_like(dvx_ref)

            def block(i, dsink):
                q0, k0 = _block_origin(i, qb, kw, half_window, length)
                valid = _band_mask_t(q0, k0, qb, kw, half_window)
                cs = cos_ref[ts, pl.ds(q0, qb), :] * scale
                sn = sin_ref[ts, pl.ds(q0, qb), :] * scale
                adj_tile = aux_ref[s, :, pl.ds(q0, qb)]
                lse_tile = lse_ref[s, :, pl.ds(q0, qb)]
                groups = range(N_KV)
                qss = [_stack_heads(q_ref, s, q0, qb, g) for g in groups]
                doss = [_stack_heads(do_ref, s, q0, qb, g) for g in groups]
                kxs = [kx_ref[g, pl.ds(k0, kw), :] for g in groups]
                sts = [lax.dot_general(kxs[g], qss[g], nt, preferred_element_type=F32) for g in groups]
                dpts = [lax.dot_general(vx_ref[g, pl.ds(k0, kw), :], doss[g], nt, preferred_element_type=F32)
                        for g in groups]
                lses = [_head_row([lse_tile[4 * g + a:4 * g + a + 1, :] for a in range(4)], qb) for g in groups]
                shifts = [_head_row([adj_tile[4 * g + a:4 * g + a + 1, :] for a in range(4)], qb) for g in groups]
                pts = [jnp.exp(jnp.where(valid, sts[g], NEG_INF) - lses[g]) for g in groups]
                dsbs = [(pts[g] * (dpts[g] + shifts[g])).astype(BF16) for g in groups]
                pbs = [pt.astype(BF16) for pt in pts]
                if with_sink:
                    for g in groups:
                        sk = _head_row([sink_ref[4 * g + a] for a in range(4)], qb)
                        dsk = jnp.exp(sk - lses[g]) * shifts[g]
                        for a in range(4):
                            tot = jnp.sum(dsk[:, a * qb:(a + 1) * qb], axis=1, keepdims=True)
                            dsink = dsink + jnp.where(lane == 4 * g + a, tot, 0.0)
                dqts = [lax.dot_general(kx_ref[g, pl.ds(k0, kw), 0:HEAD_DIM], dsbs[g], tn, preferred_element_type=F32)
                        for g in groups]
                for g in groups:
                    for pair in range(2):
                        col = (2 * g + pair) * LANES
                        tile = _rope_t(_unstack_pair_t(dqts[g], qb, pair), cs, sn)
                        dqkv_ref[s, pl.ds(q0, qb), col:col + LANES] = tile.astype(BF16)
                for g in groups:
                    dkx_ref[g, pl.ds(k0, kw), :] += jnp.dot(dsbs[g], qss[g], preferred_element_type=F32)
                    dvx_ref[g, pl.ds(k0, kw), :] += jnp.dot(pbs[g], doss[g], preferred_element_type=F32)
                return dsink

            if nblk == 1:
                dsink = block(0, jnp.zeros((1, LANES), F32))
            else:
                dsink = lax.fori_loop(0, nblk, block, jnp.zeros((1, LANES), F32))
            if with_sink:
                ds_ref[0:1, :] += dsink

            ch = min(length, 256)
            lo_c = lax.broadcasted_iota(jnp.int32, (ch, LANES), 1) < HEAD_DIM

            def fin(c, carry):
                r0 = pl.multiple_of(c * ch, ch)
                cs = cos_ref[ts, pl.ds(r0, ch), :]
                sn = sin_ref[ts, pl.ds(r0, ch), :]
                for j in range(N_KV // 2):
                    both = []
                    for acc_ref in (dkx_ref, dvx_ref):
                        t0 = acc_ref[2 * j, pl.ds(r0, ch), :]
                        t1 = acc_ref[2 * j + 1, pl.ds(r0, ch), :]
                        t0 = t0 + pltpu.roll(t0, HEAD_DIM, 1)
                        t1 = t1 + pltpu.roll(t1, HEAD_DIM, 1)
                        both.append(jnp.where(lo_c, t0, t1))
                    kcol = N_HEADS * HEAD_DIM + j * LANES
                    vcol = (N_HEADS + N_KV) * HEAD_DIM + j * LANES
                    dqkv_ref[s, pl.ds(r0, ch), kcol:kcol + LANES] = _rope_t(both[0], cs, sn).astype(BF16)
                    dqkv_ref[s, pl.ds(r0, ch), vcol:vcol + LANES] = both[1].astype(BF16)
                return carry

            lax.fori_loop(0, length // ch, fin, 0)

    seq_map = lambda n: (n, 0, 0)
    tab_map = (lambda n: (n % tab_blocks, 0, 0)) if dil >= seq_blk else (lambda n: (0, 0, 0))
    tab_rows = min(seq_blk, dil)
    in_specs = [pl.BlockSpec((seq_blk, length, N_HEADS * HEAD_DIM), seq_map),
                pl.BlockSpec((seq_blk, length, N_KV * HEAD_DIM), lambda n: (n, 0, 4)),
                pl.BlockSpec((seq_blk, length, N_KV * HEAD_DIM), lambda n: (n, 0, 5)),
                pl.BlockSpec((seq_blk, length, D_MODEL), seq_map),
                pl.BlockSpec((seq_blk, N_HEADS, length), seq_map),
                pl.BlockSpec((seq_blk, N_HEADS, length), seq_map),
                pl.BlockSpec((tab_rows, length, LANES), tab_map),
                pl.BlockSpec((tab_rows, length, LANES), tab_map)]
    args = [qkv3, qkv3, qkv3, do3, adj, lse] + tabs
    if with_sink:
        in_specs.insert(0, pl.BlockSpec(memory_space=pltpu.SMEM))
        args.insert(0, sink)
    out_specs = [pl.BlockSpec((seq_blk, length, QKV_W), seq_map)]
    out_shape = [jax.ShapeDtypeStruct((n_seq, length, QKV_W), BF16)]
    if with_sink:
        out_specs.append(pl.BlockSpec((8, LANES), lambda n: (0, 0)))
        out_shape.append(jax.ShapeDtypeStruct((8, LANES), F32))
    outs = pl.pallas_call(
        body, name=name, grid=(n_seq // seq_blk,), in_specs=in_specs, out_specs=out_specs, out_shape=out_shape,
        scratch_shapes=[pltpu.VMEM((N_KV, length, LANES), BF16), pltpu.VMEM((N_KV, length, LANES), BF16),
                        pltpu.VMEM((N_KV, length, LANES), F32), pltpu.VMEM((N_KV, length, LANES), F32)],
        compiler_params=_cp(),
    )(*args)
    dqkv = outs[0].reshape(n_seq * length, QKV_W)
    return (dqkv, outs[1]) if with_sink else (dqkv, None)


def _head_expander():
    h = jnp.arange(LANES)[:, None]
    l = jnp.arange(D_MODEL)[None, :]
    return (l // HEAD_DIM == h).astype(BF16)


def _dot_split(a, e):
    hi = a.astype(BF16)
    lo = (a - hi.astype(F32)).astype(BF16)
    return jnp.dot(hi, e, preferred_element_type=F32) + jnp.dot(lo, e, preferred_element_type=F32)


def _mix_weights(lses):
    m = jnp.maximum(jnp.maximum(lses[0], lses[1]), lses[2])
    es = [jnp.exp(v - m) for v in lses]
    tot = es[0] + es[1] + es[2]
    return [e / tot for e in es]


def _mix_fwd(os_, lses, name):
    t = os_[0].shape[0]
    tm = _row_tile(t, 512)

    def body(o0, o1, o2, l0, l1, l2, e_ref, out_ref):
        wts = _mix_weights([l0[...], l1[...], l2[...]])
        acc = jnp.zeros((tm, D_MODEL), F32)
        for w, o_ref in zip(wts, (o0, o1, o2)):
            acc = acc + _dot_split(w, e_ref[...]) * o_ref[...]
        out_ref[...] = acc.astype(BF16)

    row = pl.BlockSpec((tm, D_MODEL), lambda i: (i, 0))
    lrow = pl.BlockSpec((tm, LANES), lambda i: (i, 0))
    return pl.pallas_call(
        body, name=name, grid=(t // tm,),
        in_specs=[row] * 3 + [lrow] * 3 + [pl.BlockSpec((LANES, D_MODEL), lambda i: (0, 0))],
        out_specs=row, out_shape=jax.ShapeDtypeStruct((t, D_MODEL), BF16), compiler_params=_cp(),
    )(*os_, *lses, _head_expander())


def _mix_bwd(dmix, os_, lses, name):
    t = dmix.shape[0]
    tm = _row_tile(t, 512)

    def body(d_ref, o0, o1, o2, l0, l1, l2, e_ref, et_ref, do0, do1, do2, a0, a1, a2):
        wts = _mix_weights([l0[...], l1[...], l2[...]])
        dv = d_ref[...].astype(F32)
        cs = [_dot_split(dv * o_ref[...], et_ref[...]) for o_ref in (o0, o1, o2)]
        mean_c = wts[0] * cs[0] + wts[1] * cs[1] + wts[2] * cs[2]
        for w, c, do_ref, a_ref in zip(wts, cs, (do0, do1, do2), (a0, a1, a2)):
            do_ref[...] = (_dot_split(w, e_ref[...]) * dv).astype(BF16)
            a_ref[...] = w * (c - mean_c) - w * c

    row = pl.BlockSpec((tm, D_MODEL), lambda i: (i, 0))
    lrow = pl.BlockSpec((tm, LANES), lambda i: (i, 0))
    e = _head_expander()
    return pl.pallas_call(
        body, name=name, grid=(t // tm,),
        in_specs=[row] * 4 + [lrow] * 3 + [pl.BlockSpec((LANES, D_MODEL), lambda i: (0, 0)),
                                            pl.BlockSpec((D_MODEL, LANES), lambda i: (0, 0))],
        out_specs=[row] * 3 + [lrow] * 3,
        out_shape=[jax.ShapeDtypeStruct((t, D_MODEL), BF16)] * 3 + [jax.ShapeDtypeStruct((t, LANES), F32)] * 3,
        compiler_params=_cp(),
    )(dmix, *os_, *lses, e, e.T)


def _stats_to_tokens(stat, batch, dil):
    n_seq, _, length = stat.shape
    t = stat.transpose(0, 2, 1).reshape(n_seq * length, N_HEADS)
    return _from_residue(jnp.pad(t, ((0, 0), (0, LANES - N_HEADS))), batch, dil)


def _stats_from_tokens(stat, batch, dil, n_seq, length):
    t = _to_residue(stat[:, :N_HEADS], batch, dil)
    return t.reshape(n_seq, length, N_HEADS).transpose(0, 2, 1)


def _group_geometry(batch, seq, dil, window):
    length = seq // dil
    n_seq = batch * dil
    seq_blk = max(1, min(dil, 1024 // length))
    return n_seq, length, (window // 2) // dil, seq_blk


def _local_step(x, target, a_in, a_sink, a_out, b_in, b_out, norm_mix, norm_ffn, wg, wu, wd, final_norm):
    batch, seq, _ = x.shape
    t = batch * seq
    x0 = x.reshape(t, D_MODEL)
    tgt = target.reshape(t, D_MODEL)
    tabs = {d: _rope_tables(seq, d) for _, d in DILATED}
    nm = [norm_mix[i:i + 1] for i in range(2)]
    nf = [norm_ffn[i:i + 1] for i in range(2)]

    h0 = _rms_fwd(x0, nm[0], "rms_mix0")
    qkv0 = _qkv_proj(h0, a_in, *tabs[1], 0, "qkv0")
    o0, lse0 = _attn_fwd(qkv0, a_sink, batch, seq, HALF_WINDOW_A, 1, BF16, "attn0")
    x1, hf0 = _mm_res(o0, a_out, x0, nf[0], "out0")
    act0, g0, u0 = _ffn_up(hf0, wg[0], wu[0], 0, "ffn_up0")
    x2, h1 = _ffn_down(act0, wd[0], x1, 0, "ffn_down0", norm_w=nm[1])

    geo = [_group_geometry(batch, seq, d, w) for w, d in DILATED]
    h1g, qkv1, o1, lse1, lse1r = [], [], [], [], []
    for gi, (_, d) in enumerate(DILATED):
        n_seq, length, hw, sb = geo[gi]
        hp = _to_residue(h1, batch, d)
        pj = _qkv_proj(hp, b_in, *tabs[d], gi, f"qkv1_{gi}")
        o, lse = _attn_fwd(pj, None, n_seq, length, hw, sb, BF16, f"attn1_{gi}")
        h1g.append(hp)
        qkv1.append(pj)
        o1.append(_from_residue(o, batch, d))
        lse1r.append(lse)
        lse1.append(_stats_to_tokens(lse, batch, d))
    omix = _mix_fwd(o1, lse1, "mix")
    x3, hf1 = _mm_res(omix, b_out, x2, nf[1], "out1")
    act1, g1, u1 = _ffn_up(hf1, wg[1], wu[1], 0, "ffn_up1")
    dx4, dx4b, loss_cols, d_final = _ffn_down(act1, wd[1], x3, 0, "ffn_down1_loss",
                                                     head=(final_norm.reshape(1, D_MODEL), tgt))

    def ffn_bwd(dxo, dxob, x_mid, hf, g, u, act, layer):
        dg, du, dxm, dxmb, d_nf = _ffn_bwd(dxob, wd[layer], wg[layer], wu[layer], g, u, x_mid, nf[layer], dxo,
                                           f"ffn_bwd{layer}")
        (d_wd,) = _mm_tn(act, [dxob], f"grad_wd{layer}")
        (d_wgt,) = _mm_tn(dg, [hf], f"grad_wg{layer}")
        (d_wut,) = _mm_tn(du, [hf], f"grad_wu{layer}")
        return dxm, dxmb, d_nf, d_wgt, d_wut, d_wd

    dx3, dx3b, d_nf1, d_wg1, d_wu1, d_wd1 = ffn_bwd(dx4, dx4b, x3, hf1, g1, u1, act1, 1)

    dmix = _mm_nt(dx3b, b_out, 0, BF16, "out1_bwd")
    (d_b_out,) = _mm_tn(omix, [dx3b], "grad_b_out")
    mb = _mix_bwd(dmix, o1, lse1, "mix_bwd")
    dh1, d_b_in = [], []
    for gi, (_, d) in enumerate(DILATED):
        n_seq, length, hw, sb = geo[gi]
        dog = _to_residue(mb[gi], batch, d)
        adj = _stats_from_tokens(mb[3 + gi], batch, d, n_seq, length)
        dpj, _ = _attn_bwd(qkv1[gi], dog, adj, lse1r[gi], None, *tabs[d], n_seq, length, hw, sb, d, f"attn1_bwd{gi}")
        (dw,) = _mm_tn(h1g[gi], [dpj], f"grad_b_in{gi}")
        d_b_in.append(dw)
        dh1.append(_from_residue(_mm_nt(dpj, b_in, gi, BF16, f"qkv1_bwd{gi}"), batch, d))
    dx2, dx2b, d_nm1 = _rms_bwd(x2, nm[1], dh1, dx3, "rms_mix_bwd1")

    dx1, dx1b, d_nf0, d_wg0, d_wu0, d_wd0 = ffn_bwd(dx2, dx2b, x1, hf0, g0, u0, act0, 0)

    do0, adj0 = _out_bwd(dx1b, a_out, o0, "out0_bwd")
    (d_a_out,) = _mm_tn(o0, [dx1b], "grad_a_out")
    adj0 = _stats_from_tokens(adj0, batch, 1, batch, seq)
    dqkv0, d_sink = _attn_bwd(qkv0, do0, adj0, lse0, a_sink, *tabs[1], batch, seq, HALF_WINDOW_A, 1, 1, "attn0_bwd")
    (d_a_in,) = _mm_tn(h0, [dqkv0], "grad_a_in")
    gx, d_nm0 = _mm_nt_rms(dqkv0, a_in, x0, nm[0], dx1, "qkv0_bwd")

    grads = dict(a_in=d_a_in, a_out=d_a_out, b_in=jnp.concatenate(d_b_in, axis=1), b_out=d_b_out,
                 wg=(d_wg0, d_wg1), wu=(d_wu0, d_wu1), wd=(d_wd0, d_wd1))
    vecs = dict(norm_mix=(d_nm0, d_nm1), norm_ffn=(d_nf0, d_nf1), final=d_final, loss_cols=loss_cols, sink=d_sink)
    return gx.reshape(x.shape), grads, vecs


ANY = pl.BlockSpec(memory_space=pl.ANY)
HBM = pltpu.MemorySpace.HBM


def _me():
    return lax.axis_index("x"), lax.axis_index("y"), lax.axis_index("c")


def _chip_peer(x, y, j):
    px = 1 - x if j & 2 else x
    py = 1 - y if j & 1 else y
    return px, py, 2 * px + py


def _remote(src, dst, sems, k, dev):
    return pltpu.make_async_remote_copy(src_ref=src, dst_ref=dst, send_sem=sems[0].at[k], recv_sem=sems[1].at[k],
                                        device_id=dev, device_id_type=MESH)


def _col_window(ref, q, width):
    return ref.at[:, pl.ds(pl.multiple_of(q * width, LANES), width)]


def _half0(ref, h):
    n = ref.shape[0] // 2
    return ref.at[pl.ds(h * n, n)]


def _half1(ref, h):
    n = ref.shape[1] // 2
    return ref.at[:, pl.ds(h * n, n)]


def _half_rows(ref, h):
    n = ref.shape[-2] // 2
    if len(ref.shape) == 2:
        return ref.at[pl.ds(h * n, n)]
    return ref.at[:, pl.ds(h * n, n)]


def _place_shard(w, layer, q_arr, col, name):
    _, rows, cols = w.shape

    def body(q_ref, w_ref, o_ref):
        o_ref[...] = w_ref[...].astype(BF16)

    if col:
        out_spec = pl.BlockSpec((rows, cols), lambda l, q: (0, q[0]))
        out_shape = jax.ShapeDtypeStruct((rows, N_CHIPS * cols), BF16)
    else:
        out_spec = pl.BlockSpec((None, None, rows, cols), lambda l, q: (q[0], 0, 0, 0))
        out_shape = jax.ShapeDtypeStruct((N_CHIPS, 1, rows, cols), BF16)
    return pl.pallas_call(
        body, name=name,
        grid_spec=pltpu.PrefetchScalarGridSpec(
            num_scalar_prefetch=1, grid=(1,),
            in_specs=[pl.BlockSpec((None, rows, cols), lambda l, q: (layer, 0, 0))], out_specs=out_spec),
        out_shape=out_shape, compiler_params=_cp(),
    )(q_arr, w)


def _handshake(peers):
    barrier = pltpu.get_barrier_semaphore()
    for p in peers:
        pl.semaphore_signal(barrier, inc=1, device_id=p, device_id_type=MESH)
    pl.semaphore_wait(barrier, len(peers))


def _on_sequencer(name, collective_id, n_sem, n_local, body):
    @pl.kernel(mesh=plsc.ScalarSubcoreMesh(axis_name="seq", num_cores=1), name=name,
               scratch_types=(pltpu.SemaphoreType.DMA((n_sem,)), pltpu.SemaphoreType.DMA((n_sem,)),
                              pltpu.SemaphoreType.DMA((max(n_local, 1),))),
               compiler_params=pltpu.CompilerParams(collective_id=collective_id))
    def launch(send_sems, recv_sems, local_sems):
        body((send_sems, recv_sems), local_sems)

    launch()


def _gather_plan(outs, col_fam, sems, handshake):
    n_w = len(outs)
    x, y, c = _me()
    myq = 2 * x + y
    sib = (x, y, 1 - c)
    if handshake:
        _handshake([sib] + [_chip_peer(x, y, j)[:2] + (c,) for j in (1, 2, 3)])

    def slot(w, q):
        if col_fam[w]:
            return _col_window(outs[w], q, outs[w].shape[1] // N_CHIPS)
        return outs[w].at[q]

    first = []
    for w in range(n_w):
        for j in (1, 2, 3):
            px, py, _ = _chip_peer(x, y, j)
            mine = _half_rows(slot(w, myq), c)
            cp = _remote(mine, mine, sems, w * 6 + j - 1, (px, py, c))
            cp.start()
            first.append(cp)
    passed = []
    for w in range(n_w):
        for j in (1, 2, 3):
            _, _, pq = _chip_peer(x, y, j)
            land = _half_rows(slot(w, pq), c)
            _remote(land, land, sems, w * 6 + j - 1, sib).wait_recv()
            cp = _remote(land, land, sems, w * 6 + 2 + j, sib)
            cp.start()
            passed.append(cp)
    for w in range(n_w):
        for j in (1, 2, 3):
            _, _, pq = _chip_peer(x, y, j)
            land = _half_rows(slot(w, pq), 1 - c)
            _remote(land, land, sems, w * 6 + 2 + j, sib).wait_recv()
    for cp in first + passed:
        cp.wait_send()


def _gather_weights(bufs, col_fam):
    n_w = len(bufs)

    def body(*refs):
        _gather_plan(refs[n_w:2 * n_w], col_fam, refs[2 * n_w:2 * n_w + 2], False)

    return pl.pallas_call(
        body, name="gather_weights", in_specs=[ANY] * n_w, out_specs=[ANY] * n_w,
        out_shape=[jax.ShapeDtypeStruct(b.shape, b.dtype) for b in bufs],
        input_output_aliases={w: w for w in range(n_w)},
        scratch_shapes=[pltpu.SemaphoreType.DMA((6 * n_w,)), pltpu.SemaphoreType.DMA((6 * n_w,))],
    )(*bufs)


def _gather_weights_async(bufs, col_fam, name, collective_id):
    refs = [jax.new_ref(b, memory_space=HBM) for b in bufs]
    _on_sequencer(name, collective_id, 6 * len(bufs), 0,
                  lambda sems, _: _gather_plan(refs, col_fam, sems, True))
    return [r[...] for r in refs]


def _grad_half(ref, col, h):
    return _half0(ref, h) if col else _half1(ref, h)


def _swap_halves_with_sibling(grads, col_fam):
    n_w = len(grads)

    def body(*refs):
        _swap_plan(refs[:n_w], refs[n_w:2 * n_w], col_fam, refs[2 * n_w:], False)

    return pl.pallas_call(
        body, name="grad_swap_sibling", in_specs=[ANY] * n_w, out_specs=[ANY] * n_w,
        out_shape=_swap_shapes(grads, col_fam),
        scratch_shapes=[pltpu.SemaphoreType.DMA((n_w,)), pltpu.SemaphoreType.DMA((n_w,))],
    )(*grads)


def _swap_shapes(grads, col_fam):
    out = []
    for w, g in enumerate(grads):
        shp = (g.shape[0] // 2, g.shape[1]) if col_fam[w] else (g.shape[0], g.shape[1] // 2, g.shape[2])
        out.append(jax.ShapeDtypeStruct(shp, g.dtype))
    return out


def _swap_plan(ins, outs, col_fam, sems, handshake):
    x, y, c = _me()
    sib = (x, y, 1 - c)
    if handshake:
        _handshake([sib])
    cps = [_remote(_grad_half(ins[w], col_fam[w], 1 - c), outs[w], sems, w, sib) for w in range(len(ins))]
    for cp in cps:
        cp.start()
    for cp in cps:
        cp.wait_recv()
    for cp in cps:
        cp.wait_send()


def _swap_halves_async(grads, col_fam, name, collective_id):
    srcs = [jax.new_ref(g, memory_space=HBM) for g in grads]
    dsts = [jax.empty_ref(s, memory_space=HBM) for s in _swap_shapes(grads, col_fam)]
    _on_sequencer(name, collective_id, len(grads), 0, lambda sems, _: _swap_plan(srcs, dsts, col_fam, sems, True))
    return [r[...] for r in srcs], [r[...] for r in dsts]


def _half_add(mine, recv, c_arr, col, name):
    if col:
        rows, n = recv.shape
        tr = rows // 2
        grid = (2,)
        in_specs = [pl.BlockSpec((tr, n), lambda i, c: (2 * c[0] + i, 0)), pl.BlockSpec((tr, n), lambda i, c: (i, 0))]
        out_spec = pl.BlockSpec((tr, n), lambda i, c: (i, 0))
    else:
        _, rows, n = recv.shape
        grid = (N_CHIPS,)
        in_specs = [pl.BlockSpec((None, rows, n), lambda q, c: (q, c[0], 0)),
                    pl.BlockSpec((None, rows, n), lambda q, c: (q, 0, 0))]
        out_spec = pl.BlockSpec((None, rows, n), lambda q, c: (q, 0, 0))

    def body(c_ref, a_ref, b_ref, o_ref):
        o_ref[...] = (a_ref[...].astype(F32) + b_ref[...].astype(F32)).astype(BF16)

    return pl.pallas_call(
        body, name=name,
        grid_spec=pltpu.PrefetchScalarGridSpec(num_scalar_prefetch=1, grid=grid, in_specs=in_specs, out_specs=out_spec),
        out_shape=jax.ShapeDtypeStruct(recv.shape, BF16), compiler_params=_cp(),
    )(c_arr, mine, recv)


def _scatter_chip_sums(sums, col_fam):
    n_w = len(sums)

    def body(*refs):
        _scatter_plan(refs[:n_w], refs[n_w:2 * n_w], col_fam, refs[2 * n_w:2 * n_w + 2], refs[2 * n_w + 2], False)

    return pl.pallas_call(
        body, name="grad_scatter_chips", in_specs=[ANY] * n_w, out_specs=[ANY] * n_w,
        out_shape=_scatter_shapes(sums, col_fam),
        scratch_shapes=[pltpu.SemaphoreType.DMA((3 * n_w,)), pltpu.SemaphoreType.DMA((3 * n_w,)),
                        pltpu.SemaphoreType.DMA((n_w,))],
    )(*sums)


def _scatter_shapes(sums, col_fam):
    out = []
    for w, s in enumerate(sums):
        shp = (s.shape[0], s.shape[1] // N_CHIPS) if col_fam[w] else s.shape[1:]
        out.append(jax.ShapeDtypeStruct((N_CHIPS,) + shp, s.dtype))
    return out


def _scatter_plan(ins, outs, col_fam, sems, lsem, handshake):
    n_w = len(ins)
    x, y, c = _me()
    myq = 2 * x + y
    if handshake:
        _handshake([_chip_peer(x, y, j)[:2] + (c,) for j in (1, 2, 3)])

    def slab(w, q):
        if col_fam[w]:
            return _col_window(ins[w], q, ins[w].shape[1] // N_CHIPS)
        return ins[w].at[q]

    local = [pltpu.make_async_copy(slab(w, myq), outs[w].at[myq], lsem.at[w]) for w in range(n_w)]
    for cp in local:
        cp.start()
    cps = []
    for w in range(n_w):
        for j in (1, 2, 3):
            px, py, pq = _chip_peer(x, y, j)
            cp = _remote(slab(w, pq), outs[w].at[myq], sems, w * 3 + j - 1, (px, py, c))
            cp.start()
            cps.append(cp)
    for w in range(n_w):
        for j in (1, 2, 3):
            _, _, pq = _chip_peer(x, y, j)
            land = outs[w].at[pq]
            _remote(land, land, sems, w * 3 + j - 1, (x, y, c)).wait_recv()
    for cp in cps:
        cp.wait_send()
    for cp in local:
        cp.wait()


def _scatter_chip_sums_async(sums, col_fam, name, collective_id):
    srcs = [jax.new_ref(s, memory_space=HBM) for s in sums]
    dsts = [jax.empty_ref(s, memory_space=HBM) for s in _scatter_shapes(sums, col_fam)]
    _on_sequencer(name, collective_id, 3 * len(sums), len(sums),
                  lambda sems, lsem: _scatter_plan(srcs, dsts, col_fam, sems, lsem, True))
    return [r[...] for r in dsts]


def _sum_chips(parts, c_arr, prev, lead, shape, name):
    _, rows, n = parts.shape
    tr = rows // 2 if rows % 32 == 0 else rows
    nblk = rows // tr

    def body(c_ref, p_ref, *rest):
        o_ref = rest[-1]
        acc = p_ref[0].astype(F32)
        for q in range(1, N_CHIPS):
            acc = acc + p_ref[q].astype(F32)
        o_ref[...] = acc

    in_specs = [pl.BlockSpec((N_CHIPS, tr, n), lambda i, c: (0, i, 0))]
    args = [c_arr, parts]
    aliases = {}
    if prev is not None:
        in_specs.append(ANY)
        args.append(prev)
        aliases = {2: 0}
    return pl.pallas_call(
        body, name=name,
        grid_spec=pltpu.PrefetchScalarGridSpec(
            num_scalar_prefetch=1, grid=(nblk,), in_specs=in_specs,
            out_specs=pl.BlockSpec((None, tr, n), lambda i, c: (lead, c[0] * nblk + i, 0))),
        out_shape=jax.ShapeDtypeStruct(shape, F32), input_output_aliases=aliases, compiler_params=_cp(),
    )(*args)


def _join_plan(outs, place, sems, handshake):
    x, y, c = _me()
    sib = (x, y, 1 - c)
    if handshake:
        _handshake([sib])

    def half(k, h):
        o, lead = place[k]
        return _half_rows(outs[o].at[lead], h)

    cps = [_remote(half(k, c), half(k, c), sems, k, sib) for k in range(len(place))]
    for cp in cps:
        cp.start()
    for k in range(len(place)):
        land = half(k, 1 - c)
        _remote(land, land, sems, k, sib).wait_recv()
    for cp in cps:
        cp.wait_send()


def _join_halves(bufs, place, name):
    n_o = len(bufs)
    n_h = len(place)

    def body(*refs):
        _join_plan(refs[n_o:2 * n_o], place, refs[2 * n_o:2 * n_o + 2], False)

    return pl.pallas_call(
        body, name=name, in_specs=[ANY] * n_o, out_specs=[ANY] * n_o,
        out_shape=[jax.ShapeDtypeStruct(b.shape, b.dtype) for b in bufs],
        input_output_aliases={k: k for k in range(n_o)},
        scratch_shapes=[pltpu.SemaphoreType.DMA((n_h,)), pltpu.SemaphoreType.DMA((n_h,))],
    )(*bufs)


def _allreduce_rows(rows):
    n_dev = 8
    n_r = len(rows)
    assert n_r <= 8

    def body(*refs):
        r_refs = refs[:n_r]
        o_ref, slots, send_sems, recv_sems = refs[n_r:]
        x, y, c = _me()
        me = 4 * x + 2 * y + c
        slots[me] = jnp.concatenate([r[...] for r in r_refs] + [jnp.zeros((8 - n_r, D_MODEL), F32)], axis=0)

        def peer(k):
            return (1 - x if k & 4 else x, 1 - y if k & 2 else y, 1 - c if k & 1 else c)

        cps = []
        for k in range(1, n_dev):
            cp = pltpu.make_async_remote_copy(src_ref=slots.at[me], dst_ref=slots.at[me], send_sem=send_sems.at[k - 1],
                                              recv_sem=recv_sems.at[k - 1], device_id=peer(k), device_id_type=MESH)
            cp.start()
            cps.append(cp)
        for k in range(1, n_dev):
            px, py, pc = peer(k)
            land = slots.at[4 * px + 2 * py + pc]
            pltpu.make_async_remote_copy(src_ref=land, dst_ref=land, send_sem=send_sems.at[k - 1],
                                         recv_sem=recv_sems.at[k - 1], device_id=peer(k),
                                         device_id_type=MESH).wait_recv()
        for cp in cps:
            cp.wait_send()
        acc = slots[0]
        for d in range(1, n_dev):
            acc = acc + slots[d]
        o_ref[...] = acc

    vm = pl.BlockSpec(memory_space=pltpu.VMEM)
    return pl.pallas_call(
        body, name="allreduce_rows", in_specs=[vm] * n_r, out_specs=vm,
        out_shape=jax.ShapeDtypeStruct((8, D_MODEL), F32),
        scratch_shapes=[pltpu.VMEM((n_dev, 8, D_MODEL), F32), pltpu.SemaphoreType.DMA((n_dev - 1,)),
                        pltpu.SemaphoreType.DMA((n_dev - 1,))],
    )(*rows)


def _adamw(w, g, m, v, name):
    shape = w.shape
    if len(shape) == 1:
        lead, rows, cols = 1, 1, shape[0]
    else:
        rows, cols = shape[-2:]
        lead = math.prod(shape[:-2])
    args = [a.reshape(lead, rows, cols) for a in (w, g, m, v)]
    tr = rows // 2 if rows % 16 == 0 else rows

    def body(w_ref, g_ref, m_ref, v_ref, d_ref, nm_ref, nv_ref):
        gv = g_ref[...]
        nm = ADAM_B1 * m_ref[...] + (1.0 - ADAM_B1) * gv
        nv = ADAM_B2 * v_ref[...] + (1.0 - ADAM_B2) * jnp.square(gv)
        m_hat = nm / (1.0 - ADAM_B1 ** ADAM_STEP)
        v_hat = nv / (1.0 - ADAM_B2 ** ADAM_STEP)
        d_ref[...] = -ADAM_LR * (m_hat / (jnp.sqrt(v_hat) + ADAM_EPS) + ADAM_WD * w_ref[...])
        nm_ref[...] = nm
        nv_ref[...] = nv

    spec = pl.BlockSpec((None, tr, cols), lambda l, i: (l, i, 0))
    outs = pl.pallas_call(
        body, name=name, grid=(lead, rows // tr), in_specs=[spec] * 4, out_specs=[spec] * 3,
        out_shape=[jax.ShapeDtypeStruct((lead, rows, cols), F32)] * 3, compiler_params=_cp(),
    )(*args)
    return [o.reshape(shape) for o in outs]


def kernel(x, a_w_in, a_sink, a_w_out, b_w_in, b_w_out, norm_mix, norm_ffn, w_gate, w_up, w_down, final_norm, loss_target, m_a_w_in, m_a_sink, m_a_w_out, m_b_w_in, m_b_w_out, m_norm_mix, m_norm_ffn, m_w_gate, m_w_up, m_w_down, m_final_norm, v_a_w_in, v_a_sink, v_a_w_out, v_b_w_in, v_b_w_out, v_norm_mix, v_norm_ffn, v_w_gate, v_w_up, v_w_down, v_final_norm):
    weights = dict(a_w_in=a_w_in, a_sink=a_sink, a_w_out=a_w_out, b_w_in=b_w_in, b_w_out=b_w_out, norm_mix=norm_mix,
                   norm_ffn=norm_ffn, w_gate=w_gate, w_up=w_up, w_down=w_down, final_norm=final_norm)
    mom = dict(a_w_in=m_a_w_in, a_sink=m_a_sink, a_w_out=m_a_w_out, b_w_in=m_b_w_in, b_w_out=m_b_w_out,
               norm_mix=m_norm_mix, norm_ffn=m_norm_ffn, w_gate=m_w_gate, w_up=m_w_up, w_down=m_w_down,
               final_norm=m_final_norm)
    var = dict(a_w_in=v_a_w_in, a_sink=v_a_sink, a_w_out=v_a_w_out, b_w_in=v_b_w_in, b_w_out=v_b_w_out,
               norm_mix=v_norm_mix, norm_ffn=v_norm_ffn, w_gate=v_w_gate, w_up=v_w_up, w_down=v_w_down,
               final_norm=v_final_norm)
    order = ["a_w_in", "a_sink", "a_w_out", "b_w_in", "b_w_out", "norm_mix", "norm_ffn", "w_gate", "w_up", "w_down",
             "final_norm"]
    swapped = ("w_gate", "w_up")
    for n in swapped:
        weights[n], mom[n], var[n] = (a.transpose(0, 2, 1) for a in (weights[n], mom[n], var[n]))
    w_gate_t, w_up_t = weights["w_gate"], weights["w_up"]

    c_arr = lax.axis_index("c").astype(jnp.int32).reshape(1)
    q_arr = (2 * lax.axis_index("x") + lax.axis_index("y")).astype(jnp.int32).reshape(1)

    def placed(w, layer, col, nm):
        return _place_shard(w, layer, q_arr, col, f"place_{nm}")

    (a_in,) = _gather_weights_async([placed(a_w_in, 0, True, "a_in")], (True,), "gather_weights_first", 6)
    a_out, wg0, wu0, wd0 = _gather_weights_async(
        [placed(a_w_out, 0, False, "a_out"), placed(w_gate_t, 0, False, "wg0"), placed(w_up_t, 0, False, "wu0"),
         placed(w_down, 0, False, "wd0")], (False,) * 4, "gather_weights_layer0", 1)
    b_in, b_out, wg1, wu1, wd1 = _gather_weights_async(
        [placed(b_w_in, 0, True, "b_in"), placed(b_w_out, 0, False, "b_out"), placed(w_gate_t, 1, False, "wg1"),
         placed(w_up_t, 1, False, "wu1"), placed(w_down, 1, False, "wd1")], (True,) + (False,) * 4,
        "gather_weights_layer1", 7)
    a_out = a_out.reshape(D_MODEL, D_MODEL)
    b_out = b_out.reshape(D_MODEL, D_MODEL)
    wg, wu, wd = (wg0, wg1), (wu0, wu1), (wd0, wd1)

    gx, grads, vecs = _local_step(x, loss_target, a_in, a_sink[0], a_out, b_in, b_out, norm_mix, norm_ffn, wg, wu, wd,
                                  final_norm)

    rows_out = D_MODEL // N_CHIPS
    partials = [grads["a_in"], grads["b_in"],
                grads["a_out"].reshape(N_CHIPS, rows_out, D_MODEL), grads["b_out"].reshape(N_CHIPS, rows_out, D_MODEL),
                grads["wg"][0], grads["wg"][1], grads["wu"][0], grads["wu"][1], grads["wd"][0], grads["wd"][1]]
    col_fam = (True, True) + (False,) * 8
    names = ("a_in", "b_in", "a_out", "b_out", "wg0", "wg1", "wu0", "wu1", "wd0", "wd1")
    contrib = [None] * len(partials)

    def reduce_group(idx, tag, ids):
        parts = [partials[k] for k in idx]
        cols = tuple(col_fam[k] for k in idx)
        if ids is None:
            theirs = _swap_halves_with_sibling(parts, cols)
        else:
            parts, theirs = _swap_halves_async(parts, cols, f"grad_swap_{tag}", ids[0])
        sums = [_half_add(p, r, c_arr, cf, f"chip_sum_{names[k]}") for p, r, cf, k in zip(parts, theirs, cols, idx)]
        if ids is None:
            out = _scatter_chip_sums(sums, cols)
        else:
            out = _scatter_chip_sums_async(sums, cols, f"grad_scatter_{tag}", ids[1])
        for k, o in zip(idx, out):
            contrib[k] = o

    reduce_group([1, 3, 5, 7, 9], "layer1", (2, 3))
    reduce_group([2, 4, 6, 8], "ffn0", (4, 5))
    reduce_group([0], "a_in", None)
    shapes = [a_w_in.shape, b_w_in.shape, a_w_out.shape, b_w_out.shape, w_down.shape, w_down.shape, w_down.shape]
    place = [(0, 0), (1, 0), (2, 0), (3, 0), (4, 0), (4, 1), (5, 0), (5, 1), (6, 0), (6, 1)]
    bufs = [None] * len(shapes)
    for p, nm, (o, lead) in zip(contrib, names, place):
        bufs[o] = _sum_chips(p, c_arr, bufs[o], lead, shapes[o], f"sum_chips_{nm}")
    g_a_in, g_b_in, g_a_out, g_b_out, g_wg, g_wu, g_wd = _join_halves(bufs, place, "grad_join_sibling")

    sink_row = jnp.pad(vecs["sink"][0:1], ((0, 0), (0, D_MODEL - LANES)))
    tot = _allreduce_rows([vecs["norm_mix"][0], vecs["norm_mix"][1], vecs["norm_ffn"][0], vecs["norm_ffn"][1],
                           vecs["final"], vecs["loss_cols"], sink_row])
    loss = (0.5 / D_MODEL) * jnp.sum(tot[5])
    gw = dict(a_w_in=g_a_in, a_sink=tot[6:7, :N_HEADS], a_w_out=g_a_out, b_w_in=g_b_in, b_w_out=g_b_out,
              norm_mix=tot[0:2], norm_ffn=tot[2:4], w_gate=g_wg, w_up=g_wu, w_down=g_wd, final_norm=tot[4])

    delta, new_m, new_v = {}, {}, {}
    for n in order:
        delta[n], new_m[n], new_v[n] = _adamw(weights[n], gw[n], mom[n], var[n], f"adamw_{n}")
    for n in swapped:
        gw[n], delta[n], new_m[n], new_v[n] = (a.transpose(0, 2, 1) for a in (gw[n], delta[n], new_m[n], new_v[n]))
    return (loss, gx, *[gw[n] for n in order], *[delta[n] for n in order], *[new_m[n] for n in order],
            *[new_v[n] for n in order])
```

```python
import math

import jax
import jax.numpy as jnp
from jax import lax
from jax.experimental import pallas as pl
from jax.experimental.pallas import tpu as pltpu
from jax.experimental.pallas import tpu_sc as plsc

F32 = jnp.float32
BF16 = jnp.bfloat16

D_MODEL = 1024
HEAD_DIM = 64
N_HEADS = 16
N_KV = 4
QKV_W = 1536
D_FF = 2816
N_CHIPS = 4
FF_SH = D_FF // N_CHIPS
HALF_WINDOW_A = 128
DILATED = ((128, 1), (512, 4), (2048, 16))
ROPE_THETA = 10000.0
RMS_EPS = 1e-6
NEG_INF = -1e30
LANES = 128
ADAM_LR, ADAM_B1, ADAM_B2, ADAM_EPS, ADAM_WD, ADAM_STEP = 0.001, 0.9, 0.999, 1e-08, 0.01, 10
VMEM_LIMIT = 56 * 1024 * 1024
GRAD_TOKENS = 2048
MESH = pl.DeviceIdType.MESH


def _cp(**kw):
    return pltpu.CompilerParams(vmem_limit_bytes=VMEM_LIMIT, **kw)


def _row_tile(t, cap):
    tm = min(cap, t)
    assert t % tm == 0
    return tm


def _rope_tables(seq, dil):
    inv = 1.0 / (ROPE_THETA ** (jnp.arange(0, HEAD_DIM, 2, dtype=F32) / HEAD_DIM))
    ang = jnp.arange(seq, dtype=F32)[:, None] * inv[None, :]
    cos, sin = jnp.cos(ang), jnp.sin(ang)
    cos = jnp.tile(cos, (1, 4))
    sin = jnp.concatenate([-sin, sin, -sin, sin], axis=1)

    def perm(t):
        return t.reshape(seq // dil, dil, LANES).transpose(1, 0, 2).reshape(seq, LANES)

    return perm(cos), perm(sin)


def _swap_halves(t):
    lane = lax.broadcasted_iota(jnp.int32, t.shape, 1)
    return jnp.where((lane % HEAD_DIM) < HEAD_DIM // 2, pltpu.roll(t, LANES - 32, 1), pltpu.roll(t, 32, 1))


def _rope(t, cos, sin):
    return t * cos + _swap_halves(t) * sin


def _rope_t(t, cos, sin):
    return t * cos - _swap_halves(t) * sin


def _to_residue(t, batch, dil):
    if dil == 1:
        return t
    s = t.shape[0] // batch
    return t.reshape(batch, s // dil, dil, t.shape[1]).transpose(0, 2, 1, 3).reshape(t.shape)


def _from_residue(t, batch, dil):
    if dil == 1:
        return t
    s = t.shape[0] // batch
    return t.reshape(batch, dil, s // dil, t.shape[1]).transpose(0, 2, 1, 3).reshape(t.shape)


def _rms_fwd(x, w, name):
    t = x.shape[0]
    tm = _row_tile(t, 512)

    def body(x_ref, w_ref, o_ref):
        o_ref[...] = _rms_tile(x_ref[...], w_ref[...]).astype(BF16)

    return pl.pallas_call(
        body, name=name, grid=(t // tm,),
        in_specs=[pl.BlockSpec((tm, D_MODEL), lambda i: (i, 0)), pl.BlockSpec((1, D_MODEL), lambda i: (0, 0))],
        out_specs=pl.BlockSpec((tm, D_MODEL), lambda i: (i, 0)),
        out_shape=jax.ShapeDtypeStruct((t, D_MODEL), BF16), compiler_params=_cp(),
    )(x, w)


def _rms_bwd_tile(xv, wv, dy, dres):
    r = lax.rsqrt(jnp.mean(xv * xv, axis=-1, keepdims=True) + RMS_EPS)
    xh = xv * r
    dxh = dy * wv
    dx = dres + r * (dxh - xh * jnp.mean(dxh * xh, axis=-1, keepdims=True))
    return dx, jnp.sum(dy * xh, axis=0, keepdims=True)


def _accumulate(ref, part):
    @pl.when(pl.program_id(0) == 0)
    def _():
        ref[...] = jnp.zeros_like(ref)

    ref[...] += part


def _rms_bwd(x, w, dhs, dres, name):
    t = x.shape[0]
    tm = _row_tile(t, 512)
    n = len(dhs)

    def body(*refs):
        x_ref, w_ref = refs[0], refs[1]
        dh_refs = refs[2:2 + n]
        dres_ref = refs[2 + n]
        dx_ref, dxb_ref, dw_ref = refs[3 + n:]
        dy = dh_refs[0][...].astype(F32)
        for k in range(1, n):
            dy = dy + dh_refs[k][...].astype(F32)
        dx, dw = _rms_bwd_tile(x_ref[...], w_ref[...], dy, dres_ref[...])
        dx_ref[...] = dx
        dxb_ref[...] = dx.astype(BF16)
        _accumulate(dw_ref, dw)

    row = pl.BlockSpec((tm, D_MODEL), lambda i: (i, 0))
    vec = pl.BlockSpec((1, D_MODEL), lambda i: (0, 0))
    return pl.pallas_call(
        body, name=name, grid=(t // tm,),
        in_specs=[row, vec] + [row] * n + [row],
        out_specs=[row, row, vec],
        out_shape=[jax.ShapeDtypeStruct((t, D_MODEL), F32), jax.ShapeDtypeStruct((t, D_MODEL), BF16),
                   jax.ShapeDtypeStruct((1, D_MODEL), F32)],
        compiler_params=_cp(),
    )(x, w, *dhs, dres)


def _final_tile(xv, wv, tv):
    r = lax.rsqrt(jnp.mean(xv * xv, axis=-1, keepdims=True) + RMS_EPS)
    xh = xv * r
    err = xh * wv - tv
    dy = err * (1.0 / D_MODEL)
    dxh = dy * wv
    dx = r * (dxh - xh * jnp.mean(dxh * xh, axis=-1, keepdims=True))
    return dx, jnp.sum(err * err, axis=0, keepdims=True), jnp.sum(dy * xh, axis=0, keepdims=True)


def _qkv_proj(h, w, cos, sin, group, name):
    t = h.shape[0]
    seq = cos.shape[0]
    tm = _row_tile(seq, 1024)
    n_q = N_HEADS * HEAD_DIM // LANES
    n_rope = (N_HEADS + N_KV) * HEAD_DIM // LANES
    scale = 1.0 / math.sqrt(HEAD_DIM)

    def body(h_ref, w_ref, cos_ref, sin_ref, o_ref):
        acc = jnp.dot(h_ref[...], w_ref[...], preferred_element_type=F32)
        cs, sn = cos_ref[...], sin_ref[...]
        csq, snq = cs * scale, sn * scale
        for c in range(QKV_W // LANES):
            blk = acc[:, c * LANES:(c + 1) * LANES]
            if c < n_q:
                blk = _rope(blk, csq, snq)
            elif c < n_rope:
                blk = _rope(blk, cs, sn)
            o_ref[:, c * LANES:(c + 1) * LANES] = blk.astype(BF16)

    tab = pl.BlockSpec((tm, LANES), lambda i: (i % (seq // tm), 0))
    return pl.pallas_call(
        body, name=name, grid=(t // tm,),
        in_specs=[pl.BlockSpec((tm, D_MODEL), lambda i: (i, 0)),
                  pl.BlockSpec((D_MODEL, QKV_W), lambda i: (0, group)), tab, tab],
        out_specs=pl.BlockSpec((tm, QKV_W), lambda i: (i, 0)),
        out_shape=jax.ShapeDtypeStruct((t, QKV_W), BF16), compiler_params=_cp(),
    )(h, w, cos, sin)


def _rms_tile(xv, wv):
    return (xv * lax.rsqrt(jnp.mean(xv * xv, axis=-1, keepdims=True) + RMS_EPS)) * wv


def _mm_res(a, w, res, nw, name):
    t, k = a.shape
    tm = _row_tile(t, 512)

    def body(a_ref, w_ref, r_ref, nw_ref, o_ref, h_ref):
        xv = r_ref[...] + jnp.dot(a_ref[...], w_ref[...], preferred_element_type=F32)
        o_ref[...] = xv
        h_ref[...] = _rms_tile(xv, nw_ref[...]).astype(BF16)

    row = pl.BlockSpec((tm, D_MODEL), lambda i: (i, 0))
    return pl.pallas_call(
        body, name=name, grid=(t // tm,),
        in_specs=[pl.BlockSpec((tm, k), lambda i: (i, 0)),
                  pl.BlockSpec((k, D_MODEL), lambda i: (0, 0), pipeline_mode=pl.Buffered(1)), row,
                  pl.BlockSpec((1, D_MODEL), lambda i: (0, 0))],
        out_specs=[row, row],
        out_shape=[jax.ShapeDtypeStruct((t, D_MODEL), F32), jax.ShapeDtypeStruct((t, D_MODEL), BF16)],
        compiler_params=_cp(),
    )(a, w, res, nw)


def _mm_nt(dy, w, group, out_dtype, name):
    t, n = dy.shape
    k = w.shape[0]
    tm = _row_tile(t, 1024)

    def body(dy_ref, w_ref, o_ref):
        o_ref[...] = lax.dot_general(dy_ref[...], w_ref[...], (((1,), (1,)), ((), ())),
                                     preferred_element_type=F32).astype(out_dtype)

    return pl.pallas_call(
        body, name=name, grid=(t // tm,),
        in_specs=[pl.BlockSpec((tm, n), lambda i: (i, 0)), pl.BlockSpec((k, n), lambda i: (0, group))],
        out_specs=pl.BlockSpec((tm, k), lambda i: (i, 0)),
        out_shape=jax.ShapeDtypeStruct((t, k), out_dtype), compiler_params=_cp(),
    )(dy, w)


def _mm_nt_rms(dy, w, x, nw, dres, name):
    t, n = dy.shape
    tm = _row_tile(t, 512)

    def body(dy_ref, w_ref, x_ref, nw_ref, dres_ref, dx_ref, dw_ref):
        dh = lax.dot_general(dy_ref[...], w_ref[...], (((1,), (1,)), ((), ())), preferred_element_type=F32)
        dx, dw = _rms_bwd_tile(x_ref[...], nw_ref[...], dh, dres_ref[...])
        dx_ref[...] = dx
        _accumulate(dw_ref, dw)

    row = pl.BlockSpec((tm, D_MODEL), lambda i: (i, 0))
    vec = pl.BlockSpec((1, D_MODEL), lambda i: (0, 0))
    return pl.pallas_call(
        body, name=name, grid=(t // tm,),
        in_specs=[pl.BlockSpec((tm, n), lambda i: (i, 0)),
                  pl.BlockSpec((D_MODEL, n), lambda i: (0, 0), pipeline_mode=pl.Buffered(1)), row, vec, row],
        out_specs=[row, vec],
        out_shape=[jax.ShapeDtypeStruct((t, D_MODEL), F32), jax.ShapeDtypeStruct((1, D_MODEL), F32)],
        compiler_params=_cp(),
    )(dy, w, x, nw, dres)


def _out_bwd(dx, w, o, name):
    t = dx.shape[0]
    tm = _row_tile(t, 512)

    def body(dx_ref, w_ref, o_ref, et_ref, do_ref, adj_ref):
        do = lax.dot_general(dx_ref[...], w_ref[...], (((1,), (1,)), ((), ())), preferred_element_type=F32)
        do_ref[...] = do.astype(BF16)
        adj_ref[...] = -_dot_split(do * o_ref[...].astype(F32), et_ref[...])

    row = pl.BlockSpec((tm, D_MODEL), lambda i: (i, 0))
    return pl.pallas_call(
        body, name=name, grid=(t // tm,),
        in_specs=[row, pl.BlockSpec((D_MODEL, D_MODEL), lambda i: (0, 0)), row,
                  pl.BlockSpec((D_MODEL, LANES), lambda i: (0, 0))],
        out_specs=[row, pl.BlockSpec((tm, LANES), lambda i: (i, 0))],
        out_shape=[jax.ShapeDtypeStruct((t, D_MODEL), BF16), jax.ShapeDtypeStruct((t, LANES), F32)],
        compiler_params=_cp(),
    )(dx, w, o, _head_expander().T)


def _mm_tn(a, bs, name):
    aq = a.ndim == 3
    bq = bs[0].ndim == 3
    t, ka = a.shape[-2:]
    n = bs[0].shape[-1]
    nq = N_CHIPS if (aq or bq) else 1
    tt = _row_tile(t, GRAD_TOKENS)
    tn = n if n <= 1024 else 768
    assert n % tn == 0
    nb = len(bs)
    steps = t // tt

    def body(*refs):
        a_ref = refs[0]
        b_refs = refs[1:1 + nb]
        o_refs = refs[1 + nb:1 + 2 * nb]
        acc_refs = refs[1 + 2 * nb:]
        s = pl.program_id(2)
        av = a_ref[...]
        for b_ref, o_ref, acc_ref in zip(b_refs, o_refs, acc_refs):
            @pl.when(s == 0)
            def _():
                acc_ref[...] = jnp.zeros_like(acc_ref)

            acc_ref[...] += lax.dot_general(av, b_ref[...], (((0,), (0,)), ((), ())), preferred_element_type=F32)

            @pl.when(s == steps - 1)
            def _():
                o_ref[...] = acc_ref[...].astype(BF16)

    a_spec = (pl.BlockSpec((None, tt, ka), lambda q, j, s: (q, s, 0)) if aq
              else pl.BlockSpec((tt, ka), lambda q, j, s: (s, 0)))
    b_spec = (pl.BlockSpec((None, tt, tn), lambda q, j, s: (q, s, j)) if bq
              else pl.BlockSpec((tt, tn), lambda q, j, s: (s, j)))
    if nq > 1:
        o_spec = pl.BlockSpec((None, ka, tn), lambda q, j, s: (q, 0, j))
        o_shape = jax.ShapeDtypeStruct((nq, ka, n), BF16)
    else:
        o_spec = pl.BlockSpec((ka, tn), lambda q, j, s: (0, j))
        o_shape = jax.ShapeDtypeStruct((ka, n), BF16)
    outs = pl.pallas_call(
        body, name=name, grid=(nq, n // tn, steps),
        in_specs=[a_spec] + [b_spec] * nb, out_specs=[o_spec] * nb, out_shape=[o_shape] * nb,
        scratch_shapes=[pltpu.VMEM((ka, tn), F32)] * nb, compiler_params=_cp(),
    )(a, *bs)
    return outs


def _sigmoid(x):
    return 1.0 / (1.0 + jnp.exp(-x))


def _ffn_up(h, wg, wu, layer, name):
    t = h.shape[0]
    tm = _row_tile(t, 1024)
    nt = (((1,), (1,)), ((), ()))

    def body(h_ref, wg_ref, wu_ref, a_ref, dg_ref, du_ref):
        hv = h_ref[...]
        g = lax.dot_general(hv, wg_ref[...], nt, preferred_element_type=F32)
        u = lax.dot_general(hv, wu_ref[...], nt, preferred_element_type=F32)
        sg = _sigmoid(g)
        silu = g * sg
        a_ref[...] = (silu * u).astype(BF16)
        dg_ref[...] = (sg * (1.0 + g * (1.0 - sg)) * u).astype(BF16)
        du_ref[...] = silu.astype(BF16)

    wspec = pl.BlockSpec((None, None, FF_SH, D_MODEL), lambda q, i: (q, layer, 0, 0))
    ospec = pl.BlockSpec((None, tm, FF_SH), lambda q, i: (q, i, 0))
    oshape = jax.ShapeDtypeStruct((N_CHIPS, t, FF_SH), BF16)
    return pl.pallas_call(
        body, name=name, grid=(N_CHIPS, t // tm),
        in_specs=[pl.BlockSpec((tm, D_MODEL), lambda q, i: (i, 0)), wspec, wspec],
        out_specs=[ospec] * 3, out_shape=[oshape] * 3, compiler_params=_cp(),
    )(h, wg, wu)


def _ffn_down(a, wd, res, layer, name, norm_w=None, head=None):
    t = a.shape[1]
    tm = _row_tile(t, 512)
    resident = pl.BlockSpec((N_CHIPS, None, FF_SH, D_MODEL), lambda i: (0, layer, 0, 0), pipeline_mode=pl.Buffered(1))
    row = pl.BlockSpec((tm, D_MODEL), lambda i: (i, 0))
    vec = pl.BlockSpec((1, D_MODEL), lambda i: (0, 0))

    def hidden(a_ref, w_ref, r_ref):
        acc = r_ref[...]
        for q in range(N_CHIPS):
            acc = acc + jnp.dot(a_ref[q], w_ref[q], preferred_element_type=F32)
        return acc

    if head is None:
        def body(a_ref, w_ref, r_ref, nw_ref, o_ref, h_ref):
            xv = hidden(a_ref, w_ref, r_ref)
            o_ref[...] = xv
            h_ref[...] = _rms_tile(xv, nw_ref[...]).astype(BF16)

        return pl.pallas_call(
            body, name=name, grid=(t // tm,),
            in_specs=[pl.BlockSpec((N_CHIPS, tm, FF_SH), lambda i: (0, i, 0)), resident, row, vec],
            out_specs=[row, row],
            out_shape=[jax.ShapeDtypeStruct((t, D_MODEL), F32), jax.ShapeDtypeStruct((t, D_MODEL), BF16)],
            compiler_params=_cp(),
        )(a, wd, res, norm_w)

    def body(a_ref, w_ref, r_ref, nw_ref, t_ref, dx_ref, dxb_ref, l_ref, dw_ref):
        dx, sq, dw = _final_tile(hidden(a_ref, w_ref, r_ref), nw_ref[...], t_ref[...])
        dx_ref[...] = dx
        dxb_ref[...] = dx.astype(BF16)
        _accumulate(l_ref, sq)
        _accumulate(dw_ref, dw)

    return pl.pallas_call(
        body, name=name, grid=(t // tm,),
        in_specs=[pl.BlockSpec((N_CHIPS, tm, FF_SH), lambda i: (0, i, 0)), resident, row, vec, row],
        out_specs=[row, row, vec, vec],
        out_shape=[jax.ShapeDtypeStruct((t, D_MODEL), F32), jax.ShapeDtypeStruct((t, D_MODEL), BF16),
                   jax.ShapeDtypeStruct((1, D_MODEL), F32), jax.ShapeDtypeStruct((1, D_MODEL), F32)],
        compiler_params=_cp(),
    )(a, wd, res, *head)


def _ffn_bwd(dy, wd, wg, wu, fg, fu, x, nw, dres, name):
    t = dy.shape[0]
    tm = _row_tile(t, 256)
    nt = (((1,), (1,)), ((), ()))

    def body(dy_ref, wd_ref, wg_ref, wu_ref, fg_ref, fu_ref, x_ref, nw_ref, dres_ref,
             dg_ref, du_ref, dx_ref, dxb_ref, dw_ref):
        dyv = dy_ref[...]
        acc = jnp.zeros((tm, D_MODEL), F32)
        for q in range(N_CHIPS):
            da = lax.dot_general(dyv, wd_ref[q], nt, preferred_element_type=F32)
            dg = (da * fg_ref[q].astype(F32)).astype(BF16)
            du = (da * fu_ref[q].astype(F32)).astype(BF16)
            dg_ref[q] = dg
            du_ref[q] = du
            acc = acc + jnp.dot(dg, wg_ref[q], preferred_element_type=F32)
            acc = acc + jnp.dot(du, wu_ref[q], preferred_element_type=F32)
        dx, dw = _rms_bwd_tile(x_ref[...], nw_ref[...], acc, dres_ref[...])
        dx_ref[...] = dx
        dxb_ref[...] = dx.astype(BF16)
        _accumulate(dw_ref, dw)

    aspec = pl.BlockSpec((N_CHIPS, tm, FF_SH), lambda i: (0, i, 0))
    wspec = pl.BlockSpec((N_CHIPS, None, FF_SH, D_MODEL), lambda i: (0, 0, 0, 0), pipeline_mode=pl.Buffered(1))
    row = pl.BlockSpec((tm, D_MODEL), lambda i: (i, 0))
    vec = pl.BlockSpec((1, D_MODEL), lambda i: (0, 0))
    ashape = jax.ShapeDtypeStruct((N_CHIPS, t, FF_SH), BF16)
    return pl.pallas_call(
        body, name=name, grid=(t // tm,),
        in_specs=[row, wspec, wspec, wspec, aspec, aspec, row, vec, row],
        out_specs=[aspec, aspec, row, row, vec],
        out_shape=[ashape, ashape, jax.ShapeDtypeStruct((t, D_MODEL), F32), jax.ShapeDtypeStruct((t, D_MODEL), BF16),
                   jax.ShapeDtypeStruct((1, D_MODEL), F32)],
        compiler_params=_cp(),
    )(dy, wd, wg, wu, fg, fu, x, nw, dres)


def _attn_geometry(length, half_window):
    qb = min(LANES, length)
    kw = min(qb + 2 * half_window, length)
    return qb, kw, length // qb


def _dup_kv(src_ref, dst_ref, s, length):
    ch = min(length, 256)
    lo = lax.broadcasted_iota(jnp.int32, (ch, LANES), 1) < HEAD_DIM

    def chunk(c, carry):
        r0 = pl.multiple_of(c * ch, ch)
        for j in range(N_KV // 2):
            tile = src_ref[s, pl.ds(r0, ch), j * LANES:(j + 1) * LANES].astype(F32)
            rolled = pltpu.roll(tile, HEAD_DIM, 1)
            dst_ref[2 * j, pl.ds(r0, ch), :] = jnp.where(lo, tile, rolled).astype(BF16)
            dst_ref[2 * j + 1, pl.ds(r0, ch), :] = jnp.where(lo, rolled, tile).astype(BF16)
        return carry

    lax.fori_loop(0, length // ch, chunk, 0)


def _stack_heads(ref, s, q0, qb, g):
    lo = lax.broadcasted_iota(jnp.int32, (qb, LANES), 1) < HEAD_DIM
    parts = []
    for a in range(4):
        col = (2 * g + a // 2) * LANES
        tile = ref[s, pl.ds(q0, qb), col:col + LANES]
        keep = lo if a % 2 == 0 else jnp.logical_not(lo)
        parts.append(jnp.where(keep, tile, jnp.zeros_like(tile)))
    return jnp.concatenate(parts, axis=0)


def _unstack_pair_t(stacked_t, qb, pair):
    both = jnp.concatenate([stacked_t[:, (2 * pair) * qb:(2 * pair + 1) * qb],
                            stacked_t[:, (2 * pair + 1) * qb:(2 * pair + 2) * qb]], axis=0)
    return both.T


def _band_mask_t(q0, k0, qb, kw, half_window):
    key = lax.broadcasted_iota(jnp.int32, (kw, 4 * qb), 0)
    qry = lax.broadcasted_iota(jnp.int32, (kw, 4 * qb), 1) & (qb - 1)
    return jnp.abs((q0 + qry) - (k0 + key)) <= half_window


def _block_origin(i, qb, kw, half_window, length):
    if isinstance(i, int):
        return i * qb, min(max(i * qb - half_window, 0), length - kw)
    return (pl.multiple_of(i * qb, qb),
            pl.multiple_of(jnp.clip(i * qb - half_window, 0, length - kw), HEAD_DIM))


def _head_row(vals, qb):
    return jnp.concatenate([jnp.broadcast_to(v, (1, qb)).astype(F32) for v in vals], axis=1)


def _attn_fwd(qkv, sink, n_seq, length, half_window, seq_blk, out_dtype, name):
    qb, kw, nblk = _attn_geometry(length, half_window)
    with_sink = sink is not None
    nt = (((1,), (1,)), ((), ()))
    tn = (((0,), (0,)), ((), ()))
    qkv3 = qkv.reshape(n_seq, length, QKV_W)

    def body(*refs):
        refs = list(refs)
        sink_ref = refs.pop(0) if with_sink else None
        q_ref, k_ref, v_ref, o_ref, lse_ref = refs[:5]
        kx_ref, vx_ref = refs[-2:]
        head_row = lax.broadcasted_iota(jnp.int32, (N_HEADS, qb), 0)
        for s in range(seq_blk):
            _dup_kv(k_ref, kx_ref, s, length)
            _dup_kv(v_ref, vx_ref, s, length)

            def block(i, carry):
                q0, k0 = _block_origin(i, qb, kw, half_window, length)
                valid = _band_mask_t(q0, k0, qb, kw, half_window)
                lse_tile = jnp.zeros((N_HEADS, qb), F32)
                groups = range(N_KV)
                sts = [lax.dot_general(kx_ref[g, pl.ds(k0, kw), :], _stack_heads(q_ref, s, q0, qb, g), nt,
                                       preferred_element_type=F32) for g in groups]
                sts = [jnp.where(valid, st, NEG_INF) for st in sts]
                ms = [jnp.max(st, axis=0, keepdims=True) for st in sts]
                if with_sink:
                    sks = [_head_row([sink_ref[4 * g + a] for a in range(4)], qb) for g in groups]
                    ms = [jnp.maximum(m, sk) for m, sk in zip(ms, sks)]
                es = [jnp.exp(st - m) for st, m in zip(sts, ms)]
                dens = [jnp.sum(e, axis=0, keepdims=True) for e in es]
                if with_sink:
                    dens = [den + jnp.exp(sk - m) for den, sk, m in zip(dens, sks, ms)]
                ots = [lax.dot_general(vx_ref[g, pl.ds(k0, kw), 0:HEAD_DIM], es[g].astype(BF16), tn,
                                       preferred_element_type=F32) / dens[g] for g in groups]
                for g in groups:
                    for pair in range(2):
                        col = (2 * g + pair) * LANES
                        o_ref[s, pl.ds(q0, qb), col:col + LANES] = _unstack_pair_t(ots[g], qb, pair).astype(out_dtype)
                    lse = ms[g] + jnp.log(dens[g])
                    for a in range(4):
                        lse_tile = jnp.where(head_row == 4 * g + a, lse[:, a * qb:(a + 1) * qb], lse_tile)
                lse_ref[s, :, pl.ds(q0, qb)] = lse_tile
                return carry

            if nblk == 1:
                block(0, 0)
            else:
                lax.fori_loop(0, nblk, block, 0)

    in_specs = [pl.BlockSpec((seq_blk, length, N_HEADS * HEAD_DIM), lambda n: (n, 0, 0)),
                pl.BlockSpec((seq_blk, length, N_KV * HEAD_DIM), lambda n: (n, 0, 4)),
                pl.BlockSpec((seq_blk, length, N_KV * HEAD_DIM), lambda n: (n, 0, 5))]
    args = [qkv3, qkv3, qkv3]
    if with_sink:
        in_specs.insert(0, pl.BlockSpec(memory_space=pltpu.SMEM))
        args.insert(0, sink)
    out_specs = [pl.BlockSpec((seq_blk, length, D_MODEL), lambda n: (n, 0, 0)),
                 pl.BlockSpec((seq_blk, N_HEADS, length), lambda n: (n, 0, 0))]
    out_shape = [jax.ShapeDtypeStruct((n_seq, length, D_MODEL), out_dtype),
                 jax.ShapeDtypeStruct((n_seq, N_HEADS, length), F32)]
    o, lse = pl.pallas_call(
        body, name=name, grid=(n_seq // seq_blk,), in_specs=in_specs, out_specs=out_specs, out_shape=out_shape,
        scratch_shapes=[pltpu.VMEM((N_KV, length, LANES), BF16), pltpu.VMEM((N_KV, length, LANES), BF16)],
        compiler_params=_cp(),
    )(*args)
    return o.reshape(n_seq * length, D_MODEL), lse


def _attn_bwd(qkv, do, adj, lse, sink, cos, sin, n_seq, length, half_window, seq_blk, dil, name):
    qb, kw, nblk = _attn_geometry(length, half_window)
    scale = 1.0 / math.sqrt(HEAD_DIM)
    with_sink = sink is not None
    nt = (((1,), (1,)), ((), ()))
    tn = (((0,), (0,)), ((), ()))
    qkv3 = qkv.reshape(n_seq, length, QKV_W)
    do3 = do.reshape(n_seq, length, D_MODEL)
    tabs = [t.reshape(dil, length, LANES) for t in (cos, sin)]
    tab_blocks = dil // seq_blk if dil >= seq_blk else 1

    def body(*refs):
        refs = list(refs)
        sink_ref = refs.pop(0) if with_sink else None
        q_ref, k_ref, v_ref, do_ref, aux_ref, lse_ref, cos_ref, sin_ref, dqkv_ref = refs[:9]
        ds_ref = refs[9] if with_sink else None
        kx_ref, vx_ref, dkx_ref, dvx_ref = refs[-4:]
        lane = lax.broadcasted_iota(jnp.int32, (1, LANES), 1)
        if with_sink:
            @pl.when(pl.program_id(0) == 0)
            def _():
                ds_ref[...] = jnp.zeros_like(ds_ref)

        for s in range(seq_blk):
            ts = s % dil
            _dup_kv(k_ref, kx_ref, s, length)
            _dup_kv(v_ref, vx_ref, s, length)
            dkx_ref[...] = jnp.zeros_like(dkx_ref)
            dvx_ref[...] = jnp.zeros_like(dvx_ref)

            def block(i, dsink):
                q0, k0 = _block_origin(i, qb, kw, half_window, length)
                valid = _band_mask_t(q0, k0, qb, kw, half_window)
                cs = cos_ref[ts, pl.ds(q0, qb), :] * scale
                sn = sin_ref[ts, pl.ds(q0, qb), :] * scale
                adj_tile = aux_ref[s, :, pl.ds(q0, qb)]
                lse_tile = lse_ref[s, :, pl.ds(q0, qb)]
                groups = range(N_KV)
                qss = [_stack_heads(q_ref, s, q0, qb, g) for g in groups]
                doss = [_stack_heads(do_ref, s, q0, qb, g) for g in groups]
                kxs = [kx_ref[g, pl.ds(k0, kw), :] for g in groups]
                sts = [lax.dot_general(kxs[g], qss[g], nt, preferred_element_type=F32) for g in groups]
                dpts = [lax.dot_general(vx_ref[g, pl.ds(k0, kw), :], doss[g], nt, preferred_element_type=F32)
                        for g in groups]
                lses = [_head_row([lse_tile[4 * g + a:4 * g + a + 1, :] for a in range(4)], qb) for g in groups]
                shifts = [_head_row([adj_tile[4 * g + a:4 * g + a + 1, :] for a in range(4)], qb) for g in groups]
                pts = [jnp.exp(jnp.where(valid, sts[g], NEG_INF) - lses[g]) for g in groups]
                dsbs = [(pts[g] * (dpts[g] + shifts[g])).astype(BF16) for g in groups]
                pbs = [pt.astype(BF16) for pt in pts]
                if with_sink:
                    for g in groups:
                        sk = _head_row([sink_ref[4 * g + a] for a in range(4)], qb)
                        dsk = jnp.exp(sk - lses[g]) * shifts[g]
                        for a in range(4):
                            tot = jnp.sum(dsk[:, a * qb:(a + 1) * qb], axis=1, keepdims=True)
                            dsink = dsink + jnp.where(lane == 4 * g + a, tot, 0.0)
                dqts = [lax.dot_general(kx_ref[g, pl.ds(k0, kw), 0:HEAD_DIM], dsbs[g], tn, preferred_element_type=F32)
                        for g in groups]
                for g in groups:
                    for pair in range(2):
                        col = (2 * g + pair) * LANES
                        tile = _rope_t(_unstack_pair_t(dqts[g], qb, pair), cs, sn)
                        dqkv_ref[s, pl.ds(q0, qb), col:col + LANES] = tile.astype(BF16)
                for g in groups:
                    dkx_ref[g, pl.ds(k0, kw), :] += jnp.dot(dsbs[g], qss[g], preferred_element_type=F32)
                    dvx_ref[g, pl.ds(k0, kw), :] += jnp.dot(pbs[g], doss[g], preferred_element_type=F32)
                return dsink

            if nblk == 1:
                dsink = block(0, jnp.zeros((1, LANES), F32))
            else:
                dsink = lax.fori_loop(0, nblk, block, jnp.zeros((1, LANES), F32))
            if with_sink:
                ds_ref[0:1, :] += dsink

            ch = min(length, 256)
            lo_c = lax.broadcasted_iota(jnp.int32, (ch, LANES), 1) < HEAD_DIM

            def fin(c, carry):
                r0 = pl.multiple_of(c * ch, ch)
                cs = cos_ref[ts, pl.ds(r0, ch), :]
                sn = sin_ref[ts, pl.ds(r0, ch), :]
                for j in range(N_KV // 2):
                    both = []
                    for acc_ref in (dkx_ref, dvx_ref):
                        t0 = acc_ref[2 * j, pl.ds(r0, ch), :]
                        t1 = acc_ref[2 * j + 1, pl.ds(r0, ch), :]
                        t0 = t0 + pltpu.roll(t0, HEAD_DIM, 1)
                        t1 = t1 + pltpu.roll(t1, HEAD_DIM, 1)
                        both.append(jnp.where(lo_c, t0, t1))
                    kcol = N_HEADS * HEAD_DIM + j * LANES
                    vcol = (N_HEADS + N_KV) * HEAD_DIM + j * LANES
                    dqkv_ref[s, pl.ds(r0, ch), kcol:kcol + LANES] = _rope_t(both[0], cs, sn).astype(BF16)
                    dqkv_ref[s, pl.ds(r0, ch), vcol:vcol + LANES] = both[1].astype(BF16)
                return carry

            lax.fori_loop(0, length // ch, fin, 0)

    seq_map = lambda n: (n, 0, 0)
    tab_map = (lambda n: (n % tab_blocks, 0, 0)) if dil >= seq_blk else (lambda n: (0, 0, 0))
    tab_rows = min(seq_blk, dil)
    in_specs = [pl.BlockSpec((seq_blk, length, N_HEADS * HEAD_DIM), seq_map),
                pl.BlockSpec((seq_blk, length, N_KV * HEAD_DIM), lambda n: (n, 0, 4)),
                pl.BlockSpec((seq_blk, length, N_KV * HEAD_DIM), lambda n: (n, 0, 5)),
                pl.BlockSpec((seq_blk, length, D_MODEL), seq_map),
                pl.BlockSpec((seq_blk, N_HEADS, length), seq_map),
                pl.BlockSpec((seq_blk, N_HEADS, length), seq_map),
                pl.BlockSpec((tab_rows, length, LANES), tab_map),
                pl.BlockSpec((tab_rows, length, LANES), tab_map)]
    args = [qkv3, qkv3, qkv3, do3, adj, lse] + tabs
    if with_sink:
        in_specs.insert(0, pl.BlockSpec(memory_space=pltpu.SMEM))
        args.insert(0, sink)
    out_specs = [pl.BlockSpec((seq_blk, length, QKV_W), seq_map)]
    out_shape = [jax.ShapeDtypeStruct((n_seq, length, QKV_W), BF16)]
    if with_sink:
        out_specs.append(pl.BlockSpec((8, LANES), lambda n: (0, 0)))
        out_shape.append(jax.ShapeDtypeStruct((8, LANES), F32))
    outs = pl.pallas_call(
        body, name=name, grid=(n_seq // seq_blk,), in_specs=in_specs, out_specs=out_specs, out_shape=out_shape,
        scratch_shapes=[pltpu.VMEM((N_KV, length, LANES), BF16), pltpu.VMEM((N_KV, length, LANES), BF16),
                        pltpu.VMEM((N_KV, length, LANES), F32), pltpu.VMEM((N_KV, length, LANES), F32)],
        compiler_params=_cp(),
    )(*args)
    dqkv = outs[0].reshape(n_seq * length, QKV_W)
    return (dqkv, outs[1]) if with_sink else (dqkv, None)


def _head_expander():
    h = jnp.arange(LANES)[:, None]
    l = jnp.arange(D_MODEL)[None, :]
    return (l // HEAD_DIM == h).astype(BF16)


def _dot_split(a, e):
    hi = a.astype(BF16)
    lo = (a - hi.astype(F32)).astype(BF16)
    return jnp.dot(hi, e, preferred_element_type=F32) + jnp.dot(lo, e, preferred_element_type=F32)


def _mix_weights(lses):
    m = jnp.maximum(jnp.maximum(lses[0], lses[1]), lses[2])
    es = [jnp.exp(v - m) for v in lses]
    tot = es[0] + es[1] + es[2]
    return [e / tot for e in es]


def _mix_fwd(os_, lses, name):
    t = os_[0].shape[0]
    tm = _row_tile(t, 512)

    def body(o0, o1, o2, l0, l1, l2, e_ref, out_ref):
        wts = _mix_weights([l0[...], l1[...], l2[...]])
        acc = jnp.zeros((tm, D_MODEL), F32)
        for w, o_ref in zip(wts, (o0, o1, o2)):
            acc = acc + _dot_split(w, e_ref[...]) * o_ref[...]
        out_ref[...] = acc.astype(BF16)

    row = pl.BlockSpec((tm, D_MODEL), lambda i: (i, 0))
    lrow = pl.BlockSpec((tm, LANES), lambda i: (i, 0))
    return pl.pallas_call(
        body, name=name, grid=(t // tm,),
        in_specs=[row] * 3 + [lrow] * 3 + [pl.BlockSpec((LANES, D_MODEL), lambda i: (0, 0))],
        out_specs=row, out_shape=jax.ShapeDtypeStruct((t, D_MODEL), BF16), compiler_params=_cp(),
    )(*os_, *lses, _head_expander())


def _mix_bwd(dx, w_out, os_, lses, name):
    t = dx.shape[0]
    tm = _row_tile(t, 512)

    def body(d_ref, w_ref, o0, o1, o2, l0, l1, l2, e_ref, et_ref, do0, do1, do2, a0, a1, a2):
        wts = _mix_weights([l0[...], l1[...], l2[...]])
        dv = lax.dot_general(d_ref[...], w_ref[...], (((1,), (1,)), ((), ())), preferred_element_type=F32)
        cs = [_dot_split(dv * o_ref[...], et_ref[...]) for o_ref in (o0, o1, o2)]
        mean_c = wts[0] * cs[0] + wts[1] * cs[1] + wts[2] * cs[2]
        for w, c, do_ref, a_ref in zip(wts, cs, (do0, do1, do2), (a0, a1, a2)):
            do_ref[...] = (_dot_split(w, e_ref[...]) * dv).astype(BF16)
            a_ref[...] = w * (c - mean_c) - w * c

    row = pl.BlockSpec((tm, D_MODEL), lambda i: (i, 0))
    lrow = pl.BlockSpec((tm, LANES), lambda i: (i, 0))
    e = _head_expander()
    return pl.pallas_call(
        body, name=name, grid=(t // tm,),
        in_specs=[row, pl.BlockSpec((D_MODEL, D_MODEL), lambda i: (0, 0), pipeline_mode=pl.Buffered(1))]
        + [row] * 3 + [lrow] * 3 + [pl.BlockSpec((LANES, D_MODEL), lambda i: (0, 0)),
                                    pl.BlockSpec((D_MODEL, LANES), lambda i: (0, 0))],
        out_specs=[row] * 3 + [lrow] * 3,
        out_shape=[jax.ShapeDtypeStruct((t, D_MODEL), BF16)] * 3 + [jax.ShapeDtypeStruct((t, LANES), F32)] * 3,
        compiler_params=_cp(),
    )(dx, w_out, *os_, *lses, e, e.T)


def _stats_to_tokens(stat, batch, dil):
    n_seq, _, length = stat.shape
    t = stat.transpose(0, 2, 1).reshape(n_seq * length, N_HEADS)
    return _from_residue(jnp.pad(t, ((0, 0), (0, LANES - N_HEADS))), batch, dil)


def _stats_from_tokens(stat, batch, dil, n_seq, length):
    t = _to_residue(stat[:, :N_HEADS], batch, dil)
    return t.reshape(n_seq, length, N_HEADS).transpose(0, 2, 1)


def _group_geometry(batch, seq, dil, window):
    length = seq // dil
    n_seq = batch * dil
    seq_blk = max(1, min(dil, 1024 // length))
    return n_seq, length, (window // 2) // dil, seq_blk


def _local_step(x, target, a_in, a_sink, a_out, b_in, b_out, norm_mix, norm_ffn, wg, wu, wd, final_norm):
    batch, seq, _ = x.shape
    t = batch * seq
    x0 = x.reshape(t, D_MODEL)
    tgt = target.reshape(t, D_MODEL)
    tabs = {d: _rope_tables(seq, d) for _, d in DILATED}
    nm = [norm_mix[i:i + 1] for i in range(2)]
    nf = [norm_ffn[i:i + 1] for i in range(2)]

    h0 = _rms_fwd(x0, nm[0], "rms_mix0")
    qkv0 = _qkv_proj(h0, a_in, *tabs[1], 0, "qkv0")
    o0, lse0 = _attn_fwd(qkv0, a_sink, batch, seq, HALF_WINDOW_A, 1, BF16, "attn0")
    x1, hf0 = _mm_res(o0, a_out, x0, nf[0], "out0")
    act0, g0, u0 = _ffn_up(hf0, wg[0], wu[0], 0, "ffn_up0")
    x2, h1 = _ffn_down(act0, wd[0], x1, 0, "ffn_down0", norm_w=nm[1])

    geo = [_group_geometry(batch, seq, d, w) for w, d in DILATED]
    h1g, qkv1, o1, lse1, lse1r = [], [], [], [], []
    for gi, (_, d) in enumerate(DILATED):
        n_seq, length, hw, sb = geo[gi]
        hp = _to_residue(h1, batch, d)
        pj = _qkv_proj(hp, b_in, *tabs[d], gi, f"qkv1_{gi}")
        o, lse = _attn_fwd(pj, None, n_seq, length, hw, sb, BF16, f"attn1_{gi}")
        h1g.append(hp)
        qkv1.append(pj)
        o1.append(_from_residue(o, batch, d))
        lse1r.append(lse)
        lse1.append(_stats_to_tokens(lse, batch, d))
    omix = _mix_fwd(o1, lse1, "mix")
    x3, hf1 = _mm_res(omix, b_out, x2, nf[1], "out1")
    act1, g1, u1 = _ffn_up(hf1, wg[1], wu[1], 0, "ffn_up1")
    dx4, dx4b, loss_cols, d_final = _ffn_down(act1, wd[1], x3, 0, "ffn_down1_loss",
                                                     head=(final_norm.reshape(1, D_MODEL), tgt))

    def ffn_bwd(dxo, dxob, x_mid, hf, g, u, act, layer):
        dg, du, dxm, dxmb, d_nf = _ffn_bwd(dxob, wd[layer], wg[layer], wu[layer], g, u, x_mid, nf[layer], dxo,
                                           f"ffn_bwd{layer}")
        (d_wd,) = _mm_tn(act, [dxob], f"grad_wd{layer}")
        (d_wgt,) = _mm_tn(dg, [hf], f"grad_wg{layer}")
        (d_wut,) = _mm_tn(du, [hf], f"grad_wu{layer}")
        return dxm, dxmb, d_nf, d_wgt, d_wut, d_wd

    dx3, dx3b, d_nf1, d_wg1, d_wu1, d_wd1 = ffn_bwd(dx4, dx4b, x3, hf1, g1, u1, act1, 1)

    (d_b_out,) = _mm_tn(omix, [dx3b], "grad_b_out")
    mb = _mix_bwd(dx3b, b_out, o1, lse1, "out1_mix_bwd")
    dh1, d_b_in = [], []
    for gi, (_, d) in enumerate(DILATED):
        n_seq, length, hw, sb = geo[gi]
        dog = _to_residue(mb[gi], batch, d)
        adj = _stats_from_tokens(mb[3 + gi], batch, d, n_seq, length)
        dpj, _ = _attn_bwd(qkv1[gi], dog, adj, lse1r[gi], None, *tabs[d], n_seq, length, hw, sb, d, f"attn1_bwd{gi}")
        (dw,) = _mm_tn(h1g[gi], [dpj], f"grad_b_in{gi}")
        d_b_in.append(dw)
        dh1.append(_from_residue(_mm_nt(dpj, b_in, gi, BF16, f"qkv1_bwd{gi}"), batch, d))
    dx2, dx2b, d_nm1 = _rms_bwd(x2, nm[1], dh1, dx3, "rms_mix_bwd1")

    dx1, dx1b, d_nf0, d_wg0, d_wu0, d_wd0 = ffn_bwd(dx2, dx2b, x1, hf0, g0, u0, act0, 0)

    do0, adj0 = _out_bwd(dx1b, a_out, o0, "out0_bwd")
    (d_a_out,) = _mm_tn(o0, [dx1b], "grad_a_out")
    adj0 = _stats_from_tokens(adj0, batch, 1, batch, seq)
    dqkv0, d_sink = _attn_bwd(qkv0, do0, adj0, lse0, a_sink, *tabs[1], batch, seq, HALF_WINDOW_A, 1, 1, "attn0_bwd")
    (d_a_in,) = _mm_tn(h0, [dqkv0], "grad_a_in")
    gx, d_nm0 = _mm_nt_rms(dqkv0, a_in, x0, nm[0], dx1, "qkv0_bwd")

    grads = dict(a_in=d_a_in, a_out=d_a_out, b_in=jnp.concatenate(d_b_in, axis=1), b_out=d_b_out,
                 wg=(d_wg0, d_wg1), wu=(d_wu0, d_wu1), wd=(d_wd0, d_wd1))
    vecs = dict(norm_mix=(d_nm0, d_nm1), norm_ffn=(d_nf0, d_nf1), final=d_final, loss_cols=loss_cols, sink=d_sink)
    return gx.reshape(x.shape), grads, vecs


ANY = pl.BlockSpec(memory_space=pl.ANY)
HBM = pltpu.MemorySpace.HBM


def _me():
    return lax.axis_index("x"), lax.axis_index("y"), lax.axis_index("c")


def _chip_peer(x, y, j):
    px = 1 - x if j & 2 else x
    py = 1 - y if j & 1 else y
    return px, py, 2 * px + py


def _remote(src, dst, sems, k, dev):
    return pltpu.make_async_remote_copy(src_ref=src, dst_ref=dst, send_sem=sems[0].at[k], recv_sem=sems[1].at[k],
                                        device_id=dev, device_id_type=MESH)


def _col_window(ref, q, width):
    return ref.at[:, pl.ds(pl.multiple_of(q * width, LANES), width)]


def _half0(ref, h):
    n = ref.shape[0] // 2
    return ref.at[pl.ds(h * n, n)]


def _half1(ref, h):
    n = ref.shape[1] // 2
    return ref.at[:, pl.ds(h * n, n)]


def _half_rows(ref, h):
    n = ref.shape[-2] // 2
    if len(ref.shape) == 2:
        return ref.at[pl.ds(h * n, n)]
    return ref.at[:, pl.ds(h * n, n)]


def _place_shard(w, layer, q_arr, col, name):
    _, rows, cols = w.shape

    def body(q_ref, w_ref, o_ref):
        o_ref[...] = w_ref[...].astype(BF16)

    if col:
        out_spec = pl.BlockSpec((rows, cols), lambda l, q: (0, q[0]))
        out_shape = jax.ShapeDtypeStruct((rows, N_CHIPS * cols), BF16)
    else:
        out_spec = pl.BlockSpec((None, None, rows, cols), lambda l, q: (q[0], 0, 0, 0))
        out_shape = jax.ShapeDtypeStruct((N_CHIPS, 1, rows, cols), BF16)
    return pl.pallas_call(
        body, name=name,
        grid_spec=pltpu.PrefetchScalarGridSpec(
            num_scalar_prefetch=1, grid=(1,),
            in_specs=[pl.BlockSpec((None, rows, cols), lambda l, q: (layer, 0, 0))], out_specs=out_spec),
        out_shape=out_shape, compiler_params=_cp(),
    )(q_arr, w)


def _handshake(peers):
    barrier = pltpu.get_barrier_semaphore()
    for p in peers:
        pl.semaphore_signal(barrier, inc=1, device_id=p, device_id_type=MESH)
    pl.semaphore_wait(barrier, len(peers))


def _on_sequencer(name, collective_id, n_sem, n_local, body):
    @pl.kernel(mesh=plsc.ScalarSubcoreMesh(axis_name="seq", num_cores=1), name=name,
               scratch_types=(pltpu.SemaphoreType.DMA((n_sem,)), pltpu.SemaphoreType.DMA((n_sem,)),
                              pltpu.SemaphoreType.DMA((max(n_local, 1),))),
               compiler_params=pltpu.CompilerParams(collective_id=collective_id))
    def launch(send_sems, recv_sems, local_sems):
        body((send_sems, recv_sems), local_sems)

    launch()


def _gather_plan(outs, col_fam, sems, handshake):
    n_w = len(outs)
    x, y, c = _me()
    myq = 2 * x + y
    sib = (x, y, 1 - c)
    if handshake:
        _handshake([sib] + [_chip_peer(x, y, j)[:2] + (c,) for j in (1, 2, 3)])

    def slot(w, q):
        if col_fam[w]:
            return _col_window(outs[w], q, outs[w].shape[1] // N_CHIPS)
        return outs[w].at[q]

    first = []
    for w in range(n_w):
        for j in (1, 2, 3):
            px, py, _ = _chip_peer(x, y, j)
            mine = _half_rows(slot(w, myq), c)
            cp = _remote(mine, mine, sems, w * 6 + j - 1, (px, py, c))
            cp.start()
            first.append(cp)
    passed = []
    for w in range(n_w):
        for j in (1, 2, 3):
            _, _, pq = _chip_peer(x, y, j)
            land = _half_rows(slot(w, pq), c)
            _remote(land, land, sems, w * 6 + j - 1, sib).wait_recv()
            cp = _remote(land, land, sems, w * 6 + 2 + j, sib)
            cp.start()
            passed.append(cp)
    for w in range(n_w):
        for j in (1, 2, 3):
            _, _, pq = _chip_peer(x, y, j)
            land = _half_rows(slot(w, pq), 1 - c)
            _remote(land, land, sems, w * 6 + 2 + j, sib).wait_recv()
    for cp in first + passed:
        cp.wait_send()


def _gather_weights(bufs, col_fam):
    n_w = len(bufs)

    def body(*refs):
        _gather_plan(refs[n_w:2 * n_w], col_fam, refs[2 * n_w:2 * n_w + 2], False)

    return pl.pallas_call(
        body, name="gather_weights", in_specs=[ANY] * n_w, out_specs=[ANY] * n_w,
        out_shape=[jax.ShapeDtypeStruct(b.shape, b.dtype) for b in bufs],
        input_output_aliases={w: w for w in range(n_w)},
        scratch_shapes=[pltpu.SemaphoreType.DMA((6 * n_w,)), pltpu.SemaphoreType.DMA((6 * n_w,))],
    )(*bufs)


def _gather_weights_async(bufs, col_fam, name, collective_id):
    refs = [jax.new_ref(b, memory_space=HBM) for b in bufs]
    _on_sequencer(name, collective_id, 6 * len(bufs), 0,
                  lambda sems, _: _gather_plan(refs, col_fam, sems, True))
    return [r[...] for r in refs]


def _grad_half(ref, col, h):
    return _half0(ref, h) if col else _half1(ref, h)


def _swap_halves_with_sibling(grads, col_fam):
    n_w = len(grads)

    def body(*refs):
        _swap_plan(refs[:n_w], refs[n_w:2 * n_w], col_fam, refs[2 * n_w:], False)

    return pl.pallas_call(
        body, name="grad_swap_sibling", in_specs=[ANY] * n_w, out_specs=[ANY] * n_w,
        out_shape=_swap_shapes(grads, col_fam),
        scratch_shapes=[pltpu.SemaphoreType.DMA((n_w,)), pltpu.SemaphoreType.DMA((n_w,))],
    )(*grads)


def _swap_shapes(grads, col_fam):
    out = []
    for w, g in enumerate(grads):
        shp = (g.shape[0] // 2, g.shape[1]) if col_fam[w] else (g.shape[0], g.shape[1] // 2, g.shape[2])
        out.append(jax.ShapeDtypeStruct(shp, g.dtype))
    return out


def _swap_plan(ins, outs, col_fam, sems, handshake):
    x, y, c = _me()
    sib = (x, y, 1 - c)
    if handshake:
        _handshake([sib])
    cps = [_remote(_grad_half(ins[w], col_fam[w], 1 - c), outs[w], sems, w, sib) for w in range(len(ins))]
    for cp in cps:
        cp.start()
    for cp in cps:
        cp.wait_recv()
    for cp in cps:
        cp.wait_send()


def _swap_halves_async(grads, col_fam, name, collective_id):
    srcs = [jax.new_ref(g, memory_space=HBM) for g in grads]
    dsts = [jax.empty_ref(s, memory_space=HBM) for s in _swap_shapes(grads, col_fam)]
    _on_sequencer(name, collective_id, len(grads), 0, lambda sems, _: _swap_plan(srcs, dsts, col_fam, sems, True))
    return [r[...] for r in srcs], [r[...] for r in dsts]


def _half_add(mines, recvs, c_arr, col_fam, name):
    n_w = len(mines)
    mine_specs, recv_specs = [], []
    for recv, col in zip(recvs, col_fam):
        if col:
            rows, n = recv.shape
            tr = rows // N_CHIPS
            mine_specs.append(pl.BlockSpec((tr, n), lambda i, c: (N_CHIPS * c[0] + i, 0)))
            recv_specs.append(pl.BlockSpec((tr, n), lambda i, c: (i, 0)))
        else:
            _, rows, n = recv.shape
            mine_specs.append(pl.BlockSpec((None, rows, n), lambda q, c: (q, c[0], 0)))
            recv_specs.append(pl.BlockSpec((None, rows, n), lambda q, c: (q, 0, 0)))

    def body(c_ref, *refs):
        for a_ref, b_ref, o_ref in zip(refs[:n_w], refs[n_w:2 * n_w], refs[2 * n_w:]):
            o_ref[...] = (a_ref[...].astype(F32) + b_ref[...].astype(F32)).astype(BF16)

    return pl.pallas_call(
        body, name=name,
        grid_spec=pltpu.PrefetchScalarGridSpec(num_scalar_prefetch=1, grid=(N_CHIPS,),
                                               in_specs=mine_specs + recv_specs, out_specs=recv_specs),
        out_shape=[jax.ShapeDtypeStruct(r.shape, BF16) for r in recvs], compiler_params=_cp(),
    )(c_arr, *mines, *recvs)


def _scatter_chip_sums(sums, col_fam):
    n_w = len(sums)

    def body(*refs):
        _scatter_plan(refs[:n_w], refs[n_w:2 * n_w], col_fam, refs[2 * n_w:2 * n_w + 2], refs[2 * n_w + 2], False)

    return pl.pallas_call(
        body, name="grad_scatter_chips", in_specs=[ANY] * n_w, out_specs=[ANY] * n_w,
        out_shape=_scatter_shapes(sums, col_fam),
        scratch_shapes=[pltpu.SemaphoreType.DMA((3 * n_w,)), pltpu.SemaphoreType.DMA((3 * n_w,)),
                        pltpu.SemaphoreType.DMA((n_w,))],
    )(*sums)


def _scatter_shapes(sums, col_fam):
    out = []
    for w, s in enumerate(sums):
        shp = (s.shape[0], s.shape[1] // N_CHIPS) if col_fam[w] else s.shape[1:]
        out.append(jax.ShapeDtypeStruct((N_CHIPS,) + shp, s.dtype))
    return out


def _scatter_plan(ins, outs, col_fam, sems, lsem, handshake):
    n_w = len(ins)
    x, y, c = _me()
    myq = 2 * x + y
    if handshake:
        _handshake([_chip_peer(x, y, j)[:2] + (c,) for j in (1, 2, 3)])

    def slab(w, q):
        if col_fam[w]:
            return _col_window(ins[w], q, ins[w].shape[1] // N_CHIPS)
        return ins[w].at[q]

    local = [pltpu.make_async_copy(slab(w, myq), outs[w].at[myq], lsem.at[w]) for w in range(n_w)]
    for cp in local:
        cp.start()
    cps = []
    for w in range(n_w):
        for j in (1, 2, 3):
            px, py, pq = _chip_peer(x, y, j)
            cp = _remote(slab(w, pq), outs[w].at[myq], sems, w * 3 + j - 1, (px, py, c))
            cp.start()
            cps.append(cp)
    for w in range(n_w):
        for j in (1, 2, 3):
            _, _, pq = _chip_peer(x, y, j)
            land = outs[w].at[pq]
            _remote(land, land, sems, w * 3 + j - 1, (x, y, c)).wait_recv()
    for cp in cps:
        cp.wait_send()
    for cp in local:
        cp.wait()


def _scatter_chip_sums_async(sums, col_fam, name, collective_id):
    srcs = [jax.new_ref(s, memory_space=HBM) for s in sums]
    dsts = [jax.empty_ref(s, memory_space=HBM) for s in _scatter_shapes(sums, col_fam)]
    _on_sequencer(name, collective_id, 3 * len(sums), len(sums),
                  lambda sems, lsem: _scatter_plan(srcs, dsts, col_fam, sems, lsem, True))
    return [r[...] for r in dsts]


def _sum_chips(parts, c_arr, prev, lead, shape, name):
    _, rows, n = parts.shape
    tr = rows // 2 if rows % 32 == 0 else rows
    nblk = rows // tr

    def body(c_ref, p_ref, *rest):
        o_ref = rest[-1]
        acc = p_ref[0].astype(F32)
        for q in range(1, N_CHIPS):
            acc = acc + p_ref[q].astype(F32)
        o_ref[...] = acc

    in_specs = [pl.BlockSpec((N_CHIPS, tr, n), lambda i, c: (0, i, 0))]
    args = [c_arr, parts]
    aliases = {}
    if prev is not None:
        in_specs.append(ANY)
        args.append(prev)
        aliases = {2: 0}
    return pl.pallas_call(
        body, name=name,
        grid_spec=pltpu.PrefetchScalarGridSpec(
            num_scalar_prefetch=1, grid=(nblk,), in_specs=in_specs,
            out_specs=pl.BlockSpec((None, tr, n), lambda i, c: (lead, c[0] * nblk + i, 0))),
        out_shape=jax.ShapeDtypeStruct(shape, F32), input_output_aliases=aliases, compiler_params=_cp(),
    )(*args)


def _join_plan(outs, place, sems, handshake):
    x, y, c = _me()
    sib = (x, y, 1 - c)
    if handshake:
        _handshake([sib])

    def half(k, h):
        o, lead = place[k]
        return _half_rows(outs[o].at[lead], h)

    cps = [_remote(half(k, c), half(k, c), sems, k, sib) for k in range(len(place))]
    for cp in cps:
        cp.start()
    for k in range(len(place)):
        land = half(k, 1 - c)
        _remote(land, land, sems, k, sib).wait_recv()
    for cp in cps:
        cp.wait_send()


def _join_halves(bufs, place, name):
    n_o = len(bufs)
    n_h = len(place)

    def body(*refs):
        _join_plan(refs[n_o:2 * n_o], place, refs[2 * n_o:2 * n_o + 2], False)

    return pl.pallas_call(
        body, name=name, in_specs=[ANY] * n_o, out_specs=[ANY] * n_o,
        out_shape=[jax.ShapeDtypeStruct(b.shape, b.dtype) for b in bufs],
        input_output_aliases={k: k for k in range(n_o)},
        scratch_shapes=[pltpu.SemaphoreType.DMA((n_h,)), pltpu.SemaphoreType.DMA((n_h,))],
    )(*bufs)


def _allreduce_rows(rows):
    n_dev = 8
    n_r = len(rows)
    assert n_r <= 8

    def body(*refs):
        r_refs = refs[:n_r]
        o_ref, slots, send_sems, recv_sems = refs[n_r:]
        x, y, c = _me()
        me = 4 * x + 2 * y + c
        slots[me] = jnp.concatenate([r[...] for r in r_refs] + [jnp.zeros((8 - n_r, D_MODEL), F32)], axis=0)

        def peer(k):
            return (1 - x if k & 4 else x, 1 - y if k & 2 else y, 1 - c if k & 1 else c)

        cps = []
        for k in range(1, n_dev):
            cp = pltpu.make_async_remote_copy(src_ref=slots.at[me], dst_ref=slots.at[me], send_sem=send_sems.at[k - 1],
                                              recv_sem=recv_sems.at[k - 1], device_id=peer(k), device_id_type=MESH)
            cp.start()
            cps.append(cp)
        for k in range(1, n_dev):
            px, py, pc = peer(k)
            land = slots.at[4 * px + 2 * py + pc]
            pltpu.make_async_remote_copy(src_ref=land, dst_ref=land, send_sem=send_sems.at[k - 1],
                                         recv_sem=recv_sems.at[k - 1], device_id=peer(k),
                                         device_id_type=MESH).wait_recv()
        for cp in cps:
            cp.wait_send()
        acc = slots[0]
        for d in range(1, n_dev):
            acc = acc + slots[d]
        o_ref[...] = acc

    vm = pl.BlockSpec(memory_space=pltpu.VMEM)
    return pl.pallas_call(
        body, name="allreduce_rows", in_specs=[vm] * n_r, out_specs=vm,
        out_shape=jax.ShapeDtypeStruct((8, D_MODEL), F32),
        scratch_shapes=[pltpu.VMEM((n_dev, 8, D_MODEL), F32), pltpu.SemaphoreType.DMA((n_dev - 1,)),
                        pltpu.SemaphoreType.DMA((n_dev - 1,))],
    )(*rows)


def _adamw(w, g, m, v, name):
    shape = w.shape
    if len(shape) == 1:
        lead, rows, cols = 1, 1, shape[0]
    else:
        rows, cols = shape[-2:]
        lead = math.prod(shape[:-2])
    args = [a.reshape(lead, rows, cols) for a in (w, g, m, v)]
    tr = rows // 2 if rows % 16 == 0 else rows

    def body(w_ref, g_ref, m_ref, v_ref, d_ref, nm_ref, nv_ref):
        gv = g_ref[...]
        nm = ADAM_B1 * m_ref[...] + (1.0 - ADAM_B1) * gv
        nv = ADAM_B2 * v_ref[...] + (1.0 - ADAM_B2) * jnp.square(gv)
        m_hat = nm / (1.0 - ADAM_B1 ** ADAM_STEP)
        v_hat = nv / (1.0 - ADAM_B2 ** ADAM_STEP)
        d_ref[...] = -ADAM_LR * (m_hat / (jnp.sqrt(v_hat) + ADAM_EPS) + ADAM_WD * w_ref[...])
        nm_ref[...] = nm
        nv_ref[...] = nv

    spec = pl.BlockSpec((None, tr, cols), lambda l, i: (l, i, 0))
    outs = pl.pallas_call(
        body, name=name, grid=(lead, rows // tr), in_specs=[spec] * 4, out_specs=[spec] * 3,
        out_shape=[jax.ShapeDtypeStruct((lead, rows, cols), F32)] * 3, compiler_params=_cp(),
    )(*args)
    return [o.reshape(shape) for o in outs]


def kernel(x, a_w_in, a_sink, a_w_out, b_w_in, b_w_out, norm_mix, norm_ffn, w_gate, w_up, w_down, final_norm, loss_target, m_a_w_in, m_a_sink, m_a_w_out, m_b_w_in, m_b_w_out, m_norm_mix, m_norm_ffn, m_w_gate, m_w_up, m_w_down, m_final_norm, v_a_w_in, v_a_sink, v_a_w_out, v_b_w_in, v_b_w_out, v_norm_mix, v_norm_ffn, v_w_gate, v_w_up, v_w_down, v_final_norm):
    weights = dict(a_w_in=a_w_in, a_sink=a_sink, a_w_out=a_w_out, b_w_in=b_w_in, b_w_out=b_w_out, norm_mix=norm_mix,
                   norm_ffn=norm_ffn, w_gate=w_gate, w_up=w_up, w_down=w_down, final_norm=final_norm)
    mom = dict(a_w_in=m_a_w_in, a_sink=m_a_sink, a_w_out=m_a_w_out, b_w_in=m_b_w_in, b_w_out=m_b_w_out,
               norm_mix=m_norm_mix, norm_ffn=m_norm_ffn, w_gate=m_w_gate, w_up=m_w_up, w_down=m_w_down,
               final_norm=m_final_norm)
    var = dict(a_w_in=v_a_w_in, a_sink=v_a_sink, a_w_out=v_a_w_out, b_w_in=v_b_w_in, b_w_out=v_b_w_out,
               norm_mix=v_norm_mix, norm_ffn=v_norm_ffn, w_gate=v_w_gate, w_up=v_w_up, w_down=v_w_down,
               final_norm=v_final_norm)
    order = ["a_w_in", "a_sink", "a_w_out", "b_w_in", "b_w_out", "norm_mix", "norm_ffn", "w_gate", "w_up", "w_down",
             "final_norm"]
    swapped = ("w_gate", "w_up")
    for n in swapped:
        weights[n], mom[n], var[n] = (a.transpose(0, 2, 1) for a in (weights[n], mom[n], var[n]))
    w_gate_t, w_up_t = weights["w_gate"], weights["w_up"]

    c_arr = lax.axis_index("c").astype(jnp.int32).reshape(1)
    q_arr = (2 * lax.axis_index("x") + lax.axis_index("y")).astype(jnp.int32).reshape(1)

    def placed(w, layer, col, nm):
        return _place_shard(w, layer, q_arr, col, f"place_{nm}")

    (a_in,) = _gather_weights_async([placed(a_w_in, 0, True, "a_in")], (True,), "gather_weights_first", 6)
    a_out, wg0, wu0, wd0 = _gather_weights_async(
        [placed(a_w_out, 0, False, "a_out"), placed(w_gate_t, 0, False, "wg0"), placed(w_up_t, 0, False, "wu0"),
         placed(w_down, 0, False, "wd0")], (False,) * 4, "gather_weights_layer0", 1)
    b_in, b_out, wg1, wu1, wd1 = _gather_weights_async(
        [placed(b_w_in, 0, True, "b_in"), placed(b_w_out, 0, False, "b_out"), placed(w_gate_t, 1, False, "wg1"),
         placed(w_up_t, 1, False, "wu1"), placed(w_down, 1, False, "wd1")], (True,) + (False,) * 4,
        "gather_weights_layer1", 7)
    a_out = a_out.reshape(D_MODEL, D_MODEL)
    b_out = b_out.reshape(D_MODEL, D_MODEL)
    wg, wu, wd = (wg0, wg1), (wu0, wu1), (wd0, wd1)

    gx, grads, vecs = _local_step(x, loss_target, a_in, a_sink[0], a_out, b_in, b_out, norm_mix, norm_ffn, wg, wu, wd,
                                  final_norm)

    rows_out = D_MODEL // N_CHIPS
    partials = [grads["a_in"], grads["b_in"],
                grads["a_out"].reshape(N_CHIPS, rows_out, D_MODEL), grads["b_out"].reshape(N_CHIPS, rows_out, D_MODEL),
                grads["wg"][0], grads["wg"][1], grads["wu"][0], grads["wu"][1], grads["wd"][0], grads["wd"][1]]
    col_fam = (True, True) + (False,) * 8
    names = ("a_in", "b_in", "a_out", "b_out", "wg0", "wg1", "wu0", "wu1", "wd0", "wd1")
    contrib = [None] * len(partials)

    def reduce_group(idx, tag, ids):
        parts = [partials[k] for k in idx]
        cols = tuple(col_fam[k] for k in idx)
        if ids is None:
            theirs = _swap_halves_with_sibling(parts, cols)
        else:
            parts, theirs = _swap_halves_async(parts, cols, f"grad_swap_{tag}", ids[0])
        sums = _half_add(parts, theirs, c_arr, cols, f"chip_sum_{tag}")
        if ids is None:
            out = _scatter_chip_sums(sums, cols)
        else:
            out = _scatter_chip_sums_async(sums, cols, f"grad_scatter_{tag}", ids[1])
        for k, o in zip(idx, out):
            contrib[k] = o

    reduce_group([1, 3, 5, 7, 9], "layer1", (2, 3))
    reduce_group([2, 4, 6, 8], "ffn0", (4, 5))
    reduce_group([0], "a_in", None)
    shapes = [a_w_in.shape, b_w_in.shape, a_w_out.shape, b_w_out.shape, w_down.shape, w_down.shape, w_down.shape]
    place = [(0, 0), (1, 0), (2, 0), (3, 0), (4, 0), (4, 1), (5, 0), (5, 1), (6, 0), (6, 1)]
    bufs = [None] * len(shapes)
    for p, nm, (o, lead) in zip(contrib, names, place):
        bufs[o] = _sum_chips(p, c_arr, bufs[o], lead, shapes[o], f"sum_chips_{nm}")
    g_a_in, g_b_in, g_a_out, g_b_out, g_wg, g_wu, g_wd = _join_halves(bufs, place, "grad_join_sibling")

    sink_row = jnp.pad(vecs["sink"][0:1], ((0, 0), (0, D_MODEL - LANES)))
    tot = _allreduce_rows([vecs["norm_mix"][0], vecs["norm_mix"][1], vecs["norm_ffn"][0], vecs["norm_ffn"][1],
                           vecs["final"], vecs["loss_cols"], sink_row])
    loss = (0.5 / D_MODEL) * jnp.sum(tot[5])
    gw = dict(a_w_in=g_a_in, a_sink=tot[6:7, :N_HEADS], a_w_out=g_a_out, b_w_in=g_b_in, b_w_out=g_b_out,
              norm_mix=tot[0:2], norm_ffn=tot[2:4], w_gate=g_wg, w_up=g_wu, w_down=g_wd, final_norm=tot[4])

    delta, new_m, new_v = {}, {}, {}
    for n in order:
        delta[n], new_m[n], new_v[n] = _adamw(weights[n], gw[n], mom[n], var[n], f"adamw_{n}")
    for n in swapped:
        gw[n], delta[n], new_m[n], new_v[n] = (a.transpose(0, 2, 1) for a in (gw[n], delta[n], new_m[n], new_v[n]))
    return (loss, gx, *[gw[n] for n in order], *[delta[n] for n in order], *[new_m[n] for n in order],
            *[new_v[n] for n in order])
```

```python
import math

import jax
import jax.numpy as jnp
from jax import lax
from jax.experimental import pallas as pl
from jax.experimental.pallas import tpu as pltpu
from jax.experimental.pallas import tpu_sc as plsc

F32 = jnp.float32
BF16 = jnp.bfloat16

D_MODEL = 1024
HEAD_DIM = 64
N_HEADS = 16
N_KV = 4
QKV_W = 1536
D_FF = 2816
N_CHIPS = 4
FF_SH = D_FF // N_CHIPS
HALF_WINDOW_A = 128
DILATED = ((128, 1), (512, 4), (2048, 16))
ROPE_THETA = 10000.0
RMS_EPS = 1e-6
NEG_INF = -1e30
LANES = 128
ADAM_LR, ADAM_B1, ADAM_B2, ADAM_EPS, ADAM_WD, ADAM_STEP = 0.001, 0.9, 0.999, 1e-08, 0.01, 10
VMEM_LIMIT = 56 * 1024 * 1024
LOG2E = math.log2(math.e)
LN2 = math.log(2.0)
Q_SCALE = LOG2E / math.sqrt(HEAD_DIM)
GRAD_TOKENS = 2048
MESH = pl.DeviceIdType.MESH


def _cp(**kw):
    return pltpu.CompilerParams(vmem_limit_bytes=VMEM_LIMIT, **kw)


def _row_tile(t, cap):
    tm = min(cap, t)
    assert t % tm == 0
    return tm


def _rope_tables(seq, dil):
    inv = 1.0 / (ROPE_THETA ** (jnp.arange(0, HEAD_DIM, 2, dtype=F32) / HEAD_DIM))
    ang = jnp.arange(seq, dtype=F32)[:, None] * inv[None, :]
    cos, sin = jnp.cos(ang), jnp.sin(ang)
    cos = jnp.tile(cos, (1, 4))
    sin = jnp.concatenate([-sin, sin, -sin, sin], axis=1)

    def perm(t):
        return t.reshape(seq // dil, dil, LANES).transpose(1, 0, 2).reshape(seq, LANES)

    return perm(cos), perm(sin)


def _swap_halves(t):
    lane = lax.broadcasted_iota(jnp.int32, t.shape, 1)
    return jnp.where((lane % HEAD_DIM) < HEAD_DIM // 2, pltpu.roll(t, LANES - 32, 1), pltpu.roll(t, 32, 1))


def _rope(t, cos, sin):
    return t * cos + _swap_halves(t) * sin


def _rope_t(t, cos, sin):
    return t * cos - _swap_halves(t) * sin


def _to_residue(t, batch, dil):
    if dil == 1:
        return t
    s = t.shape[0] // batch
    return t.reshape(batch, s // dil, dil, t.shape[1]).transpose(0, 2, 1, 3).reshape(t.shape)


def _from_residue(t, batch, dil):
    if dil == 1:
        return t
    s = t.shape[0] // batch
    return t.reshape(batch, dil, s // dil, t.shape[1]).transpose(0, 2, 1, 3).reshape(t.shape)


def _rms_fwd(x, w, name):
    t = x.shape[0]
    tm = _row_tile(t, 512)

    def body(x_ref, w_ref, o_ref):
        o_ref[...] = _rms_tile(x_ref[...], w_ref[...]).astype(BF16)

    return pl.pallas_call(
        body, name=name, grid=(t // tm,),
        in_specs=[pl.BlockSpec((tm, D_MODEL), lambda i: (i, 0)), pl.BlockSpec((1, D_MODEL), lambda i: (0, 0))],
        out_specs=pl.BlockSpec((tm, D_MODEL), lambda i: (i, 0)),
        out_shape=jax.ShapeDtypeStruct((t, D_MODEL), BF16), compiler_params=_cp(),
    )(x, w)


def _rms_bwd_tile(xv, wv, dy, dres):
    r = lax.rsqrt(jnp.mean(xv * xv, axis=-1, keepdims=True) + RMS_EPS)
    xh = xv * r
    dxh = dy * wv
    dx = dres + r * (dxh - xh * jnp.mean(dxh * xh, axis=-1, keepdims=True))
    return dx, jnp.sum(dy * xh, axis=0, keepdims=True)


def _accumulate(ref, part):
    @pl.when(pl.program_id(0) == 0)
    def _():
        ref[...] = jnp.zeros_like(ref)

    ref[...] += part


def _rms_bwd(x, w, dhs, dres, name):
    t = x.shape[0]
    tm = _row_tile(t, 512)
    n = len(dhs)

    def body(*refs):
        x_ref, w_ref = refs[0], refs[1]
        dh_refs = refs[2:2 + n]
        dres_ref = refs[2 + n]
        dx_ref, dxb_ref, dw_ref = refs[3 + n:]
        dy = dh_refs[0][...].astype(F32)
        for k in range(1, n):
            dy = dy + dh_refs[k][...].astype(F32)
        dx, dw = _rms_bwd_tile(x_ref[...], w_ref[...], dy, dres_ref[...])
        dx_ref[...] = dx
        dxb_ref[...] = dx.astype(BF16)
        _accumulate(dw_ref, dw)

    row = pl.BlockSpec((tm, D_MODEL), lambda i: (i, 0))
    vec = pl.BlockSpec((1, D_MODEL), lambda i: (0, 0))
    return pl.pallas_call(
        body, name=name, grid=(t // tm,),
        in_specs=[row, vec] + [row] * n + [row],
        out_specs=[row, row, vec],
        out_shape=[jax.ShapeDtypeStruct((t, D_MODEL), F32), jax.ShapeDtypeStruct((t, D_MODEL), BF16),
                   jax.ShapeDtypeStruct((1, D_MODEL), F32)],
        compiler_params=_cp(),
    )(x, w, *dhs, dres)


def _final_tile(xv, wv, tv):
    r = lax.rsqrt(jnp.mean(xv * xv, axis=-1, keepdims=True) + RMS_EPS)
    xh = xv * r
    err = xh * wv - tv
    dy = err * (1.0 / D_MODEL)
    dxh = dy * wv
    dx = r * (dxh - xh * jnp.mean(dxh * xh, axis=-1, keepdims=True))
    return dx, jnp.sum(err * err, axis=0, keepdims=True), jnp.sum(dy * xh, axis=0, keepdims=True)


def _qkv_proj(h, w, cos, sin, group, name):
    t = h.shape[0]
    seq = cos.shape[0]
    tm = _row_tile(seq, 1024)
    n_q = N_HEADS * HEAD_DIM // LANES
    n_rope = (N_HEADS + N_KV) * HEAD_DIM // LANES
    scale = Q_SCALE

    def body(h_ref, w_ref, cos_ref, sin_ref, o_ref):
        acc = jnp.dot(h_ref[...], w_ref[...], preferred_element_type=F32)
        cs, sn = cos_ref[...], sin_ref[...]
        csq, snq = cs * scale, sn * scale
        for c in range(QKV_W // LANES):
            blk = acc[:, c * LANES:(c + 1) * LANES]
            if c < n_q:
                blk = _rope(blk, csq, snq)
            elif c < n_rope:
                blk = _rope(blk, cs, sn)
            o_ref[:, c * LANES:(c + 1) * LANES] = blk.astype(BF16)

    tab = pl.BlockSpec((tm, LANES), lambda i: (i % (seq // tm), 0))
    return pl.pallas_call(
        body, name=name, grid=(t // tm,),
        in_specs=[pl.BlockSpec((tm, D_MODEL), lambda i: (i, 0)),
                  pl.BlockSpec((D_MODEL, QKV_W), lambda i: (0, group)), tab, tab],
        out_specs=pl.BlockSpec((tm, QKV_W), lambda i: (i, 0)),
        out_shape=jax.ShapeDtypeStruct((t, QKV_W), BF16), compiler_params=_cp(),
    )(h, w, cos, sin)


def _rms_tile(xv, wv):
    return (xv * lax.rsqrt(jnp.mean(xv * xv, axis=-1, keepdims=True) + RMS_EPS)) * wv


def _mm_res(a, w, res, nw, name):
    t, k = a.shape
    tm = _row_tile(t, 512)

    def body(a_ref, w_ref, r_ref, nw_ref, o_ref, h_ref):
        xv = r_ref[...] + jnp.dot(a_ref[...], w_ref[...], preferred_element_type=F32)
        o_ref[...] = xv
        h_ref[...] = _rms_tile(xv, nw_ref[...]).astype(BF16)

    row = pl.BlockSpec((tm, D_MODEL), lambda i: (i, 0))
    return pl.pallas_call(
        body, name=name, grid=(t // tm,),
        in_specs=[pl.BlockSpec((tm, k), lambda i: (i, 0)),
                  pl.BlockSpec((k, D_MODEL), lambda i: (0, 0), pipeline_mode=pl.Buffered(1)), row,
                  pl.BlockSpec((1, D_MODEL), lambda i: (0, 0))],
        out_specs=[row, row],
        out_shape=[jax.ShapeDtypeStruct((t, D_MODEL), F32), jax.ShapeDtypeStruct((t, D_MODEL), BF16)],
        compiler_params=_cp(),
    )(a, w, res, nw)


def _mm_nt(dy, w, group, out_dtype, name):
    t, n = dy.shape
    k = w.shape[0]
    tm = _row_tile(t, 1024)

    def body(dy_ref, w_ref, o_ref):
        o_ref[...] = lax.dot_general(dy_ref[...], w_ref[...], (((1,), (1,)), ((), ())),
                                     preferred_element_type=F32).astype(out_dtype)

    return pl.pallas_call(
        body, name=name, grid=(t // tm,),
        in_specs=[pl.BlockSpec((tm, n), lambda i: (i, 0)), pl.BlockSpec((k, n), lambda i: (0, group))],
        out_specs=pl.BlockSpec((tm, k), lambda i: (i, 0)),
        out_shape=jax.ShapeDtypeStruct((t, k), out_dtype), compiler_params=_cp(),
    )(dy, w)


def _mm_nt_rms(dy, w, x, nw, dres, name):
    t, n = dy.shape
    tm = _row_tile(t, 512)

    def body(dy_ref, w_ref, x_ref, nw_ref, dres_ref, dx_ref, dw_ref):
        dh = lax.dot_general(dy_ref[...], w_ref[...], (((1,), (1,)), ((), ())), preferred_element_type=F32)
        dx, dw = _rms_bwd_tile(x_ref[...], nw_ref[...], dh, dres_ref[...])
        dx_ref[...] = dx
        _accumulate(dw_ref, dw)

    row = pl.BlockSpec((tm, D_MODEL), lambda i: (i, 0))
    vec = pl.BlockSpec((1, D_MODEL), lambda i: (0, 0))
    return pl.pallas_call(
        body, name=name, grid=(t // tm,),
        in_specs=[pl.BlockSpec((tm, n), lambda i: (i, 0)),
                  pl.BlockSpec((D_MODEL, n), lambda i: (0, 0), pipeline_mode=pl.Buffered(1)), row, vec, row],
        out_specs=[row, vec],
        out_shape=[jax.ShapeDtypeStruct((t, D_MODEL), F32), jax.ShapeDtypeStruct((1, D_MODEL), F32)],
        compiler_params=_cp(),
    )(dy, w, x, nw, dres)


def _out_bwd(dx, w, o, name):
    t = dx.shape[0]
    tm = _row_tile(t, 512)

    def body(dx_ref, w_ref, o_ref, et_ref, do_ref, adj_ref):
        do = lax.dot_general(dx_ref[...], w_ref[...], (((1,), (1,)), ((), ())), preferred_element_type=F32)
        do_ref[...] = do.astype(BF16)
        adj_ref[...] = -_dot_split(do * o_ref[...].astype(F32), et_ref[...])

    row = pl.BlockSpec((tm, D_MODEL), lambda i: (i, 0))
    return pl.pallas_call(
        body, name=name, grid=(t // tm,),
        in_specs=[row, pl.BlockSpec((D_MODEL, D_MODEL), lambda i: (0, 0)), row,
                  pl.BlockSpec((D_MODEL, LANES), lambda i: (0, 0))],
        out_specs=[row, pl.BlockSpec((tm, LANES), lambda i: (i, 0))],
        out_shape=[jax.ShapeDtypeStruct((t, D_MODEL), BF16), jax.ShapeDtypeStruct((t, LANES), F32)],
        compiler_params=_cp(),
    )(dx, w, o, _head_expander().T)


def _mm_tn(a, bs, name):
    aq = a.ndim == 3
    bq = bs[0].ndim == 3
    t, ka = a.shape[-2:]
    n = bs[0].shape[-1]
    nq = N_CHIPS if (aq or bq) else 1
    tt = _row_tile(t, GRAD_TOKENS)
    tn = n if n <= 1024 else 768
    assert n % tn == 0
    nb = len(bs)
    steps = t // tt

    def body(*refs):
        a_ref = refs[0]
        b_refs = refs[1:1 + nb]
        o_refs = refs[1 + nb:1 + 2 * nb]
        acc_refs = refs[1 + 2 * nb:]
        s = pl.program_id(2)
        av = a_ref[...]
        for b_ref, o_ref, acc_ref in zip(b_refs, o_refs, acc_refs):
            @pl.when(s == 0)
            def _():
                acc_ref[...] = jnp.zeros_like(acc_ref)

            acc_ref[...] += lax.dot_general(av, b_ref[...], (((0,), (0,)), ((), ())), preferred_element_type=F32)

            @pl.when(s == steps - 1)
            def _():
                o_ref[...] = acc_ref[...].astype(BF16)

    a_spec = (pl.BlockSpec((None, tt, ka), lambda q, j, s: (q, s, 0)) if aq
              else pl.BlockSpec((tt, ka), lambda q, j, s: (s, 0)))
    b_spec = (pl.BlockSpec((None, tt, tn), lambda q, j, s: (q, s, j)) if bq
              else pl.BlockSpec((tt, tn), lambda q, j, s: (s, j)))
    if nq > 1:
        o_spec = pl.BlockSpec((None, ka, tn), lambda q, j, s: (q, 0, j))
        o_shape = jax.ShapeDtypeStruct((nq, ka, n), BF16)
    else:
        o_spec = pl.BlockSpec((ka, tn), lambda q, j, s: (0, j))
        o_shape = jax.ShapeDtypeStruct((ka, n), BF16)
    outs = pl.pallas_call(
        body, name=name, grid=(nq, n // tn, steps),
        in_specs=[a_spec] + [b_spec] * nb, out_specs=[o_spec] * nb, out_shape=[o_shape] * nb,
        scratch_shapes=[pltpu.VMEM((ka, tn), F32)] * nb, compiler_params=_cp(),
    )(a, *bs)
    return outs


def _sigmoid(x):
    return 1.0 / (1.0 + jnp.exp(-x))


def _ffn_up(h, wg, wu, layer, name):
    t = h.shape[0]
    tm = _row_tile(t, 1024)
    nt = (((1,), (1,)), ((), ()))

    def body(h_ref, wg_ref, wu_ref, a_ref, dg_ref, du_ref):
        hv = h_ref[...]
        g = lax.dot_general(hv, wg_ref[...], nt, preferred_element_type=F32)
        u = lax.dot_general(hv, wu_ref[...], nt, preferred_element_type=F32)
        sg = _sigmoid(g)
        silu = g * sg
        a_ref[...] = (silu * u).astype(BF16)
        dg_ref[...] = (sg * (1.0 + g * (1.0 - sg)) * u).astype(BF16)
        du_ref[...] = silu.astype(BF16)

    wspec = pl.BlockSpec((None, None, FF_SH, D_MODEL), lambda q, i: (q, layer, 0, 0))
    ospec = pl.BlockSpec((None, tm, FF_SH), lambda q, i: (q, i, 0))
    oshape = jax.ShapeDtypeStruct((N_CHIPS, t, FF_SH), BF16)
    return pl.pallas_call(
        body, name=name, grid=(N_CHIPS, t // tm),
        in_specs=[pl.BlockSpec((tm, D_MODEL), lambda q, i: (i, 0)), wspec, wspec],
        out_specs=[ospec] * 3, out_shape=[oshape] * 3, compiler_params=_cp(),
    )(h, wg, wu)


def _ffn_down(a, wd, res, layer, name, norm_w=None, head=None):
    t = a.shape[1]
    tm = _row_tile(t, 512)
    resident = pl.BlockSpec((N_CHIPS, None, FF_SH, D_MODEL), lambda i: (0, layer, 0, 0), pipeline_mode=pl.Buffered(1))
    row = pl.BlockSpec((tm, D_MODEL), lambda i: (i, 0))
    vec = pl.BlockSpec((1, D_MODEL), lambda i: (0, 0))

    def hidden(a_ref, w_ref, r_ref):
        acc = r_ref[...]
        for q in range(N_CHIPS):
            acc = acc + jnp.dot(a_ref[q], w_ref[q], preferred_element_type=F32)
        return acc

    if head is None:
        def body(a_ref, w_ref, r_ref, nw_ref, o_ref, h_ref):
            xv = hidden(a_ref, w_ref, r_ref)
            o_ref[...] = xv
            h_ref[...] = _rms_tile(xv, nw_ref[...]).astype(BF16)

        return pl.pallas_call(
            body, name=name, grid=(t // tm,),
            in_specs=[pl.BlockSpec((N_CHIPS, tm, FF_SH), lambda i: (0, i, 0)), resident, row, vec],
            out_specs=[row, row],
            out_shape=[jax.ShapeDtypeStruct((t, D_MODEL), F32), jax.ShapeDtypeStruct((t, D_MODEL), BF16)],
            compiler_params=_cp(),
        )(a, wd, res, norm_w)

    def body(a_ref, w_ref, r_ref, nw_ref, t_ref, dx_ref, dxb_ref, l_ref, dw_ref):
        dx, sq, dw = _final_tile(hidden(a_ref, w_ref, r_ref), nw_ref[...], t_ref[...])
        dx_ref[...] = dx
        dxb_ref[...] = dx.astype(BF16)
        _accumulate(l_ref, sq)
        _accumulate(dw_ref, dw)

    return pl.pallas_call(
        body, name=name, grid=(t // tm,),
        in_specs=[pl.BlockSpec((N_CHIPS, tm, FF_SH), lambda i: (0, i, 0)), resident, row, vec, row],
        out_specs=[row, row, vec, vec],
        out_shape=[jax.ShapeDtypeStruct((t, D_MODEL), F32), jax.ShapeDtypeStruct((t, D_MODEL), BF16),
                   jax.ShapeDtypeStruct((1, D_MODEL), F32), jax.ShapeDtypeStruct((1, D_MODEL), F32)],
        compiler_params=_cp(),
    )(a, wd, res, *head)


def _ffn_bwd(dy, wd, wg, wu, fg, fu, x, nw, dres, name):
    t = dy.shape[0]
    tm = _row_tile(t, 256)
    nt = (((1,), (1,)), ((), ()))

    def body(dy_ref, wd_ref, wg_ref, wu_ref, fg_ref, fu_ref, x_ref, nw_ref, dres_ref,
             dg_ref, du_ref, dx_ref, dxb_ref, dw_ref):
        dyv = dy_ref[...]
        acc = jnp.zeros((tm, D_MODEL), F32)
        for q in range(N_CHIPS):
            da = lax.dot_general(dyv, wd_ref[q], nt, preferred_element_type=F32)
            dg = (da * fg_ref[q].astype(F32)).astype(BF16)
            du = (da * fu_ref[q].astype(F32)).astype(BF16)
            dg_ref[q] = dg
            du_ref[q] = du
            acc = acc + jnp.dot(dg, wg_ref[q], preferred_element_type=F32)
            acc = acc + jnp.dot(du, wu_ref[q], preferred_element_type=F32)
        dx, dw = _rms_bwd_tile(x_ref[...], nw_ref[...], acc, dres_ref[...])
        dx_ref[...] = dx
        dxb_ref[...] = dx.astype(BF16)
        _accumulate(dw_ref, dw)

    aspec = pl.BlockSpec((N_CHIPS, tm, FF_SH), lambda i: (0, i, 0))
    wspec = pl.BlockSpec((N_CHIPS, None, FF_SH, D_MODEL), lambda i: (0, 0, 0, 0), pipeline_mode=pl.Buffered(1))
    row = pl.BlockSpec((tm, D_MODEL), lambda i: (i, 0))
    vec = pl.BlockSpec((1, D_MODEL), lambda i: (0, 0))
    ashape = jax.ShapeDtypeStruct((N_CHIPS, t, FF_SH), BF16)
    return pl.pallas_call(
        body, name=name, grid=(t // tm,),
        in_specs=[row, wspec, wspec, wspec, aspec, aspec, row, vec, row],
        out_specs=[aspec, aspec, row, row, vec],
        out_shape=[ashape, ashape, jax.ShapeDtypeStruct((t, D_MODEL), F32), jax.ShapeDtypeStruct((t, D_MODEL), BF16),
                   jax.ShapeDtypeStruct((1, D_MODEL), F32)],
        compiler_params=_cp(),
    )(dy, wd, wg, wu, fg, fu, x, nw, dres)


def _attn_geometry(length, half_window):
    qb = min(LANES, length)
    kw = min(qb + 2 * half_window, length)
    return qb, kw, length // qb


def _dup_kv(src_ref, dst_ref, s, length):
    ch = min(length, 256)
    lo = lax.broadcasted_iota(jnp.int32, (ch, LANES), 1) < HEAD_DIM

    def chunk(c, carry):
        r0 = pl.multiple_of(c * ch, ch)
        for j in range(N_KV // 2):
            tile = src_ref[s, pl.ds(r0, ch), j * LANES:(j + 1) * LANES].astype(F32)
            rolled = pltpu.roll(tile, HEAD_DIM, 1)
            dst_ref[2 * j, pl.ds(r0, ch), :] = jnp.where(lo, tile, rolled).astype(BF16)
            dst_ref[2 * j + 1, pl.ds(r0, ch), :] = jnp.where(lo, rolled, tile).astype(BF16)
        return carry

    lax.fori_loop(0, length // ch, chunk, 0)


def _stack_heads(ref, s, q0, qb, g):
    lo = lax.broadcasted_iota(jnp.int32, (qb, LANES), 1) < HEAD_DIM
    parts = []
    for a in range(4):
        col = (2 * g + a // 2) * LANES
        tile = ref[s, pl.ds(q0, qb), col:col + LANES]
        keep = lo if a % 2 == 0 else jnp.logical_not(lo)
        parts.append(jnp.where(keep, tile, jnp.zeros_like(tile)))
    return jnp.concatenate(parts, axis=0)


def _unstack_pair_t(stacked_t, qb, pair):
    both = jnp.concatenate([stacked_t[:, (2 * pair) * qb:(2 * pair + 1) * qb],
                            stacked_t[:, (2 * pair + 1) * qb:(2 * pair + 2) * qb]], axis=0)
    return both.T


def _band_mask_t(q0, k0, qb, kw, half_window):
    key = lax.broadcasted_iota(jnp.int32, (kw, 4 * qb), 0)
    qry = lax.broadcasted_iota(jnp.int32, (kw, 4 * qb), 1) & (qb - 1)
    return jnp.abs((q0 + qry) - (k0 + key)) <= half_window


def _block_origin(i, qb, kw, half_window, length):
    if isinstance(i, int):
        return i * qb, min(max(i * qb - half_window, 0), length - kw)
    return (pl.multiple_of(i * qb, qb),
            pl.multiple_of(jnp.clip(i * qb - half_window, 0, length - kw), HEAD_DIM))


def _head_row(vals, qb):
    return jnp.concatenate([jnp.broadcast_to(v, (1, qb)).astype(F32) for v in vals], axis=1)


def _attn_fwd(qkv, sink, n_seq, length, half_window, seq_blk, out_dtype, name):
    qb, kw, nblk = _attn_geometry(length, half_window)
    with_sink = sink is not None
    nt = (((1,), (1,)), ((), ()))
    tn = (((0,), (0,)), ((), ()))
    qkv3 = qkv.reshape(n_seq, length, QKV_W)

    def body(*refs):
        refs = list(refs)
        sink_ref = refs.pop(0) if with_sink else None
        q_ref, k_ref, v_ref, o_ref, lse_ref = refs[:5]
        kx_ref, vx_ref = refs[-2:]
        head_row = lax.broadcasted_iota(jnp.int32, (N_HEADS, qb), 0)
        for s in range(seq_blk):
            _dup_kv(k_ref, kx_ref, s, length)
            _dup_kv(v_ref, vx_ref, s, length)

            def block(i, carry):
                q0, k0 = _block_origin(i, qb, kw, half_window, length)
                valid = _band_mask_t(q0, k0, qb, kw, half_window)
                lse_tile = jnp.zeros((N_HEADS, qb), F32)
                groups = range(N_KV)
                sts = [lax.dot_general(kx_ref[g, pl.ds(k0, kw), :], _stack_heads(q_ref, s, q0, qb, g), nt,
                                       preferred_element_type=F32) for g in groups]
                sts = [jnp.where(valid, st, NEG_INF) for st in sts]
                ms = [jnp.max(st, axis=0, keepdims=True) for st in sts]
                if with_sink:
                    sks = [_head_row([sink_ref[4 * g + a] * LOG2E for a in range(4)], qb) for g in groups]
                    ms = [jnp.maximum(m, sk) for m, sk in zip(ms, sks)]
                es = [jnp.exp2(st - m) for st, m in zip(sts, ms)]
                dens = [jnp.sum(e, axis=0, keepdims=True) for e in es]
                if with_sink:
                    dens = [den + jnp.exp2(sk - m) for den, sk, m in zip(dens, sks, ms)]
                ots = [lax.dot_general(vx_ref[g, pl.ds(k0, kw), 0:HEAD_DIM], es[g].astype(BF16), tn,
                                       preferred_element_type=F32) / dens[g] for g in groups]
                for g in groups:
                    for pair in range(2):
                        col = (2 * g + pair) * LANES
                        o_ref[s, pl.ds(q0, qb), col:col + LANES] = _unstack_pair_t(ots[g], qb, pair).astype(out_dtype)
                    lse = ms[g] * LN2 + jnp.log(dens[g])
                    for a in range(4):
                        lse_tile = jnp.where(head_row == 4 * g + a, lse[:, a * qb:(a + 1) * qb], lse_tile)
                lse_ref[s, :, pl.ds(q0, qb)] = lse_tile
                return carry

            if nblk == 1:
                block(0, 0)
            else:
                lax.fori_loop(0, nblk, block, 0)

    in_specs = [pl.BlockSpec((seq_blk, length, N_HEADS * HEAD_DIM), lambda n: (n, 0, 0)),
                pl.BlockSpec((seq_blk, length, N_KV * HEAD_DIM), lambda n: (n, 0, 4)),
                pl.BlockSpec((seq_blk, length, N_KV * HEAD_DIM), lambda n: (n, 0, 5))]
    args = [qkv3, qkv3, qkv3]
    if with_sink:
        in_specs.insert(0, pl.BlockSpec(memory_space=pltpu.SMEM))
        args.insert(0, sink)
    out_specs = [pl.BlockSpec((seq_blk, length, D_MODEL), lambda n: (n, 0, 0)),
                 pl.BlockSpec((seq_blk, N_HEADS, length), lambda n: (n, 0, 0))]
    out_shape = [jax.ShapeDtypeStruct((n_seq, length, D_MODEL), out_dtype),
                 jax.ShapeDtypeStruct((n_seq, N_HEADS, length), F32)]
    o, lse = pl.pallas_call(
        body, name=name, grid=(n_seq // seq_blk,), in_specs=in_specs, out_specs=out_specs, out_shape=out_shape,
        scratch_shapes=[pltpu.VMEM((N_KV, length, LANES), BF16), pltpu.VMEM((N_KV, length, LANES), BF16)],
        compiler_params=_cp(),
    )(*args)
    return o.reshape(n_seq * length, D_MODEL), lse


def _attn_bwd(qkv, do, adj, lse, sink, cos, sin, n_seq, length, half_window, seq_blk, dil, name):
    qb, kw, nblk = _attn_geometry(length, half_window)
    scale = 1.0 / math.sqrt(HEAD_DIM)
    with_sink = sink is not None
    nt = (((1,), (1,)), ((), ()))
    tn = (((0,), (0,)), ((), ()))
    qkv3 = qkv.reshape(n_seq, length, QKV_W)
    do3 = do.reshape(n_seq, length, D_MODEL)
    tabs = [t.reshape(dil, length, LANES) for t in (cos, sin)]
    tab_blocks = dil // seq_blk if dil >= seq_blk else 1

    def body(*refs):
        refs = list(refs)
        sink_ref = refs.pop(0) if with_sink else None
        q_ref, k_ref, v_ref, do_ref, aux_ref, lse_ref, cos_ref, sin_ref, dqkv_ref = refs[:9]
        ds_ref = refs[9] if with_sink else None
        kx_ref, vx_ref, dkx_ref, dvx_ref = refs[-4:]
        lane = lax.broadcasted_iota(jnp.int32, (1, LANES), 1)
        if with_sink:
            @pl.when(pl.program_id(0) == 0)
            def _():
                ds_ref[...] = jnp.zeros_like(ds_ref)

        for s in range(seq_blk):
            ts = s % dil
            _dup_kv(k_ref, kx_ref, s, length)
            _dup_kv(v_ref, vx_ref, s, length)
            dkx_ref[...] = jnp.zeros_like(dkx_ref)
            dvx_ref[...] = jnp.zeros_like(dvx_ref)

            def block(i, dsink):
                q0, k0 = _block_origin(i, qb, kw, half_window, length)
                valid = _band_mask_t(q0, k0, qb, kw, half_window)
                cs = cos_ref[ts, pl.ds(q0, qb), :] * scale
                sn = sin_ref[ts, pl.ds(q0, qb), :] * scale
                adj_tile = aux_ref[s, :, pl.ds(q0, qb)]
                lse_tile = lse_ref[s, :, pl.ds(q0, qb)]
                groups = range(N_KV)
                qss = [_stack_heads(q_ref, s, q0, qb, g) for g in groups]
                doss = [_stack_heads(do_ref, s, q0, qb, g) for g in groups]
                kxs = [kx_ref[g, pl.ds(k0, kw), :] for g in groups]
                sts = [lax.dot_general(kxs[g], qss[g], nt, preferred_element_type=F32) for g in groups]
                dpts = [lax.dot_general(vx_ref[g, pl.ds(k0, kw), :], doss[g], nt, preferred_element_type=F32)
                        for g in groups]
                lses = [_head_row([lse_tile[4 * g + a:4 * g + a + 1, :] * LOG2E for a in range(4)], qb) for g in groups]
                shifts = [_head_row([adj_tile[4 * g + a:4 * g + a + 1, :] for a in range(4)], qb) for g in groups]
                pts = [jnp.exp2(jnp.where(valid, sts[g], NEG_INF) - lses[g]) for g in groups]
                dsbs = [(pts[g] * (dpts[g] + shifts[g])).astype(BF16) for g in groups]
                pbs = [pt.astype(BF16) for pt in pts]
                if with_sink:
                    for g in groups:
                        sk = _head_row([sink_ref[4 * g + a] * LOG2E for a in range(4)], qb)
                        dsk = jnp.exp2(sk - lses[g]) * shifts[g]
                        for a in range(4):
                            tot = jnp.sum(dsk[:, a * qb:(a + 1) * qb], axis=1, keepdims=True)
                            dsink = dsink + jnp.where(lane == 4 * g + a, tot, 0.0)
                dqts = [lax.dot_general(kx_ref[g, pl.ds(k0, kw), 0:HEAD_DIM], dsbs[g], tn, preferred_element_type=F32)
                        for g in groups]
                for g in groups:
                    for pair in range(2):
                        col = (2 * g + pair) * LANES
                        tile = _rope_t(_unstack_pair_t(dqts[g], qb, pair), cs, sn)
                        dqkv_ref[s, pl.ds(q0, qb), col:col + LANES] = tile.astype(BF16)
                for g in groups:
                    dkx_ref[g, pl.ds(k0, kw), :] += jnp.dot(dsbs[g], qss[g], preferred_element_type=F32)
                    dvx_ref[g, pl.ds(k0, kw), :] += jnp.dot(pbs[g], doss[g], preferred_element_type=F32)
                return dsink

            if nblk == 1:
                dsink = block(0, jnp.zeros((1, LANES), F32))
            else:
                dsink = lax.fori_loop(0, nblk, block, jnp.zeros((1, LANES), F32))
            if with_sink:
                ds_ref[0:1, :] += dsink

            ch = min(length, 256)
            lo_c = lax.broadcasted_iota(jnp.int32, (ch, LANES), 1) < HEAD_DIM

            def fin(c, carry):
                r0 = pl.multiple_of(c * ch, ch)
                cs = cos_ref[ts, pl.ds(r0, ch), :]
                sn = sin_ref[ts, pl.ds(r0, ch), :]
                for j in range(N_KV // 2):
                    both = []
                    for acc_ref in (dkx_ref, dvx_ref):
                        t0 = acc_ref[2 * j, pl.ds(r0, ch), :]
                        t1 = acc_ref[2 * j + 1, pl.ds(r0, ch), :]
                        both.append(jnp.where(lo_c, t0, t1) + pltpu.roll(jnp.where(lo_c, t1, t0), HEAD_DIM, 1))
                    kcol = N_HEADS * HEAD_DIM + j * LANES
                    vcol = (N_HEADS + N_KV) * HEAD_DIM + j * LANES
                    dqkv_ref[s, pl.ds(r0, ch), kcol:kcol + LANES] = _rope_t(both[0] * LN2, cs, sn).astype(BF16)
                    dqkv_ref[s, pl.ds(r0, ch), vcol:vcol + LANES] = both[1].astype(BF16)
                return carry

            lax.fori_loop(0, length // ch, fin, 0)

    seq_map = lambda n: (n, 0, 0)
    tab_map = (lambda n: (n % tab_blocks, 0, 0)) if dil >= seq_blk else (lambda n: (0, 0, 0))
    tab_rows = min(seq_blk, dil)
    in_specs = [pl.BlockSpec((seq_blk, length, N_HEADS * HEAD_DIM), seq_map),
                pl.BlockSpec((seq_blk, length, N_KV * HEAD_DIM), lambda n: (n, 0, 4)),
                pl.BlockSpec((seq_blk, length, N_KV * HEAD_DIM), lambda n: (n, 0, 5)),
                pl.BlockSpec((seq_blk, length, D_MODEL), seq_map),
                pl.BlockSpec((seq_blk, N_HEADS, length), seq_map),
                pl.BlockSpec((seq_blk, N_HEADS, length), seq_map),
                pl.BlockSpec((tab_rows, length, LANES), tab_map),
                pl.BlockSpec((tab_rows, length, LANES), tab_map)]
    args = [qkv3, qkv3, qkv3, do3, adj, lse] + tabs
    if with_sink:
        in_specs.insert(0, pl.BlockSpec(memory_space=pltpu.SMEM))
        args.insert(0, sink)
    out_specs = [pl.BlockSpec((seq_blk, length, QKV_W), seq_map)]
    out_shape = [jax.ShapeDtypeStruct((n_seq, length, QKV_W), BF16)]
    if with_sink:
        out_specs.append(pl.BlockSpec((8, LANES), lambda n: (0, 0)))
        out_shape.append(jax.ShapeDtypeStruct((8, LANES), F32))
    outs = pl.pallas_call(
        body, name=name, grid=(n_seq // seq_blk,), in_specs=in_specs, out_specs=out_specs, out_shape=out_shape,
        scratch_shapes=[pltpu.VMEM((N_KV, length, LANES), BF16), pltpu.VMEM((N_KV, length, LANES), BF16),
                        pltpu.VMEM((N_KV, length, LANES), F32), pltpu.VMEM((N_KV, length, LANES), F32)],
        compiler_params=_cp(),
    )(*args)
    dqkv = outs[0].reshape(n_seq * length, QKV_W)
    return (dqkv, outs[1]) if with_sink else (dqkv, None)


def _head_expander():
    h = jnp.arange(LANES)[:, None]
    l = jnp.arange(D_MODEL)[None, :]
    return (l // HEAD_DIM == h).astype(BF16)


def _dot_split(a, e):
    hi = a.astype(BF16)
    lo = (a - hi.astype(F32)).astype(BF16)
    return jnp.dot(hi, e, preferred_element_type=F32) + jnp.dot(lo, e, preferred_element_type=F32)


def _mix_weights(lses):
    m = jnp.maximum(jnp.maximum(lses[0], lses[1]), lses[2])
    es = [jnp.exp(v - m) for v in lses]
    tot = es[0] + es[1] + es[2]
    return [e / tot for e in es]


def _mix_fwd(os_, lses, name):
    t = os_[0].shape[0]
    tm = _row_tile(t, 512)

    def body(o0, o1, o2, l0, l1, l2, e_ref, out_ref):
        wts = _mix_weights([l0[...], l1[...], l2[...]])
        acc = jnp.zeros((tm, D_MODEL), F32)
        for w, o_ref in zip(wts, (o0, o1, o2)):
            acc = acc + _dot_split(w, e_ref[...]) * o_ref[...]
        out_ref[...] = acc.astype(BF16)

    row = pl.BlockSpec((tm, D_MODEL), lambda i: (i, 0))
    lrow = pl.BlockSpec((tm, LANES), lambda i: (i, 0))
    return pl.pallas_call(
        body, name=name, grid=(t // tm,),
        in_specs=[row] * 3 + [lrow] * 3 + [pl.BlockSpec((LANES, D_MODEL), lambda i: (0, 0))],
        out_specs=row, out_shape=jax.ShapeDtypeStruct((t, D_MODEL), BF16), compiler_params=_cp(),
    )(*os_, *lses, _head_expander())


def _mix_bwd(dx, w_out, os_, lses, name):
    t = dx.shape[0]
    tm = _row_tile(t, 512)

    def body(d_ref, w_ref, o0, o1, o2, l0, l1, l2, e_ref, et_ref, do0, do1, do2, a0, a1, a2):
        wts = _mix_weights([l0[...], l1[...], l2[...]])
        dv = lax.dot_general(d_ref[...], w_ref[...], (((1,), (1,)), ((), ())), preferred_element_type=F32)
        cs = [_dot_split(dv * o_ref[...], et_ref[...]) for o_ref in (o0, o1, o2)]
        mean_c = wts[0] * cs[0] + wts[1] * cs[1] + wts[2] * cs[2]
        for w, c, do_ref, a_ref in zip(wts, cs, (do0, do1, do2), (a0, a1, a2)):
            do_ref[...] = (_dot_split(w, e_ref[...]) * dv).astype(BF16)
            a_ref[...] = w * (c - mean_c) - w * c

    row = pl.BlockSpec((tm, D_MODEL), lambda i: (i, 0))
    lrow = pl.BlockSpec((tm, LANES), lambda i: (i, 0))
    e = _head_expander()
    return pl.pallas_call(
        body, name=name, grid=(t // tm,),
        in_specs=[row, pl.BlockSpec((D_MODEL, D_MODEL), lambda i: (0, 0), pipeline_mode=pl.Buffered(1))]
        + [row] * 3 + [lrow] * 3 + [pl.BlockSpec((LANES, D_MODEL), lambda i: (0, 0)),
                                    pl.BlockSpec((D_MODEL, LANES), lambda i: (0, 0))],
        out_specs=[row] * 3 + [lrow] * 3,
        out_shape=[jax.ShapeDtypeStruct((t, D_MODEL), BF16)] * 3 + [jax.ShapeDtypeStruct((t, LANES), F32)] * 3,
        compiler_params=_cp(),
    )(dx, w_out, *os_, *lses, e, e.T)


def _stats_to_tokens(stat, batch, dil):
    n_seq, _, length = stat.shape
    t = stat.transpose(0, 2, 1).reshape(n_seq * length, N_HEADS)
    return _from_residue(jnp.pad(t, ((0, 0), (0, LANES - N_HEADS))), batch, dil)


def _stats_from_tokens(stat, batch, dil, n_seq, length):
    t = _to_residue(stat[:, :N_HEADS], batch, dil)
    return t.reshape(n_seq, length, N_HEADS).transpose(0, 2, 1)


def _group_geometry(batch, seq, dil, window):
    length = seq // dil
    n_seq = batch * dil
    seq_blk = max(1, min(dil, 1024 // length))
    return n_seq, length, (window // 2) // dil, seq_blk


def _local_step(x, target, a_in, a_sink, a_out, b_in, b_out, norm_mix, norm_ffn, wg, wu, wd, final_norm):
    batch, seq, _ = x.shape
    t = batch * seq
    x0 = x.reshape(t, D_MODEL)
    tgt = target.reshape(t, D_MODEL)
    tabs = {d: _rope_tables(seq, d) for _, d in DILATED}
    nm = [norm_mix[i:i + 1] for i in range(2)]
    nf = [norm_ffn[i:i + 1] for i in range(2)]

    h0 = _rms_fwd(x0, nm[0], "rms_mix0")
    qkv0 = _qkv_proj(h0, a_in, *tabs[1], 0, "qkv0")
    o0, lse0 = _attn_fwd(qkv0, a_sink, batch, seq, HALF_WINDOW_A, 1, BF16, "attn0")
    x1, hf0 = _mm_res(o0, a_out, x0, nf[0], "out0")
    act0, g0, u0 = _ffn_up(hf0, wg[0], wu[0], 0, "ffn_up0")
    x2, h1 = _ffn_down(act0, wd[0], x1, 0, "ffn_down0", norm_w=nm[1])

    geo = [_group_geometry(batch, seq, d, w) for w, d in DILATED]
    h1g, qkv1, o1, lse1, lse1r = [], [], [], [], []
    for gi, (_, d) in enumerate(DILATED):
        n_seq, length, hw, sb = geo[gi]
        hp = _to_residue(h1, batch, d)
        pj = _qkv_proj(hp, b_in, *tabs[d], gi, f"qkv1_{gi}")
        o, lse = _attn_fwd(pj, None, n_seq, length, hw, sb, BF16, f"attn1_{gi}")
        h1g.append(hp)
        qkv1.append(pj)
        o1.append(_from_residue(o, batch, d))
        lse1r.append(lse)
        lse1.append(_stats_to_tokens(lse, batch, d))
    omix = _mix_fwd(o1, lse1, "mix")
    x3, hf1 = _mm_res(omix, b_out, x2, nf[1], "out1")
    act1, g1, u1 = _ffn_up(hf1, wg[1], wu[1], 0, "ffn_up1")
    dx4, dx4b, loss_cols, d_final = _ffn_down(act1, wd[1], x3, 0, "ffn_down1_loss",
                                                     head=(final_norm.reshape(1, D_MODEL), tgt))

    def ffn_bwd(dxo, dxob, x_mid, hf, g, u, act, layer):
        dg, du, dxm, dxmb, d_nf = _ffn_bwd(dxob, wd[layer], wg[layer], wu[layer], g, u, x_mid, nf[layer], dxo,
                                           f"ffn_bwd{layer}")
        (d_wd,) = _mm_tn(act, [dxob], f"grad_wd{layer}")
        (d_wgt,) = _mm_tn(dg, [hf], f"grad_wg{layer}")
        (d_wut,) = _mm_tn(du, [hf], f"grad_wu{layer}")
        return dxm, dxmb, d_nf, d_wgt, d_wut, d_wd

    dx3, dx3b, d_nf1, d_wg1, d_wu1, d_wd1 = ffn_bwd(dx4, dx4b, x3, hf1, g1, u1, act1, 1)

    (d_b_out,) = _mm_tn(omix, [dx3b], "grad_b_out")
    mb = _mix_bwd(dx3b, b_out, o1, lse1, "out1_mix_bwd")
    dh1, d_b_in = [], []
    for gi, (_, d) in enumerate(DILATED):
        n_seq, length, hw, sb = geo[gi]
        dog = _to_residue(mb[gi], batch, d)
        adj = _stats_from_tokens(mb[3 + gi], batch, d, n_seq, length)
        dpj, _ = _attn_bwd(qkv1[gi], dog, adj, lse1r[gi], None, *tabs[d], n_seq, length, hw, sb, d, f"attn1_bwd{gi}")
        (dw,) = _mm_tn(h1g[gi], [dpj], f"grad_b_in{gi}")
        d_b_in.append(dw)
        dh1.append(_from_residue(_mm_nt(dpj, b_in, gi, BF16, f"qkv1_bwd{gi}"), batch, d))
    dx2, dx2b, d_nm1 = _rms_bwd(x2, nm[1], dh1, dx3, "rms_mix_bwd1")

    dx1, dx1b, d_nf0, d_wg0, d_wu0, d_wd0 = ffn_bwd(dx2, dx2b, x1, hf0, g0, u0, act0, 0)

    do0, adj0 = _out_bwd(dx1b, a_out, o0, "out0_bwd")
    (d_a_out,) = _mm_tn(o0, [dx1b], "grad_a_out")
    adj0 = _stats_from_tokens(adj0, batch, 1, batch, seq)
    dqkv0, d_sink = _attn_bwd(qkv0, do0, adj0, lse0, a_sink, *tabs[1], batch, seq, HALF_WINDOW_A, 1, 1, "attn0_bwd")
    (d_a_in,) = _mm_tn(h0, [dqkv0], "grad_a_in")
    gx, d_nm0 = _mm_nt_rms(dqkv0, a_in, x0, nm[0], dx1, "qkv0_bwd")

    grads = dict(a_in=d_a_in, a_out=d_a_out, b_in=jnp.concatenate(d_b_in, axis=1), b_out=d_b_out,
                 wg=(d_wg0, d_wg1), wu=(d_wu0, d_wu1), wd=(d_wd0, d_wd1))
    vecs = dict(norm_mix=(d_nm0, d_nm1), norm_ffn=(d_nf0, d_nf1), final=d_final, loss_cols=loss_cols, sink=d_sink)
    return gx.reshape(x.shape), grads, vecs


ANY = pl.BlockSpec(memory_space=pl.ANY)
HBM = pltpu.MemorySpace.HBM


def _me():
    return lax.axis_index("x"), lax.axis_index("y"), lax.axis_index("c")


def _chip_peer(x, y, j):
    px = 1 - x if j & 2 else x
    py = 1 - y if j & 1 else y
    return px, py, 2 * px + py


def _remote(src, dst, sems, k, dev):
    return pltpu.make_async_remote_copy(src_ref=src, dst_ref=dst, send_sem=sems[0].at[k], recv_sem=sems[1].at[k],
                                        device_id=dev, device_id_type=MESH)


def _col_window(ref, q, width):
    return ref.at[:, pl.ds(pl.multiple_of(q * width, LANES), width)]


def _half0(ref, h):
    n = ref.shape[0] // 2
    return ref.at[pl.ds(h * n, n)]


def _half1(ref, h):
    n = ref.shape[1] // 2
    return ref.at[:, pl.ds(h * n, n)]


def _half_rows(ref, h):
    n = ref.shape[-2] // 2
    if len(ref.shape) == 2:
        return ref.at[pl.ds(h * n, n)]
    return ref.at[:, pl.ds(h * n, n)]


def _place_shard(w, layer, q_arr, col, name):
    _, rows, cols = w.shape

    def body(q_ref, w_ref, o_ref):
        o_ref[...] = w_ref[...].astype(BF16)

    if col:
        out_spec = pl.BlockSpec((rows, cols), lambda l, q: (0, q[0]))
        out_shape = jax.ShapeDtypeStruct((rows, N_CHIPS * cols), BF16)
    else:
        out_spec = pl.BlockSpec((None, None, rows, cols), lambda l, q: (q[0], 0, 0, 0))
        out_shape = jax.ShapeDtypeStruct((N_CHIPS, 1, rows, cols), BF16)
    return pl.pallas_call(
        body, name=name,
        grid_spec=pltpu.PrefetchScalarGridSpec(
            num_scalar_prefetch=1, grid=(1,),
            in_specs=[pl.BlockSpec((None, rows, cols), lambda l, q: (layer, 0, 0))], out_specs=out_spec),
        out_shape=out_shape, compiler_params=_cp(),
    )(q_arr, w)


def _handshake(peers):
    barrier = pltpu.get_barrier_semaphore()
    for p in peers:
        pl.semaphore_signal(barrier, inc=1, device_id=p, device_id_type=MESH)
    pl.semaphore_wait(barrier, len(peers))


def _on_sequencer(name, collective_id, n_sem, n_local, body):
    @pl.kernel(mesh=plsc.ScalarSubcoreMesh(axis_name="seq", num_cores=1), name=name,
               scratch_types=(pltpu.SemaphoreType.DMA((n_sem,)), pltpu.SemaphoreType.DMA((n_sem,)),
                              pltpu.SemaphoreType.DMA((max(n_local, 1),))),
               compiler_params=pltpu.CompilerParams(collective_id=collective_id))
    def launch(send_sems, recv_sems, local_sems):
        body((send_sems, recv_sems), local_sems)

    launch()


def _gather_plan(outs, col_fam, sems, handshake):
    n_w = len(outs)
    x, y, c = _me()
    myq = 2 * x + y
    sib = (x, y, 1 - c)
    if handshake:
        _handshake([sib] + [_chip_peer(x, y, j)[:2] + (c,) for j in (1, 2, 3)])

    def slot(w, q):
        if col_fam[w]:
            return _col_window(outs[w], q, outs[w].shape[1] // N_CHIPS)
        return outs[w].at[q]

    first = []
    for w in range(n_w):
        for j in (1, 2, 3):
            px, py, _ = _chip_peer(x, y, j)
            mine = _half_rows(slot(w, myq), c)
            cp = _remote(mine, mine, sems, w * 6 + j - 1, (px, py, c))
            cp.start()
            first.append(cp)
    passed = []
    for w in range(n_w):
        for j in (1, 2, 3):
            _, _, pq = _chip_peer(x, y, j)
            land = _half_rows(slot(w, pq), c)
            _remote(land, land, sems, w * 6 + j - 1, sib).wait_recv()
            cp = _remote(land, land, sems, w * 6 + 2 + j, sib)
            cp.start()
            passed.append(cp)
    for w in range(n_w):
        for j in (1, 2, 3):
            _, _, pq = _chip_peer(x, y, j)
            land = _half_rows(slot(w, pq), 1 - c)
            _remote(land, land, sems, w * 6 + 2 + j, sib).wait_recv()
    for cp in first + passed:
        cp.wait_send()


def _gather_weights(bufs, col_fam):
    n_w = len(bufs)

    def body(*refs):
        _gather_plan(refs[n_w:2 * n_w], col_fam, refs[2 * n_w:2 * n_w + 2], False)

    return pl.pallas_call(
        body, name="gather_weights", in_specs=[ANY] * n_w, out_specs=[ANY] * n_w,
        out_shape=[jax.ShapeDtypeStruct(b.shape, b.dtype) for b in bufs],
        input_output_aliases={w: w for w in range(n_w)},
        scratch_shapes=[pltpu.SemaphoreType.DMA((6 * n_w,)), pltpu.SemaphoreType.DMA((6 * n_w,))],
    )(*bufs)


def _gather_weights_async(bufs, col_fam, name, collective_id):
    refs = [jax.new_ref(b, memory_space=HBM) for b in bufs]
    _on_sequencer(name, collective_id, 6 * len(bufs), 0,
                  lambda sems, _: _gather_plan(refs, col_fam, sems, True))
    return [r[...] for r in refs]


def _grad_half(ref, col, h):
    return _half0(ref, h) if col else _half1(ref, h)


def _swap_halves_with_sibling(grads, col_fam):
    n_w = len(grads)

    def body(*refs):
        _swap_plan(refs[:n_w], refs[n_w:2 * n_w], col_fam, refs[2 * n_w:], False)

    return pl.pallas_call(
        body, name="grad_swap_sibling", in_specs=[ANY] * n_w, out_specs=[ANY] * n_w,
        out_shape=_swap_shapes(grads, col_fam),
        scratch_shapes=[pltpu.SemaphoreType.DMA((n_w,)), pltpu.SemaphoreType.DMA((n_w,))],
    )(*grads)


def _swap_shapes(grads, col_fam):
    out = []
    for w, g in enumerate(grads):
        shp = (g.shape[0] // 2, g.shape[1]) if col_fam[w] else (g.shape[0], g.shape[1] // 2, g.shape[2])
        out.append(jax.ShapeDtypeStruct(shp, g.dtype))
    return out


def _swap_plan(ins, outs, col_fam, sems, handshake):
    x, y, c = _me()
    sib = (x, y, 1 - c)
    if handshake:
        _handshake([sib])
    cps = [_remote(_grad_half(ins[w], col_fam[w], 1 - c), outs[w], sems, w, sib) for w in range(len(ins))]
    for cp in cps:
        cp.start()
    for cp in cps:
        cp.wait_recv()
    for cp in cps:
        cp.wait_send()


def _swap_halves_async(grads, col_fam, name, collective_id):
    srcs = [jax.new_ref(g, memory_space=HBM) for g in grads]
    dsts = [jax.empty_ref(s, memory_space=HBM) for s in _swap_shapes(grads, col_fam)]
    _on_sequencer(name, collective_id, len(grads), 0, lambda sems, _: _swap_plan(srcs, dsts, col_fam, sems, True))
    return [r[...] for r in srcs], [r[...] for r in dsts]


def _half_add(mines, recvs, c_arr, col_fam, name):
    n_w = len(mines)
    mine_specs, recv_specs = [], []
    for recv, col in zip(recvs, col_fam):
        if col:
            rows, n = recv.shape
            tr = rows // N_CHIPS
            mine_specs.append(pl.BlockSpec((tr, n), lambda i, c: (N_CHIPS * c[0] + i, 0)))
            recv_specs.append(pl.BlockSpec((tr, n), lambda i, c: (i, 0)))
        else:
            _, rows, n = recv.shape
            mine_specs.append(pl.BlockSpec((None, rows, n), lambda q, c: (q, c[0], 0)))
            recv_specs.append(pl.BlockSpec((None, rows, n), lambda q, c: (q, 0, 0)))

    def body(c_ref, *refs):
        for a_ref, b_ref, o_ref in zip(refs[:n_w], refs[n_w:2 * n_w], refs[2 * n_w:]):
            o_ref[...] = (a_ref[...].astype(F32) + b_ref[...].astype(F32)).astype(BF16)

    return pl.pallas_call(
        body, name=name,
        grid_spec=pltpu.PrefetchScalarGridSpec(num_scalar_prefetch=1, grid=(N_CHIPS,),
                                               in_specs=mine_specs + recv_specs, out_specs=recv_specs),
        out_shape=[jax.ShapeDtypeStruct(r.shape, BF16) for r in recvs], compiler_params=_cp(),
    )(c_arr, *mines, *recvs)


def _scatter_chip_sums(sums, col_fam):
    n_w = len(sums)

    def body(*refs):
        _scatter_plan(refs[:n_w], refs[n_w:2 * n_w], col_fam, refs[2 * n_w:2 * n_w + 2], refs[2 * n_w + 2], False)

    return pl.pallas_call(
        body, name="grad_scatter_chips", in_specs=[ANY] * n_w, out_specs=[ANY] * n_w,
        out_shape=_scatter_shapes(sums, col_fam),
        scratch_shapes=[pltpu.SemaphoreType.DMA((3 * n_w,)), pltpu.SemaphoreType.DMA((3 * n_w,)),
                        pltpu.SemaphoreType.DMA((n_w,))],
    )(*sums)


def _scatter_shapes(sums, col_fam):
    out = []
    for w, s in enumerate(sums):
        shp = (s.shape[0], s.shape[1] // N_CHIPS) if col_fam[w] else s.shape[1:]
        out.append(jax.ShapeDtypeStruct((N_CHIPS,) + shp, s.dtype))
    return out


def _scatter_plan(ins, outs, col_fam, sems, lsem, handshake):
    n_w = len(ins)
    x, y, c = _me()
    myq = 2 * x + y
    if handshake:
        _handshake([_chip_peer(x, y, j)[:2] + (c,) for j in (1, 2, 3)])

    def slab(w, q):
        if col_fam[w]:
            return _col_window(ins[w], q, ins[w].shape[1] // N_CHIPS)
        return ins[w].at[q]

    local = [pltpu.make_async_copy(slab(w, myq), outs[w].at[myq], lsem.at[w]) for w in range(n_w)]
    for cp in local:
        cp.start()
    cps = []
    for w in range(n_w):
        for j in (1, 2, 3):
            px, py, pq = _chip_peer(x, y, j)
            cp = _remote(slab(w, pq), outs[w].at[myq], sems, w * 3 + j - 1, (px, py, c))
            cp.start()
            cps.append(cp)
    for w in range(n_w):
        for j in (1, 2, 3):
            _, _, pq = _chip_peer(x, y, j)
            land = outs[w].at[pq]
            _remote(land, land, sems, w * 3 + j - 1, (x, y, c)).wait_recv()
    for cp in cps:
        cp.wait_send()
    for cp in local:
        cp.wait()


def _scatter_chip_sums_async(sums, col_fam, name, collective_id):
    srcs = [jax.new_ref(s, memory_space=HBM) for s in sums]
    dsts = [jax.empty_ref(s, memory_space=HBM) for s in _scatter_shapes(sums, col_fam)]
    _on_sequencer(name, collective_id, 3 * len(sums), len(sums),
                  lambda sems, lsem: _scatter_plan(srcs, dsts, col_fam, sems, lsem, True))
    return [r[...] for r in dsts]


def _sum_chips(parts, c_arr, prev, lead, shape, name):
    _, rows, n = parts.shape
    tr = rows // 2 if rows % 32 == 0 else rows
    nblk = rows // tr

    def body(c_ref, p_ref, *rest):
        o_ref = rest[-1]
        acc = p_ref[0].astype(F32)
        for q in range(1, N_CHIPS):
            acc = acc + p_ref[q].astype(F32)
        o_ref[...] = acc

    in_specs = [pl.BlockSpec((N_CHIPS, tr, n), lambda i, c: (0, i, 0))]
    args = [c_arr, parts]
    aliases = {}
    if prev is not None:
        in_specs.append(ANY)
        args.append(prev)
        aliases = {2: 0}
    return pl.pallas_call(
        body, name=name,
        grid_spec=pltpu.PrefetchScalarGridSpec(
            num_scalar_prefetch=1, grid=(nblk,), in_specs=in_specs,
            out_specs=pl.BlockSpec((None, tr, n), lambda i, c: (lead, c[0] * nblk + i, 0))),
        out_shape=jax.ShapeDtypeStruct(shape, F32), input_output_aliases=aliases, compiler_params=_cp(),
    )(*args)


def _join_plan(outs, place, sems, handshake):
    x, y, c = _me()
    sib = (x, y, 1 - c)
    if handshake:
        _handshake([sib])

    def half(k, h):
        o, lead = place[k]
        return _half_rows(outs[o].at[lead], h)

    cps = [_remote(half(k, c), half(k, c), sems, k, sib) for k in range(len(place))]
    for cp in cps:
        cp.start()
    for k in range(len(place)):
        land = half(k, 1 - c)
        _remote(land, land, sems, k, sib).wait_recv()
    for cp in cps:
        cp.wait_send()


def _join_halves(bufs, place, name):
    n_o = len(bufs)
    n_h = len(place)

    def body(*refs):
        _join_plan(refs[n_o:2 * n_o], place, refs[2 * n_o:2 * n_o + 2], False)

    return pl.pallas_call(
        body, name=name, in_specs=[ANY] * n_o, out_specs=[ANY] * n_o,
        out_shape=[jax.ShapeDtypeStruct(b.shape, b.dtype) for b in bufs],
        input_output_aliases={k: k for k in range(n_o)},
        scratch_shapes=[pltpu.SemaphoreType.DMA((n_h,)), pltpu.SemaphoreType.DMA((n_h,))],
    )(*bufs)


def _allreduce_rows(rows):
    n_dev = 8
    n_r = len(rows)
    assert n_r <= 8

    def body(*refs):
        r_refs = refs[:n_r]
        o_ref, slots, send_sems, recv_sems = refs[n_r:]
        x, y, c = _me()
        me = 4 * x + 2 * y + c
        slots[me] = jnp.concatenate([r[...] for r in r_refs] + [jnp.zeros((8 - n_r, D_MODEL), F32)], axis=0)

        def peer(k):
            return (1 - x if k & 4 else x, 1 - y if k & 2 else y, 1 - c if k & 1 else c)

        cps = []
        for k in range(1, n_dev):
            cp = pltpu.make_async_remote_copy(src_ref=slots.at[me], dst_ref=slots.at[me], send_sem=send_sems.at[k - 1],
                                              recv_sem=recv_sems.at[k - 1], device_id=peer(k), device_id_type=MESH)
            cp.start()
            cps.append(cp)
        for k in range(1, n_dev):
            px, py, pc = peer(k)
            land = slots.at[4 * px + 2 * py + pc]
            pltpu.make_async_remote_copy(src_ref=land, dst_ref=land, send_sem=send_sems.at[k - 1],
                                         recv_sem=recv_sems.at[k - 1], device_id=peer(k),
                                         device_id_type=MESH).wait_recv()
        for cp in cps:
            cp.wait_send()
        acc = slots[0]
        for d in range(1, n_dev):
            acc = acc + slots[d]
        o_ref[...] = acc

    vm = pl.BlockSpec(memory_space=pltpu.VMEM)
    return pl.pallas_call(
        body, name="allreduce_rows", in_specs=[vm] * n_r, out_specs=vm,
        out_shape=jax.ShapeDtypeStruct((8, D_MODEL), F32),
        scratch_shapes=[pltpu.VMEM((n_dev, 8, D_MODEL), F32), pltpu.SemaphoreType.DMA((n_dev - 1,)),
                        pltpu.SemaphoreType.DMA((n_dev - 1,))],
    )(*rows)


def _adamw(w, g, m, v, name):
    shape = w.shape
    if len(shape) == 1:
        lead, rows, cols = 1, 1, shape[0]
    else:
        rows, cols = shape[-2:]
        lead = math.prod(shape[:-2])
    args = [a.reshape(lead, rows, cols) for a in (w, g, m, v)]
    tr = rows // 2 if rows % 16 == 0 else rows

    def body(w_ref, g_ref, m_ref, v_ref, d_ref, nm_ref, nv_ref):
        gv = g_ref[...]
        nm = ADAM_B1 * m_ref[...] + (1.0 - ADAM_B1) * gv
        nv = ADAM_B2 * v_ref[...] + (1.0 - ADAM_B2) * jnp.square(gv)
        m_hat = nm / (1.0 - ADAM_B1 ** ADAM_STEP)
        v_hat = nv / (1.0 - ADAM_B2 ** ADAM_STEP)
        d_ref[...] = -ADAM_LR * (m_hat / (jnp.sqrt(v_hat) + ADAM_EPS) + ADAM_WD * w_ref[...])
        nm_ref[...] = nm
        nv_ref[...] = nv

    spec = pl.BlockSpec((None, tr, cols), lambda l, i: (l, i, 0))
    outs = pl.pallas_call(
        body, name=name, grid=(lead, rows // tr), in_specs=[spec] * 4, out_specs=[spec] * 3,
        out_shape=[jax.ShapeDtypeStruct((lead, rows, cols), F32)] * 3, compiler_params=_cp(),
    )(*args)
    return [o.reshape(shape) for o in outs]


def kernel(x, a_w_in, a_sink, a_w_out, b_w_in, b_w_out, norm_mix, norm_ffn, w_gate, w_up, w_down, final_norm, loss_target, m_a_w_in, m_a_sink, m_a_w_out, m_b_w_in, m_b_w_out, m_norm_mix, m_norm_ffn, m_w_gate, m_w_up, m_w_down, m_final_norm, v_a_w_in, v_a_sink, v_a_w_out, v_b_w_in, v_b_w_out, v_norm_mix, v_norm_ffn, v_w_gate, v_w_up, v_w_down, v_final_norm):
    weights = dict(a_w_in=a_w_in, a_sink=a_sink, a_w_out=a_w_out, b_w_in=b_w_in, b_w_out=b_w_out, norm_mix=norm_mix,
                   norm_ffn=norm_ffn, w_gate=w_gate, w_up=w_up, w_down=w_down, final_norm=final_norm)
    mom = dict(a_w_in=m_a_w_in, a_sink=m_a_sink, a_w_out=m_a_w_out, b_w_in=m_b_w_in, b_w_out=m_b_w_out,
               norm_mix=m_norm_mix, norm_ffn=m_norm_ffn, w_gate=m_w_gate, w_up=m_w_up, w_down=m_w_down,
               final_norm=m_final_norm)
    var = dict(a_w_in=v_a_w_in, a_sink=v_a_sink, a_w_out=v_a_w_out, b_w_in=v_b_w_in, b_w_out=v_b_w_out,
               norm_mix=v_norm_mix, norm_ffn=v_norm_ffn, w_gate=v_w_gate, w_up=v_w_up, w_down=v_w_down,
               final_norm=v_final_norm)
    order = ["a_w_in", "a_sink", "a_w_out", "b_w_in", "b_w_out", "norm_mix", "norm_ffn", "w_gate", "w_up", "w_down",
             "final_norm"]
    swapped = ("w_gate", "w_up")
    for n in swapped:
        weights[n], mom[n], var[n] = (a.transpose(0, 2, 1) for a in (weights[n], mom[n], var[n]))
    w_gate_t, w_up_t = weights["w_gate"], weights["w_up"]

    c_arr = lax.axis_index("c").astype(jnp.int32).reshape(1)
    q_arr = (2 * lax.axis_index("x") + lax.axis_index("y")).astype(jnp.int32).reshape(1)

    def placed(w, layer, col, nm):
        return _place_shard(w, layer, q_arr, col, f"place_{nm}")

    (a_in,) = _gather_weights_async([placed(a_w_in, 0, True, "a_in")], (True,), "gather_weights_first", 6)
    a_out, wg0, wu0, wd0 = _gather_weights_async(
        [placed(a_w_out, 0, False, "a_out"), placed(w_gate_t, 0, False, "wg0"), placed(w_up_t, 0, False, "wu0"),
         placed(w_down, 0, False, "wd0")], (False,) * 4, "gather_weights_layer0", 1)
    b_in, b_out, wg1, wu1, wd1 = _gather_weights_async(
        [placed(b_w_in, 0, True, "b_in"), placed(b_w_out, 0, False, "b_out"), placed(w_gate_t, 1, False, "wg1"),
         placed(w_up_t, 1, False, "wu1"), placed(w_down, 1, False, "wd1")], (True,) + (False,) * 4,
        "gather_weights_layer1", 7)
    a_out = a_out.reshape(D_MODEL, D_MODEL)
    b_out = b_out.reshape(D_MODEL, D_MODEL)
    wg, wu, wd = (wg0, wg1), (wu0, wu1), (wd0, wd1)

    gx, grads, vecs = _local_step(x, loss_target, a_in, a_sink[0], a_out, b_in, b_out, norm_mix, norm_ffn, wg, wu, wd,
                                  final_norm)

    rows_out = D_MODEL // N_CHIPS
    partials = [grads["a_in"], grads["b_in"],
                grads["a_out"].reshape(N_CHIPS, rows_out, D_MODEL), grads["b_out"].reshape(N_CHIPS, rows_out, D_MODEL),
                grads["wg"][0], grads["wg"][1], grads["wu"][0], grads["wu"][1], grads["wd"][0], grads["wd"][1]]
    col_fam = (True, True) + (False,) * 8
    names = ("a_in", "b_in", "a_out", "b_out", "wg0", "wg1", "wu0", "wu1", "wd0", "wd1")
    contrib = [None] * len(partials)

    def reduce_group(idx, tag, ids):
        parts = [partials[k] for k in idx]
        cols = tuple(col_fam[k] for k in idx)
        if ids is None:
            theirs = _swap_halves_with_sibling(parts, cols)
        else:
            parts, theirs = _swap_halves_async(parts, cols, f"grad_swap_{tag}", ids[0])
        sums = _half_add(parts, theirs, c_arr, cols, f"chip_sum_{tag}")
        if ids is None:
            out = _scatter_chip_sums(sums, cols)
        else:
            out = _scatter_chip_sums_async(sums, cols, f"grad_scatter_{tag}", ids[1])
        for k, o in zip(idx, out):
            contrib[k] = o

    reduce_group([1, 3, 5, 7, 9], "layer1", (2, 3))
    reduce_group([2, 4, 6, 8], "ffn0", (4, 5))
    reduce_group([0], "a_in", None)
    shapes = [a_w_in.shape, b_w_in.shape, a_w_out.shape, b_w_out.shape, w_down.shape, w_down.shape, w_down.shape]
    place = [(0, 0), (1, 0), (2, 0), (3, 0), (4, 0), (4, 1), (5, 0), (5, 1), (6, 0), (6, 1)]
    bufs = [None] * len(shapes)
    for p, nm, (o, lead) in zip(contrib, names, place):
        bufs[o] = _sum_chips(p, c_arr, bufs[o], lead, shapes[o], f"sum_chips_{nm}")
    g_a_in, g_b_in, g_a_out, g_b_out, g_wg, g_wu, g_wd = _join_halves(bufs, place, "grad_join_sibling")

    sink_row = jnp.pad(vecs["sink"][0:1], ((0, 0), (0, D_MODEL - LANES)))
    tot = _allreduce_rows([vecs["norm_mix"][0], vecs["norm_mix"][1], vecs["norm_ffn"][0], vecs["norm_ffn"][1],
                           vecs["final"], vecs["loss_cols"], sink_row])
    loss = (0.5 / D_MODEL) * jnp.sum(tot[5])
    gw = dict(a_w_in=g_a_in, a_sink=tot[6:7, :N_HEADS], a_w_out=g_a_out, b_w_in=g_b_in, b_w_out=g_b_out,
              norm_mix=tot[0:2], norm_ffn=tot[2:4], w_gate=g_wg, w_up=g_wu, w_down=g_wd, final_norm=tot[4])

    delta, new_m, new_v = {}, {}, {}
    for n in order:
        delta[n], new_m[n], new_v[n] = _adamw(weights[n], gw[n], mom[n], var[n], f"adamw_{n}")
    for n in swapped:
        gw[n], delta[n], new_m[n], new_v[n] = (a.transpose(0, 2, 1) for a in (gw[n], delta[n], new_m[n], new_v[n]))
    return (loss, gx, *[gw[n] for n in order], *[delta[n] for n in order], *[new_m[n] for n in order],
            *[new_v[n] for n in order])
```

```python
import math

import jax
import jax.numpy as jnp
from jax import lax
from jax.experimental import pallas as pl
from jax.experimental.pallas import tpu as pltpu
from jax.experimental.pallas import tpu_sc as plsc

F32 = jnp.float32
BF16 = jnp.bfloat16

D_MODEL = 1024
HEAD_DIM = 64
N_HEADS = 16
N_KV = 4
QKV_W = 1536
D_FF = 2816
N_CHIPS = 4
FF_SH = D_FF // N_CHIPS
HALF_WINDOW_A = 128
DILATED = ((128, 1), (512, 4), (2048, 16))
ROPE_THETA = 10000.0
RMS_EPS = 1e-6
NEG_INF = -1e30
LANES = 128
ADAM_LR, ADAM_B1, ADAM_B2, ADAM_EPS, ADAM_WD, ADAM_STEP = 0.001, 0.9, 0.999, 1e-08, 0.01, 10
VMEM_LIMIT = 56 * 1024 * 1024
ROWS = 512
MATMUL_ROWS = 1024
FFN_BWD_ROWS = 256
LOG2E = math.log2(math.e)
LN2 = math.log(2.0)
Q_SCALE = LOG2E / math.sqrt(HEAD_DIM)
GRAD_TOKENS = 2048
MESH = pl.DeviceIdType.MESH


def _cp(**kw):
    return pltpu.CompilerParams(vmem_limit_bytes=VMEM_LIMIT, **kw)


def _row_tile(t, cap):
    tm = min(cap, t)
    assert t % tm == 0
    return tm


def _rope_tables(seq, dil):
    inv = 1.0 / (ROPE_THETA ** (jnp.arange(0, HEAD_DIM, 2, dtype=F32) / HEAD_DIM))
    ang = jnp.arange(seq, dtype=F32)[:, None] * inv[None, :]
    cos, sin = jnp.cos(ang), jnp.sin(ang)
    cos = jnp.tile(cos, (1, 4))
    sin = jnp.concatenate([-sin, sin, -sin, sin], axis=1)

    def perm(t):
        return t.reshape(seq // dil, dil, LANES).transpose(1, 0, 2).reshape(seq, LANES)

    return perm(cos), perm(sin)


def _swap_halves(t):
    lane = lax.broadcasted_iota(jnp.int32, t.shape, 1)
    return jnp.where((lane % HEAD_DIM) < HEAD_DIM // 2, pltpu.roll(t, LANES - 32, 1), pltpu.roll(t, 32, 1))


def _rope(t, cos, sin):
    return t * cos + _swap_halves(t) * sin


def _rope_t(t, cos, sin):
    return t * cos - _swap_halves(t) * sin


def _to_residue(t, batch, dil):
    if dil == 1:
        return t
    s = t.shape[0] // batch
    return t.reshape(batch, s // dil, dil, t.shape[1]).transpose(0, 2, 1, 3).reshape(t.shape)


def _from_residue(t, batch, dil):
    if dil == 1:
        return t
    s = t.shape[0] // batch
    return t.reshape(batch, dil, s // dil, t.shape[1]).transpose(0, 2, 1, 3).reshape(t.shape)


def _rms_fwd(x, w, name):
    t = x.shape[0]
    tm = _row_tile(t, ROWS)

    def body(x_ref, w_ref, o_ref):
        o_ref[...] = _rms_tile(x_ref[...], w_ref[...]).astype(BF16)

    return pl.pallas_call(
        body, name=name, grid=(t // tm,),
        in_specs=[pl.BlockSpec((tm, D_MODEL), lambda i: (i, 0)), pl.BlockSpec((1, D_MODEL), lambda i: (0, 0))],
        out_specs=pl.BlockSpec((tm, D_MODEL), lambda i: (i, 0)),
        out_shape=jax.ShapeDtypeStruct((t, D_MODEL), BF16), compiler_params=_cp(),
    )(x, w)


def _rms_bwd_tile(xv, wv, dy, dres):
    r = lax.rsqrt(jnp.mean(xv * xv, axis=-1, keepdims=True) + RMS_EPS)
    xh = xv * r
    dxh = dy * wv
    dx = dres + r * (dxh - xh * jnp.mean(dxh * xh, axis=-1, keepdims=True))
    return dx, jnp.sum(dy * xh, axis=0, keepdims=True)


def _accumulate(ref, part):
    @pl.when(pl.program_id(0) == 0)
    def _():
        ref[...] = jnp.zeros_like(ref)

    ref[...] += part


def _rms_bwd(x, w, dhs, dres, name):
    t = x.shape[0]
    tm = _row_tile(t, ROWS)
    n = len(dhs)

    def body(*refs):
        x_ref, w_ref = refs[0], refs[1]
        dh_refs = refs[2:2 + n]
        dres_ref = refs[2 + n]
        dx_ref, dxb_ref, dw_ref = refs[3 + n:]
        dy = dh_refs[0][...].astype(F32)
        for k in range(1, n):
            dy = dy + dh_refs[k][...].astype(F32)
        dx, dw = _rms_bwd_tile(x_ref[...], w_ref[...], dy, dres_ref[...])
        dx_ref[...] = dx
        dxb_ref[...] = dx.astype(BF16)
        _accumulate(dw_ref, dw)

    row = pl.BlockSpec((tm, D_MODEL), lambda i: (i, 0))
    vec = pl.BlockSpec((1, D_MODEL), lambda i: (0, 0))
    return pl.pallas_call(
        body, name=name, grid=(t // tm,),
        in_specs=[row, vec] + [row] * n + [row],
        out_specs=[row, row, vec],
        out_shape=[jax.ShapeDtypeStruct((t, D_MODEL), F32), jax.ShapeDtypeStruct((t, D_MODEL), BF16),
                   jax.ShapeDtypeStruct((1, D_MODEL), F32)],
        compiler_params=_cp(),
    )(x, w, *dhs, dres)


def _final_tile(xv, wv, tv):
    r = lax.rsqrt(jnp.mean(xv * xv, axis=-1, keepdims=True) + RMS_EPS)
    xh = xv * r
    err = xh * wv - tv
    dy = err * (1.0 / D_MODEL)
    dxh = dy * wv
    dx = r * (dxh - xh * jnp.mean(dxh * xh, axis=-1, keepdims=True))
    return dx, jnp.sum(err * err, axis=0, keepdims=True), jnp.sum(dy * xh, axis=0, keepdims=True)


def _qkv_proj(h, w, cos, sin, group, name):
    t = h.shape[0]
    seq = cos.shape[0]
    tm = _row_tile(seq, MATMUL_ROWS)
    n_q = N_HEADS * HEAD_DIM // LANES
    n_rope = (N_HEADS + N_KV) * HEAD_DIM // LANES
    scale = Q_SCALE

    def body(h_ref, w_ref, cos_ref, sin_ref, o_ref):
        acc = jnp.dot(h_ref[...], w_ref[...], preferred_element_type=F32)
        cs, sn = cos_ref[...], sin_ref[...]
        csq, snq = cs * scale, sn * scale
        for c in range(QKV_W // LANES):
            blk = acc[:, c * LANES:(c + 1) * LANES]
            if c < n_q:
                blk = _rope(blk, csq, snq)
            elif c < n_rope:
                blk = _rope(blk, cs, sn)
            o_ref[:, c * LANES:(c + 1) * LANES] = blk.astype(BF16)

    tab = pl.BlockSpec((tm, LANES), lambda i: (i % (seq // tm), 0))
    return pl.pallas_call(
        body, name=name, grid=(t // tm,),
        in_specs=[pl.BlockSpec((tm, D_MODEL), lambda i: (i, 0)),
                  pl.BlockSpec((D_MODEL, QKV_W), lambda i: (0, group)), tab, tab],
        out_specs=pl.BlockSpec((tm, QKV_W), lambda i: (i, 0)),
        out_shape=jax.ShapeDtypeStruct((t, QKV_W), BF16), compiler_params=_cp(),
    )(h, w, cos, sin)


def _rms_tile(xv, wv):
    return (xv * lax.rsqrt(jnp.mean(xv * xv, axis=-1, keepdims=True) + RMS_EPS)) * wv


def _mm_res(a, w, res, nw, name):
    t, k = a.shape
    tm = _row_tile(t, ROWS)

    def body(a_ref, w_ref, r_ref, nw_ref, o_ref, h_ref):
        xv = r_ref[...] + jnp.dot(a_ref[...], w_ref[...], preferred_element_type=F32)
        o_ref[...] = xv
        h_ref[...] = _rms_tile(xv, nw_ref[...]).astype(BF16)

    row = pl.BlockSpec((tm, D_MODEL), lambda i: (i, 0))
    return pl.pallas_call(
        body, name=name, grid=(t // tm,),
        in_specs=[pl.BlockSpec((tm, k), lambda i: (i, 0)),
                  pl.BlockSpec((k, D_MODEL), lambda i: (0, 0), pipeline_mode=pl.Buffered(1)), row,
                  pl.BlockSpec((1, D_MODEL), lambda i: (0, 0))],
        out_specs=[row, row],
        out_shape=[jax.ShapeDtypeStruct((t, D_MODEL), F32), jax.ShapeDtypeStruct((t, D_MODEL), BF16)],
        compiler_params=_cp(),
    )(a, w, res, nw)


def _mm_nt(dy, w, group, out_dtype, name):
    t, n = dy.shape
    k = w.shape[0]
    tm = _row_tile(t, MATMUL_ROWS)

    def body(dy_ref, w_ref, o_ref):
        o_ref[...] = lax.dot_general(dy_ref[...], w_ref[...], (((1,), (1,)), ((), ())),
                                     preferred_element_type=F32).astype(out_dtype)

    return pl.pallas_call(
        body, name=name, grid=(t // tm,),
        in_specs=[pl.BlockSpec((tm, n), lambda i: (i, 0)), pl.BlockSpec((k, n), lambda i: (0, group))],
        out_specs=pl.BlockSpec((tm, k), lambda i: (i, 0)),
        out_shape=jax.ShapeDtypeStruct((t, k), out_dtype), compiler_params=_cp(),
    )(dy, w)


def _mm_nt_rms(dy, w, x, nw, dres, name):
    t, n = dy.shape
    tm = _row_tile(t, ROWS)

    def body(dy_ref, w_ref, x_ref, nw_ref, dres_ref, dx_ref, dw_ref):
        dh = lax.dot_general(dy_ref[...], w_ref[...], (((1,), (1,)), ((), ())), preferred_element_type=F32)
        dx, dw = _rms_bwd_tile(x_ref[...], nw_ref[...], dh, dres_ref[...])
        dx_ref[...] = dx
        _accumulate(dw_ref, dw)

    row = pl.BlockSpec((tm, D_MODEL), lambda i: (i, 0))
    vec = pl.BlockSpec((1, D_MODEL), lambda i: (0, 0))
    return pl.pallas_call(
        body, name=name, grid=(t // tm,),
        in_specs=[pl.BlockSpec((tm, n), lambda i: (i, 0)),
                  pl.BlockSpec((D_MODEL, n), lambda i: (0, 0), pipeline_mode=pl.Buffered(1)), row, vec, row],
        out_specs=[row, vec],
        out_shape=[jax.ShapeDtypeStruct((t, D_MODEL), F32), jax.ShapeDtypeStruct((1, D_MODEL), F32)],
        compiler_params=_cp(),
    )(dy, w, x, nw, dres)


def _out_bwd(dx, w, o, name):
    t = dx.shape[0]
    tm = _row_tile(t, ROWS)

    def body(dx_ref, w_ref, o_ref, et_ref, do_ref, adj_ref):
        do = lax.dot_general(dx_ref[...], w_ref[...], (((1,), (1,)), ((), ())), preferred_element_type=F32)
        do_ref[...] = do.astype(BF16)
        adj_ref[...] = -_dot_heads(do * o_ref[...].astype(F32), et_ref[...])

    row = pl.BlockSpec((tm, D_MODEL), lambda i: (i, 0))
    return pl.pallas_call(
        body, name=name, grid=(t // tm,),
        in_specs=[row, pl.BlockSpec((D_MODEL, D_MODEL), lambda i: (0, 0)), row,
                  pl.BlockSpec((D_MODEL, LANES), lambda i: (0, 0))],
        out_specs=[row, pl.BlockSpec((tm, LANES), lambda i: (i, 0))],
        out_shape=[jax.ShapeDtypeStruct((t, D_MODEL), BF16), jax.ShapeDtypeStruct((t, LANES), F32)],
        compiler_params=_cp(),
    )(dx, w, o, _head_expander().T)


def _mm_tn(a, bs, name):
    aq = a.ndim == 3
    bq = bs[0].ndim == 3
    t, ka = a.shape[-2:]
    n = bs[0].shape[-1]
    nq = N_CHIPS if (aq or bq) else 1
    tt = _row_tile(t, GRAD_TOKENS)
    tn = n if n <= 1024 else 768
    assert n % tn == 0
    nb = len(bs)
    steps = t // tt

    def body(*refs):
        a_ref = refs[0]
        b_refs = refs[1:1 + nb]
        o_refs = refs[1 + nb:1 + 2 * nb]
        acc_refs = refs[1 + 2 * nb:]
        s = pl.program_id(2)
        av = a_ref[...]
        for b_ref, o_ref, acc_ref in zip(b_refs, o_refs, acc_refs):
            @pl.when(s == 0)
            def _():
                acc_ref[...] = jnp.zeros_like(acc_ref)

            acc_ref[...] += lax.dot_general(av, b_ref[...], (((0,), (0,)), ((), ())), preferred_element_type=F32)

            @pl.when(s == steps - 1)
            def _():
                o_ref[...] = acc_ref[...].astype(BF16)

    a_spec = (pl.BlockSpec((None, tt, ka), lambda q, j, s: (q, s, 0)) if aq
              else pl.BlockSpec((tt, ka), lambda q, j, s: (s, 0)))
    b_spec = (pl.BlockSpec((None, tt, tn), lambda q, j, s: (q, s, j)) if bq
              else pl.BlockSpec((tt, tn), lambda q, j, s: (s, j)))
    if nq > 1:
        o_spec = pl.BlockSpec((None, ka, tn), lambda q, j, s: (q, 0, j))
        o_shape = jax.ShapeDtypeStruct((nq, ka, n), BF16)
    else:
        o_spec = pl.BlockSpec((ka, tn), lambda q, j, s: (0, j))
        o_shape = jax.ShapeDtypeStruct((ka, n), BF16)
    outs = pl.pallas_call(
        body, name=name, grid=(nq, n // tn, steps),
        in_specs=[a_spec] + [b_spec] * nb, out_specs=[o_spec] * nb, out_shape=[o_shape] * nb,
        scratch_shapes=[pltpu.VMEM((ka, tn), F32)] * nb, compiler_params=_cp(),
    )(a, *bs)
    return outs


def _sigmoid(x):
    return 1.0 / (1.0 + jnp.exp(-x))


def _ffn_up(h, wg, wu, layer, name):
    t = h.shape[0]
    tm = _row_tile(t, MATMUL_ROWS)
    nt = (((1,), (1,)), ((), ()))

    def body(h_ref, wg_ref, wu_ref, a_ref, dg_ref, du_ref):
        hv = h_ref[...]
        g = lax.dot_general(hv, wg_ref[...], nt, preferred_element_type=F32)
        u = lax.dot_general(hv, wu_ref[...], nt, preferred_element_type=F32)
        sg = _sigmoid(g)
        silu = g * sg
        a_ref[...] = (silu * u).astype(BF16)
        dg_ref[...] = (sg * (1.0 + g * (1.0 - sg)) * u).astype(BF16)
        du_ref[...] = silu.astype(BF16)

    wspec = pl.BlockSpec((None, None, FF_SH, D_MODEL), lambda q, i: (q, layer, 0, 0))
    ospec = pl.BlockSpec((None, tm, FF_SH), lambda q, i: (q, i, 0))
    oshape = jax.ShapeDtypeStruct((N_CHIPS, t, FF_SH), BF16)
    return pl.pallas_call(
        body, name=name, grid=(N_CHIPS, t // tm),
        in_specs=[pl.BlockSpec((tm, D_MODEL), lambda q, i: (i, 0)), wspec, wspec],
        out_specs=[ospec] * 3, out_shape=[oshape] * 3, compiler_params=_cp(),
    )(h, wg, wu)


def _ffn_down(a, wd, res, layer, name, norm_w=None, head=None):
    t = a.shape[1]
    tm = _row_tile(t, ROWS)
    resident = pl.BlockSpec((N_CHIPS, None, FF_SH, D_MODEL), lambda i: (0, layer, 0, 0), pipeline_mode=pl.Buffered(1))
    row = pl.BlockSpec((tm, D_MODEL), lambda i: (i, 0))
    vec = pl.BlockSpec((1, D_MODEL), lambda i: (0, 0))

    def hidden(a_ref, w_ref, r_ref):
        acc = r_ref[...]
        for q in range(N_CHIPS):
            acc = acc + jnp.dot(a_ref[q], w_ref[q], preferred_element_type=F32)
        return acc

    if head is None:
        def body(a_ref, w_ref, r_ref, nw_ref, o_ref, h_ref):
            xv = hidden(a_ref, w_ref, r_ref)
            o_ref[...] = xv
            h_ref[...] = _rms_tile(xv, nw_ref[...]).astype(BF16)

        return pl.pallas_call(
            body, name=name, grid=(t // tm,),
            in_specs=[pl.BlockSpec((N_CHIPS, tm, FF_SH), lambda i: (0, i, 0)), resident, row, vec],
            out_specs=[row, row],
            out_shape=[jax.ShapeDtypeStruct((t, D_MODEL), F32), jax.ShapeDtypeStruct((t, D_MODEL), BF16)],
            compiler_params=_cp(),
        )(a, wd, res, norm_w)

    def body(a_ref, w_ref, r_ref, nw_ref, t_ref, dx_ref, dxb_ref, l_ref, dw_ref):
        dx, sq, dw = _final_tile(hidden(a_ref, w_ref, r_ref), nw_ref[...], t_ref[...])
        dx_ref[...] = dx
        dxb_ref[...] = dx.astype(BF16)
        _accumulate(l_ref, sq)
        _accumulate(dw_ref, dw)

    return pl.pallas_call(
        body, name=name, grid=(t // tm,),
        in_specs=[pl.BlockSpec((N_CHIPS, tm, FF_SH), lambda i: (0, i, 0)), resident, row, vec, row],
        out_specs=[row, row, vec, vec],
        out_shape=[jax.ShapeDtypeStruct((t, D_MODEL), F32), jax.ShapeDtypeStruct((t, D_MODEL), BF16),
                   jax.ShapeDtypeStruct((1, D_MODEL), F32), jax.ShapeDtypeStruct((1, D_MODEL), F32)],
        compiler_params=_cp(),
    )(a, wd, res, *head)


def _ffn_bwd(dy, wd, wg, wu, fg, fu, x, nw, dres, name):
    t = dy.shape[0]
    tm = _row_tile(t, FFN_BWD_ROWS)
    nt = (((1,), (1,)), ((), ()))

    def body(dy_ref, wd_ref, wg_ref, wu_ref, fg_ref, fu_ref, x_ref, nw_ref, dres_ref,
             dg_ref, du_ref, dx_ref, dxb_ref, dw_ref):
        dyv = dy_ref[...]
        acc = jnp.zeros((tm, D_MODEL), F32)
        for q in range(N_CHIPS):
            da = lax.dot_general(dyv, wd_ref[q], nt, preferred_element_type=F32)
            dg = (da * fg_ref[q].astype(F32)).astype(BF16)
            du = (da * fu_ref[q].astype(F32)).astype(BF16)
            dg_ref[q] = dg
            du_ref[q] = du
            acc = acc + jnp.dot(dg, wg_ref[q], preferred_element_type=F32)
            acc = acc + jnp.dot(du, wu_ref[q], preferred_element_type=F32)
        dx, dw = _rms_bwd_tile(x_ref[...], nw_ref[...], acc, dres_ref[...])
        dx_ref[...] = dx
        dxb_ref[...] = dx.astype(BF16)
        _accumulate(dw_ref, dw)

    aspec = pl.BlockSpec((N_CHIPS, tm, FF_SH), lambda i: (0, i, 0))
    wspec = pl.BlockSpec((N_CHIPS, None, FF_SH, D_MODEL), lambda i: (0, 0, 0, 0), pipeline_mode=pl.Buffered(1))
    row = pl.BlockSpec((tm, D_MODEL), lambda i: (i, 0))
    vec = pl.BlockSpec((1, D_MODEL), lambda i: (0, 0))
    ashape = jax.ShapeDtypeStruct((N_CHIPS, t, FF_SH), BF16)
    return pl.pallas_call(
        body, name=name, grid=(t // tm,),
        in_specs=[row, wspec, wspec, wspec, aspec, aspec, row, vec, row],
        out_specs=[aspec, aspec, row, row, vec],
        out_shape=[ashape, ashape, jax.ShapeDtypeStruct((t, D_MODEL), F32), jax.ShapeDtypeStruct((t, D_MODEL), BF16),
                   jax.ShapeDtypeStruct((1, D_MODEL), F32)],
        compiler_params=_cp(),
    )(dy, wd, wg, wu, fg, fu, x, nw, dres)


def _attn_geometry(length, half_window):
    qb = min(LANES, length)
    kw = min(qb + 2 * half_window, length)
    return qb, kw, length // qb


def _dup_kv(src_ref, dst_ref, s, length):
    ch = min(length, 256)
    lo = lax.broadcasted_iota(jnp.int32, (ch, LANES), 1) < HEAD_DIM

    def chunk(c, carry):
        r0 = pl.multiple_of(c * ch, ch)
        for j in range(N_KV // 2):
            tile = src_ref[s, pl.ds(r0, ch), j * LANES:(j + 1) * LANES].astype(F32)
            rolled = pltpu.roll(tile, HEAD_DIM, 1)
            dst_ref[2 * j, pl.ds(r0, ch), :] = jnp.where(lo, tile, rolled).astype(BF16)
            dst_ref[2 * j + 1, pl.ds(r0, ch), :] = jnp.where(lo, rolled, tile).astype(BF16)
        return carry

    lax.fori_loop(0, length // ch, chunk, 0)


def _stack_heads(ref, s, q0, qb, g):
    lo = lax.broadcasted_iota(jnp.int32, (qb, LANES), 1) < HEAD_DIM
    parts = []
    for a in range(4):
        col = (2 * g + a // 2) * LANES
        tile = ref[s, pl.ds(q0, qb), col:col + LANES]
        keep = lo if a % 2 == 0 else jnp.logical_not(lo)
        parts.append(jnp.where(keep, tile, jnp.zeros_like(tile)))
    return jnp.concatenate(parts, axis=0)


def _unstack_pair_t(stacked_t, qb, pair):
    both = jnp.concatenate([stacked_t[:, (2 * pair) * qb:(2 * pair + 1) * qb],
                            stacked_t[:, (2 * pair + 1) * qb:(2 * pair + 2) * qb]], axis=0)
    return both.T


def _band_mask_t(q0, k0, qb, kw, half_window):
    key = lax.broadcasted_iota(jnp.int32, (kw, 4 * qb), 0)
    qry = lax.broadcasted_iota(jnp.int32, (kw, 4 * qb), 1) & (qb - 1)
    return jnp.abs((q0 + qry) - (k0 + key)) <= half_window


def _block_origin(i, qb, kw, half_window, length):
    if isinstance(i, int):
        return i * qb, min(max(i * qb - half_window, 0), length - kw)
    return (pl.multiple_of(i * qb, qb),
            pl.multiple_of(jnp.clip(i * qb - half_window, 0, length - kw), HEAD_DIM))


def _head_row(vals, qb):
    return jnp.concatenate([jnp.broadcast_to(v, (1, qb)).astype(F32) for v in vals], axis=1)


def _attn_fwd(qkv, sink, n_seq, length, half_window, seq_blk, out_dtype, name):
    qb, kw, nblk = _attn_geometry(length, half_window)
    with_sink = sink is not None
    nt = (((1,), (1,)), ((), ()))
    tn = (((0,), (0,)), ((), ()))
    qkv3 = qkv.reshape(n_seq, length, QKV_W)

    def body(*refs):
        refs = list(refs)
        sink_ref = refs.pop(0) if with_sink else None
        q_ref, k_ref, v_ref, o_ref, lse_ref = refs[:5]
        kx_ref, vx_ref = refs[-2:]
        head_row = lax.broadcasted_iota(jnp.int32, (N_HEADS, qb), 0)
        for s in range(seq_blk):
            _dup_kv(k_ref, kx_ref, s, length)
            _dup_kv(v_ref, vx_ref, s, length)

            def block(i, carry):
                q0, k0 = _block_origin(i, qb, kw, half_window, length)
                valid = _band_mask_t(q0, k0, qb, kw, half_window)
                lse_tile = jnp.zeros((N_HEADS, qb), F32)
                groups = range(N_KV)
                sts = [lax.dot_general(kx_ref[g, pl.ds(k0, kw), :], _stack_heads(q_ref, s, q0, qb, g), nt,
                                       preferred_element_type=F32) for g in groups]
                sts = [jnp.where(valid, st, NEG_INF) for st in sts]
                ms = [jnp.max(st, axis=0, keepdims=True) for st in sts]
                if with_sink:
                    sks = [_head_row([sink_ref[4 * g + a] * LOG2E for a in range(4)], qb) for g in groups]
                    ms = [jnp.maximum(m, sk) for m, sk in zip(ms, sks)]
                es = [jnp.exp2(st - m) for st, m in zip(sts, ms)]
                dens = [jnp.sum(e, axis=0, keepdims=True) for e in es]
                if with_sink:
                    dens = [den + jnp.exp2(sk - m) for den, sk, m in zip(dens, sks, ms)]
                ots = [lax.dot_general(vx_ref[g, pl.ds(k0, kw), 0:HEAD_DIM], es[g].astype(BF16), tn,
                                       preferred_element_type=F32) / dens[g] for g in groups]
                for g in groups:
                    for pair in range(2):
                        col = (2 * g + pair) * LANES
                        o_ref[s, pl.ds(q0, qb), col:col + LANES] = _unstack_pair_t(ots[g], qb, pair).astype(out_dtype)
                    lse = ms[g] * LN2 + jnp.log(dens[g])
                    for a in range(4):
                        lse_tile = jnp.where(head_row == 4 * g + a, lse[:, a * qb:(a + 1) * qb], lse_tile)
                lse_ref[s, :, pl.ds(q0, qb)] = lse_tile
                return carry

            if nblk == 1:
                block(0, 0)
            else:
                lax.fori_loop(0, nblk, block, 0)

    in_specs = [pl.BlockSpec((seq_blk, length, N_HEADS * HEAD_DIM), lambda n: (n, 0, 0)),
                pl.BlockSpec((seq_blk, length, N_KV * HEAD_DIM), lambda n: (n, 0, 4)),
                pl.BlockSpec((seq_blk, length, N_KV * HEAD_DIM), lambda n: (n, 0, 5))]
    args = [qkv3, qkv3, qkv3]
    if with_sink:
        in_specs.insert(0, pl.BlockSpec(memory_space=pltpu.SMEM))
        args.insert(0, sink)
    out_specs = [pl.BlockSpec((seq_blk, length, D_MODEL), lambda n: (n, 0, 0)),
                 pl.BlockSpec((seq_blk, N_HEADS, length), lambda n: (n, 0, 0))]
    out_shape = [jax.ShapeDtypeStruct((n_seq, length, D_MODEL), out_dtype),
                 jax.ShapeDtypeStruct((n_seq, N_HEADS, length), F32)]
    o, lse = pl.pallas_call(
        body, name=name, grid=(n_seq // seq_blk,), in_specs=in_specs, out_specs=out_specs, out_shape=out_shape,
        scratch_shapes=[pltpu.VMEM((N_KV, length, LANES), BF16), pltpu.VMEM((N_KV, length, LANES), BF16)],
        compiler_params=_cp(),
    )(*args)
    return o.reshape(n_seq * length, D_MODEL), lse


def _attn_bwd(qkv, do, adj, lse, sink, cos, sin, n_seq, length, half_window, seq_blk, dil, name):
    qb, kw, nblk = _attn_geometry(length, half_window)
    scale = 1.0 / math.sqrt(HEAD_DIM)
    with_sink = sink is not None
    nt = (((1,), (1,)), ((), ()))
    tn = (((0,), (0,)), ((), ()))
    qkv3 = qkv.reshape(n_seq, length, QKV_W)
    do3 = do.reshape(n_seq, length, D_MODEL)
    tabs = [t.reshape(dil, length, LANES) for t in (cos, sin)]
    tab_blocks = dil // seq_blk if dil >= seq_blk else 1

    def body(*refs):
        refs = list(refs)
        sink_ref = refs.pop(0) if with_sink else None
        q_ref, k_ref, v_ref, do_ref, aux_ref, lse_ref, cos_ref, sin_ref, dqkv_ref = refs[:9]
        ds_ref = refs[9] if with_sink else None
        kx_ref, vx_ref, dkx_ref, dvx_ref = refs[-4:]
        lane = lax.broadcasted_iota(jnp.int32, (1, LANES), 1)
        if with_sink:
            @pl.when(pl.program_id(0) == 0)
            def _():
                ds_ref[...] = jnp.zeros_like(ds_ref)

        for s in range(seq_blk):
            ts = s % dil
            _dup_kv(k_ref, kx_ref, s, length)
            _dup_kv(v_ref, vx_ref, s, length)
            dkx_ref[...] = jnp.zeros_like(dkx_ref)
            dvx_ref[...] = jnp.zeros_like(dvx_ref)

            def block(i, dsink):
                q0, k0 = _block_origin(i, qb, kw, half_window, length)
                valid = _band_mask_t(q0, k0, qb, kw, half_window)
                cs = cos_ref[ts, pl.ds(q0, qb), :] * scale
                sn = sin_ref[ts, pl.ds(q0, qb), :] * scale
                adj_tile = aux_ref[s, :, pl.ds(q0, qb)]
                lse_tile = lse_ref[s, :, pl.ds(q0, qb)]
                groups = range(N_KV)
                qss = [_stack_heads(q_ref, s, q0, qb, g) for g in groups]
                doss = [_stack_heads(do_ref, s, q0, qb, g) for g in groups]
                kxs = [kx_ref[g, pl.ds(k0, kw), :] for g in groups]
                sts = [lax.dot_general(kxs[g], qss[g], nt, preferred_element_type=F32) for g in groups]
                dpts = [lax.dot_general(vx_ref[g, pl.ds(k0, kw), :], doss[g], nt, preferred_element_type=F32)
                        for g in groups]
                lses = [_head_row([lse_tile[4 * g + a:4 * g + a + 1, :] * LOG2E for a in range(4)], qb) for g in groups]
                shifts = [_head_row([adj_tile[4 * g + a:4 * g + a + 1, :] for a in range(4)], qb) for g in groups]
                pts = [jnp.exp2(jnp.where(valid, sts[g], NEG_INF) - lses[g]) for g in groups]
                dsbs = [(pts[g] * (dpts[g] + shifts[g])).astype(BF16) for g in groups]
                pbs = [pt.astype(BF16) for pt in pts]
                if with_sink:
                    for g in groups:
                        sk = _head_row([sink_ref[4 * g + a] * LOG2E for a in range(4)], qb)
                        dsk = jnp.exp2(sk - lses[g]) * shifts[g]
                        for a in range(4):
                            tot = jnp.sum(dsk[:, a * qb:(a + 1) * qb], axis=1, keepdims=True)
                            dsink = dsink + jnp.where(lane == 4 * g + a, tot, 0.0)
                dqts = [lax.dot_general(kx_ref[g, pl.ds(k0, kw), 0:HEAD_DIM], dsbs[g], tn, preferred_element_type=F32)
                        for g in groups]
                for g in groups:
                    for pair in range(2):
                        col = (2 * g + pair) * LANES
                        tile = _rope_t(_unstack_pair_t(dqts[g], qb, pair), cs, sn)
                        dqkv_ref[s, pl.ds(q0, qb), col:col + LANES] = tile.astype(BF16)
                for g in groups:
                    dkx_ref[g, pl.ds(k0, kw), :] += jnp.dot(dsbs[g], qss[g], preferred_element_type=F32)
                    dvx_ref[g, pl.ds(k0, kw), :] += jnp.dot(pbs[g], doss[g], preferred_element_type=F32)
                return dsink

            if nblk == 1:
                dsink = block(0, jnp.zeros((1, LANES), F32))
            else:
                dsink = lax.fori_loop(0, nblk, block, jnp.zeros((1, LANES), F32))
            if with_sink:
                ds_ref[0:1, :] += dsink

            ch = min(length, 256)
            lo_c = lax.broadcasted_iota(jnp.int32, (ch, LANES), 1) < HEAD_DIM

            def fin(c, carry):
                r0 = pl.multiple_of(c * ch, ch)
                cs = cos_ref[ts, pl.ds(r0, ch), :]
                sn = sin_ref[ts, pl.ds(r0, ch), :]
                for j in range(N_KV // 2):
                    both = []
                    for acc_ref in (dkx_ref, dvx_ref):
                        t0 = acc_ref[2 * j, pl.ds(r0, ch), :]
                        t1 = acc_ref[2 * j + 1, pl.ds(r0, ch), :]
                        both.append(jnp.where(lo_c, t0, t1) + pltpu.roll(jnp.where(lo_c, t1, t0), HEAD_DIM, 1))
                    kcol = N_HEADS * HEAD_DIM + j * LANES
                    vcol = (N_HEADS + N_KV) * HEAD_DIM + j * LANES
                    dqkv_ref[s, pl.ds(r0, ch), kcol:kcol + LANES] = _rope_t(both[0] * LN2, cs, sn).astype(BF16)
                    dqkv_ref[s, pl.ds(r0, ch), vcol:vcol + LANES] = both[1].astype(BF16)
                return carry

            lax.fori_loop(0, length // ch, fin, 0)

    seq_map = lambda n: (n, 0, 0)
    tab_map = (lambda n: (n % tab_blocks, 0, 0)) if dil >= seq_blk else (lambda n: (0, 0, 0))
    tab_rows = min(seq_blk, dil)
    in_specs = [pl.BlockSpec((seq_blk, length, N_HEADS * HEAD_DIM), seq_map),
                pl.BlockSpec((seq_blk, length, N_KV * HEAD_DIM), lambda n: (n, 0, 4)),
                pl.BlockSpec((seq_blk, length, N_KV * HEAD_DIM), lambda n: (n, 0, 5)),
                pl.BlockSpec((seq_blk, length, D_MODEL), seq_map),
                pl.BlockSpec((seq_blk, N_HEADS, length), seq_map),
                pl.BlockSpec((seq_blk, N_HEADS, length), seq_map),
                pl.BlockSpec((tab_rows, length, LANES), tab_map),
                pl.BlockSpec((tab_rows, length, LANES), tab_map)]
    args = [qkv3, qkv3, qkv3, do3, adj, lse] + tabs
    if with_sink:
        in_specs.insert(0, pl.BlockSpec(memory_space=pltpu.SMEM))
        args.insert(0, sink)
    out_specs = [pl.BlockSpec((seq_blk, length, QKV_W), seq_map)]
    out_shape = [jax.ShapeDtypeStruct((n_seq, length, QKV_W), BF16)]
    if with_sink:
        out_specs.append(pl.BlockSpec((8, LANES), lambda n: (0, 0)))
        out_shape.append(jax.ShapeDtypeStruct((8, LANES), F32))
    outs = pl.pallas_call(
        body, name=name, grid=(n_seq // seq_blk,), in_specs=in_specs, out_specs=out_specs, out_shape=out_shape,
        scratch_shapes=[pltpu.VMEM((N_KV, length, LANES), BF16), pltpu.VMEM((N_KV, length, LANES), BF16),
                        pltpu.VMEM((N_KV, length, LANES), F32), pltpu.VMEM((N_KV, length, LANES), F32)],
        compiler_params=_cp(),
    )(*args)
    dqkv = outs[0].reshape(n_seq * length, QKV_W)
    return (dqkv, outs[1]) if with_sink else (dqkv, None)


def _head_expander():
    h = jnp.arange(LANES)[:, None]
    l = jnp.arange(D_MODEL)[None, :]
    return (l // HEAD_DIM == h).astype(BF16)


def _dot_split(a, e):
    hi = a.astype(BF16)
    lo = (a - hi.astype(F32)).astype(BF16)
    return jnp.dot(hi, e, preferred_element_type=F32) + jnp.dot(lo, e, preferred_element_type=F32)


def _dot_heads(a, e):
    return jnp.dot(a.astype(BF16), e, preferred_element_type=F32)


def _mix_weights(lses):
    m = jnp.maximum(jnp.maximum(lses[0], lses[1]), lses[2])
    es = [jnp.exp(v - m) for v in lses]
    tot = es[0] + es[1] + es[2]
    return [e / tot for e in es]


def _mix_fwd(os_, lses, name):
    t = os_[0].shape[0]
    tm = _row_tile(t, ROWS)

    def body(o0, o1, o2, l0, l1, l2, e_ref, out_ref):
        wts = _mix_weights([l0[...], l1[...], l2[...]])
        acc = jnp.zeros((tm, D_MODEL), F32)
        for w, o_ref in zip(wts, (o0, o1, o2)):
            acc = acc + _dot_split(w, e_ref[...]) * o_ref[...]
        out_ref[...] = acc.astype(BF16)

    row = pl.BlockSpec((tm, D_MODEL), lambda i: (i, 0))
    lrow = pl.BlockSpec((tm, LANES), lambda i: (i, 0))
    return pl.pallas_call(
        body, name=name, grid=(t // tm,),
        in_specs=[row] * 3 + [lrow] * 3 + [pl.BlockSpec((LANES, D_MODEL), lambda i: (0, 0))],
        out_specs=row, out_shape=jax.ShapeDtypeStruct((t, D_MODEL), BF16), compiler_params=_cp(),
    )(*os_, *lses, _head_expander())


def _mix_bwd(dx, w_out, os_, lses, name):
    t = dx.shape[0]
    tm = _row_tile(t, ROWS)

    def body(d_ref, w_ref, o0, o1, o2, l0, l1, l2, e_ref, et_ref, do0, do1, do2, a0, a1, a2):
        wts = _mix_weights([l0[...], l1[...], l2[...]])
        dv = lax.dot_general(d_ref[...], w_ref[...], (((1,), (1,)), ((), ())), preferred_element_type=F32)
        cs = [_dot_heads(dv * o_ref[...], et_ref[...]) for o_ref in (o0, o1, o2)]
        mean_c = wts[0] * cs[0] + wts[1] * cs[1] + wts[2] * cs[2]
        for w, c, do_ref, a_ref in zip(wts, cs, (do0, do1, do2), (a0, a1, a2)):
            do_ref[...] = (_dot_heads(w, e_ref[...]) * dv).astype(BF16)
            a_ref[...] = w * (c - mean_c) - w * c

    row = pl.BlockSpec((tm, D_MODEL), lambda i: (i, 0))
    lrow = pl.BlockSpec((tm, LANES), lambda i: (i, 0))
    e = _head_expander()
    return pl.pallas_call(
        body, name=name, grid=(t // tm,),
        in_specs=[row, pl.BlockSpec((D_MODEL, D_MODEL), lambda i: (0, 0), pipeline_mode=pl.Buffered(1))]
        + [row] * 3 + [lrow] * 3 + [pl.BlockSpec((LANES, D_MODEL), lambda i: (0, 0)),
                                    pl.BlockSpec((D_MODEL, LANES), lambda i: (0, 0))],
        out_specs=[row] * 3 + [lrow] * 3,
        out_shape=[jax.ShapeDtypeStruct((t, D_MODEL), BF16)] * 3 + [jax.ShapeDtypeStruct((t, LANES), F32)] * 3,
        compiler_params=_cp(),
    )(dx, w_out, *os_, *lses, e, e.T)


def _stats_to_tokens(stat, batch, dil):
    n_seq, _, length = stat.shape
    t = stat.transpose(0, 2, 1).reshape(n_seq * length, N_HEADS)
    return _from_residue(jnp.pad(t, ((0, 0), (0, LANES - N_HEADS))), batch, dil)


def _stats_from_tokens(stat, batch, dil, n_seq, length):
    t = _to_residue(stat[:, :N_HEADS], batch, dil)
    return t.reshape(n_seq, length, N_HEADS).transpose(0, 2, 1)


def _group_geometry(batch, seq, dil, window):
    length = seq // dil
    n_seq = batch * dil
    seq_blk = max(1, min(dil, 1024 // length))
    return n_seq, length, (window // 2) // dil, seq_blk


def _local_step(x, target, a_in, a_sink, a_out, b_in, b_out, norm_mix, norm_ffn, wg, wu, wd, final_norm):
    batch, seq, _ = x.shape
    t = batch * seq
    x0 = x.reshape(t, D_MODEL)
    tgt = target.reshape(t, D_MODEL)
    tabs = {d: _rope_tables(seq, d) for _, d in DILATED}
    nm = [norm_mix[i:i + 1] for i in range(2)]
    nf = [norm_ffn[i:i + 1] for i in range(2)]

    h0 = _rms_fwd(x0, nm[0], "rms_mix0")
    qkv0 = _qkv_proj(h0, a_in, *tabs[1], 0, "qkv0")
    o0, lse0 = _attn_fwd(qkv0, a_sink, batch, seq, HALF_WINDOW_A, 1, BF16, "attn0")
    x1, hf0 = _mm_res(o0, a_out, x0, nf[0], "out0")
    act0, g0, u0 = _ffn_up(hf0, wg[0], wu[0], 0, "ffn_up0")
    x2, h1 = _ffn_down(act0, wd[0], x1, 0, "ffn_down0", norm_w=nm[1])

    geo = [_group_geometry(batch, seq, d, w) for w, d in DILATED]
    h1g, qkv1, o1, lse1, lse1r = [], [], [], [], []
    for gi, (_, d) in enumerate(DILATED):
        n_seq, length, hw, sb = geo[gi]
        hp = _to_residue(h1, batch, d)
        pj = _qkv_proj(hp, b_in, *tabs[d], gi, f"qkv1_{gi}")
        o, lse = _attn_fwd(pj, None, n_seq, length, hw, sb, BF16, f"attn1_{gi}")
        h1g.append(hp)
        qkv1.append(pj)
        o1.append(_from_residue(o, batch, d))
        lse1r.append(lse)
        lse1.append(_stats_to_tokens(lse, batch, d))
    omix = _mix_fwd(o1, lse1, "mix")
    x3, hf1 = _mm_res(omix, b_out, x2, nf[1], "out1")
    act1, g1, u1 = _ffn_up(hf1, wg[1], wu[1], 0, "ffn_up1")
    dx4, dx4b, loss_cols, d_final = _ffn_down(act1, wd[1], x3, 0, "ffn_down1_loss",
                                                     head=(final_norm.reshape(1, D_MODEL), tgt))

    def ffn_bwd(dxo, dxob, x_mid, hf, g, u, act, layer):
        dg, du, dxm, dxmb, d_nf = _ffn_bwd(dxob, wd[layer], wg[layer], wu[layer], g, u, x_mid, nf[layer], dxo,
                                           f"ffn_bwd{layer}")
        (d_wd,) = _mm_tn(act, [dxob], f"grad_wd{layer}")
        (d_wgt,) = _mm_tn(dg, [hf], f"grad_wg{layer}")
        (d_wut,) = _mm_tn(du, [hf], f"grad_wu{layer}")
        return dxm, dxmb, d_nf, d_wgt, d_wut, d_wd

    dx3, dx3b, d_nf1, d_wg1, d_wu1, d_wd1 = ffn_bwd(dx4, dx4b, x3, hf1, g1, u1, act1, 1)

    (d_b_out,) = _mm_tn(omix, [dx3b], "grad_b_out")
    mb = _mix_bwd(dx3b, b_out, o1, lse1, "out1_mix_bwd")
    dh1, d_b_in = [], []
    for gi, (_, d) in enumerate(DILATED):
        n_seq, length, hw, sb = geo[gi]
        dog = _to_residue(mb[gi], batch, d)
        adj = _stats_from_tokens(mb[3 + gi], batch, d, n_seq, length)
        dpj, _ = _attn_bwd(qkv1[gi], dog, adj, lse1r[gi], None, *tabs[d], n_seq, length, hw, sb, d, f"attn1_bwd{gi}")
        (dw,) = _mm_tn(h1g[gi], [dpj], f"grad_b_in{gi}")
        d_b_in.append(dw)
        dh1.append(_from_residue(_mm_nt(dpj, b_in, gi, BF16, f"qkv1_bwd{gi}"), batch, d))
    dx2, dx2b, d_nm1 = _rms_bwd(x2, nm[1], dh1, dx3, "rms_mix_bwd1")

    dx1, dx1b, d_nf0, d_wg0, d_wu0, d_wd0 = ffn_bwd(dx2, dx2b, x1, hf0, g0, u0, act0, 0)

    do0, adj0 = _out_bwd(dx1b, a_out, o0, "out0_bwd")
    (d_a_out,) = _mm_tn(o0, [dx1b], "grad_a_out")
    adj0 = _stats_from_tokens(adj0, batch, 1, batch, seq)
    dqkv0, d_sink = _attn_bwd(qkv0, do0, adj0, lse0, a_sink, *tabs[1], batch, seq, HALF_WINDOW_A, 1, 1, "attn0_bwd")
    (d_a_in,) = _mm_tn(h0, [dqkv0], "grad_a_in")
    gx, d_nm0 = _mm_nt_rms(dqkv0, a_in, x0, nm[0], dx1, "qkv0_bwd")

    grads = dict(a_in=d_a_in, a_out=d_a_out, b_in=jnp.concatenate(d_b_in, axis=1), b_out=d_b_out,
                 wg=(d_wg0, d_wg1), wu=(d_wu0, d_wu1), wd=(d_wd0, d_wd1))
    vecs = dict(norm_mix=(d_nm0, d_nm1), norm_ffn=(d_nf0, d_nf1), final=d_final, loss_cols=loss_cols, sink=d_sink)
    return gx.reshape(x.shape), grads, vecs


ANY = pl.BlockSpec(memory_space=pl.ANY)
HBM = pltpu.MemorySpace.HBM


def _me():
    return lax.axis_index("x"), lax.axis_index("y"), lax.axis_index("c")


def _chip_peer(x, y, j):
    px = 1 - x if j & 2 else x
    py = 1 - y if j & 1 else y
    return px, py, 2 * px + py


def _remote(src, dst, sems, k, dev):
    return pltpu.make_async_remote_copy(src_ref=src, dst_ref=dst, send_sem=sems[0].at[k], recv_sem=sems[1].at[k],
                                        device_id=dev, device_id_type=MESH)


def _col_window(ref, q, width):
    return ref.at[:, pl.ds(pl.multiple_of(q * width, LANES), width)]


def _half0(ref, h):
    n = ref.shape[0] // 2
    return ref.at[pl.ds(h * n, n)]


def _half1(ref, h):
    n = ref.shape[1] // 2
    return ref.at[:, pl.ds(h * n, n)]


def _half_rows(ref, h):
    n = ref.shape[-2] // 2
    if len(ref.shape) == 2:
        return ref.at[pl.ds(h * n, n)]
    return ref.at[:, pl.ds(h * n, n)]


def _place_shard(w, layer, q_arr, col, name):
    _, rows, cols = w.shape

    def body(q_ref, w_ref, o_ref):
        o_ref[...] = w_ref[...].astype(BF16)

    if col:
        out_spec = pl.BlockSpec((rows, cols), lambda l, q: (0, q[0]))
        out_shape = jax.ShapeDtypeStruct((rows, N_CHIPS * cols), BF16)
    else:
        out_spec = pl.BlockSpec((None, None, rows, cols), lambda l, q: (q[0], 0, 0, 0))
        out_shape = jax.ShapeDtypeStruct((N_CHIPS, 1, rows, cols), BF16)
    return pl.pallas_call(
        body, name=name,
        grid_spec=pltpu.PrefetchScalarGridSpec(
            num_scalar_prefetch=1, grid=(1,),
            in_specs=[pl.BlockSpec((None, rows, cols), lambda l, q: (layer, 0, 0))], out_specs=out_spec),
        out_shape=out_shape, compiler_params=_cp(),
    )(q_arr, w)


def _handshake(peers):
    barrier = pltpu.get_barrier_semaphore()
    for p in peers:
        pl.semaphore_signal(barrier, inc=1, device_id=p, device_id_type=MESH)
    pl.semaphore_wait(barrier, len(peers))


def _on_sequencer(name, collective_id, n_sem, n_local, body):
    @pl.kernel(mesh=plsc.ScalarSubcoreMesh(axis_name="seq", num_cores=1), name=name,
               scratch_types=(pltpu.SemaphoreType.DMA((n_sem,)), pltpu.SemaphoreType.DMA((n_sem,)),
                              pltpu.SemaphoreType.DMA((max(n_local, 1),))),
               compiler_params=pltpu.CompilerParams(collective_id=collective_id))
    def launch(send_sems, recv_sems, local_sems):
        body((send_sems, recv_sems), local_sems)

    launch()


def _gather_plan(outs, col_fam, sems, handshake):
    n_w = len(outs)
    x, y, c = _me()
    myq = 2 * x + y
    sib = (x, y, 1 - c)
    if handshake:
        _handshake([sib] + [_chip_peer(x, y, j)[:2] + (c,) for j in (1, 2, 3)])

    def slot(w, q):
        if col_fam[w]:
            return _col_window(outs[w], q, outs[w].shape[1] // N_CHIPS)
        return outs[w].at[q]

    first = []
    for w in range(n_w):
        for j in (1, 2, 3):
            px, py, _ = _chip_peer(x, y, j)
            mine = _half_rows(slot(w, myq), c)
            cp = _remote(mine, mine, sems, w * 6 + j - 1, (px, py, c))
            cp.start()
            first.append(cp)
    passed = []
    for w in range(n_w):
        for j in (1, 2, 3):
            _, _, pq = _chip_peer(x, y, j)
            land = _half_rows(slot(w, pq), c)
            _remote(land, land, sems, w * 6 + j - 1, sib).wait_recv()
            cp = _remote(land, land, sems, w * 6 + 2 + j, sib)
            cp.start()
            passed.append(cp)
    for w in range(n_w):
        for j in (1, 2, 3):
            _, _, pq = _chip_peer(x, y, j)
            land = _half_rows(slot(w, pq), 1 - c)
            _remote(land, land, sems, w * 6 + 2 + j, sib).wait_recv()
    for cp in first + passed:
        cp.wait_send()


def _gather_weights(bufs, col_fam):
    n_w = len(bufs)

    def body(*refs):
        _gather_plan(refs[n_w:2 * n_w], col_fam, refs[2 * n_w:2 * n_w + 2], False)

    return pl.pallas_call(
        body, name="gather_weights", in_specs=[ANY] * n_w, out_specs=[ANY] * n_w,
        out_shape=[jax.ShapeDtypeStruct(b.shape, b.dtype) for b in bufs],
        input_output_aliases={w: w for w in range(n_w)},
        scratch_shapes=[pltpu.SemaphoreType.DMA((6 * n_w,)), pltpu.SemaphoreType.DMA((6 * n_w,))],
    )(*bufs)


def _gather_weights_async(bufs, col_fam, name, collective_id):
    refs = [jax.new_ref(b, memory_space=HBM) for b in bufs]
    _on_sequencer(name, collective_id, 6 * len(bufs), 0,
                  lambda sems, _: _gather_plan(refs, col_fam, sems, True))
    return [r[...] for r in refs]


def _grad_half(ref, col, h):
    return _half0(ref, h) if col else _half1(ref, h)


def _swap_halves_with_sibling(grads, col_fam):
    n_w = len(grads)

    def body(*refs):
        _swap_plan(refs[:n_w], refs[n_w:2 * n_w], col_fam, refs[2 * n_w:], False)

    return pl.pallas_call(
        body, name="grad_swap_sibling", in_specs=[ANY] * n_w, out_specs=[ANY] * n_w,
        out_shape=_swap_shapes(grads, col_fam),
        scratch_shapes=[pltpu.SemaphoreType.DMA((n_w,)), pltpu.SemaphoreType.DMA((n_w,))],
    )(*grads)


def _swap_shapes(grads, col_fam):
    out = []
    for w, g in enumerate(grads):
        shp = (g.shape[0] // 2, g.shape[1]) if col_fam[w] else (g.shape[0], g.shape[1] // 2, g.shape[2])
        out.append(jax.ShapeDtypeStruct(shp, g.dtype))
    return out


def _swap_plan(ins, outs, col_fam, sems, handshake):
    x, y, c = _me()
    sib = (x, y, 1 - c)
    if handshake:
        _handshake([sib])
    cps = [_remote(_grad_half(ins[w], col_fam[w], 1 - c), outs[w], sems, w, sib) for w in range(len(ins))]
    for cp in cps:
        cp.start()
    for cp in cps:
        cp.wait_recv()
    for cp in cps:
        cp.wait_send()


def _swap_halves_async(grads, col_fam, name, collective_id):
    srcs = [jax.new_ref(g, memory_space=HBM) for g in grads]
    dsts = [jax.empty_ref(s, memory_space=HBM) for s in _swap_shapes(grads, col_fam)]
    _on_sequencer(name, collective_id, len(grads), 0, lambda sems, _: _swap_plan(srcs, dsts, col_fam, sems, True))
    return [r[...] for r in srcs], [r[...] for r in dsts]


def _half_add(mines, recvs, c_arr, col_fam, name):
    n_w = len(mines)
    mine_specs, recv_specs = [], []
    for recv, col in zip(recvs, col_fam):
        if col:
            rows, n = recv.shape
            tr = rows // N_CHIPS
            mine_specs.append(pl.BlockSpec((tr, n), lambda i, c: (N_CHIPS * c[0] + i, 0)))
            recv_specs.append(pl.BlockSpec((tr, n), lambda i, c: (i, 0)))
        else:
            _, rows, n = recv.shape
            mine_specs.append(pl.BlockSpec((None, rows, n), lambda q, c: (q, c[0], 0)))
            recv_specs.append(pl.BlockSpec((None, rows, n), lambda q, c: (q, 0, 0)))

    def body(c_ref, *refs):
        for a_ref, b_ref, o_ref in zip(refs[:n_w], refs[n_w:2 * n_w], refs[2 * n_w:]):
            o_ref[...] = (a_ref[...].astype(F32) + b_ref[...].astype(F32)).astype(BF16)

    return pl.pallas_call(
        body, name=name,
        grid_spec=pltpu.PrefetchScalarGridSpec(num_scalar_prefetch=1, grid=(N_CHIPS,),
                                               in_specs=mine_specs + recv_specs, out_specs=recv_specs),
        out_shape=[jax.ShapeDtypeStruct(r.shape, BF16) for r in recvs], compiler_params=_cp(),
    )(c_arr, *mines, *recvs)


def _scatter_chip_sums(sums, col_fam):
    n_w = len(sums)

    def body(*refs):
        _scatter_plan(refs[:n_w], refs[n_w:2 * n_w], col_fam, refs[2 * n_w:2 * n_w + 2], refs[2 * n_w + 2], False)

    return pl.pallas_call(
        body, name="grad_scatter_chips", in_specs=[ANY] * n_w, out_specs=[ANY] * n_w,
        out_shape=_scatter_shapes(sums, col_fam),
        scratch_shapes=[pltpu.SemaphoreType.DMA((3 * n_w,)), pltpu.SemaphoreType.DMA((3 * n_w,)),
                        pltpu.SemaphoreType.DMA((n_w,))],
    )(*sums)


def _scatter_shapes(sums, col_fam):
    out = []
    for w, s in enumerate(sums):
        shp = (s.shape[0], s.shape[1] // N_CHIPS) if col_fam[w] else s.shape[1:]
        out.append(jax.ShapeDtypeStruct((N_CHIPS,) + shp, s.dtype))
    return out


def _scatter_plan(ins, outs, col_fam, sems, lsem, handshake):
    n_w = len(ins)
    x, y, c = _me()
    myq = 2 * x + y
    if handshake:
        _handshake([_chip_peer(x, y, j)[:2] + (c,) for j in (1, 2, 3)])

    def slab(w, q):
        if col_fam[w]:
            return _col_window(ins[w], q, ins[w].shape[1] // N_CHIPS)
        return ins[w].at[q]

    local = [pltpu.make_async_copy(slab(w, myq), outs[w].at[myq], lsem.at[w]) for w in range(n_w)]
    for cp in local:
        cp.start()
    cps = []
    for w in range(n_w):
        for j in (1, 2, 3):
            px, py, pq = _chip_peer(x, y, j)
            cp = _remote(slab(w, pq), outs[w].at[myq], sems, w * 3 + j - 1, (px, py, c))
            cp.start()
            cps.append(cp)
    for w in range(n_w):
        for j in (1, 2, 3):
            _, _, pq = _chip_peer(x, y, j)
            land = outs[w].at[pq]
            _remote(land, land, sems, w * 3 + j - 1, (x, y, c)).wait_recv()
    for cp in cps:
        cp.wait_send()
    for cp in local:
        cp.wait()


def _scatter_chip_sums_async(sums, col_fam, name, collective_id):
    srcs = [jax.new_ref(s, memory_space=HBM) for s in sums]
    dsts = [jax.empty_ref(s, memory_space=HBM) for s in _scatter_shapes(sums, col_fam)]
    _on_sequencer(name, collective_id, 3 * len(sums), len(sums),
                  lambda sems, lsem: _scatter_plan(srcs, dsts, col_fam, sems, lsem, True))
    return [r[...] for r in dsts]


def _sum_chips(parts, c_arr, prev, lead, shape, name):
    _, rows, n = parts.shape
    tr = rows // 2 if rows % 32 == 0 else rows
    nblk = rows // tr

    def body(c_ref, p_ref, *rest):
        o_ref = rest[-1]
        acc = p_ref[0].astype(F32)
        for q in range(1, N_CHIPS):
            acc = acc + p_ref[q].astype(F32)
        o_ref[...] = acc

    in_specs = [pl.BlockSpec((N_CHIPS, tr, n), lambda i, c: (0, i, 0))]
    args = [c_arr, parts]
    aliases = {}
    if prev is not None:
        in_specs.append(ANY)
        args.append(prev)
        aliases = {2: 0}
    return pl.pallas_call(
        body, name=name,
        grid_spec=pltpu.PrefetchScalarGridSpec(
            num_scalar_prefetch=1, grid=(nblk,), in_specs=in_specs,
            out_specs=pl.BlockSpec((None, tr, n), lambda i, c: (lead, c[0] * nblk + i, 0))),
        out_shape=jax.ShapeDtypeStruct(shape, F32), input_output_aliases=aliases, compiler_params=_cp(),
    )(*args)


def _join_plan(outs, place, sems, handshake):
    x, y, c = _me()
    sib = (x, y, 1 - c)
    if handshake:
        _handshake([sib])

    def half(k, h):
        o, lead = place[k]
        return _half_rows(outs[o].at[lead], h)

    cps = [_remote(half(k, c), half(k, c), sems, k, sib) for k in range(len(place))]
    for cp in cps:
        cp.start()
    for k in range(len(place)):
        land = half(k, 1 - c)
        _remote(land, land, sems, k, sib).wait_recv()
    for cp in cps:
        cp.wait_send()


def _join_halves(bufs, place, name):
    n_o = len(bufs)
    n_h = len(place)

    def body(*refs):
        _join_plan(refs[n_o:2 * n_o], place, refs[2 * n_o:2 * n_o + 2], False)

    return pl.pallas_call(
        body, name=name, in_specs=[ANY] * n_o, out_specs=[ANY] * n_o,
        out_shape=[jax.ShapeDtypeStruct(b.shape, b.dtype) for b in bufs],
        input_output_aliases={k: k for k in range(n_o)},
        scratch_shapes=[pltpu.SemaphoreType.DMA((n_h,)), pltpu.SemaphoreType.DMA((n_h,))],
    )(*bufs)


def _allreduce_rows(rows):
    n_dev = 8
    n_r = len(rows)
    assert n_r <= 8

    def body(*refs):
        r_refs = refs[:n_r]
        o_ref, slots, send_sems, recv_sems = refs[n_r:]
        x, y, c = _me()
        me = 4 * x + 2 * y + c
        slots[me] = jnp.concatenate([r[...] for r in r_refs] + [jnp.zeros((8 - n_r, D_MODEL), F32)], axis=0)

        def peer(k):
            return (1 - x if k & 4 else x, 1 - y if k & 2 else y, 1 - c if k & 1 else c)

        cps = []
        for k in range(1, n_dev):
            cp = pltpu.make_async_remote_copy(src_ref=slots.at[me], dst_ref=slots.at[me], send_sem=send_sems.at[k - 1],
                                              recv_sem=recv_sems.at[k - 1], device_id=peer(k), device_id_type=MESH)
            cp.start()
            cps.append(cp)
        for k in range(1, n_dev):
            px, py, pc = peer(k)
            land = slots.at[4 * px + 2 * py + pc]
            pltpu.make_async_remote_copy(src_ref=land, dst_ref=land, send_sem=send_sems.at[k - 1],
                                         recv_sem=recv_sems.at[k - 1], device_id=peer(k),
                                         device_id_type=MESH).wait_recv()
        for cp in cps:
            cp.wait_send()
        acc = slots[0]
        for d in range(1, n_dev):
            acc = acc + slots[d]
        o_ref[...] = acc

    vm = pl.BlockSpec(memory_space=pltpu.VMEM)
    return pl.pallas_call(
        body, name="allreduce_rows", in_specs=[vm] * n_r, out_specs=vm,
        out_shape=jax.ShapeDtypeStruct((8, D_MODEL), F32),
        scratch_shapes=[pltpu.VMEM((n_dev, 8, D_MODEL), F32), pltpu.SemaphoreType.DMA((n_dev - 1,)),
                        pltpu.SemaphoreType.DMA((n_dev - 1,))],
    )(*rows)


def _adamw(w, g, m, v, name):
    shape = w.shape
    if len(shape) == 1:
        lead, rows, cols = 1, 1, shape[0]
    else:
        rows, cols = shape[-2:]
        lead = math.prod(shape[:-2])
    args = [a.reshape(lead, rows, cols) for a in (w, g, m, v)]
    tr = rows // 2 if rows % 16 == 0 else rows

    def body(w_ref, g_ref, m_ref, v_ref, d_ref, nm_ref, nv_ref):
        gv = g_ref[...]
        nm = ADAM_B1 * m_ref[...] + (1.0 - ADAM_B1) * gv
        nv = ADAM_B2 * v_ref[...] + (1.0 - ADAM_B2) * jnp.square(gv)
        m_hat = nm / (1.0 - ADAM_B1 ** ADAM_STEP)
        v_hat = nv / (1.0 - ADAM_B2 ** ADAM_STEP)
        d_ref[...] = -ADAM_LR * (m_hat / (jnp.sqrt(v_hat) + ADAM_EPS) + ADAM_WD * w_ref[...])
        nm_ref[...] = nm
        nv_ref[...] = nv

    spec = pl.BlockSpec((None, tr, cols), lambda l, i: (l, i, 0))
    outs = pl.pallas_call(
        body, name=name, grid=(lead, rows // tr), in_specs=[spec] * 4, out_specs=[spec] * 3,
        out_shape=[jax.ShapeDtypeStruct((lead, rows, cols), F32)] * 3, compiler_params=_cp(),
    )(*args)
    return [o.reshape(shape) for o in outs]


def kernel(x, a_w_in, a_sink, a_w_out, b_w_in, b_w_out, norm_mix, norm_ffn, w_gate, w_up, w_down, final_norm, loss_target, m_a_w_in, m_a_sink, m_a_w_out, m_b_w_in, m_b_w_out, m_norm_mix, m_norm_ffn, m_w_gate, m_w_up, m_w_down, m_final_norm, v_a_w_in, v_a_sink, v_a_w_out, v_b_w_in, v_b_w_out, v_norm_mix, v_norm_ffn, v_w_gate, v_w_up, v_w_down, v_final_norm):
    weights = dict(a_w_in=a_w_in, a_sink=a_sink, a_w_out=a_w_out, b_w_in=b_w_in, b_w_out=b_w_out, norm_mix=norm_mix,
                   norm_ffn=norm_ffn, w_gate=w_gate, w_up=w_up, w_down=w_down, final_norm=final_norm)
    mom = dict(a_w_in=m_a_w_in, a_sink=m_a_sink, a_w_out=m_a_w_out, b_w_in=m_b_w_in, b_w_out=m_b_w_out,
               norm_mix=m_norm_mix, norm_ffn=m_norm_ffn, w_gate=m_w_gate, w_up=m_w_up, w_down=m_w_down,
               final_norm=m_final_norm)
    var = dict(a_w_in=v_a_w_in, a_sink=v_a_sink, a_w_out=v_a_w_out, b_w_in=v_b_w_in, b_w_out=v_b_w_out,
               norm_mix=v_norm_mix, norm_ffn=v_norm_ffn, w_gate=v_w_gate, w_up=v_w_up, w_down=v_w_down,
               final_norm=v_final_norm)
    order = ["a_w_in", "a_sink", "a_w_out", "b_w_in", "b_w_out", "norm_mix", "norm_ffn", "w_gate", "w_up", "w_down",
             "final_norm"]
    swapped = ("w_gate", "w_up")
    for n in swapped:
        weights[n], mom[n], var[n] = (a.transpose(0, 2, 1) for a in (weights[n], mom[n], var[n]))
    w_gate_t, w_up_t = weights["w_gate"], weights["w_up"]

    c_arr = lax.axis_index("c").astype(jnp.int32).reshape(1)
    q_arr = (2 * lax.axis_index("x") + lax.axis_index("y")).astype(jnp.int32).reshape(1)

    def placed(w, layer, col, nm):
        return _place_shard(w, layer, q_arr, col, f"place_{nm}")

    (a_in,) = _gather_weights_async([placed(a_w_in, 0, True, "a_in")], (True,), "gather_weights_first", 6)
    a_out, wg0, wu0, wd0 = _gather_weights_async(
        [placed(a_w_out, 0, False, "a_out"), placed(w_gate_t, 0, False, "wg0"), placed(w_up_t, 0, False, "wu0"),
         placed(w_down, 0, False, "wd0")], (False,) * 4, "gather_weights_layer0", 1)
    b_in, b_out, wg1, wu1, wd1 = _gather_weights_async(
        [placed(b_w_in, 0, True, "b_in"), placed(b_w_out, 0, False, "b_out"), placed(w_gate_t, 1, False, "wg1"),
         placed(w_up_t, 1, False, "wu1"), placed(w_down, 1, False, "wd1")], (True,) + (False,) * 4,
        "gather_weights_layer1", 7)
    a_out = a_out.reshape(D_MODEL, D_MODEL)
    b_out = b_out.reshape(D_MODEL, D_MODEL)
    wg, wu, wd = (wg0, wg1), (wu0, wu1), (wd0, wd1)

    gx, grads, vecs = _local_step(x, loss_target, a_in, a_sink[0], a_out, b_in, b_out, norm_mix, norm_ffn, wg, wu, wd,
                                  final_norm)

    rows_out = D_MODEL // N_CHIPS
    partials = [grads["a_in"], grads["b_in"],
                grads["a_out"].reshape(N_CHIPS, rows_out, D_MODEL), grads["b_out"].reshape(N_CHIPS, rows_out, D_MODEL),
                grads["wg"][0], grads["wg"][1], grads["wu"][0], grads["wu"][1], grads["wd"][0], grads["wd"][1]]
    col_fam = (True, True) + (False,) * 8
    names = ("a_in", "b_in", "a_out", "b_out", "wg0", "wg1", "wu0", "wu1", "wd0", "wd1")
    contrib = [None] * len(partials)

    def reduce_group(idx, tag, ids):
        parts = [partials[k] for k in idx]
        cols = tuple(col_fam[k] for k in idx)
        if ids is None:
            theirs = _swap_halves_with_sibling(parts, cols)
        else:
            parts, theirs = _swap_halves_async(parts, cols, f"grad_swap_{tag}", ids[0])
        sums = _half_add(parts, theirs, c_arr, cols, f"chip_sum_{tag}")
        if ids is None:
            out = _scatter_chip_sums(sums, cols)
        else:
            out = _scatter_chip_sums_async(sums, cols, f"grad_scatter_{tag}", ids[1])
        for k, o in zip(idx, out):
            contrib[k] = o

    reduce_group([1, 3, 5, 7, 9], "layer1", (2, 3))
    reduce_group([2, 4, 6, 8], "ffn0", (4, 5))
    reduce_group([0], "a_in", None)
    shapes = [a_w_in.shape, b_w_in.shape, a_w_out.shape, b_w_out.shape, w_down.shape, w_down.shape, w_down.shape]
    place = [(0, 0), (1, 0), (2, 0), (3, 0), (4, 0), (4, 1), (5, 0), (5, 1), (6, 0), (6, 1)]
    bufs = [None] * len(shapes)
    for p, nm, (o, lead) in zip(contrib, names, place):
        bufs[o] = _sum_chips(p, c_arr, bufs[o], lead, shapes[o], f"sum_chips_{nm}")
    g_a_in, g_b_in, g_a_out, g_b_out, g_wg, g_wu, g_wd = _join_halves(bufs, place, "grad_join_sibling")

    sink_row = jnp.pad(vecs["sink"][0:1], ((0, 0), (0, D_MODEL - LANES)))
    tot = _allreduce_rows([vecs["norm_mix"][0], vecs["norm_mix"][1], vecs["norm_ffn"][0], vecs["norm_ffn"][1],
                           vecs["final"], vecs["loss_cols"], sink_row])
    loss = (0.5 / D_MODEL) * jnp.sum(tot[5])
    gw = dict(a_w_in=g_a_in, a_sink=tot[6:7, :N_HEADS], a_w_out=g_a_out, b_w_in=g_b_in, b_w_out=g_b_out,
              norm_mix=tot[0:2], norm_ffn=tot[2:4], w_gate=g_wg, w_up=g_wu, w_down=g_wd, final_norm=tot[4])

    delta, new_m, new_v = {}, {}, {}
    for n in order:
        delta[n], new_m[n], new_v[n] = _adamw(weights[n], gw[n], mom[n], var[n], f"adamw_{n}")
    for n in swapped:
        gw[n], delta[n], new_m[n], new_v[n] = (a.transpose(0, 2, 1) for a in (gw[n], delta[n], new_m[n], new_v[n]))
    return (loss, gx, *[gw[n] for n in order], *[delta[n] for n in order], *[new_m[n] for n in order],
            *[new_v[n] for n in order])
```

```python
import math

import jax
import jax.numpy as jnp
from jax import lax
from jax.experimental import pallas as pl
from jax.experimental.pallas import tpu as pltpu
from jax.experimental.pallas import tpu_sc as plsc

F32 = jnp.float32
BF16 = jnp.bfloat16

D_MODEL = 1024
HEAD_DIM = 64
N_HEADS = 16
N_KV = 4
QKV_W = 1536
D_FF = 2816
N_CHIPS = 4
FF_SH = D_FF // N_CHIPS
HALF_WINDOW_A = 128
DILATED = ((128, 1), (512, 4), (2048, 16))
ROPE_THETA = 10000.0
RMS_EPS = 1e-6
NEG_INF = -1e30
LANES = 128
ADAM_LR, ADAM_B1, ADAM_B2, ADAM_EPS, ADAM_WD, ADAM_STEP = 0.001, 0.9, 0.999, 1e-08, 0.01, 10
VMEM_LIMIT = 56 * 1024 * 1024
ROWS = 512
MATMUL_ROWS = 1024
FFN_BWD_ROWS = 256
LOG2E = math.log2(math.e)
LN2 = math.log(2.0)
Q_SCALE = LOG2E / math.sqrt(HEAD_DIM)
GRAD_TOKENS = 2048
MESH = pl.DeviceIdType.MESH


def _cp(**kw):
    return pltpu.CompilerParams(vmem_limit_bytes=VMEM_LIMIT, **kw)


def _row_tile(t, cap):
    tm = min(cap, t)
    assert t % tm == 0
    return tm


def _rope_tables(seq, dil):
    inv = 1.0 / (ROPE_THETA ** (jnp.arange(0, HEAD_DIM, 2, dtype=F32) / HEAD_DIM))
    ang = jnp.arange(seq, dtype=F32)[:, None] * inv[None, :]
    cos, sin = jnp.cos(ang), jnp.sin(ang)
    cos = jnp.tile(cos, (1, 4))
    sin = jnp.concatenate([-sin, sin, -sin, sin], axis=1)

    def perm(t):
        return t.reshape(seq // dil, dil, LANES).transpose(1, 0, 2).reshape(seq, LANES)

    return perm(cos), perm(sin)


def _swap_halves(t):
    lane = lax.broadcasted_iota(jnp.int32, t.shape, 1)
    return jnp.where((lane % HEAD_DIM) < HEAD_DIM // 2, pltpu.roll(t, LANES - 32, 1), pltpu.roll(t, 32, 1))


def _rope(t, cos, sin):
    return t * cos + _swap_halves(t) * sin


def _rope_t(t, cos, sin):
    return t * cos - _swap_halves(t) * sin


def _to_residue(t, batch, dil):
    if dil == 1:
        return t
    s = t.shape[0] // batch
    return t.reshape(batch, s // dil, dil, t.shape[1]).transpose(0, 2, 1, 3).reshape(t.shape)


def _from_residue(t, batch, dil):
    if dil == 1:
        return t
    s = t.shape[0] // batch
    return t.reshape(batch, dil, s // dil, t.shape[1]).transpose(0, 2, 1, 3).reshape(t.shape)


def _rms_fwd(x, w, name):
    t = x.shape[0]
    tm = _row_tile(t, ROWS)

    def body(x_ref, w_ref, o_ref):
        o_ref[...] = _rms_tile(x_ref[...], w_ref[...]).astype(BF16)

    return pl.pallas_call(
        body, name=name, grid=(t // tm,),
        in_specs=[pl.BlockSpec((tm, D_MODEL), lambda i: (i, 0)), pl.BlockSpec((1, D_MODEL), lambda i: (0, 0))],
        out_specs=pl.BlockSpec((tm, D_MODEL), lambda i: (i, 0)),
        out_shape=jax.ShapeDtypeStruct((t, D_MODEL), BF16), compiler_params=_cp(),
    )(x, w)


def _rms_bwd_tile(xv, wv, dy, dres):
    r = lax.rsqrt(jnp.mean(xv * xv, axis=-1, keepdims=True) + RMS_EPS)
    xh = xv * r
    dxh = dy * wv
    dx = dres + r * (dxh - xh * jnp.mean(dxh * xh, axis=-1, keepdims=True))
    return dx, jnp.sum(dy * xh, axis=0, keepdims=True)


def _accumulate(ref, part):
    @pl.when(pl.program_id(0) == 0)
    def _():
        ref[...] = jnp.zeros_like(ref)

    ref[...] += part


def _rms_bwd(x, w, dhs, dres, name):
    t = x.shape[0]
    tm = _row_tile(t, ROWS)
    n = len(dhs)

    def body(*refs):
        x_ref, w_ref = refs[0], refs[1]
        dh_refs = refs[2:2 + n]
        dres_ref = refs[2 + n]
        dx_ref, dxb_ref, dw_ref = refs[3 + n:]
        dy = dh_refs[0][...].astype(F32)
        for k in range(1, n):
            dy = dy + dh_refs[k][...].astype(F32)
        dx, dw = _rms_bwd_tile(x_ref[...], w_ref[...], dy, dres_ref[...])
        dx_ref[...] = dx
        dxb_ref[...] = dx.astype(BF16)
        _accumulate(dw_ref, dw)

    row = pl.BlockSpec((tm, D_MODEL), lambda i: (i, 0))
    vec = pl.BlockSpec((1, D_MODEL), lambda i: (0, 0))
    return pl.pallas_call(
        body, name=name, grid=(t // tm,),
        in_specs=[row, vec] + [row] * n + [row],
        out_specs=[row, row, vec],
        out_shape=[jax.ShapeDtypeStruct((t, D_MODEL), F32), jax.ShapeDtypeStruct((t, D_MODEL), BF16),
                   jax.ShapeDtypeStruct((1, D_MODEL), F32)],
        compiler_params=_cp(),
    )(x, w, *dhs, dres)


def _final_tile(xv, wv, tv):
    r = lax.rsqrt(jnp.mean(xv * xv, axis=-1, keepdims=True) + RMS_EPS)
    xh = xv * r
    err = xh * wv - tv
    dy = err * (1.0 / D_MODEL)
    dxh = dy * wv
    dx = r * (dxh - xh * jnp.mean(dxh * xh, axis=-1, keepdims=True))
    return dx, jnp.sum(err * err, axis=0, keepdims=True), jnp.sum(dy * xh, axis=0, keepdims=True)


def _qkv_proj(h, w, cos, sin, group, name):
    t = h.shape[0]
    seq = cos.shape[0]
    tm = _row_tile(seq, MATMUL_ROWS)
    n_q = N_HEADS * HEAD_DIM // LANES
    n_rope = (N_HEADS + N_KV) * HEAD_DIM // LANES
    scale = Q_SCALE

    def body(h_ref, w_ref, cos_ref, sin_ref, o_ref):
        acc = jnp.dot(h_ref[...], w_ref[...], preferred_element_type=F32)
        cs, sn = cos_ref[...], sin_ref[...]
        csq, snq = cs * scale, sn * scale
        for c in range(QKV_W // LANES):
            blk = acc[:, c * LANES:(c + 1) * LANES]
            if c < n_q:
                blk = _rope(blk, csq, snq)
            elif c < n_rope:
                blk = _rope(blk, cs, sn)
            o_ref[:, c * LANES:(c + 1) * LANES] = blk.astype(BF16)

    tab = pl.BlockSpec((tm, LANES), lambda i: (i % (seq // tm), 0))
    return pl.pallas_call(
        body, name=name, grid=(t // tm,),
        in_specs=[pl.BlockSpec((tm, D_MODEL), lambda i: (i, 0)),
                  pl.BlockSpec((D_MODEL, QKV_W), lambda i: (0, group)), tab, tab],
        out_specs=pl.BlockSpec((tm, QKV_W), lambda i: (i, 0)),
        out_shape=jax.ShapeDtypeStruct((t, QKV_W), BF16), compiler_params=_cp(),
    )(h, w, cos, sin)


def _rms_tile(xv, wv):
    return (xv * lax.rsqrt(jnp.mean(xv * xv, axis=-1, keepdims=True) + RMS_EPS)) * wv


def _mm_res(a, w, res, nw, name):
    t, k = a.shape
    tm = _row_tile(t, ROWS)

    def body(a_ref, w_ref, r_ref, nw_ref, o_ref, h_ref):
        xv = r_ref[...] + jnp.dot(a_ref[...], w_ref[...], preferred_element_type=F32)
        o_ref[...] = xv
        h_ref[...] = _rms_tile(xv, nw_ref[...]).astype(BF16)

    row = pl.BlockSpec((tm, D_MODEL), lambda i: (i, 0))
    return pl.pallas_call(
        body, name=name, grid=(t // tm,),
        in_specs=[pl.BlockSpec((tm, k), lambda i: (i, 0)),
                  pl.BlockSpec((k, D_MODEL), lambda i: (0, 0), pipeline_mode=pl.Buffered(1)), row,
                  pl.BlockSpec((1, D_MODEL), lambda i: (0, 0))],
        out_specs=[row, row],
        out_shape=[jax.ShapeDtypeStruct((t, D_MODEL), F32), jax.ShapeDtypeStruct((t, D_MODEL), BF16)],
        compiler_params=_cp(),
    )(a, w, res, nw)


def _mm_nt(dy, w, group, out_dtype, name):
    t, n = dy.shape
    k = w.shape[0]
    tm = _row_tile(t, MATMUL_ROWS)

    def body(dy_ref, w_ref, o_ref):
        o_ref[...] = lax.dot_general(dy_ref[...], w_ref[...], (((1,), (1,)), ((), ())),
                                     preferred_element_type=F32).astype(out_dtype)

    return pl.pallas_call(
        body, name=name, grid=(t // tm,),
        in_specs=[pl.BlockSpec((tm, n), lambda i: (i, 0)), pl.BlockSpec((k, n), lambda i: (0, group))],
        out_specs=pl.BlockSpec((tm, k), lambda i: (i, 0)),
        out_shape=jax.ShapeDtypeStruct((t, k), out_dtype), compiler_params=_cp(),
    )(dy, w)


def _mm_nt_rms(dy, w, x, nw, dres, name):
    t, n = dy.shape
    tm = _row_tile(t, ROWS)

    def body(dy_ref, w_ref, x_ref, nw_ref, dres_ref, dx_ref, dw_ref):
        dh = lax.dot_general(dy_ref[...], w_ref[...], (((1,), (1,)), ((), ())), preferred_element_type=F32)
        dx, dw = _rms_bwd_tile(x_ref[...], nw_ref[...], dh, dres_ref[...])
        dx_ref[...] = dx
        _accumulate(dw_ref, dw)

    row = pl.BlockSpec((tm, D_MODEL), lambda i: (i, 0))
    vec = pl.BlockSpec((1, D_MODEL), lambda i: (0, 0))
    return pl.pallas_call(
        body, name=name, grid=(t // tm,),
        in_specs=[pl.BlockSpec((tm, n), lambda i: (i, 0)),
                  pl.BlockSpec((D_MODEL, n), lambda i: (0, 0), pipeline_mode=pl.Buffered(1)), row, vec, row],
        out_specs=[row, vec],
        out_shape=[jax.ShapeDtypeStruct((t, D_MODEL), F32), jax.ShapeDtypeStruct((1, D_MODEL), F32)],
        compiler_params=_cp(),
    )(dy, w, x, nw, dres)


def _out_bwd(dx, w, o, name):
    t = dx.shape[0]
    tm = _row_tile(t, ROWS)

    def body(dx_ref, w_ref, o_ref, et_ref, do_ref, adj_ref):
        do = lax.dot_general(dx_ref[...], w_ref[...], (((1,), (1,)), ((), ())), preferred_element_type=F32)
        do_ref[...] = do.astype(BF16)
        adj_ref[...] = -_dot_heads(do * o_ref[...].astype(F32), et_ref[...])

    row = pl.BlockSpec((tm, D_MODEL), lambda i: (i, 0))
    return pl.pallas_call(
        body, name=name, grid=(t // tm,),
        in_specs=[row, pl.BlockSpec((D_MODEL, D_MODEL), lambda i: (0, 0)), row,
                  pl.BlockSpec((D_MODEL, LANES), lambda i: (0, 0))],
        out_specs=[row, pl.BlockSpec((tm, LANES), lambda i: (i, 0))],
        out_shape=[jax.ShapeDtypeStruct((t, D_MODEL), BF16), jax.ShapeDtypeStruct((t, LANES), F32)],
        compiler_params=_cp(),
    )(dx, w, o, _head_expander().T)


def _mm_tn(a, bs, name):
    aq = a.ndim == 3
    bq = bs[0].ndim == 3
    t, ka = a.shape[-2:]
    n = bs[0].shape[-1]
    nq = N_CHIPS if (aq or bq) else 1
    tt = _row_tile(t, GRAD_TOKENS)
    tn = n if n <= 1024 else 768
    assert n % tn == 0
    nb = len(bs)
    steps = t // tt

    def body(*refs):
        a_ref = refs[0]
        b_refs = refs[1:1 + nb]
        o_refs = refs[1 + nb:1 + 2 * nb]
        acc_refs = refs[1 + 2 * nb:]
        s = pl.program_id(2)
        av = a_ref[...]
        for b_ref, o_ref, acc_ref in zip(b_refs, o_refs, acc_refs):
            @pl.when(s == 0)
            def _():
                acc_ref[...] = jnp.zeros_like(acc_ref)

            acc_ref[...] += lax.dot_general(av, b_ref[...], (((0,), (0,)), ((), ())), preferred_element_type=F32)

            @pl.when(s == steps - 1)
            def _():
                o_ref[...] = acc_ref[...].astype(BF16)

    a_spec = (pl.BlockSpec((None, tt, ka), lambda q, j, s: (q, s, 0)) if aq
              else pl.BlockSpec((tt, ka), lambda q, j, s: (s, 0)))
    b_spec = (pl.BlockSpec((None, tt, tn), lambda q, j, s: (q, s, j)) if bq
              else pl.BlockSpec((tt, tn), lambda q, j, s: (s, j)))
    if nq > 1:
        o_spec = pl.BlockSpec((None, ka, tn), lambda q, j, s: (q, 0, j))
        o_shape = jax.ShapeDtypeStruct((nq, ka, n), BF16)
    else:
        o_spec = pl.BlockSpec((ka, tn), lambda q, j, s: (0, j))
        o_shape = jax.ShapeDtypeStruct((ka, n), BF16)
    outs = pl.pallas_call(
        body, name=name, grid=(nq, n // tn, steps),
        in_specs=[a_spec] + [b_spec] * nb, out_specs=[o_spec] * nb, out_shape=[o_shape] * nb,
        scratch_shapes=[pltpu.VMEM((ka, tn), F32)] * nb, compiler_params=_cp(),
    )(a, *bs)
    return outs


def _sigmoid(x):
    return 1.0 / (1.0 + jnp.exp(-x))


def _ffn_up(h, wg, wu, layer, name):
    t = h.shape[0]
    tm = _row_tile(t, MATMUL_ROWS)
    nt = (((1,), (1,)), ((), ()))

    def body(h_ref, wg_ref, wu_ref, a_ref, dg_ref, du_ref):
        hv = h_ref[...]
        g = lax.dot_general(hv, wg_ref[...], nt, preferred_element_type=F32)
        u = lax.dot_general(hv, wu_ref[...], nt, preferred_element_type=F32)
        sg = _sigmoid(g)
        silu = g * sg
        a_ref[...] = (silu * u).astype(BF16)
        dg_ref[...] = (sg * (1.0 + g * (1.0 - sg)) * u).astype(BF16)
        du_ref[...] = silu.astype(BF16)

    wspec = pl.BlockSpec((None, None, FF_SH, D_MODEL), lambda q, i: (q, layer, 0, 0))
    ospec = pl.BlockSpec((None, tm, FF_SH), lambda q, i: (q, i, 0))
    oshape = jax.ShapeDtypeStruct((N_CHIPS, t, FF_SH), BF16)
    return pl.pallas_call(
        body, name=name, grid=(N_CHIPS, t // tm),
        in_specs=[pl.BlockSpec((tm, D_MODEL), lambda q, i: (i, 0)), wspec, wspec],
        out_specs=[ospec] * 3, out_shape=[oshape] * 3, compiler_params=_cp(),
    )(h, wg, wu)


def _ffn_down(a, wd, res, layer, name, norm_w=None, head=None):
    t = a.shape[1]
    tm = _row_tile(t, ROWS)
    resident = pl.BlockSpec((N_CHIPS, None, FF_SH, D_MODEL), lambda i: (0, layer, 0, 0), pipeline_mode=pl.Buffered(1))
    row = pl.BlockSpec((tm, D_MODEL), lambda i: (i, 0))
    vec = pl.BlockSpec((1, D_MODEL), lambda i: (0, 0))

    def hidden(a_ref, w_ref, r_ref):
        acc = r_ref[...]
        for q in range(N_CHIPS):
            acc = acc + jnp.dot(a_ref[q], w_ref[q], preferred_element_type=F32)
        return acc

    if head is None:
        def body(a_ref, w_ref, r_ref, nw_ref, o_ref, h_ref):
            xv = hidden(a_ref, w_ref, r_ref)
            o_ref[...] = xv
            h_ref[...] = _rms_tile(xv, nw_ref[...]).astype(BF16)

        return pl.pallas_call(
            body, name=name, grid=(t // tm,),
            in_specs=[pl.BlockSpec((N_CHIPS, tm, FF_SH), lambda i: (0, i, 0)), resident, row, vec],
            out_specs=[row, row],
            out_shape=[jax.ShapeDtypeStruct((t, D_MODEL), F32), jax.ShapeDtypeStruct((t, D_MODEL), BF16)],
            compiler_params=_cp(),
        )(a, wd, res, norm_w)

    def body(a_ref, w_ref, r_ref, nw_ref, t_ref, dx_ref, dxb_ref, l_ref, dw_ref):
        dx, sq, dw = _final_tile(hidden(a_ref, w_ref, r_ref), nw_ref[...], t_ref[...])
        dx_ref[...] = dx
        dxb_ref[...] = dx.astype(BF16)
        _accumulate(l_ref, sq)
        _accumulate(dw_ref, dw)

    return pl.pallas_call(
        body, name=name, grid=(t // tm,),
        in_specs=[pl.BlockSpec((N_CHIPS, tm, FF_SH), lambda i: (0, i, 0)), resident, row, vec, row],
        out_specs=[row, row, vec, vec],
        out_shape=[jax.ShapeDtypeStruct((t, D_MODEL), F32), jax.ShapeDtypeStruct((t, D_MODEL), BF16),
                   jax.ShapeDtypeStruct((1, D_MODEL), F32), jax.ShapeDtypeStruct((1, D_MODEL), F32)],
        compiler_params=_cp(),
    )(a, wd, res, *head)


def _ffn_bwd(dy, wd, wg, wu, fg, fu, x, nw, dres, name):
    t = dy.shape[0]
    tm = _row_tile(t, FFN_BWD_ROWS)
    nt = (((1,), (1,)), ((), ()))

    def body(dy_ref, wd_ref, wg_ref, wu_ref, fg_ref, fu_ref, x_ref, nw_ref, dres_ref,
             dg_ref, du_ref, dx_ref, dxb_ref, dw_ref):
        dyv = dy_ref[...]
        acc = jnp.zeros((tm, D_MODEL), F32)
        for q in range(N_CHIPS):
            da = lax.dot_general(dyv, wd_ref[q], nt, preferred_element_type=F32)
            dg = (da * fg_ref[q].astype(F32)).astype(BF16)
            du = (da * fu_ref[q].astype(F32)).astype(BF16)
            dg_ref[q] = dg
            du_ref[q] = du
            acc = acc + jnp.dot(dg, wg_ref[q], preferred_element_type=F32)
            acc = acc + jnp.dot(du, wu_ref[q], preferred_element_type=F32)
        dx, dw = _rms_bwd_tile(x_ref[...], nw_ref[...], acc, dres_ref[...])
        dx_ref[...] = dx
        dxb_ref[...] = dx.astype(BF16)
        _accumulate(dw_ref, dw)

    aspec = pl.BlockSpec((N_CHIPS, tm, FF_SH), lambda i: (0, i, 0))
    wspec = pl.BlockSpec((N_CHIPS, None, FF_SH, D_MODEL), lambda i: (0, 0, 0, 0), pipeline_mode=pl.Buffered(1))
    row = pl.BlockSpec((tm, D_MODEL), lambda i: (i, 0))
    vec = pl.BlockSpec((1, D_MODEL), lambda i: (0, 0))
    ashape = jax.ShapeDtypeStruct((N_CHIPS, t, FF_SH), BF16)
    return pl.pallas_call(
        body, name=name, grid=(t // tm,),
        in_specs=[row, wspec, wspec, wspec, aspec, aspec, row, vec, row],
        out_specs=[aspec, aspec, row, row, vec],
        out_shape=[ashape, ashape, jax.ShapeDtypeStruct((t, D_MODEL), F32), jax.ShapeDtypeStruct((t, D_MODEL), BF16),
                   jax.ShapeDtypeStruct((1, D_MODEL), F32)],
        compiler_params=_cp(),
    )(dy, wd, wg, wu, fg, fu, x, nw, dres)


def _attn_geometry(length, half_window):
    qb = min(LANES, length)
    kw = min(qb + 2 * half_window, length)
    return qb, kw, length // qb


def _dup_kv(src_ref, dst_ref, s, length):
    ch = min(length, 256)
    lo = lax.broadcasted_iota(jnp.int32, (ch, LANES), 1) < HEAD_DIM

    def chunk(c, carry):
        r0 = pl.multiple_of(c * ch, ch)
        for j in range(N_KV // 2):
            tile = src_ref[s, pl.ds(r0, ch), j * LANES:(j + 1) * LANES].astype(F32)
            rolled = pltpu.roll(tile, HEAD_DIM, 1)
            dst_ref[2 * j, pl.ds(r0, ch), :] = jnp.where(lo, tile, rolled).astype(BF16)
            dst_ref[2 * j + 1, pl.ds(r0, ch), :] = jnp.where(lo, rolled, tile).astype(BF16)
        return carry

    lax.fori_loop(0, length // ch, chunk, 0)


def _stack_heads(ref, s, q0, qb, g):
    lo = lax.broadcasted_iota(jnp.int32, (qb, LANES), 1) < HEAD_DIM
    parts = []
    for a in range(4):
        col = (2 * g + a // 2) * LANES
        tile = ref[s, pl.ds(q0, qb), col:col + LANES]
        keep = lo if a % 2 == 0 else jnp.logical_not(lo)
        parts.append(jnp.where(keep, tile, jnp.zeros_like(tile)))
    return jnp.concatenate(parts, axis=0)


def _unstack_pair_t(stacked_t, qb, pair):
    both = jnp.concatenate([stacked_t[:, (2 * pair) * qb:(2 * pair + 1) * qb],
                            stacked_t[:, (2 * pair + 1) * qb:(2 * pair + 2) * qb]], axis=0)
    return both.T


def _band_mask_t(q0, k0, qb, kw, half_window):
    key = lax.broadcasted_iota(jnp.int32, (kw, 4 * qb), 0)
    qry = lax.broadcasted_iota(jnp.int32, (kw, 4 * qb), 1) & (qb - 1)
    return jnp.abs((q0 + qry) - (k0 + key)) <= half_window


def _block_origin(i, qb, kw, half_window, length):
    if isinstance(i, int):
        return i * qb, min(max(i * qb - half_window, 0), length - kw)
    return (pl.multiple_of(i * qb, qb),
            pl.multiple_of(jnp.clip(i * qb - half_window, 0, length - kw), HEAD_DIM))


def _head_row(vals, qb):
    return jnp.concatenate([jnp.broadcast_to(v, (1, qb)).astype(F32) for v in vals], axis=1)


def _attn_fwd(qkv, sink, n_seq, length, half_window, seq_blk, out_dtype, name):
    qb, kw, nblk = _attn_geometry(length, half_window)
    with_sink = sink is not None
    nt = (((1,), (1,)), ((), ()))
    tn = (((0,), (0,)), ((), ()))
    qkv3 = qkv.reshape(n_seq, length, QKV_W)

    def body(*refs):
        refs = list(refs)
        sink_ref = refs.pop(0) if with_sink else None
        q_ref, k_ref, v_ref, o_ref, lse_ref = refs[:5]
        kx_ref, vx_ref = refs[-2:]
        head_row = lax.broadcasted_iota(jnp.int32, (N_HEADS, qb), 0)
        for s in range(seq_blk):
            _dup_kv(k_ref, kx_ref, s, length)
            _dup_kv(v_ref, vx_ref, s, length)

            def block(i, carry):
                q0, k0 = _block_origin(i, qb, kw, half_window, length)
                valid = _band_mask_t(q0, k0, qb, kw, half_window)
                lse_tile = jnp.zeros((N_HEADS, qb), F32)
                groups = range(N_KV)
                sts = [lax.dot_general(kx_ref[g, pl.ds(k0, kw), :], _stack_heads(q_ref, s, q0, qb, g), nt,
                                       preferred_element_type=F32) for g in groups]
                sts = [jnp.where(valid, st, NEG_INF) for st in sts]
                ms = [jnp.max(st, axis=0, keepdims=True) for st in sts]
                if with_sink:
                    sks = [_head_row([sink_ref[4 * g + a] * LOG2E for a in range(4)], qb) for g in groups]
                    ms = [jnp.maximum(m, sk) for m, sk in zip(ms, sks)]
                es = [jnp.exp2(st - m) for st, m in zip(sts, ms)]
                dens = [jnp.sum(e, axis=0, keepdims=True) for e in es]
                if with_sink:
                    dens = [den + jnp.exp2(sk - m) for den, sk, m in zip(dens, sks, ms)]
                ots = [lax.dot_general(vx_ref[g, pl.ds(k0, kw), 0:HEAD_DIM], es[g].astype(BF16), tn,
                                       preferred_element_type=F32) / dens[g] for g in groups]
                for g in groups:
                    for pair in range(2):
                        col = (2 * g + pair) * LANES
                        o_ref[s, pl.ds(q0, qb), col:col + LANES] = _unstack_pair_t(ots[g], qb, pair).astype(out_dtype)
                    lse = ms[g] * LN2 + jnp.log(dens[g])
                    for a in range(4):
                        lse_tile = jnp.where(head_row == 4 * g + a, lse[:, a * qb:(a + 1) * qb], lse_tile)
                lse_ref[s, :, pl.ds(q0, qb)] = lse_tile
                return carry

            if nblk == 1:
                block(0, 0)
            else:
                lax.fori_loop(0, nblk, block, 0)

    in_specs = [pl.BlockSpec((seq_blk, length, N_HEADS * HEAD_DIM), lambda n: (n, 0, 0)),
                pl.BlockSpec((seq_blk, length, N_KV * HEAD_DIM), lambda n: (n, 0, 4)),
                pl.BlockSpec((seq_blk, length, N_KV * HEAD_DIM), lambda n: (n, 0, 5))]
    args = [qkv3, qkv3, qkv3]
    if with_sink:
        in_specs.insert(0, pl.BlockSpec(memory_space=pltpu.SMEM))
        args.insert(0, sink)
    out_specs = [pl.BlockSpec((seq_blk, length, D_MODEL), lambda n: (n, 0, 0)),
                 pl.BlockSpec((seq_blk, N_HEADS, length), lambda n: (n, 0, 0))]
    out_shape = [jax.ShapeDtypeStruct((n_seq, length, D_MODEL), out_dtype),
                 jax.ShapeDtypeStruct((n_seq, N_HEADS, length), F32)]
    o, lse = pl.pallas_call(
        body, name=name, grid=(n_seq // seq_blk,), in_specs=in_specs, out_specs=out_specs, out_shape=out_shape,
        scratch_shapes=[pltpu.VMEM((N_KV, length, LANES), BF16), pltpu.VMEM((N_KV, length, LANES), BF16)],
        compiler_params=_cp(),
    )(*args)
    return o.reshape(n_seq * length, D_MODEL), lse


def _attn_bwd(qkv, do, adj, lse, sink, cos, sin, n_seq, length, half_window, seq_blk, dil, name):
    qb, kw, nblk = _attn_geometry(length, half_window)
    scale = 1.0 / math.sqrt(HEAD_DIM)
    with_sink = sink is not None
    nt = (((1,), (1,)), ((), ()))
    tn = (((0,), (0,)), ((), ()))
    qkv3 = qkv.reshape(n_seq, length, QKV_W)
    do3 = do.reshape(n_seq, length, D_MODEL)
    tabs = [t.reshape(dil, length, LANES) for t in (cos, sin)]
    tab_blocks = dil // seq_blk if dil >= seq_blk else 1

    def body(*refs):
        refs = list(refs)
        sink_ref = refs.pop(0) if with_sink else None
        q_ref, k_ref, v_ref, do_ref, aux_ref, lse_ref, cos_ref, sin_ref, dqkv_ref = refs[:9]
        ds_ref = refs[9] if with_sink else None
        kx_ref, vx_ref, dkx_ref, dvx_ref = refs[-4:]
        lane = lax.broadcasted_iota(jnp.int32, (1, LANES), 1)
        if with_sink:
            @pl.when(pl.program_id(0) == 0)
            def _():
                ds_ref[...] = jnp.zeros_like(ds_ref)

        for s in range(seq_blk):
            ts = s % dil
            _dup_kv(k_ref, kx_ref, s, length)
            _dup_kv(v_ref, vx_ref, s, length)
            dkx_ref[...] = jnp.zeros_like(dkx_ref)
            dvx_ref[...] = jnp.zeros_like(dvx_ref)

            def block(i, dsink):
                q0, k0 = _block_origin(i, qb, kw, half_window, length)
                valid = _band_mask_t(q0, k0, qb, kw, half_window)
                cs = cos_ref[ts, pl.ds(q0, qb), :] * scale
                sn = sin_ref[ts, pl.ds(q0, qb), :] * scale
                adj_tile = aux_ref[s, :, pl.ds(q0, qb)]
                lse_tile = lse_ref[s, :, pl.ds(q0, qb)]
                groups = range(N_KV)
                qss = [_stack_heads(q_ref, s, q0, qb, g) for g in groups]
                doss = [_stack_heads(do_ref, s, q0, qb, g) for g in groups]
                kxs = [kx_ref[g, pl.ds(k0, kw), :] for g in groups]
                sts = [lax.dot_general(kxs[g], qss[g], nt, preferred_element_type=F32) for g in groups]
                dpts = [lax.dot_general(vx_ref[g, pl.ds(k0, kw), :], doss[g], nt, preferred_element_type=F32)
                        for g in groups]
                lses = [_head_row([lse_tile[4 * g + a:4 * g + a + 1, :] * LOG2E for a in range(4)], qb) for g in groups]
                shifts = [_head_row([adj_tile[4 * g + a:4 * g + a + 1, :] for a in range(4)], qb) for g in groups]
                pts = [jnp.exp2(jnp.where(valid, sts[g], NEG_INF) - lses[g]) for g in groups]
                dsbs = [(pts[g] * (dpts[g] + shifts[g])).astype(BF16) for g in groups]
                pbs = [pt.astype(BF16) for pt in pts]
                if with_sink:
                    for g in groups:
                        sk = _head_row([sink_ref[4 * g + a] * LOG2E for a in range(4)], qb)
                        dsk = jnp.exp2(sk - lses[g]) * shifts[g]
                        for a in range(4):
                            tot = jnp.sum(dsk[:, a * qb:(a + 1) * qb], axis=1, keepdims=True)
                            dsink = dsink + jnp.where(lane == 4 * g + a, tot, 0.0)
                dqts = [lax.dot_general(kx_ref[g, pl.ds(k0, kw), 0:HEAD_DIM], dsbs[g], tn, preferred_element_type=F32)
                        for g in groups]
                for g in groups:
                    for pair in range(2):
                        col = (2 * g + pair) * LANES
                        tile = _rope_t(_unstack_pair_t(dqts[g], qb, pair), cs, sn)
                        dqkv_ref[s, pl.ds(q0, qb), col:col + LANES] = tile.astype(BF16)
                for g in groups:
                    dkx_ref[g, pl.ds(k0, kw), :] += jnp.dot(dsbs[g], qss[g], preferred_element_type=F32)
                    dvx_ref[g, pl.ds(k0, kw), :] += jnp.dot(pbs[g], doss[g], preferred_element_type=F32)
                return dsink

            if nblk == 1:
                dsink = block(0, jnp.zeros((1, LANES), F32))
            else:
                dsink = lax.fori_loop(0, nblk, block, jnp.zeros((1, LANES), F32))
            if with_sink:
                ds_ref[0:1, :] += dsink

            ch = min(length, 256)
            lo_c = lax.broadcasted_iota(jnp.int32, (ch, LANES), 1) < HEAD_DIM

            def fin(c, carry):
                r0 = pl.multiple_of(c * ch, ch)
                cs = cos_ref[ts, pl.ds(r0, ch), :]
                sn = sin_ref[ts, pl.ds(r0, ch), :]
                for j in range(N_KV // 2):
                    both = []
                    for acc_ref in (dkx_ref, dvx_ref):
                        t0 = acc_ref[2 * j, pl.ds(r0, ch), :]
                        t1 = acc_ref[2 * j + 1, pl.ds(r0, ch), :]
                        both.append(jnp.where(lo_c, t0, t1) + pltpu.roll(jnp.where(lo_c, t1, t0), HEAD_DIM, 1))
                    kcol = N_HEADS * HEAD_DIM + j * LANES
                    vcol = (N_HEADS + N_KV) * HEAD_DIM + j * LANES
                    dqkv_ref[s, pl.ds(r0, ch), kcol:kcol + LANES] = _rope_t(both[0] * LN2, cs, sn).astype(BF16)
                    dqkv_ref[s, pl.ds(r0, ch), vcol:vcol + LANES] = both[1].astype(BF16)
                return carry

            lax.fori_loop(0, length // ch, fin, 0)

    seq_map = lambda n: (n, 0, 0)
    tab_map = (lambda n: (n % tab_blocks, 0, 0)) if dil >= seq_blk else (lambda n: (0, 0, 0))
    tab_rows = min(seq_blk, dil)
    in_specs = [pl.BlockSpec((seq_blk, length, N_HEADS * HEAD_DIM), seq_map),
                pl.BlockSpec((seq_blk, length, N_KV * HEAD_DIM), lambda n: (n, 0, 4)),
                pl.BlockSpec((seq_blk, length, N_KV * HEAD_DIM), lambda n: (n, 0, 5)),
                pl.BlockSpec((seq_blk, length, D_MODEL), seq_map),
                pl.BlockSpec((seq_blk, N_HEADS, length), seq_map),
                pl.BlockSpec((seq_blk, N_HEADS, length), seq_map),
                pl.BlockSpec((tab_rows, length, LANES), tab_map),
                pl.BlockSpec((tab_rows, length, LANES), tab_map)]
    args = [qkv3, qkv3, qkv3, do3, adj, lse] + tabs
    if with_sink:
        in_specs.insert(0, pl.BlockSpec(memory_space=pltpu.SMEM))
        args.insert(0, sink)
    out_specs = [pl.BlockSpec((seq_blk, length, QKV_W), seq_map)]
    out_shape = [jax.ShapeDtypeStruct((n_seq, length, QKV_W), BF16)]
    if with_sink:
        out_specs.append(pl.BlockSpec((8, LANES), lambda n: (0, 0)))
        out_shape.append(jax.ShapeDtypeStruct((8, LANES), F32))
    outs = pl.pallas_call(
        body, name=name, grid=(n_seq // seq_blk,), in_specs=in_specs, out_specs=out_specs, out_shape=out_shape,
        scratch_shapes=[pltpu.VMEM((N_KV, length, LANES), BF16), pltpu.VMEM((N_KV, length, LANES), BF16),
                        pltpu.VMEM((N_KV, length, LANES), F32), pltpu.VMEM((N_KV, length, LANES), F32)],
        compiler_params=_cp(),
    )(*args)
    dqkv = outs[0].reshape(n_seq * length, QKV_W)
    return (dqkv, outs[1]) if with_sink else (dqkv, None)


def _head_expander():
    h = jnp.arange(LANES)[:, None]
    l = jnp.arange(D_MODEL)[None, :]
    return (l // HEAD_DIM == h).astype(BF16)


def _dot_split(a, e):
    hi = a.astype(BF16)
    lo = (a - hi.astype(F32)).astype(BF16)
    return jnp.dot(hi, e, preferred_element_type=F32) + jnp.dot(lo, e, preferred_element_type=F32)


def _dot_heads(a, e):
    return jnp.dot(a.astype(BF16), e, preferred_element_type=F32)


def _mix_weights(lses):
    m = jnp.maximum(jnp.maximum(lses[0], lses[1]), lses[2])
    es = [jnp.exp(v - m) for v in lses]
    tot = es[0] + es[1] + es[2]
    return [e / tot for e in es]


def _mix_fwd(os_, lses, name):
    t = os_[0].shape[0]
    tm = _row_tile(t, ROWS)

    def body(o0, o1, o2, l0, l1, l2, e_ref, out_ref):
        wts = _mix_weights([l0[...], l1[...], l2[...]])
        acc = jnp.zeros((tm, D_MODEL), F32)
        for w, o_ref in zip(wts, (o0, o1, o2)):
            acc = acc + _dot_split(w, e_ref[...]) * o_ref[...]
        out_ref[...] = acc.astype(BF16)

    row = pl.BlockSpec((tm, D_MODEL), lambda i: (i, 0))
    lrow = pl.BlockSpec((tm, LANES), lambda i: (i, 0))
    return pl.pallas_call(
        body, name=name, grid=(t // tm,),
        in_specs=[row] * 3 + [lrow] * 3 + [pl.BlockSpec((LANES, D_MODEL), lambda i: (0, 0))],
        out_specs=row, out_shape=jax.ShapeDtypeStruct((t, D_MODEL), BF16), compiler_params=_cp(),
    )(*os_, *lses, _head_expander())


def _mix_bwd(dx, w_out, os_, lses, name):
    t = dx.shape[0]
    tm = _row_tile(t, ROWS)

    def body(d_ref, w_ref, o0, o1, o2, l0, l1, l2, e_ref, et_ref, do0, do1, do2, a0, a1, a2):
        wts = _mix_weights([l0[...], l1[...], l2[...]])
        dv = lax.dot_general(d_ref[...], w_ref[...], (((1,), (1,)), ((), ())), preferred_element_type=F32)
        cs = [_dot_heads(dv * o_ref[...], et_ref[...]) for o_ref in (o0, o1, o2)]
        mean_c = wts[0] * cs[0] + wts[1] * cs[1] + wts[2] * cs[2]
        for w, c, do_ref, a_ref in zip(wts, cs, (do0, do1, do2), (a0, a1, a2)):
            do_ref[...] = (_dot_heads(w, e_ref[...]) * dv).astype(BF16)
            a_ref[...] = w * (c - mean_c) - w * c

    row = pl.BlockSpec((tm, D_MODEL), lambda i: (i, 0))
    lrow = pl.BlockSpec((tm, LANES), lambda i: (i, 0))
    e = _head_expander()
    return pl.pallas_call(
        body, name=name, grid=(t // tm,),
        in_specs=[row, pl.BlockSpec((D_MODEL, D_MODEL), lambda i: (0, 0), pipeline_mode=pl.Buffered(1))]
        + [row] * 3 + [lrow] * 3 + [pl.BlockSpec((LANES, D_MODEL), lambda i: (0, 0)),
                                    pl.BlockSpec((D_MODEL, LANES), lambda i: (0, 0))],
        out_specs=[row] * 3 + [lrow] * 3,
        out_shape=[jax.ShapeDtypeStruct((t, D_MODEL), BF16)] * 3 + [jax.ShapeDtypeStruct((t, LANES), F32)] * 3,
        compiler_params=_cp(),
    )(dx, w_out, *os_, *lses, e, e.T)


def _stats_to_tokens(stat, batch, dil):
    n_seq, _, length = stat.shape
    t = stat.transpose(0, 2, 1).reshape(n_seq * length, N_HEADS)
    return _from_residue(jnp.pad(t, ((0, 0), (0, LANES - N_HEADS))), batch, dil)


def _stats_from_tokens(stat, batch, dil, n_seq, length):
    t = _to_residue(stat[:, :N_HEADS], batch, dil)
    return t.reshape(n_seq, length, N_HEADS).transpose(0, 2, 1)


def _group_geometry(batch, seq, dil, window):
    length = seq // dil
    n_seq = batch * dil
    seq_blk = max(1, min(dil, 1024 // length))
    return n_seq, length, (window // 2) // dil, seq_blk


def _local_step(x, target, a_in, a_sink, a_out, b_in, b_out, norm_mix, norm_ffn, wg, wu, wd, final_norm):
    batch, seq, _ = x.shape
    t = batch * seq
    x0 = x.reshape(t, D_MODEL)
    tgt = target.reshape(t, D_MODEL)
    tabs = {d: _rope_tables(seq, d) for _, d in DILATED}
    nm = [norm_mix[i:i + 1] for i in range(2)]
    nf = [norm_ffn[i:i + 1] for i in range(2)]

    h0 = _rms_fwd(x0, nm[0], "rms_mix0")
    qkv0 = _qkv_proj(h0, a_in, *tabs[1], 0, "qkv0")
    o0, lse0 = _attn_fwd(qkv0, a_sink, batch, seq, HALF_WINDOW_A, 1, BF16, "attn0")
    x1, hf0 = _mm_res(o0, a_out, x0, nf[0], "out0")
    act0, g0, u0 = _ffn_up(hf0, wg[0], wu[0], 0, "ffn_up0")
    x2, h1 = _ffn_down(act0, wd[0], x1, 0, "ffn_down0", norm_w=nm[1])

    geo = [_group_geometry(batch, seq, d, w) for w, d in DILATED]
    h1g, qkv1, o1, lse1, lse1r = [], [], [], [], []
    for gi, (_, d) in enumerate(DILATED):
        n_seq, length, hw, sb = geo[gi]
        hp = _to_residue(h1, batch, d)
        pj = _qkv_proj(hp, b_in, *tabs[d], gi, f"qkv1_{gi}")
        o, lse = _attn_fwd(pj, None, n_seq, length, hw, sb, BF16, f"attn1_{gi}")
        h1g.append(hp)
        qkv1.append(pj)
        o1.append(_from_residue(o, batch, d))
        lse1r.append(lse)
        lse1.append(_stats_to_tokens(lse, batch, d))
    omix = _mix_fwd(o1, lse1, "mix")
    x3, hf1 = _mm_res(omix, b_out, x2, nf[1], "out1")
    act1, g1, u1 = _ffn_up(hf1, wg[1], wu[1], 0, "ffn_up1")
    dx4, dx4b, loss_cols, d_final = _ffn_down(act1, wd[1], x3, 0, "ffn_down1_loss",
                                                     head=(final_norm.reshape(1, D_MODEL), tgt))

    def ffn_bwd(dxo, dxob, x_mid, hf, g, u, act, layer):
        dg, du, dxm, dxmb, d_nf = _ffn_bwd(dxob, wd[layer], wg[layer], wu[layer], g, u, x_mid, nf[layer], dxo,
                                           f"ffn_bwd{layer}")
        (d_wd,) = _mm_tn(act, [dxob], f"grad_wd{layer}")
        (d_wgt,) = _mm_tn(dg, [hf], f"grad_wg{layer}")
        (d_wut,) = _mm_tn(du, [hf], f"grad_wu{layer}")
        return dxm, dxmb, d_nf, d_wgt, d_wut, d_wd

    dx3, dx3b, d_nf1, d_wg1, d_wu1, d_wd1 = ffn_bwd(dx4, dx4b, x3, hf1, g1, u1, act1, 1)

    (d_b_out,) = _mm_tn(omix, [dx3b], "grad_b_out")
    mb = _mix_bwd(dx3b, b_out, o1, lse1, "out1_mix_bwd")
    dh1, d_b_in = [], []
    for gi, (_, d) in enumerate(DILATED):
        n_seq, length, hw, sb = geo[gi]
        dog = _to_residue(mb[gi], batch, d)
        adj = _stats_from_tokens(mb[3 + gi], batch, d, n_seq, length)
        dpj, _ = _attn_bwd(qkv1[gi], dog, adj, lse1r[gi], None, *tabs[d], n_seq, length, hw, sb, d, f"attn1_bwd{gi}")
        (dw,) = _mm_tn(h1g[gi], [dpj], f"grad_b_in{gi}")
        d_b_in.append(dw)
        dh1.append(_from_residue(_mm_nt(dpj, b_in, gi, BF16, f"qkv1_bwd{gi}"), batch, d))
    dx2, dx2b, d_nm1 = _rms_bwd(x2, nm[1], dh1, dx3, "rms_mix_bwd1")

    dx1, dx1b, d_nf0, d_wg0, d_wu0, d_wd0 = ffn_bwd(dx2, dx2b, x1, hf0, g0, u0, act0, 0)

    do0, adj0 = _out_bwd(dx1b, a_out, o0, "out0_bwd")
    (d_a_out,) = _mm_tn(o0, [dx1b], "grad_a_out")
    adj0 = _stats_from_tokens(adj0, batch, 1, batch, seq)
    dqkv0, d_sink = _attn_bwd(qkv0, do0, adj0, lse0, a_sink, *tabs[1], batch, seq, HALF_WINDOW_A, 1, 1, "attn0_bwd")
    (d_a_in,) = _mm_tn(h0, [dqkv0], "grad_a_in")
    gx, d_nm0 = _mm_nt_rms(dqkv0, a_in, x0, nm[0], dx1, "qkv0_bwd")

    grads = dict(a_in=d_a_in, a_out=d_a_out, b_in=jnp.concatenate(d_b_in, axis=1), b_out=d_b_out,
                 wg=(d_wg0, d_wg1), wu=(d_wu0, d_wu1), wd=(d_wd0, d_wd1))
    vecs = dict(norm_mix=(d_nm0, d_nm1), norm_ffn=(d_nf0, d_nf1), final=d_final, loss_cols=loss_cols, sink=d_sink)
    return gx.reshape(x.shape), grads, vecs


ANY = pl.BlockSpec(memory_space=pl.ANY)
HBM = pltpu.MemorySpace.HBM


def _me():
    return lax.axis_index("x"), lax.axis_index("y"), lax.axis_index("c")


def _chip_peer(x, y, j):
    px = 1 - x if j & 2 else x
    py = 1 - y if j & 1 else y
    return px, py, 2 * px + py


def _remote(src, dst, sems, k, dev):
    return pltpu.make_async_remote_copy(src_ref=src, dst_ref=dst, send_sem=sems[0].at[k], recv_sem=sems[1].at[k],
                                        device_id=dev, device_id_type=MESH)


def _col_window(ref, q, width):
    return ref.at[:, pl.ds(pl.multiple_of(q * width, LANES), width)]


def _half0(ref, h):
    n = ref.shape[0] // 2
    return ref.at[pl.ds(h * n, n)]


def _half1(ref, h):
    n = ref.shape[1] // 2
    return ref.at[:, pl.ds(h * n, n)]


def _half_rows(ref, h):
    n = ref.shape[-2] // 2
    if len(ref.shape) == 2:
        return ref.at[pl.ds(h * n, n)]
    return ref.at[:, pl.ds(h * n, n)]


def _place_shard(w, layer, q_arr, col, name):
    _, rows, cols = w.shape

    def body(q_ref, w_ref, o_ref):
        o_ref[...] = w_ref[...].astype(BF16)

    if col:
        out_spec = pl.BlockSpec((rows, cols), lambda l, q: (0, q[0]))
        out_shape = jax.ShapeDtypeStruct((rows, N_CHIPS * cols), BF16)
    else:
        out_spec = pl.BlockSpec((None, None, rows, cols), lambda l, q: (q[0], 0, 0, 0))
        out_shape = jax.ShapeDtypeStruct((N_CHIPS, 1, rows, cols), BF16)
    return pl.pallas_call(
        body, name=name,
        grid_spec=pltpu.PrefetchScalarGridSpec(
            num_scalar_prefetch=1, grid=(1,),
            in_specs=[pl.BlockSpec((None, rows, cols), lambda l, q: (layer, 0, 0))], out_specs=out_spec),
        out_shape=out_shape, compiler_params=_cp(),
    )(q_arr, w)


def _handshake(peers):
    barrier = pltpu.get_barrier_semaphore()
    for p in peers:
        pl.semaphore_signal(barrier, inc=1, device_id=p, device_id_type=MESH)
    pl.semaphore_wait(barrier, len(peers))


def _on_sequencer(name, collective_id, n_sem, n_local, body):
    @pl.kernel(mesh=plsc.ScalarSubcoreMesh(axis_name="seq", num_cores=1), name=name,
               scratch_types=(pltpu.SemaphoreType.DMA((n_sem,)), pltpu.SemaphoreType.DMA((n_sem,)),
                              pltpu.SemaphoreType.DMA((max(n_local, 1),))),
               compiler_params=pltpu.CompilerParams(collective_id=collective_id))
    def launch(send_sems, recv_sems, local_sems):
        body((send_sems, recv_sems), local_sems)

    launch()


def _gather_plan(outs, col_fam, sems, handshake):
    n_w = len(outs)
    x, y, c = _me()
    myq = 2 * x + y
    sib = (x, y, 1 - c)
    if handshake:
        _handshake([sib] + [_chip_peer(x, y, j)[:2] + (c,) for j in (1, 2, 3)])

    def slot(w, q):
        if col_fam[w]:
            return _col_window(outs[w], q, outs[w].shape[1] // N_CHIPS)
        return outs[w].at[q]

    first = []
    for w in range(n_w):
        for j in (1, 2, 3):
            px, py, _ = _chip_peer(x, y, j)
            mine = _half_rows(slot(w, myq), c)
            cp = _remote(mine, mine, sems, w * 6 + j - 1, (px, py, c))
            cp.start()
            first.append(cp)
    passed = []
    for w in range(n_w):
        for j in (1, 2, 3):
            _, _, pq = _chip_peer(x, y, j)
            land = _half_rows(slot(w, pq), c)
            _remote(land, land, sems, w * 6 + j - 1, sib).wait_recv()
            cp = _remote(land, land, sems, w * 6 + 2 + j, sib)
            cp.start()
            passed.append(cp)
    for w in range(n_w):
        for j in (1, 2, 3):
            _, _, pq = _chip_peer(x, y, j)
            land = _half_rows(slot(w, pq), 1 - c)
            _remote(land, land, sems, w * 6 + 2 + j, sib).wait_recv()
    for cp in first + passed:
        cp.wait_send()


def _gather_weights(bufs, col_fam):
    n_w = len(bufs)

    def body(*refs):
        _gather_plan(refs[n_w:2 * n_w], col_fam, refs[2 * n_w:2 * n_w + 2], False)

    return pl.pallas_call(
        body, name="gather_weights", in_specs=[ANY] * n_w, out_specs=[ANY] * n_w,
        out_shape=[jax.ShapeDtypeStruct(b.shape, b.dtype) for b in bufs],
        input_output_aliases={w: w for w in range(n_w)},
        scratch_shapes=[pltpu.SemaphoreType.DMA((6 * n_w,)), pltpu.SemaphoreType.DMA((6 * n_w,))],
    )(*bufs)


def _gather_weights_async(bufs, col_fam, name, collective_id):
    refs = [jax.new_ref(b, memory_space=HBM) for b in bufs]
    _on_sequencer(name, collective_id, 6 * len(bufs), 0,
                  lambda sems, _: _gather_plan(refs, col_fam, sems, True))
    return [r[...] for r in refs]


def _grad_half(ref, col, h):
    return _half0(ref, h) if col else _half1(ref, h)


def _swap_halves_with_sibling(grads, col_fam):
    n_w = len(grads)

    def body(*refs):
        _swap_plan(refs[:n_w], refs[n_w:2 * n_w], col_fam, refs[2 * n_w:], False)

    return pl.pallas_call(
        body, name="grad_swap_sibling", in_specs=[ANY] * n_w, out_specs=[ANY] * n_w,
        out_shape=_swap_shapes(grads, col_fam),
        scratch_shapes=[pltpu.SemaphoreType.DMA((n_w,)), pltpu.SemaphoreType.DMA((n_w,))],
    )(*grads)


def _swap_shapes(grads, col_fam):
    out = []
    for w, g in enumerate(grads):
        shp = (g.shape[0] // 2, g.shape[1]) if col_fam[w] else (g.shape[0], g.shape[1] // 2, g.shape[2])
        out.append(jax.ShapeDtypeStruct(shp, g.dtype))
    return out


def _swap_plan(ins, outs, col_fam, sems, handshake):
    x, y, c = _me()
    sib = (x, y, 1 - c)
    if handshake:
        _handshake([sib])
    cps = [_remote(_grad_half(ins[w], col_fam[w], 1 - c), outs[w], sems, w, sib) for w in range(len(ins))]
    for cp in cps:
        cp.start()
    for cp in cps:
        cp.wait_recv()
    for cp in cps:
        cp.wait_send()


def _swap_halves_async(grads, col_fam, name, collective_id):
    srcs = [jax.new_ref(g, memory_space=HBM) for g in grads]
    dsts = [jax.empty_ref(s, memory_space=HBM) for s in _swap_shapes(grads, col_fam)]
    _on_sequencer(name, collective_id, len(grads), 0, lambda sems, _: _swap_plan(srcs, dsts, col_fam, sems, True))
    return [r[...] for r in srcs], [r[...] for r in dsts]


def _half_add(mines, recvs, c_arr, col_fam, name):
    n_w = len(mines)
    mine_specs, recv_specs = [], []
    for recv, col in zip(recvs, col_fam):
        if col:
            rows, n = recv.shape
            tr = rows // N_CHIPS
            mine_specs.append(pl.BlockSpec((tr, n), lambda i, c: (N_CHIPS * c[0] + i, 0)))
            recv_specs.append(pl.BlockSpec((tr, n), lambda i, c: (i, 0)))
        else:
            _, rows, n = recv.shape
            mine_specs.append(pl.BlockSpec((None, rows, n), lambda q, c: (q, c[0], 0)))
            recv_specs.append(pl.BlockSpec((None, rows, n), lambda q, c: (q, 0, 0)))

    def body(c_ref, *refs):
        for a_ref, b_ref, o_ref in zip(refs[:n_w], refs[n_w:2 * n_w], refs[2 * n_w:]):
            o_ref[...] = (a_ref[...].astype(F32) + b_ref[...].astype(F32)).astype(BF16)

    return pl.pallas_call(
        body, name=name,
        grid_spec=pltpu.PrefetchScalarGridSpec(num_scalar_prefetch=1, grid=(N_CHIPS,),
                                               in_specs=mine_specs + recv_specs, out_specs=recv_specs),
        out_shape=[jax.ShapeDtypeStruct(r.shape, BF16) for r in recvs], compiler_params=_cp(),
    )(c_arr, *mines, *recvs)


def _scatter_chip_sums(sums, col_fam):
    n_w = len(sums)

    def body(*refs):
        _scatter_plan(refs[:n_w], refs[n_w:2 * n_w], col_fam, refs[2 * n_w:2 * n_w + 2], refs[2 * n_w + 2], False)

    return pl.pallas_call(
        body, name="grad_scatter_chips", in_specs=[ANY] * n_w, out_specs=[ANY] * n_w,
        out_shape=_scatter_shapes(sums, col_fam),
        scratch_shapes=[pltpu.SemaphoreType.DMA((3 * n_w,)), pltpu.SemaphoreType.DMA((3 * n_w,)),
                        pltpu.SemaphoreType.DMA((n_w,))],
    )(*sums)


def _scatter_shapes(sums, col_fam):
    out = []
    for w, s in enumerate(sums):
        shp = (s.shape[0], s.shape[1] // N_CHIPS) if col_fam[w] else s.shape[1:]
        out.append(jax.ShapeDtypeStruct((N_CHIPS,) + shp, s.dtype))
    return out


def _scatter_plan(ins, outs, col_fam, sems, lsem, handshake):
    n_w = len(ins)
    x, y, c = _me()
    myq = 2 * x + y
    if handshake:
        _handshake([_chip_peer(x, y, j)[:2] + (c,) for j in (1, 2, 3)])

    def slab(w, q):
        if col_fam[w]:
            return _col_window(ins[w], q, ins[w].shape[1] // N_CHIPS)
        return ins[w].at[q]

    local = [pltpu.make_async_copy(slab(w, myq), outs[w].at[myq], lsem.at[w]) for w in range(n_w)]
    for cp in local:
        cp.start()
    cps = []
    for w in range(n_w):
        for j in (1, 2, 3):
            px, py, pq = _chip_peer(x, y, j)
            cp = _remote(slab(w, pq), outs[w].at[myq], sems, w * 3 + j - 1, (px, py, c))
            cp.start()
            cps.append(cp)
    for w in range(n_w):
        for j in (1, 2, 3):
            _, _, pq = _chip_peer(x, y, j)
            land = outs[w].at[pq]
            _remote(land, land, sems, w * 3 + j - 1, (x, y, c)).wait_recv()
    for cp in cps:
        cp.wait_send()
    for cp in local:
        cp.wait()


def _scatter_chip_sums_async(sums, col_fam, name, collective_id):
    srcs = [jax.new_ref(s, memory_space=HBM) for s in sums]
    dsts = [jax.empty_ref(s, memory_space=HBM) for s in _scatter_shapes(sums, col_fam)]
    _on_sequencer(name, collective_id, 3 * len(sums), len(sums),
                  lambda sems, lsem: _scatter_plan(srcs, dsts, col_fam, sems, lsem, True))
    return [r[...] for r in dsts]


def _sum_chips(parts, c_arr, prev, lead, shape, name):
    _, rows, n = parts.shape
    tr = rows // 2 if rows % 32 == 0 else rows
    nblk = rows // tr

    def body(c_ref, p_ref, *rest):
        o_ref = rest[-1]
        acc = p_ref[0].astype(F32)
        for q in range(1, N_CHIPS):
            acc = acc + p_ref[q].astype(F32)
        o_ref[...] = acc

    in_specs = [pl.BlockSpec((N_CHIPS, tr, n), lambda i, c: (0, i, 0))]
    args = [c_arr, parts]
    aliases = {}
    if prev is not None:
        in_specs.append(ANY)
        args.append(prev)
        aliases = {2: 0}
    return pl.pallas_call(
        body, name=name,
        grid_spec=pltpu.PrefetchScalarGridSpec(
            num_scalar_prefetch=1, grid=(nblk,), in_specs=in_specs,
            out_specs=pl.BlockSpec((None, tr, n), lambda i, c: (lead, c[0] * nblk + i, 0))),
        out_shape=jax.ShapeDtypeStruct(shape, F32), input_output_aliases=aliases, compiler_params=_cp(),
    )(*args)


def _join_plan(outs, place, sems, handshake):
    x, y, c = _me()
    sib = (x, y, 1 - c)
    if handshake:
        _handshake([sib])

    def half(k, h):
        o, lead = place[k]
        return _half_rows(outs[o].at[lead], h)

    cps = [_remote(half(k, c), half(k, c), sems, k, sib) for k in range(len(place))]
    for cp in cps:
        cp.start()
    for k in range(len(place)):
        land = half(k, 1 - c)
        _remote(land, land, sems, k, sib).wait_recv()
    for cp in cps:
        cp.wait_send()


def _join_halves(bufs, place, name):
    n_o = len(bufs)
    n_h = len(place)

    def body(*refs):
        _join_plan(refs[n_o:2 * n_o], place, refs[2 * n_o:2 * n_o + 2], False)

    return pl.pallas_call(
        body, name=name, in_specs=[ANY] * n_o, out_specs=[ANY] * n_o,
        out_shape=[jax.ShapeDtypeStruct(b.shape, b.dtype) for b in bufs],
        input_output_aliases={k: k for k in range(n_o)},
        scratch_shapes=[pltpu.SemaphoreType.DMA((n_h,)), pltpu.SemaphoreType.DMA((n_h,))],
    )(*bufs)


def _allreduce_rows(rows):
    n_dev = 8
    n_r = len(rows)
    assert n_r <= 8

    def body(*refs):
        r_refs = refs[:n_r]
        o_ref, slots, send_sems, recv_sems = refs[n_r:]
        x, y, c = _me()
        me = 4 * x + 2 * y + c
        slots[me] = jnp.concatenate([r[...] for r in r_refs] + [jnp.zeros((8 - n_r, D_MODEL), F32)], axis=0)

        def peer(k):
            return (1 - x if k & 4 else x, 1 - y if k & 2 else y, 1 - c if k & 1 else c)

        cps = []
        for k in range(1, n_dev):
            cp = pltpu.make_async_remote_copy(src_ref=slots.at[me], dst_ref=slots.at[me], send_sem=send_sems.at[k - 1],
                                              recv_sem=recv_sems.at[k - 1], device_id=peer(k), device_id_type=MESH)
            cp.start()
            cps.append(cp)
        for k in range(1, n_dev):
            px, py, pc = peer(k)
            land = slots.at[4 * px + 2 * py + pc]
            pltpu.make_async_remote_copy(src_ref=land, dst_ref=land, send_sem=send_sems.at[k - 1],
                                         recv_sem=recv_sems.at[k - 1], device_id=peer(k),
                                         device_id_type=MESH).wait_recv()
        for cp in cps:
            cp.wait_send()
        acc = slots[0]
        for d in range(1, n_dev):
            acc = acc + slots[d]
        o_ref[...] = acc

    vm = pl.BlockSpec(memory_space=pltpu.VMEM)
    return pl.pallas_call(
        body, name="allreduce_rows", in_specs=[vm] * n_r, out_specs=vm,
        out_shape=jax.ShapeDtypeStruct((8, D_MODEL), F32),
        scratch_shapes=[pltpu.VMEM((n_dev, 8, D_MODEL), F32), pltpu.SemaphoreType.DMA((n_dev - 1,)),
                        pltpu.SemaphoreType.DMA((n_dev - 1,))],
    )(*rows)


def _adamw(w, g, m, v, name):
    shape = w.shape
    if len(shape) == 1:
        lead, rows, cols = 1, 1, shape[0]
    else:
        rows, cols = shape[-2:]
        lead = math.prod(shape[:-2])
    args = [a.reshape(lead, rows, cols) for a in (w, g, m, v)]
    tr = rows // 2 if rows % 16 == 0 else rows

    def body(w_ref, g_ref, m_ref, v_ref, d_ref, nm_ref, nv_ref):
        gv = g_ref[...]
        nm = ADAM_B1 * m_ref[...] + (1.0 - ADAM_B1) * gv
        nv = ADAM_B2 * v_ref[...] + (1.0 - ADAM_B2) * jnp.square(gv)
        m_hat = nm / (1.0 - ADAM_B1 ** ADAM_STEP)
        v_hat = nv / (1.0 - ADAM_B2 ** ADAM_STEP)
        d_ref[...] = -ADAM_LR * (m_hat / (jnp.sqrt(v_hat) + ADAM_EPS) + ADAM_WD * w_ref[...])
        nm_ref[...] = nm
        nv_ref[...] = nv

    spec = pl.BlockSpec((None, tr, cols), lambda l, i: (l, i, 0))
    outs = pl.pallas_call(
        body, name=name, grid=(lead, rows // tr), in_specs=[spec] * 4, out_specs=[spec] * 3,
        out_shape=[jax.ShapeDtypeStruct((lead, rows, cols), F32)] * 3, compiler_params=_cp(),
    )(*args)
    return [o.reshape(shape) for o in outs]


def kernel(x, a_w_in, a_sink, a_w_out, b_w_in, b_w_out, norm_mix, norm_ffn, w_gate, w_up, w_down, final_norm, loss_target, m_a_w_in, m_a_sink, m_a_w_out, m_b_w_in, m_b_w_out, m_norm_mix, m_norm_ffn, m_w_gate, m_w_up, m_w_down, m_final_norm, v_a_w_in, v_a_sink, v_a_w_out, v_b_w_in, v_b_w_out, v_norm_mix, v_norm_ffn, v_w_gate, v_w_up, v_w_down, v_final_norm):
    weights = dict(a_w_in=a_w_in, a_sink=a_sink, a_w_out=a_w_out, b_w_in=b_w_in, b_w_out=b_w_out, norm_mix=norm_mix,
                   norm_ffn=norm_ffn, w_gate=w_gate, w_up=w_up, w_down=w_down, final_norm=final_norm)
    mom = dict(a_w_in=m_a_w_in, a_sink=m_a_sink, a_w_out=m_a_w_out, b_w_in=m_b_w_in, b_w_out=m_b_w_out,
               norm_mix=m_norm_mix, norm_ffn=m_norm_ffn, w_gate=m_w_gate, w_up=m_w_up, w_down=m_w_down,
               final_norm=m_final_norm)
    var = dict(a_w_in=v_a_w_in, a_sink=v_a_sink, a_w_out=v_a_w_out, b_w_in=v_b_w_in, b_w_out=v_b_w_out,
               norm_mix=v_norm_mix, norm_ffn=v_norm_ffn, w_gate=v_w_gate, w_up=v_w_up, w_down=v_w_down,
               final_norm=v_final_norm)
    order = ["a_w_in", "a_sink", "a_w_out", "b_w_in", "b_w_out", "norm_mix", "norm_ffn", "w_gate", "w_up", "w_down",
             "final_norm"]
    swapped = ("w_gate", "w_up")
    for n in swapped:
        weights[n], mom[n], var[n] = (a.transpose(0, 2, 1) for a in (weights[n], mom[n], var[n]))
    w_gate_t, w_up_t = weights["w_gate"], weights["w_up"]

    c_arr = lax.axis_index("c").astype(jnp.int32).reshape(1)
    q_arr = (2 * lax.axis_index("x") + lax.axis_index("y")).astype(jnp.int32).reshape(1)

    def placed(w, layer, col, nm):
        return _place_shard(w, layer, q_arr, col, f"place_{nm}")

    (a_in,) = _gather_weights_async([placed(a_w_in, 0, True, "a_in")], (True,), "gather_weights_first", 6)
    a_out, wg0, wu0, wd0 = _gather_weights_async(
        [placed(a_w_out, 0, False, "a_out"), placed(w_gate_t, 0, False, "wg0"), placed(w_up_t, 0, False, "wu0"),
         placed(w_down, 0, False, "wd0")], (False,) * 4, "gather_weights_layer0", 1)
    b_in, b_out, wg1, wu1, wd1 = _gather_weights_async(
        [placed(b_w_in, 0, True, "b_in"), placed(b_w_out, 0, False, "b_out"), placed(w_gate_t, 1, False, "wg1"),
         placed(w_up_t, 1, False, "wu1"), placed(w_down, 1, False, "wd1")], (True,) + (False,) * 4,
        "gather_weights_layer1", 7)
    a_out = a_out.reshape(D_MODEL, D_MODEL)
    b_out = b_out.reshape(D_MODEL, D_MODEL)
    wg, wu, wd = (wg0, wg1), (wu0, wu1), (wd0, wd1)

    gx, grads, vecs = _local_step(x, loss_target, a_in, a_sink[0], a_out, b_in, b_out, norm_mix, norm_ffn, wg, wu, wd,
                                  final_norm)

    rows_out = D_MODEL // N_CHIPS
    partials = [grads["a_in"], grads["b_in"],
                grads["a_out"].reshape(N_CHIPS, rows_out, D_MODEL), grads["b_out"].reshape(N_CHIPS, rows_out, D_MODEL),
                grads["wg"][0], grads["wg"][1], grads["wu"][0], grads["wu"][1], grads["wd"][0], grads["wd"][1]]
    col_fam = (True, True) + (False,) * 8
    names = ("a_in", "b_in", "a_out", "b_out", "wg0", "wg1", "wu0", "wu1", "wd0", "wd1")
    contrib = [None] * len(partials)

    def reduce_group(idx, tag, ids):
        parts = [partials[k] for k in idx]
        cols = tuple(col_fam[k] for k in idx)
        if ids is None:
            theirs = _swap_halves_with_sibling(parts, cols)
        else:
            parts, theirs = _swap_halves_async(parts, cols, f"grad_swap_{tag}", ids[0])
        sums = _half_add(parts, theirs, c_arr, cols, f"chip_sum_{tag}")
        if ids is None:
            out = _scatter_chip_sums(sums, cols)
        else:
            out = _scatter_chip_sums_async(sums, cols, f"grad_scatter_{tag}", ids[1])
        for k, o in zip(idx, out):
            contrib[k] = o

    reduce_group([1, 3, 5, 7, 9], "layer1", (2, 3))
    reduce_group([2, 4, 6, 8], "ffn0", (4, 5))
    reduce_group([0], "a_in", (8, 9))
    shapes = [a_w_in.shape, b_w_in.shape, a_w_out.shape, b_w_out.shape, w_down.shape, w_down.shape, w_down.shape]
    place = [(0, 0), (1, 0), (2, 0), (3, 0), (4, 0), (4, 1), (5, 0), (5, 1), (6, 0), (6, 1)]
    bufs = [None] * len(shapes)
    for p, nm, (o, lead) in zip(contrib, names, place):
        bufs[o] = _sum_chips(p, c_arr, bufs[o], lead, shapes[o], f"sum_chips_{nm}")
    g_a_in, g_b_in, g_a_out, g_b_out, g_wg, g_wu, g_wd = _join_halves(bufs, place, "grad_join_sibling")

    sink_row = jnp.pad(vecs["sink"][0:1], ((0, 0), (0, D_MODEL - LANES)))
    tot = _allreduce_rows([vecs["norm_mix"][0], vecs["norm_mix"][1], vecs["norm_ffn"][0], vecs["norm_ffn"][1],
                           vecs["final"], vecs["loss_cols"], sink_row])
    loss = (0.5 / D_MODEL) * jnp.sum(tot[5])
    gw = dict(a_w_in=g_a_in, a_sink=tot[6:7, :N_HEADS], a_w_out=g_a_out, b_w_in=g_b_in, b_w_out=g_b_out,
              norm_mix=tot[0:2], norm_ffn=tot[2:4], w_gate=g_wg, w_up=g_wu, w_down=g_wd, final_norm=tot[4])

    delta, new_m, new_v = {}, {}, {}
    for n in order:
        delta[n], new_m[n], new_v[n] = _adamw(weights[n], gw[n], mom[n], var[n], f"adamw_{n}")
    for n in swapped:
        gw[n], delta[n], new_m[n], new_v[n] = (a.transpose(0, 2, 1) for a in (gw[n], delta[n], new_m[n], new_v[n]))
    return (loss, gx, *[gw[n] for n in order], *[delta[n] for n in order], *[new_m[n] for n in order],
            *[new_v[n] for n in order])
```

```python
import math

import jax
import jax.numpy as jnp
from jax import lax
from jax.experimental import pallas as pl
from jax.experimental.pallas import tpu as pltpu
from jax.experimental.pallas import tpu_sc as plsc

F32 = jnp.float32
BF16 = jnp.bfloat16

D_MODEL = 1024
HEAD_DIM = 64
N_HEADS = 16
N_KV = 4
QKV_W = 1536
D_FF = 2816
N_CHIPS = 4
FF_SH = D_FF // N_CHIPS
HALF_WINDOW_A = 128
DILATED = ((128, 1), (512, 4), (2048, 16))
ROPE_THETA = 10000.0
RMS_EPS = 1e-6
NEG_INF = -1e30
LANES = 128
ADAM_LR, ADAM_B1, ADAM_B2, ADAM_EPS, ADAM_WD, ADAM_STEP = 0.001, 0.9, 0.999, 1e-08, 0.01, 10
VMEM_LIMIT = 56 * 1024 * 1024
ROWS = 512
MATMUL_ROWS = 1024
FFN_BWD_ROWS = 256
LOG2E = math.log2(math.e)
LN2 = math.log(2.0)
Q_SCALE = LOG2E / math.sqrt(HEAD_DIM)
GRAD_TOKENS = 2048
MESH = pl.DeviceIdType.MESH


def _cp(**kw):
    return pltpu.CompilerParams(vmem_limit_bytes=VMEM_LIMIT, **kw)


def _row_tile(t, cap):
    tm = min(cap, t)
    assert t % tm == 0
    return tm


def _rope_tables(seq, dil):
    inv = 1.0 / (ROPE_THETA ** (jnp.arange(0, HEAD_DIM, 2, dtype=F32) / HEAD_DIM))
    ang = jnp.arange(seq, dtype=F32)[:, None] * inv[None, :]
    cos, sin = jnp.cos(ang), jnp.sin(ang)
    cos = jnp.tile(cos, (1, 4))
    sin = jnp.concatenate([-sin, sin, -sin, sin], axis=1)

    def perm(t):
        return t.reshape(seq // dil, dil, LANES).transpose(1, 0, 2).reshape(seq, LANES)

    return perm(cos), perm(sin)


def _swap_halves(t):
    lane = lax.broadcasted_iota(jnp.int32, t.shape, 1)
    return jnp.where((lane % HEAD_DIM) < HEAD_DIM // 2, pltpu.roll(t, LANES - 32, 1), pltpu.roll(t, 32, 1))


def _rope(t, cos, sin):
    return t * cos + _swap_halves(t) * sin


def _rope_t(t, cos, sin):
    return t * cos - _swap_halves(t) * sin


def _to_residue(t, batch, dil):
    if dil == 1:
        return t
    s = t.shape[0] // batch
    return t.reshape(batch, s // dil, dil, t.shape[1]).transpose(0, 2, 1, 3).reshape(t.shape)


def _from_residue(t, batch, dil):
    if dil == 1:
        return t
    s = t.shape[0] // batch
    return t.reshape(batch, dil, s // dil, t.shape[1]).transpose(0, 2, 1, 3).reshape(t.shape)


def _rms_fwd(x, w, name):
    t = x.shape[0]
    tm = _row_tile(t, ROWS)

    def body(x_ref, w_ref, o_ref):
        o_ref[...] = _rms_tile(x_ref[...], w_ref[...]).astype(BF16)

    return pl.pallas_call(
        body, name=name, grid=(t // tm,),
        in_specs=[pl.BlockSpec((tm, D_MODEL), lambda i: (i, 0)), pl.BlockSpec((1, D_MODEL), lambda i: (0, 0))],
        out_specs=pl.BlockSpec((tm, D_MODEL), lambda i: (i, 0)),
        out_shape=jax.ShapeDtypeStruct((t, D_MODEL), BF16), compiler_params=_cp(),
    )(x, w)


def _rms_bwd_tile(xv, wv, dy, dres):
    r = lax.rsqrt(jnp.mean(xv * xv, axis=-1, keepdims=True) + RMS_EPS)
    xh = xv * r
    dxh = dy * wv
    dx = dres + r * (dxh - xh * jnp.mean(dxh * xh, axis=-1, keepdims=True))
    return dx, jnp.sum(dy * xh, axis=0, keepdims=True)


def _accumulate(ref, part):
    @pl.when(pl.program_id(0) == 0)
    def _():
        ref[...] = jnp.zeros_like(ref)

    ref[...] += part


def _rms_bwd(x, w, dhs, dres, name):
    t = x.shape[0]
    tm = _row_tile(t, ROWS)
    n = len(dhs)

    def body(*refs):
        x_ref, w_ref = refs[0], refs[1]
        dh_refs = refs[2:2 + n]
        dres_ref = refs[2 + n]
        dx_ref, dxb_ref, dw_ref = refs[3 + n:]
        dy = dh_refs[0][...].astype(F32)
        for k in range(1, n):
            dy = dy + dh_refs[k][...].astype(F32)
        dx, dw = _rms_bwd_tile(x_ref[...], w_ref[...], dy, dres_ref[...])
        dx_ref[...] = dx
        dxb_ref[...] = dx.astype(BF16)
        _accumulate(dw_ref, dw)

    row = pl.BlockSpec((tm, D_MODEL), lambda i: (i, 0))
    vec = pl.BlockSpec((1, D_MODEL), lambda i: (0, 0))
    return pl.pallas_call(
        body, name=name, grid=(t // tm,),
        in_specs=[row, vec] + [row] * n + [row],
        out_specs=[row, row, vec],
        out_shape=[jax.ShapeDtypeStruct((t, D_MODEL), F32), jax.ShapeDtypeStruct((t, D_MODEL), BF16),
                   jax.ShapeDtypeStruct((1, D_MODEL), F32)],
        compiler_params=_cp(),
    )(x, w, *dhs, dres)


def _final_tile(xv, wv, tv):
    r = lax.rsqrt(jnp.mean(xv * xv, axis=-1, keepdims=True) + RMS_EPS)
    xh = xv * r
    err = xh * wv - tv
    dy = err * (1.0 / D_MODEL)
    dxh = dy * wv
    dx = r * (dxh - xh * jnp.mean(dxh * xh, axis=-1, keepdims=True))
    return dx, jnp.sum(err * err, axis=0, keepdims=True), jnp.sum(dy * xh, axis=0, keepdims=True)


def _qkv_proj(h, w, cos, sin, group, name):
    t = h.shape[0]
    seq = cos.shape[0]
    tm = _row_tile(seq, MATMUL_ROWS)
    n_q = N_HEADS * HEAD_DIM // LANES
    n_rope = (N_HEADS + N_KV) * HEAD_DIM // LANES
    scale = Q_SCALE

    def body(h_ref, w_ref, cos_ref, sin_ref, o_ref):
        acc = jnp.dot(h_ref[...], w_ref[...], preferred_element_type=F32)
        cs, sn = cos_ref[...], sin_ref[...]
        csq, snq = cs * scale, sn * scale
        for c in range(QKV_W // LANES):
            blk = acc[:, c * LANES:(c + 1) * LANES]
            if c < n_q:
                blk = _rope(blk, csq, snq)
            elif c < n_rope:
                blk = _rope(blk, cs, sn)
            o_ref[:, c * LANES:(c + 1) * LANES] = blk.astype(BF16)

    tab = pl.BlockSpec((tm, LANES), lambda i: (i % (seq // tm), 0))
    return pl.pallas_call(
        body, name=name, grid=(t // tm,),
        in_specs=[pl.BlockSpec((tm, D_MODEL), lambda i: (i, 0)),
                  pl.BlockSpec((D_MODEL, QKV_W), lambda i: (0, group)), tab, tab],
        out_specs=pl.BlockSpec((tm, QKV_W), lambda i: (i, 0)),
        out_shape=jax.ShapeDtypeStruct((t, QKV_W), BF16), compiler_params=_cp(),
    )(h, w, cos, sin)


def _rms_tile(xv, wv):
    return (xv * lax.rsqrt(jnp.mean(xv * xv, axis=-1, keepdims=True) + RMS_EPS)) * wv


def _mm_res(a, w, res, nw, name):
    t, k = a.shape
    tm = _row_tile(t, ROWS)

    def body(a_ref, w_ref, r_ref, nw_ref, o_ref, h_ref):
        xv = r_ref[...] + jnp.dot(a_ref[...], w_ref[...], preferred_element_type=F32)
        o_ref[...] = xv
        h_ref[...] = _rms_tile(xv, nw_ref[...]).astype(BF16)

    row = pl.BlockSpec((tm, D_MODEL), lambda i: (i, 0))
    return pl.pallas_call(
        body, name=name, grid=(t // tm,),
        in_specs=[pl.BlockSpec((tm, k), lambda i: (i, 0)),
                  pl.BlockSpec((k, D_MODEL), lambda i: (0, 0), pipeline_mode=pl.Buffered(1)), row,
                  pl.BlockSpec((1, D_MODEL), lambda i: (0, 0))],
        out_specs=[row, row],
        out_shape=[jax.ShapeDtypeStruct((t, D_MODEL), F32), jax.ShapeDtypeStruct((t, D_MODEL), BF16)],
        compiler_params=_cp(),
    )(a, w, res, nw)


def _mm_nt(dy, w, group, out_dtype, name):
    t, n = dy.shape
    k = w.shape[0]
    tm = _row_tile(t, MATMUL_ROWS)

    def body(dy_ref, w_ref, o_ref):
        o_ref[...] = lax.dot_general(dy_ref[...], w_ref[...], (((1,), (1,)), ((), ())),
                                     preferred_element_type=F32).astype(out_dtype)

    return pl.pallas_call(
        body, name=name, grid=(t // tm,),
        in_specs=[pl.BlockSpec((tm, n), lambda i: (i, 0)), pl.BlockSpec((k, n), lambda i: (0, group))],
        out_specs=pl.BlockSpec((tm, k), lambda i: (i, 0)),
        out_shape=jax.ShapeDtypeStruct((t, k), out_dtype), compiler_params=_cp(),
    )(dy, w)


def _mm_nt_rms(dy, w, x, nw, dres, name):
    t, n = dy.shape
    tm = _row_tile(t, ROWS)

    def body(dy_ref, w_ref, x_ref, nw_ref, dres_ref, dx_ref, dw_ref):
        dh = lax.dot_general(dy_ref[...], w_ref[...], (((1,), (1,)), ((), ())), preferred_element_type=F32)
        dx, dw = _rms_bwd_tile(x_ref[...], nw_ref[...], dh, dres_ref[...])
        dx_ref[...] = dx
        _accumulate(dw_ref, dw)

    row = pl.BlockSpec((tm, D_MODEL), lambda i: (i, 0))
    vec = pl.BlockSpec((1, D_MODEL), lambda i: (0, 0))
    return pl.pallas_call(
        body, name=name, grid=(t // tm,),
        in_specs=[pl.BlockSpec((tm, n), lambda i: (i, 0)),
                  pl.BlockSpec((D_MODEL, n), lambda i: (0, 0), pipeline_mode=pl.Buffered(1)), row, vec, row],
        out_specs=[row, vec],
        out_shape=[jax.ShapeDtypeStruct((t, D_MODEL), F32), jax.ShapeDtypeStruct((1, D_MODEL), F32)],
        compiler_params=_cp(),
    )(dy, w, x, nw, dres)


def _out_bwd(dx, w, o, name):
    t = dx.shape[0]
    tm = _row_tile(t, ROWS)

    def body(dx_ref, w_ref, o_ref, et_ref, do_ref, adj_ref):
        do = lax.dot_general(dx_ref[...], w_ref[...], (((1,), (1,)), ((), ())), preferred_element_type=F32)
        do_ref[...] = do.astype(BF16)
        adj_ref[...] = -_dot_heads(do * o_ref[...].astype(F32), et_ref[...])

    row = pl.BlockSpec((tm, D_MODEL), lambda i: (i, 0))
    return pl.pallas_call(
        body, name=name, grid=(t // tm,),
        in_specs=[row, pl.BlockSpec((D_MODEL, D_MODEL), lambda i: (0, 0)), row,
                  pl.BlockSpec((D_MODEL, LANES), lambda i: (0, 0))],
        out_specs=[row, pl.BlockSpec((tm, LANES), lambda i: (i, 0))],
        out_shape=[jax.ShapeDtypeStruct((t, D_MODEL), BF16), jax.ShapeDtypeStruct((t, LANES), F32)],
        compiler_params=_cp(),
    )(dx, w, o, _head_expander().T)


def _mm_tn(a, bs, name):
    aq = a.ndim == 3
    bq = bs[0].ndim == 3
    t, ka = a.shape[-2:]
    n = bs[0].shape[-1]
    nq = N_CHIPS if (aq or bq) else 1
    tt = _row_tile(t, GRAD_TOKENS)
    tn = n if n <= 1024 else 768
    assert n % tn == 0
    nb = len(bs)
    steps = t // tt

    def body(*refs):
        a_ref = refs[0]
        b_refs = refs[1:1 + nb]
        o_refs = refs[1 + nb:1 + 2 * nb]
        acc_refs = refs[1 + 2 * nb:]
        s = pl.program_id(2)
        av = a_ref[...]
        for b_ref, o_ref, acc_ref in zip(b_refs, o_refs, acc_refs):
            @pl.when(s == 0)
            def _():
                acc_ref[...] = jnp.zeros_like(acc_ref)

            acc_ref[...] += lax.dot_general(av, b_ref[...], (((0,), (0,)), ((), ())), preferred_element_type=F32)

            @pl.when(s == steps - 1)
            def _():
                o_ref[...] = acc_ref[...].astype(BF16)

    a_spec = (pl.BlockSpec((None, tt, ka), lambda q, j, s: (q, s, 0)) if aq
              else pl.BlockSpec((tt, ka), lambda q, j, s: (s, 0)))
    b_spec = (pl.BlockSpec((None, tt, tn), lambda q, j, s: (q, s, j)) if bq
              else pl.BlockSpec((tt, tn), lambda q, j, s: (s, j)))
    if nq > 1:
        o_spec = pl.BlockSpec((None, ka, tn), lambda q, j, s: (q, 0, j))
        o_shape = jax.ShapeDtypeStruct((nq, ka, n), BF16)
    else:
        o_spec = pl.BlockSpec((ka, tn), lambda q, j, s: (0, j))
        o_shape = jax.ShapeDtypeStruct((ka, n), BF16)
    outs = pl.pallas_call(
        body, name=name, grid=(nq, n // tn, steps),
        in_specs=[a_spec] + [b_spec] * nb, out_specs=[o_spec] * nb, out_shape=[o_shape] * nb,
        scratch_shapes=[pltpu.VMEM((ka, tn), F32)] * nb, compiler_params=_cp(),
    )(a, *bs)
    return outs


def _sigmoid(x):
    return 1.0 / (1.0 + jnp.exp(-x))


def _ffn_up(h, wg, wu, layer, name):
    t = h.shape[0]
    tm = _row_tile(t, MATMUL_ROWS)
    nt = (((1,), (1,)), ((), ()))

    def body(h_ref, wg_ref, wu_ref, a_ref, dg_ref, du_ref):
        hv = h_ref[...]
        g = lax.dot_general(hv, wg_ref[...], nt, preferred_element_type=F32)
        u = lax.dot_general(hv, wu_ref[...], nt, preferred_element_type=F32)
        sg = _sigmoid(g)
        silu = g * sg
        a_ref[...] = (silu * u).astype(BF16)
        dg_ref[...] = (sg * (1.0 + g * (1.0 - sg)) * u).astype(BF16)
        du_ref[...] = silu.astype(BF16)

    wspec = pl.BlockSpec((None, None, FF_SH, D_MODEL), lambda q, i: (q, layer, 0, 0))
    ospec = pl.BlockSpec((None, tm, FF_SH), lambda q, i: (q, i, 0))
    oshape = jax.ShapeDtypeStruct((N_CHIPS, t, FF_SH), BF16)
    return pl.pallas_call(
        body, name=name, grid=(N_CHIPS, t // tm),
        in_specs=[pl.BlockSpec((tm, D_MODEL), lambda q, i: (i, 0)), wspec, wspec],
        out_specs=[ospec] * 3, out_shape=[oshape] * 3, compiler_params=_cp(),
    )(h, wg, wu)


def _ffn_down(a, wd, res, layer, name, norm_w=None, head=None):
    t = a.shape[1]
    tm = _row_tile(t, ROWS)
    resident = pl.BlockSpec((N_CHIPS, None, FF_SH, D_MODEL), lambda i: (0, layer, 0, 0), pipeline_mode=pl.Buffered(1))
    row = pl.BlockSpec((tm, D_MODEL), lambda i: (i, 0))
    vec = pl.BlockSpec((1, D_MODEL), lambda i: (0, 0))

    def hidden(a_ref, w_ref, r_ref):
        acc = r_ref[...]
        for q in range(N_CHIPS):
            acc = acc + jnp.dot(a_ref[q], w_ref[q], preferred_element_type=F32)
        return acc

    if head is None:
        def body(a_ref, w_ref, r_ref, nw_ref, o_ref, h_ref):
            xv = hidden(a_ref, w_ref, r_ref)
            o_ref[...] = xv
            h_ref[...] = _rms_tile(xv, nw_ref[...]).astype(BF16)

        return pl.pallas_call(
            body, name=name, grid=(t // tm,),
            in_specs=[pl.BlockSpec((N_CHIPS, tm, FF_SH), lambda i: (0, i, 0)), resident, row, vec],
            out_specs=[row, row],
            out_shape=[jax.ShapeDtypeStruct((t, D_MODEL), F32), jax.ShapeDtypeStruct((t, D_MODEL), BF16)],
            compiler_params=_cp(),
        )(a, wd, res, norm_w)

    def body(a_ref, w_ref, r_ref, nw_ref, t_ref, dx_ref, dxb_ref, l_ref, dw_ref):
        dx, sq, dw = _final_tile(hidden(a_ref, w_ref, r_ref), nw_ref[...], t_ref[...])
        dx_ref[...] = dx
        dxb_ref[...] = dx.astype(BF16)
        _accumulate(l_ref, sq)
        _accumulate(dw_ref, dw)

    return pl.pallas_call(
        body, name=name, grid=(t // tm,),
        in_specs=[pl.BlockSpec((N_CHIPS, tm, FF_SH), lambda i: (0, i, 0)), resident, row, vec, row],
        out_specs=[row, row, vec, vec],
        out_shape=[jax.ShapeDtypeStruct((t, D_MODEL), F32), jax.ShapeDtypeStruct((t, D_MODEL), BF16),
                   jax.ShapeDtypeStruct((1, D_MODEL), F32), jax.ShapeDtypeStruct((1, D_MODEL), F32)],
        compiler_params=_cp(),
    )(a, wd, res, *head)


def _ffn_bwd(dy, wd, wg, wu, fg, fu, x, nw, dres, name):
    t = dy.shape[0]
    tm = _row_tile(t, FFN_BWD_ROWS)
    nt = (((1,), (1,)), ((), ()))

    def body(dy_ref, wd_ref, wg_ref, wu_ref, fg_ref, fu_ref, x_ref, nw_ref, dres_ref,
             dg_ref, du_ref, dx_ref, dxb_ref, dw_ref):
        dyv = dy_ref[...]
        acc = jnp.zeros((tm, D_MODEL), F32)
        for q in range(N_CHIPS):
            da = lax.dot_general(dyv, wd_ref[q], nt, preferred_element_type=F32)
            dg = (da * fg_ref[q].astype(F32)).astype(BF16)
            du = (da * fu_ref[q].astype(F32)).astype(BF16)
            dg_ref[q] = dg
            du_ref[q] = du
            acc = acc + jnp.dot(dg, wg_ref[q], preferred_element_type=F32)
            acc = acc + jnp.dot(du, wu_ref[q], preferred_element_type=F32)
        dx, dw = _rms_bwd_tile(x_ref[...], nw_ref[...], acc, dres_ref[...])
        dx_ref[...] = dx
        dxb_ref[...] = dx.astype(BF16)
        _accumulate(dw_ref, dw)

    aspec = pl.BlockSpec((N_CHIPS, tm, FF_SH), lambda i: (0, i, 0))
    wspec = pl.BlockSpec((N_CHIPS, None, FF_SH, D_MODEL), lambda i: (0, 0, 0, 0), pipeline_mode=pl.Buffered(1))
    row = pl.BlockSpec((tm, D_MODEL), lambda i: (i, 0))
    vec = pl.BlockSpec((1, D_MODEL), lambda i: (0, 0))
    ashape = jax.ShapeDtypeStruct((N_CHIPS, t, FF_SH), BF16)
    return pl.pallas_call(
        body, name=name, grid=(t // tm,),
        in_specs=[row, wspec, wspec, wspec, aspec, aspec, row, vec, row],
        out_specs=[aspec, aspec, row, row, vec],
        out_shape=[ashape, ashape, jax.ShapeDtypeStruct((t, D_MODEL), F32), jax.ShapeDtypeStruct((t, D_MODEL), BF16),
                   jax.ShapeDtypeStruct((1, D_MODEL), F32)],
        compiler_params=_cp(),
    )(dy, wd, wg, wu, fg, fu, x, nw, dres)


def _attn_geometry(length, half_window):
    qb = min(LANES, length)
    kw = min(qb + 2 * half_window, length)
    return qb, kw, length // qb


def _dup_kv(src_ref, dst_ref, s, length):
    ch = min(length, 256)
    lo = lax.broadcasted_iota(jnp.int32, (ch, LANES), 1) < HEAD_DIM

    def chunk(c, carry):
        r0 = pl.multiple_of(c * ch, ch)
        for j in range(N_KV // 2):
            tile = src_ref[s, pl.ds(r0, ch), j * LANES:(j + 1) * LANES].astype(F32)
            rolled = pltpu.roll(tile, HEAD_DIM, 1)
            dst_ref[s, 2 * j, pl.ds(r0, ch), :] = jnp.where(lo, tile, rolled).astype(BF16)
            dst_ref[s, 2 * j + 1, pl.ds(r0, ch), :] = jnp.where(lo, rolled, tile).astype(BF16)
        return carry

    lax.fori_loop(0, length // ch, chunk, 0)


def _stack_heads(ref, s, q0, qb, g):
    lo = lax.broadcasted_iota(jnp.int32, (qb, LANES), 1) < HEAD_DIM
    parts = []
    for a in range(4):
        col = (2 * g + a // 2) * LANES
        tile = ref[s, pl.ds(q0, qb), col:col + LANES]
        keep = lo if a % 2 == 0 else jnp.logical_not(lo)
        parts.append(jnp.where(keep, tile, jnp.zeros_like(tile)))
    return jnp.concatenate(parts, axis=0)


def _unstack_pair_t(stacked_t, qb, pair):
    both = jnp.concatenate([stacked_t[:, (2 * pair) * qb:(2 * pair + 1) * qb],
                            stacked_t[:, (2 * pair + 1) * qb:(2 * pair + 2) * qb]], axis=0)
    return both.T


def _band_mask_t(q0, k0, qb, kw, half_window):
    key = lax.broadcasted_iota(jnp.int32, (kw, 4 * qb), 0)
    qry = lax.broadcasted_iota(jnp.int32, (kw, 4 * qb), 1) & (qb - 1)
    return jnp.abs((q0 + qry) - (k0 + key)) <= half_window


def _block_origin(i, qb, kw, half_window, length):
    if isinstance(i, int):
        return i * qb, min(max(i * qb - half_window, 0), length - kw)
    return (pl.multiple_of(i * qb, qb),
            pl.multiple_of(jnp.clip(i * qb - half_window, 0, length - kw), HEAD_DIM))


def _head_row(vals, qb):
    return jnp.concatenate([jnp.broadcast_to(v, (1, qb)).astype(F32) for v in vals], axis=1)


def _attn_fwd(qkv, sink, n_seq, length, half_window, seq_blk, out_dtype, name):
    qb, kw, nblk = _attn_geometry(length, half_window)
    with_sink = sink is not None
    nt = (((1,), (1,)), ((), ()))
    tn = (((0,), (0,)), ((), ()))
    qkv3 = qkv.reshape(n_seq, length, QKV_W)

    def body(*refs):
        refs = list(refs)
        sink_ref = refs.pop(0) if with_sink else None
        q_ref, k_ref, v_ref, o_ref, lse_ref, rden_ref, e_ref = refs[:7]
        kx_ref, vx_ref = refs[-2:]
        head_row = lax.broadcasted_iota(jnp.int32, (N_HEADS, qb), 0)
        i = pl.program_id(1) if nblk > 1 else 0
        groups = range(N_KV)
        for s in range(seq_blk):
            def prepare():
                _dup_kv(k_ref, kx_ref, s, length)
                _dup_kv(v_ref, vx_ref, s, length)

            if nblk > 1:
                pl.when(i == 0)(prepare)
            else:
                prepare()
            q0, k0 = _block_origin(i, qb, kw, half_window, length)
            valid = _band_mask_t(q0, k0, qb, kw, half_window)
            lse_tile = jnp.zeros((N_HEADS, qb), F32)
            rden_tile = jnp.zeros((N_HEADS, qb), F32)
            sts = [lax.dot_general(kx_ref[s, g, pl.ds(k0, kw), :], _stack_heads(q_ref, s, q0, qb, g), nt,
                                   preferred_element_type=F32) for g in groups]
            sts = [jnp.where(valid, st, NEG_INF) for st in sts]
            ms = [jnp.max(st, axis=0, keepdims=True) for st in sts]
            if with_sink:
                sks = [_head_row([sink_ref[4 * g + a] * LOG2E for a in range(4)], qb) for g in groups]
                ms = [jnp.maximum(m, sk) for m, sk in zip(ms, sks)]
            es = [jnp.exp2(st - m) for st, m in zip(sts, ms)]
            dens = [jnp.sum(e, axis=0, keepdims=True) for e in es]
            ebs = [e.astype(BF16) for e in es]
            if with_sink:
                dens = [den + jnp.exp2(sk - m) for den, sk, m in zip(dens, sks, ms)]
            rdens = [1.0 / den for den in dens]
            ots = [lax.dot_general(vx_ref[s, g, pl.ds(k0, kw), 0:HEAD_DIM], ebs[g], tn,
                                   preferred_element_type=F32) * rdens[g] for g in groups]
            for g in groups:
                e_ref[s, g] = ebs[g]
                for pair in range(2):
                    col = (2 * g + pair) * LANES
                    o_ref[s, pl.ds(q0, qb), col:col + LANES] = _unstack_pair_t(ots[g], qb, pair).astype(out_dtype)
                lse = ms[g] * LN2 + jnp.log(dens[g])
                for a in range(4):
                    lse_tile = jnp.where(head_row == 4 * g + a, lse[:, a * qb:(a + 1) * qb], lse_tile)
                    rden_tile = jnp.where(head_row == 4 * g + a, rdens[g][:, a * qb:(a + 1) * qb], rden_tile)
            lse_ref[s, :, pl.ds(q0, qb)] = lse_tile
            rden_ref[s, :, pl.ds(q0, qb)] = rden_tile

    in_specs = [pl.BlockSpec((seq_blk, length, N_HEADS * HEAD_DIM), lambda n, i: (n, 0, 0)),
                pl.BlockSpec((seq_blk, length, N_KV * HEAD_DIM), lambda n, i: (n, 0, 4)),
                pl.BlockSpec((seq_blk, length, N_KV * HEAD_DIM), lambda n, i: (n, 0, 5))]
    args = [qkv3, qkv3, qkv3]
    if with_sink:
        in_specs.insert(0, pl.BlockSpec(memory_space=pltpu.SMEM))
        args.insert(0, sink)
    stat_spec = pl.BlockSpec((seq_blk, N_HEADS, length), lambda n, i: (n, 0, 0))
    stat_shape = jax.ShapeDtypeStruct((n_seq, N_HEADS, length), F32)
    out_specs = [pl.BlockSpec((seq_blk, length, D_MODEL), lambda n, i: (n, 0, 0)), stat_spec, stat_spec,
                 pl.BlockSpec((seq_blk, None, N_KV, kw, 4 * qb), lambda n, i: (n, i, 0, 0, 0))]
    out_shape = [jax.ShapeDtypeStruct((n_seq, length, D_MODEL), out_dtype), stat_shape, stat_shape,
                 jax.ShapeDtypeStruct((n_seq, nblk, N_KV, kw, 4 * qb), BF16)]
    o, lse, rden, e = pl.pallas_call(
        body, name=name, grid=(n_seq // seq_blk, nblk), in_specs=in_specs, out_specs=out_specs, out_shape=out_shape,
        scratch_shapes=[pltpu.VMEM((seq_blk, N_KV, length, LANES), BF16),
                        pltpu.VMEM((seq_blk, N_KV, length, LANES), BF16)],
        compiler_params=_cp(),
    )(*args)
    return o.reshape(n_seq * length, D_MODEL), lse, rden, e


def _attn_bwd(qkv, do, adj, lse, rden, e, sink, cos, sin, n_seq, length, half_window, seq_blk, dil, name):
    qb, kw, nblk = _attn_geometry(length, half_window)
    scale = 1.0 / math.sqrt(HEAD_DIM)
    with_sink = sink is not None
    nt = (((1,), (1,)), ((), ()))
    tn = (((0,), (0,)), ((), ()))
    qkv3 = qkv.reshape(n_seq, length, QKV_W)
    do3 = do.reshape(n_seq, length, D_MODEL)
    tabs = [t.reshape(dil, length, LANES) for t in (cos, sin)]
    tab_blocks = dil // seq_blk if dil >= seq_blk else 1

    def body(*refs):
        refs = list(refs)
        sink_ref = refs.pop(0) if with_sink else None
        q_ref, k_ref, v_ref, do_ref, aux_ref, lse_ref, rden_ref, e_ref, cos_ref, sin_ref, dqkv_ref = refs[:11]
        ds_ref = refs[11] if with_sink else None
        kx_ref, vx_ref, dkx_ref, dvx_ref = refs[-4:]
        lane = lax.broadcasted_iota(jnp.int32, (1, LANES), 1)
        i = pl.program_id(1) if nblk > 1 else 0
        groups = range(N_KV)
        if with_sink:
            @pl.when(jnp.logical_and(pl.program_id(0) == 0, i == 0))
            def _():
                ds_ref[...] = jnp.zeros_like(ds_ref)

        for s in range(seq_blk):
            ts = s % dil

            def prepare():
                _dup_kv(k_ref, kx_ref, s, length)
                _dup_kv(v_ref, vx_ref, s, length)
                dkx_ref[s] = jnp.zeros(dkx_ref.shape[1:], F32)
                dvx_ref[s] = jnp.zeros(dvx_ref.shape[1:], F32)

            if nblk > 1:
                pl.when(i == 0)(prepare)
            else:
                prepare()
            q0, k0 = _block_origin(i, qb, kw, half_window, length)
            cs = cos_ref[ts, pl.ds(q0, qb), :] * scale
            sn = sin_ref[ts, pl.ds(q0, qb), :] * scale
            adj_tile = aux_ref[s, :, pl.ds(q0, qb)]
            rden_tile = rden_ref[s, :, pl.ds(q0, qb)]
            qss = [_stack_heads(q_ref, s, q0, qb, g) for g in groups]
            doss = [_stack_heads(do_ref, s, q0, qb, g) for g in groups]
            dpts = [lax.dot_general(vx_ref[s, g, pl.ds(k0, kw), :], doss[g], nt, preferred_element_type=F32)
                    for g in groups]
            rdens = [_head_row([rden_tile[4 * g + a:4 * g + a + 1, :] for a in range(4)], qb) for g in groups]
            shifts = [_head_row([adj_tile[4 * g + a:4 * g + a + 1, :] for a in range(4)], qb) for g in groups]
            pts = [e_ref[s, g].astype(F32) * rdens[g] for g in groups]
            dsbs = [(pts[g] * (dpts[g] + shifts[g])).astype(BF16) for g in groups]
            pbs = [pt.astype(BF16) for pt in pts]
            if with_sink:
                lse_tile = lse_ref[s, :, pl.ds(q0, qb)]
                dsink = jnp.zeros((1, LANES), F32)
                for g in groups:
                    sk = _head_row([sink_ref[4 * g + a] for a in range(4)], qb)
                    lse = _head_row([lse_tile[4 * g + a:4 * g + a + 1, :] for a in range(4)], qb)
                    dsk = jnp.exp(sk - lse) * shifts[g]
                    for a in range(4):
                        tot = jnp.sum(dsk[:, a * qb:(a + 1) * qb], axis=1, keepdims=True)
                        dsink = dsink + jnp.where(lane == 4 * g + a, tot, 0.0)
                ds_ref[0:1, :] += dsink
            dqts = [lax.dot_general(kx_ref[s, g, pl.ds(k0, kw), 0:HEAD_DIM], dsbs[g], tn, preferred_element_type=F32)
                    for g in groups]
            for g in groups:
                for pair in range(2):
                    col = (2 * g + pair) * LANES
                    tile = _rope_t(_unstack_pair_t(dqts[g], qb, pair), cs, sn)
                    dqkv_ref[s, pl.ds(q0, qb), col:col + LANES] = tile.astype(BF16)
            for g in groups:
                dkx_ref[s, g, pl.ds(k0, kw), :] += jnp.dot(dsbs[g], qss[g], preferred_element_type=F32)
                dvx_ref[s, g, pl.ds(k0, kw), :] += jnp.dot(pbs[g], doss[g], preferred_element_type=F32)

            ch = min(length, 256)
            lo_c = lax.broadcasted_iota(jnp.int32, (ch, LANES), 1) < HEAD_DIM

            def fin(c, carry):
                r0 = pl.multiple_of(c * ch, ch)
                cs_k = cos_ref[ts, pl.ds(r0, ch), :]
                sn_k = sin_ref[ts, pl.ds(r0, ch), :]
                for j in range(N_KV // 2):
                    both = []
                    for acc_ref in (dkx_ref, dvx_ref):
                        t0 = acc_ref[s, 2 * j, pl.ds(r0, ch), :]
                        t1 = acc_ref[s, 2 * j + 1, pl.ds(r0, ch), :]
                        both.append(jnp.where(lo_c, t0, t1) + pltpu.roll(jnp.where(lo_c, t1, t0), HEAD_DIM, 1))
                    kcol = N_HEADS * HEAD_DIM + j * LANES
                    vcol = (N_HEADS + N_KV) * HEAD_DIM + j * LANES
                    dqkv_ref[s, pl.ds(r0, ch), kcol:kcol + LANES] = _rope_t(both[0] * LN2, cs_k, sn_k).astype(BF16)
                    dqkv_ref[s, pl.ds(r0, ch), vcol:vcol + LANES] = both[1].astype(BF16)
                return carry

            def finish():
                lax.fori_loop(0, length // ch, fin, 0)

            if nblk > 1:
                pl.when(i == nblk - 1)(finish)
            else:
                finish()

    once = dict(pipeline_mode=pl.Buffered(1))
    seq_map = lambda n, i: (n, 0, 0)
    tab_map = (lambda n, i: (n % tab_blocks, 0, 0)) if dil >= seq_blk else (lambda n, i: (0, 0, 0))
    tab_rows = min(seq_blk, dil)
    stat_spec = pl.BlockSpec((seq_blk, N_HEADS, length), seq_map)
    in_specs = [pl.BlockSpec((seq_blk, length, N_HEADS * HEAD_DIM), seq_map, **once),
                pl.BlockSpec((seq_blk, length, N_KV * HEAD_DIM), lambda n, i: (n, 0, 4), **once),
                pl.BlockSpec((seq_blk, length, N_KV * HEAD_DIM), lambda n, i: (n, 0, 5), **once),
                pl.BlockSpec((seq_blk, length, D_MODEL), seq_map, **once),
                stat_spec, stat_spec, stat_spec,
                pl.BlockSpec((seq_blk, None, N_KV, kw, 4 * qb), lambda n, i: (n, i, 0, 0, 0)),
                pl.BlockSpec((tab_rows, length, LANES), tab_map, **once),
                pl.BlockSpec((tab_rows, length, LANES), tab_map, **once)]
    args = [qkv3, qkv3, qkv3, do3, adj, lse, rden, e] + tabs
    if with_sink:
        in_specs.insert(0, pl.BlockSpec(memory_space=pltpu.SMEM))
        args.insert(0, sink)
    out_specs = [pl.BlockSpec((seq_blk, length, QKV_W), seq_map)]
    out_shape = [jax.ShapeDtypeStruct((n_seq, length, QKV_W), BF16)]
    if with_sink:
        out_specs.append(pl.BlockSpec((8, LANES), lambda n, i: (0, 0)))
        out_shape.append(jax.ShapeDtypeStruct((8, LANES), F32))
    outs = pl.pallas_call(
        body, name=name, grid=(n_seq // seq_blk, nblk), in_specs=in_specs, out_specs=out_specs, out_shape=out_shape,
        scratch_shapes=[pltpu.VMEM((seq_blk, N_KV, length, LANES), BF16), pltpu.VMEM((seq_blk, N_KV, length, LANES), BF16),
                        pltpu.VMEM((seq_blk, N_KV, length, LANES), F32), pltpu.VMEM((seq_blk, N_KV, length, LANES), F32)],
        compiler_params=_cp(),
    )(*args)
    dqkv = outs[0].reshape(n_seq * length, QKV_W)
    return (dqkv, outs[1]) if with_sink else (dqkv, None)


def _head_expander():
    h = jnp.arange(LANES)[:, None]
    l = jnp.arange(D_MODEL)[None, :]
    return (l // HEAD_DIM == h).astype(BF16)


def _dot_split(a, e):
    hi = a.astype(BF16)
    lo = (a - hi.astype(F32)).astype(BF16)
    return jnp.dot(hi, e, preferred_element_type=F32) + jnp.dot(lo, e, preferred_element_type=F32)


def _dot_heads(a, e):
    return jnp.dot(a.astype(BF16), e, preferred_element_type=F32)


def _mix_weights(lses):
    m = jnp.maximum(jnp.maximum(lses[0], lses[1]), lses[2])
    es = [jnp.exp(v - m) for v in lses]
    tot = es[0] + es[1] + es[2]
    return [e / tot for e in es]


def _mix_fwd(os_, lses, name):
    t = os_[0].shape[0]
    tm = _row_tile(t, ROWS)

    def body(o0, o1, o2, l0, l1, l2, e_ref, out_ref):
        wts = _mix_weights([l0[...], l1[...], l2[...]])
        acc = jnp.zeros((tm, D_MODEL), F32)
        for w, o_ref in zip(wts, (o0, o1, o2)):
            acc = acc + _dot_split(w, e_ref[...]) * o_ref[...]
        out_ref[...] = acc.astype(BF16)

    row = pl.BlockSpec((tm, D_MODEL), lambda i: (i, 0))
    lrow = pl.BlockSpec((tm, LANES), lambda i: (i, 0))
    return pl.pallas_call(
        body, name=name, grid=(t // tm,),
        in_specs=[row] * 3 + [lrow] * 3 + [pl.BlockSpec((LANES, D_MODEL), lambda i: (0, 0))],
        out_specs=row, out_shape=jax.ShapeDtypeStruct((t, D_MODEL), BF16), compiler_params=_cp(),
    )(*os_, *lses, _head_expander())


def _mix_bwd(dx, w_out, os_, lses, name):
    t = dx.shape[0]
    tm = _row_tile(t, ROWS)

    def body(d_ref, w_ref, o0, o1, o2, l0, l1, l2, e_ref, et_ref, do0, do1, do2, a0, a1, a2):
        wts = _mix_weights([l0[...], l1[...], l2[...]])
        dv = lax.dot_general(d_ref[...], w_ref[...], (((1,), (1,)), ((), ())), preferred_element_type=F32)
        cs = [_dot_heads(dv * o_ref[...], et_ref[...]) for o_ref in (o0, o1, o2)]
        mean_c = wts[0] * cs[0] + wts[1] * cs[1] + wts[2] * cs[2]
        for w, c, do_ref, a_ref in zip(wts, cs, (do0, do1, do2), (a0, a1, a2)):
            do_ref[...] = (_dot_heads(w, e_ref[...]) * dv).astype(BF16)
            a_ref[...] = w * (c - mean_c) - w * c

    row = pl.BlockSpec((tm, D_MODEL), lambda i: (i, 0))
    lrow = pl.BlockSpec((tm, LANES), lambda i: (i, 0))
    e = _head_expander()
    return pl.pallas_call(
        body, name=name, grid=(t // tm,),
        in_specs=[row, pl.BlockSpec((D_MODEL, D_MODEL), lambda i: (0, 0), pipeline_mode=pl.Buffered(1))]
        + [row] * 3 + [lrow] * 3 + [pl.BlockSpec((LANES, D_MODEL), lambda i: (0, 0)),
                                    pl.BlockSpec((D_MODEL, LANES), lambda i: (0, 0))],
        out_specs=[row] * 3 + [lrow] * 3,
        out_shape=[jax.ShapeDtypeStruct((t, D_MODEL), BF16)] * 3 + [jax.ShapeDtypeStruct((t, LANES), F32)] * 3,
        compiler_params=_cp(),
    )(dx, w_out, *os_, *lses, e, e.T)


def _stats_to_tokens(stat, batch, dil):
    n_seq, _, length = stat.shape
    t = stat.transpose(0, 2, 1).reshape(n_seq * length, N_HEADS)
    return _from_residue(jnp.pad(t, ((0, 0), (0, LANES - N_HEADS))), batch, dil)


def _stats_from_tokens(stat, batch, dil, n_seq, length):
    t = _to_residue(stat[:, :N_HEADS], batch, dil)
    return t.reshape(n_seq, length, N_HEADS).transpose(0, 2, 1)


def _group_geometry(batch, seq, dil, window):
    length = seq // dil
    n_seq = batch * dil
    seq_blk = max(1, min(dil, 1024 // length))
    return n_seq, length, (window // 2) // dil, seq_blk


def _local_step(x, target, a_in, a_sink, a_out, b_in, b_out, norm_mix, norm_ffn, wg, wu, wd, final_norm):
    batch, seq, _ = x.shape
    t = batch * seq
    x0 = x.reshape(t, D_MODEL)
    tgt = target.reshape(t, D_MODEL)
    tabs = {d: _rope_tables(seq, d) for _, d in DILATED}
    nm = [norm_mix[i:i + 1] for i in range(2)]
    nf = [norm_ffn[i:i + 1] for i in range(2)]

    h0 = _rms_fwd(x0, nm[0], "rms_mix0")
    qkv0 = _qkv_proj(h0, a_in, *tabs[1], 0, "qkv0")
    o0, *saved0 = _attn_fwd(qkv0, a_sink, batch, seq, HALF_WINDOW_A, 1, BF16, "attn0")
    x1, hf0 = _mm_res(o0, a_out, x0, nf[0], "out0")
    act0, g0, u0 = _ffn_up(hf0, wg[0], wu[0], 0, "ffn_up0")
    x2, h1 = _ffn_down(act0, wd[0], x1, 0, "ffn_down0", norm_w=nm[1])

    geo = [_group_geometry(batch, seq, d, w) for w, d in DILATED]
    h1g, qkv1, o1, lse1, lse1r = [], [], [], [], []
    for gi, (_, d) in enumerate(DILATED):
        n_seq, length, hw, sb = geo[gi]
        hp = _to_residue(h1, batch, d)
        pj = _qkv_proj(hp, b_in, *tabs[d], gi, f"qkv1_{gi}")
        o, lse, rden, e = _attn_fwd(pj, None, n_seq, length, hw, sb, BF16, f"attn1_{gi}")
        h1g.append(hp)
        qkv1.append(pj)
        o1.append(_from_residue(o, batch, d))
        lse1r.append((lse, rden, e))
        lse1.append(_stats_to_tokens(lse, batch, d))
    omix = _mix_fwd(o1, lse1, "mix")
    x3, hf1 = _mm_res(omix, b_out, x2, nf[1], "out1")
    act1, g1, u1 = _ffn_up(hf1, wg[1], wu[1], 0, "ffn_up1")
    dx4, dx4b, loss_cols, d_final = _ffn_down(act1, wd[1], x3, 0, "ffn_down1_loss",
                                                     head=(final_norm.reshape(1, D_MODEL), tgt))

    def ffn_bwd(dxo, dxob, x_mid, hf, g, u, act, layer):
        dg, du, dxm, dxmb, d_nf = _ffn_bwd(dxob, wd[layer], wg[layer], wu[layer], g, u, x_mid, nf[layer], dxo,
                                           f"ffn_bwd{layer}")
        (d_wd,) = _mm_tn(act, [dxob], f"grad_wd{layer}")
        (d_wgt,) = _mm_tn(dg, [hf], f"grad_wg{layer}")
        (d_wut,) = _mm_tn(du, [hf], f"grad_wu{layer}")
        return dxm, dxmb, d_nf, d_wgt, d_wut, d_wd

    dx3, dx3b, d_nf1, d_wg1, d_wu1, d_wd1 = ffn_bwd(dx4, dx4b, x3, hf1, g1, u1, act1, 1)

    (d_b_out,) = _mm_tn(omix, [dx3b], "grad_b_out")
    mb = _mix_bwd(dx3b, b_out, o1, lse1, "out1_mix_bwd")
    dh1, d_b_in = [], []
    for gi, (_, d) in enumerate(DILATED):
        n_seq, length, hw, sb = geo[gi]
        dog = _to_residue(mb[gi], batch, d)
        adj = _stats_from_tokens(mb[3 + gi], batch, d, n_seq, length)
        dpj, _ = _attn_bwd(qkv1[gi], dog, adj, *lse1r[gi], None, *tabs[d], n_seq, length, hw, sb, d, f"attn1_bwd{gi}")
        (dw,) = _mm_tn(h1g[gi], [dpj], f"grad_b_in{gi}")
        d_b_in.append(dw)
        dh1.append(_from_residue(_mm_nt(dpj, b_in, gi, BF16, f"qkv1_bwd{gi}"), batch, d))
    dx2, dx2b, d_nm1 = _rms_bwd(x2, nm[1], dh1, dx3, "rms_mix_bwd1")

    dx1, dx1b, d_nf0, d_wg0, d_wu0, d_wd0 = ffn_bwd(dx2, dx2b, x1, hf0, g0, u0, act0, 0)

    do0, adj0 = _out_bwd(dx1b, a_out, o0, "out0_bwd")
    (d_a_out,) = _mm_tn(o0, [dx1b], "grad_a_out")
    adj0 = _stats_from_tokens(adj0, batch, 1, batch, seq)
    dqkv0, d_sink = _attn_bwd(qkv0, do0, adj0, *saved0, a_sink, *tabs[1], batch, seq, HALF_WINDOW_A, 1, 1, "attn0_bwd")
    (d_a_in,) = _mm_tn(h0, [dqkv0], "grad_a_in")
    gx, d_nm0 = _mm_nt_rms(dqkv0, a_in, x0, nm[0], dx1, "qkv0_bwd")

    grads = dict(a_in=d_a_in, a_out=d_a_out, b_in=jnp.concatenate(d_b_in, axis=1), b_out=d_b_out,
                 wg=(d_wg0, d_wg1), wu=(d_wu0, d_wu1), wd=(d_wd0, d_wd1))
    vecs = dict(norm_mix=(d_nm0, d_nm1), norm_ffn=(d_nf0, d_nf1), final=d_final, loss_cols=loss_cols, sink=d_sink)
    return gx.reshape(x.shape), grads, vecs


ANY = pl.BlockSpec(memory_space=pl.ANY)
HBM = pltpu.MemorySpace.HBM


def _me():
    return lax.axis_index("x"), lax.axis_index("y"), lax.axis_index("c")


def _chip_peer(x, y, j):
    px = 1 - x if j & 2 else x
    py = 1 - y if j & 1 else y
    return px, py, 2 * px + py


def _remote(src, dst, sems, k, dev):
    return pltpu.make_async_remote_copy(src_ref=src, dst_ref=dst, send_sem=sems[0].at[k], recv_sem=sems[1].at[k],
                                        device_id=dev, device_id_type=MESH)


def _col_window(ref, q, width):
    return ref.at[:, pl.ds(pl.multiple_of(q * width, LANES), width)]


def _half0(ref, h):
    n = ref.shape[0] // 2
    return ref.at[pl.ds(h * n, n)]


def _half1(ref, h):
    n = ref.shape[1] // 2
    return ref.at[:, pl.ds(h * n, n)]


def _half_rows(ref, h):
    n = ref.shape[-2] // 2
    if len(ref.shape) == 2:
        return ref.at[pl.ds(h * n, n)]
    return ref.at[:, pl.ds(h * n, n)]


def _place_shard(w, layer, q_arr, col, name):
    _, rows, cols = w.shape

    def body(q_ref, w_ref, o_ref):
        o_ref[...] = w_ref[...].astype(BF16)

    if col:
        out_spec = pl.BlockSpec((rows, cols), lambda l, q: (0, q[0]))
        out_shape = jax.ShapeDtypeStruct((rows, N_CHIPS * cols), BF16)
    else:
        out_spec = pl.BlockSpec((None, None, rows, cols), lambda l, q: (q[0], 0, 0, 0))
        out_shape = jax.ShapeDtypeStruct((N_CHIPS, 1, rows, cols), BF16)
    return pl.pallas_call(
        body, name=name,
        grid_spec=pltpu.PrefetchScalarGridSpec(
            num_scalar_prefetch=1, grid=(1,),
            in_specs=[pl.BlockSpec((None, rows, cols), lambda l, q: (layer, 0, 0))], out_specs=out_spec),
        out_shape=out_shape, compiler_params=_cp(),
    )(q_arr, w)


def _handshake(peers):
    barrier = pltpu.get_barrier_semaphore()
    for p in peers:
        pl.semaphore_signal(barrier, inc=1, device_id=p, device_id_type=MESH)
    pl.semaphore_wait(barrier, len(peers))


def _on_sequencer(name, collective_id, n_sem, n_local, body):
    @pl.kernel(mesh=plsc.ScalarSubcoreMesh(axis_name="seq", num_cores=1), name=name,
               scratch_types=(pltpu.SemaphoreType.DMA((n_sem,)), pltpu.SemaphoreType.DMA((n_sem,)),
                              pltpu.SemaphoreType.DMA((max(n_local, 1),))),
               compiler_params=pltpu.CompilerParams(collective_id=collective_id))
    def launch(send_sems, recv_sems, local_sems):
        body((send_sems, recv_sems), local_sems)

    launch()


def _gather_plan(outs, col_fam, sems, handshake):
    n_w = len(outs)
    x, y, c = _me()
    myq = 2 * x + y
    sib = (x, y, 1 - c)
    if handshake:
        _handshake([sib] + [_chip_peer(x, y, j)[:2] + (c,) for j in (1, 2, 3)])

    def slot(w, q):
        if col_fam[w]:
            return _col_window(outs[w], q, outs[w].shape[1] // N_CHIPS)
        return outs[w].at[q]

    first = []
    for w in range(n_w):
        for j in (1, 2, 3):
            px, py, _ = _chip_peer(x, y, j)
            mine = _half_rows(slot(w, myq), c)
            cp = _remote(mine, mine, sems, w * 6 + j - 1, (px, py, c))
            cp.start()
            first.append(cp)
    passed = []
    for w in range(n_w):
        for j in (1, 2, 3):
            _, _, pq = _chip_peer(x, y, j)
            land = _half_rows(slot(w, pq), c)
            _remote(land, land, sems, w * 6 + j - 1, sib).wait_recv()
            cp = _remote(land, land, sems, w * 6 + 2 + j, sib)
            cp.start()
            passed.append(cp)
    for w in range(n_w):
        for j in (1, 2, 3):
            _, _, pq = _chip_peer(x, y, j)
            land = _half_rows(slot(w, pq), 1 - c)
            _remote(land, land, sems, w * 6 + 2 + j, sib).wait_recv()
    for cp in first + passed:
        cp.wait_send()


def _gather_weights(bufs, col_fam):
    n_w = len(bufs)

    def body(*refs):
        _gather_plan(refs[n_w:2 * n_w], col_fam, refs[2 * n_w:2 * n_w + 2], False)

    return pl.pallas_call(
        body, name="gather_weights", in_specs=[ANY] * n_w, out_specs=[ANY] * n_w,
        out_shape=[jax.ShapeDtypeStruct(b.shape, b.dtype) for b in bufs],
        input_output_aliases={w: w for w in range(n_w)},
        scratch_shapes=[pltpu.SemaphoreType.DMA((6 * n_w,)), pltpu.SemaphoreType.DMA((6 * n_w,))],
    )(*bufs)


def _gather_weights_async(bufs, col_fam, name, collective_id):
    refs = [jax.new_ref(b, memory_space=HBM) for b in bufs]
    _on_sequencer(name, collective_id, 6 * len(bufs), 0,
                  lambda sems, _: _gather_plan(refs, col_fam, sems, True))
    return [r[...] for r in refs]


def _grad_half(ref, col, h):
    return _half0(ref, h) if col else _half1(ref, h)


def _swap_halves_with_sibling(grads, col_fam):
    n_w = len(grads)

    def body(*refs):
        _swap_plan(refs[:n_w], refs[n_w:2 * n_w], col_fam, refs[2 * n_w:], False)

    return pl.pallas_call(
        body, name="grad_swap_sibling", in_specs=[ANY] * n_w, out_specs=[ANY] * n_w,
        out_shape=_swap_shapes(grads, col_fam),
        scratch_shapes=[pltpu.SemaphoreType.DMA((n_w,)), pltpu.SemaphoreType.DMA((n_w,))],
    )(*grads)


def _swap_shapes(grads, col_fam):
    out = []
    for w, g in enumerate(grads):
        shp = (g.shape[0] // 2, g.shape[1]) if col_fam[w] else (g.shape[0], g.shape[1] // 2, g.shape[2])
        out.append(jax.ShapeDtypeStruct(shp, g.dtype))
    return out


def _swap_plan(ins, outs, col_fam, sems, handshake):
    x, y, c = _me()
    sib = (x, y, 1 - c)
    if handshake:
        _handshake([sib])
    cps = [_remote(_grad_half(ins[w], col_fam[w], 1 - c), outs[w], sems, w, sib) for w in range(len(ins))]
    for cp in cps:
        cp.start()
    for cp in cps:
        cp.wait_recv()
    for cp in cps:
        cp.wait_send()


def _swap_halves_async(grads, col_fam, name, collective_id):
    srcs = [jax.new_ref(g, memory_space=HBM) for g in grads]
    dsts = [jax.empty_ref(s, memory_space=HBM) for s in _swap_shapes(grads, col_fam)]
    _on_sequencer(name, collective_id, len(grads), 0, lambda sems, _: _swap_plan(srcs, dsts, col_fam, sems, True))
    return [r[...] for r in srcs], [r[...] for r in dsts]


def _half_add(mines, recvs, c_arr, col_fam, name):
    n_w = len(mines)
    mine_specs, recv_specs = [], []
    for recv, col in zip(recvs, col_fam):
        if col:
            rows, n = recv.shape
            tr = rows // N_CHIPS
            mine_specs.append(pl.BlockSpec((tr, n), lambda i, c: (N_CHIPS * c[0] + i, 0)))
            recv_specs.append(pl.BlockSpec((tr, n), lambda i, c: (i, 0)))
        else:
            _, rows, n = recv.shape
            mine_specs.append(pl.BlockSpec((None, rows, n), lambda q, c: (q, c[0], 0)))
            recv_specs.append(pl.BlockSpec((None, rows, n), lambda q, c: (q, 0, 0)))

    def body(c_ref, *refs):
        for a_ref, b_ref, o_ref in zip(refs[:n_w], refs[n_w:2 * n_w], refs[2 * n_w:]):
            o_ref[...] = (a_ref[...].astype(F32) + b_ref[...].astype(F32)).astype(BF16)

    return pl.pallas_call(
        body, name=name,
        grid_spec=pltpu.PrefetchScalarGridSpec(num_scalar_prefetch=1, grid=(N_CHIPS,),
                                               in_specs=mine_specs + recv_specs, out_specs=recv_specs),
        out_shape=[jax.ShapeDtypeStruct(r.shape, BF16) for r in recvs], compiler_params=_cp(),
    )(c_arr, *mines, *recvs)


def _scatter_chip_sums(sums, col_fam):
    n_w = len(sums)

    def body(*refs):
        _scatter_plan(refs[:n_w], refs[n_w:2 * n_w], col_fam, refs[2 * n_w:2 * n_w + 2], refs[2 * n_w + 2], False)

    return pl.pallas_call(
        body, name="grad_scatter_chips", in_specs=[ANY] * n_w, out_specs=[ANY] * n_w,
        out_shape=_scatter_shapes(sums, col_fam),
        scratch_shapes=[pltpu.SemaphoreType.DMA((3 * n_w,)), pltpu.SemaphoreType.DMA((3 * n_w,)),
                        pltpu.SemaphoreType.DMA((n_w,))],
    )(*sums)


def _scatter_shapes(sums, col_fam):
    out = []
    for w, s in enumerate(sums):
        shp = (s.shape[0], s.shape[1] // N_CHIPS) if col_fam[w] else s.shape[1:]
        out.append(jax.ShapeDtypeStruct((N_CHIPS,) + shp, s.dtype))
    return out


def _scatter_plan(ins, outs, col_fam, sems, lsem, handshake):
    n_w = len(ins)
    x, y, c = _me()
    myq = 2 * x + y
    if handshake:
        _handshake([_chip_peer(x, y, j)[:2] + (c,) for j in (1, 2, 3)])

    def slab(w, q):
        if col_fam[w]:
            return _col_window(ins[w], q, ins[w].shape[1] // N_CHIPS)
        return ins[w].at[q]

    local = [pltpu.make_async_copy(slab(w, myq), outs[w].at[myq], lsem.at[w]) for w in range(n_w)]
    for cp in local:
        cp.start()
    cps = []
    for w in range(n_w):
        for j in (1, 2, 3):
            px, py, pq = _chip_peer(x, y, j)
            cp = _remote(slab(w, pq), outs[w].at[myq], sems, w * 3 + j - 1, (px, py, c))
            cp.start()
            cps.append(cp)
    for w in range(n_w):
        for j in (1, 2, 3):
            _, _, pq = _chip_peer(x, y, j)
            land = outs[w].at[pq]
            _remote(land, land, sems, w * 3 + j - 1, (x, y, c)).wait_recv()
    for cp in cps:
        cp.wait_send()
    for cp in local:
        cp.wait()


def _scatter_chip_sums_async(sums, col_fam, name, collective_id):
    srcs = [jax.new_ref(s, memory_space=HBM) for s in sums]
    dsts = [jax.empty_ref(s, memory_space=HBM) for s in _scatter_shapes(sums, col_fam)]
    _on_sequencer(name, collective_id, 3 * len(sums), len(sums),
                  lambda sems, lsem: _scatter_plan(srcs, dsts, col_fam, sems, lsem, True))
    return [r[...] for r in dsts]


def _sum_chips(parts, c_arr, prev, lead, shape, name):
    _, rows, n = parts.shape
    tr = rows // 2 if rows % 32 == 0 else rows
    nblk = rows // tr

    def body(c_ref, p_ref, *rest):
        o_ref = rest[-1]
        acc = p_ref[0].astype(F32)
        for q in range(1, N_CHIPS):
            acc = acc + p_ref[q].astype(F32)
        o_ref[...] = acc

    in_specs = [pl.BlockSpec((N_CHIPS, tr, n), lambda i, c: (0, i, 0))]
    args = [c_arr, parts]
    aliases = {}
    if prev is not None:
        in_specs.append(ANY)
        args.append(prev)
        aliases = {2: 0}
    return pl.pallas_call(
        body, name=name,
        grid_spec=pltpu.PrefetchScalarGridSpec(
            num_scalar_prefetch=1, grid=(nblk,), in_specs=in_specs,
            out_specs=pl.BlockSpec((None, tr, n), lambda i, c: (lead, c[0] * nblk + i, 0))),
        out_shape=jax.ShapeDtypeStruct(shape, F32), input_output_aliases=aliases, compiler_params=_cp(),
    )(*args)


def _join_plan(outs, place, sems, handshake):
    x, y, c = _me()
    sib = (x, y, 1 - c)
    if handshake:
        _handshake([sib])

    def half(k, h):
        o, lead = place[k]
        return _half_rows(outs[o].at[lead], h)

    cps = [_remote(half(k, c), half(k, c), sems, k, sib) for k in range(len(place))]
    for cp in cps:
        cp.start()
    for k in range(len(place)):
        land = half(k, 1 - c)
        _remote(land, land, sems, k, sib).wait_recv()
    for cp in cps:
        cp.wait_send()


def _join_halves(bufs, place, name):
    n_o = len(bufs)
    n_h = len(place)

    def body(*refs):
        _join_plan(refs[n_o:2 * n_o], place, refs[2 * n_o:2 * n_o + 2], False)

    return pl.pallas_call(
        body, name=name, in_specs=[ANY] * n_o, out_specs=[ANY] * n_o,
        out_shape=[jax.ShapeDtypeStruct(b.shape, b.dtype) for b in bufs],
        input_output_aliases={k: k for k in range(n_o)},
        scratch_shapes=[pltpu.SemaphoreType.DMA((n_h,)), pltpu.SemaphoreType.DMA((n_h,))],
    )(*bufs)


def _allreduce_rows(rows):
    n_dev = 8
    n_r = len(rows)
    assert n_r <= 8

    def body(*refs):
        r_refs = refs[:n_r]
        o_ref, slots, send_sems, recv_sems = refs[n_r:]
        x, y, c = _me()
        me = 4 * x + 2 * y + c
        slots[me] = jnp.concatenate([r[...] for r in r_refs] + [jnp.zeros((8 - n_r, D_MODEL), F32)], axis=0)

        def peer(k):
            return (1 - x if k & 4 else x, 1 - y if k & 2 else y, 1 - c if k & 1 else c)

        cps = []
        for k in range(1, n_dev):
            cp = pltpu.make_async_remote_copy(src_ref=slots.at[me], dst_ref=slots.at[me], send_sem=send_sems.at[k - 1],
                                              recv_sem=recv_sems.at[k - 1], device_id=peer(k), device_id_type=MESH)
            cp.start()
            cps.append(cp)
        for k in range(1, n_dev):
            px, py, pc = peer(k)
            land = slots.at[4 * px + 2 * py + pc]
            pltpu.make_async_remote_copy(src_ref=land, dst_ref=land, send_sem=send_sems.at[k - 1],
                                         recv_sem=recv_sems.at[k - 1], device_id=peer(k),
                                         device_id_type=MESH).wait_recv()
        for cp in cps:
            cp.wait_send()
        acc = slots[0]
        for d in range(1, n_dev):
            acc = acc + slots[d]
        o_ref[...] = acc

    vm = pl.BlockSpec(memory_space=pltpu.VMEM)
    return pl.pallas_call(
        body, name="allreduce_rows", in_specs=[vm] * n_r, out_specs=vm,
        out_shape=jax.ShapeDtypeStruct((8, D_MODEL), F32),
        scratch_shapes=[pltpu.VMEM((n_dev, 8, D_MODEL), F32), pltpu.SemaphoreType.DMA((n_dev - 1,)),
                        pltpu.SemaphoreType.DMA((n_dev - 1,))],
    )(*rows)


def _adamw(w, g, m, v, name):
    shape = w.shape
    if len(shape) == 1:
        lead, rows, cols = 1, 1, shape[0]
    else:
        rows, cols = shape[-2:]
        lead = math.prod(shape[:-2])
    args = [a.reshape(lead, rows, cols) for a in (w, g, m, v)]
    tr = rows // 2 if rows % 16 == 0 else rows

    def body(w_ref, g_ref, m_ref, v_ref, d_ref, nm_ref, nv_ref):
        gv = g_ref[...]
        nm = ADAM_B1 * m_ref[...] + (1.0 - ADAM_B1) * gv
        nv = ADAM_B2 * v_ref[...] + (1.0 - ADAM_B2) * jnp.square(gv)
        m_hat = nm / (1.0 - ADAM_B1 ** ADAM_STEP)
        v_hat = nv / (1.0 - ADAM_B2 ** ADAM_STEP)
        d_ref[...] = -ADAM_LR * (m_hat / (jnp.sqrt(v_hat) + ADAM_EPS) + ADAM_WD * w_ref[...])
        nm_ref[...] = nm
        nv_ref[...] = nv

    spec = pl.BlockSpec((None, tr, cols), lambda l, i: (l, i, 0))
    outs = pl.pallas_call(
        body, name=name, grid=(lead, rows // tr), in_specs=[spec] * 4, out_specs=[spec] * 3,
        out_shape=[jax.ShapeDtypeStruct((lead, rows, cols), F32)] * 3, compiler_params=_cp(),
    )(*args)
    return [o.reshape(shape) for o in outs]


def kernel(x, a_w_in, a_sink, a_w_out, b_w_in, b_w_out, norm_mix, norm_ffn, w_gate, w_up, w_down, final_norm, loss_target, m_a_w_in, m_a_sink, m_a_w_out, m_b_w_in, m_b_w_out, m_norm_mix, m_norm_ffn, m_w_gate, m_w_up, m_w_down, m_final_norm, v_a_w_in, v_a_sink, v_a_w_out, v_b_w_in, v_b_w_out, v_norm_mix, v_norm_ffn, v_w_gate, v_w_up, v_w_down, v_final_norm):
    weights = dict(a_w_in=a_w_in, a_sink=a_sink, a_w_out=a_w_out, b_w_in=b_w_in, b_w_out=b_w_out, norm_mix=norm_mix,
                   norm_ffn=norm_ffn, w_gate=w_gate, w_up=w_up, w_down=w_down, final_norm=final_norm)
    mom = dict(a_w_in=m_a_w_in, a_sink=m_a_sink, a_w_out=m_a_w_out, b_w_in=m_b_w_in, b_w_out=m_b_w_out,
               norm_mix=m_norm_mix, norm_ffn=m_norm_ffn, w_gate=m_w_gate, w_up=m_w_up, w_down=m_w_down,
               final_norm=m_final_norm)
    var = dict(a_w_in=v_a_w_in, a_sink=v_a_sink, a_w_out=v_a_w_out, b_w_in=v_b_w_in, b_w_out=v_b_w_out,
               norm_mix=v_norm_mix, norm_ffn=v_norm_ffn, w_gate=v_w_gate, w_up=v_w_up, w_down=v_w_down,
               final_norm=v_final_norm)
    order = ["a_w_in", "a_sink", "a_w_out", "b_w_in", "b_w_out", "norm_mix", "norm_ffn", "w_gate", "w_up", "w_down",
             "final_norm"]
    swapped = ("w_gate", "w_up")
    for n in swapped:
        weights[n], mom[n], var[n] = (a.transpose(0, 2, 1) for a in (weights[n], mom[n], var[n]))
    w_gate_t, w_up_t = weights["w_gate"], weights["w_up"]

    c_arr = lax.axis_index("c").astype(jnp.int32).reshape(1)
    q_arr = (2 * lax.axis_index("x") + lax.axis_index("y")).astype(jnp.int32).reshape(1)

    def placed(w, layer, col, nm):
        return _place_shard(w, layer, q_arr, col, f"place_{nm}")

    (a_in,) = _gather_weights_async([placed(a_w_in, 0, True, "a_in")], (True,), "gather_weights_first", 6)
    a_out, wg0, wu0, wd0 = _gather_weights_async(
        [placed(a_w_out, 0, False, "a_out"), placed(w_gate_t, 0, False, "wg0"), placed(w_up_t, 0, False, "wu0"),
         placed(w_down, 0, False, "wd0")], (False,) * 4, "gather_weights_layer0", 1)
    b_in, b_out, wg1, wu1, wd1 = _gather_weights_async(
        [placed(b_w_in, 0, True, "b_in"), placed(b_w_out, 0, False, "b_out"), placed(w_gate_t, 1, False, "wg1"),
         placed(w_up_t, 1, False, "wu1"), placed(w_down, 1, False, "wd1")], (True,) + (False,) * 4,
        "gather_weights_layer1", 7)
    a_out = a_out.reshape(D_MODEL, D_MODEL)
    b_out = b_out.reshape(D_MODEL, D_MODEL)
    wg, wu, wd = (wg0, wg1), (wu0, wu1), (wd0, wd1)

    gx, grads, vecs = _local_step(x, loss_target, a_in, a_sink[0], a_out, b_in, b_out, norm_mix, norm_ffn, wg, wu, wd,
                                  final_norm)

    rows_out = D_MODEL // N_CHIPS
    partials = [grads["a_in"], grads["b_in"],
                grads["a_out"].reshape(N_CHIPS, rows_out, D_MODEL), grads["b_out"].reshape(N_CHIPS, rows_out, D_MODEL),
                grads["wg"][0], grads["wg"][1], grads["wu"][0], grads["wu"][1], grads["wd"][0], grads["wd"][1]]
    col_fam = (True, True) + (False,) * 8
    names = ("a_in", "b_in", "a_out", "b_out", "wg0", "wg1", "wu0", "wu1", "wd0", "wd1")
    contrib = [None] * len(partials)

    def reduce_group(idx, tag, ids):
        parts = [partials[k] for k in idx]
        cols = tuple(col_fam[k] for k in idx)
        if ids is None:
            theirs = _swap_halves_with_sibling(parts, cols)
        else:
            parts, theirs = _swap_halves_async(parts, cols, f"grad_swap_{tag}", ids[0])
        sums = _half_add(parts, theirs, c_arr, cols, f"chip_sum_{tag}")
        if ids is None:
            out = _scatter_chip_sums(sums, cols)
        else:
            out = _scatter_chip_sums_async(sums, cols, f"grad_scatter_{tag}", ids[1])
        for k, o in zip(idx, out):
            contrib[k] = o

    reduce_group([1, 3, 5, 7, 9], "layer1", (2, 3))
    reduce_group([2, 4, 6, 8], "ffn0", (4, 5))
    reduce_group([0], "a_in", None)
    shapes = [a_w_in.shape, b_w_in.shape, a_w_out.shape, b_w_out.shape, w_down.shape, w_down.shape, w_down.shape]
    place = [(0, 0), (1, 0), (2, 0), (3, 0), (4, 0), (4, 1), (5, 0), (5, 1), (6, 0), (6, 1)]
    bufs = [None] * len(shapes)
    for p, nm, (o, lead) in zip(contrib, names, place):
        bufs[o] = _sum_chips(p, c_arr, bufs[o], lead, shapes[o], f"sum_chips_{nm}")
    g_a_in, g_b_in, g_a_out, g_b_out, g_wg, g_wu, g_wd = _join_halves(bufs, place, "grad_join_sibling")

    sink_row = jnp.pad(vecs["sink"][0:1], ((0, 0), (0, D_MODEL - LANES)))
    tot = _allreduce_rows([vecs["norm_mix"][0], vecs["norm_mix"][1], vecs["norm_ffn"][0], vecs["norm_ffn"][1],
                           vecs["final"], vecs["loss_cols"], sink_row])
    loss = (0.5 / D_MODEL) * jnp.sum(tot[5])
    gw = dict(a_w_in=g_a_in, a_sink=tot[6:7, :N_HEADS], a_w_out=g_a_out, b_w_in=g_b_in, b_w_out=g_b_out,
              norm_mix=tot[0:2], norm_ffn=tot[2:4], w_gate=g_wg, w_up=g_wu, w_down=g_wd, final_norm=tot[4])

    delta, new_m, new_v = {}, {}, {}
    for n in order:
        delta[n], new_m[n], new_v[n] = _adamw(weights[n], gw[n], mom[n], var[n], f"adamw_{n}")
    for n in swapped:
        gw[n], delta[n], new_m[n], new_v[n] = (a.transpose(0, 2, 1) for a in (gw[n], delta[n], new_m[n], new_v[n]))
    return (loss, gx, *[gw[n] for n in order], *[delta[n] for n in order], *[new_m[n] for n in order],
            *[new_v[n] for n in order])
```

```python
import math

import jax
import jax.numpy as jnp
from jax import lax
from jax.experimental import pallas as pl
from jax.experimental.pallas import tpu as pltpu
from jax.experimental.pallas import tpu_sc as plsc

F32 = jnp.float32
BF16 = jnp.bfloat16

D_MODEL = 1024
HEAD_DIM = 64
N_HEADS = 16
N_KV = 4
QKV_W = 1536
D_FF = 2816
N_CHIPS = 4
FF_SH = D_FF // N_CHIPS
HALF_WINDOW_A = 128
DILATED = ((128, 1), (512, 4), (2048, 16))
ROPE_THETA = 10000.0
RMS_EPS = 1e-6
NEG_INF = -1e30
LANES = 128
ADAM_LR, ADAM_B1, ADAM_B2, ADAM_EPS, ADAM_WD, ADAM_STEP = 0.001, 0.9, 0.999, 1e-08, 0.01, 10
VMEM_LIMIT = 56 * 1024 * 1024
ROWS = 512
MATMUL_ROWS = 1024
FFN_BWD_ROWS = 256
LOG2E = math.log2(math.e)
LN2 = math.log(2.0)
Q_SCALE = LOG2E / math.sqrt(HEAD_DIM)
GRAD_TOKENS = 2048
MESH = pl.DeviceIdType.MESH


def _cp(**kw):
    return pltpu.CompilerParams(vmem_limit_bytes=VMEM_LIMIT, **kw)


def _row_tile(t, cap):
    tm = min(cap, t)
    assert t % tm == 0
    return tm


def _rope_tables(seq, dil):
    inv = 1.0 / (ROPE_THETA ** (jnp.arange(0, HEAD_DIM, 2, dtype=F32) / HEAD_DIM))
    ang = jnp.arange(seq, dtype=F32)[:, None] * inv[None, :]
    cos, sin = jnp.cos(ang), jnp.sin(ang)
    cos = jnp.tile(cos, (1, 4))
    sin = jnp.concatenate([-sin, sin, -sin, sin], axis=1)

    def perm(t):
        return t.reshape(seq // dil, dil, LANES).transpose(1, 0, 2).reshape(seq, LANES)

    return perm(cos), perm(sin)


def _swap_halves(t):
    lane = lax.broadcasted_iota(jnp.int32, t.shape, 1)
    return jnp.where((lane % HEAD_DIM) < HEAD_DIM // 2, pltpu.roll(t, LANES - 32, 1), pltpu.roll(t, 32, 1))


def _rope(t, cos, sin):
    return t * cos + _swap_halves(t) * sin


def _rope_t(t, cos, sin):
    return t * cos - _swap_halves(t) * sin


def _to_residue(t, batch, dil):
    if dil == 1:
        return t
    if t.ndim == 2:
        t = t.reshape(batch, t.shape[0] // batch // dil, dil, t.shape[1])
    return t.transpose(0, 2, 1, 3).reshape(-1, t.shape[-1])


def _needs_fold(dil):
    return dil > 1 and dil % 16 != 0


def _folded_shape(batch, seq, dil, cols):
    return (batch, seq // dil, dil, cols)


def _from_residue(t, batch, dil, fold=False):
    if dil == 1:
        return t
    s = t.shape[0] // batch
    nat = t.reshape(batch, dil, s // dil, t.shape[1]).transpose(0, 2, 1, 3)
    return nat if fold else nat.reshape(t.shape)


def _token_rows_spec(a, tm):
    if a.ndim == 2:
        return pl.BlockSpec((tm, a.shape[1]), lambda i: (i, 0))
    _, length, dil, c = a.shape
    per_seq = length * dil // tm
    return pl.BlockSpec((None, tm // dil, dil, c), lambda i: (i // per_seq, i % per_seq, 0, 0))


def _token_rows(ref):
    v = ref[...]
    return v if v.ndim == 2 else v.reshape(v.shape[0] * v.shape[1], v.shape[2])


def _rms_fwd(x, w, name):
    t = x.shape[0]
    tm = _row_tile(t, ROWS)

    def body(x_ref, w_ref, o_ref):
        o_ref[...] = _rms_tile(x_ref[...], w_ref[...]).astype(BF16)

    return pl.pallas_call(
        body, name=name, grid=(t // tm,),
        in_specs=[pl.BlockSpec((tm, D_MODEL), lambda i: (i, 0)), pl.BlockSpec((1, D_MODEL), lambda i: (0, 0))],
        out_specs=pl.BlockSpec((tm, D_MODEL), lambda i: (i, 0)),
        out_shape=jax.ShapeDtypeStruct((t, D_MODEL), BF16), compiler_params=_cp(),
    )(x, w)


def _rms_bwd_tile(xv, wv, dy, dres):
    r = lax.rsqrt(jnp.mean(xv * xv, axis=-1, keepdims=True) + RMS_EPS)
    xh = xv * r
    dxh = dy * wv
    dx = dres + r * (dxh - xh * jnp.mean(dxh * xh, axis=-1, keepdims=True))
    return dx, jnp.sum(dy * xh, axis=0, keepdims=True)


def _accumulate(ref, part):
    @pl.when(pl.program_id(0) == 0)
    def _():
        ref[...] = jnp.zeros_like(ref)

    ref[...] += part


def _rms_bwd(x, w, dhs, dres, name):
    t = x.shape[0]
    tm = _row_tile(t, ROWS)
    n = len(dhs)

    def body(*refs):
        x_ref, w_ref = refs[0], refs[1]
        dh_refs = refs[2:2 + n]
        dres_ref = refs[2 + n]
        dx_ref, dxb_ref, dw_ref = refs[3 + n:]
        dy = _token_rows(dh_refs[0]).astype(F32)
        for k in range(1, n):
            dy = dy + _token_rows(dh_refs[k]).astype(F32)
        dx, dw = _rms_bwd_tile(x_ref[...], w_ref[...], dy, dres_ref[...])
        dx_ref[...] = dx
        dxb_ref[...] = dx.astype(BF16)
        _accumulate(dw_ref, dw)

    row = pl.BlockSpec((tm, D_MODEL), lambda i: (i, 0))
    vec = pl.BlockSpec((1, D_MODEL), lambda i: (0, 0))
    return pl.pallas_call(
        body, name=name, grid=(t // tm,),
        in_specs=[row, vec] + [_token_rows_spec(dh, tm) for dh in dhs] + [row],
        out_specs=[row, row, vec],
        out_shape=[jax.ShapeDtypeStruct((t, D_MODEL), F32), jax.ShapeDtypeStruct((t, D_MODEL), BF16),
                   jax.ShapeDtypeStruct((1, D_MODEL), F32)],
        compiler_params=_cp(),
    )(x, w, *dhs, dres)


def _final_tile(xv, wv, tv):
    r = lax.rsqrt(jnp.mean(xv * xv, axis=-1, keepdims=True) + RMS_EPS)
    xh = xv * r
    err = xh * wv - tv
    dy = err * (1.0 / D_MODEL)
    dxh = dy * wv
    dx = r * (dxh - xh * jnp.mean(dxh * xh, axis=-1, keepdims=True))
    return dx, jnp.sum(err * err, axis=0, keepdims=True), jnp.sum(dy * xh, axis=0, keepdims=True)


def _qkv_proj(h, w, cos, sin, group, name):
    t = h.shape[0]
    seq = cos.shape[0]
    tm = _row_tile(seq, MATMUL_ROWS)
    n_q = N_HEADS * HEAD_DIM // LANES
    n_rope = (N_HEADS + N_KV) * HEAD_DIM // LANES
    scale = Q_SCALE

    def body(h_ref, w_ref, cos_ref, sin_ref, o_ref):
        acc = jnp.dot(h_ref[...], w_ref[...], preferred_element_type=F32)
        cs, sn = cos_ref[...], sin_ref[...]
        csq, snq = cs * scale, sn * scale
        for c in range(QKV_W // LANES):
            blk = acc[:, c * LANES:(c + 1) * LANES]
            if c < n_q:
                blk = _rope(blk, csq, snq)
            elif c < n_rope:
                blk = _rope(blk, cs, sn)
            o_ref[:, c * LANES:(c + 1) * LANES] = blk.astype(BF16)

    tab = pl.BlockSpec((tm, LANES), lambda i: (i % (seq // tm), 0))
    return pl.pallas_call(
        body, name=name, grid=(t // tm,),
        in_specs=[pl.BlockSpec((tm, D_MODEL), lambda i: (i, 0)),
                  pl.BlockSpec((D_MODEL, QKV_W), lambda i: (0, group)), tab, tab],
        out_specs=pl.BlockSpec((tm, QKV_W), lambda i: (i, 0)),
        out_shape=jax.ShapeDtypeStruct((t, QKV_W), BF16), compiler_params=_cp(),
    )(h, w, cos, sin)


def _rms_tile(xv, wv):
    return (xv * lax.rsqrt(jnp.mean(xv * xv, axis=-1, keepdims=True) + RMS_EPS)) * wv


def _mm_res(a, w, res, nw, name):
    t, k = a.shape
    tm = _row_tile(t, ROWS)

    def body(a_ref, w_ref, r_ref, nw_ref, o_ref, h_ref):
        xv = r_ref[...] + jnp.dot(a_ref[...], w_ref[...], preferred_element_type=F32)
        o_ref[...] = xv
        h_ref[...] = _rms_tile(xv, nw_ref[...]).astype(BF16)

    row = pl.BlockSpec((tm, D_MODEL), lambda i: (i, 0))
    return pl.pallas_call(
        body, name=name, grid=(t // tm,),
        in_specs=[pl.BlockSpec((tm, k), lambda i: (i, 0)),
                  pl.BlockSpec((k, D_MODEL), lambda i: (0, 0), pipeline_mode=pl.Buffered(1)), row,
                  pl.BlockSpec((1, D_MODEL), lambda i: (0, 0))],
        out_specs=[row, row],
        out_shape=[jax.ShapeDtypeStruct((t, D_MODEL), F32), jax.ShapeDtypeStruct((t, D_MODEL), BF16)],
        compiler_params=_cp(),
    )(a, w, res, nw)


def _mm_nt(dy, w, group, out_dtype, name):
    t, n = dy.shape
    k = w.shape[0]
    tm = _row_tile(t, MATMUL_ROWS)

    def body(dy_ref, w_ref, o_ref):
        o_ref[...] = lax.dot_general(dy_ref[...], w_ref[...], (((1,), (1,)), ((), ())),
                                     preferred_element_type=F32).astype(out_dtype)

    return pl.pallas_call(
        body, name=name, grid=(t // tm,),
        in_specs=[pl.BlockSpec((tm, n), lambda i: (i, 0)), pl.BlockSpec((k, n), lambda i: (0, group))],
        out_specs=pl.BlockSpec((tm, k), lambda i: (i, 0)),
        out_shape=jax.ShapeDtypeStruct((t, k), out_dtype), compiler_params=_cp(),
    )(dy, w)


def _mm_nt_rms(dy, w, x, nw, dres, name):
    t, n = dy.shape
    tm = _row_tile(t, ROWS)

    def body(dy_ref, w_ref, x_ref, nw_ref, dres_ref, dx_ref, dw_ref):
        dh = lax.dot_general(dy_ref[...], w_ref[...], (((1,), (1,)), ((), ())), preferred_element_type=F32)
        dx, dw = _rms_bwd_tile(x_ref[...], nw_ref[...], dh, dres_ref[...])
        dx_ref[...] = dx
        _accumulate(dw_ref, dw)

    row = pl.BlockSpec((tm, D_MODEL), lambda i: (i, 0))
    vec = pl.BlockSpec((1, D_MODEL), lambda i: (0, 0))
    return pl.pallas_call(
        body, name=name, grid=(t // tm,),
        in_specs=[pl.BlockSpec((tm, n), lambda i: (i, 0)),
                  pl.BlockSpec((D_MODEL, n), lambda i: (0, 0), pipeline_mode=pl.Buffered(1)), row, vec, row],
        out_specs=[row, vec],
        out_shape=[jax.ShapeDtypeStruct((t, D_MODEL), F32), jax.ShapeDtypeStruct((1, D_MODEL), F32)],
        compiler_params=_cp(),
    )(dy, w, x, nw, dres)


def _out_bwd(dx, w, o, name):
    t = dx.shape[0]
    tm = _row_tile(t, ROWS)

    def body(dx_ref, w_ref, o_ref, et_ref, do_ref, adj_ref):
        do = lax.dot_general(dx_ref[...], w_ref[...], (((1,), (1,)), ((), ())), preferred_element_type=F32)
        do_ref[...] = do.astype(BF16)
        adj_ref[...] = -_dot_heads(do * o_ref[...].astype(F32), et_ref[...])

    row = pl.BlockSpec((tm, D_MODEL), lambda i: (i, 0))
    return pl.pallas_call(
        body, name=name, grid=(t // tm,),
        in_specs=[row, pl.BlockSpec((D_MODEL, D_MODEL), lambda i: (0, 0)), row,
                  pl.BlockSpec((D_MODEL, LANES), lambda i: (0, 0))],
        out_specs=[row, pl.BlockSpec((tm, LANES), lambda i: (i, 0))],
        out_shape=[jax.ShapeDtypeStruct((t, D_MODEL), BF16), jax.ShapeDtypeStruct((t, LANES), F32)],
        compiler_params=_cp(),
    )(dx, w, o, _head_expander().T)


def _mm_tn(a, bs, name):
    aq = a.ndim == 3
    bq = bs[0].ndim == 3
    t, ka = a.shape[-2:]
    n = bs[0].shape[-1]
    nq = N_CHIPS if (aq or bq) else 1
    tt = _row_tile(t, GRAD_TOKENS)
    tn = n if n <= 1024 else 768
    assert n % tn == 0
    nb = len(bs)
    steps = t // tt

    def body(*refs):
        a_ref = refs[0]
        b_refs = refs[1:1 + nb]
        o_refs = refs[1 + nb:1 + 2 * nb]
        acc_refs = refs[1 + 2 * nb:]
        s = pl.program_id(2)
        av = a_ref[...]
        for b_ref, o_ref, acc_ref in zip(b_refs, o_refs, acc_refs):
            @pl.when(s == 0)
            def _():
                acc_ref[...] = jnp.zeros_like(acc_ref)

            acc_ref[...] += lax.dot_general(av, b_ref[...], (((0,), (0,)), ((), ())), preferred_element_type=F32)

            @pl.when(s == steps - 1)
            def _():
                o_ref[...] = acc_ref[...].astype(BF16)

    a_spec = (pl.BlockSpec((None, tt, ka), lambda q, j, s: (q, s, 0)) if aq
              else pl.BlockSpec((tt, ka), lambda q, j, s: (s, 0)))
    b_spec = (pl.BlockSpec((None, tt, tn), lambda q, j, s: (q, s, j)) if bq
              else pl.BlockSpec((tt, tn), lambda q, j, s: (s, j)))
    if nq > 1:
        o_spec = pl.BlockSpec((None, ka, tn), lambda q, j, s: (q, 0, j))
        o_shape = jax.ShapeDtypeStruct((nq, ka, n), BF16)
    else:
        o_spec = pl.BlockSpec((ka, tn), lambda q, j, s: (0, j))
        o_shape = jax.ShapeDtypeStruct((ka, n), BF16)
    outs = pl.pallas_call(
        body, name=name, grid=(nq, n // tn, steps),
        in_specs=[a_spec] + [b_spec] * nb, out_specs=[o_spec] * nb, out_shape=[o_shape] * nb,
        scratch_shapes=[pltpu.VMEM((ka, tn), F32)] * nb, compiler_params=_cp(),
    )(a, *bs)
    return outs


def _sigmoid(x):
    return 1.0 / (1.0 + jnp.exp(-x))


def _ffn_up(h, wg, wu, layer, name):
    t = h.shape[0]
    tm = _row_tile(t, MATMUL_ROWS)
    nt = (((1,), (1,)), ((), ()))

    def body(h_ref, wg_ref, wu_ref, a_ref, dg_ref, du_ref):
        hv = h_ref[...]
        g = lax.dot_general(hv, wg_ref[...], nt, preferred_element_type=F32)
        u = lax.dot_general(hv, wu_ref[...], nt, preferred_element_type=F32)
        sg = _sigmoid(g)
        silu = g * sg
        a_ref[...] = (silu * u).astype(BF16)
        dg_ref[...] = (sg * (1.0 + g * (1.0 - sg)) * u).astype(BF16)
        du_ref[...] = silu.astype(BF16)

    wspec = pl.BlockSpec((None, None, FF_SH, D_MODEL), lambda q, i: (q, layer, 0, 0))
    ospec = pl.BlockSpec((None, tm, FF_SH), lambda q, i: (q, i, 0))
    oshape = jax.ShapeDtypeStruct((N_CHIPS, t, FF_SH), BF16)
    return pl.pallas_call(
        body, name=name, grid=(N_CHIPS, t // tm),
        in_specs=[pl.BlockSpec((tm, D_MODEL), lambda q, i: (i, 0)), wspec, wspec],
        out_specs=[ospec] * 3, out_shape=[oshape] * 3, compiler_params=_cp(),
    )(h, wg, wu)


def _ffn_down(a, wd, res, layer, name, norm_w=None, fold_shapes=(), head=None):
    t = a.shape[1]
    tm = _row_tile(t, ROWS)
    resident = pl.BlockSpec((N_CHIPS, None, FF_SH, D_MODEL), lambda i: (0, layer, 0, 0), pipeline_mode=pl.Buffered(1))
    row = pl.BlockSpec((tm, D_MODEL), lambda i: (i, 0))
    vec = pl.BlockSpec((1, D_MODEL), lambda i: (0, 0))

    def hidden(a_ref, w_ref, r_ref):
        acc = r_ref[...]
        for q in range(N_CHIPS):
            acc = acc + jnp.dot(a_ref[q], w_ref[q], preferred_element_type=F32)
        return acc

    if head is None:
        folds = [jax.ShapeDtypeStruct(s, BF16) for s in fold_shapes]

        def body(a_ref, w_ref, r_ref, nw_ref, o_ref, h_ref, *hf_refs):
            xv = hidden(a_ref, w_ref, r_ref)
            o_ref[...] = xv
            hb = _rms_tile(xv, nw_ref[...]).astype(BF16)
            h_ref[...] = hb
            for hf_ref in hf_refs:
                hf_ref[...] = hb.reshape(hf_ref.shape)

        return pl.pallas_call(
            body, name=name, grid=(t // tm,),
            in_specs=[pl.BlockSpec((N_CHIPS, tm, FF_SH), lambda i: (0, i, 0)), resident, row, vec],
            out_specs=[row, row] + [_token_rows_spec(f, tm) for f in folds],
            out_shape=[jax.ShapeDtypeStruct((t, D_MODEL), F32), jax.ShapeDtypeStruct((t, D_MODEL), BF16)] + folds,
            compiler_params=_cp(),
        )(a, wd, res, norm_w)

    def body(a_ref, w_ref, r_ref, nw_ref, t_ref, dx_ref, dxb_ref, l_ref, dw_ref):
        dx, sq, dw = _final_tile(hidden(a_ref, w_ref, r_ref), nw_ref[...], t_ref[...])
        dx_ref[...] = dx
        dxb_ref[...] = dx.astype(BF16)
        _accumulate(l_ref, sq)
        _accumulate(dw_ref, dw)

    return pl.pallas_call(
        body, name=name, grid=(t // tm,),
        in_specs=[pl.BlockSpec((N_CHIPS, tm, FF_SH), lambda i: (0, i, 0)), resident, row, vec, row],
        out_specs=[row, row, vec, vec],
        out_shape=[jax.ShapeDtypeStruct((t, D_MODEL), F32), jax.ShapeDtypeStruct((t, D_MODEL), BF16),
                   jax.ShapeDtypeStruct((1, D_MODEL), F32), jax.ShapeDtypeStruct((1, D_MODEL), F32)],
        compiler_params=_cp(),
    )(a, wd, res, *head)


def _ffn_bwd(dy, wd, wg, wu, fg, fu, x, nw, dres, name):
    t = dy.shape[0]
    tm = _row_tile(t, FFN_BWD_ROWS)
    nt = (((1,), (1,)), ((), ()))

    def body(dy_ref, wd_ref, wg_ref, wu_ref, fg_ref, fu_ref, x_ref, nw_ref, dres_ref,
             dg_ref, du_ref, dx_ref, dxb_ref, dw_ref):
        dyv = dy_ref[...]
        acc = jnp.zeros((tm, D_MODEL), F32)
        for q in range(N_CHIPS):
            da = lax.dot_general(dyv, wd_ref[q], nt, preferred_element_type=F32)
            dg = (da * fg_ref[q].astype(F32)).astype(BF16)
            du = (da * fu_ref[q].astype(F32)).astype(BF16)
            dg_ref[q] = dg
            du_ref[q] = du
            acc = acc + jnp.dot(dg, wg_ref[q], preferred_element_type=F32)
            acc = acc + jnp.dot(du, wu_ref[q], preferred_element_type=F32)
        dx, dw = _rms_bwd_tile(x_ref[...], nw_ref[...], acc, dres_ref[...])
        dx_ref[...] = dx
        dxb_ref[...] = dx.astype(BF16)
        _accumulate(dw_ref, dw)

    aspec = pl.BlockSpec((N_CHIPS, tm, FF_SH), lambda i: (0, i, 0))
    wspec = pl.BlockSpec((N_CHIPS, None, FF_SH, D_MODEL), lambda i: (0, 0, 0, 0), pipeline_mode=pl.Buffered(1))
    row = pl.BlockSpec((tm, D_MODEL), lambda i: (i, 0))
    vec = pl.BlockSpec((1, D_MODEL), lambda i: (0, 0))
    ashape = jax.ShapeDtypeStruct((N_CHIPS, t, FF_SH), BF16)
    return pl.pallas_call(
        body, name=name, grid=(t // tm,),
        in_specs=[row, wspec, wspec, wspec, aspec, aspec, row, vec, row],
        out_specs=[aspec, aspec, row, row, vec],
        out_shape=[ashape, ashape, jax.ShapeDtypeStruct((t, D_MODEL), F32), jax.ShapeDtypeStruct((t, D_MODEL), BF16),
                   jax.ShapeDtypeStruct((1, D_MODEL), F32)],
        compiler_params=_cp(),
    )(dy, wd, wg, wu, fg, fu, x, nw, dres)


def _attn_geometry(length, half_window):
    qb = min(LANES, length)
    kw = min(qb + 2 * half_window, length)
    return qb, kw, length // qb


def _dup_kv(src_ref, dst_ref, s, length):
    ch = min(length, 256)
    lo = lax.broadcasted_iota(jnp.int32, (ch, LANES), 1) < HEAD_DIM

    def chunk(c, carry):
        r0 = pl.multiple_of(c * ch, ch)
        for j in range(N_KV // 2):
            tile = src_ref[s, pl.ds(r0, ch), j * LANES:(j + 1) * LANES].astype(F32)
            rolled = pltpu.roll(tile, HEAD_DIM, 1)
            dst_ref[2 * j, pl.ds(r0, ch), :] = jnp.where(lo, tile, rolled).astype(BF16)
            dst_ref[2 * j + 1, pl.ds(r0, ch), :] = jnp.where(lo, rolled, tile).astype(BF16)
        return carry

    lax.fori_loop(0, length // ch, chunk, 0)


def _stack_heads(ref, s, q0, qb, g):
    lo = lax.broadcasted_iota(jnp.int32, (qb, LANES), 1) < HEAD_DIM
    parts = []
    for a in range(4):
        col = (2 * g + a // 2) * LANES
        tile = ref[s, pl.ds(q0, qb), col:col + LANES]
        keep = lo if a % 2 == 0 else jnp.logical_not(lo)
        parts.append(jnp.where(keep, tile, jnp.zeros_like(tile)))
    return jnp.concatenate(parts, axis=0)


def _unstack_pair_t(stacked_t, qb, pair):
    both = jnp.concatenate([stacked_t[:, (2 * pair) * qb:(2 * pair + 1) * qb],
                            stacked_t[:, (2 * pair + 1) * qb:(2 * pair + 2) * qb]], axis=0)
    return both.T


def _band_mask_t(q0, k0, qb, kw, half_window):
    key = lax.broadcasted_iota(jnp.int32, (kw, 4 * qb), 0)
    qry = lax.broadcasted_iota(jnp.int32, (kw, 4 * qb), 1) & (qb - 1)
    return jnp.abs((q0 + qry) - (k0 + key)) <= half_window


def _block_origin(i, qb, kw, half_window, length):
    if isinstance(i, int):
        return i * qb, min(max(i * qb - half_window, 0), length - kw)
    return (pl.multiple_of(i * qb, qb),
            pl.multiple_of(jnp.clip(i * qb - half_window, 0, length - kw), HEAD_DIM))


def _head_row(vals, qb):
    return jnp.concatenate([jnp.broadcast_to(v, (1, qb)).astype(F32) for v in vals], axis=1)


def _attn_fwd(qkv, sink, n_seq, length, half_window, seq_blk, out_dtype, name):
    qb, kw, nblk = _attn_geometry(length, half_window)
    with_sink = sink is not None
    nt = (((1,), (1,)), ((), ()))
    tn = (((0,), (0,)), ((), ()))
    qkv3 = qkv.reshape(n_seq, length, QKV_W)

    def body(*refs):
        refs = list(refs)
        sink_ref = refs.pop(0) if with_sink else None
        q_ref, k_ref, v_ref, o_ref, lse_ref = refs[:5]
        kx_ref, vx_ref = refs[-2:]
        head_row = lax.broadcasted_iota(jnp.int32, (N_HEADS, qb), 0)
        for s in range(seq_blk):
            _dup_kv(k_ref, kx_ref, s, length)
            _dup_kv(v_ref, vx_ref, s, length)

            def block(i, carry):
                q0, k0 = _block_origin(i, qb, kw, half_window, length)
                valid = _band_mask_t(q0, k0, qb, kw, half_window)
                lse_tile = jnp.zeros((N_HEADS, qb), F32)
                groups = range(N_KV)
                sts = [lax.dot_general(kx_ref[g, pl.ds(k0, kw), :], _stack_heads(q_ref, s, q0, qb, g), nt,
                                       preferred_element_type=F32) for g in groups]
                sts = [jnp.where(valid, st, NEG_INF) for st in sts]
                ms = [jnp.max(st, axis=0, keepdims=True) for st in sts]
                if with_sink:
                    sks = [_head_row([sink_ref[4 * g + a] * LOG2E for a in range(4)], qb) for g in groups]
                    ms = [jnp.maximum(m, sk) for m, sk in zip(ms, sks)]
                es = [jnp.exp2(st - m) for st, m in zip(sts, ms)]
                dens = [jnp.sum(e, axis=0, keepdims=True) for e in es]
                if with_sink:
                    dens = [den + jnp.exp2(sk - m) for den, sk, m in zip(dens, sks, ms)]
                ots = [lax.dot_general(vx_ref[g, pl.ds(k0, kw), 0:HEAD_DIM], es[g].astype(BF16), tn,
                                       preferred_element_type=F32) / dens[g] for g in groups]
                for g in groups:
                    for pair in range(2):
                        col = (2 * g + pair) * LANES
                        o_ref[s, pl.ds(q0, qb), col:col + LANES] = _unstack_pair_t(ots[g], qb, pair).astype(out_dtype)
                    lse = ms[g] * LN2 + jnp.log(dens[g])
                    for a in range(4):
                        lse_tile = jnp.where(head_row == 4 * g + a, lse[:, a * qb:(a + 1) * qb], lse_tile)
                lse_ref[s, :, pl.ds(q0, qb)] = lse_tile
                return carry

            if nblk == 1:
                block(0, 0)
            else:
                lax.fori_loop(0, nblk, block, 0)

    in_specs = [pl.BlockSpec((seq_blk, length, N_HEADS * HEAD_DIM), lambda n: (n, 0, 0)),
                pl.BlockSpec((seq_blk, length, N_KV * HEAD_DIM), lambda n: (n, 0, 4)),
                pl.BlockSpec((seq_blk, length, N_KV * HEAD_DIM), lambda n: (n, 0, 5))]
    args = [qkv3, qkv3, qkv3]
    if with_sink:
        in_specs.insert(0, pl.BlockSpec(memory_space=pltpu.SMEM))
        args.insert(0, sink)
    out_specs = [pl.BlockSpec((seq_blk, length, D_MODEL), lambda n: (n, 0, 0)),
                 pl.BlockSpec((seq_blk, N_HEADS, length), lambda n: (n, 0, 0))]
    out_shape = [jax.ShapeDtypeStruct((n_seq, length, D_MODEL), out_dtype),
                 jax.ShapeDtypeStruct((n_seq, N_HEADS, length), F32)]
    o, lse = pl.pallas_call(
        body, name=name, grid=(n_seq // seq_blk,), in_specs=in_specs, out_specs=out_specs, out_shape=out_shape,
        scratch_shapes=[pltpu.VMEM((N_KV, length, LANES), BF16), pltpu.VMEM((N_KV, length, LANES), BF16)],
        compiler_params=_cp(),
    )(*args)
    return o.reshape(n_seq * length, D_MODEL), lse


def _attn_bwd(qkv, do, adj, lse, sink, cos, sin, n_seq, length, half_window, seq_blk, dil, name):
    qb, kw, nblk = _attn_geometry(length, half_window)
    scale = 1.0 / math.sqrt(HEAD_DIM)
    with_sink = sink is not None
    nt = (((1,), (1,)), ((), ()))
    tn = (((0,), (0,)), ((), ()))
    qkv3 = qkv.reshape(n_seq, length, QKV_W)
    do3 = do.reshape(n_seq, length, D_MODEL)
    tabs = [t.reshape(dil, length, LANES) for t in (cos, sin)]
    tab_blocks = dil // seq_blk if dil >= seq_blk else 1

    def body(*refs):
        refs = list(refs)
        sink_ref = refs.pop(0) if with_sink else None
        q_ref, k_ref, v_ref, do_ref, aux_ref, lse_ref, cos_ref, sin_ref, dqkv_ref = refs[:9]
        ds_ref = refs[9] if with_sink else None
        kx_ref, vx_ref, dkx_ref, dvx_ref = refs[-4:]
        lane = lax.broadcasted_iota(jnp.int32, (1, LANES), 1)
        if with_sink:
            @pl.when(pl.program_id(0) == 0)
            def _():
                ds_ref[...] = jnp.zeros_like(ds_ref)

        for s in range(seq_blk):
            ts = s % dil
            _dup_kv(k_ref, kx_ref, s, length)
            _dup_kv(v_ref, vx_ref, s, length)
            dkx_ref[...] = jnp.zeros_like(dkx_ref)
            dvx_ref[...] = jnp.zeros_like(dvx_ref)

            def block(i, dsink):
                q0, k0 = _block_origin(i, qb, kw, half_window, length)
                valid = _band_mask_t(q0, k0, qb, kw, half_window)
                cs = cos_ref[ts, pl.ds(q0, qb), :] * scale
                sn = sin_ref[ts, pl.ds(q0, qb), :] * scale
                adj_tile = aux_ref[s, :, pl.ds(q0, qb)]
                lse_tile = lse_ref[s, :, pl.ds(q0, qb)]
                groups = range(N_KV)
                qss = [_stack_heads(q_ref, s, q0, qb, g) for g in groups]
                doss = [_stack_heads(do_ref, s, q0, qb, g) for g in groups]
                kxs = [kx_ref[g, pl.ds(k0, kw), :] for g in groups]
                sts = [lax.dot_general(kxs[g], qss[g], nt, preferred_element_type=F32) for g in groups]
                dpts = [lax.dot_general(vx_ref[g, pl.ds(k0, kw), :], doss[g], nt, preferred_element_type=F32)
                        for g in groups]
                lses = [_head_row([lse_tile[4 * g + a:4 * g + a + 1, :] * LOG2E for a in range(4)], qb) for g in groups]
                shifts = [_head_row([adj_tile[4 * g + a:4 * g + a + 1, :] for a in range(4)], qb) for g in groups]
                pts = [jnp.exp2(jnp.where(valid, sts[g], NEG_INF) - lses[g]) for g in groups]
                dsbs = [(pts[g] * (dpts[g] + shifts[g])).astype(BF16) for g in groups]
                pbs = [pt.astype(BF16) for pt in pts]
                if with_sink:
                    for g in groups:
                        sk = _head_row([sink_ref[4 * g + a] * LOG2E for a in range(4)], qb)
                        dsk = jnp.exp2(sk - lses[g]) * shifts[g]
                        for a in range(4):
                            tot = jnp.sum(dsk[:, a * qb:(a + 1) * qb], axis=1, keepdims=True)
                            dsink = dsink + jnp.where(lane == 4 * g + a, tot, 0.0)
                dqts = [lax.dot_general(kx_ref[g, pl.ds(k0, kw), 0:HEAD_DIM], dsbs[g], tn, preferred_element_type=F32)
                        for g in groups]
                for g in groups:
                    for pair in range(2):
                        col = (2 * g + pair) * LANES
                        tile = _rope_t(_unstack_pair_t(dqts[g], qb, pair), cs, sn)
                        dqkv_ref[s, pl.ds(q0, qb), col:col + LANES] = tile.astype(BF16)
                for g in groups:
                    dkx_ref[g, pl.ds(k0, kw), :] += jnp.dot(dsbs[g], qss[g], preferred_element_type=F32)
                    dvx_ref[g, pl.ds(k0, kw), :] += jnp.dot(pbs[g], doss[g], preferred_element_type=F32)
                return dsink

            if nblk == 1:
                dsink = block(0, jnp.zeros((1, LANES), F32))
            else:
                dsink = lax.fori_loop(0, nblk, block, jnp.zeros((1, LANES), F32))
            if with_sink:
                ds_ref[0:1, :] += dsink

            ch = min(length, 256)
            lo_c = lax.broadcasted_iota(jnp.int32, (ch, LANES), 1) < HEAD_DIM

            def fin(c, carry):
                r0 = pl.multiple_of(c * ch, ch)
                cs = cos_ref[ts, pl.ds(r0, ch), :]
                sn = sin_ref[ts, pl.ds(r0, ch), :]
                for j in range(N_KV // 2):
                    both = []
                    for acc_ref in (dkx_ref, dvx_ref):
                        t0 = acc_ref[2 * j, pl.ds(r0, ch), :]
                        t1 = acc_ref[2 * j + 1, pl.ds(r0, ch), :]
                        both.append(jnp.where(lo_c, t0, t1) + pltpu.roll(jnp.where(lo_c, t1, t0), HEAD_DIM, 1))
                    kcol = N_HEADS * HEAD_DIM + j * LANES
                    vcol = (N_HEADS + N_KV) * HEAD_DIM + j * LANES
                    dqkv_ref[s, pl.ds(r0, ch), kcol:kcol + LANES] = _rope_t(both[0] * LN2, cs, sn).astype(BF16)
                    dqkv_ref[s, pl.ds(r0, ch), vcol:vcol + LANES] = both[1].astype(BF16)
                return carry

            lax.fori_loop(0, length // ch, fin, 0)

    seq_map = lambda n: (n, 0, 0)
    tab_map = (lambda n: (n % tab_blocks, 0, 0)) if dil >= seq_blk else (lambda n: (0, 0, 0))
    tab_rows = min(seq_blk, dil)
    in_specs = [pl.BlockSpec((seq_blk, length, N_HEADS * HEAD_DIM), seq_map),
                pl.BlockSpec((seq_blk, length, N_KV * HEAD_DIM), lambda n: (n, 0, 4)),
                pl.BlockSpec((seq_blk, length, N_KV * HEAD_DIM), lambda n: (n, 0, 5)),
                pl.BlockSpec((seq_blk, length, D_MODEL), seq_map),
                pl.BlockSpec((seq_blk, N_HEADS, length), seq_map),
                pl.BlockSpec((seq_blk, N_HEADS, length), seq_map),
                pl.BlockSpec((tab_rows, length, LANES), tab_map),
                pl.BlockSpec((tab_rows, length, LANES), tab_map)]
    args = [qkv3, qkv3, qkv3, do3, adj, lse] + tabs
    if with_sink:
        in_specs.insert(0, pl.BlockSpec(memory_space=pltpu.SMEM))
        args.insert(0, sink)
    out_specs = [pl.BlockSpec((seq_blk, length, QKV_W), seq_map)]
    out_shape = [jax.ShapeDtypeStruct((n_seq, length, QKV_W), BF16)]
    if with_sink:
        out_specs.append(pl.BlockSpec((8, LANES), lambda n: (0, 0)))
        out_shape.append(jax.ShapeDtypeStruct((8, LANES), F32))
    outs = pl.pallas_call(
        body, name=name, grid=(n_seq // seq_blk,), in_specs=in_specs, out_specs=out_specs, out_shape=out_shape,
        scratch_shapes=[pltpu.VMEM((N_KV, length, LANES), BF16), pltpu.VMEM((N_KV, length, LANES), BF16),
                        pltpu.VMEM((N_KV, length, LANES), F32), pltpu.VMEM((N_KV, length, LANES), F32)],
        compiler_params=_cp(),
    )(*args)
    dqkv = outs[0].reshape(n_seq * length, QKV_W)
    return (dqkv, outs[1]) if with_sink else (dqkv, None)


def _head_expander():
    h = jnp.arange(LANES)[:, None]
    l = jnp.arange(D_MODEL)[None, :]
    return (l // HEAD_DIM == h).astype(BF16)


def _dot_split(a, e):
    hi = a.astype(BF16)
    lo = (a - hi.astype(F32)).astype(BF16)
    return jnp.dot(hi, e, preferred_element_type=F32) + jnp.dot(lo, e, preferred_element_type=F32)


def _dot_heads(a, e):
    return jnp.dot(a.astype(BF16), e, preferred_element_type=F32)


def _mix_weights(lses):
    m = jnp.maximum(jnp.maximum(lses[0], lses[1]), lses[2])
    es = [jnp.exp(v - m) for v in lses]
    tot = es[0] + es[1] + es[2]
    return [e / tot for e in es]


def _mix_fwd(os_, lses, name):
    t = os_[0].shape[0]
    tm = _row_tile(t, ROWS)

    def body(o0, o1, o2, l0, l1, l2, e_ref, out_ref):
        wts = _mix_weights([l0[...], l1[...], l2[...]])
        acc = jnp.zeros((tm, D_MODEL), F32)
        for w, o_ref in zip(wts, (o0, o1, o2)):
            acc = acc + _dot_split(w, e_ref[...]) * _token_rows(o_ref)
        out_ref[...] = acc.astype(BF16)

    row = pl.BlockSpec((tm, D_MODEL), lambda i: (i, 0))
    lrow = pl.BlockSpec((tm, LANES), lambda i: (i, 0))
    return pl.pallas_call(
        body, name=name, grid=(t // tm,),
        in_specs=[_token_rows_spec(o, tm) for o in os_] + [lrow] * 3 + [pl.BlockSpec((LANES, D_MODEL), lambda i: (0, 0))],
        out_specs=row, out_shape=jax.ShapeDtypeStruct((t, D_MODEL), BF16), compiler_params=_cp(),
    )(*os_, *lses, _head_expander())


def _mix_bwd(dx, w_out, os_, lses, do_shapes, name):
    t = dx.shape[0]
    tm = _row_tile(t, ROWS)
    do_structs = [jax.ShapeDtypeStruct(s, BF16) for s in do_shapes]

    def body(d_ref, w_ref, o0, o1, o2, l0, l1, l2, e_ref, et_ref, do0, do1, do2, a0, a1, a2):
        wts = _mix_weights([l0[...], l1[...], l2[...]])
        dv = lax.dot_general(d_ref[...], w_ref[...], (((1,), (1,)), ((), ())), preferred_element_type=F32)
        cs = [_dot_heads(dv * _token_rows(o_ref), et_ref[...]) for o_ref in (o0, o1, o2)]
        mean_c = wts[0] * cs[0] + wts[1] * cs[1] + wts[2] * cs[2]
        for w, c, do_ref, a_ref in zip(wts, cs, (do0, do1, do2), (a0, a1, a2)):
            do_ref[...] = (_dot_heads(w, e_ref[...]) * dv).astype(BF16).reshape(do_ref.shape)
            a_ref[...] = w * (c - mean_c) - w * c

    row = pl.BlockSpec((tm, D_MODEL), lambda i: (i, 0))
    lrow = pl.BlockSpec((tm, LANES), lambda i: (i, 0))
    e = _head_expander()
    return pl.pallas_call(
        body, name=name, grid=(t // tm,),
        in_specs=[row, pl.BlockSpec((D_MODEL, D_MODEL), lambda i: (0, 0), pipeline_mode=pl.Buffered(1))]
        + [_token_rows_spec(o, tm) for o in os_] + [lrow] * 3 + [pl.BlockSpec((LANES, D_MODEL), lambda i: (0, 0)),
                                    pl.BlockSpec((D_MODEL, LANES), lambda i: (0, 0))],
        out_specs=[_token_rows_spec(d, tm) for d in do_structs] + [lrow] * 3,
        out_shape=do_structs + [jax.ShapeDtypeStruct((t, LANES), F32)] * 3,
        compiler_params=_cp(),
    )(dx, w_out, *os_, *lses, e, e.T)


def _stats_to_tokens(stat, batch, dil):
    n_seq, _, length = stat.shape
    t = stat.transpose(0, 2, 1).reshape(n_seq * length, N_HEADS)
    return _from_residue(jnp.pad(t, ((0, 0), (0, LANES - N_HEADS))), batch, dil)


def _stats_from_tokens(stat, batch, dil, n_seq, length):
    t = _to_residue(stat[:, :N_HEADS], batch, dil)
    return t.reshape(n_seq, length, N_HEADS).transpose(0, 2, 1)


def _group_geometry(batch, seq, dil, window):
    length = seq // dil
    n_seq = batch * dil
    seq_blk = max(1, min(dil, 1024 // length))
    return n_seq, length, (window // 2) // dil, seq_blk


def _local_step(x, target, a_in, a_sink, a_out, b_in, b_out, norm_mix, norm_ffn, wg, wu, wd, final_norm):
    batch, seq, _ = x.shape
    t = batch * seq
    x0 = x.reshape(t, D_MODEL)
    tgt = target.reshape(t, D_MODEL)
    tabs = {d: _rope_tables(seq, d) for _, d in DILATED}
    nm = [norm_mix[i:i + 1] for i in range(2)]
    nf = [norm_ffn[i:i + 1] for i in range(2)]

    h0 = _rms_fwd(x0, nm[0], "rms_mix0")
    qkv0 = _qkv_proj(h0, a_in, *tabs[1], 0, "qkv0")
    o0, lse0 = _attn_fwd(qkv0, a_sink, batch, seq, HALF_WINDOW_A, 1, BF16, "attn0")
    x1, hf0 = _mm_res(o0, a_out, x0, nf[0], "out0")
    act0, g0, u0 = _ffn_up(hf0, wg[0], wu[0], 0, "ffn_up0")
    fold_dils = [d for _, d in DILATED if _needs_fold(d)]
    x2, h1, *h1_folded = _ffn_down(act0, wd[0], x1, 0, "ffn_down0", norm_w=nm[1],
                                   fold_shapes=[_folded_shape(batch, seq, d, D_MODEL) for d in fold_dils])
    h1_by_dil = dict(zip(fold_dils, h1_folded))

    geo = [_group_geometry(batch, seq, d, w) for w, d in DILATED]
    h1g, qkv1, o1, lse1, lse1r = [], [], [], [], []
    for gi, (_, d) in enumerate(DILATED):
        n_seq, length, hw, sb = geo[gi]
        hp = _to_residue(h1_by_dil.get(d, h1), batch, d)
        pj = _qkv_proj(hp, b_in, *tabs[d], gi, f"qkv1_{gi}")
        o, lse = _attn_fwd(pj, None, n_seq, length, hw, sb, BF16, f"attn1_{gi}")
        h1g.append(hp)
        qkv1.append(pj)
        o1.append(_from_residue(o, batch, d, fold=True))
        lse1r.append(lse)
        lse1.append(_stats_to_tokens(lse, batch, d))
    omix = _mix_fwd(o1, lse1, "mix")
    x3, hf1 = _mm_res(omix, b_out, x2, nf[1], "out1")
    act1, g1, u1 = _ffn_up(hf1, wg[1], wu[1], 0, "ffn_up1")
    dx4, dx4b, loss_cols, d_final = _ffn_down(act1, wd[1], x3, 0, "ffn_down1_loss",
                                                     head=(final_norm.reshape(1, D_MODEL), tgt))

    def ffn_bwd(dxo, dxob, x_mid, hf, g, u, act, layer):
        dg, du, dxm, dxmb, d_nf = _ffn_bwd(dxob, wd[layer], wg[layer], wu[layer], g, u, x_mid, nf[layer], dxo,
                                           f"ffn_bwd{layer}")
        (d_wd,) = _mm_tn(act, [dxob], f"grad_wd{layer}")
        (d_wgt,) = _mm_tn(dg, [hf], f"grad_wg{layer}")
        (d_wut,) = _mm_tn(du, [hf], f"grad_wu{layer}")
        return dxm, dxmb, d_nf, d_wgt, d_wut, d_wd

    dx3, dx3b, d_nf1, d_wg1, d_wu1, d_wd1 = ffn_bwd(dx4, dx4b, x3, hf1, g1, u1, act1, 1)

    (d_b_out,) = _mm_tn(omix, [dx3b], "grad_b_out")
    do_shapes = [_folded_shape(batch, seq, d, D_MODEL) if _needs_fold(d) else (t, D_MODEL) for _, d in DILATED]
    mb = _mix_bwd(dx3b, b_out, o1, lse1, do_shapes, "out1_mix_bwd")
    dh1, d_b_in = [], []
    for gi, (_, d) in enumerate(DILATED):
        n_seq, length, hw, sb = geo[gi]
        dog = _to_residue(mb[gi], batch, d)
        adj = _stats_from_tokens(mb[3 + gi], batch, d, n_seq, length)
        dpj, _ = _attn_bwd(qkv1[gi], dog, adj, lse1r[gi], None, *tabs[d], n_seq, length, hw, sb, d, f"attn1_bwd{gi}")
        (dw,) = _mm_tn(h1g[gi], [dpj], f"grad_b_in{gi}")
        d_b_in.append(dw)
        dh1.append(_from_residue(_mm_nt(dpj, b_in, gi, BF16, f"qkv1_bwd{gi}"), batch, d, fold=True))
    dx2, dx2b, d_nm1 = _rms_bwd(x2, nm[1], dh1, dx3, "rms_mix_bwd1")

    dx1, dx1b, d_nf0, d_wg0, d_wu0, d_wd0 = ffn_bwd(dx2, dx2b, x1, hf0, g0, u0, act0, 0)

    do0, adj0 = _out_bwd(dx1b, a_out, o0, "out0_bwd")
    (d_a_out,) = _mm_tn(o0, [dx1b], "grad_a_out")
    adj0 = _stats_from_tokens(adj0, batch, 1, batch, seq)
    dqkv0, d_sink = _attn_bwd(qkv0, do0, adj0, lse0, a_sink, *tabs[1], batch, seq, HALF_WINDOW_A, 1, 1, "attn0_bwd")
    (d_a_in,) = _mm_tn(h0, [dqkv0], "grad_a_in")
    gx, d_nm0 = _mm_nt_rms(dqkv0, a_in, x0, nm[0], dx1, "qkv0_bwd")

    grads = dict(a_in=d_a_in, a_out=d_a_out, b_in=jnp.concatenate(d_b_in, axis=1), b_out=d_b_out,
                 wg=(d_wg0, d_wg1), wu=(d_wu0, d_wu1), wd=(d_wd0, d_wd1))
    vecs = dict(norm_mix=(d_nm0, d_nm1), norm_ffn=(d_nf0, d_nf1), final=d_final, loss_cols=loss_cols, sink=d_sink)
    return gx.reshape(x.shape), grads, vecs


ANY = pl.BlockSpec(memory_space=pl.ANY)
HBM = pltpu.MemorySpace.HBM


def _me():
    return lax.axis_index("x"), lax.axis_index("y"), lax.axis_index("c")


def _chip_peer(x, y, j):
    px = 1 - x if j & 2 else x
    py = 1 - y if j & 1 else y
    return px, py, 2 * px + py


def _remote(src, dst, sems, k, dev):
    return pltpu.make_async_remote_copy(src_ref=src, dst_ref=dst, send_sem=sems[0].at[k], recv_sem=sems[1].at[k],
                                        device_id=dev, device_id_type=MESH)


def _col_window(ref, q, width):
    return ref.at[:, pl.ds(pl.multiple_of(q * width, LANES), width)]


def _half0(ref, h):
    n = ref.shape[0] // 2
    return ref.at[pl.ds(h * n, n)]


def _half1(ref, h):
    n = ref.shape[1] // 2
    return ref.at[:, pl.ds(h * n, n)]


def _half_rows(ref, h):
    n = ref.shape[-2] // 2
    if len(ref.shape) == 2:
        return ref.at[pl.ds(h * n, n)]
    return ref.at[:, pl.ds(h * n, n)]


def _place_shard(w, layer, q_arr, col, name):
    _, rows, cols = w.shape

    def body(q_ref, w_ref, o_ref):
        o_ref[...] = w_ref[...].astype(BF16)

    if col:
        out_spec = pl.BlockSpec((rows, cols), lambda l, q: (0, q[0]))
        out_shape = jax.ShapeDtypeStruct((rows, N_CHIPS * cols), BF16)
    else:
        out_spec = pl.BlockSpec((None, None, rows, cols), lambda l, q: (q[0], 0, 0, 0))
        out_shape = jax.ShapeDtypeStruct((N_CHIPS, 1, rows, cols), BF16)
    return pl.pallas_call(
        body, name=name,
        grid_spec=pltpu.PrefetchScalarGridSpec(
            num_scalar_prefetch=1, grid=(1,),
            in_specs=[pl.BlockSpec((None, rows, cols), lambda l, q: (layer, 0, 0))], out_specs=out_spec),
        out_shape=out_shape, compiler_params=_cp(),
    )(q_arr, w)


def _handshake(peers):
    barrier = pltpu.get_barrier_semaphore()
    for p in peers:
        pl.semaphore_signal(barrier, inc=1, device_id=p, device_id_type=MESH)
    pl.semaphore_wait(barrier, len(peers))


def _on_sequencer(name, collective_id, n_sem, n_local, body):
    @pl.kernel(mesh=plsc.ScalarSubcoreMesh(axis_name="seq", num_cores=1), name=name,
               scratch_types=(pltpu.SemaphoreType.DMA((n_sem,)), pltpu.SemaphoreType.DMA((n_sem,)),
                              pltpu.SemaphoreType.DMA((max(n_local, 1),))),
               compiler_params=pltpu.CompilerParams(collective_id=collective_id))
    def launch(send_sems, recv_sems, local_sems):
        body((send_sems, recv_sems), local_sems)

    launch()


def _gather_plan(outs, col_fam, sems, handshake):
    n_w = len(outs)
    x, y, c = _me()
    myq = 2 * x + y
    sib = (x, y, 1 - c)
    if handshake:
        _handshake([sib] + [_chip_peer(x, y, j)[:2] + (c,) for j in (1, 2, 3)])

    def slot(w, q):
        if col_fam[w]:
            return _col_window(outs[w], q, outs[w].shape[1] // N_CHIPS)
        return outs[w].at[q]

    first = []
    for w in range(n_w):
        for j in (1, 2, 3):
            px, py, _ = _chip_peer(x, y, j)
            mine = _half_rows(slot(w, myq), c)
            cp = _remote(mine, mine, sems, w * 6 + j - 1, (px, py, c))
            cp.start()
            first.append(cp)
    passed = []
    for w in range(n_w):
        for j in (1, 2, 3):
            _, _, pq = _chip_peer(x, y, j)
            land = _half_rows(slot(w, pq), c)
            _remote(land, land, sems, w * 6 + j - 1, sib).wait_recv()
            cp = _remote(land, land, sems, w * 6 + 2 + j, sib)
            cp.start()
            passed.append(cp)
    for w in range(n_w):
        for j in (1, 2, 3):
            _, _, pq = _chip_peer(x, y, j)
            land = _half_rows(slot(w, pq), 1 - c)
            _remote(land, land, sems, w * 6 + 2 + j, sib).wait_recv()
    for cp in first + passed:
        cp.wait_send()


def _gather_weights(bufs, col_fam):
    n_w = len(bufs)

    def body(*refs):
        _gather_plan(refs[n_w:2 * n_w], col_fam, refs[2 * n_w:2 * n_w + 2], False)

    return pl.pallas_call(
        body, name="gather_weights", in_specs=[ANY] * n_w, out_specs=[ANY] * n_w,
        out_shape=[jax.ShapeDtypeStruct(b.shape, b.dtype) for b in bufs],
        input_output_aliases={w: w for w in range(n_w)},
        scratch_shapes=[pltpu.SemaphoreType.DMA((6 * n_w,)), pltpu.SemaphoreType.DMA((6 * n_w,))],
    )(*bufs)


def _gather_weights_async(bufs, col_fam, name, collective_id):
    refs = [jax.new_ref(b, memory_space=HBM) for b in bufs]
    _on_sequencer(name, collective_id, 6 * len(bufs), 0,
                  lambda sems, _: _gather_plan(refs, col_fam, sems, True))
    return [r[...] for r in refs]


def _grad_half(ref, col, h):
    return _half0(ref, h) if col else _half1(ref, h)


def _swap_halves_with_sibling(grads, col_fam):
    n_w = len(grads)

    def body(*refs):
        _swap_plan(refs[:n_w], refs[n_w:2 * n_w], col_fam, refs[2 * n_w:], False)

    return pl.pallas_call(
        body, name="grad_swap_sibling", in_specs=[ANY] * n_w, out_specs=[ANY] * n_w,
        out_shape=_swap_shapes(grads, col_fam),
        scratch_shapes=[pltpu.SemaphoreType.DMA((n_w,)), pltpu.SemaphoreType.DMA((n_w,))],
    )(*grads)


def _swap_shapes(grads, col_fam):
    out = []
    for w, g in enumerate(grads):
        shp = (g.shape[0] // 2, g.shape[1]) if col_fam[w] else (g.shape[0], g.shape[1] // 2, g.shape[2])
        out.append(jax.ShapeDtypeStruct(shp, g.dtype))
    return out


def _swap_plan(ins, outs, col_fam, sems, handshake):
    x, y, c = _me()
    sib = (x, y, 1 - c)
    if handshake:
        _handshake([sib])
    cps = [_remote(_grad_half(ins[w], col_fam[w], 1 - c), outs[w], sems, w, sib) for w in range(len(ins))]
    for cp in cps:
        cp.start()
    for cp in cps:
        cp.wait_recv()
    for cp in cps:
        cp.wait_send()


def _swap_halves_async(grads, col_fam, name, collective_id):
    srcs = [jax.new_ref(g, memory_space=HBM) for g in grads]
    dsts = [jax.empty_ref(s, memory_space=HBM) for s in _swap_shapes(grads, col_fam)]
    _on_sequencer(name, collective_id, len(grads), 0, lambda sems, _: _swap_plan(srcs, dsts, col_fam, sems, True))
    return [r[...] for r in srcs], [r[...] for r in dsts]


def _half_add(mines, recvs, c_arr, col_fam, name):
    n_w = len(mines)
    mine_specs, recv_specs = [], []
    for recv, col in zip(recvs, col_fam):
        if col:
            rows, n = recv.shape
            tr = rows // N_CHIPS
            mine_specs.append(pl.BlockSpec((tr, n), lambda i, c: (N_CHIPS * c[0] + i, 0)))
            recv_specs.append(pl.BlockSpec((tr, n), lambda i, c: (i, 0)))
        else:
            _, rows, n = recv.shape
            mine_specs.append(pl.BlockSpec((None, rows, n), lambda q, c: (q, c[0], 0)))
            recv_specs.append(pl.BlockSpec((None, rows, n), lambda q, c: (q, 0, 0)))

    def body(c_ref, *refs):
        for a_ref, b_ref, o_ref in zip(refs[:n_w], refs[n_w:2 * n_w], refs[2 * n_w:]):
            o_ref[...] = (a_ref[...].astype(F32) + b_ref[...].astype(F32)).astype(BF16)

    return pl.pallas_call(
        body, name=name,
        grid_spec=pltpu.PrefetchScalarGridSpec(num_scalar_prefetch=1, grid=(N_CHIPS,),
                                               in_specs=mine_specs + recv_specs, out_specs=recv_specs),
        out_shape=[jax.ShapeDtypeStruct(r.shape, BF16) for r in recvs], compiler_params=_cp(),
    )(c_arr, *mines, *recvs)


def _scatter_chip_sums(sums, col_fam):
    n_w = len(sums)

    def body(*refs):
        _scatter_plan(refs[:n_w], refs[n_w:2 * n_w], col_fam, refs[2 * n_w:2 * n_w + 2], refs[2 * n_w + 2], False)

    return pl.pallas_call(
        body, name="grad_scatter_chips", in_specs=[ANY] * n_w, out_specs=[ANY] * n_w,
        out_shape=_scatter_shapes(sums, col_fam),
        scratch_shapes=[pltpu.SemaphoreType.DMA((3 * n_w,)), pltpu.SemaphoreType.DMA((3 * n_w,)),
                        pltpu.SemaphoreType.DMA((n_w,))],
    )(*sums)


def _scatter_shapes(sums, col_fam):
    out = []
    for w, s in enumerate(sums):
        shp = (s.shape[0], s.shape[1] // N_CHIPS) if col_fam[w] else s.shape[1:]
        out.append(jax.ShapeDtypeStruct((N_CHIPS,) + shp, s.dtype))
    return out


def _scatter_plan(ins, outs, col_fam, sems, lsem, handshake):
    n_w = len(ins)
    x, y, c = _me()
    myq = 2 * x + y
    if handshake:
        _handshake([_chip_peer(x, y, j)[:2] + (c,) for j in (1, 2, 3)])

    def slab(w, q):
        if col_fam[w]:
            return _col_window(ins[w], q, ins[w].shape[1] // N_CHIPS)
        return ins[w].at[q]

    local = [pltpu.make_async_copy(slab(w, myq), outs[w].at[myq], lsem.at[w]) for w in range(n_w)]
    for cp in local:
        cp.start()
    cps = []
    for w in range(n_w):
        for j in (1, 2, 3):
            px, py, pq = _chip_peer(x, y, j)
            cp = _remote(slab(w, pq), outs[w].at[myq], sems, w * 3 + j - 1, (px, py, c))
            cp.start()
            cps.append(cp)
    for w in range(n_w):
        for j in (1, 2, 3):
            _, _, pq = _chip_peer(x, y, j)
            land = outs[w].at[pq]
            _remote(land, land, sems, w * 3 + j - 1, (x, y, c)).wait_recv()
    for cp in cps:
        cp.wait_send()
    for cp in local:
        cp.wait()


def _scatter_chip_sums_async(sums, col_fam, name, collective_id):
    srcs = [jax.new_ref(s, memory_space=HBM) for s in sums]
    dsts = [jax.empty_ref(s, memory_space=HBM) for s in _scatter_shapes(sums, col_fam)]
    _on_sequencer(name, collective_id, 3 * len(sums), len(sums),
                  lambda sems, lsem: _scatter_plan(srcs, dsts, col_fam, sems, lsem, True))
    return [r[...] for r in dsts]


def _sum_chips(parts, c_arr, prev, lead, shape, name):
    _, rows, n = parts.shape
    tr = rows // 2 if rows % 32 == 0 else rows
    nblk = rows // tr

    def body(c_ref, p_ref, *rest):
        o_ref = rest[-1]
        acc = p_ref[0].astype(F32)
        for q in range(1, N_CHIPS):
            acc = acc + p_ref[q].astype(F32)
        o_ref[...] = acc

    in_specs = [pl.BlockSpec((N_CHIPS, tr, n), lambda i, c: (0, i, 0))]
    args = [c_arr, parts]
    aliases = {}
    if prev is not None:
        in_specs.append(ANY)
        args.append(prev)
        aliases = {2: 0}
    return pl.pallas_call(
        body, name=name,
        grid_spec=pltpu.PrefetchScalarGridSpec(
            num_scalar_prefetch=1, grid=(nblk,), in_specs=in_specs,
            out_specs=pl.BlockSpec((None, tr, n), lambda i, c: (lead, c[0] * nblk + i, 0))),
        out_shape=jax.ShapeDtypeStruct(shape, F32), input_output_aliases=aliases, compiler_params=_cp(),
    )(*args)


def _join_plan(outs, place, sems, handshake):
    x, y, c = _me()
    sib = (x, y, 1 - c)
    if handshake:
        _handshake([sib])

    def half(k, h):
        o, lead = place[k]
        return _half_rows(outs[o].at[lead], h)

    cps = [_remote(half(k, c), half(k, c), sems, k, sib) for k in range(len(place))]
    for cp in cps:
        cp.start()
    for k in range(len(place)):
        land = half(k, 1 - c)
        _remote(land, land, sems, k, sib).wait_recv()
    for cp in cps:
        cp.wait_send()


def _join_halves(bufs, place, name):
    n_o = len(bufs)
    n_h = len(place)

    def body(*refs):
        _join_plan(refs[n_o:2 * n_o], place, refs[2 * n_o:2 * n_o + 2], False)

    return pl.pallas_call(
        body, name=name, in_specs=[ANY] * n_o, out_specs=[ANY] * n_o,
        out_shape=[jax.ShapeDtypeStruct(b.shape, b.dtype) for b in bufs],
        input_output_aliases={k: k for k in range(n_o)},
        scratch_shapes=[pltpu.SemaphoreType.DMA((n_h,)), pltpu.SemaphoreType.DMA((n_h,))],
    )(*bufs)


def _allreduce_rows(rows):
    n_dev = 8
    n_r = len(rows)
    assert n_r <= 8

    def body(*refs):
        r_refs = refs[:n_r]
        o_ref, slots, send_sems, recv_sems = refs[n_r:]
        x, y, c = _me()
        me = 4 * x + 2 * y + c
        slots[me] = jnp.concatenate([r[...] for r in r_refs] + [jnp.zeros((8 - n_r, D_MODEL), F32)], axis=0)

        def peer(k):
            return (1 - x if k & 4 else x, 1 - y if k & 2 else y, 1 - c if k & 1 else c)

        cps = []
        for k in range(1, n_dev):
            cp = pltpu.make_async_remote_copy(src_ref=slots.at[me], dst_ref=slots.at[me], send_sem=send_sems.at[k - 1],
                                              recv_sem=recv_sems.at[k - 1], device_id=peer(k), device_id_type=MESH)
            cp.start()
            cps.append(cp)
        for k in range(1, n_dev):
            px, py, pc = peer(k)
            land = slots.at[4 * px + 2 * py + pc]
            pltpu.make_async_remote_copy(src_ref=land, dst_ref=land, send_sem=send_sems.at[k - 1],
                                         recv_sem=recv_sems.at[k - 1], device_id=peer(k),
                                         device_id_type=MESH).wait_recv()
        for cp in cps:
            cp.wait_send()
        acc = slots[0]
        for d in range(1, n_dev):
            acc = acc + slots[d]
        o_ref[...] = acc

    vm = pl.BlockSpec(memory_space=pltpu.VMEM)
    return pl.pallas_call(
        body, name="allreduce_rows", in_specs=[vm] * n_r, out_specs=vm,
        out_shape=jax.ShapeDtypeStruct((8, D_MODEL), F32),
        scratch_shapes=[pltpu.VMEM((n_dev, 8, D_MODEL), F32), pltpu.SemaphoreType.DMA((n_dev - 1,)),
                        pltpu.SemaphoreType.DMA((n_dev - 1,))],
    )(*rows)


def _adamw(w, g, m, v, name):
    shape = w.shape
    if len(shape) == 1:
        lead, rows, cols = 1, 1, shape[0]
    else:
        rows, cols = shape[-2:]
        lead = math.prod(shape[:-2])
    args = [a.reshape(lead, rows, cols) for a in (w, g, m, v)]
    tr = rows // 2 if rows % 16 == 0 else rows

    def body(w_ref, g_ref, m_ref, v_ref, d_ref, nm_ref, nv_ref):
        gv = g_ref[...]
        nm = ADAM_B1 * m_ref[...] + (1.0 - ADAM_B1) * gv
        nv = ADAM_B2 * v_ref[...] + (1.0 - ADAM_B2) * jnp.square(gv)
        m_hat = nm / (1.0 - ADAM_B1 ** ADAM_STEP)
        v_hat = nv / (1.0 - ADAM_B2 ** ADAM_STEP)
        d_ref[...] = -ADAM_LR * (m_hat / (jnp.sqrt(v_hat) + ADAM_EPS) + ADAM_WD * w_ref[...])
        nm_ref[...] = nm
        nv_ref[...] = nv

    spec = pl.BlockSpec((None, tr, cols), lambda l, i: (l, i, 0))
    outs = pl.pallas_call(
        body, name=name, grid=(lead, rows // tr), in_specs=[spec] * 4, out_specs=[spec] * 3,
        out_shape=[jax.ShapeDtypeStruct((lead, rows, cols), F32)] * 3, compiler_params=_cp(),
    )(*args)
    return [o.reshape(shape) for o in outs]


def kernel(x, a_w_in, a_sink, a_w_out, b_w_in, b_w_out, norm_mix, norm_ffn, w_gate, w_up, w_down, final_norm, loss_target, m_a_w_in, m_a_sink, m_a_w_out, m_b_w_in, m_b_w_out, m_norm_mix, m_norm_ffn, m_w_gate, m_w_up, m_w_down, m_final_norm, v_a_w_in, v_a_sink, v_a_w_out, v_b_w_in, v_b_w_out, v_norm_mix, v_norm_ffn, v_w_gate, v_w_up, v_w_down, v_final_norm):
    weights = dict(a_w_in=a_w_in, a_sink=a_sink, a_w_out=a_w_out, b_w_in=b_w_in, b_w_out=b_w_out, norm_mix=norm_mix,
                   norm_ffn=norm_ffn, w_gate=w_gate, w_up=w_up, w_down=w_down, final_norm=final_norm)
    mom = dict(a_w_in=m_a_w_in, a_sink=m_a_sink, a_w_out=m_a_w_out, b_w_in=m_b_w_in, b_w_out=m_b_w_out,
               norm_mix=m_norm_mix, norm_ffn=m_norm_ffn, w_gate=m_w_gate, w_up=m_w_up, w_down=m_w_down,
               final_norm=m_final_norm)
    var = dict(a_w_in=v_a_w_in, a_sink=v_a_sink, a_w_out=v_a_w_out, b_w_in=v_b_w_in, b_w_out=v_b_w_out,
               norm_mix=v_norm_mix, norm_ffn=v_norm_ffn, w_gate=v_w_gate, w_up=v_w_up, w_down=v_w_down,
               final_norm=v_final_norm)
    order = ["a_w_in", "a_sink", "a_w_out", "b_w_in", "b_w_out", "norm_mix", "norm_ffn", "w_gate", "w_up", "w_down",
             "final_norm"]
    swapped = ("w_gate", "w_up")
    for n in swapped:
        weights[n], mom[n], var[n] = (a.transpose(0, 2, 1) for a in (weights[n], mom[n], var[n]))
    w_gate_t, w_up_t = weights["w_gate"], weights["w_up"]

    c_arr = lax.axis_index("c").astype(jnp.int32).reshape(1)
    q_arr = (2 * lax.axis_index("x") + lax.axis_index("y")).astype(jnp.int32).reshape(1)

    def placed(w, layer, col, nm):
        return _place_shard(w, layer, q_arr, col, f"place_{nm}")

    (a_in,) = _gather_weights_async([placed(a_w_in, 0, True, "a_in")], (True,), "gather_weights_first", 6)
    a_out, wg0, wu0, wd0 = _gather_weights_async(
        [placed(a_w_out, 0, False, "a_out"), placed(w_gate_t, 0, False, "wg0"), placed(w_up_t, 0, False, "wu0"),
         placed(w_down, 0, False, "wd0")], (False,) * 4, "gather_weights_layer0", 1)
    b_in, b_out, wg1, wu1, wd1 = _gather_weights_async(
        [placed(b_w_in, 0, True, "b_in"), placed(b_w_out, 0, False, "b_out"), placed(w_gate_t, 1, False, "wg1"),
         placed(w_up_t, 1, False, "wu1"), placed(w_down, 1, False, "wd1")], (True,) + (False,) * 4,
        "gather_weights_layer1", 7)
    a_out = a_out.reshape(D_MODEL, D_MODEL)
    b_out = b_out.reshape(D_MODEL, D_MODEL)
    wg, wu, wd = (wg0, wg1), (wu0, wu1), (wd0, wd1)

    gx, grads, vecs = _local_step(x, loss_target, a_in, a_sink[0], a_out, b_in, b_out, norm_mix, norm_ffn, wg, wu, wd,
                                  final_norm)

    rows_out = D_MODEL // N_CHIPS
    partials = [grads["a_in"], grads["b_in"],
                grads["a_out"].reshape(N_CHIPS, rows_out, D_MODEL), grads["b_out"].reshape(N_CHIPS, rows_out, D_MODEL),
                grads["wg"][0], grads["wg"][1], grads["wu"][0], grads["wu"][1], grads["wd"][0], grads["wd"][1]]
    col_fam = (True, True) + (False,) * 8
    names = ("a_in", "b_in", "a_out", "b_out", "wg0", "wg1", "wu0", "wu1", "wd0", "wd1")
    contrib = [None] * len(partials)

    def reduce_group(idx, tag, ids):
        parts = [partials[k] for k in idx]
        cols = tuple(col_fam[k] for k in idx)
        if ids is None:
            theirs = _swap_halves_with_sibling(parts, cols)
        else:
            parts, theirs = _swap_halves_async(parts, cols, f"grad_swap_{tag}", ids[0])
        sums = _half_add(parts, theirs, c_arr, cols, f"chip_sum_{tag}")
        if ids is None:
            out = _scatter_chip_sums(sums, cols)
        else:
            out = _scatter_chip_sums_async(sums, cols, f"grad_scatter_{tag}", ids[1])
        for k, o in zip(idx, out):
            contrib[k] = o

    reduce_group([1, 3, 5, 7, 9], "layer1", (2, 3))
    reduce_group([2, 4, 6, 8], "ffn0", (4, 5))
    reduce_group([0], "a_in", None)
    shapes = [a_w_in.shape, b_w_in.shape, a_w_out.shape, b_w_out.shape, w_down.shape, w_down.shape, w_down.shape]
    place = [(0, 0), (1, 0), (2, 0), (3, 0), (4, 0), (4, 1), (5, 0), (5, 1), (6, 0), (6, 1)]
    bufs = [None] * len(shapes)
    for p, nm, (o, lead) in zip(contrib, names, place):
        bufs[o] = _sum_chips(p, c_arr, bufs[o], lead, shapes[o], f"sum_chips_{nm}")
    g_a_in, g_b_in, g_a_out, g_b_out, g_wg, g_wu, g_wd = _join_halves(bufs, place, "grad_join_sibling")

    sink_row = jnp.pad(vecs["sink"][0:1], ((0, 0), (0, D_MODEL - LANES)))
    tot = _allreduce_rows([vecs["norm_mix"][0], vecs["norm_mix"][1], vecs["norm_ffn"][0], vecs["norm_ffn"][1],
                           vecs["final"], vecs["loss_cols"], sink_row])
    loss = (0.5 / D_MODEL) * jnp.sum(tot[5])
    gw = dict(a_w_in=g_a_in, a_sink=tot[6:7, :N_HEADS], a_w_out=g_a_out, b_w_in=g_b_in, b_w_out=g_b_out,
              norm_mix=tot[0:2], norm_ffn=tot[2:4], w_gate=g_wg, w_up=g_wu, w_down=g_wd, final_norm=tot[4])

    delta, new_m, new_v = {}, {}, {}
    for n in order:
        delta[n], new_m[n], new_v[n] = _adamw(weights[n], gw[n], mom[n], var[n], f"adamw_{n}")
    for n in swapped:
        gw[n], delta[n], new_m[n], new_v[n] = (a.transpose(0, 2, 1) for a in (gw[n], delta[n], new_m[n], new_v[n]))
    return (loss, gx, *[gw[n] for n in order], *[delta[n] for n in order], *[new_m[n] for n in order],
            *[new_v[n] for n in order])
```

```python
import math

import jax
import jax.numpy as jnp
from jax import lax
from jax.experimental import pallas as pl
from jax.experimental.pallas import tpu as pltpu
from jax.experimental.pallas import tpu_sc as plsc

F32 = jnp.float32
BF16 = jnp.bfloat16

D_MODEL = 1024
HEAD_DIM = 64
N_HEADS = 16
N_KV = 4
QKV_W = 1536
D_FF = 2816
N_CHIPS = 4
FF_SH = D_FF // N_CHIPS
HALF_WINDOW_A = 128
DILATED = ((128, 1), (512, 4), (2048, 16))
ROPE_THETA = 10000.0
RMS_EPS = 1e-6
NEG_INF = -1e30
LANES = 128
ADAM_LR, ADAM_B1, ADAM_B2, ADAM_EPS, ADAM_WD, ADAM_STEP = 0.001, 0.9, 0.999, 1e-08, 0.01, 10
VMEM_LIMIT = 56 * 1024 * 1024
ROWS = 512
MATMUL_ROWS = 1024
FFN_BWD_ROWS = 256
LOG2E = math.log2(math.e)
LN2 = math.log(2.0)
Q_SCALE = LOG2E / math.sqrt(HEAD_DIM)
GRAD_TOKENS = 2048
MESH = pl.DeviceIdType.MESH


def _cp(**kw):
    return pltpu.CompilerParams(vmem_limit_bytes=VMEM_LIMIT, **kw)


def _row_tile(t, cap):
    tm = min(cap, t)
    assert t % tm == 0
    return tm


def _rope_tables(seq, dil):
    inv = 1.0 / (ROPE_THETA ** (jnp.arange(0, HEAD_DIM, 2, dtype=F32) / HEAD_DIM))
    ang = jnp.arange(seq, dtype=F32)[:, None] * inv[None, :]
    cos, sin = jnp.cos(ang), jnp.sin(ang)
    cos = jnp.tile(cos, (1, 4))
    sin = jnp.concatenate([-sin, sin, -sin, sin], axis=1)

    def perm(t):
        return t.reshape(seq // dil, dil, LANES).transpose(1, 0, 2).reshape(seq, LANES)

    return perm(cos), perm(sin)


def _swap_halves(t):
    lane = lax.broadcasted_iota(jnp.int32, t.shape, 1)
    return jnp.where((lane % HEAD_DIM) < HEAD_DIM // 2, pltpu.roll(t, LANES - 32, 1), pltpu.roll(t, 32, 1))


def _rope(t, cos, sin):
    return t * cos + _swap_halves(t) * sin


def _rope_t(t, cos, sin):
    return t * cos - _swap_halves(t) * sin


def _to_residue(t, batch, dil):
    if dil == 1:
        return t
    if t.ndim == 2:
        t = t.reshape(batch, t.shape[0] // batch // dil, dil, t.shape[1])
    return t.transpose(0, 2, 1, 3).reshape(-1, t.shape[-1])


def _needs_fold(dil):
    return dil > 1 and dil % 16 != 0


def _folded_shape(batch, seq, dil, cols):
    return (batch, seq // dil, dil, cols)


def _from_residue(t, batch, dil, fold=False):
    if dil == 1:
        return t
    s = t.shape[0] // batch
    nat = t.reshape(batch, dil, s // dil, t.shape[1]).transpose(0, 2, 1, 3)
    return nat if fold else nat.reshape(t.shape)


def _token_rows_spec(a, tm):
    if a.ndim == 2:
        return pl.BlockSpec((tm, a.shape[1]), lambda i: (i, 0))
    _, length, dil, c = a.shape
    per_seq = length * dil // tm
    return pl.BlockSpec((None, tm // dil, dil, c), lambda i: (i // per_seq, i % per_seq, 0, 0))


def _token_rows(ref):
    v = ref[...]
    return v if v.ndim == 2 else v.reshape(v.shape[0] * v.shape[1], v.shape[2])


def _rms_fwd(x, w, name):
    t = x.shape[0]
    tm = _row_tile(t, ROWS)

    def body(x_ref, w_ref, o_ref):
        o_ref[...] = _rms_tile(x_ref[...], w_ref[...]).astype(BF16)

    return pl.pallas_call(
        body, name=name, grid=(t // tm,),
        in_specs=[pl.BlockSpec((tm, D_MODEL), lambda i: (i, 0)), pl.BlockSpec((1, D_MODEL), lambda i: (0, 0))],
        out_specs=pl.BlockSpec((tm, D_MODEL), lambda i: (i, 0)),
        out_shape=jax.ShapeDtypeStruct((t, D_MODEL), BF16), compiler_params=_cp(),
    )(x, w)


def _rms_bwd_tile(xv, wv, dy, dres):
    r = lax.rsqrt(jnp.mean(xv * xv, axis=-1, keepdims=True) + RMS_EPS)
    xh = xv * r
    dxh = dy * wv
    dx = dres + r * (dxh - xh * jnp.mean(dxh * xh, axis=-1, keepdims=True))
    return dx, jnp.sum(dy * xh, axis=0, keepdims=True)


def _accumulate(ref, part):
    @pl.when(pl.program_id(0) == 0)
    def _():
        ref[...] = jnp.zeros_like(ref)

    ref[...] += part


def _rms_bwd(x, w, dhs, dres, name):
    t = x.shape[0]
    tm = _row_tile(t, ROWS)
    n = len(dhs)

    def body(*refs):
        x_ref, w_ref = refs[0], refs[1]
        dh_refs = refs[2:2 + n]
        dres_ref = refs[2 + n]
        dx_ref, dxb_ref, dw_ref = refs[3 + n:]
        dy = _token_rows(dh_refs[0]).astype(F32)
        for k in range(1, n):
            dy = dy + _token_rows(dh_refs[k]).astype(F32)
        dx, dw = _rms_bwd_tile(x_ref[...], w_ref[...], dy, dres_ref[...])
        dx_ref[...] = dx
        dxb_ref[...] = dx.astype(BF16)
        _accumulate(dw_ref, dw)

    row = pl.BlockSpec((tm, D_MODEL), lambda i: (i, 0))
    vec = pl.BlockSpec((1, D_MODEL), lambda i: (0, 0))
    return pl.pallas_call(
        body, name=name, grid=(t // tm,),
        in_specs=[row, vec] + [_token_rows_spec(dh, tm) for dh in dhs] + [row],
        out_specs=[row, row, vec],
        out_shape=[jax.ShapeDtypeStruct((t, D_MODEL), F32), jax.ShapeDtypeStruct((t, D_MODEL), BF16),
                   jax.ShapeDtypeStruct((1, D_MODEL), F32)],
        compiler_params=_cp(),
    )(x, w, *dhs, dres)


def _final_tile(xv, wv, tv):
    r = lax.rsqrt(jnp.mean(xv * xv, axis=-1, keepdims=True) + RMS_EPS)
    xh = xv * r
    err = xh * wv - tv
    dy = err * (1.0 / D_MODEL)
    dxh = dy * wv
    dx = r * (dxh - xh * jnp.mean(dxh * xh, axis=-1, keepdims=True))
    return dx, jnp.sum(err * err, axis=0, keepdims=True), jnp.sum(dy * xh, axis=0, keepdims=True)


def _qkv_proj(h, w, cos, sin, group, name):
    t = h.shape[0]
    seq = cos.shape[0]
    tm = _row_tile(seq, MATMUL_ROWS)
    n_q = N_HEADS * HEAD_DIM // LANES
    n_rope = (N_HEADS + N_KV) * HEAD_DIM // LANES
    scale = Q_SCALE

    def body(h_ref, w_ref, cos_ref, sin_ref, o_ref):
        acc = jnp.dot(h_ref[...], w_ref[...], preferred_element_type=F32)
        cs, sn = cos_ref[...], sin_ref[...]
        csq, snq = cs * scale, sn * scale
        for c in range(QKV_W // LANES):
            blk = acc[:, c * LANES:(c + 1) * LANES]
            if c < n_q:
                blk = _rope(blk, csq, snq)
            elif c < n_rope:
                blk = _rope(blk, cs, sn)
            o_ref[:, c * LANES:(c + 1) * LANES] = blk.astype(BF16)

    tab = pl.BlockSpec((tm, LANES), lambda i: (i % (seq // tm), 0))
    return pl.pallas_call(
        body, name=name, grid=(t // tm,),
        in_specs=[pl.BlockSpec((tm, D_MODEL), lambda i: (i, 0)),
                  pl.BlockSpec((D_MODEL, QKV_W), lambda i: (0, group)), tab, tab],
        out_specs=pl.BlockSpec((tm, QKV_W), lambda i: (i, 0)),
        out_shape=jax.ShapeDtypeStruct((t, QKV_W), BF16), compiler_params=_cp(),
    )(h, w, cos, sin)


def _rms_tile(xv, wv):
    return (xv * lax.rsqrt(jnp.mean(xv * xv, axis=-1, keepdims=True) + RMS_EPS)) * wv


def _mm_res(a, w, res, nw, name):
    t, k = a.shape
    tm = _row_tile(t, ROWS)

    def body(a_ref, w_ref, r_ref, nw_ref, o_ref, h_ref):
        xv = r_ref[...] + jnp.dot(a_ref[...], w_ref[...], preferred_element_type=F32)
        o_ref[...] = xv
        h_ref[...] = _rms_tile(xv, nw_ref[...]).astype(BF16)

    row = pl.BlockSpec((tm, D_MODEL), lambda i: (i, 0))
    return pl.pallas_call(
        body, name=name, grid=(t // tm,),
        in_specs=[pl.BlockSpec((tm, k), lambda i: (i, 0)),
                  pl.BlockSpec((k, D_MODEL), lambda i: (0, 0), pipeline_mode=pl.Buffered(1)), row,
                  pl.BlockSpec((1, D_MODEL), lambda i: (0, 0))],
        out_specs=[row, row],
        out_shape=[jax.ShapeDtypeStruct((t, D_MODEL), F32), jax.ShapeDtypeStruct((t, D_MODEL), BF16)],
        compiler_params=_cp(),
    )(a, w, res, nw)


def _mm_nt(dy, w, group, out_dtype, name):
    t, n = dy.shape
    k = w.shape[0]
    tm = _row_tile(t, MATMUL_ROWS)

    def body(dy_ref, w_ref, o_ref):
        o_ref[...] = lax.dot_general(dy_ref[...], w_ref[...], (((1,), (1,)), ((), ())),
                                     preferred_element_type=F32).astype(out_dtype)

    return pl.pallas_call(
        body, name=name, grid=(t // tm,),
        in_specs=[pl.BlockSpec((tm, n), lambda i: (i, 0)), pl.BlockSpec((k, n), lambda i: (0, group))],
        out_specs=pl.BlockSpec((tm, k), lambda i: (i, 0)),
        out_shape=jax.ShapeDtypeStruct((t, k), out_dtype), compiler_params=_cp(),
    )(dy, w)


def _mm_nt_rms(dy, w, x, nw, dres, name):
    t, n = dy.shape
    tm = _row_tile(t, ROWS)

    def body(dy_ref, w_ref, x_ref, nw_ref, dres_ref, dx_ref, dw_ref):
        dh = lax.dot_general(dy_ref[...], w_ref[...], (((1,), (1,)), ((), ())), preferred_element_type=F32)
        dx, dw = _rms_bwd_tile(x_ref[...], nw_ref[...], dh, dres_ref[...])
        dx_ref[...] = dx
        _accumulate(dw_ref, dw)

    row = pl.BlockSpec((tm, D_MODEL), lambda i: (i, 0))
    vec = pl.BlockSpec((1, D_MODEL), lambda i: (0, 0))
    return pl.pallas_call(
        body, name=name, grid=(t // tm,),
        in_specs=[pl.BlockSpec((tm, n), lambda i: (i, 0)),
                  pl.BlockSpec((D_MODEL, n), lambda i: (0, 0), pipeline_mode=pl.Buffered(1)), row, vec, row],
        out_specs=[row, vec],
        out_shape=[jax.ShapeDtypeStruct((t, D_MODEL), F32), jax.ShapeDtypeStruct((1, D_MODEL), F32)],
        compiler_params=_cp(),
    )(dy, w, x, nw, dres)


def _out_bwd(dx, w, o, name):
    t = dx.shape[0]
    tm = _row_tile(t, ROWS)

    def body(dx_ref, w_ref, o_ref, et_ref, do_ref, adj_ref):
        do = lax.dot_general(dx_ref[...], w_ref[...], (((1,), (1,)), ((), ())), preferred_element_type=F32)
        do_ref[...] = do.astype(BF16)
        adj_ref[...] = -_dot_heads(do * o_ref[...].astype(F32), et_ref[...])

    row = pl.BlockSpec((tm, D_MODEL), lambda i: (i, 0))
    return pl.pallas_call(
        body, name=name, grid=(t // tm,),
        in_specs=[row, pl.BlockSpec((D_MODEL, D_MODEL), lambda i: (0, 0)), row,
                  pl.BlockSpec((D_MODEL, LANES), lambda i: (0, 0))],
        out_specs=[row, pl.BlockSpec((tm, LANES), lambda i: (i, 0))],
        out_shape=[jax.ShapeDtypeStruct((t, D_MODEL), BF16), jax.ShapeDtypeStruct((t, LANES), F32)],
        compiler_params=_cp(),
    )(dx, w, o, _head_expander().T)


def _mm_tn(a, bs, name, part=None):
    aq = a.ndim == 3
    bq = bs[0].ndim == 3
    t, ka = a.shape[-2:]
    n = bs[0].shape[-1]
    nq = N_CHIPS if (aq or bq) else 1
    tt = _row_tile(t, GRAD_TOKENS)
    tn = n if n <= 1024 else 768
    assert n % tn == 0
    nb = len(bs)
    steps = t // tt
    carried = part is not None and part[2] is not None

    def body(*refs):
        a_ref = refs[0]
        b_refs = refs[1:1 + nb]
        o_refs = refs[1 + nb + carried:1 + 2 * nb + carried]
        acc_refs = refs[1 + 2 * nb + carried:]
        s = pl.program_id(2)
        av = a_ref[...]
        for b_ref, o_ref, acc_ref in zip(b_refs, o_refs, acc_refs):
            @pl.when(s == 0)
            def _():
                acc_ref[...] = jnp.zeros_like(acc_ref)

            acc_ref[...] += lax.dot_general(av, b_ref[...], (((0,), (0,)), ((), ())), preferred_element_type=F32)

            @pl.when(s == steps - 1)
            def _():
                o_ref[...] = acc_ref[...].astype(BF16)

    a_spec = (pl.BlockSpec((None, tt, ka), lambda q, j, s: (q, s, 0)) if aq
              else pl.BlockSpec((tt, ka), lambda q, j, s: (s, 0)))
    b_spec = (pl.BlockSpec((None, tt, tn), lambda q, j, s: (q, s, j)) if bq
              else pl.BlockSpec((tt, tn), lambda q, j, s: (s, j)))
    extra_specs, extra_args, aliases = [], [], {}
    if nq > 1:
        o_spec = pl.BlockSpec((None, ka, tn), lambda q, j, s: (q, 0, j))
        o_shape = jax.ShapeDtypeStruct((nq, ka, n), BF16)
    elif part is not None:
        assert nb == 1
        k, n_parts, buf = part
        o_spec = pl.BlockSpec((ka, tn), lambda q, j, s: (0, k * (n // tn) + j))
        o_shape = jax.ShapeDtypeStruct((ka, n_parts * n), BF16)
        if buf is not None:
            extra_specs, extra_args, aliases = [ANY], [buf], {1 + nb: 0}
    else:
        o_spec = pl.BlockSpec((ka, tn), lambda q, j, s: (0, j))
        o_shape = jax.ShapeDtypeStruct((ka, n), BF16)
    outs = pl.pallas_call(
        body, name=name, grid=(nq, n // tn, steps),
        in_specs=[a_spec] + [b_spec] * nb + extra_specs, out_specs=[o_spec] * nb, out_shape=[o_shape] * nb,
        scratch_shapes=[pltpu.VMEM((ka, tn), F32)] * nb, input_output_aliases=aliases, compiler_params=_cp(),
    )(a, *bs, *extra_args)
    return outs


def _sigmoid(x):
    return 1.0 / (1.0 + jnp.exp(-x))


def _ffn_up(h, wg, wu, layer, name):
    t = h.shape[0]
    tm = _row_tile(t, MATMUL_ROWS)
    nt = (((1,), (1,)), ((), ()))

    def body(h_ref, wg_ref, wu_ref, a_ref, dg_ref, du_ref):
        hv = h_ref[...]
        g = lax.dot_general(hv, wg_ref[...], nt, preferred_element_type=F32)
        u = lax.dot_general(hv, wu_ref[...], nt, preferred_element_type=F32)
        sg = _sigmoid(g)
        silu = g * sg
        a_ref[...] = (silu * u).astype(BF16)
        dg_ref[...] = (sg * (1.0 + g * (1.0 - sg)) * u).astype(BF16)
        du_ref[...] = silu.astype(BF16)

    wspec = pl.BlockSpec((None, None, FF_SH, D_MODEL), lambda q, i: (q, layer, 0, 0))
    ospec = pl.BlockSpec((None, tm, FF_SH), lambda q, i: (q, i, 0))
    oshape = jax.ShapeDtypeStruct((N_CHIPS, t, FF_SH), BF16)
    return pl.pallas_call(
        body, name=name, grid=(N_CHIPS, t // tm),
        in_specs=[pl.BlockSpec((tm, D_MODEL), lambda q, i: (i, 0)), wspec, wspec],
        out_specs=[ospec] * 3, out_shape=[oshape] * 3, compiler_params=_cp(),
    )(h, wg, wu)


def _ffn_down(a, wd, res, layer, name, norm_w=None, fold_shapes=(), head=None):
    t = a.shape[1]
    tm = _row_tile(t, ROWS)
    resident = pl.BlockSpec((N_CHIPS, None, FF_SH, D_MODEL), lambda i: (0, layer, 0, 0), pipeline_mode=pl.Buffered(1))
    row = pl.BlockSpec((tm, D_MODEL), lambda i: (i, 0))
    vec = pl.BlockSpec((1, D_MODEL), lambda i: (0, 0))

    def hidden(a_ref, w_ref, r_ref):
        acc = r_ref[...]
        for q in range(N_CHIPS):
            acc = acc + jnp.dot(a_ref[q], w_ref[q], preferred_element_type=F32)
        return acc

    if head is None:
        folds = [jax.ShapeDtypeStruct(s, BF16) for s in fold_shapes]

        def body(a_ref, w_ref, r_ref, nw_ref, o_ref, h_ref, *hf_refs):
            xv = hidden(a_ref, w_ref, r_ref)
            o_ref[...] = xv
            hb = _rms_tile(xv, nw_ref[...]).astype(BF16)
            h_ref[...] = hb
            for hf_ref in hf_refs:
                hf_ref[...] = hb.reshape(hf_ref.shape)

        return pl.pallas_call(
            body, name=name, grid=(t // tm,),
            in_specs=[pl.BlockSpec((N_CHIPS, tm, FF_SH), lambda i: (0, i, 0)), resident, row, vec],
            out_specs=[row, row] + [_token_rows_spec(f, tm) for f in folds],
            out_shape=[jax.ShapeDtypeStruct((t, D_MODEL), F32), jax.ShapeDtypeStruct((t, D_MODEL), BF16)] + folds,
            compiler_params=_cp(),
        )(a, wd, res, norm_w)

    def body(a_ref, w_ref, r_ref, nw_ref, t_ref, dx_ref, dxb_ref, l_ref, dw_ref):
        dx, sq, dw = _final_tile(hidden(a_ref, w_ref, r_ref), nw_ref[...], t_ref[...])
        dx_ref[...] = dx
        dxb_ref[...] = dx.astype(BF16)
        _accumulate(l_ref, sq)
        _accumulate(dw_ref, dw)

    return pl.pallas_call(
        body, name=name, grid=(t // tm,),
        in_specs=[pl.BlockSpec((N_CHIPS, tm, FF_SH), lambda i: (0, i, 0)), resident, row, vec, row],
        out_specs=[row, row, vec, vec],
        out_shape=[jax.ShapeDtypeStruct((t, D_MODEL), F32), jax.ShapeDtypeStruct((t, D_MODEL), BF16),
                   jax.ShapeDtypeStruct((1, D_MODEL), F32), jax.ShapeDtypeStruct((1, D_MODEL), F32)],
        compiler_params=_cp(),
    )(a, wd, res, *head)


def _ffn_bwd(dy, wd, wg, wu, fg, fu, x, nw, dres, name):
    t = dy.shape[0]
    tm = _row_tile(t, FFN_BWD_ROWS)
    nt = (((1,), (1,)), ((), ()))

    def body(dy_ref, wd_ref, wg_ref, wu_ref, fg_ref, fu_ref, x_ref, nw_ref, dres_ref,
             dg_ref, du_ref, dx_ref, dxb_ref, dw_ref):
        dyv = dy_ref[...]
        acc = jnp.zeros((tm, D_MODEL), F32)
        for q in range(N_CHIPS):
            da = lax.dot_general(dyv, wd_ref[q], nt, preferred_element_type=F32)
            dg = (da * fg_ref[q].astype(F32)).astype(BF16)
            du = (da * fu_ref[q].astype(F32)).astype(BF16)
            dg_ref[q] = dg
            du_ref[q] = du
            acc = acc + jnp.dot(dg, wg_ref[q], preferred_element_type=F32)
            acc = acc + jnp.dot(du, wu_ref[q], preferred_element_type=F32)
        dx, dw = _rms_bwd_tile(x_ref[...], nw_ref[...], acc, dres_ref[...])
        dx_ref[...] = dx
        dxb_ref[...] = dx.astype(BF16)
        _accumulate(dw_ref, dw)

    aspec = pl.BlockSpec((N_CHIPS, tm, FF_SH), lambda i: (0, i, 0))
    wspec = pl.BlockSpec((N_CHIPS, None, FF_SH, D_MODEL), lambda i: (0, 0, 0, 0), pipeline_mode=pl.Buffered(1))
    row = pl.BlockSpec((tm, D_MODEL), lambda i: (i, 0))
    vec = pl.BlockSpec((1, D_MODEL), lambda i: (0, 0))
    ashape = jax.ShapeDtypeStruct((N_CHIPS, t, FF_SH), BF16)
    return pl.pallas_call(
        body, name=name, grid=(t // tm,),
        in_specs=[row, wspec, wspec, wspec, aspec, aspec, row, vec, row],
        out_specs=[aspec, aspec, row, row, vec],
        out_shape=[ashape, ashape, jax.ShapeDtypeStruct((t, D_MODEL), F32), jax.ShapeDtypeStruct((t, D_MODEL), BF16),
                   jax.ShapeDtypeStruct((1, D_MODEL), F32)],
        compiler_params=_cp(),
    )(dy, wd, wg, wu, fg, fu, x, nw, dres)


def _attn_geometry(length, half_window):
    qb = min(LANES, length)
    kw = min(qb + 2 * half_window, length)
    return qb, kw, length // qb


def _dup_kv(src_ref, dst_ref, s, length):
    ch = min(length, 256)
    lo = lax.broadcasted_iota(jnp.int32, (ch, LANES), 1) < HEAD_DIM

    def chunk(c, carry):
        r0 = pl.multiple_of(c * ch, ch)
        for j in range(N_KV // 2):
            tile = src_ref[s, pl.ds(r0, ch), j * LANES:(j + 1) * LANES].astype(F32)
            rolled = pltpu.roll(tile, HEAD_DIM, 1)
            dst_ref[2 * j, pl.ds(r0, ch), :] = jnp.where(lo, tile, rolled).astype(BF16)
            dst_ref[2 * j + 1, pl.ds(r0, ch), :] = jnp.where(lo, rolled, tile).astype(BF16)
        return carry

    lax.fori_loop(0, length // ch, chunk, 0)


def _stack_heads(ref, s, q0, qb, g):
    lo = lax.broadcasted_iota(jnp.int32, (qb, LANES), 1) < HEAD_DIM
    parts = []
    for a in range(4):
        col = (2 * g + a // 2) * LANES
        tile = ref[s, pl.ds(q0, qb), col:col + LANES]
        keep = lo if a % 2 == 0 else jnp.logical_not(lo)
        parts.append(jnp.where(keep, tile, jnp.zeros_like(tile)))
    return jnp.concatenate(parts, axis=0)


def _unstack_pair_t(stacked_t, qb, pair):
    both = jnp.concatenate([stacked_t[:, (2 * pair) * qb:(2 * pair + 1) * qb],
                            stacked_t[:, (2 * pair + 1) * qb:(2 * pair + 2) * qb]], axis=0)
    return both.T


def _band_mask_t(q0, k0, qb, kw, half_window):
    key = lax.broadcasted_iota(jnp.int32, (kw, 4 * qb), 0)
    qry = lax.broadcasted_iota(jnp.int32, (kw, 4 * qb), 1) & (qb - 1)
    return jnp.abs((q0 + qry) - (k0 + key)) <= half_window


def _block_origin(i, qb, kw, half_window, length):
    if isinstance(i, int):
        return i * qb, min(max(i * qb - half_window, 0), length - kw)
    return (pl.multiple_of(i * qb, qb),
            pl.multiple_of(jnp.clip(i * qb - half_window, 0, length - kw), HEAD_DIM))


def _head_row(vals, qb):
    return jnp.concatenate([jnp.broadcast_to(v, (1, qb)).astype(F32) for v in vals], axis=1)


def _attn_fwd(qkv, sink, n_seq, length, half_window, seq_blk, out_dtype, name):
    qb, kw, nblk = _attn_geometry(length, half_window)
    with_sink = sink is not None
    nt = (((1,), (1,)), ((), ()))
    tn = (((0,), (0,)), ((), ()))
    qkv3 = qkv.reshape(n_seq, length, QKV_W)

    def body(*refs):
        refs = list(refs)
        sink_ref = refs.pop(0) if with_sink else None
        q_ref, k_ref, v_ref, o_ref, lse_ref = refs[:5]
        kx_ref, vx_ref = refs[-2:]
        head_row = lax.broadcasted_iota(jnp.int32, (N_HEADS, qb), 0)
        for s in range(seq_blk):
            _dup_kv(k_ref, kx_ref, s, length)
            _dup_kv(v_ref, vx_ref, s, length)

            def block(i, carry):
                q0, k0 = _block_origin(i, qb, kw, half_window, length)
                valid = _band_mask_t(q0, k0, qb, kw, half_window)
                lse_tile = jnp.zeros((N_HEADS, qb), F32)
                groups = range(N_KV)
                sts = [lax.dot_general(kx_ref[g, pl.ds(k0, kw), :], _stack_heads(q_ref, s, q0, qb, g), nt,
                                       preferred_element_type=F32) for g in groups]
                sts = [jnp.where(valid, st, NEG_INF) for st in sts]
                ms = [jnp.max(st, axis=0, keepdims=True) for st in sts]
                if with_sink:
                    sks = [_head_row([sink_ref[4 * g + a] * LOG2E for a in range(4)], qb) for g in groups]
                    ms = [jnp.maximum(m, sk) for m, sk in zip(ms, sks)]
                es = [jnp.exp2(st - m) for st, m in zip(sts, ms)]
                dens = [jnp.sum(e, axis=0, keepdims=True) for e in es]
                if with_sink:
                    dens = [den + jnp.exp2(sk - m) for den, sk, m in zip(dens, sks, ms)]
                ots = [lax.dot_general(vx_ref[g, pl.ds(k0, kw), 0:HEAD_DIM], es[g].astype(BF16), tn,
                                       preferred_element_type=F32) / dens[g] for g in groups]
                for g in groups:
                    for pair in range(2):
                        col = (2 * g + pair) * LANES
                        o_ref[s, pl.ds(q0, qb), col:col + LANES] = _unstack_pair_t(ots[g], qb, pair).astype(out_dtype)
                    lse = ms[g] * LN2 + jnp.log(dens[g])
                    for a in range(4):
                        lse_tile = jnp.where(head_row == 4 * g + a, lse[:, a * qb:(a + 1) * qb], lse_tile)
                lse_ref[s, :, pl.ds(q0, qb)] = lse_tile
                return carry

            if nblk == 1:
                block(0, 0)
            else:
                lax.fori_loop(0, nblk, block, 0)

    in_specs = [pl.BlockSpec((seq_blk, length, N_HEADS * HEAD_DIM), lambda n: (n, 0, 0)),
                pl.BlockSpec((seq_blk, length, N_KV * HEAD_DIM), lambda n: (n, 0, 4)),
                pl.BlockSpec((seq_blk, length, N_KV * HEAD_DIM), lambda n: (n, 0, 5))]
    args = [qkv3, qkv3, qkv3]
    if with_sink:
        in_specs.insert(0, pl.BlockSpec(memory_space=pltpu.SMEM))
        args.insert(0, sink)
    out_specs = [pl.BlockSpec((seq_blk, length, D_MODEL), lambda n: (n, 0, 0)),
                 pl.BlockSpec((seq_blk, N_HEADS, length), lambda n: (n, 0, 0))]
    out_shape = [jax.ShapeDtypeStruct((n_seq, length, D_MODEL), out_dtype),
                 jax.ShapeDtypeStruct((n_seq, N_HEADS, length), F32)]
    o, lse = pl.pallas_call(
        body, name=name, grid=(n_seq // seq_blk,), in_specs=in_specs, out_specs=out_specs, out_shape=out_shape,
        scratch_shapes=[pltpu.VMEM((N_KV, length, LANES), BF16), pltpu.VMEM((N_KV, length, LANES), BF16)],
        compiler_params=_cp(),
    )(*args)
    return o.reshape(n_seq * length, D_MODEL), lse


def _attn_bwd(qkv, do, adj, lse, sink, cos, sin, n_seq, length, half_window, seq_blk, dil, name):
    qb, kw, nblk = _attn_geometry(length, half_window)
    scale = 1.0 / math.sqrt(HEAD_DIM)
    with_sink = sink is not None
    nt = (((1,), (1,)), ((), ()))
    tn = (((0,), (0,)), ((), ()))
    qkv3 = qkv.reshape(n_seq, length, QKV_W)
    do3 = do.reshape(n_seq, length, D_MODEL)
    tabs = [t.reshape(dil, length, LANES) for t in (cos, sin)]
    tab_blocks = dil // seq_blk if dil >= seq_blk else 1

    def body(*refs):
        refs = list(refs)
        sink_ref = refs.pop(0) if with_sink else None
        q_ref, k_ref, v_ref, do_ref, aux_ref, lse_ref, cos_ref, sin_ref, dqkv_ref = refs[:9]
        ds_ref = refs[9] if with_sink else None
        kx_ref, vx_ref, dkx_ref, dvx_ref = refs[-4:]
        lane = lax.broadcasted_iota(jnp.int32, (1, LANES), 1)
        if with_sink:
            @pl.when(pl.program_id(0) == 0)
            def _():
                ds_ref[...] = jnp.zeros_like(ds_ref)

        for s in range(seq_blk):
            ts = s % dil
            _dup_kv(k_ref, kx_ref, s, length)
            _dup_kv(v_ref, vx_ref, s, length)
            dkx_ref[...] = jnp.zeros_like(dkx_ref)
            dvx_ref[...] = jnp.zeros_like(dvx_ref)

            def block(i, dsink):
                q0, k0 = _block_origin(i, qb, kw, half_window, length)
                valid = _band_mask_t(q0, k0, qb, kw, half_window)
                cs = cos_ref[ts, pl.ds(q0, qb), :] * scale
                sn = sin_ref[ts, pl.ds(q0, qb), :] * scale
                adj_tile = aux_ref[s, :, pl.ds(q0, qb)]
                lse_tile = lse_ref[s, :, pl.ds(q0, qb)]
                groups = range(N_KV)
                qss = [_stack_heads(q_ref, s, q0, qb, g) for g in groups]
                doss = [_stack_heads(do_ref, s, q0, qb, g) for g in groups]
                kxs = [kx_ref[g, pl.ds(k0, kw), :] for g in groups]
                sts = [lax.dot_general(kxs[g], qss[g], nt, preferred_element_type=F32) for g in groups]
                dpts = [lax.dot_general(vx_ref[g, pl.ds(k0, kw), :], doss[g], nt, preferred_element_type=F32)
                        for g in groups]
                lses = [_head_row([lse_tile[4 * g + a:4 * g + a + 1, :] * LOG2E for a in range(4)], qb) for g in groups]
                shifts = [_head_row([adj_tile[4 * g + a:4 * g + a + 1, :] for a in range(4)], qb) for g in groups]
                pts = [jnp.exp2(jnp.where(valid, sts[g], NEG_INF) - lses[g]) for g in groups]
                dsbs = [(pts[g] * (dpts[g] + shifts[g])).astype(BF16) for g in groups]
                pbs = [pt.astype(BF16) for pt in pts]
                if with_sink:
                    for g in groups:
                        sk = _head_row([sink_ref[4 * g + a] * LOG2E for a in range(4)], qb)
                        dsk = jnp.exp2(sk - lses[g]) * shifts[g]
                        for a in range(4):
                            tot = jnp.sum(dsk[:, a * qb:(a + 1) * qb], axis=1, keepdims=True)
                            dsink = dsink + jnp.where(lane == 4 * g + a, tot, 0.0)
                dqts = [lax.dot_general(kx_ref[g, pl.ds(k0, kw), 0:HEAD_DIM], dsbs[g], tn, preferred_element_type=F32)
                        for g in groups]
                for g in groups:
                    for pair in range(2):
                        col = (2 * g + pair) * LANES
                        tile = _rope_t(_unstack_pair_t(dqts[g], qb, pair), cs, sn)
                        dqkv_ref[s, pl.ds(q0, qb), col:col + LANES] = tile.astype(BF16)
                for g in groups:
                    dkx_ref[g, pl.ds(k0, kw), :] += jnp.dot(dsbs[g], qss[g], preferred_element_type=F32)
                    dvx_ref[g, pl.ds(k0, kw), :] += jnp.dot(pbs[g], doss[g], preferred_element_type=F32)
                return dsink

            if nblk == 1:
                dsink = block(0, jnp.zeros((1, LANES), F32))
            else:
                dsink = lax.fori_loop(0, nblk, block, jnp.zeros((1, LANES), F32))
            if with_sink:
                ds_ref[0:1, :] += dsink

            ch = min(length, 256)
            lo_c = lax.broadcasted_iota(jnp.int32, (ch, LANES), 1) < HEAD_DIM

            def fin(c, carry):
                r0 = pl.multiple_of(c * ch, ch)
                cs = cos_ref[ts, pl.ds(r0, ch), :]
                sn = sin_ref[ts, pl.ds(r0, ch), :]
                for j in range(N_KV // 2):
                    both = []
                    for acc_ref in (dkx_ref, dvx_ref):
                        t0 = acc_ref[2 * j, pl.ds(r0, ch), :]
                        t1 = acc_ref[2 * j + 1, pl.ds(r0, ch), :]
                        both.append(jnp.where(lo_c, t0, t1) + pltpu.roll(jnp.where(lo_c, t1, t0), HEAD_DIM, 1))
                    kcol = N_HEADS * HEAD_DIM + j * LANES
                    vcol = (N_HEADS + N_KV) * HEAD_DIM + j * LANES
                    dqkv_ref[s, pl.ds(r0, ch), kcol:kcol + LANES] = _rope_t(both[0] * LN2, cs, sn).astype(BF16)
                    dqkv_ref[s, pl.ds(r0, ch), vcol:vcol + LANES] = both[1].astype(BF16)
                return carry

            lax.fori_loop(0, length // ch, fin, 0)

    seq_map = lambda n: (n, 0, 0)
    tab_map = (lambda n: (n % tab_blocks, 0, 0)) if dil >= seq_blk else (lambda n: (0, 0, 0))
    tab_rows = min(seq_blk, dil)
    in_specs = [pl.BlockSpec((seq_blk, length, N_HEADS * HEAD_DIM), seq_map),
                pl.BlockSpec((seq_blk, length, N_KV * HEAD_DIM), lambda n: (n, 0, 4)),
                pl.BlockSpec((seq_blk, length, N_KV * HEAD_DIM), lambda n: (n, 0, 5)),
                pl.BlockSpec((seq_blk, length, D_MODEL), seq_map),
                pl.BlockSpec((seq_blk, N_HEADS, length), seq_map),
                pl.BlockSpec((seq_blk, N_HEADS, length), seq_map),
                pl.BlockSpec((tab_rows, length, LANES), tab_map),
                pl.BlockSpec((tab_rows, length, LANES), tab_map)]
    args = [qkv3, qkv3, qkv3, do3, adj, lse] + tabs
    if with_sink:
        in_specs.insert(0, pl.BlockSpec(memory_space=pltpu.SMEM))
        args.insert(0, sink)
    out_specs = [pl.BlockSpec((seq_blk, length, QKV_W), seq_map)]
    out_shape = [jax.ShapeDtypeStruct((n_seq, length, QKV_W), BF16)]
    if with_sink:
        out_specs.append(pl.BlockSpec((8, LANES), lambda n: (0, 0)))
        out_shape.append(jax.ShapeDtypeStruct((8, LANES), F32))
    outs = pl.pallas_call(
        body, name=name, grid=(n_seq // seq_blk,), in_specs=in_specs, out_specs=out_specs, out_shape=out_shape,
        scratch_shapes=[pltpu.VMEM((N_KV, length, LANES), BF16), pltpu.VMEM((N_KV, length, LANES), BF16),
                        pltpu.VMEM((N_KV, length, LANES), F32), pltpu.VMEM((N_KV, length, LANES), F32)],
        compiler_params=_cp(),
    )(*args)
    dqkv = outs[0].reshape(n_seq * length, QKV_W)
    return (dqkv, outs[1]) if with_sink else (dqkv, None)


def _head_expander():
    h = jnp.arange(LANES)[:, None]
    l = jnp.arange(D_MODEL)[None, :]
    return (l // HEAD_DIM == h).astype(BF16)


def _dot_split(a, e):
    hi = a.astype(BF16)
    lo = (a - hi.astype(F32)).astype(BF16)
    return jnp.dot(hi, e, preferred_element_type=F32) + jnp.dot(lo, e, preferred_element_type=F32)


def _dot_heads(a, e):
    return jnp.dot(a.astype(BF16), e, preferred_element_type=F32)


def _mix_weights(lses):
    m = jnp.maximum(jnp.maximum(lses[0], lses[1]), lses[2])
    es = [jnp.exp(v - m) for v in lses]
    tot = es[0] + es[1] + es[2]
    return [e / tot for e in es]


def _mix_fwd(os_, lses, name):
    t = os_[0].shape[0]
    tm = _row_tile(t, ROWS)

    def body(o0, o1, o2, l0, l1, l2, e_ref, out_ref):
        wts = _mix_weights([l0[...], l1[...], l2[...]])
        acc = jnp.zeros((tm, D_MODEL), F32)
        for w, o_ref in zip(wts, (o0, o1, o2)):
            acc = acc + _dot_split(w, e_ref[...]) * _token_rows(o_ref)
        out_ref[...] = acc.astype(BF16)

    row = pl.BlockSpec((tm, D_MODEL), lambda i: (i, 0))
    lrow = pl.BlockSpec((tm, LANES), lambda i: (i, 0))
    return pl.pallas_call(
        body, name=name, grid=(t // tm,),
        in_specs=[_token_rows_spec(o, tm) for o in os_] + [lrow] * 3 + [pl.BlockSpec((LANES, D_MODEL), lambda i: (0, 0))],
        out_specs=row, out_shape=jax.ShapeDtypeStruct((t, D_MODEL), BF16), compiler_params=_cp(),
    )(*os_, *lses, _head_expander())


def _mix_bwd(dx, w_out, os_, lses, do_shapes, name):
    t = dx.shape[0]
    tm = _row_tile(t, ROWS)
    do_structs = [jax.ShapeDtypeStruct(s, BF16) for s in do_shapes]

    def body(d_ref, w_ref, o0, o1, o2, l0, l1, l2, e_ref, et_ref, do0, do1, do2, a0, a1, a2):
        wts = _mix_weights([l0[...], l1[...], l2[...]])
        dv = lax.dot_general(d_ref[...], w_ref[...], (((1,), (1,)), ((), ())), preferred_element_type=F32)
        cs = [_dot_heads(dv * _token_rows(o_ref), et_ref[...]) for o_ref in (o0, o1, o2)]
        mean_c = wts[0] * cs[0] + wts[1] * cs[1] + wts[2] * cs[2]
        for w, c, do_ref, a_ref in zip(wts, cs, (do0, do1, do2), (a0, a1, a2)):
            do_ref[...] = (_dot_heads(w, e_ref[...]) * dv).astype(BF16).reshape(do_ref.shape)
            a_ref[...] = w * (c - mean_c) - w * c

    row = pl.BlockSpec((tm, D_MODEL), lambda i: (i, 0))
    lrow = pl.BlockSpec((tm, LANES), lambda i: (i, 0))
    e = _head_expander()
    return pl.pallas_call(
        body, name=name, grid=(t // tm,),
        in_specs=[row, pl.BlockSpec((D_MODEL, D_MODEL), lambda i: (0, 0), pipeline_mode=pl.Buffered(1))]
        + [_token_rows_spec(o, tm) for o in os_] + [lrow] * 3 + [pl.BlockSpec((LANES, D_MODEL), lambda i: (0, 0)),
                                    pl.BlockSpec((D_MODEL, LANES), lambda i: (0, 0))],
        out_specs=[_token_rows_spec(d, tm) for d in do_structs] + [lrow] * 3,
        out_shape=do_structs + [jax.ShapeDtypeStruct((t, LANES), F32)] * 3,
        compiler_params=_cp(),
    )(dx, w_out, *os_, *lses, e, e.T)


def _stats_to_tokens(stat, batch, dil):
    n_seq, _, length = stat.shape
    t = stat.transpose(0, 2, 1).reshape(n_seq * length, N_HEADS)
    return _from_residue(jnp.pad(t, ((0, 0), (0, LANES - N_HEADS))), batch, dil)


def _stats_from_tokens(stat, batch, dil, n_seq, length):
    t = _to_residue(stat[:, :N_HEADS], batch, dil)
    return t.reshape(n_seq, length, N_HEADS).transpose(0, 2, 1)


def _group_geometry(batch, seq, dil, window):
    length = seq // dil
    n_seq = batch * dil
    seq_blk = max(1, min(dil, 1024 // length))
    return n_seq, length, (window // 2) // dil, seq_blk


def _local_step(x, target, a_in, a_sink, a_out, b_in, b_out, norm_mix, norm_ffn, wg, wu, wd, final_norm):
    batch, seq, _ = x.shape
    t = batch * seq
    x0 = x.reshape(t, D_MODEL)
    tgt = target.reshape(t, D_MODEL)
    tabs = {d: _rope_tables(seq, d) for _, d in DILATED}
    nm = [norm_mix[i:i + 1] for i in range(2)]
    nf = [norm_ffn[i:i + 1] for i in range(2)]

    h0 = _rms_fwd(x0, nm[0], "rms_mix0")
    qkv0 = _qkv_proj(h0, a_in, *tabs[1], 0, "qkv0")
    o0, lse0 = _attn_fwd(qkv0, a_sink, batch, seq, HALF_WINDOW_A, 1, BF16, "attn0")
    x1, hf0 = _mm_res(o0, a_out, x0, nf[0], "out0")
    act0, g0, u0 = _ffn_up(hf0, wg[0], wu[0], 0, "ffn_up0")
    fold_dils = [d for _, d in DILATED if _needs_fold(d)]
    x2, h1, *h1_folded = _ffn_down(act0, wd[0], x1, 0, "ffn_down0", norm_w=nm[1],
                                   fold_shapes=[_folded_shape(batch, seq, d, D_MODEL) for d in fold_dils])
    h1_by_dil = dict(zip(fold_dils, h1_folded))

    geo = [_group_geometry(batch, seq, d, w) for w, d in DILATED]
    h1g, qkv1, o1, lse1, lse1r = [], [], [], [], []
    for gi, (_, d) in enumerate(DILATED):
        n_seq, length, hw, sb = geo[gi]
        hp = _to_residue(h1_by_dil.get(d, h1), batch, d)
        pj = _qkv_proj(hp, b_in, *tabs[d], gi, f"qkv1_{gi}")
        o, lse = _attn_fwd(pj, None, n_seq, length, hw, sb, BF16, f"attn1_{gi}")
        h1g.append(hp)
        qkv1.append(pj)
        o1.append(_from_residue(o, batch, d, fold=True))
        lse1r.append(lse)
        lse1.append(_stats_to_tokens(lse, batch, d))
    omix = _mix_fwd(o1, lse1, "mix")
    x3, hf1 = _mm_res(omix, b_out, x2, nf[1], "out1")
    act1, g1, u1 = _ffn_up(hf1, wg[1], wu[1], 0, "ffn_up1")
    dx4, dx4b, loss_cols, d_final = _ffn_down(act1, wd[1], x3, 0, "ffn_down1_loss",
                                                     head=(final_norm.reshape(1, D_MODEL), tgt))

    def ffn_bwd(dxo, dxob, x_mid, hf, g, u, act, layer):
        dg, du, dxm, dxmb, d_nf = _ffn_bwd(dxob, wd[layer], wg[layer], wu[layer], g, u, x_mid, nf[layer], dxo,
                                           f"ffn_bwd{layer}")
        (d_wd,) = _mm_tn(act, [dxob], f"grad_wd{layer}")
        (d_wgt,) = _mm_tn(dg, [hf], f"grad_wg{layer}")
        (d_wut,) = _mm_tn(du, [hf], f"grad_wu{layer}")
        return dxm, dxmb, d_nf, d_wgt, d_wut, d_wd

    dx3, dx3b, d_nf1, d_wg1, d_wu1, d_wd1 = ffn_bwd(dx4, dx4b, x3, hf1, g1, u1, act1, 1)

    (d_b_out,) = _mm_tn(omix, [dx3b], "grad_b_out")
    do_shapes = [_folded_shape(batch, seq, d, D_MODEL) if _needs_fold(d) else (t, D_MODEL) for _, d in DILATED]
    mb = _mix_bwd(dx3b, b_out, o1, lse1, do_shapes, "out1_mix_bwd")
    dh1, d_b_in = [], None
    for gi, (_, d) in enumerate(DILATED):
        n_seq, length, hw, sb = geo[gi]
        dog = _to_residue(mb[gi], batch, d)
        adj = _stats_from_tokens(mb[3 + gi], batch, d, n_seq, length)
        dpj, _ = _attn_bwd(qkv1[gi], dog, adj, lse1r[gi], None, *tabs[d], n_seq, length, hw, sb, d, f"attn1_bwd{gi}")
        (d_b_in,) = _mm_tn(h1g[gi], [dpj], f"grad_b_in{gi}", part=(gi, len(DILATED), d_b_in))
        dh1.append(_from_residue(_mm_nt(dpj, b_in, gi, BF16, f"qkv1_bwd{gi}"), batch, d, fold=True))
    dx2, dx2b, d_nm1 = _rms_bwd(x2, nm[1], dh1, dx3, "rms_mix_bwd1")

    dx1, dx1b, d_nf0, d_wg0, d_wu0, d_wd0 = ffn_bwd(dx2, dx2b, x1, hf0, g0, u0, act0, 0)

    do0, adj0 = _out_bwd(dx1b, a_out, o0, "out0_bwd")
    (d_a_out,) = _mm_tn(o0, [dx1b], "grad_a_out")
    adj0 = _stats_from_tokens(adj0, batch, 1, batch, seq)
    dqkv0, d_sink = _attn_bwd(qkv0, do0, adj0, lse0, a_sink, *tabs[1], batch, seq, HALF_WINDOW_A, 1, 1, "attn0_bwd")
    (d_a_in,) = _mm_tn(h0, [dqkv0], "grad_a_in")
    gx, d_nm0 = _mm_nt_rms(dqkv0, a_in, x0, nm[0], dx1, "qkv0_bwd")

    grads = dict(a_in=d_a_in, a_out=d_a_out, b_in=d_b_in, b_out=d_b_out,
                 wg=(d_wg0, d_wg1), wu=(d_wu0, d_wu1), wd=(d_wd0, d_wd1))
    vecs = dict(norm_mix=(d_nm0, d_nm1), norm_ffn=(d_nf0, d_nf1), final=d_final, loss_cols=loss_cols, sink=d_sink)
    return gx.reshape(x.shape), grads, vecs


ANY = pl.BlockSpec(memory_space=pl.ANY)
HBM = pltpu.MemorySpace.HBM


def _me():
    return lax.axis_index("x"), lax.axis_index("y"), lax.axis_index("c")


def _chip_peer(x, y, j):
    px = 1 - x if j & 2 else x
    py = 1 - y if j & 1 else y
    return px, py, 2 * px + py


def _remote(src, dst, sems, k, dev):
    return pltpu.make_async_remote_copy(src_ref=src, dst_ref=dst, send_sem=sems[0].at[k], recv_sem=sems[1].at[k],
                                        device_id=dev, device_id_type=MESH)


def _col_window(ref, q, width):
    return ref.at[:, pl.ds(pl.multiple_of(q * width, LANES), width)]


def _half0(ref, h):
    n = ref.shape[0] // 2
    return ref.at[pl.ds(h * n, n)]


def _half1(ref, h):
    n = ref.shape[1] // 2
    return ref.at[:, pl.ds(h * n, n)]


def _half_rows(ref, h):
    n = ref.shape[-2] // 2
    if len(ref.shape) == 2:
        return ref.at[pl.ds(h * n, n)]
    return ref.at[:, pl.ds(h * n, n)]


def _place_shard(w, layer, q_arr, col, name):
    _, rows, cols = w.shape

    def body(q_ref, w_ref, o_ref):
        o_ref[...] = w_ref[...].astype(BF16)

    if col:
        out_spec = pl.BlockSpec((rows, cols), lambda l, q: (0, q[0]))
        out_shape = jax.ShapeDtypeStruct((rows, N_CHIPS * cols), BF16)
    else:
        out_spec = pl.BlockSpec((None, None, rows, cols), lambda l, q: (q[0], 0, 0, 0))
        out_shape = jax.ShapeDtypeStruct((N_CHIPS, 1, rows, cols), BF16)
    return pl.pallas_call(
        body, name=name,
        grid_spec=pltpu.PrefetchScalarGridSpec(
            num_scalar_prefetch=1, grid=(1,),
            in_specs=[pl.BlockSpec((None, rows, cols), lambda l, q: (layer, 0, 0))], out_specs=out_spec),
        out_shape=out_shape, compiler_params=_cp(),
    )(q_arr, w)


def _handshake(peers):
    barrier = pltpu.get_barrier_semaphore()
    for p in peers:
        pl.semaphore_signal(barrier, inc=1, device_id=p, device_id_type=MESH)
    pl.semaphore_wait(barrier, len(peers))


def _on_sequencer(name, collective_id, n_sem, n_local, body):
    @pl.kernel(mesh=plsc.ScalarSubcoreMesh(axis_name="seq", num_cores=1), name=name,
               scratch_types=(pltpu.SemaphoreType.DMA((n_sem,)), pltpu.SemaphoreType.DMA((n_sem,)),
                              pltpu.SemaphoreType.DMA((max(n_local, 1),))),
               compiler_params=pltpu.CompilerParams(collective_id=collective_id))
    def launch(send_sems, recv_sems, local_sems):
        body((send_sems, recv_sems), local_sems)

    launch()


def _gather_plan(outs, col_fam, sems, handshake):
    n_w = len(outs)
    x, y, c = _me()
    myq = 2 * x + y
    sib = (x, y, 1 - c)
    if handshake:
        _handshake([sib] + [_chip_peer(x, y, j)[:2] + (c,) for j in (1, 2, 3)])

    def slot(w, q):
        if col_fam[w]:
            return _col_window(outs[w], q, outs[w].shape[1] // N_CHIPS)
        return outs[w].at[q]

    first = []
    for w in range(n_w):
        for j in (1, 2, 3):
            px, py, _ = _chip_peer(x, y, j)
            mine = _half_rows(slot(w, myq), c)
            cp = _remote(mine, mine, sems, w * 6 + j - 1, (px, py, c))
            cp.start()
            first.append(cp)
    passed = []
    for w in range(n_w):
        for j in (1, 2, 3):
            _, _, pq = _chip_peer(x, y, j)
            land = _half_rows(slot(w, pq), c)
            _remote(land, land, sems, w * 6 + j - 1, sib).wait_recv()
            cp = _remote(land, land, sems, w * 6 + 2 + j, sib)
            cp.start()
            passed.append(cp)
    for w in range(n_w):
        for j in (1, 2, 3):
            _, _, pq = _chip_peer(x, y, j)
            land = _half_rows(slot(w, pq), 1 - c)
            _remote(land, land, sems, w * 6 + 2 + j, sib).wait_recv()
    for cp in first + passed:
        cp.wait_send()


def _gather_weights(bufs, col_fam):
    n_w = len(bufs)

    def body(*refs):
        _gather_plan(refs[n_w:2 * n_w], col_fam, refs[2 * n_w:2 * n_w + 2], False)

    return pl.pallas_call(
        body, name="gather_weights", in_specs=[ANY] * n_w, out_specs=[ANY] * n_w,
        out_shape=[jax.ShapeDtypeStruct(b.shape, b.dtype) for b in bufs],
        input_output_aliases={w: w for w in range(n_w)},
        scratch_shapes=[pltpu.SemaphoreType.DMA((6 * n_w,)), pltpu.SemaphoreType.DMA((6 * n_w,))],
    )(*bufs)


def _gather_weights_async(bufs, col_fam, name, collective_id):
    refs = [jax.new_ref(b, memory_space=HBM) for b in bufs]
    _on_sequencer(name, collective_id, 6 * len(bufs), 0,
                  lambda sems, _: _gather_plan(refs, col_fam, sems, True))
    return [r[...] for r in refs]


def _grad_half(ref, col, h):
    return _half0(ref, h) if col else _half1(ref, h)


def _swap_halves_with_sibling(grads, col_fam):
    n_w = len(grads)

    def body(*refs):
        _swap_plan(refs[:n_w], refs[n_w:2 * n_w], col_fam, refs[2 * n_w:], False)

    return pl.pallas_call(
        body, name="grad_swap_sibling", in_specs=[ANY] * n_w, out_specs=[ANY] * n_w,
        out_shape=_swap_shapes(grads, col_fam),
        scratch_shapes=[pltpu.SemaphoreType.DMA((n_w,)), pltpu.SemaphoreType.DMA((n_w,))],
    )(*grads)


def _swap_shapes(grads, col_fam):
    out = []
    for w, g in enumerate(grads):
        shp = (g.shape[0] // 2, g.shape[1]) if col_fam[w] else (g.shape[0], g.shape[1] // 2, g.shape[2])
        out.append(jax.ShapeDtypeStruct(shp, g.dtype))
    return out


def _swap_plan(ins, outs, col_fam, sems, handshake):
    x, y, c = _me()
    sib = (x, y, 1 - c)
    if handshake:
        _handshake([sib])
    cps = [_remote(_grad_half(ins[w], col_fam[w], 1 - c), outs[w], sems, w, sib) for w in range(len(ins))]
    for cp in cps:
        cp.start()
    for cp in cps:
        cp.wait_recv()
    for cp in cps:
        cp.wait_send()


def _swap_halves_async(grads, col_fam, name, collective_id):
    srcs = [jax.new_ref(g, memory_space=HBM) for g in grads]
    dsts = [jax.empty_ref(s, memory_space=HBM) for s in _swap_shapes(grads, col_fam)]
    _on_sequencer(name, collective_id, len(grads), 0, lambda sems, _: _swap_plan(srcs, dsts, col_fam, sems, True))
    return [r[...] for r in srcs], [r[...] for r in dsts]


def _half_add(mines, recvs, c_arr, col_fam, name):
    n_w = len(mines)
    mine_specs, recv_specs = [], []
    for recv, col in zip(recvs, col_fam):
        if col:
            rows, n = recv.shape
            tr = rows // N_CHIPS
            mine_specs.append(pl.BlockSpec((tr, n), lambda i, c: (N_CHIPS * c[0] + i, 0)))
            recv_specs.append(pl.BlockSpec((tr, n), lambda i, c: (i, 0)))
        else:
            _, rows, n = recv.shape
            mine_specs.append(pl.BlockSpec((None, rows, n), lambda q, c: (q, c[0], 0)))
            recv_specs.append(pl.BlockSpec((None, rows, n), lambda q, c: (q, 0, 0)))

    def body(c_ref, *refs):
        for a_ref, b_ref, o_ref in zip(refs[:n_w], refs[n_w:2 * n_w], refs[2 * n_w:]):
            o_ref[...] = (a_ref[...].astype(F32) + b_ref[...].astype(F32)).astype(BF16)

    return pl.pallas_call(
        body, name=name,
        grid_spec=pltpu.PrefetchScalarGridSpec(num_scalar_prefetch=1, grid=(N_CHIPS,),
                                               in_specs=mine_specs + recv_specs, out_specs=recv_specs),
        out_shape=[jax.ShapeDtypeStruct(r.shape, BF16) for r in recvs], compiler_params=_cp(),
    )(c_arr, *mines, *recvs)


def _scatter_chip_sums(sums, col_fam):
    n_w = len(sums)

    def body(*refs):
        _scatter_plan(refs[:n_w], refs[n_w:2 * n_w], col_fam, refs[2 * n_w:2 * n_w + 2], refs[2 * n_w + 2], False)

    return pl.pallas_call(
        body, name="grad_scatter_chips", in_specs=[ANY] * n_w, out_specs=[ANY] * n_w,
        out_shape=_scatter_shapes(sums, col_fam),
        scratch_shapes=[pltpu.SemaphoreType.DMA((3 * n_w,)), pltpu.SemaphoreType.DMA((3 * n_w,)),
                        pltpu.SemaphoreType.DMA((n_w,))],
    )(*sums)


def _scatter_shapes(sums, col_fam):
    out = []
    for w, s in enumerate(sums):
        shp = (s.shape[0], s.shape[1] // N_CHIPS) if col_fam[w] else s.shape[1:]
        out.append(jax.ShapeDtypeStruct((N_CHIPS,) + shp, s.dtype))
    return out


def _scatter_plan(ins, outs, col_fam, sems, lsem, handshake):
    n_w = len(ins)
    x, y, c = _me()
    myq = 2 * x + y
    if handshake:
        _handshake([_chip_peer(x, y, j)[:2] + (c,) for j in (1, 2, 3)])

    def slab(w, q):
        if col_fam[w]:
            return _col_window(ins[w], q, ins[w].shape[1] // N_CHIPS)
        return ins[w].at[q]

    local = [pltpu.make_async_copy(slab(w, myq), outs[w].at[myq], lsem.at[w]) for w in range(n_w)]
    for cp in local:
        cp.start()
    cps = []
    for w in range(n_w):
        for j in (1, 2, 3):
            px, py, pq = _chip_peer(x, y, j)
            cp = _remote(slab(w, pq), outs[w].at[myq], sems, w * 3 + j - 1, (px, py, c))
            cp.start()
            cps.append(cp)
    for w in range(n_w):
        for j in (1, 2, 3):
            _, _, pq = _chip_peer(x, y, j)
            land = outs[w].at[pq]
            _remote(land, land, sems, w * 3 + j - 1, (x, y, c)).wait_recv()
    for cp in cps:
        cp.wait_send()
    for cp in local:
        cp.wait()


def _scatter_chip_sums_async(sums, col_fam, name, collective_id):
    srcs = [jax.new_ref(s, memory_space=HBM) for s in sums]
    dsts = [jax.empty_ref(s, memory_space=HBM) for s in _scatter_shapes(sums, col_fam)]
    _on_sequencer(name, collective_id, 3 * len(sums), len(sums),
                  lambda sems, lsem: _scatter_plan(srcs, dsts, col_fam, sems, lsem, True))
    return [r[...] for r in dsts]


def _sum_chips(parts, c_arr, prev, lead, shape, name):
    _, rows, n = parts.shape
    tr = rows // 2 if rows % 32 == 0 else rows
    nblk = rows // tr

    def body(c_ref, p_ref, *rest):
        o_ref = rest[-1]
        acc = p_ref[0].astype(F32)
        for q in range(1, N_CHIPS):
            acc = acc + p_ref[q].astype(F32)
        o_ref[...] = acc

    in_specs = [pl.BlockSpec((N_CHIPS, tr, n), lambda i, c: (0, i, 0))]
    args = [c_arr, parts]
    aliases = {}
    if prev is not None:
        in_specs.append(ANY)
        args.append(prev)
        aliases = {2: 0}
    return pl.pallas_call(
        body, name=name,
        grid_spec=pltpu.PrefetchScalarGridSpec(
            num_scalar_prefetch=1, grid=(nblk,), in_specs=in_specs,
            out_specs=pl.BlockSpec((None, tr, n), lambda i, c: (lead, c[0] * nblk + i, 0))),
        out_shape=jax.ShapeDtypeStruct(shape, F32), input_output_aliases=aliases, compiler_params=_cp(),
    )(*args)


def _join_plan(outs, place, sems, handshake):
    x, y, c = _me()
    sib = (x, y, 1 - c)
    if handshake:
        _handshake([sib])

    def half(k, h):
        o, lead = place[k]
        return _half_rows(outs[o].at[lead], h)

    cps = [_remote(half(k, c), half(k, c), sems, k, sib) for k in range(len(place))]
    for cp in cps:
        cp.start()
    for k in range(len(place)):
        land = half(k, 1 - c)
        _remote(land, land, sems, k, sib).wait_recv()
    for cp in cps:
        cp.wait_send()


def _join_halves(bufs, place, name):
    n_o = len(bufs)
    n_h = len(place)

    def body(*refs):
        _join_plan(refs[n_o:2 * n_o], place, refs[2 * n_o:2 * n_o + 2], False)

    return pl.pallas_call(
        body, name=name, in_specs=[ANY] * n_o, out_specs=[ANY] * n_o,
        out_shape=[jax.ShapeDtypeStruct(b.shape, b.dtype) for b in bufs],
        input_output_aliases={k: k for k in range(n_o)},
        scratch_shapes=[pltpu.SemaphoreType.DMA((n_h,)), pltpu.SemaphoreType.DMA((n_h,))],
    )(*bufs)


def _allreduce_rows(rows):
    n_dev = 8
    n_r = len(rows)
    assert n_r <= 8

    def body(*refs):
        r_refs = refs[:n_r]
        o_ref, slots, send_sems, recv_sems = refs[n_r:]
        x, y, c = _me()
        me = 4 * x + 2 * y + c
        slots[me] = jnp.concatenate([r[...] for r in r_refs] + [jnp.zeros((8 - n_r, D_MODEL), F32)], axis=0)

        def peer(k):
            return (1 - x if k & 4 else x, 1 - y if k & 2 else y, 1 - c if k & 1 else c)

        cps = []
        for k in range(1, n_dev):
            cp = pltpu.make_async_remote_copy(src_ref=slots.at[me], dst_ref=slots.at[me], send_sem=send_sems.at[k - 1],
                                              recv_sem=recv_sems.at[k - 1], device_id=peer(k), device_id_type=MESH)
            cp.start()
            cps.append(cp)
        for k in range(1, n_dev):
            px, py, pc = peer(k)
            land = slots.at[4 * px + 2 * py + pc]
            pltpu.make_async_remote_copy(src_ref=land, dst_ref=land, send_sem=send_sems.at[k - 1],
                                         recv_sem=recv_sems.at[k - 1], device_id=peer(k),
                                         device_id_type=MESH).wait_recv()
        for cp in cps:
            cp.wait_send()
        acc = slots[0]
        for d in range(1, n_dev):
            acc = acc + slots[d]
        o_ref[...] = acc

    vm = pl.BlockSpec(memory_space=pltpu.VMEM)
    return pl.pallas_call(
        body, name="allreduce_rows", in_specs=[vm] * n_r, out_specs=vm,
        out_shape=jax.ShapeDtypeStruct((8, D_MODEL), F32),
        scratch_shapes=[pltpu.VMEM((n_dev, 8, D_MODEL), F32), pltpu.SemaphoreType.DMA((n_dev - 1,)),
                        pltpu.SemaphoreType.DMA((n_dev - 1,))],
    )(*rows)


def _adamw(w, g, m, v, name):
    shape = w.shape
    if len(shape) == 1:
        lead, rows, cols = 1, 1, shape[0]
    else:
        rows, cols = shape[-2:]
        lead = math.prod(shape[:-2])
    args = [a.reshape(lead, rows, cols) for a in (w, g, m, v)]
    tr = rows // 2 if rows % 16 == 0 else rows

    def body(w_ref, g_ref, m_ref, v_ref, d_ref, nm_ref, nv_ref):
        gv = g_ref[...]
        nm = ADAM_B1 * m_ref[...] + (1.0 - ADAM_B1) * gv
        nv = ADAM_B2 * v_ref[...] + (1.0 - ADAM_B2) * jnp.square(gv)
        m_hat = nm / (1.0 - ADAM_B1 ** ADAM_STEP)
        v_hat = nv / (1.0 - ADAM_B2 ** ADAM_STEP)
        d_ref[...] = -ADAM_LR * (m_hat / (jnp.sqrt(v_hat) + ADAM_EPS) + ADAM_WD * w_ref[...])
        nm_ref[...] = nm
        nv_ref[...] = nv

    spec = pl.BlockSpec((None, tr, cols), lambda l, i: (l, i, 0))
    outs = pl.pallas_call(
        body, name=name, grid=(lead, rows // tr), in_specs=[spec] * 4, out_specs=[spec] * 3,
        out_shape=[jax.ShapeDtypeStruct((lead, rows, cols), F32)] * 3, compiler_params=_cp(),
    )(*args)
    return [o.reshape(shape) for o in outs]


def kernel(x, a_w_in, a_sink, a_w_out, b_w_in, b_w_out, norm_mix, norm_ffn, w_gate, w_up, w_down, final_norm, loss_target, m_a_w_in, m_a_sink, m_a_w_out, m_b_w_in, m_b_w_out, m_norm_mix, m_norm_ffn, m_w_gate, m_w_up, m_w_down, m_final_norm, v_a_w_in, v_a_sink, v_a_w_out, v_b_w_in, v_b_w_out, v_norm_mix, v_norm_ffn, v_w_gate, v_w_up, v_w_down, v_final_norm):
    weights = dict(a_w_in=a_w_in, a_sink=a_sink, a_w_out=a_w_out, b_w_in=b_w_in, b_w_out=b_w_out, norm_mix=norm_mix,
                   norm_ffn=norm_ffn, w_gate=w_gate, w_up=w_up, w_down=w_down, final_norm=final_norm)
    mom = dict(a_w_in=m_a_w_in, a_sink=m_a_sink, a_w_out=m_a_w_out, b_w_in=m_b_w_in, b_w_out=m_b_w_out,
               norm_mix=m_norm_mix, norm_ffn=m_norm_ffn, w_gate=m_w_gate, w_up=m_w_up, w_down=m_w_down,
               final_norm=m_final_norm)
    var = dict(a_w_in=v_a_w_in, a_sink=v_a_sink, a_w_out=v_a_w_out, b_w_in=v_b_w_in, b_w_out=v_b_w_out,
               norm_mix=v_norm_mix, norm_ffn=v_norm_ffn, w_gate=v_w_gate, w_up=v_w_up, w_down=v_w_down,
               final_norm=v_final_norm)
    order = ["a_w_in", "a_sink", "a_w_out", "b_w_in", "b_w_out", "norm_mix", "norm_ffn", "w_gate", "w_up", "w_down",
             "final_norm"]
    swapped = ("w_gate", "w_up")
    for n in swapped:
        weights[n], mom[n], var[n] = (a.transpose(0, 2, 1) for a in (weights[n], mom[n], var[n]))
    w_gate_t, w_up_t = weights["w_gate"], weights["w_up"]

    c_arr = lax.axis_index("c").astype(jnp.int32).reshape(1)
    q_arr = (2 * lax.axis_index("x") + lax.axis_index("y")).astype(jnp.int32).reshape(1)

    def placed(w, layer, col, nm):
        return _place_shard(w, layer, q_arr, col, f"place_{nm}")

    (a_in,) = _gather_weights_async([placed(a_w_in, 0, True, "a_in")], (True,), "gather_weights_first", 6)
    a_out, wg0, wu0, wd0 = _gather_weights_async(
        [placed(a_w_out, 0, False, "a_out"), placed(w_gate_t, 0, False, "wg0"), placed(w_up_t, 0, False, "wu0"),
         placed(w_down, 0, False, "wd0")], (False,) * 4, "gather_weights_layer0", 1)
    b_in, b_out, wg1, wu1, wd1 = _gather_weights_async(
        [placed(b_w_in, 0, True, "b_in"), placed(b_w_out, 0, False, "b_out"), placed(w_gate_t, 1, False, "wg1"),
         placed(w_up_t, 1, False, "wu1"), placed(w_down, 1, False, "wd1")], (True,) + (False,) * 4,
        "gather_weights_layer1", 7)
    a_out = a_out.reshape(D_MODEL, D_MODEL)
    b_out = b_out.reshape(D_MODEL, D_MODEL)
    wg, wu, wd = (wg0, wg1), (wu0, wu1), (wd0, wd1)

    gx, grads, vecs = _local_step(x, loss_target, a_in, a_sink[0], a_out, b_in, b_out, norm_mix, norm_ffn, wg, wu, wd,
                                  final_norm)

    rows_out = D_MODEL // N_CHIPS
    partials = [grads["a_in"], grads["b_in"],
                grads["a_out"].reshape(N_CHIPS, rows_out, D_MODEL), grads["b_out"].reshape(N_CHIPS, rows_out, D_MODEL),
                grads["wg"][0], grads["wg"][1], grads["wu"][0], grads["wu"][1], grads["wd"][0], grads["wd"][1]]
    col_fam = (True, True) + (False,) * 8
    names = ("a_in", "b_in", "a_out", "b_out", "wg0", "wg1", "wu0", "wu1", "wd0", "wd1")
    contrib = [None] * len(partials)

    def reduce_group(idx, tag, ids):
        parts = [partials[k] for k in idx]
        cols = tuple(col_fam[k] for k in idx)
        if ids is None:
            theirs = _swap_halves_with_sibling(parts, cols)
        else:
            parts, theirs = _swap_halves_async(parts, cols, f"grad_swap_{tag}", ids[0])
        sums = _half_add(parts, theirs, c_arr, cols, f"chip_sum_{tag}")
        if ids is None:
            out = _scatter_chip_sums(sums, cols)
        else:
            out = _scatter_chip_sums_async(sums, cols, f"grad_scatter_{tag}", ids[1])
        for k, o in zip(idx, out):
            contrib[k] = o

    reduce_group([1, 3, 5, 7, 9], "layer1", (2, 3))
    reduce_group([2, 4, 6, 8], "ffn0", (4, 5))
    reduce_group([0], "a_in", None)
    shapes = [a_w_in.shape, b_w_in.shape, a_w_out.shape, b_w_out.shape, w_down.shape, w_down.shape, w_down.shape]
    place = [(0, 0), (1, 0), (2, 0), (3, 0), (4, 0), (4, 1), (5, 0), (5, 1), (6, 0), (6, 1)]
    bufs = [None] * len(shapes)
    for p, nm, (o, lead) in zip(contrib, names, place):
        bufs[o] = _sum_chips(p, c_arr, bufs[o], lead, shapes[o], f"sum_chips_{nm}")
    g_a_in, g_b_in, g_a_out, g_b_out, g_wg, g_wu, g_wd = _join_halves(bufs, place, "grad_join_sibling")

    sink_row = jnp.pad(vecs["sink"][0:1], ((0, 0), (0, D_MODEL - LANES)))
    tot = _allreduce_rows([vecs["norm_mix"][0], vecs["norm_mix"][1], vecs["norm_ffn"][0], vecs["norm_ffn"][1],
                           vecs["final"], vecs["loss_cols"], sink_row])
    loss = (0.5 / D_MODEL) * jnp.sum(tot[5])
    gw = dict(a_w_in=g_a_in, a_sink=tot[6:7, :N_HEADS], a_w_out=g_a_out, b_w_in=g_b_in, b_w_out=g_b_out,
              norm_mix=tot[0:2], norm_ffn=tot[2:4], w_gate=g_wg, w_up=g_wu, w_down=g_wd, final_norm=tot[4])

    delta, new_m, new_v = {}, {}, {}
    for n in order:
        delta[n], new_m[n], new_v[n] = _adamw(weights[n], gw[n], mom[n], var[n], f"adamw_{n}")
    for n in swapped:
        gw[n], delta[n], new_m[n], new_v[n] = (a.transpose(0, 2, 1) for a in (gw[n], delta[n], new_m[n], new_v[n]))
    return (loss, gx, *[gw[n] for n in order], *[delta[n] for n in order], *[new_m[n] for n in order],
            *[new_v[n] for n in order])
```

```python
import math

import jax
import jax.numpy as jnp
import numpy as np
from jax import lax
from jax.experimental import pallas as pl
from jax.experimental.pallas import tpu as pltpu
from jax.experimental.pallas import tpu_sc as plsc

F32 = jnp.float32
BF16 = jnp.bfloat16

D_MODEL = 1024
HEAD_DIM = 64
N_HEADS = 16
N_KV = 4
QKV_W = 1536
D_FF = 2816
N_CHIPS = 4
FF_SH = D_FF // N_CHIPS
HALF_WINDOW_A = 128
DILATED = ((128, 1), (512, 4), (2048, 16))
ROPE_THETA = 10000.0
RMS_EPS = 1e-6
NEG_INF = -1e30
LANES = 128
ADAM_LR, ADAM_B1, ADAM_B2, ADAM_EPS, ADAM_WD, ADAM_STEP = 0.001, 0.9, 0.999, 1e-08, 0.01, 10
VMEM_LIMIT = 56 * 1024 * 1024
ROWS = 512
MATMUL_ROWS = 1024
FFN_BWD_ROWS = 256
LOG2E = math.log2(math.e)
LN2 = math.log(2.0)
Q_SCALE = LOG2E / math.sqrt(HEAD_DIM)
GRAD_TOKENS = 2048
MESH = pl.DeviceIdType.MESH


def _cp(**kw):
    return pltpu.CompilerParams(vmem_limit_bytes=VMEM_LIMIT, **kw)


def _row_tile(t, cap):
    tm = min(cap, t)
    assert t % tm == 0
    return tm


def _rope_tables(seq, dil):
    inv = 1.0 / (ROPE_THETA ** (np.arange(0, HEAD_DIM, 2, dtype=np.float32) / HEAD_DIM))
    ang = np.arange(seq, dtype=np.float32)[:, None] * inv.astype(np.float32)[None, :]
    cos, sin = np.cos(ang), np.sin(ang)
    cos = np.tile(cos, (1, 4))
    sin = np.concatenate([-sin, sin, -sin, sin], axis=1)

    def perm(t):
        return jnp.asarray(t.reshape(seq // dil, dil, LANES).transpose(1, 0, 2).reshape(seq, LANES), dtype=F32)

    return perm(cos), perm(sin)


def _swap_halves(t):
    lane = lax.broadcasted_iota(jnp.int32, t.shape, 1)
    return jnp.where((lane % HEAD_DIM) < HEAD_DIM // 2, pltpu.roll(t, LANES - 32, 1), pltpu.roll(t, 32, 1))


def _rope(t, cos, sin):
    return t * cos + _swap_halves(t) * sin


def _rope_t(t, cos, sin):
    return t * cos - _swap_halves(t) * sin


def _to_residue(t, batch, dil):
    if dil == 1:
        return t
    if t.ndim == 2:
        t = t.reshape(batch, t.shape[0] // batch // dil, dil, t.shape[1])
    return t.transpose(0, 2, 1, 3).reshape(-1, t.shape[-1])


def _needs_fold(dil):
    return dil > 1 and dil % 16 != 0


def _folded_shape(batch, seq, dil, cols):
    return (batch, seq // dil, dil, cols)


def _from_residue(t, batch, dil, fold=False):
    if dil == 1:
        return t
    s = t.shape[0] // batch
    nat = t.reshape(batch, dil, s // dil, t.shape[1]).transpose(0, 2, 1, 3)
    return nat if fold else nat.reshape(t.shape)


def _token_rows_spec(a, tm):
    if a.ndim == 2:
        return pl.BlockSpec((tm, a.shape[1]), lambda i: (i, 0))
    _, length, dil, c = a.shape
    per_seq = length * dil // tm
    return pl.BlockSpec((None, tm // dil, dil, c), lambda i: (i // per_seq, i % per_seq, 0, 0))


def _token_rows(ref):
    v = ref[...]
    return v if v.ndim == 2 else v.reshape(v.shape[0] * v.shape[1], v.shape[2])


def _rms_fwd(x, w, name):
    t = x.shape[0]
    tm = _row_tile(t, ROWS)

    def body(x_ref, w_ref, o_ref):
        o_ref[...] = _rms_tile(x_ref[...], w_ref[...]).astype(BF16)

    return pl.pallas_call(
        body, name=name, grid=(t // tm,),
        in_specs=[pl.BlockSpec((tm, D_MODEL), lambda i: (i, 0)), pl.BlockSpec((1, D_MODEL), lambda i: (0, 0))],
        out_specs=pl.BlockSpec((tm, D_MODEL), lambda i: (i, 0)),
        out_shape=jax.ShapeDtypeStruct((t, D_MODEL), BF16), compiler_params=_cp(),
    )(x, w)


def _rms_bwd_tile(xv, wv, dy, dres):
    r = lax.rsqrt(jnp.mean(xv * xv, axis=-1, keepdims=True) + RMS_EPS)
    xh = xv * r
    dxh = dy * wv
    dx = dres + r * (dxh - xh * jnp.mean(dxh * xh, axis=-1, keepdims=True))
    return dx, jnp.sum(dy * xh, axis=0, keepdims=True)


def _accumulate(ref, part):
    @pl.when(pl.program_id(0) == 0)
    def _():
        ref[...] = jnp.zeros_like(ref)

    ref[...] += part


def _rms_bwd(x, w, dhs, dres, name):
    t = x.shape[0]
    tm = _row_tile(t, ROWS)
    n = len(dhs)

    def body(*refs):
        x_ref, w_ref = refs[0], refs[1]
        dh_refs = refs[2:2 + n]
        dres_ref = refs[2 + n]
        dx_ref, dxb_ref, dw_ref = refs[3 + n:]
        dy = _token_rows(dh_refs[0]).astype(F32)
        for k in range(1, n):
            dy = dy + _token_rows(dh_refs[k]).astype(F32)
        dx, dw = _rms_bwd_tile(x_ref[...], w_ref[...], dy, dres_ref[...])
        dx_ref[...] = dx
        dxb_ref[...] = dx.astype(BF16)
        _accumulate(dw_ref, dw)

    row = pl.BlockSpec((tm, D_MODEL), lambda i: (i, 0))
    vec = pl.BlockSpec((1, D_MODEL), lambda i: (0, 0))
    return pl.pallas_call(
        body, name=name, grid=(t // tm,),
        in_specs=[row, vec] + [_token_rows_spec(dh, tm) for dh in dhs] + [row],
        out_specs=[row, row, vec],
        out_shape=[jax.ShapeDtypeStruct((t, D_MODEL), F32), jax.ShapeDtypeStruct((t, D_MODEL), BF16),
                   jax.ShapeDtypeStruct((1, D_MODEL), F32)],
        compiler_params=_cp(),
    )(x, w, *dhs, dres)


def _final_tile(xv, wv, tv):
    r = lax.rsqrt(jnp.mean(xv * xv, axis=-1, keepdims=True) + RMS_EPS)
    xh = xv * r
    err = xh * wv - tv
    dy = err * (1.0 / D_MODEL)
    dxh = dy * wv
    dx = r * (dxh - xh * jnp.mean(dxh * xh, axis=-1, keepdims=True))
    return dx, jnp.sum(err * err, axis=0, keepdims=True), jnp.sum(dy * xh, axis=0, keepdims=True)


def _qkv_proj(h, w, cos, sin, group, name):
    t = h.shape[0]
    seq = cos.shape[0]
    tm = _row_tile(seq, MATMUL_ROWS)
    n_q = N_HEADS * HEAD_DIM // LANES
    n_rope = (N_HEADS + N_KV) * HEAD_DIM // LANES
    scale = Q_SCALE

    def body(h_ref, w_ref, cos_ref, sin_ref, o_ref):
        acc = jnp.dot(h_ref[...], w_ref[...], preferred_element_type=F32)
        cs, sn = cos_ref[...], sin_ref[...]
        csq, snq = cs * scale, sn * scale
        for c in range(QKV_W // LANES):
            blk = acc[:, c * LANES:(c + 1) * LANES]
            if c < n_q:
                blk = _rope(blk, csq, snq)
            elif c < n_rope:
                blk = _rope(blk, cs, sn)
            o_ref[:, c * LANES:(c + 1) * LANES] = blk.astype(BF16)

    tab = pl.BlockSpec((tm, LANES), lambda i: (i % (seq // tm), 0))
    return pl.pallas_call(
        body, name=name, grid=(t // tm,),
        in_specs=[pl.BlockSpec((tm, D_MODEL), lambda i: (i, 0)),
                  pl.BlockSpec((D_MODEL, QKV_W), lambda i: (0, group)), tab, tab],
        out_specs=pl.BlockSpec((tm, QKV_W), lambda i: (i, 0)),
        out_shape=jax.ShapeDtypeStruct((t, QKV_W), BF16), compiler_params=_cp(),
    )(h, w, cos, sin)


def _rms_tile(xv, wv):
    return (xv * lax.rsqrt(jnp.mean(xv * xv, axis=-1, keepdims=True) + RMS_EPS)) * wv


def _mm_res(a, w, res, nw, name):
    t, k = a.shape
    tm = _row_tile(t, ROWS)

    def body(a_ref, w_ref, r_ref, nw_ref, o_ref, h_ref):
        xv = r_ref[...] + jnp.dot(a_ref[...], w_ref[...], preferred_element_type=F32)
        o_ref[...] = xv
        h_ref[...] = _rms_tile(xv, nw_ref[...]).astype(BF16)

    row = pl.BlockSpec((tm, D_MODEL), lambda i: (i, 0))
    return pl.pallas_call(
        body, name=name, grid=(t // tm,),
        in_specs=[pl.BlockSpec((tm, k), lambda i: (i, 0)),
                  pl.BlockSpec((k, D_MODEL), lambda i: (0, 0), pipeline_mode=pl.Buffered(1)), row,
                  pl.BlockSpec((1, D_MODEL), lambda i: (0, 0))],
        out_specs=[row, row],
        out_shape=[jax.ShapeDtypeStruct((t, D_MODEL), F32), jax.ShapeDtypeStruct((t, D_MODEL), BF16)],
        compiler_params=_cp(),
    )(a, w, res, nw)


def _mm_nt(dy, w, group, out_dtype, name):
    t, n = dy.shape
    k = w.shape[0]
    tm = _row_tile(t, MATMUL_ROWS)

    def body(dy_ref, w_ref, o_ref):
        o_ref[...] = lax.dot_general(dy_ref[...], w_ref[...], (((1,), (1,)), ((), ())),
                                     preferred_element_type=F32).astype(out_dtype)

    return pl.pallas_call(
        body, name=name, grid=(t // tm,),
        in_specs=[pl.BlockSpec((tm, n), lambda i: (i, 0)), pl.BlockSpec((k, n), lambda i: (0, group))],
        out_specs=pl.BlockSpec((tm, k), lambda i: (i, 0)),
        out_shape=jax.ShapeDtypeStruct((t, k), out_dtype), compiler_params=_cp(),
    )(dy, w)


def _mm_nt_rms(dy, w, x, nw, dres, name):
    t, n = dy.shape
    tm = _row_tile(t, ROWS)

    def body(dy_ref, w_ref, x_ref, nw_ref, dres_ref, dx_ref, dw_ref):
        dh = lax.dot_general(dy_ref[...], w_ref[...], (((1,), (1,)), ((), ())), preferred_element_type=F32)
        dx, dw = _rms_bwd_tile(x_ref[...], nw_ref[...], dh, dres_ref[...])
        dx_ref[...] = dx
        _accumulate(dw_ref, dw)

    row = pl.BlockSpec((tm, D_MODEL), lambda i: (i, 0))
    vec = pl.BlockSpec((1, D_MODEL), lambda i: (0, 0))
    return pl.pallas_call(
        body, name=name, grid=(t // tm,),
        in_specs=[pl.BlockSpec((tm, n), lambda i: (i, 0)),
                  pl.BlockSpec((D_MODEL, n), lambda i: (0, 0), pipeline_mode=pl.Buffered(1)), row, vec, row],
        out_specs=[row, vec],
        out_shape=[jax.ShapeDtypeStruct((t, D_MODEL), F32), jax.ShapeDtypeStruct((1, D_MODEL), F32)],
        compiler_params=_cp(),
    )(dy, w, x, nw, dres)


def _out_bwd(dx, w, o, name):
    t = dx.shape[0]
    tm = _row_tile(t, ROWS)

    def body(dx_ref, w_ref, o_ref, et_ref, do_ref, adj_ref):
        do = lax.dot_general(dx_ref[...], w_ref[...], (((1,), (1,)), ((), ())), preferred_element_type=F32)
        do_ref[...] = do.astype(BF16)
        adj_ref[...] = -_dot_heads(do * o_ref[...].astype(F32), et_ref[...])

    row = pl.BlockSpec((tm, D_MODEL), lambda i: (i, 0))
    return pl.pallas_call(
        body, name=name, grid=(t // tm,),
        in_specs=[row, pl.BlockSpec((D_MODEL, D_MODEL), lambda i: (0, 0)), row,
                  pl.BlockSpec((D_MODEL, LANES), lambda i: (0, 0))],
        out_specs=[row, pl.BlockSpec((tm, LANES), lambda i: (i, 0))],
        out_shape=[jax.ShapeDtypeStruct((t, D_MODEL), BF16), jax.ShapeDtypeStruct((t, LANES), F32)],
        compiler_params=_cp(),
    )(dx, w, o, _head_expander().T)


def _mm_tn(a, bs, name, part=None):
    aq = a.ndim == 3
    bq = bs[0].ndim == 3
    t, ka = a.shape[-2:]
    n = bs[0].shape[-1]
    nq = N_CHIPS if (aq or bq) else 1
    tt = _row_tile(t, GRAD_TOKENS)
    tn = n if n <= 1024 else 768
    assert n % tn == 0
    nb = len(bs)
    steps = t // tt
    carried = part is not None and part[2] is not None

    def body(*refs):
        a_ref = refs[0]
        b_refs = refs[1:1 + nb]
        o_refs = refs[1 + nb + carried:1 + 2 * nb + carried]
        acc_refs = refs[1 + 2 * nb + carried:]
        s = pl.program_id(2)
        av = a_ref[...]
        for b_ref, o_ref, acc_ref in zip(b_refs, o_refs, acc_refs):
            @pl.when(s == 0)
            def _():
                acc_ref[...] = jnp.zeros_like(acc_ref)

            acc_ref[...] += lax.dot_general(av, b_ref[...], (((0,), (0,)), ((), ())), preferred_element_type=F32)

            @pl.when(s == steps - 1)
            def _():
                o_ref[...] = acc_ref[...].astype(BF16)

    a_spec = (pl.BlockSpec((None, tt, ka), lambda q, j, s: (q, s, 0)) if aq
              else pl.BlockSpec((tt, ka), lambda q, j, s: (s, 0)))
    b_spec = (pl.BlockSpec((None, tt, tn), lambda q, j, s: (q, s, j)) if bq
              else pl.BlockSpec((tt, tn), lambda q, j, s: (s, j)))
    extra_specs, extra_args, aliases = [], [], {}
    if nq > 1:
        o_spec = pl.BlockSpec((None, ka, tn), lambda q, j, s: (q, 0, j))
        o_shape = jax.ShapeDtypeStruct((nq, ka, n), BF16)
    elif part is not None:
        assert nb == 1
        k, n_parts, buf = part
        o_spec = pl.BlockSpec((ka, tn), lambda q, j, s: (0, k * (n // tn) + j))
        o_shape = jax.ShapeDtypeStruct((ka, n_parts * n), BF16)
        if buf is not None:
            extra_specs, extra_args, aliases = [ANY], [buf], {1 + nb: 0}
    else:
        o_spec = pl.BlockSpec((ka, tn), lambda q, j, s: (0, j))
        o_shape = jax.ShapeDtypeStruct((ka, n), BF16)
    outs = pl.pallas_call(
        body, name=name, grid=(nq, n // tn, steps),
        in_specs=[a_spec] + [b_spec] * nb + extra_specs, out_specs=[o_spec] * nb, out_shape=[o_shape] * nb,
        scratch_shapes=[pltpu.VMEM((ka, tn), F32)] * nb, input_output_aliases=aliases, compiler_params=_cp(),
    )(a, *bs, *extra_args)
    return outs


def _sigmoid(x):
    return 1.0 / (1.0 + jnp.exp(-x))


def _ffn_up(h, wg, wu, layer, name):
    t = h.shape[0]
    tm = _row_tile(t, MATMUL_ROWS)
    nt = (((1,), (1,)), ((), ()))

    def body(h_ref, wg_ref, wu_ref, a_ref, dg_ref, du_ref):
        hv = h_ref[...]
        g = lax.dot_general(hv, wg_ref[...], nt, preferred_element_type=F32)
        u = lax.dot_general(hv, wu_ref[...], nt, preferred_element_type=F32)
        sg = _sigmoid(g)
        silu = g * sg
        a_ref[...] = (silu * u).astype(BF16)
        dg_ref[...] = (sg * (1.0 + g * (1.0 - sg)) * u).astype(BF16)
        du_ref[...] = silu.astype(BF16)

    wspec = pl.BlockSpec((None, None, FF_SH, D_MODEL), lambda q, i: (q, layer, 0, 0))
    ospec = pl.BlockSpec((None, tm, FF_SH), lambda q, i: (q, i, 0))
    oshape = jax.ShapeDtypeStruct((N_CHIPS, t, FF_SH), BF16)
    return pl.pallas_call(
        body, name=name, grid=(N_CHIPS, t // tm),
        in_specs=[pl.BlockSpec((tm, D_MODEL), lambda q, i: (i, 0)), wspec, wspec],
        out_specs=[ospec] * 3, out_shape=[oshape] * 3, compiler_params=_cp(),
    )(h, wg, wu)


def _ffn_down(a, wd, res, layer, name, norm_w=None, fold_shapes=(), head=None):
    t = a.shape[1]
    tm = _row_tile(t, ROWS)
    resident = pl.BlockSpec((N_CHIPS, None, FF_SH, D_MODEL), lambda i: (0, layer, 0, 0), pipeline_mode=pl.Buffered(1))
    row = pl.BlockSpec((tm, D_MODEL), lambda i: (i, 0))
    vec = pl.BlockSpec((1, D_MODEL), lambda i: (0, 0))

    def hidden(a_ref, w_ref, r_ref):
        acc = r_ref[...]
        for q in range(N_CHIPS):
            acc = acc + jnp.dot(a_ref[q], w_ref[q], preferred_element_type=F32)
        return acc

    if head is None:
        folds = [jax.ShapeDtypeStruct(s, BF16) for s in fold_shapes]

        def body(a_ref, w_ref, r_ref, nw_ref, o_ref, h_ref, *hf_refs):
            xv = hidden(a_ref, w_ref, r_ref)
            o_ref[...] = xv
            hb = _rms_tile(xv, nw_ref[...]).astype(BF16)
            h_ref[...] = hb
            for hf_ref in hf_refs:
                hf_ref[...] = hb.reshape(hf_ref.shape)

        return pl.pallas_call(
            body, name=name, grid=(t // tm,),
            in_specs=[pl.BlockSpec((N_CHIPS, tm, FF_SH), lambda i: (0, i, 0)), resident, row, vec],
            out_specs=[row, row] + [_token_rows_spec(f, tm) for f in folds],
            out_shape=[jax.ShapeDtypeStruct((t, D_MODEL), F32), jax.ShapeDtypeStruct((t, D_MODEL), BF16)] + folds,
            compiler_params=_cp(),
        )(a, wd, res, norm_w)

    def body(a_ref, w_ref, r_ref, nw_ref, t_ref, dx_ref, dxb_ref, l_ref, dw_ref):
        dx, sq, dw = _final_tile(hidden(a_ref, w_ref, r_ref), nw_ref[...], t_ref[...])
        dx_ref[...] = dx
        dxb_ref[...] = dx.astype(BF16)
        _accumulate(l_ref, sq)
        _accumulate(dw_ref, dw)

    return pl.pallas_call(
        body, name=name, grid=(t // tm,),
        in_specs=[pl.BlockSpec((N_CHIPS, tm, FF_SH), lambda i: (0, i, 0)), resident, row, vec, row],
        out_specs=[row, row, vec, vec],
        out_shape=[jax.ShapeDtypeStruct((t, D_MODEL), F32), jax.ShapeDtypeStruct((t, D_MODEL), BF16),
                   jax.ShapeDtypeStruct((1, D_MODEL), F32), jax.ShapeDtypeStruct((1, D_MODEL), F32)],
        compiler_params=_cp(),
    )(a, wd, res, *head)


def _ffn_bwd(dy, wd, wg, wu, fg, fu, x, nw, dres, name):
    t = dy.shape[0]
    tm = _row_tile(t, FFN_BWD_ROWS)
    nt = (((1,), (1,)), ((), ()))

    def body(dy_ref, wd_ref, wg_ref, wu_ref, fg_ref, fu_ref, x_ref, nw_ref, dres_ref,
             dg_ref, du_ref, dx_ref, dxb_ref, dw_ref):
        dyv = dy_ref[...]
        acc = jnp.zeros((tm, D_MODEL), F32)
        for q in range(N_CHIPS):
            da = lax.dot_general(dyv, wd_ref[q], nt, preferred_element_type=F32)
            dg = (da * fg_ref[q].astype(F32)).astype(BF16)
            du = (da * fu_ref[q].astype(F32)).astype(BF16)
            dg_ref[q] = dg
            du_ref[q] = du
            acc = acc + jnp.dot(dg, wg_ref[q], preferred_element_type=F32)
            acc = acc + jnp.dot(du, wu_ref[q], preferred_element_type=F32)
        dx, dw = _rms_bwd_tile(x_ref[...], nw_ref[...], acc, dres_ref[...])
        dx_ref[...] = dx
        dxb_ref[...] = dx.astype(BF16)
        _accumulate(dw_ref, dw)

    aspec = pl.BlockSpec((N_CHIPS, tm, FF_SH), lambda i: (0, i, 0))
    wspec = pl.BlockSpec((N_CHIPS, None, FF_SH, D_MODEL), lambda i: (0, 0, 0, 0), pipeline_mode=pl.Buffered(1))
    row = pl.BlockSpec((tm, D_MODEL), lambda i: (i, 0))
    vec = pl.BlockSpec((1, D_MODEL), lambda i: (0, 0))
    ashape = jax.ShapeDtypeStruct((N_CHIPS, t, FF_SH), BF16)
    return pl.pallas_call(
        body, name=name, grid=(t // tm,),
        in_specs=[row, wspec, wspec, wspec, aspec, aspec, row, vec, row],
        out_specs=[aspec, aspec, row, row, vec],
        out_shape=[ashape, ashape, jax.ShapeDtypeStruct((t, D_MODEL), F32), jax.ShapeDtypeStruct((t, D_MODEL), BF16),
                   jax.ShapeDtypeStruct((1, D_MODEL), F32)],
        compiler_params=_cp(),
    )(dy, wd, wg, wu, fg, fu, x, nw, dres)


def _attn_geometry(length, half_window):
    qb = min(LANES, length)
    kw = min(qb + 2 * half_window, length)
    return qb, kw, length // qb


def _dup_kv(src_ref, dst_ref, s, length):
    ch = min(length, 256)
    lo = lax.broadcasted_iota(jnp.int32, (ch, LANES), 1) < HEAD_DIM

    def chunk(c, carry):
        r0 = pl.multiple_of(c * ch, ch)
        for j in range(N_KV // 2):
            tile = src_ref[s, pl.ds(r0, ch), j * LANES:(j + 1) * LANES].astype(F32)
            rolled = pltpu.roll(tile, HEAD_DIM, 1)
            dst_ref[2 * j, pl.ds(r0, ch), :] = jnp.where(lo, tile, rolled).astype(BF16)
            dst_ref[2 * j + 1, pl.ds(r0, ch), :] = jnp.where(lo, rolled, tile).astype(BF16)
        return carry

    lax.fori_loop(0, length // ch, chunk, 0)


def _stack_heads(ref, s, q0, qb, g):
    lo = lax.broadcasted_iota(jnp.int32, (qb, LANES), 1) < HEAD_DIM
    parts = []
    for a in range(4):
        col = (2 * g + a // 2) * LANES
        tile = ref[s, pl.ds(q0, qb), col:col + LANES]
        keep = lo if a % 2 == 0 else jnp.logical_not(lo)
        parts.append(jnp.where(keep, tile, jnp.zeros_like(tile)))
    return jnp.concatenate(parts, axis=0)


def _unstack_pair_t(stacked_t, qb, pair):
    both = jnp.concatenate([stacked_t[:, (2 * pair) * qb:(2 * pair + 1) * qb],
                            stacked_t[:, (2 * pair + 1) * qb:(2 * pair + 2) * qb]], axis=0)
    return both.T


def _band_mask_t(q0, k0, qb, kw, half_window):
    key = lax.broadcasted_iota(jnp.int32, (kw, 4 * qb), 0)
    qry = lax.broadcasted_iota(jnp.int32, (kw, 4 * qb), 1) & (qb - 1)
    return jnp.abs((q0 + qry) - (k0 + key)) <= half_window


def _block_origin(i, qb, kw, half_window, length):
    if isinstance(i, int):
        return i * qb, min(max(i * qb - half_window, 0), length - kw)
    return (pl.multiple_of(i * qb, qb),
            pl.multiple_of(jnp.clip(i * qb - half_window, 0, length - kw), HEAD_DIM))


def _head_row(vals, qb):
    return jnp.concatenate([jnp.broadcast_to(v, (1, qb)).astype(F32) for v in vals], axis=1)


def _attn_fwd(qkv, sink, n_seq, length, half_window, seq_blk, out_dtype, name):
    qb, kw, nblk = _attn_geometry(length, half_window)
    with_sink = sink is not None
    nt = (((1,), (1,)), ((), ()))
    tn = (((0,), (0,)), ((), ()))
    qkv3 = qkv.reshape(n_seq, length, QKV_W)

    def body(*refs):
        refs = list(refs)
        sink_ref = refs.pop(0) if with_sink else None
        q_ref, k_ref, v_ref, o_ref, lse_ref = refs[:5]
        kx_ref, vx_ref = refs[-2:]
        head_row = lax.broadcasted_iota(jnp.int32, (N_HEADS, qb), 0)
        for s in range(seq_blk):
            _dup_kv(k_ref, kx_ref, s, length)
            _dup_kv(v_ref, vx_ref, s, length)

            def block(i, carry):
                q0, k0 = _block_origin(i, qb, kw, half_window, length)
                valid = _band_mask_t(q0, k0, qb, kw, half_window)
                lse_tile = jnp.zeros((N_HEADS, qb), F32)
                groups = range(N_KV)
                sts = [lax.dot_general(kx_ref[g, pl.ds(k0, kw), :], _stack_heads(q_ref, s, q0, qb, g), nt,
                                       preferred_element_type=F32) for g in groups]
                sts = [jnp.where(valid, st, NEG_INF) for st in sts]
                ms = [jnp.max(st, axis=0, keepdims=True) for st in sts]
                if with_sink:
                    sks = [_head_row([sink_ref[4 * g + a] * LOG2E for a in range(4)], qb) for g in groups]
                    ms = [jnp.maximum(m, sk) for m, sk in zip(ms, sks)]
                es = [jnp.exp2(st - m) for st, m in zip(sts, ms)]
                dens = [jnp.sum(e, axis=0, keepdims=True) for e in es]
                if with_sink:
                    dens = [den + jnp.exp2(sk - m) for den, sk, m in zip(dens, sks, ms)]
                ots = [lax.dot_general(vx_ref[g, pl.ds(k0, kw), 0:HEAD_DIM], es[g].astype(BF16), tn,
                                       preferred_element_type=F32) / dens[g] for g in groups]
                for g in groups:
                    for pair in range(2):
                        col = (2 * g + pair) * LANES
                        o_ref[s, pl.ds(q0, qb), col:col + LANES] = _unstack_pair_t(ots[g], qb, pair).astype(out_dtype)
                    lse = ms[g] * LN2 + jnp.log(dens[g])
                    for a in range(4):
                        lse_tile = jnp.where(head_row == 4 * g + a, lse[:, a * qb:(a + 1) * qb], lse_tile)
                lse_ref[s, :, pl.ds(q0, qb)] = lse_tile
                return carry

            if nblk == 1:
                block(0, 0)
            else:
                lax.fori_loop(0, nblk, block, 0)

    in_specs = [pl.BlockSpec((seq_blk, length, N_HEADS * HEAD_DIM), lambda n: (n, 0, 0)),
                pl.BlockSpec((seq_blk, length, N_KV * HEAD_DIM), lambda n: (n, 0, 4)),
                pl.BlockSpec((seq_blk, length, N_KV * HEAD_DIM), lambda n: (n, 0, 5))]
    args = [qkv3, qkv3, qkv3]
    if with_sink:
        in_specs.insert(0, pl.BlockSpec(memory_space=pltpu.SMEM))
        args.insert(0, sink)
    out_specs = [pl.BlockSpec((seq_blk, length, D_MODEL), lambda n: (n, 0, 0)),
                 pl.BlockSpec((seq_blk, N_HEADS, length), lambda n: (n, 0, 0))]
    out_shape = [jax.ShapeDtypeStruct((n_seq, length, D_MODEL), out_dtype),
                 jax.ShapeDtypeStruct((n_seq, N_HEADS, length), F32)]
    o, lse = pl.pallas_call(
        body, name=name, grid=(n_seq // seq_blk,), in_specs=in_specs, out_specs=out_specs, out_shape=out_shape,
        scratch_shapes=[pltpu.VMEM((N_KV, length, LANES), BF16), pltpu.VMEM((N_KV, length, LANES), BF16)],
        compiler_params=_cp(),
    )(*args)
    return o.reshape(n_seq * length, D_MODEL), lse


def _attn_bwd(qkv, do, adj, lse, sink, cos, sin, n_seq, length, half_window, seq_blk, dil, name):
    qb, kw, nblk = _attn_geometry(length, half_window)
    scale = 1.0 / math.sqrt(HEAD_DIM)
    with_sink = sink is not None
    nt = (((1,), (1,)), ((), ()))
    tn = (((0,), (0,)), ((), ()))
    qkv3 = qkv.reshape(n_seq, length, QKV_W)
    do3 = do.reshape(n_seq, length, D_MODEL)
    tabs = [t.reshape(dil, length, LANES) for t in (cos, sin)]
    tab_blocks = dil // seq_blk if dil >= seq_blk else 1

    def body(*refs):
        refs = list(refs)
        sink_ref = refs.pop(0) if with_sink else None
        q_ref, k_ref, v_ref, do_ref, aux_ref, lse_ref, cos_ref, sin_ref, dqkv_ref = refs[:9]
        ds_ref = refs[9] if with_sink else None
        kx_ref, vx_ref, dkx_ref, dvx_ref = refs[-4:]
        lane = lax.broadcasted_iota(jnp.int32, (1, LANES), 1)
        if with_sink:
            @pl.when(pl.program_id(0) == 0)
            def _():
                ds_ref[...] = jnp.zeros_like(ds_ref)

        for s in range(seq_blk):
            ts = s % dil
            _dup_kv(k_ref, kx_ref, s, length)
            _dup_kv(v_ref, vx_ref, s, length)
            dkx_ref[...] = jnp.zeros_like(dkx_ref)
            dvx_ref[...] = jnp.zeros_like(dvx_ref)

            def block(i, dsink):
                q0, k0 = _block_origin(i, qb, kw, half_window, length)
                valid = _band_mask_t(q0, k0, qb, kw, half_window)
                cs = cos_ref[ts, pl.ds(q0, qb), :] * scale
                sn = sin_ref[ts, pl.ds(q0, qb), :] * scale
                adj_tile = aux_ref[s, :, pl.ds(q0, qb)]
                lse_tile = lse_ref[s, :, pl.ds(q0, qb)]
                groups = range(N_KV)
                qss = [_stack_heads(q_ref, s, q0, qb, g) for g in groups]
                doss = [_stack_heads(do_ref, s, q0, qb, g) for g in groups]
                kxs = [kx_ref[g, pl.ds(k0, kw), :] for g in groups]
                sts = [lax.dot_general(kxs[g], qss[g], nt, preferred_element_type=F32) for g in groups]
                dpts = [lax.dot_general(vx_ref[g, pl.ds(k0, kw), :], doss[g], nt, preferred_element_type=F32)
                        for g in groups]
                lses = [_head_row([lse_tile[4 * g + a:4 * g + a + 1, :] * LOG2E for a in range(4)], qb) for g in groups]
                shifts = [_head_row([adj_tile[4 * g + a:4 * g + a + 1, :] for a in range(4)], qb) for g in groups]
                pts = [jnp.exp2(jnp.where(valid, sts[g], NEG_INF) - lses[g]) for g in groups]
                dsbs = [(pts[g] * (dpts[g] + shifts[g])).astype(BF16) for g in groups]
                pbs = [pt.astype(BF16) for pt in pts]
                if with_sink:
                    for g in groups:
                        sk = _head_row([sink_ref[4 * g + a] * LOG2E for a in range(4)], qb)
                        dsk = jnp.exp2(sk - lses[g]) * shifts[g]
                        for a in range(4):
                            tot = jnp.sum(dsk[:, a * qb:(a + 1) * qb], axis=1, keepdims=True)
                            dsink = dsink + jnp.where(lane == 4 * g + a, tot, 0.0)
                dqts = [lax.dot_general(kx_ref[g, pl.ds(k0, kw), 0:HEAD_DIM], dsbs[g], tn, preferred_element_type=F32)
                        for g in groups]
                for g in groups:
                    for pair in range(2):
                        col = (2 * g + pair) * LANES
                        tile = _rope_t(_unstack_pair_t(dqts[g], qb, pair), cs, sn)
                        dqkv_ref[s, pl.ds(q0, qb), col:col + LANES] = tile.astype(BF16)
                for g in groups:
                    dkx_ref[g, pl.ds(k0, kw), :] += jnp.dot(dsbs[g], qss[g], preferred_element_type=F32)
                    dvx_ref[g, pl.ds(k0, kw), :] += jnp.dot(pbs[g], doss[g], preferred_element_type=F32)
                return dsink

            if nblk == 1:
                dsink = block(0, jnp.zeros((1, LANES), F32))
            else:
                dsink = lax.fori_loop(0, nblk, block, jnp.zeros((1, LANES), F32))
            if with_sink:
                ds_ref[0:1, :] += dsink

            ch = min(length, 256)
            lo_c = lax.broadcasted_iota(jnp.int32, (ch, LANES), 1) < HEAD_DIM

            def fin(c, carry):
                r0 = pl.multiple_of(c * ch, ch)
                cs = cos_ref[ts, pl.ds(r0, ch), :]
                sn = sin_ref[ts, pl.ds(r0, ch), :]
                for j in range(N_KV // 2):
                    both = []
                    for acc_ref in (dkx_ref, dvx_ref):
                        t0 = acc_ref[2 * j, pl.ds(r0, ch), :]
                        t1 = acc_ref[2 * j + 1, pl.ds(r0, ch), :]
                        both.append(jnp.where(lo_c, t0, t1) + pltpu.roll(jnp.where(lo_c, t1, t0), HEAD_DIM, 1))
                    kcol = N_HEADS * HEAD_DIM + j * LANES
                    vcol = (N_HEADS + N_KV) * HEAD_DIM + j * LANES
                    dqkv_ref[s, pl.ds(r0, ch), kcol:kcol + LANES] = _rope_t(both[0] * LN2, cs, sn).astype(BF16)
                    dqkv_ref[s, pl.ds(r0, ch), vcol:vcol + LANES] = both[1].astype(BF16)
                return carry

            lax.fori_loop(0, length // ch, fin, 0)

    seq_map = lambda n: (n, 0, 0)
    tab_map = (lambda n: (n % tab_blocks, 0, 0)) if dil >= seq_blk else (lambda n: (0, 0, 0))
    tab_rows = min(seq_blk, dil)
    in_specs = [pl.BlockSpec((seq_blk, length, N_HEADS * HEAD_DIM), seq_map),
                pl.BlockSpec((seq_blk, length, N_KV * HEAD_DIM), lambda n: (n, 0, 4)),
                pl.BlockSpec((seq_blk, length, N_KV * HEAD_DIM), lambda n: (n, 0, 5)),
                pl.BlockSpec((seq_blk, length, D_MODEL), seq_map),
                pl.BlockSpec((seq_blk, N_HEADS, length), seq_map),
                pl.BlockSpec((seq_blk, N_HEADS, length), seq_map),
                pl.BlockSpec((tab_rows, length, LANES), tab_map),
                pl.BlockSpec((tab_rows, length, LANES), tab_map)]
    args = [qkv3, qkv3, qkv3, do3, adj, lse] + tabs
    if with_sink:
        in_specs.insert(0, pl.BlockSpec(memory_space=pltpu.SMEM))
        args.insert(0, sink)
    out_specs = [pl.BlockSpec((seq_blk, length, QKV_W), seq_map)]
    out_shape = [jax.ShapeDtypeStruct((n_seq, length, QKV_W), BF16)]
    if with_sink:
        out_specs.append(pl.BlockSpec((8, LANES), lambda n: (0, 0)))
        out_shape.append(jax.ShapeDtypeStruct((8, LANES), F32))
    outs = pl.pallas_call(
        body, name=name, grid=(n_seq // seq_blk,), in_specs=in_specs, out_specs=out_specs, out_shape=out_shape,
        scratch_shapes=[pltpu.VMEM((N_KV, length, LANES), BF16), pltpu.VMEM((N_KV, length, LANES), BF16),
                        pltpu.VMEM((N_KV, length, LANES), F32), pltpu.VMEM((N_KV, length, LANES), F32)],
        compiler_params=_cp(),
    )(*args)
    dqkv = outs[0].reshape(n_seq * length, QKV_W)
    return (dqkv, outs[1]) if with_sink else (dqkv, None)


def _head_expander():
    h = jnp.arange(LANES)[:, None]
    l = jnp.arange(D_MODEL)[None, :]
    return (l // HEAD_DIM == h).astype(BF16)


def _dot_split(a, e):
    hi = a.astype(BF16)
    lo = (a - hi.astype(F32)).astype(BF16)
    return jnp.dot(hi, e, preferred_element_type=F32) + jnp.dot(lo, e, preferred_element_type=F32)


def _dot_heads(a, e):
    return jnp.dot(a.astype(BF16), e, preferred_element_type=F32)


def _mix_weights(lses):
    m = jnp.maximum(jnp.maximum(lses[0], lses[1]), lses[2])
    es = [jnp.exp(v - m) for v in lses]
    tot = es[0] + es[1] + es[2]
    return [e / tot for e in es]


def _mix_fwd(os_, lses, name):
    t = os_[0].shape[0]
    tm = _row_tile(t, ROWS)

    def body(o0, o1, o2, l0, l1, l2, e_ref, out_ref):
        wts = _mix_weights([l0[...], l1[...], l2[...]])
        acc = jnp.zeros((tm, D_MODEL), F32)
        for w, o_ref in zip(wts, (o0, o1, o2)):
            acc = acc + _dot_split(w, e_ref[...]) * _token_rows(o_ref)
        out_ref[...] = acc.astype(BF16)

    row = pl.BlockSpec((tm, D_MODEL), lambda i: (i, 0))
    lrow = pl.BlockSpec((tm, LANES), lambda i: (i, 0))
    return pl.pallas_call(
        body, name=name, grid=(t // tm,),
        in_specs=[_token_rows_spec(o, tm) for o in os_] + [lrow] * 3 + [pl.BlockSpec((LANES, D_MODEL), lambda i: (0, 0))],
        out_specs=row, out_shape=jax.ShapeDtypeStruct((t, D_MODEL), BF16), compiler_params=_cp(),
    )(*os_, *lses, _head_expander())


def _mix_bwd(dx, w_out, os_, lses, do_shapes, name):
    t = dx.shape[0]
    tm = _row_tile(t, ROWS)
    do_structs = [jax.ShapeDtypeStruct(s, BF16) for s in do_shapes]

    def body(d_ref, w_ref, o0, o1, o2, l0, l1, l2, e_ref, et_ref, do0, do1, do2, a0, a1, a2):
        wts = _mix_weights([l0[...], l1[...], l2[...]])
        dv = lax.dot_general(d_ref[...], w_ref[...], (((1,), (1,)), ((), ())), preferred_element_type=F32)
        cs = [_dot_heads(dv * _token_rows(o_ref), et_ref[...]) for o_ref in (o0, o1, o2)]
        mean_c = wts[0] * cs[0] + wts[1] * cs[1] + wts[2] * cs[2]
        for w, c, do_ref, a_ref in zip(wts, cs, (do0, do1, do2), (a0, a1, a2)):
            do_ref[...] = (_dot_heads(w, e_ref[...]) * dv).astype(BF16).reshape(do_ref.shape)
            a_ref[...] = w * (c - mean_c) - w * c

    row = pl.BlockSpec((tm, D_MODEL), lambda i: (i, 0))
    lrow = pl.BlockSpec((tm, LANES), lambda i: (i, 0))
    e = _head_expander()
    return pl.pallas_call(
        body, name=name, grid=(t // tm,),
        in_specs=[row, pl.BlockSpec((D_MODEL, D_MODEL), lambda i: (0, 0), pipeline_mode=pl.Buffered(1))]
        + [_token_rows_spec(o, tm) for o in os_] + [lrow] * 3 + [pl.BlockSpec((LANES, D_MODEL), lambda i: (0, 0)),
                                    pl.BlockSpec((D_MODEL, LANES), lambda i: (0, 0))],
        out_specs=[_token_rows_spec(d, tm) for d in do_structs] + [lrow] * 3,
        out_shape=do_structs + [jax.ShapeDtypeStruct((t, LANES), F32)] * 3,
        compiler_params=_cp(),
    )(dx, w_out, *os_, *lses, e, e.T)


def _stats_to_tokens(stat, batch, dil):
    n_seq, _, length = stat.shape
    t = stat.transpose(0, 2, 1).reshape(n_seq * length, N_HEADS)
    return _from_residue(jnp.pad(t, ((0, 0), (0, LANES - N_HEADS))), batch, dil)


def _stats_from_tokens(stat, batch, dil, n_seq, length):
    t = _to_residue(stat[:, :N_HEADS], batch, dil)
    return t.reshape(n_seq, length, N_HEADS).transpose(0, 2, 1)


def _group_geometry(batch, seq, dil, window):
    length = seq // dil
    n_seq = batch * dil
    seq_blk = max(1, min(dil, 1024 // length))
    return n_seq, length, (window // 2) // dil, seq_blk


def _local_step(x, target, a_in, a_sink, a_out, b_in, b_out, norm_mix, norm_ffn, wg, wu, wd, final_norm):
    batch, seq, _ = x.shape
    t = batch * seq
    x0 = x.reshape(t, D_MODEL)
    tgt = target.reshape(t, D_MODEL)
    tabs = {d: _rope_tables(seq, d) for _, d in DILATED}
    nm = [norm_mix[i:i + 1] for i in range(2)]
    nf = [norm_ffn[i:i + 1] for i in range(2)]

    h0 = _rms_fwd(x0, nm[0], "rms_mix0")
    qkv0 = _qkv_proj(h0, a_in, *tabs[1], 0, "qkv0")
    o0, lse0 = _attn_fwd(qkv0, a_sink, batch, seq, HALF_WINDOW_A, 1, BF16, "attn0")
    x1, hf0 = _mm_res(o0, a_out, x0, nf[0], "out0")
    act0, g0, u0 = _ffn_up(hf0, wg[0], wu[0], 0, "ffn_up0")
    fold_dils = [d for _, d in DILATED if _needs_fold(d)]
    x2, h1, *h1_folded = _ffn_down(act0, wd[0], x1, 0, "ffn_down0", norm_w=nm[1],
                                   fold_shapes=[_folded_shape(batch, seq, d, D_MODEL) for d in fold_dils])
    h1_by_dil = dict(zip(fold_dils, h1_folded))

    geo = [_group_geometry(batch, seq, d, w) for w, d in DILATED]
    h1g, qkv1, o1, lse1, lse1r = [], [], [], [], []
    for gi, (_, d) in enumerate(DILATED):
        n_seq, length, hw, sb = geo[gi]
        hp = _to_residue(h1_by_dil.get(d, h1), batch, d)
        pj = _qkv_proj(hp, b_in, *tabs[d], gi, f"qkv1_{gi}")
        o, lse = _attn_fwd(pj, None, n_seq, length, hw, sb, BF16, f"attn1_{gi}")
        h1g.append(hp)
        qkv1.append(pj)
        o1.append(_from_residue(o, batch, d, fold=True))
        lse1r.append(lse)
        lse1.append(_stats_to_tokens(lse, batch, d))
    omix = _mix_fwd(o1, lse1, "mix")
    x3, hf1 = _mm_res(omix, b_out, x2, nf[1], "out1")
    act1, g1, u1 = _ffn_up(hf1, wg[1], wu[1], 0, "ffn_up1")
    dx4, dx4b, loss_cols, d_final = _ffn_down(act1, wd[1], x3, 0, "ffn_down1_loss",
                                                     head=(final_norm.reshape(1, D_MODEL), tgt))

    def ffn_bwd(dxo, dxob, x_mid, hf, g, u, act, layer):
        dg, du, dxm, dxmb, d_nf = _ffn_bwd(dxob, wd[layer], wg[layer], wu[layer], g, u, x_mid, nf[layer], dxo,
                                           f"ffn_bwd{layer}")
        (d_wd,) = _mm_tn(act, [dxob], f"grad_wd{layer}")
        (d_wgt,) = _mm_tn(dg, [hf], f"grad_wg{layer}")
        (d_wut,) = _mm_tn(du, [hf], f"grad_wu{layer}")
        return dxm, dxmb, d_nf, d_wgt, d_wut, d_wd

    dx3, dx3b, d_nf1, d_wg1, d_wu1, d_wd1 = ffn_bwd(dx4, dx4b, x3, hf1, g1, u1, act1, 1)

    (d_b_out,) = _mm_tn(omix, [dx3b], "grad_b_out")
    do_shapes = [_folded_shape(batch, seq, d, D_MODEL) if _needs_fold(d) else (t, D_MODEL) for _, d in DILATED]
    mb = _mix_bwd(dx3b, b_out, o1, lse1, do_shapes, "out1_mix_bwd")
    dh1, d_b_in = [], None
    for gi, (_, d) in enumerate(DILATED):
        n_seq, length, hw, sb = geo[gi]
        dog = _to_residue(mb[gi], batch, d)
        adj = _stats_from_tokens(mb[3 + gi], batch, d, n_seq, length)
        dpj, _ = _attn_bwd(qkv1[gi], dog, adj, lse1r[gi], None, *tabs[d], n_seq, length, hw, sb, d, f"attn1_bwd{gi}")
        (d_b_in,) = _mm_tn(h1g[gi], [dpj], f"grad_b_in{gi}", part=(gi, len(DILATED), d_b_in))
        dh1.append(_from_residue(_mm_nt(dpj, b_in, gi, BF16, f"qkv1_bwd{gi}"), batch, d, fold=True))
    dx2, dx2b, d_nm1 = _rms_bwd(x2, nm[1], dh1, dx3, "rms_mix_bwd1")

    dx1, dx1b, d_nf0, d_wg0, d_wu0, d_wd0 = ffn_bwd(dx2, dx2b, x1, hf0, g0, u0, act0, 0)

    do0, adj0 = _out_bwd(dx1b, a_out, o0, "out0_bwd")
    (d_a_out,) = _mm_tn(o0, [dx1b], "grad_a_out")
    adj0 = _stats_from_tokens(adj0, batch, 1, batch, seq)
    dqkv0, d_sink = _attn_bwd(qkv0, do0, adj0, lse0, a_sink, *tabs[1], batch, seq, HALF_WINDOW_A, 1, 1, "attn0_bwd")
    (d_a_in,) = _mm_tn(h0, [dqkv0], "grad_a_in")
    gx, d_nm0 = _mm_nt_rms(dqkv0, a_in, x0, nm[0], dx1, "qkv0_bwd")

    grads = dict(a_in=d_a_in, a_out=d_a_out, b_in=d_b_in, b_out=d_b_out,
                 wg=(d_wg0, d_wg1), wu=(d_wu0, d_wu1), wd=(d_wd0, d_wd1))
    vecs = dict(norm_mix=(d_nm0, d_nm1), norm_ffn=(d_nf0, d_nf1), final=d_final, loss_cols=loss_cols, sink=d_sink)
    return gx.reshape(x.shape), grads, vecs


ANY = pl.BlockSpec(memory_space=pl.ANY)
HBM = pltpu.MemorySpace.HBM


def _me():
    return lax.axis_index("x"), lax.axis_index("y"), lax.axis_index("c")


def _chip_peer(x, y, j):
    px = 1 - x if j & 2 else x
    py = 1 - y if j & 1 else y
    return px, py, 2 * px + py


def _remote(src, dst, sems, k, dev):
    return pltpu.make_async_remote_copy(src_ref=src, dst_ref=dst, send_sem=sems[0].at[k], recv_sem=sems[1].at[k],
                                        device_id=dev, device_id_type=MESH)


def _col_window(ref, q, width):
    return ref.at[:, pl.ds(pl.multiple_of(q * width, LANES), width)]


def _half0(ref, h):
    n = ref.shape[0] // 2
    return ref.at[pl.ds(h * n, n)]


def _half1(ref, h):
    n = ref.shape[1] // 2
    return ref.at[:, pl.ds(h * n, n)]


def _half_rows(ref, h):
    n = ref.shape[-2] // 2
    if len(ref.shape) == 2:
        return ref.at[pl.ds(h * n, n)]
    return ref.at[:, pl.ds(h * n, n)]


def _place_shard(w, layer, q_arr, col, name):
    _, rows, cols = w.shape

    def body(q_ref, w_ref, o_ref):
        o_ref[...] = w_ref[...].astype(BF16)

    if col:
        out_spec = pl.BlockSpec((rows, cols), lambda l, q: (0, q[0]))
        out_shape = jax.ShapeDtypeStruct((rows, N_CHIPS * cols), BF16)
    else:
        out_spec = pl.BlockSpec((None, None, rows, cols), lambda l, q: (q[0], 0, 0, 0))
        out_shape = jax.ShapeDtypeStruct((N_CHIPS, 1, rows, cols), BF16)
    return pl.pallas_call(
        body, name=name,
        grid_spec=pltpu.PrefetchScalarGridSpec(
            num_scalar_prefetch=1, grid=(1,),
            in_specs=[pl.BlockSpec((None, rows, cols), lambda l, q: (layer, 0, 0))], out_specs=out_spec),
        out_shape=out_shape, compiler_params=_cp(),
    )(q_arr, w)


def _handshake(peers):
    barrier = pltpu.get_barrier_semaphore()
    for p in peers:
        pl.semaphore_signal(barrier, inc=1, device_id=p, device_id_type=MESH)
    pl.semaphore_wait(barrier, len(peers))


def _on_sequencer(name, collective_id, n_sem, n_local, body):
    @pl.kernel(mesh=plsc.ScalarSubcoreMesh(axis_name="seq", num_cores=1), name=name,
               scratch_types=(pltpu.SemaphoreType.DMA((n_sem,)), pltpu.SemaphoreType.DMA((n_sem,)),
                              pltpu.SemaphoreType.DMA((max(n_local, 1),))),
               compiler_params=pltpu.CompilerParams(collective_id=collective_id))
    def launch(send_sems, recv_sems, local_sems):
        body((send_sems, recv_sems), local_sems)

    launch()


def _gather_plan(outs, col_fam, sems, handshake):
    n_w = len(outs)
    x, y, c = _me()
    myq = 2 * x + y
    sib = (x, y, 1 - c)
    if handshake:
        _handshake([sib] + [_chip_peer(x, y, j)[:2] + (c,) for j in (1, 2, 3)])

    def slot(w, q):
        if col_fam[w]:
            return _col_window(outs[w], q, outs[w].shape[1] // N_CHIPS)
        return outs[w].at[q]

    first = []
    for w in range(n_w):
        for j in (1, 2, 3):
            px, py, _ = _chip_peer(x, y, j)
            mine = _half_rows(slot(w, myq), c)
            cp = _remote(mine, mine, sems, w * 6 + j - 1, (px, py, c))
            cp.start()
            first.append(cp)
    passed = []
    for w in range(n_w):
        for j in (1, 2, 3):
            _, _, pq = _chip_peer(x, y, j)
            land = _half_rows(slot(w, pq), c)
            _remote(land, land, sems, w * 6 + j - 1, sib).wait_recv()
            cp = _remote(land, land, sems, w * 6 + 2 + j, sib)
            cp.start()
            passed.append(cp)
    for w in range(n_w):
        for j in (1, 2, 3):
            _, _, pq = _chip_peer(x, y, j)
            land = _half_rows(slot(w, pq), 1 - c)
            _remote(land, land, sems, w * 6 + 2 + j, sib).wait_recv()
    for cp in first + passed:
        cp.wait_send()


def _gather_weights(bufs, col_fam):
    n_w = len(bufs)

    def body(*refs):
        _gather_plan(refs[n_w:2 * n_w], col_fam, refs[2 * n_w:2 * n_w + 2], False)

    return pl.pallas_call(
        body, name="gather_weights", in_specs=[ANY] * n_w, out_specs=[ANY] * n_w,
        out_shape=[jax.ShapeDtypeStruct(b.shape, b.dtype) for b in bufs],
        input_output_aliases={w: w for w in range(n_w)},
        scratch_shapes=[pltpu.SemaphoreType.DMA((6 * n_w,)), pltpu.SemaphoreType.DMA((6 * n_w,))],
    )(*bufs)


def _gather_weights_async(bufs, col_fam, name, collective_id):
    refs = [jax.new_ref(b, memory_space=HBM) for b in bufs]
    _on_sequencer(name, collective_id, 6 * len(bufs), 0,
                  lambda sems, _: _gather_plan(refs, col_fam, sems, True))
    return [r[...] for r in refs]


def _grad_half(ref, col, h):
    return _half0(ref, h) if col else _half1(ref, h)


def _swap_halves_with_sibling(grads, col_fam):
    n_w = len(grads)

    def body(*refs):
        _swap_plan(refs[:n_w], refs[n_w:2 * n_w], col_fam, refs[2 * n_w:], False)

    return pl.pallas_call(
        body, name="grad_swap_sibling", in_specs=[ANY] * n_w, out_specs=[ANY] * n_w,
        out_shape=_swap_shapes(grads, col_fam),
        scratch_shapes=[pltpu.SemaphoreType.DMA((n_w,)), pltpu.SemaphoreType.DMA((n_w,))],
    )(*grads)


def _swap_shapes(grads, col_fam):
    out = []
    for w, g in enumerate(grads):
        shp = (g.shape[0] // 2, g.shape[1]) if col_fam[w] else (g.shape[0], g.shape[1] // 2, g.shape[2])
        out.append(jax.ShapeDtypeStruct(shp, g.dtype))
    return out


def _swap_plan(ins, outs, col_fam, sems, handshake):
    x, y, c = _me()
    sib = (x, y, 1 - c)
    if handshake:
        _handshake([sib])
    cps = [_remote(_grad_half(ins[w], col_fam[w], 1 - c), outs[w], sems, w, sib) for w in range(len(ins))]
    for cp in cps:
        cp.start()
    for cp in cps:
        cp.wait_recv()
    for cp in cps:
        cp.wait_send()


def _swap_halves_async(grads, col_fam, name, collective_id):
    srcs = [jax.new_ref(g, memory_space=HBM) for g in grads]
    dsts = [jax.empty_ref(s, memory_space=HBM) for s in _swap_shapes(grads, col_fam)]
    _on_sequencer(name, collective_id, len(grads), 0, lambda sems, _: _swap_plan(srcs, dsts, col_fam, sems, True))
    return [r[...] for r in srcs], [r[...] for r in dsts]


def _half_add(mines, recvs, c_arr, col_fam, name):
    n_w = len(mines)
    mine_specs, recv_specs = [], []
    for recv, col in zip(recvs, col_fam):
        if col:
            rows, n = recv.shape
            tr = rows // N_CHIPS
            mine_specs.append(pl.BlockSpec((tr, n), lambda i, c: (N_CHIPS * c[0] + i, 0)))
            recv_specs.append(pl.BlockSpec((tr, n), lambda i, c: (i, 0)))
        else:
            _, rows, n = recv.shape
            mine_specs.append(pl.BlockSpec((None, rows, n), lambda q, c: (q, c[0], 0)))
            recv_specs.append(pl.BlockSpec((None, rows, n), lambda q, c: (q, 0, 0)))

    def body(c_ref, *refs):
        for a_ref, b_ref, o_ref in zip(refs[:n_w], refs[n_w:2 * n_w], refs[2 * n_w:]):
            o_ref[...] = (a_ref[...].astype(F32) + b_ref[...].astype(F32)).astype(BF16)

    return pl.pallas_call(
        body, name=name,
        grid_spec=pltpu.PrefetchScalarGridSpec(num_scalar_prefetch=1, grid=(N_CHIPS,),
                                               in_specs=mine_specs + recv_specs, out_specs=recv_specs),
        out_shape=[jax.ShapeDtypeStruct(r.shape, BF16) for r in recvs], compiler_params=_cp(),
    )(c_arr, *mines, *recvs)


def _scatter_chip_sums(sums, col_fam):
    n_w = len(sums)

    def body(*refs):
        _scatter_plan(refs[:n_w], refs[n_w:2 * n_w], col_fam, refs[2 * n_w:2 * n_w + 2], refs[2 * n_w + 2], False)

    return pl.pallas_call(
        body, name="grad_scatter_chips", in_specs=[ANY] * n_w, out_specs=[ANY] * n_w,
        out_shape=_scatter_shapes(sums, col_fam),
        scratch_shapes=[pltpu.SemaphoreType.DMA((3 * n_w,)), pltpu.SemaphoreType.DMA((3 * n_w,)),
                        pltpu.SemaphoreType.DMA((n_w,))],
    )(*sums)


def _scatter_shapes(sums, col_fam):
    out = []
    for w, s in enumerate(sums):
        shp = (s.shape[0], s.shape[1] // N_CHIPS) if col_fam[w] else s.shape[1:]
        out.append(jax.ShapeDtypeStruct((N_CHIPS,) + shp, s.dtype))
    return out


def _scatter_plan(ins, outs, col_fam, sems, lsem, handshake):
    n_w = len(ins)
    x, y, c = _me()
    myq = 2 * x + y
    if handshake:
        _handshake([_chip_peer(x, y, j)[:2] + (c,) for j in (1, 2, 3)])

    def slab(w, q):
        if col_fam[w]:
            return _col_window(ins[w], q, ins[w].shape[1] // N_CHIPS)
        return ins[w].at[q]

    local = [pltpu.make_async_copy(slab(w, myq), outs[w].at[myq], lsem.at[w]) for w in range(n_w)]
    for cp in local:
        cp.start()
    cps = []
    for w in range(n_w):
        for j in (1, 2, 3):
            px, py, pq = _chip_peer(x, y, j)
            cp = _remote(slab(w, pq), outs[w].at[myq], sems, w * 3 + j - 1, (px, py, c))
            cp.start()
            cps.append(cp)
    for w in range(n_w):
        for j in (1, 2, 3):
            _, _, pq = _chip_peer(x, y, j)
            land = outs[w].at[pq]
            _remote(land, land, sems, w * 3 + j - 1, (x, y, c)).wait_recv()
    for cp in cps:
        cp.wait_send()
    for cp in local:
        cp.wait()


def _scatter_chip_sums_async(sums, col_fam, name, collective_id):
    srcs = [jax.new_ref(s, memory_space=HBM) for s in sums]
    dsts = [jax.empty_ref(s, memory_space=HBM) for s in _scatter_shapes(sums, col_fam)]
    _on_sequencer(name, collective_id, 3 * len(sums), len(sums),
                  lambda sems, lsem: _scatter_plan(srcs, dsts, col_fam, sems, lsem, True))
    return [r[...] for r in dsts]


def _sum_chips(parts, c_arr, prev, lead, shape, name):
    _, rows, n = parts.shape
    tr = rows // 2 if rows % 32 == 0 else rows
    nblk = rows // tr

    def body(c_ref, p_ref, *rest):
        o_ref = rest[-1]
        acc = p_ref[0].astype(F32)
        for q in range(1, N_CHIPS):
            acc = acc + p_ref[q].astype(F32)
        o_ref[...] = acc

    in_specs = [pl.BlockSpec((N_CHIPS, tr, n), lambda i, c: (0, i, 0))]
    args = [c_arr, parts]
    aliases = {}
    if prev is not None:
        in_specs.append(ANY)
        args.append(prev)
        aliases = {2: 0}
    return pl.pallas_call(
        body, name=name,
        grid_spec=pltpu.PrefetchScalarGridSpec(
            num_scalar_prefetch=1, grid=(nblk,), in_specs=in_specs,
            out_specs=pl.BlockSpec((None, tr, n), lambda i, c: (lead, c[0] * nblk + i, 0))),
        out_shape=jax.ShapeDtypeStruct(shape, F32), input_output_aliases=aliases, compiler_params=_cp(),
    )(*args)


def _join_plan(outs, place, sems, handshake):
    x, y, c = _me()
    sib = (x, y, 1 - c)
    if handshake:
        _handshake([sib])

    def half(k, h):
        o, lead = place[k]
        return _half_rows(outs[o].at[lead], h)

    cps = [_remote(half(k, c), half(k, c), sems, k, sib) for k in range(len(place))]
    for cp in cps:
        cp.start()
    for k in range(len(place)):
        land = half(k, 1 - c)
        _remote(land, land, sems, k, sib).wait_recv()
    for cp in cps:
        cp.wait_send()


def _join_halves(bufs, place, name):
    n_o = len(bufs)
    n_h = len(place)

    def body(*refs):
        _join_plan(refs[n_o:2 * n_o], place, refs[2 * n_o:2 * n_o + 2], False)

    return pl.pallas_call(
        body, name=name, in_specs=[ANY] * n_o, out_specs=[ANY] * n_o,
        out_shape=[jax.ShapeDtypeStruct(b.shape, b.dtype) for b in bufs],
        input_output_aliases={k: k for k in range(n_o)},
        scratch_shapes=[pltpu.SemaphoreType.DMA((n_h,)), pltpu.SemaphoreType.DMA((n_h,))],
    )(*bufs)


def _allreduce_rows(rows):
    n_dev = 8
    n_r = len(rows)
    assert n_r <= 8

    def body(*refs):
        r_refs = refs[:n_r]
        o_ref, slots, send_sems, recv_sems = refs[n_r:]
        x, y, c = _me()
        me = 4 * x + 2 * y + c
        slots[me] = jnp.concatenate([r[...] for r in r_refs] + [jnp.zeros((8 - n_r, D_MODEL), F32)], axis=0)

        def peer(k):
            return (1 - x if k & 4 else x, 1 - y if k & 2 else y, 1 - c if k & 1 else c)

        cps = []
        for k in range(1, n_dev):
            cp = pltpu.make_async_remote_copy(src_ref=slots.at[me], dst_ref=slots.at[me], send_sem=send_sems.at[k - 1],
                                              recv_sem=recv_sems.at[k - 1], device_id=peer(k), device_id_type=MESH)
            cp.start()
            cps.append(cp)
        for k in range(1, n_dev):
            px, py, pc = peer(k)
            land = slots.at[4 * px + 2 * py + pc]
            pltpu.make_async_remote_copy(src_ref=land, dst_ref=land, send_sem=send_sems.at[k - 1],
                                         recv_sem=recv_sems.at[k - 1], device_id=peer(k),
                                         device_id_type=MESH).wait_recv()
        for cp in cps:
            cp.wait_send()
        acc = slots[0]
        for d in range(1, n_dev):
            acc = acc + slots[d]
        o_ref[...] = acc

    vm = pl.BlockSpec(memory_space=pltpu.VMEM)
    return pl.pallas_call(
        body, name="allreduce_rows", in_specs=[vm] * n_r, out_specs=vm,
        out_shape=jax.ShapeDtypeStruct((8, D_MODEL), F32),
        scratch_shapes=[pltpu.VMEM((n_dev, 8, D_MODEL), F32), pltpu.SemaphoreType.DMA((n_dev - 1,)),
                        pltpu.SemaphoreType.DMA((n_dev - 1,))],
    )(*rows)


def _adamw(w, g, m, v, name):
    shape = w.shape
    if len(shape) == 1:
        lead, rows, cols = 1, 1, shape[0]
    else:
        rows, cols = shape[-2:]
        lead = math.prod(shape[:-2])
    args = [a.reshape(lead, rows, cols) for a in (w, g, m, v)]
    tr = rows // 2 if rows % 16 == 0 else rows

    def body(w_ref, g_ref, m_ref, v_ref, d_ref, nm_ref, nv_ref):
        gv = g_ref[...]
        nm = ADAM_B1 * m_ref[...] + (1.0 - ADAM_B1) * gv
        nv = ADAM_B2 * v_ref[...] + (1.0 - ADAM_B2) * jnp.square(gv)
        m_hat = nm / (1.0 - ADAM_B1 ** ADAM_STEP)
        v_hat = nv / (1.0 - ADAM_B2 ** ADAM_STEP)
        d_ref[...] = -ADAM_LR * (m_hat / (jnp.sqrt(v_hat) + ADAM_EPS) + ADAM_WD * w_ref[...])
        nm_ref[...] = nm
        nv_ref[...] = nv

    spec = pl.BlockSpec((None, tr, cols), lambda l, i: (l, i, 0))
    outs = pl.pallas_call(
        body, name=name, grid=(lead, rows // tr), in_specs=[spec] * 4, out_specs=[spec] * 3,
        out_shape=[jax.ShapeDtypeStruct((lead, rows, cols), F32)] * 3, compiler_params=_cp(),
    )(*args)
    return [o.reshape(shape) for o in outs]


def kernel(x, a_w_in, a_sink, a_w_out, b_w_in, b_w_out, norm_mix, norm_ffn, w_gate, w_up, w_down, final_norm, loss_target, m_a_w_in, m_a_sink, m_a_w_out, m_b_w_in, m_b_w_out, m_norm_mix, m_norm_ffn, m_w_gate, m_w_up, m_w_down, m_final_norm, v_a_w_in, v_a_sink, v_a_w_out, v_b_w_in, v_b_w_out, v_norm_mix, v_norm_ffn, v_w_gate, v_w_up, v_w_down, v_final_norm):
    weights = dict(a_w_in=a_w_in, a_sink=a_sink, a_w_out=a_w_out, b_w_in=b_w_in, b_w_out=b_w_out, norm_mix=norm_mix,
                   norm_ffn=norm_ffn, w_gate=w_gate, w_up=w_up, w_down=w_down, final_norm=final_norm)
    mom = dict(a_w_in=m_a_w_in, a_sink=m_a_sink, a_w_out=m_a_w_out, b_w_in=m_b_w_in, b_w_out=m_b_w_out,
               norm_mix=m_norm_mix, norm_ffn=m_norm_ffn, w_gate=m_w_gate, w_up=m_w_up, w_down=m_w_down,
               final_norm=m_final_norm)
    var = dict(a_w_in=v_a_w_in, a_sink=v_a_sink, a_w_out=v_a_w_out, b_w_in=v_b_w_in, b_w_out=v_b_w_out,
               norm_mix=v_norm_mix, norm_ffn=v_norm_ffn, w_gate=v_w_gate, w_up=v_w_up, w_down=v_w_down,
               final_norm=v_final_norm)
    order = ["a_w_in", "a_sink", "a_w_out", "b_w_in", "b_w_out", "norm_mix", "norm_ffn", "w_gate", "w_up", "w_down",
             "final_norm"]
    swapped = ("w_gate", "w_up")
    for n in swapped:
        weights[n], mom[n], var[n] = (a.transpose(0, 2, 1) for a in (weights[n], mom[n], var[n]))
    w_gate_t, w_up_t = weights["w_gate"], weights["w_up"]

    c_arr = lax.axis_index("c").astype(jnp.int32).reshape(1)
    q_arr = (2 * lax.axis_index("x") + lax.axis_index("y")).astype(jnp.int32).reshape(1)

    def placed(w, layer, col, nm):
        return _place_shard(w, layer, q_arr, col, f"place_{nm}")

    (a_in,) = _gather_weights_async([placed(a_w_in, 0, True, "a_in")], (True,), "gather_weights_first", 6)
    a_out, wg0, wu0, wd0 = _gather_weights_async(
        [placed(a_w_out, 0, False, "a_out"), placed(w_gate_t, 0, False, "wg0"), placed(w_up_t, 0, False, "wu0"),
         placed(w_down, 0, False, "wd0")], (False,) * 4, "gather_weights_layer0", 1)
    b_in, b_out, wg1, wu1, wd1 = _gather_weights_async(
        [placed(b_w_in, 0, True, "b_in"), placed(b_w_out, 0, False, "b_out"), placed(w_gate_t, 1, False, "wg1"),
         placed(w_up_t, 1, False, "wu1"), placed(w_down, 1, False, "wd1")], (True,) + (False,) * 4,
        "gather_weights_layer1", 7)
    a_out = a_out.reshape(D_MODEL, D_MODEL)
    b_out = b_out.reshape(D_MODEL, D_MODEL)
    wg, wu, wd = (wg0, wg1), (wu0, wu1), (wd0, wd1)

    gx, grads, vecs = _local_step(x, loss_target, a_in, a_sink[0], a_out, b_in, b_out, norm_mix, norm_ffn, wg, wu, wd,
                                  final_norm)

    rows_out = D_MODEL // N_CHIPS
    partials = [grads["a_in"], grads["b_in"],
                grads["a_out"].reshape(N_CHIPS, rows_out, D_MODEL), grads["b_out"].reshape(N_CHIPS, rows_out, D_MODEL),
                grads["wg"][0], grads["wg"][1], grads["wu"][0], grads["wu"][1], grads["wd"][0], grads["wd"][1]]
    col_fam = (True, True) + (False,) * 8
    names = ("a_in", "b_in", "a_out", "b_out", "wg0", "wg1", "wu0", "wu1", "wd0", "wd1")
    contrib = [None] * len(partials)

    def reduce_group(idx, tag, ids):
        parts = [partials[k] for k in idx]
        cols = tuple(col_fam[k] for k in idx)
        if ids is None:
            theirs = _swap_halves_with_sibling(parts, cols)
        else:
            parts, theirs = _swap_halves_async(parts, cols, f"grad_swap_{tag}", ids[0])
        sums = _half_add(parts, theirs, c_arr, cols, f"chip_sum_{tag}")
        if ids is None:
            out = _scatter_chip_sums(sums, cols)
        else:
            out = _scatter_chip_sums_async(sums, cols, f"grad_scatter_{tag}", ids[1])
        for k, o in zip(idx, out):
            contrib[k] = o

    reduce_group([1, 3, 5, 7, 9], "layer1", (2, 3))
    reduce_group([2, 4, 6, 8], "ffn0", (4, 5))
    reduce_group([0], "a_in", None)
    shapes = [a_w_in.shape, b_w_in.shape, a_w_out.shape, b_w_out.shape, w_down.shape, w_down.shape, w_down.shape]
    place = [(0, 0), (1, 0), (2, 0), (3, 0), (4, 0), (4, 1), (5, 0), (5, 1), (6, 0), (6, 1)]
    bufs = [None] * len(shapes)
    for p, nm, (o, lead) in zip(contrib, names, place):
        bufs[o] = _sum_chips(p, c_arr, bufs[o], lead, shapes[o], f"sum_chips_{nm}")
    g_a_in, g_b_in, g_a_out, g_b_out, g_wg, g_wu, g_wd = _join_halves(bufs, place, "grad_join_sibling")

    sink_row = jnp.pad(vecs["sink"][0:1], ((0, 0), (0, D_MODEL - LANES)))
    tot = _allreduce_rows([vecs["norm_mix"][0], vecs["norm_mix"][1], vecs["norm_ffn"][0], vecs["norm_ffn"][1],
                           vecs["final"], vecs["loss_cols"], sink_row])
    loss = (0.5 / D_MODEL) * jnp.sum(tot[5])
    gw = dict(a_w_in=g_a_in, a_sink=tot[6:7, :N_HEADS], a_w_out=g_a_out, b_w_in=g_b_in, b_w_out=g_b_out,
              norm_mix=tot[0:2], norm_ffn=tot[2:4], w_gate=g_wg, w_up=g_wu, w_down=g_wd, final_norm=tot[4])

    delta, new_m, new_v = {}, {}, {}
    for n in order:
        delta[n], new_m[n], new_v[n] = _adamw(weights[n], gw[n], mom[n], var[n], f"adamw_{n}")
    for n in swapped:
        gw[n], delta[n], new_m[n], new_v[n] = (a.transpose(0, 2, 1) for a in (gw[n], delta[n], new_m[n], new_v[n]))
    return (loss, gx, *[gw[n] for n in order], *[delta[n] for n in order], *[new_m[n] for n in order],
            *[new_v[n] for n in order])
```

```python
import math

import jax
import jax.numpy as jnp
import numpy as np
from jax import lax
from jax.experimental import pallas as pl
from jax.experimental.pallas import tpu as pltpu
from jax.experimental.pallas import tpu_sc as plsc

F32 = jnp.float32
BF16 = jnp.bfloat16

D_MODEL = 1024
HEAD_DIM = 64
N_HEADS = 16
N_KV = 4
QKV_W = 1536
D_FF = 2816
N_CHIPS = 4
FF_SH = D_FF // N_CHIPS
HALF_WINDOW_A = 128
DILATED = ((128, 1), (512, 4), (2048, 16))
ROPE_THETA = 10000.0
RMS_EPS = 1e-6
NEG_INF = -1e30
LANES = 128
ADAM_LR, ADAM_B1, ADAM_B2, ADAM_EPS, ADAM_WD, ADAM_STEP = 0.001, 0.9, 0.999, 1e-08, 0.01, 10
VMEM_LIMIT = 56 * 1024 * 1024
ROWS = 512
MATMUL_ROWS = 1024
FFN_BWD_ROWS = 256
LOG2E = math.log2(math.e)
LN2 = math.log(2.0)
Q_SCALE = LOG2E / math.sqrt(HEAD_DIM)
GRAD_TOKENS = 2048
MESH = pl.DeviceIdType.MESH


def _cp(**kw):
    return pltpu.CompilerParams(vmem_limit_bytes=VMEM_LIMIT, **kw)


def _row_tile(t, cap):
    tm = min(cap, t)
    assert t % tm == 0
    return tm


def _rope_tables(seq, dil):
    inv = 1.0 / (ROPE_THETA ** (np.arange(0, HEAD_DIM, 2, dtype=np.float32) / HEAD_DIM))
    ang = np.arange(seq, dtype=np.float32)[:, None] * inv.astype(np.float32)[None, :]
    cos, sin = np.cos(ang), np.sin(ang)
    cos = np.tile(cos, (1, 4))
    sin = np.concatenate([-sin, sin, -sin, sin], axis=1)

    def perm(t):
        return jnp.asarray(t.reshape(seq // dil, dil, LANES).transpose(1, 0, 2).reshape(seq, LANES), dtype=F32)

    return perm(cos), perm(sin)


def _swap_halves(t):
    lane = lax.broadcasted_iota(jnp.int32, t.shape, 1)
    return jnp.where((lane % HEAD_DIM) < HEAD_DIM // 2, pltpu.roll(t, LANES - 32, 1), pltpu.roll(t, 32, 1))


def _rope(t, cos, sin):
    return t * cos + _swap_halves(t) * sin


def _rope_t(t, cos, sin):
    return t * cos - _swap_halves(t) * sin


def _to_residue(t, batch, dil):
    if dil == 1:
        return t
    if t.ndim == 2:
        t = t.reshape(batch, t.shape[0] // batch // dil, dil, t.shape[1])
    return t.transpose(0, 2, 1, 3).reshape(-1, t.shape[-1])


def _needs_fold(dil):
    return dil > 1 and dil % 16 != 0


def _folded_shape(batch, seq, dil, cols):
    return (batch, seq // dil, dil, cols)


def _from_residue(t, batch, dil, fold=False):
    if dil == 1:
        return t
    s = t.shape[0] // batch
    nat = t.reshape(batch, dil, s // dil, t.shape[1]).transpose(0, 2, 1, 3)
    return nat if fold else nat.reshape(t.shape)


def _token_rows_spec(a, tm):
    if a.ndim == 2:
        return pl.BlockSpec((tm, a.shape[1]), lambda i: (i, 0))
    _, length, dil, c = a.shape
    per_seq = length * dil // tm
    return pl.BlockSpec((None, tm // dil, dil, c), lambda i: (i // per_seq, i % per_seq, 0, 0))


def _token_rows(ref):
    v = ref[...]
    return v if v.ndim == 2 else v.reshape(v.shape[0] * v.shape[1], v.shape[2])


def _rms_fwd(x, w, name):
    t = x.shape[0]
    tm = _row_tile(t, ROWS)

    def body(x_ref, w_ref, o_ref):
        o_ref[...] = _rms_tile(x_ref[...], w_ref[...]).astype(BF16)

    return pl.pallas_call(
        body, name=name, grid=(t // tm,),
        in_specs=[pl.BlockSpec((tm, D_MODEL), lambda i: (i, 0)), pl.BlockSpec((1, D_MODEL), lambda i: (0, 0))],
        out_specs=pl.BlockSpec((tm, D_MODEL), lambda i: (i, 0)),
        out_shape=jax.ShapeDtypeStruct((t, D_MODEL), BF16), compiler_params=_cp(),
    )(x, w)


def _rms_bwd_tile(xv, wv, dy, dres):
    r = lax.rsqrt(jnp.mean(xv * xv, axis=-1, keepdims=True) + RMS_EPS)
    xh = xv * r
    dxh = dy * wv
    dx = dres + r * (dxh - xh * jnp.mean(dxh * xh, axis=-1, keepdims=True))
    return dx, jnp.sum(dy * xh, axis=0, keepdims=True)


def _accumulate(ref, part):
    @pl.when(pl.program_id(0) == 0)
    def _():
        ref[...] = jnp.zeros_like(ref)

    ref[...] += part


def _rms_bwd(x, w, dhs, dres, name):
    t = x.shape[0]
    tm = _row_tile(t, ROWS)
    n = len(dhs)

    def body(*refs):
        x_ref, w_ref = refs[0], refs[1]
        dh_refs = refs[2:2 + n]
        dres_ref = refs[2 + n]
        dx_ref, dxb_ref, dw_ref = refs[3 + n:]
        dy = _token_rows(dh_refs[0]).astype(F32)
        for k in range(1, n):
            dy = dy + _token_rows(dh_refs[k]).astype(F32)
        dx, dw = _rms_bwd_tile(x_ref[...], w_ref[...], dy, dres_ref[...])
        dx_ref[...] = dx
        dxb_ref[...] = dx.astype(BF16)
        _accumulate(dw_ref, dw)

    row = pl.BlockSpec((tm, D_MODEL), lambda i: (i, 0))
    vec = pl.BlockSpec((1, D_MODEL), lambda i: (0, 0))
    return pl.pallas_call(
        body, name=name, grid=(t // tm,),
        in_specs=[row, vec] + [_token_rows_spec(dh, tm) for dh in dhs] + [row],
        out_specs=[row, row, vec],
        out_shape=[jax.ShapeDtypeStruct((t, D_MODEL), F32), jax.ShapeDtypeStruct((t, D_MODEL), BF16),
                   jax.ShapeDtypeStruct((1, D_MODEL), F32)],
        compiler_params=_cp(),
    )(x, w, *dhs, dres)


def _final_tile(xv, wv, tv):
    r = lax.rsqrt(jnp.mean(xv * xv, axis=-1, keepdims=True) + RMS_EPS)
    xh = xv * r
    err = xh * wv - tv
    dy = err * (1.0 / D_MODEL)
    dxh = dy * wv
    dx = r * (dxh - xh * jnp.mean(dxh * xh, axis=-1, keepdims=True))
    return dx, jnp.sum(err * err, axis=0, keepdims=True), jnp.sum(dy * xh, axis=0, keepdims=True)


def _qkv_proj(h, w, cos, sin, group, name):
    t = h.shape[0]
    seq = cos.shape[0]
    tm = _row_tile(seq, MATMUL_ROWS)
    n_q = N_HEADS * HEAD_DIM // LANES
    n_rope = (N_HEADS + N_KV) * HEAD_DIM // LANES
    scale = Q_SCALE

    def body(h_ref, w_ref, cos_ref, sin_ref, o_ref):
        acc = jnp.dot(h_ref[...], w_ref[...], preferred_element_type=F32)
        cs, sn = cos_ref[...], sin_ref[...]
        csq, snq = cs * scale, sn * scale
        for c in range(QKV_W // LANES):
            blk = acc[:, c * LANES:(c + 1) * LANES]
            if c < n_q:
                blk = _rope(blk, csq, snq)
            elif c < n_rope:
                blk = _rope(blk, cs, sn)
            o_ref[:, c * LANES:(c + 1) * LANES] = blk.astype(BF16)

    tab = pl.BlockSpec((tm, LANES), lambda i: (i % (seq // tm), 0))
    return pl.pallas_call(
        body, name=name, grid=(t // tm,),
        in_specs=[pl.BlockSpec((tm, D_MODEL), lambda i: (i, 0)),
                  pl.BlockSpec((D_MODEL, QKV_W), lambda i: (0, group)), tab, tab],
        out_specs=pl.BlockSpec((tm, QKV_W), lambda i: (i, 0)),
        out_shape=jax.ShapeDtypeStruct((t, QKV_W), BF16), compiler_params=_cp(),
    )(h, w, cos, sin)


def _rms_tile(xv, wv):
    return (xv * lax.rsqrt(jnp.mean(xv * xv, axis=-1, keepdims=True) + RMS_EPS)) * wv


def _mm_res(a, w, res, nw, name):
    t, k = a.shape
    tm = _row_tile(t, ROWS)

    def body(a_ref, w_ref, r_ref, nw_ref, o_ref, h_ref):
        xv = r_ref[...] + jnp.dot(a_ref[...], w_ref[...], preferred_element_type=F32)
        o_ref[...] = xv
        h_ref[...] = _rms_tile(xv, nw_ref[...]).astype(BF16)

    row = pl.BlockSpec((tm, D_MODEL), lambda i: (i, 0))
    return pl.pallas_call(
        body, name=name, grid=(t // tm,),
        in_specs=[pl.BlockSpec((tm, k), lambda i: (i, 0)),
                  pl.BlockSpec((k, D_MODEL), lambda i: (0, 0), pipeline_mode=pl.Buffered(1)), row,
                  pl.BlockSpec((1, D_MODEL), lambda i: (0, 0))],
        out_specs=[row, row],
        out_shape=[jax.ShapeDtypeStruct((t, D_MODEL), F32), jax.ShapeDtypeStruct((t, D_MODEL), BF16)],
        compiler_params=_cp(),
    )(a, w, res, nw)


def _mm_nt(dy, w, group, out_dtype, name):
    t, n = dy.shape
    k = w.shape[0]
    tm = _row_tile(t, MATMUL_ROWS)

    def body(dy_ref, w_ref, o_ref):
        o_ref[...] = lax.dot_general(dy_ref[...], w_ref[...], (((1,), (1,)), ((), ())),
                                     preferred_element_type=F32).astype(out_dtype)

    return pl.pallas_call(
        body, name=name, grid=(t // tm,),
        in_specs=[pl.BlockSpec((tm, n), lambda i: (i, 0)), pl.BlockSpec((k, n), lambda i: (0, group))],
        out_specs=pl.BlockSpec((tm, k), lambda i: (i, 0)),
        out_shape=jax.ShapeDtypeStruct((t, k), out_dtype), compiler_params=_cp(),
    )(dy, w)


def _mm_nt_rms(dy, w, x, nw, dres, name, scatter=None):
    t, n = dy.shape
    tm = _row_tile(t, ROWS)
    steps = t // tm
    sums, col_fam = scatter if scatter else ((), ())
    n_w = len(sums)

    def body(dy_ref, w_ref, x_ref, nw_ref, dres_ref, *rest):
        ins, (dx_ref, dw_ref), outs = rest[:n_w], rest[n_w:n_w + 2], rest[n_w + 2:2 * n_w + 2]
        sems, lsem = rest[2 * n_w + 2:2 * n_w + 4], rest[-1]
        if n_w:
            pl.when(pl.program_id(0) == 0)(lambda: _scatter_start(ins, outs, col_fam, sems, lsem))
        dh = lax.dot_general(dy_ref[...], w_ref[...], (((1,), (1,)), ((), ())), preferred_element_type=F32)
        dx, dw = _rms_bwd_tile(x_ref[...], nw_ref[...], dh, dres_ref[...])
        dx_ref[...] = dx
        _accumulate(dw_ref, dw)
        if n_w:
            pl.when(pl.program_id(0) == steps - 1)(lambda: _scatter_wait(ins, outs, col_fam, sems, lsem))

    row = pl.BlockSpec((tm, D_MODEL), lambda i: (i, 0))
    vec = pl.BlockSpec((1, D_MODEL), lambda i: (0, 0))
    scratch = [pltpu.SemaphoreType.DMA((3 * n_w,)), pltpu.SemaphoreType.DMA((3 * n_w,)),
               pltpu.SemaphoreType.DMA((n_w,))] if n_w else []
    return pl.pallas_call(
        body, name=name, grid=(steps,),
        in_specs=[pl.BlockSpec((tm, n), lambda i: (i, 0)),
                  pl.BlockSpec((D_MODEL, n), lambda i: (0, 0), pipeline_mode=pl.Buffered(1)), row, vec, row]
        + [ANY] * n_w,
        out_specs=[row, vec] + [ANY] * n_w,
        out_shape=[jax.ShapeDtypeStruct((t, D_MODEL), F32), jax.ShapeDtypeStruct((1, D_MODEL), F32)]
        + (_scatter_shapes(sums, col_fam) if n_w else []),
        scratch_shapes=scratch, compiler_params=_cp(),
    )(dy, w, x, nw, dres, *sums)


def _out_bwd(dx, w, o, name):
    t = dx.shape[0]
    tm = _row_tile(t, ROWS)

    def body(dx_ref, w_ref, o_ref, et_ref, do_ref, adj_ref):
        do = lax.dot_general(dx_ref[...], w_ref[...], (((1,), (1,)), ((), ())), preferred_element_type=F32)
        do_ref[...] = do.astype(BF16)
        adj_ref[...] = -_dot_heads(do * o_ref[...].astype(F32), et_ref[...])

    row = pl.BlockSpec((tm, D_MODEL), lambda i: (i, 0))
    return pl.pallas_call(
        body, name=name, grid=(t // tm,),
        in_specs=[row, pl.BlockSpec((D_MODEL, D_MODEL), lambda i: (0, 0)), row,
                  pl.BlockSpec((D_MODEL, LANES), lambda i: (0, 0))],
        out_specs=[row, pl.BlockSpec((tm, LANES), lambda i: (i, 0))],
        out_shape=[jax.ShapeDtypeStruct((t, D_MODEL), BF16), jax.ShapeDtypeStruct((t, LANES), F32)],
        compiler_params=_cp(),
    )(dx, w, o, _head_expander().T)


def _mm_tn(a, bs, name, part=None):
    aq = a.ndim == 3
    bq = bs[0].ndim == 3
    t, ka = a.shape[-2:]
    n = bs[0].shape[-1]
    nq = N_CHIPS if (aq or bq) else 1
    tt = _row_tile(t, GRAD_TOKENS)
    tn = n if n <= 1024 else 768
    assert n % tn == 0
    nb = len(bs)
    steps = t // tt
    carried = part is not None and part[2] is not None

    def body(*refs):
        a_ref = refs[0]
        b_refs = refs[1:1 + nb]
        o_refs = refs[1 + nb + carried:1 + 2 * nb + carried]
        acc_refs = refs[1 + 2 * nb + carried:]
        s = pl.program_id(2)
        av = a_ref[...]
        for b_ref, o_ref, acc_ref in zip(b_refs, o_refs, acc_refs):
            @pl.when(s == 0)
            def _():
                acc_ref[...] = jnp.zeros_like(acc_ref)

            acc_ref[...] += lax.dot_general(av, b_ref[...], (((0,), (0,)), ((), ())), preferred_element_type=F32)

            @pl.when(s == steps - 1)
            def _():
                o_ref[...] = acc_ref[...].astype(BF16)

    a_spec = (pl.BlockSpec((None, tt, ka), lambda q, j, s: (q, s, 0)) if aq
              else pl.BlockSpec((tt, ka), lambda q, j, s: (s, 0)))
    b_spec = (pl.BlockSpec((None, tt, tn), lambda q, j, s: (q, s, j)) if bq
              else pl.BlockSpec((tt, tn), lambda q, j, s: (s, j)))
    extra_specs, extra_args, aliases = [], [], {}
    if nq > 1:
        o_spec = pl.BlockSpec((None, ka, tn), lambda q, j, s: (q, 0, j))
        o_shape = jax.ShapeDtypeStruct((nq, ka, n), BF16)
    elif part is not None:
        assert nb == 1
        k, n_parts, buf = part
        o_spec = pl.BlockSpec((ka, tn), lambda q, j, s: (0, k * (n // tn) + j))
        o_shape = jax.ShapeDtypeStruct((ka, n_parts * n), BF16)
        if buf is not None:
            extra_specs, extra_args, aliases = [ANY], [buf], {1 + nb: 0}
    else:
        o_spec = pl.BlockSpec((ka, tn), lambda q, j, s: (0, j))
        o_shape = jax.ShapeDtypeStruct((ka, n), BF16)
    outs = pl.pallas_call(
        body, name=name, grid=(nq, n // tn, steps),
        in_specs=[a_spec] + [b_spec] * nb + extra_specs, out_specs=[o_spec] * nb, out_shape=[o_shape] * nb,
        scratch_shapes=[pltpu.VMEM((ka, tn), F32)] * nb, input_output_aliases=aliases, compiler_params=_cp(),
    )(a, *bs, *extra_args)
    return outs


def _sigmoid(x):
    return 1.0 / (1.0 + jnp.exp(-x))


def _ffn_up(h, wg, wu, layer, name):
    t = h.shape[0]
    tm = _row_tile(t, MATMUL_ROWS)
    nt = (((1,), (1,)), ((), ()))

    def body(h_ref, wg_ref, wu_ref, a_ref, dg_ref, du_ref):
        hv = h_ref[...]
        g = lax.dot_general(hv, wg_ref[...], nt, preferred_element_type=F32)
        u = lax.dot_general(hv, wu_ref[...], nt, preferred_element_type=F32)
        sg = _sigmoid(g)
        silu = g * sg
        a_ref[...] = (silu * u).astype(BF16)
        dg_ref[...] = (sg * (1.0 + g * (1.0 - sg)) * u).astype(BF16)
        du_ref[...] = silu.astype(BF16)

    wspec = pl.BlockSpec((None, None, FF_SH, D_MODEL), lambda q, i: (q, layer, 0, 0))
    ospec = pl.BlockSpec((None, tm, FF_SH), lambda q, i: (q, i, 0))
    oshape = jax.ShapeDtypeStruct((N_CHIPS, t, FF_SH), BF16)
    return pl.pallas_call(
        body, name=name, grid=(N_CHIPS, t // tm),
        in_specs=[pl.BlockSpec((tm, D_MODEL), lambda q, i: (i, 0)), wspec, wspec],
        out_specs=[ospec] * 3, out_shape=[oshape] * 3, compiler_params=_cp(),
    )(h, wg, wu)


def _ffn_down(a, wd, res, layer, name, norm_w=None, fold_shapes=(), head=None):
    t = a.shape[1]
    tm = _row_tile(t, ROWS)
    resident = pl.BlockSpec((N_CHIPS, None, FF_SH, D_MODEL), lambda i: (0, layer, 0, 0), pipeline_mode=pl.Buffered(1))
    row = pl.BlockSpec((tm, D_MODEL), lambda i: (i, 0))
    vec = pl.BlockSpec((1, D_MODEL), lambda i: (0, 0))

    def hidden(a_ref, w_ref, r_ref):
        acc = r_ref[...]
        for q in range(N_CHIPS):
            acc = acc + jnp.dot(a_ref[q], w_ref[q], preferred_element_type=F32)
        return acc

    if head is None:
        folds = [jax.ShapeDtypeStruct(s, BF16) for s in fold_shapes]

        def body(a_ref, w_ref, r_ref, nw_ref, o_ref, h_ref, *hf_refs):
            xv = hidden(a_ref, w_ref, r_ref)
            o_ref[...] = xv
            hb = _rms_tile(xv, nw_ref[...]).astype(BF16)
            h_ref[...] = hb
            for hf_ref in hf_refs:
                hf_ref[...] = hb.reshape(hf_ref.shape)

        return pl.pallas_call(
            body, name=name, grid=(t // tm,),
            in_specs=[pl.BlockSpec((N_CHIPS, tm, FF_SH), lambda i: (0, i, 0)), resident, row, vec],
            out_specs=[row, row] + [_token_rows_spec(f, tm) for f in folds],
            out_shape=[jax.ShapeDtypeStruct((t, D_MODEL), F32), jax.ShapeDtypeStruct((t, D_MODEL), BF16)] + folds,
            compiler_params=_cp(),
        )(a, wd, res, norm_w)

    def body(a_ref, w_ref, r_ref, nw_ref, t_ref, dx_ref, dxb_ref, l_ref, dw_ref):
        dx, sq, dw = _final_tile(hidden(a_ref, w_ref, r_ref), nw_ref[...], t_ref[...])
        dx_ref[...] = dx
        dxb_ref[...] = dx.astype(BF16)
        _accumulate(l_ref, sq)
        _accumulate(dw_ref, dw)

    return pl.pallas_call(
        body, name=name, grid=(t // tm,),
        in_specs=[pl.BlockSpec((N_CHIPS, tm, FF_SH), lambda i: (0, i, 0)), resident, row, vec, row],
        out_specs=[row, row, vec, vec],
        out_shape=[jax.ShapeDtypeStruct((t, D_MODEL), F32), jax.ShapeDtypeStruct((t, D_MODEL), BF16),
                   jax.ShapeDtypeStruct((1, D_MODEL), F32), jax.ShapeDtypeStruct((1, D_MODEL), F32)],
        compiler_params=_cp(),
    )(a, wd, res, *head)


def _ffn_bwd(dy, wd, wg, wu, fg, fu, x, nw, dres, name):
    t = dy.shape[0]
    tm = _row_tile(t, FFN_BWD_ROWS)
    nt = (((1,), (1,)), ((), ()))

    def body(dy_ref, wd_ref, wg_ref, wu_ref, fg_ref, fu_ref, x_ref, nw_ref, dres_ref,
             dg_ref, du_ref, dx_ref, dxb_ref, dw_ref):
        dyv = dy_ref[...]
        acc = jnp.zeros((tm, D_MODEL), F32)
        for q in range(N_CHIPS):
            da = lax.dot_general(dyv, wd_ref[q], nt, preferred_element_type=F32)
            dg = (da * fg_ref[q].astype(F32)).astype(BF16)
            du = (da * fu_ref[q].astype(F32)).astype(BF16)
            dg_ref[q] = dg
            du_ref[q] = du
            acc = acc + jnp.dot(dg, wg_ref[q], preferred_element_type=F32)
            acc = acc + jnp.dot(du, wu_ref[q], preferred_element_type=F32)
        dx, dw = _rms_bwd_tile(x_ref[...], nw_ref[...], acc, dres_ref[...])
        dx_ref[...] = dx
        dxb_ref[...] = dx.astype(BF16)
        _accumulate(dw_ref, dw)

    aspec = pl.BlockSpec((N_CHIPS, tm, FF_SH), lambda i: (0, i, 0))
    wspec = pl.BlockSpec((N_CHIPS, None, FF_SH, D_MODEL), lambda i: (0, 0, 0, 0), pipeline_mode=pl.Buffered(1))
    row = pl.BlockSpec((tm, D_MODEL), lambda i: (i, 0))
    vec = pl.BlockSpec((1, D_MODEL), lambda i: (0, 0))
    ashape = jax.ShapeDtypeStruct((N_CHIPS, t, FF_SH), BF16)
    return pl.pallas_call(
        body, name=name, grid=(t // tm,),
        in_specs=[row, wspec, wspec, wspec, aspec, aspec, row, vec, row],
        out_specs=[aspec, aspec, row, row, vec],
        out_shape=[ashape, ashape, jax.ShapeDtypeStruct((t, D_MODEL), F32), jax.ShapeDtypeStruct((t, D_MODEL), BF16),
                   jax.ShapeDtypeStruct((1, D_MODEL), F32)],
        compiler_params=_cp(),
    )(dy, wd, wg, wu, fg, fu, x, nw, dres)


def _attn_geometry(length, half_window):
    qb = min(LANES, length)
    kw = min(qb + 2 * half_window, length)
    return qb, kw, length // qb


def _dup_kv(src_ref, dst_ref, s, length):
    ch = min(length, 256)
    lo = lax.broadcasted_iota(jnp.int32, (ch, LANES), 1) < HEAD_DIM

    def chunk(c, carry):
        r0 = pl.multiple_of(c * ch, ch)
        for j in range(N_KV // 2):
            tile = src_ref[s, pl.ds(r0, ch), j * LANES:(j + 1) * LANES].astype(F32)
            rolled = pltpu.roll(tile, HEAD_DIM, 1)
            dst_ref[2 * j, pl.ds(r0, ch), :] = jnp.where(lo, tile, rolled).astype(BF16)
            dst_ref[2 * j + 1, pl.ds(r0, ch), :] = jnp.where(lo, rolled, tile).astype(BF16)
        return carry

    lax.fori_loop(0, length // ch, chunk, 0)


def _stack_heads(ref, s, q0, qb, g):
    lo = lax.broadcasted_iota(jnp.int32, (qb, LANES), 1) < HEAD_DIM
    parts = []
    for a in range(4):
        col = (2 * g + a // 2) * LANES
        tile = ref[s, pl.ds(q0, qb), col:col + LANES]
        keep = lo if a % 2 == 0 else jnp.logical_not(lo)
        parts.append(jnp.where(keep, tile, jnp.zeros_like(tile)))
    return jnp.concatenate(parts, axis=0)


def _unstack_pair_t(stacked_t, qb, pair):
    both = jnp.concatenate([stacked_t[:, (2 * pair) * qb:(2 * pair + 1) * qb],
                            stacked_t[:, (2 * pair + 1) * qb:(2 * pair + 2) * qb]], axis=0)
    return both.T


def _band_mask_t(q0, k0, qb, kw, half_window):
    key = lax.broadcasted_iota(jnp.int32, (kw, 4 * qb), 0)
    qry = lax.broadcasted_iota(jnp.int32, (kw, 4 * qb), 1) & (qb - 1)
    return jnp.abs((q0 + qry) - (k0 + key)) <= half_window


def _block_origin(i, qb, kw, half_window, length):
    if isinstance(i, int):
        return i * qb, min(max(i * qb - half_window, 0), length - kw)
    return (pl.multiple_of(i * qb, qb),
            pl.multiple_of(jnp.clip(i * qb - half_window, 0, length - kw), HEAD_DIM))


def _head_row(vals, qb):
    return jnp.concatenate([jnp.broadcast_to(v, (1, qb)).astype(F32) for v in vals], axis=1)


def _attn_fwd(qkv, sink, n_seq, length, half_window, seq_blk, out_dtype, name):
    qb, kw, nblk = _attn_geometry(length, half_window)
    with_sink = sink is not None
    nt = (((1,), (1,)), ((), ()))
    tn = (((0,), (0,)), ((), ()))
    qkv3 = qkv.reshape(n_seq, length, QKV_W)

    def body(*refs):
        refs = list(refs)
        sink_ref = refs.pop(0) if with_sink else None
        q_ref, k_ref, v_ref, o_ref, lse_ref = refs[:5]
        kx_ref, vx_ref = refs[-2:]
        head_row = lax.broadcasted_iota(jnp.int32, (N_HEADS, qb), 0)
        for s in range(seq_blk):
            _dup_kv(k_ref, kx_ref, s, length)
            _dup_kv(v_ref, vx_ref, s, length)

            def block(i, carry):
                q0, k0 = _block_origin(i, qb, kw, half_window, length)
                valid = _band_mask_t(q0, k0, qb, kw, half_window)
                lse_tile = jnp.zeros((N_HEADS, qb), F32)
                groups = range(N_KV)
                sts = [lax.dot_general(kx_ref[g, pl.ds(k0, kw), :], _stack_heads(q_ref, s, q0, qb, g), nt,
                                       preferred_element_type=F32) for g in groups]
                sts = [jnp.where(valid, st, NEG_INF) for st in sts]
                ms = [jnp.max(st, axis=0, keepdims=True) for st in sts]
                if with_sink:
                    sks = [_head_row([sink_ref[4 * g + a] * LOG2E for a in range(4)], qb) for g in groups]
                    ms = [jnp.maximum(m, sk) for m, sk in zip(ms, sks)]
                es = [jnp.exp2(st - m) for st, m in zip(sts, ms)]
                dens = [jnp.sum(e, axis=0, keepdims=True) for e in es]
                if with_sink:
                    dens = [den + jnp.exp2(sk - m) for den, sk, m in zip(dens, sks, ms)]
                ots = [lax.dot_general(vx_ref[g, pl.ds(k0, kw), 0:HEAD_DIM], es[g].astype(BF16), tn,
                                       preferred_element_type=F32) / dens[g] for g in groups]
                for g in groups:
                    for pair in range(2):
                        col = (2 * g + pair) * LANES
                        o_ref[s, pl.ds(q0, qb), col:col + LANES] = _unstack_pair_t(ots[g], qb, pair).astype(out_dtype)
                    lse = ms[g] * LN2 + jnp.log(dens[g])
                    for a in range(4):
                        lse_tile = jnp.where(head_row == 4 * g + a, lse[:, a * qb:(a + 1) * qb], lse_tile)
                lse_ref[s, :, pl.ds(q0, qb)] = lse_tile
                return carry

            if nblk == 1:
                block(0, 0)
            else:
                lax.fori_loop(0, nblk, block, 0)

    in_specs = [pl.BlockSpec((seq_blk, length, N_HEADS * HEAD_DIM), lambda n: (n, 0, 0)),
                pl.BlockSpec((seq_blk, length, N_KV * HEAD_DIM), lambda n: (n, 0, 4)),
                pl.BlockSpec((seq_blk, length, N_KV * HEAD_DIM), lambda n: (n, 0, 5))]
    args = [qkv3, qkv3, qkv3]
    if with_sink:
        in_specs.insert(0, pl.BlockSpec(memory_space=pltpu.SMEM))
        args.insert(0, sink)
    out_specs = [pl.BlockSpec((seq_blk, length, D_MODEL), lambda n: (n, 0, 0)),
                 pl.BlockSpec((seq_blk, N_HEADS, length), lambda n: (n, 0, 0))]
    out_shape = [jax.ShapeDtypeStruct((n_seq, length, D_MODEL), out_dtype),
                 jax.ShapeDtypeStruct((n_seq, N_HEADS, length), F32)]
    o, lse = pl.pallas_call(
        body, name=name, grid=(n_seq // seq_blk,), in_specs=in_specs, out_specs=out_specs, out_shape=out_shape,
        scratch_shapes=[pltpu.VMEM((N_KV, length, LANES), BF16), pltpu.VMEM((N_KV, length, LANES), BF16)],
        compiler_params=_cp(),
    )(*args)
    return o.reshape(n_seq * length, D_MODEL), lse


def _attn_bwd(qkv, do, adj, lse, sink, cos, sin, n_seq, length, half_window, seq_blk, dil, name):
    qb, kw, nblk = _attn_geometry(length, half_window)
    scale = 1.0 / math.sqrt(HEAD_DIM)
    with_sink = sink is not None
    nt = (((1,), (1,)), ((), ()))
    tn = (((0,), (0,)), ((), ()))
    qkv3 = qkv.reshape(n_seq, length, QKV_W)
    do3 = do.reshape(n_seq, length, D_MODEL)
    tabs = [t.reshape(dil, length, LANES) for t in (cos, sin)]
    tab_blocks = dil // seq_blk if dil >= seq_blk else 1

    def body(*refs):
        refs = list(refs)
        sink_ref = refs.pop(0) if with_sink else None
        q_ref, k_ref, v_ref, do_ref, aux_ref, lse_ref, cos_ref, sin_ref, dqkv_ref = refs[:9]
        ds_ref = refs[9] if with_sink else None
        kx_ref, vx_ref, dkx_ref, dvx_ref = refs[-4:]
        lane = lax.broadcasted_iota(jnp.int32, (1, LANES), 1)
        if with_sink:
            @pl.when(pl.program_id(0) == 0)
            def _():
                ds_ref[...] = jnp.zeros_like(ds_ref)

        for s in range(seq_blk):
            ts = s % dil
            _dup_kv(k_ref, kx_ref, s, length)
            _dup_kv(v_ref, vx_ref, s, length)
            dkx_ref[...] = jnp.zeros_like(dkx_ref)
            dvx_ref[...] = jnp.zeros_like(dvx_ref)

            def block(i, dsink):
                q0, k0 = _block_origin(i, qb, kw, half_window, length)
                valid = _band_mask_t(q0, k0, qb, kw, half_window)
                cs = cos_ref[ts, pl.ds(q0, qb), :] * scale
                sn = sin_ref[ts, pl.ds(q0, qb), :] * scale
                adj_tile = aux_ref[s, :, pl.ds(q0, qb)]
                lse_tile = lse_ref[s, :, pl.ds(q0, qb)]
                groups = range(N_KV)
                qss = [_stack_heads(q_ref, s, q0, qb, g) for g in groups]
                doss = [_stack_heads(do_ref, s, q0, qb, g) for g in groups]
                kxs = [kx_ref[g, pl.ds(k0, kw), :] for g in groups]
                sts = [lax.dot_general(kxs[g], qss[g], nt, preferred_element_type=F32) for g in groups]
                dpts = [lax.dot_general(vx_ref[g, pl.ds(k0, kw), :], doss[g], nt, preferred_element_type=F32)
                        for g in groups]
                lses = [_head_row([lse_tile[4 * g + a:4 * g + a + 1, :] * LOG2E for a in range(4)], qb) for g in groups]
                shifts = [_head_row([adj_tile[4 * g + a:4 * g + a + 1, :] for a in range(4)], qb) for g in groups]
                pts = [jnp.exp2(jnp.where(valid, sts[g], NEG_INF) - lses[g]) for g in groups]
                dsbs = [(pts[g] * (dpts[g] + shifts[g])).astype(BF16) for g in groups]
                pbs = [pt.astype(BF16) for pt in pts]
                if with_sink:
                    for g in groups:
                        sk = _head_row([sink_ref[4 * g + a] * LOG2E for a in range(4)], qb)
                        dsk = jnp.exp2(sk - lses[g]) * shifts[g]
                        for a in range(4):
                            tot = jnp.sum(dsk[:, a * qb:(a + 1) * qb], axis=1, keepdims=True)
                            dsink = dsink + jnp.where(lane == 4 * g + a, tot, 0.0)
                dqts = [lax.dot_general(kx_ref[g, pl.ds(k0, kw), 0:HEAD_DIM], dsbs[g], tn, preferred_element_type=F32)
                        for g in groups]
                for g in groups:
                    for pair in range(2):
                        col = (2 * g + pair) * LANES
                        tile = _rope_t(_unstack_pair_t(dqts[g], qb, pair), cs, sn)
                        dqkv_ref[s, pl.ds(q0, qb), col:col + LANES] = tile.astype(BF16)
                for g in groups:
                    dkx_ref[g, pl.ds(k0, kw), :] += jnp.dot(dsbs[g], qss[g], preferred_element_type=F32)
                    dvx_ref[g, pl.ds(k0, kw), :] += jnp.dot(pbs[g], doss[g], preferred_element_type=F32)
                return dsink

            if nblk == 1:
                dsink = block(0, jnp.zeros((1, LANES), F32))
            else:
                dsink = lax.fori_loop(0, nblk, block, jnp.zeros((1, LANES), F32))
            if with_sink:
                ds_ref[0:1, :] += dsink

            ch = min(length, 256)
            lo_c = lax.broadcasted_iota(jnp.int32, (ch, LANES), 1) < HEAD_DIM

            def fin(c, carry):
                r0 = pl.multiple_of(c * ch, ch)
                cs = cos_ref[ts, pl.ds(r0, ch), :]
                sn = sin_ref[ts, pl.ds(r0, ch), :]
                for j in range(N_KV // 2):
                    both = []
                    for acc_ref in (dkx_ref, dvx_ref):
                        t0 = acc_ref[2 * j, pl.ds(r0, ch), :]
                        t1 = acc_ref[2 * j + 1, pl.ds(r0, ch), :]
                        both.append(jnp.where(lo_c, t0, t1) + pltpu.roll(jnp.where(lo_c, t1, t0), HEAD_DIM, 1))
                    kcol = N_HEADS * HEAD_DIM + j * LANES
                    vcol = (N_HEADS + N_KV) * HEAD_DIM + j * LANES
                    dqkv_ref[s, pl.ds(r0, ch), kcol:kcol + LANES] = _rope_t(both[0] * LN2, cs, sn).astype(BF16)
                    dqkv_ref[s, pl.ds(r0, ch), vcol:vcol + LANES] = both[1].astype(BF16)
                return carry

            lax.fori_loop(0, length // ch, fin, 0)

    seq_map = lambda n: (n, 0, 0)
    tab_map = (lambda n: (n % tab_blocks, 0, 0)) if dil >= seq_blk else (lambda n: (0, 0, 0))
    tab_rows = min(seq_blk, dil)
    in_specs = [pl.BlockSpec((seq_blk, length, N_HEADS * HEAD_DIM), seq_map),
                pl.BlockSpec((seq_blk, length, N_KV * HEAD_DIM), lambda n: (n, 0, 4)),
                pl.BlockSpec((seq_blk, length, N_KV * HEAD_DIM), lambda n: (n, 0, 5)),
                pl.BlockSpec((seq_blk, length, D_MODEL), seq_map),
                pl.BlockSpec((seq_blk, N_HEADS, length), seq_map),
                pl.BlockSpec((seq_blk, N_HEADS, length), seq_map),
                pl.BlockSpec((tab_rows, length, LANES), tab_map),
                pl.BlockSpec((tab_rows, length, LANES), tab_map)]
    args = [qkv3, qkv3, qkv3, do3, adj, lse] + tabs
    if with_sink:
        in_specs.insert(0, pl.BlockSpec(memory_space=pltpu.SMEM))
        args.insert(0, sink)
    out_specs = [pl.BlockSpec((seq_blk, length, QKV_W), seq_map)]
    out_shape = [jax.ShapeDtypeStruct((n_seq, length, QKV_W), BF16)]
    if with_sink:
        out_specs.append(pl.BlockSpec((8, LANES), lambda n: (0, 0)))
        out_shape.append(jax.ShapeDtypeStruct((8, LANES), F32))
    outs = pl.pallas_call(
        body, name=name, grid=(n_seq // seq_blk,), in_specs=in_specs, out_specs=out_specs, out_shape=out_shape,
        scratch_shapes=[pltpu.VMEM((N_KV, length, LANES), BF16), pltpu.VMEM((N_KV, length, LANES), BF16),
                        pltpu.VMEM((N_KV, length, LANES), F32), pltpu.VMEM((N_KV, length, LANES), F32)],
        compiler_params=_cp(),
    )(*args)
    dqkv = outs[0].reshape(n_seq * length, QKV_W)
    return (dqkv, outs[1]) if with_sink else (dqkv, None)


def _head_expander():
    h = jnp.arange(LANES)[:, None]
    l = jnp.arange(D_MODEL)[None, :]
    return (l // HEAD_DIM == h).astype(BF16)


def _dot_split(a, e):
    hi = a.astype(BF16)
    lo = (a - hi.astype(F32)).astype(BF16)
    return jnp.dot(hi, e, preferred_element_type=F32) + jnp.dot(lo, e, preferred_element_type=F32)


def _dot_heads(a, e):
    return jnp.dot(a.astype(BF16), e, preferred_element_type=F32)


def _mix_weights(lses):
    m = jnp.maximum(jnp.maximum(lses[0], lses[1]), lses[2])
    es = [jnp.exp(v - m) for v in lses]
    tot = es[0] + es[1] + es[2]
    return [e / tot for e in es]


def _mix_fwd(os_, lses, name):
    t = os_[0].shape[0]
    tm = _row_tile(t, ROWS)

    def body(o0, o1, o2, l0, l1, l2, e_ref, out_ref):
        wts = _mix_weights([l0[...], l1[...], l2[...]])
        acc = jnp.zeros((tm, D_MODEL), F32)
        for w, o_ref in zip(wts, (o0, o1, o2)):
            acc = acc + _dot_split(w, e_ref[...]) * _token_rows(o_ref)
        out_ref[...] = acc.astype(BF16)

    row = pl.BlockSpec((tm, D_MODEL), lambda i: (i, 0))
    lrow = pl.BlockSpec((tm, LANES), lambda i: (i, 0))
    return pl.pallas_call(
        body, name=name, grid=(t // tm,),
        in_specs=[_token_rows_spec(o, tm) for o in os_] + [lrow] * 3 + [pl.BlockSpec((LANES, D_MODEL), lambda i: (0, 0))],
        out_specs=row, out_shape=jax.ShapeDtypeStruct((t, D_MODEL), BF16), compiler_params=_cp(),
    )(*os_, *lses, _head_expander())


def _mix_bwd(dx, w_out, os_, lses, do_shapes, name):
    t = dx.shape[0]
    tm = _row_tile(t, ROWS)
    do_structs = [jax.ShapeDtypeStruct(s, BF16) for s in do_shapes]

    def body(d_ref, w_ref, o0, o1, o2, l0, l1, l2, e_ref, et_ref, do0, do1, do2, a0, a1, a2):
        wts = _mix_weights([l0[...], l1[...], l2[...]])
        dv = lax.dot_general(d_ref[...], w_ref[...], (((1,), (1,)), ((), ())), preferred_element_type=F32)
        cs = [_dot_heads(dv * _token_rows(o_ref), et_ref[...]) for o_ref in (o0, o1, o2)]
        mean_c = wts[0] * cs[0] + wts[1] * cs[1] + wts[2] * cs[2]
        for w, c, do_ref, a_ref in zip(wts, cs, (do0, do1, do2), (a0, a1, a2)):
            do_ref[...] = (_dot_heads(w, e_ref[...]) * dv).astype(BF16).reshape(do_ref.shape)
            a_ref[...] = w * (c - mean_c) - w * c

    row = pl.BlockSpec((tm, D_MODEL), lambda i: (i, 0))
    lrow = pl.BlockSpec((tm, LANES), lambda i: (i, 0))
    e = _head_expander()
    return pl.pallas_call(
        body, name=name, grid=(t // tm,),
        in_specs=[row, pl.BlockSpec((D_MODEL, D_MODEL), lambda i: (0, 0), pipeline_mode=pl.Buffered(1))]
        + [_token_rows_spec(o, tm) for o in os_] + [lrow] * 3 + [pl.BlockSpec((LANES, D_MODEL), lambda i: (0, 0)),
                                    pl.BlockSpec((D_MODEL, LANES), lambda i: (0, 0))],
        out_specs=[_token_rows_spec(d, tm) for d in do_structs] + [lrow] * 3,
        out_shape=do_structs + [jax.ShapeDtypeStruct((t, LANES), F32)] * 3,
        compiler_params=_cp(),
    )(dx, w_out, *os_, *lses, e, e.T)


def _stats_to_tokens(stat, batch, dil):
    n_seq, _, length = stat.shape
    t = stat.transpose(0, 2, 1).reshape(n_seq * length, N_HEADS)
    return _from_residue(jnp.pad(t, ((0, 0), (0, LANES - N_HEADS))), batch, dil)


def _stats_from_tokens(stat, batch, dil, n_seq, length):
    t = _to_residue(stat[:, :N_HEADS], batch, dil)
    return t.reshape(n_seq, length, N_HEADS).transpose(0, 2, 1)


def _group_geometry(batch, seq, dil, window):
    length = seq // dil
    n_seq = batch * dil
    seq_blk = max(1, min(dil, 1024 // length))
    return n_seq, length, (window // 2) // dil, seq_blk


def _local_step(x, target, a_in, a_sink, a_out, b_in, b_out, norm_mix, norm_ffn, wg, wu, wd, final_norm,
                a_in_chip_sum=None):
    batch, seq, _ = x.shape
    t = batch * seq
    x0 = x.reshape(t, D_MODEL)
    tgt = target.reshape(t, D_MODEL)
    tabs = {d: _rope_tables(seq, d) for _, d in DILATED}
    nm = [norm_mix[i:i + 1] for i in range(2)]
    nf = [norm_ffn[i:i + 1] for i in range(2)]

    h0 = _rms_fwd(x0, nm[0], "rms_mix0")
    qkv0 = _qkv_proj(h0, a_in, *tabs[1], 0, "qkv0")
    o0, lse0 = _attn_fwd(qkv0, a_sink, batch, seq, HALF_WINDOW_A, 1, BF16, "attn0")
    x1, hf0 = _mm_res(o0, a_out, x0, nf[0], "out0")
    act0, g0, u0 = _ffn_up(hf0, wg[0], wu[0], 0, "ffn_up0")
    fold_dils = [d for _, d in DILATED if _needs_fold(d)]
    x2, h1, *h1_folded = _ffn_down(act0, wd[0], x1, 0, "ffn_down0", norm_w=nm[1],
                                   fold_shapes=[_folded_shape(batch, seq, d, D_MODEL) for d in fold_dils])
    h1_by_dil = dict(zip(fold_dils, h1_folded))

    geo = [_group_geometry(batch, seq, d, w) for w, d in DILATED]
    h1g, qkv1, o1, lse1, lse1r = [], [], [], [], []
    for gi, (_, d) in enumerate(DILATED):
        n_seq, length, hw, sb = geo[gi]
        hp = _to_residue(h1_by_dil.get(d, h1), batch, d)
        pj = _qkv_proj(hp, b_in, *tabs[d], gi, f"qkv1_{gi}")
        o, lse = _attn_fwd(pj, None, n_seq, length, hw, sb, BF16, f"attn1_{gi}")
        h1g.append(hp)
        qkv1.append(pj)
        o1.append(_from_residue(o, batch, d, fold=True))
        lse1r.append(lse)
        lse1.append(_stats_to_tokens(lse, batch, d))
    omix = _mix_fwd(o1, lse1, "mix")
    x3, hf1 = _mm_res(omix, b_out, x2, nf[1], "out1")
    act1, g1, u1 = _ffn_up(hf1, wg[1], wu[1], 0, "ffn_up1")
    dx4, dx4b, loss_cols, d_final = _ffn_down(act1, wd[1], x3, 0, "ffn_down1_loss",
                                                     head=(final_norm.reshape(1, D_MODEL), tgt))

    def ffn_bwd(dxo, dxob, x_mid, hf, g, u, act, layer):
        dg, du, dxm, dxmb, d_nf = _ffn_bwd(dxob, wd[layer], wg[layer], wu[layer], g, u, x_mid, nf[layer], dxo,
                                           f"ffn_bwd{layer}")
        (d_wd,) = _mm_tn(act, [dxob], f"grad_wd{layer}")
        (d_wgt,) = _mm_tn(dg, [hf], f"grad_wg{layer}")
        (d_wut,) = _mm_tn(du, [hf], f"grad_wu{layer}")
        return dxm, dxmb, d_nf, d_wgt, d_wut, d_wd

    dx3, dx3b, d_nf1, d_wg1, d_wu1, d_wd1 = ffn_bwd(dx4, dx4b, x3, hf1, g1, u1, act1, 1)

    (d_b_out,) = _mm_tn(omix, [dx3b], "grad_b_out")
    do_shapes = [_folded_shape(batch, seq, d, D_MODEL) if _needs_fold(d) else (t, D_MODEL) for _, d in DILATED]
    mb = _mix_bwd(dx3b, b_out, o1, lse1, do_shapes, "out1_mix_bwd")
    dh1, d_b_in = [], None
    for gi, (_, d) in enumerate(DILATED):
        n_seq, length, hw, sb = geo[gi]
        dog = _to_residue(mb[gi], batch, d)
        adj = _stats_from_tokens(mb[3 + gi], batch, d, n_seq, length)
        dpj, _ = _attn_bwd(qkv1[gi], dog, adj, lse1r[gi], None, *tabs[d], n_seq, length, hw, sb, d, f"attn1_bwd{gi}")
        (d_b_in,) = _mm_tn(h1g[gi], [dpj], f"grad_b_in{gi}", part=(gi, len(DILATED), d_b_in))
        dh1.append(_from_residue(_mm_nt(dpj, b_in, gi, BF16, f"qkv1_bwd{gi}"), batch, d, fold=True))
    dx2, dx2b, d_nm1 = _rms_bwd(x2, nm[1], dh1, dx3, "rms_mix_bwd1")

    dx1, dx1b, d_nf0, d_wg0, d_wu0, d_wd0 = ffn_bwd(dx2, dx2b, x1, hf0, g0, u0, act0, 0)

    do0, adj0 = _out_bwd(dx1b, a_out, o0, "out0_bwd")
    (d_a_out,) = _mm_tn(o0, [dx1b], "grad_a_out")
    adj0 = _stats_from_tokens(adj0, batch, 1, batch, seq)
    dqkv0, d_sink = _attn_bwd(qkv0, do0, adj0, lse0, a_sink, *tabs[1], batch, seq, HALF_WINDOW_A, 1, 1, "attn0_bwd")
    (d_a_in,) = _mm_tn(h0, [dqkv0], "grad_a_in")
    if a_in_chip_sum is None:
        gx, d_nm0 = _mm_nt_rms(dqkv0, a_in, x0, nm[0], dx1, "qkv0_bwd")
    else:
        gx, d_nm0, d_a_in = _mm_nt_rms(dqkv0, a_in, x0, nm[0], dx1, "qkv0_bwd",
                                       scatter=([a_in_chip_sum(d_a_in)], (True,)))

    grads = dict(a_in=d_a_in, a_out=d_a_out, b_in=d_b_in, b_out=d_b_out,
                 wg=(d_wg0, d_wg1), wu=(d_wu0, d_wu1), wd=(d_wd0, d_wd1))
    vecs = dict(norm_mix=(d_nm0, d_nm1), norm_ffn=(d_nf0, d_nf1), final=d_final, loss_cols=loss_cols, sink=d_sink)
    return gx.reshape(x.shape), grads, vecs


ANY = pl.BlockSpec(memory_space=pl.ANY)
HBM = pltpu.MemorySpace.HBM


def _me():
    return lax.axis_index("x"), lax.axis_index("y"), lax.axis_index("c")


def _chip_peer(x, y, j):
    px = 1 - x if j & 2 else x
    py = 1 - y if j & 1 else y
    return px, py, 2 * px + py


def _remote(src, dst, sems, k, dev):
    return pltpu.make_async_remote_copy(src_ref=src, dst_ref=dst, send_sem=sems[0].at[k], recv_sem=sems[1].at[k],
                                        device_id=dev, device_id_type=MESH)


def _col_window(ref, q, width):
    return ref.at[:, pl.ds(pl.multiple_of(q * width, LANES), width)]


def _half0(ref, h):
    n = ref.shape[0] // 2
    return ref.at[pl.ds(h * n, n)]


def _half1(ref, h):
    n = ref.shape[1] // 2
    return ref.at[:, pl.ds(h * n, n)]


def _half_rows(ref, h):
    n = ref.shape[-2] // 2
    if len(ref.shape) == 2:
        return ref.at[pl.ds(h * n, n)]
    return ref.at[:, pl.ds(h * n, n)]


def _place_shard(w, layer, q_arr, col, name):
    _, rows, cols = w.shape

    def body(q_ref, w_ref, o_ref):
        o_ref[...] = w_ref[...].astype(BF16)

    if col:
        out_spec = pl.BlockSpec((rows, cols), lambda l, q: (0, q[0]))
        out_shape = jax.ShapeDtypeStruct((rows, N_CHIPS * cols), BF16)
    else:
        out_spec = pl.BlockSpec((None, None, rows, cols), lambda l, q: (q[0], 0, 0, 0))
        out_shape = jax.ShapeDtypeStruct((N_CHIPS, 1, rows, cols), BF16)
    return pl.pallas_call(
        body, name=name,
        grid_spec=pltpu.PrefetchScalarGridSpec(
            num_scalar_prefetch=1, grid=(1,),
            in_specs=[pl.BlockSpec((None, rows, cols), lambda l, q: (layer, 0, 0))], out_specs=out_spec),
        out_shape=out_shape, compiler_params=_cp(),
    )(q_arr, w)


def _handshake(peers):
    barrier = pltpu.get_barrier_semaphore()
    for p in peers:
        pl.semaphore_signal(barrier, inc=1, device_id=p, device_id_type=MESH)
    pl.semaphore_wait(barrier, len(peers))


def _on_sequencer(name, collective_id, n_sem, n_local, body):
    @pl.kernel(mesh=plsc.ScalarSubcoreMesh(axis_name="seq", num_cores=1), name=name,
               scratch_types=(pltpu.SemaphoreType.DMA((n_sem,)), pltpu.SemaphoreType.DMA((n_sem,)),
                              pltpu.SemaphoreType.DMA((max(n_local, 1),))),
               compiler_params=pltpu.CompilerParams(collective_id=collective_id))
    def launch(send_sems, recv_sems, local_sems):
        body((send_sems, recv_sems), local_sems)

    launch()


def _gather_plan(outs, col_fam, sems, handshake):
    n_w = len(outs)
    x, y, c = _me()
    myq = 2 * x + y
    sib = (x, y, 1 - c)
    if handshake:
        _handshake([sib] + [_chip_peer(x, y, j)[:2] + (c,) for j in (1, 2, 3)])

    def slot(w, q):
        if col_fam[w]:
            return _col_window(outs[w], q, outs[w].shape[1] // N_CHIPS)
        return outs[w].at[q]

    first = []
    for w in range(n_w):
        for j in (1, 2, 3):
            px, py, _ = _chip_peer(x, y, j)
            mine = _half_rows(slot(w, myq), c)
            cp = _remote(mine, mine, sems, w * 6 + j - 1, (px, py, c))
            cp.start()
            first.append(cp)
    passed = []
    for w in range(n_w):
        for j in (1, 2, 3):
            _, _, pq = _chip_peer(x, y, j)
            land = _half_rows(slot(w, pq), c)
            _remote(land, land, sems, w * 6 + j - 1, sib).wait_recv()
            cp = _remote(land, land, sems, w * 6 + 2 + j, sib)
            cp.start()
            passed.append(cp)
    for w in range(n_w):
        for j in (1, 2, 3):
            _, _, pq = _chip_peer(x, y, j)
            land = _half_rows(slot(w, pq), 1 - c)
            _remote(land, land, sems, w * 6 + 2 + j, sib).wait_recv()
    for cp in first + passed:
        cp.wait_send()


def _gather_weights(bufs, col_fam):
    n_w = len(bufs)

    def body(*refs):
        _gather_plan(refs[n_w:2 * n_w], col_fam, refs[2 * n_w:2 * n_w + 2], False)

    return pl.pallas_call(
        body, name="gather_weights", in_specs=[ANY] * n_w, out_specs=[ANY] * n_w,
        out_shape=[jax.ShapeDtypeStruct(b.shape, b.dtype) for b in bufs],
        input_output_aliases={w: w for w in range(n_w)},
        scratch_shapes=[pltpu.SemaphoreType.DMA((6 * n_w,)), pltpu.SemaphoreType.DMA((6 * n_w,))],
    )(*bufs)


def _gather_weights_async(bufs, col_fam, name, collective_id):
    refs = [jax.new_ref(b, memory_space=HBM) for b in bufs]
    _on_sequencer(name, collective_id, 6 * len(bufs), 0,
                  lambda sems, _: _gather_plan(refs, col_fam, sems, True))
    return [r[...] for r in refs]


def _grad_half(ref, col, h):
    return _half0(ref, h) if col else _half1(ref, h)


def _swap_halves_with_sibling(grads, col_fam):
    n_w = len(grads)

    def body(*refs):
        _swap_plan(refs[:n_w], refs[n_w:2 * n_w], col_fam, refs[2 * n_w:], False)

    return pl.pallas_call(
        body, name="grad_swap_sibling", in_specs=[ANY] * n_w, out_specs=[ANY] * n_w,
        out_shape=_swap_shapes(grads, col_fam),
        scratch_shapes=[pltpu.SemaphoreType.DMA((n_w,)), pltpu.SemaphoreType.DMA((n_w,))],
    )(*grads)


def _swap_shapes(grads, col_fam):
    out = []
    for w, g in enumerate(grads):
        shp = (g.shape[0] // 2, g.shape[1]) if col_fam[w] else (g.shape[0], g.shape[1] // 2, g.shape[2])
        out.append(jax.ShapeDtypeStruct(shp, g.dtype))
    return out


def _swap_plan(ins, outs, col_fam, sems, handshake):
    x, y, c = _me()
    sib = (x, y, 1 - c)
    if handshake:
        _handshake([sib])
    cps = [_remote(_grad_half(ins[w], col_fam[w], 1 - c), outs[w], sems, w, sib) for w in range(len(ins))]
    for cp in cps:
        cp.start()
    for cp in cps:
        cp.wait_recv()
    for cp in cps:
        cp.wait_send()


def _swap_halves_async(grads, col_fam, name, collective_id):
    srcs = [jax.new_ref(g, memory_space=HBM) for g in grads]
    dsts = [jax.empty_ref(s, memory_space=HBM) for s in _swap_shapes(grads, col_fam)]
    _on_sequencer(name, collective_id, len(grads), 0, lambda sems, _: _swap_plan(srcs, dsts, col_fam, sems, True))
    return [r[...] for r in srcs], [r[...] for r in dsts]


def _half_add(mines, recvs, c_arr, col_fam, name):
    n_w = len(mines)
    mine_specs, recv_specs = [], []
    for recv, col in zip(recvs, col_fam):
        if col:
            rows, n = recv.shape
            tr = rows // N_CHIPS
            mine_specs.append(pl.BlockSpec((tr, n), lambda i, c: (N_CHIPS * c[0] + i, 0)))
            recv_specs.append(pl.BlockSpec((tr, n), lambda i, c: (i, 0)))
        else:
            _, rows, n = recv.shape
            mine_specs.append(pl.BlockSpec((None, rows, n), lambda q, c: (q, c[0], 0)))
            recv_specs.append(pl.BlockSpec((None, rows, n), lambda q, c: (q, 0, 0)))

    def body(c_ref, *refs):
        for a_ref, b_ref, o_ref in zip(refs[:n_w], refs[n_w:2 * n_w], refs[2 * n_w:]):
            o_ref[...] = (a_ref[...].astype(F32) + b_ref[...].astype(F32)).astype(BF16)

    return pl.pallas_call(
        body, name=name,
        grid_spec=pltpu.PrefetchScalarGridSpec(num_scalar_prefetch=1, grid=(N_CHIPS,),
                                               in_specs=mine_specs + recv_specs, out_specs=recv_specs),
        out_shape=[jax.ShapeDtypeStruct(r.shape, BF16) for r in recvs], compiler_params=_cp(),
    )(c_arr, *mines, *recvs)


def _scatter_shapes(sums, col_fam):
    out = []
    for w, s in enumerate(sums):
        shp = (s.shape[0], s.shape[1] // N_CHIPS) if col_fam[w] else s.shape[1:]
        out.append(jax.ShapeDtypeStruct((N_CHIPS,) + shp, s.dtype))
    return out


def _scatter_copies(ins, outs, col_fam, sems, lsem):
    n_w = len(ins)
    x, y, c = _me()
    myq = 2 * x + y

    def slab(w, q):
        if col_fam[w]:
            return _col_window(ins[w], q, ins[w].shape[1] // N_CHIPS)
        return ins[w].at[q]

    local = [pltpu.make_async_copy(slab(w, myq), outs[w].at[myq], lsem.at[w]) for w in range(n_w)]
    sends, lands = [], []
    for w in range(n_w):
        for j in (1, 2, 3):
            px, py, pq = _chip_peer(x, y, j)
            sends.append(_remote(slab(w, pq), outs[w].at[myq], sems, w * 3 + j - 1, (px, py, c)))
            land = outs[w].at[pq]
            lands.append(_remote(land, land, sems, w * 3 + j - 1, (x, y, c)))
    return local, sends, lands


def _scatter_start(ins, outs, col_fam, sems, lsem):
    local, sends, _ = _scatter_copies(ins, outs, col_fam, sems, lsem)
    for cp in local + sends:
        cp.start()


def _scatter_wait(ins, outs, col_fam, sems, lsem):
    local, sends, lands = _scatter_copies(ins, outs, col_fam, sems, lsem)
    for cp in lands:
        cp.wait_recv()
    for cp in sends:
        cp.wait_send()
    for cp in local:
        cp.wait()


def _scatter_chip_sums_async(sums, col_fam, name, collective_id):
    srcs = [jax.new_ref(s, memory_space=HBM) for s in sums]
    dsts = [jax.empty_ref(s, memory_space=HBM) for s in _scatter_shapes(sums, col_fam)]

    def plan(sems, lsem):
        x, y, c = _me()
        _handshake([_chip_peer(x, y, j)[:2] + (c,) for j in (1, 2, 3)])
        _scatter_start(srcs, dsts, col_fam, sems, lsem)
        _scatter_wait(srcs, dsts, col_fam, sems, lsem)

    _on_sequencer(name, collective_id, 3 * len(sums), len(sums), plan)
    return [r[...] for r in dsts]


def _sum_chips(parts, c_arr, prev, lead, shape, name):
    _, rows, n = parts.shape
    tr = rows // 2 if rows % 32 == 0 else rows
    nblk = rows // tr

    def body(c_ref, p_ref, *rest):
        o_ref = rest[-1]
        acc = p_ref[0].astype(F32)
        for q in range(1, N_CHIPS):
            acc = acc + p_ref[q].astype(F32)
        o_ref[...] = acc

    in_specs = [pl.BlockSpec((N_CHIPS, tr, n), lambda i, c: (0, i, 0))]
    args = [c_arr, parts]
    aliases = {}
    if prev is not None:
        in_specs.append(ANY)
        args.append(prev)
        aliases = {2: 0}
    return pl.pallas_call(
        body, name=name,
        grid_spec=pltpu.PrefetchScalarGridSpec(
            num_scalar_prefetch=1, grid=(nblk,), in_specs=in_specs,
            out_specs=pl.BlockSpec((None, tr, n), lambda i, c: (lead, c[0] * nblk + i, 0))),
        out_shape=jax.ShapeDtypeStruct(shape, F32), input_output_aliases=aliases, compiler_params=_cp(),
    )(*args)


def _join_plan(outs, place, sems, handshake):
    x, y, c = _me()
    sib = (x, y, 1 - c)
    if handshake:
        _handshake([sib])

    def half(k, h):
        o, lead = place[k]
        return _half_rows(outs[o].at[lead], h)

    cps = [_remote(half(k, c), half(k, c), sems, k, sib) for k in range(len(place))]
    for cp in cps:
        cp.start()
    for k in range(len(place)):
        land = half(k, 1 - c)
        _remote(land, land, sems, k, sib).wait_recv()
    for cp in cps:
        cp.wait_send()


def _join_halves(bufs, place, name):
    n_o = len(bufs)
    n_h = len(place)

    def body(*refs):
        _join_plan(refs[n_o:2 * n_o], place, refs[2 * n_o:2 * n_o + 2], False)

    return pl.pallas_call(
        body, name=name, in_specs=[ANY] * n_o, out_specs=[ANY] * n_o,
        out_shape=[jax.ShapeDtypeStruct(b.shape, b.dtype) for b in bufs],
        input_output_aliases={k: k for k in range(n_o)},
        scratch_shapes=[pltpu.SemaphoreType.DMA((n_h,)), pltpu.SemaphoreType.DMA((n_h,))],
    )(*bufs)


def _allreduce_rows(rows):
    n_dev = 8
    n_r = len(rows)
    assert n_r <= 8

    def body(*refs):
        r_refs = refs[:n_r]
        o_ref, slots, send_sems, recv_sems = refs[n_r:]
        x, y, c = _me()
        me = 4 * x + 2 * y + c
        slots[me] = jnp.concatenate([r[...] for r in r_refs] + [jnp.zeros((8 - n_r, D_MODEL), F32)], axis=0)

        def peer(k):
            return (1 - x if k & 4 else x, 1 - y if k & 2 else y, 1 - c if k & 1 else c)

        cps = []
        for k in range(1, n_dev):
            cp = pltpu.make_async_remote_copy(src_ref=slots.at[me], dst_ref=slots.at[me], send_sem=send_sems.at[k - 1],
                                              recv_sem=recv_sems.at[k - 1], device_id=peer(k), device_id_type=MESH)
            cp.start()
            cps.append(cp)
        for k in range(1, n_dev):
            px, py, pc = peer(k)
            land = slots.at[4 * px + 2 * py + pc]
            pltpu.make_async_remote_copy(src_ref=land, dst_ref=land, send_sem=send_sems.at[k - 1],
                                         recv_sem=recv_sems.at[k - 1], device_id=peer(k),
                                         device_id_type=MESH).wait_recv()
        for cp in cps:
            cp.wait_send()
        acc = slots[0]
        for d in range(1, n_dev):
            acc = acc + slots[d]
        o_ref[...] = acc

    vm = pl.BlockSpec(memory_space=pltpu.VMEM)
    return pl.pallas_call(
        body, name="allreduce_rows", in_specs=[vm] * n_r, out_specs=vm,
        out_shape=jax.ShapeDtypeStruct((8, D_MODEL), F32),
        scratch_shapes=[pltpu.VMEM((n_dev, 8, D_MODEL), F32), pltpu.SemaphoreType.DMA((n_dev - 1,)),
                        pltpu.SemaphoreType.DMA((n_dev - 1,))],
    )(*rows)


def _adamw(w, g, m, v, name):
    shape = w.shape
    if len(shape) == 1:
        lead, rows, cols = 1, 1, shape[0]
    else:
        rows, cols = shape[-2:]
        lead = math.prod(shape[:-2])
    args = [a.reshape(lead, rows, cols) for a in (w, g, m, v)]
    tr = rows // 2 if rows % 16 == 0 else rows

    def body(w_ref, g_ref, m_ref, v_ref, d_ref, nm_ref, nv_ref):
        gv = g_ref[...]
        nm = ADAM_B1 * m_ref[...] + (1.0 - ADAM_B1) * gv
        nv = ADAM_B2 * v_ref[...] + (1.0 - ADAM_B2) * jnp.square(gv)
        m_hat = nm / (1.0 - ADAM_B1 ** ADAM_STEP)
        v_hat = nv / (1.0 - ADAM_B2 ** ADAM_STEP)
        d_ref[...] = -ADAM_LR * (m_hat / (jnp.sqrt(v_hat) + ADAM_EPS) + ADAM_WD * w_ref[...])
        nm_ref[...] = nm
        nv_ref[...] = nv

    spec = pl.BlockSpec((None, tr, cols), lambda l, i: (l, i, 0))
    outs = pl.pallas_call(
        body, name=name, grid=(lead, rows // tr), in_specs=[spec] * 4, out_specs=[spec] * 3,
        out_shape=[jax.ShapeDtypeStruct((lead, rows, cols), F32)] * 3, compiler_params=_cp(),
    )(*args)
    return [o.reshape(shape) for o in outs]


def kernel(x, a_w_in, a_sink, a_w_out, b_w_in, b_w_out, norm_mix, norm_ffn, w_gate, w_up, w_down, final_norm, loss_target, m_a_w_in, m_a_sink, m_a_w_out, m_b_w_in, m_b_w_out, m_norm_mix, m_norm_ffn, m_w_gate, m_w_up, m_w_down, m_final_norm, v_a_w_in, v_a_sink, v_a_w_out, v_b_w_in, v_b_w_out, v_norm_mix, v_norm_ffn, v_w_gate, v_w_up, v_w_down, v_final_norm):
    weights = dict(a_w_in=a_w_in, a_sink=a_sink, a_w_out=a_w_out, b_w_in=b_w_in, b_w_out=b_w_out, norm_mix=norm_mix,
                   norm_ffn=norm_ffn, w_gate=w_gate, w_up=w_up, w_down=w_down, final_norm=final_norm)
    mom = dict(a_w_in=m_a_w_in, a_sink=m_a_sink, a_w_out=m_a_w_out, b_w_in=m_b_w_in, b_w_out=m_b_w_out,
               norm_mix=m_norm_mix, norm_ffn=m_norm_ffn, w_gate=m_w_gate, w_up=m_w_up, w_down=m_w_down,
               final_norm=m_final_norm)
    var = dict(a_w_in=v_a_w_in, a_sink=v_a_sink, a_w_out=v_a_w_out, b_w_in=v_b_w_in, b_w_out=v_b_w_out,
               norm_mix=v_norm_mix, norm_ffn=v_norm_ffn, w_gate=v_w_gate, w_up=v_w_up, w_down=v_w_down,
               final_norm=v_final_norm)
    order = ["a_w_in", "a_sink", "a_w_out", "b_w_in", "b_w_out", "norm_mix", "norm_ffn", "w_gate", "w_up", "w_down",
             "final_norm"]
    swapped = ("w_gate", "w_up")
    for n in swapped:
        weights[n], mom[n], var[n] = (a.transpose(0, 2, 1) for a in (weights[n], mom[n], var[n]))
    w_gate_t, w_up_t = weights["w_gate"], weights["w_up"]

    c_arr = lax.axis_index("c").astype(jnp.int32).reshape(1)
    q_arr = (2 * lax.axis_index("x") + lax.axis_index("y")).astype(jnp.int32).reshape(1)

    def placed(w, layer, col, nm):
        return _place_shard(w, layer, q_arr, col, f"place_{nm}")

    (a_in,) = _gather_weights_async([placed(a_w_in, 0, True, "a_in")], (True,), "gather_weights_first", 6)
    a_out, wg0, wu0, wd0 = _gather_weights_async(
        [placed(a_w_out, 0, False, "a_out"), placed(w_gate_t, 0, False, "wg0"), placed(w_up_t, 0, False, "wu0"),
         placed(w_down, 0, False, "wd0")], (False,) * 4, "gather_weights_layer0", 1)
    b_in, b_out, wg1, wu1, wd1 = _gather_weights_async(
        [placed(b_w_in, 0, True, "b_in"), placed(b_w_out, 0, False, "b_out"), placed(w_gate_t, 1, False, "wg1"),
         placed(w_up_t, 1, False, "wu1"), placed(w_down, 1, False, "wd1")], (True,) + (False,) * 4,
        "gather_weights_layer1", 7)
    a_out = a_out.reshape(D_MODEL, D_MODEL)
    b_out = b_out.reshape(D_MODEL, D_MODEL)
    wg, wu, wd = (wg0, wg1), (wu0, wu1), (wd0, wd1)

    def a_in_chip_sum(part):
        theirs = _swap_halves_with_sibling([part], (True,))
        return _half_add([part], theirs, c_arr, (True,), "chip_sum_a_in")[0]

    gx, grads, vecs = _local_step(x, loss_target, a_in, a_sink[0], a_out, b_in, b_out, norm_mix, norm_ffn, wg, wu, wd,
                                  final_norm, a_in_chip_sum)

    rows_out = D_MODEL // N_CHIPS
    partials = [grads["a_in"], grads["b_in"],
                grads["a_out"].reshape(N_CHIPS, rows_out, D_MODEL), grads["b_out"].reshape(N_CHIPS, rows_out, D_MODEL),
                grads["wg"][0], grads["wg"][1], grads["wu"][0], grads["wu"][1], grads["wd"][0], grads["wd"][1]]
    col_fam = (True, True) + (False,) * 8
    names = ("a_in", "b_in", "a_out", "b_out", "wg0", "wg1", "wu0", "wu1", "wd0", "wd1")
    contrib = [None] * len(partials)

    def reduce_group(idx, tag, ids):
        parts = [partials[k] for k in idx]
        cols = tuple(col_fam[k] for k in idx)
        parts, theirs = _swap_halves_async(parts, cols, f"grad_swap_{tag}", ids[0])
        sums = _half_add(parts, theirs, c_arr, cols, f"chip_sum_{tag}")
        for k, o in zip(idx, _scatter_chip_sums_async(sums, cols, f"grad_scatter_{tag}", ids[1])):
            contrib[k] = o

    reduce_group([1, 3, 5, 7, 9], "layer1", (2, 3))
    reduce_group([2, 4, 6, 8], "ffn0", (4, 5))
    contrib[0] = grads["a_in"]
    shapes = [a_w_in.shape, b_w_in.shape, a_w_out.shape, b_w_out.shape, w_down.shape, w_down.shape, w_down.shape]
    place = [(0, 0), (1, 0), (2, 0), (3, 0), (4, 0), (4, 1), (5, 0), (5, 1), (6, 0), (6, 1)]
    bufs = [None] * len(shapes)
    for p, nm, (o, lead) in zip(contrib, names, place):
        bufs[o] = _sum_chips(p, c_arr, bufs[o], lead, shapes[o], f"sum_chips_{nm}")
    g_a_in, g_b_in, g_a_out, g_b_out, g_wg, g_wu, g_wd = _join_halves(bufs, place, "grad_join_sibling")

    sink_row = jnp.pad(vecs["sink"][0:1], ((0, 0), (0, D_MODEL - LANES)))
    tot = _allreduce_rows([vecs["norm_mix"][0], vecs["norm_mix"][1], vecs["norm_ffn"][0], vecs["norm_ffn"][1],
                           vecs["final"], vecs["loss_cols"], sink_row])
    loss = (0.5 / D_MODEL) * jnp.sum(tot[5])
    gw = dict(a_w_in=g_a_in, a_sink=tot[6:7, :N_HEADS], a_w_out=g_a_out, b_w_in=g_b_in, b_w_out=g_b_out,
              norm_mix=tot[0:2], norm_ffn=tot[2:4], w_gate=g_wg, w_up=g_wu, w_down=g_wd, final_norm=tot[4])

    delta, new_m, new_v = {}, {}, {}
    for n in order:
        delta[n], new_m[n], new_v[n] = _adamw(weights[n], gw[n], mom[n], var[n], f"adamw_{n}")
    for n in swapped:
        gw[n], delta[n], new_m[n], new_v[n] = (a.transpose(0, 2, 1) for a in (gw[n], delta[n], new_m[n], new_v[n]))
    return (loss, gx, *[gw[n] for n in order], *[delta[n] for n in order], *[new_m[n] for n in order],
            *[new_v[n] for n in order])
```

```python
import math

import jax
import jax.numpy as jnp
import numpy as np
from jax import lax
from jax.experimental import pallas as pl
from jax.experimental.pallas import tpu as pltpu
from jax.experimental.pallas import tpu_sc as plsc

F32 = jnp.float32
BF16 = jnp.bfloat16

D_MODEL = 1024
HEAD_DIM = 64
N_HEADS = 16
N_KV = 4
QKV_W = 1536
D_FF = 2816
N_CHIPS = 4
FF_SH = D_FF // N_CHIPS
HALF_WINDOW_A = 128
DILATED = ((128, 1), (512, 4), (2048, 16))
ROPE_THETA = 10000.0
RMS_EPS = 1e-6
NEG_INF = -1e30
LANES = 128
ADAM_LR, ADAM_B1, ADAM_B2, ADAM_EPS, ADAM_WD, ADAM_STEP = 0.001, 0.9, 0.999, 1e-08, 0.01, 10
VMEM_LIMIT = 56 * 1024 * 1024
ROWS = 512
MATMUL_ROWS = 1024
FFN_BWD_ROWS = 256
LOG2E = math.log2(math.e)
LN2 = math.log(2.0)
Q_SCALE = LOG2E / math.sqrt(HEAD_DIM)
GRAD_TOKENS = 2048
MESH = pl.DeviceIdType.MESH


def _cp(**kw):
    return pltpu.CompilerParams(vmem_limit_bytes=VMEM_LIMIT, **kw)


def _row_tile(t, cap):
    tm = min(cap, t)
    assert t % tm == 0
    return tm


def _rope_tables(seq, dil):
    inv = 1.0 / (ROPE_THETA ** (np.arange(0, HEAD_DIM, 2, dtype=np.float32) / HEAD_DIM))
    ang = np.arange(seq, dtype=np.float32)[:, None] * inv.astype(np.float32)[None, :]
    cos, sin = np.cos(ang), np.sin(ang)
    cos = np.tile(cos, (1, 4))
    sin = np.concatenate([-sin, sin, -sin, sin], axis=1)

    def perm(t):
        return jnp.asarray(t.reshape(seq // dil, dil, LANES).transpose(1, 0, 2).reshape(seq, LANES), dtype=F32)

    return perm(cos), perm(sin)


def _swap_halves(t):
    lane = lax.broadcasted_iota(jnp.int32, t.shape, 1)
    return jnp.where((lane % HEAD_DIM) < HEAD_DIM // 2, pltpu.roll(t, LANES - 32, 1), pltpu.roll(t, 32, 1))


def _rope(t, cos, sin):
    return t * cos + _swap_halves(t) * sin


def _rope_t(t, cos, sin):
    return t * cos - _swap_halves(t) * sin


def _to_residue(t, batch, dil):
    if dil == 1:
        return t
    if t.ndim == 2:
        t = t.reshape(batch, t.shape[0] // batch // dil, dil, t.shape[1])
    return t.transpose(0, 2, 1, 3).reshape(-1, t.shape[-1])


def _needs_fold(dil):
    return dil > 1 and dil % 16 != 0


def _folded_shape(batch, seq, dil, cols):
    return (batch, seq // dil, dil, cols)


def _from_residue(t, batch, dil, fold=False):
    if dil == 1:
        return t
    s = t.shape[0] // batch
    nat = t.reshape(batch, dil, s // dil, t.shape[1]).transpose(0, 2, 1, 3)
    return nat if fold else nat.reshape(t.shape)


def _token_rows_spec(a, tm):
    if a.ndim == 2:
        return pl.BlockSpec((tm, a.shape[1]), lambda i: (i, 0))
    _, length, dil, c = a.shape
    per_seq = length * dil // tm
    return pl.BlockSpec((None, tm // dil, dil, c), lambda i: (i // per_seq, i % per_seq, 0, 0))


def _token_rows(ref):
    v = ref[...]
    return v if v.ndim == 2 else v.reshape(v.shape[0] * v.shape[1], v.shape[2])


def _rms_fwd(x, w, name):
    t = x.shape[0]
    tm = _row_tile(t, ROWS)

    def body(x_ref, w_ref, o_ref):
        o_ref[...] = _rms_tile(x_ref[...], w_ref[...]).astype(BF16)

    return pl.pallas_call(
        body, name=name, grid=(t // tm,),
        in_specs=[pl.BlockSpec((tm, D_MODEL), lambda i: (i, 0)), pl.BlockSpec((1, D_MODEL), lambda i: (0, 0))],
        out_specs=pl.BlockSpec((tm, D_MODEL), lambda i: (i, 0)),
        out_shape=jax.ShapeDtypeStruct((t, D_MODEL), BF16), compiler_params=_cp(),
    )(x, w)


def _rms_bwd_tile(xv, wv, dy, dres):
    r = lax.rsqrt(jnp.mean(xv * xv, axis=-1, keepdims=True) + RMS_EPS)
    xh = xv * r
    dxh = dy * wv
    dx = dres + r * (dxh - xh * jnp.mean(dxh * xh, axis=-1, keepdims=True))
    return dx, jnp.sum(dy * xh, axis=0, keepdims=True)


def _accumulate(ref, part):
    @pl.when(pl.program_id(0) == 0)
    def _():
        ref[...] = jnp.zeros_like(ref)

    ref[...] += part


def _rms_bwd(x, w, dhs, dres, name):
    t = x.shape[0]
    tm = _row_tile(t, ROWS)
    n = len(dhs)

    def body(*refs):
        x_ref, w_ref = refs[0], refs[1]
        dh_refs = refs[2:2 + n]
        dres_ref = refs[2 + n]
        dx_ref, dxb_ref, dw_ref = refs[3 + n:]
        dy = _token_rows(dh_refs[0]).astype(F32)
        for k in range(1, n):
            dy = dy + _token_rows(dh_refs[k]).astype(F32)
        dx, dw = _rms_bwd_tile(x_ref[...], w_ref[...], dy, dres_ref[...])
        dx_ref[...] = dx
        dxb_ref[...] = dx.astype(BF16)
        _accumulate(dw_ref, dw)

    row = pl.BlockSpec((tm, D_MODEL), lambda i: (i, 0))
    vec = pl.BlockSpec((1, D_MODEL), lambda i: (0, 0))
    return pl.pallas_call(
        body, name=name, grid=(t // tm,),
        in_specs=[row, vec] + [_token_rows_spec(dh, tm) for dh in dhs] + [row],
        out_specs=[row, row, vec],
        out_shape=[jax.ShapeDtypeStruct((t, D_MODEL), F32), jax.ShapeDtypeStruct((t, D_MODEL), BF16),
                   jax.ShapeDtypeStruct((1, D_MODEL), F32)],
        compiler_params=_cp(),
    )(x, w, *dhs, dres)


def _final_tile(xv, wv, tv):
    r = lax.rsqrt(jnp.mean(xv * xv, axis=-1, keepdims=True) + RMS_EPS)
    xh = xv * r
    err = xh * wv - tv
    dy = err * (1.0 / D_MODEL)
    dxh = dy * wv
    dx = r * (dxh - xh * jnp.mean(dxh * xh, axis=-1, keepdims=True))
    return dx, jnp.sum(err * err, axis=0, keepdims=True), jnp.sum(dy * xh, axis=0, keepdims=True)


def _qkv_proj(h, w, cos, sin, group, name):
    t = h.shape[0]
    seq = cos.shape[0]
    tm = _row_tile(seq, MATMUL_ROWS)
    n_q = N_HEADS * HEAD_DIM // LANES
    n_rope = (N_HEADS + N_KV) * HEAD_DIM // LANES
    scale = Q_SCALE

    def body(h_ref, w_ref, cos_ref, sin_ref, o_ref):
        acc = jnp.dot(h_ref[...], w_ref[...], preferred_element_type=F32)
        cs, sn = cos_ref[...], sin_ref[...]
        csq, snq = cs * scale, sn * scale
        for c in range(QKV_W // LANES):
            blk = acc[:, c * LANES:(c + 1) * LANES]
            if c < n_q:
                blk = _rope(blk, csq, snq)
            elif c < n_rope:
                blk = _rope(blk, cs, sn)
            o_ref[:, c * LANES:(c + 1) * LANES] = blk.astype(BF16)

    tab = pl.BlockSpec((tm, LANES), lambda i: (i % (seq // tm), 0))
    return pl.pallas_call(
        body, name=name, grid=(t // tm,),
        in_specs=[pl.BlockSpec((tm, D_MODEL), lambda i: (i, 0)),
                  pl.BlockSpec((D_MODEL, QKV_W), lambda i: (0, group)), tab, tab],
        out_specs=pl.BlockSpec((tm, QKV_W), lambda i: (i, 0)),
        out_shape=jax.ShapeDtypeStruct((t, QKV_W), BF16), compiler_params=_cp(),
    )(h, w, cos, sin)


def _rms_tile(xv, wv):
    return (xv * lax.rsqrt(jnp.mean(xv * xv, axis=-1, keepdims=True) + RMS_EPS)) * wv


def _mm_res(a, w, res, nw, name):
    t, k = a.shape
    tm = _row_tile(t, ROWS)

    def body(a_ref, w_ref, r_ref, nw_ref, o_ref, h_ref):
        xv = r_ref[...] + jnp.dot(a_ref[...], w_ref[...], preferred_element_type=F32)
        o_ref[...] = xv
        h_ref[...] = _rms_tile(xv, nw_ref[...]).astype(BF16)

    row = pl.BlockSpec((tm, D_MODEL), lambda i: (i, 0))
    return pl.pallas_call(
        body, name=name, grid=(t // tm,),
        in_specs=[pl.BlockSpec((tm, k), lambda i: (i, 0)),
                  pl.BlockSpec((k, D_MODEL), lambda i: (0, 0), pipeline_mode=pl.Buffered(1)), row,
                  pl.BlockSpec((1, D_MODEL), lambda i: (0, 0))],
        out_specs=[row, row],
        out_shape=[jax.ShapeDtypeStruct((t, D_MODEL), F32), jax.ShapeDtypeStruct((t, D_MODEL), BF16)],
        compiler_params=_cp(),
    )(a, w, res, nw)


def _mm_nt(dy, w, group, out_dtype, name):
    t, n = dy.shape
    k = w.shape[0]
    tm = _row_tile(t, MATMUL_ROWS)

    def body(dy_ref, w_ref, o_ref):
        o_ref[...] = lax.dot_general(dy_ref[...], w_ref[...], (((1,), (1,)), ((), ())),
                                     preferred_element_type=F32).astype(out_dtype)

    return pl.pallas_call(
        body, name=name, grid=(t // tm,),
        in_specs=[pl.BlockSpec((tm, n), lambda i: (i, 0)), pl.BlockSpec((k, n), lambda i: (0, group))],
        out_specs=pl.BlockSpec((tm, k), lambda i: (i, 0)),
        out_shape=jax.ShapeDtypeStruct((t, k), out_dtype), compiler_params=_cp(),
    )(dy, w)


def _mm_nt_rms(dy, w, x, nw, dres, name):
    t, n = dy.shape
    tm = _row_tile(t, ROWS)

    def body(dy_ref, w_ref, x_ref, nw_ref, dres_ref, dx_ref, dw_ref):
        dh = lax.dot_general(dy_ref[...], w_ref[...], (((1,), (1,)), ((), ())), preferred_element_type=F32)
        dx, dw = _rms_bwd_tile(x_ref[...], nw_ref[...], dh, dres_ref[...])
        dx_ref[...] = dx
        _accumulate(dw_ref, dw)

    row = pl.BlockSpec((tm, D_MODEL), lambda i: (i, 0))
    vec = pl.BlockSpec((1, D_MODEL), lambda i: (0, 0))
    return pl.pallas_call(
        body, name=name, grid=(t // tm,),
        in_specs=[pl.BlockSpec((tm, n), lambda i: (i, 0)),
                  pl.BlockSpec((D_MODEL, n), lambda i: (0, 0), pipeline_mode=pl.Buffered(1)), row, vec, row],
        out_specs=[row, vec],
        out_shape=[jax.ShapeDtypeStruct((t, D_MODEL), F32), jax.ShapeDtypeStruct((1, D_MODEL), F32)],
        compiler_params=_cp(),
    )(dy, w, x, nw, dres)


def _out_bwd(dx, w, o, name):
    t = dx.shape[0]
    tm = _row_tile(t, ROWS)

    def body(dx_ref, w_ref, o_ref, et_ref, do_ref, adj_ref):
        do = lax.dot_general(dx_ref[...], w_ref[...], (((1,), (1,)), ((), ())), preferred_element_type=F32)
        do_ref[...] = do.astype(BF16)
        adj_ref[...] = -_dot_heads(do * o_ref[...].astype(F32), et_ref[...])

    row = pl.BlockSpec((tm, D_MODEL), lambda i: (i, 0))
    return pl.pallas_call(
        body, name=name, grid=(t // tm,),
        in_specs=[row, pl.BlockSpec((D_MODEL, D_MODEL), lambda i: (0, 0)), row,
                  pl.BlockSpec((D_MODEL, LANES), lambda i: (0, 0))],
        out_specs=[row, pl.BlockSpec((tm, LANES), lambda i: (i, 0))],
        out_shape=[jax.ShapeDtypeStruct((t, D_MODEL), BF16), jax.ShapeDtypeStruct((t, LANES), F32)],
        compiler_params=_cp(),
    )(dx, w, o, _head_expander().T)


def _mm_tn(a, bs, name, part=None):
    aq = a.ndim == 3
    bq = bs[0].ndim == 3
    t, ka = a.shape[-2:]
    n = bs[0].shape[-1]
    nq = N_CHIPS if (aq or bq) else 1
    tt = _row_tile(t, GRAD_TOKENS)
    tn = n if n <= 1024 else 768
    assert n % tn == 0
    nb = len(bs)
    steps = t // tt
    carried = part is not None and part[2] is not None

    def body(*refs):
        a_ref = refs[0]
        b_refs = refs[1:1 + nb]
        o_refs = refs[1 + nb + carried:1 + 2 * nb + carried]
        acc_refs = refs[1 + 2 * nb + carried:]
        s = pl.program_id(2)
        av = a_ref[...]
        for b_ref, o_ref, acc_ref in zip(b_refs, o_refs, acc_refs):
            @pl.when(s == 0)
            def _():
                acc_ref[...] = jnp.zeros_like(acc_ref)

            acc_ref[...] += lax.dot_general(av, b_ref[...], (((0,), (0,)), ((), ())), preferred_element_type=F32)

            @pl.when(s == steps - 1)
            def _():
                o_ref[...] = acc_ref[...].astype(BF16)

    a_spec = (pl.BlockSpec((None, tt, ka), lambda q, j, s: (q, s, 0)) if aq
              else pl.BlockSpec((tt, ka), lambda q, j, s: (s, 0)))
    b_spec = (pl.BlockSpec((None, tt, tn), lambda q, j, s: (q, s, j)) if bq
              else pl.BlockSpec((tt, tn), lambda q, j, s: (s, j)))
    extra_specs, extra_args, aliases = [], [], {}
    if nq > 1:
        o_spec = pl.BlockSpec((None, ka, tn), lambda q, j, s: (q, 0, j))
        o_shape = jax.ShapeDtypeStruct((nq, ka, n), BF16)
    elif part is not None:
        assert nb == 1
        k, n_parts, buf = part
        o_spec = pl.BlockSpec((ka, tn), lambda q, j, s: (0, k * (n // tn) + j))
        o_shape = jax.ShapeDtypeStruct((ka, n_parts * n), BF16)
        if buf is not None:
            extra_specs, extra_args, aliases = [ANY], [buf], {1 + nb: 0}
    else:
        o_spec = pl.BlockSpec((ka, tn), lambda q, j, s: (0, j))
        o_shape = jax.ShapeDtypeStruct((ka, n), BF16)
    outs = pl.pallas_call(
        body, name=name, grid=(nq, n // tn, steps),
        in_specs=[a_spec] + [b_spec] * nb + extra_specs, out_specs=[o_spec] * nb, out_shape=[o_shape] * nb,
        scratch_shapes=[pltpu.VMEM((ka, tn), F32)] * nb, input_output_aliases=aliases, compiler_params=_cp(),
    )(a, *bs, *extra_args)
    return outs


def _sigmoid(x):
    return 1.0 / (1.0 + jnp.exp(-x))


def _ffn_up(h, wg, wu, layer, name):
    t = h.shape[0]
    tm = _row_tile(t, MATMUL_ROWS)
    nt = (((1,), (1,)), ((), ()))

    def body(h_ref, wg_ref, wu_ref, a_ref, dg_ref, du_ref):
        hv = h_ref[...]
        g = lax.dot_general(hv, wg_ref[...], nt, preferred_element_type=F32)
        u = lax.dot_general(hv, wu_ref[...], nt, preferred_element_type=F32)
        sg = _sigmoid(g)
        silu = g * sg
        a_ref[...] = (silu * u).astype(BF16)
        dg_ref[...] = (sg * (1.0 + g * (1.0 - sg)) * u).astype(BF16)
        du_ref[...] = silu.astype(BF16)

    wspec = pl.BlockSpec((None, None, FF_SH, D_MODEL), lambda q, i: (q, layer, 0, 0))
    ospec = pl.BlockSpec((None, tm, FF_SH), lambda q, i: (q, i, 0))
    oshape = jax.ShapeDtypeStruct((N_CHIPS, t, FF_SH), BF16)
    return pl.pallas_call(
        body, name=name, grid=(N_CHIPS, t // tm),
        in_specs=[pl.BlockSpec((tm, D_MODEL), lambda q, i: (i, 0)), wspec, wspec],
        out_specs=[ospec] * 3, out_shape=[oshape] * 3, compiler_params=_cp(),
    )(h, wg, wu)


def _ffn_down(a, wd, res, layer, name, norm_w=None, fold_shapes=(), head=None):
    t = a.shape[1]
    tm = _row_tile(t, ROWS)
    resident = pl.BlockSpec((N_CHIPS, None, FF_SH, D_MODEL), lambda i: (0, layer, 0, 0), pipeline_mode=pl.Buffered(1))
    row = pl.BlockSpec((tm, D_MODEL), lambda i: (i, 0))
    vec = pl.BlockSpec((1, D_MODEL), lambda i: (0, 0))

    def hidden(a_ref, w_ref, r_ref):
        acc = r_ref[...]
        for q in range(N_CHIPS):
            acc = acc + jnp.dot(a_ref[q], w_ref[q], preferred_element_type=F32)
        return acc

    if head is None:
        folds = [jax.ShapeDtypeStruct(s, BF16) for s in fold_shapes]

        def body(a_ref, w_ref, r_ref, nw_ref, o_ref, h_ref, *hf_refs):
            xv = hidden(a_ref, w_ref, r_ref)
            o_ref[...] = xv
            hb = _rms_tile(xv, nw_ref[...]).astype(BF16)
            h_ref[...] = hb
            for hf_ref in hf_refs:
                hf_ref[...] = hb.reshape(hf_ref.shape)

        return pl.pallas_call(
            body, name=name, grid=(t // tm,),
            in_specs=[pl.BlockSpec((N_CHIPS, tm, FF_SH), lambda i: (0, i, 0)), resident, row, vec],
            out_specs=[row, row] + [_token_rows_spec(f, tm) for f in folds],
            out_shape=[jax.ShapeDtypeStruct((t, D_MODEL), F32), jax.ShapeDtypeStruct((t, D_MODEL), BF16)] + folds,
            compiler_params=_cp(),
        )(a, wd, res, norm_w)

    def body(a_ref, w_ref, r_ref, nw_ref, t_ref, dx_ref, dxb_ref, l_ref, dw_ref):
        dx, sq, dw = _final_tile(hidden(a_ref, w_ref, r_ref), nw_ref[...], t_ref[...])
        dx_ref[...] = dx
        dxb_ref[...] = dx.astype(BF16)
        _accumulate(l_ref, sq)
        _accumulate(dw_ref, dw)

    return pl.pallas_call(
        body, name=name, grid=(t // tm,),
        in_specs=[pl.BlockSpec((N_CHIPS, tm, FF_SH), lambda i: (0, i, 0)), resident, row, vec, row],
        out_specs=[row, row, vec, vec],
        out_shape=[jax.ShapeDtypeStruct((t, D_MODEL), F32), jax.ShapeDtypeStruct((t, D_MODEL), BF16),
                   jax.ShapeDtypeStruct((1, D_MODEL), F32), jax.ShapeDtypeStruct((1, D_MODEL), F32)],
        compiler_params=_cp(),
    )(a, wd, res, *head)


def _ffn_bwd(dy, wd, wg, wu, fg, fu, x, nw, dres, name):
    t = dy.shape[0]
    tm = _row_tile(t, FFN_BWD_ROWS)
    nt = (((1,), (1,)), ((), ()))

    def body(dy_ref, wd_ref, wg_ref, wu_ref, fg_ref, fu_ref, x_ref, nw_ref, dres_ref,
             dg_ref, du_ref, dx_ref, dxb_ref, dw_ref):
        dyv = dy_ref[...]
        acc = jnp.zeros((tm, D_MODEL), F32)
        for q in range(N_CHIPS):
            da = lax.dot_general(dyv, wd_ref[q], nt, preferred_element_type=F32)
            dg = (da * fg_ref[q].astype(F32)).astype(BF16)
            du = (da * fu_ref[q].astype(F32)).astype(BF16)
            dg_ref[q] = dg
            du_ref[q] = du
            acc = acc + jnp.dot(dg, wg_ref[q], preferred_element_type=F32)
            acc = acc + jnp.dot(du, wu_ref[q], preferred_element_type=F32)
        dx, dw = _rms_bwd_tile(x_ref[...], nw_ref[...], acc, dres_ref[...])
        dx_ref[...] = dx
        dxb_ref[...] = dx.astype(BF16)
        _accumulate(dw_ref, dw)

    aspec = pl.BlockSpec((N_CHIPS, tm, FF_SH), lambda i: (0, i, 0))
    wspec = pl.BlockSpec((N_CHIPS, None, FF_SH, D_MODEL), lambda i: (0, 0, 0, 0), pipeline_mode=pl.Buffered(1))
    row = pl.BlockSpec((tm, D_MODEL), lambda i: (i, 0))
    vec = pl.BlockSpec((1, D_MODEL), lambda i: (0, 0))
    ashape = jax.ShapeDtypeStruct((N_CHIPS, t, FF_SH), BF16)
    return pl.pallas_call(
        body, name=name, grid=(t // tm,),
        in_specs=[row, wspec, wspec, wspec, aspec, aspec, row, vec, row],
        out_specs=[aspec, aspec, row, row, vec],
        out_shape=[ashape, ashape, jax.ShapeDtypeStruct((t, D_MODEL), F32), jax.ShapeDtypeStruct((t, D_MODEL), BF16),
                   jax.ShapeDtypeStruct((1, D_MODEL), F32)],
        compiler_params=_cp(),
    )(dy, wd, wg, wu, fg, fu, x, nw, dres)


def _attn_geometry(length, half_window):
    qb = min(LANES, length)
    kw = min(qb + 2 * half_window, length)
    return qb, kw, length // qb


def _dup_kv(src_ref, dst_ref, s, length):
    ch = min(length, 256)
    lo = lax.broadcasted_iota(jnp.int32, (ch, LANES), 1) < HEAD_DIM

    def chunk(c, carry):
        r0 = pl.multiple_of(c * ch, ch)
        for j in range(N_KV // 2):
            tile = src_ref[s, pl.ds(r0, ch), j * LANES:(j + 1) * LANES].astype(F32)
            rolled = pltpu.roll(tile, HEAD_DIM, 1)
            dst_ref[2 * j, pl.ds(r0, ch), :] = jnp.where(lo, tile, rolled).astype(BF16)
            dst_ref[2 * j + 1, pl.ds(r0, ch), :] = jnp.where(lo, rolled, tile).astype(BF16)
        return carry

    lax.fori_loop(0, length // ch, chunk, 0)


def _stack_heads(ref, s, q0, qb, g):
    lo = lax.broadcasted_iota(jnp.int32, (qb, LANES), 1) < HEAD_DIM
    parts = []
    for a in range(4):
        col = (2 * g + a // 2) * LANES
        tile = ref[s, pl.ds(q0, qb), col:col + LANES]
        keep = lo if a % 2 == 0 else jnp.logical_not(lo)
        parts.append(jnp.where(keep, tile, jnp.zeros_like(tile)))
    return jnp.concatenate(parts, axis=0)


def _unstack_pair_t(stacked_t, qb, pair):
    both = jnp.concatenate([stacked_t[:, (2 * pair) * qb:(2 * pair + 1) * qb],
                            stacked_t[:, (2 * pair + 1) * qb:(2 * pair + 2) * qb]], axis=0)
    return both.T


def _band_mask_t(q0, k0, qb, kw, half_window):
    key = lax.broadcasted_iota(jnp.int32, (kw, 4 * qb), 0)
    qry = lax.broadcasted_iota(jnp.int32, (kw, 4 * qb), 1) & (qb - 1)
    return jnp.abs((q0 + qry) - (k0 + key)) <= half_window


def _block_origin(i, qb, kw, half_window, length):
    if isinstance(i, int):
        return i * qb, min(max(i * qb - half_window, 0), length - kw)
    return (pl.multiple_of(i * qb, qb),
            pl.multiple_of(jnp.clip(i * qb - half_window, 0, length - kw), HEAD_DIM))


def _head_row(vals, qb):
    return jnp.concatenate([jnp.broadcast_to(v, (1, qb)).astype(F32) for v in vals], axis=1)


def _attn_fwd(qkv, sink, n_seq, length, half_window, seq_blk, out_dtype, name):
    qb, kw, nblk = _attn_geometry(length, half_window)
    with_sink = sink is not None
    nt = (((1,), (1,)), ((), ()))
    tn = (((0,), (0,)), ((), ()))
    qkv3 = qkv.reshape(n_seq, length, QKV_W)

    def body(*refs):
        refs = list(refs)
        sink_ref = refs.pop(0) if with_sink else None
        q_ref, k_ref, v_ref, o_ref, lse_ref = refs[:5]
        kx_ref, vx_ref = refs[-2:]
        head_row = lax.broadcasted_iota(jnp.int32, (N_HEADS, qb), 0)
        for s in range(seq_blk):
            _dup_kv(k_ref, kx_ref, s, length)
            _dup_kv(v_ref, vx_ref, s, length)

            def block(i, carry):
                q0, k0 = _block_origin(i, qb, kw, half_window, length)
                valid = _band_mask_t(q0, k0, qb, kw, half_window)
                lse_tile = jnp.zeros((N_HEADS, qb), F32)
                groups = range(N_KV)
                sts = [lax.dot_general(kx_ref[g, pl.ds(k0, kw), :], _stack_heads(q_ref, s, q0, qb, g), nt,
                                       preferred_element_type=F32) for g in groups]
                sts = [jnp.where(valid, st, NEG_INF) for st in sts]
                ms = [jnp.max(st, axis=0, keepdims=True) for st in sts]
                if with_sink:
                    sks = [_head_row([sink_ref[4 * g + a] * LOG2E for a in range(4)], qb) for g in groups]
                    ms = [jnp.maximum(m, sk) for m, sk in zip(ms, sks)]
                es = [jnp.exp2(st - m) for st, m in zip(sts, ms)]
                dens = [jnp.sum(e, axis=0, keepdims=True) for e in es]
                if with_sink:
                    dens = [den + jnp.exp2(sk - m) for den, sk, m in zip(dens, sks, ms)]
                ots = [lax.dot_general(vx_ref[g, pl.ds(k0, kw), 0:HEAD_DIM], es[g].astype(BF16), tn,
                                       preferred_element_type=F32) / dens[g] for g in groups]
                for g in groups:
                    for pair in range(2):
                        col = (2 * g + pair) * LANES
                        o_ref[s, pl.ds(q0, qb), col:col + LANES] = _unstack_pair_t(ots[g], qb, pair).astype(out_dtype)
                    lse = ms[g] * LN2 + jnp.log(dens[g])
                    for a in range(4):
                        lse_tile = jnp.where(head_row == 4 * g + a, lse[:, a * qb:(a + 1) * qb], lse_tile)
                lse_ref[s, :, pl.ds(q0, qb)] = lse_tile
                return carry

            if nblk == 1:
                block(0, 0)
            else:
                lax.fori_loop(0, nblk, block, 0)

    in_specs = [pl.BlockSpec((seq_blk, length, N_HEADS * HEAD_DIM), lambda n: (n, 0, 0)),
                pl.BlockSpec((seq_blk, length, N_KV * HEAD_DIM), lambda n: (n, 0, 4)),
                pl.BlockSpec((seq_blk, length, N_KV * HEAD_DIM), lambda n: (n, 0, 5))]
    args = [qkv3, qkv3, qkv3]
    if with_sink:
        in_specs.insert(0, pl.BlockSpec(memory_space=pltpu.SMEM))
        args.insert(0, sink)
    out_specs = [pl.BlockSpec((seq_blk, length, D_MODEL), lambda n: (n, 0, 0)),
                 pl.BlockSpec((seq_blk, N_HEADS, length), lambda n: (n, 0, 0))]
    out_shape = [jax.ShapeDtypeStruct((n_seq, length, D_MODEL), out_dtype),
                 jax.ShapeDtypeStruct((n_seq, N_HEADS, length), F32)]
    o, lse = pl.pallas_call(
        body, name=name, grid=(n_seq // seq_blk,), in_specs=in_specs, out_specs=out_specs, out_shape=out_shape,
        scratch_shapes=[pltpu.VMEM((N_KV, length, LANES), BF16), pltpu.VMEM((N_KV, length, LANES), BF16)],
        compiler_params=_cp(),
    )(*args)
    return o.reshape(n_seq * length, D_MODEL), lse


def _attn_bwd(qkv, do, adj, lse, sink, cos, sin, n_seq, length, half_window, seq_blk, dil, name):
    qb, kw, nblk = _attn_geometry(length, half_window)
    scale = 1.0 / math.sqrt(HEAD_DIM)
    with_sink = sink is not None
    nt = (((1,), (1,)), ((), ()))
    tn = (((0,), (0,)), ((), ()))
    qkv3 = qkv.reshape(n_seq, length, QKV_W)
    do3 = do.reshape(n_seq, length, D_MODEL)
    tabs = [t.reshape(dil, length, LANES) for t in (cos, sin)]
    tab_blocks = dil // seq_blk if dil >= seq_blk else 1

    def body(*refs):
        refs = list(refs)
        sink_ref = refs.pop(0) if with_sink else None
        q_ref, k_ref, v_ref, do_ref, aux_ref, lse_ref, cos_ref, sin_ref, dqkv_ref = refs[:9]
        ds_ref = refs[9] if with_sink else None
        kx_ref, vx_ref, dkx_ref, dvx_ref = refs[-4:]
        lane = lax.broadcasted_iota(jnp.int32, (1, LANES), 1)
        if with_sink:
            @pl.when(pl.program_id(0) == 0)
            def _():
                ds_ref[...] = jnp.zeros_like(ds_ref)

        for s in range(seq_blk):
            ts = s % dil
            _dup_kv(k_ref, kx_ref, s, length)
            _dup_kv(v_ref, vx_ref, s, length)
            dkx_ref[...] = jnp.zeros_like(dkx_ref)
            dvx_ref[...] = jnp.zeros_like(dvx_ref)

            def block(i, dsink):
                q0, k0 = _block_origin(i, qb, kw, half_window, length)
                valid = _band_mask_t(q0, k0, qb, kw, half_window)
                cs = cos_ref[ts, pl.ds(q0, qb), :] * scale
                sn = sin_ref[ts, pl.ds(q0, qb), :] * scale
                adj_tile = aux_ref[s, :, pl.ds(q0, qb)]
                lse_tile = lse_ref[s, :, pl.ds(q0, qb)]
                groups = range(N_KV)
                qss = [_stack_heads(q_ref, s, q0, qb, g) for g in groups]
                doss = [_stack_heads(do_ref, s, q0, qb, g) for g in groups]
                kxs = [kx_ref[g, pl.ds(k0, kw), :] for g in groups]
                sts = [lax.dot_general(kxs[g], qss[g], nt, preferred_element_type=F32) for g in groups]
                dpts = [lax.dot_general(vx_ref[g, pl.ds(k0, kw), :], doss[g], nt, preferred_element_type=F32)
                        for g in groups]
                lses = [_head_row([lse_tile[4 * g + a:4 * g + a + 1, :] * LOG2E for a in range(4)], qb) for g in groups]
                shifts = [_head_row([adj_tile[4 * g + a:4 * g + a + 1, :] for a in range(4)], qb) for g in groups]
                pts = [jnp.exp2(jnp.where(valid, sts[g], NEG_INF) - lses[g]) for g in groups]
                dsbs = [(pts[g] * (dpts[g] + shifts[g])).astype(BF16) for g in groups]
                pbs = [pt.astype(BF16) for pt in pts]
                if with_sink:
                    for g in groups:
                        sk = _head_row([sink_ref[4 * g + a] * LOG2E for a in range(4)], qb)
                        dsk = jnp.exp2(sk - lses[g]) * shifts[g]
                        for a in range(4):
                            tot = jnp.sum(dsk[:, a * qb:(a + 1) * qb], axis=1, keepdims=True)
                            dsink = dsink + jnp.where(lane == 4 * g + a, tot, 0.0)
                dqts = [lax.dot_general(kx_ref[g, pl.ds(k0, kw), 0:HEAD_DIM], dsbs[g], tn, preferred_element_type=F32)
                        for g in groups]
                for g in groups:
                    for pair in range(2):
                        col = (2 * g + pair) * LANES
                        tile = _rope_t(_unstack_pair_t(dqts[g], qb, pair), cs, sn)
                        dqkv_ref[s, pl.ds(q0, qb), col:col + LANES] = tile.astype(BF16)
                for g in groups:
                    dkx_ref[g, pl.ds(k0, kw), :] += jnp.dot(dsbs[g], qss[g], preferred_element_type=F32)
                    dvx_ref[g, pl.ds(k0, kw), :] += jnp.dot(pbs[g], doss[g], preferred_element_type=F32)
                return dsink

            if nblk == 1:
                dsink = block(0, jnp.zeros((1, LANES), F32))
            else:
                dsink = lax.fori_loop(0, nblk, block, jnp.zeros((1, LANES), F32))
            if with_sink:
                ds_ref[0:1, :] += dsink

            ch = min(length, 256)
            lo_c = lax.broadcasted_iota(jnp.int32, (ch, LANES), 1) < HEAD_DIM

            def fin(c, carry):
                r0 = pl.multiple_of(c * ch, ch)
                cs = cos_ref[ts, pl.ds(r0, ch), :]
                sn = sin_ref[ts, pl.ds(r0, ch), :]
                for j in range(N_KV // 2):
                    both = []
                    for acc_ref in (dkx_ref, dvx_ref):
                        t0 = acc_ref[2 * j, pl.ds(r0, ch), :]
                        t1 = acc_ref[2 * j + 1, pl.ds(r0, ch), :]
                        both.append(jnp.where(lo_c, t0, t1) + pltpu.roll(jnp.where(lo_c, t1, t0), HEAD_DIM, 1))
                    kcol = N_HEADS * HEAD_DIM + j * LANES
                    vcol = (N_HEADS + N_KV) * HEAD_DIM + j * LANES
                    dqkv_ref[s, pl.ds(r0, ch), kcol:kcol + LANES] = _rope_t(both[0] * LN2, cs, sn).astype(BF16)
                    dqkv_ref[s, pl.ds(r0, ch), vcol:vcol + LANES] = both[1].astype(BF16)
                return carry

            lax.fori_loop(0, length // ch, fin, 0)

    seq_map = lambda n: (n, 0, 0)
    tab_map = (lambda n: (n % tab_blocks, 0, 0)) if dil >= seq_blk else (lambda n: (0, 0, 0))
    tab_rows = min(seq_blk, dil)
    in_specs = [pl.BlockSpec((seq_blk, length, N_HEADS * HEAD_DIM), seq_map),
                pl.BlockSpec((seq_blk, length, N_KV * HEAD_DIM), lambda n: (n, 0, 4)),
                pl.BlockSpec((seq_blk, length, N_KV * HEAD_DIM), lambda n: (n, 0, 5)),
                pl.BlockSpec((seq_blk, length, D_MODEL), seq_map),
                pl.BlockSpec((seq_blk, N_HEADS, length), seq_map),
                pl.BlockSpec((seq_blk, N_HEADS, length), seq_map),
                pl.BlockSpec((tab_rows, length, LANES), tab_map),
                pl.BlockSpec((tab_rows, length, LANES), tab_map)]
    args = [qkv3, qkv3, qkv3, do3, adj, lse] + tabs
    if with_sink:
        in_specs.insert(0, pl.BlockSpec(memory_space=pltpu.SMEM))
        args.insert(0, sink)
    out_specs = [pl.BlockSpec((seq_blk, length, QKV_W), seq_map)]
    out_shape = [jax.ShapeDtypeStruct((n_seq, length, QKV_W), BF16)]
    if with_sink:
        out_specs.append(pl.BlockSpec((8, LANES), lambda n: (0, 0)))
        out_shape.append(jax.ShapeDtypeStruct((8, LANES), F32))
    outs = pl.pallas_call(
        body, name=name, grid=(n_seq // seq_blk,), in_specs=in_specs, out_specs=out_specs, out_shape=out_shape,
        scratch_shapes=[pltpu.VMEM((N_KV, length, LANES), BF16), pltpu.VMEM((N_KV, length, LANES), BF16),
                        pltpu.VMEM((N_KV, length, LANES), F32), pltpu.VMEM((N_KV, length, LANES), F32)],
        compiler_params=_cp(),
    )(*args)
    dqkv = outs[0].reshape(n_seq * length, QKV_W)
    return (dqkv, outs[1]) if with_sink else (dqkv, None)


def _head_expander():
    h = jnp.arange(LANES)[:, None]
    l = jnp.arange(D_MODEL)[None, :]
    return (l // HEAD_DIM == h).astype(BF16)


def _dot_split(a, e):
    hi = a.astype(BF16)
    lo = (a - hi.astype(F32)).astype(BF16)
    return jnp.dot(hi, e, preferred_element_type=F32) + jnp.dot(lo, e, preferred_element_type=F32)


def _dot_heads(a, e):
    return jnp.dot(a.astype(BF16), e, preferred_element_type=F32)


def _mix_weights(lses):
    m = jnp.maximum(jnp.maximum(lses[0], lses[1]), lses[2])
    es = [jnp.exp(v - m) for v in lses]
    tot = es[0] + es[1] + es[2]
    return [e / tot for e in es]


def _mix_fwd(os_, lses, name):
    t = os_[0].shape[0]
    tm = _row_tile(t, ROWS)

    def body(o0, o1, o2, l0, l1, l2, e_ref, out_ref):
        wts = _mix_weights([l0[...], l1[...], l2[...]])
        acc = jnp.zeros((tm, D_MODEL), F32)
        for w, o_ref in zip(wts, (o0, o1, o2)):
            acc = acc + _dot_split(w, e_ref[...]) * _token_rows(o_ref)
        out_ref[...] = acc.astype(BF16)

    row = pl.BlockSpec((tm, D_MODEL), lambda i: (i, 0))
    lrow = pl.BlockSpec((tm, LANES), lambda i: (i, 0))
    return pl.pallas_call(
        body, name=name, grid=(t // tm,),
        in_specs=[_token_rows_spec(o, tm) for o in os_] + [lrow] * 3 + [pl.BlockSpec((LANES, D_MODEL), lambda i: (0, 0))],
        out_specs=row, out_shape=jax.ShapeDtypeStruct((t, D_MODEL), BF16), compiler_params=_cp(),
    )(*os_, *lses, _head_expander())


def _mix_bwd(dx, w_out, os_, lses, do_shapes, name):
    t = dx.shape[0]
    tm = _row_tile(t, ROWS)
    do_structs = [jax.ShapeDtypeStruct(s, BF16) for s in do_shapes]

    def body(d_ref, w_ref, o0, o1, o2, l0, l1, l2, e_ref, et_ref, do0, do1, do2, a0, a1, a2):
        wts = _mix_weights([l0[...], l1[...], l2[...]])
        dv = lax.dot_general(d_ref[...], w_ref[...], (((1,), (1,)), ((), ())), preferred_element_type=F32)
        cs = [_dot_heads(dv * _token_rows(o_ref), et_ref[...]) for o_ref in (o0, o1, o2)]
        mean_c = wts[0] * cs[0] + wts[1] * cs[1] + wts[2] * cs[2]
        for w, c, do_ref, a_ref in zip(wts, cs, (do0, do1, do2), (a0, a1, a2)):
            do_ref[...] = (_dot_heads(w, e_ref[...]) * dv).astype(BF16).reshape(do_ref.shape)
            a_ref[...] = w * (c - mean_c) - w * c

    row = pl.BlockSpec((tm, D_MODEL), lambda i: (i, 0))
    lrow = pl.BlockSpec((tm, LANES), lambda i: (i, 0))
    e = _head_expander()
    return pl.pallas_call(
        body, name=name, grid=(t // tm,),
        in_specs=[row, pl.BlockSpec((D_MODEL, D_MODEL), lambda i: (0, 0), pipeline_mode=pl.Buffered(1))]
        + [_token_rows_spec(o, tm) for o in os_] + [lrow] * 3 + [pl.BlockSpec((LANES, D_MODEL), lambda i: (0, 0)),
                                    pl.BlockSpec((D_MODEL, LANES), lambda i: (0, 0))],
        out_specs=[_token_rows_spec(d, tm) for d in do_structs] + [lrow] * 3,
        out_shape=do_structs + [jax.ShapeDtypeStruct((t, LANES), F32)] * 3,
        compiler_params=_cp(),
    )(dx, w_out, *os_, *lses, e, e.T)


def _stats_to_tokens(stat, batch, dil):
    n_seq, _, length = stat.shape
    t = stat.transpose(0, 2, 1).reshape(n_seq * length, N_HEADS)
    return _from_residue(jnp.pad(t, ((0, 0), (0, LANES - N_HEADS))), batch, dil)


def _stats_from_tokens(stat, batch, dil, n_seq, length):
    t = _to_residue(stat[:, :N_HEADS], batch, dil)
    return t.reshape(n_seq, length, N_HEADS).transpose(0, 2, 1)


def _group_geometry(batch, seq, dil, window):
    length = seq // dil
    n_seq = batch * dil
    seq_blk = max(1, min(dil, 1024 // length))
    return n_seq, length, (window // 2) // dil, seq_blk


def _local_step(x, target, a_in, a_sink, a_out, b_in, b_out, norm_mix, norm_ffn, wg, wu, wd, final_norm):
    batch, seq, _ = x.shape
    t = batch * seq
    x0 = x.reshape(t, D_MODEL)
    tgt = target.reshape(t, D_MODEL)
    tabs = {d: _rope_tables(seq, d) for _, d in DILATED}
    nm = [norm_mix[i:i + 1] for i in range(2)]
    nf = [norm_ffn[i:i + 1] for i in range(2)]

    h0 = _rms_fwd(x0, nm[0], "rms_mix0")
    qkv0 = _qkv_proj(h0, a_in, *tabs[1], 0, "qkv0")
    o0, lse0 = _attn_fwd(qkv0, a_sink, batch, seq, HALF_WINDOW_A, 1, BF16, "attn0")
    x1, hf0 = _mm_res(o0, a_out, x0, nf[0], "out0")
    act0, g0, u0 = _ffn_up(hf0, wg[0], wu[0], 0, "ffn_up0")
    fold_dils = [d for _, d in DILATED if _needs_fold(d)]
    x2, h1, *h1_folded = _ffn_down(act0, wd[0], x1, 0, "ffn_down0", norm_w=nm[1],
                                   fold_shapes=[_folded_shape(batch, seq, d, D_MODEL) for d in fold_dils])
    h1_by_dil = dict(zip(fold_dils, h1_folded))

    geo = [_group_geometry(batch, seq, d, w) for w, d in DILATED]
    h1g, qkv1, o1, lse1, lse1r = [], [], [], [], []
    for gi, (_, d) in enumerate(DILATED):
        n_seq, length, hw, sb = geo[gi]
        hp = _to_residue(h1_by_dil.get(d, h1), batch, d)
        pj = _qkv_proj(hp, b_in, *tabs[d], gi, f"qkv1_{gi}")
        o, lse = _attn_fwd(pj, None, n_seq, length, hw, sb, BF16, f"attn1_{gi}")
        h1g.append(hp)
        qkv1.append(pj)
        o1.append(_from_residue(o, batch, d, fold=True))
        lse1r.append(lse)
        lse1.append(_stats_to_tokens(lse, batch, d))
    omix = _mix_fwd(o1, lse1, "mix")
    x3, hf1 = _mm_res(omix, b_out, x2, nf[1], "out1")
    act1, g1, u1 = _ffn_up(hf1, wg[1], wu[1], 0, "ffn_up1")
    dx4, dx4b, loss_cols, d_final = _ffn_down(act1, wd[1], x3, 0, "ffn_down1_loss",
                                                     head=(final_norm.reshape(1, D_MODEL), tgt))

    def ffn_bwd(dxo, dxob, x_mid, hf, g, u, act, layer):
        dg, du, dxm, dxmb, d_nf = _ffn_bwd(dxob, wd[layer], wg[layer], wu[layer], g, u, x_mid, nf[layer], dxo,
                                           f"ffn_bwd{layer}")
        (d_wd,) = _mm_tn(act, [dxob], f"grad_wd{layer}")
        (d_wgt,) = _mm_tn(dg, [hf], f"grad_wg{layer}")
        (d_wut,) = _mm_tn(du, [hf], f"grad_wu{layer}")
        return dxm, dxmb, d_nf, d_wgt, d_wut, d_wd

    dx3, dx3b, d_nf1, d_wg1, d_wu1, d_wd1 = ffn_bwd(dx4, dx4b, x3, hf1, g1, u1, act1, 1)

    (d_b_out,) = _mm_tn(omix, [dx3b], "grad_b_out")
    do_shapes = [_folded_shape(batch, seq, d, D_MODEL) if _needs_fold(d) else (t, D_MODEL) for _, d in DILATED]
    mb = _mix_bwd(dx3b, b_out, o1, lse1, do_shapes, "out1_mix_bwd")
    dh1, d_b_in = [], None
    for gi, (_, d) in enumerate(DILATED):
        n_seq, length, hw, sb = geo[gi]
        dog = _to_residue(mb[gi], batch, d)
        adj = _stats_from_tokens(mb[3 + gi], batch, d, n_seq, length)
        dpj, _ = _attn_bwd(qkv1[gi], dog, adj, lse1r[gi], None, *tabs[d], n_seq, length, hw, sb, d, f"attn1_bwd{gi}")
        (d_b_in,) = _mm_tn(h1g[gi], [dpj], f"grad_b_in{gi}", part=(gi, len(DILATED), d_b_in))
        dh1.append(_from_residue(_mm_nt(dpj, b_in, gi, BF16, f"qkv1_bwd{gi}"), batch, d, fold=True))
    dx2, dx2b, d_nm1 = _rms_bwd(x2, nm[1], dh1, dx3, "rms_mix_bwd1")

    dx1, dx1b, d_nf0, d_wg0, d_wu0, d_wd0 = ffn_bwd(dx2, dx2b, x1, hf0, g0, u0, act0, 0)

    do0, adj0 = _out_bwd(dx1b, a_out, o0, "out0_bwd")
    (d_a_out,) = _mm_tn(o0, [dx1b], "grad_a_out")
    adj0 = _stats_from_tokens(adj0, batch, 1, batch, seq)
    dqkv0, d_sink = _attn_bwd(qkv0, do0, adj0, lse0, a_sink, *tabs[1], batch, seq, HALF_WINDOW_A, 1, 1, "attn0_bwd")
    (d_a_in,) = _mm_tn(h0, [dqkv0], "grad_a_in")
    gx, d_nm0 = _mm_nt_rms(dqkv0, a_in, x0, nm[0], dx1, "qkv0_bwd")

    grads = dict(a_in=d_a_in, a_out=d_a_out, b_in=d_b_in, b_out=d_b_out,
                 wg=(d_wg0, d_wg1), wu=(d_wu0, d_wu1), wd=(d_wd0, d_wd1))
    vecs = dict(norm_mix=(d_nm0, d_nm1), norm_ffn=(d_nf0, d_nf1), final=d_final, loss_cols=loss_cols, sink=d_sink)
    return gx.reshape(x.shape), grads, vecs


ANY = pl.BlockSpec(memory_space=pl.ANY)
HBM = pltpu.MemorySpace.HBM


def _me():
    return lax.axis_index("x"), lax.axis_index("y"), lax.axis_index("c")


def _chip_peer(x, y, j):
    px = 1 - x if j & 2 else x
    py = 1 - y if j & 1 else y
    return px, py, 2 * px + py


def _remote(src, dst, sems, k, dev):
    return pltpu.make_async_remote_copy(src_ref=src, dst_ref=dst, send_sem=sems[0].at[k], recv_sem=sems[1].at[k],
                                        device_id=dev, device_id_type=MESH)


def _col_window(ref, q, width):
    return ref.at[:, pl.ds(pl.multiple_of(q * width, LANES), width)]


def _half0(ref, h):
    n = ref.shape[0] // 2
    return ref.at[pl.ds(h * n, n)]


def _half1(ref, h):
    n = ref.shape[1] // 2
    return ref.at[:, pl.ds(h * n, n)]


def _half_rows(ref, h):
    n = ref.shape[-2] // 2
    if len(ref.shape) == 2:
        return ref.at[pl.ds(h * n, n)]
    return ref.at[:, pl.ds(h * n, n)]


def _place_shard(w, layer, q_arr, col, name):
    _, rows, cols = w.shape

    def body(q_ref, w_ref, o_ref):
        o_ref[...] = w_ref[...].astype(BF16)

    if col:
        out_spec = pl.BlockSpec((rows, cols), lambda l, q: (0, q[0]))
        out_shape = jax.ShapeDtypeStruct((rows, N_CHIPS * cols), BF16)
    else:
        out_spec = pl.BlockSpec((None, None, rows, cols), lambda l, q: (q[0], 0, 0, 0))
        out_shape = jax.ShapeDtypeStruct((N_CHIPS, 1, rows, cols), BF16)
    return pl.pallas_call(
        body, name=name,
        grid_spec=pltpu.PrefetchScalarGridSpec(
            num_scalar_prefetch=1, grid=(1,),
            in_specs=[pl.BlockSpec((None, rows, cols), lambda l, q: (layer, 0, 0))], out_specs=out_spec),
        out_shape=out_shape, compiler_params=_cp(),
    )(q_arr, w)


def _handshake(peers):
    barrier = pltpu.get_barrier_semaphore()
    for p in peers:
        pl.semaphore_signal(barrier, inc=1, device_id=p, device_id_type=MESH)
    pl.semaphore_wait(barrier, len(peers))


def _on_sequencer(name, collective_id, n_sem, n_local, body):
    @pl.kernel(mesh=plsc.ScalarSubcoreMesh(axis_name="seq", num_cores=1), name=name,
               scratch_types=(pltpu.SemaphoreType.DMA((n_sem,)), pltpu.SemaphoreType.DMA((n_sem,)),
                              pltpu.SemaphoreType.DMA((max(n_local, 1),))),
               compiler_params=pltpu.CompilerParams(collective_id=collective_id))
    def launch(send_sems, recv_sems, local_sems):
        body((send_sems, recv_sems), local_sems)

    launch()


def _gather_plan(outs, col_fam, sems, handshake):
    n_w = len(outs)
    x, y, c = _me()
    myq = 2 * x + y
    sib = (x, y, 1 - c)
    if handshake:
        _handshake([sib] + [_chip_peer(x, y, j)[:2] + (c,) for j in (1, 2, 3)])

    def slot(w, q):
        if col_fam[w]:
            return _col_window(outs[w], q, outs[w].shape[1] // N_CHIPS)
        return outs[w].at[q]

    first = []
    for w in range(n_w):
        for j in (1, 2, 3):
            px, py, _ = _chip_peer(x, y, j)
            mine = _half_rows(slot(w, myq), c)
            cp = _remote(mine, mine, sems, w * 6 + j - 1, (px, py, c))
            cp.start()
            first.append(cp)
    passed = []
    for w in range(n_w):
        for j in (1, 2, 3):
            _, _, pq = _chip_peer(x, y, j)
            land = _half_rows(slot(w, pq), c)
            _remote(land, land, sems, w * 6 + j - 1, sib).wait_recv()
            cp = _remote(land, land, sems, w * 6 + 2 + j, sib)
            cp.start()
            passed.append(cp)
    for w in range(n_w):
        for j in (1, 2, 3):
            _, _, pq = _chip_peer(x, y, j)
            land = _half_rows(slot(w, pq), 1 - c)
            _remote(land, land, sems, w * 6 + 2 + j, sib).wait_recv()
    for cp in first + passed:
        cp.wait_send()


def _gather_weights(bufs, col_fam):
    n_w = len(bufs)

    def body(*refs):
        _gather_plan(refs[n_w:2 * n_w], col_fam, refs[2 * n_w:2 * n_w + 2], False)

    return pl.pallas_call(
        body, name="gather_weights", in_specs=[ANY] * n_w, out_specs=[ANY] * n_w,
        out_shape=[jax.ShapeDtypeStruct(b.shape, b.dtype) for b in bufs],
        input_output_aliases={w: w for w in range(n_w)},
        scratch_shapes=[pltpu.SemaphoreType.DMA((6 * n_w,)), pltpu.SemaphoreType.DMA((6 * n_w,))],
    )(*bufs)


def _gather_weights_async(bufs, col_fam, name, collective_id):
    refs = [jax.new_ref(b, memory_space=HBM) for b in bufs]
    _on_sequencer(name, collective_id, 6 * len(bufs), 0,
                  lambda sems, _: _gather_plan(refs, col_fam, sems, True))
    return [r[...] for r in refs]


def _grad_half(ref, col, h):
    return _half0(ref, h) if col else _half1(ref, h)


def _swap_halves_with_sibling(grads, col_fam):
    n_w = len(grads)

    def body(*refs):
        _swap_plan(refs[:n_w], refs[n_w:2 * n_w], col_fam, refs[2 * n_w:], False)

    return pl.pallas_call(
        body, name="grad_swap_sibling", in_specs=[ANY] * n_w, out_specs=[ANY] * n_w,
        out_shape=_swap_shapes(grads, col_fam),
        scratch_shapes=[pltpu.SemaphoreType.DMA((n_w,)), pltpu.SemaphoreType.DMA((n_w,))],
    )(*grads)


def _swap_shapes(grads, col_fam):
    out = []
    for w, g in enumerate(grads):
        shp = (g.shape[0] // 2, g.shape[1]) if col_fam[w] else (g.shape[0], g.shape[1] // 2, g.shape[2])
        out.append(jax.ShapeDtypeStruct(shp, g.dtype))
    return out


def _swap_plan(ins, outs, col_fam, sems, handshake):
    x, y, c = _me()
    sib = (x, y, 1 - c)
    if handshake:
        _handshake([sib])
    cps = [_remote(_grad_half(ins[w], col_fam[w], 1 - c), outs[w], sems, w, sib) for w in range(len(ins))]
    for cp in cps:
        cp.start()
    for cp in cps:
        cp.wait_recv()
    for cp in cps:
        cp.wait_send()


def _swap_halves_async(grads, col_fam, name, collective_id):
    srcs = [jax.new_ref(g, memory_space=HBM) for g in grads]
    dsts = [jax.empty_ref(s, memory_space=HBM) for s in _swap_shapes(grads, col_fam)]
    _on_sequencer(name, collective_id, len(grads), 0, lambda sems, _: _swap_plan(srcs, dsts, col_fam, sems, True))
    return [r[...] for r in srcs], [r[...] for r in dsts]


def _half_add(mines, recvs, c_arr, col_fam, name):
    n_w = len(mines)
    mine_specs, recv_specs = [], []
    for recv, col in zip(recvs, col_fam):
        if col:
            rows, n = recv.shape
            tr = rows // N_CHIPS
            mine_specs.append(pl.BlockSpec((tr, n), lambda i, c: (N_CHIPS * c[0] + i, 0)))
            recv_specs.append(pl.BlockSpec((tr, n), lambda i, c: (i, 0)))
        else:
            _, rows, n = recv.shape
            mine_specs.append(pl.BlockSpec((None, rows, n), lambda q, c: (q, c[0], 0)))
            recv_specs.append(pl.BlockSpec((None, rows, n), lambda q, c: (q, 0, 0)))

    def body(c_ref, *refs):
        for a_ref, b_ref, o_ref in zip(refs[:n_w], refs[n_w:2 * n_w], refs[2 * n_w:]):
            o_ref[...] = (a_ref[...].astype(F32) + b_ref[...].astype(F32)).astype(BF16)

    return pl.pallas_call(
        body, name=name,
        grid_spec=pltpu.PrefetchScalarGridSpec(num_scalar_prefetch=1, grid=(N_CHIPS,),
                                               in_specs=mine_specs + recv_specs, out_specs=recv_specs),
        out_shape=[jax.ShapeDtypeStruct(r.shape, BF16) for r in recvs], compiler_params=_cp(),
    )(c_arr, *mines, *recvs)


def _scatter_shapes(sums, col_fam):
    out = []
    for w, s in enumerate(sums):
        shp = (s.shape[0], s.shape[1] // N_CHIPS) if col_fam[w] else s.shape[1:]
        out.append(jax.ShapeDtypeStruct((N_CHIPS,) + shp, s.dtype))
    return out


def _scatter_copies(ins, outs, col_fam, sems, lsem):
    n_w = len(ins)
    x, y, c = _me()
    myq = 2 * x + y

    def slab(w, q):
        if col_fam[w]:
            return _col_window(ins[w], q, ins[w].shape[1] // N_CHIPS)
        return ins[w].at[q]

    local = [pltpu.make_async_copy(slab(w, myq), outs[w].at[myq], lsem.at[w]) for w in range(n_w)]
    sends, lands = [], []
    for w in range(n_w):
        for j in (1, 2, 3):
            px, py, pq = _chip_peer(x, y, j)
            sends.append(_remote(slab(w, pq), outs[w].at[myq], sems, w * 3 + j - 1, (px, py, c)))
            land = outs[w].at[pq]
            lands.append(_remote(land, land, sems, w * 3 + j - 1, (x, y, c)))
    return local, sends, lands


def _scatter_start(ins, outs, col_fam, sems, lsem):
    local, sends, _ = _scatter_copies(ins, outs, col_fam, sems, lsem)
    for cp in local + sends:
        cp.start()


def _scatter_wait(ins, outs, col_fam, sems, lsem):
    local, sends, lands = _scatter_copies(ins, outs, col_fam, sems, lsem)
    for cp in lands:
        cp.wait_recv()
    for cp in sends:
        cp.wait_send()
    for cp in local:
        cp.wait()


def _scatter_chip_sums_async(sums, col_fam, name, collective_id):
    srcs = [jax.new_ref(s, memory_space=HBM) for s in sums]
    dsts = [jax.empty_ref(s, memory_space=HBM) for s in _scatter_shapes(sums, col_fam)]

    def plan(sems, lsem):
        x, y, c = _me()
        _handshake([_chip_peer(x, y, j)[:2] + (c,) for j in (1, 2, 3)])
        _scatter_start(srcs, dsts, col_fam, sems, lsem)
        _scatter_wait(srcs, dsts, col_fam, sems, lsem)

    _on_sequencer(name, collective_id, 3 * len(sums), len(sums), plan)
    return [r[...] for r in dsts]


def _sum_chips(parts, c_arr, prev, lead, shape, name):
    _, rows, n = parts.shape
    tr = rows // 2 if rows % 32 == 0 else rows
    nblk = rows // tr

    def body(c_ref, p_ref, *rest):
        o_ref = rest[-1]
        acc = p_ref[0].astype(F32)
        for q in range(1, N_CHIPS):
            acc = acc + p_ref[q].astype(F32)
        o_ref[...] = acc

    in_specs = [pl.BlockSpec((N_CHIPS, tr, n), lambda i, c: (0, i, 0))]
    args = [c_arr, parts]
    aliases = {}
    if prev is not None:
        in_specs.append(ANY)
        args.append(prev)
        aliases = {2: 0}
    return pl.pallas_call(
        body, name=name,
        grid_spec=pltpu.PrefetchScalarGridSpec(
            num_scalar_prefetch=1, grid=(nblk,), in_specs=in_specs,
            out_specs=pl.BlockSpec((None, tr, n), lambda i, c: (lead, c[0] * nblk + i, 0))),
        out_shape=jax.ShapeDtypeStruct(shape, F32), input_output_aliases=aliases, compiler_params=_cp(),
    )(*args)


def _join_plan(outs, place, sems, handshake):
    x, y, c = _me()
    sib = (x, y, 1 - c)
    if handshake:
        _handshake([sib])

    def half(k, h):
        o, lead = place[k]
        return _half_rows(outs[o].at[lead], h)

    cps = [_remote(half(k, c), half(k, c), sems, k, sib) for k in range(len(place))]
    for cp in cps:
        cp.start()
    for k in range(len(place)):
        land = half(k, 1 - c)
        _remote(land, land, sems, k, sib).wait_recv()
    for cp in cps:
        cp.wait_send()


def _join_halves(bufs, place, name, scatter=None):
    n_o = len(bufs)
    n_h = len(place)
    sums, col_fam = scatter if scatter else ((), ())
    n_w = len(sums)

    def body(*refs):
        ins, outs = refs[n_o:n_o + n_w], refs[2 * n_o + n_w:2 * (n_o + n_w)]
        scratch = refs[2 * (n_o + n_w):]
        if n_w:
            _scatter_start(ins, outs, col_fam, scratch[2:4], scratch[4])
        _join_plan(refs[n_o + n_w:2 * n_o + n_w], place, scratch[:2], False)
        if n_w:
            _scatter_wait(ins, outs, col_fam, scratch[2:4], scratch[4])

    scratch_shapes = [pltpu.SemaphoreType.DMA((n_h,)), pltpu.SemaphoreType.DMA((n_h,))]
    if n_w:
        scratch_shapes += [pltpu.SemaphoreType.DMA((3 * n_w,)), pltpu.SemaphoreType.DMA((3 * n_w,)),
                           pltpu.SemaphoreType.DMA((n_w,))]
    res = pl.pallas_call(
        body, name=name, in_specs=[ANY] * (n_o + n_w), out_specs=[ANY] * (n_o + n_w),
        out_shape=[jax.ShapeDtypeStruct(b.shape, b.dtype) for b in bufs] + _scatter_shapes(sums, col_fam),
        input_output_aliases={k: k for k in range(n_o)}, scratch_shapes=scratch_shapes,
    )(*bufs, *sums)
    return res[:n_o], res[n_o:]


def _allreduce_rows(rows):
    n_dev = 8
    n_r = len(rows)
    assert n_r <= 8

    def body(*refs):
        r_refs = refs[:n_r]
        o_ref, slots, send_sems, recv_sems = refs[n_r:]
        x, y, c = _me()
        me = 4 * x + 2 * y + c
        slots[me] = jnp.concatenate([r[...] for r in r_refs] + [jnp.zeros((8 - n_r, D_MODEL), F32)], axis=0)

        def peer(k):
            return (1 - x if k & 4 else x, 1 - y if k & 2 else y, 1 - c if k & 1 else c)

        cps = []
        for k in range(1, n_dev):
            cp = pltpu.make_async_remote_copy(src_ref=slots.at[me], dst_ref=slots.at[me], send_sem=send_sems.at[k - 1],
                                              recv_sem=recv_sems.at[k - 1], device_id=peer(k), device_id_type=MESH)
            cp.start()
            cps.append(cp)
        for k in range(1, n_dev):
            px, py, pc = peer(k)
            land = slots.at[4 * px + 2 * py + pc]
            pltpu.make_async_remote_copy(src_ref=land, dst_ref=land, send_sem=send_sems.at[k - 1],
                                         recv_sem=recv_sems.at[k - 1], device_id=peer(k),
                                         device_id_type=MESH).wait_recv()
        for cp in cps:
            cp.wait_send()
        acc = slots[0]
        for d in range(1, n_dev):
            acc = acc + slots[d]
        o_ref[...] = acc

    vm = pl.BlockSpec(memory_space=pltpu.VMEM)
    return pl.pallas_call(
        body, name="allreduce_rows", in_specs=[vm] * n_r, out_specs=vm,
        out_shape=jax.ShapeDtypeStruct((8, D_MODEL), F32),
        scratch_shapes=[pltpu.VMEM((n_dev, 8, D_MODEL), F32), pltpu.SemaphoreType.DMA((n_dev - 1,)),
                        pltpu.SemaphoreType.DMA((n_dev - 1,))],
    )(*rows)


def _adamw(w, g, m, v, name):
    shape = w.shape
    if len(shape) == 1:
        lead, rows, cols = 1, 1, shape[0]
    else:
        rows, cols = shape[-2:]
        lead = math.prod(shape[:-2])
    args = [a.reshape(lead, rows, cols) for a in (w, g, m, v)]
    tr = rows // 2 if rows % 16 == 0 else rows

    def body(w_ref, g_ref, m_ref, v_ref, d_ref, nm_ref, nv_ref):
        gv = g_ref[...]
        nm = ADAM_B1 * m_ref[...] + (1.0 - ADAM_B1) * gv
        nv = ADAM_B2 * v_ref[...] + (1.0 - ADAM_B2) * jnp.square(gv)
        m_hat = nm / (1.0 - ADAM_B1 ** ADAM_STEP)
        v_hat = nv / (1.0 - ADAM_B2 ** ADAM_STEP)
        d_ref[...] = -ADAM_LR * (m_hat / (jnp.sqrt(v_hat) + ADAM_EPS) + ADAM_WD * w_ref[...])
        nm_ref[...] = nm
        nv_ref[...] = nv

    spec = pl.BlockSpec((None, tr, cols), lambda l, i: (l, i, 0))
    outs = pl.pallas_call(
        body, name=name, grid=(lead, rows // tr), in_specs=[spec] * 4, out_specs=[spec] * 3,
        out_shape=[jax.ShapeDtypeStruct((lead, rows, cols), F32)] * 3, compiler_params=_cp(),
    )(*args)
    return [o.reshape(shape) for o in outs]


def kernel(x, a_w_in, a_sink, a_w_out, b_w_in, b_w_out, norm_mix, norm_ffn, w_gate, w_up, w_down, final_norm, loss_target, m_a_w_in, m_a_sink, m_a_w_out, m_b_w_in, m_b_w_out, m_norm_mix, m_norm_ffn, m_w_gate, m_w_up, m_w_down, m_final_norm, v_a_w_in, v_a_sink, v_a_w_out, v_b_w_in, v_b_w_out, v_norm_mix, v_norm_ffn, v_w_gate, v_w_up, v_w_down, v_final_norm):
    weights = dict(a_w_in=a_w_in, a_sink=a_sink, a_w_out=a_w_out, b_w_in=b_w_in, b_w_out=b_w_out, norm_mix=norm_mix,
                   norm_ffn=norm_ffn, w_gate=w_gate, w_up=w_up, w_down=w_down, final_norm=final_norm)
    mom = dict(a_w_in=m_a_w_in, a_sink=m_a_sink, a_w_out=m_a_w_out, b_w_in=m_b_w_in, b_w_out=m_b_w_out,
               norm_mix=m_norm_mix, norm_ffn=m_norm_ffn, w_gate=m_w_gate, w_up=m_w_up, w_down=m_w_down,
               final_norm=m_final_norm)
    var = dict(a_w_in=v_a_w_in, a_sink=v_a_sink, a_w_out=v_a_w_out, b_w_in=v_b_w_in, b_w_out=v_b_w_out,
               norm_mix=v_norm_mix, norm_ffn=v_norm_ffn, w_gate=v_w_gate, w_up=v_w_up, w_down=v_w_down,
               final_norm=v_final_norm)
    order = ["a_w_in", "a_sink", "a_w_out", "b_w_in", "b_w_out", "norm_mix", "norm_ffn", "w_gate", "w_up", "w_down",
             "final_norm"]
    swapped = ("w_gate", "w_up")
    for n in swapped:
        weights[n], mom[n], var[n] = (a.transpose(0, 2, 1) for a in (weights[n], mom[n], var[n]))
    w_gate_t, w_up_t = weights["w_gate"], weights["w_up"]

    c_arr = lax.axis_index("c").astype(jnp.int32).reshape(1)
    q_arr = (2 * lax.axis_index("x") + lax.axis_index("y")).astype(jnp.int32).reshape(1)

    def placed(w, layer, col, nm):
        return _place_shard(w, layer, q_arr, col, f"place_{nm}")

    (a_in,) = _gather_weights_async([placed(a_w_in, 0, True, "a_in")], (True,), "gather_weights_first", 6)
    a_out, wg0, wu0, wd0 = _gather_weights_async(
        [placed(a_w_out, 0, False, "a_out"), placed(w_gate_t, 0, False, "wg0"), placed(w_up_t, 0, False, "wu0"),
         placed(w_down, 0, False, "wd0")], (False,) * 4, "gather_weights_layer0", 1)
    b_in, b_out, wg1, wu1, wd1 = _gather_weights_async(
        [placed(b_w_in, 0, True, "b_in"), placed(b_w_out, 0, False, "b_out"), placed(w_gate_t, 1, False, "wg1"),
         placed(w_up_t, 1, False, "wu1"), placed(w_down, 1, False, "wd1")], (True,) + (False,) * 4,
        "gather_weights_layer1", 7)
    a_out = a_out.reshape(D_MODEL, D_MODEL)
    b_out = b_out.reshape(D_MODEL, D_MODEL)
    wg, wu, wd = (wg0, wg1), (wu0, wu1), (wd0, wd1)

    gx, grads, vecs = _local_step(x, loss_target, a_in, a_sink[0], a_out, b_in, b_out, norm_mix, norm_ffn, wg, wu, wd,
                                  final_norm)

    rows_out = D_MODEL // N_CHIPS
    partials = [grads["a_in"], grads["b_in"],
                grads["a_out"].reshape(N_CHIPS, rows_out, D_MODEL), grads["b_out"].reshape(N_CHIPS, rows_out, D_MODEL),
                grads["wg"][0], grads["wg"][1], grads["wu"][0], grads["wu"][1], grads["wd"][0], grads["wd"][1]]
    col_fam = (True, True) + (False,) * 8
    names = ("a_in", "b_in", "a_out", "b_out", "wg0", "wg1", "wu0", "wu1", "wd0", "wd1")
    contrib = [None] * len(partials)

    def reduce_group(idx, tag, ids):
        parts = [partials[k] for k in idx]
        cols = tuple(col_fam[k] for k in idx)
        parts, theirs = _swap_halves_async(parts, cols, f"grad_swap_{tag}", ids[0])
        sums = _half_add(parts, theirs, c_arr, cols, f"chip_sum_{tag}")
        for k, o in zip(idx, _scatter_chip_sums_async(sums, cols, f"grad_scatter_{tag}", ids[1])):
            contrib[k] = o

    reduce_group([1, 3, 5, 7, 9], "layer1", (2, 3))
    reduce_group([2, 4, 6, 8], "ffn0", (4, 5))
    a_in_theirs = _swap_halves_with_sibling(partials[:1], col_fam[:1])
    a_in_sum = _half_add(partials[:1], a_in_theirs, c_arr, col_fam[:1], "chip_sum_a_in")
    shapes = [a_w_in.shape, b_w_in.shape, a_w_out.shape, b_w_out.shape, w_down.shape, w_down.shape, w_down.shape]
    place = [(0, 0), (1, 0), (2, 0), (3, 0), (4, 0), (4, 1), (5, 0), (5, 1), (6, 0), (6, 1)]
    bufs = [None] * len(shapes)
    for p, nm, (o, lead) in list(zip(contrib, names, place))[1:]:
        bufs[o] = _sum_chips(p, c_arr, bufs[o], lead, shapes[o], f"sum_chips_{nm}")
    (g_b_in, g_a_out, g_b_out, g_wg, g_wu, g_wd), contrib[:1] = _join_halves(
        bufs[1:], [(o - 1, lead) for o, lead in place[1:]], "grad_join_sibling", scatter=(a_in_sum, col_fam[:1]))
    bufs[0] = _sum_chips(contrib[0], c_arr, None, 0, shapes[0], "sum_chips_a_in")
    (g_a_in,), _ = _join_halves(bufs[:1], place[:1], "grad_join_a_in")

    sink_row = jnp.pad(vecs["sink"][0:1], ((0, 0), (0, D_MODEL - LANES)))
    tot = _allreduce_rows([vecs["norm_mix"][0], vecs["norm_mix"][1], vecs["norm_ffn"][0], vecs["norm_ffn"][1],
                           vecs["final"], vecs["loss_cols"], sink_row])
    loss = (0.5 / D_MODEL) * jnp.sum(tot[5])
    gw = dict(a_w_in=g_a_in, a_sink=tot[6:7, :N_HEADS], a_w_out=g_a_out, b_w_in=g_b_in, b_w_out=g_b_out,
              norm_mix=tot[0:2], norm_ffn=tot[2:4], w_gate=g_wg, w_up=g_wu, w_down=g_wd, final_norm=tot[4])

    delta, new_m, new_v = {}, {}, {}
    for n in order:
        delta[n], new_m[n], new_v[n] = _adamw(weights[n], gw[n], mom[n], var[n], f"adamw_{n}")
    for n in swapped:
        gw[n], delta[n], new_m[n], new_v[n] = (a.transpose(0, 2, 1) for a in (gw[n], delta[n], new_m[n], new_v[n]))
    return (loss, gx, *[gw[n] for n in order], *[delta[n] for n in order], *[new_m[n] for n in order],
            *[new_v[n] for n in order])
```

```python
import math

import jax
import jax.numpy as jnp
import numpy as np
from jax import lax
from jax.experimental import pallas as pl
from jax.experimental.pallas import tpu as pltpu
from jax.experimental.pallas import tpu_sc as plsc

F32 = jnp.float32
BF16 = jnp.bfloat16

D_MODEL = 1024
HEAD_DIM = 64
N_HEADS = 16
N_KV = 4
QKV_W = 1536
D_FF = 2816
N_CHIPS = 4
FF_SH = D_FF // N_CHIPS
HALF_WINDOW_A = 128
DILATED = ((128, 1), (512, 4), (2048, 16))
ROPE_THETA = 10000.0
RMS_EPS = 1e-6
NEG_INF = -1e30
LANES = 128
ADAM_LR, ADAM_B1, ADAM_B2, ADAM_EPS, ADAM_WD, ADAM_STEP = 0.001, 0.9, 0.999, 1e-08, 0.01, 10
VMEM_LIMIT = 56 * 1024 * 1024
ROWS = 512
MATMUL_ROWS = 1024
FFN_BWD_ROWS = 256
LOG2E = math.log2(math.e)
LN2 = math.log(2.0)
Q_SCALE = LOG2E / math.sqrt(HEAD_DIM)
GRAD_TOKENS = 2048
MESH = pl.DeviceIdType.MESH


def _cp(**kw):
    return pltpu.CompilerParams(vmem_limit_bytes=VMEM_LIMIT, **kw)


def _row_tile(t, cap):
    tm = min(cap, t)
    assert t % tm == 0
    return tm


def _rope_tables(seq, dil):
    inv = 1.0 / (ROPE_THETA ** (np.arange(0, HEAD_DIM, 2, dtype=np.float32) / HEAD_DIM))
    ang = np.arange(seq, dtype=np.float32)[:, None] * inv.astype(np.float32)[None, :]
    cos, sin = np.cos(ang), np.sin(ang)
    cos = np.tile(cos, (1, 4))
    sin = np.concatenate([-sin, sin, -sin, sin], axis=1)

    def perm(t):
        return jnp.asarray(t.reshape(seq // dil, dil, LANES).transpose(1, 0, 2).reshape(seq, LANES), dtype=F32)

    return perm(cos), perm(sin)


def _swap_halves(t):
    lane = lax.broadcasted_iota(jnp.int32, t.shape, 1)
    return jnp.where((lane % HEAD_DIM) < HEAD_DIM // 2, pltpu.roll(t, LANES - 32, 1), pltpu.roll(t, 32, 1))


def _rope(t, cos, sin):
    return t * cos + _swap_halves(t) * sin


def _rope_t(t, cos, sin):
    return t * cos - _swap_halves(t) * sin


def _to_residue(t, batch, dil):
    if dil == 1:
        return t
    if t.ndim == 2:
        t = t.reshape(batch, t.shape[0] // batch // dil, dil, t.shape[1])
    return t.transpose(0, 2, 1, 3).reshape(-1, t.shape[-1])


def _needs_fold(dil):
    return dil > 1 and dil % 16 != 0


def _folded_shape(batch, seq, dil, cols):
    return (batch, seq // dil, dil, cols)


def _from_residue(t, batch, dil, fold=False):
    if dil == 1:
        return t
    s = t.shape[0] // batch
    nat = t.reshape(batch, dil, s // dil, t.shape[1]).transpose(0, 2, 1, 3)
    return nat if fold else nat.reshape(t.shape)


def _token_rows_spec(a, tm):
    if a.ndim == 2:
        return pl.BlockSpec((tm, a.shape[1]), lambda i: (i, 0))
    _, length, dil, c = a.shape
    per_seq = length * dil // tm
    return pl.BlockSpec((None, tm // dil, dil, c), lambda i: (i // per_seq, i % per_seq, 0, 0))


def _token_rows(ref):
    v = ref[...]
    return v if v.ndim == 2 else v.reshape(v.shape[0] * v.shape[1], v.shape[2])


def _rms_fwd(x, w, name):
    t = x.shape[0]
    tm = _row_tile(t, ROWS)

    def body(x_ref, w_ref, o_ref):
        o_ref[...] = _rms_tile(x_ref[...], w_ref[...]).astype(BF16)

    return pl.pallas_call(
        body, name=name, grid=(t // tm,),
        in_specs=[pl.BlockSpec((tm, D_MODEL), lambda i: (i, 0)), pl.BlockSpec((1, D_MODEL), lambda i: (0, 0))],
        out_specs=pl.BlockSpec((tm, D_MODEL), lambda i: (i, 0)),
        out_shape=jax.ShapeDtypeStruct((t, D_MODEL), BF16), compiler_params=_cp(),
    )(x, w)


def _rms_bwd_tile(xv, wv, dy, dres):
    r = lax.rsqrt(jnp.mean(xv * xv, axis=-1, keepdims=True) + RMS_EPS)
    xh = xv * r
    dxh = dy * wv
    dx = dres + r * (dxh - xh * jnp.mean(dxh * xh, axis=-1, keepdims=True))
    return dx, jnp.sum(dy * xh, axis=0, keepdims=True)


def _accumulate(ref, part):
    @pl.when(pl.program_id(0) == 0)
    def _():
        ref[...] = jnp.zeros_like(ref)

    ref[...] += part


def _rms_bwd(x, w, dhs, dres, name):
    t = x.shape[0]
    tm = _row_tile(t, ROWS)
    n = len(dhs)

    def body(*refs):
        x_ref, w_ref = refs[0], refs[1]
        dh_refs = refs[2:2 + n]
        dres_ref = refs[2 + n]
        dx_ref, dxb_ref, dw_ref = refs[3 + n:]
        dy = _token_rows(dh_refs[0]).astype(F32)
        for k in range(1, n):
            dy = dy + _token_rows(dh_refs[k]).astype(F32)
        dx, dw = _rms_bwd_tile(x_ref[...], w_ref[...], dy, dres_ref[...])
        dx_ref[...] = dx
        dxb_ref[...] = dx.astype(BF16)
        _accumulate(dw_ref, dw)

    row = pl.BlockSpec((tm, D_MODEL), lambda i: (i, 0))
    vec = pl.BlockSpec((1, D_MODEL), lambda i: (0, 0))
    return pl.pallas_call(
        body, name=name, grid=(t // tm,),
        in_specs=[row, vec] + [_token_rows_spec(dh, tm) for dh in dhs] + [row],
        out_specs=[row, row, vec],
        out_shape=[jax.ShapeDtypeStruct((t, D_MODEL), F32), jax.ShapeDtypeStruct((t, D_MODEL), BF16),
                   jax.ShapeDtypeStruct((1, D_MODEL), F32)],
        compiler_params=_cp(),
    )(x, w, *dhs, dres)


def _final_tile(xv, wv, tv):
    r = lax.rsqrt(jnp.mean(xv * xv, axis=-1, keepdims=True) + RMS_EPS)
    xh = xv * r
    err = xh * wv - tv
    dy = err * (1.0 / D_MODEL)
    dxh = dy * wv
    dx = r * (dxh - xh * jnp.mean(dxh * xh, axis=-1, keepdims=True))
    return dx, jnp.sum(err * err, axis=0, keepdims=True), jnp.sum(dy * xh, axis=0, keepdims=True)


def _qkv_proj(h, w, cos, sin, group, name):
    t = h.shape[0]
    seq = cos.shape[0]
    tm = _row_tile(seq, MATMUL_ROWS)
    n_q = N_HEADS * HEAD_DIM // LANES
    n_rope = (N_HEADS + N_KV) * HEAD_DIM // LANES
    scale = Q_SCALE

    def body(h_ref, w_ref, cos_ref, sin_ref, o_ref):
        acc = jnp.dot(h_ref[...], w_ref[...], preferred_element_type=F32)
        cs, sn = cos_ref[...], sin_ref[...]
        csq, snq = cs * scale, sn * scale
        for c in range(QKV_W // LANES):
            blk = acc[:, c * LANES:(c + 1) * LANES]
            if c < n_q:
                blk = _rope(blk, csq, snq)
            elif c < n_rope:
                blk = _rope(blk, cs, sn)
            o_ref[:, c * LANES:(c + 1) * LANES] = blk.astype(BF16)

    tab = pl.BlockSpec((tm, LANES), lambda i: (i % (seq // tm), 0))
    return pl.pallas_call(
        body, name=name, grid=(t // tm,),
        in_specs=[pl.BlockSpec((tm, D_MODEL), lambda i: (i, 0)),
                  pl.BlockSpec((D_MODEL, QKV_W), lambda i: (0, group)), tab, tab],
        out_specs=pl.BlockSpec((tm, QKV_W), lambda i: (i, 0)),
        out_shape=jax.ShapeDtypeStruct((t, QKV_W), BF16), compiler_params=_cp(),
    )(h, w, cos, sin)


def _rms_tile(xv, wv):
    return (xv * lax.rsqrt(jnp.mean(xv * xv, axis=-1, keepdims=True) + RMS_EPS)) * wv


def _mm_res(a, w, res, nw, name):
    t, k = a.shape
    tm = _row_tile(t, ROWS)

    def body(a_ref, w_ref, r_ref, nw_ref, o_ref, h_ref):
        xv = r_ref[...] + jnp.dot(a_ref[...], w_ref[...], preferred_element_type=F32)
        o_ref[...] = xv
        h_ref[...] = _rms_tile(xv, nw_ref[...]).astype(BF16)

    row = pl.BlockSpec((tm, D_MODEL), lambda i: (i, 0))
    return pl.pallas_call(
        body, name=name, grid=(t // tm,),
        in_specs=[pl.BlockSpec((tm, k), lambda i: (i, 0)),
                  pl.BlockSpec((k, D_MODEL), lambda i: (0, 0), pipeline_mode=pl.Buffered(1)), row,
                  pl.BlockSpec((1, D_MODEL), lambda i: (0, 0))],
        out_specs=[row, row],
        out_shape=[jax.ShapeDtypeStruct((t, D_MODEL), F32), jax.ShapeDtypeStruct((t, D_MODEL), BF16)],
        compiler_params=_cp(),
    )(a, w, res, nw)


def _mm_nt(dy, w, group, out_dtype, name):
    t, n = dy.shape
    k = w.shape[0]
    tm = _row_tile(t, MATMUL_ROWS)

    def body(dy_ref, w_ref, o_ref):
        o_ref[...] = lax.dot_general(dy_ref[...], w_ref[...], (((1,), (1,)), ((), ())),
                                     preferred_element_type=F32).astype(out_dtype)

    return pl.pallas_call(
        body, name=name, grid=(t // tm,),
        in_specs=[pl.BlockSpec((tm, n), lambda i: (i, 0)), pl.BlockSpec((k, n), lambda i: (0, group))],
        out_specs=pl.BlockSpec((tm, k), lambda i: (i, 0)),
        out_shape=jax.ShapeDtypeStruct((t, k), out_dtype), compiler_params=_cp(),
    )(dy, w)


def _mm_nt_rms(dy, w, x, nw, dres, name):
    t, n = dy.shape
    tm = _row_tile(t, ROWS)

    def body(dy_ref, w_ref, x_ref, nw_ref, dres_ref, dx_ref, dw_ref):
        dh = lax.dot_general(dy_ref[...], w_ref[...], (((1,), (1,)), ((), ())), preferred_element_type=F32)
        dx, dw = _rms_bwd_tile(x_ref[...], nw_ref[...], dh, dres_ref[...])
        dx_ref[...] = dx
        _accumulate(dw_ref, dw)

    row = pl.BlockSpec((tm, D_MODEL), lambda i: (i, 0))
    vec = pl.BlockSpec((1, D_MODEL), lambda i: (0, 0))
    return pl.pallas_call(
        body, name=name, grid=(t // tm,),
        in_specs=[pl.BlockSpec((tm, n), lambda i: (i, 0)),
                  pl.BlockSpec((D_MODEL, n), lambda i: (0, 0), pipeline_mode=pl.Buffered(1)), row, vec, row],
        out_specs=[row, vec],
        out_shape=[jax.ShapeDtypeStruct((t, D_MODEL), F32), jax.ShapeDtypeStruct((1, D_MODEL), F32)],
        compiler_params=_cp(),
    )(dy, w, x, nw, dres)


def _out_bwd(dx, w, o, name):
    t = dx.shape[0]
    tm = _row_tile(t, ROWS)

    def body(dx_ref, w_ref, o_ref, et_ref, do_ref, adj_ref):
        do = lax.dot_general(dx_ref[...], w_ref[...], (((1,), (1,)), ((), ())), preferred_element_type=F32)
        do_ref[...] = do.astype(BF16)
        adj_ref[...] = -_dot_heads(do * o_ref[...].astype(F32), et_ref[...])

    row = pl.BlockSpec((tm, D_MODEL), lambda i: (i, 0))
    return pl.pallas_call(
        body, name=name, grid=(t // tm,),
        in_specs=[row, pl.BlockSpec((D_MODEL, D_MODEL), lambda i: (0, 0)), row,
                  pl.BlockSpec((D_MODEL, LANES), lambda i: (0, 0))],
        out_specs=[row, pl.BlockSpec((tm, LANES), lambda i: (i, 0))],
        out_shape=[jax.ShapeDtypeStruct((t, D_MODEL), BF16), jax.ShapeDtypeStruct((t, LANES), F32)],
        compiler_params=_cp(),
    )(dx, w, o, _head_expander().T)


def _mm_tn(a, bs, name, part=None):
    aq = a.ndim == 3
    bq = bs[0].ndim == 3
    t, ka = a.shape[-2:]
    n = bs[0].shape[-1]
    nq = N_CHIPS if (aq or bq) else 1
    tt = _row_tile(t, GRAD_TOKENS)
    tn = n if n <= 1024 else 768
    assert n % tn == 0
    nb = len(bs)
    steps = t // tt
    carried = part is not None and part[2] is not None

    def body(*refs):
        a_ref = refs[0]
        b_refs = refs[1:1 + nb]
        o_refs = refs[1 + nb + carried:1 + 2 * nb + carried]
        acc_refs = refs[1 + 2 * nb + carried:]
        s = pl.program_id(2)
        av = a_ref[...]
        for b_ref, o_ref, acc_ref in zip(b_refs, o_refs, acc_refs):
            @pl.when(s == 0)
            def _():
                acc_ref[...] = jnp.zeros_like(acc_ref)

            acc_ref[...] += lax.dot_general(av, b_ref[...], (((0,), (0,)), ((), ())), preferred_element_type=F32)

            @pl.when(s == steps - 1)
            def _():
                o_ref[...] = acc_ref[...].astype(BF16)

    a_spec = (pl.BlockSpec((None, tt, ka), lambda q, j, s: (q, s, 0)) if aq
              else pl.BlockSpec((tt, ka), lambda q, j, s: (s, 0)))
    b_spec = (pl.BlockSpec((None, tt, tn), lambda q, j, s: (q, s, j)) if bq
              else pl.BlockSpec((tt, tn), lambda q, j, s: (s, j)))
    extra_specs, extra_args, aliases = [], [], {}
    if nq > 1:
        o_spec = pl.BlockSpec((None, ka, tn), lambda q, j, s: (q, 0, j))
        o_shape = jax.ShapeDtypeStruct((nq, ka, n), BF16)
    elif part is not None:
        assert nb == 1
        k, n_parts, buf = part
        o_spec = pl.BlockSpec((ka, tn), lambda q, j, s: (0, k * (n // tn) + j))
        o_shape = jax.ShapeDtypeStruct((ka, n_parts * n), BF16)
        if buf is not None:
            extra_specs, extra_args, aliases = [ANY], [buf], {1 + nb: 0}
    else:
        o_spec = pl.BlockSpec((ka, tn), lambda q, j, s: (0, j))
        o_shape = jax.ShapeDtypeStruct((ka, n), BF16)
    outs = pl.pallas_call(
        body, name=name, grid=(nq, n // tn, steps),
        in_specs=[a_spec] + [b_spec] * nb + extra_specs, out_specs=[o_spec] * nb, out_shape=[o_shape] * nb,
        scratch_shapes=[pltpu.VMEM((ka, tn), F32)] * nb, input_output_aliases=aliases, compiler_params=_cp(),
    )(a, *bs, *extra_args)
    return outs


def _sigmoid(x):
    return 1.0 / (1.0 + jnp.exp(-x))


def _ffn_up(h, wg, wu, layer, name):
    t = h.shape[0]
    tm = _row_tile(t, MATMUL_ROWS)
    nt = (((1,), (1,)), ((), ()))

    def body(h_ref, wg_ref, wu_ref, a_ref, dg_ref, du_ref):
        hv = h_ref[...]
        g = lax.dot_general(hv, wg_ref[...], nt, preferred_element_type=F32)
        u = lax.dot_general(hv, wu_ref[...], nt, preferred_element_type=F32)
        sg = _sigmoid(g)
        silu = g * sg
        a_ref[...] = (silu * u).astype(BF16)
        dg_ref[...] = (sg * (1.0 + g * (1.0 - sg)) * u).astype(BF16)
        du_ref[...] = silu.astype(BF16)

    wspec = pl.BlockSpec((None, None, FF_SH, D_MODEL), lambda q, i: (q, layer, 0, 0))
    ospec = pl.BlockSpec((None, tm, FF_SH), lambda q, i: (q, i, 0))
    oshape = jax.ShapeDtypeStruct((N_CHIPS, t, FF_SH), BF16)
    return pl.pallas_call(
        body, name=name, grid=(N_CHIPS, t // tm),
        in_specs=[pl.BlockSpec((tm, D_MODEL), lambda q, i: (i, 0)), wspec, wspec],
        out_specs=[ospec] * 3, out_shape=[oshape] * 3, compiler_params=_cp(),
    )(h, wg, wu)


def _ffn_down(a, wd, res, layer, name, norm_w=None, fold_shapes=(), head=None):
    t = a.shape[1]
    tm = _row_tile(t, ROWS)
    resident = pl.BlockSpec((N_CHIPS, None, FF_SH, D_MODEL), lambda i: (0, layer, 0, 0), pipeline_mode=pl.Buffered(1))
    row = pl.BlockSpec((tm, D_MODEL), lambda i: (i, 0))
    vec = pl.BlockSpec((1, D_MODEL), lambda i: (0, 0))

    def hidden(a_ref, w_ref, r_ref):
        acc = r_ref[...]
        for q in range(N_CHIPS):
            acc = acc + jnp.dot(a_ref[q], w_ref[q], preferred_element_type=F32)
        return acc

    if head is None:
        folds = [jax.ShapeDtypeStruct(s, BF16) for s in fold_shapes]

        def body(a_ref, w_ref, r_ref, nw_ref, o_ref, h_ref, *hf_refs):
            xv = hidden(a_ref, w_ref, r_ref)
            o_ref[...] = xv
            hb = _rms_tile(xv, nw_ref[...]).astype(BF16)
            h_ref[...] = hb
            for hf_ref in hf_refs:
                hf_ref[...] = hb.reshape(hf_ref.shape)

        return pl.pallas_call(
            body, name=name, grid=(t // tm,),
            in_specs=[pl.BlockSpec((N_CHIPS, tm, FF_SH), lambda i: (0, i, 0)), resident, row, vec],
            out_specs=[row, row] + [_token_rows_spec(f, tm) for f in folds],
            out_shape=[jax.ShapeDtypeStruct((t, D_MODEL), F32), jax.ShapeDtypeStruct((t, D_MODEL), BF16)] + folds,
            compiler_params=_cp(),
        )(a, wd, res, norm_w)

    def body(a_ref, w_ref, r_ref, nw_ref, t_ref, dx_ref, dxb_ref, l_ref, dw_ref):
        dx, sq, dw = _final_tile(hidden(a_ref, w_ref, r_ref), nw_ref[...], t_ref[...])
        dx_ref[...] = dx
        dxb_ref[...] = dx.astype(BF16)
        _accumulate(l_ref, sq)
        _accumulate(dw_ref, dw)

    return pl.pallas_call(
        body, name=name, grid=(t // tm,),
        in_specs=[pl.BlockSpec((N_CHIPS, tm, FF_SH), lambda i: (0, i, 0)), resident, row, vec, row],
        out_specs=[row, row, vec, vec],
        out_shape=[jax.ShapeDtypeStruct((t, D_MODEL), F32), jax.ShapeDtypeStruct((t, D_MODEL), BF16),
                   jax.ShapeDtypeStruct((1, D_MODEL), F32), jax.ShapeDtypeStruct((1, D_MODEL), F32)],
        compiler_params=_cp(),
    )(a, wd, res, *head)


def _ffn_bwd(dy, wd, wg, wu, fg, fu, x, nw, dres, name):
    t = dy.shape[0]
    tm = _row_tile(t, FFN_BWD_ROWS)
    nt = (((1,), (1,)), ((), ()))

    def body(dy_ref, wd_ref, wg_ref, wu_ref, fg_ref, fu_ref, x_ref, nw_ref, dres_ref,
             dg_ref, du_ref, dx_ref, dxb_ref, dw_ref):
        dyv = dy_ref[...]
        acc = jnp.zeros((tm, D_MODEL), F32)
        for q in range(N_CHIPS):
            da = lax.dot_general(dyv, wd_ref[q], nt, preferred_element_type=F32)
            dg = (da * fg_ref[q].astype(F32)).astype(BF16)
            du = (da * fu_ref[q].astype(F32)).astype(BF16)
            dg_ref[q] = dg
            du_ref[q] = du
            acc = acc + jnp.dot(dg, wg_ref[q], preferred_element_type=F32)
            acc = acc + jnp.dot(du, wu_ref[q], preferred_element_type=F32)
        dx, dw = _rms_bwd_tile(x_ref[...], nw_ref[...], acc, dres_ref[...])
        dx_ref[...] = dx
        dxb_ref[...] = dx.astype(BF16)
        _accumulate(dw_ref, dw)

    aspec = pl.BlockSpec((N_CHIPS, tm, FF_SH), lambda i: (0, i, 0))
    wspec = pl.BlockSpec((N_CHIPS, None, FF_SH, D_MODEL), lambda i: (0, 0, 0, 0), pipeline_mode=pl.Buffered(1))
    row = pl.BlockSpec((tm, D_MODEL), lambda i: (i, 0))
    vec = pl.BlockSpec((1, D_MODEL), lambda i: (0, 0))
    ashape = jax.ShapeDtypeStruct((N_CHIPS, t, FF_SH), BF16)
    return pl.pallas_call(
        body, name=name, grid=(t // tm,),
        in_specs=[row, wspec, wspec, wspec, aspec, aspec, row, vec, row],
        out_specs=[aspec, aspec, row, row, vec],
        out_shape=[ashape, ashape, jax.ShapeDtypeStruct((t, D_MODEL), F32), jax.ShapeDtypeStruct((t, D_MODEL), BF16),
                   jax.ShapeDtypeStruct((1, D_MODEL), F32)],
        compiler_params=_cp(),
    )(dy, wd, wg, wu, fg, fu, x, nw, dres)


def _attn_geometry(length, half_window):
    qb = min(LANES, length)
    kw = min(qb + 2 * half_window, length)
    return qb, kw, length // qb


def _dup_kv(src_ref, dst_ref, s, length):
    ch = min(length, 256)
    lo = lax.broadcasted_iota(jnp.int32, (ch, LANES), 1) < HEAD_DIM

    def chunk(c, carry):
        r0 = pl.multiple_of(c * ch, ch)
        for j in range(N_KV // 2):
            tile = src_ref[s, pl.ds(r0, ch), j * LANES:(j + 1) * LANES].astype(F32)
            rolled = pltpu.roll(tile, HEAD_DIM, 1)
            dst_ref[2 * j, pl.ds(r0, ch), :] = jnp.where(lo, tile, rolled).astype(BF16)
            dst_ref[2 * j + 1, pl.ds(r0, ch), :] = jnp.where(lo, rolled, tile).astype(BF16)
        return carry

    lax.fori_loop(0, length // ch, chunk, 0)


def _stack_heads(ref, s, q0, qb, g):
    lo = lax.broadcasted_iota(jnp.int32, (qb, LANES), 1) < HEAD_DIM
    parts = []
    for a in range(4):
        col = (2 * g + a // 2) * LANES
        tile = ref[s, pl.ds(q0, qb), col:col + LANES]
        keep = lo if a % 2 == 0 else jnp.logical_not(lo)
        parts.append(jnp.where(keep, tile, jnp.zeros_like(tile)))
    return jnp.concatenate(parts, axis=0)


def _unstack_pair_t(stacked_t, qb, pair):
    both = jnp.concatenate([stacked_t[:, (2 * pair) * qb:(2 * pair + 1) * qb],
                            stacked_t[:, (2 * pair + 1) * qb:(2 * pair + 2) * qb]], axis=0)
    return both.T


def _band_mask_t(q0, k0, qb, kw, half_window):
    key = lax.broadcasted_iota(jnp.int32, (kw, 4 * qb), 0)
    qry = lax.broadcasted_iota(jnp.int32, (kw, 4 * qb), 1) & (qb - 1)
    return jnp.abs((q0 + qry) - (k0 + key)) <= half_window


def _block_origin(i, qb, kw, half_window, length):
    if isinstance(i, int):
        return i * qb, min(max(i * qb - half_window, 0), length - kw)
    return (pl.multiple_of(i * qb, qb),
            pl.multiple_of(jnp.clip(i * qb - half_window, 0, length - kw), HEAD_DIM))


def _head_row(vals, qb):
    return jnp.concatenate([jnp.broadcast_to(v, (1, qb)).astype(F32) for v in vals], axis=1)


def _attn_fwd(qkv, sink, n_seq, length, half_window, seq_blk, out_dtype, name):
    qb, kw, nblk = _attn_geometry(length, half_window)
    with_sink = sink is not None
    nt = (((1,), (1,)), ((), ()))
    tn = (((0,), (0,)), ((), ()))
    qkv3 = qkv.reshape(n_seq, length, QKV_W)

    def body(*refs):
        refs = list(refs)
        sink_ref = refs.pop(0) if with_sink else None
        q_ref, k_ref, v_ref, o_ref, lse_ref = refs[:5]
        kx_ref, vx_ref = refs[-2:]
        head_row = lax.broadcasted_iota(jnp.int32, (N_HEADS, qb), 0)
        for s in range(seq_blk):
            _dup_kv(k_ref, kx_ref, s, length)
            _dup_kv(v_ref, vx_ref, s, length)

            def block(i, carry):
                q0, k0 = _block_origin(i, qb, kw, half_window, length)
                valid = _band_mask_t(q0, k0, qb, kw, half_window)
                lse_tile = jnp.zeros((N_HEADS, qb), F32)
                groups = range(N_KV)
                sts = [lax.dot_general(kx_ref[g, pl.ds(k0, kw), :], _stack_heads(q_ref, s, q0, qb, g), nt,
                                       preferred_element_type=F32) for g in groups]
                sts = [jnp.where(valid, st, NEG_INF) for st in sts]
                ms = [jnp.max(st, axis=0, keepdims=True) for st in sts]
                if with_sink:
                    sks = [_head_row([sink_ref[4 * g + a] * LOG2E for a in range(4)], qb) for g in groups]
                    ms = [jnp.maximum(m, sk) for m, sk in zip(ms, sks)]
                es = [jnp.exp2(st - m) for st, m in zip(sts, ms)]
                dens = [jnp.sum(e, axis=0, keepdims=True) for e in es]
                if with_sink:
                    dens = [den + jnp.exp2(sk - m) for den, sk, m in zip(dens, sks, ms)]
                ots = [lax.dot_general(vx_ref[g, pl.ds(k0, kw), 0:HEAD_DIM], es[g].astype(BF16), tn,
                                       preferred_element_type=F32) / dens[g] for g in groups]
                for g in groups:
                    for pair in range(2):
                        col = (2 * g + pair) * LANES
                        o_ref[s, pl.ds(q0, qb), col:col + LANES] = _unstack_pair_t(ots[g], qb, pair).astype(out_dtype)
                    lse = ms[g] * LN2 + jnp.log(dens[g])
                    for a in range(4):
                        lse_tile = jnp.where(head_row == 4 * g + a, lse[:, a * qb:(a + 1) * qb], lse_tile)
                lse_ref[s, :, pl.ds(q0, qb)] = lse_tile
                return carry

            if nblk == 1:
                block(0, 0)
            else:
                lax.fori_loop(0, nblk, block, 0)

    in_specs = [pl.BlockSpec((seq_blk, length, N_HEADS * HEAD_DIM), lambda n: (n, 0, 0)),
                pl.BlockSpec((seq_blk, length, N_KV * HEAD_DIM), lambda n: (n, 0, 4)),
                pl.BlockSpec((seq_blk, length, N_KV * HEAD_DIM), lambda n: (n, 0, 5))]
    args = [qkv3, qkv3, qkv3]
    if with_sink:
        in_specs.insert(0, pl.BlockSpec(memory_space=pltpu.SMEM))
        args.insert(0, sink)
    out_specs = [pl.BlockSpec((seq_blk, length, D_MODEL), lambda n: (n, 0, 0)),
                 pl.BlockSpec((seq_blk, N_HEADS, length), lambda n: (n, 0, 0))]
    out_shape = [jax.ShapeDtypeStruct((n_seq, length, D_MODEL), out_dtype),
                 jax.ShapeDtypeStruct((n_seq, N_HEADS, length), F32)]
    o, lse = pl.pallas_call(
        body, name=name, grid=(n_seq // seq_blk,), in_specs=in_specs, out_specs=out_specs, out_shape=out_shape,
        scratch_shapes=[pltpu.VMEM((N_KV, length, LANES), BF16), pltpu.VMEM((N_KV, length, LANES), BF16)],
        compiler_params=_cp(),
    )(*args)
    return o.reshape(n_seq * length, D_MODEL), lse


def _attn_bwd(qkv, do, adj, lse, sink, cos, sin, n_seq, length, half_window, seq_blk, dil, name):
    qb, kw, nblk = _attn_geometry(length, half_window)
    scale = 1.0 / math.sqrt(HEAD_DIM)
    with_sink = sink is not None
    nt = (((1,), (1,)), ((), ()))
    tn = (((0,), (0,)), ((), ()))
    qkv3 = qkv.reshape(n_seq, length, QKV_W)
    do3 = do.reshape(n_seq, length, D_MODEL)
    tabs = [t.reshape(dil, length, LANES) for t in (cos, sin)]
    tab_blocks = dil // seq_blk if dil >= seq_blk else 1

    def body(*refs):
        refs = list(refs)
        sink_ref = refs.pop(0) if with_sink else None
        q_ref, k_ref, v_ref, do_ref, aux_ref, lse_ref, cos_ref, sin_ref, dqkv_ref = refs[:9]
        ds_ref = refs[9] if with_sink else None
        kx_ref, vx_ref, dkx_ref, dvx_ref = refs[-4:]
        lane = lax.broadcasted_iota(jnp.int32, (1, LANES), 1)
        if with_sink:
            @pl.when(pl.program_id(0) == 0)
            def _():
                ds_ref[...] = jnp.zeros_like(ds_ref)

        for s in range(seq_blk):
            ts = s % dil
            _dup_kv(k_ref, kx_ref, s, length)
            _dup_kv(v_ref, vx_ref, s, length)
            dkx_ref[...] = jnp.zeros_like(dkx_ref)
            dvx_ref[...] = jnp.zeros_like(dvx_ref)

            def block(i, dsink):
                q0, k0 = _block_origin(i, qb, kw, half_window, length)
                valid = _band_mask_t(q0, k0, qb, kw, half_window)
                cs = cos_ref[ts, pl.ds(q0, qb), :] * scale
                sn = sin_ref[ts, pl.ds(q0, qb), :] * scale
                adj_tile = aux_ref[s, :, pl.ds(q0, qb)]
                lse_tile = lse_ref[s, :, pl.ds(q0, qb)]
                groups = range(N_KV)
                qss = [_stack_heads(q_ref, s, q0, qb, g) for g in groups]
                doss = [_stack_heads(do_ref, s, q0, qb, g) for g in groups]
                kxs = [kx_ref[g, pl.ds(k0, kw), :] for g in groups]
                sts = [lax.dot_general(kxs[g], qss[g], nt, preferred_element_type=F32) for g in groups]
                dpts = [lax.dot_general(vx_ref[g, pl.ds(k0, kw), :], doss[g], nt, preferred_element_type=F32)
                        for g in groups]
                lses = [_head_row([lse_tile[4 * g + a:4 * g + a + 1, :] * LOG2E for a in range(4)], qb) for g in groups]
                shifts = [_head_row([adj_tile[4 * g + a:4 * g + a + 1, :] for a in range(4)], qb) for g in groups]
                pts = [jnp.exp2(jnp.where(valid, sts[g], NEG_INF) - lses[g]) for g in groups]
                dsbs = [(pts[g] * (dpts[g] + shifts[g])).astype(BF16) for g in groups]
                pbs = [pt.astype(BF16) for pt in pts]
                if with_sink:
                    for g in groups:
                        sk = _head_row([sink_ref[4 * g + a] * LOG2E for a in range(4)], qb)
                        dsk = jnp.exp2(sk - lses[g]) * shifts[g]
                        for a in range(4):
                            tot = jnp.sum(dsk[:, a * qb:(a + 1) * qb], axis=1, keepdims=True)
                            dsink = dsink + jnp.where(lane == 4 * g + a, tot, 0.0)
                dqts = [lax.dot_general(kx_ref[g, pl.ds(k0, kw), 0:HEAD_DIM], dsbs[g], tn, preferred_element_type=F32)
                        for g in groups]
                for g in groups:
                    for pair in range(2):
                        col = (2 * g + pair) * LANES
                        tile = _rope_t(_unstack_pair_t(dqts[g], qb, pair), cs, sn)
                        dqkv_ref[s, pl.ds(q0, qb), col:col + LANES] = tile.astype(BF16)
                for g in groups:
                    dkx_ref[g, pl.ds(k0, kw), :] += jnp.dot(dsbs[g], qss[g], preferred_element_type=F32)
                    dvx_ref[g, pl.ds(k0, kw), :] += jnp.dot(pbs[g], doss[g], preferred_element_type=F32)
                return dsink

            if nblk == 1:
                dsink = block(0, jnp.zeros((1, LANES), F32))
            else:
                dsink = lax.fori_loop(0, nblk, block, jnp.zeros((1, LANES), F32))
            if with_sink:
                ds_ref[0:1, :] += dsink

            ch = min(length, 256)
            lo_c = lax.broadcasted_iota(jnp.int32, (ch, LANES), 1) < HEAD_DIM

            def fin(c, carry):
                r0 = pl.multiple_of(c * ch, ch)
                cs = cos_ref[ts, pl.ds(r0, ch), :]
                sn = sin_ref[ts, pl.ds(r0, ch), :]
                for j in range(N_KV // 2):
                    both = []
                    for acc_ref in (dkx_ref, dvx_ref):
                        t0 = acc_ref[2 * j, pl.ds(r0, ch), :]
                        t1 = acc_ref[2 * j + 1, pl.ds(r0, ch), :]
                        both.append(jnp.where(lo_c, t0, t1) + pltpu.roll(jnp.where(lo_c, t1, t0), HEAD_DIM, 1))
                    kcol = N_HEADS * HEAD_DIM + j * LANES
                    vcol = (N_HEADS + N_KV) * HEAD_DIM + j * LANES
                    dqkv_ref[s, pl.ds(r0, ch), kcol:kcol + LANES] = _rope_t(both[0] * LN2, cs, sn).astype(BF16)
                    dqkv_ref[s, pl.ds(r0, ch), vcol:vcol + LANES] = both[1].astype(BF16)
                return carry

            lax.fori_loop(0, length // ch, fin, 0)

    seq_map = lambda n: (n, 0, 0)
    tab_map = (lambda n: (n % tab_blocks, 0, 0)) if dil >= seq_blk else (lambda n: (0, 0, 0))
    tab_rows = min(seq_blk, dil)
    in_specs = [pl.BlockSpec((seq_blk, length, N_HEADS * HEAD_DIM), seq_map),
                pl.BlockSpec((seq_blk, length, N_KV * HEAD_DIM), lambda n: (n, 0, 4)),
                pl.BlockSpec((seq_blk, length, N_KV * HEAD_DIM), lambda n: (n, 0, 5)),
                pl.BlockSpec((seq_blk, length, D_MODEL), seq_map),
                pl.BlockSpec((seq_blk, N_HEADS, length), seq_map),
                pl.BlockSpec((seq_blk, N_HEADS, length), seq_map),
                pl.BlockSpec((tab_rows, length, LANES), tab_map),
                pl.BlockSpec((tab_rows, length, LANES), tab_map)]
    args = [qkv3, qkv3, qkv3, do3, adj, lse] + tabs
    if with_sink:
        in_specs.insert(0, pl.BlockSpec(memory_space=pltpu.SMEM))
        args.insert(0, sink)
    out_specs = [pl.BlockSpec((seq_blk, length, QKV_W), seq_map)]
    out_shape = [jax.ShapeDtypeStruct((n_seq, length, QKV_W), BF16)]
    if with_sink:
        out_specs.append(pl.BlockSpec((8, LANES), lambda n: (0, 0)))
        out_shape.append(jax.ShapeDtypeStruct((8, LANES), F32))
    outs = pl.pallas_call(
        body, name=name, grid=(n_seq // seq_blk,), in_specs=in_specs, out_specs=out_specs, out_shape=out_shape,
        scratch_shapes=[pltpu.VMEM((N_KV, length, LANES), BF16), pltpu.VMEM((N_KV, length, LANES), BF16),
                        pltpu.VMEM((N_KV, length, LANES), F32), pltpu.VMEM((N_KV, length, LANES), F32)],
        compiler_params=_cp(),
    )(*args)
    dqkv = outs[0].reshape(n_seq * length, QKV_W)
    return (dqkv, outs[1]) if with_sink else (dqkv, None)


def _head_expander():
    h = jnp.arange(LANES)[:, None]
    l = jnp.arange(D_MODEL)[None, :]
    return (l // HEAD_DIM == h).astype(BF16)


def _dot_split(a, e):
    hi = a.astype(BF16)
    lo = (a - hi.astype(F32)).astype(BF16)
    return jnp.dot(hi, e, preferred_element_type=F32) + jnp.dot(lo, e, preferred_element_type=F32)


def _dot_heads(a, e):
    return jnp.dot(a.astype(BF16), e, preferred_element_type=F32)


def _mix_weights(lses):
    m = jnp.maximum(jnp.maximum(lses[0], lses[1]), lses[2])
    es = [jnp.exp(v - m) for v in lses]
    tot = es[0] + es[1] + es[2]
    return [e / tot for e in es]


def _mix_fwd(os_, lses, name):
    t = os_[0].shape[0]
    tm = _row_tile(t, ROWS)

    def body(o0, o1, o2, l0, l1, l2, e_ref, out_ref):
        wts = _mix_weights([l0[...], l1[...], l2[...]])
        acc = jnp.zeros((tm, D_MODEL), F32)
        for w, o_ref in zip(wts, (o0, o1, o2)):
            acc = acc + _dot_split(w, e_ref[...]) * _token_rows(o_ref)
        out_ref[...] = acc.astype(BF16)

    row = pl.BlockSpec((tm, D_MODEL), lambda i: (i, 0))
    lrow = pl.BlockSpec((tm, LANES), lambda i: (i, 0))
    return pl.pallas_call(
        body, name=name, grid=(t // tm,),
        in_specs=[_token_rows_spec(o, tm) for o in os_] + [lrow] * 3 + [pl.BlockSpec((LANES, D_MODEL), lambda i: (0, 0))],
        out_specs=row, out_shape=jax.ShapeDtypeStruct((t, D_MODEL), BF16), compiler_params=_cp(),
    )(*os_, *lses, _head_expander())


def _mix_bwd(dx, w_out, os_, lses, do_shapes, name):
    t = dx.shape[0]
    tm = _row_tile(t, ROWS)
    do_structs = [jax.ShapeDtypeStruct(s, BF16) for s in do_shapes]

    def body(d_ref, w_ref, o0, o1, o2, l0, l1, l2, e_ref, et_ref, do0, do1, do2, a0, a1, a2):
        wts = _mix_weights([l0[...], l1[...], l2[...]])
        dv = lax.dot_general(d_ref[...], w_ref[...], (((1,), (1,)), ((), ())), preferred_element_type=F32)
        cs = [_dot_heads(dv * _token_rows(o_ref), et_ref[...]) for o_ref in (o0, o1, o2)]
        mean_c = wts[0] * cs[0] + wts[1] * cs[1] + wts[2] * cs[2]
        for w, c, do_ref, a_ref in zip(wts, cs, (do0, do1, do2), (a0, a1, a2)):
            do_ref[...] = (_dot_heads(w, e_ref[...]) * dv).astype(BF16).reshape(do_ref.shape)
            a_ref[...] = w * (c - mean_c) - w * c

    row = pl.BlockSpec((tm, D_MODEL), lambda i: (i, 0))
    lrow = pl.BlockSpec((tm, LANES), lambda i: (i, 0))
    e = _head_expander()
    return pl.pallas_call(
        body, name=name, grid=(t // tm,),
        in_specs=[row, pl.BlockSpec((D_MODEL, D_MODEL), lambda i: (0, 0), pipeline_mode=pl.Buffered(1))]
        + [_token_rows_spec(o, tm) for o in os_] + [lrow] * 3 + [pl.BlockSpec((LANES, D_MODEL), lambda i: (0, 0)),
                                    pl.BlockSpec((D_MODEL, LANES), lambda i: (0, 0))],
        out_specs=[_token_rows_spec(d, tm) for d in do_structs] + [lrow] * 3,
        out_shape=do_structs + [jax.ShapeDtypeStruct((t, LANES), F32)] * 3,
        compiler_params=_cp(),
    )(dx, w_out, *os_, *lses, e, e.T)


def _stats_to_tokens(stat, batch, dil):
    n_seq, _, length = stat.shape
    t = stat.transpose(0, 2, 1).reshape(n_seq * length, N_HEADS)
    return _from_residue(jnp.pad(t, ((0, 0), (0, LANES - N_HEADS))), batch, dil)


def _stats_from_tokens(stat, batch, dil, n_seq, length):
    t = _to_residue(stat[:, :N_HEADS], batch, dil)
    return t.reshape(n_seq, length, N_HEADS).transpose(0, 2, 1)


def _group_geometry(batch, seq, dil, window):
    length = seq // dil
    n_seq = batch * dil
    seq_blk = max(1, min(dil, 1024 // length))
    return n_seq, length, (window // 2) // dil, seq_blk


def _local_step(x, target, a_in, a_sink, a_out, b_in, b_out, norm_mix, norm_ffn, wg, wu, wd, final_norm):
    batch, seq, _ = x.shape
    t = batch * seq
    x0 = x.reshape(t, D_MODEL)
    tgt = target.reshape(t, D_MODEL)
    tabs = {d: _rope_tables(seq, d) for _, d in DILATED}
    nm = [norm_mix[i:i + 1] for i in range(2)]
    nf = [norm_ffn[i:i + 1] for i in range(2)]

    h0 = _rms_fwd(x0, nm[0], "rms_mix0")
    qkv0 = _qkv_proj(h0, a_in, *tabs[1], 0, "qkv0")
    o0, lse0 = _attn_fwd(qkv0, a_sink, batch, seq, HALF_WINDOW_A, 1, BF16, "attn0")
    x1, hf0 = _mm_res(o0, a_out, x0, nf[0], "out0")
    act0, g0, u0 = _ffn_up(hf0, wg[0], wu[0], 0, "ffn_up0")
    fold_dils = [d for _, d in DILATED if _needs_fold(d)]
    x2, h1, *h1_folded = _ffn_down(act0, wd[0], x1, 0, "ffn_down0", norm_w=nm[1],
                                   fold_shapes=[_folded_shape(batch, seq, d, D_MODEL) for d in fold_dils])
    h1_by_dil = dict(zip(fold_dils, h1_folded))

    geo = [_group_geometry(batch, seq, d, w) for w, d in DILATED]
    h1g, qkv1, o1, lse1, lse1r = [], [], [], [], []
    for gi, (_, d) in enumerate(DILATED):
        n_seq, length, hw, sb = geo[gi]
        hp = _to_residue(h1_by_dil.get(d, h1), batch, d)
        pj = _qkv_proj(hp, b_in, *tabs[d], gi, f"qkv1_{gi}")
        o, lse = _attn_fwd(pj, None, n_seq, length, hw, sb, BF16, f"attn1_{gi}")
        h1g.append(hp)
        qkv1.append(pj)
        o1.append(_from_residue(o, batch, d, fold=True))
        lse1r.append(lse)
        lse1.append(_stats_to_tokens(lse, batch, d))
    omix = _mix_fwd(o1, lse1, "mix")
    x3, hf1 = _mm_res(omix, b_out, x2, nf[1], "out1")
    act1, g1, u1 = _ffn_up(hf1, wg[1], wu[1], 0, "ffn_up1")
    dx4, dx4b, loss_cols, d_final = _ffn_down(act1, wd[1], x3, 0, "ffn_down1_loss",
                                                     head=(final_norm.reshape(1, D_MODEL), tgt))

    def ffn_bwd(dxo, dxob, x_mid, hf, g, u, act, layer):
        dg, du, dxm, dxmb, d_nf = _ffn_bwd(dxob, wd[layer], wg[layer], wu[layer], g, u, x_mid, nf[layer], dxo,
                                           f"ffn_bwd{layer}")
        (d_wd,) = _mm_tn(act, [dxob], f"grad_wd{layer}")
        (d_wgt,) = _mm_tn(dg, [hf], f"grad_wg{layer}")
        (d_wut,) = _mm_tn(du, [hf], f"grad_wu{layer}")
        return dxm, dxmb, d_nf, d_wgt, d_wut, d_wd

    dx3, dx3b, d_nf1, d_wg1, d_wu1, d_wd1 = ffn_bwd(dx4, dx4b, x3, hf1, g1, u1, act1, 1)

    (d_b_out,) = _mm_tn(omix, [dx3b], "grad_b_out")
    do_shapes = [_folded_shape(batch, seq, d, D_MODEL) if _needs_fold(d) else (t, D_MODEL) for _, d in DILATED]
    mb = _mix_bwd(dx3b, b_out, o1, lse1, do_shapes, "out1_mix_bwd")
    dh1, d_b_in = [], None
    for gi, (_, d) in enumerate(DILATED):
        n_seq, length, hw, sb = geo[gi]
        dog = _to_residue(mb[gi], batch, d)
        adj = _stats_from_tokens(mb[3 + gi], batch, d, n_seq, length)
        dpj, _ = _attn_bwd(qkv1[gi], dog, adj, lse1r[gi], None, *tabs[d], n_seq, length, hw, sb, d, f"attn1_bwd{gi}")
        (d_b_in,) = _mm_tn(h1g[gi], [dpj], f"grad_b_in{gi}", part=(gi, len(DILATED), d_b_in))
        dh1.append(_from_residue(_mm_nt(dpj, b_in, gi, BF16, f"qkv1_bwd{gi}"), batch, d, fold=True))
    dx2, dx2b, d_nm1 = _rms_bwd(x2, nm[1], dh1, dx3, "rms_mix_bwd1")

    dx1, dx1b, d_nf0, d_wg0, d_wu0, d_wd0 = ffn_bwd(dx2, dx2b, x1, hf0, g0, u0, act0, 0)

    do0, adj0 = _out_bwd(dx1b, a_out, o0, "out0_bwd")
    (d_a_out,) = _mm_tn(o0, [dx1b], "grad_a_out")
    adj0 = _stats_from_tokens(adj0, batch, 1, batch, seq)
    dqkv0, d_sink = _attn_bwd(qkv0, do0, adj0, lse0, a_sink, *tabs[1], batch, seq, HALF_WINDOW_A, 1, 1, "attn0_bwd")
    (d_a_in,) = _mm_tn(h0, [dqkv0], "grad_a_in")
    gx, d_nm0 = _mm_nt_rms(dqkv0, a_in, x0, nm[0], dx1, "qkv0_bwd")

    grads = dict(a_in=d_a_in, a_out=d_a_out, b_in=d_b_in, b_out=d_b_out,
                 wg=(d_wg0, d_wg1), wu=(d_wu0, d_wu1), wd=(d_wd0, d_wd1))
    vecs = dict(norm_mix=(d_nm0, d_nm1), norm_ffn=(d_nf0, d_nf1), final=d_final, loss_cols=loss_cols, sink=d_sink)
    return gx.reshape(x.shape), grads, vecs


ANY = pl.BlockSpec(memory_space=pl.ANY)
HBM = pltpu.MemorySpace.HBM


def _me():
    return lax.axis_index("x"), lax.axis_index("y"), lax.axis_index("c")


def _chip_peer(x, y, j):
    px = 1 - x if j & 2 else x
    py = 1 - y if j & 1 else y
    return px, py, 2 * px + py


def _remote(src, dst, sems, k, dev):
    return pltpu.make_async_remote_copy(src_ref=src, dst_ref=dst, send_sem=sems[0].at[k], recv_sem=sems[1].at[k],
                                        device_id=dev, device_id_type=MESH)


def _col_window(ref, q, width):
    return ref.at[:, pl.ds(pl.multiple_of(q * width, LANES), width)]


def _half0(ref, h):
    n = ref.shape[0] // 2
    return ref.at[pl.ds(h * n, n)]


def _half1(ref, h):
    n = ref.shape[1] // 2
    return ref.at[:, pl.ds(h * n, n)]


def _half_rows(ref, h):
    n = ref.shape[-2] // 2
    if len(ref.shape) == 2:
        return ref.at[pl.ds(h * n, n)]
    return ref.at[:, pl.ds(h * n, n)]


def _place_shard(w, layer, q_arr, col, name):
    _, rows, cols = w.shape

    def body(q_ref, w_ref, o_ref):
        o_ref[...] = w_ref[...].astype(BF16)

    if col:
        out_spec = pl.BlockSpec((rows, cols), lambda l, q: (0, q[0]))
        out_shape = jax.ShapeDtypeStruct((rows, N_CHIPS * cols), BF16)
    else:
        out_spec = pl.BlockSpec((None, None, rows, cols), lambda l, q: (q[0], 0, 0, 0))
        out_shape = jax.ShapeDtypeStruct((N_CHIPS, 1, rows, cols), BF16)
    return pl.pallas_call(
        body, name=name,
        grid_spec=pltpu.PrefetchScalarGridSpec(
            num_scalar_prefetch=1, grid=(1,),
            in_specs=[pl.BlockSpec((None, rows, cols), lambda l, q: (layer, 0, 0))], out_specs=out_spec),
        out_shape=out_shape, compiler_params=_cp(),
    )(q_arr, w)


def _handshake(peers):
    barrier = pltpu.get_barrier_semaphore()
    for p in peers:
        pl.semaphore_signal(barrier, inc=1, device_id=p, device_id_type=MESH)
    pl.semaphore_wait(barrier, len(peers))


def _on_sequencer(name, collective_id, n_sem, n_local, body):
    @pl.kernel(mesh=plsc.ScalarSubcoreMesh(axis_name="seq", num_cores=1), name=name,
               scratch_types=(pltpu.SemaphoreType.DMA((n_sem,)), pltpu.SemaphoreType.DMA((n_sem,)),
                              pltpu.SemaphoreType.DMA((max(n_local, 1),))),
               compiler_params=pltpu.CompilerParams(collective_id=collective_id))
    def launch(send_sems, recv_sems, local_sems):
        body((send_sems, recv_sems), local_sems)

    launch()


def _gather_plan(outs, col_fam, sems):
    n_w = len(outs)
    x, y, c = _me()
    myq = 2 * x + y
    sib = (x, y, 1 - c)
    _handshake([sib] + [_chip_peer(x, y, j)[:2] + (c,) for j in (1, 2, 3)])

    def slot(w, q):
        if col_fam[w]:
            return _col_window(outs[w], q, outs[w].shape[1] // N_CHIPS)
        return outs[w].at[q]

    first = []
    for w in range(n_w):
        for j in (1, 2, 3):
            px, py, _ = _chip_peer(x, y, j)
            mine = _half_rows(slot(w, myq), c)
            cp = _remote(mine, mine, sems, w * 6 + j - 1, (px, py, c))
            cp.start()
            first.append(cp)
    passed = []
    for w in range(n_w):
        for j in (1, 2, 3):
            _, _, pq = _chip_peer(x, y, j)
            land = _half_rows(slot(w, pq), c)
            _remote(land, land, sems, w * 6 + j - 1, sib).wait_recv()
            cp = _remote(land, land, sems, w * 6 + 2 + j, sib)
            cp.start()
            passed.append(cp)
    for w in range(n_w):
        for j in (1, 2, 3):
            _, _, pq = _chip_peer(x, y, j)
            land = _half_rows(slot(w, pq), 1 - c)
            _remote(land, land, sems, w * 6 + 2 + j, sib).wait_recv()
    for cp in first + passed:
        cp.wait_send()


def _gather_weights_async(bufs, col_fam, name, collective_id):
    refs = [jax.new_ref(b, memory_space=HBM) for b in bufs]
    _on_sequencer(name, collective_id, 6 * len(bufs), 0,
                  lambda sems, _: _gather_plan(refs, col_fam, sems))
    return [r[...] for r in refs]


def _grad_half(ref, col, h):
    return _half0(ref, h) if col else _half1(ref, h)


def _swap_halves_with_sibling(grads, col_fam):
    n_w = len(grads)

    def body(*refs):
        _swap_plan(refs[:n_w], refs[n_w:2 * n_w], col_fam, refs[2 * n_w:], False)

    return pl.pallas_call(
        body, name="grad_swap_sibling", in_specs=[ANY] * n_w, out_specs=[ANY] * n_w,
        out_shape=_swap_shapes(grads, col_fam),
        scratch_shapes=[pltpu.SemaphoreType.DMA((n_w,)), pltpu.SemaphoreType.DMA((n_w,))],
    )(*grads)


def _swap_shapes(grads, col_fam):
    out = []
    for w, g in enumerate(grads):
        shp = (g.shape[0] // 2, g.shape[1]) if col_fam[w] else (g.shape[0], g.shape[1] // 2, g.shape[2])
        out.append(jax.ShapeDtypeStruct(shp, g.dtype))
    return out


def _swap_plan(ins, outs, col_fam, sems, handshake):
    x, y, c = _me()
    sib = (x, y, 1 - c)
    if handshake:
        _handshake([sib])
    cps = [_remote(_grad_half(ins[w], col_fam[w], 1 - c), outs[w], sems, w, sib) for w in range(len(ins))]
    for cp in cps:
        cp.start()
    for cp in cps:
        cp.wait_recv()
    for cp in cps:
        cp.wait_send()


def _swap_halves_async(grads, col_fam, name, collective_id):
    srcs = [jax.new_ref(g, memory_space=HBM) for g in grads]
    dsts = [jax.empty_ref(s, memory_space=HBM) for s in _swap_shapes(grads, col_fam)]
    _on_sequencer(name, collective_id, len(grads), 0, lambda sems, _: _swap_plan(srcs, dsts, col_fam, sems, True))
    return [r[...] for r in srcs], [r[...] for r in dsts]


def _half_add(mines, recvs, c_arr, col_fam, name):
    n_w = len(mines)
    mine_specs, recv_specs = [], []
    for recv, col in zip(recvs, col_fam):
        if col:
            rows, n = recv.shape
            tr = rows // N_CHIPS
            mine_specs.append(pl.BlockSpec((tr, n), lambda i, c: (N_CHIPS * c[0] + i, 0)))
            recv_specs.append(pl.BlockSpec((tr, n), lambda i, c: (i, 0)))
        else:
            _, rows, n = recv.shape
            mine_specs.append(pl.BlockSpec((None, rows, n), lambda q, c: (q, c[0], 0)))
            recv_specs.append(pl.BlockSpec((None, rows, n), lambda q, c: (q, 0, 0)))

    def body(c_ref, *refs):
        for a_ref, b_ref, o_ref in zip(refs[:n_w], refs[n_w:2 * n_w], refs[2 * n_w:]):
            o_ref[...] = (a_ref[...].astype(F32) + b_ref[...].astype(F32)).astype(BF16)

    return pl.pallas_call(
        body, name=name,
        grid_spec=pltpu.PrefetchScalarGridSpec(num_scalar_prefetch=1, grid=(N_CHIPS,),
                                               in_specs=mine_specs + recv_specs, out_specs=recv_specs),
        out_shape=[jax.ShapeDtypeStruct(r.shape, BF16) for r in recvs], compiler_params=_cp(),
    )(c_arr, *mines, *recvs)


def _scatter_shapes(sums, col_fam):
    out = []
    for w, s in enumerate(sums):
        shp = (s.shape[0], s.shape[1] // N_CHIPS) if col_fam[w] else s.shape[1:]
        out.append(jax.ShapeDtypeStruct((N_CHIPS,) + shp, s.dtype))
    return out


def _scatter_copies(ins, outs, col_fam, sems, lsem):
    n_w = len(ins)
    x, y, c = _me()
    myq = 2 * x + y

    def slab(w, q):
        if col_fam[w]:
            return _col_window(ins[w], q, ins[w].shape[1] // N_CHIPS)
        return ins[w].at[q]

    local = [pltpu.make_async_copy(slab(w, myq), outs[w].at[myq], lsem.at[w]) for w in range(n_w)]
    sends, lands = [], []
    for w in range(n_w):
        for j in (1, 2, 3):
            px, py, pq = _chip_peer(x, y, j)
            sends.append(_remote(slab(w, pq), outs[w].at[myq], sems, w * 3 + j - 1, (px, py, c)))
            land = outs[w].at[pq]
            lands.append(_remote(land, land, sems, w * 3 + j - 1, (x, y, c)))
    return local, sends, lands


def _scatter_start(ins, outs, col_fam, sems, lsem):
    local, sends, _ = _scatter_copies(ins, outs, col_fam, sems, lsem)
    for cp in local + sends:
        cp.start()


def _scatter_wait(ins, outs, col_fam, sems, lsem):
    local, sends, lands = _scatter_copies(ins, outs, col_fam, sems, lsem)
    for cp in lands:
        cp.wait_recv()
    for cp in sends:
        cp.wait_send()
    for cp in local:
        cp.wait()


def _scatter_chip_sums_async(sums, col_fam, name, collective_id):
    srcs = [jax.new_ref(s, memory_space=HBM) for s in sums]
    dsts = [jax.empty_ref(s, memory_space=HBM) for s in _scatter_shapes(sums, col_fam)]

    def plan(sems, lsem):
        x, y, c = _me()
        _handshake([_chip_peer(x, y, j)[:2] + (c,) for j in (1, 2, 3)])
        _scatter_start(srcs, dsts, col_fam, sems, lsem)
        _scatter_wait(srcs, dsts, col_fam, sems, lsem)

    _on_sequencer(name, collective_id, 3 * len(sums), len(sums), plan)
    return [r[...] for r in dsts]


def _sum_chips(parts, c_arr, prev, lead, shape, name):
    _, rows, n = parts.shape
    tr = rows // 2 if rows % 32 == 0 else rows
    nblk = rows // tr

    def body(c_ref, p_ref, *rest):
        o_ref = rest[-1]
        acc = p_ref[0].astype(F32)
        for q in range(1, N_CHIPS):
            acc = acc + p_ref[q].astype(F32)
        o_ref[...] = acc

    in_specs = [pl.BlockSpec((N_CHIPS, tr, n), lambda i, c: (0, i, 0))]
    args = [c_arr, parts]
    aliases = {}
    if prev is not None:
        in_specs.append(ANY)
        args.append(prev)
        aliases = {2: 0}
    return pl.pallas_call(
        body, name=name,
        grid_spec=pltpu.PrefetchScalarGridSpec(
            num_scalar_prefetch=1, grid=(nblk,), in_specs=in_specs,
            out_specs=pl.BlockSpec((None, tr, n), lambda i, c: (lead, c[0] * nblk + i, 0))),
        out_shape=jax.ShapeDtypeStruct(shape, F32), input_output_aliases=aliases, compiler_params=_cp(),
    )(*args)


def _join_plan(outs, place, sems, handshake):
    x, y, c = _me()
    sib = (x, y, 1 - c)
    if handshake:
        _handshake([sib])

    def half(k, h):
        o, lead = place[k]
        return _half_rows(outs[o].at[lead], h)

    cps = [_remote(half(k, c), half(k, c), sems, k, sib) for k in range(len(place))]
    for cp in cps:
        cp.start()
    for k in range(len(place)):
        land = half(k, 1 - c)
        _remote(land, land, sems, k, sib).wait_recv()
    for cp in cps:
        cp.wait_send()


def _join_halves(bufs, place, name, scatter=None):
    n_o = len(bufs)
    n_h = len(place)
    sums, col_fam = scatter if scatter else ((), ())
    n_w = len(sums)

    def body(*refs):
        ins, outs = refs[n_o:n_o + n_w], refs[2 * n_o + n_w:2 * (n_o + n_w)]
        scratch = refs[2 * (n_o + n_w):]
        if n_w:
            _scatter_start(ins, outs, col_fam, scratch[2:4], scratch[4])
        _join_plan(refs[n_o + n_w:2 * n_o + n_w], place, scratch[:2], False)
        if n_w:
            _scatter_wait(ins, outs, col_fam, scratch[2:4], scratch[4])

    scratch_shapes = [pltpu.SemaphoreType.DMA((n_h,)), pltpu.SemaphoreType.DMA((n_h,))]
    if n_w:
        scratch_shapes += [pltpu.SemaphoreType.DMA((3 * n_w,)), pltpu.SemaphoreType.DMA((3 * n_w,)),
                           pltpu.SemaphoreType.DMA((n_w,))]
    res = pl.pallas_call(
        body, name=name, in_specs=[ANY] * (n_o + n_w), out_specs=[ANY] * (n_o + n_w),
        out_shape=[jax.ShapeDtypeStruct(b.shape, b.dtype) for b in bufs] + _scatter_shapes(sums, col_fam),
        input_output_aliases={k: k for k in range(n_o)}, scratch_shapes=scratch_shapes,
    )(*bufs, *sums)
    return res[:n_o], res[n_o:]


def _allreduce_rows(rows):
    n_dev = 8
    n_r = len(rows)
    assert n_r <= 8

    def body(*refs):
        r_refs = refs[:n_r]
        o_ref, slots, send_sems, recv_sems = refs[n_r:]
        x, y, c = _me()
        me = 4 * x + 2 * y + c
        slots[me] = jnp.concatenate([r[...] for r in r_refs] + [jnp.zeros((8 - n_r, D_MODEL), F32)], axis=0)

        def peer(k):
            return (1 - x if k & 4 else x, 1 - y if k & 2 else y, 1 - c if k & 1 else c)

        cps = []
        for k in range(1, n_dev):
            cp = pltpu.make_async_remote_copy(src_ref=slots.at[me], dst_ref=slots.at[me], send_sem=send_sems.at[k - 1],
                                              recv_sem=recv_sems.at[k - 1], device_id=peer(k), device_id_type=MESH)
            cp.start()
            cps.append(cp)
        for k in range(1, n_dev):
            px, py, pc = peer(k)
            land = slots.at[4 * px + 2 * py + pc]
            pltpu.make_async_remote_copy(src_ref=land, dst_ref=land, send_sem=send_sems.at[k - 1],
                                         recv_sem=recv_sems.at[k - 1], device_id=peer(k),
                                         device_id_type=MESH).wait_recv()
        for cp in cps:
            cp.wait_send()
        acc = slots[0]
        for d in range(1, n_dev):
            acc = acc + slots[d]
        o_ref[...] = acc

    vm = pl.BlockSpec(memory_space=pltpu.VMEM)
    return pl.pallas_call(
        body, name="allreduce_rows", in_specs=[vm] * n_r, out_specs=vm,
        out_shape=jax.ShapeDtypeStruct((8, D_MODEL), F32),
        scratch_shapes=[pltpu.VMEM((n_dev, 8, D_MODEL), F32), pltpu.SemaphoreType.DMA((n_dev - 1,)),
                        pltpu.SemaphoreType.DMA((n_dev - 1,))],
    )(*rows)


def _adamw(w, g, m, v, name, emit_g=False):
    n_out = 4 if emit_g else 3
    shape = w.shape
    if len(shape) == 1:
        lead, rows, cols = 1, 1, shape[0]
    else:
        rows, cols = shape[-2:]
        lead = math.prod(shape[:-2])
    args = [a.reshape(lead, rows, cols) for a in (w, g, m, v)]
    tr = rows // 2 if rows % 16 == 0 else rows

    def body(w_ref, g_ref, m_ref, v_ref, d_ref, nm_ref, nv_ref, *go_ref):
        gv = g_ref[...]
        for r in go_ref:
            r[...] = gv
        nm = ADAM_B1 * m_ref[...] + (1.0 - ADAM_B1) * gv
        nv = ADAM_B2 * v_ref[...] + (1.0 - ADAM_B2) * jnp.square(gv)
        m_hat = nm / (1.0 - ADAM_B1 ** ADAM_STEP)
        v_hat = nv / (1.0 - ADAM_B2 ** ADAM_STEP)
        d_ref[...] = -ADAM_LR * (m_hat / (jnp.sqrt(v_hat) + ADAM_EPS) + ADAM_WD * w_ref[...])
        nm_ref[...] = nm
        nv_ref[...] = nv

    spec = pl.BlockSpec((None, tr, cols), lambda l, i: (l, i, 0))
    outs = pl.pallas_call(
        body, name=name, grid=(lead, rows // tr), in_specs=[spec] * 4, out_specs=[spec] * n_out,
        out_shape=[jax.ShapeDtypeStruct((lead, rows, cols), F32)] * n_out, compiler_params=_cp(),
    )(*args)
    return [o.reshape(shape) for o in outs]


def kernel(x, a_w_in, a_sink, a_w_out, b_w_in, b_w_out, norm_mix, norm_ffn, w_gate, w_up, w_down, final_norm, loss_target, m_a_w_in, m_a_sink, m_a_w_out, m_b_w_in, m_b_w_out, m_norm_mix, m_norm_ffn, m_w_gate, m_w_up, m_w_down, m_final_norm, v_a_w_in, v_a_sink, v_a_w_out, v_b_w_in, v_b_w_out, v_norm_mix, v_norm_ffn, v_w_gate, v_w_up, v_w_down, v_final_norm):
    weights = dict(a_w_in=a_w_in, a_sink=a_sink, a_w_out=a_w_out, b_w_in=b_w_in, b_w_out=b_w_out, norm_mix=norm_mix,
                   norm_ffn=norm_ffn, w_gate=w_gate, w_up=w_up, w_down=w_down, final_norm=final_norm)
    mom = dict(a_w_in=m_a_w_in, a_sink=m_a_sink, a_w_out=m_a_w_out, b_w_in=m_b_w_in, b_w_out=m_b_w_out,
               norm_mix=m_norm_mix, norm_ffn=m_norm_ffn, w_gate=m_w_gate, w_up=m_w_up, w_down=m_w_down,
               final_norm=m_final_norm)
    var = dict(a_w_in=v_a_w_in, a_sink=v_a_sink, a_w_out=v_a_w_out, b_w_in=v_b_w_in, b_w_out=v_b_w_out,
               norm_mix=v_norm_mix, norm_ffn=v_norm_ffn, w_gate=v_w_gate, w_up=v_w_up, w_down=v_w_down,
               final_norm=v_final_norm)
    order = ["a_w_in", "a_sink", "a_w_out", "b_w_in", "b_w_out", "norm_mix", "norm_ffn", "w_gate", "w_up", "w_down",
             "final_norm"]
    swapped = ("w_gate", "w_up")
    for n in swapped:
        weights[n], mom[n], var[n] = (a.transpose(0, 2, 1) for a in (weights[n], mom[n], var[n]))
    w_gate_t, w_up_t = weights["w_gate"], weights["w_up"]

    c_arr = lax.axis_index("c").astype(jnp.int32).reshape(1)
    q_arr = (2 * lax.axis_index("x") + lax.axis_index("y")).astype(jnp.int32).reshape(1)

    def placed(w, layer, col, nm):
        return _place_shard(w, layer, q_arr, col, f"place_{nm}")

    (a_in,) = _gather_weights_async([placed(a_w_in, 0, True, "a_in")], (True,), "gather_weights_first", 6)
    a_out, wg0, wu0, wd0 = _gather_weights_async(
        [placed(a_w_out, 0, False, "a_out"), placed(w_gate_t, 0, False, "wg0"), placed(w_up_t, 0, False, "wu0"),
         placed(w_down, 0, False, "wd0")], (False,) * 4, "gather_weights_layer0", 1)
    b_in, b_out, wg1, wu1, wd1 = _gather_weights_async(
        [placed(b_w_in, 0, True, "b_in"), placed(b_w_out, 0, False, "b_out"), placed(w_gate_t, 1, False, "wg1"),
         placed(w_up_t, 1, False, "wu1"), placed(w_down, 1, False, "wd1")], (True,) + (False,) * 4,
        "gather_weights_layer1", 7)
    a_out = a_out.reshape(D_MODEL, D_MODEL)
    b_out = b_out.reshape(D_MODEL, D_MODEL)
    wg, wu, wd = (wg0, wg1), (wu0, wu1), (wd0, wd1)

    gx, grads, vecs = _local_step(x, loss_target, a_in, a_sink[0], a_out, b_in, b_out, norm_mix, norm_ffn, wg, wu, wd,
                                  final_norm)

    rows_out = D_MODEL // N_CHIPS
    partials = [grads["a_in"], grads["b_in"],
                grads["a_out"].reshape(N_CHIPS, rows_out, D_MODEL), grads["b_out"].reshape(N_CHIPS, rows_out, D_MODEL),
                grads["wg"][0], grads["wg"][1], grads["wu"][0], grads["wu"][1], grads["wd"][0], grads["wd"][1]]
    col_fam = (True, True) + (False,) * 8
    names = ("a_in", "b_in", "a_out", "b_out", "wg0", "wg1", "wu0", "wu1", "wd0", "wd1")
    contrib = [None] * len(partials)

    def reduce_group(idx, tag, ids):
        parts = [partials[k] for k in idx]
        cols = tuple(col_fam[k] for k in idx)
        parts, theirs = _swap_halves_async(parts, cols, f"grad_swap_{tag}", ids[0])
        sums = _half_add(parts, theirs, c_arr, cols, f"chip_sum_{tag}")
        for k, o in zip(idx, _scatter_chip_sums_async(sums, cols, f"grad_scatter_{tag}", ids[1])):
            contrib[k] = o

    reduce_group([1, 3, 5, 7, 9], "layer1", (2, 3))
    reduce_group([2, 4, 6, 8], "ffn0", (4, 5))
    a_in_theirs = _swap_halves_with_sibling(partials[:1], col_fam[:1])
    a_in_sum = _half_add(partials[:1], a_in_theirs, c_arr, col_fam[:1], "chip_sum_a_in")
    shapes = [a_w_in.shape, b_w_in.shape, a_w_out.shape, b_w_out.shape, w_down.shape, w_down.shape, w_down.shape]
    place = [(0, 0), (1, 0), (2, 0), (3, 0), (4, 0), (4, 1), (5, 0), (5, 1), (6, 0), (6, 1)]
    bufs = [None] * len(shapes)
    for p, nm, (o, lead) in list(zip(contrib, names, place))[1:]:
        bufs[o] = _sum_chips(p, c_arr, bufs[o], lead, shapes[o], f"sum_chips_{nm}")
    (g_b_in, g_a_out, g_b_out, g_wg, g_wu, g_wd), contrib[:1] = _join_halves(
        bufs[1:], [(o - 1, lead) for o, lead in place[1:]], "grad_join_sibling", scatter=(a_in_sum, col_fam[:1]))
    bufs[0] = _sum_chips(contrib[0], c_arr, None, 0, shapes[0], "sum_chips_a_in")
    (g_a_in,), _ = _join_halves(bufs[:1], place[:1], "grad_join_a_in")

    sink_row = jnp.pad(vecs["sink"][0:1], ((0, 0), (0, D_MODEL - LANES)))
    tot = _allreduce_rows([vecs["norm_mix"][0], vecs["norm_mix"][1], vecs["norm_ffn"][0], vecs["norm_ffn"][1],
                           vecs["final"], vecs["loss_cols"], sink_row])
    loss = (0.5 / D_MODEL) * jnp.sum(tot[5])
    gw = dict(a_w_in=g_a_in, a_sink=tot[6:7, :N_HEADS], a_w_out=g_a_out, b_w_in=g_b_in, b_w_out=g_b_out,
              norm_mix=tot[0:2], norm_ffn=tot[2:4], w_gate=g_wg, w_up=g_wu, w_down=g_wd, final_norm=tot[4])

    delta, new_m, new_v = {}, {}, {}
    joined = ("a_w_in", "a_w_out", "b_w_in", "b_w_out", "w_gate", "w_up", "w_down")
    for n in order:
        delta[n], new_m[n], new_v[n], *g_again = _adamw(weights[n], gw[n], mom[n], var[n], f"adamw_{n}", n in joined)
        if g_again:
            (gw[n],) = g_again
    for n in swapped:
        gw[n], delta[n], new_m[n], new_v[n] = (a.transpose(0, 2, 1) for a in (gw[n], delta[n], new_m[n], new_v[n]))
    return (loss, gx, *[gw[n] for n in order], *[delta[n] for n in order], *[new_m[n] for n in order],
            *[new_v[n] for n in order])
```
